```python
import math
import jax, jax.numpy as jnp
from jax import lax
import numpy as np

D_MODEL = 1024
BATCH = 8
SEQ = 4096
DEPTH = 1

D_FF = 2816
EPS = 1e-6
N_HEADS_MLA = 16
QK_NOPE = 64
QK_ROPE = 32
QK_HEAD = QK_NOPE + QK_ROPE
V_HEAD = 64
Q_LORA = 384
KV_LORA = 256
ROPE_BASE = 10000.0
Q_BLOCK = 128
D_INNER = 2 * D_MODEL
SSM_HEAD_DIM = 64
SSM_HEADS = D_INNER // SSM_HEAD_DIM
SSM_GROUPS = 4
D_STATE = 128
CONV_WIDTH = 5
CHUNK = 128
XBC_DIM = D_INNER + 2 * SSM_GROUPS * D_STATE
IN_SPLITS = (Q_LORA, KV_LORA, QK_ROPE, D_INNER, XBC_DIM, SSM_HEADS, SSM_HEADS, D_MODEL, D_MODEL)
IN_DIM = sum(IN_SPLITS)

kernel_name = "hybrid_mla_bissd_macaron_block"


def rmsnorm(x, g):
    xf = x.astype(jnp.float32)
    y = xf * lax.rsqrt(jnp.mean(xf * xf, axis=-1, keepdims=True) + EPS)
    return (y * g).astype(x.dtype)


def swiglu(h, w_gate, w_up, w_down):
    return (jax.nn.silu(h @ w_gate) * (h @ w_up)) @ w_down


def rope_tables(positions):
    inv_freq = 1.0 / (ROPE_BASE ** (jnp.arange(0, QK_ROPE, 2, dtype=jnp.float32) / QK_ROPE))
    ang = positions.astype(jnp.float32)[..., None] * inv_freq
    return jnp.cos(ang)[:, :, None, :], jnp.sin(ang)[:, :, None, :]


def apply_rope(t, cos, sin):
    tf = t.astype(jnp.float32)
    t1, t2 = tf[..., : QK_ROPE // 2], tf[..., QK_ROPE // 2:]
    return jnp.concatenate([t1 * cos - t2 * sin, t2 * cos + t1 * sin], axis=-1).astype(t.dtype)


def mla(c_q, c_kv, k_pe, positions, q_a_norm, w_q_b, kv_a_norm, w_kv_b, q_head_norm, k_head_norm):
    b, s, _ = c_q.shape
    q = (rmsnorm(c_q, q_a_norm) @ w_q_b).reshape(b, s, N_HEADS_MLA, QK_HEAD)
    kv = (rmsnorm(c_kv, kv_a_norm) @ w_kv_b).reshape(b, s, N_HEADS_MLA, QK_NOPE + V_HEAD)
    k_nope, v = kv[..., :QK_NOPE], kv[..., QK_NOPE:]
    k_pe_h = jnp.broadcast_to(k_pe[:, :, None, :], (b, s, N_HEADS_MLA, QK_ROPE))
    k = jnp.concatenate([k_nope, k_pe_h], axis=-1)
    q = rmsnorm(q, q_head_norm)
    k = rmsnorm(k, k_head_norm)
    cos, sin = rope_tables(positions)
    q = jnp.concatenate([q[..., :QK_NOPE], apply_rope(q[..., QK_NOPE:], cos, sin)], axis=-1)
    k = jnp.concatenate([k[..., :QK_NOPE], apply_rope(k[..., QK_NOPE:], cos, sin)], axis=-1)
    scale = 1.0 / math.sqrt(QK_HEAD)
    n_blk = s // Q_BLOCK
    qb = q.reshape(b, n_blk, Q_BLOCK, N_HEADS_MLA, QK_HEAD).transpose(1, 0, 2, 3, 4)

    def attend(q_blk):
        sc = jnp.einsum('bqhd,bkhd->bhqk', q_blk, k).astype(jnp.float32) * scale
        p = jax.nn.softmax(sc, axis=-1).astype(v.dtype)
        return jnp.einsum('bhqk,bkhd->bqhd', p, v)

    o = lax.map(attend, qb)
    return o.transpose(1, 0, 2, 3, 4).reshape(b, s, N_HEADS_MLA * V_HEAD)


def ssd(x, dt, a, bm, cm):
    b, l, h, p = x.shape
    g, n = bm.shape[2], bm.shape[3]
    hg = h // g
    nc = l // CHUNK
    f32 = jnp.float32
    xdt = (x.astype(f32) * dt[..., None]).reshape(b, nc, CHUNK, g, hg, p)
    da = jnp.moveaxis((dt * a).reshape(b, nc, CHUNK, g, hg), 2, -1)
    a_cs = jnp.cumsum(da, axis=-1)
    bc = bm.astype(f32).reshape(b, nc, CHUNK, g, n)
    cc = cm.astype(f32).reshape(b, nc, CHUNK, g, n)
    tril = jnp.tril(jnp.ones((CHUNK, CHUNK), dtype=bool))
    seg = a_cs[..., :, None] - a_cs[..., None, :]
    decay = jnp.exp(jnp.where(tril, seg, -jnp.inf))
    cb = jnp.einsum('bclgn,bcsgn->bcgls', cc, bc)
    y_diag = jnp.einsum('bcgls,bcghls,bcsghp->bclghp', cb, decay, xdt)
    decay_states = jnp.exp(a_cs[..., -1:] - a_cs)
    states = jnp.einsum('bclgn,bcghl,bclghp->bcghpn', bc, decay_states, xdt)
    chunk_decay = jnp.exp(a_cs[..., -1])

    def step(carry, inp):
        st, dec = inp
        return carry * dec[..., None, None] + st, carry

    init = jnp.zeros((b, g, hg, p, n), f32)
    _, prev = lax.scan(step, init, (jnp.moveaxis(states, 1, 0), jnp.moveaxis(chunk_decay, 1, 0)))
    prev = jnp.moveaxis(prev, 0, 1)
    y_off = jnp.einsum('bclgn,bcghpn,bcghl->bclghp', cc, prev, jnp.exp(a_cs))
    return (y_diag + y_off).reshape(b, l, h, p)


def bi_mamba2(xbc, z, dt_f_raw, dt_b_raw, conv_w, conv_b, a_log_fwd, a_log_bwd,
              dt_bias_fwd, dt_bias_bwd, d_skip, ssm_norm):
    b, s, _ = xbc.shape
    pad = CONV_WIDTH // 2
    xbc = lax.conv_general_dilated(xbc, conv_w, window_strides=(1,), padding=[(pad, pad)],
                                   dimension_numbers=('NWC', 'WIO', 'NWC'),
                                   feature_group_count=XBC_DIM)
    xbc = jax.nn.silu(xbc + conv_b)
    xs, bm, cm = jnp.split(xbc, [D_INNER, D_INNER + SSM_GROUPS * D_STATE], axis=-1)
    xs = xs.reshape(b, s, SSM_HEADS, SSM_HEAD_DIM)
    bm = bm.reshape(b, s, SSM_GROUPS, D_STATE)
    cm = cm.reshape(b, s, SSM_GROUPS, D_STATE)
    dt_f = jax.nn.softplus(dt_f_raw.astype(jnp.float32) + dt_bias_fwd)
    dt_b = jax.nn.softplus(dt_b_raw.astype(jnp.float32) + dt_bias_bwd)
    a_f = -jnp.exp(a_log_fwd.astype(jnp.float32))
    a_b = -jnp.exp(a_log_bwd.astype(jnp.float32))
    rev = lambda t: jnp.flip(t, axis=1)
    y_f = ssd(xs, dt_f, a_f, bm, cm)
    y_b = rev(ssd(rev(xs), rev(dt_b), a_b, rev(bm), rev(cm)))
    y = y_f + y_b + d_skip.astype(jnp.float32)[:, None] * xs.astype(jnp.float32)
    y = y.reshape(b, s, D_INNER) * jax.nn.silu(z.astype(jnp.float32))
    yg = y.reshape(b, s, SSM_GROUPS, D_INNER // SSM_GROUPS)
    yg = yg * lax.rsqrt(jnp.mean(yg * yg, axis=-1, keepdims=True) + EPS)
    return (yg.reshape(b, s, D_INNER) * ssm_norm).astype(xs.dtype)


def _fwd_setup_inputs(seed: int = 0) -> dict:
    key = jax.random.key(seed)
    ks = iter(jax.random.split(key, 40))
    f32 = jnp.float32

    def nrm(shape, scale):
        return jax.random.normal(next(ks), (DEPTH,) + shape, f32) * scale

    def gain(n):
        return 1.0 + 0.01 * jax.random.normal(next(ks), (DEPTH, n), f32)

    x = jax.random.normal(next(ks), (BATCH, SEQ, D_MODEL), f32)
    positions = jnp.broadcast_to(jnp.arange(SEQ, dtype=jnp.int32)[None, :], (BATCH, SEQ))
    d = {}
    d['x'] = x
    d['positions'] = positions
    d['ffn1_norm'] = gain(D_MODEL)
    d['ffn1_w_gate'] = nrm((D_MODEL, D_FF), D_MODEL ** -0.5)
    d['ffn1_w_up'] = nrm((D_MODEL, D_FF), D_MODEL ** -0.5)
    d['ffn1_w_down'] = nrm((D_FF, D_MODEL), D_FF ** -0.5)
    d['mix_norm'] = gain(D_MODEL)
    d['w_in'] = nrm((D_MODEL, IN_DIM), D_MODEL ** -0.5)
    d['q_a_norm'] = gain(Q_LORA)
    d['w_q_b'] = nrm((Q_LORA, N_HEADS_MLA * QK_HEAD), Q_LORA ** -0.5)
    d['kv_a_norm'] = gain(KV_LORA)
    d['w_kv_b'] = nrm((KV_LORA, N_HEADS_MLA * (QK_NOPE + V_HEAD)), KV_LORA ** -0.5)
    d['q_head_norm'] = gain(QK_HEAD)
    d['k_head_norm'] = gain(QK_HEAD)
    d['conv_w'] = nrm((CONV_WIDTH, 1, XBC_DIM), CONV_WIDTH ** -0.5)
    d['conv_b'] = nrm((XBC_DIM,), 0.01)
    d['a_log_fwd'] = jnp.log(jax.random.uniform(next(ks), (DEPTH, SSM_HEADS), f32, 1.0, 16.0))
    d['a_log_bwd'] = jnp.log(jax.random.uniform(next(ks), (DEPTH, SSM_HEADS), f32, 1.0, 16.0))
    dt0_f = jnp.exp(jax.random.uniform(next(ks), (DEPTH, SSM_HEADS), f32, math.log(1e-3), math.log(1e-1)))
    dt0_b = jnp.exp(jax.random.uniform(next(ks), (DEPTH, SSM_HEADS), f32, math.log(1e-3), math.log(1e-1)))
    d['dt_bias_fwd'] = dt0_f + jnp.log(-jnp.expm1(-dt0_f))
    d['dt_bias_bwd'] = dt0_b + jnp.log(-jnp.expm1(-dt0_b))
    d['d_skip'] = gain(SSM_HEADS)
    d['ssm_norm'] = gain(D_INNER)
    d['w_attn_branch'] = nrm((N_HEADS_MLA * V_HEAD, D_MODEL), (N_HEADS_MLA * V_HEAD) ** -0.5)
    d['w_ssm_branch'] = nrm((D_INNER, D_MODEL), D_INNER ** -0.5)
    d['w_out'] = nrm((D_MODEL, D_MODEL), D_MODEL ** -0.5)
    d['ffn2_norm'] = gain(D_MODEL)
    d['ffn2_w_gate'] = nrm((D_MODEL, D_FF), D_MODEL ** -0.5)
    d['ffn2_w_up'] = nrm((D_MODEL, D_FF), D_MODEL ** -0.5)
    d['ffn2_w_down'] = nrm((D_FF, D_MODEL), D_FF ** -0.5)
    return d


def _fwd_reference(x, positions, ffn1_norm, ffn1_w_gate, ffn1_w_up, ffn1_w_down, mix_norm, w_in,
              q_a_norm, w_q_b, kv_a_norm, w_kv_b, q_head_norm, k_head_norm,
              conv_w, conv_b, a_log_fwd, a_log_bwd, dt_bias_fwd, dt_bias_bwd, d_skip, ssm_norm,
              w_attn_branch, w_ssm_branch, w_out,
              ffn2_norm, ffn2_w_gate, ffn2_w_up, ffn2_w_down):
    split_idx = list(np.cumsum(IN_SPLITS)[:-1])
    for l in range(DEPTH):
        x = x + 0.5 * swiglu(rmsnorm(x, ffn1_norm[l]), ffn1_w_gate[l], ffn1_w_up[l], ffn1_w_down[l])
        h = rmsnorm(x, mix_norm[l])
        u = h @ w_in[l]
        c_q, c_kv, k_pe, z, xbc, dt_f, dt_b, g_a, g_b = jnp.split(u, split_idx, axis=-1)
        a = mla(c_q, c_kv, k_pe, positions, q_a_norm[l], w_q_b[l], kv_a_norm[l], w_kv_b[l],
                q_head_norm[l], k_head_norm[l])
        m = bi_mamba2(xbc, z, dt_f, dt_b, conv_w[l], conv_b[l], a_log_fwd[l], a_log_bwd[l],
                      dt_bias_fwd[l], dt_bias_bwd[l], d_skip[l], ssm_norm[l])
        merged = jax.nn.sigmoid(g_a) * (a @ w_attn_branch[l]) + jax.nn.sigmoid(g_b) * (m @ w_ssm_branch[l])
        x = x + merged @ w_out[l]
        x = x + 0.5 * swiglu(rmsnorm(x, ffn2_norm[l]), ffn2_w_gate[l], ffn2_w_up[l], ffn2_w_down[l])
    return x


import jax as _jax
import jax.numpy as _jnp

TWIN_FORMAT = 'train_step'
FWD_PARAMS = ['x', 'positions', 'ffn1_norm', 'ffn1_w_gate', 'ffn1_w_up', 'ffn1_w_down', 'mix_norm', 'w_in', 'q_a_norm', 'w_q_b', 'kv_a_norm', 'w_kv_b', 'q_head_norm', 'k_head_norm', 'conv_w', 'conv_b', 'a_log_fwd', 'a_log_bwd', 'dt_bias_fwd', 'dt_bias_bwd', 'd_skip', 'ssm_norm', 'w_attn_branch', 'w_ssm_branch', 'w_out', 'ffn2_norm', 'ffn2_w_gate', 'ffn2_w_up', 'ffn2_w_down']
TWIN_WEIGHTS = ['ffn1_norm', 'ffn1_w_gate', 'ffn1_w_up', 'ffn1_w_down', 'mix_norm', 'w_in', 'q_a_norm', 'w_q_b', 'kv_a_norm', 'w_kv_b', 'q_head_norm', 'k_head_norm', 'conv_w', 'conv_b', 'a_log_fwd', 'a_log_bwd', 'dt_bias_fwd', 'dt_bias_bwd', 'd_skip', 'ssm_norm', 'w_attn_branch', 'w_ssm_branch', 'w_out', 'ffn2_norm', 'ffn2_w_gate', 'ffn2_w_up', 'ffn2_w_down']
TWIN_DIFF_INPUT = 'x'
TWIN_INPUTS = ['x', 'positions', 'ffn1_norm', 'ffn1_w_gate', 'ffn1_w_up', 'ffn1_w_down', 'mix_norm', 'w_in', 'q_a_norm', 'w_q_b', 'kv_a_norm', 'w_kv_b', 'q_head_norm', 'k_head_norm', 'conv_w', 'conv_b', 'a_log_fwd', 'a_log_bwd', 'dt_bias_fwd', 'dt_bias_bwd', 'd_skip', 'ssm_norm', 'w_attn_branch', 'w_ssm_branch', 'w_out', 'ffn2_norm', 'ffn2_w_gate', 'ffn2_w_up', 'ffn2_w_down', 'loss_target', 'm_ffn1_norm', 'm_ffn1_w_gate', 'm_ffn1_w_up', 'm_ffn1_w_down', 'm_mix_norm', 'm_w_in', 'm_q_a_norm', 'm_w_q_b', 'm_kv_a_norm', 'm_w_kv_b', 'm_q_head_norm', 'm_k_head_norm', 'm_conv_w', 'm_conv_b', 'm_a_log_fwd', 'm_a_log_bwd', 'm_dt_bias_fwd', 'm_dt_bias_bwd', 'm_d_skip', 'm_ssm_norm', 'm_w_attn_branch', 'm_w_ssm_branch', 'm_w_out', 'm_ffn2_norm', 'm_ffn2_w_gate', 'm_ffn2_w_up', 'm_ffn2_w_down', 'v_ffn1_norm', 'v_ffn1_w_gate', 'v_ffn1_w_up', 'v_ffn1_w_down', 'v_mix_norm', 'v_w_in', 'v_q_a_norm', 'v_w_q_b', 'v_kv_a_norm', 'v_w_kv_b', 'v_q_head_norm', 'v_k_head_norm', 'v_conv_w', 'v_conv_b', 'v_a_log_fwd', 'v_a_log_bwd', 'v_dt_bias_fwd', 'v_dt_bias_bwd', 'v_d_skip', 'v_ssm_norm', 'v_w_attn_branch', 'v_w_ssm_branch', 'v_w_out', 'v_ffn2_norm', 'v_ffn2_w_gate', 'v_ffn2_w_up', 'v_ffn2_w_down']
TWIN_OUTPUTS = ['loss', 'grad_x', 'grad_ffn1_norm', 'grad_ffn1_w_gate', 'grad_ffn1_w_up', 'grad_ffn1_w_down', 'grad_mix_norm', 'grad_w_in', 'grad_q_a_norm', 'grad_w_q_b', 'grad_kv_a_norm', 'grad_w_kv_b', 'grad_q_head_norm', 'grad_k_head_norm', 'grad_conv_w', 'grad_conv_b', 'grad_a_log_fwd', 'grad_a_log_bwd', 'grad_dt_bias_fwd', 'grad_dt_bias_bwd', 'grad_d_skip', 'grad_ssm_norm', 'grad_w_attn_branch', 'grad_w_ssm_branch', 'grad_w_out', 'grad_ffn2_norm', 'grad_ffn2_w_gate', 'grad_ffn2_w_up', 'grad_ffn2_w_down', 'delta_ffn1_norm', 'delta_ffn1_w_gate', 'delta_ffn1_w_up', 'delta_ffn1_w_down', 'delta_mix_norm', 'delta_w_in', 'delta_q_a_norm', 'delta_w_q_b', 'delta_kv_a_norm', 'delta_w_kv_b', 'delta_q_head_norm', 'delta_k_head_norm', 'delta_conv_w', 'delta_conv_b', 'delta_a_log_fwd', 'delta_a_log_bwd', 'delta_dt_bias_fwd', 'delta_dt_bias_bwd', 'delta_d_skip', 'delta_ssm_norm', 'delta_w_attn_branch', 'delta_w_ssm_branch', 'delta_w_out', 'delta_ffn2_norm', 'delta_ffn2_w_gate', 'delta_ffn2_w_up', 'delta_ffn2_w_down', 'new_m_ffn1_norm', 'new_m_ffn1_w_gate', 'new_m_ffn1_w_up', 'new_m_ffn1_w_down', 'new_m_mix_norm', 'new_m_w_in', 'new_m_q_a_norm', 'new_m_w_q_b', 'new_m_kv_a_norm', 'new_m_w_kv_b', 'new_m_q_head_norm', 'new_m_k_head_norm', 'new_m_conv_w', 'new_m_conv_b', 'new_m_a_log_fwd', 'new_m_a_log_bwd', 'new_m_dt_bias_fwd', 'new_m_dt_bias_bwd', 'new_m_d_skip', 'new_m_ssm_norm', 'new_m_w_attn_branch', 'new_m_w_ssm_branch', 'new_m_w_out', 'new_m_ffn2_norm', 'new_m_ffn2_w_gate', 'new_m_ffn2_w_up', 'new_m_ffn2_w_down', 'new_v_ffn1_norm', 'new_v_ffn1_w_gate', 'new_v_ffn1_w_up', 'new_v_ffn1_w_down', 'new_v_mix_norm', 'new_v_w_in', 'new_v_q_a_norm', 'new_v_w_q_b', 'new_v_kv_a_norm', 'new_v_w_kv_b', 'new_v_q_head_norm', 'new_v_k_head_norm', 'new_v_conv_w', 'new_v_conv_b', 'new_v_a_log_fwd', 'new_v_a_log_bwd', 'new_v_dt_bias_fwd', 'new_v_dt_bias_bwd', 'new_v_d_skip', 'new_v_ssm_norm', 'new_v_w_attn_branch', 'new_v_w_ssm_branch', 'new_v_w_out', 'new_v_ffn2_norm', 'new_v_ffn2_w_gate', 'new_v_ffn2_w_up', 'new_v_ffn2_w_down']
TWIN_LEAF_KINDS = {'loss': 'loss', 'grad_x': 'grad_x', 'grad_ffn1_norm': 'grad_w', 'grad_ffn1_w_gate': 'grad_w', 'grad_ffn1_w_up': 'grad_w', 'grad_ffn1_w_down': 'grad_w', 'grad_mix_norm': 'grad_w', 'grad_w_in': 'grad_w', 'grad_q_a_norm': 'grad_w', 'grad_w_q_b': 'grad_w', 'grad_kv_a_norm': 'grad_w', 'grad_w_kv_b': 'grad_w', 'grad_q_head_norm': 'grad_w', 'grad_k_head_norm': 'grad_w', 'grad_conv_w': 'grad_w', 'grad_conv_b': 'grad_w', 'grad_a_log_fwd': 'grad_w', 'grad_a_log_bwd': 'grad_w', 'grad_dt_bias_fwd': 'grad_w', 'grad_dt_bias_bwd': 'grad_w', 'grad_d_skip': 'grad_w', 'grad_ssm_norm': 'grad_w', 'grad_w_attn_branch': 'grad_w', 'grad_w_ssm_branch': 'grad_w', 'grad_w_out': 'grad_w', 'grad_ffn2_norm': 'grad_w', 'grad_ffn2_w_gate': 'grad_w', 'grad_ffn2_w_up': 'grad_w', 'grad_ffn2_w_down': 'grad_w', 'delta_ffn1_norm': 'delta_w', 'delta_ffn1_w_gate': 'delta_w', 'delta_ffn1_w_up': 'delta_w', 'delta_ffn1_w_down': 'delta_w', 'delta_mix_norm': 'delta_w', 'delta_w_in': 'delta_w', 'delta_q_a_norm': 'delta_w', 'delta_w_q_b': 'delta_w', 'delta_kv_a_norm': 'delta_w', 'delta_w_kv_b': 'delta_w', 'delta_q_head_norm': 'delta_w', 'delta_k_head_norm': 'delta_w', 'delta_conv_w': 'delta_w', 'delta_conv_b': 'delta_w', 'delta_a_log_fwd': 'delta_w', 'delta_a_log_bwd': 'delta_w', 'delta_dt_bias_fwd': 'delta_w', 'delta_dt_bias_bwd': 'delta_w', 'delta_d_skip': 'delta_w', 'delta_ssm_norm': 'delta_w', 'delta_w_attn_branch': 'delta_w', 'delta_w_ssm_branch': 'delta_w', 'delta_w_out': 'delta_w', 'delta_ffn2_norm': 'delta_w', 'delta_ffn2_w_gate': 'delta_w', 'delta_ffn2_w_up': 'delta_w', 'delta_ffn2_w_down': 'delta_w', 'new_m_ffn1_norm': 'new_m', 'new_m_ffn1_w_gate': 'new_m', 'new_m_ffn1_w_up': 'new_m', 'new_m_ffn1_w_down': 'new_m', 'new_m_mix_norm': 'new_m', 'new_m_w_in': 'new_m', 'new_m_q_a_norm': 'new_m', 'new_m_w_q_b': 'new_m', 'new_m_kv_a_norm': 'new_m', 'new_m_w_kv_b': 'new_m', 'new_m_q_head_norm': 'new_m', 'new_m_k_head_norm': 'new_m', 'new_m_conv_w': 'new_m', 'new_m_conv_b': 'new_m', 'new_m_a_log_fwd': 'new_m', 'new_m_a_log_bwd': 'new_m', 'new_m_dt_bias_fwd': 'new_m', 'new_m_dt_bias_bwd': 'new_m', 'new_m_d_skip': 'new_m', 'new_m_ssm_norm': 'new_m', 'new_m_w_attn_branch': 'new_m', 'new_m_w_ssm_branch': 'new_m', 'new_m_w_out': 'new_m', 'new_m_ffn2_norm': 'new_m', 'new_m_ffn2_w_gate': 'new_m', 'new_m_ffn2_w_up': 'new_m', 'new_m_ffn2_w_down': 'new_m', 'new_v_ffn1_norm': 'new_v', 'new_v_ffn1_w_gate': 'new_v', 'new_v_ffn1_w_up': 'new_v', 'new_v_ffn1_w_down': 'new_v', 'new_v_mix_norm': 'new_v', 'new_v_w_in': 'new_v', 'new_v_q_a_norm': 'new_v', 'new_v_w_q_b': 'new_v', 'new_v_kv_a_norm': 'new_v', 'new_v_w_kv_b': 'new_v', 'new_v_q_head_norm': 'new_v', 'new_v_k_head_norm': 'new_v', 'new_v_conv_w': 'new_v', 'new_v_conv_b': 'new_v', 'new_v_a_log_fwd': 'new_v', 'new_v_a_log_bwd': 'new_v', 'new_v_dt_bias_fwd': 'new_v', 'new_v_dt_bias_bwd': 'new_v', 'new_v_d_skip': 'new_v', 'new_v_ssm_norm': 'new_v', 'new_v_w_attn_branch': 'new_v', 'new_v_w_ssm_branch': 'new_v', 'new_v_w_out': 'new_v', 'new_v_ffn2_norm': 'new_v', 'new_v_ffn2_w_gate': 'new_v', 'new_v_ffn2_w_up': 'new_v', 'new_v_ffn2_w_down': 'new_v'}


def _forward(args):
    return _fwd_reference(*[args[k] for k in FWD_PARAMS])


def _output_shape():
    out = _jax.eval_shape(lambda: _forward(_fwd_setup_inputs(0)))
    return out.shape, out.dtype

N_MICROBATCH = 1
ADAM_LR = 0.001
ADAM_B1 = 0.9
ADAM_B2 = 0.999
ADAM_EPS = 1e-08
ADAM_WD = 0.01
ADAM_STEP = 10
PER_EXAMPLE_BATCH_AXIS = {'x': 0, 'positions': 0, 'loss_target': 0}
SHARED_INPUTS = []
_WEIGHT_DTYPES = {'ffn1_norm': _jnp.float32, 'ffn1_w_gate': _jnp.float32, 'ffn1_w_up': _jnp.float32, 'ffn1_w_down': _jnp.float32, 'mix_norm': _jnp.float32, 'w_in': _jnp.float32, 'q_a_norm': _jnp.float32, 'w_q_b': _jnp.float32, 'kv_a_norm': _jnp.float32, 'w_kv_b': _jnp.float32, 'q_head_norm': _jnp.float32, 'k_head_norm': _jnp.float32, 'conv_w': _jnp.float32, 'conv_b': _jnp.float32, 'a_log_fwd': _jnp.float32, 'a_log_bwd': _jnp.float32, 'dt_bias_fwd': _jnp.float32, 'dt_bias_bwd': _jnp.float32, 'd_skip': _jnp.float32, 'ssm_norm': _jnp.float32, 'w_attn_branch': _jnp.float32, 'w_ssm_branch': _jnp.float32, 'w_out': _jnp.float32, 'ffn2_norm': _jnp.float32, 'ffn2_w_gate': _jnp.float32, 'ffn2_w_up': _jnp.float32, 'ffn2_w_down': _jnp.float32}
MOMENT_SCALE = {'ffn1_norm': 6.119545e+00, 'ffn1_w_gate': 7.787244e-02, 'ffn1_w_up': 8.326708e-02, 'ffn1_w_down': 1.356889e-01, 'mix_norm': 6.878040e-01, 'w_in': 1.119427e-01, 'q_a_norm': 3.385912e-02, 'w_q_b': 1.636161e-02, 'kv_a_norm': 1.178420e-01, 'w_kv_b': 2.200305e-02, 'q_head_norm': 2.246336e-01, 'k_head_norm': 2.242429e-01, 'conv_w': 1.979911e-01, 'conv_b': 7.416872e-01, 'a_log_fwd': 4.832849e-01, 'a_log_bwd': 1.357209e+00, 'dt_bias_fwd': 2.387272e-01, 'dt_bias_bwd': 3.173999e-01, 'd_skip': 8.700189e-01, 'ssm_norm': 6.382465e+00, 'w_attn_branch': 2.527324e-02, 'w_ssm_branch': 5.755673e-01, 'w_out': 4.493686e-01, 'ffn2_norm': 6.174242e+00, 'ffn2_w_gate': 7.438750e-02, 'ffn2_w_up': 7.562590e-02, 'ffn2_w_down': 1.212996e-01}


def _to_microbatches(a, axis):
    t = _jnp.moveaxis(a, axis, 0)
    t = t.reshape((N_MICROBATCH, t.shape[0] // N_MICROBATCH) + t.shape[1:])
    return _jnp.moveaxis(t, 1, axis + 1)


def setup_inputs(seed: int = 0) -> dict:
    inp = _fwd_setup_inputs(seed)
    key = _jax.random.fold_in(_jax.random.key(seed), 7919)
    shape, _ = _output_shape()
    out = dict(inp)
    out["loss_target"] = _jax.random.normal(_jax.random.fold_in(key, 0), shape, _jnp.float32)
    for i, name in enumerate(TWIN_WEIGHTS):
        w = inp[name].astype(_jnp.float32)
        if MOMENT_SCALE is None:
            s = _jnp.sqrt(_jnp.mean(_jnp.square(w)) + 1e-30)
        else:
            s = MOMENT_SCALE[name]
        km, kv = _jax.random.split(_jax.random.fold_in(key, i + 1))
        out[name] = w
        out["m_" + name] = s * _jax.random.normal(km, w.shape, _jnp.float32)
        out["v_" + name] = (s * s) * _jax.random.uniform(kv, w.shape, _jnp.float32, 0.5, 1.5)
    if N_MICROBATCH > 1:
        for name, axis in PER_EXAMPLE_BATCH_AXIS.items():
            out[name] = _to_microbatches(out[name], axis)
    return {'x': out['x'], 'positions': out['positions'], 'ffn1_norm': out['ffn1_norm'], 'ffn1_w_gate': out['ffn1_w_gate'], 'ffn1_w_up': out['ffn1_w_up'], 'ffn1_w_down': out['ffn1_w_down'], 'mix_norm': out['mix_norm'], 'w_in': out['w_in'], 'q_a_norm': out['q_a_norm'], 'w_q_b': out['w_q_b'], 'kv_a_norm': out['kv_a_norm'], 'w_kv_b': out['w_kv_b'], 'q_head_norm': out['q_head_norm'], 'k_head_norm': out['k_head_norm'], 'conv_w': out['conv_w'], 'conv_b': out['conv_b'], 'a_log_fwd': out['a_log_fwd'], 'a_log_bwd': out['a_log_bwd'], 'dt_bias_fwd': out['dt_bias_fwd'], 'dt_bias_bwd': out['dt_bias_bwd'], 'd_skip': out['d_skip'], 'ssm_norm': out['ssm_norm'], 'w_attn_branch': out['w_attn_branch'], 'w_ssm_branch': out['w_ssm_branch'], 'w_out': out['w_out'], 'ffn2_norm': out['ffn2_norm'], 'ffn2_w_gate': out['ffn2_w_gate'], 'ffn2_w_up': out['ffn2_w_up'], 'ffn2_w_down': out['ffn2_w_down'], 'loss_target': out['loss_target'], 'm_ffn1_norm': out['m_ffn1_norm'], 'm_ffn1_w_gate': out['m_ffn1_w_gate'], 'm_ffn1_w_up': out['m_ffn1_w_up'], 'm_ffn1_w_down': out['m_ffn1_w_down'], 'm_mix_norm': out['m_mix_norm'], 'm_w_in': out['m_w_in'], 'm_q_a_norm': out['m_q_a_norm'], 'm_w_q_b': out['m_w_q_b'], 'm_kv_a_norm': out['m_kv_a_norm'], 'm_w_kv_b': out['m_w_kv_b'], 'm_q_head_norm': out['m_q_head_norm'], 'm_k_head_norm': out['m_k_head_norm'], 'm_conv_w': out['m_conv_w'], 'm_conv_b': out['m_conv_b'], 'm_a_log_fwd': out['m_a_log_fwd'], 'm_a_log_bwd': out['m_a_log_bwd'], 'm_dt_bias_fwd': out['m_dt_bias_fwd'], 'm_dt_bias_bwd': out['m_dt_bias_bwd'], 'm_d_skip': out['m_d_skip'], 'm_ssm_norm': out['m_ssm_norm'], 'm_w_attn_branch': out['m_w_attn_branch'], 'm_w_ssm_branch': out['m_w_ssm_branch'], 'm_w_out': out['m_w_out'], 'm_ffn2_norm': out['m_ffn2_norm'], 'm_ffn2_w_gate': out['m_ffn2_w_gate'], 'm_ffn2_w_up': out['m_ffn2_w_up'], 'm_ffn2_w_down': out['m_ffn2_w_down'], 'v_ffn1_norm': out['v_ffn1_norm'], 'v_ffn1_w_gate': out['v_ffn1_w_gate'], 'v_ffn1_w_up': out['v_ffn1_w_up'], 'v_ffn1_w_down': out['v_ffn1_w_down'], 'v_mix_norm': out['v_mix_norm'], 'v_w_in': out['v_w_in'], 'v_q_a_norm': out['v_q_a_norm'], 'v_w_q_b': out['v_w_q_b'], 'v_kv_a_norm': out['v_kv_a_norm'], 'v_w_kv_b': out['v_w_kv_b'], 'v_q_head_norm': out['v_q_head_norm'], 'v_k_head_norm': out['v_k_head_norm'], 'v_conv_w': out['v_conv_w'], 'v_conv_b': out['v_conv_b'], 'v_a_log_fwd': out['v_a_log_fwd'], 'v_a_log_bwd': out['v_a_log_bwd'], 'v_dt_bias_fwd': out['v_dt_bias_fwd'], 'v_dt_bias_bwd': out['v_dt_bias_bwd'], 'v_d_skip': out['v_d_skip'], 'v_ssm_norm': out['v_ssm_norm'], 'v_w_attn_branch': out['v_w_attn_branch'], 'v_w_ssm_branch': out['v_w_ssm_branch'], 'v_w_out': out['v_w_out'], 'v_ffn2_norm': out['v_ffn2_norm'], 'v_ffn2_w_gate': out['v_ffn2_w_gate'], 'v_ffn2_w_up': out['v_ffn2_w_up'], 'v_ffn2_w_down': out['v_ffn2_w_down']}


def _loss(weights, diff, rest, loss_target):
    with _jax.named_scope("forward"):
        args = {**rest, TWIN_DIFF_INPUT: diff, **{k: w.astype(_WEIGHT_DTYPES[k]) for k, w in weights.items()}}
        y = _forward(args)
    with _jax.named_scope("loss_head"):
        err = _jnp.square(y.astype(_jnp.float32) - loss_target)
        return 0.5 * _jnp.sum(_jnp.mean(err, axis=-1)) if err.ndim else 0.5 * err


def _adamw(w, g, m, v):
    m = ADAM_B1 * m + (1.0 - ADAM_B1) * g
    v = ADAM_B2 * v + (1.0 - ADAM_B2) * _jnp.square(g)
    m_hat = m / (1.0 - ADAM_B1 ** ADAM_STEP)
    v_hat = v / (1.0 - ADAM_B2 ** ADAM_STEP)
    delta = -ADAM_LR * (m_hat / (_jnp.sqrt(v_hat) + ADAM_EPS) + ADAM_WD * w)
    return delta, m, v


def reference(x, positions, ffn1_norm, ffn1_w_gate, ffn1_w_up, ffn1_w_down, mix_norm, w_in, q_a_norm, w_q_b, kv_a_norm, w_kv_b, q_head_norm, k_head_norm, conv_w, conv_b, a_log_fwd, a_log_bwd, dt_bias_fwd, dt_bias_bwd, d_skip, ssm_norm, w_attn_branch, w_ssm_branch, w_out, ffn2_norm, ffn2_w_gate, ffn2_w_up, ffn2_w_down, loss_target, m_ffn1_norm, m_ffn1_w_gate, m_ffn1_w_up, m_ffn1_w_down, m_mix_norm, m_w_in, m_q_a_norm, m_w_q_b, m_kv_a_norm, m_w_kv_b, m_q_head_norm, m_k_head_norm, m_conv_w, m_conv_b, m_a_log_fwd, m_a_log_bwd, m_dt_bias_fwd, m_dt_bias_bwd, m_d_skip, m_ssm_norm, m_w_attn_branch, m_w_ssm_branch, m_w_out, m_ffn2_norm, m_ffn2_w_gate, m_ffn2_w_up, m_ffn2_w_down, v_ffn1_norm, v_ffn1_w_gate, v_ffn1_w_up, v_ffn1_w_down, v_mix_norm, v_w_in, v_q_a_norm, v_w_q_b, v_kv_a_norm, v_w_kv_b, v_q_head_norm, v_k_head_norm, v_conv_w, v_conv_b, v_a_log_fwd, v_a_log_bwd, v_dt_bias_fwd, v_dt_bias_bwd, v_d_skip, v_ssm_norm, v_w_attn_branch, v_w_ssm_branch, v_w_out, v_ffn2_norm, v_ffn2_w_gate, v_ffn2_w_up, v_ffn2_w_down):
    given = dict(x=x, positions=positions, ffn1_norm=ffn1_norm, ffn1_w_gate=ffn1_w_gate, ffn1_w_up=ffn1_w_up, ffn1_w_down=ffn1_w_down, mix_norm=mix_norm, w_in=w_in, q_a_norm=q_a_norm, w_q_b=w_q_b, kv_a_norm=kv_a_norm, w_kv_b=w_kv_b, q_head_norm=q_head_norm, k_head_norm=k_head_norm, conv_w=conv_w, conv_b=conv_b, a_log_fwd=a_log_fwd, a_log_bwd=a_log_bwd, dt_bias_fwd=dt_bias_fwd, dt_bias_bwd=dt_bias_bwd, d_skip=d_skip, ssm_norm=ssm_norm, w_attn_branch=w_attn_branch, w_ssm_branch=w_ssm_branch, w_out=w_out, ffn2_norm=ffn2_norm, ffn2_w_gate=ffn2_w_gate, ffn2_w_up=ffn2_w_up, ffn2_w_down=ffn2_w_down, loss_target=loss_target, m_ffn1_norm=m_ffn1_norm, m_ffn1_w_gate=m_ffn1_w_gate, m_ffn1_w_up=m_ffn1_w_up, m_ffn1_w_down=m_ffn1_w_down, m_mix_norm=m_mix_norm, m_w_in=m_w_in, m_q_a_norm=m_q_a_norm, m_w_q_b=m_w_q_b, m_kv_a_norm=m_kv_a_norm, m_w_kv_b=m_w_kv_b, m_q_head_norm=m_q_head_norm, m_k_head_norm=m_k_head_norm, m_conv_w=m_conv_w, m_conv_b=m_conv_b, m_a_log_fwd=m_a_log_fwd, m_a_log_bwd=m_a_log_bwd, m_dt_bias_fwd=m_dt_bias_fwd, m_dt_bias_bwd=m_dt_bias_bwd, m_d_skip=m_d_skip, m_ssm_norm=m_ssm_norm, m_w_attn_branch=m_w_attn_branch, m_w_ssm_branch=m_w_ssm_branch, m_w_out=m_w_out, m_ffn2_norm=m_ffn2_norm, m_ffn2_w_gate=m_ffn2_w_gate, m_ffn2_w_up=m_ffn2_w_up, m_ffn2_w_down=m_ffn2_w_down, v_ffn1_norm=v_ffn1_norm, v_ffn1_w_gate=v_ffn1_w_gate, v_ffn1_w_up=v_ffn1_w_up, v_ffn1_w_down=v_ffn1_w_down, v_mix_norm=v_mix_norm, v_w_in=v_w_in, v_q_a_norm=v_q_a_norm, v_w_q_b=v_w_q_b, v_kv_a_norm=v_kv_a_norm, v_w_kv_b=v_w_kv_b, v_q_head_norm=v_q_head_norm, v_k_head_norm=v_k_head_norm, v_conv_w=v_conv_w, v_conv_b=v_conv_b, v_a_log_fwd=v_a_log_fwd, v_a_log_bwd=v_a_log_bwd, v_dt_bias_fwd=v_dt_bias_fwd, v_dt_bias_bwd=v_dt_bias_bwd, v_d_skip=v_d_skip, v_ssm_norm=v_ssm_norm, v_w_attn_branch=v_w_attn_branch, v_w_ssm_branch=v_w_ssm_branch, v_w_out=v_w_out, v_ffn2_norm=v_ffn2_norm, v_ffn2_w_gate=v_ffn2_w_gate, v_ffn2_w_up=v_ffn2_w_up, v_ffn2_w_down=v_ffn2_w_down)
    weights = {n: given[n] for n in TWIN_WEIGHTS}
    shared = {n: given[n] for n in SHARED_INPUTS}
    per_example = {n: given[n] for n in ['x', 'positions']}
    grad_fn = _jax.value_and_grad(_loss, argnums=(0, 1))

    def one_microbatch(ex, loss_target):
        ex = dict(ex)
        diff = ex.pop(TWIN_DIFF_INPUT)
        return grad_fn(weights, diff, {**shared, **ex}, loss_target)

    if N_MICROBATCH == 1:
        loss, (grad_w, grad_x) = one_microbatch(per_example, given["loss_target"])
    else:
        def body(carry, xs):
            loss_sum, grad_sum = carry
            l_k, (gw_k, gx_k) = one_microbatch(xs[0], xs[1])
            with _jax.named_scope("update"):
                return (loss_sum + l_k, _jax.tree.map(_jnp.add, grad_sum, gw_k)), gx_k

        init = (_jnp.zeros((), _jnp.float32), _jax.tree.map(_jnp.zeros_like, weights))
        (loss, grad_w), grad_x = _jax.lax.scan(body, init, (per_example, given["loss_target"]))
    with _jax.named_scope("update"):
        delta_w, new_m, new_v = {}, {}, {}
        for n in TWIN_WEIGHTS:
            delta_w[n], new_m[n], new_v[n] = _adamw(weights[n], grad_w[n], given["m_" + n], given["v_" + n])
    return (loss, grad_x, *[grad_w[n] for n in TWIN_WEIGHTS], *[delta_w[n] for n in TWIN_WEIGHTS],
            *[new_m[n] for n in TWIN_WEIGHTS], *[new_v[n] for n in TWIN_WEIGHTS])
```

```python
import functools
import math

import jax
import jax.numpy as jnp
from jax import lax
from jax.experimental import pallas as pl
from jax.experimental.pallas import tpu as pltpu

BF = jnp.bfloat16
F32 = jnp.float32
HI = lax.Precision.HIGHEST
MESH = pl.DeviceIdType.MESH

D_MODEL = 1024
D_FF = 2816
EPS = 1e-6
N_HEADS = 16
QK_NOPE = 64
QK_ROPE = 32
QK_HEAD = 96
V_HEAD = 64
Q_LORA = 384
KV_LORA = 256
ROPE_BASE = 10000.0
D_INNER = 2048
SSM_HEADS = 32
SSM_GROUPS = 4
D_STATE = 128
CONV_WIDTH = 5
CHUNK = 128
XBC_DIM = 3072
HP = 128
GW = D_INNER // SSM_GROUPS
HG = SSM_HEADS // SSM_GROUPS
PH = 64
U_Z, U_GA, U_GB, U_XBC, U_SMALL = 0, 2048, 3072, 4096, 7168
S_CQ, S_CKV, S_KPE, S_DT, SMALL_W = 0, 384, 640, 768, 896
U_PAD = U_SMALL + SMALL_W
IN_SPLITS = (Q_LORA, KV_LORA, QK_ROPE, D_INNER, XBC_DIM, SSM_HEADS, SSM_HEADS, D_MODEL, D_MODEL)

ADAM_LR = 0.001
ADAM_B1 = 0.9
ADAM_B2 = 0.999
ADAM_EPS = 1e-08
ADAM_WD = 0.01
ADAM_STEP = 10

NN = (((1,), (0,)), ((), ()))
NT = (((1,), (1,)), ((), ()))
TN = (((0,), (0,)), ((), ()))


def _pick(n, pref):
    best = None
    d = 128
    while d <= min(n, pref):
        if n % d == 0:
            best = d
        d += 128
    return best if best is not None else n


def _silu(x):
    return x * jax.nn.sigmoid(x)


def _dsilu(x):
    s = jax.nn.sigmoid(x)
    return s * (1.0 + x * (1.0 - s))


def _softplus(x):
    return jnp.maximum(x, 0.0) + jnp.log(1.0 + jnp.exp(-jnp.abs(x)))


def _mm(As, Bs, *, name, ta=False, tb=False, out_dtypes=(F32,), epilogue=None, extras=(), extra_offs=None,
        tm=512, tn=512, tk=2048):
    As, Bs, extras = list(As), list(Bs), list(extras)
    a0, b0 = As[0], Bs[0]
    M, K = (a0.shape[1], a0.shape[0]) if ta else a0.shape
    N = b0.shape[0] if tb else b0.shape[1]
    tm, tn, tk = _pick(M, tm), _pick(N, tn), _pick(K, tk)
    nk = K // tk
    n_a, n_b, n_e, n_o = len(As), len(Bs), len(extras), len(out_dtypes)
    n_acc = n_b if n_a == 1 else 1
    if extra_offs is None:
        extra_offs = (0,) * n_e
    dn = (((0,) if ta else (1,), (1,) if tb else (0,)), ((), ()))

    def body(*refs):
        a_refs, b_refs = refs[:n_a], refs[n_a:n_a + n_b]
        e_refs = refs[n_a + n_b:n_a + n_b + n_e]
        o_refs = refs[n_a + n_b + n_e:n_a + n_b + n_e + n_o]
        acc_refs = refs[n_a + n_b + n_e + n_o:]
        k = pl.program_id(2)

        @pl.when(k == 0)
        def _():
            for acc in acc_refs:
                acc[...] = jnp.zeros_like(acc)

        if n_a == 1:
            a = a_refs[0][...].astype(BF)
            for i in range(n_b):
                acc_refs[i][...] += lax.dot_general(a, b_refs[i][...].astype(BF), dn, preferred_element_type=F32)
        else:
            for i in range(n_a):
                acc_refs[0][...] += lax.dot_general(a_refs[i][...].astype(BF), b_refs[i][...].astype(BF), dn,
                                                    preferred_element_type=F32)

        @pl.when(k == nk - 1)
        def _():
            accs = [acc[...] for acc in acc_refs]
            ex = [e[...] for e in e_refs]
            outs = epilogue(*accs, *ex) if epilogue is not None else tuple(accs)
            for o_ref, val in zip(o_refs, outs):
                o_ref[...] = val.astype(o_ref.dtype)

    a_spec = pl.BlockSpec((tk, tm), lambda i, j, k: (k, i)) if ta else pl.BlockSpec((tm, tk), lambda i, j, k: (i, k))
    b_spec = pl.BlockSpec((tn, tk), lambda i, j, k: (j, k)) if tb else pl.BlockSpec((tk, tn), lambda i, j, k: (k, j))
    e_specs = [pl.BlockSpec((tm, tn), functools.partial(lambda i, j, k, o: (i, j + o), o=off // tn)) for off in extra_offs]
    for off in extra_offs:
        assert off % tn == 0
    outs = pl.pallas_call(
        body, name=name,
        out_shape=tuple(jax.ShapeDtypeStruct((M, N), dt) for dt in out_dtypes),
        grid=(M // tm, N // tn, nk),
        in_specs=[a_spec] * n_a + [b_spec] * n_b + e_specs,
        out_specs=tuple(pl.BlockSpec((tm, tn), lambda i, j, k: (i, j)) for _ in out_dtypes),
        scratch_shapes=[pltpu.VMEM((tm, tn), F32)] * n_acc,
        compiler_params=pltpu.CompilerParams(dimension_semantics=("parallel", "parallel", "arbitrary")),
    )(*As, *Bs, *extras)
    return outs[0] if n_o == 1 else outs


def _rms_fwd(x, g, *, name, blk_w=None, blk_idx=0, off=0, width=None, out_dtype=BF):
    T = x.shape[0]
    blk_w = x.shape[1] if blk_w is None else blk_w
    width = blk_w if width is None else width
    tt = _pick(T, 512)

    def body(x_ref, g_ref, o_ref):
        xf = x_ref[:, off:off + width]
        r = lax.rsqrt(jnp.mean(xf * xf, axis=-1, keepdims=True) + EPS)
        o_ref[...] = (xf * r * g_ref[...]).astype(o_ref.dtype)

    return pl.pallas_call(
        body, name=name, out_shape=jax.ShapeDtypeStruct((T, width), out_dtype), grid=(T // tt,),
        in_specs=[pl.BlockSpec((tt, blk_w), lambda i: (i, blk_idx)), pl.BlockSpec((1, width), lambda i: (0, 0))],
        out_specs=pl.BlockSpec((tt, width), lambda i: (i, 0)),
    )(x, g)


def _rms_bwd(dy, x, g, *, name, blk_w=None, blk_idx=0, off=0, width=None, add=None, out_dtype=F32):
    T = x.shape[0]
    blk_w = x.shape[1] if blk_w is None else blk_w
    width = blk_w if width is None else width
    tt = _pick(T, 512)
    has_add = add is not None

    def body(*refs):
        if has_add:
            dy_ref, x_ref, g_ref, add_ref, dx_ref, dg_ref = refs
        else:
            dy_ref, x_ref, g_ref, dx_ref, dg_ref = refs
        xf = x_ref[:, off:off + width]
        d = dy_ref[...].astype(F32)
        r = lax.rsqrt(jnp.mean(xf * xf, axis=-1, keepdims=True) + EPS)
        gd = d * g_ref[...]
        dx = r * gd - xf * (r * r * r) * jnp.mean(gd * xf, axis=-1, keepdims=True)
        if has_add:
            dx = dx + add_ref[...]
        dx_ref[...] = dx.astype(dx_ref.dtype)

        @pl.when(pl.program_id(0) == 0)
        def _():
            dg_ref[...] = jnp.zeros_like(dg_ref)

        dg_ref[...] += jnp.broadcast_to(jnp.sum(d * xf * r, axis=0, keepdims=True), dg_ref.shape)

    row = pl.BlockSpec((tt, width), lambda i: (i, 0))
    in_specs = [row, pl.BlockSpec((tt, blk_w), lambda i: (i, blk_idx)), pl.BlockSpec((1, width), lambda i: (0, 0))]
    args = [dy, x, g]
    if has_add:
        in_specs.append(row)
        args.append(add)
    return pl.pallas_call(
        body, name=name,
        out_shape=(jax.ShapeDtypeStruct((T, width), out_dtype), jax.ShapeDtypeStruct((8, width), F32)),
        grid=(T // tt,), in_specs=in_specs,
        out_specs=(row, pl.BlockSpec((8, width), lambda i: (0, 0))),
        compiler_params=pltpu.CompilerParams(dimension_semantics=("arbitrary",)),
    )(*args)


def _rope_tables(pos_col, freq_lane):
    T = pos_col.shape[0]
    tt = _pick(T, 512)

    def body(p_ref, f_ref, c_ref, s_ref):
        ang = p_ref[...] * f_ref[...]
        lane = lax.broadcasted_iota(jnp.int32, ang.shape, 1)
        c_ref[...] = jnp.where(lane < QK_HEAD, jnp.cos(ang), 0.0)
        sn = jnp.sin(ang)
        s_ref[...] = jnp.where((lane >= QK_NOPE) & (lane < QK_NOPE + 16), -sn,
                               jnp.where((lane >= QK_NOPE + 16) & (lane < QK_HEAD), sn, 0.0))

    return pl.pallas_call(
        body, name="rope_tables", out_shape=(jax.ShapeDtypeStruct((T, HP), F32),) * 2, grid=(T // tt,),
        in_specs=[pl.BlockSpec((tt, 1), lambda i: (i, 0)), pl.BlockSpec((1, HP), lambda i: (0, 0))],
        out_specs=(pl.BlockSpec((tt, HP), lambda i: (i, 0)),) * 2,
    )(pos_col, freq_lane)


def _swap_rope_halves(n):
    lane = lax.broadcasted_iota(jnp.int32, n.shape, 1)
    lo = (lane >= QK_NOPE) & (lane < QK_NOPE + 16)
    hi = (lane >= QK_NOPE + 16) & (lane < QK_HEAD)
    return jnp.where(lo, pltpu.roll(n, HP - 16, 1), jnp.where(hi, pltpu.roll(n, 16, 1), 0.0))


def _qk_prep_fwd(raw, kpe, gain, C, S, *, name, kpe_blk=0):
    T = raw.shape[0]
    tt = _pick(T, 512)
    has_kpe = kpe is not None

    def body(*refs):
        if has_kpe:
            raw_ref, kpe_ref, g_ref, c_ref, s_ref, o_ref = refs
            xr = raw_ref[...] + kpe_ref[...]
        else:
            raw_ref, g_ref, c_ref, s_ref, o_ref = refs
            xr = raw_ref[...]
        r = lax.rsqrt(jnp.sum(xr * xr, axis=-1, keepdims=True) * (1.0 / QK_HEAD) + EPS)
        n = xr * r * g_ref[...]
        o_ref[...] = (n * c_ref[...] + _swap_rope_halves(n) * s_ref[...]).astype(o_ref.dtype)

    head = pl.BlockSpec((tt, HP), lambda i, h: (i, h))
    shared = pl.BlockSpec((tt, HP), lambda i, h: (i, 0))
    kpe_spec = pl.BlockSpec((tt, HP), lambda i, h: (i, kpe_blk))
    in_specs = [head] + ([kpe_spec] if has_kpe else []) + [pl.BlockSpec((1, HP), lambda i, h: (0, 0)), shared, shared]
    args = [raw] + ([kpe] if has_kpe else []) + [gain, C, S]
    return pl.pallas_call(
        body, name=name, out_shape=jax.ShapeDtypeStruct(raw.shape, BF), grid=(T // tt, N_HEADS),
        in_specs=in_specs, out_specs=head,
    )(*args)


def _qk_prep_bwd(dout, raw, kpe, gain, C, S, *, name, kpe_blk=0):
    T = raw.shape[0]
    tt = _pick(T, 512)
    has_kpe = kpe is not None

    def body(*refs):
        if has_kpe:
            d_ref, raw_ref, kpe_ref, g_ref, c_ref, s_ref, dx_ref, dg_ref, dkpe_ref = refs
            xr = raw_ref[...] + kpe_ref[...]
        else:
            d_ref, raw_ref, g_ref, c_ref, s_ref, dx_ref, dg_ref = refs
            xr = raw_ref[...]
        i, h = pl.program_id(0), pl.program_id(1)
        d = d_ref[...].astype(F32)
        r = lax.rsqrt(jnp.sum(xr * xr, axis=-1, keepdims=True) * (1.0 / QK_HEAD) + EPS)
        dn = d * c_ref[...] + _swap_rope_halves(d * s_ref[...])
        gd = dn * g_ref[...]
        dx = r * gd - xr * (r * r * r) * (jnp.sum(gd * xr, axis=-1, keepdims=True) * (1.0 / QK_HEAD))
        dx_ref[...] = dx.astype(dx_ref.dtype)

        @pl.when((i == 0) & (h == 0))
        def _():
            dg_ref[...] = jnp.zeros_like(dg_ref)

        dg_ref[...] += jnp.broadcast_to(jnp.sum(dn * xr * r, axis=0, keepdims=True), dg_ref.shape)
        if has_kpe:
            @pl.when(h == 0)
            def _():
                dkpe_ref[...] = jnp.zeros_like(dkpe_ref)

            dkpe_ref[...] += dx

    head = pl.BlockSpec((tt, HP), lambda i, h: (i, h))
    shared = pl.BlockSpec((tt, HP), lambda i, h: (i, 0))
    kpe_spec = pl.BlockSpec((tt, HP), lambda i, h: (i, kpe_blk))
    in_specs = [head, head] + ([kpe_spec] if has_kpe else []) + [pl.BlockSpec((1, HP), lambda i, h: (0, 0)), shared, shared]
    args = [dout, raw] + ([kpe] if has_kpe else []) + [gain, C, S]
    out_shape = [jax.ShapeDtypeStruct(raw.shape, BF), jax.ShapeDtypeStruct((8, HP), F32)]
    out_specs = [head, pl.BlockSpec((8, HP), lambda i, h: (0, 0))]
    if has_kpe:
        out_shape.append(jax.ShapeDtypeStruct((T, HP), F32))
        out_specs.append(shared)
    return pl.pallas_call(
        body, name=name, out_shape=tuple(out_shape), grid=(T // tt, N_HEADS),
        in_specs=in_specs, out_specs=tuple(out_specs),
        compiler_params=pltpu.CompilerParams(dimension_semantics=("arbitrary", "arbitrary")),
    )(*args)


ATTN_SCALE = 1.0 / math.sqrt(QK_HEAD)


def _attn_fwd(q, k, v):
    T = q.shape[0]
    tq = _pick(T, 256)

    def body(q_ref, k_ref, v_ref, o_ref, lse_ref):
        s = lax.dot_general(q_ref[...], k_ref[...], NT, preferred_element_type=F32) * ATTN_SCALE
        m = jnp.max(s, axis=-1, keepdims=True)
        p = jnp.exp(s - m)
        l = jnp.sum(p, axis=-1, keepdims=True)
        o = jnp.dot(p.astype(BF), v_ref[...], preferred_element_type=F32)
        o_ref[...] = o / l
        lse_ref[...] = jnp.broadcast_to(m + jnp.log(l), lse_ref.shape)

    qs = pl.BlockSpec((tq, HP), lambda h, i: (i, h))
    kv = pl.BlockSpec((T, HP), lambda h, i: (0, h))
    return pl.pallas_call(
        body, name="attn_fwd", out_shape=(jax.ShapeDtypeStruct(q.shape, F32),) * 2, grid=(N_HEADS, T // tq),
        in_specs=[qs, kv, kv], out_specs=(qs, qs),
        compiler_params=pltpu.CompilerParams(dimension_semantics=("parallel", "parallel")),
    )(q, k, v)


def _attn_bwd(q, k, v, do, o, lse):
    T = q.shape[0]
    tb = _pick(T, 512)
    nb = T // tb

    def body(q_ref, k_ref, v_ref, do_ref, o_ref, lse_ref, dq_ref, dk_ref, dv_ref):
        dq_ref[...] = jnp.zeros_like(dq_ref)

        def k_loop(j, carry):
            ks = pl.ds(pl.multiple_of(j * tb, tb), tb)
            kj, vj = k_ref[ks, :], v_ref[ks, :]

            def q_loop(i, acc):
                dk_acc, dv_acc = acc
                qs = pl.ds(pl.multiple_of(i * tb, tb), tb)
                qi = q_ref[qs, :]
                doi = do_ref[qs, :]
                delta = jnp.sum(doi * o_ref[qs, :], axis=-1, keepdims=True)
                dob = doi.astype(BF)
                s = lax.dot_general(qi, kj, NT, preferred_element_type=F32) * ATTN_SCALE
                p = jnp.exp(s - lse_ref[qs, 0:1])
                dp = lax.dot_general(dob, vj, NT, preferred_element_type=F32)
                ds = (p * (dp - delta) * ATTN_SCALE).astype(BF)
                dv_acc = dv_acc + lax.dot_general(p.astype(BF), dob, TN, preferred_element_type=F32)
                dk_acc = dk_acc + lax.dot_general(ds, qi, TN, preferred_element_type=F32)
                dq_ref[qs, :] += jnp.dot(ds, kj, preferred_element_type=F32)
                return dk_acc, dv_acc

            zero = jnp.zeros((tb, HP), F32)
            dk_acc, dv_acc = lax.fori_loop(0, nb, q_loop, (zero, zero))
            dk_ref[ks, :] = dk_acc
            dv_ref[ks, :] = dv_acc.astype(dv_ref.dtype)
            return carry

        lax.fori_loop(0, nb, k_loop, 0)

    spec = pl.BlockSpec((T, HP), lambda h: (0, h))
    return pl.pallas_call(
        body, name="attn_bwd",
        out_shape=(jax.ShapeDtypeStruct(q.shape, F32), jax.ShapeDtypeStruct(q.shape, F32), jax.ShapeDtypeStruct(q.shape, BF)),
        grid=(N_HEADS,), in_specs=[spec] * 6, out_specs=(spec,) * 3,
        compiler_params=pltpu.CompilerParams(dimension_semantics=("parallel",)),
    )(q, k, v, do, o, lse)


CONV_TC = 512
CONV_PAD = CONV_WIDTH // 2


def _halo_specs(tr, col_of):
    r8 = tr // 8
    cur = pl.BlockSpec((tr, CONV_TC), lambda j, i: (i, col_of(j)))
    prev = pl.BlockSpec((8, CONV_TC), lambda j, i: (jnp.maximum(i * r8 - 1, 0), col_of(j)))

    def nxt_map(j, i, n8):
        return (jnp.minimum((i + 1) * r8, n8 - 1), col_of(j))

    return cur, prev, nxt_map


def _with_halo(prev_ref, cur_ref, next_ref, i, n_i):
    prev = jnp.where(i == 0, 0.0, prev_ref[...].astype(F32))
    nxt = jnp.where(i == n_i - 1, 0.0, next_ref[...].astype(F32))
    return jnp.concatenate([prev, cur_ref[...].astype(F32), nxt], axis=0)


def _conv_fwd(u, w8, b):
    T = u.shape[0]
    tr = _pick(T, 512)
    n_i = T // tr
    c0 = U_XBC // CONV_TC
    cur, prev, nxt_map = _halo_specs(tr, lambda j: c0 + j)
    nxt = pl.BlockSpec((8, CONV_TC), functools.partial(nxt_map, n8=T // 8))

    def body(p_ref, c_ref, n_ref, w_ref, b_ref, pre_ref, act_ref):
        i = pl.program_id(1)
        full = _with_halo(p_ref, c_ref, n_ref, i, n_i)
        acc = jnp.broadcast_to(b_ref[...], (tr, CONV_TC))
        for kk in range(CONV_WIDTH):
            acc = acc + full[8 - CONV_PAD + kk:8 - CONV_PAD + kk + tr, :] * w_ref[kk:kk + 1, :]
        pre_ref[...] = acc
        act_ref[...] = _silu(acc)

    out = pl.BlockSpec((tr, CONV_TC), lambda j, i: (i, j))
    return pl.pallas_call(
        body, name="conv_fwd", out_shape=(jax.ShapeDtypeStruct((T, XBC_DIM), F32),) * 2,
        grid=(XBC_DIM // CONV_TC, n_i),
        in_specs=[prev, cur, nxt, pl.BlockSpec((8, CONV_TC), lambda j, i: (0, j)), pl.BlockSpec((1, CONV_TC), lambda j, i: (0, j))],
        out_specs=(out, out),
    )(u, u, u, w8, b)


def _conv_dpre(dacts, pre, col0, *, name):
    T, width = dacts[0].shape
    tt = _pick(T, 512)
    n_d = len(dacts)
    c0 = col0 // CONV_TC

    def body(*refs):
        d = refs[0][...]
        for r in refs[1:n_d]:
            d = d + r[...]
        refs[n_d + 1][...] = d * _dsilu(refs[n_d][...])

    blk = pl.BlockSpec((tt, CONV_TC), lambda j, i: (i, j))
    return pl.pallas_call(
        body, name=name, out_shape=jax.ShapeDtypeStruct((T, width), F32), grid=(width // CONV_TC, T // tt),
        in_specs=[blk] * n_d + [pl.BlockSpec((tt, CONV_TC), lambda j, i: (i, c0 + j))], out_specs=blk,
    )(*dacts, pre)


def _conv_bwd(dpre, u, w8, col0, *, name):
    T, width = dpre.shape
    tr = _pick(T, 512)
    n_i = T // tr
    cd = col0 // CONV_TC
    cx = (U_XBC + col0) // CONV_TC
    d_cur, d_prev, d_nxt_map = _halo_specs(tr, lambda j: j)
    x_cur, x_prev, x_nxt_map = _halo_specs(tr, lambda j: cx + j)
    d_nxt = pl.BlockSpec((8, CONV_TC), functools.partial(d_nxt_map, n8=T // 8))
    x_nxt = pl.BlockSpec((8, CONV_TC), functools.partial(x_nxt_map, n8=T // 8))

    def body(dp_ref, dc_ref, dn_ref, xp_ref, xc_ref, xn_ref, w_ref, dx_ref, dw_ref):
        i = pl.program_id(1)
        dfull = _with_halo(dp_ref, dc_ref, dn_ref, i, n_i)
        xfull = _with_halo(xp_ref, xc_ref, xn_ref, i, n_i)
        dcur = dc_ref[...]
        dx = jnp.zeros((tr, CONV_TC), F32)
        rows = []
        for kk in range(CONV_WIDTH):
            dx = dx + dfull[8 + CONV_PAD - kk:8 + CONV_PAD - kk + tr, :] * w_ref[kk:kk + 1, :]
            rows.append(jnp.sum(dcur * xfull[8 - CONV_PAD + kk:8 - CONV_PAD + kk + tr, :], axis=0, keepdims=True))
        rows.append(jnp.sum(dcur, axis=0, keepdims=True))
        rows.append(jnp.zeros((2, CONV_TC), F32))
        dx_ref[...] = dx.astype(dx_ref.dtype)

        @pl.when(i == 0)
        def _():
            dw_ref[...] = jnp.zeros_like(dw_ref)

        dw_ref[...] += jnp.concatenate(rows, axis=0)

    out = pl.BlockSpec((tr, CONV_TC), lambda j, i: (i, j))
    return pl.pallas_call(
        body, name=name, out_shape=(jax.ShapeDtypeStruct((T, width), BF), jax.ShapeDtypeStruct((8, width), F32)),
        grid=(width // CONV_TC, n_i),
        in_specs=[d_prev, d_cur, d_nxt, x_prev, x_cur, x_nxt, pl.BlockSpec((8, CONV_TC), lambda j, i: (0, cd + j))],
        out_specs=(out, pl.BlockSpec((8, CONV_TC), lambda j, i: (0, j))),
        compiler_params=pltpu.CompilerParams(dimension_semantics=("parallel", "arbitrary")),
    )(dpre, dpre, dpre, u, u, u, w8)


def _ssd_common(dt_ref, bias_ref, a_ref, b_ref, c_ref, g, rev):
    rows = lax.broadcasted_iota(jnp.int32, (CHUNK, CHUNK), 0)
    cols = lax.broadcasted_iota(jnp.int32, (CHUNK, CHUNK), 1)
    head0 = (SSM_HEADS if rev else 0) + HG * g
    sel = jnp.where((rows == head0 + cols) & (cols < HG), 1.0, 0.0).astype(F32)
    pre = jnp.dot(dt_ref[...] + bias_ref[0:1, :], sel, precision=HI, preferred_element_type=F32)
    dt = _softplus(pre)
    a_sel = jnp.dot(-jnp.exp(a_ref[...]), sel, precision=HI, preferred_element_type=F32)[0:1, :]
    da = dt * a_sel
    incl = (cols >= rows) if rev else (cols <= rows)
    tri = jnp.where(incl, 1.0, 0.0).astype(F32)
    cs = jnp.dot(tri, da, precision=HI, preferred_element_type=F32)
    tot = cs[0:1, :] if rev else cs[CHUNK - 1:CHUNK, :]
    bm, cm = b_ref[...].astype(BF), c_ref[...].astype(BF)
    cb = lax.dot_general(cm, bm, NT, preferred_element_type=F32)
    return dict(sel=sel, pre=pre, dt=dt, a_sel=a_sel, cs=cs, csT=cs.T, tot=tot, bm=bm, cm=cm, cb=cb, incl=incl, tri=tri)


def _ssd_specs(T, rev, bwd):
    nc = T // CHUNK
    fwd_order = (lambda c: nc - 1 - c) if rev else (lambda c: c)
    cm = (lambda c: fwd_order(nc - 1 - c)) if bwd else fwd_order
    xs = pl.BlockSpec((CHUNK, GW), lambda c, g: (cm(c), g))
    bs = pl.BlockSpec((CHUNK, D_STATE), lambda c, g: (cm(c), D_INNER // D_STATE + g))
    cs = pl.BlockSpec((CHUNK, D_STATE), lambda c, g: (cm(c), (D_INNER + SSM_GROUPS * D_STATE) // D_STATE + g))
    dt = pl.BlockSpec((CHUNK, HP), lambda c, g: (cm(c), (U_SMALL + S_DT) // HP))
    vec = pl.BlockSpec((8, HP), lambda c, g: (0, 0))
    st = pl.BlockSpec((1, GW, D_STATE), lambda c, g: (cm(c), g, 0))
    return nc, cm, xs, bs, cs, dt, vec, st


def _ssd_fwd(act, u, bias8, a8, *, rev, name):
    T = act.shape[0]
    nc, cm, xs_s, b_s, c_s, dt_s, vec_s, st_s = _ssd_specs(T, rev, False)

    def body(x_ref, b_ref, c_ref, dt_ref, bias_ref, a_ref, y_ref, st_ref, state):
        c, g = pl.program_id(0), pl.program_id(1)

        @pl.when(c == 0)
        def _():
            state[g] = jnp.zeros((GW, D_STATE), F32)

        q = _ssd_common(dt_ref, bias_ref, a_ref, b_ref, c_ref, g, rev)
        for hh in range(HG):
            hs = slice(PH * hh, PH * (hh + 1))
            col, row = q["cs"][:, hh:hh + 1], q["csT"][hh:hh + 1, :]
            xdt = x_ref[:, hs] * q["dt"][:, hh:hh + 1]
            lmat = jnp.where(q["incl"], jnp.exp(col - row), 0.0)
            yd = jnp.dot((q["cb"] * lmat).astype(BF), xdt.astype(BF), preferred_element_type=F32)
            prev = state[g, hs, :]
            yo = lax.dot_general(q["cm"], prev.astype(BF), NT, preferred_element_type=F32) * jnp.exp(col)
            y_ref[:, hs] = yd + yo
            tot_h = q["tot"][:, hh:hh + 1]
            s_new = lax.dot_general((xdt * jnp.exp(tot_h - col)).astype(BF), q["bm"], TN, preferred_element_type=F32)
            st_ref[0, hs, :] = prev
            state[g, hs, :] = prev * jnp.exp(tot_h) + s_new

    return pl.pallas_call(
        body, name=name,
        out_shape=(jax.ShapeDtypeStruct((T, D_INNER), F32), jax.ShapeDtypeStruct((nc, D_INNER, D_STATE), F32)),
        grid=(nc, SSM_GROUPS), in_specs=[xs_s, b_s, c_s, dt_s, vec_s, vec_s], out_specs=(xs_s, st_s),
        scratch_shapes=[pltpu.VMEM((SSM_GROUPS, GW, D_STATE), F32)],
        compiler_params=pltpu.CompilerParams(dimension_semantics=("arbitrary", "arbitrary")),
    )(act, act, act, u, bias8, a8)


def _ssd_bwd(act, u, bias8, a8, states, dy, ddt_in, *, rev, name):
    T = act.shape[0]
    nc, cm, xs_s, b_s, c_s, dt_s, vec_s, st_s = _ssd_specs(T, rev, True)

    def body(x_ref, b_ref, c_ref, dt_ref, bias_ref, a_ref, st_ref, dy_ref, ddt_in_ref,
             dx_ref, db_ref, dc_ref, ddt_ref, da_ref, dbias_ref, dstate):
        c, g = pl.program_id(0), pl.program_id(1)

        @pl.when(c == 0)
        def _():
            dstate[g] = jnp.zeros((GW, D_STATE), F32)

        @pl.when((c == 0) & (g == 0))
        def _():
            da_ref[...] = jnp.zeros_like(da_ref)
            dbias_ref[...] = jnp.zeros_like(dbias_ref)

        @pl.when(g == 0)
        def _():
            ddt_ref[...] = ddt_in_ref[...]

        q = _ssd_common(dt_ref, bias_ref, a_ref, b_ref, c_ref, g, rev)
        bm, cmat = q["bm"], q["cm"]
        lane = lax.broadcasted_iota(jnp.int32, (1, CHUNK), 1)
        sub = lax.broadcasted_iota(jnp.int32, (CHUNK, 1), 0)
        zero = jnp.zeros((CHUNK, CHUNK), F32)
        dcb, db_acc, dc_acc = zero, zero, zero
        dcs_col, dcs_row, ddt_x, dtot = zero, zero, zero, jnp.zeros((1, CHUNK), F32)
        for hh in range(HG):
            hs = slice(PH * hh, PH * (hh + 1))
            col, row = q["cs"][:, hh:hh + 1], q["csT"][hh:hh + 1, :]
            tot_h = q["tot"][:, hh:hh + 1]
            x = x_ref[:, hs]
            dth = q["dt"][:, hh:hh + 1]
            xdt = x * dth
            xdb = xdt.astype(BF)
            lmat = jnp.where(q["incl"], jnp.exp(col - row), 0.0)
            mmat = q["cb"] * lmat
            prev = st_ref[0, hs, :]
            pb = prev.astype(BF)
            ds_ = dstate[g, hs, :]
            dsb = ds_.astype(BF)
            dyh = dy_ref[:, hs]
            dyb = dyh.astype(BF)
            e = jnp.exp(col)
            w = jnp.exp(tot_h - col)
            etot = jnp.exp(tot_h)
            dprev = lax.dot_general((dyh * e).astype(BF), cmat, TN, preferred_element_type=F32) + ds_ * etot
            cp = lax.dot_general(cmat, pb, NT, preferred_element_type=F32)
            dcs_h = jnp.sum(dyh * cp, axis=1, keepdims=True) * e
            dc_acc = dc_acc + jnp.dot(dyb, pb, preferred_element_type=F32) * e
            dm = lax.dot_general(dyb, xdb, NT, preferred_element_type=F32)
            dxdt = lax.dot_general(mmat.astype(BF), dyb, TN, preferred_element_type=F32)
            qm = dm * mmat
            dcs_h = dcs_h + jnp.sum(qm, axis=1, keepdims=True)
            dcs_row = dcs_row + jnp.where(sub == hh, jnp.sum(qm, axis=0, keepdims=True), 0.0)
            dcb = dcb + dm * lmat
            bds = lax.dot_general(bm, dsb, NT, preferred_element_type=F32) * w
            dxdt = dxdt + bds
            db_acc = db_acc + jnp.dot(xdb, dsb, preferred_element_type=F32) * w
            t = jnp.sum(xdt * bds, axis=1, keepdims=True)
            dtot_h = jnp.sum(t) + jnp.sum(ds_ * prev) * etot
            dcs_h = dcs_h - t
            dcs_col = dcs_col + jnp.where(lane == hh, dcs_h, 0.0)
            dtot = dtot + jnp.where(lane == hh, dtot_h, 0.0)
            ddt_x = ddt_x + jnp.where(lane == hh, jnp.sum(dxdt * x, axis=1, keepdims=True), 0.0)
            dx_ref[:, hs] = dxdt * dth
            dstate[g, hs, :] = dprev
        dcbb = dcb.astype(BF)
        dc_ref[...] = dc_acc + jnp.dot(dcbb, bm, preferred_element_type=F32)
        db_ref[...] = db_acc + lax.dot_general(dcbb, cmat, TN, preferred_element_type=F32)
        dcs = dcs_col - dcs_row.T
        tri_t = jnp.where(q["incl"], 0.0, 1.0).astype(F32) + jnp.where(
            lax.broadcasted_iota(jnp.int32, (CHUNK, CHUNK), 0) == lax.broadcasted_iota(jnp.int32, (CHUNK, CHUNK), 1), 1.0, 0.0)
        dda = jnp.dot(tri_t, dcs, precision=HI, preferred_element_type=F32) + dtot
        ddt = ddt_x + dda * q["a_sel"]
        dpre = ddt * jax.nn.sigmoid(q["pre"])
        dpre_full = lax.dot_general(dpre, q["sel"], NT, precision=HI, preferred_element_type=F32)
        ddt_ref[...] += dpre_full
        dbias_ref[...] += jnp.broadcast_to(jnp.sum(dpre_full, axis=0, keepdims=True), (8, CHUNK))
        dalog_sel = jnp.broadcast_to(jnp.sum(dda * q["dt"], axis=0, keepdims=True) * q["a_sel"], (8, CHUNK))
        da_ref[...] += lax.dot_general(dalog_sel, q["sel"], NT, precision=HI, preferred_element_type=F32)

    bc_out = pl.BlockSpec((CHUNK, D_STATE), lambda c, g: (cm(c), g))
    dt_out = pl.BlockSpec((CHUNK, HP), lambda c, g: (cm(c), 0))
    return pl.pallas_call(
        body, name=name,
        out_shape=(jax.ShapeDtypeStruct((T, D_INNER), F32), jax.ShapeDtypeStruct((T, SSM_GROUPS * D_STATE), F32),
                   jax.ShapeDtypeStruct((T, SSM_GROUPS * D_STATE), F32), jax.ShapeDtypeStruct((T, HP), F32),
                   jax.ShapeDtypeStruct((8, HP), F32), jax.ShapeDtypeStruct((8, HP), F32)),
        grid=(nc, SSM_GROUPS), in_specs=[xs_s, b_s, c_s, dt_s, vec_s, vec_s, st_s, xs_s, dt_out],
        out_specs=(xs_s, bc_out, bc_out, dt_out, vec_s, vec_s),
        scratch_shapes=[pltpu.VMEM((SSM_GROUPS, GW, D_STATE), F32)],
        compiler_params=pltpu.CompilerParams(dimension_semantics=("arbitrary", "arbitrary")),
    )(act, act, act, u, bias8, a8, states, dy, ddt_in)


def _ssm_combine_fwd(y_f, y_b, act, u, dskip, gain):
    T = y_f.shape[0]
    tt = _pick(T, 256)

    def body(yf_ref, yb_ref, x_ref, z_ref, ds_ref, g_ref, y_ref, m_ref):
        y = yf_ref[...] + yb_ref[...] + ds_ref[...] * x_ref[...]
        y2 = y * _silu(z_ref[...])
        r = lax.rsqrt(jnp.mean(y2 * y2, axis=-1, keepdims=True) + EPS)
        y_ref[...] = y
        m_ref[...] = (y2 * r * g_ref[...]).astype(m_ref.dtype)

    blk = pl.BlockSpec((tt, GW), lambda i, g: (i, g))
    vec = pl.BlockSpec((1, GW), lambda i, g: (0, g))
    return pl.pallas_call(
        body, name="ssm_combine_fwd",
        out_shape=(jax.ShapeDtypeStruct((T, D_INNER), F32), jax.ShapeDtypeStruct((T, D_INNER), BF)),
        grid=(T // tt, SSM_GROUPS), in_specs=[blk, blk, blk, blk, vec, vec], out_specs=(blk, blk),
    )(y_f, y_b, act, u, dskip, gain)


def _ssm_combine_bwd(dm, y, act, u, dskip, gain):
    T = y.shape[0]
    tt = _pick(T, 256)

    def body(dm_ref, y_ref, x_ref, z_ref, ds_ref, g_ref, dy_ref, dz_ref, dxs_ref, dg_ref, dsk_ref):
        z = z_ref[...]
        y = y_ref[...]
        x = x_ref[...]
        sz = _silu(z)
        y2 = y * sz
        r = lax.rsqrt(jnp.mean(y2 * y2, axis=-1, keepdims=True) + EPS)
        d = dm_ref[...]
        gd = d * g_ref[...]
        dy2 = r * gd - y2 * (r * r * r) * jnp.mean(gd * y2, axis=-1, keepdims=True)
        dy = dy2 * sz
        dy_ref[...] = dy
        dz_ref[...] = (dy2 * y * _dsilu(z)).astype(dz_ref.dtype)
        dxs_ref[...] = dy * ds_ref[...]

        @pl.when(pl.program_id(1) == 0)
        def _():
            dg_ref[...] = jnp.zeros_like(dg_ref)
            dsk_ref[...] = jnp.zeros_like(dsk_ref)

        dg_ref[...] += jnp.broadcast_to(jnp.sum(d * y2 * r, axis=0, keepdims=True), dg_ref.shape)
        lane_sum = jnp.broadcast_to(jnp.sum(dy * x, axis=0, keepdims=True), (8, GW))
        src = lax.broadcasted_iota(jnp.int32, (GW, HP), 0)
        head = lax.broadcasted_iota(jnp.int32, (GW, HP), 1)
        to_head = jnp.where((src >= PH * head) & (src < PH * (head + 1)), 1.0, 0.0).astype(F32)
        dsk_ref[...] += jnp.dot(lane_sum, to_head, precision=HI, preferred_element_type=F32)

    blk = pl.BlockSpec((tt, GW), lambda g, i: (i, g))
    vec = pl.BlockSpec((1, GW), lambda g, i: (0, g))
    acc = pl.BlockSpec((8, GW), lambda g, i: (0, g))
    return pl.pallas_call(
        body, name="ssm_combine_bwd",
        out_shape=(jax.ShapeDtypeStruct((T, D_INNER), F32), jax.ShapeDtypeStruct((T, D_INNER), BF),
                   jax.ShapeDtypeStruct((T, D_INNER), F32), jax.ShapeDtypeStruct((8, D_INNER), F32),
                   jax.ShapeDtypeStruct((8, SSM_GROUPS * HP), F32)),
        grid=(SSM_GROUPS, T // tt), in_specs=[blk, blk, blk, blk, vec, vec],
        out_specs=(blk, blk, blk, acc, pl.BlockSpec((8, HP), lambda g, i: (0, g))),
        compiler_params=pltpu.CompilerParams(dimension_semantics=("parallel", "arbitrary")),
    )(dm, y, act, u, dskip, gain)


def _loss_head(y, target):
    T, D = y.shape
    tt = _pick(T, 512)

    def body(y_ref, t_ref, dy_ref, l_ref):
        e = y_ref[...] - t_ref[...]
        dy_ref[...] = e * (1.0 / D)

        @pl.when(pl.program_id(0) == 0)
        def _():
            l_ref[...] = jnp.zeros_like(l_ref)

        l_ref[...] += jnp.sum(e * e) * (0.5 / D)

    blk = pl.BlockSpec((tt, D), lambda i: (i, 0))
    return pl.pallas_call(
        body, name="loss_head",
        out_shape=(jax.ShapeDtypeStruct((T, D), F32), jax.ShapeDtypeStruct((8, 128), F32)),
        grid=(T // tt,), in_specs=[blk, blk], out_specs=(blk, pl.BlockSpec((8, 128), lambda i: (0, 0))),
        compiler_params=pltpu.CompilerParams(dimension_semantics=("arbitrary",)),
    )(y, target)


def _adamw(w, g, m, v, *, name):
    R, C = w.shape
    cap = max(8, (1 << 18) // C)
    tr = R
    if R % 8 == 0:
        tr = 8
        for cand in range(8, min(R, cap) + 1, 8):
            if R % cand == 0:
                tr = cand

    def body(w_ref, g_ref, m_ref, v_ref, d_ref, nm_ref, nv_ref):
        gg = g_ref[...]
        nm = ADAM_B1 * m_ref[...] + (1.0 - ADAM_B1) * gg
        nv = ADAM_B2 * v_ref[...] + (1.0 - ADAM_B2) * jnp.square(gg)
        m_hat = nm / (1.0 - ADAM_B1 ** ADAM_STEP)
        v_hat = nv / (1.0 - ADAM_B2 ** ADAM_STEP)
        d_ref[...] = -ADAM_LR * (m_hat / (jnp.sqrt(v_hat) + ADAM_EPS) + ADAM_WD * w_ref[...])
        nm_ref[...] = nm
        nv_ref[...] = nv

    blk = pl.BlockSpec((tr, C), lambda i: (i, 0))
    return pl.pallas_call(
        body, name=name, out_shape=(jax.ShapeDtypeStruct((R, C), F32),) * 3, grid=(R // tr,),
        in_specs=[blk] * 4, out_specs=(blk,) * 3,
    )(w, g, m, v)


ANY = pl.BlockSpec(memory_space=pl.ANY)


def _chip_peers():
    x, y, c = lax.axis_index("x"), lax.axis_index("y"), lax.axis_index("c")
    return x, y, c, [(1 - x, y), (x, 1 - y), (1 - x, 1 - y)]


def _gather_chips(wb, wf):
    def body(wb_ref, wf_ref, ob_ref, of_ref, send_sems, recv_sems, loc_sems):
        x, y, c, peers = _chip_peers()
        me = 2 * x + y
        local = [pltpu.make_async_copy(wb_ref, ob_ref.at[me], loc_sems.at[0]),
                 pltpu.make_async_copy(wf_ref, of_ref.at[me], loc_sems.at[1])]
        for cp in local:
            cp.start()

        def copies(src_slot_of):
            out = []
            for k, (px, py) in enumerate(peers):
                slot = src_slot_of(px, py)
                for n, (src, dst) in enumerate(((wb_ref, ob_ref), (wf_ref, of_ref))):
                    out.append(pltpu.make_async_remote_copy(
                        src_ref=src, dst_ref=dst.at[slot], send_sem=send_sems.at[2 * k + n], recv_sem=recv_sems.at[2 * k + n],
                        device_id=(px, py, c), device_id_type=MESH))
            return out

        sends = copies(lambda px, py: me)
        for cp in sends:
            cp.start()
        for cp in copies(lambda px, py: 2 * px + py):
            cp.wait_recv()
        for cp in sends:
            cp.wait_send()
        for cp in local:
            cp.wait()

    return pl.pallas_call(
        body, name="gather_weights",
        out_shape=(jax.ShapeDtypeStruct((4,) + wb.shape, wb.dtype), jax.ShapeDtypeStruct((4,) + wf.shape, wf.dtype)),
        in_specs=[ANY, ANY], out_specs=(ANY, ANY),
        scratch_shapes=[pltpu.SemaphoreType.DMA((6,)), pltpu.SemaphoreType.DMA((6,)), pltpu.SemaphoreType.DMA((2,))],
    )(wb, wf)


def _exchange_chips(gp):
    def body(gp_ref, out_ref, send_sems, recv_sems, loc_sem):
        x, y, c, peers = _chip_peers()
        me = 2 * x + y
        local = pltpu.make_async_copy(gp_ref.at[me], out_ref.at[me], loc_sem)
        local.start()

        def copies(sending):
            out = []
            for k, (px, py) in enumerate(peers):
                p = 2 * px + py
                out.append(pltpu.make_async_remote_copy(
                    src_ref=gp_ref.at[p], dst_ref=out_ref.at[me if sending else p],
                    send_sem=send_sems.at[k], recv_sem=recv_sems.at[k], device_id=(px, py, c), device_id_type=MESH))
            return out

        sends = copies(True)
        for cp in sends:
            cp.start()
        for cp in copies(False):
            cp.wait_recv()
        for cp in sends:
            cp.wait_send()
        local.wait()

    return pl.pallas_call(
        body, name="exchange_grads", out_shape=jax.ShapeDtypeStruct(gp.shape, gp.dtype),
        in_specs=[ANY], out_specs=ANY,
        scratch_shapes=[pltpu.SemaphoreType.DMA((3,)), pltpu.SemaphoreType.DMA((3,)), pltpu.SemaphoreType.DMA],
    )(gp)


def _sum_slots(r4):
    _, R, C = r4.shape
    tr = 16
    for cand in range(16, 513, 16):
        if R % cand == 0:
            tr = cand

    def body(r_ref, o_ref):
        acc = r_ref[0].astype(F32)
        for s in range(1, 4):
            acc = acc + r_ref[s].astype(F32)
        o_ref[...] = acc

    return pl.pallas_call(
        body, name="sum_slots", out_shape=jax.ShapeDtypeStruct((R, C), F32), grid=(R // tr,),
        in_specs=[pl.BlockSpec((4, tr, C), lambda i: (0, i, 0))], out_specs=pl.BlockSpec((tr, C), lambda i: (i, 0)),
    )(r4)


def _swap_cores(s):
    def body(s_ref, o_ref, send_sem, recv_sem):
        x, y, c = lax.axis_index("x"), lax.axis_index("y"), lax.axis_index("c")
        cp = pltpu.make_async_remote_copy(src_ref=s_ref, dst_ref=o_ref, send_sem=send_sem, recv_sem=recv_sem,
                                          device_id=(x, y, 1 - c), device_id_type=MESH)
        cp.start()
        cp.wait()

    return pl.pallas_call(
        body, name="swap_cores", out_shape=jax.ShapeDtypeStruct(s.shape, s.dtype), in_specs=[ANY], out_specs=ANY,
        scratch_shapes=[pltpu.SemaphoreType.DMA, pltpu.SemaphoreType.DMA],
    )(s)


def _add2(a, b):
    R, C = a.shape
    tr = 16
    for cand in range(16, 513, 16):
        if R % cand == 0:
            tr = cand

    def body(a_ref, b_ref, o_ref):
        o_ref[...] = a_ref[...] + b_ref[...]

    blk = pl.BlockSpec((tr, C), lambda i: (i, 0))
    return pl.pallas_call(body, name="add_cores", out_shape=jax.ShapeDtypeStruct((R, C), F32), grid=(R // tr,),
                          in_specs=[blk, blk], out_specs=blk)(a, b)


N_DEV = 8


def _allreduce_small(p):
    rs = p.shape[0]

    def body(x_ref, sum_ref, all_ref, send_sems, recv_sems, local_sem):
        x, y, c = lax.axis_index("x"), lax.axis_index("y"), lax.axis_index("c")
        me, sibling = (x, y, c), (x, y, 1 - c)
        chips = [(1 - x, y), (x, 1 - y), (1 - x, 1 - y)]

        def rows(px, py, pc):
            return all_ref.at[pl.ds((4 * px + 2 * py + pc) * rs, rs), :]

        def copy(k, block, to, src=None):
            return pltpu.make_async_remote_copy(
                src_ref=rows(*block) if src is None else src, dst_ref=rows(*block),
                send_sem=send_sems.at[k], recv_sem=recv_sems.at[k], device_id=to, device_id_type=MESH)

        mine = pltpu.make_async_copy(x_ref, rows(*me), local_sem)
        mine.start()
        first = [copy(0, me, sibling, src=x_ref)]
        first += [copy(1 + j, me, (*chip, c), src=x_ref) for j, chip in enumerate(chips)]
        for cp in first:
            cp.start()
        passed = [copy(4 + j, (*chip, c), sibling) for j, chip in enumerate(chips)]
        for j, chip in enumerate(chips):
            copy(1 + j, (*chip, c), me).wait_recv()
            passed[j].start()
        copy(0, sibling, me).wait_recv()
        for j, chip in enumerate(chips):
            copy(4 + j, (*chip, 1 - c), me).wait_recv()
        for cp in first + passed:
            cp.wait_send()
        mine.wait()
        acc = all_ref[0:rs, :]
        for d in range(1, N_DEV):
            acc = acc + all_ref[d * rs:(d + 1) * rs, :]
        sum_ref[...] = acc

    vmem = pl.BlockSpec(memory_space=pltpu.VMEM)
    return pl.pallas_call(
        body, name="allreduce_small", out_shape=jax.ShapeDtypeStruct((rs, 128), F32),
        in_specs=[vmem], out_specs=vmem,
        scratch_shapes=[pltpu.VMEM((N_DEV * rs, 128), F32), pltpu.SemaphoreType.DMA((7,)), pltpu.SemaphoreType.DMA((7,)),
                        pltpu.SemaphoreType.DMA],
    )(p)


WEIGHTS = ('ffn1_norm', 'ffn1_w_gate', 'ffn1_w_up', 'ffn1_w_down', 'mix_norm', 'w_in', 'q_a_norm', 'w_q_b',
           'kv_a_norm', 'w_kv_b', 'q_head_norm', 'k_head_norm', 'conv_w', 'conv_b', 'a_log_fwd', 'a_log_bwd',
           'dt_bias_fwd', 'dt_bias_bwd', 'd_skip', 'ssm_norm', 'w_attn_branch', 'w_ssm_branch', 'w_out',
           'ffn2_norm', 'ffn2_w_gate', 'ffn2_w_up', 'ffn2_w_down')
PACKED = (('ffn1_w_gate', (D_MODEL, D_FF), 1), ('ffn1_w_up', (D_MODEL, D_FF), 1), ('ffn1_w_down', (D_FF, D_MODEL), 0),
          ('w_in', (D_MODEL, sum(IN_SPLITS)), 1), ('w_q_b', (Q_LORA, N_HEADS * QK_HEAD), 1),
          ('w_kv_b', (KV_LORA, N_HEADS * (QK_NOPE + V_HEAD)), 1),
          ('w_attn_branch', (N_HEADS * V_HEAD, D_MODEL), 0), ('w_ssm_branch', (D_INNER, D_MODEL), 0),
          ('w_out', (D_MODEL, D_MODEL), 0),
          ('ffn2_w_gate', (D_MODEL, D_FF), 1), ('ffn2_w_up', (D_MODEL, D_FF), 1), ('ffn2_w_down', (D_FF, D_MODEL), 0))
PACK_W = 1024
N_CHIPS = 4
SMALL = (('ffn1_norm', 1024), ('mix_norm', 1024), ('q_a_norm', 384), ('kv_a_norm', 256), ('q_head_norm', 96),
         ('k_head_norm', 96), ('conv_b', 3072), ('a_log_fwd', 32), ('a_log_bwd', 32), ('dt_bias_fwd', 32),
         ('dt_bias_bwd', 32), ('d_skip', 32), ('ssm_norm', 2048), ('ffn2_norm', 1024),
         ('conv_w', CONV_WIDTH * XBC_DIM), ('loss', 1))


def _shard_shape(shape, axis):
    return tuple(s // N_CHIPS if a == axis else s for a, s in enumerate(shape))


def _pack_rows():
    rows = sum(math.prod(_shard_shape(shape, axis)) // PACK_W for _, shape, axis in PACKED)
    return -(-rows // 16) * 16


def _pack(shards):
    parts = [shards[name].reshape(-1, PACK_W) for name, _, _ in PACKED]
    rows = sum(p.shape[0] for p in parts)
    parts.append(jnp.zeros((_pack_rows() - rows, PACK_W), parts[0].dtype))
    return jnp.concatenate(parts, axis=0)


def _unpack(packed):
    out, r = {}, 0
    for name, shape, axis in PACKED:
        sh = _shard_shape(shape, axis)
        n = math.prod(sh) // PACK_W
        out[name] = packed[r:r + n].reshape(sh)
        r += n
    return out


def _pack_small(vals):
    parts = []
    for name, n in SMALL:
        pad = -(-n // 128) * 128 - n
        parts.append(jnp.pad(vals[name].reshape(-1).astype(F32), (0, pad)).reshape(-1, 128))
    rows = sum(p.shape[0] for p in parts)
    parts.append(jnp.zeros((-(-rows // 8) * 8 - rows, 128), F32))
    return jnp.concatenate(parts, axis=0)


def _unpack_small(packed):
    out, r = {}, 0
    for name, n in SMALL:
        k = -(-n // 128)
        out[name] = packed[r:r + k].reshape(-1)[:n]
        r += k
    return out


def _pad_heads(w, axis, per_head, lo, hi):
    shape = w.shape
    w = w.reshape(shape[:axis] + (N_HEADS, per_head) + shape[axis + 1:])
    w = lax.slice_in_dim(w, lo, hi, axis=axis + 1)
    pad = [(0, 0)] * w.ndim
    pad[axis + 1] = (0, HP - (hi - lo))
    w = jnp.pad(w, pad)
    return w.reshape(shape[:axis] + (N_HEADS * HP,) + shape[axis + 1:])


def _unpad_heads(w, axis, keep):
    shape = w.shape
    w = w.reshape(shape[:axis] + (N_HEADS, HP) + shape[axis + 1:])
    return lax.slice_in_dim(w, 0, keep, axis=axis + 1)


def _split_w_in(w):
    o = [0]
    for s in IN_SPLITS:
        o.append(o[-1] + s)
    return [w[:, o[i]:o[i + 1]] for i in range(len(IN_SPLITS))]


def _pad_w_in(w):
    cq, ckv, kpe, z, xbc, dtf, dtb, ga, gb = _split_w_in(w)
    kpe_pad = jnp.pad(kpe, ((0, 0), (QK_NOPE, HP - QK_HEAD)))
    dt_pad = jnp.pad(jnp.concatenate([dtf, dtb], axis=1), ((0, 0), (0, HP - 2 * SSM_HEADS)))
    return jnp.concatenate([z, ga, gb, xbc, cq, ckv, kpe_pad, dt_pad], axis=1)


def _unpad_w_in(g):
    z, ga, gb, xbc = g[:, U_Z:U_GA], g[:, U_GA:U_GB], g[:, U_GB:U_XBC], g[:, U_XBC:U_SMALL]
    s = g[:, U_SMALL:]
    cq, ckv = s[:, S_CQ:S_CKV], s[:, S_CKV:S_KPE]
    kpe = s[:, S_KPE + QK_NOPE:S_KPE + QK_HEAD]
    dtf, dtb = s[:, S_DT:S_DT + SSM_HEADS], s[:, S_DT + SSM_HEADS:S_DT + 2 * SSM_HEADS]
    return jnp.concatenate([cq, ckv, kpe, z, xbc, dtf, dtb, ga, gb], axis=1)


def _lanes128(parts):
    row = jnp.concatenate([p.reshape(-1) for p in parts])
    return jnp.pad(row, (0, HP - row.shape[0])).reshape(1, HP)


def _ffn_fwd(x, g, wg, wu, wd, tag):
    h = _rms_fwd(x, g, name=tag + "_norm")
    gate, up, act = _mm([h], [wg, wu], name=tag + "_up", out_dtypes=(F32, F32, BF),
                        epilogue=lambda a, b: (a, b, _silu(a) * b))
    out = _mm([act], [wd], name=tag + "_down", extras=[x], epilogue=lambda acc, r: (r + 0.5 * acc,))
    return out, (h, gate, up, act)


def _ffn_bwd(dout, x, g, wg, wu, wd, saved, tag):
    h, gate, up, act = saved
    dgate, dup = _mm([dout], [wd], name=tag + "_down_dx", tb=True, extras=[gate, up], out_dtypes=(BF, BF),
                     epilogue=lambda acc, a, b: (0.5 * acc * b * _dsilu(a), 0.5 * acc * _silu(a)))
    dwd = _mm([act], [dout], name=tag + "_down_dw", ta=True, epilogue=lambda acc: (0.5 * acc,))
    dwg, dwu = _mm([h], [dgate, dup], name=tag + "_up_dw", ta=True, out_dtypes=(F32, F32))
    dh = _mm([dgate, dup], [wg, wu], name=tag + "_up_dx", tb=True)
    dx, dg = _rms_bwd(dh, x, g, name=tag + "_norm_bwd", add=dout)
    return dx, dg, dwg, dwu, dwd


KPE_BLK = (U_SMALL + S_KPE) // HP
SMALL_BLK = U_SMALL // SMALL_W


def _local_step(x, pos_col, target, W, P):
    T = x.shape[0]
    sig = jax.nn.sigmoid
    x1, ffn1 = _ffn_fwd(x, P["ffn1_norm"], W["wg1"], W["wu1"], W["wd1"], "ffn1")
    h = _rms_fwd(x1, P["mix_norm"], name="mix_norm")
    u = _mm([h], [W["w_in"]], name="in_proj", tn=1152)
    cqn = _rms_fwd(u, P["q_a_norm"], name="q_a_norm", blk_w=SMALL_W, blk_idx=SMALL_BLK, off=S_CQ, width=Q_LORA)
    ckvn = _rms_fwd(u, P["kv_a_norm"], name="kv_a_norm", blk_w=SMALL_W, blk_idx=SMALL_BLK, off=S_CKV, width=KV_LORA)
    q_raw = _mm([cqn], [W["wq"]], name="q_proj")
    k_raw, v = _mm([ckvn], [W["wk"], W["wv"]], name="kv_proj", out_dtypes=(F32, BF))
    rc, rs = _rope_tables(pos_col, P["freq"])
    q = _qk_prep_fwd(q_raw, None, P["q_head_norm"], rc, rs, name="q_prep")
    k = _qk_prep_fwd(k_raw, u, P["k_head_norm"], rc, rs, name="k_prep", kpe_blk=KPE_BLK)
    o, lse = _attn_fwd(q, k, v)
    pre, act = _conv_fwd(u, P["conv_w8"], P["conv_b"])
    y_f, st_f = _ssd_fwd(act, u, P["dt_bias8"], P["a_log8"], rev=False, name="ssd_fwd_f")
    y_b, st_b = _ssd_fwd(act, u, P["dt_bias8"], P["a_log8"], rev=True, name="ssd_fwd_b")
    ysum, m = _ssm_combine_fwd(y_f, y_b, act, u, P["d_skip_lanes"], P["ssm_norm"])
    ab = _mm([o], [W["pa"]], name="attn_branch")
    mb, merged = _mm([m], [W["pb"]], name="ssm_branch", extras=[ab, u, u], extra_offs=(0, U_GA, U_GB), out_dtypes=(F32, BF),
                     epilogue=lambda acc, a, ga, gb: (acc, sig(ga) * a + sig(gb) * acc))
    x2 = _mm([merged], [W["wo"]], name="out_proj", extras=[x1], epilogue=lambda acc, r: (r + acc,))
    y, ffn2 = _ffn_fwd(x2, P["ffn2_norm"], W["wg2"], W["wu2"], W["wd2"], "ffn2")
    dy, loss = _loss_head(y, target)
    dx2, dg_ffn2, dwg2, dwu2, dwd2 = _ffn_bwd(dy, x2, P["ffn2_norm"], W["wg2"], W["wu2"], W["wd2"], ffn2, "ffn2")

    def gate_bwd(dmrg, a, b, ga, gb):
        sa, sb = sig(ga), sig(gb)
        return dmrg * sa, dmrg * sb, dmrg * a * sa * (1.0 - sa), dmrg * b * sb * (1.0 - sb)

    dab, dmb, dga, dgb = _mm([dx2], [W["wo"]], name="out_proj_dx", tb=True, extras=[ab, mb, u, u],
                             extra_offs=(0, 0, U_GA, U_GB), out_dtypes=(BF,) * 4, epilogue=gate_bwd)
    dwo = _mm([merged], [dx2], name="out_proj_dw", ta=True)
    dpa = _mm([o], [dab], name="attn_branch_dw", ta=True)
    do = _mm([dab], [W["pa"]], name="attn_branch_dx", tb=True)
    dpb = _mm([m], [dmb], name="ssm_branch_dw", ta=True)
    dm = _mm([dmb], [W["pb"]], name="ssm_branch_dx", tb=True)
    dyssd, dz, dxs_skip, dg_ssm, dskip = _ssm_combine_bwd(dm, ysum, act, u, P["d_skip_lanes"], P["ssm_norm"])
    dxs_f, db_f, dc_f, ddt, dalog_f, dbias_f = _ssd_bwd(act, u, P["dt_bias8"], P["a_log8"], st_f, dyssd,
                                                        jnp.zeros((T, HP), F32), rev=False, name="ssd_bwd_f")
    dxs_b, db_b, dc_b, ddt, dalog_b, dbias_b = _ssd_bwd(act, u, P["dt_bias8"], P["a_log8"], st_b, dyssd, ddt,
                                                        rev=True, name="ssd_bwd_b")
    dxbc, dconv = [], []
    for tag, col0, parts in (("x", 0, [dxs_f, dxs_b, dxs_skip]), ("b", D_INNER, [db_f, db_b]),
                             ("c", D_INNER + SSM_GROUPS * D_STATE, [dc_f, dc_b])):
        dpre = _conv_dpre(parts, pre, col0, name="conv_dpre_" + tag)
        dxp, dwp = _conv_bwd(dpre, u, P["conv_w8"], col0, name="conv_bwd_" + tag)
        dxbc.append(dxp)
        dconv.append(dwp)
    dconv = jnp.concatenate(dconv, axis=1)
    dq, dk, dv = _attn_bwd(q, k, v, do, o, lse)
    dq_raw, dg_qh = _qk_prep_bwd(dq, q_raw, None, P["q_head_norm"], rc, rs, name="q_prep_bwd")
    dk_raw, dg_kh, dkpe = _qk_prep_bwd(dk, k_raw, u, P["k_head_norm"], rc, rs, name="k_prep_bwd", kpe_blk=KPE_BLK)
    dwq = _mm([cqn], [dq_raw], name="q_proj_dw", ta=True)
    dcqn = _mm([dq_raw], [W["wq"]], name="q_proj_dx", tb=True)
    dwk, dwv = _mm([ckvn], [dk_raw, dv], name="kv_proj_dw", ta=True, out_dtypes=(F32, F32))
    dckvn = _mm([dk_raw, dv], [W["wk"], W["wv"]], name="kv_proj_dx", tb=True)
    dcq, dg_qa = _rms_bwd(dcqn, u, P["q_a_norm"], name="q_a_norm_bwd", blk_w=SMALL_W, blk_idx=SMALL_BLK, off=S_CQ,
                          width=Q_LORA, out_dtype=BF)
    dckv, dg_kva = _rms_bwd(dckvn, u, P["kv_a_norm"], name="kv_a_norm_bwd", blk_w=SMALL_W, blk_idx=SMALL_BLK,
                            off=S_CKV, width=KV_LORA, out_dtype=BF)
    du = jnp.concatenate([dz, dga, dgb] + dxbc + [dcq, dckv, dkpe.astype(BF), ddt.astype(BF)], axis=1)
    dw_in = _mm([h], [du], name="in_proj_dw", ta=True, tn=1152)
    dh = _mm([du], [W["w_in"]], name="in_proj_dx", tb=True)
    dx1, dg_mix = _rms_bwd(dh, x1, P["mix_norm"], name="mix_norm_bwd", add=dx2)
    dx, dg_ffn1, dwg1, dwu1, dwd1 = _ffn_bwd(dx1, x, P["ffn1_norm"], W["wg1"], W["wu1"], W["wd1"], ffn1, "ffn1")
    dW = dict(wg1=dwg1, wu1=dwu1, wd1=dwd1, w_in=dw_in, wq=dwq, wk=dwk, wv=dwv, pa=dpa, pb=dpb, wo=dwo,
              wg2=dwg2, wu2=dwu2, wd2=dwd2)
    dP = dict(ffn1_norm=dg_ffn1[0], mix_norm=dg_mix[0], q_a_norm=dg_qa[0], kv_a_norm=dg_kva[0],
              q_head_norm=dg_qh[0, :QK_HEAD], k_head_norm=dg_kh[0, :QK_HEAD], conv_b=dconv[CONV_WIDTH],
              a_log_fwd=dalog_f[0, :SSM_HEADS], a_log_bwd=dalog_b[0, SSM_HEADS:2 * SSM_HEADS],
              dt_bias_fwd=dbias_f[0, :SSM_HEADS], dt_bias_bwd=dbias_b[0, SSM_HEADS:2 * SSM_HEADS],
              d_skip=dskip[0].reshape(SSM_GROUPS, HP)[:, :HG], ssm_norm=dg_ssm[0], ffn2_norm=dg_ffn2[0],
              conv_w=dconv[:CONV_WIDTH], loss=loss[0, 0])
    return dx, dW, dP


def _prepare(w, conv_w_full):
    kvb = w["w_kv_b"]
    W = dict(wg1=w["ffn1_w_gate"], wu1=w["ffn1_w_up"], wd1=w["ffn1_w_down"], w_in=_pad_w_in(w["w_in"]),
             wq=_pad_heads(w["w_q_b"], 1, QK_HEAD, 0, QK_HEAD),
             wk=_pad_heads(kvb, 1, QK_NOPE + V_HEAD, 0, QK_NOPE),
             wv=_pad_heads(kvb, 1, QK_NOPE + V_HEAD, QK_NOPE, QK_NOPE + V_HEAD),
             pa=_pad_heads(w["w_attn_branch"], 0, V_HEAD, 0, V_HEAD), pb=w["w_ssm_branch"], wo=w["w_out"],
             wg2=w["ffn2_w_gate"], wu2=w["ffn2_w_up"], wd2=w["ffn2_w_down"])
    inv_freq = [1.0 / (ROPE_BASE ** (j / QK_ROPE)) for j in range(0, QK_ROPE, 2)]
    freq = [0.0] * QK_NOPE + inv_freq + inv_freq + [0.0] * (HP - QK_HEAD)
    P = {n: w[n] for n in ("ffn1_norm", "mix_norm", "q_a_norm", "kv_a_norm", "ssm_norm", "ffn2_norm", "conv_b")}
    P.update(q_head_norm=_lanes128([w["q_head_norm"]]), k_head_norm=_lanes128([w["k_head_norm"]]),
             conv_w8=jnp.pad(conv_w_full, ((0, 8 - CONV_WIDTH), (0, 0))),
             dt_bias8=jnp.broadcast_to(_lanes128([w["dt_bias_fwd"], w["dt_bias_bwd"]]), (8, HP)),
             a_log8=jnp.broadcast_to(_lanes128([w["a_log_fwd"], w["a_log_bwd"]]), (8, HP)),
             d_skip_lanes=jnp.repeat(w["d_skip"].reshape(-1), PH).reshape(1, D_INNER),
             freq=jnp.asarray(freq, F32).reshape(1, HP))
    return W, P


def _unprepare(dW):
    dkvb = jnp.concatenate([_unpad_heads(dW["wk"], 1, QK_NOPE), _unpad_heads(dW["wv"], 1, V_HEAD)], axis=2)
    return dict(ffn1_w_gate=dW["wg1"], ffn1_w_up=dW["wu1"], ffn1_w_down=dW["wd1"], w_in=_unpad_w_in(dW["w_in"]),
                w_q_b=_unpad_heads(dW["wq"], 1, QK_HEAD).reshape(Q_LORA, N_HEADS * QK_HEAD),
                w_kv_b=dkvb.reshape(KV_LORA, N_HEADS * (QK_NOPE + V_HEAD)),
                w_attn_branch=_unpad_heads(dW["pa"], 0, V_HEAD).reshape(N_HEADS * V_HEAD, D_MODEL),
                w_ssm_branch=dW["pb"], w_out=dW["wo"],
                ffn2_w_gate=dW["wg2"], ffn2_w_up=dW["wu2"], ffn2_w_down=dW["wd2"])


def kernel(x, positions, ffn1_norm, ffn1_w_gate, ffn1_w_up, ffn1_w_down, mix_norm, w_in, q_a_norm, w_q_b, kv_a_norm, w_kv_b, q_head_norm, k_head_norm, conv_w, conv_b, a_log_fwd, a_log_bwd, dt_bias_fwd, dt_bias_bwd, d_skip, ssm_norm, w_attn_branch, w_ssm_branch, w_out, ffn2_norm, ffn2_w_gate, ffn2_w_up, ffn2_w_down, loss_target, m_ffn1_norm, m_ffn1_w_gate, m_ffn1_w_up, m_ffn1_w_down, m_mix_norm, m_w_in, m_q_a_norm, m_w_q_b, m_kv_a_norm, m_w_kv_b, m_q_head_norm, m_k_head_norm, m_conv_w, m_conv_b, m_a_log_fwd, m_a_log_bwd, m_dt_bias_fwd, m_dt_bias_bwd, m_d_skip, m_ssm_norm, m_w_attn_branch, m_w_ssm_branch, m_w_out, m_ffn2_norm, m_ffn2_w_gate, m_ffn2_w_up, m_ffn2_w_down, v_ffn1_norm, v_ffn1_w_gate, v_ffn1_w_up, v_ffn1_w_down, v_mix_norm, v_w_in, v_q_a_norm, v_w_q_b, v_kv_a_norm, v_w_kv_b, v_q_head_norm, v_k_head_norm, v_conv_w, v_conv_b, v_a_log_fwd, v_a_log_bwd, v_dt_bias_fwd, v_dt_bias_bwd, v_d_skip, v_ssm_norm, v_w_attn_branch, v_w_ssm_branch, v_w_out, v_ffn2_norm, v_ffn2_w_gate, v_ffn2_w_up, v_ffn2_w_down):
    given = dict(locals())
    T = x.shape[1]
    packed_names = [name for name, _, _ in PACKED]

    def two_d(a):
        return a.reshape(a.shape[1], -1) if a.ndim > 2 else a

    w_loc = {n: two_d(given[n]) for n in WEIGHTS}
    wb = _pack({n: w_loc[n].astype(BF) for n in packed_names})
    wf = jnp.pad(w_loc["conv_w"], ((0, 8 - CONV_WIDTH), (0, 0)))
    gb, gf = _gather_chips(wb, wf)
    per_chip = [_unpack(gb[j]) for j in range(N_CHIPS)]
    full = {n: jnp.concatenate([per_chip[j][n] for j in range(N_CHIPS)], axis=axis) for n, _, axis in PACKED}
    conv_w_full = jnp.concatenate([gf[j, :CONV_WIDTH] for j in range(N_CHIPS)], axis=1)
    full.update({n: w_loc[n] for n in WEIGHTS if n not in full and n != "conv_w"})
    W, P = _prepare(full, conv_w_full)
    dx, dW, dP = _local_step(x.reshape(T, D_MODEL), positions.reshape(T, 1).astype(F32), loss_target.reshape(T, D_MODEL), W, P)
    g_full = _unprepare(dW)
    slots = []
    for j in range(N_CHIPS):
        shards = {}
        for n, shape, axis in PACKED:
            size = shape[axis] // N_CHIPS
            shards[n] = lax.slice_in_dim(g_full[n], j * size, (j + 1) * size, axis=axis).astype(BF)
        slots.append(_pack(shards))
    got = _exchange_chips(jnp.stack(slots))
    mine = _sum_slots(got)
    g_packed = _unpack(_add2(mine, _swap_cores(mine)))
    small = _unpack_small(_allreduce_small(_pack_small(dP)))
    chip = 2 * lax.axis_index("x") + lax.axis_index("y")
    grads = dict(g_packed)
    grads.update({n: small[n].reshape(1, -1) for n, _ in SMALL if n not in ("conv_w", "loss")})
    grads["conv_w"] = lax.dynamic_slice_in_dim(small["conv_w"].reshape(CONV_WIDTH, XBC_DIM), chip * (XBC_DIM // N_CHIPS),
                                               XBC_DIM // N_CHIPS, axis=1)
    out_g, out_d, out_m, out_v = [], [], [], []
    for n in WEIGHTS:
        shape = given[n].shape
        delta, new_m, new_v = _adamw(w_loc[n], grads[n], two_d(given["m_" + n]), two_d(given["v_" + n]), name="adamw_" + n)
        out_g.append(grads[n].reshape(shape))
        out_d.append(delta.reshape(shape))
        out_m.append(new_m.reshape(shape))
        out_v.append(new_v.reshape(shape))
    return (small["loss"].reshape(()), dx.reshape(x.shape), *out_g, *out_d, *out_m, *out_v)
```

```python
import functools
import math

import jax
import jax.numpy as jnp
from jax import lax
from jax.experimental import pallas as pl
from jax.experimental.pallas import tpu as pltpu

BF = jnp.bfloat16
F32 = jnp.float32
HI = lax.Precision.HIGHEST
MESH = pl.DeviceIdType.MESH

D_MODEL = 1024
D_FF = 2816
EPS = 1e-6
N_HEADS = 16
QK_NOPE = 64
QK_ROPE = 32
QK_HEAD = 96
V_HEAD = 64
Q_LORA = 384
KV_LORA = 256
ROPE_BASE = 10000.0
D_INNER = 2048
SSM_HEADS = 32
SSM_GROUPS = 4
D_STATE = 128
CONV_WIDTH = 5
CHUNK = 128
XBC_DIM = 3072
HP = 128
GW = D_INNER // SSM_GROUPS
HG = SSM_HEADS // SSM_GROUPS
PH = 64
U_Z, U_GA, U_GB, U_XBC, U_SMALL = 0, 2048, 3072, 4096, 7168
S_CQ, S_CKV, S_KPE, S_DT, SMALL_W = 0, 384, 640, 768, 896
U_PAD = U_SMALL + SMALL_W
IN_SPLITS = (Q_LORA, KV_LORA, QK_ROPE, D_INNER, XBC_DIM, SSM_HEADS, SSM_HEADS, D_MODEL, D_MODEL)

ADAM_LR = 0.001
ADAM_B1 = 0.9
ADAM_B2 = 0.999
ADAM_EPS = 1e-08
ADAM_WD = 0.01
ADAM_STEP = 10

NN = (((1,), (0,)), ((), ()))
NT = (((1,), (1,)), ((), ()))
TN = (((0,), (0,)), ((), ()))


def _pick(n, pref):
    best = None
    d = 128
    while d <= min(n, pref):
        if n % d == 0:
            best = d
        d += 128
    return best if best is not None else n


def _silu(x):
    return x * jax.nn.sigmoid(x)


def _dsilu(x):
    s = jax.nn.sigmoid(x)
    return s * (1.0 + x * (1.0 - s))


def _softplus(x):
    return jnp.maximum(x, 0.0) + jnp.log(1.0 + jnp.exp(-jnp.abs(x)))


def _mm(As, Bs, *, name, ta=False, tb=False, out_dtypes=(F32,), epilogue=None, extras=(), extra_offs=None,
        tm=1024, tn=512, tk=2048):
    As, Bs, extras = list(As), list(Bs), list(extras)
    a0, b0 = As[0], Bs[0]
    M, K = (a0.shape[1], a0.shape[0]) if ta else a0.shape
    N = b0.shape[0] if tb else b0.shape[1]
    tm, tn, tk = _pick(M, tm), _pick(N, tn), _pick(K, tk)
    nk = K // tk
    n_a, n_b, n_e, n_o = len(As), len(Bs), len(extras), len(out_dtypes)
    n_acc = (n_b if n_a == 1 else 1) if nk > 1 else 0
    if extra_offs is None:
        extra_offs = (0,) * n_e
    dn = (((0,) if ta else (1,), (1,) if tb else (0,)), ((), ()))
    bytes_a = sum(a.size * a.dtype.itemsize for a in As)
    bytes_b = sum(b.size * b.dtype.itemsize for b in Bs)
    n_outer = (N // tn) * bytes_a + bytes_b < (M // tm) * bytes_b + bytes_a

    def products(a_refs, b_refs):
        if n_a == 1:
            a = a_refs[0][...].astype(BF)
            return [lax.dot_general(a, b[...].astype(BF), dn, preferred_element_type=F32) for b in b_refs]
        total = None
        for a, b in zip(a_refs, b_refs):
            p = lax.dot_general(a[...].astype(BF), b[...].astype(BF), dn, preferred_element_type=F32)
            total = p if total is None else total + p
        return [total]

    def finish(accs, e_refs, o_refs):
        ex = [e[...] for e in e_refs]
        outs = epilogue(*accs, *ex) if epilogue is not None else tuple(accs)
        for o_ref, val in zip(o_refs, outs):
            o_ref[...] = val.astype(o_ref.dtype)

    def body(*refs):
        a_refs, b_refs = refs[:n_a], refs[n_a:n_a + n_b]
        e_refs = refs[n_a + n_b:n_a + n_b + n_e]
        o_refs = refs[n_a + n_b + n_e:n_a + n_b + n_e + n_o]
        acc_refs = refs[n_a + n_b + n_e + n_o:]
        if nk == 1:
            finish(products(a_refs, b_refs), e_refs, o_refs)
            return
        k = pl.program_id(2)

        @pl.when(k == 0)
        def _():
            for acc in acc_refs:
                acc[...] = jnp.zeros_like(acc)

        for acc, p in zip(acc_refs, products(a_refs, b_refs)):
            acc[...] += p

        @pl.when(k == nk - 1)
        def _():
            finish([acc[...] for acc in acc_refs], e_refs, o_refs)

    def at(f):
        return (lambda j, i, k: f(i, j, k)) if n_outer else f

    a_spec = pl.BlockSpec((tk, tm), at(lambda i, j, k: (k, i))) if ta else pl.BlockSpec((tm, tk), at(lambda i, j, k: (i, k)))
    b_spec = pl.BlockSpec((tn, tk), at(lambda i, j, k: (j, k))) if tb else pl.BlockSpec((tk, tn), at(lambda i, j, k: (k, j)))
    e_specs = [pl.BlockSpec((tm, tn), at(functools.partial(lambda i, j, k, o: (i, j + o), o=off // tn))) for off in extra_offs]
    for off in extra_offs:
        assert off % tn == 0
    outs = pl.pallas_call(
        body, name=name,
        out_shape=tuple(jax.ShapeDtypeStruct((M, N), dt) for dt in out_dtypes),
        grid=(N // tn, M // tm, nk) if n_outer else (M // tm, N // tn, nk),
        in_specs=[a_spec] * n_a + [b_spec] * n_b + e_specs,
        out_specs=tuple(pl.BlockSpec((tm, tn), at(lambda i, j, k: (i, j))) for _ in out_dtypes),
        scratch_shapes=[pltpu.VMEM((tm, tn), F32)] * n_acc,
        compiler_params=pltpu.CompilerParams(dimension_semantics=("parallel", "parallel", "arbitrary")),
    )(*As, *Bs, *extras)
    return outs[0] if n_o == 1 else outs


def _rms_fwd(x, g, *, name, blk_w=None, blk_idx=0, off=0, width=None, out_dtype=BF):
    T = x.shape[0]
    blk_w = x.shape[1] if blk_w is None else blk_w
    width = blk_w if width is None else width
    tt = _pick(T, 512)

    def body(x_ref, g_ref, o_ref):
        xf = x_ref[:, off:off + width]
        r = lax.rsqrt(jnp.mean(xf * xf, axis=-1, keepdims=True) + EPS)
        o_ref[...] = (xf * r * g_ref[...]).astype(o_ref.dtype)

    return pl.pallas_call(
        body, name=name, out_shape=jax.ShapeDtypeStruct((T, width), out_dtype), grid=(T // tt,),
        in_specs=[pl.BlockSpec((tt, blk_w), lambda i: (i, blk_idx)), pl.BlockSpec((1, width), lambda i: (0, 0))],
        out_specs=pl.BlockSpec((tt, width), lambda i: (i, 0)),
    )(x, g)


def _rms_bwd(dy, x, g, *, name, blk_w=None, blk_idx=0, off=0, width=None, add=None, out_dtypes=(F32,)):
    T = x.shape[0]
    blk_w = x.shape[1] if blk_w is None else blk_w
    width = blk_w if width is None else width
    tt = _pick(T, 512)
    has_add = add is not None
    n_dx = len(out_dtypes)

    def body(*refs):
        dy_ref, x_ref, g_ref = refs[:3]
        dx_refs, dg_ref = refs[3 + has_add:3 + has_add + n_dx], refs[-1]
        xf = x_ref[:, off:off + width]
        d = dy_ref[...].astype(F32)
        r = lax.rsqrt(jnp.mean(xf * xf, axis=-1, keepdims=True) + EPS)
        gd = d * g_ref[...]
        dx = r * gd - xf * (r * r * r) * jnp.mean(gd * xf, axis=-1, keepdims=True)
        if has_add:
            dx = dx + refs[3][...]
        for dx_ref in dx_refs:
            dx_ref[...] = dx.astype(dx_ref.dtype)

        @pl.when(pl.program_id(0) == 0)
        def _():
            dg_ref[...] = jnp.zeros_like(dg_ref)

        dg_ref[...] += jnp.broadcast_to(jnp.sum(d * xf * r, axis=0, keepdims=True), dg_ref.shape)

    row = pl.BlockSpec((tt, width), lambda i: (i, 0))
    in_specs = [row, pl.BlockSpec((tt, blk_w), lambda i: (i, blk_idx)), pl.BlockSpec((1, width), lambda i: (0, 0))]
    args = [dy, x, g]
    if has_add:
        in_specs.append(row)
        args.append(add)
    return pl.pallas_call(
        body, name=name,
        out_shape=tuple(jax.ShapeDtypeStruct((T, width), dt) for dt in out_dtypes) + (jax.ShapeDtypeStruct((8, width), F32),),
        grid=(T // tt,), in_specs=in_specs,
        out_specs=(row,) * n_dx + (pl.BlockSpec((8, width), lambda i: (0, 0)),),
        compiler_params=pltpu.CompilerParams(dimension_semantics=("arbitrary",)),
    )(*args)


def _rope_tables(pos_col, freq_lane):
    T = pos_col.shape[0]
    tt = _pick(T, 512)

    def body(p_ref, f_ref, c_ref, s_ref):
        ang = p_ref[...] * f_ref[...]
        lane = lax.broadcasted_iota(jnp.int32, ang.shape, 1)
        c_ref[...] = jnp.where(lane < QK_HEAD, jnp.cos(ang), 0.0)
        sn = jnp.sin(ang)
        s_ref[...] = jnp.where((lane >= QK_NOPE) & (lane < QK_NOPE + 16), -sn,
                               jnp.where((lane >= QK_NOPE + 16) & (lane < QK_HEAD), sn, 0.0))

    return pl.pallas_call(
        body, name="rope_tables", out_shape=(jax.ShapeDtypeStruct((T, HP), F32),) * 2, grid=(T // tt,),
        in_specs=[pl.BlockSpec((tt, 1), lambda i: (i, 0)), pl.BlockSpec((1, HP), lambda i: (0, 0))],
        out_specs=(pl.BlockSpec((tt, HP), lambda i: (i, 0)),) * 2,
    )(pos_col, freq_lane)


def _swap_rope_halves(n):
    lane = lax.broadcasted_iota(jnp.int32, n.shape, 1)
    lo = (lane >= QK_NOPE) & (lane < QK_NOPE + 16)
    hi = (lane >= QK_NOPE + 16) & (lane < QK_HEAD)
    return jnp.where(lo, pltpu.roll(n, HP - 16, 1), jnp.where(hi, pltpu.roll(n, 16, 1), 0.0))


def _qk_prep_fwd(raw, kpe, gain, C, S, *, name, kpe_blk=0, out_scale=1.0):
    T = raw.shape[0]
    tt = _pick(T, 512)
    has_kpe = kpe is not None

    def body(*refs):
        if has_kpe:
            raw_ref, kpe_ref, g_ref, c_ref, s_ref, o_ref = refs
            xr = raw_ref[...] + kpe_ref[...]
        else:
            raw_ref, g_ref, c_ref, s_ref, o_ref = refs
            xr = raw_ref[...]
        r = lax.rsqrt(jnp.sum(xr * xr, axis=-1, keepdims=True) * (1.0 / QK_HEAD) + EPS)
        n = xr * r * g_ref[...]
        o_ref[...] = ((n * c_ref[...] + _swap_rope_halves(n) * s_ref[...]) * out_scale).astype(o_ref.dtype)

    head = pl.BlockSpec((tt, HP), lambda i, h: (i, h))
    shared = pl.BlockSpec((tt, HP), lambda i, h: (i, 0))
    kpe_spec = pl.BlockSpec((tt, HP), lambda i, h: (i, kpe_blk))
    in_specs = [head] + ([kpe_spec] if has_kpe else []) + [pl.BlockSpec((1, HP), lambda i, h: (0, 0)), shared, shared]
    args = [raw] + ([kpe] if has_kpe else []) + [gain, C, S]
    return pl.pallas_call(
        body, name=name, out_shape=jax.ShapeDtypeStruct(raw.shape, BF), grid=(T // tt, N_HEADS),
        in_specs=in_specs, out_specs=head,
    )(*args)


def _qk_prep_bwd(dout, raw, kpe, gain, C, S, *, name, kpe_blk=0, in_scale=1.0):
    T = raw.shape[0]
    tt = _pick(T, 512)
    has_kpe = kpe is not None

    def body(*refs):
        if has_kpe:
            d_ref, raw_ref, kpe_ref, g_ref, c_ref, s_ref, dx_ref, dg_ref, dkpe_ref = refs
            xr = raw_ref[...] + kpe_ref[...]
        else:
            d_ref, raw_ref, g_ref, c_ref, s_ref, dx_ref, dg_ref = refs
            xr = raw_ref[...]
        i, h = pl.program_id(0), pl.program_id(1)
        d = d_ref[...].astype(F32) * in_scale
        r = lax.rsqrt(jnp.sum(xr * xr, axis=-1, keepdims=True) * (1.0 / QK_HEAD) + EPS)
        dn = d * c_ref[...] + _swap_rope_halves(d * s_ref[...])
        gd = dn * g_ref[...]
        dx = r * gd - xr * (r * r * r) * (jnp.sum(gd * xr, axis=-1, keepdims=True) * (1.0 / QK_HEAD))
        dx_ref[...] = dx.astype(dx_ref.dtype)

        @pl.when((i == 0) & (h == 0))
        def _():
            dg_ref[...] = jnp.zeros_like(dg_ref)

        dg_ref[...] += jnp.broadcast_to(jnp.sum(dn * xr * r, axis=0, keepdims=True), dg_ref.shape)
        if has_kpe:
            @pl.when(h == 0)
            def _():
                dkpe_ref[...] = jnp.zeros_like(dkpe_ref)

            dkpe_ref[...] += dx

    head = pl.BlockSpec((tt, HP), lambda i, h: (i, h))
    shared = pl.BlockSpec((tt, HP), lambda i, h: (i, 0))
    kpe_spec = pl.BlockSpec((tt, HP), lambda i, h: (i, kpe_blk))
    in_specs = [head, head] + ([kpe_spec] if has_kpe else []) + [pl.BlockSpec((1, HP), lambda i, h: (0, 0)), shared, shared]
    args = [dout, raw] + ([kpe] if has_kpe else []) + [gain, C, S]
    out_shape = [jax.ShapeDtypeStruct(raw.shape, BF), jax.ShapeDtypeStruct((8, HP), F32)]
    out_specs = [head, pl.BlockSpec((8, HP), lambda i, h: (0, 0))]
    if has_kpe:
        out_shape.append(jax.ShapeDtypeStruct((T, HP), F32))
        out_specs.append(shared)
    return pl.pallas_call(
        body, name=name, out_shape=tuple(out_shape), grid=(T // tt, N_HEADS),
        in_specs=in_specs, out_specs=tuple(out_specs),
        compiler_params=pltpu.CompilerParams(dimension_semantics=("arbitrary", "arbitrary")),
    )(*args)


ATTN_SCALE = 1.0 / math.sqrt(QK_HEAD)
LOG2E = 1.0 / math.log(2.0)
Q_SCALE = ATTN_SCALE * LOG2E


def _attn_fwd(q, k, v):
    T = q.shape[0]
    tq = _pick(T, 256)

    def body(q_ref, k_ref, v_ref, o_ref, lse_ref):
        s = lax.dot_general(q_ref[...], k_ref[...], NT, preferred_element_type=F32)
        m = jnp.max(s, axis=-1, keepdims=True)
        p = jnp.exp2(s - m)
        l = jnp.sum(p, axis=-1, keepdims=True)
        o = jnp.dot(p.astype(BF), v_ref[...], preferred_element_type=F32)
        o_ref[...] = o / l
        lse_ref[...] = jnp.broadcast_to(m + jnp.log2(l), lse_ref.shape)

    qs = pl.BlockSpec((tq, HP), lambda h, i: (i, h))
    kv = pl.BlockSpec((T, HP), lambda h, i: (0, h))
    return pl.pallas_call(
        body, name="attn_fwd", out_shape=(jax.ShapeDtypeStruct(q.shape, F32),) * 2, grid=(N_HEADS, T // tq),
        in_specs=[qs, kv, kv], out_specs=(qs, qs),
        compiler_params=pltpu.CompilerParams(dimension_semantics=("parallel", "parallel")),
    )(q, k, v)


def _attn_bwd(q, k, v, do, o, lse):
    T = q.shape[0]
    tb = _pick(T, 512)
    nb = T // tb

    def body(q_ref, k_ref, v_ref, do_ref, o_ref, lse_ref, dq_ref, dk_ref, dv_ref):
        dq_ref[...] = jnp.zeros_like(dq_ref)

        def k_loop(j, carry):
            ks = pl.ds(pl.multiple_of(j * tb, tb), tb)
            kj, vj = k_ref[ks, :], v_ref[ks, :]

            def q_loop(i, acc):
                dk_acc, dv_acc = acc
                qs = pl.ds(pl.multiple_of(i * tb, tb), tb)
                qi = q_ref[qs, :]
                doi = do_ref[qs, :]
                delta = jnp.sum(doi * o_ref[qs, :], axis=-1, keepdims=True)
                dob = doi.astype(BF)
                s = lax.dot_general(qi, kj, NT, preferred_element_type=F32)
                p = jnp.exp2(s - lse_ref[qs, 0:1])
                dp = lax.dot_general(dob, vj, NT, preferred_element_type=F32)
                ds = (p * (dp - delta)).astype(BF)
                dv_acc = dv_acc + lax.dot_general(p.astype(BF), dob, TN, preferred_element_type=F32)
                dk_acc = dk_acc + lax.dot_general(ds, qi, TN, preferred_element_type=F32)
                dq_ref[qs, :] += jnp.dot(ds, kj, preferred_element_type=F32)
                return dk_acc, dv_acc

            zero = jnp.zeros((tb, HP), F32)
            dk_acc, dv_acc = lax.fori_loop(0, nb, q_loop, (zero, zero))
            dk_ref[ks, :] = dk_acc
            dv_ref[ks, :] = dv_acc.astype(dv_ref.dtype)
            return carry

        lax.fori_loop(0, nb, k_loop, 0)

    spec = pl.BlockSpec((T, HP), lambda h: (0, h))
    return pl.pallas_call(
        body, name="attn_bwd",
        out_shape=(jax.ShapeDtypeStruct(q.shape, F32), jax.ShapeDtypeStruct(q.shape, F32), jax.ShapeDtypeStruct(q.shape, BF)),
        grid=(N_HEADS,), in_specs=[spec] * 6, out_specs=(spec,) * 3,
        compiler_params=pltpu.CompilerParams(dimension_semantics=("parallel",)),
    )(q, k, v, do, o, lse)


CONV_TC = 512
CONV_PAD = CONV_WIDTH // 2


def _halo_specs(tr, col_of):
    r8 = tr // 8
    cur = pl.BlockSpec((tr, CONV_TC), lambda j, i: (i, col_of(j)))
    prev = pl.BlockSpec((8, CONV_TC), lambda j, i: (jnp.maximum(i * r8 - 1, 0), col_of(j)))

    def nxt_map(j, i, n8):
        return (jnp.minimum((i + 1) * r8, n8 - 1), col_of(j))

    return cur, prev, nxt_map


def _with_halo(prev_ref, cur_ref, next_ref, i, n_i):
    prev = jnp.where(i == 0, 0.0, prev_ref[...].astype(F32))
    nxt = jnp.where(i == n_i - 1, 0.0, next_ref[...].astype(F32))
    return jnp.concatenate([prev, cur_ref[...].astype(F32), nxt], axis=0)


def _conv_fwd(u, w8, b):
    T = u.shape[0]
    tr = _pick(T, 512)
    n_i = T // tr
    c0 = U_XBC // CONV_TC
    cur, prev, nxt_map = _halo_specs(tr, lambda j: c0 + j)
    nxt = pl.BlockSpec((8, CONV_TC), functools.partial(nxt_map, n8=T // 8))

    def body(p_ref, c_ref, n_ref, w_ref, b_ref, pre_ref, act_ref):
        i = pl.program_id(1)
        full = _with_halo(p_ref, c_ref, n_ref, i, n_i)
        acc = jnp.broadcast_to(b_ref[...], (tr, CONV_TC))
        for kk in range(CONV_WIDTH):
            acc = acc + full[8 - CONV_PAD + kk:8 - CONV_PAD + kk + tr, :] * w_ref[kk:kk + 1, :]
        pre_ref[...] = acc
        act_ref[...] = _silu(acc)

    out = pl.BlockSpec((tr, CONV_TC), lambda j, i: (i, j))
    return pl.pallas_call(
        body, name="conv_fwd", out_shape=(jax.ShapeDtypeStruct((T, XBC_DIM), F32),) * 2,
        grid=(XBC_DIM // CONV_TC, n_i),
        in_specs=[prev, cur, nxt, pl.BlockSpec((8, CONV_TC), lambda j, i: (0, j)), pl.BlockSpec((1, CONV_TC), lambda j, i: (0, j))],
        out_specs=(out, out),
    )(u, u, u, w8, b)


def _conv_dpre(dacts, pre, col0, *, name):
    T, width = dacts[0].shape
    tt = _pick(T, 512)
    n_d = len(dacts)
    c0 = col0 // CONV_TC

    def body(*refs):
        d = refs[0][...]
        for r in refs[1:n_d]:
            d = d + r[...]
        refs[n_d + 1][...] = d * _dsilu(refs[n_d][...])

    blk = pl.BlockSpec((tt, CONV_TC), lambda j, i: (i, j))
    return pl.pallas_call(
        body, name=name, out_shape=jax.ShapeDtypeStruct((T, width), F32), grid=(width // CONV_TC, T // tt),
        in_specs=[blk] * n_d + [pl.BlockSpec((tt, CONV_TC), lambda j, i: (i, c0 + j))], out_specs=blk,
    )(*dacts, pre)


def _conv_bwd(dpre, u, w8, col0, *, name):
    T, width = dpre.shape
    tr = _pick(T, 512)
    n_i = T // tr
    cd = col0 // CONV_TC
    cx = (U_XBC + col0) // CONV_TC
    d_cur, d_prev, d_nxt_map = _halo_specs(tr, lambda j: j)
    x_cur, x_prev, x_nxt_map = _halo_specs(tr, lambda j: cx + j)
    d_nxt = pl.BlockSpec((8, CONV_TC), functools.partial(d_nxt_map, n8=T // 8))
    x_nxt = pl.BlockSpec((8, CONV_TC), functools.partial(x_nxt_map, n8=T // 8))

    def body(dp_ref, dc_ref, dn_ref, xp_ref, xc_ref, xn_ref, w_ref, dx_ref, dw_ref):
        i = pl.program_id(1)
        dfull = _with_halo(dp_ref, dc_ref, dn_ref, i, n_i)
        xfull = _with_halo(xp_ref, xc_ref, xn_ref, i, n_i)
        dcur = dc_ref[...]
        dx = jnp.zeros((tr, CONV_TC), F32)
        rows = []
        for kk in range(CONV_WIDTH):
            dx = dx + dfull[8 + CONV_PAD - kk:8 + CONV_PAD - kk + tr, :] * w_ref[kk:kk + 1, :]
            rows.append(jnp.sum(dcur * xfull[8 - CONV_PAD + kk:8 - CONV_PAD + kk + tr, :], axis=0, keepdims=True))
        rows.append(jnp.sum(dcur, axis=0, keepdims=True))
        rows.append(jnp.zeros((2, CONV_TC), F32))
        dx_ref[...] = dx.astype(dx_ref.dtype)

        @pl.when(i == 0)
        def _():
            dw_ref[...] = jnp.zeros_like(dw_ref)

        dw_ref[...] += jnp.concatenate(rows, axis=0)

    out = pl.BlockSpec((tr, CONV_TC), lambda j, i: (i, j))
    return pl.pallas_call(
        body, name=name, out_shape=(jax.ShapeDtypeStruct((T, width), BF), jax.ShapeDtypeStruct((8, width), F32)),
        grid=(width // CONV_TC, n_i),
        in_specs=[d_prev, d_cur, d_nxt, x_prev, x_cur, x_nxt, pl.BlockSpec((8, CONV_TC), lambda j, i: (0, cd + j))],
        out_specs=(out, pl.BlockSpec((8, CONV_TC), lambda j, i: (0, j))),
        compiler_params=pltpu.CompilerParams(dimension_semantics=("parallel", "arbitrary")),
    )(dpre, dpre, dpre, u, u, u, w8)


def _ssd_common(dt_ref, bias_ref, a_ref, b_ref, c_ref, g, rev):
    rows = lax.broadcasted_iota(jnp.int32, (CHUNK, CHUNK), 0)
    cols = lax.broadcasted_iota(jnp.int32, (CHUNK, CHUNK), 1)
    head0 = (SSM_HEADS if rev else 0) + HG * g
    sel = jnp.where((rows == head0 + cols) & (cols < HG), 1.0, 0.0).astype(F32)
    pre = jnp.dot(dt_ref[...] + bias_ref[0:1, :], sel, precision=HI, preferred_element_type=F32)
    dt = _softplus(pre)
    a_sel = jnp.dot(-jnp.exp(a_ref[...]), sel, precision=HI, preferred_element_type=F32)[0:1, :]
    da = dt * a_sel
    incl = (cols >= rows) if rev else (cols <= rows)
    tri = jnp.where(incl, 1.0, 0.0).astype(F32)
    cs = jnp.dot(tri, da, precision=HI, preferred_element_type=F32)
    tot = cs[0:1, :] if rev else cs[CHUNK - 1:CHUNK, :]
    bm, cm = b_ref[...].astype(BF), c_ref[...].astype(BF)
    cb = lax.dot_general(cm, bm, NT, preferred_element_type=F32)
    return dict(sel=sel, pre=pre, dt=dt, a_sel=a_sel, cs=cs, csT=cs.T, tot=tot, bm=bm, cm=cm, cb=cb, incl=incl, tri=tri)


def _ssd_specs(T, rev, bwd):
    nc = T // CHUNK
    fwd_order = (lambda c: nc - 1 - c) if rev else (lambda c: c)
    cm = (lambda c: fwd_order(nc - 1 - c)) if bwd else fwd_order
    xs = pl.BlockSpec((CHUNK, GW), lambda c, g: (cm(c), g))
    bs = pl.BlockSpec((CHUNK, D_STATE), lambda c, g: (cm(c), D_INNER // D_STATE + g))
    cs = pl.BlockSpec((CHUNK, D_STATE), lambda c, g: (cm(c), (D_INNER + SSM_GROUPS * D_STATE) // D_STATE + g))
    dt = pl.BlockSpec((CHUNK, HP), lambda c, g: (cm(c), (U_SMALL + S_DT) // HP))
    vec = pl.BlockSpec((8, HP), lambda c, g: (0, 0))
    st = pl.BlockSpec((1, GW, D_STATE), lambda c, g: (cm(c), g, 0))
    return nc, cm, xs, bs, cs, dt, vec, st


def _ssd_fwd(act, u, bias8, a8, *, rev, name):
    T = act.shape[0]
    nc, cm, xs_s, b_s, c_s, dt_s, vec_s, st_s = _ssd_specs(T, rev, False)

    def body(x_ref, b_ref, c_ref, dt_ref, bias_ref, a_ref, y_ref, st_ref, state):
        c, g = pl.program_id(0), pl.program_id(1)

        @pl.when(c == 0)
        def _():
            state[g] = jnp.zeros((GW, D_STATE), F32)

        q = _ssd_common(dt_ref, bias_ref, a_ref, b_ref, c_ref, g, rev)
        for hh in range(HG):
            hs = slice(PH * hh, PH * (hh + 1))
            col, row = q["cs"][:, hh:hh + 1], q["csT"][hh:hh + 1, :]
            xdt = x_ref[:, hs] * q["dt"][:, hh:hh + 1]
            lmat = jnp.where(q["incl"], jnp.exp(col - row), 0.0)
            yd = jnp.dot((q["cb"] * lmat).astype(BF), xdt.astype(BF), preferred_element_type=F32)
            prev = state[g, hs, :]
            yo = lax.dot_general(q["cm"], prev.astype(BF), NT, preferred_element_type=F32) * jnp.exp(col)
            y_ref[:, hs] = yd + yo
            tot_h = q["tot"][:, hh:hh + 1]
            s_new = lax.dot_general((xdt * jnp.exp(tot_h - col)).astype(BF), q["bm"], TN, preferred_element_type=F32)
            st_ref[0, hs, :] = prev
            state[g, hs, :] = prev * jnp.exp(tot_h) + s_new

    return pl.pallas_call(
        body, name=name,
        out_shape=(jax.ShapeDtypeStruct((T, D_INNER), F32), jax.ShapeDtypeStruct((nc, D_INNER, D_STATE), F32)),
        grid=(nc, SSM_GROUPS), in_specs=[xs_s, b_s, c_s, dt_s, vec_s, vec_s], out_specs=(xs_s, st_s),
        scratch_shapes=[pltpu.VMEM((SSM_GROUPS, GW, D_STATE), F32)],
        compiler_params=pltpu.CompilerParams(dimension_semantics=("arbitrary", "arbitrary")),
    )(act, act, act, u, bias8, a8)


def _ssd_bwd(act, u, bias8, a8, states, dy, ddt_in, *, rev, name):
    T = act.shape[0]
    nc, cm, xs_s, b_s, c_s, dt_s, vec_s, st_s = _ssd_specs(T, rev, True)

    def body(x_ref, b_ref, c_ref, dt_ref, bias_ref, a_ref, st_ref, dy_ref, ddt_in_ref,
             dx_ref, db_ref, dc_ref, ddt_ref, da_ref, dbias_ref, dstate):
        c, g = pl.program_id(0), pl.program_id(1)

        @pl.when(c == 0)
        def _():
            dstate[g] = jnp.zeros((GW, D_STATE), F32)

        @pl.when((c == 0) & (g == 0))
        def _():
            da_ref[...] = jnp.zeros_like(da_ref)
            dbias_ref[...] = jnp.zeros_like(dbias_ref)

        @pl.when(g == 0)
        def _():
            ddt_ref[...] = ddt_in_ref[...]

        q = _ssd_common(dt_ref, bias_ref, a_ref, b_ref, c_ref, g, rev)
        bm, cmat = q["bm"], q["cm"]
        lane = lax.broadcasted_iota(jnp.int32, (1, CHUNK), 1)
        sub = lax.broadcasted_iota(jnp.int32, (CHUNK, 1), 0)
        zero = jnp.zeros((CHUNK, CHUNK), F32)
        dcb, db_acc, dc_acc = zero, zero, zero
        dcs_col, dcs_row, ddt_x, dtot = zero, zero, zero, jnp.zeros((1, CHUNK), F32)
        for hh in range(HG):
            hs = slice(PH * hh, PH * (hh + 1))
            col, row = q["cs"][:, hh:hh + 1], q["csT"][hh:hh + 1, :]
            tot_h = q["tot"][:, hh:hh + 1]
            x = x_ref[:, hs]
            dth = q["dt"][:, hh:hh + 1]
            xdt = x * dth
            xdb = xdt.astype(BF)
            lmat = jnp.where(q["incl"], jnp.exp(col - row), 0.0)
            mmat = q["cb"] * lmat
            prev = st_ref[0, hs, :]
            pb = prev.astype(BF)
            ds_ = dstate[g, hs, :]
            dsb = ds_.astype(BF)
            dyh = dy_ref[:, hs]
            dyb = dyh.astype(BF)
            e = jnp.exp(col)
            w = jnp.exp(tot_h - col)
            etot = jnp.exp(tot_h)
            dprev = lax.dot_general((dyh * e).astype(BF), cmat, TN, preferred_element_type=F32) + ds_ * etot
            cp = lax.dot_general(cmat, pb, NT, preferred_element_type=F32)
            dcs_h = jnp.sum(dyh * cp, axis=1, keepdims=True) * e
            dc_acc = dc_acc + jnp.dot(dyb, pb, preferred_element_type=F32) * e
            dm = lax.dot_general(dyb, xdb, NT, preferred_element_type=F32)
            dxdt = lax.dot_general(mmat.astype(BF), dyb, TN, preferred_element_type=F32)
            qm = dm * mmat
            dcs_h = dcs_h + jnp.sum(qm, axis=1, keepdims=True)
            dcs_row = dcs_row + jnp.where(sub == hh, jnp.sum(qm, axis=0, keepdims=True), 0.0)
            dcb = dcb + dm * lmat
            bds = lax.dot_general(bm, dsb, NT, preferred_element_type=F32) * w
            dxdt = dxdt + bds
            db_acc = db_acc + jnp.dot(xdb, dsb, preferred_element_type=F32) * w
            t = jnp.sum(xdt * bds, axis=1, keepdims=True)
            dtot_h = jnp.sum(t) + jnp.sum(ds_ * prev) * etot
            dcs_h = dcs_h - t
            dcs_col = dcs_col + jnp.where(lane == hh, dcs_h, 0.0)
            dtot = dtot + jnp.where(lane == hh, dtot_h, 0.0)
            ddt_x = ddt_x + jnp.where(lane == hh, jnp.sum(dxdt * x, axis=1, keepdims=True), 0.0)
            dx_ref[:, hs] = dxdt * dth
            dstate[g, hs, :] = dprev
        dcbb = dcb.astype(BF)
        dc_ref[...] = dc_acc + jnp.dot(dcbb, bm, preferred_element_type=F32)
        db_ref[...] = db_acc + lax.dot_general(dcbb, cmat, TN, preferred_element_type=F32)
        dcs = dcs_col - dcs_row.T
        tri_t = jnp.where(q["incl"], 0.0, 1.0).astype(F32) + jnp.where(
            lax.broadcasted_iota(jnp.int32, (CHUNK, CHUNK), 0) == lax.broadcasted_iota(jnp.int32, (CHUNK, CHUNK), 1), 1.0, 0.0)
        dda = jnp.dot(tri_t, dcs, precision=HI, preferred_element_type=F32) + dtot
        ddt = ddt_x + dda * q["a_sel"]
        dpre = ddt * jax.nn.sigmoid(q["pre"])
        dpre_full = lax.dot_general(dpre, q["sel"], NT, precision=HI, preferred_element_type=F32)
        ddt_ref[...] += dpre_full
        dbias_ref[...] += jnp.broadcast_to(jnp.sum(dpre_full, axis=0, keepdims=True), (8, CHUNK))
        dalog_sel = jnp.broadcast_to(jnp.sum(dda * q["dt"], axis=0, keepdims=True) * q["a_sel"], (8, CHUNK))
        da_ref[...] += lax.dot_general(dalog_sel, q["sel"], NT, precision=HI, preferred_element_type=F32)

    bc_out = pl.BlockSpec((CHUNK, D_STATE), lambda c, g: (cm(c), g))
    dt_out = pl.BlockSpec((CHUNK, HP), lambda c, g: (cm(c), 0))
    return pl.pallas_call(
        body, name=name,
        out_shape=(jax.ShapeDtypeStruct((T, D_INNER), F32), jax.ShapeDtypeStruct((T, SSM_GROUPS * D_STATE), F32),
                   jax.ShapeDtypeStruct((T, SSM_GROUPS * D_STATE), F32), jax.ShapeDtypeStruct((T, HP), F32),
                   jax.ShapeDtypeStruct((8, HP), F32), jax.ShapeDtypeStruct((8, HP), F32)),
        grid=(nc, SSM_GROUPS), in_specs=[xs_s, b_s, c_s, dt_s, vec_s, vec_s, st_s, xs_s, dt_out],
        out_specs=(xs_s, bc_out, bc_out, dt_out, vec_s, vec_s),
        scratch_shapes=[pltpu.VMEM((SSM_GROUPS, GW, D_STATE), F32)],
        compiler_params=pltpu.CompilerParams(dimension_semantics=("arbitrary", "arbitrary")),
    )(act, act, act, u, bias8, a8, states, dy, ddt_in)


def _ssm_combine_fwd(y_f, y_b, act, u, dskip, gain):
    T = y_f.shape[0]
    tt = _pick(T, 256)

    def body(yf_ref, yb_ref, x_ref, z_ref, ds_ref, g_ref, y_ref, m_ref):
        y = yf_ref[...] + yb_ref[...] + ds_ref[...] * x_ref[...]
        y2 = y * _silu(z_ref[...])
        r = lax.rsqrt(jnp.mean(y2 * y2, axis=-1, keepdims=True) + EPS)
        y_ref[...] = y
        m_ref[...] = (y2 * r * g_ref[...]).astype(m_ref.dtype)

    blk = pl.BlockSpec((tt, GW), lambda i, g: (i, g))
    vec = pl.BlockSpec((1, GW), lambda i, g: (0, g))
    return pl.pallas_call(
        body, name="ssm_combine_fwd",
        out_shape=(jax.ShapeDtypeStruct((T, D_INNER), F32), jax.ShapeDtypeStruct((T, D_INNER), BF)),
        grid=(T // tt, SSM_GROUPS), in_specs=[blk, blk, blk, blk, vec, vec], out_specs=(blk, blk),
    )(y_f, y_b, act, u, dskip, gain)


def _ssm_combine_bwd(dm, y, act, u, dskip, gain):
    T = y.shape[0]
    tt = _pick(T, 256)

    def body(dm_ref, y_ref, x_ref, z_ref, ds_ref, g_ref, dy_ref, dz_ref, dxs_ref, dg_ref, dsk_ref):
        z = z_ref[...]
        y = y_ref[...]
        x = x_ref[...]
        sz = _silu(z)
        y2 = y * sz
        r = lax.rsqrt(jnp.mean(y2 * y2, axis=-1, keepdims=True) + EPS)
        d = dm_ref[...]
        gd = d * g_ref[...]
        dy2 = r * gd - y2 * (r * r * r) * jnp.mean(gd * y2, axis=-1, keepdims=True)
        dy = dy2 * sz
        dy_ref[...] = dy
        dz_ref[...] = (dy2 * y * _dsilu(z)).astype(dz_ref.dtype)
        dxs_ref[...] = dy * ds_ref[...]

        @pl.when(pl.program_id(1) == 0)
        def _():
            dg_ref[...] = jnp.zeros_like(dg_ref)
            dsk_ref[...] = jnp.zeros_like(dsk_ref)

        dg_ref[...] += jnp.broadcast_to(jnp.sum(d * y2 * r, axis=0, keepdims=True), dg_ref.shape)
        lane_sum = jnp.broadcast_to(jnp.sum(dy * x, axis=0, keepdims=True), (8, GW))
        src = lax.broadcasted_iota(jnp.int32, (GW, HP), 0)
        head = lax.broadcasted_iota(jnp.int32, (GW, HP), 1)
        to_head = jnp.where((src >= PH * head) & (src < PH * (head + 1)), 1.0, 0.0).astype(F32)
        dsk_ref[...] += jnp.dot(lane_sum, to_head, precision=HI, preferred_element_type=F32)

    blk = pl.BlockSpec((tt, GW), lambda g, i: (i, g))
    vec = pl.BlockSpec((1, GW), lambda g, i: (0, g))
    acc = pl.BlockSpec((8, GW), lambda g, i: (0, g))
    return pl.pallas_call(
        body, name="ssm_combine_bwd",
        out_shape=(jax.ShapeDtypeStruct((T, D_INNER), F32), jax.ShapeDtypeStruct((T, D_INNER), BF),
                   jax.ShapeDtypeStruct((T, D_INNER), F32), jax.ShapeDtypeStruct((8, D_INNER), F32),
                   jax.ShapeDtypeStruct((8, SSM_GROUPS * HP), F32)),
        grid=(SSM_GROUPS, T // tt), in_specs=[blk, blk, blk, blk, vec, vec],
        out_specs=(blk, blk, blk, acc, pl.BlockSpec((8, HP), lambda g, i: (0, g))),
        compiler_params=pltpu.CompilerParams(dimension_semantics=("parallel", "arbitrary")),
    )(dm, y, act, u, dskip, gain)


def _loss_head(y, target):
    T, D = y.shape
    tt = _pick(T, 512)

    def body(y_ref, t_ref, dy_ref, dyb_ref, l_ref):
        e = y_ref[...] - t_ref[...]
        dy_ref[...] = e * (1.0 / D)
        dyb_ref[...] = (e * (1.0 / D)).astype(dyb_ref.dtype)

        @pl.when(pl.program_id(0) == 0)
        def _():
            l_ref[...] = jnp.zeros_like(l_ref)

        l_ref[...] += jnp.sum(e * e) * (0.5 / D)

    blk = pl.BlockSpec((tt, D), lambda i: (i, 0))
    return pl.pallas_call(
        body, name="loss_head",
        out_shape=(jax.ShapeDtypeStruct((T, D), F32), jax.ShapeDtypeStruct((T, D), BF), jax.ShapeDtypeStruct((8, 128), F32)),
        grid=(T // tt,), in_specs=[blk, blk], out_specs=(blk, blk, pl.BlockSpec((8, 128), lambda i: (0, 0))),
        compiler_params=pltpu.CompilerParams(dimension_semantics=("arbitrary",)),
    )(y, target)


def _adamw(w, g, m, v, *, name):
    R, C = w.shape
    cap = max(8, (1 << 18) // C)
    tr = R
    if R % 8 == 0:
        tr = 8
        for cand in range(8, min(R, cap) + 1, 8):
            if R % cand == 0:
                tr = cand

    def body(w_ref, g_ref, m_ref, v_ref, d_ref, nm_ref, nv_ref):
        gg = g_ref[...]
        nm = ADAM_B1 * m_ref[...] + (1.0 - ADAM_B1) * gg
        nv = ADAM_B2 * v_ref[...] + (1.0 - ADAM_B2) * jnp.square(gg)
        m_hat = nm / (1.0 - ADAM_B1 ** ADAM_STEP)
        v_hat = nv / (1.0 - ADAM_B2 ** ADAM_STEP)
        d_ref[...] = -ADAM_LR * (m_hat / (jnp.sqrt(v_hat) + ADAM_EPS) + ADAM_WD * w_ref[...])
        nm_ref[...] = nm
        nv_ref[...] = nv

    blk = pl.BlockSpec((tr, C), lambda i: (i, 0))
    return pl.pallas_call(
        body, name=name, out_shape=(jax.ShapeDtypeStruct((R, C), F32),) * 3, grid=(R // tr,),
        in_specs=[blk] * 4, out_specs=(blk,) * 3,
    )(w, g, m, v)


ANY = pl.BlockSpec(memory_space=pl.ANY)


def _chip_peers():
    x, y, c = lax.axis_index("x"), lax.axis_index("y"), lax.axis_index("c")
    return x, y, c, [(1 - x, y), (x, 1 - y), (1 - x, 1 - y)]


def _half_rows(c, rh):
    return pl.ds(pl.multiple_of(c * rh, 16), rh)


def _gather_chips(wb, wf):
    rh = wb.shape[0] // 2

    def body(wb_ref, wf_ref, ob_ref, of_ref, send_sems, recv_sems, loc_sems):
        x, y, c, peers = _chip_peers()
        me = 2 * x + y
        half, other = _half_rows(c, rh), _half_rows(1 - c, rh)
        local = [pltpu.make_async_copy(wb_ref, ob_ref.at[me], loc_sems.at[0]),
                 pltpu.make_async_copy(wf_ref, of_ref.at[me], loc_sems.at[1])]
        for cp in local:
            cp.start()

        def chip_copy(k, slot):
            px, py = peers[k]
            return pltpu.make_async_remote_copy(
                src_ref=wb_ref.at[half], dst_ref=ob_ref.at[slot, half], send_sem=send_sems.at[k], recv_sem=recv_sems.at[k],
                device_id=(px, py, c), device_id_type=MESH)

        def passed_on(k, slot, rows):
            return pltpu.make_async_remote_copy(
                src_ref=ob_ref.at[slot, rows], dst_ref=ob_ref.at[slot, rows], send_sem=send_sems.at[3 + k],
                recv_sem=recv_sems.at[3 + k], device_id=(x, y, 1 - c), device_id_type=MESH)

        def small_copy(k, slot):
            px, py = peers[k]
            return pltpu.make_async_remote_copy(
                src_ref=wf_ref, dst_ref=of_ref.at[slot], send_sem=send_sems.at[6 + k], recv_sem=recv_sems.at[6 + k],
                device_id=(px, py, c), device_id_type=MESH)

        sends = [chip_copy(k, me) for k in range(3)] + [small_copy(k, me) for k in range(3)]
        for cp in sends:
            cp.start()
        chip_of = [2 * px + py for px, py in peers]
        for k in range(3):
            chip_copy(k, chip_of[k]).wait_recv()
            cp = passed_on(k, chip_of[k], half)
            cp.start()
            sends.append(cp)
        for k in range(3):
            passed_on(k, chip_of[k], other).wait_recv()
            small_copy(k, chip_of[k]).wait_recv()
        for cp in sends:
            cp.wait_send()
        for cp in local:
            cp.wait()

    return pl.pallas_call(
        body, name="gather_weights",
        out_shape=(jax.ShapeDtypeStruct((4,) + wb.shape, wb.dtype), jax.ShapeDtypeStruct((4,) + wf.shape, wf.dtype)),
        in_specs=[ANY, ANY], out_specs=(ANY, ANY),
        scratch_shapes=[pltpu.SemaphoreType.DMA((9,)), pltpu.SemaphoreType.DMA((9,)), pltpu.SemaphoreType.DMA((2,))],
    )(wb, wf)


def _halves_to_sibling(gp):
    rh = gp.shape[1] // 2

    def body(gp_ref, o_ref, send_sem, recv_sem):
        x, y, c = lax.axis_index("x"), lax.axis_index("y"), lax.axis_index("c")
        cp = pltpu.make_async_remote_copy(src_ref=gp_ref.at[:, _half_rows(1 - c, rh), :], dst_ref=o_ref, send_sem=send_sem,
                                          recv_sem=recv_sem, device_id=(x, y, 1 - c), device_id_type=MESH)
        cp.start()
        cp.wait()

    return pl.pallas_call(
        body, name="halves_to_sibling", out_shape=jax.ShapeDtypeStruct((gp.shape[0], rh, gp.shape[2]), gp.dtype),
        in_specs=[ANY], out_specs=ANY, scratch_shapes=[pltpu.SemaphoreType.DMA, pltpu.SemaphoreType.DMA],
    )(gp)


def _row_tile(rows, cap=1024):
    tr = 16
    for cand in range(16, cap + 1, 16):
        if rows % cand == 0:
            tr = cand
    return tr


def _add_halves(gp, sib, core):
    n, rh, C = sib.shape
    tr = _row_tile(rh)
    nt = rh // tr

    def body(c_ref, g_ref, s_ref, o_ref):
        o_ref[...] = (g_ref[...].astype(F32) + s_ref[...].astype(F32)).astype(o_ref.dtype)

    blk = pl.BlockSpec((1, tr, C), lambda j, i, c: (j, i, 0))
    return pl.pallas_call(
        body, name="add_halves", out_shape=jax.ShapeDtypeStruct(sib.shape, sib.dtype),
        grid_spec=pltpu.PrefetchScalarGridSpec(
            num_scalar_prefetch=1, grid=(n, nt),
            in_specs=[pl.BlockSpec((1, tr, C), lambda j, i, c: (j, c[0] * nt + i, 0)), blk], out_specs=blk),
    )(core, gp, sib)


def _join_halves(mine):
    rh = mine.shape[0]

    def body(m_ref, o_ref, send_sem, recv_sem, loc_sem):
        x, y, c = lax.axis_index("x"), lax.axis_index("y"), lax.axis_index("c")
        half, other = _half_rows(c, rh), _half_rows(1 - c, rh)
        local = pltpu.make_async_copy(m_ref, o_ref.at[half], loc_sem)
        local.start()

        def copy(rows):
            return pltpu.make_async_remote_copy(src_ref=m_ref, dst_ref=o_ref.at[rows], send_sem=send_sem, recv_sem=recv_sem,
                                                device_id=(x, y, 1 - c), device_id_type=MESH)

        send = copy(half)
        send.start()
        copy(other).wait_recv()
        send.wait_send()
        local.wait()

    return pl.pallas_call(
        body, name="join_halves", out_shape=jax.ShapeDtypeStruct((2 * rh, mine.shape[1]), mine.dtype),
        in_specs=[ANY], out_specs=ANY,
        scratch_shapes=[pltpu.SemaphoreType.DMA, pltpu.SemaphoreType.DMA, pltpu.SemaphoreType.DMA],
    )(mine)


def _exchange_chips(gp):
    def body(gp_ref, out_ref, send_sems, recv_sems, loc_sem):
        x, y, c, peers = _chip_peers()
        me = 2 * x + y
        local = pltpu.make_async_copy(gp_ref.at[me], out_ref.at[me], loc_sem)
        local.start()

        def copies(sending):
            out = []
            for k, (px, py) in enumerate(peers):
                p = 2 * px + py
                out.append(pltpu.make_async_remote_copy(
                    src_ref=gp_ref.at[p], dst_ref=out_ref.at[me if sending else p],
                    send_sem=send_sems.at[k], recv_sem=recv_sems.at[k], device_id=(px, py, c), device_id_type=MESH))
            return out

        sends = copies(True)
        for cp in sends:
            cp.start()
        for cp in copies(False):
            cp.wait_recv()
        for cp in sends:
            cp.wait_send()
        local.wait()

    return pl.pallas_call(
        body, name="exchange_grads", out_shape=jax.ShapeDtypeStruct(gp.shape, gp.dtype),
        in_specs=[ANY], out_specs=ANY,
        scratch_shapes=[pltpu.SemaphoreType.DMA((3,)), pltpu.SemaphoreType.DMA((3,)), pltpu.SemaphoreType.DMA],
    )(gp)


def _sum_slots(r4):
    _, R, C = r4.shape
    tr = _row_tile(R)

    def body(r_ref, o_ref):
        acc = r_ref[0].astype(F32)
        for s in range(1, 4):
            acc = acc + r_ref[s].astype(F32)
        o_ref[...] = acc

    return pl.pallas_call(
        body, name="sum_slots", out_shape=jax.ShapeDtypeStruct((R, C), F32), grid=(R // tr,),
        in_specs=[pl.BlockSpec((4, tr, C), lambda i: (0, i, 0))], out_specs=pl.BlockSpec((tr, C), lambda i: (i, 0)),
    )(r4)


N_DEV = 8


def _allreduce_small(p):
    rs = p.shape[0]

    def body(x_ref, sum_ref, all_ref, send_sems, recv_sems, local_sem):
        x, y, c = lax.axis_index("x"), lax.axis_index("y"), lax.axis_index("c")
        me, sibling = (x, y, c), (x, y, 1 - c)
        chips = [(1 - x, y), (x, 1 - y), (1 - x, 1 - y)]

        def rows(px, py, pc):
            return all_ref.at[pl.ds((4 * px + 2 * py + pc) * rs, rs), :]

        def copy(k, block, to, src=None):
            return pltpu.make_async_remote_copy(
                src_ref=rows(*block) if src is None else src, dst_ref=rows(*block),
                send_sem=send_sems.at[k], recv_sem=recv_sems.at[k], device_id=to, device_id_type=MESH)

        mine = pltpu.make_async_copy(x_ref, rows(*me), local_sem)
        mine.start()
        first = [copy(0, me, sibling, src=x_ref)]
        first += [copy(1 + j, me, (*chip, c), src=x_ref) for j, chip in enumerate(chips)]
        for cp in first:
            cp.start()
        passed = [copy(4 + j, (*chip, c), sibling) for j, chip in enumerate(chips)]
        for j, chip in enumerate(chips):
            copy(1 + j, (*chip, c), me).wait_recv()
            passed[j].start()
        copy(0, sibling, me).wait_recv()
        for j, chip in enumerate(chips):
            copy(4 + j, (*chip, 1 - c), me).wait_recv()
        for cp in first + passed:
            cp.wait_send()
        mine.wait()
        acc = all_ref[0:rs, :]
        for d in range(1, N_DEV):
            acc = acc + all_ref[d * rs:(d + 1) * rs, :]
        sum_ref[...] = acc

    vmem = pl.BlockSpec(memory_space=pltpu.VMEM)
    return pl.pallas_call(
        body, name="allreduce_small", out_shape=jax.ShapeDtypeStruct((rs, 128), F32),
        in_specs=[vmem], out_specs=vmem,
        scratch_shapes=[pltpu.VMEM((N_DEV * rs, 128), F32), pltpu.SemaphoreType.DMA((7,)), pltpu.SemaphoreType.DMA((7,)),
                        pltpu.SemaphoreType.DMA],
    )(p)


WEIGHTS = ('ffn1_norm', 'ffn1_w_gate', 'ffn1_w_up', 'ffn1_w_down', 'mix_norm', 'w_in', 'q_a_norm', 'w_q_b',
           'kv_a_norm', 'w_kv_b', 'q_head_norm', 'k_head_norm', 'conv_w', 'conv_b', 'a_log_fwd', 'a_log_bwd',
           'dt_bias_fwd', 'dt_bias_bwd', 'd_skip', 'ssm_norm', 'w_attn_branch', 'w_ssm_branch', 'w_out',
           'ffn2_norm', 'ffn2_w_gate', 'ffn2_w_up', 'ffn2_w_down')
PACKED = (('ffn1_w_gate', (D_MODEL, D_FF), 1), ('ffn1_w_up', (D_MODEL, D_FF), 1), ('ffn1_w_down', (D_FF, D_MODEL), 0),
          ('w_in', (D_MODEL, sum(IN_SPLITS)), 1), ('w_q_b', (Q_LORA, N_HEADS * QK_HEAD), 1),
          ('w_kv_b', (KV_LORA, N_HEADS * (QK_NOPE + V_HEAD)), 1),
          ('w_attn_branch', (N_HEADS * V_HEAD, D_MODEL), 0), ('w_ssm_branch', (D_INNER, D_MODEL), 0),
          ('w_out', (D_MODEL, D_MODEL), 0),
          ('ffn2_w_gate', (D_MODEL, D_FF), 1), ('ffn2_w_up', (D_MODEL, D_FF), 1), ('ffn2_w_down', (D_FF, D_MODEL), 0))
PACK_W = 1024
N_CHIPS = 4
SMALL = (('ffn1_norm', 1024), ('mix_norm', 1024), ('q_a_norm', 384), ('kv_a_norm', 256), ('q_head_norm', 96),
         ('k_head_norm', 96), ('conv_b', 3072), ('a_log_fwd', 32), ('a_log_bwd', 32), ('dt_bias_fwd', 32),
         ('dt_bias_bwd', 32), ('d_skip', 32), ('ssm_norm', 2048), ('ffn2_norm', 1024),
         ('conv_w', CONV_WIDTH * XBC_DIM), ('loss', 1))


def _shard_shape(shape, axis):
    return tuple(s // N_CHIPS if a == axis else s for a, s in enumerate(shape))


def _pack_rows():
    rows = sum(math.prod(_shard_shape(shape, axis)) // PACK_W for _, shape, axis in PACKED)
    return -(-rows // 32) * 32


def _pack(shards):
    parts = [shards[name].reshape(-1, PACK_W) for name, _, _ in PACKED]
    rows = sum(p.shape[0] for p in parts)
    parts.append(jnp.zeros((_pack_rows() - rows, PACK_W), parts[0].dtype))
    return jnp.concatenate(parts, axis=0)


def _unpack(packed):
    out, r = {}, 0
    for name, shape, axis in PACKED:
        sh = _shard_shape(shape, axis)
        n = math.prod(sh) // PACK_W
        out[name] = packed[r:r + n].reshape(sh)
        r += n
    return out


def _pack_small(vals):
    parts = []
    for name, n in SMALL:
        pad = -(-n // 128) * 128 - n
        parts.append(jnp.pad(vals[name].reshape(-1).astype(F32), (0, pad)).reshape(-1, 128))
    rows = sum(p.shape[0] for p in parts)
    parts.append(jnp.zeros((-(-rows // 8) * 8 - rows, 128), F32))
    return jnp.concatenate(parts, axis=0)


def _unpack_small(packed):
    out, r = {}, 0
    for name, n in SMALL:
        k = -(-n // 128)
        out[name] = packed[r:r + k].reshape(-1)[:n]
        r += k
    return out


def _pad_heads(w, axis, per_head, lo, hi):
    shape = w.shape
    w = w.reshape(shape[:axis] + (N_HEADS, per_head) + shape[axis + 1:])
    w = lax.slice_in_dim(w, lo, hi, axis=axis + 1)
    pad = [(0, 0)] * w.ndim
    pad[axis + 1] = (0, HP - (hi - lo))
    w = jnp.pad(w, pad)
    return w.reshape(shape[:axis] + (N_HEADS * HP,) + shape[axis + 1:])


def _unpad_heads(w, axis, keep):
    shape = w.shape
    w = w.reshape(shape[:axis] + (N_HEADS, HP) + shape[axis + 1:])
    return lax.slice_in_dim(w, 0, keep, axis=axis + 1)


def _split_w_in(w):
    o = [0]
    for s in IN_SPLITS:
        o.append(o[-1] + s)
    return [w[:, o[i]:o[i + 1]] for i in range(len(IN_SPLITS))]


def _pad_w_in(w):
    cq, ckv, kpe, z, xbc, dtf, dtb, ga, gb = _split_w_in(w)
    kpe_pad = jnp.pad(kpe, ((0, 0), (QK_NOPE, HP - QK_HEAD)))
    dt_pad = jnp.pad(jnp.concatenate([dtf, dtb], axis=1), ((0, 0), (0, HP - 2 * SSM_HEADS)))
    return jnp.concatenate([z, ga, gb, xbc, cq, ckv, kpe_pad, dt_pad], axis=1)


def _unpad_w_in(g):
    z, ga, gb, xbc = g[:, U_Z:U_GA], g[:, U_GA:U_GB], g[:, U_GB:U_XBC], g[:, U_XBC:U_SMALL]
    s = g[:, U_SMALL:]
    cq, ckv = s[:, S_CQ:S_CKV], s[:, S_CKV:S_KPE]
    kpe = s[:, S_KPE + QK_NOPE:S_KPE + QK_HEAD]
    dtf, dtb = s[:, S_DT:S_DT + SSM_HEADS], s[:, S_DT + SSM_HEADS:S_DT + 2 * SSM_HEADS]
    return jnp.concatenate([cq, ckv, kpe, z, xbc, dtf, dtb, ga, gb], axis=1)


def _lanes128(parts):
    row = jnp.concatenate([p.reshape(-1) for p in parts])
    return jnp.pad(row, (0, HP - row.shape[0])).reshape(1, HP)


FF_TILE = D_FF // 2


def _ffn_fwd(x, g, wg, wu, wd, tag):
    h = _rms_fwd(x, g, name=tag + "_norm")
    gate, up, act = _mm([h], [wg, wu], name=tag + "_up", out_dtypes=(F32, F32, BF), tm=512, tn=FF_TILE,
                        epilogue=lambda a, b: (a, b, _silu(a) * b))
    out = _mm([act], [wd], name=tag + "_down", extras=[x], epilogue=lambda acc, r: (r + 0.5 * acc,))
    return out, (h, gate, up, act)


def _ffn_bwd(dout, dout_bf, x, g, wg, wu, wd, saved, tag):
    h, gate, up, act = saved
    dgate, dup = _mm([dout_bf], [wd], name=tag + "_down_dx", tb=True, extras=[gate, up], out_dtypes=(BF, BF),
                     tm=512, tn=FF_TILE, epilogue=lambda acc, a, b: (0.5 * acc * b * _dsilu(a), 0.5 * acc * _silu(a)))
    dwd = _mm([act], [dout_bf], name=tag + "_down_dw", ta=True, tm=FF_TILE, tk=1024, epilogue=lambda acc: (0.5 * acc,))
    dwg, dwu = _mm([h], [dgate, dup], name=tag + "_up_dw", ta=True, out_dtypes=(F32, F32), tm=512, tn=FF_TILE, tk=1024)
    dh = _mm([dgate, dup], [wg, wu], name=tag + "_up_dx", tb=True)
    dx, dx_bf, dg = _rms_bwd(dh, x, g, name=tag + "_norm_bwd", add=dout, out_dtypes=(F32, BF))
    return dx, dx_bf, dg, dwg, dwu, dwd


KPE_BLK = (U_SMALL + S_KPE) // HP
SMALL_BLK = U_SMALL // SMALL_W


def _local_step(x, pos_col, target, W, P):
    T = x.shape[0]
    sig = jax.nn.sigmoid
    x1, ffn1 = _ffn_fwd(x, P["ffn1_norm"], W["wg1"], W["wu1"], W["wd1"], "ffn1")
    h = _rms_fwd(x1, P["mix_norm"], name="mix_norm")
    u = _mm([h], [W["w_in"]], name="in_proj", tn=1152)
    cqn = _rms_fwd(u, P["q_a_norm"], name="q_a_norm", blk_w=SMALL_W, blk_idx=SMALL_BLK, off=S_CQ, width=Q_LORA)
    ckvn = _rms_fwd(u, P["kv_a_norm"], name="kv_a_norm", blk_w=SMALL_W, blk_idx=SMALL_BLK, off=S_CKV, width=KV_LORA)
    q_raw = _mm([cqn], [W["wq"]], name="q_proj")
    k_raw, v = _mm([ckvn], [W["wk"], W["wv"]], name="kv_proj", out_dtypes=(F32, BF))
    rc, rs = _rope_tables(pos_col, P["freq"])
    q = _qk_prep_fwd(q_raw, None, P["q_head_norm"], rc, rs, name="q_prep", out_scale=Q_SCALE)
    k = _qk_prep_fwd(k_raw, u, P["k_head_norm"], rc, rs, name="k_prep", kpe_blk=KPE_BLK)
    o, lse = _attn_fwd(q, k, v)
    pre, act = _conv_fwd(u, P["conv_w8"], P["conv_b"])
    y_f, st_f = _ssd_fwd(act, u, P["dt_bias8"], P["a_log8"], rev=False, name="ssd_fwd_f")
    y_b, st_b = _ssd_fwd(act, u, P["dt_bias8"], P["a_log8"], rev=True, name="ssd_fwd_b")
    ysum, m = _ssm_combine_fwd(y_f, y_b, act, u, P["d_skip_lanes"], P["ssm_norm"])
    ab = _mm([o], [W["pa"]], name="attn_branch")
    mb, merged = _mm([m], [W["pb"]], name="ssm_branch", extras=[ab, u, u], extra_offs=(0, U_GA, U_GB), out_dtypes=(F32, BF),
                     epilogue=lambda acc, a, ga, gb: (acc, sig(ga) * a + sig(gb) * acc))
    x2 = _mm([merged], [W["wo"]], name="out_proj", extras=[x1], epilogue=lambda acc, r: (r + acc,))
    y, ffn2 = _ffn_fwd(x2, P["ffn2_norm"], W["wg2"], W["wu2"], W["wd2"], "ffn2")
    dy, dy_bf, loss = _loss_head(y, target)
    dx2, dx2_bf, dg_ffn2, dwg2, dwu2, dwd2 = _ffn_bwd(dy, dy_bf, x2, P["ffn2_norm"], W["wg2"], W["wu2"], W["wd2"], ffn2,
                                                      "ffn2")

    def gate_bwd(dmrg, a, b, ga, gb):
        sa, sb = sig(ga), sig(gb)
        return dmrg * sa, dmrg * sb, dmrg * a * sa * (1.0 - sa), dmrg * b * sb * (1.0 - sb)

    dab, dmb, dga, dgb = _mm([dx2_bf], [W["wo"]], name="out_proj_dx", tb=True, extras=[ab, mb, u, u],
                             extra_offs=(0, 0, U_GA, U_GB), out_dtypes=(BF,) * 4, epilogue=gate_bwd)
    dwo = _mm([merged], [dx2_bf], name="out_proj_dw", ta=True)
    dpa = _mm([o], [dab], name="attn_branch_dw", ta=True)
    do = _mm([dab], [W["pa"]], name="attn_branch_dx", tb=True)
    dpb = _mm([m], [dmb], name="ssm_branch_dw", ta=True)
    dm = _mm([dmb], [W["pb"]], name="ssm_branch_dx", tb=True)
    dyssd, dz, dxs_skip, dg_ssm, dskip = _ssm_combine_bwd(dm, ysum, act, u, P["d_skip_lanes"], P["ssm_norm"])
    dxs_f, db_f, dc_f, ddt, dalog_f, dbias_f = _ssd_bwd(act, u, P["dt_bias8"], P["a_log8"], st_f, dyssd,
                                                        jnp.zeros((T, HP), F32), rev=False, name="ssd_bwd_f")
    dxs_b, db_b, dc_b, ddt, dalog_b, dbias_b = _ssd_bwd(act, u, P["dt_bias8"], P["a_log8"], st_b, dyssd, ddt,
                                                        rev=True, name="ssd_bwd_b")
    dxbc, dconv = [], []
    for tag, col0, parts in (("x", 0, [dxs_f, dxs_b, dxs_skip]), ("b", D_INNER, [db_f, db_b]),
                             ("c", D_INNER + SSM_GROUPS * D_STATE, [dc_f, dc_b])):
        dpre = _conv_dpre(parts, pre, col0, name="conv_dpre_" + tag)
        dxp, dwp = _conv_bwd(dpre, u, P["conv_w8"], col0, name="conv_bwd_" + tag)
        dxbc.append(dxp)
        dconv.append(dwp)
    dconv = jnp.concatenate(dconv, axis=1)
    dq, dk, dv = _attn_bwd(q, k, v, do, o, lse)
    dq_raw, dg_qh = _qk_prep_bwd(dq, q_raw, None, P["q_head_norm"], rc, rs, name="q_prep_bwd", in_scale=ATTN_SCALE)
    dk_raw, dg_kh, dkpe = _qk_prep_bwd(dk, k_raw, u, P["k_head_norm"], rc, rs, name="k_prep_bwd", kpe_blk=KPE_BLK,
                                       in_scale=1.0 / LOG2E)
    dwq = _mm([cqn], [dq_raw], name="q_proj_dw", ta=True)
    dcqn = _mm([dq_raw], [W["wq"]], name="q_proj_dx", tb=True)
    dwk, dwv = _mm([ckvn], [dk_raw, dv], name="kv_proj_dw", ta=True, out_dtypes=(F32, F32))
    dckvn = _mm([dk_raw, dv], [W["wk"], W["wv"]], name="kv_proj_dx", tb=True)
    dcq, dg_qa = _rms_bwd(dcqn, u, P["q_a_norm"], name="q_a_norm_bwd", blk_w=SMALL_W, blk_idx=SMALL_BLK, off=S_CQ,
                          width=Q_LORA, out_dtypes=(BF,))
    dckv, dg_kva = _rms_bwd(dckvn, u, P["kv_a_norm"], name="kv_a_norm_bwd", blk_w=SMALL_W, blk_idx=SMALL_BLK,
                            off=S_CKV, width=KV_LORA, out_dtypes=(BF,))
    du = jnp.concatenate([dz, dga, dgb] + dxbc + [dcq, dckv, dkpe.astype(BF), ddt.astype(BF)], axis=1)
    dw_in = _mm([h], [du], name="in_proj_dw", ta=True, tn=1152)
    dh = _mm([du], [W["w_in"]], name="in_proj_dx", tb=True)
    dx1, dx1_bf, dg_mix = _rms_bwd(dh, x1, P["mix_norm"], name="mix_norm_bwd", add=dx2, out_dtypes=(F32, BF))
    dx, _, dg_ffn1, dwg1, dwu1, dwd1 = _ffn_bwd(dx1, dx1_bf, x, P["ffn1_norm"], W["wg1"], W["wu1"], W["wd1"], ffn1, "ffn1")
    dW = dict(wg1=dwg1, wu1=dwu1, wd1=dwd1, w_in=dw_in, wq=dwq, wk=dwk, wv=dwv, pa=dpa, pb=dpb, wo=dwo,
              wg2=dwg2, wu2=dwu2, wd2=dwd2)
    dP = dict(ffn1_norm=dg_ffn1[0], mix_norm=dg_mix[0], q_a_norm=dg_qa[0], kv_a_norm=dg_kva[0],
              q_head_norm=dg_qh[0, :QK_HEAD], k_head_norm=dg_kh[0, :QK_HEAD], conv_b=dconv[CONV_WIDTH],
              a_log_fwd=dalog_f[0, :SSM_HEADS], a_log_bwd=dalog_b[0, SSM_HEADS:2 * SSM_HEADS],
              dt_bias_fwd=dbias_f[0, :SSM_HEADS], dt_bias_bwd=dbias_b[0, SSM_HEADS:2 * SSM_HEADS],
              d_skip=dskip[0].reshape(SSM_GROUPS, HP)[:, :HG], ssm_norm=dg_ssm[0], ffn2_norm=dg_ffn2[0],
              conv_w=dconv[:CONV_WIDTH], loss=loss[0, 0])
    return dx, dW, dP


def _prepare(w, conv_w_full):
    kvb = w["w_kv_b"]
    W = dict(wg1=w["ffn1_w_gate"], wu1=w["ffn1_w_up"], wd1=w["ffn1_w_down"], w_in=_pad_w_in(w["w_in"]),
             wq=_pad_heads(w["w_q_b"], 1, QK_HEAD, 0, QK_HEAD),
             wk=_pad_heads(kvb, 1, QK_NOPE + V_HEAD, 0, QK_NOPE),
             wv=_pad_heads(kvb, 1, QK_NOPE + V_HEAD, QK_NOPE, QK_NOPE + V_HEAD),
             pa=_pad_heads(w["w_attn_branch"], 0, V_HEAD, 0, V_HEAD), pb=w["w_ssm_branch"], wo=w["w_out"],
             wg2=w["ffn2_w_gate"], wu2=w["ffn2_w_up"], wd2=w["ffn2_w_down"])
    inv_freq = [1.0 / (ROPE_BASE ** (j / QK_ROPE)) for j in range(0, QK_ROPE, 2)]
    freq = [0.0] * QK_NOPE + inv_freq + inv_freq + [0.0] * (HP - QK_HEAD)
    P = {n: w[n] for n in ("ffn1_norm", "mix_norm", "q_a_norm", "kv_a_norm", "ssm_norm", "ffn2_norm", "conv_b")}
    P.update(q_head_norm=_lanes128([w["q_head_norm"]]), k_head_norm=_lanes128([w["k_head_norm"]]),
             conv_w8=jnp.pad(conv_w_full, ((0, 8 - CONV_WIDTH), (0, 0))),
             dt_bias8=jnp.broadcast_to(_lanes128([w["dt_bias_fwd"], w["dt_bias_bwd"]]), (8, HP)),
             a_log8=jnp.broadcast_to(_lanes128([w["a_log_fwd"], w["a_log_bwd"]]), (8, HP)),
             d_skip_lanes=jnp.repeat(w["d_skip"].reshape(-1), PH).reshape(1, D_INNER),
             freq=jnp.asarray(freq, F32).reshape(1, HP))
    return W, P


def _unprepare(dW):
    dkvb = jnp.concatenate([_unpad_heads(dW["wk"], 1, QK_NOPE), _unpad_heads(dW["wv"], 1, V_HEAD)], axis=2)
    return dict(ffn1_w_gate=dW["wg1"], ffn1_w_up=dW["wu1"], ffn1_w_down=dW["wd1"], w_in=_unpad_w_in(dW["w_in"]),
                w_q_b=_unpad_heads(dW["wq"], 1, QK_HEAD).reshape(Q_LORA, N_HEADS * QK_HEAD),
                w_kv_b=dkvb.reshape(KV_LORA, N_HEADS * (QK_NOPE + V_HEAD)),
                w_attn_branch=_unpad_heads(dW["pa"], 0, V_HEAD).reshape(N_HEADS * V_HEAD, D_MODEL),
                w_ssm_branch=dW["pb"], w_out=dW["wo"],
                ffn2_w_gate=dW["wg2"], ffn2_w_up=dW["wu2"], ffn2_w_down=dW["wd2"])


def kernel(x, positions, ffn1_norm, ffn1_w_gate, ffn1_w_up, ffn1_w_down, mix_norm, w_in, q_a_norm, w_q_b, kv_a_norm, w_kv_b, q_head_norm, k_head_norm, conv_w, conv_b, a_log_fwd, a_log_bwd, dt_bias_fwd, dt_bias_bwd, d_skip, ssm_norm, w_attn_branch, w_ssm_branch, w_out, ffn2_norm, ffn2_w_gate, ffn2_w_up, ffn2_w_down, loss_target, m_ffn1_norm, m_ffn1_w_gate, m_ffn1_w_up, m_ffn1_w_down, m_mix_norm, m_w_in, m_q_a_norm, m_w_q_b, m_kv_a_norm, m_w_kv_b, m_q_head_norm, m_k_head_norm, m_conv_w, m_conv_b, m_a_log_fwd, m_a_log_bwd, m_dt_bias_fwd, m_dt_bias_bwd, m_d_skip, m_ssm_norm, m_w_attn_branch, m_w_ssm_branch, m_w_out, m_ffn2_norm, m_ffn2_w_gate, m_ffn2_w_up, m_ffn2_w_down, v_ffn1_norm, v_ffn1_w_gate, v_ffn1_w_up, v_ffn1_w_down, v_mix_norm, v_w_in, v_q_a_norm, v_w_q_b, v_kv_a_norm, v_w_kv_b, v_q_head_norm, v_k_head_norm, v_conv_w, v_conv_b, v_a_log_fwd, v_a_log_bwd, v_dt_bias_fwd, v_dt_bias_bwd, v_d_skip, v_ssm_norm, v_w_attn_branch, v_w_ssm_branch, v_w_out, v_ffn2_norm, v_ffn2_w_gate, v_ffn2_w_up, v_ffn2_w_down):
    given = dict(locals())
    T = x.shape[1]
    packed_names = [name for name, _, _ in PACKED]

    def two_d(a):
        return a.reshape(a.shape[1], -1) if a.ndim > 2 else a

    w_loc = {n: two_d(given[n]) for n in WEIGHTS}
    wb = _pack({n: w_loc[n].astype(BF) for n in packed_names})
    wf = jnp.pad(w_loc["conv_w"], ((0, 8 - CONV_WIDTH), (0, 0)))
    gb, gf = _gather_chips(wb, wf)
    per_chip = [_unpack(gb[j]) for j in range(N_CHIPS)]
    full = {n: jnp.concatenate([per_chip[j][n] for j in range(N_CHIPS)], axis=axis) for n, _, axis in PACKED}
    conv_w_full = jnp.concatenate([gf[j, :CONV_WIDTH] for j in range(N_CHIPS)], axis=1)
    full.update({n: w_loc[n] for n in WEIGHTS if n not in full and n != "conv_w"})
    W, P = _prepare(full, conv_w_full)
    dx, dW, dP = _local_step(x.reshape(T, D_MODEL), positions.reshape(T, 1).astype(F32), loss_target.reshape(T, D_MODEL), W, P)
    g_full = _unprepare(dW)
    slots = []
    for j in range(N_CHIPS):
        shards = {}
        for n, shape, axis in PACKED:
            size = shape[axis] // N_CHIPS
            shards[n] = lax.slice_in_dim(g_full[n], j * size, (j + 1) * size, axis=axis).astype(BF)
        slots.append(_pack(shards))
    gp = jnp.stack(slots)
    core = lax.axis_index("c").astype(jnp.int32).reshape(1)
    both_cores = _add_halves(gp, _halves_to_sibling(gp), core)
    g_packed = _unpack(_join_halves(_sum_slots(_exchange_chips(both_cores))))
    small = _unpack_small(_allreduce_small(_pack_small(dP)))
    chip = 2 * lax.axis_index("x") + lax.axis_index("y")
    grads = dict(g_packed)
    grads.update({n: small[n].reshape(1, -1) for n, _ in SMALL if n not in ("conv_w", "loss")})
    grads["conv_w"] = lax.dynamic_slice_in_dim(small["conv_w"].reshape(CONV_WIDTH, XBC_DIM), chip * (XBC_DIM // N_CHIPS),
                                               XBC_DIM // N_CHIPS, axis=1)
    out_g, out_d, out_m, out_v = [], [], [], []
    for n in WEIGHTS:
        shape = given[n].shape
        delta, new_m, new_v = _adamw(w_loc[n], grads[n], two_d(given["m_" + n]), two_d(given["v_" + n]), name="adamw_" + n)
        out_g.append(grads[n].reshape(shape))
        out_d.append(delta.reshape(shape))
        out_m.append(new_m.reshape(shape))
        out_v.append(new_v.reshape(shape))
    return (small["loss"].reshape(()), dx.reshape(x.shape), *out_g, *out_d, *out_m, *out_v)
```

```python
import functools
import math

import jax
import jax.numpy as jnp
from jax import lax
from jax.experimental import pallas as pl
from jax.experimental.pallas import tpu as pltpu

BF = jnp.bfloat16
F32 = jnp.float32
HI = lax.Precision.HIGHEST
MESH = pl.DeviceIdType.MESH

D_MODEL = 1024
D_FF = 2816
EPS = 1e-6
N_HEADS = 16
QK_NOPE = 64
QK_ROPE = 32
QK_HEAD = 96
V_HEAD = 64
Q_LORA = 384
KV_LORA = 256
ROPE_BASE = 10000.0
D_INNER = 2048
SSM_HEADS = 32
SSM_GROUPS = 4
D_STATE = 128
CONV_WIDTH = 5
CHUNK = 128
XBC_DIM = 3072
HP = 128
GW = D_INNER // SSM_GROUPS
HG = SSM_HEADS // SSM_GROUPS
PH = 64
U_Z, U_GA, U_GB, U_XBC, U_SMALL = 0, 2048, 3072, 4096, 7168
S_CQ, S_CKV, S_KPE, S_DT, SMALL_W = 0, 384, 640, 768, 896
U_PAD = U_SMALL + SMALL_W
IN_SPLITS = (Q_LORA, KV_LORA, QK_ROPE, D_INNER, XBC_DIM, SSM_HEADS, SSM_HEADS, D_MODEL, D_MODEL)

ADAM_LR = 0.001
ADAM_B1 = 0.9
ADAM_B2 = 0.999
ADAM_EPS = 1e-08
ADAM_WD = 0.01
ADAM_STEP = 10

NN = (((1,), (0,)), ((), ()))
NT = (((1,), (1,)), ((), ()))
TN = (((0,), (0,)), ((), ()))


def _pick(n, pref):
    best = None
    d = 128
    while d <= min(n, pref):
        if n % d == 0:
            best = d
        d += 128
    return best if best is not None else n


def _silu(x):
    return x * jax.nn.sigmoid(x)


def _dsilu(x):
    s = jax.nn.sigmoid(x)
    return s * (1.0 + x * (1.0 - s))


def _softplus(x):
    return jnp.maximum(x, 0.0) + jnp.log(1.0 + jnp.exp(-jnp.abs(x)))


def _mm(As, Bs, *, name, ta=False, tb=False, out_dtypes=(F32,), epilogue=None, extras=(), extra_offs=None,
        tm=1024, tn=512, tk=2048):
    As, Bs, extras = list(As), list(Bs), list(extras)
    a0, b0 = As[0], Bs[0]
    M, K = (a0.shape[1], a0.shape[0]) if ta else a0.shape
    N = b0.shape[0] if tb else b0.shape[1]
    tm, tn, tk = _pick(M, tm), _pick(N, tn), _pick(K, tk)
    nk = K // tk
    n_a, n_b, n_e, n_o = len(As), len(Bs), len(extras), len(out_dtypes)
    n_acc = (n_b if n_a == 1 else 1) if nk > 1 else 0
    if extra_offs is None:
        extra_offs = (0,) * n_e
    dn = (((0,) if ta else (1,), (1,) if tb else (0,)), ((), ()))
    bytes_a = sum(a.size * a.dtype.itemsize for a in As)
    bytes_b = sum(b.size * b.dtype.itemsize for b in Bs)
    n_outer = (N // tn) * bytes_a + bytes_b < (M // tm) * bytes_b + bytes_a

    def products(a_refs, b_refs):
        if n_a == 1:
            a = a_refs[0][...].astype(BF)
            return [lax.dot_general(a, b[...].astype(BF), dn, preferred_element_type=F32) for b in b_refs]
        total = None
        for a, b in zip(a_refs, b_refs):
            p = lax.dot_general(a[...].astype(BF), b[...].astype(BF), dn, preferred_element_type=F32)
            total = p if total is None else total + p
        return [total]

    def finish(accs, e_refs, o_refs):
        ex = [e[...] for e in e_refs]
        outs = epilogue(*accs, *ex) if epilogue is not None else tuple(accs)
        for o_ref, val in zip(o_refs, outs):
            o_ref[...] = val.astype(o_ref.dtype)

    def body(*refs):
        a_refs, b_refs = refs[:n_a], refs[n_a:n_a + n_b]
        e_refs = refs[n_a + n_b:n_a + n_b + n_e]
        o_refs = refs[n_a + n_b + n_e:n_a + n_b + n_e + n_o]
        acc_refs = refs[n_a + n_b + n_e + n_o:]
        if nk == 1:
            finish(products(a_refs, b_refs), e_refs, o_refs)
            return
        k = pl.program_id(2)

        @pl.when(k == 0)
        def _():
            for acc in acc_refs:
                acc[...] = jnp.zeros_like(acc)

        for acc, p in zip(acc_refs, products(a_refs, b_refs)):
            acc[...] += p

        @pl.when(k == nk - 1)
        def _():
            finish([acc[...] for acc in acc_refs], e_refs, o_refs)

    def at(f):
        return (lambda j, i, k: f(i, j, k)) if n_outer else f

    a_spec = pl.BlockSpec((tk, tm), at(lambda i, j, k: (k, i))) if ta else pl.BlockSpec((tm, tk), at(lambda i, j, k: (i, k)))
    b_spec = pl.BlockSpec((tn, tk), at(lambda i, j, k: (j, k))) if tb else pl.BlockSpec((tk, tn), at(lambda i, j, k: (k, j)))
    e_specs = [pl.BlockSpec((tm, tn), at(functools.partial(lambda i, j, k, o: (i, j + o), o=off // tn))) for off in extra_offs]
    for off in extra_offs:
        assert off % tn == 0
    outs = pl.pallas_call(
        body, name=name,
        out_shape=tuple(jax.ShapeDtypeStruct((M, N), dt) for dt in out_dtypes),
        grid=(N // tn, M // tm, nk) if n_outer else (M // tm, N // tn, nk),
        in_specs=[a_spec] * n_a + [b_spec] * n_b + e_specs,
        out_specs=tuple(pl.BlockSpec((tm, tn), at(lambda i, j, k: (i, j))) for _ in out_dtypes),
        scratch_shapes=[pltpu.VMEM((tm, tn), F32)] * n_acc,
        compiler_params=pltpu.CompilerParams(dimension_semantics=("parallel", "parallel", "arbitrary")),
    )(*As, *Bs, *extras)
    return outs[0] if n_o == 1 else outs


def _rms_fwd(x, g, *, name, blk_w=None, blk_idx=0, off=0, width=None, out_dtype=BF):
    T = x.shape[0]
    blk_w = x.shape[1] if blk_w is None else blk_w
    width = blk_w if width is None else width
    tt = _pick(T, 512)

    def body(x_ref, g_ref, o_ref):
        xf = x_ref[:, off:off + width]
        r = lax.rsqrt(jnp.mean(xf * xf, axis=-1, keepdims=True) + EPS)
        o_ref[...] = (xf * r * g_ref[...]).astype(o_ref.dtype)

    return pl.pallas_call(
        body, name=name, out_shape=jax.ShapeDtypeStruct((T, width), out_dtype), grid=(T // tt,),
        in_specs=[pl.BlockSpec((tt, blk_w), lambda i: (i, blk_idx)), pl.BlockSpec((1, width), lambda i: (0, 0))],
        out_specs=pl.BlockSpec((tt, width), lambda i: (i, 0)),
    )(x, g)


def _rms_bwd(dy, x, g, *, name, blk_w=None, blk_idx=0, off=0, width=None, add=None, out_dtypes=(F32,)):
    T = x.shape[0]
    blk_w = x.shape[1] if blk_w is None else blk_w
    width = blk_w if width is None else width
    tt = _pick(T, 512)
    has_add = add is not None
    n_dx = len(out_dtypes)

    def body(*refs):
        dy_ref, x_ref, g_ref = refs[:3]
        dx_refs, dg_ref = refs[3 + has_add:3 + has_add + n_dx], refs[-1]
        xf = x_ref[:, off:off + width]
        d = dy_ref[...].astype(F32)
        r = lax.rsqrt(jnp.mean(xf * xf, axis=-1, keepdims=True) + EPS)
        gd = d * g_ref[...]
        dx = r * gd - xf * (r * r * r) * jnp.mean(gd * xf, axis=-1, keepdims=True)
        if has_add:
            dx = dx + refs[3][...]
        for dx_ref in dx_refs:
            dx_ref[...] = dx.astype(dx_ref.dtype)

        @pl.when(pl.program_id(0) == 0)
        def _():
            dg_ref[...] = jnp.zeros_like(dg_ref)

        dg_ref[...] += jnp.broadcast_to(jnp.sum(d * xf * r, axis=0, keepdims=True), dg_ref.shape)

    row = pl.BlockSpec((tt, width), lambda i: (i, 0))
    in_specs = [row, pl.BlockSpec((tt, blk_w), lambda i: (i, blk_idx)), pl.BlockSpec((1, width), lambda i: (0, 0))]
    args = [dy, x, g]
    if has_add:
        in_specs.append(row)
        args.append(add)
    return pl.pallas_call(
        body, name=name,
        out_shape=tuple(jax.ShapeDtypeStruct((T, width), dt) for dt in out_dtypes) + (jax.ShapeDtypeStruct((8, width), F32),),
        grid=(T // tt,), in_specs=in_specs,
        out_specs=(row,) * n_dx + (pl.BlockSpec((8, width), lambda i: (0, 0)),),
        compiler_params=pltpu.CompilerParams(dimension_semantics=("arbitrary",)),
    )(*args)


def _rope_tables(pos_col, freq_lane):
    T = pos_col.shape[0]
    tt = _pick(T, 512)

    def body(p_ref, f_ref, c_ref, s_ref):
        ang = p_ref[...] * f_ref[...]
        lane = lax.broadcasted_iota(jnp.int32, ang.shape, 1)
        c_ref[...] = jnp.where(lane < QK_HEAD, jnp.cos(ang), 0.0)
        sn = jnp.sin(ang)
        s_ref[...] = jnp.where((lane >= QK_NOPE) & (lane < QK_NOPE + 16), -sn,
                               jnp.where((lane >= QK_NOPE + 16) & (lane < QK_HEAD), sn, 0.0))

    return pl.pallas_call(
        body, name="rope_tables", out_shape=(jax.ShapeDtypeStruct((T, HP), F32),) * 2, grid=(T // tt,),
        in_specs=[pl.BlockSpec((tt, 1), lambda i: (i, 0)), pl.BlockSpec((1, HP), lambda i: (0, 0))],
        out_specs=(pl.BlockSpec((tt, HP), lambda i: (i, 0)),) * 2,
    )(pos_col, freq_lane)


def _swap_rope_halves(n):
    lane = lax.broadcasted_iota(jnp.int32, n.shape, 1)
    lo = (lane >= QK_NOPE) & (lane < QK_NOPE + 16)
    hi = (lane >= QK_NOPE + 16) & (lane < QK_HEAD)
    return jnp.where(lo, pltpu.roll(n, HP - 16, 1), jnp.where(hi, pltpu.roll(n, 16, 1), 0.0))


def _qk_prep_fwd(raw, kpe, gain, C, S, *, name, kpe_blk=0, out_scale=1.0):
    T = raw.shape[0]
    tt = _pick(T, 512)
    has_kpe = kpe is not None

    def body(*refs):
        if has_kpe:
            raw_ref, kpe_ref, g_ref, c_ref, s_ref, o_ref = refs
            xr = raw_ref[...] + kpe_ref[...]
        else:
            raw_ref, g_ref, c_ref, s_ref, o_ref = refs
            xr = raw_ref[...]
        r = lax.rsqrt(jnp.sum(xr * xr, axis=-1, keepdims=True) * (1.0 / QK_HEAD) + EPS)
        n = xr * r * g_ref[...]
        o_ref[...] = ((n * c_ref[...] + _swap_rope_halves(n) * s_ref[...]) * out_scale).astype(o_ref.dtype)

    head = pl.BlockSpec((tt, HP), lambda i, h: (i, h))
    shared = pl.BlockSpec((tt, HP), lambda i, h: (i, 0))
    kpe_spec = pl.BlockSpec((tt, HP), lambda i, h: (i, kpe_blk))
    in_specs = [head] + ([kpe_spec] if has_kpe else []) + [pl.BlockSpec((1, HP), lambda i, h: (0, 0)), shared, shared]
    args = [raw] + ([kpe] if has_kpe else []) + [gain, C, S]
    return pl.pallas_call(
        body, name=name, out_shape=jax.ShapeDtypeStruct(raw.shape, BF), grid=(T // tt, N_HEADS),
        in_specs=in_specs, out_specs=head,
    )(*args)


def _qk_prep_bwd(dout, raw, kpe, gain, C, S, *, name, kpe_blk=0, in_scale=1.0):
    T = raw.shape[0]
    tt = _pick(T, 512)
    has_kpe = kpe is not None

    def body(*refs):
        if has_kpe:
            d_ref, raw_ref, kpe_ref, g_ref, c_ref, s_ref, dx_ref, dg_ref, dkpe_ref = refs
            xr = raw_ref[...] + kpe_ref[...]
        else:
            d_ref, raw_ref, g_ref, c_ref, s_ref, dx_ref, dg_ref = refs
            xr = raw_ref[...]
        i, h = pl.program_id(0), pl.program_id(1)
        d = d_ref[...].astype(F32) * in_scale
        r = lax.rsqrt(jnp.sum(xr * xr, axis=-1, keepdims=True) * (1.0 / QK_HEAD) + EPS)
        dn = d * c_ref[...] + _swap_rope_halves(d * s_ref[...])
        gd = dn * g_ref[...]
        dx = r * gd - xr * (r * r * r) * (jnp.sum(gd * xr, axis=-1, keepdims=True) * (1.0 / QK_HEAD))
        dx_ref[...] = dx.astype(dx_ref.dtype)

        @pl.when((i == 0) & (h == 0))
        def _():
            dg_ref[...] = jnp.zeros_like(dg_ref)

        dg_ref[...] += jnp.broadcast_to(jnp.sum(dn * xr * r, axis=0, keepdims=True), dg_ref.shape)
        if has_kpe:
            @pl.when(h == 0)
            def _():
                dkpe_ref[...] = jnp.zeros_like(dkpe_ref)

            dkpe_ref[...] += dx

    head = pl.BlockSpec((tt, HP), lambda i, h: (i, h))
    shared = pl.BlockSpec((tt, HP), lambda i, h: (i, 0))
    kpe_spec = pl.BlockSpec((tt, HP), lambda i, h: (i, kpe_blk))
    in_specs = [head, head] + ([kpe_spec] if has_kpe else []) + [pl.BlockSpec((1, HP), lambda i, h: (0, 0)), shared, shared]
    args = [dout, raw] + ([kpe] if has_kpe else []) + [gain, C, S]
    out_shape = [jax.ShapeDtypeStruct(raw.shape, BF), jax.ShapeDtypeStruct((8, HP), F32)]
    out_specs = [head, pl.BlockSpec((8, HP), lambda i, h: (0, 0))]
    if has_kpe:
        out_shape.append(jax.ShapeDtypeStruct((T, HP), F32))
        out_specs.append(shared)
    return pl.pallas_call(
        body, name=name, out_shape=tuple(out_shape), grid=(T // tt, N_HEADS),
        in_specs=in_specs, out_specs=tuple(out_specs),
        compiler_params=pltpu.CompilerParams(dimension_semantics=("arbitrary", "arbitrary")),
    )(*args)


ATTN_SCALE = 1.0 / math.sqrt(QK_HEAD)
LOG2E = 1.0 / math.log(2.0)
Q_SCALE = ATTN_SCALE * LOG2E


def _attn_fwd(q, k, v):
    T = q.shape[0]
    tq = _pick(T, 256)

    def body(q_ref, k_ref, v_ref, o_ref, lse_ref):
        s = lax.dot_general(q_ref[...], k_ref[...], NT, preferred_element_type=F32)
        m = jnp.max(s, axis=-1, keepdims=True)
        p = jnp.exp2(s - m)
        l = jnp.sum(p, axis=-1, keepdims=True)
        o = jnp.dot(p.astype(BF), v_ref[...], preferred_element_type=F32)
        o_ref[...] = o / l
        lse_ref[...] = jnp.broadcast_to(m + jnp.log2(l), lse_ref.shape)

    qs = pl.BlockSpec((tq, HP), lambda h, i: (i, h))
    kv = pl.BlockSpec((T, HP), lambda h, i: (0, h))
    return pl.pallas_call(
        body, name="attn_fwd", out_shape=(jax.ShapeDtypeStruct(q.shape, F32),) * 2, grid=(N_HEADS, T // tq),
        in_specs=[qs, kv, kv], out_specs=(qs, qs),
        compiler_params=pltpu.CompilerParams(dimension_semantics=("parallel", "parallel")),
    )(q, k, v)


def _attn_bwd(q, k, v, do, o, lse):
    T = q.shape[0]
    tb = _pick(T, 512)
    nb = T // tb

    def body(q_ref, k_ref, v_ref, do_ref, o_ref, lse_ref, dq_ref, dk_ref, dv_ref):
        dq_ref[...] = jnp.zeros_like(dq_ref)

        def k_loop(j, carry):
            ks = pl.ds(pl.multiple_of(j * tb, tb), tb)
            kj, vj = k_ref[ks, :], v_ref[ks, :]

            def q_loop(i, acc):
                dk_acc, dv_acc = acc
                qs = pl.ds(pl.multiple_of(i * tb, tb), tb)
                qi = q_ref[qs, :]
                doi = do_ref[qs, :]
                delta = jnp.sum(doi * o_ref[qs, :], axis=-1, keepdims=True)
                dob = doi.astype(BF)
                s = lax.dot_general(qi, kj, NT, preferred_element_type=F32)
                p = jnp.exp2(s - lse_ref[qs, 0:1])
                dp = lax.dot_general(dob, vj, NT, preferred_element_type=F32)
                ds = (p * (dp - delta)).astype(BF)
                dv_acc = dv_acc + lax.dot_general(p.astype(BF), dob, TN, preferred_element_type=F32)
                dk_acc = dk_acc + lax.dot_general(ds, qi, TN, preferred_element_type=F32)
                dq_ref[qs, :] += jnp.dot(ds, kj, preferred_element_type=F32)
                return dk_acc, dv_acc

            zero = jnp.zeros((tb, HP), F32)
            dk_acc, dv_acc = lax.fori_loop(0, nb, q_loop, (zero, zero))
            dk_ref[ks, :] = dk_acc
            dv_ref[ks, :] = dv_acc.astype(dv_ref.dtype)
            return carry

        lax.fori_loop(0, nb, k_loop, 0)

    spec = pl.BlockSpec((T, HP), lambda h: (0, h))
    return pl.pallas_call(
        body, name="attn_bwd",
        out_shape=(jax.ShapeDtypeStruct(q.shape, F32), jax.ShapeDtypeStruct(q.shape, F32), jax.ShapeDtypeStruct(q.shape, BF)),
        grid=(N_HEADS,), in_specs=[spec] * 6, out_specs=(spec,) * 3,
        compiler_params=pltpu.CompilerParams(dimension_semantics=("parallel",)),
    )(q, k, v, do, o, lse)


CONV_TC = 512
CONV_PAD = CONV_WIDTH // 2


def _halo_specs(tr, col_of):
    r8 = tr // 8
    cur = pl.BlockSpec((tr, CONV_TC), lambda j, i: (i, col_of(j)))
    prev = pl.BlockSpec((8, CONV_TC), lambda j, i: (jnp.maximum(i * r8 - 1, 0), col_of(j)))

    def nxt_map(j, i, n8):
        return (jnp.minimum((i + 1) * r8, n8 - 1), col_of(j))

    return cur, prev, nxt_map


def _with_halo(prev_ref, cur_ref, next_ref, i, n_i):
    prev = jnp.where(i == 0, 0.0, prev_ref[...].astype(F32))
    nxt = jnp.where(i == n_i - 1, 0.0, next_ref[...].astype(F32))
    return jnp.concatenate([prev, cur_ref[...].astype(F32), nxt], axis=0)


def _conv_fwd(u, w8, b):
    T = u.shape[0]
    tr = _pick(T, 512)
    n_i = T // tr
    c0 = U_XBC // CONV_TC
    cur, prev, nxt_map = _halo_specs(tr, lambda j: c0 + j)
    nxt = pl.BlockSpec((8, CONV_TC), functools.partial(nxt_map, n8=T // 8))

    def body(p_ref, c_ref, n_ref, w_ref, b_ref, pre_ref, act_ref):
        i = pl.program_id(1)
        full = _with_halo(p_ref, c_ref, n_ref, i, n_i)
        acc = jnp.broadcast_to(b_ref[...], (tr, CONV_TC))
        for kk in range(CONV_WIDTH):
            acc = acc + full[8 - CONV_PAD + kk:8 - CONV_PAD + kk + tr, :] * w_ref[kk:kk + 1, :]
        pre_ref[...] = acc
        act_ref[...] = _silu(acc)

    out = pl.BlockSpec((tr, CONV_TC), lambda j, i: (i, j))
    return pl.pallas_call(
        body, name="conv_fwd", out_shape=(jax.ShapeDtypeStruct((T, XBC_DIM), F32),) * 2,
        grid=(XBC_DIM // CONV_TC, n_i),
        in_specs=[prev, cur, nxt, pl.BlockSpec((8, CONV_TC), lambda j, i: (0, j)), pl.BlockSpec((1, CONV_TC), lambda j, i: (0, j))],
        out_specs=(out, out),
    )(u, u, u, w8, b)


def _conv_dpre(dacts, pre, col0, *, name):
    T, width = dacts[0].shape
    tt = _pick(T, 512)
    n_d = len(dacts)
    c0 = col0 // CONV_TC

    def body(*refs):
        d = refs[0][...]
        for r in refs[1:n_d]:
            d = d + r[...]
        refs[n_d + 1][...] = d * _dsilu(refs[n_d][...])

    blk = pl.BlockSpec((tt, CONV_TC), lambda j, i: (i, j))
    return pl.pallas_call(
        body, name=name, out_shape=jax.ShapeDtypeStruct((T, width), F32), grid=(width // CONV_TC, T // tt),
        in_specs=[blk] * n_d + [pl.BlockSpec((tt, CONV_TC), lambda j, i: (i, c0 + j))], out_specs=blk,
    )(*dacts, pre)


def _conv_bwd(dpre, u, w8, col0, *, name):
    T, width = dpre.shape
    tr = _pick(T, 512)
    n_i = T // tr
    cd = col0 // CONV_TC
    cx = (U_XBC + col0) // CONV_TC
    d_cur, d_prev, d_nxt_map = _halo_specs(tr, lambda j: j)
    x_cur, x_prev, x_nxt_map = _halo_specs(tr, lambda j: cx + j)
    d_nxt = pl.BlockSpec((8, CONV_TC), functools.partial(d_nxt_map, n8=T // 8))
    x_nxt = pl.BlockSpec((8, CONV_TC), functools.partial(x_nxt_map, n8=T // 8))

    def body(dp_ref, dc_ref, dn_ref, xp_ref, xc_ref, xn_ref, w_ref, dx_ref, dw_ref):
        i = pl.program_id(1)
        dfull = _with_halo(dp_ref, dc_ref, dn_ref, i, n_i)
        xfull = _with_halo(xp_ref, xc_ref, xn_ref, i, n_i)
        dcur = dc_ref[...]
        dx = jnp.zeros((tr, CONV_TC), F32)
        rows = []
        for kk in range(CONV_WIDTH):
            dx = dx + dfull[8 + CONV_PAD - kk:8 + CONV_PAD - kk + tr, :] * w_ref[kk:kk + 1, :]
            rows.append(jnp.sum(dcur * xfull[8 - CONV_PAD + kk:8 - CONV_PAD + kk + tr, :], axis=0, keepdims=True))
        rows.append(jnp.sum(dcur, axis=0, keepdims=True))
        rows.append(jnp.zeros((2, CONV_TC), F32))
        dx_ref[...] = dx.astype(dx_ref.dtype)

        @pl.when(i == 0)
        def _():
            dw_ref[...] = jnp.zeros_like(dw_ref)

        dw_ref[...] += jnp.concatenate(rows, axis=0)

    out = pl.BlockSpec((tr, CONV_TC), lambda j, i: (i, j))
    return pl.pallas_call(
        body, name=name, out_shape=(jax.ShapeDtypeStruct((T, width), BF), jax.ShapeDtypeStruct((8, width), F32)),
        grid=(width // CONV_TC, n_i),
        in_specs=[d_prev, d_cur, d_nxt, x_prev, x_cur, x_nxt, pl.BlockSpec((8, CONV_TC), lambda j, i: (0, cd + j))],
        out_specs=(out, pl.BlockSpec((8, CONV_TC), lambda j, i: (0, j))),
        compiler_params=pltpu.CompilerParams(dimension_semantics=("parallel", "arbitrary")),
    )(dpre, dpre, dpre, u, u, u, w8)


def _ssd_common(dt_ref, bias_ref, a_ref, b_ref, c_ref, g, rev):
    rows = lax.broadcasted_iota(jnp.int32, (CHUNK, CHUNK), 0)
    cols = lax.broadcasted_iota(jnp.int32, (CHUNK, CHUNK), 1)
    head0 = (SSM_HEADS if rev else 0) + HG * g
    sel = jnp.where((rows == head0 + cols) & (cols < HG), 1.0, 0.0).astype(F32)
    pre = jnp.dot(dt_ref[...] + bias_ref[0:1, :], sel, precision=HI, preferred_element_type=F32)
    dt = _softplus(pre)
    a_sel = jnp.dot(-jnp.exp(a_ref[...]), sel, precision=HI, preferred_element_type=F32)[0:1, :]
    da = dt * a_sel
    incl = (cols >= rows) if rev else (cols <= rows)
    tri = jnp.where(incl, 1.0, 0.0).astype(F32)
    cs = jnp.dot(tri, da, precision=HI, preferred_element_type=F32)
    tot = cs[0:1, :] if rev else cs[CHUNK - 1:CHUNK, :]
    bm, cm = b_ref[...].astype(BF), c_ref[...].astype(BF)
    cb = lax.dot_general(cm, bm, NT, preferred_element_type=F32)
    return dict(sel=sel, pre=pre, dt=dt, a_sel=a_sel, cs=cs, csT=cs.T, tot=tot, bm=bm, cm=cm, cb=cb, incl=incl, tri=tri)


def _ssd_specs(T, rev, bwd):
    nc = T // CHUNK
    fwd_order = (lambda c: nc - 1 - c) if rev else (lambda c: c)
    cm = (lambda c: fwd_order(nc - 1 - c)) if bwd else fwd_order
    xs = pl.BlockSpec((CHUNK, GW), lambda c, g: (cm(c), g))
    bs = pl.BlockSpec((CHUNK, D_STATE), lambda c, g: (cm(c), D_INNER // D_STATE + g))
    cs = pl.BlockSpec((CHUNK, D_STATE), lambda c, g: (cm(c), (D_INNER + SSM_GROUPS * D_STATE) // D_STATE + g))
    dt = pl.BlockSpec((CHUNK, HP), lambda c, g: (cm(c), (U_SMALL + S_DT) // HP))
    vec = pl.BlockSpec((8, HP), lambda c, g: (0, 0))
    st = pl.BlockSpec((1, GW, D_STATE), lambda c, g: (cm(c), g, 0))
    return nc, cm, xs, bs, cs, dt, vec, st


def _ssd_fwd(act, u, bias8, a8, *, rev, name):
    T = act.shape[0]
    nc, cm, xs_s, b_s, c_s, dt_s, vec_s, st_s = _ssd_specs(T, rev, False)

    def body(x_ref, b_ref, c_ref, dt_ref, bias_ref, a_ref, y_ref, st_ref, state):
        c, g = pl.program_id(0), pl.program_id(1)

        @pl.when(c == 0)
        def _():
            state[g] = jnp.zeros((GW, D_STATE), F32)

        q = _ssd_common(dt_ref, bias_ref, a_ref, b_ref, c_ref, g, rev)
        for hh in range(HG):
            hs = slice(PH * hh, PH * (hh + 1))
            col, row = q["cs"][:, hh:hh + 1], q["csT"][hh:hh + 1, :]
            xdt = x_ref[:, hs] * q["dt"][:, hh:hh + 1]
            lmat = jnp.where(q["incl"], jnp.exp(col - row), 0.0)
            yd = jnp.dot((q["cb"] * lmat).astype(BF), xdt.astype(BF), preferred_element_type=F32)
            prev = state[g, hs, :]
            yo = lax.dot_general(q["cm"], prev.astype(BF), NT, preferred_element_type=F32) * jnp.exp(col)
            y_ref[:, hs] = yd + yo
            tot_h = q["tot"][:, hh:hh + 1]
            s_new = lax.dot_general((xdt * jnp.exp(tot_h - col)).astype(BF), q["bm"], TN, preferred_element_type=F32)
            st_ref[0, hs, :] = prev
            state[g, hs, :] = prev * jnp.exp(tot_h) + s_new

    return pl.pallas_call(
        body, name=name,
        out_shape=(jax.ShapeDtypeStruct((T, D_INNER), F32), jax.ShapeDtypeStruct((nc, D_INNER, D_STATE), F32)),
        grid=(nc, SSM_GROUPS), in_specs=[xs_s, b_s, c_s, dt_s, vec_s, vec_s], out_specs=(xs_s, st_s),
        scratch_shapes=[pltpu.VMEM((SSM_GROUPS, GW, D_STATE), F32)],
        compiler_params=pltpu.CompilerParams(dimension_semantics=("arbitrary", "arbitrary")),
    )(act, act, act, u, bias8, a8)


def _ssd_bwd(act, u, bias8, a8, states, dy, ddt_in, *, rev, name):
    T = act.shape[0]
    nc, cm, xs_s, b_s, c_s, dt_s, vec_s, st_s = _ssd_specs(T, rev, True)

    def body(x_ref, b_ref, c_ref, dt_ref, bias_ref, a_ref, st_ref, dy_ref, ddt_in_ref,
             dx_ref, db_ref, dc_ref, ddt_ref, da_ref, dbias_ref, dstate):
        c, g = pl.program_id(0), pl.program_id(1)

        @pl.when(c == 0)
        def _():
            dstate[g] = jnp.zeros((GW, D_STATE), F32)

        @pl.when((c == 0) & (g == 0))
        def _():
            da_ref[...] = jnp.zeros_like(da_ref)
            dbias_ref[...] = jnp.zeros_like(dbias_ref)

        @pl.when(g == 0)
        def _():
            ddt_ref[...] = ddt_in_ref[...]

        q = _ssd_common(dt_ref, bias_ref, a_ref, b_ref, c_ref, g, rev)
        bm, cmat = q["bm"], q["cm"]
        lane = lax.broadcasted_iota(jnp.int32, (1, CHUNK), 1)
        sub = lax.broadcasted_iota(jnp.int32, (CHUNK, 1), 0)
        zero = jnp.zeros((CHUNK, CHUNK), F32)
        dcb, db_acc, dc_acc = zero, zero, zero
        dcs_col, dcs_row, ddt_x, dtot = zero, zero, zero, jnp.zeros((1, CHUNK), F32)
        for hh in range(HG):
            hs = slice(PH * hh, PH * (hh + 1))
            col, row = q["cs"][:, hh:hh + 1], q["csT"][hh:hh + 1, :]
            tot_h = q["tot"][:, hh:hh + 1]
            x = x_ref[:, hs]
            dth = q["dt"][:, hh:hh + 1]
            xdt = x * dth
            xdb = xdt.astype(BF)
            lmat = jnp.where(q["incl"], jnp.exp(col - row), 0.0)
            mmat = q["cb"] * lmat
            prev = st_ref[0, hs, :]
            pb = prev.astype(BF)
            ds_ = dstate[g, hs, :]
            dsb = ds_.astype(BF)
            dyh = dy_ref[:, hs]
            dyb = dyh.astype(BF)
            e = jnp.exp(col)
            w = jnp.exp(tot_h - col)
            etot = jnp.exp(tot_h)
            dprev = lax.dot_general((dyh * e).astype(BF), cmat, TN, preferred_element_type=F32) + ds_ * etot
            cp = lax.dot_general(cmat, pb, NT, preferred_element_type=F32)
            dcs_h = jnp.sum(dyh * cp, axis=1, keepdims=True) * e
            dc_acc = dc_acc + jnp.dot(dyb, pb, preferred_element_type=F32) * e
            dm = lax.dot_general(dyb, xdb, NT, preferred_element_type=F32)
            dxdt = lax.dot_general(mmat.astype(BF), dyb, TN, preferred_element_type=F32)
            qm = dm * mmat
            dcs_h = dcs_h + jnp.sum(qm, axis=1, keepdims=True)
            dcs_row = dcs_row + jnp.where(sub == hh, jnp.sum(qm, axis=0, keepdims=True), 0.0)
            dcb = dcb + dm * lmat
            bds = lax.dot_general(bm, dsb, NT, preferred_element_type=F32) * w
            dxdt = dxdt + bds
            db_acc = db_acc + jnp.dot(xdb, dsb, preferred_element_type=F32) * w
            t = jnp.sum(xdt * bds, axis=1, keepdims=True)
            dtot_h = jnp.sum(t) + jnp.sum(ds_ * prev) * etot
            dcs_h = dcs_h - t
            dcs_col = dcs_col + jnp.where(lane == hh, dcs_h, 0.0)
            dtot = dtot + jnp.where(lane == hh, dtot_h, 0.0)
            ddt_x = ddt_x + jnp.where(lane == hh, jnp.sum(dxdt * x, axis=1, keepdims=True), 0.0)
            dx_ref[:, hs] = dxdt * dth
            dstate[g, hs, :] = dprev
        dcbb = dcb.astype(BF)
        dc_ref[...] = dc_acc + jnp.dot(dcbb, bm, preferred_element_type=F32)
        db_ref[...] = db_acc + lax.dot_general(dcbb, cmat, TN, preferred_element_type=F32)
        dcs = dcs_col - dcs_row.T
        tri_t = jnp.where(q["incl"], 0.0, 1.0).astype(F32) + jnp.where(
            lax.broadcasted_iota(jnp.int32, (CHUNK, CHUNK), 0) == lax.broadcasted_iota(jnp.int32, (CHUNK, CHUNK), 1), 1.0, 0.0)
        dda = jnp.dot(tri_t, dcs, precision=HI, preferred_element_type=F32) + dtot
        ddt = ddt_x + dda * q["a_sel"]
        dpre = ddt * jax.nn.sigmoid(q["pre"])
        dpre_full = lax.dot_general(dpre, q["sel"], NT, precision=HI, preferred_element_type=F32)
        ddt_ref[...] += dpre_full
        dbias_ref[...] += jnp.broadcast_to(jnp.sum(dpre_full, axis=0, keepdims=True), (8, CHUNK))
        dalog_sel = jnp.broadcast_to(jnp.sum(dda * q["dt"], axis=0, keepdims=True) * q["a_sel"], (8, CHUNK))
        da_ref[...] += lax.dot_general(dalog_sel, q["sel"], NT, precision=HI, preferred_element_type=F32)

    bc_out = pl.BlockSpec((CHUNK, D_STATE), lambda c, g: (cm(c), g))
    dt_out = pl.BlockSpec((CHUNK, HP), lambda c, g: (cm(c), 0))
    return pl.pallas_call(
        body, name=name,
        out_shape=(jax.ShapeDtypeStruct((T, D_INNER), F32), jax.ShapeDtypeStruct((T, SSM_GROUPS * D_STATE), F32),
                   jax.ShapeDtypeStruct((T, SSM_GROUPS * D_STATE), F32), jax.ShapeDtypeStruct((T, HP), F32),
                   jax.ShapeDtypeStruct((8, HP), F32), jax.ShapeDtypeStruct((8, HP), F32)),
        grid=(nc, SSM_GROUPS), in_specs=[xs_s, b_s, c_s, dt_s, vec_s, vec_s, st_s, xs_s, dt_out],
        out_specs=(xs_s, bc_out, bc_out, dt_out, vec_s, vec_s),
        scratch_shapes=[pltpu.VMEM((SSM_GROUPS, GW, D_STATE), F32)],
        compiler_params=pltpu.CompilerParams(dimension_semantics=("arbitrary", "arbitrary")),
    )(act, act, act, u, bias8, a8, states, dy, ddt_in)


def _ssm_combine_fwd(y_f, y_b, act, u, dskip, gain):
    T = y_f.shape[0]
    tt = _pick(T, 256)

    def body(yf_ref, yb_ref, x_ref, z_ref, ds_ref, g_ref, y_ref, m_ref):
        y = yf_ref[...] + yb_ref[...] + ds_ref[...] * x_ref[...]
        y2 = y * _silu(z_ref[...])
        r = lax.rsqrt(jnp.mean(y2 * y2, axis=-1, keepdims=True) + EPS)
        y_ref[...] = y
        m_ref[...] = (y2 * r * g_ref[...]).astype(m_ref.dtype)

    blk = pl.BlockSpec((tt, GW), lambda i, g: (i, g))
    vec = pl.BlockSpec((1, GW), lambda i, g: (0, g))
    return pl.pallas_call(
        body, name="ssm_combine_fwd",
        out_shape=(jax.ShapeDtypeStruct((T, D_INNER), F32), jax.ShapeDtypeStruct((T, D_INNER), BF)),
        grid=(T // tt, SSM_GROUPS), in_specs=[blk, blk, blk, blk, vec, vec], out_specs=(blk, blk),
    )(y_f, y_b, act, u, dskip, gain)


def _ssm_combine_bwd(dm, y, act, u, dskip, gain):
    T = y.shape[0]
    tt = _pick(T, 256)

    def body(dm_ref, y_ref, x_ref, z_ref, ds_ref, g_ref, dy_ref, dz_ref, dxs_ref, dg_ref, dsk_ref):
        z = z_ref[...]
        y = y_ref[...]
        x = x_ref[...]
        sz = _silu(z)
        y2 = y * sz
        r = lax.rsqrt(jnp.mean(y2 * y2, axis=-1, keepdims=True) + EPS)
        d = dm_ref[...]
        gd = d * g_ref[...]
        dy2 = r * gd - y2 * (r * r * r) * jnp.mean(gd * y2, axis=-1, keepdims=True)
        dy = dy2 * sz
        dy_ref[...] = dy
        dz_ref[...] = (dy2 * y * _dsilu(z)).astype(dz_ref.dtype)
        dxs_ref[...] = dy * ds_ref[...]

        @pl.when(pl.program_id(1) == 0)
        def _():
            dg_ref[...] = jnp.zeros_like(dg_ref)
            dsk_ref[...] = jnp.zeros_like(dsk_ref)

        dg_ref[...] += jnp.broadcast_to(jnp.sum(d * y2 * r, axis=0, keepdims=True), dg_ref.shape)
        lane_sum = jnp.broadcast_to(jnp.sum(dy * x, axis=0, keepdims=True), (8, GW))
        src = lax.broadcasted_iota(jnp.int32, (GW, HP), 0)
        head = lax.broadcasted_iota(jnp.int32, (GW, HP), 1)
        to_head = jnp.where((src >= PH * head) & (src < PH * (head + 1)), 1.0, 0.0).astype(F32)
        dsk_ref[...] += jnp.dot(lane_sum, to_head, precision=HI, preferred_element_type=F32)

    blk = pl.BlockSpec((tt, GW), lambda g, i: (i, g))
    vec = pl.BlockSpec((1, GW), lambda g, i: (0, g))
    acc = pl.BlockSpec((8, GW), lambda g, i: (0, g))
    return pl.pallas_call(
        body, name="ssm_combine_bwd",
        out_shape=(jax.ShapeDtypeStruct((T, D_INNER), F32), jax.ShapeDtypeStruct((T, D_INNER), BF),
                   jax.ShapeDtypeStruct((T, D_INNER), F32), jax.ShapeDtypeStruct((8, D_INNER), F32),
                   jax.ShapeDtypeStruct((8, SSM_GROUPS * HP), F32)),
        grid=(SSM_GROUPS, T // tt), in_specs=[blk, blk, blk, blk, vec, vec],
        out_specs=(blk, blk, blk, acc, pl.BlockSpec((8, HP), lambda g, i: (0, g))),
        compiler_params=pltpu.CompilerParams(dimension_semantics=("parallel", "arbitrary")),
    )(dm, y, act, u, dskip, gain)


def _loss_head(y, target):
    T, D = y.shape
    tt = _pick(T, 512)

    def body(y_ref, t_ref, dy_ref, dyb_ref, l_ref):
        e = y_ref[...] - t_ref[...]
        dy_ref[...] = e * (1.0 / D)
        dyb_ref[...] = (e * (1.0 / D)).astype(dyb_ref.dtype)

        @pl.when(pl.program_id(0) == 0)
        def _():
            l_ref[...] = jnp.zeros_like(l_ref)

        l_ref[...] += jnp.sum(e * e) * (0.5 / D)

    blk = pl.BlockSpec((tt, D), lambda i: (i, 0))
    return pl.pallas_call(
        body, name="loss_head",
        out_shape=(jax.ShapeDtypeStruct((T, D), F32), jax.ShapeDtypeStruct((T, D), BF), jax.ShapeDtypeStruct((8, 128), F32)),
        grid=(T // tt,), in_specs=[blk, blk], out_specs=(blk, blk, pl.BlockSpec((8, 128), lambda i: (0, 0))),
        compiler_params=pltpu.CompilerParams(dimension_semantics=("arbitrary",)),
    )(y, target)


def _adamw(w, g, m, v, *, name):
    R, C = w.shape
    cap = max(8, (1 << 18) // C)
    tr = R
    if R % 8 == 0:
        tr = 8
        for cand in range(8, min(R, cap) + 1, 8):
            if R % cand == 0:
                tr = cand

    def body(w_ref, g_ref, m_ref, v_ref, d_ref, nm_ref, nv_ref):
        gg = g_ref[...]
        nm = ADAM_B1 * m_ref[...] + (1.0 - ADAM_B1) * gg
        nv = ADAM_B2 * v_ref[...] + (1.0 - ADAM_B2) * jnp.square(gg)
        m_hat = nm / (1.0 - ADAM_B1 ** ADAM_STEP)
        v_hat = nv / (1.0 - ADAM_B2 ** ADAM_STEP)
        d_ref[...] = -ADAM_LR * (m_hat / (jnp.sqrt(v_hat) + ADAM_EPS) + ADAM_WD * w_ref[...])
        nm_ref[...] = nm
        nv_ref[...] = nv

    blk = pl.BlockSpec((tr, C), lambda i: (i, 0))
    return pl.pallas_call(
        body, name=name, out_shape=(jax.ShapeDtypeStruct((R, C), F32),) * 3, grid=(R // tr,),
        in_specs=[blk] * 4, out_specs=(blk,) * 3,
    )(w, g, m, v)


ANY = pl.BlockSpec(memory_space=pl.ANY)


def _chip_peers():
    x, y, c = lax.axis_index("x"), lax.axis_index("y"), lax.axis_index("c")
    return x, y, c, [(1 - x, y), (x, 1 - y), (1 - x, 1 - y)]


def _half_rows(c, rh):
    return pl.ds(pl.multiple_of(c * rh, 16), rh)


def _my_chip():
    return 2 * lax.axis_index("x") + lax.axis_index("y")


def _gather_chips(wb, wf):
    rh = wb.shape[0] // 2

    def body(wb_ref, wf_ref, ob_ref, of_ref, send_sems, recv_sems):
        x, y, c, peers = _chip_peers()
        me = 2 * x + y
        half, other = _half_rows(c, rh), _half_rows(1 - c, rh)

        def chip_copy(k, slot):
            px, py = peers[k]
            return pltpu.make_async_remote_copy(
                src_ref=wb_ref.at[half], dst_ref=ob_ref.at[slot, half], send_sem=send_sems.at[k], recv_sem=recv_sems.at[k],
                device_id=(px, py, c), device_id_type=MESH)

        def passed_on(k, slot, rows):
            return pltpu.make_async_remote_copy(
                src_ref=ob_ref.at[slot, rows], dst_ref=ob_ref.at[slot, rows], send_sem=send_sems.at[3 + k],
                recv_sem=recv_sems.at[3 + k], device_id=(x, y, 1 - c), device_id_type=MESH)

        def small_copy(k, slot):
            px, py = peers[k]
            return pltpu.make_async_remote_copy(
                src_ref=wf_ref, dst_ref=of_ref.at[slot], send_sem=send_sems.at[6 + k], recv_sem=recv_sems.at[6 + k],
                device_id=(px, py, c), device_id_type=MESH)

        sends = [chip_copy(k, me) for k in range(3)] + [small_copy(k, me) for k in range(3)]
        for cp in sends:
            cp.start()
        chip_of = [2 * px + py for px, py in peers]
        for k in range(3):
            chip_copy(k, chip_of[k]).wait_recv()
            cp = passed_on(k, chip_of[k], half)
            cp.start()
            sends.append(cp)
        for k in range(3):
            passed_on(k, chip_of[k], other).wait_recv()
            small_copy(k, chip_of[k]).wait_recv()
        for cp in sends:
            cp.wait_send()

    ob, of = pl.pallas_call(
        body, name="gather_weights",
        out_shape=(jax.ShapeDtypeStruct((4,) + wb.shape, wb.dtype), jax.ShapeDtypeStruct((4,) + wf.shape, wf.dtype)),
        in_specs=[ANY, ANY], out_specs=(ANY, ANY),
        scratch_shapes=[pltpu.SemaphoreType.DMA((9,)), pltpu.SemaphoreType.DMA((9,))],
    )(wb, wf)
    me = _my_chip()
    return lax.dynamic_update_slice(ob, wb[None], (me, 0, 0)), lax.dynamic_update_slice(of, wf[None], (me, 0, 0))


def _halves_to_sibling(gp):
    rh = gp.shape[1] // 2

    def body(gp_ref, o_ref, send_sem, recv_sem):
        x, y, c = lax.axis_index("x"), lax.axis_index("y"), lax.axis_index("c")
        cp = pltpu.make_async_remote_copy(src_ref=gp_ref.at[:, _half_rows(1 - c, rh), :], dst_ref=o_ref, send_sem=send_sem,
                                          recv_sem=recv_sem, device_id=(x, y, 1 - c), device_id_type=MESH)
        cp.start()
        cp.wait()

    return pl.pallas_call(
        body, name="halves_to_sibling", out_shape=jax.ShapeDtypeStruct((gp.shape[0], rh, gp.shape[2]), gp.dtype),
        in_specs=[ANY], out_specs=ANY, scratch_shapes=[pltpu.SemaphoreType.DMA, pltpu.SemaphoreType.DMA],
    )(gp)


def _row_tile(rows, cap=1024):
    tr = 16
    for cand in range(16, cap + 1, 16):
        if rows % cand == 0:
            tr = cand
    return tr


def _add_halves(gp, sib, core):
    n, rh, C = sib.shape
    tr = _row_tile(rh)
    nt = rh // tr

    def body(c_ref, g_ref, s_ref, o_ref):
        o_ref[...] = (g_ref[...].astype(F32) + s_ref[...].astype(F32)).astype(o_ref.dtype)

    blk = pl.BlockSpec((1, tr, C), lambda j, i, c: (j, i, 0))
    return pl.pallas_call(
        body, name="add_halves", out_shape=jax.ShapeDtypeStruct(sib.shape, sib.dtype),
        grid_spec=pltpu.PrefetchScalarGridSpec(
            num_scalar_prefetch=1, grid=(n, nt),
            in_specs=[pl.BlockSpec((1, tr, C), lambda j, i, c: (j, c[0] * nt + i, 0)), blk], out_specs=blk),
    )(core, gp, sib)


def _join_halves(mine):
    rh = mine.shape[0]

    def body(m_ref, o_ref, send_sem, recv_sem):
        x, y, c = lax.axis_index("x"), lax.axis_index("y"), lax.axis_index("c")
        half, other = _half_rows(c, rh), _half_rows(1 - c, rh)

        def copy(rows):
            return pltpu.make_async_remote_copy(src_ref=m_ref, dst_ref=o_ref.at[rows], send_sem=send_sem, recv_sem=recv_sem,
                                                device_id=(x, y, 1 - c), device_id_type=MESH)

        send = copy(half)
        send.start()
        copy(other).wait_recv()
        send.wait_send()

    out = pl.pallas_call(
        body, name="join_halves", out_shape=jax.ShapeDtypeStruct((2 * rh, mine.shape[1]), mine.dtype),
        in_specs=[ANY], out_specs=ANY, scratch_shapes=[pltpu.SemaphoreType.DMA, pltpu.SemaphoreType.DMA],
    )(mine)
    return lax.dynamic_update_slice(out, mine, (lax.axis_index("c") * rh, 0))


def _exchange_chips(gp):
    def body(gp_ref, out_ref, send_sems, recv_sems):
        x, y, c, peers = _chip_peers()
        me = 2 * x + y

        def copies(sending):
            out = []
            for k, (px, py) in enumerate(peers):
                p = 2 * px + py
                out.append(pltpu.make_async_remote_copy(
                    src_ref=gp_ref.at[p], dst_ref=out_ref.at[me if sending else p],
                    send_sem=send_sems.at[k], recv_sem=recv_sems.at[k], device_id=(px, py, c), device_id_type=MESH))
            return out

        sends = copies(True)
        for cp in sends:
            cp.start()
        for cp in copies(False):
            cp.wait_recv()
        for cp in sends:
            cp.wait_send()

    out = pl.pallas_call(
        body, name="exchange_grads", out_shape=jax.ShapeDtypeStruct(gp.shape, gp.dtype),
        in_specs=[ANY], out_specs=ANY,
        scratch_shapes=[pltpu.SemaphoreType.DMA((3,)), pltpu.SemaphoreType.DMA((3,))],
    )(gp)
    me = _my_chip()
    return lax.dynamic_update_slice(out, lax.dynamic_slice_in_dim(gp, me, 1, axis=0), (me, 0, 0))


def _sum_slots(r4):
    _, R, C = r4.shape
    tr = _row_tile(R)

    def body(r_ref, o_ref):
        acc = r_ref[0].astype(F32)
        for s in range(1, 4):
            acc = acc + r_ref[s].astype(F32)
        o_ref[...] = acc

    return pl.pallas_call(
        body, name="sum_slots", out_shape=jax.ShapeDtypeStruct((R, C), F32), grid=(R // tr,),
        in_specs=[pl.BlockSpec((4, tr, C), lambda i: (0, i, 0))], out_specs=pl.BlockSpec((tr, C), lambda i: (i, 0)),
    )(r4)


N_DEV = 8


def _allreduce_small(p):
    rs = p.shape[0]

    def body(x_ref, sum_ref, all_ref, send_sems, recv_sems, local_sem):
        x, y, c = lax.axis_index("x"), lax.axis_index("y"), lax.axis_index("c")
        me, sibling = (x, y, c), (x, y, 1 - c)
        chips = [(1 - x, y), (x, 1 - y), (1 - x, 1 - y)]

        def rows(px, py, pc):
            return all_ref.at[pl.ds((4 * px + 2 * py + pc) * rs, rs), :]

        def copy(k, block, to, src=None):
            return pltpu.make_async_remote_copy(
                src_ref=rows(*block) if src is None else src, dst_ref=rows(*block),
                send_sem=send_sems.at[k], recv_sem=recv_sems.at[k], device_id=to, device_id_type=MESH)

        mine = pltpu.make_async_copy(x_ref, rows(*me), local_sem)
        mine.start()
        first = [copy(0, me, sibling, src=x_ref)]
        first += [copy(1 + j, me, (*chip, c), src=x_ref) for j, chip in enumerate(chips)]
        for cp in first:
            cp.start()
        passed = [copy(4 + j, (*chip, c), sibling) for j, chip in enumerate(chips)]
        for j, chip in enumerate(chips):
            copy(1 + j, (*chip, c), me).wait_recv()
            passed[j].start()
        copy(0, sibling, me).wait_recv()
        for j, chip in enumerate(chips):
            copy(4 + j, (*chip, 1 - c), me).wait_recv()
        for cp in first + passed:
            cp.wait_send()
        mine.wait()
        acc = all_ref[0:rs, :]
        for d in range(1, N_DEV):
            acc = acc + all_ref[d * rs:(d + 1) * rs, :]
        sum_ref[...] = acc

    vmem = pl.BlockSpec(memory_space=pltpu.VMEM)
    return pl.pallas_call(
        body, name="allreduce_small", out_shape=jax.ShapeDtypeStruct((rs, 128), F32),
        in_specs=[vmem], out_specs=vmem,
        scratch_shapes=[pltpu.VMEM((N_DEV * rs, 128), F32), pltpu.SemaphoreType.DMA((7,)), pltpu.SemaphoreType.DMA((7,)),
                        pltpu.SemaphoreType.DMA],
    )(p)


WEIGHTS = ('ffn1_norm', 'ffn1_w_gate', 'ffn1_w_up', 'ffn1_w_down', 'mix_norm', 'w_in', 'q_a_norm', 'w_q_b',
           'kv_a_norm', 'w_kv_b', 'q_head_norm', 'k_head_norm', 'conv_w', 'conv_b', 'a_log_fwd', 'a_log_bwd',
           'dt_bias_fwd', 'dt_bias_bwd', 'd_skip', 'ssm_norm', 'w_attn_branch', 'w_ssm_branch', 'w_out',
           'ffn2_norm', 'ffn2_w_gate', 'ffn2_w_up', 'ffn2_w_down')
PACKED = (('ffn1_w_gate', (D_MODEL, D_FF), 1), ('ffn1_w_up', (D_MODEL, D_FF), 1), ('ffn1_w_down', (D_FF, D_MODEL), 0),
          ('w_in', (D_MODEL, sum(IN_SPLITS)), 1), ('w_q_b', (Q_LORA, N_HEADS * QK_HEAD), 1),
          ('w_kv_b', (KV_LORA, N_HEADS * (QK_NOPE + V_HEAD)), 1),
          ('w_attn_branch', (N_HEADS * V_HEAD, D_MODEL), 0), ('w_ssm_branch', (D_INNER, D_MODEL), 0),
          ('w_out', (D_MODEL, D_MODEL), 0),
          ('ffn2_w_gate', (D_MODEL, D_FF), 1), ('ffn2_w_up', (D_MODEL, D_FF), 1), ('ffn2_w_down', (D_FF, D_MODEL), 0))
PACK_W = 1024
N_CHIPS = 4
SMALL = (('ffn1_norm', 1024), ('mix_norm', 1024), ('q_a_norm', 384), ('kv_a_norm', 256), ('q_head_norm', 96),
         ('k_head_norm', 96), ('conv_b', 3072), ('a_log_fwd', 32), ('a_log_bwd', 32), ('dt_bias_fwd', 32),
         ('dt_bias_bwd', 32), ('d_skip', 32), ('ssm_norm', 2048), ('ffn2_norm', 1024),
         ('conv_w', CONV_WIDTH * XBC_DIM), ('loss', 1))


def _shard_shape(shape, axis):
    return tuple(s // N_CHIPS if a == axis else s for a, s in enumerate(shape))


def _pack_rows():
    rows = sum(math.prod(_shard_shape(shape, axis)) // PACK_W for _, shape, axis in PACKED)
    return -(-rows // 32) * 32


def _pack(shards):
    parts = [shards[name].reshape(-1, PACK_W) for name, _, _ in PACKED]
    rows = sum(p.shape[0] for p in parts)
    parts.append(jnp.zeros((_pack_rows() - rows, PACK_W), parts[0].dtype))
    return jnp.concatenate(parts, axis=0)


def _unpack(packed):
    out, r = {}, 0
    for name, shape, axis in PACKED:
        sh = _shard_shape(shape, axis)
        n = math.prod(sh) // PACK_W
        out[name] = packed[r:r + n].reshape(sh)
        r += n
    return out


def _pack_small(vals):
    parts = []
    for name, n in SMALL:
        pad = -(-n // 128) * 128 - n
        parts.append(jnp.pad(vals[name].reshape(-1).astype(F32), (0, pad)).reshape(-1, 128))
    rows = sum(p.shape[0] for p in parts)
    parts.append(jnp.zeros((-(-rows // 8) * 8 - rows, 128), F32))
    return jnp.concatenate(parts, axis=0)


def _unpack_small(packed):
    out, r = {}, 0
    for name, n in SMALL:
        k = -(-n // 128)
        out[name] = packed[r:r + k].reshape(-1)[:n]
        r += k
    return out


def _pad_heads(w, axis, per_head, lo, hi):
    shape = w.shape
    w = w.reshape(shape[:axis] + (N_HEADS, per_head) + shape[axis + 1:])
    w = lax.slice_in_dim(w, lo, hi, axis=axis + 1)
    pad = [(0, 0)] * w.ndim
    pad[axis + 1] = (0, HP - (hi - lo))
    w = jnp.pad(w, pad)
    return w.reshape(shape[:axis] + (N_HEADS * HP,) + shape[axis + 1:])


def _unpad_heads(w, axis, keep):
    shape = w.shape
    w = w.reshape(shape[:axis] + (N_HEADS, HP) + shape[axis + 1:])
    return lax.slice_in_dim(w, 0, keep, axis=axis + 1)


def _split_w_in(w):
    o = [0]
    for s in IN_SPLITS:
        o.append(o[-1] + s)
    return [w[:, o[i]:o[i + 1]] for i in range(len(IN_SPLITS))]


def _pad_w_in(w):
    cq, ckv, kpe, z, xbc, dtf, dtb, ga, gb = _split_w_in(w)
    kpe_pad = jnp.pad(kpe, ((0, 0), (QK_NOPE, HP - QK_HEAD)))
    dt_pad = jnp.pad(jnp.concatenate([dtf, dtb], axis=1), ((0, 0), (0, HP - 2 * SSM_HEADS)))
    return jnp.concatenate([z, ga, gb, xbc, cq, ckv, kpe_pad, dt_pad], axis=1)


def _unpad_w_in(g):
    z, ga, gb, xbc = g[:, U_Z:U_GA], g[:, U_GA:U_GB], g[:, U_GB:U_XBC], g[:, U_XBC:U_SMALL]
    s = g[:, U_SMALL:]
    cq, ckv = s[:, S_CQ:S_CKV], s[:, S_CKV:S_KPE]
    kpe = s[:, S_KPE + QK_NOPE:S_KPE + QK_HEAD]
    dtf, dtb = s[:, S_DT:S_DT + SSM_HEADS], s[:, S_DT + SSM_HEADS:S_DT + 2 * SSM_HEADS]
    return jnp.concatenate([cq, ckv, kpe, z, xbc, dtf, dtb, ga, gb], axis=1)


def _lanes128(parts):
    row = jnp.concatenate([p.reshape(-1) for p in parts])
    return jnp.pad(row, (0, HP - row.shape[0])).reshape(1, HP)


FF_TILE = D_FF // 2


def _ffn_fwd(x, g, wg, wu, wd, tag):
    h = _rms_fwd(x, g, name=tag + "_norm")
    gate, up, act = _mm([h], [wg, wu], name=tag + "_up", out_dtypes=(F32, F32, BF), tm=512, tn=FF_TILE,
                        epilogue=lambda a, b: (a, b, _silu(a) * b))
    out = _mm([act], [wd], name=tag + "_down", extras=[x], epilogue=lambda acc, r: (r + 0.5 * acc,))
    return out, (h, gate, up, act)


def _ffn_bwd(dout, dout_bf, x, g, wg, wu, wd, saved, tag):
    h, gate, up, act = saved
    dgate, dup = _mm([dout_bf], [wd], name=tag + "_down_dx", tb=True, extras=[gate, up], out_dtypes=(BF, BF),
                     tm=512, tn=FF_TILE, epilogue=lambda acc, a, b: (0.5 * acc * b * _dsilu(a), 0.5 * acc * _silu(a)))
    dwd = _mm([act], [dout_bf], name=tag + "_down_dw", ta=True, tm=FF_TILE, tk=1024, epilogue=lambda acc: (0.5 * acc,))
    dwg, dwu = _mm([h], [dgate, dup], name=tag + "_up_dw", ta=True, out_dtypes=(F32, F32), tm=512, tn=FF_TILE, tk=1024)
    dh = _mm([dgate, dup], [wg, wu], name=tag + "_up_dx", tb=True)
    dx, dx_bf, dg = _rms_bwd(dh, x, g, name=tag + "_norm_bwd", add=dout, out_dtypes=(F32, BF))
    return dx, dx_bf, dg, dwg, dwu, dwd


KPE_BLK = (U_SMALL + S_KPE) // HP
SMALL_BLK = U_SMALL // SMALL_W


def _local_step(x, pos_col, target, W, P):
    T = x.shape[0]
    sig = jax.nn.sigmoid
    x1, ffn1 = _ffn_fwd(x, P["ffn1_norm"], W["wg1"], W["wu1"], W["wd1"], "ffn1")
    h = _rms_fwd(x1, P["mix_norm"], name="mix_norm")
    u = _mm([h], [W["w_in"]], name="in_proj", tn=1152)
    cqn = _rms_fwd(u, P["q_a_norm"], name="q_a_norm", blk_w=SMALL_W, blk_idx=SMALL_BLK, off=S_CQ, width=Q_LORA)
    ckvn = _rms_fwd(u, P["kv_a_norm"], name="kv_a_norm", blk_w=SMALL_W, blk_idx=SMALL_BLK, off=S_CKV, width=KV_LORA)
    q_raw = _mm([cqn], [W["wq"]], name="q_proj")
    k_raw, v = _mm([ckvn], [W["wk"], W["wv"]], name="kv_proj", out_dtypes=(F32, BF))
    rc, rs = _rope_tables(pos_col, P["freq"])
    q = _qk_prep_fwd(q_raw, None, P["q_head_norm"], rc, rs, name="q_prep", out_scale=Q_SCALE)
    k = _qk_prep_fwd(k_raw, u, P["k_head_norm"], rc, rs, name="k_prep", kpe_blk=KPE_BLK)
    o, lse = _attn_fwd(q, k, v)
    pre, act = _conv_fwd(u, P["conv_w8"], P["conv_b"])
    y_f, st_f = _ssd_fwd(act, u, P["dt_bias8"], P["a_log8"], rev=False, name="ssd_fwd_f")
    y_b, st_b = _ssd_fwd(act, u, P["dt_bias8"], P["a_log8"], rev=True, name="ssd_fwd_b")
    ysum, m = _ssm_combine_fwd(y_f, y_b, act, u, P["d_skip_lanes"], P["ssm_norm"])
    ab = _mm([o], [W["pa"]], name="attn_branch")
    mb, merged = _mm([m], [W["pb"]], name="ssm_branch", extras=[ab, u, u], extra_offs=(0, U_GA, U_GB), out_dtypes=(F32, BF),
                     epilogue=lambda acc, a, ga, gb: (acc, sig(ga) * a + sig(gb) * acc))
    x2 = _mm([merged], [W["wo"]], name="out_proj", extras=[x1], epilogue=lambda acc, r: (r + acc,))
    y, ffn2 = _ffn_fwd(x2, P["ffn2_norm"], W["wg2"], W["wu2"], W["wd2"], "ffn2")
    dy, dy_bf, loss = _loss_head(y, target)
    dx2, dx2_bf, dg_ffn2, dwg2, dwu2, dwd2 = _ffn_bwd(dy, dy_bf, x2, P["ffn2_norm"], W["wg2"], W["wu2"], W["wd2"], ffn2,
                                                      "ffn2")

    def gate_bwd(dmrg, a, b, ga, gb):
        sa, sb = sig(ga), sig(gb)
        return dmrg * sa, dmrg * sb, dmrg * a * sa * (1.0 - sa), dmrg * b * sb * (1.0 - sb)

    dab, dmb, dga, dgb = _mm([dx2_bf], [W["wo"]], name="out_proj_dx", tb=True, extras=[ab, mb, u, u],
                             extra_offs=(0, 0, U_GA, U_GB), out_dtypes=(BF,) * 4, epilogue=gate_bwd)
    dwo = _mm([merged], [dx2_bf], name="out_proj_dw", ta=True)
    dpa = _mm([o], [dab], name="attn_branch_dw", ta=True)
    do = _mm([dab], [W["pa"]], name="attn_branch_dx", tb=True)
    dpb = _mm([m], [dmb], name="ssm_branch_dw", ta=True)
    dm = _mm([dmb], [W["pb"]], name="ssm_branch_dx", tb=True)
    dyssd, dz, dxs_skip, dg_ssm, dskip = _ssm_combine_bwd(dm, ysum, act, u, P["d_skip_lanes"], P["ssm_norm"])
    dxs_f, db_f, dc_f, ddt, dalog_f, dbias_f = _ssd_bwd(act, u, P["dt_bias8"], P["a_log8"], st_f, dyssd,
                                                        jnp.zeros((T, HP), F32), rev=False, name="ssd_bwd_f")
    dxs_b, db_b, dc_b, ddt, dalog_b, dbias_b = _ssd_bwd(act, u, P["dt_bias8"], P["a_log8"], st_b, dyssd, ddt,
                                                        rev=True, name="ssd_bwd_b")
    dxbc, dconv = [], []
    for tag, col0, parts in (("x", 0, [dxs_f, dxs_b, dxs_skip]), ("b", D_INNER, [db_f, db_b]),
                             ("c", D_INNER + SSM_GROUPS * D_STATE, [dc_f, dc_b])):
        dpre = _conv_dpre(parts, pre, col0, name="conv_dpre_" + tag)
        dxp, dwp = _conv_bwd(dpre, u, P["conv_w8"], col0, name="conv_bwd_" + tag)
        dxbc.append(dxp)
        dconv.append(dwp)
    dconv = jnp.concatenate(dconv, axis=1)
    dq, dk, dv = _attn_bwd(q, k, v, do, o, lse)
    dq_raw, dg_qh = _qk_prep_bwd(dq, q_raw, None, P["q_head_norm"], rc, rs, name="q_prep_bwd", in_scale=ATTN_SCALE)
    dk_raw, dg_kh, dkpe = _qk_prep_bwd(dk, k_raw, u, P["k_head_norm"], rc, rs, name="k_prep_bwd", kpe_blk=KPE_BLK,
                                       in_scale=1.0 / LOG2E)
    dwq = _mm([cqn], [dq_raw], name="q_proj_dw", ta=True)
    dcqn = _mm([dq_raw], [W["wq"]], name="q_proj_dx", tb=True)
    dwk, dwv = _mm([ckvn], [dk_raw, dv], name="kv_proj_dw", ta=True, out_dtypes=(F32, F32))
    dckvn = _mm([dk_raw, dv], [W["wk"], W["wv"]], name="kv_proj_dx", tb=True)
    dcq, dg_qa = _rms_bwd(dcqn, u, P["q_a_norm"], name="q_a_norm_bwd", blk_w=SMALL_W, blk_idx=SMALL_BLK, off=S_CQ,
                          width=Q_LORA, out_dtypes=(BF,))
    dckv, dg_kva = _rms_bwd(dckvn, u, P["kv_a_norm"], name="kv_a_norm_bwd", blk_w=SMALL_W, blk_idx=SMALL_BLK,
                            off=S_CKV, width=KV_LORA, out_dtypes=(BF,))
    du = jnp.concatenate([dz, dga, dgb] + dxbc + [dcq, dckv, dkpe.astype(BF), ddt.astype(BF)], axis=1)
    dw_in = _mm([h], [du], name="in_proj_dw", ta=True, tn=1152)
    dh = _mm([du], [W["w_in"]], name="in_proj_dx", tb=True)
    dx1, dx1_bf, dg_mix = _rms_bwd(dh, x1, P["mix_norm"], name="mix_norm_bwd", add=dx2, out_dtypes=(F32, BF))
    dx, _, dg_ffn1, dwg1, dwu1, dwd1 = _ffn_bwd(dx1, dx1_bf, x, P["ffn1_norm"], W["wg1"], W["wu1"], W["wd1"], ffn1, "ffn1")
    dW = dict(wg1=dwg1, wu1=dwu1, wd1=dwd1, w_in=dw_in, wq=dwq, wk=dwk, wv=dwv, pa=dpa, pb=dpb, wo=dwo,
              wg2=dwg2, wu2=dwu2, wd2=dwd2)
    dP = dict(ffn1_norm=dg_ffn1[0], mix_norm=dg_mix[0], q_a_norm=dg_qa[0], kv_a_norm=dg_kva[0],
              q_head_norm=dg_qh[0, :QK_HEAD], k_head_norm=dg_kh[0, :QK_HEAD], conv_b=dconv[CONV_WIDTH],
              a_log_fwd=dalog_f[0, :SSM_HEADS], a_log_bwd=dalog_b[0, SSM_HEADS:2 * SSM_HEADS],
              dt_bias_fwd=dbias_f[0, :SSM_HEADS], dt_bias_bwd=dbias_b[0, SSM_HEADS:2 * SSM_HEADS],
              d_skip=dskip[0].reshape(SSM_GROUPS, HP)[:, :HG], ssm_norm=dg_ssm[0], ffn2_norm=dg_ffn2[0],
              conv_w=dconv[:CONV_WIDTH], loss=loss[0, 0])
    return dx, dW, dP


def _prepare(w, conv_w_full):
    kvb = w["w_kv_b"]
    W = dict(wg1=w["ffn1_w_gate"], wu1=w["ffn1_w_up"], wd1=w["ffn1_w_down"], w_in=_pad_w_in(w["w_in"]),
             wq=_pad_heads(w["w_q_b"], 1, QK_HEAD, 0, QK_HEAD),
             wk=_pad_heads(kvb, 1, QK_NOPE + V_HEAD, 0, QK_NOPE),
             wv=_pad_heads(kvb, 1, QK_NOPE + V_HEAD, QK_NOPE, QK_NOPE + V_HEAD),
             pa=_pad_heads(w["w_attn_branch"], 0, V_HEAD, 0, V_HEAD), pb=w["w_ssm_branch"], wo=w["w_out"],
             wg2=w["ffn2_w_gate"], wu2=w["ffn2_w_up"], wd2=w["ffn2_w_down"])
    inv_freq = [1.0 / (ROPE_BASE ** (j / QK_ROPE)) for j in range(0, QK_ROPE, 2)]
    freq = [0.0] * QK_NOPE + inv_freq + inv_freq + [0.0] * (HP - QK_HEAD)
    P = {n: w[n] for n in ("ffn1_norm", "mix_norm", "q_a_norm", "kv_a_norm", "ssm_norm", "ffn2_norm", "conv_b")}
    P.update(q_head_norm=_lanes128([w["q_head_norm"]]), k_head_norm=_lanes128([w["k_head_norm"]]),
             conv_w8=jnp.pad(conv_w_full, ((0, 8 - CONV_WIDTH), (0, 0))),
             dt_bias8=jnp.broadcast_to(_lanes128([w["dt_bias_fwd"], w["dt_bias_bwd"]]), (8, HP)),
             a_log8=jnp.broadcast_to(_lanes128([w["a_log_fwd"], w["a_log_bwd"]]), (8, HP)),
             d_skip_lanes=jnp.repeat(w["d_skip"].reshape(-1), PH).reshape(1, D_INNER),
             freq=jnp.asarray(freq, F32).reshape(1, HP))
    return W, P


def _unprepare(dW):
    dkvb = jnp.concatenate([_unpad_heads(dW["wk"], 1, QK_NOPE), _unpad_heads(dW["wv"], 1, V_HEAD)], axis=2)
    return dict(ffn1_w_gate=dW["wg1"], ffn1_w_up=dW["wu1"], ffn1_w_down=dW["wd1"], w_in=_unpad_w_in(dW["w_in"]),
                w_q_b=_unpad_heads(dW["wq"], 1, QK_HEAD).reshape(Q_LORA, N_HEADS * QK_HEAD),
                w_kv_b=dkvb.reshape(KV_LORA, N_HEADS * (QK_NOPE + V_HEAD)),
                w_attn_branch=_unpad_heads(dW["pa"], 0, V_HEAD).reshape(N_HEADS * V_HEAD, D_MODEL),
                w_ssm_branch=dW["pb"], w_out=dW["wo"],
                ffn2_w_gate=dW["wg2"], ffn2_w_up=dW["wu2"], ffn2_w_down=dW["wd2"])


def kernel(x, positions, ffn1_norm, ffn1_w_gate, ffn1_w_up, ffn1_w_down, mix_norm, w_in, q_a_norm, w_q_b, kv_a_norm, w_kv_b, q_head_norm, k_head_norm, conv_w, conv_b, a_log_fwd, a_log_bwd, dt_bias_fwd, dt_bias_bwd, d_skip, ssm_norm, w_attn_branch, w_ssm_branch, w_out, ffn2_norm, ffn2_w_gate, ffn2_w_up, ffn2_w_down, loss_target, m_ffn1_norm, m_ffn1_w_gate, m_ffn1_w_up, m_ffn1_w_down, m_mix_norm, m_w_in, m_q_a_norm, m_w_q_b, m_kv_a_norm, m_w_kv_b, m_q_head_norm, m_k_head_norm, m_conv_w, m_conv_b, m_a_log_fwd, m_a_log_bwd, m_dt_bias_fwd, m_dt_bias_bwd, m_d_skip, m_ssm_norm, m_w_attn_branch, m_w_ssm_branch, m_w_out, m_ffn2_norm, m_ffn2_w_gate, m_ffn2_w_up, m_ffn2_w_down, v_ffn1_norm, v_ffn1_w_gate, v_ffn1_w_up, v_ffn1_w_down, v_mix_norm, v_w_in, v_q_a_norm, v_w_q_b, v_kv_a_norm, v_w_kv_b, v_q_head_norm, v_k_head_norm, v_conv_w, v_conv_b, v_a_log_fwd, v_a_log_bwd, v_dt_bias_fwd, v_dt_bias_bwd, v_d_skip, v_ssm_norm, v_w_attn_branch, v_w_ssm_branch, v_w_out, v_ffn2_norm, v_ffn2_w_gate, v_ffn2_w_up, v_ffn2_w_down):
    given = dict(locals())
    T = x.shape[1]
    packed_names = [name for name, _, _ in PACKED]

    def two_d(a):
        return a.reshape(a.shape[1], -1) if a.ndim > 2 else a

    w_loc = {n: two_d(given[n]) for n in WEIGHTS}
    wb = _pack({n: w_loc[n].astype(BF) for n in packed_names})
    wf = jnp.pad(w_loc["conv_w"], ((0, 8 - CONV_WIDTH), (0, 0)))
    gb, gf = _gather_chips(wb, wf)
    per_chip = [_unpack(gb[j]) for j in range(N_CHIPS)]
    full = {n: jnp.concatenate([per_chip[j][n] for j in range(N_CHIPS)], axis=axis) for n, _, axis in PACKED}
    conv_w_full = jnp.concatenate([gf[j, :CONV_WIDTH] for j in range(N_CHIPS)], axis=1)
    full.update({n: w_loc[n] for n in WEIGHTS if n not in full and n != "conv_w"})
    W, P = _prepare(full, conv_w_full)
    dx, dW, dP = _local_step(x.reshape(T, D_MODEL), positions.reshape(T, 1).astype(F32), loss_target.reshape(T, D_MODEL), W, P)
    g_full = _unprepare(dW)
    slots = []
    for j in range(N_CHIPS):
        shards = {}
        for n, shape, axis in PACKED:
            size = shape[axis] // N_CHIPS
            shards[n] = lax.slice_in_dim(g_full[n], j * size, (j + 1) * size, axis=axis).astype(BF)
        slots.append(_pack(shards))
    gp = jnp.stack(slots)
    core = lax.axis_index("c").astype(jnp.int32).reshape(1)
    both_cores = _add_halves(gp, _halves_to_sibling(gp), core)
    g_packed = _unpack(_join_halves(_sum_slots(_exchange_chips(both_cores))))
    small = _unpack_small(_allreduce_small(_pack_small(dP)))
    chip = 2 * lax.axis_index("x") + lax.axis_index("y")
    grads = dict(g_packed)
    grads.update({n: small[n].reshape(1, -1) for n, _ in SMALL if n not in ("conv_w", "loss")})
    grads["conv_w"] = lax.dynamic_slice_in_dim(small["conv_w"].reshape(CONV_WIDTH, XBC_DIM), chip * (XBC_DIM // N_CHIPS),
                                               XBC_DIM // N_CHIPS, axis=1)
    out_g, out_d, out_m, out_v = [], [], [], []
    for n in WEIGHTS:
        shape = given[n].shape
        delta, new_m, new_v = _adamw(w_loc[n], grads[n], two_d(given["m_" + n]), two_d(given["v_" + n]), name="adamw_" + n)
        out_g.append(grads[n].reshape(shape))
        out_d.append(delta.reshape(shape))
        out_m.append(new_m.reshape(shape))
        out_v.append(new_v.reshape(shape))
    return (small["loss"].reshape(()), dx.reshape(x.shape), *out_g, *out_d, *out_m, *out_v)
```

```python
import functools
import math

import jax
import jax.numpy as jnp
from jax import lax
from jax.experimental import pallas as pl
from jax.experimental.pallas import tpu as pltpu

BF = jnp.bfloat16
F32 = jnp.float32
HI = lax.Precision.HIGHEST
MESH = pl.DeviceIdType.MESH

D_MODEL = 1024
D_FF = 2816
EPS = 1e-6
N_HEADS = 16
QK_NOPE = 64
QK_ROPE = 32
QK_HEAD = 96
V_HEAD = 64
Q_LORA = 384
KV_LORA = 256
ROPE_BASE = 10000.0
D_INNER = 2048
SSM_HEADS = 32
SSM_GROUPS = 4
D_STATE = 128
CONV_WIDTH = 5
CHUNK = 128
XBC_DIM = 3072
HP = 128
GW = D_INNER // SSM_GROUPS
HG = SSM_HEADS // SSM_GROUPS
PH = 64
U_Z, U_GA, U_GB, U_XBC, U_SMALL = 0, 2048, 3072, 4096, 7168
S_CQ, S_CKV, S_KPE, S_DT, SMALL_W = 0, 384, 640, 768, 896
U_PAD = U_SMALL + SMALL_W
IN_SPLITS = (Q_LORA, KV_LORA, QK_ROPE, D_INNER, XBC_DIM, SSM_HEADS, SSM_HEADS, D_MODEL, D_MODEL)

ADAM_LR = 0.001
ADAM_B1 = 0.9
ADAM_B2 = 0.999
ADAM_EPS = 1e-08
ADAM_WD = 0.01
ADAM_STEP = 10

NN = (((1,), (0,)), ((), ()))
NT = (((1,), (1,)), ((), ()))
TN = (((0,), (0,)), ((), ()))


def _pick(n, pref):
    best = None
    d = 128
    while d <= min(n, pref):
        if n % d == 0:
            best = d
        d += 128
    return best if best is not None else n


def _silu(x):
    return x * jax.nn.sigmoid(x)


def _dsilu(x):
    s = jax.nn.sigmoid(x)
    return s * (1.0 + x * (1.0 - s))


def _softplus(x):
    return jnp.maximum(x, 0.0) + jnp.log(1.0 + jnp.exp(-jnp.abs(x)))


def _mm(As, Bs, *, name, ta=False, tb=False, out_dtypes=(F32,), epilogue=None, extras=(), extra_offs=None,
        tm=1024, tn=512, tk=2048):
    As, Bs, extras = list(As), list(Bs), list(extras)
    a0, b0 = As[0], Bs[0]
    M, K = (a0.shape[1], a0.shape[0]) if ta else a0.shape
    N = b0.shape[0] if tb else b0.shape[1]
    tm, tn, tk = _pick(M, tm), _pick(N, tn), _pick(K, tk)
    nk = K // tk
    n_a, n_b, n_e, n_o = len(As), len(Bs), len(extras), len(out_dtypes)
    n_acc = (n_b if n_a == 1 else 1) if nk > 1 else 0
    if extra_offs is None:
        extra_offs = (0,) * n_e
    dn = (((0,) if ta else (1,), (1,) if tb else (0,)), ((), ()))
    bytes_a = sum(a.size * a.dtype.itemsize for a in As)
    bytes_b = sum(b.size * b.dtype.itemsize for b in Bs)
    n_outer = (N // tn) * bytes_a + bytes_b < (M // tm) * bytes_b + bytes_a

    def products(a_refs, b_refs):
        if n_a == 1:
            a = a_refs[0][...].astype(BF)
            return [lax.dot_general(a, b[...].astype(BF), dn, preferred_element_type=F32) for b in b_refs]
        total = None
        for a, b in zip(a_refs, b_refs):
            p = lax.dot_general(a[...].astype(BF), b[...].astype(BF), dn, preferred_element_type=F32)
            total = p if total is None else total + p
        return [total]

    def finish(accs, e_refs, o_refs):
        ex = [e[...] for e in e_refs]
        outs = epilogue(*accs, *ex) if epilogue is not None else tuple(accs)
        for o_ref, val in zip(o_refs, outs):
            o_ref[...] = val.astype(o_ref.dtype)

    def body(*refs):
        a_refs, b_refs = refs[:n_a], refs[n_a:n_a + n_b]
        e_refs = refs[n_a + n_b:n_a + n_b + n_e]
        o_refs = refs[n_a + n_b + n_e:n_a + n_b + n_e + n_o]
        acc_refs = refs[n_a + n_b + n_e + n_o:]
        if nk == 1:
            finish(products(a_refs, b_refs), e_refs, o_refs)
            return
        k = pl.program_id(2)

        @pl.when(k == 0)
        def _():
            for acc in acc_refs:
                acc[...] = jnp.zeros_like(acc)

        for acc, p in zip(acc_refs, products(a_refs, b_refs)):
            acc[...] += p

        @pl.when(k == nk - 1)
        def _():
            finish([acc[...] for acc in acc_refs], e_refs, o_refs)

    def at(f):
        return (lambda j, i, k: f(i, j, k)) if n_outer else f

    a_spec = pl.BlockSpec((tk, tm), at(lambda i, j, k: (k, i))) if ta else pl.BlockSpec((tm, tk), at(lambda i, j, k: (i, k)))
    b_spec = pl.BlockSpec((tn, tk), at(lambda i, j, k: (j, k))) if tb else pl.BlockSpec((tk, tn), at(lambda i, j, k: (k, j)))
    e_specs = [pl.BlockSpec((tm, tn), at(functools.partial(lambda i, j, k, o: (i, j + o), o=off // tn))) for off in extra_offs]
    for off in extra_offs:
        assert off % tn == 0
    outs = pl.pallas_call(
        body, name=name,
        out_shape=tuple(jax.ShapeDtypeStruct((M, N), dt) for dt in out_dtypes),
        grid=(N // tn, M // tm, nk) if n_outer else (M // tm, N // tn, nk),
        in_specs=[a_spec] * n_a + [b_spec] * n_b + e_specs,
        out_specs=tuple(pl.BlockSpec((tm, tn), at(lambda i, j, k: (i, j))) for _ in out_dtypes),
        scratch_shapes=[pltpu.VMEM((tm, tn), F32)] * n_acc,
        compiler_params=pltpu.CompilerParams(dimension_semantics=("parallel", "parallel", "arbitrary")),
    )(*As, *Bs, *extras)
    return outs[0] if n_o == 1 else outs


def _rms_fwd(x, g, *, name, blk_w=None, blk_idx=0, off=0, width=None, out_dtype=BF):
    T = x.shape[0]
    blk_w = x.shape[1] if blk_w is None else blk_w
    width = blk_w if width is None else width
    tt = _pick(T, 512)

    def body(x_ref, g_ref, o_ref):
        xf = x_ref[:, off:off + width]
        r = lax.rsqrt(jnp.mean(xf * xf, axis=-1, keepdims=True) + EPS)
        o_ref[...] = (xf * r * g_ref[...]).astype(o_ref.dtype)

    return pl.pallas_call(
        body, name=name, out_shape=jax.ShapeDtypeStruct((T, width), out_dtype), grid=(T // tt,),
        in_specs=[pl.BlockSpec((tt, blk_w), lambda i: (i, blk_idx)), pl.BlockSpec((1, width), lambda i: (0, 0))],
        out_specs=pl.BlockSpec((tt, width), lambda i: (i, 0)),
    )(x, g)


def _rms_bwd(dy, x, g, *, name, blk_w=None, blk_idx=0, off=0, width=None, add=None, out_dtypes=(F32,)):
    T = x.shape[0]
    blk_w = x.shape[1] if blk_w is None else blk_w
    width = blk_w if width is None else width
    tt = _pick(T, 512)
    has_add = add is not None
    n_dx = len(out_dtypes)

    def body(*refs):
        dy_ref, x_ref, g_ref = refs[:3]
        dx_refs, dg_ref = refs[3 + has_add:3 + has_add + n_dx], refs[-1]
        xf = x_ref[:, off:off + width]
        d = dy_ref[...].astype(F32)
        r = lax.rsqrt(jnp.mean(xf * xf, axis=-1, keepdims=True) + EPS)
        gd = d * g_ref[...]
        dx = r * gd - xf * (r * r * r) * jnp.mean(gd * xf, axis=-1, keepdims=True)
        if has_add:
            dx = dx + refs[3][...]
        for dx_ref in dx_refs:
            dx_ref[...] = dx.astype(dx_ref.dtype)

        @pl.when(pl.program_id(0) == 0)
        def _():
            dg_ref[...] = jnp.zeros_like(dg_ref)

        dg_ref[...] += jnp.broadcast_to(jnp.sum(d * xf * r, axis=0, keepdims=True), dg_ref.shape)

    row = pl.BlockSpec((tt, width), lambda i: (i, 0))
    in_specs = [row, pl.BlockSpec((tt, blk_w), lambda i: (i, blk_idx)), pl.BlockSpec((1, width), lambda i: (0, 0))]
    args = [dy, x, g]
    if has_add:
        in_specs.append(row)
        args.append(add)
    return pl.pallas_call(
        body, name=name,
        out_shape=tuple(jax.ShapeDtypeStruct((T, width), dt) for dt in out_dtypes) + (jax.ShapeDtypeStruct((8, width), F32),),
        grid=(T // tt,), in_specs=in_specs,
        out_specs=(row,) * n_dx + (pl.BlockSpec((8, width), lambda i: (0, 0)),),
        compiler_params=pltpu.CompilerParams(dimension_semantics=("arbitrary",)),
    )(*args)


def _rope_tables(pos_col, freq_lane):
    T = pos_col.shape[0]
    tt = _pick(T, 512)

    def body(p_ref, f_ref, c_ref, s_ref):
        ang = p_ref[...] * f_ref[...]
        lane = lax.broadcasted_iota(jnp.int32, ang.shape, 1)
        c_ref[...] = jnp.where(lane < QK_HEAD, jnp.cos(ang), 0.0)
        sn = jnp.sin(ang)
        s_ref[...] = jnp.where((lane >= QK_NOPE) & (lane < QK_NOPE + 16), -sn,
                               jnp.where((lane >= QK_NOPE + 16) & (lane < QK_HEAD), sn, 0.0))

    return pl.pallas_call(
        body, name="rope_tables", out_shape=(jax.ShapeDtypeStruct((T, HP), F32),) * 2, grid=(T // tt,),
        in_specs=[pl.BlockSpec((tt, 1), lambda i: (i, 0)), pl.BlockSpec((1, HP), lambda i: (0, 0))],
        out_specs=(pl.BlockSpec((tt, HP), lambda i: (i, 0)),) * 2,
    )(pos_col, freq_lane)


def _swap_rope_halves(n):
    lane = lax.broadcasted_iota(jnp.int32, n.shape, 1)
    lo = (lane >= QK_NOPE) & (lane < QK_NOPE + 16)
    hi = (lane >= QK_NOPE + 16) & (lane < QK_HEAD)
    return jnp.where(lo, pltpu.roll(n, HP - 16, 1), jnp.where(hi, pltpu.roll(n, 16, 1), 0.0))


def _qk_prep_fwd(raw, kpe, gain, C, S, *, name, kpe_blk=0, out_scale=1.0):
    T = raw.shape[0]
    tt = _pick(T, 512)
    has_kpe = kpe is not None

    def body(*refs):
        if has_kpe:
            raw_ref, kpe_ref, g_ref, c_ref, s_ref, o_ref = refs
            xr = raw_ref[...] + kpe_ref[...]
        else:
            raw_ref, g_ref, c_ref, s_ref, o_ref = refs
            xr = raw_ref[...]
        r = lax.rsqrt(jnp.sum(xr * xr, axis=-1, keepdims=True) * (1.0 / QK_HEAD) + EPS)
        n = xr * r * g_ref[...]
        o_ref[...] = ((n * c_ref[...] + _swap_rope_halves(n) * s_ref[...]) * out_scale).astype(o_ref.dtype)

    head = pl.BlockSpec((tt, HP), lambda i, h: (i, h))
    shared = pl.BlockSpec((tt, HP), lambda i, h: (i, 0))
    kpe_spec = pl.BlockSpec((tt, HP), lambda i, h: (i, kpe_blk))
    in_specs = [head] + ([kpe_spec] if has_kpe else []) + [pl.BlockSpec((1, HP), lambda i, h: (0, 0)), shared, shared]
    args = [raw] + ([kpe] if has_kpe else []) + [gain, C, S]
    return pl.pallas_call(
        body, name=name, out_shape=jax.ShapeDtypeStruct(raw.shape, BF), grid=(T // tt, N_HEADS),
        in_specs=in_specs, out_specs=head,
    )(*args)


def _qk_prep_bwd(dout, raw, kpe, gain, C, S, *, name, kpe_blk=0, in_scale=1.0):
    T = raw.shape[0]
    tt = _pick(T, 512)
    has_kpe = kpe is not None

    def body(*refs):
        if has_kpe:
            d_ref, raw_ref, kpe_ref, g_ref, c_ref, s_ref, dx_ref, dg_ref, dkpe_ref = refs
            xr = raw_ref[...] + kpe_ref[...]
        else:
            d_ref, raw_ref, g_ref, c_ref, s_ref, dx_ref, dg_ref = refs
            xr = raw_ref[...]
        i, h = pl.program_id(0), pl.program_id(1)
        d = d_ref[...].astype(F32) * in_scale
        r = lax.rsqrt(jnp.sum(xr * xr, axis=-1, keepdims=True) * (1.0 / QK_HEAD) + EPS)
        dn = d * c_ref[...] + _swap_rope_halves(d * s_ref[...])
        gd = dn * g_ref[...]
        dx = r * gd - xr * (r * r * r) * (jnp.sum(gd * xr, axis=-1, keepdims=True) * (1.0 / QK_HEAD))
        dx_ref[...] = dx.astype(dx_ref.dtype)

        @pl.when((i == 0) & (h == 0))
        def _():
            dg_ref[...] = jnp.zeros_like(dg_ref)

        dg_ref[...] += jnp.broadcast_to(jnp.sum(dn * xr * r, axis=0, keepdims=True), dg_ref.shape)
        if has_kpe:
            @pl.when(h == 0)
            def _():
                dkpe_ref[...] = jnp.zeros_like(dkpe_ref)

            dkpe_ref[...] += dx

    head = pl.BlockSpec((tt, HP), lambda i, h: (i, h))
    shared = pl.BlockSpec((tt, HP), lambda i, h: (i, 0))
    kpe_spec = pl.BlockSpec((tt, HP), lambda i, h: (i, kpe_blk))
    in_specs = [head, head] + ([kpe_spec] if has_kpe else []) + [pl.BlockSpec((1, HP), lambda i, h: (0, 0)), shared, shared]
    args = [dout, raw] + ([kpe] if has_kpe else []) + [gain, C, S]
    out_shape = [jax.ShapeDtypeStruct(raw.shape, BF), jax.ShapeDtypeStruct((8, HP), F32)]
    out_specs = [head, pl.BlockSpec((8, HP), lambda i, h: (0, 0))]
    if has_kpe:
        out_shape.append(jax.ShapeDtypeStruct((T, HP), F32))
        out_specs.append(shared)
    return pl.pallas_call(
        body, name=name, out_shape=tuple(out_shape), grid=(T // tt, N_HEADS),
        in_specs=in_specs, out_specs=tuple(out_specs),
        compiler_params=pltpu.CompilerParams(dimension_semantics=("arbitrary", "arbitrary")),
    )(*args)


ATTN_SCALE = 1.0 / math.sqrt(QK_HEAD)
LOG2E = 1.0 / math.log(2.0)
Q_SCALE = ATTN_SCALE * LOG2E


def _attn_fwd(q, k, v):
    T = q.shape[0]
    tq = _pick(T, 256)

    def body(q_ref, k_ref, v_ref, o_ref, lse_ref):
        s = lax.dot_general(q_ref[...], k_ref[...], NT, preferred_element_type=F32)
        m = jnp.max(s, axis=-1, keepdims=True)
        p = jnp.exp2(s - m)
        l = jnp.sum(p, axis=-1, keepdims=True)
        o = jnp.dot(p.astype(BF), v_ref[...], preferred_element_type=F32)
        o_ref[...] = o / l
        lse_ref[...] = jnp.broadcast_to(m + jnp.log2(l), lse_ref.shape)

    qs = pl.BlockSpec((tq, HP), lambda h, i: (i, h))
    kv = pl.BlockSpec((T, HP), lambda h, i: (0, h))
    return pl.pallas_call(
        body, name="attn_fwd", out_shape=(jax.ShapeDtypeStruct(q.shape, F32),) * 2, grid=(N_HEADS, T // tq),
        in_specs=[qs, kv, kv], out_specs=(qs, qs),
        compiler_params=pltpu.CompilerParams(dimension_semantics=("parallel", "parallel")),
    )(q, k, v)


def _attn_bwd(q, k, v, do, o, lse):
    T = q.shape[0]
    tb = _pick(T, 512)
    nb = T // tb

    def body(q_ref, k_ref, v_ref, do_ref, o_ref, lse_ref, dq_ref, dk_ref, dv_ref):
        dq_ref[...] = jnp.zeros_like(dq_ref)

        def k_loop(j, carry):
            ks = pl.ds(pl.multiple_of(j * tb, tb), tb)
            kj, vj = k_ref[ks, :], v_ref[ks, :]

            def q_loop(i, acc):
                dk_acc, dv_acc = acc
                qs = pl.ds(pl.multiple_of(i * tb, tb), tb)
                qi = q_ref[qs, :]
                doi = do_ref[qs, :]
                delta = jnp.sum(doi * o_ref[qs, :], axis=-1, keepdims=True)
                dob = doi.astype(BF)
                s = lax.dot_general(qi, kj, NT, preferred_element_type=F32)
                p = jnp.exp2(s - lse_ref[qs, 0:1])
                dp = lax.dot_general(dob, vj, NT, preferred_element_type=F32)
                ds = (p * (dp - delta)).astype(BF)
                dv_acc = dv_acc + lax.dot_general(p.astype(BF), dob, TN, preferred_element_type=F32)
                dk_acc = dk_acc + lax.dot_general(ds, qi, TN, preferred_element_type=F32)
                dq_ref[qs, :] += jnp.dot(ds, kj, preferred_element_type=F32)
                return dk_acc, dv_acc

            zero = jnp.zeros((tb, HP), F32)
            dk_acc, dv_acc = lax.fori_loop(0, nb, q_loop, (zero, zero))
            dk_ref[ks, :] = dk_acc
            dv_ref[ks, :] = dv_acc.astype(dv_ref.dtype)
            return carry

        lax.fori_loop(0, nb, k_loop, 0)

    spec = pl.BlockSpec((T, HP), lambda h: (0, h))
    return pl.pallas_call(
        body, name="attn_bwd",
        out_shape=(jax.ShapeDtypeStruct(q.shape, F32), jax.ShapeDtypeStruct(q.shape, F32), jax.ShapeDtypeStruct(q.shape, BF)),
        grid=(N_HEADS,), in_specs=[spec] * 6, out_specs=(spec,) * 3,
        compiler_params=pltpu.CompilerParams(dimension_semantics=("parallel",)),
    )(q, k, v, do, o, lse)


CONV_TC = 512
CONV_PAD = CONV_WIDTH // 2


def _halo_specs(tr, col_of):
    r8 = tr // 8
    cur = pl.BlockSpec((tr, CONV_TC), lambda j, i: (i, col_of(j)))
    prev = pl.BlockSpec((8, CONV_TC), lambda j, i: (jnp.maximum(i * r8 - 1, 0), col_of(j)))

    def nxt_map(j, i, n8):
        return (jnp.minimum((i + 1) * r8, n8 - 1), col_of(j))

    return cur, prev, nxt_map


def _with_halo(prev_ref, cur_ref, next_ref, i, n_i):
    prev = jnp.where(i == 0, 0.0, prev_ref[...].astype(F32))
    nxt = jnp.where(i == n_i - 1, 0.0, next_ref[...].astype(F32))
    return jnp.concatenate([prev, cur_ref[...].astype(F32), nxt], axis=0)


def _conv_fwd(u, w8, b):
    T = u.shape[0]
    tr = _pick(T, 512)
    n_i = T // tr
    c0 = U_XBC // CONV_TC
    cur, prev, nxt_map = _halo_specs(tr, lambda j: c0 + j)
    nxt = pl.BlockSpec((8, CONV_TC), functools.partial(nxt_map, n8=T // 8))

    def body(p_ref, c_ref, n_ref, w_ref, b_ref, pre_ref, act_ref):
        i = pl.program_id(1)
        full = _with_halo(p_ref, c_ref, n_ref, i, n_i)
        acc = jnp.broadcast_to(b_ref[...], (tr, CONV_TC))
        for kk in range(CONV_WIDTH):
            acc = acc + full[8 - CONV_PAD + kk:8 - CONV_PAD + kk + tr, :] * w_ref[kk:kk + 1, :]
        pre_ref[...] = acc
        act_ref[...] = _silu(acc)

    out = pl.BlockSpec((tr, CONV_TC), lambda j, i: (i, j))
    return pl.pallas_call(
        body, name="conv_fwd", out_shape=(jax.ShapeDtypeStruct((T, XBC_DIM), F32),) * 2,
        grid=(XBC_DIM // CONV_TC, n_i),
        in_specs=[prev, cur, nxt, pl.BlockSpec((8, CONV_TC), lambda j, i: (0, j)), pl.BlockSpec((1, CONV_TC), lambda j, i: (0, j))],
        out_specs=(out, out),
    )(u, u, u, w8, b)


def _conv_dpre(dacts, pre, col0, *, name):
    T, width = dacts[0].shape
    tt = _pick(T, 512)
    n_d = len(dacts)
    c0 = col0 // CONV_TC

    def body(*refs):
        d = refs[0][...]
        for r in refs[1:n_d]:
            d = d + r[...]
        refs[n_d + 1][...] = d * _dsilu(refs[n_d][...])

    blk = pl.BlockSpec((tt, CONV_TC), lambda j, i: (i, j))
    return pl.pallas_call(
        body, name=name, out_shape=jax.ShapeDtypeStruct((T, width), F32), grid=(width // CONV_TC, T // tt),
        in_specs=[blk] * n_d + [pl.BlockSpec((tt, CONV_TC), lambda j, i: (i, c0 + j))], out_specs=blk,
    )(*dacts, pre)


def _conv_bwd(dpre, u, w8, col0, *, name):
    T, width = dpre.shape
    tr = _pick(T, 512)
    n_i = T // tr
    cd = col0 // CONV_TC
    cx = (U_XBC + col0) // CONV_TC
    d_cur, d_prev, d_nxt_map = _halo_specs(tr, lambda j: j)
    x_cur, x_prev, x_nxt_map = _halo_specs(tr, lambda j: cx + j)
    d_nxt = pl.BlockSpec((8, CONV_TC), functools.partial(d_nxt_map, n8=T // 8))
    x_nxt = pl.BlockSpec((8, CONV_TC), functools.partial(x_nxt_map, n8=T // 8))

    def body(dp_ref, dc_ref, dn_ref, xp_ref, xc_ref, xn_ref, w_ref, dx_ref, dw_ref):
        i = pl.program_id(1)
        dfull = _with_halo(dp_ref, dc_ref, dn_ref, i, n_i)
        xfull = _with_halo(xp_ref, xc_ref, xn_ref, i, n_i)
        dcur = dc_ref[...]
        dx = jnp.zeros((tr, CONV_TC), F32)
        rows = []
        for kk in range(CONV_WIDTH):
            dx = dx + dfull[8 + CONV_PAD - kk:8 + CONV_PAD - kk + tr, :] * w_ref[kk:kk + 1, :]
            rows.append(jnp.sum(dcur * xfull[8 - CONV_PAD + kk:8 - CONV_PAD + kk + tr, :], axis=0, keepdims=True))
        rows.append(jnp.sum(dcur, axis=0, keepdims=True))
        rows.append(jnp.zeros((2, CONV_TC), F32))
        dx_ref[...] = dx.astype(dx_ref.dtype)

        @pl.when(i == 0)
        def _():
            dw_ref[...] = jnp.zeros_like(dw_ref)

        dw_ref[...] += jnp.concatenate(rows, axis=0)

    out = pl.BlockSpec((tr, CONV_TC), lambda j, i: (i, j))
    return pl.pallas_call(
        body, name=name, out_shape=(jax.ShapeDtypeStruct((T, width), BF), jax.ShapeDtypeStruct((8, width), F32)),
        grid=(width // CONV_TC, n_i),
        in_specs=[d_prev, d_cur, d_nxt, x_prev, x_cur, x_nxt, pl.BlockSpec((8, CONV_TC), lambda j, i: (0, cd + j))],
        out_specs=(out, pl.BlockSpec((8, CONV_TC), lambda j, i: (0, j))),
        compiler_params=pltpu.CompilerParams(dimension_semantics=("parallel", "arbitrary")),
    )(dpre, dpre, dpre, u, u, u, w8)


N_HB = 2 * SSM_GROUPS
P_DT, P_CS, P_E, P_W = 0, HP, 2 * HP, 3 * HP
DT_BLK = (U_SMALL + S_DT) // HP


def _tri(rev, transpose=False):
    rows = lax.broadcasted_iota(jnp.int32, (CHUNK, CHUNK), 0)
    cols = lax.broadcasted_iota(jnp.int32, (CHUNK, CHUNK), 1)
    if transpose:
        rows, cols = cols, rows
    return (cols >= rows) if rev else (cols <= rows)


def _ssd_prep(u, bias8, alog8):
    T = u.shape[0]
    nc = T // CHUNK

    def body(dt_ref, bias_ref, a_ref, cols_ref, rows_ref):
        lane = lax.broadcasted_iota(jnp.int32, (CHUNK, HP), 1)
        dt = _softplus(dt_ref[...] + bias_ref[0:1, :])
        da = dt * (-jnp.exp(a_ref[0:1, :]))
        cs_f = jnp.dot(jnp.where(_tri(False), 1.0, 0.0).astype(F32), da, precision=HI, preferred_element_type=F32)
        cs_b = jnp.dot(jnp.where(_tri(True), 1.0, 0.0).astype(F32), da, precision=HI, preferred_element_type=F32)
        cs = jnp.where(lane < SSM_HEADS, cs_f, cs_b)
        tot = jnp.where(lane[0:1] < SSM_HEADS, cs_f[CHUNK - 1:CHUNK, :], cs_b[0:1, :])
        e, w = jnp.exp(cs), jnp.exp(tot - cs)
        tot8 = jnp.broadcast_to(tot, (8, HP))
        etot8 = jnp.exp(tot8)
        for b in range(N_HB):
            down = (HP - HG * b) % HP

            def rolled(v):
                return pltpu.roll(v, down, 1) if down else v

            cols_ref[b, :, P_DT:P_DT + HP] = rolled(dt)
            cs_r = rolled(cs)
            cols_ref[b, :, P_CS:P_CS + HP] = cs_r
            cols_ref[b, :, P_E:P_E + HP] = rolled(e)
            cols_ref[b, :, P_W:P_W + HP] = rolled(w)
            rows_ref[b, 0, 0:8, :] = cs_r.T[0:8, :]
            r8 = lax.broadcasted_iota(jnp.int32, (8, HP), 0)
            rows_ref[b, 0, 8:16, :] = jnp.where(r8 == 0, rolled(tot8), jnp.where(r8 == 1, rolled(etot8), 0.0))

    vec = pl.BlockSpec((8, HP), lambda c: (0, 0))
    return pl.pallas_call(
        body, name="ssd_prep",
        out_shape=(jax.ShapeDtypeStruct((N_HB, T, 4 * HP), F32), jax.ShapeDtypeStruct((N_HB, nc, 16, HP), F32)),
        grid=(nc,), in_specs=[pl.BlockSpec((CHUNK, HP), lambda c: (c, DT_BLK)), vec, vec],
        out_specs=(pl.BlockSpec((N_HB, CHUNK, 4 * HP), lambda c: (0, c, 0)), pl.BlockSpec((N_HB, 1, 16, HP), lambda c: (0, c, 0, 0))),
    )(u, bias8, alog8)


def _ssd_specs(T, rev, bwd):
    nc = T // CHUNK
    fwd_order = (lambda c: nc - 1 - c) if rev else (lambda c: c)
    cm = (lambda c: fwd_order(nc - 1 - c)) if bwd else fwd_order
    hb0 = SSM_GROUPS if rev else 0
    xs = pl.BlockSpec((CHUNK, GW), lambda c, g: (cm(c), g))
    bs = pl.BlockSpec((CHUNK, D_STATE), lambda c, g: (cm(c), D_INNER // D_STATE + g))
    cs = pl.BlockSpec((CHUNK, D_STATE), lambda c, g: (cm(c), (D_INNER + SSM_GROUPS * D_STATE) // D_STATE + g))
    cols = pl.BlockSpec((1, CHUNK, 4 * HP), lambda c, g: (hb0 + g, cm(c), 0))
    rows = pl.BlockSpec((1, 1, 16, HP), lambda c, g: (hb0 + g, cm(c), 0, 0))
    return nc, cm, xs, bs, cs, cols, rows


def _head_terms(cols_ref, rows_ref, hh, incl):
    dt = cols_ref[0, :, P_DT + hh:P_DT + hh + 1]
    col = cols_ref[0, :, P_CS + hh:P_CS + hh + 1]
    e = cols_ref[0, :, P_E + hh:P_E + hh + 1]
    w = cols_ref[0, :, P_W + hh:P_W + hh + 1]
    row = rows_ref[0, 0, hh:hh + 1, :]
    etot = rows_ref[0, 0, 9:10, hh:hh + 1]
    lmat = jnp.where(incl, jnp.exp(col - row), 0.0)
    return dt, col, row, e, w, etot, lmat


def _ssd_fwd(act, cols, rows, *, rev, name):
    T = act.shape[0]
    nc, cm, xs_s, b_s, c_s, cols_s, rows_s = _ssd_specs(T, rev, False)

    def body(x_ref, b_ref, c_ref, cols_ref, rows_ref, y_ref, st_ref, state):
        c, g = pl.program_id(0), pl.program_id(1)

        @pl.when(c == 0)
        def _():
            state[g] = jnp.zeros((D_STATE, GW), F32)

        incl = _tri(rev)
        bm, cmat = b_ref[...].astype(BF), c_ref[...].astype(BF)
        bm_t = b_ref[...].T.astype(BF)
        cb = lax.dot_general(cmat, bm, NT, preferred_element_type=F32)
        prev_all = state[g]
        st_ref[...] = prev_all
        for hh in range(HG):
            hs = slice(PH * hh, PH * (hh + 1))
            dt, col, row, e, w, etot, lmat = _head_terms(cols_ref, rows_ref, hh, incl)
            xdt = x_ref[:, hs] * dt
            prev = prev_all[:, hs]
            yd = jnp.dot((cb * lmat).astype(BF), xdt.astype(BF), preferred_element_type=F32)
            yo = jnp.dot(cmat, prev.astype(BF), preferred_element_type=F32) * e
            y_ref[:, hs] = yd + yo
            state[g, :, hs] = prev * etot + jnp.dot(bm_t, (xdt * w).astype(BF), preferred_element_type=F32)

    return pl.pallas_call(
        body, name=name,
        out_shape=(jax.ShapeDtypeStruct((T, D_INNER), F32), jax.ShapeDtypeStruct((nc * D_STATE, D_INNER), F32)),
        grid=(nc, SSM_GROUPS), in_specs=[xs_s, b_s, c_s, cols_s, rows_s], out_specs=(xs_s, xs_s),
        scratch_shapes=[pltpu.VMEM((SSM_GROUPS, D_STATE, GW), F32)],
        compiler_params=pltpu.CompilerParams(dimension_semantics=("arbitrary", "arbitrary")),
    )(act, act, act, cols, rows)


def _ssd_bwd(act, cols, rows, states, dy, *, rev, name):
    T = act.shape[0]
    nc, cm, xs_s, b_s, c_s, cols_s, rows_s = _ssd_specs(T, rev, True)

    def body(x_ref, b_ref, c_ref, cols_ref, rows_ref, st_ref, dy_ref, dx_ref, db_ref, dc_ref, dsel_ref, dtot_ref, dstate):
        c, g = pl.program_id(0), pl.program_id(1)

        @pl.when(c == 0)
        def _():
            dstate[g] = jnp.zeros((D_STATE, GW), F32)

        incl, incl_t = _tri(rev), _tri(rev, transpose=True)
        bm, cmat = b_ref[...].astype(BF), c_ref[...].astype(BF)
        cm_t = c_ref[...].T.astype(BF)
        cb = lax.dot_general(cmat, bm, NT, preferred_element_type=F32)
        cb_t = lax.dot_general(bm, cmat, NT, preferred_element_type=F32)
        prev_all, ds_all = st_ref[...], dstate[g]
        lane = lax.broadcasted_iota(jnp.int32, (1, CHUNK), 1)
        sub = lax.broadcasted_iota(jnp.int32, (CHUNK, 1), 0)
        zero = jnp.zeros((CHUNK, CHUNK), F32)
        dcb, db_acc, dc_acc = zero, zero, zero
        dcs_col, dcs_row, ddt_x, dtot = zero, zero, zero, jnp.zeros((1, CHUNK), F32)
        for hh in range(HG):
            hs = slice(PH * hh, PH * (hh + 1))
            dth, col, row, e, w, etot, lmat = _head_terms(cols_ref, rows_ref, hh, incl)
            x = x_ref[:, hs]
            xdt = x * dth
            xdb = xdt.astype(BF)
            mmat = cb * lmat
            mmat_t = cb_t * jnp.where(incl_t, jnp.exp(row - col), 0.0)
            prev, ds_ = prev_all[:, hs], ds_all[:, hs]
            pb, dsb = prev.astype(BF), ds_.astype(BF)
            dyh = dy_ref[:, hs]
            dyb = dyh.astype(BF)
            dprev = jnp.dot(cm_t, (dyh * e).astype(BF), preferred_element_type=F32) + ds_ * etot
            cp = jnp.dot(cmat, pb, preferred_element_type=F32)
            dcs_h = jnp.sum(dyh * cp, axis=1, keepdims=True) * e
            dc_acc = dc_acc + lax.dot_general(dyb, pb, NT, preferred_element_type=F32) * e
            dm = lax.dot_general(dyb, xdb, NT, preferred_element_type=F32)
            dxdt = jnp.dot(mmat_t.astype(BF), dyb, preferred_element_type=F32)
            qm = dm * mmat
            dcs_h = dcs_h + jnp.sum(qm, axis=1, keepdims=True)
            dcs_row = dcs_row + jnp.where(sub == hh, jnp.sum(qm, axis=0, keepdims=True), 0.0)
            dcb = dcb + dm * lmat
            bds = jnp.dot(bm, dsb, preferred_element_type=F32) * w
            dxdt = dxdt + bds
            db_acc = db_acc + lax.dot_general(xdb, dsb, NT, preferred_element_type=F32) * w
            t = jnp.sum(xdt * bds, axis=1, keepdims=True)
            dtot_h = jnp.sum(t) + jnp.sum(ds_ * prev) * etot
            dcs_h = dcs_h - t
            dcs_col = dcs_col + jnp.where(lane == hh, dcs_h, 0.0)
            dtot = dtot + jnp.where(lane == hh, dtot_h, 0.0)
            ddt_x = ddt_x + jnp.where(lane == hh, jnp.sum(dxdt * x, axis=1, keepdims=True), 0.0)
            dx_ref[:, hs] = dxdt * dth
            dstate[g, :, hs] = dprev
        dc_ref[...] = dc_acc + jnp.dot(dcb.astype(BF), bm, preferred_element_type=F32)
        db_ref[...] = db_acc + jnp.dot(dcb.T.astype(BF), cmat, preferred_element_type=F32)
        dsel_ref[0, :, 0:HP] = ddt_x
        dsel_ref[0, :, HP:2 * HP] = dcs_col - dcs_row.T
        dtot_ref[0, 0] = jnp.broadcast_to(dtot, (8, HP))

    bc_out = pl.BlockSpec((CHUNK, D_STATE), lambda c, g: (cm(c), g))
    return pl.pallas_call(
        body, name=name,
        out_shape=(jax.ShapeDtypeStruct((T, D_INNER), F32), jax.ShapeDtypeStruct((T, SSM_GROUPS * D_STATE), F32),
                   jax.ShapeDtypeStruct((T, SSM_GROUPS * D_STATE), F32), jax.ShapeDtypeStruct((SSM_GROUPS, T, 2 * HP), F32),
                   jax.ShapeDtypeStruct((SSM_GROUPS, nc, 8, HP), F32)),
        grid=(nc, SSM_GROUPS), in_specs=[xs_s, b_s, c_s, cols_s, rows_s, xs_s, xs_s],
        out_specs=(xs_s, bc_out, bc_out, pl.BlockSpec((1, CHUNK, 2 * HP), lambda c, g: (g, cm(c), 0)),
                   pl.BlockSpec((1, 1, 8, HP), lambda c, g: (g, cm(c), 0, 0))),
        scratch_shapes=[pltpu.VMEM((SSM_GROUPS, D_STATE, GW), F32)],
        compiler_params=pltpu.CompilerParams(dimension_semantics=("arbitrary", "arbitrary")),
    )(act, act, act, cols, rows, states, dy)


def _ssd_prep_bwd(u, bias8, alog8, dsel_f, dtot_f, dsel_b, dtot_b):
    T = u.shape[0]
    nc = T // CHUNK

    def body(dt_ref, bias_ref, a_ref, sf_ref, tf_ref, sb_ref, tb_ref, ddt_ref, da_ref, dbias_ref):
        @pl.when(pl.program_id(0) == 0)
        def _():
            da_ref[...] = jnp.zeros_like(da_ref)
            dbias_ref[...] = jnp.zeros_like(dbias_ref)

        lane = lax.broadcasted_iota(jnp.int32, (CHUNK, HP), 1)
        pre = dt_ref[...] + bias_ref[0:1, :]
        dt = _softplus(pre)
        a = -jnp.exp(a_ref[0:1, :])
        ddt_x, dcs, dtot = jnp.zeros((CHUNK, HP), F32), jnp.zeros((CHUNK, HP), F32), jnp.zeros((8, HP), F32)
        for b in range(N_HB):
            s_ref, t_ref, g = (sf_ref, tf_ref, b) if b < SSM_GROUPS else (sb_ref, tb_ref, b - SSM_GROUPS)
            mine = (lane >= HG * b) & (lane < HG * (b + 1))

            def up(v):
                return pltpu.roll(v, HG * b, 1) if b else v

            ddt_x = ddt_x + jnp.where(mine, up(s_ref[g, :, 0:HP]), 0.0)
            dcs = dcs + jnp.where(mine, up(s_ref[g, :, HP:2 * HP]), 0.0)
            dtot = dtot + jnp.where(mine[0:8], up(t_ref[g, 0]), 0.0)
        tri_f = jnp.where(_tri(False, transpose=True), 1.0, 0.0).astype(F32)
        tri_b = jnp.where(_tri(True, transpose=True), 1.0, 0.0).astype(F32)
        dda = jnp.where(lane < SSM_HEADS, jnp.dot(tri_f, dcs, precision=HI, preferred_element_type=F32),
                        jnp.dot(tri_b, dcs, precision=HI, preferred_element_type=F32)) + dtot[0:1, :]
        dpre = (ddt_x + dda * a) * jax.nn.sigmoid(pre)
        ddt_ref[...] = jnp.where(lane < 2 * SSM_HEADS, dpre, 0.0)
        dbias_ref[...] += jnp.broadcast_to(jnp.sum(dpre, axis=0, keepdims=True), (8, HP))
        da_ref[...] += jnp.broadcast_to(jnp.sum(dda * dt, axis=0, keepdims=True) * a, (8, HP))

    vec = pl.BlockSpec((8, HP), lambda c: (0, 0))
    sel = pl.BlockSpec((SSM_GROUPS, CHUNK, 2 * HP), lambda c: (0, c, 0))
    tot = pl.BlockSpec((SSM_GROUPS, 1, 8, HP), lambda c: (0, c, 0, 0))
    tile = pl.BlockSpec((CHUNK, HP), lambda c: (c, 0))
    return pl.pallas_call(
        body, name="ssd_prep_bwd",
        out_shape=(jax.ShapeDtypeStruct((T, HP), F32), jax.ShapeDtypeStruct((8, HP), F32), jax.ShapeDtypeStruct((8, HP), F32)),
        grid=(nc,), in_specs=[pl.BlockSpec((CHUNK, HP), lambda c: (c, DT_BLK)), vec, vec, sel, tot, sel, tot],
        out_specs=(tile, vec, vec),
        compiler_params=pltpu.CompilerParams(dimension_semantics=("arbitrary",)),
    )(u, bias8, alog8, dsel_f, dtot_f, dsel_b, dtot_b)


def _ssm_combine_fwd(y_f, y_b, act, u, dskip, gain):
    T = y_f.shape[0]
    tt = _pick(T, 256)

    def body(yf_ref, yb_ref, x_ref, z_ref, ds_ref, g_ref, y_ref, m_ref):
        y = yf_ref[...] + yb_ref[...] + ds_ref[...] * x_ref[...]
        y2 = y * _silu(z_ref[...])
        r = lax.rsqrt(jnp.mean(y2 * y2, axis=-1, keepdims=True) + EPS)
        y_ref[...] = y
        m_ref[...] = (y2 * r * g_ref[...]).astype(m_ref.dtype)

    blk = pl.BlockSpec((tt, GW), lambda i, g: (i, g))
    vec = pl.BlockSpec((1, GW), lambda i, g: (0, g))
    return pl.pallas_call(
        body, name="ssm_combine_fwd",
        out_shape=(jax.ShapeDtypeStruct((T, D_INNER), F32), jax.ShapeDtypeStruct((T, D_INNER), BF)),
        grid=(T // tt, SSM_GROUPS), in_specs=[blk, blk, blk, blk, vec, vec], out_specs=(blk, blk),
    )(y_f, y_b, act, u, dskip, gain)


def _ssm_combine_bwd(dm, y, act, u, dskip, gain):
    T = y.shape[0]
    tt = _pick(T, 256)

    def body(dm_ref, y_ref, x_ref, z_ref, ds_ref, g_ref, dy_ref, dz_ref, dxs_ref, dg_ref, dsk_ref):
        z = z_ref[...]
        y = y_ref[...]
        x = x_ref[...]
        sz = _silu(z)
        y2 = y * sz
        r = lax.rsqrt(jnp.mean(y2 * y2, axis=-1, keepdims=True) + EPS)
        d = dm_ref[...]
        gd = d * g_ref[...]
        dy2 = r * gd - y2 * (r * r * r) * jnp.mean(gd * y2, axis=-1, keepdims=True)
        dy = dy2 * sz
        dy_ref[...] = dy
        dz_ref[...] = (dy2 * y * _dsilu(z)).astype(dz_ref.dtype)
        dxs_ref[...] = dy * ds_ref[...]

        @pl.when(pl.program_id(1) == 0)
        def _():
            dg_ref[...] = jnp.zeros_like(dg_ref)
            dsk_ref[...] = jnp.zeros_like(dsk_ref)

        dg_ref[...] += jnp.broadcast_to(jnp.sum(d * y2 * r, axis=0, keepdims=True), dg_ref.shape)
        lane_sum = jnp.broadcast_to(jnp.sum(dy * x, axis=0, keepdims=True), (8, GW))
        src = lax.broadcasted_iota(jnp.int32, (GW, HP), 0)
        head = lax.broadcasted_iota(jnp.int32, (GW, HP), 1)
        to_head = jnp.where((src >= PH * head) & (src < PH * (head + 1)), 1.0, 0.0).astype(F32)
        dsk_ref[...] += jnp.dot(lane_sum, to_head, precision=HI, preferred_element_type=F32)

    blk = pl.BlockSpec((tt, GW), lambda g, i: (i, g))
    vec = pl.BlockSpec((1, GW), lambda g, i: (0, g))
    acc = pl.BlockSpec((8, GW), lambda g, i: (0, g))
    return pl.pallas_call(
        body, name="ssm_combine_bwd",
        out_shape=(jax.ShapeDtypeStruct((T, D_INNER), F32), jax.ShapeDtypeStruct((T, D_INNER), BF),
                   jax.ShapeDtypeStruct((T, D_INNER), F32), jax.ShapeDtypeStruct((8, D_INNER), F32),
                   jax.ShapeDtypeStruct((8, SSM_GROUPS * HP), F32)),
        grid=(SSM_GROUPS, T // tt), in_specs=[blk, blk, blk, blk, vec, vec],
        out_specs=(blk, blk, blk, acc, pl.BlockSpec((8, HP), lambda g, i: (0, g))),
        compiler_params=pltpu.CompilerParams(dimension_semantics=("parallel", "arbitrary")),
    )(dm, y, act, u, dskip, gain)


def _loss_head(y, target):
    T, D = y.shape
    tt = _pick(T, 512)

    def body(y_ref, t_ref, dy_ref, dyb_ref, l_ref):
        e = y_ref[...] - t_ref[...]
        dy_ref[...] = e * (1.0 / D)
        dyb_ref[...] = (e * (1.0 / D)).astype(dyb_ref.dtype)

        @pl.when(pl.program_id(0) == 0)
        def _():
            l_ref[...] = jnp.zeros_like(l_ref)

        l_ref[...] += jnp.sum(e * e) * (0.5 / D)

    blk = pl.BlockSpec((tt, D), lambda i: (i, 0))
    return pl.pallas_call(
        body, name="loss_head",
        out_shape=(jax.ShapeDtypeStruct((T, D), F32), jax.ShapeDtypeStruct((T, D), BF), jax.ShapeDtypeStruct((8, 128), F32)),
        grid=(T // tt,), in_specs=[blk, blk], out_specs=(blk, blk, pl.BlockSpec((8, 128), lambda i: (0, 0))),
        compiler_params=pltpu.CompilerParams(dimension_semantics=("arbitrary",)),
    )(y, target)


def _adamw(w, g, m, v, *, name):
    R, C = w.shape
    cap = max(8, (1 << 18) // C)
    tr = R
    if R % 8 == 0:
        tr = 8
        for cand in range(8, min(R, cap) + 1, 8):
            if R % cand == 0:
                tr = cand

    def body(w_ref, g_ref, m_ref, v_ref, d_ref, nm_ref, nv_ref):
        gg = g_ref[...]
        nm = ADAM_B1 * m_ref[...] + (1.0 - ADAM_B1) * gg
        nv = ADAM_B2 * v_ref[...] + (1.0 - ADAM_B2) * jnp.square(gg)
        m_hat = nm / (1.0 - ADAM_B1 ** ADAM_STEP)
        v_hat = nv / (1.0 - ADAM_B2 ** ADAM_STEP)
        d_ref[...] = -ADAM_LR * (m_hat / (jnp.sqrt(v_hat) + ADAM_EPS) + ADAM_WD * w_ref[...])
        nm_ref[...] = nm
        nv_ref[...] = nv

    blk = pl.BlockSpec((tr, C), lambda i: (i, 0))
    return pl.pallas_call(
        body, name=name, out_shape=(jax.ShapeDtypeStruct((R, C), F32),) * 3, grid=(R // tr,),
        in_specs=[blk] * 4, out_specs=(blk,) * 3,
    )(w, g, m, v)


ANY = pl.BlockSpec(memory_space=pl.ANY)


def _chip_peers():
    x, y, c = lax.axis_index("x"), lax.axis_index("y"), lax.axis_index("c")
    return x, y, c, [(1 - x, y), (x, 1 - y), (1 - x, 1 - y)]


def _half_rows(c, rh):
    return pl.ds(pl.multiple_of(c * rh, 16), rh)


def _my_chip():
    return 2 * lax.axis_index("x") + lax.axis_index("y")


def _gather_chips(wb, wf):
    rh = wb.shape[0] // 2

    def body(wb_ref, wf_ref, ob_ref, of_ref, send_sems, recv_sems):
        x, y, c, peers = _chip_peers()
        me = 2 * x + y
        half, other = _half_rows(c, rh), _half_rows(1 - c, rh)

        def chip_copy(k, slot):
            px, py = peers[k]
            return pltpu.make_async_remote_copy(
                src_ref=wb_ref.at[half], dst_ref=ob_ref.at[slot, half], send_sem=send_sems.at[k], recv_sem=recv_sems.at[k],
                device_id=(px, py, c), device_id_type=MESH)

        def passed_on(k, slot, rows):
            return pltpu.make_async_remote_copy(
                src_ref=ob_ref.at[slot, rows], dst_ref=ob_ref.at[slot, rows], send_sem=send_sems.at[3 + k],
                recv_sem=recv_sems.at[3 + k], device_id=(x, y, 1 - c), device_id_type=MESH)

        def small_copy(k, slot):
            px, py = peers[k]
            return pltpu.make_async_remote_copy(
                src_ref=wf_ref, dst_ref=of_ref.at[slot], send_sem=send_sems.at[6 + k], recv_sem=recv_sems.at[6 + k],
                device_id=(px, py, c), device_id_type=MESH)

        sends = [chip_copy(k, me) for k in range(3)] + [small_copy(k, me) for k in range(3)]
        for cp in sends:
            cp.start()
        chip_of = [2 * px + py for px, py in peers]
        for k in range(3):
            chip_copy(k, chip_of[k]).wait_recv()
            cp = passed_on(k, chip_of[k], half)
            cp.start()
            sends.append(cp)
        for k in range(3):
            passed_on(k, chip_of[k], other).wait_recv()
            small_copy(k, chip_of[k]).wait_recv()
        for cp in sends:
            cp.wait_send()

    ob, of = pl.pallas_call(
        body, name="gather_weights",
        out_shape=(jax.ShapeDtypeStruct((4,) + wb.shape, wb.dtype), jax.ShapeDtypeStruct((4,) + wf.shape, wf.dtype)),
        in_specs=[ANY, ANY], out_specs=(ANY, ANY),
        scratch_shapes=[pltpu.SemaphoreType.DMA((9,)), pltpu.SemaphoreType.DMA((9,))],
    )(wb, wf)
    me = _my_chip()
    return lax.dynamic_update_slice(ob, wb[None], (me, 0, 0)), lax.dynamic_update_slice(of, wf[None], (me, 0, 0))


def _halves_to_sibling(gp):
    rh = gp.shape[1] // 2

    def body(gp_ref, o_ref, send_sem, recv_sem):
        x, y, c = lax.axis_index("x"), lax.axis_index("y"), lax.axis_index("c")
        cp = pltpu.make_async_remote_copy(src_ref=gp_ref.at[:, _half_rows(1 - c, rh), :], dst_ref=o_ref, send_sem=send_sem,
                                          recv_sem=recv_sem, device_id=(x, y, 1 - c), device_id_type=MESH)
        cp.start()
        cp.wait()

    return pl.pallas_call(
        body, name="halves_to_sibling", out_shape=jax.ShapeDtypeStruct((gp.shape[0], rh, gp.shape[2]), gp.dtype),
        in_specs=[ANY], out_specs=ANY, scratch_shapes=[pltpu.SemaphoreType.DMA, pltpu.SemaphoreType.DMA],
    )(gp)


def _row_tile(rows, cap=1024):
    tr = 16
    for cand in range(16, cap + 1, 16):
        if rows % cand == 0:
            tr = cand
    return tr


def _add_halves(gp, sib, core):
    n, rh, C = sib.shape
    tr = _row_tile(rh)
    nt = rh // tr

    def body(c_ref, g_ref, s_ref, o_ref):
        o_ref[...] = (g_ref[...].astype(F32) + s_ref[...].astype(F32)).astype(o_ref.dtype)

    blk = pl.BlockSpec((1, tr, C), lambda j, i, c: (j, i, 0))
    return pl.pallas_call(
        body, name="add_halves", out_shape=jax.ShapeDtypeStruct(sib.shape, sib.dtype),
        grid_spec=pltpu.PrefetchScalarGridSpec(
            num_scalar_prefetch=1, grid=(n, nt),
            in_specs=[pl.BlockSpec((1, tr, C), lambda j, i, c: (j, c[0] * nt + i, 0)), blk], out_specs=blk),
    )(core, gp, sib)


def _join_halves(mine):
    rh = mine.shape[0]

    def body(m_ref, o_ref, send_sem, recv_sem):
        x, y, c = lax.axis_index("x"), lax.axis_index("y"), lax.axis_index("c")
        half, other = _half_rows(c, rh), _half_rows(1 - c, rh)

        def copy(rows):
            return pltpu.make_async_remote_copy(src_ref=m_ref, dst_ref=o_ref.at[rows], send_sem=send_sem, recv_sem=recv_sem,
                                                device_id=(x, y, 1 - c), device_id_type=MESH)

        send = copy(half)
        send.start()
        copy(other).wait_recv()
        send.wait_send()

    out = pl.pallas_call(
        body, name="join_halves", out_shape=jax.ShapeDtypeStruct((2 * rh, mine.shape[1]), mine.dtype),
        in_specs=[ANY], out_specs=ANY, scratch_shapes=[pltpu.SemaphoreType.DMA, pltpu.SemaphoreType.DMA],
    )(mine)
    return lax.dynamic_update_slice(out, mine, (lax.axis_index("c") * rh, 0))


def _exchange_chips(gp):
    def body(gp_ref, out_ref, send_sems, recv_sems):
        x, y, c, peers = _chip_peers()
        me = 2 * x + y

        def copies(sending):
            out = []
            for k, (px, py) in enumerate(peers):
                p = 2 * px + py
                out.append(pltpu.make_async_remote_copy(
                    src_ref=gp_ref.at[p], dst_ref=out_ref.at[me if sending else p],
                    send_sem=send_sems.at[k], recv_sem=recv_sems.at[k], device_id=(px, py, c), device_id_type=MESH))
            return out

        sends = copies(True)
        for cp in sends:
            cp.start()
        for cp in copies(False):
            cp.wait_recv()
        for cp in sends:
            cp.wait_send()

    out = pl.pallas_call(
        body, name="exchange_grads", out_shape=jax.ShapeDtypeStruct(gp.shape, gp.dtype),
        in_specs=[ANY], out_specs=ANY,
        scratch_shapes=[pltpu.SemaphoreType.DMA((3,)), pltpu.SemaphoreType.DMA((3,))],
    )(gp)
    me = _my_chip()
    return lax.dynamic_update_slice(out, lax.dynamic_slice_in_dim(gp, me, 1, axis=0), (me, 0, 0))


def _sum_slots(r4):
    _, R, C = r4.shape
    tr = _row_tile(R)

    def body(r_ref, o_ref):
        acc = r_ref[0].astype(F32)
        for s in range(1, 4):
            acc = acc + r_ref[s].astype(F32)
        o_ref[...] = acc

    return pl.pallas_call(
        body, name="sum_slots", out_shape=jax.ShapeDtypeStruct((R, C), F32), grid=(R // tr,),
        in_specs=[pl.BlockSpec((4, tr, C), lambda i: (0, i, 0))], out_specs=pl.BlockSpec((tr, C), lambda i: (i, 0)),
    )(r4)


N_DEV = 8


def _allreduce_small(p):
    rs = p.shape[0]

    def body(x_ref, sum_ref, all_ref, send_sems, recv_sems, local_sem):
        x, y, c = lax.axis_index("x"), lax.axis_index("y"), lax.axis_index("c")
        me, sibling = (x, y, c), (x, y, 1 - c)
        chips = [(1 - x, y), (x, 1 - y), (1 - x, 1 - y)]

        def rows(px, py, pc):
            return all_ref.at[pl.ds((4 * px + 2 * py + pc) * rs, rs), :]

        def copy(k, block, to, src=None):
            return pltpu.make_async_remote_copy(
                src_ref=rows(*block) if src is None else src, dst_ref=rows(*block),
                send_sem=send_sems.at[k], recv_sem=recv_sems.at[k], device_id=to, device_id_type=MESH)

        mine = pltpu.make_async_copy(x_ref, rows(*me), local_sem)
        mine.start()
        first = [copy(0, me, sibling, src=x_ref)]
        first += [copy(1 + j, me, (*chip, c), src=x_ref) for j, chip in enumerate(chips)]
        for cp in first:
            cp.start()
        passed = [copy(4 + j, (*chip, c), sibling) for j, chip in enumerate(chips)]
        for j, chip in enumerate(chips):
            copy(1 + j, (*chip, c), me).wait_recv()
            passed[j].start()
        copy(0, sibling, me).wait_recv()
        for j, chip in enumerate(chips):
            copy(4 + j, (*chip, 1 - c), me).wait_recv()
        for cp in first + passed:
            cp.wait_send()
        mine.wait()
        acc = all_ref[0:rs, :]
        for d in range(1, N_DEV):
            acc = acc + all_ref[d * rs:(d + 1) * rs, :]
        sum_ref[...] = acc

    vmem = pl.BlockSpec(memory_space=pltpu.VMEM)
    return pl.pallas_call(
        body, name="allreduce_small", out_shape=jax.ShapeDtypeStruct((rs, 128), F32),
        in_specs=[vmem], out_specs=vmem,
        scratch_shapes=[pltpu.VMEM((N_DEV * rs, 128), F32), pltpu.SemaphoreType.DMA((7,)), pltpu.SemaphoreType.DMA((7,)),
                        pltpu.SemaphoreType.DMA],
    )(p)


WEIGHTS = ('ffn1_norm', 'ffn1_w_gate', 'ffn1_w_up', 'ffn1_w_down', 'mix_norm', 'w_in', 'q_a_norm', 'w_q_b',
           'kv_a_norm', 'w_kv_b', 'q_head_norm', 'k_head_norm', 'conv_w', 'conv_b', 'a_log_fwd', 'a_log_bwd',
           'dt_bias_fwd', 'dt_bias_bwd', 'd_skip', 'ssm_norm', 'w_attn_branch', 'w_ssm_branch', 'w_out',
           'ffn2_norm', 'ffn2_w_gate', 'ffn2_w_up', 'ffn2_w_down')
PACKED = (('ffn1_w_gate', (D_MODEL, D_FF), 1), ('ffn1_w_up', (D_MODEL, D_FF), 1), ('ffn1_w_down', (D_FF, D_MODEL), 0),
          ('w_in', (D_MODEL, sum(IN_SPLITS)), 1), ('w_q_b', (Q_LORA, N_HEADS * QK_HEAD), 1),
          ('w_kv_b', (KV_LORA, N_HEADS * (QK_NOPE + V_HEAD)), 1),
          ('w_attn_branch', (N_HEADS * V_HEAD, D_MODEL), 0), ('w_ssm_branch', (D_INNER, D_MODEL), 0),
          ('w_out', (D_MODEL, D_MODEL), 0),
          ('ffn2_w_gate', (D_MODEL, D_FF), 1), ('ffn2_w_up', (D_MODEL, D_FF), 1), ('ffn2_w_down', (D_FF, D_MODEL), 0))
PACK_W = 1024
N_CHIPS = 4
SMALL = (('ffn1_norm', 1024), ('mix_norm', 1024), ('q_a_norm', 384), ('kv_a_norm', 256), ('q_head_norm', 96),
         ('k_head_norm', 96), ('conv_b', 3072), ('a_log_fwd', 32), ('a_log_bwd', 32), ('dt_bias_fwd', 32),
         ('dt_bias_bwd', 32), ('d_skip', 32), ('ssm_norm', 2048), ('ffn2_norm', 1024),
         ('conv_w', CONV_WIDTH * XBC_DIM), ('loss', 1))


def _shard_shape(shape, axis):
    return tuple(s // N_CHIPS if a == axis else s for a, s in enumerate(shape))


def _pack_rows():
    rows = sum(math.prod(_shard_shape(shape, axis)) // PACK_W for _, shape, axis in PACKED)
    return -(-rows // 32) * 32


def _pack(shards):
    parts = [shards[name].reshape(-1, PACK_W) for name, _, _ in PACKED]
    rows = sum(p.shape[0] for p in parts)
    parts.append(jnp.zeros((_pack_rows() - rows, PACK_W), parts[0].dtype))
    return jnp.concatenate(parts, axis=0)


def _unpack(packed):
    out, r = {}, 0
    for name, shape, axis in PACKED:
        sh = _shard_shape(shape, axis)
        n = math.prod(sh) // PACK_W
        out[name] = packed[r:r + n].reshape(sh)
        r += n
    return out


def _pack_small(vals):
    parts = []
    for name, n in SMALL:
        pad = -(-n // 128) * 128 - n
        parts.append(jnp.pad(vals[name].reshape(-1).astype(F32), (0, pad)).reshape(-1, 128))
    rows = sum(p.shape[0] for p in parts)
    parts.append(jnp.zeros((-(-rows // 8) * 8 - rows, 128), F32))
    return jnp.concatenate(parts, axis=0)


def _unpack_small(packed):
    out, r = {}, 0
    for name, n in SMALL:
        k = -(-n // 128)
        out[name] = packed[r:r + k].reshape(-1)[:n]
        r += k
    return out


def _pad_heads(w, axis, per_head, lo, hi):
    shape = w.shape
    w = w.reshape(shape[:axis] + (N_HEADS, per_head) + shape[axis + 1:])
    w = lax.slice_in_dim(w, lo, hi, axis=axis + 1)
    pad = [(0, 0)] * w.ndim
    pad[axis + 1] = (0, HP - (hi - lo))
    w = jnp.pad(w, pad)
    return w.reshape(shape[:axis] + (N_HEADS * HP,) + shape[axis + 1:])


def _unpad_heads(w, axis, keep):
    shape = w.shape
    w = w.reshape(shape[:axis] + (N_HEADS, HP) + shape[axis + 1:])
    return lax.slice_in_dim(w, 0, keep, axis=axis + 1)


def _split_w_in(w):
    o = [0]
    for s in IN_SPLITS:
        o.append(o[-1] + s)
    return [w[:, o[i]:o[i + 1]] for i in range(len(IN_SPLITS))]


def _pad_w_in(w):
    cq, ckv, kpe, z, xbc, dtf, dtb, ga, gb = _split_w_in(w)
    kpe_pad = jnp.pad(kpe, ((0, 0), (QK_NOPE, HP - QK_HEAD)))
    dt_pad = jnp.pad(jnp.concatenate([dtf, dtb], axis=1), ((0, 0), (0, HP - 2 * SSM_HEADS)))
    return jnp.concatenate([z, ga, gb, xbc, cq, ckv, kpe_pad, dt_pad], axis=1)


def _unpad_w_in(g):
    z, ga, gb, xbc = g[:, U_Z:U_GA], g[:, U_GA:U_GB], g[:, U_GB:U_XBC], g[:, U_XBC:U_SMALL]
    s = g[:, U_SMALL:]
    cq, ckv = s[:, S_CQ:S_CKV], s[:, S_CKV:S_KPE]
    kpe = s[:, S_KPE + QK_NOPE:S_KPE + QK_HEAD]
    dtf, dtb = s[:, S_DT:S_DT + SSM_HEADS], s[:, S_DT + SSM_HEADS:S_DT + 2 * SSM_HEADS]
    return jnp.concatenate([cq, ckv, kpe, z, xbc, dtf, dtb, ga, gb], axis=1)


def _lanes128(parts):
    row = jnp.concatenate([p.reshape(-1) for p in parts])
    return jnp.pad(row, (0, HP - row.shape[0])).reshape(1, HP)


FF_TILE = D_FF // 2


def _ffn_fwd(x, g, wg, wu, wd, tag):
    h = _rms_fwd(x, g, name=tag + "_norm")
    gate, up, act = _mm([h], [wg, wu], name=tag + "_up", out_dtypes=(F32, F32, BF), tm=512, tn=FF_TILE,
                        epilogue=lambda a, b: (a, b, _silu(a) * b))
    out = _mm([act], [wd], name=tag + "_down", extras=[x], epilogue=lambda acc, r: (r + 0.5 * acc,))
    return out, (h, gate, up, act)


def _ffn_bwd(dout, dout_bf, x, g, wg, wu, wd, saved, tag):
    h, gate, up, act = saved
    dgate, dup = _mm([dout_bf], [wd], name=tag + "_down_dx", tb=True, extras=[gate, up], out_dtypes=(BF, BF),
                     tm=512, tn=FF_TILE, epilogue=lambda acc, a, b: (0.5 * acc * b * _dsilu(a), 0.5 * acc * _silu(a)))
    dwd = _mm([act], [dout_bf], name=tag + "_down_dw", ta=True, tm=FF_TILE, tk=1024, epilogue=lambda acc: (0.5 * acc,))
    dwg, dwu = _mm([h], [dgate, dup], name=tag + "_up_dw", ta=True, out_dtypes=(F32, F32), tm=512, tn=FF_TILE, tk=1024)
    dh = _mm([dgate, dup], [wg, wu], name=tag + "_up_dx", tb=True)
    dx, dx_bf, dg = _rms_bwd(dh, x, g, name=tag + "_norm_bwd", add=dout, out_dtypes=(F32, BF))
    return dx, dx_bf, dg, dwg, dwu, dwd


KPE_BLK = (U_SMALL + S_KPE) // HP
SMALL_BLK = U_SMALL // SMALL_W


def _local_step(x, pos_col, target, W, P):
    T = x.shape[0]
    sig = jax.nn.sigmoid
    x1, ffn1 = _ffn_fwd(x, P["ffn1_norm"], W["wg1"], W["wu1"], W["wd1"], "ffn1")
    h = _rms_fwd(x1, P["mix_norm"], name="mix_norm")
    u = _mm([h], [W["w_in"]], name="in_proj", tn=1152)
    cqn = _rms_fwd(u, P["q_a_norm"], name="q_a_norm", blk_w=SMALL_W, blk_idx=SMALL_BLK, off=S_CQ, width=Q_LORA)
    ckvn = _rms_fwd(u, P["kv_a_norm"], name="kv_a_norm", blk_w=SMALL_W, blk_idx=SMALL_BLK, off=S_CKV, width=KV_LORA)
    q_raw = _mm([cqn], [W["wq"]], name="q_proj")
    k_raw, v = _mm([ckvn], [W["wk"], W["wv"]], name="kv_proj", out_dtypes=(F32, BF))
    rc, rs = _rope_tables(pos_col, P["freq"])
    q = _qk_prep_fwd(q_raw, None, P["q_head_norm"], rc, rs, name="q_prep", out_scale=Q_SCALE)
    k = _qk_prep_fwd(k_raw, u, P["k_head_norm"], rc, rs, name="k_prep", kpe_blk=KPE_BLK)
    o, lse = _attn_fwd(q, k, v)
    pre, act = _conv_fwd(u, P["conv_w8"], P["conv_b"])
    scan_cols, scan_rows = _ssd_prep(u, P["dt_bias8"], P["a_log8"])
    y_f, st_f = _ssd_fwd(act, scan_cols, scan_rows, rev=False, name="ssd_fwd_f")
    y_b, st_b = _ssd_fwd(act, scan_cols, scan_rows, rev=True, name="ssd_fwd_b")
    ysum, m = _ssm_combine_fwd(y_f, y_b, act, u, P["d_skip_lanes"], P["ssm_norm"])
    ab = _mm([o], [W["pa"]], name="attn_branch")
    mb, merged = _mm([m], [W["pb"]], name="ssm_branch", extras=[ab, u, u], extra_offs=(0, U_GA, U_GB), out_dtypes=(F32, BF),
                     epilogue=lambda acc, a, ga, gb: (acc, sig(ga) * a + sig(gb) * acc))
    x2 = _mm([merged], [W["wo"]], name="out_proj", extras=[x1], epilogue=lambda acc, r: (r + acc,))
    y, ffn2 = _ffn_fwd(x2, P["ffn2_norm"], W["wg2"], W["wu2"], W["wd2"], "ffn2")
    dy, dy_bf, loss = _loss_head(y, target)
    dx2, dx2_bf, dg_ffn2, dwg2, dwu2, dwd2 = _ffn_bwd(dy, dy_bf, x2, P["ffn2_norm"], W["wg2"], W["wu2"], W["wd2"], ffn2,
                                                      "ffn2")

    def gate_bwd(dmrg, a, b, ga, gb):
        sa, sb = sig(ga), sig(gb)
        return dmrg * sa, dmrg * sb, dmrg * a * sa * (1.0 - sa), dmrg * b * sb * (1.0 - sb)

    dab, dmb, dga, dgb = _mm([dx2_bf], [W["wo"]], name="out_proj_dx", tb=True, extras=[ab, mb, u, u],
                             extra_offs=(0, 0, U_GA, U_GB), out_dtypes=(BF,) * 4, epilogue=gate_bwd)
    dwo = _mm([merged], [dx2_bf], name="out_proj_dw", ta=True)
    dpa = _mm([o], [dab], name="attn_branch_dw", ta=True)
    do = _mm([dab], [W["pa"]], name="attn_branch_dx", tb=True)
    dpb = _mm([m], [dmb], name="ssm_branch_dw", ta=True)
    dm = _mm([dmb], [W["pb"]], name="ssm_branch_dx", tb=True)
    dyssd, dz, dxs_skip, dg_ssm, dskip = _ssm_combine_bwd(dm, ysum, act, u, P["d_skip_lanes"], P["ssm_norm"])
    dxs_f, db_f, dc_f, dsel_f, dtot_f = _ssd_bwd(act, scan_cols, scan_rows, st_f, dyssd, rev=False, name="ssd_bwd_f")
    dxs_b, db_b, dc_b, dsel_b, dtot_b = _ssd_bwd(act, scan_cols, scan_rows, st_b, dyssd, rev=True, name="ssd_bwd_b")
    ddt, dalog, dbias = _ssd_prep_bwd(u, P["dt_bias8"], P["a_log8"], dsel_f, dtot_f, dsel_b, dtot_b)
    dxbc, dconv = [], []
    for tag, col0, parts in (("x", 0, [dxs_f, dxs_b, dxs_skip]), ("b", D_INNER, [db_f, db_b]),
                             ("c", D_INNER + SSM_GROUPS * D_STATE, [dc_f, dc_b])):
        dpre = _conv_dpre(parts, pre, col0, name="conv_dpre_" + tag)
        dxp, dwp = _conv_bwd(dpre, u, P["conv_w8"], col0, name="conv_bwd_" + tag)
        dxbc.append(dxp)
        dconv.append(dwp)
    dconv = jnp.concatenate(dconv, axis=1)
    dq, dk, dv = _attn_bwd(q, k, v, do, o, lse)
    dq_raw, dg_qh = _qk_prep_bwd(dq, q_raw, None, P["q_head_norm"], rc, rs, name="q_prep_bwd", in_scale=ATTN_SCALE)
    dk_raw, dg_kh, dkpe = _qk_prep_bwd(dk, k_raw, u, P["k_head_norm"], rc, rs, name="k_prep_bwd", kpe_blk=KPE_BLK,
                                       in_scale=1.0 / LOG2E)
    dwq = _mm([cqn], [dq_raw], name="q_proj_dw", ta=True)
    dcqn = _mm([dq_raw], [W["wq"]], name="q_proj_dx", tb=True)
    dwk, dwv = _mm([ckvn], [dk_raw, dv], name="kv_proj_dw", ta=True, out_dtypes=(F32, F32))
    dckvn = _mm([dk_raw, dv], [W["wk"], W["wv"]], name="kv_proj_dx", tb=True)
    dcq, dg_qa = _rms_bwd(dcqn, u, P["q_a_norm"], name="q_a_norm_bwd", blk_w=SMALL_W, blk_idx=SMALL_BLK, off=S_CQ,
                          width=Q_LORA, out_dtypes=(BF,))
    dckv, dg_kva = _rms_bwd(dckvn, u, P["kv_a_norm"], name="kv_a_norm_bwd", blk_w=SMALL_W, blk_idx=SMALL_BLK,
                            off=S_CKV, width=KV_LORA, out_dtypes=(BF,))
    du = jnp.concatenate([dz, dga, dgb] + dxbc + [dcq, dckv, dkpe.astype(BF), ddt.astype(BF)], axis=1)
    dw_in = _mm([h], [du], name="in_proj_dw", ta=True, tn=1152)
    dh = _mm([du], [W["w_in"]], name="in_proj_dx", tb=True)
    dx1, dx1_bf, dg_mix = _rms_bwd(dh, x1, P["mix_norm"], name="mix_norm_bwd", add=dx2, out_dtypes=(F32, BF))
    dx, _, dg_ffn1, dwg1, dwu1, dwd1 = _ffn_bwd(dx1, dx1_bf, x, P["ffn1_norm"], W["wg1"], W["wu1"], W["wd1"], ffn1, "ffn1")
    dW = dict(wg1=dwg1, wu1=dwu1, wd1=dwd1, w_in=dw_in, wq=dwq, wk=dwk, wv=dwv, pa=dpa, pb=dpb, wo=dwo,
              wg2=dwg2, wu2=dwu2, wd2=dwd2)
    dP = dict(ffn1_norm=dg_ffn1[0], mix_norm=dg_mix[0], q_a_norm=dg_qa[0], kv_a_norm=dg_kva[0],
              q_head_norm=dg_qh[0, :QK_HEAD], k_head_norm=dg_kh[0, :QK_HEAD], conv_b=dconv[CONV_WIDTH],
              a_log_fwd=dalog[0, :SSM_HEADS], a_log_bwd=dalog[0, SSM_HEADS:2 * SSM_HEADS],
              dt_bias_fwd=dbias[0, :SSM_HEADS], dt_bias_bwd=dbias[0, SSM_HEADS:2 * SSM_HEADS],
              d_skip=dskip[0].reshape(SSM_GROUPS, HP)[:, :HG], ssm_norm=dg_ssm[0], ffn2_norm=dg_ffn2[0],
              conv_w=dconv[:CONV_WIDTH], loss=loss[0, 0])
    return dx, dW, dP


def _prepare(w, conv_w_full):
    kvb = w["w_kv_b"]
    W = dict(wg1=w["ffn1_w_gate"], wu1=w["ffn1_w_up"], wd1=w["ffn1_w_down"], w_in=_pad_w_in(w["w_in"]),
             wq=_pad_heads(w["w_q_b"], 1, QK_HEAD, 0, QK_HEAD),
             wk=_pad_heads(kvb, 1, QK_NOPE + V_HEAD, 0, QK_NOPE),
             wv=_pad_heads(kvb, 1, QK_NOPE + V_HEAD, QK_NOPE, QK_NOPE + V_HEAD),
             pa=_pad_heads(w["w_attn_branch"], 0, V_HEAD, 0, V_HEAD), pb=w["w_ssm_branch"], wo=w["w_out"],
             wg2=w["ffn2_w_gate"], wu2=w["ffn2_w_up"], wd2=w["ffn2_w_down"])
    inv_freq = [1.0 / (ROPE_BASE ** (j / QK_ROPE)) for j in range(0, QK_ROPE, 2)]
    freq = [0.0] * QK_NOPE + inv_freq + inv_freq + [0.0] * (HP - QK_HEAD)
    P = {n: w[n] for n in ("ffn1_norm", "mix_norm", "q_a_norm", "kv_a_norm", "ssm_norm", "ffn2_norm", "conv_b")}
    P.update(q_head_norm=_lanes128([w["q_head_norm"]]), k_head_norm=_lanes128([w["k_head_norm"]]),
             conv_w8=jnp.pad(conv_w_full, ((0, 8 - CONV_WIDTH), (0, 0))),
             dt_bias8=jnp.broadcast_to(_lanes128([w["dt_bias_fwd"], w["dt_bias_bwd"]]), (8, HP)),
             a_log8=jnp.broadcast_to(_lanes128([w["a_log_fwd"], w["a_log_bwd"]]), (8, HP)),
             d_skip_lanes=jnp.repeat(w["d_skip"].reshape(-1), PH).reshape(1, D_INNER),
             freq=jnp.asarray(freq, F32).reshape(1, HP))
    return W, P


def _unprepare(dW):
    dkvb = jnp.concatenate([_unpad_heads(dW["wk"], 1, QK_NOPE), _unpad_heads(dW["wv"], 1, V_HEAD)], axis=2)
    return dict(ffn1_w_gate=dW["wg1"], ffn1_w_up=dW["wu1"], ffn1_w_down=dW["wd1"], w_in=_unpad_w_in(dW["w_in"]),
                w_q_b=_unpad_heads(dW["wq"], 1, QK_HEAD).reshape(Q_LORA, N_HEADS * QK_HEAD),
                w_kv_b=dkvb.reshape(KV_LORA, N_HEADS * (QK_NOPE + V_HEAD)),
                w_attn_branch=_unpad_heads(dW["pa"], 0, V_HEAD).reshape(N_HEADS * V_HEAD, D_MODEL),
                w_ssm_branch=dW["pb"], w_out=dW["wo"],
                ffn2_w_gate=dW["wg2"], ffn2_w_up=dW["wu2"], ffn2_w_down=dW["wd2"])


def kernel(x, positions, ffn1_norm, ffn1_w_gate, ffn1_w_up, ffn1_w_down, mix_norm, w_in, q_a_norm, w_q_b, kv_a_norm, w_kv_b, q_head_norm, k_head_norm, conv_w, conv_b, a_log_fwd, a_log_bwd, dt_bias_fwd, dt_bias_bwd, d_skip, ssm_norm, w_attn_branch, w_ssm_branch, w_out, ffn2_norm, ffn2_w_gate, ffn2_w_up, ffn2_w_down, loss_target, m_ffn1_norm, m_ffn1_w_gate, m_ffn1_w_up, m_ffn1_w_down, m_mix_norm, m_w_in, m_q_a_norm, m_w_q_b, m_kv_a_norm, m_w_kv_b, m_q_head_norm, m_k_head_norm, m_conv_w, m_conv_b, m_a_log_fwd, m_a_log_bwd, m_dt_bias_fwd, m_dt_bias_bwd, m_d_skip, m_ssm_norm, m_w_attn_branch, m_w_ssm_branch, m_w_out, m_ffn2_norm, m_ffn2_w_gate, m_ffn2_w_up, m_ffn2_w_down, v_ffn1_norm, v_ffn1_w_gate, v_ffn1_w_up, v_ffn1_w_down, v_mix_norm, v_w_in, v_q_a_norm, v_w_q_b, v_kv_a_norm, v_w_kv_b, v_q_head_norm, v_k_head_norm, v_conv_w, v_conv_b, v_a_log_fwd, v_a_log_bwd, v_dt_bias_fwd, v_dt_bias_bwd, v_d_skip, v_ssm_norm, v_w_attn_branch, v_w_ssm_branch, v_w_out, v_ffn2_norm, v_ffn2_w_gate, v_ffn2_w_up, v_ffn2_w_down):
    given = dict(locals())
    T = x.shape[1]
    packed_names = [name for name, _, _ in PACKED]

    def two_d(a):
        return a.reshape(a.shape[1], -1) if a.ndim > 2 else a

    w_loc = {n: two_d(given[n]) for n in WEIGHTS}
    wb = _pack({n: w_loc[n].astype(BF) for n in packed_names})
    wf = jnp.pad(w_loc["conv_w"], ((0, 8 - CONV_WIDTH), (0, 0)))
    gb, gf = _gather_chips(wb, wf)
    per_chip = [_unpack(gb[j]) for j in range(N_CHIPS)]
    full = {n: jnp.concatenate([per_chip[j][n] for j in range(N_CHIPS)], axis=axis) for n, _, axis in PACKED}
    conv_w_full = jnp.concatenate([gf[j, :CONV_WIDTH] for j in range(N_CHIPS)], axis=1)
    full.update({n: w_loc[n] for n in WEIGHTS if n not in full and n != "conv_w"})
    W, P = _prepare(full, conv_w_full)
    dx, dW, dP = _local_step(x.reshape(T, D_MODEL), positions.reshape(T, 1).astype(F32), loss_target.reshape(T, D_MODEL), W, P)
    g_full = _unprepare(dW)
    slots = []
    for j in range(N_CHIPS):
        shards = {}
        for n, shape, axis in PACKED:
            size = shape[axis] // N_CHIPS
            shards[n] = lax.slice_in_dim(g_full[n], j * size, (j + 1) * size, axis=axis).astype(BF)
        slots.append(_pack(shards))
    gp = jnp.stack(slots)
    core = lax.axis_index("c").astype(jnp.int32).reshape(1)
    both_cores = _add_halves(gp, _halves_to_sibling(gp), core)
    g_packed = _unpack(_join_halves(_sum_slots(_exchange_chips(both_cores))))
    small = _unpack_small(_allreduce_small(_pack_small(dP)))
    chip = 2 * lax.axis_index("x") + lax.axis_index("y")
    grads = dict(g_packed)
    grads.update({n: small[n].reshape(1, -1) for n, _ in SMALL if n not in ("conv_w", "loss")})
    grads["conv_w"] = lax.dynamic_slice_in_dim(small["conv_w"].reshape(CONV_WIDTH, XBC_DIM), chip * (XBC_DIM // N_CHIPS),
                                               XBC_DIM // N_CHIPS, axis=1)
    out_g, out_d, out_m, out_v = [], [], [], []
    for n in WEIGHTS:
        shape = given[n].shape
        delta, new_m, new_v = _adamw(w_loc[n], grads[n], two_d(given["m_" + n]), two_d(given["v_" + n]), name="adamw_" + n)
        out_g.append(grads[n].reshape(shape))
        out_d.append(delta.reshape(shape))
        out_m.append(new_m.reshape(shape))
        out_v.append(new_v.reshape(shape))
    return (small["loss"].reshape(()), dx.reshape(x.shape), *out_g, *out_d, *out_m, *out_v)
```

```python
import functools
import math

import jax
import jax.numpy as jnp
from jax import lax
from jax.experimental import pallas as pl
from jax.experimental.pallas import tpu as pltpu

BF = jnp.bfloat16
F32 = jnp.float32
HI = lax.Precision.HIGHEST
MESH = pl.DeviceIdType.MESH

D_MODEL = 1024
D_FF = 2816
EPS = 1e-6
N_HEADS = 16
QK_NOPE = 64
QK_ROPE = 32
QK_HEAD = 96
V_HEAD = 64
Q_LORA = 384
KV_LORA = 256
ROPE_BASE = 10000.0
D_INNER = 2048
SSM_HEADS = 32
SSM_GROUPS = 4
D_STATE = 128
CONV_WIDTH = 5
CHUNK = 128
XBC_DIM = 3072
HP = 128
GW = D_INNER // SSM_GROUPS
HG = SSM_HEADS // SSM_GROUPS
PH = 64
U_Z, U_GA, U_GB, U_XBC, U_SMALL = 0, 2048, 3072, 4096, 7168
S_CQ, S_CKV, S_KPE, S_DT, SMALL_W = 0, 384, 640, 768, 896
U_PAD = U_SMALL + SMALL_W
IN_SPLITS = (Q_LORA, KV_LORA, QK_ROPE, D_INNER, XBC_DIM, SSM_HEADS, SSM_HEADS, D_MODEL, D_MODEL)

ADAM_LR = 0.001
ADAM_B1 = 0.9
ADAM_B2 = 0.999
ADAM_EPS = 1e-08
ADAM_WD = 0.01
ADAM_STEP = 10

NN = (((1,), (0,)), ((), ()))
NT = (((1,), (1,)), ((), ()))
TN = (((0,), (0,)), ((), ()))


def _pick(n, pref):
    best = None
    d = 128
    while d <= min(n, pref):
        if n % d == 0:
            best = d
        d += 128
    return best if best is not None else n


def _silu(x):
    return x * jax.nn.sigmoid(x)


def _dsilu(x):
    s = jax.nn.sigmoid(x)
    return s * (1.0 + x * (1.0 - s))


def _softplus(x):
    return jnp.maximum(x, 0.0) + jnp.log(1.0 + jnp.exp(-jnp.abs(x)))


def _mm(As, Bs, *, name, ta=False, tb=False, out_dtypes=(F32,), epilogue=None, extras=(), extra_offs=None,
        tm=1024, tn=512, tk=2048):
    As, Bs, extras = list(As), list(Bs), list(extras)
    a0, b0 = As[0], Bs[0]
    M, K = (a0.shape[1], a0.shape[0]) if ta else a0.shape
    N = b0.shape[0] if tb else b0.shape[1]
    tm, tn, tk = _pick(M, tm), _pick(N, tn), _pick(K, tk)
    nk = K // tk
    n_a, n_b, n_e, n_o = len(As), len(Bs), len(extras), len(out_dtypes)
    n_acc = (n_b if n_a == 1 else 1) if nk > 1 else 0
    if extra_offs is None:
        extra_offs = (0,) * n_e
    dn = (((0,) if ta else (1,), (1,) if tb else (0,)), ((), ()))
    bytes_a = sum(a.size * a.dtype.itemsize for a in As)
    bytes_b = sum(b.size * b.dtype.itemsize for b in Bs)
    n_outer = (N // tn) * bytes_a + bytes_b < (M // tm) * bytes_b + bytes_a

    def products(a_refs, b_refs):
        if n_a == 1:
            a = a_refs[0][...].astype(BF)
            return [lax.dot_general(a, b[...].astype(BF), dn, preferred_element_type=F32) for b in b_refs]
        total = None
        for a, b in zip(a_refs, b_refs):
            p = lax.dot_general(a[...].astype(BF), b[...].astype(BF), dn, preferred_element_type=F32)
            total = p if total is None else total + p
        return [total]

    def finish(accs, e_refs, o_refs):
        ex = [e[...] for e in e_refs]
        outs = epilogue(*accs, *ex) if epilogue is not None else tuple(accs)
        for o_ref, val in zip(o_refs, outs):
            o_ref[...] = val.astype(o_ref.dtype)

    def body(*refs):
        a_refs, b_refs = refs[:n_a], refs[n_a:n_a + n_b]
        e_refs = refs[n_a + n_b:n_a + n_b + n_e]
        o_refs = refs[n_a + n_b + n_e:n_a + n_b + n_e + n_o]
        acc_refs = refs[n_a + n_b + n_e + n_o:]
        if nk == 1:
            finish(products(a_refs, b_refs), e_refs, o_refs)
            return
        k = pl.program_id(2)

        @pl.when(k == 0)
        def _():
            for acc in acc_refs:
                acc[...] = jnp.zeros_like(acc)

        for acc, p in zip(acc_refs, products(a_refs, b_refs)):
            acc[...] += p

        @pl.when(k == nk - 1)
        def _():
            finish([acc[...] for acc in acc_refs], e_refs, o_refs)

    def at(f):
        return (lambda j, i, k: f(i, j, k)) if n_outer else f

    a_spec = pl.BlockSpec((tk, tm), at(lambda i, j, k: (k, i))) if ta else pl.BlockSpec((tm, tk), at(lambda i, j, k: (i, k)))
    b_spec = pl.BlockSpec((tn, tk), at(lambda i, j, k: (j, k))) if tb else pl.BlockSpec((tk, tn), at(lambda i, j, k: (k, j)))
    e_specs = [pl.BlockSpec((tm, tn), at(functools.partial(lambda i, j, k, o: (i, j + o), o=off // tn))) for off in extra_offs]
    for off in extra_offs:
        assert off % tn == 0
    outs = pl.pallas_call(
        body, name=name,
        out_shape=tuple(jax.ShapeDtypeStruct((M, N), dt) for dt in out_dtypes),
        grid=(N // tn, M // tm, nk) if n_outer else (M // tm, N // tn, nk),
        in_specs=[a_spec] * n_a + [b_spec] * n_b + e_specs,
        out_specs=tuple(pl.BlockSpec((tm, tn), at(lambda i, j, k: (i, j))) for _ in out_dtypes),
        scratch_shapes=[pltpu.VMEM((tm, tn), F32)] * n_acc,
        compiler_params=pltpu.CompilerParams(dimension_semantics=("parallel", "parallel", "arbitrary")),
    )(*As, *Bs, *extras)
    return outs[0] if n_o == 1 else outs


def _rms_fwd(x, g, *, name, blk_w=None, blk_idx=0, off=0, width=None, out_dtype=BF):
    T = x.shape[0]
    blk_w = x.shape[1] if blk_w is None else blk_w
    width = blk_w if width is None else width
    tt = _pick(T, 512)

    def body(x_ref, g_ref, o_ref):
        xf = x_ref[:, off:off + width]
        r = lax.rsqrt(jnp.mean(xf * xf, axis=-1, keepdims=True) + EPS)
        o_ref[...] = (xf * r * g_ref[...]).astype(o_ref.dtype)

    return pl.pallas_call(
        body, name=name, out_shape=jax.ShapeDtypeStruct((T, width), out_dtype), grid=(T // tt,),
        in_specs=[pl.BlockSpec((tt, blk_w), lambda i: (i, blk_idx)), pl.BlockSpec((1, width), lambda i: (0, 0))],
        out_specs=pl.BlockSpec((tt, width), lambda i: (i, 0)),
    )(x, g)


def _rms_bwd(dy, x, g, *, name, blk_w=None, blk_idx=0, off=0, width=None, add=None, out_dtypes=(F32,)):
    T = x.shape[0]
    blk_w = x.shape[1] if blk_w is None else blk_w
    width = blk_w if width is None else width
    tt = _pick(T, 512)
    has_add = add is not None
    n_dx = len(out_dtypes)

    def body(*refs):
        dy_ref, x_ref, g_ref = refs[:3]
        dx_refs, dg_ref = refs[3 + has_add:3 + has_add + n_dx], refs[-1]
        xf = x_ref[:, off:off + width]
        d = dy_ref[...].astype(F32)
        r = lax.rsqrt(jnp.mean(xf * xf, axis=-1, keepdims=True) + EPS)
        gd = d * g_ref[...]
        dx = r * gd - xf * (r * r * r) * jnp.mean(gd * xf, axis=-1, keepdims=True)
        if has_add:
            dx = dx + refs[3][...]
        for dx_ref in dx_refs:
            dx_ref[...] = dx.astype(dx_ref.dtype)

        @pl.when(pl.program_id(0) == 0)
        def _():
            dg_ref[...] = jnp.zeros_like(dg_ref)

        dg_ref[...] += jnp.broadcast_to(jnp.sum(d * xf * r, axis=0, keepdims=True), dg_ref.shape)

    row = pl.BlockSpec((tt, width), lambda i: (i, 0))
    in_specs = [row, pl.BlockSpec((tt, blk_w), lambda i: (i, blk_idx)), pl.BlockSpec((1, width), lambda i: (0, 0))]
    args = [dy, x, g]
    if has_add:
        in_specs.append(row)
        args.append(add)
    return pl.pallas_call(
        body, name=name,
        out_shape=tuple(jax.ShapeDtypeStruct((T, width), dt) for dt in out_dtypes) + (jax.ShapeDtypeStruct((8, width), F32),),
        grid=(T // tt,), in_specs=in_specs,
        out_specs=(row,) * n_dx + (pl.BlockSpec((8, width), lambda i: (0, 0)),),
        compiler_params=pltpu.CompilerParams(dimension_semantics=("arbitrary",)),
    )(*args)


def _rope_tables(pos_col, freq_lane):
    T = pos_col.shape[0]
    tt = _pick(T, 512)

    def body(p_ref, f_ref, c_ref, s_ref):
        ang = p_ref[...] * f_ref[...]
        lane = lax.broadcasted_iota(jnp.int32, ang.shape, 1)
        c_ref[...] = jnp.where(lane < QK_HEAD, jnp.cos(ang), 0.0)
        sn = jnp.sin(ang)
        s_ref[...] = jnp.where((lane >= QK_NOPE) & (lane < QK_NOPE + 16), -sn,
                               jnp.where((lane >= QK_NOPE + 16) & (lane < QK_HEAD), sn, 0.0))

    return pl.pallas_call(
        body, name="rope_tables", out_shape=(jax.ShapeDtypeStruct((T, HP), F32),) * 2, grid=(T // tt,),
        in_specs=[pl.BlockSpec((tt, 1), lambda i: (i, 0)), pl.BlockSpec((1, HP), lambda i: (0, 0))],
        out_specs=(pl.BlockSpec((tt, HP), lambda i: (i, 0)),) * 2,
    )(pos_col, freq_lane)


def _swap_rope_halves(n):
    lane = lax.broadcasted_iota(jnp.int32, n.shape, 1)
    lo = (lane >= QK_NOPE) & (lane < QK_NOPE + 16)
    hi = (lane >= QK_NOPE + 16) & (lane < QK_HEAD)
    return jnp.where(lo, pltpu.roll(n, HP - 16, 1), jnp.where(hi, pltpu.roll(n, 16, 1), 0.0))


def _qk_prep_fwd(raw, kpe, gain, C, S, *, name, kpe_blk=0, out_scale=1.0):
    T = raw.shape[0]
    tt = _pick(T, 256)
    has_kpe = kpe is not None

    def body(*refs):
        if has_kpe:
            raw_ref, kpe_ref, g_ref, c_ref, s_ref, o_ref = refs
        else:
            raw_ref, g_ref, c_ref, s_ref, o_ref = refs
        for h in range(N_HEADS):
            hs = slice(HP * h, HP * (h + 1))
            xr = raw_ref[:, hs] + kpe_ref[...] if has_kpe else raw_ref[:, hs]
            r = lax.rsqrt(jnp.sum(xr * xr, axis=-1, keepdims=True) * (1.0 / QK_HEAD) + EPS)
            n = xr * r * g_ref[...]
            o_ref[:, hs] = ((n * c_ref[...] + _swap_rope_halves(n) * s_ref[...]) * out_scale).astype(o_ref.dtype)

    heads = pl.BlockSpec((tt, N_HEADS * HP), lambda i: (i, 0))
    shared = pl.BlockSpec((tt, HP), lambda i: (i, 0))
    kpe_spec = pl.BlockSpec((tt, HP), lambda i: (i, kpe_blk))
    in_specs = [heads] + ([kpe_spec] if has_kpe else []) + [pl.BlockSpec((1, HP), lambda i: (0, 0)), shared, shared]
    args = [raw] + ([kpe] if has_kpe else []) + [gain, C, S]
    return pl.pallas_call(
        body, name=name, out_shape=jax.ShapeDtypeStruct(raw.shape, BF), grid=(T // tt,),
        in_specs=in_specs, out_specs=heads,
    )(*args)


def _qk_prep_bwd(dout, raw, kpe, gain, C, S, *, name, kpe_blk=0, in_scale=1.0):
    T = raw.shape[0]
    tt = _pick(T, 256)
    has_kpe = kpe is not None

    def body(*refs):
        if has_kpe:
            d_ref, raw_ref, kpe_ref, g_ref, c_ref, s_ref, dx_ref, dg_ref, dkpe_ref = refs
        else:
            d_ref, raw_ref, g_ref, c_ref, s_ref, dx_ref, dg_ref = refs
        dg = jnp.zeros((1, HP), F32)
        dkpe = jnp.zeros((tt, HP), F32)
        for h in range(N_HEADS):
            hs = slice(HP * h, HP * (h + 1))
            xr = raw_ref[:, hs] + kpe_ref[...] if has_kpe else raw_ref[:, hs]
            d = d_ref[:, hs].astype(F32) * in_scale
            r = lax.rsqrt(jnp.sum(xr * xr, axis=-1, keepdims=True) * (1.0 / QK_HEAD) + EPS)
            dn = d * c_ref[...] + _swap_rope_halves(d * s_ref[...])
            gd = dn * g_ref[...]
            dx = r * gd - xr * (r * r * r) * (jnp.sum(gd * xr, axis=-1, keepdims=True) * (1.0 / QK_HEAD))
            dx_ref[:, hs] = dx.astype(dx_ref.dtype)
            dg = dg + jnp.sum(dn * xr * r, axis=0, keepdims=True)
            dkpe = dkpe + dx

        @pl.when(pl.program_id(0) == 0)
        def _():
            dg_ref[...] = jnp.zeros_like(dg_ref)

        dg_ref[...] += jnp.broadcast_to(dg, dg_ref.shape)
        if has_kpe:
            dkpe_ref[...] = dkpe

    heads = pl.BlockSpec((tt, N_HEADS * HP), lambda i: (i, 0))
    shared = pl.BlockSpec((tt, HP), lambda i: (i, 0))
    kpe_spec = pl.BlockSpec((tt, HP), lambda i: (i, kpe_blk))
    in_specs = [heads, heads] + ([kpe_spec] if has_kpe else []) + [pl.BlockSpec((1, HP), lambda i: (0, 0)), shared, shared]
    args = [dout, raw] + ([kpe] if has_kpe else []) + [gain, C, S]
    out_shape = [jax.ShapeDtypeStruct(raw.shape, BF), jax.ShapeDtypeStruct((8, HP), F32)]
    out_specs = [heads, pl.BlockSpec((8, HP), lambda i: (0, 0))]
    if has_kpe:
        out_shape.append(jax.ShapeDtypeStruct((T, HP), F32))
        out_specs.append(shared)
    return pl.pallas_call(
        body, name=name, out_shape=tuple(out_shape), grid=(T // tt,),
        in_specs=in_specs, out_specs=tuple(out_specs),
        compiler_params=pltpu.CompilerParams(dimension_semantics=("arbitrary",)),
    )(*args)


ATTN_SCALE = 1.0 / math.sqrt(QK_HEAD)
LOG2E = 1.0 / math.log(2.0)
Q_SCALE = ATTN_SCALE * LOG2E


def _attn_fwd(q, k, v):
    T = q.shape[0]
    tq = _pick(T, 256)

    def body(q_ref, k_ref, v_ref, o_ref, lse_ref):
        s = lax.dot_general(q_ref[...], k_ref[...], NT, preferred_element_type=F32)
        m = jnp.max(s, axis=-1, keepdims=True)
        p = jnp.exp2(s - m)
        l = jnp.sum(p, axis=-1, keepdims=True)
        o = jnp.dot(p.astype(BF), v_ref[...], preferred_element_type=F32)
        o_ref[...] = o / l
        lse_ref[...] = jnp.broadcast_to(m + jnp.log2(l), lse_ref.shape)

    qs = pl.BlockSpec((tq, HP), lambda h, i: (i, h))
    kv = pl.BlockSpec((T, HP), lambda h, i: (0, h))
    return pl.pallas_call(
        body, name="attn_fwd", out_shape=(jax.ShapeDtypeStruct(q.shape, F32),) * 2, grid=(N_HEADS, T // tq),
        in_specs=[qs, kv, kv], out_specs=(qs, qs),
        compiler_params=pltpu.CompilerParams(dimension_semantics=("parallel", "parallel")),
    )(q, k, v)


def _attn_bwd(q, k, v, do, o, lse):
    T = q.shape[0]
    tb = _pick(T, 512)
    nb = T // tb

    def body(q_ref, k_ref, v_ref, do_ref, o_ref, lse_ref, dq_ref, dk_ref, dv_ref, delta_scr, dob_scr):
        dq_ref[...] = jnp.zeros_like(dq_ref)

        def per_q_tile(i, carry):
            qs = pl.ds(pl.multiple_of(i * tb, tb), tb)
            doi = do_ref[qs, :]
            delta_scr[qs, :] = jnp.sum(doi * o_ref[qs, :], axis=-1, keepdims=True)
            dob_scr[qs, :] = doi.astype(BF)
            return carry

        lax.fori_loop(0, nb, per_q_tile, 0)

        def k_loop(j, carry):
            ks = pl.ds(pl.multiple_of(j * tb, tb), tb)
            kj, vj = k_ref[ks, :], v_ref[ks, :]

            def q_loop(i, acc):
                dk_acc, dv_acc = acc
                qs = pl.ds(pl.multiple_of(i * tb, tb), tb)
                qi = q_ref[qs, :]
                delta = delta_scr[qs, :]
                dob = dob_scr[qs, :]
                s = lax.dot_general(qi, kj, NT, preferred_element_type=F32)
                p = jnp.exp2(s - lse_ref[qs, 0:1])
                dp = lax.dot_general(dob, vj, NT, preferred_element_type=F32)
                ds = (p * (dp - delta)).astype(BF)
                dv_acc = dv_acc + lax.dot_general(p.astype(BF), dob, TN, preferred_element_type=F32)
                dk_acc = dk_acc + lax.dot_general(ds, qi, TN, preferred_element_type=F32)
                dq_ref[qs, :] += jnp.dot(ds, kj, preferred_element_type=F32)
                return dk_acc, dv_acc

            zero = jnp.zeros((tb, HP), F32)
            dk_acc, dv_acc = lax.fori_loop(0, nb, q_loop, (zero, zero))
            dk_ref[ks, :] = dk_acc
            dv_ref[ks, :] = dv_acc.astype(dv_ref.dtype)
            return carry

        lax.fori_loop(0, nb, k_loop, 0)

    spec = pl.BlockSpec((T, HP), lambda h: (0, h))
    return pl.pallas_call(
        body, name="attn_bwd",
        out_shape=(jax.ShapeDtypeStruct(q.shape, F32), jax.ShapeDtypeStruct(q.shape, F32), jax.ShapeDtypeStruct(q.shape, BF)),
        grid=(N_HEADS,), in_specs=[spec] * 6, out_specs=(spec,) * 3,
        scratch_shapes=[pltpu.VMEM((T, 1), F32), pltpu.VMEM((T, HP), BF)],
        compiler_params=pltpu.CompilerParams(dimension_semantics=("parallel",), vmem_limit_bytes=2 * 15 * T * HP * 2 + (8 << 20)),
    )(q, k, v, do, o, lse)


CONV_TC = 512
CONV_PAD = CONV_WIDTH // 2


def _halo_specs(tr, col_of):
    r8 = tr // 8
    cur = pl.BlockSpec((tr, CONV_TC), lambda j, i: (i, col_of(j)))
    prev = pl.BlockSpec((8, CONV_TC), lambda j, i: (jnp.maximum(i * r8 - 1, 0), col_of(j)))

    def nxt_map(j, i, n8):
        return (jnp.minimum((i + 1) * r8, n8 - 1), col_of(j))

    return cur, prev, nxt_map


def _with_halo(prev_ref, cur_ref, next_ref, i, n_i):
    prev = jnp.where(i == 0, 0.0, prev_ref[...].astype(F32))
    nxt = jnp.where(i == n_i - 1, 0.0, next_ref[...].astype(F32))
    return jnp.concatenate([prev, cur_ref[...].astype(F32), nxt], axis=0)


def _conv_fwd(u, w8, b):
    T = u.shape[0]
    tr = _pick(T, 512)
    n_i = T // tr
    c0 = U_XBC // CONV_TC
    cur, prev, nxt_map = _halo_specs(tr, lambda j: c0 + j)
    nxt = pl.BlockSpec((8, CONV_TC), functools.partial(nxt_map, n8=T // 8))

    def body(p_ref, c_ref, n_ref, w_ref, b_ref, pre_ref, act_ref):
        i = pl.program_id(1)
        full = _with_halo(p_ref, c_ref, n_ref, i, n_i)
        acc = jnp.broadcast_to(b_ref[...], (tr, CONV_TC))
        for kk in range(CONV_WIDTH):
            acc = acc + full[8 - CONV_PAD + kk:8 - CONV_PAD + kk + tr, :] * w_ref[kk:kk + 1, :]
        pre_ref[...] = acc
        act_ref[...] = _silu(acc)

    out = pl.BlockSpec((tr, CONV_TC), lambda j, i: (i, j))
    return pl.pallas_call(
        body, name="conv_fwd", out_shape=(jax.ShapeDtypeStruct((T, XBC_DIM), F32),) * 2,
        grid=(XBC_DIM // CONV_TC, n_i),
        in_specs=[prev, cur, nxt, pl.BlockSpec((8, CONV_TC), lambda j, i: (0, j)), pl.BlockSpec((1, CONV_TC), lambda j, i: (0, j))],
        out_specs=(out, out),
    )(u, u, u, w8, b)


def _conv_dpre(dacts, pre, col0, *, name):
    T, width = dacts[0].shape
    tt = _pick(T, 512)
    n_d = len(dacts)
    c0 = col0 // CONV_TC

    def body(*refs):
        d = refs[0][...]
        for r in refs[1:n_d]:
            d = d + r[...]
        refs[n_d + 1][...] = d * _dsilu(refs[n_d][...])

    blk = pl.BlockSpec((tt, CONV_TC), lambda j, i: (i, j))
    return pl.pallas_call(
        body, name=name, out_shape=jax.ShapeDtypeStruct((T, width), F32), grid=(width // CONV_TC, T // tt),
        in_specs=[blk] * n_d + [pl.BlockSpec((tt, CONV_TC), lambda j, i: (i, c0 + j))], out_specs=blk,
    )(*dacts, pre)


def _conv_bwd(dpre, u, w8, col0, *, name):
    T, width = dpre.shape
    tr = _pick(T, 512)
    n_i = T // tr
    cd = col0 // CONV_TC
    cx = (U_XBC + col0) // CONV_TC
    d_cur, d_prev, d_nxt_map = _halo_specs(tr, lambda j: j)
    x_cur, x_prev, x_nxt_map = _halo_specs(tr, lambda j: cx + j)
    d_nxt = pl.BlockSpec((8, CONV_TC), functools.partial(d_nxt_map, n8=T // 8))
    x_nxt = pl.BlockSpec((8, CONV_TC), functools.partial(x_nxt_map, n8=T // 8))

    def body(dp_ref, dc_ref, dn_ref, xp_ref, xc_ref, xn_ref, w_ref, dx_ref, dw_ref):
        i = pl.program_id(1)
        dfull = _with_halo(dp_ref, dc_ref, dn_ref, i, n_i)
        xfull = _with_halo(xp_ref, xc_ref, xn_ref, i, n_i)
        dcur = dc_ref[...]
        dx = jnp.zeros((tr, CONV_TC), F32)
        rows = []
        for kk in range(CONV_WIDTH):
            dx = dx + dfull[8 + CONV_PAD - kk:8 + CONV_PAD - kk + tr, :] * w_ref[kk:kk + 1, :]
            rows.append(jnp.sum(dcur * xfull[8 - CONV_PAD + kk:8 - CONV_PAD + kk + tr, :], axis=0, keepdims=True))
        rows.append(jnp.sum(dcur, axis=0, keepdims=True))
        rows.append(jnp.zeros((2, CONV_TC), F32))
        dx_ref[...] = dx.astype(dx_ref.dtype)

        @pl.when(i == 0)
        def _():
            dw_ref[...] = jnp.zeros_like(dw_ref)

        dw_ref[...] += jnp.concatenate(rows, axis=0)

    out = pl.BlockSpec((tr, CONV_TC), lambda j, i: (i, j))
    return pl.pallas_call(
        body, name=name, out_shape=(jax.ShapeDtypeStruct((T, width), BF), jax.ShapeDtypeStruct((8, width), F32)),
        grid=(width // CONV_TC, n_i),
        in_specs=[d_prev, d_cur, d_nxt, x_prev, x_cur, x_nxt, pl.BlockSpec((8, CONV_TC), lambda j, i: (0, cd + j))],
        out_specs=(out, pl.BlockSpec((8, CONV_TC), lambda j, i: (0, j))),
        compiler_params=pltpu.CompilerParams(dimension_semantics=("parallel", "arbitrary")),
    )(dpre, dpre, dpre, u, u, u, w8)


N_HB = 2 * SSM_GROUPS
P_DT, P_CS, P_E, P_W = 0, HP, 2 * HP, 3 * HP
DT_BLK = (U_SMALL + S_DT) // HP


def _tri(rev, transpose=False):
    rows = lax.broadcasted_iota(jnp.int32, (CHUNK, CHUNK), 0)
    cols = lax.broadcasted_iota(jnp.int32, (CHUNK, CHUNK), 1)
    if transpose:
        rows, cols = cols, rows
    return (cols >= rows) if rev else (cols <= rows)


def _ssd_prep(u, bias8, alog8):
    T = u.shape[0]
    nc = T // CHUNK

    def body(dt_ref, bias_ref, a_ref, cols_ref, rows_ref):
        lane = lax.broadcasted_iota(jnp.int32, (CHUNK, HP), 1)
        dt = _softplus(dt_ref[...] + bias_ref[0:1, :])
        da = dt * (-jnp.exp(a_ref[0:1, :]))
        cs_f = jnp.dot(jnp.where(_tri(False), 1.0, 0.0).astype(F32), da, precision=HI, preferred_element_type=F32)
        cs_b = jnp.dot(jnp.where(_tri(True), 1.0, 0.0).astype(F32), da, precision=HI, preferred_element_type=F32)
        cs = jnp.where(lane < SSM_HEADS, cs_f, cs_b)
        tot = jnp.where(lane[0:1] < SSM_HEADS, cs_f[CHUNK - 1:CHUNK, :], cs_b[0:1, :])
        e, w = jnp.exp(cs), jnp.exp(tot - cs)
        tot8 = jnp.broadcast_to(tot, (8, HP))
        etot8 = jnp.exp(tot8)
        for b in range(N_HB):
            down = (HP - HG * b) % HP

            def rolled(v):
                return pltpu.roll(v, down, 1) if down else v

            cols_ref[b, :, P_DT:P_DT + HP] = rolled(dt)
            cs_r = rolled(cs)
            cols_ref[b, :, P_CS:P_CS + HP] = cs_r
            cols_ref[b, :, P_E:P_E + HP] = rolled(e)
            cols_ref[b, :, P_W:P_W + HP] = rolled(w)
            rows_ref[b, 0, 0:8, :] = cs_r.T[0:8, :]
            r8 = lax.broadcasted_iota(jnp.int32, (8, HP), 0)
            rows_ref[b, 0, 8:16, :] = jnp.where(r8 == 0, rolled(tot8), jnp.where(r8 == 1, rolled(etot8), 0.0))

    vec = pl.BlockSpec((8, HP), lambda c: (0, 0))
    return pl.pallas_call(
        body, name="ssd_prep",
        out_shape=(jax.ShapeDtypeStruct((N_HB, T, 4 * HP), F32), jax.ShapeDtypeStruct((N_HB, nc, 16, HP), F32)),
        grid=(nc,), in_specs=[pl.BlockSpec((CHUNK, HP), lambda c: (c, DT_BLK)), vec, vec],
        out_specs=(pl.BlockSpec((N_HB, CHUNK, 4 * HP), lambda c: (0, c, 0)), pl.BlockSpec((N_HB, 1, 16, HP), lambda c: (0, c, 0, 0))),
    )(u, bias8, alog8)


def _ssd_specs(T, rev, bwd):
    nc = T // CHUNK
    fwd_order = (lambda c: nc - 1 - c) if rev else (lambda c: c)
    cm = (lambda c: fwd_order(nc - 1 - c)) if bwd else fwd_order
    hb0 = SSM_GROUPS if rev else 0
    xs = pl.BlockSpec((CHUNK, GW), lambda c, g: (cm(c), g))
    bs = pl.BlockSpec((CHUNK, D_STATE), lambda c, g: (cm(c), D_INNER // D_STATE + g))
    cs = pl.BlockSpec((CHUNK, D_STATE), lambda c, g: (cm(c), (D_INNER + SSM_GROUPS * D_STATE) // D_STATE + g))
    cols = pl.BlockSpec((1, CHUNK, 4 * HP), lambda c, g: (hb0 + g, cm(c), 0))
    rows = pl.BlockSpec((1, 1, 16, HP), lambda c, g: (hb0 + g, cm(c), 0, 0))
    return nc, cm, xs, bs, cs, cols, rows


def _head_terms(cols_ref, rows_ref, hh, incl):
    dt = cols_ref[0, :, P_DT + hh:P_DT + hh + 1]
    col = cols_ref[0, :, P_CS + hh:P_CS + hh + 1]
    e = cols_ref[0, :, P_E + hh:P_E + hh + 1]
    w = cols_ref[0, :, P_W + hh:P_W + hh + 1]
    row = rows_ref[0, 0, hh:hh + 1, :]
    etot = rows_ref[0, 0, 9:10, hh:hh + 1]
    lmat = jnp.where(incl, jnp.exp(col - row), 0.0)
    return dt, col, row, e, w, etot, lmat


def _ssd_fwd(act, cols, rows, *, rev, name):
    T = act.shape[0]
    nc, cm, xs_s, b_s, c_s, cols_s, rows_s = _ssd_specs(T, rev, False)

    def body(x_ref, b_ref, c_ref, cols_ref, rows_ref, y_ref, st_ref, state, xdw):
        c, g = pl.program_id(0), pl.program_id(1)

        @pl.when(c == 0)
        def _():
            state[g] = jnp.zeros((D_STATE, GW), F32)

        incl = _tri(rev)
        bm, cmat = b_ref[...].astype(BF), c_ref[...].astype(BF)
        bm_t = b_ref[...].T.astype(BF)
        cb = lax.dot_general(cmat, bm, NT, preferred_element_type=F32)
        prev_all = state[g]
        st_ref[...] = prev_all
        yo_all = jnp.dot(cmat, prev_all.astype(BF), preferred_element_type=F32)
        for hh in range(HG):
            hs = slice(PH * hh, PH * (hh + 1))
            dt, col, row, e, w, etot, lmat = _head_terms(cols_ref, rows_ref, hh, incl)
            xdt = x_ref[:, hs] * dt
            xdw[:, hs] = (xdt * w).astype(BF)
            yd = jnp.dot((cb * lmat).astype(BF), xdt.astype(BF), preferred_element_type=F32)
            y_ref[:, hs] = yd + yo_all[:, hs] * e
            state[g, :, hs] = prev_all[:, hs] * etot
        state[g] += jnp.dot(bm_t, xdw[...], preferred_element_type=F32)

    return pl.pallas_call(
        body, name=name,
        out_shape=(jax.ShapeDtypeStruct((T, D_INNER), F32), jax.ShapeDtypeStruct((nc * D_STATE, D_INNER), F32)),
        grid=(nc, SSM_GROUPS), in_specs=[xs_s, b_s, c_s, cols_s, rows_s], out_specs=(xs_s, xs_s),
        scratch_shapes=[pltpu.VMEM((SSM_GROUPS, D_STATE, GW), F32), pltpu.VMEM((CHUNK, GW), BF)],
        compiler_params=pltpu.CompilerParams(dimension_semantics=("arbitrary", "arbitrary")),
    )(act, act, act, cols, rows)


def _ssd_bwd(act, cols, rows, states, dy, *, rev, name):
    T = act.shape[0]
    nc, cm, xs_s, b_s, c_s, cols_s, rows_s = _ssd_specs(T, rev, True)

    def body(x_ref, b_ref, c_ref, cols_ref, rows_ref, st_ref, dy_ref, dx_ref, db_ref, dc_ref, dsel_ref, dtot_ref,
             dstate, dye, xdw):
        c, g = pl.program_id(0), pl.program_id(1)

        @pl.when(c == 0)
        def _():
            dstate[g] = jnp.zeros((D_STATE, GW), F32)

        incl, incl_t = _tri(rev), _tri(rev, transpose=True)
        bm, cmat = b_ref[...].astype(BF), c_ref[...].astype(BF)
        cm_t = c_ref[...].T.astype(BF)
        cb = lax.dot_general(cmat, bm, NT, preferred_element_type=F32)
        cb_t = lax.dot_general(bm, cmat, NT, preferred_element_type=F32)
        prev_all, ds_all = st_ref[...], dstate[g]
        pb_all, dsb_all = prev_all.astype(BF), ds_all.astype(BF)
        cp_all = jnp.dot(cmat, pb_all, preferred_element_type=F32)
        bds_all = jnp.dot(bm, dsb_all, preferred_element_type=F32)
        lane = lax.broadcasted_iota(jnp.int32, (1, CHUNK), 1)
        sub = lax.broadcasted_iota(jnp.int32, (CHUNK, 1), 0)
        zero = jnp.zeros((CHUNK, CHUNK), F32)
        dcb, dcs_col, dcs_row, ddt_x, dtot = zero, zero, zero, zero, jnp.zeros((1, CHUNK), F32)
        for hh in range(HG):
            hs = slice(PH * hh, PH * (hh + 1))
            dth, col, row, e, w, etot, lmat = _head_terms(cols_ref, rows_ref, hh, incl)
            x = x_ref[:, hs]
            xdt = x * dth
            mmat = cb * lmat
            mmat_t = cb_t * jnp.where(incl_t, jnp.exp(row - col), 0.0)
            prev, ds_ = prev_all[:, hs], ds_all[:, hs]
            dyh = dy_ref[:, hs]
            dyb = dyh.astype(BF)
            dye[:, hs] = (dyh * e).astype(BF)
            xdw[:, hs] = (xdt * w).astype(BF)
            dcs_h = jnp.sum(dyh * cp_all[:, hs], axis=1, keepdims=True) * e
            dm = lax.dot_general(dyb, xdt.astype(BF), NT, preferred_element_type=F32)
            dxdt = jnp.dot(mmat_t.astype(BF), dyb, preferred_element_type=F32)
            qm = dm * mmat
            dcs_h = dcs_h + jnp.sum(qm, axis=1, keepdims=True)
            dcs_row = dcs_row + jnp.where(sub == hh, jnp.sum(qm, axis=0, keepdims=True), 0.0)
            dcb = dcb + dm * lmat
            bds = bds_all[:, hs] * w
            dxdt = dxdt + bds
            t = jnp.sum(xdt * bds, axis=1, keepdims=True)
            dtot_h = jnp.sum(t) + jnp.sum(ds_ * prev) * etot
            dcs_h = dcs_h - t
            dcs_col = dcs_col + jnp.where(lane == hh, dcs_h, 0.0)
            dtot = dtot + jnp.where(lane == hh, dtot_h, 0.0)
            ddt_x = ddt_x + jnp.where(lane == hh, jnp.sum(dxdt * x, axis=1, keepdims=True), 0.0)
            dx_ref[:, hs] = dxdt * dth
            dstate[g, :, hs] = ds_ * etot
        dye_all, xdw_all = dye[...], xdw[...]
        dstate[g] += jnp.dot(cm_t, dye_all, preferred_element_type=F32)
        dc_ref[...] = (lax.dot_general(dye_all, pb_all, NT, preferred_element_type=F32)
                       + jnp.dot(dcb.astype(BF), bm, preferred_element_type=F32))
        db_ref[...] = (lax.dot_general(xdw_all, dsb_all, NT, preferred_element_type=F32)
                       + jnp.dot(dcb.T.astype(BF), cmat, preferred_element_type=F32))
        dsel_ref[0, :, 0:HP] = ddt_x
        dsel_ref[0, :, HP:2 * HP] = dcs_col - dcs_row.T
        dtot_ref[0, 0] = jnp.broadcast_to(dtot, (8, HP))

    bc_out = pl.BlockSpec((CHUNK, D_STATE), lambda c, g: (cm(c), g))
    return pl.pallas_call(
        body, name=name,
        out_shape=(jax.ShapeDtypeStruct((T, D_INNER), F32), jax.ShapeDtypeStruct((T, SSM_GROUPS * D_STATE), F32),
                   jax.ShapeDtypeStruct((T, SSM_GROUPS * D_STATE), F32), jax.ShapeDtypeStruct((SSM_GROUPS, T, 2 * HP), F32),
                   jax.ShapeDtypeStruct((SSM_GROUPS, nc, 8, HP), F32)),
        grid=(nc, SSM_GROUPS), in_specs=[xs_s, b_s, c_s, cols_s, rows_s, xs_s, xs_s],
        out_specs=(xs_s, bc_out, bc_out, pl.BlockSpec((1, CHUNK, 2 * HP), lambda c, g: (g, cm(c), 0)),
                   pl.BlockSpec((1, 1, 8, HP), lambda c, g: (g, cm(c), 0, 0))),
        scratch_shapes=[pltpu.VMEM((SSM_GROUPS, D_STATE, GW), F32), pltpu.VMEM((CHUNK, GW), BF), pltpu.VMEM((CHUNK, GW), BF)],
        compiler_params=pltpu.CompilerParams(dimension_semantics=("arbitrary", "arbitrary")),
    )(act, act, act, cols, rows, states, dy)


def _ssd_prep_bwd(u, bias8, alog8, dsel_f, dtot_f, dsel_b, dtot_b):
    T = u.shape[0]
    nc = T // CHUNK

    def body(dt_ref, bias_ref, a_ref, sf_ref, tf_ref, sb_ref, tb_ref, ddt_ref, da_ref, dbias_ref):
        @pl.when(pl.program_id(0) == 0)
        def _():
            da_ref[...] = jnp.zeros_like(da_ref)
            dbias_ref[...] = jnp.zeros_like(dbias_ref)

        lane = lax.broadcasted_iota(jnp.int32, (CHUNK, HP), 1)
        pre = dt_ref[...] + bias_ref[0:1, :]
        dt = _softplus(pre)
        a = -jnp.exp(a_ref[0:1, :])
        ddt_x, dcs, dtot = jnp.zeros((CHUNK, HP), F32), jnp.zeros((CHUNK, HP), F32), jnp.zeros((8, HP), F32)
        for b in range(N_HB):
            s_ref, t_ref, g = (sf_ref, tf_ref, b) if b < SSM_GROUPS else (sb_ref, tb_ref, b - SSM_GROUPS)
            mine = (lane >= HG * b) & (lane < HG * (b + 1))

            def up(v):
                return pltpu.roll(v, HG * b, 1) if b else v

            ddt_x = ddt_x + jnp.where(mine, up(s_ref[g, :, 0:HP]), 0.0)
            dcs = dcs + jnp.where(mine, up(s_ref[g, :, HP:2 * HP]), 0.0)
            dtot = dtot + jnp.where(mine[0:8], up(t_ref[g, 0]), 0.0)
        tri_f = jnp.where(_tri(False, transpose=True), 1.0, 0.0).astype(F32)
        tri_b = jnp.where(_tri(True, transpose=True), 1.0, 0.0).astype(F32)
        dda = jnp.where(lane < SSM_HEADS, jnp.dot(tri_f, dcs, precision=HI, preferred_element_type=F32),
                        jnp.dot(tri_b, dcs, precision=HI, preferred_element_type=F32)) + dtot[0:1, :]
        dpre = (ddt_x + dda * a) * jax.nn.sigmoid(pre)
        ddt_ref[...] = jnp.where(lane < 2 * SSM_HEADS, dpre, 0.0)
        dbias_ref[...] += jnp.broadcast_to(jnp.sum(dpre, axis=0, keepdims=True), (8, HP))
        da_ref[...] += jnp.broadcast_to(jnp.sum(dda * dt, axis=0, keepdims=True) * a, (8, HP))

    vec = pl.BlockSpec((8, HP), lambda c: (0, 0))
    sel = pl.BlockSpec((SSM_GROUPS, CHUNK, 2 * HP), lambda c: (0, c, 0))
    tot = pl.BlockSpec((SSM_GROUPS, 1, 8, HP), lambda c: (0, c, 0, 0))
    tile = pl.BlockSpec((CHUNK, HP), lambda c: (c, 0))
    return pl.pallas_call(
        body, name="ssd_prep_bwd",
        out_shape=(jax.ShapeDtypeStruct((T, HP), F32), jax.ShapeDtypeStruct((8, HP), F32), jax.ShapeDtypeStruct((8, HP), F32)),
        grid=(nc,), in_specs=[pl.BlockSpec((CHUNK, HP), lambda c: (c, DT_BLK)), vec, vec, sel, tot, sel, tot],
        out_specs=(tile, vec, vec),
        compiler_params=pltpu.CompilerParams(dimension_semantics=("arbitrary",)),
    )(u, bias8, alog8, dsel_f, dtot_f, dsel_b, dtot_b)


def _ssm_combine_fwd(y_f, y_b, act, u, dskip, gain):
    T = y_f.shape[0]
    tt = _pick(T, 256)

    def body(yf_ref, yb_ref, x_ref, z_ref, ds_ref, g_ref, y_ref, m_ref):
        y = yf_ref[...] + yb_ref[...] + ds_ref[...] * x_ref[...]
        y2 = y * _silu(z_ref[...])
        r = lax.rsqrt(jnp.mean(y2 * y2, axis=-1, keepdims=True) + EPS)
        y_ref[...] = y
        m_ref[...] = (y2 * r * g_ref[...]).astype(m_ref.dtype)

    blk = pl.BlockSpec((tt, GW), lambda i, g: (i, g))
    vec = pl.BlockSpec((1, GW), lambda i, g: (0, g))
    return pl.pallas_call(
        body, name="ssm_combine_fwd",
        out_shape=(jax.ShapeDtypeStruct((T, D_INNER), F32), jax.ShapeDtypeStruct((T, D_INNER), BF)),
        grid=(T // tt, SSM_GROUPS), in_specs=[blk, blk, blk, blk, vec, vec], out_specs=(blk, blk),
    )(y_f, y_b, act, u, dskip, gain)


def _ssm_combine_bwd(dm, y, act, u, dskip, gain):
    T = y.shape[0]
    tt = _pick(T, 256)

    def body(dm_ref, y_ref, x_ref, z_ref, ds_ref, g_ref, dy_ref, dz_ref, dxs_ref, dg_ref, dsk_ref):
        z = z_ref[...]
        y = y_ref[...]
        x = x_ref[...]
        sz = _silu(z)
        y2 = y * sz
        r = lax.rsqrt(jnp.mean(y2 * y2, axis=-1, keepdims=True) + EPS)
        d = dm_ref[...]
        gd = d * g_ref[...]
        dy2 = r * gd - y2 * (r * r * r) * jnp.mean(gd * y2, axis=-1, keepdims=True)
        dy = dy2 * sz
        dy_ref[...] = dy
        dz_ref[...] = (dy2 * y * _dsilu(z)).astype(dz_ref.dtype)
        dxs_ref[...] = dy * ds_ref[...]

        @pl.when(pl.program_id(1) == 0)
        def _():
            dg_ref[...] = jnp.zeros_like(dg_ref)
            dsk_ref[...] = jnp.zeros_like(dsk_ref)

        dg_ref[...] += jnp.broadcast_to(jnp.sum(d * y2 * r, axis=0, keepdims=True), dg_ref.shape)
        lane_sum = jnp.broadcast_to(jnp.sum(dy * x, axis=0, keepdims=True), (8, GW))
        src = lax.broadcasted_iota(jnp.int32, (GW, HP), 0)
        head = lax.broadcasted_iota(jnp.int32, (GW, HP), 1)
        to_head = jnp.where((src >= PH * head) & (src < PH * (head + 1)), 1.0, 0.0).astype(F32)
        dsk_ref[...] += jnp.dot(lane_sum, to_head, precision=HI, preferred_element_type=F32)

    blk = pl.BlockSpec((tt, GW), lambda g, i: (i, g))
    vec = pl.BlockSpec((1, GW), lambda g, i: (0, g))
    acc = pl.BlockSpec((8, GW), lambda g, i: (0, g))
    return pl.pallas_call(
        body, name="ssm_combine_bwd",
        out_shape=(jax.ShapeDtypeStruct((T, D_INNER), F32), jax.ShapeDtypeStruct((T, D_INNER), BF),
                   jax.ShapeDtypeStruct((T, D_INNER), F32), jax.ShapeDtypeStruct((8, D_INNER), F32),
                   jax.ShapeDtypeStruct((8, SSM_GROUPS * HP), F32)),
        grid=(SSM_GROUPS, T // tt), in_specs=[blk, blk, blk, blk, vec, vec],
        out_specs=(blk, blk, blk, acc, pl.BlockSpec((8, HP), lambda g, i: (0, g))),
        compiler_params=pltpu.CompilerParams(dimension_semantics=("parallel", "arbitrary")),
    )(dm, y, act, u, dskip, gain)


def _loss_head(y, target):
    T, D = y.shape
    tt = _pick(T, 512)

    def body(y_ref, t_ref, dy_ref, dyb_ref, l_ref):
        e = y_ref[...] - t_ref[...]
        dy_ref[...] = e * (1.0 / D)
        dyb_ref[...] = (e * (1.0 / D)).astype(dyb_ref.dtype)

        @pl.when(pl.program_id(0) == 0)
        def _():
            l_ref[...] = jnp.zeros_like(l_ref)

        l_ref[...] += jnp.sum(e * e) * (0.5 / D)

    blk = pl.BlockSpec((tt, D), lambda i: (i, 0))
    return pl.pallas_call(
        body, name="loss_head",
        out_shape=(jax.ShapeDtypeStruct((T, D), F32), jax.ShapeDtypeStruct((T, D), BF), jax.ShapeDtypeStruct((8, 128), F32)),
        grid=(T // tt,), in_specs=[blk, blk], out_specs=(blk, blk, pl.BlockSpec((8, 128), lambda i: (0, 0))),
        compiler_params=pltpu.CompilerParams(dimension_semantics=("arbitrary",)),
    )(y, target)


def _adamw(w, g, m, v, *, name):
    R, C = w.shape
    cap = max(8, (1 << 18) // C)
    tr = R
    if R % 8 == 0:
        tr = 8
        for cand in range(8, min(R, cap) + 1, 8):
            if R % cand == 0:
                tr = cand

    def body(w_ref, g_ref, m_ref, v_ref, d_ref, nm_ref, nv_ref):
        gg = g_ref[...]
        nm = ADAM_B1 * m_ref[...] + (1.0 - ADAM_B1) * gg
        nv = ADAM_B2 * v_ref[...] + (1.0 - ADAM_B2) * jnp.square(gg)
        m_hat = nm / (1.0 - ADAM_B1 ** ADAM_STEP)
        v_hat = nv / (1.0 - ADAM_B2 ** ADAM_STEP)
        d_ref[...] = -ADAM_LR * (m_hat / (jnp.sqrt(v_hat) + ADAM_EPS) + ADAM_WD * w_ref[...])
        nm_ref[...] = nm
        nv_ref[...] = nv

    blk = pl.BlockSpec((tr, C), lambda i: (i, 0))
    return pl.pallas_call(
        body, name=name, out_shape=(jax.ShapeDtypeStruct((R, C), F32),) * 3, grid=(R // tr,),
        in_specs=[blk] * 4, out_specs=(blk,) * 3,
    )(w, g, m, v)


ANY = pl.BlockSpec(memory_space=pl.ANY)


def _chip_peers():
    x, y, c = lax.axis_index("x"), lax.axis_index("y"), lax.axis_index("c")
    return x, y, c, [(1 - x, y), (x, 1 - y), (1 - x, 1 - y)]


def _half_rows(c, rh):
    return pl.ds(pl.multiple_of(c * rh, 16), rh)


def _my_chip():
    return 2 * lax.axis_index("x") + lax.axis_index("y")


def _gather_chips(wb, wf):
    rh = wb.shape[0] // 2

    def body(wb_ref, wf_ref, ob_ref, of_ref, send_sems, recv_sems):
        x, y, c, peers = _chip_peers()
        me = 2 * x + y
        half, other = _half_rows(c, rh), _half_rows(1 - c, rh)

        def chip_copy(k, slot):
            px, py = peers[k]
            return pltpu.make_async_remote_copy(
                src_ref=wb_ref.at[half], dst_ref=ob_ref.at[slot, half], send_sem=send_sems.at[k], recv_sem=recv_sems.at[k],
                device_id=(px, py, c), device_id_type=MESH)

        def passed_on(k, slot, rows):
            return pltpu.make_async_remote_copy(
                src_ref=ob_ref.at[slot, rows], dst_ref=ob_ref.at[slot, rows], send_sem=send_sems.at[3 + k],
                recv_sem=recv_sems.at[3 + k], device_id=(x, y, 1 - c), device_id_type=MESH)

        def small_copy(k, slot):
            px, py = peers[k]
            return pltpu.make_async_remote_copy(
                src_ref=wf_ref, dst_ref=of_ref.at[slot], send_sem=send_sems.at[6 + k], recv_sem=recv_sems.at[6 + k],
                device_id=(px, py, c), device_id_type=MESH)

        sends = [chip_copy(k, me) for k in range(3)] + [small_copy(k, me) for k in range(3)]
        for cp in sends:
            cp.start()
        chip_of = [2 * px + py for px, py in peers]
        for k in range(3):
            chip_copy(k, chip_of[k]).wait_recv()
            cp = passed_on(k, chip_of[k], half)
            cp.start()
            sends.append(cp)
        for k in range(3):
            passed_on(k, chip_of[k], other).wait_recv()
            small_copy(k, chip_of[k]).wait_recv()
        for cp in sends:
            cp.wait_send()

    ob, of = pl.pallas_call(
        body, name="gather_weights",
        out_shape=(jax.ShapeDtypeStruct((4,) + wb.shape, wb.dtype), jax.ShapeDtypeStruct((4,) + wf.shape, wf.dtype)),
        in_specs=[ANY, ANY], out_specs=(ANY, ANY),
        scratch_shapes=[pltpu.SemaphoreType.DMA((9,)), pltpu.SemaphoreType.DMA((9,))],
    )(wb, wf)
    me = _my_chip()
    return lax.dynamic_update_slice(ob, wb[None], (me, 0, 0)), lax.dynamic_update_slice(of, wf[None], (me, 0, 0))


def _halves_to_sibling(gp):
    rh = gp.shape[1] // 2

    def body(gp_ref, o_ref, send_sem, recv_sem):
        x, y, c = lax.axis_index("x"), lax.axis_index("y"), lax.axis_index("c")
        cp = pltpu.make_async_remote_copy(src_ref=gp_ref.at[:, _half_rows(1 - c, rh), :], dst_ref=o_ref, send_sem=send_sem,
                                          recv_sem=recv_sem, device_id=(x, y, 1 - c), device_id_type=MESH)
        cp.start()
        cp.wait()

    return pl.pallas_call(
        body, name="halves_to_sibling", out_shape=jax.ShapeDtypeStruct((gp.shape[0], rh, gp.shape[2]), gp.dtype),
        in_specs=[ANY], out_specs=ANY, scratch_shapes=[pltpu.SemaphoreType.DMA, pltpu.SemaphoreType.DMA],
    )(gp)


def _row_tile(rows, cap=1024):
    tr = 16
    for cand in range(16, cap + 1, 16):
        if rows % cand == 0:
            tr = cand
    return tr


def _add_halves(gp, sib, core):
    n, rh, C = sib.shape
    tr = _row_tile(rh)
    nt = rh // tr

    def body(c_ref, g_ref, s_ref, o_ref):
        o_ref[...] = (g_ref[...].astype(F32) + s_ref[...].astype(F32)).astype(o_ref.dtype)

    blk = pl.BlockSpec((1, tr, C), lambda j, i, c: (j, i, 0))
    return pl.pallas_call(
        body, name="add_halves", out_shape=jax.ShapeDtypeStruct(sib.shape, sib.dtype),
        grid_spec=pltpu.PrefetchScalarGridSpec(
            num_scalar_prefetch=1, grid=(n, nt),
            in_specs=[pl.BlockSpec((1, tr, C), lambda j, i, c: (j, c[0] * nt + i, 0)), blk], out_specs=blk),
    )(core, gp, sib)


def _join_halves(mine):
    rh = mine.shape[0]

    def body(m_ref, o_ref, send_sem, recv_sem):
        x, y, c = lax.axis_index("x"), lax.axis_index("y"), lax.axis_index("c")
        half, other = _half_rows(c, rh), _half_rows(1 - c, rh)

        def copy(rows):
            return pltpu.make_async_remote_copy(src_ref=m_ref, dst_ref=o_ref.at[rows], send_sem=send_sem, recv_sem=recv_sem,
                                                device_id=(x, y, 1 - c), device_id_type=MESH)

        send = copy(half)
        send.start()
        copy(other).wait_recv()
        send.wait_send()

    out = pl.pallas_call(
        body, name="join_halves", out_shape=jax.ShapeDtypeStruct((2 * rh, mine.shape[1]), mine.dtype),
        in_specs=[ANY], out_specs=ANY, scratch_shapes=[pltpu.SemaphoreType.DMA, pltpu.SemaphoreType.DMA],
    )(mine)
    return lax.dynamic_update_slice(out, mine, (lax.axis_index("c") * rh, 0))


def _exchange_chips(gp):
    def body(gp_ref, out_ref, send_sems, recv_sems):
        x, y, c, peers = _chip_peers()
        me = 2 * x + y

        def copies(sending):
            out = []
            for k, (px, py) in enumerate(peers):
                p = 2 * px + py
                out.append(pltpu.make_async_remote_copy(
                    src_ref=gp_ref.at[p], dst_ref=out_ref.at[me if sending else p],
                    send_sem=send_sems.at[k], recv_sem=recv_sems.at[k], device_id=(px, py, c), device_id_type=MESH))
            return out

        sends = copies(True)
        for cp in sends:
            cp.start()
        for cp in copies(False):
            cp.wait_recv()
        for cp in sends:
            cp.wait_send()

    out = pl.pallas_call(
        body, name="exchange_grads", out_shape=jax.ShapeDtypeStruct(gp.shape, gp.dtype),
        in_specs=[ANY], out_specs=ANY,
        scratch_shapes=[pltpu.SemaphoreType.DMA((3,)), pltpu.SemaphoreType.DMA((3,))],
    )(gp)
    me = _my_chip()
    return lax.dynamic_update_slice(out, lax.dynamic_slice_in_dim(gp, me, 1, axis=0), (me, 0, 0))


def _sum_slots(r4):
    _, R, C = r4.shape
    tr = _row_tile(R)

    def body(r_ref, o_ref):
        acc = r_ref[0].astype(F32)
        for s in range(1, 4):
            acc = acc + r_ref[s].astype(F32)
        o_ref[...] = acc

    return pl.pallas_call(
        body, name="sum_slots", out_shape=jax.ShapeDtypeStruct((R, C), F32), grid=(R // tr,),
        in_specs=[pl.BlockSpec((4, tr, C), lambda i: (0, i, 0))], out_specs=pl.BlockSpec((tr, C), lambda i: (i, 0)),
    )(r4)


N_DEV = 8


def _allreduce_small(p):
    rs = p.shape[0]

    def body(x_ref, sum_ref, all_ref, send_sems, recv_sems, local_sem):
        x, y, c = lax.axis_index("x"), lax.axis_index("y"), lax.axis_index("c")
        me, sibling = (x, y, c), (x, y, 1 - c)
        chips = [(1 - x, y), (x, 1 - y), (1 - x, 1 - y)]

        def rows(px, py, pc):
            return all_ref.at[pl.ds((4 * px + 2 * py + pc) * rs, rs), :]

        def copy(k, block, to, src=None):
            return pltpu.make_async_remote_copy(
                src_ref=rows(*block) if src is None else src, dst_ref=rows(*block),
                send_sem=send_sems.at[k], recv_sem=recv_sems.at[k], device_id=to, device_id_type=MESH)

        mine = pltpu.make_async_copy(x_ref, rows(*me), local_sem)
        mine.start()
        first = [copy(0, me, sibling, src=x_ref)]
        first += [copy(1 + j, me, (*chip, c), src=x_ref) for j, chip in enumerate(chips)]
        for cp in first:
            cp.start()
        passed = [copy(4 + j, (*chip, c), sibling) for j, chip in enumerate(chips)]
        for j, chip in enumerate(chips):
            copy(1 + j, (*chip, c), me).wait_recv()
            passed[j].start()
        copy(0, sibling, me).wait_recv()
        for j, chip in enumerate(chips):
            copy(4 + j, (*chip, 1 - c), me).wait_recv()
        for cp in first + passed:
            cp.wait_send()
        mine.wait()
        acc = all_ref[0:rs, :]
        for d in range(1, N_DEV):
            acc = acc + all_ref[d * rs:(d + 1) * rs, :]
        sum_ref[...] = acc

    vmem = pl.BlockSpec(memory_space=pltpu.VMEM)
    return pl.pallas_call(
        body, name="allreduce_small", out_shape=jax.ShapeDtypeStruct((rs, 128), F32),
        in_specs=[vmem], out_specs=vmem,
        scratch_shapes=[pltpu.VMEM((N_DEV * rs, 128), F32), pltpu.SemaphoreType.DMA((7,)), pltpu.SemaphoreType.DMA((7,)),
                        pltpu.SemaphoreType.DMA],
    )(p)


WEIGHTS = ('ffn1_norm', 'ffn1_w_gate', 'ffn1_w_up', 'ffn1_w_down', 'mix_norm', 'w_in', 'q_a_norm', 'w_q_b',
           'kv_a_norm', 'w_kv_b', 'q_head_norm', 'k_head_norm', 'conv_w', 'conv_b', 'a_log_fwd', 'a_log_bwd',
           'dt_bias_fwd', 'dt_bias_bwd', 'd_skip', 'ssm_norm', 'w_attn_branch', 'w_ssm_branch', 'w_out',
           'ffn2_norm', 'ffn2_w_gate', 'ffn2_w_up', 'ffn2_w_down')
PACKED = (('ffn1_w_gate', (D_MODEL, D_FF), 1), ('ffn1_w_up', (D_MODEL, D_FF), 1), ('ffn1_w_down', (D_FF, D_MODEL), 0),
          ('w_in', (D_MODEL, sum(IN_SPLITS)), 1), ('w_q_b', (Q_LORA, N_HEADS * QK_HEAD), 1),
          ('w_kv_b', (KV_LORA, N_HEADS * (QK_NOPE + V_HEAD)), 1),
          ('w_attn_branch', (N_HEADS * V_HEAD, D_MODEL), 0), ('w_ssm_branch', (D_INNER, D_MODEL), 0),
          ('w_out', (D_MODEL, D_MODEL), 0),
          ('ffn2_w_gate', (D_MODEL, D_FF), 1), ('ffn2_w_up', (D_MODEL, D_FF), 1), ('ffn2_w_down', (D_FF, D_MODEL), 0))
PACK_W = 1024
N_CHIPS = 4
SMALL = (('ffn1_norm', 1024), ('mix_norm', 1024), ('q_a_norm', 384), ('kv_a_norm', 256), ('q_head_norm', 96),
         ('k_head_norm', 96), ('conv_b', 3072), ('a_log_fwd', 32), ('a_log_bwd', 32), ('dt_bias_fwd', 32),
         ('dt_bias_bwd', 32), ('d_skip', 32), ('ssm_norm', 2048), ('ffn2_norm', 1024),
         ('conv_w', CONV_WIDTH * XBC_DIM), ('loss', 1))


def _shard_shape(shape, axis):
    return tuple(s // N_CHIPS if a == axis else s for a, s in enumerate(shape))


def _pack_rows():
    rows = sum(math.prod(_shard_shape(shape, axis)) // PACK_W for _, shape, axis in PACKED)
    return -(-rows // 32) * 32


def _pack(shards):
    parts = [shards[name].reshape(-1, PACK_W) for name, _, _ in PACKED]
    rows = sum(p.shape[0] for p in parts)
    parts.append(jnp.zeros((_pack_rows() - rows, PACK_W), parts[0].dtype))
    return jnp.concatenate(parts, axis=0)


def _unpack(packed):
    out, r = {}, 0
    for name, shape, axis in PACKED:
        sh = _shard_shape(shape, axis)
        n = math.prod(sh) // PACK_W
        out[name] = packed[r:r + n].reshape(sh)
        r += n
    return out


def _pack_small(vals):
    parts = []
    for name, n in SMALL:
        pad = -(-n // 128) * 128 - n
        parts.append(jnp.pad(vals[name].reshape(-1).astype(F32), (0, pad)).reshape(-1, 128))
    rows = sum(p.shape[0] for p in parts)
    parts.append(jnp.zeros((-(-rows // 8) * 8 - rows, 128), F32))
    return jnp.concatenate(parts, axis=0)


def _unpack_small(packed):
    out, r = {}, 0
    for name, n in SMALL:
        k = -(-n // 128)
        out[name] = packed[r:r + k].reshape(-1)[:n]
        r += k
    return out


def _pad_heads(w, axis, per_head, lo, hi):
    shape = w.shape
    w = w.reshape(shape[:axis] + (N_HEADS, per_head) + shape[axis + 1:])
    w = lax.slice_in_dim(w, lo, hi, axis=axis + 1)
    pad = [(0, 0)] * w.ndim
    pad[axis + 1] = (0, HP - (hi - lo))
    w = jnp.pad(w, pad)
    return w.reshape(shape[:axis] + (N_HEADS * HP,) + shape[axis + 1:])


def _unpad_heads(w, axis, keep):
    shape = w.shape
    w = w.reshape(shape[:axis] + (N_HEADS, HP) + shape[axis + 1:])
    return lax.slice_in_dim(w, 0, keep, axis=axis + 1)


def _split_w_in(w):
    o = [0]
    for s in IN_SPLITS:
        o.append(o[-1] + s)
    return [w[:, o[i]:o[i + 1]] for i in range(len(IN_SPLITS))]


def _pad_w_in(w):
    cq, ckv, kpe, z, xbc, dtf, dtb, ga, gb = _split_w_in(w)
    kpe_pad = jnp.pad(kpe, ((0, 0), (QK_NOPE, HP - QK_HEAD)))
    dt_pad = jnp.pad(jnp.concatenate([dtf, dtb], axis=1), ((0, 0), (0, HP - 2 * SSM_HEADS)))
    return jnp.concatenate([z, ga, gb, xbc, cq, ckv, kpe_pad, dt_pad], axis=1)


def _unpad_w_in(g):
    z, ga, gb, xbc = g[:, U_Z:U_GA], g[:, U_GA:U_GB], g[:, U_GB:U_XBC], g[:, U_XBC:U_SMALL]
    s = g[:, U_SMALL:]
    cq, ckv = s[:, S_CQ:S_CKV], s[:, S_CKV:S_KPE]
    kpe = s[:, S_KPE + QK_NOPE:S_KPE + QK_HEAD]
    dtf, dtb = s[:, S_DT:S_DT + SSM_HEADS], s[:, S_DT + SSM_HEADS:S_DT + 2 * SSM_HEADS]
    return jnp.concatenate([cq, ckv, kpe, z, xbc, dtf, dtb, ga, gb], axis=1)


def _lanes128(parts):
    row = jnp.concatenate([p.reshape(-1) for p in parts])
    return jnp.pad(row, (0, HP - row.shape[0])).reshape(1, HP)


FF_TILE = D_FF // 2
WGRAD = BF


def _ffn_fwd(x, g, wg, wu, wd, tag):
    h = _rms_fwd(x, g, name=tag + "_norm")
    gate, up, act = _mm([h], [wg, wu], name=tag + "_up", out_dtypes=(F32, F32, BF), tm=512, tn=FF_TILE,
                        epilogue=lambda a, b: (a, b, _silu(a) * b))
    out = _mm([act], [wd], name=tag + "_down", extras=[x], epilogue=lambda acc, r: (r + 0.5 * acc,))
    return out, (h, gate, up, act)


def _ffn_bwd(dout, dout_bf, x, g, wg, wu, wd, saved, tag):
    h, gate, up, act = saved
    dgate, dup = _mm([dout_bf], [wd], name=tag + "_down_dx", tb=True, extras=[gate, up], out_dtypes=(BF, BF),
                     tm=512, tn=FF_TILE, epilogue=lambda acc, a, b: (0.5 * acc * b * _dsilu(a), 0.5 * acc * _silu(a)))
    dwd = _mm([act], [dout_bf], name=tag + "_down_dw", ta=True, tm=FF_TILE, tk=1024, out_dtypes=(WGRAD,),
              epilogue=lambda acc: (0.5 * acc,))
    dwg, dwu = _mm([h], [dgate, dup], name=tag + "_up_dw", ta=True, out_dtypes=(WGRAD, WGRAD), tm=512, tn=FF_TILE, tk=1024)
    dh = _mm([dgate, dup], [wg, wu], name=tag + "_up_dx", tb=True)
    dx, dx_bf, dg = _rms_bwd(dh, x, g, name=tag + "_norm_bwd", add=dout, out_dtypes=(F32, BF))
    return dx, dx_bf, dg, dwg, dwu, dwd


KPE_BLK = (U_SMALL + S_KPE) // HP
SMALL_BLK = U_SMALL // SMALL_W


def _local_step(x, pos_col, target, W, P):
    T = x.shape[0]
    sig = jax.nn.sigmoid
    x1, ffn1 = _ffn_fwd(x, P["ffn1_norm"], W["wg1"], W["wu1"], W["wd1"], "ffn1")
    h = _rms_fwd(x1, P["mix_norm"], name="mix_norm")
    u = _mm([h], [W["w_in"]], name="in_proj", tn=1152)
    cqn = _rms_fwd(u, P["q_a_norm"], name="q_a_norm", blk_w=SMALL_W, blk_idx=SMALL_BLK, off=S_CQ, width=Q_LORA)
    ckvn = _rms_fwd(u, P["kv_a_norm"], name="kv_a_norm", blk_w=SMALL_W, blk_idx=SMALL_BLK, off=S_CKV, width=KV_LORA)
    q_raw = _mm([cqn], [W["wq"]], name="q_proj")
    k_raw, v = _mm([ckvn], [W["wk"], W["wv"]], name="kv_proj", out_dtypes=(F32, BF))
    rc, rs = _rope_tables(pos_col, P["freq"])
    q = _qk_prep_fwd(q_raw, None, P["q_head_norm"], rc, rs, name="q_prep", out_scale=Q_SCALE)
    k = _qk_prep_fwd(k_raw, u, P["k_head_norm"], rc, rs, name="k_prep", kpe_blk=KPE_BLK)
    o, lse = _attn_fwd(q, k, v)
    pre, act = _conv_fwd(u, P["conv_w8"], P["conv_b"])
    scan_cols, scan_rows = _ssd_prep(u, P["dt_bias8"], P["a_log8"])
    y_f, st_f = _ssd_fwd(act, scan_cols, scan_rows, rev=False, name="ssd_fwd_f")
    y_b, st_b = _ssd_fwd(act, scan_cols, scan_rows, rev=True, name="ssd_fwd_b")
    ysum, m = _ssm_combine_fwd(y_f, y_b, act, u, P["d_skip_lanes"], P["ssm_norm"])
    ab = _mm([o], [W["pa"]], name="attn_branch")
    mb, merged = _mm([m], [W["pb"]], name="ssm_branch", extras=[ab, u, u], extra_offs=(0, U_GA, U_GB), out_dtypes=(F32, BF),
                     epilogue=lambda acc, a, ga, gb: (acc, sig(ga) * a + sig(gb) * acc))
    x2 = _mm([merged], [W["wo"]], name="out_proj", extras=[x1], epilogue=lambda acc, r: (r + acc,))
    y, ffn2 = _ffn_fwd(x2, P["ffn2_norm"], W["wg2"], W["wu2"], W["wd2"], "ffn2")
    dy, dy_bf, loss = _loss_head(y, target)
    dx2, dx2_bf, dg_ffn2, dwg2, dwu2, dwd2 = _ffn_bwd(dy, dy_bf, x2, P["ffn2_norm"], W["wg2"], W["wu2"], W["wd2"], ffn2,
                                                      "ffn2")

    def gate_bwd(dmrg, a, b, ga, gb):
        sa, sb = sig(ga), sig(gb)
        return dmrg * sa, dmrg * sb, dmrg * a * sa * (1.0 - sa), dmrg * b * sb * (1.0 - sb)

    dab, dmb, dga, dgb = _mm([dx2_bf], [W["wo"]], name="out_proj_dx", tb=True, extras=[ab, mb, u, u],
                             extra_offs=(0, 0, U_GA, U_GB), out_dtypes=(BF,) * 4, epilogue=gate_bwd)
    dwo = _mm([merged], [dx2_bf], name="out_proj_dw", ta=True, out_dtypes=(WGRAD,))
    dpa = _mm([o], [dab], name="attn_branch_dw", ta=True, out_dtypes=(WGRAD,))
    do = _mm([dab], [W["pa"]], name="attn_branch_dx", tb=True)
    dpb = _mm([m], [dmb], name="ssm_branch_dw", ta=True, out_dtypes=(WGRAD,))
    dm = _mm([dmb], [W["pb"]], name="ssm_branch_dx", tb=True)
    dyssd, dz, dxs_skip, dg_ssm, dskip = _ssm_combine_bwd(dm, ysum, act, u, P["d_skip_lanes"], P["ssm_norm"])
    dxs_f, db_f, dc_f, dsel_f, dtot_f = _ssd_bwd(act, scan_cols, scan_rows, st_f, dyssd, rev=False, name="ssd_bwd_f")
    dxs_b, db_b, dc_b, dsel_b, dtot_b = _ssd_bwd(act, scan_cols, scan_rows, st_b, dyssd, rev=True, name="ssd_bwd_b")
    ddt, dalog, dbias = _ssd_prep_bwd(u, P["dt_bias8"], P["a_log8"], dsel_f, dtot_f, dsel_b, dtot_b)
    dxbc, dconv = [], []
    for tag, col0, parts in (("x", 0, [dxs_f, dxs_b, dxs_skip]), ("b", D_INNER, [db_f, db_b]),
                             ("c", D_INNER + SSM_GROUPS * D_STATE, [dc_f, dc_b])):
        dpre = _conv_dpre(parts, pre, col0, name="conv_dpre_" + tag)
        dxp, dwp = _conv_bwd(dpre, u, P["conv_w8"], col0, name="conv_bwd_" + tag)
        dxbc.append(dxp)
        dconv.append(dwp)
    dconv = jnp.concatenate(dconv, axis=1)
    dq, dk, dv = _attn_bwd(q, k, v, do, o, lse)
    dq_raw, dg_qh = _qk_prep_bwd(dq, q_raw, None, P["q_head_norm"], rc, rs, name="q_prep_bwd", in_scale=ATTN_SCALE)
    dk_raw, dg_kh, dkpe = _qk_prep_bwd(dk, k_raw, u, P["k_head_norm"], rc, rs, name="k_prep_bwd", kpe_blk=KPE_BLK,
                                       in_scale=1.0 / LOG2E)
    dwq = _mm([cqn], [dq_raw], name="q_proj_dw", ta=True, out_dtypes=(WGRAD,))
    dcqn = _mm([dq_raw], [W["wq"]], name="q_proj_dx", tb=True)
    dwk, dwv = _mm([ckvn], [dk_raw, dv], name="kv_proj_dw", ta=True, out_dtypes=(WGRAD, WGRAD))
    dckvn = _mm([dk_raw, dv], [W["wk"], W["wv"]], name="kv_proj_dx", tb=True)
    dcq, dg_qa = _rms_bwd(dcqn, u, P["q_a_norm"], name="q_a_norm_bwd", blk_w=SMALL_W, blk_idx=SMALL_BLK, off=S_CQ,
                          width=Q_LORA, out_dtypes=(BF,))
    dckv, dg_kva = _rms_bwd(dckvn, u, P["kv_a_norm"], name="kv_a_norm_bwd", blk_w=SMALL_W, blk_idx=SMALL_BLK,
                            off=S_CKV, width=KV_LORA, out_dtypes=(BF,))
    du = jnp.concatenate([dz, dga, dgb] + dxbc + [dcq, dckv, dkpe.astype(BF), ddt.astype(BF)], axis=1)
    dw_in = _mm([h], [du], name="in_proj_dw", ta=True, tn=1152, out_dtypes=(WGRAD,))
    dh = _mm([du], [W["w_in"]], name="in_proj_dx", tb=True)
    dx1, dx1_bf, dg_mix = _rms_bwd(dh, x1, P["mix_norm"], name="mix_norm_bwd", add=dx2, out_dtypes=(F32, BF))
    dx, _, dg_ffn1, dwg1, dwu1, dwd1 = _ffn_bwd(dx1, dx1_bf, x, P["ffn1_norm"], W["wg1"], W["wu1"], W["wd1"], ffn1, "ffn1")
    dW = dict(wg1=dwg1, wu1=dwu1, wd1=dwd1, w_in=dw_in, wq=dwq, wk=dwk, wv=dwv, pa=dpa, pb=dpb, wo=dwo,
              wg2=dwg2, wu2=dwu2, wd2=dwd2)
    dP = dict(ffn1_norm=dg_ffn1[0], mix_norm=dg_mix[0], q_a_norm=dg_qa[0], kv_a_norm=dg_kva[0],
              q_head_norm=dg_qh[0, :QK_HEAD], k_head_norm=dg_kh[0, :QK_HEAD], conv_b=dconv[CONV_WIDTH],
              a_log_fwd=dalog[0, :SSM_HEADS], a_log_bwd=dalog[0, SSM_HEADS:2 * SSM_HEADS],
              dt_bias_fwd=dbias[0, :SSM_HEADS], dt_bias_bwd=dbias[0, SSM_HEADS:2 * SSM_HEADS],
              d_skip=dskip[0].reshape(SSM_GROUPS, HP)[:, :HG], ssm_norm=dg_ssm[0], ffn2_norm=dg_ffn2[0],
              conv_w=dconv[:CONV_WIDTH], loss=loss[0, 0])
    return dx, dW, dP


def _prepare(w, conv_w_full):
    kvb = w["w_kv_b"]
    W = dict(wg1=w["ffn1_w_gate"], wu1=w["ffn1_w_up"], wd1=w["ffn1_w_down"], w_in=_pad_w_in(w["w_in"]),
             wq=_pad_heads(w["w_q_b"], 1, QK_HEAD, 0, QK_HEAD),
             wk=_pad_heads(kvb, 1, QK_NOPE + V_HEAD, 0, QK_NOPE),
             wv=_pad_heads(kvb, 1, QK_NOPE + V_HEAD, QK_NOPE, QK_NOPE + V_HEAD),
             pa=_pad_heads(w["w_attn_branch"], 0, V_HEAD, 0, V_HEAD), pb=w["w_ssm_branch"], wo=w["w_out"],
             wg2=w["ffn2_w_gate"], wu2=w["ffn2_w_up"], wd2=w["ffn2_w_down"])
    inv_freq = [1.0 / (ROPE_BASE ** (j / QK_ROPE)) for j in range(0, QK_ROPE, 2)]
    freq = [0.0] * QK_NOPE + inv_freq + inv_freq + [0.0] * (HP - QK_HEAD)
    P = {n: w[n] for n in ("ffn1_norm", "mix_norm", "q_a_norm", "kv_a_norm", "ssm_norm", "ffn2_norm", "conv_b")}
    P.update(q_head_norm=_lanes128([w["q_head_norm"]]), k_head_norm=_lanes128([w["k_head_norm"]]),
             conv_w8=jnp.pad(conv_w_full, ((0, 8 - CONV_WIDTH), (0, 0))),
             dt_bias8=jnp.broadcast_to(_lanes128([w["dt_bias_fwd"], w["dt_bias_bwd"]]), (8, HP)),
             a_log8=jnp.broadcast_to(_lanes128([w["a_log_fwd"], w["a_log_bwd"]]), (8, HP)),
             d_skip_lanes=jnp.repeat(w["d_skip"].reshape(-1), PH).reshape(1, D_INNER),
             freq=jnp.asarray(freq, F32).reshape(1, HP))
    return W, P


def _unprepare(dW):
    dkvb = jnp.concatenate([_unpad_heads(dW["wk"], 1, QK_NOPE), _unpad_heads(dW["wv"], 1, V_HEAD)], axis=2)
    return dict(ffn1_w_gate=dW["wg1"], ffn1_w_up=dW["wu1"], ffn1_w_down=dW["wd1"], w_in=_unpad_w_in(dW["w_in"]),
                w_q_b=_unpad_heads(dW["wq"], 1, QK_HEAD).reshape(Q_LORA, N_HEADS * QK_HEAD),
                w_kv_b=dkvb.reshape(KV_LORA, N_HEADS * (QK_NOPE + V_HEAD)),
                w_attn_branch=_unpad_heads(dW["pa"], 0, V_HEAD).reshape(N_HEADS * V_HEAD, D_MODEL),
                w_ssm_branch=dW["pb"], w_out=dW["wo"],
                ffn2_w_gate=dW["wg2"], ffn2_w_up=dW["wu2"], ffn2_w_down=dW["wd2"])


def kernel(x, positions, ffn1_norm, ffn1_w_gate, ffn1_w_up, ffn1_w_down, mix_norm, w_in, q_a_norm, w_q_b, kv_a_norm, w_kv_b, q_head_norm, k_head_norm, conv_w, conv_b, a_log_fwd, a_log_bwd, dt_bias_fwd, dt_bias_bwd, d_skip, ssm_norm, w_attn_branch, w_ssm_branch, w_out, ffn2_norm, ffn2_w_gate, ffn2_w_up, ffn2_w_down, loss_target, m_ffn1_norm, m_ffn1_w_gate, m_ffn1_w_up, m_ffn1_w_down, m_mix_norm, m_w_in, m_q_a_norm, m_w_q_b, m_kv_a_norm, m_w_kv_b, m_q_head_norm, m_k_head_norm, m_conv_w, m_conv_b, m_a_log_fwd, m_a_log_bwd, m_dt_bias_fwd, m_dt_bias_bwd, m_d_skip, m_ssm_norm, m_w_attn_branch, m_w_ssm_branch, m_w_out, m_ffn2_norm, m_ffn2_w_gate, m_ffn2_w_up, m_ffn2_w_down, v_ffn1_norm, v_ffn1_w_gate, v_ffn1_w_up, v_ffn1_w_down, v_mix_norm, v_w_in, v_q_a_norm, v_w_q_b, v_kv_a_norm, v_w_kv_b, v_q_head_norm, v_k_head_norm, v_conv_w, v_conv_b, v_a_log_fwd, v_a_log_bwd, v_dt_bias_fwd, v_dt_bias_bwd, v_d_skip, v_ssm_norm, v_w_attn_branch, v_w_ssm_branch, v_w_out, v_ffn2_norm, v_ffn2_w_gate, v_ffn2_w_up, v_ffn2_w_down):
    given = dict(locals())
    T = x.shape[1]
    packed_names = [name for name, _, _ in PACKED]

    def two_d(a):
        return a.reshape(a.shape[1], -1) if a.ndim > 2 else a

    w_loc = {n: two_d(given[n]) for n in WEIGHTS}
    wb = _pack({n: w_loc[n].astype(BF) for n in packed_names})
    wf = jnp.pad(w_loc["conv_w"], ((0, 8 - CONV_WIDTH), (0, 0)))
    gb, gf = _gather_chips(wb, wf)
    per_chip = [_unpack(gb[j]) for j in range(N_CHIPS)]
    full = {n: jnp.concatenate([per_chip[j][n] for j in range(N_CHIPS)], axis=axis) for n, _, axis in PACKED}
    conv_w_full = jnp.concatenate([gf[j, :CONV_WIDTH] for j in range(N_CHIPS)], axis=1)
    full.update({n: w_loc[n] for n in WEIGHTS if n not in full and n != "conv_w"})
    W, P = _prepare(full, conv_w_full)
    dx, dW, dP = _local_step(x.reshape(T, D_MODEL), positions.reshape(T, 1).astype(F32), loss_target.reshape(T, D_MODEL), W, P)
    g_full = _unprepare(dW)
    slots = []
    for j in range(N_CHIPS):
        shards = {}
        for n, shape, axis in PACKED:
            size = shape[axis] // N_CHIPS
            shards[n] = lax.slice_in_dim(g_full[n], j * size, (j + 1) * size, axis=axis).astype(BF)
        slots.append(_pack(shards))
    gp = jnp.stack(slots)
    core = lax.axis_index("c").astype(jnp.int32).reshape(1)
    both_cores = _add_halves(gp, _halves_to_sibling(gp), core)
    g_packed = _unpack(_join_halves(_sum_slots(_exchange_chips(both_cores))))
    small = _unpack_small(_allreduce_small(_pack_small(dP)))
    chip = 2 * lax.axis_index("x") + lax.axis_index("y")
    grads = dict(g_packed)
    grads.update({n: small[n].reshape(1, -1) for n, _ in SMALL if n not in ("conv_w", "loss")})
    grads["conv_w"] = lax.dynamic_slice_in_dim(small["conv_w"].reshape(CONV_WIDTH, XBC_DIM), chip * (XBC_DIM // N_CHIPS),
                                               XBC_DIM // N_CHIPS, axis=1)
    out_g, out_d, out_m, out_v = [], [], [], []
    for n in WEIGHTS:
        shape = given[n].shape
        delta, new_m, new_v = _adamw(w_loc[n], grads[n], two_d(given["m_" + n]), two_d(given["v_" + n]), name="adamw_" + n)
        out_g.append(grads[n].reshape(shape))
        out_d.append(delta.reshape(shape))
        out_m.append(new_m.reshape(shape))
        out_v.append(new_v.reshape(shape))
    return (small["loss"].reshape(()), dx.reshape(x.shape), *out_g, *out_d, *out_m, *out_v)
```

```python
import functools
import math

import jax
import jax.numpy as jnp
from jax import lax
from jax.experimental import pallas as pl
from jax.experimental.pallas import tpu as pltpu

BF = jnp.bfloat16
F32 = jnp.float32
HI = lax.Precision.HIGHEST
MESH = pl.DeviceIdType.MESH

D_MODEL = 1024
D_FF = 2816
EPS = 1e-6
N_HEADS = 16
QK_NOPE = 64
QK_ROPE = 32
QK_HEAD = 96
V_HEAD = 64
Q_LORA = 384
KV_LORA = 256
ROPE_BASE = 10000.0
D_INNER = 2048
SSM_HEADS = 32
SSM_GROUPS = 4
D_STATE = 128
CONV_WIDTH = 5
CHUNK = 128
XBC_DIM = 3072
HP = 128
GW = D_INNER // SSM_GROUPS
HG = SSM_HEADS // SSM_GROUPS
PH = 64
U_Z, U_GA, U_GB, U_XBC, U_SMALL = 0, 2048, 3072, 4096, 7168
S_CQ, S_CKV, S_KPE, S_DT, SMALL_W = 0, 384, 640, 768, 896
U_PAD = U_SMALL + SMALL_W
IN_SPLITS = (Q_LORA, KV_LORA, QK_ROPE, D_INNER, XBC_DIM, SSM_HEADS, SSM_HEADS, D_MODEL, D_MODEL)

ADAM_LR = 0.001
ADAM_B1 = 0.9
ADAM_B2 = 0.999
ADAM_EPS = 1e-08
ADAM_WD = 0.01
ADAM_STEP = 10

NN = (((1,), (0,)), ((), ()))
NT = (((1,), (1,)), ((), ()))
TN = (((0,), (0,)), ((), ()))


def _pick(n, pref):
    best = None
    d = 128
    while d <= min(n, pref):
        if n % d == 0:
            best = d
        d += 128
    return best if best is not None else n


def _silu(x):
    return x * jax.nn.sigmoid(x)


def _dsilu(x):
    s = jax.nn.sigmoid(x)
    return s * (1.0 + x * (1.0 - s))


def _softplus(x):
    return jnp.maximum(x, 0.0) + jnp.log(1.0 + jnp.exp(-jnp.abs(x)))


def _mm(As, Bs, *, name, ta=False, tb=False, out_dtypes=(F32,), epilogue=None, extras=(), extra_offs=None,
        tm=1024, tn=512, tk=2048):
    As, Bs, extras = list(As), list(Bs), list(extras)
    a0, b0 = As[0], Bs[0]
    M, K = (a0.shape[1], a0.shape[0]) if ta else a0.shape
    N = b0.shape[0] if tb else b0.shape[1]
    tm, tn, tk = _pick(M, tm), _pick(N, tn), _pick(K, tk)
    nk = K // tk
    n_a, n_b, n_e, n_o = len(As), len(Bs), len(extras), len(out_dtypes)
    n_acc = (n_b if n_a == 1 else 1) if nk > 1 else 0
    if extra_offs is None:
        extra_offs = (0,) * n_e
    dn = (((0,) if ta else (1,), (1,) if tb else (0,)), ((), ()))
    bytes_a = sum(a.size * a.dtype.itemsize for a in As)
    bytes_b = sum(b.size * b.dtype.itemsize for b in Bs)
    n_outer = (N // tn) * bytes_a + bytes_b < (M // tm) * bytes_b + bytes_a

    def products(a_refs, b_refs):
        if n_a == 1:
            a = a_refs[0][...].astype(BF)
            return [lax.dot_general(a, b[...].astype(BF), dn, preferred_element_type=F32) for b in b_refs]
        total = None
        for a, b in zip(a_refs, b_refs):
            p = lax.dot_general(a[...].astype(BF), b[...].astype(BF), dn, preferred_element_type=F32)
            total = p if total is None else total + p
        return [total]

    def finish(accs, e_refs, o_refs):
        ex = [e[...] for e in e_refs]
        outs = epilogue(*accs, *ex) if epilogue is not None else tuple(accs)
        for o_ref, val in zip(o_refs, outs):
            o_ref[...] = val.astype(o_ref.dtype)

    def body(*refs):
        a_refs, b_refs = refs[:n_a], refs[n_a:n_a + n_b]
        e_refs = refs[n_a + n_b:n_a + n_b + n_e]
        o_refs = refs[n_a + n_b + n_e:n_a + n_b + n_e + n_o]
        acc_refs = refs[n_a + n_b + n_e + n_o:]
        if nk == 1:
            finish(products(a_refs, b_refs), e_refs, o_refs)
            return
        k = pl.program_id(2)

        @pl.when(k == 0)
        def _():
            for acc in acc_refs:
                acc[...] = jnp.zeros_like(acc)

        for acc, p in zip(acc_refs, products(a_refs, b_refs)):
            acc[...] += p

        @pl.when(k == nk - 1)
        def _():
            finish([acc[...] for acc in acc_refs], e_refs, o_refs)

    def at(f):
        return (lambda j, i, k: f(i, j, k)) if n_outer else f

    a_spec = pl.BlockSpec((tk, tm), at(lambda i, j, k: (k, i))) if ta else pl.BlockSpec((tm, tk), at(lambda i, j, k: (i, k)))
    b_spec = pl.BlockSpec((tn, tk), at(lambda i, j, k: (j, k))) if tb else pl.BlockSpec((tk, tn), at(lambda i, j, k: (k, j)))
    e_specs = [pl.BlockSpec((tm, tn), at(functools.partial(lambda i, j, k, o: (i, j + o), o=off // tn))) for off in extra_offs]
    for off in extra_offs:
        assert off % tn == 0
    outs = pl.pallas_call(
        body, name=name,
        out_shape=tuple(jax.ShapeDtypeStruct((M, N), dt) for dt in out_dtypes),
        grid=(N // tn, M // tm, nk) if n_outer else (M // tm, N // tn, nk),
        in_specs=[a_spec] * n_a + [b_spec] * n_b + e_specs,
        out_specs=tuple(pl.BlockSpec((tm, tn), at(lambda i, j, k: (i, j))) for _ in out_dtypes),
        scratch_shapes=[pltpu.VMEM((tm, tn), F32)] * n_acc,
        compiler_params=pltpu.CompilerParams(dimension_semantics=("parallel", "parallel", "arbitrary")),
    )(*As, *Bs, *extras)
    return outs[0] if n_o == 1 else outs


def _rms_fwd(x, g, *, name, blk_w=None, blk_idx=0, off=0, width=None, out_dtype=BF):
    T = x.shape[0]
    blk_w = x.shape[1] if blk_w is None else blk_w
    width = blk_w if width is None else width
    tt = _pick(T, 512)

    def body(x_ref, g_ref, o_ref):
        xf = x_ref[:, off:off + width]
        r = lax.rsqrt(jnp.mean(xf * xf, axis=-1, keepdims=True) + EPS)
        o_ref[...] = (xf * r * g_ref[...]).astype(o_ref.dtype)

    return pl.pallas_call(
        body, name=name, out_shape=jax.ShapeDtypeStruct((T, width), out_dtype), grid=(T // tt,),
        in_specs=[pl.BlockSpec((tt, blk_w), lambda i: (i, blk_idx)), pl.BlockSpec((1, width), lambda i: (0, 0))],
        out_specs=pl.BlockSpec((tt, width), lambda i: (i, 0)),
    )(x, g)


def _rms_bwd(dy, x, g, *, name, blk_w=None, blk_idx=0, off=0, width=None, add=None, out_dtypes=(F32,)):
    T = x.shape[0]
    blk_w = x.shape[1] if blk_w is None else blk_w
    width = blk_w if width is None else width
    tt = _pick(T, 512)
    has_add = add is not None
    n_dx = len(out_dtypes)

    def body(*refs):
        dy_ref, x_ref, g_ref = refs[:3]
        dx_refs, dg_ref = refs[3 + has_add:3 + has_add + n_dx], refs[-1]
        xf = x_ref[:, off:off + width]
        d = dy_ref[...].astype(F32)
        r = lax.rsqrt(jnp.mean(xf * xf, axis=-1, keepdims=True) + EPS)
        gd = d * g_ref[...]
        dx = r * gd - xf * (r * r * r) * jnp.mean(gd * xf, axis=-1, keepdims=True)
        if has_add:
            dx = dx + refs[3][...]
        for dx_ref in dx_refs:
            dx_ref[...] = dx.astype(dx_ref.dtype)

        @pl.when(pl.program_id(0) == 0)
        def _():
            dg_ref[...] = jnp.zeros_like(dg_ref)

        dg_ref[...] += jnp.broadcast_to(jnp.sum(d * xf * r, axis=0, keepdims=True), dg_ref.shape)

    row = pl.BlockSpec((tt, width), lambda i: (i, 0))
    in_specs = [row, pl.BlockSpec((tt, blk_w), lambda i: (i, blk_idx)), pl.BlockSpec((1, width), lambda i: (0, 0))]
    args = [dy, x, g]
    if has_add:
        in_specs.append(row)
        args.append(add)
    return pl.pallas_call(
        body, name=name,
        out_shape=tuple(jax.ShapeDtypeStruct((T, width), dt) for dt in out_dtypes) + (jax.ShapeDtypeStruct((8, width), F32),),
        grid=(T // tt,), in_specs=in_specs,
        out_specs=(row,) * n_dx + (pl.BlockSpec((8, width), lambda i: (0, 0)),),
        compiler_params=pltpu.CompilerParams(dimension_semantics=("arbitrary",)),
    )(*args)


def _rope_tables(pos_col, freq_lane):
    T = pos_col.shape[0]
    tt = _pick(T, 512)

    def body(p_ref, f_ref, c_ref, s_ref):
        ang = p_ref[...] * f_ref[...]
        lane = lax.broadcasted_iota(jnp.int32, ang.shape, 1)
        c_ref[...] = jnp.where(lane < QK_HEAD, jnp.cos(ang), 0.0)
        sn = jnp.sin(ang)
        s_ref[...] = jnp.where((lane >= QK_NOPE) & (lane < QK_NOPE + 16), -sn,
                               jnp.where((lane >= QK_NOPE + 16) & (lane < QK_HEAD), sn, 0.0))

    return pl.pallas_call(
        body, name="rope_tables", out_shape=(jax.ShapeDtypeStruct((T, HP), F32),) * 2, grid=(T // tt,),
        in_specs=[pl.BlockSpec((tt, 1), lambda i: (i, 0)), pl.BlockSpec((1, HP), lambda i: (0, 0))],
        out_specs=(pl.BlockSpec((tt, HP), lambda i: (i, 0)),) * 2,
    )(pos_col, freq_lane)


def _swap_rope_halves(n):
    lane = lax.broadcasted_iota(jnp.int32, n.shape, 1)
    lo = (lane >= QK_NOPE) & (lane < QK_NOPE + 16)
    hi = (lane >= QK_NOPE + 16) & (lane < QK_HEAD)
    return jnp.where(lo, pltpu.roll(n, HP - 16, 1), jnp.where(hi, pltpu.roll(n, 16, 1), 0.0))


def _qk_prep_fwd(raw, kpe, gain, C, S, *, name, kpe_blk=0, out_scale=1.0):
    T = raw.shape[0]
    tt = _pick(T, 256)
    has_kpe = kpe is not None

    def body(*refs):
        if has_kpe:
            raw_ref, kpe_ref, g_ref, c_ref, s_ref, o_ref = refs
        else:
            raw_ref, g_ref, c_ref, s_ref, o_ref = refs
        for h in range(N_HEADS):
            hs = slice(HP * h, HP * (h + 1))
            xr = raw_ref[:, hs] + kpe_ref[...] if has_kpe else raw_ref[:, hs]
            r = lax.rsqrt(jnp.sum(xr * xr, axis=-1, keepdims=True) * (1.0 / QK_HEAD) + EPS)
            n = xr * r * g_ref[...]
            o_ref[:, hs] = ((n * c_ref[...] + _swap_rope_halves(n) * s_ref[...]) * out_scale).astype(o_ref.dtype)

    heads = pl.BlockSpec((tt, N_HEADS * HP), lambda i: (i, 0))
    shared = pl.BlockSpec((tt, HP), lambda i: (i, 0))
    kpe_spec = pl.BlockSpec((tt, HP), lambda i: (i, kpe_blk))
    in_specs = [heads] + ([kpe_spec] if has_kpe else []) + [pl.BlockSpec((1, HP), lambda i: (0, 0)), shared, shared]
    args = [raw] + ([kpe] if has_kpe else []) + [gain, C, S]
    return pl.pallas_call(
        body, name=name, out_shape=jax.ShapeDtypeStruct(raw.shape, BF), grid=(T // tt,),
        in_specs=in_specs, out_specs=heads,
    )(*args)


def _qk_prep_bwd(dout, raw, kpe, gain, C, S, *, name, kpe_blk=0, in_scale=1.0):
    T = raw.shape[0]
    tt = _pick(T, 256)
    has_kpe = kpe is not None

    def body(*refs):
        if has_kpe:
            d_ref, raw_ref, kpe_ref, g_ref, c_ref, s_ref, dx_ref, dg_ref, dkpe_ref = refs
        else:
            d_ref, raw_ref, g_ref, c_ref, s_ref, dx_ref, dg_ref = refs
        dg = jnp.zeros((1, HP), F32)
        dkpe = jnp.zeros((tt, HP), F32)
        for h in range(N_HEADS):
            hs = slice(HP * h, HP * (h + 1))
            xr = raw_ref[:, hs] + kpe_ref[...] if has_kpe else raw_ref[:, hs]
            d = d_ref[:, hs].astype(F32) * in_scale
            r = lax.rsqrt(jnp.sum(xr * xr, axis=-1, keepdims=True) * (1.0 / QK_HEAD) + EPS)
            dn = d * c_ref[...] + _swap_rope_halves(d * s_ref[...])
            gd = dn * g_ref[...]
            dx = r * gd - xr * (r * r * r) * (jnp.sum(gd * xr, axis=-1, keepdims=True) * (1.0 / QK_HEAD))
            dx_ref[:, hs] = dx.astype(dx_ref.dtype)
            dg = dg + jnp.sum(dn * xr * r, axis=0, keepdims=True)
            dkpe = dkpe + dx

        @pl.when(pl.program_id(0) == 0)
        def _():
            dg_ref[...] = jnp.zeros_like(dg_ref)

        dg_ref[...] += jnp.broadcast_to(dg, dg_ref.shape)
        if has_kpe:
            dkpe_ref[...] = dkpe

    heads = pl.BlockSpec((tt, N_HEADS * HP), lambda i: (i, 0))
    shared = pl.BlockSpec((tt, HP), lambda i: (i, 0))
    kpe_spec = pl.BlockSpec((tt, HP), lambda i: (i, kpe_blk))
    in_specs = [heads, heads] + ([kpe_spec] if has_kpe else []) + [pl.BlockSpec((1, HP), lambda i: (0, 0)), shared, shared]
    args = [dout, raw] + ([kpe] if has_kpe else []) + [gain, C, S]
    out_shape = [jax.ShapeDtypeStruct(raw.shape, BF), jax.ShapeDtypeStruct((8, HP), F32)]
    out_specs = [heads, pl.BlockSpec((8, HP), lambda i: (0, 0))]
    if has_kpe:
        out_shape.append(jax.ShapeDtypeStruct((T, HP), F32))
        out_specs.append(shared)
    return pl.pallas_call(
        body, name=name, out_shape=tuple(out_shape), grid=(T // tt,),
        in_specs=in_specs, out_specs=tuple(out_specs),
        compiler_params=pltpu.CompilerParams(dimension_semantics=("arbitrary",)),
    )(*args)


ATTN_SCALE = 1.0 / math.sqrt(QK_HEAD)
LOG2E = 1.0 / math.log(2.0)
Q_SCALE = ATTN_SCALE * LOG2E


def _attn_fwd(q, k, v):
    T = q.shape[0]
    tq = _pick(T, 256)

    def body(q_ref, k_ref, v_ref, o_ref, lse_ref):
        s = lax.dot_general(q_ref[...], k_ref[...], NT, preferred_element_type=F32)
        m = jnp.max(s, axis=-1, keepdims=True)
        p = jnp.exp2(s - m)
        l = jnp.sum(p, axis=-1, keepdims=True)
        o = jnp.dot(p.astype(BF), v_ref[...], preferred_element_type=F32)
        o_ref[...] = o / l
        lse_ref[...] = jnp.broadcast_to(m + jnp.log2(l), lse_ref.shape)

    qs = pl.BlockSpec((tq, HP), lambda h, i: (i, h))
    kv = pl.BlockSpec((T, HP), lambda h, i: (0, h))
    return pl.pallas_call(
        body, name="attn_fwd", out_shape=(jax.ShapeDtypeStruct(q.shape, F32),) * 2, grid=(N_HEADS, T // tq),
        in_specs=[qs, kv, kv], out_specs=(qs, qs),
        compiler_params=pltpu.CompilerParams(dimension_semantics=("parallel", "parallel")),
    )(q, k, v)


def _attn_bwd(q, k, v, do, o, lse):
    T = q.shape[0]
    tb = _pick(T, 512)
    nb = T // tb

    def body(q_ref, k_ref, v_ref, do_ref, o_ref, lse_ref, dq_ref, dk_ref, dv_ref, delta_scr, dob_scr):
        dq_ref[...] = jnp.zeros_like(dq_ref)

        def per_q_tile(i, carry):
            qs = pl.ds(pl.multiple_of(i * tb, tb), tb)
            doi = do_ref[qs, :]
            delta_scr[qs, :] = jnp.sum(doi * o_ref[qs, :], axis=-1, keepdims=True)
            dob_scr[qs, :] = doi.astype(BF)
            return carry

        lax.fori_loop(0, nb, per_q_tile, 0)

        def k_loop(j, carry):
            ks = pl.ds(pl.multiple_of(j * tb, tb), tb)
            kj, vj = k_ref[ks, :], v_ref[ks, :]

            def q_loop(i, acc):
                dk_acc, dv_acc = acc
                qs = pl.ds(pl.multiple_of(i * tb, tb), tb)
                qi = q_ref[qs, :]
                delta = delta_scr[qs, :]
                dob = dob_scr[qs, :]
                s = lax.dot_general(qi, kj, NT, preferred_element_type=F32)
                p = jnp.exp2(s - lse_ref[qs, 0:1])
                dp = lax.dot_general(dob, vj, NT, preferred_element_type=F32)
                ds = (p * (dp - delta)).astype(BF)
                dv_acc = dv_acc + lax.dot_general(p.astype(BF), dob, TN, preferred_element_type=F32)
                dk_acc = dk_acc + lax.dot_general(ds, qi, TN, preferred_element_type=F32)
                dq_ref[qs, :] += jnp.dot(ds, kj, preferred_element_type=F32)
                return dk_acc, dv_acc

            zero = jnp.zeros((tb, HP), F32)
            dk_acc, dv_acc = lax.fori_loop(0, nb, q_loop, (zero, zero))
            dk_ref[ks, :] = dk_acc
            dv_ref[ks, :] = dv_acc.astype(dv_ref.dtype)
            return carry

        lax.fori_loop(0, nb, k_loop, 0)

    spec = pl.BlockSpec((T, HP), lambda h: (0, h))
    return pl.pallas_call(
        body, name="attn_bwd",
        out_shape=(jax.ShapeDtypeStruct(q.shape, F32), jax.ShapeDtypeStruct(q.shape, F32), jax.ShapeDtypeStruct(q.shape, BF)),
        grid=(N_HEADS,), in_specs=[spec] * 6, out_specs=(spec,) * 3,
        scratch_shapes=[pltpu.VMEM((T, 1), F32), pltpu.VMEM((T, HP), BF)],
        compiler_params=pltpu.CompilerParams(dimension_semantics=("parallel",), vmem_limit_bytes=2 * 15 * T * HP * 2 + (8 << 20)),
    )(q, k, v, do, o, lse)


CONV_TC = 512
CONV_PAD = CONV_WIDTH // 2


def _halo_specs(tr, col_of):
    r8 = tr // 8
    cur = pl.BlockSpec((tr, CONV_TC), lambda j, i: (i, col_of(j)))
    prev = pl.BlockSpec((8, CONV_TC), lambda j, i: (jnp.maximum(i * r8 - 1, 0), col_of(j)))

    def nxt_map(j, i, n8):
        return (jnp.minimum((i + 1) * r8, n8 - 1), col_of(j))

    return cur, prev, nxt_map


def _with_halo(prev_ref, cur_ref, next_ref, i, n_i):
    prev = jnp.where(i == 0, 0.0, prev_ref[...].astype(F32))
    nxt = jnp.where(i == n_i - 1, 0.0, next_ref[...].astype(F32))
    return jnp.concatenate([prev, cur_ref[...].astype(F32), nxt], axis=0)


def _conv_fwd(u, w8, b):
    T = u.shape[0]
    tr = _pick(T, 512)
    n_i = T // tr
    c0 = U_XBC // CONV_TC
    cur, prev, nxt_map = _halo_specs(tr, lambda j: c0 + j)
    nxt = pl.BlockSpec((8, CONV_TC), functools.partial(nxt_map, n8=T // 8))

    def body(p_ref, c_ref, n_ref, w_ref, b_ref, pre_ref, act_ref):
        i = pl.program_id(1)
        full = _with_halo(p_ref, c_ref, n_ref, i, n_i)
        acc = jnp.broadcast_to(b_ref[...], (tr, CONV_TC))
        for kk in range(CONV_WIDTH):
            acc = acc + full[8 - CONV_PAD + kk:8 - CONV_PAD + kk + tr, :] * w_ref[kk:kk + 1, :]
        pre_ref[...] = acc
        act_ref[...] = _silu(acc)

    out = pl.BlockSpec((tr, CONV_TC), lambda j, i: (i, j))
    return pl.pallas_call(
        body, name="conv_fwd", out_shape=(jax.ShapeDtypeStruct((T, XBC_DIM), F32),) * 2,
        grid=(XBC_DIM // CONV_TC, n_i),
        in_specs=[prev, cur, nxt, pl.BlockSpec((8, CONV_TC), lambda j, i: (0, j)), pl.BlockSpec((1, CONV_TC), lambda j, i: (0, j))],
        out_specs=(out, out),
    )(u, u, u, w8, b)


def _conv_dpre(dacts, pre, col0, *, name):
    T, width = dacts[0].shape
    tt = _pick(T, 512)
    n_d = len(dacts)
    c0 = col0 // CONV_TC

    def body(*refs):
        d = refs[0][...]
        for r in refs[1:n_d]:
            d = d + r[...]
        refs[n_d + 1][...] = d * _dsilu(refs[n_d][...])

    blk = pl.BlockSpec((tt, CONV_TC), lambda j, i: (i, j))
    return pl.pallas_call(
        body, name=name, out_shape=jax.ShapeDtypeStruct((T, width), F32), grid=(width // CONV_TC, T // tt),
        in_specs=[blk] * n_d + [pl.BlockSpec((tt, CONV_TC), lambda j, i: (i, c0 + j))], out_specs=blk,
    )(*dacts, pre)


def _conv_bwd(dpre, u, w8, col0, *, name):
    T, width = dpre.shape
    tr = _pick(T, 512)
    n_i = T // tr
    cd = col0 // CONV_TC
    cx = (U_XBC + col0) // CONV_TC
    d_cur, d_prev, d_nxt_map = _halo_specs(tr, lambda j: j)
    x_cur, x_prev, x_nxt_map = _halo_specs(tr, lambda j: cx + j)
    d_nxt = pl.BlockSpec((8, CONV_TC), functools.partial(d_nxt_map, n8=T // 8))
    x_nxt = pl.BlockSpec((8, CONV_TC), functools.partial(x_nxt_map, n8=T // 8))

    def body(dp_ref, dc_ref, dn_ref, xp_ref, xc_ref, xn_ref, w_ref, dx_ref, dw_ref):
        i = pl.program_id(1)
        dfull = _with_halo(dp_ref, dc_ref, dn_ref, i, n_i)
        xfull = _with_halo(xp_ref, xc_ref, xn_ref, i, n_i)
        dcur = dc_ref[...]
        dx = jnp.zeros((tr, CONV_TC), F32)
        rows = []
        for kk in range(CONV_WIDTH):
            dx = dx + dfull[8 + CONV_PAD - kk:8 + CONV_PAD - kk + tr, :] * w_ref[kk:kk + 1, :]
            rows.append(jnp.sum(dcur * xfull[8 - CONV_PAD + kk:8 - CONV_PAD + kk + tr, :], axis=0, keepdims=True))
        rows.append(jnp.sum(dcur, axis=0, keepdims=True))
        rows.append(jnp.zeros((2, CONV_TC), F32))
        dx_ref[...] = dx.astype(dx_ref.dtype)

        @pl.when(i == 0)
        def _():
            dw_ref[...] = jnp.zeros_like(dw_ref)

        dw_ref[...] += jnp.concatenate(rows, axis=0)

    out = pl.BlockSpec((tr, CONV_TC), lambda j, i: (i, j))
    return pl.pallas_call(
        body, name=name, out_shape=(jax.ShapeDtypeStruct((T, width), BF), jax.ShapeDtypeStruct((8, width), F32)),
        grid=(width // CONV_TC, n_i),
        in_specs=[d_prev, d_cur, d_nxt, x_prev, x_cur, x_nxt, pl.BlockSpec((8, CONV_TC), lambda j, i: (0, cd + j))],
        out_specs=(out, pl.BlockSpec((8, CONV_TC), lambda j, i: (0, j))),
        compiler_params=pltpu.CompilerParams(dimension_semantics=("parallel", "arbitrary")),
    )(dpre, dpre, dpre, u, u, u, w8)


N_HB = 2 * SSM_GROUPS
P_DT, P_CS, P_E, P_W = 0, HP, 2 * HP, 3 * HP
DT_BLK = (U_SMALL + S_DT) // HP


def _tri(rev, transpose=False):
    rows = lax.broadcasted_iota(jnp.int32, (CHUNK, CHUNK), 0)
    cols = lax.broadcasted_iota(jnp.int32, (CHUNK, CHUNK), 1)
    if transpose:
        rows, cols = cols, rows
    return (cols >= rows) if rev else (cols <= rows)


def _ssd_prep(u, bias8, alog8):
    T = u.shape[0]
    nc = T // CHUNK

    def body(dt_ref, bias_ref, a_ref, cols_ref, rows_ref):
        lane = lax.broadcasted_iota(jnp.int32, (CHUNK, HP), 1)
        dt = _softplus(dt_ref[...] + bias_ref[0:1, :])
        da = dt * (-jnp.exp(a_ref[0:1, :]))
        cs_f = jnp.dot(jnp.where(_tri(False), 1.0, 0.0).astype(F32), da, precision=HI, preferred_element_type=F32)
        cs_b = jnp.dot(jnp.where(_tri(True), 1.0, 0.0).astype(F32), da, precision=HI, preferred_element_type=F32)
        cs = jnp.where(lane < SSM_HEADS, cs_f, cs_b)
        tot = jnp.where(lane[0:1] < SSM_HEADS, cs_f[CHUNK - 1:CHUNK, :], cs_b[0:1, :])
        e, w = jnp.exp(cs), jnp.exp(tot - cs)
        tot8 = jnp.broadcast_to(tot, (8, HP))
        etot8 = jnp.exp(tot8)
        for b in range(N_HB):
            down = (HP - HG * b) % HP

            def rolled(v):
                return pltpu.roll(v, down, 1) if down else v

            cols_ref[b, :, P_DT:P_DT + HP] = rolled(dt)
            cs_r = rolled(cs)
            cols_ref[b, :, P_CS:P_CS + HP] = cs_r
            cols_ref[b, :, P_E:P_E + HP] = rolled(e)
            cols_ref[b, :, P_W:P_W + HP] = rolled(w)
            rows_ref[b, 0, 0:8, :] = cs_r.T[0:8, :]
            r8 = lax.broadcasted_iota(jnp.int32, (8, HP), 0)
            rows_ref[b, 0, 8:16, :] = jnp.where(r8 == 0, rolled(tot8), jnp.where(r8 == 1, rolled(etot8), 0.0))

    vec = pl.BlockSpec((8, HP), lambda c: (0, 0))
    return pl.pallas_call(
        body, name="ssd_prep",
        out_shape=(jax.ShapeDtypeStruct((N_HB, T, 4 * HP), F32), jax.ShapeDtypeStruct((N_HB, nc, 16, HP), F32)),
        grid=(nc,), in_specs=[pl.BlockSpec((CHUNK, HP), lambda c: (c, DT_BLK)), vec, vec],
        out_specs=(pl.BlockSpec((N_HB, CHUNK, 4 * HP), lambda c: (0, c, 0)), pl.BlockSpec((N_HB, 1, 16, HP), lambda c: (0, c, 0, 0))),
    )(u, bias8, alog8)


def _ssd_specs(T, rev, bwd):
    nc = T // CHUNK
    fwd_order = (lambda c: nc - 1 - c) if rev else (lambda c: c)
    cm = (lambda c: fwd_order(nc - 1 - c)) if bwd else fwd_order
    hb0 = SSM_GROUPS if rev else 0
    xs = pl.BlockSpec((CHUNK, GW), lambda c, g: (cm(c), g))
    bs = pl.BlockSpec((CHUNK, D_STATE), lambda c, g: (cm(c), D_INNER // D_STATE + g))
    cs = pl.BlockSpec((CHUNK, D_STATE), lambda c, g: (cm(c), (D_INNER + SSM_GROUPS * D_STATE) // D_STATE + g))
    cols = pl.BlockSpec((1, CHUNK, 4 * HP), lambda c, g: (hb0 + g, cm(c), 0))
    rows = pl.BlockSpec((1, 1, 16, HP), lambda c, g: (hb0 + g, cm(c), 0, 0))
    return nc, cm, xs, bs, cs, cols, rows


def _head_terms(cols_ref, rows_ref, hh, incl):
    dt = cols_ref[0, :, P_DT + hh:P_DT + hh + 1]
    col = cols_ref[0, :, P_CS + hh:P_CS + hh + 1]
    e = cols_ref[0, :, P_E + hh:P_E + hh + 1]
    w = cols_ref[0, :, P_W + hh:P_W + hh + 1]
    row = rows_ref[0, 0, hh:hh + 1, :]
    etot = rows_ref[0, 0, 9:10, hh:hh + 1]
    lmat = jnp.where(incl, jnp.exp(col - row), 0.0)
    return dt, col, row, e, w, etot, lmat


def _ssd_fwd(act, cols, rows, *, rev, name):
    T = act.shape[0]
    nc, cm, xs_s, b_s, c_s, cols_s, rows_s = _ssd_specs(T, rev, False)

    def body(x_ref, b_ref, c_ref, cols_ref, rows_ref, y_ref, st_ref, state, xdw):
        c, g = pl.program_id(0), pl.program_id(1)

        @pl.when(c == 0)
        def _():
            state[g] = jnp.zeros((D_STATE, GW), F32)

        incl = _tri(rev)
        bm, cmat = b_ref[...].astype(BF), c_ref[...].astype(BF)
        bm_t = b_ref[...].T.astype(BF)
        cb = lax.dot_general(cmat, bm, NT, preferred_element_type=F32)
        prev_all = state[g]
        st_ref[...] = prev_all
        yo_all = jnp.dot(cmat, prev_all.astype(BF), preferred_element_type=F32)
        for hh in range(HG):
            hs = slice(PH * hh, PH * (hh + 1))
            dt, col, row, e, w, etot, lmat = _head_terms(cols_ref, rows_ref, hh, incl)
            xdt = x_ref[:, hs] * dt
            xdw[:, hs] = (xdt * w).astype(BF)
            yd = jnp.dot((cb * lmat).astype(BF), xdt.astype(BF), preferred_element_type=F32)
            y_ref[:, hs] = yd + yo_all[:, hs] * e
            state[g, :, hs] = prev_all[:, hs] * etot
        state[g] += jnp.dot(bm_t, xdw[...], preferred_element_type=F32)

    return pl.pallas_call(
        body, name=name,
        out_shape=(jax.ShapeDtypeStruct((T, D_INNER), F32), jax.ShapeDtypeStruct((nc * D_STATE, D_INNER), F32)),
        grid=(nc, SSM_GROUPS), in_specs=[xs_s, b_s, c_s, cols_s, rows_s], out_specs=(xs_s, xs_s),
        scratch_shapes=[pltpu.VMEM((SSM_GROUPS, D_STATE, GW), F32), pltpu.VMEM((CHUNK, GW), BF)],
        compiler_params=pltpu.CompilerParams(dimension_semantics=("arbitrary", "arbitrary")),
    )(act, act, act, cols, rows)


def _ssd_bwd(act, cols, rows, states, dy, *, rev, name):
    T = act.shape[0]
    nc, cm, xs_s, b_s, c_s, cols_s, rows_s = _ssd_specs(T, rev, True)

    def body(x_ref, b_ref, c_ref, cols_ref, rows_ref, st_ref, dy_ref, dx_ref, db_ref, dc_ref, dsel_ref, dtot_ref,
             dstate, dye, xdw, dcs_rows, dcb):
        c, g = pl.program_id(0), pl.program_id(1)

        @pl.when(c == 0)
        def _():
            dstate[g] = jnp.zeros((D_STATE, GW), F32)

        incl, incl_t = _tri(rev), _tri(rev, transpose=True)
        bm, cmat = b_ref[...].astype(BF), c_ref[...].astype(BF)
        cm_t = c_ref[...].T.astype(BF)
        cb = lax.dot_general(cmat, bm, NT, preferred_element_type=F32)
        cb_t = lax.dot_general(bm, cmat, NT, preferred_element_type=F32)
        prev_all, ds_all = st_ref[...], dstate[g]
        pb_all, dsb_all = prev_all.astype(BF), ds_all.astype(BF)
        cp_all = jnp.dot(cmat, pb_all, preferred_element_type=F32)
        bds_all = jnp.dot(bm, dsb_all, preferred_element_type=F32)
        dsel_ref[...] = jnp.zeros_like(dsel_ref)
        dtot_ref[...] = jnp.zeros_like(dtot_ref)
        dcs_rows[...] = jnp.zeros_like(dcs_rows)
        dcb[...] = jnp.zeros_like(dcb)
        for hh in range(HG):
            hs = slice(PH * hh, PH * (hh + 1))
            dth, col, row, e, w, etot, lmat = _head_terms(cols_ref, rows_ref, hh, incl)
            x = x_ref[:, hs]
            xdt = x * dth
            mmat = cb * lmat
            mmat_t = cb_t * jnp.where(incl_t, jnp.exp(row - col), 0.0)
            prev, ds_ = prev_all[:, hs], ds_all[:, hs]
            dyh = dy_ref[:, hs]
            dyb = dyh.astype(BF)
            dye[:, hs] = (dyh * e).astype(BF)
            xdw[:, hs] = (xdt * w).astype(BF)
            dcs_h = jnp.sum(dyh * cp_all[:, hs], axis=1, keepdims=True) * e
            dm = lax.dot_general(dyb, xdt.astype(BF), NT, preferred_element_type=F32)
            dxdt = jnp.dot(mmat_t.astype(BF), dyb, preferred_element_type=F32)
            qm = dm * mmat
            dcs_h = dcs_h + jnp.sum(qm, axis=1, keepdims=True)
            dcs_rows[hh:hh + 1, :] = jnp.sum(qm, axis=0, keepdims=True)
            dcb[...] += dm * lmat
            bds = bds_all[:, hs] * w
            dxdt = dxdt + bds
            t = jnp.sum(xdt * bds, axis=1, keepdims=True)
            dtot_ref[0, 0, 0:1, hh:hh + 1] = jnp.sum(t, axis=0, keepdims=True) + jnp.sum(ds_ * prev) * etot
            dsel_ref[0, :, HP + hh:HP + hh + 1] = dcs_h - t
            dsel_ref[0, :, hh:hh + 1] = jnp.sum(dxdt * x, axis=1, keepdims=True)
            dx_ref[:, hs] = dxdt * dth
            dstate[g, :, hs] = ds_ * etot
        dye_all, xdw_all, dcb_all = dye[...], xdw[...], dcb[...]
        dstate[g] += jnp.dot(cm_t, dye_all, preferred_element_type=F32)
        dc_ref[...] = (lax.dot_general(dye_all, pb_all, NT, preferred_element_type=F32)
                       + jnp.dot(dcb_all.astype(BF), bm, preferred_element_type=F32))
        db_ref[...] = (lax.dot_general(xdw_all, dsb_all, NT, preferred_element_type=F32)
                       + jnp.dot(dcb_all.T.astype(BF), cmat, preferred_element_type=F32))
        dsel_ref[0, :, HP:2 * HP] -= dcs_rows[...].T

    bc_out = pl.BlockSpec((CHUNK, D_STATE), lambda c, g: (cm(c), g))
    return pl.pallas_call(
        body, name=name,
        out_shape=(jax.ShapeDtypeStruct((T, D_INNER), F32), jax.ShapeDtypeStruct((T, SSM_GROUPS * D_STATE), F32),
                   jax.ShapeDtypeStruct((T, SSM_GROUPS * D_STATE), F32), jax.ShapeDtypeStruct((SSM_GROUPS, T, 2 * HP), F32),
                   jax.ShapeDtypeStruct((SSM_GROUPS, nc, 8, HP), F32)),
        grid=(nc, SSM_GROUPS), in_specs=[xs_s, b_s, c_s, cols_s, rows_s, xs_s, xs_s],
        out_specs=(xs_s, bc_out, bc_out, pl.BlockSpec((1, CHUNK, 2 * HP), lambda c, g: (g, cm(c), 0)),
                   pl.BlockSpec((1, 1, 8, HP), lambda c, g: (g, cm(c), 0, 0))),
        scratch_shapes=[pltpu.VMEM((SSM_GROUPS, D_STATE, GW), F32), pltpu.VMEM((CHUNK, GW), BF), pltpu.VMEM((CHUNK, GW), BF),
                        pltpu.VMEM((CHUNK, CHUNK), F32), pltpu.VMEM((CHUNK, CHUNK), F32)],
        compiler_params=pltpu.CompilerParams(dimension_semantics=("arbitrary", "arbitrary")),
    )(act, act, act, cols, rows, states, dy)


def _ssd_prep_bwd(u, bias8, alog8, dsel_f, dtot_f, dsel_b, dtot_b):
    T = u.shape[0]
    nc = T // CHUNK

    def body(dt_ref, bias_ref, a_ref, sf_ref, tf_ref, sb_ref, tb_ref, ddt_ref, da_ref, dbias_ref):
        @pl.when(pl.program_id(0) == 0)
        def _():
            da_ref[...] = jnp.zeros_like(da_ref)
            dbias_ref[...] = jnp.zeros_like(dbias_ref)

        lane = lax.broadcasted_iota(jnp.int32, (CHUNK, HP), 1)
        pre = dt_ref[...] + bias_ref[0:1, :]
        dt = _softplus(pre)
        a = -jnp.exp(a_ref[0:1, :])
        ddt_x, dcs, dtot = jnp.zeros((CHUNK, HP), F32), jnp.zeros((CHUNK, HP), F32), jnp.zeros((8, HP), F32)
        for b in range(N_HB):
            s_ref, t_ref, g = (sf_ref, tf_ref, b) if b < SSM_GROUPS else (sb_ref, tb_ref, b - SSM_GROUPS)
            mine = (lane >= HG * b) & (lane < HG * (b + 1))

            def up(v):
                return pltpu.roll(v, HG * b, 1) if b else v

            ddt_x = ddt_x + jnp.where(mine, up(s_ref[g, :, 0:HP]), 0.0)
            dcs = dcs + jnp.where(mine, up(s_ref[g, :, HP:2 * HP]), 0.0)
            dtot = dtot + jnp.where(mine[0:8], up(t_ref[g, 0]), 0.0)
        tri_f = jnp.where(_tri(False, transpose=True), 1.0, 0.0).astype(F32)
        tri_b = jnp.where(_tri(True, transpose=True), 1.0, 0.0).astype(F32)
        dda = jnp.where(lane < SSM_HEADS, jnp.dot(tri_f, dcs, precision=HI, preferred_element_type=F32),
                        jnp.dot(tri_b, dcs, precision=HI, preferred_element_type=F32)) + dtot[0:1, :]
        dpre = (ddt_x + dda * a) * jax.nn.sigmoid(pre)
        ddt_ref[...] = jnp.where(lane < 2 * SSM_HEADS, dpre, 0.0)
        dbias_ref[...] += jnp.broadcast_to(jnp.sum(dpre, axis=0, keepdims=True), (8, HP))
        da_ref[...] += jnp.broadcast_to(jnp.sum(dda * dt, axis=0, keepdims=True) * a, (8, HP))

    vec = pl.BlockSpec((8, HP), lambda c: (0, 0))
    sel = pl.BlockSpec((SSM_GROUPS, CHUNK, 2 * HP), lambda c: (0, c, 0))
    tot = pl.BlockSpec((SSM_GROUPS, 1, 8, HP), lambda c: (0, c, 0, 0))
    tile = pl.BlockSpec((CHUNK, HP), lambda c: (c, 0))
    return pl.pallas_call(
        body, name="ssd_prep_bwd",
        out_shape=(jax.ShapeDtypeStruct((T, HP), F32), jax.ShapeDtypeStruct((8, HP), F32), jax.ShapeDtypeStruct((8, HP), F32)),
        grid=(nc,), in_specs=[pl.BlockSpec((CHUNK, HP), lambda c: (c, DT_BLK)), vec, vec, sel, tot, sel, tot],
        out_specs=(tile, vec, vec),
        compiler_params=pltpu.CompilerParams(dimension_semantics=("arbitrary",)),
    )(u, bias8, alog8, dsel_f, dtot_f, dsel_b, dtot_b)


def _ssm_combine_fwd(y_f, y_b, act, u, dskip, gain):
    T = y_f.shape[0]
    tt = _pick(T, 256)

    def body(yf_ref, yb_ref, x_ref, z_ref, ds_ref, g_ref, y_ref, m_ref):
        y = yf_ref[...] + yb_ref[...] + ds_ref[...] * x_ref[...]
        y2 = y * _silu(z_ref[...])
        r = lax.rsqrt(jnp.mean(y2 * y2, axis=-1, keepdims=True) + EPS)
        y_ref[...] = y
        m_ref[...] = (y2 * r * g_ref[...]).astype(m_ref.dtype)

    blk = pl.BlockSpec((tt, GW), lambda i, g: (i, g))
    vec = pl.BlockSpec((1, GW), lambda i, g: (0, g))
    return pl.pallas_call(
        body, name="ssm_combine_fwd",
        out_shape=(jax.ShapeDtypeStruct((T, D_INNER), F32), jax.ShapeDtypeStruct((T, D_INNER), BF)),
        grid=(T // tt, SSM_GROUPS), in_specs=[blk, blk, blk, blk, vec, vec], out_specs=(blk, blk),
    )(y_f, y_b, act, u, dskip, gain)


def _ssm_combine_bwd(dm, y, act, u, dskip, gain):
    T = y.shape[0]
    tt = _pick(T, 256)

    def body(dm_ref, y_ref, x_ref, z_ref, ds_ref, g_ref, dy_ref, dz_ref, dxs_ref, dg_ref, dsk_ref):
        z = z_ref[...]
        y = y_ref[...]
        x = x_ref[...]
        sz = _silu(z)
        y2 = y * sz
        r = lax.rsqrt(jnp.mean(y2 * y2, axis=-1, keepdims=True) + EPS)
        d = dm_ref[...]
        gd = d * g_ref[...]
        dy2 = r * gd - y2 * (r * r * r) * jnp.mean(gd * y2, axis=-1, keepdims=True)
        dy = dy2 * sz
        dy_ref[...] = dy
        dz_ref[...] = (dy2 * y * _dsilu(z)).astype(dz_ref.dtype)
        dxs_ref[...] = dy * ds_ref[...]

        @pl.when(pl.program_id(1) == 0)
        def _():
            dg_ref[...] = jnp.zeros_like(dg_ref)
            dsk_ref[...] = jnp.zeros_like(dsk_ref)

        dg_ref[...] += jnp.broadcast_to(jnp.sum(d * y2 * r, axis=0, keepdims=True), dg_ref.shape)
        lane_sum = jnp.broadcast_to(jnp.sum(dy * x, axis=0, keepdims=True), (8, GW))
        src = lax.broadcasted_iota(jnp.int32, (GW, HP), 0)
        head = lax.broadcasted_iota(jnp.int32, (GW, HP), 1)
        to_head = jnp.where((src >= PH * head) & (src < PH * (head + 1)), 1.0, 0.0).astype(F32)
        dsk_ref[...] += jnp.dot(lane_sum, to_head, precision=HI, preferred_element_type=F32)

    blk = pl.BlockSpec((tt, GW), lambda g, i: (i, g))
    vec = pl.BlockSpec((1, GW), lambda g, i: (0, g))
    acc = pl.BlockSpec((8, GW), lambda g, i: (0, g))
    return pl.pallas_call(
        body, name="ssm_combine_bwd",
        out_shape=(jax.ShapeDtypeStruct((T, D_INNER), F32), jax.ShapeDtypeStruct((T, D_INNER), BF),
                   jax.ShapeDtypeStruct((T, D_INNER), F32), jax.ShapeDtypeStruct((8, D_INNER), F32),
                   jax.ShapeDtypeStruct((8, SSM_GROUPS * HP), F32)),
        grid=(SSM_GROUPS, T // tt), in_specs=[blk, blk, blk, blk, vec, vec],
        out_specs=(blk, blk, blk, acc, pl.BlockSpec((8, HP), lambda g, i: (0, g))),
        compiler_params=pltpu.CompilerParams(dimension_semantics=("parallel", "arbitrary")),
    )(dm, y, act, u, dskip, gain)


def _loss_head(y, target):
    T, D = y.shape
    tt = _pick(T, 512)

    def body(y_ref, t_ref, dy_ref, dyb_ref, l_ref):
        e = y_ref[...] - t_ref[...]
        dy_ref[...] = e * (1.0 / D)
        dyb_ref[...] = (e * (1.0 / D)).astype(dyb_ref.dtype)

        @pl.when(pl.program_id(0) == 0)
        def _():
            l_ref[...] = jnp.zeros_like(l_ref)

        l_ref[...] += jnp.sum(e * e) * (0.5 / D)

    blk = pl.BlockSpec((tt, D), lambda i: (i, 0))
    return pl.pallas_call(
        body, name="loss_head",
        out_shape=(jax.ShapeDtypeStruct((T, D), F32), jax.ShapeDtypeStruct((T, D), BF), jax.ShapeDtypeStruct((8, 128), F32)),
        grid=(T // tt,), in_specs=[blk, blk], out_specs=(blk, blk, pl.BlockSpec((8, 128), lambda i: (0, 0))),
        compiler_params=pltpu.CompilerParams(dimension_semantics=("arbitrary",)),
    )(y, target)


def _adamw(w, g, m, v, *, name):
    R, C = w.shape
    cap = max(8, (1 << 18) // C)
    tr = R
    if R % 8 == 0:
        tr = 8
        for cand in range(8, min(R, cap) + 1, 8):
            if R % cand == 0:
                tr = cand

    def body(w_ref, g_ref, m_ref, v_ref, d_ref, nm_ref, nv_ref):
        gg = g_ref[...]
        nm = ADAM_B1 * m_ref[...] + (1.0 - ADAM_B1) * gg
        nv = ADAM_B2 * v_ref[...] + (1.0 - ADAM_B2) * jnp.square(gg)
        m_hat = nm / (1.0 - ADAM_B1 ** ADAM_STEP)
        v_hat = nv / (1.0 - ADAM_B2 ** ADAM_STEP)
        d_ref[...] = -ADAM_LR * (m_hat / (jnp.sqrt(v_hat) + ADAM_EPS) + ADAM_WD * w_ref[...])
        nm_ref[...] = nm
        nv_ref[...] = nv

    blk = pl.BlockSpec((tr, C), lambda i: (i, 0))
    return pl.pallas_call(
        body, name=name, out_shape=(jax.ShapeDtypeStruct((R, C), F32),) * 3, grid=(R // tr,),
        in_specs=[blk] * 4, out_specs=(blk,) * 3,
    )(w, g, m, v)


ANY = pl.BlockSpec(memory_space=pl.ANY)


def _chip_peers():
    x, y, c = lax.axis_index("x"), lax.axis_index("y"), lax.axis_index("c")
    return x, y, c, [(1 - x, y), (x, 1 - y), (1 - x, 1 - y)]


def _half_rows(c, rh):
    return pl.ds(pl.multiple_of(c * rh, 16), rh)


def _my_chip():
    return 2 * lax.axis_index("x") + lax.axis_index("y")


def _gather_chips(wb, wf):
    rh = wb.shape[0] // 2

    def body(wb_ref, wf_ref, ob_ref, of_ref, send_sems, recv_sems):
        x, y, c, peers = _chip_peers()
        me = 2 * x + y
        half, other = _half_rows(c, rh), _half_rows(1 - c, rh)

        def chip_copy(k, slot):
            px, py = peers[k]
            return pltpu.make_async_remote_copy(
                src_ref=wb_ref.at[half], dst_ref=ob_ref.at[slot, half], send_sem=send_sems.at[k], recv_sem=recv_sems.at[k],
                device_id=(px, py, c), device_id_type=MESH)

        def passed_on(k, slot, rows):
            return pltpu.make_async_remote_copy(
                src_ref=ob_ref.at[slot, rows], dst_ref=ob_ref.at[slot, rows], send_sem=send_sems.at[3 + k],
                recv_sem=recv_sems.at[3 + k], device_id=(x, y, 1 - c), device_id_type=MESH)

        def small_copy(k, slot):
            px, py = peers[k]
            return pltpu.make_async_remote_copy(
                src_ref=wf_ref, dst_ref=of_ref.at[slot], send_sem=send_sems.at[6 + k], recv_sem=recv_sems.at[6 + k],
                device_id=(px, py, c), device_id_type=MESH)

        sends = [chip_copy(k, me) for k in range(3)] + [small_copy(k, me) for k in range(3)]
        for cp in sends:
            cp.start()
        chip_of = [2 * px + py for px, py in peers]
        for k in range(3):
            chip_copy(k, chip_of[k]).wait_recv()
            cp = passed_on(k, chip_of[k], half)
            cp.start()
            sends.append(cp)
        for k in range(3):
            passed_on(k, chip_of[k], other).wait_recv()
            small_copy(k, chip_of[k]).wait_recv()
        for cp in sends:
            cp.wait_send()

    ob, of = pl.pallas_call(
        body, name="gather_weights",
        out_shape=(jax.ShapeDtypeStruct((4,) + wb.shape, wb.dtype), jax.ShapeDtypeStruct((4,) + wf.shape, wf.dtype)),
        in_specs=[ANY, ANY], out_specs=(ANY, ANY),
        scratch_shapes=[pltpu.SemaphoreType.DMA((9,)), pltpu.SemaphoreType.DMA((9,))],
    )(wb, wf)
    me = _my_chip()
    return lax.dynamic_update_slice(ob, wb[None], (me, 0, 0)), lax.dynamic_update_slice(of, wf[None], (me, 0, 0))


def _halves_to_sibling(gp):
    rh = gp.shape[1] // 2

    def body(gp_ref, o_ref, send_sem, recv_sem):
        x, y, c = lax.axis_index("x"), lax.axis_index("y"), lax.axis_index("c")
        cp = pltpu.make_async_remote_copy(src_ref=gp_ref.at[:, _half_rows(1 - c, rh), :], dst_ref=o_ref, send_sem=send_sem,
                                          recv_sem=recv_sem, device_id=(x, y, 1 - c), device_id_type=MESH)
        cp.start()
        cp.wait()

    return pl.pallas_call(
        body, name="halves_to_sibling", out_shape=jax.ShapeDtypeStruct((gp.shape[0], rh, gp.shape[2]), gp.dtype),
        in_specs=[ANY], out_specs=ANY, scratch_shapes=[pltpu.SemaphoreType.DMA, pltpu.SemaphoreType.DMA],
    )(gp)


def _row_tile(rows, cap=1024):
    tr = 16
    for cand in range(16, cap + 1, 16):
        if rows % cand == 0:
            tr = cand
    return tr


def _add_halves(gp, sib, core):
    n, rh, C = sib.shape
    tr = _row_tile(rh)
    nt = rh // tr

    def body(c_ref, g_ref, s_ref, o_ref):
        o_ref[...] = (g_ref[...].astype(F32) + s_ref[...].astype(F32)).astype(o_ref.dtype)

    blk = pl.BlockSpec((1, tr, C), lambda j, i, c: (j, i, 0))
    return pl.pallas_call(
        body, name="add_halves", out_shape=jax.ShapeDtypeStruct(sib.shape, sib.dtype),
        grid_spec=pltpu.PrefetchScalarGridSpec(
            num_scalar_prefetch=1, grid=(n, nt),
            in_specs=[pl.BlockSpec((1, tr, C), lambda j, i, c: (j, c[0] * nt + i, 0)), blk], out_specs=blk),
    )(core, gp, sib)


def _join_halves(mine):
    rh = mine.shape[0]

    def body(m_ref, o_ref, send_sem, recv_sem):
        x, y, c = lax.axis_index("x"), lax.axis_index("y"), lax.axis_index("c")
        half, other = _half_rows(c, rh), _half_rows(1 - c, rh)

        def copy(rows):
            return pltpu.make_async_remote_copy(src_ref=m_ref, dst_ref=o_ref.at[rows], send_sem=send_sem, recv_sem=recv_sem,
                                                device_id=(x, y, 1 - c), device_id_type=MESH)

        send = copy(half)
        send.start()
        copy(other).wait_recv()
        send.wait_send()

    out = pl.pallas_call(
        body, name="join_halves", out_shape=jax.ShapeDtypeStruct((2 * rh, mine.shape[1]), mine.dtype),
        in_specs=[ANY], out_specs=ANY, scratch_shapes=[pltpu.SemaphoreType.DMA, pltpu.SemaphoreType.DMA],
    )(mine)
    return lax.dynamic_update_slice(out, mine, (lax.axis_index("c") * rh, 0))


def _exchange_chips(gp):
    def body(gp_ref, out_ref, send_sems, recv_sems):
        x, y, c, peers = _chip_peers()
        me = 2 * x + y

        def copies(sending):
            out = []
            for k, (px, py) in enumerate(peers):
                p = 2 * px + py
                out.append(pltpu.make_async_remote_copy(
                    src_ref=gp_ref.at[p], dst_ref=out_ref.at[me if sending else p],
                    send_sem=send_sems.at[k], recv_sem=recv_sems.at[k], device_id=(px, py, c), device_id_type=MESH))
            return out

        sends = copies(True)
        for cp in sends:
            cp.start()
        for cp in copies(False):
            cp.wait_recv()
        for cp in sends:
            cp.wait_send()

    out = pl.pallas_call(
        body, name="exchange_grads", out_shape=jax.ShapeDtypeStruct(gp.shape, gp.dtype),
        in_specs=[ANY], out_specs=ANY,
        scratch_shapes=[pltpu.SemaphoreType.DMA((3,)), pltpu.SemaphoreType.DMA((3,))],
    )(gp)
    me = _my_chip()
    return lax.dynamic_update_slice(out, lax.dynamic_slice_in_dim(gp, me, 1, axis=0), (me, 0, 0))


def _sum_slots(r4):
    _, R, C = r4.shape
    tr = _row_tile(R)

    def body(r_ref, o_ref):
        acc = r_ref[0].astype(F32)
        for s in range(1, 4):
            acc = acc + r_ref[s].astype(F32)
        o_ref[...] = acc

    return pl.pallas_call(
        body, name="sum_slots", out_shape=jax.ShapeDtypeStruct((R, C), F32), grid=(R // tr,),
        in_specs=[pl.BlockSpec((4, tr, C), lambda i: (0, i, 0))], out_specs=pl.BlockSpec((tr, C), lambda i: (i, 0)),
    )(r4)


N_DEV = 8


def _allreduce_small(p):
    rs = p.shape[0]

    def body(x_ref, sum_ref, all_ref, send_sems, recv_sems, local_sem):
        x, y, c = lax.axis_index("x"), lax.axis_index("y"), lax.axis_index("c")
        me, sibling = (x, y, c), (x, y, 1 - c)
        chips = [(1 - x, y), (x, 1 - y), (1 - x, 1 - y)]

        def rows(px, py, pc):
            return all_ref.at[pl.ds((4 * px + 2 * py + pc) * rs, rs), :]

        def copy(k, block, to, src=None):
            return pltpu.make_async_remote_copy(
                src_ref=rows(*block) if src is None else src, dst_ref=rows(*block),
                send_sem=send_sems.at[k], recv_sem=recv_sems.at[k], device_id=to, device_id_type=MESH)

        mine = pltpu.make_async_copy(x_ref, rows(*me), local_sem)
        mine.start()
        first = [copy(0, me, sibling, src=x_ref)]
        first += [copy(1 + j, me, (*chip, c), src=x_ref) for j, chip in enumerate(chips)]
        for cp in first:
            cp.start()
        passed = [copy(4 + j, (*chip, c), sibling) for j, chip in enumerate(chips)]
        for j, chip in enumerate(chips):
            copy(1 + j, (*chip, c), me).wait_recv()
            passed[j].start()
        copy(0, sibling, me).wait_recv()
        for j, chip in enumerate(chips):
            copy(4 + j, (*chip, 1 - c), me).wait_recv()
        for cp in first + passed:
            cp.wait_send()
        mine.wait()
        acc = all_ref[0:rs, :]
        for d in range(1, N_DEV):
            acc = acc + all_ref[d * rs:(d + 1) * rs, :]
        sum_ref[...] = acc

    vmem = pl.BlockSpec(memory_space=pltpu.VMEM)
    return pl.pallas_call(
        body, name="allreduce_small", out_shape=jax.ShapeDtypeStruct((rs, 128), F32),
        in_specs=[vmem], out_specs=vmem,
        scratch_shapes=[pltpu.VMEM((N_DEV * rs, 128), F32), pltpu.SemaphoreType.DMA((7,)), pltpu.SemaphoreType.DMA((7,)),
                        pltpu.SemaphoreType.DMA],
    )(p)


WEIGHTS = ('ffn1_norm', 'ffn1_w_gate', 'ffn1_w_up', 'ffn1_w_down', 'mix_norm', 'w_in', 'q_a_norm', 'w_q_b',
           'kv_a_norm', 'w_kv_b', 'q_head_norm', 'k_head_norm', 'conv_w', 'conv_b', 'a_log_fwd', 'a_log_bwd',
           'dt_bias_fwd', 'dt_bias_bwd', 'd_skip', 'ssm_norm', 'w_attn_branch', 'w_ssm_branch', 'w_out',
           'ffn2_norm', 'ffn2_w_gate', 'ffn2_w_up', 'ffn2_w_down')
PACKED = (('ffn1_w_gate', (D_MODEL, D_FF), 1), ('ffn1_w_up', (D_MODEL, D_FF), 1), ('ffn1_w_down', (D_FF, D_MODEL), 0),
          ('w_in', (D_MODEL, sum(IN_SPLITS)), 1), ('w_q_b', (Q_LORA, N_HEADS * QK_HEAD), 1),
          ('w_kv_b', (KV_LORA, N_HEADS * (QK_NOPE + V_HEAD)), 1),
          ('w_attn_branch', (N_HEADS * V_HEAD, D_MODEL), 0), ('w_ssm_branch', (D_INNER, D_MODEL), 0),
          ('w_out', (D_MODEL, D_MODEL), 0),
          ('ffn2_w_gate', (D_MODEL, D_FF), 1), ('ffn2_w_up', (D_MODEL, D_FF), 1), ('ffn2_w_down', (D_FF, D_MODEL), 0))
PACK_W = 1024
N_CHIPS = 4
SMALL = (('ffn1_norm', 1024), ('mix_norm', 1024), ('q_a_norm', 384), ('kv_a_norm', 256), ('q_head_norm', 96),
         ('k_head_norm', 96), ('conv_b', 3072), ('a_log_fwd', 32), ('a_log_bwd', 32), ('dt_bias_fwd', 32),
         ('dt_bias_bwd', 32), ('d_skip', 32), ('ssm_norm', 2048), ('ffn2_norm', 1024),
         ('conv_w', CONV_WIDTH * XBC_DIM), ('loss', 1))


def _shard_shape(shape, axis):
    return tuple(s // N_CHIPS if a == axis else s for a, s in enumerate(shape))


def _pack_rows():
    rows = sum(math.prod(_shard_shape(shape, axis)) // PACK_W for _, shape, axis in PACKED)
    return -(-rows // 32) * 32


def _pack(shards):
    parts = [shards[name].reshape(-1, PACK_W) for name, _, _ in PACKED]
    rows = sum(p.shape[0] for p in parts)
    parts.append(jnp.zeros((_pack_rows() - rows, PACK_W), parts[0].dtype))
    return jnp.concatenate(parts, axis=0)


def _unpack(packed):
    out, r = {}, 0
    for name, shape, axis in PACKED:
        sh = _shard_shape(shape, axis)
        n = math.prod(sh) // PACK_W
        out[name] = packed[r:r + n].reshape(sh)
        r += n
    return out


def _pack_small(vals):
    parts = []
    for name, n in SMALL:
        pad = -(-n // 128) * 128 - n
        parts.append(jnp.pad(vals[name].reshape(-1).astype(F32), (0, pad)).reshape(-1, 128))
    rows = sum(p.shape[0] for p in parts)
    parts.append(jnp.zeros((-(-rows // 8) * 8 - rows, 128), F32))
    return jnp.concatenate(parts, axis=0)


def _unpack_small(packed):
    out, r = {}, 0
    for name, n in SMALL:
        k = -(-n // 128)
        out[name] = packed[r:r + k].reshape(-1)[:n]
        r += k
    return out


def _pad_heads(w, axis, per_head, lo, hi):
    shape = w.shape
    w = w.reshape(shape[:axis] + (N_HEADS, per_head) + shape[axis + 1:])
    w = lax.slice_in_dim(w, lo, hi, axis=axis + 1)
    pad = [(0, 0)] * w.ndim
    pad[axis + 1] = (0, HP - (hi - lo))
    w = jnp.pad(w, pad)
    return w.reshape(shape[:axis] + (N_HEADS * HP,) + shape[axis + 1:])


def _unpad_heads(w, axis, keep):
    shape = w.shape
    w = w.reshape(shape[:axis] + (N_HEADS, HP) + shape[axis + 1:])
    return lax.slice_in_dim(w, 0, keep, axis=axis + 1)


def _split_w_in(w):
    o = [0]
    for s in IN_SPLITS:
        o.append(o[-1] + s)
    return [w[:, o[i]:o[i + 1]] for i in range(len(IN_SPLITS))]


def _pad_w_in(w):
    cq, ckv, kpe, z, xbc, dtf, dtb, ga, gb = _split_w_in(w)
    kpe_pad = jnp.pad(kpe, ((0, 0), (QK_NOPE, HP - QK_HEAD)))
    dt_pad = jnp.pad(jnp.concatenate([dtf, dtb], axis=1), ((0, 0), (0, HP - 2 * SSM_HEADS)))
    return jnp.concatenate([z, ga, gb, xbc, cq, ckv, kpe_pad, dt_pad], axis=1)


def _unpad_w_in(g):
    z, ga, gb, xbc = g[:, U_Z:U_GA], g[:, U_GA:U_GB], g[:, U_GB:U_XBC], g[:, U_XBC:U_SMALL]
    s = g[:, U_SMALL:]
    cq, ckv = s[:, S_CQ:S_CKV], s[:, S_CKV:S_KPE]
    kpe = s[:, S_KPE + QK_NOPE:S_KPE + QK_HEAD]
    dtf, dtb = s[:, S_DT:S_DT + SSM_HEADS], s[:, S_DT + SSM_HEADS:S_DT + 2 * SSM_HEADS]
    return jnp.concatenate([cq, ckv, kpe, z, xbc, dtf, dtb, ga, gb], axis=1)


def _lanes128(parts):
    row = jnp.concatenate([p.reshape(-1) for p in parts])
    return jnp.pad(row, (0, HP - row.shape[0])).reshape(1, HP)


FF_TILE = D_FF // 2
WGRAD = BF


def _ffn_fwd(x, g, wg, wu, wd, tag):
    h = _rms_fwd(x, g, name=tag + "_norm")
    gate, up, act = _mm([h], [wg, wu], name=tag + "_up", out_dtypes=(F32, F32, BF), tm=512, tn=FF_TILE,
                        epilogue=lambda a, b: (a, b, _silu(a) * b))
    out = _mm([act], [wd], name=tag + "_down", extras=[x], epilogue=lambda acc, r: (r + 0.5 * acc,))
    return out, (h, gate, up, act)


def _ffn_bwd(dout, dout_bf, x, g, wg, wu, wd, saved, tag):
    h, gate, up, act = saved
    dgate, dup = _mm([dout_bf], [wd], name=tag + "_down_dx", tb=True, extras=[gate, up], out_dtypes=(BF, BF),
                     tm=512, tn=FF_TILE, epilogue=lambda acc, a, b: (0.5 * acc * b * _dsilu(a), 0.5 * acc * _silu(a)))
    dwd = _mm([act], [dout_bf], name=tag + "_down_dw", ta=True, tm=FF_TILE, tk=1024, out_dtypes=(WGRAD,),
              epilogue=lambda acc: (0.5 * acc,))
    dwg, dwu = _mm([h], [dgate, dup], name=tag + "_up_dw", ta=True, out_dtypes=(WGRAD, WGRAD), tm=512, tn=FF_TILE, tk=1024)
    dh = _mm([dgate, dup], [wg, wu], name=tag + "_up_dx", tb=True)
    dx, dx_bf, dg = _rms_bwd(dh, x, g, name=tag + "_norm_bwd", add=dout, out_dtypes=(F32, BF))
    return dx, dx_bf, dg, dwg, dwu, dwd


KPE_BLK = (U_SMALL + S_KPE) // HP
SMALL_BLK = U_SMALL // SMALL_W


def _local_step(x, pos_col, target, W, P):
    T = x.shape[0]
    sig = jax.nn.sigmoid
    x1, ffn1 = _ffn_fwd(x, P["ffn1_norm"], W["wg1"], W["wu1"], W["wd1"], "ffn1")
    h = _rms_fwd(x1, P["mix_norm"], name="mix_norm")
    u = _mm([h], [W["w_in"]], name="in_proj", tn=1152)
    cqn = _rms_fwd(u, P["q_a_norm"], name="q_a_norm", blk_w=SMALL_W, blk_idx=SMALL_BLK, off=S_CQ, width=Q_LORA)
    ckvn = _rms_fwd(u, P["kv_a_norm"], name="kv_a_norm", blk_w=SMALL_W, blk_idx=SMALL_BLK, off=S_CKV, width=KV_LORA)
    q_raw = _mm([cqn], [W["wq"]], name="q_proj")
    k_raw, v = _mm([ckvn], [W["wk"], W["wv"]], name="kv_proj", out_dtypes=(F32, BF))
    rc, rs = _rope_tables(pos_col, P["freq"])
    q = _qk_prep_fwd(q_raw, None, P["q_head_norm"], rc, rs, name="q_prep", out_scale=Q_SCALE)
    k = _qk_prep_fwd(k_raw, u, P["k_head_norm"], rc, rs, name="k_prep", kpe_blk=KPE_BLK)
    o, lse = _attn_fwd(q, k, v)
    pre, act = _conv_fwd(u, P["conv_w8"], P["conv_b"])
    scan_cols, scan_rows = _ssd_prep(u, P["dt_bias8"], P["a_log8"])
    y_f, st_f = _ssd_fwd(act, scan_cols, scan_rows, rev=False, name="ssd_fwd_f")
    y_b, st_b = _ssd_fwd(act, scan_cols, scan_rows, rev=True, name="ssd_fwd_b")
    ysum, m = _ssm_combine_fwd(y_f, y_b, act, u, P["d_skip_lanes"], P["ssm_norm"])
    ab = _mm([o], [W["pa"]], name="attn_branch")
    mb, merged = _mm([m], [W["pb"]], name="ssm_branch", extras=[ab, u, u], extra_offs=(0, U_GA, U_GB), out_dtypes=(F32, BF),
                     epilogue=lambda acc, a, ga, gb: (acc, sig(ga) * a + sig(gb) * acc))
    x2 = _mm([merged], [W["wo"]], name="out_proj", extras=[x1], epilogue=lambda acc, r: (r + acc,))
    y, ffn2 = _ffn_fwd(x2, P["ffn2_norm"], W["wg2"], W["wu2"], W["wd2"], "ffn2")
    dy, dy_bf, loss = _loss_head(y, target)
    dx2, dx2_bf, dg_ffn2, dwg2, dwu2, dwd2 = _ffn_bwd(dy, dy_bf, x2, P["ffn2_norm"], W["wg2"], W["wu2"], W["wd2"], ffn2,
                                                      "ffn2")

    def gate_bwd(dmrg, a, b, ga, gb):
        sa, sb = sig(ga), sig(gb)
        return dmrg * sa, dmrg * sb, dmrg * a * sa * (1.0 - sa), dmrg * b * sb * (1.0 - sb)

    dab, dmb, dga, dgb = _mm([dx2_bf], [W["wo"]], name="out_proj_dx", tb=True, extras=[ab, mb, u, u],
                             extra_offs=(0, 0, U_GA, U_GB), out_dtypes=(BF,) * 4, epilogue=gate_bwd)
    dwo = _mm([merged], [dx2_bf], name="out_proj_dw", ta=True, out_dtypes=(WGRAD,))
    dpa = _mm([o], [dab], name="attn_branch_dw", ta=True, out_dtypes=(WGRAD,))
    do = _mm([dab], [W["pa"]], name="attn_branch_dx", tb=True)
    dpb = _mm([m], [dmb], name="ssm_branch_dw", ta=True, out_dtypes=(WGRAD,))
    dm = _mm([dmb], [W["pb"]], name="ssm_branch_dx", tb=True)
    dyssd, dz, dxs_skip, dg_ssm, dskip = _ssm_combine_bwd(dm, ysum, act, u, P["d_skip_lanes"], P["ssm_norm"])
    dxs_f, db_f, dc_f, dsel_f, dtot_f = _ssd_bwd(act, scan_cols, scan_rows, st_f, dyssd, rev=False, name="ssd_bwd_f")
    dxs_b, db_b, dc_b, dsel_b, dtot_b = _ssd_bwd(act, scan_cols, scan_rows, st_b, dyssd, rev=True, name="ssd_bwd_b")
    ddt, dalog, dbias = _ssd_prep_bwd(u, P["dt_bias8"], P["a_log8"], dsel_f, dtot_f, dsel_b, dtot_b)
    dxbc, dconv = [], []
    for tag, col0, parts in (("x", 0, [dxs_f, dxs_b, dxs_skip]), ("b", D_INNER, [db_f, db_b]),
                             ("c", D_INNER + SSM_GROUPS * D_STATE, [dc_f, dc_b])):
        dpre = _conv_dpre(parts, pre, col0, name="conv_dpre_" + tag)
        dxp, dwp = _conv_bwd(dpre, u, P["conv_w8"], col0, name="conv_bwd_" + tag)
        dxbc.append(dxp)
        dconv.append(dwp)
    dconv = jnp.concatenate(dconv, axis=1)
    dq, dk, dv = _attn_bwd(q, k, v, do, o, lse)
    dq_raw, dg_qh = _qk_prep_bwd(dq, q_raw, None, P["q_head_norm"], rc, rs, name="q_prep_bwd", in_scale=ATTN_SCALE)
    dk_raw, dg_kh, dkpe = _qk_prep_bwd(dk, k_raw, u, P["k_head_norm"], rc, rs, name="k_prep_bwd", kpe_blk=KPE_BLK,
                                       in_scale=1.0 / LOG2E)
    dwq = _mm([cqn], [dq_raw], name="q_proj_dw", ta=True, out_dtypes=(WGRAD,))
    dcqn = _mm([dq_raw], [W["wq"]], name="q_proj_dx", tb=True)
    dwk, dwv = _mm([ckvn], [dk_raw, dv], name="kv_proj_dw", ta=True, out_dtypes=(WGRAD, WGRAD))
    dckvn = _mm([dk_raw, dv], [W["wk"], W["wv"]], name="kv_proj_dx", tb=True)
    dcq, dg_qa = _rms_bwd(dcqn, u, P["q_a_norm"], name="q_a_norm_bwd", blk_w=SMALL_W, blk_idx=SMALL_BLK, off=S_CQ,
                          width=Q_LORA, out_dtypes=(BF,))
    dckv, dg_kva = _rms_bwd(dckvn, u, P["kv_a_norm"], name="kv_a_norm_bwd", blk_w=SMALL_W, blk_idx=SMALL_BLK,
                            off=S_CKV, width=KV_LORA, out_dtypes=(BF,))
    du = jnp.concatenate([dz, dga, dgb] + dxbc + [dcq, dckv, dkpe.astype(BF), ddt.astype(BF)], axis=1)
    dw_in = _mm([h], [du], name="in_proj_dw", ta=True, tn=1152, out_dtypes=(WGRAD,))
    dh = _mm([du], [W["w_in"]], name="in_proj_dx", tb=True)
    dx1, dx1_bf, dg_mix = _rms_bwd(dh, x1, P["mix_norm"], name="mix_norm_bwd", add=dx2, out_dtypes=(F32, BF))
    dx, _, dg_ffn1, dwg1, dwu1, dwd1 = _ffn_bwd(dx1, dx1_bf, x, P["ffn1_norm"], W["wg1"], W["wu1"], W["wd1"], ffn1, "ffn1")
    dW = dict(wg1=dwg1, wu1=dwu1, wd1=dwd1, w_in=dw_in, wq=dwq, wk=dwk, wv=dwv, pa=dpa, pb=dpb, wo=dwo,
              wg2=dwg2, wu2=dwu2, wd2=dwd2)
    dP = dict(ffn1_norm=dg_ffn1[0], mix_norm=dg_mix[0], q_a_norm=dg_qa[0], kv_a_norm=dg_kva[0],
              q_head_norm=dg_qh[0, :QK_HEAD], k_head_norm=dg_kh[0, :QK_HEAD], conv_b=dconv[CONV_WIDTH],
              a_log_fwd=dalog[0, :SSM_HEADS], a_log_bwd=dalog[0, SSM_HEADS:2 * SSM_HEADS],
              dt_bias_fwd=dbias[0, :SSM_HEADS], dt_bias_bwd=dbias[0, SSM_HEADS:2 * SSM_HEADS],
              d_skip=dskip[0].reshape(SSM_GROUPS, HP)[:, :HG], ssm_norm=dg_ssm[0], ffn2_norm=dg_ffn2[0],
              conv_w=dconv[:CONV_WIDTH], loss=loss[0, 0])
    return dx, dW, dP


def _prepare(w, conv_w_full):
    kvb = w["w_kv_b"]
    W = dict(wg1=w["ffn1_w_gate"], wu1=w["ffn1_w_up"], wd1=w["ffn1_w_down"], w_in=_pad_w_in(w["w_in"]),
             wq=_pad_heads(w["w_q_b"], 1, QK_HEAD, 0, QK_HEAD),
             wk=_pad_heads(kvb, 1, QK_NOPE + V_HEAD, 0, QK_NOPE),
             wv=_pad_heads(kvb, 1, QK_NOPE + V_HEAD, QK_NOPE, QK_NOPE + V_HEAD),
             pa=_pad_heads(w["w_attn_branch"], 0, V_HEAD, 0, V_HEAD), pb=w["w_ssm_branch"], wo=w["w_out"],
             wg2=w["ffn2_w_gate"], wu2=w["ffn2_w_up"], wd2=w["ffn2_w_down"])
    inv_freq = [1.0 / (ROPE_BASE ** (j / QK_ROPE)) for j in range(0, QK_ROPE, 2)]
    freq = [0.0] * QK_NOPE + inv_freq + inv_freq + [0.0] * (HP - QK_HEAD)
    P = {n: w[n] for n in ("ffn1_norm", "mix_norm", "q_a_norm", "kv_a_norm", "ssm_norm", "ffn2_norm", "conv_b")}
    P.update(q_head_norm=_lanes128([w["q_head_norm"]]), k_head_norm=_lanes128([w["k_head_norm"]]),
             conv_w8=jnp.pad(conv_w_full, ((0, 8 - CONV_WIDTH), (0, 0))),
             dt_bias8=jnp.broadcast_to(_lanes128([w["dt_bias_fwd"], w["dt_bias_bwd"]]), (8, HP)),
             a_log8=jnp.broadcast_to(_lanes128([w["a_log_fwd"], w["a_log_bwd"]]), (8, HP)),
             d_skip_lanes=jnp.repeat(w["d_skip"].reshape(-1), PH).reshape(1, D_INNER),
             freq=jnp.asarray(freq, F32).reshape(1, HP))
    return W, P


def _unprepare(dW):
    dkvb = jnp.concatenate([_unpad_heads(dW["wk"], 1, QK_NOPE), _unpad_heads(dW["wv"], 1, V_HEAD)], axis=2)
    return dict(ffn1_w_gate=dW["wg1"], ffn1_w_up=dW["wu1"], ffn1_w_down=dW["wd1"], w_in=_unpad_w_in(dW["w_in"]),
                w_q_b=_unpad_heads(dW["wq"], 1, QK_HEAD).reshape(Q_LORA, N_HEADS * QK_HEAD),
                w_kv_b=dkvb.reshape(KV_LORA, N_HEADS * (QK_NOPE + V_HEAD)),
                w_attn_branch=_unpad_heads(dW["pa"], 0, V_HEAD).reshape(N_HEADS * V_HEAD, D_MODEL),
                w_ssm_branch=dW["pb"], w_out=dW["wo"],
                ffn2_w_gate=dW["wg2"], ffn2_w_up=dW["wu2"], ffn2_w_down=dW["wd2"])


def kernel(x, positions, ffn1_norm, ffn1_w_gate, ffn1_w_up, ffn1_w_down, mix_norm, w_in, q_a_norm, w_q_b, kv_a_norm, w_kv_b, q_head_norm, k_head_norm, conv_w, conv_b, a_log_fwd, a_log_bwd, dt_bias_fwd, dt_bias_bwd, d_skip, ssm_norm, w_attn_branch, w_ssm_branch, w_out, ffn2_norm, ffn2_w_gate, ffn2_w_up, ffn2_w_down, loss_target, m_ffn1_norm, m_ffn1_w_gate, m_ffn1_w_up, m_ffn1_w_down, m_mix_norm, m_w_in, m_q_a_norm, m_w_q_b, m_kv_a_norm, m_w_kv_b, m_q_head_norm, m_k_head_norm, m_conv_w, m_conv_b, m_a_log_fwd, m_a_log_bwd, m_dt_bias_fwd, m_dt_bias_bwd, m_d_skip, m_ssm_norm, m_w_attn_branch, m_w_ssm_branch, m_w_out, m_ffn2_norm, m_ffn2_w_gate, m_ffn2_w_up, m_ffn2_w_down, v_ffn1_norm, v_ffn1_w_gate, v_ffn1_w_up, v_ffn1_w_down, v_mix_norm, v_w_in, v_q_a_norm, v_w_q_b, v_kv_a_norm, v_w_kv_b, v_q_head_norm, v_k_head_norm, v_conv_w, v_conv_b, v_a_log_fwd, v_a_log_bwd, v_dt_bias_fwd, v_dt_bias_bwd, v_d_skip, v_ssm_norm, v_w_attn_branch, v_w_ssm_branch, v_w_out, v_ffn2_norm, v_ffn2_w_gate, v_ffn2_w_up, v_ffn2_w_down):
    given = dict(locals())
    T = x.shape[1]
    packed_names = [name for name, _, _ in PACKED]

    def two_d(a):
        return a.reshape(a.shape[1], -1) if a.ndim > 2 else a

    w_loc = {n: two_d(given[n]) for n in WEIGHTS}
    wb = _pack({n: w_loc[n].astype(BF) for n in packed_names})
    wf = jnp.pad(w_loc["conv_w"], ((0, 8 - CONV_WIDTH), (0, 0)))
    gb, gf = _gather_chips(wb, wf)
    per_chip = [_unpack(gb[j]) for j in range(N_CHIPS)]
    full = {n: jnp.concatenate([per_chip[j][n] for j in range(N_CHIPS)], axis=axis) for n, _, axis in PACKED}
    conv_w_full = jnp.concatenate([gf[j, :CONV_WIDTH] for j in range(N_CHIPS)], axis=1)
    full.update({n: w_loc[n] for n in WEIGHTS if n not in full and n != "conv_w"})
    W, P = _prepare(full, conv_w_full)
    dx, dW, dP = _local_step(x.reshape(T, D_MODEL), positions.reshape(T, 1).astype(F32), loss_target.reshape(T, D_MODEL), W, P)
    g_full = _unprepare(dW)
    slots = []
    for j in range(N_CHIPS):
        shards = {}
        for n, shape, axis in PACKED:
            size = shape[axis] // N_CHIPS
            shards[n] = lax.slice_in_dim(g_full[n], j * size, (j + 1) * size, axis=axis).astype(BF)
        slots.append(_pack(shards))
    gp = jnp.stack(slots)
    core = lax.axis_index("c").astype(jnp.int32).reshape(1)
    both_cores = _add_halves(gp, _halves_to_sibling(gp), core)
    g_packed = _unpack(_join_halves(_sum_slots(_exchange_chips(both_cores))))
    small = _unpack_small(_allreduce_small(_pack_small(dP)))
    chip = 2 * lax.axis_index("x") + lax.axis_index("y")
    grads = dict(g_packed)
    grads.update({n: small[n].reshape(1, -1) for n, _ in SMALL if n not in ("conv_w", "loss")})
    grads["conv_w"] = lax.dynamic_slice_in_dim(small["conv_w"].reshape(CONV_WIDTH, XBC_DIM), chip * (XBC_DIM // N_CHIPS),
                                               XBC_DIM // N_CHIPS, axis=1)
    out_g, out_d, out_m, out_v = [], [], [], []
    for n in WEIGHTS:
        shape = given[n].shape
        delta, new_m, new_v = _adamw(w_loc[n], grads[n], two_d(given["m_" + n]), two_d(given["v_" + n]), name="adamw_" + n)
        out_g.append(grads[n].reshape(shape))
        out_d.append(delta.reshape(shape))
        out_m.append(new_m.reshape(shape))
        out_v.append(new_v.reshape(shape))
    return (small["loss"].reshape(()), dx.reshape(x.shape), *out_g, *out_d, *out_m, *out_v)
```

```python
import functools
import math

import jax
import jax.numpy as jnp
from jax import lax
from jax.experimental import pallas as pl
from jax.experimental.pallas import tpu as pltpu

BF = jnp.bfloat16
F32 = jnp.float32
HI = lax.Precision.HIGHEST
MESH = pl.DeviceIdType.MESH

D_MODEL = 1024
D_FF = 2816
EPS = 1e-6
N_HEADS = 16
QK_NOPE = 64
QK_ROPE = 32
QK_HEAD = 96
V_HEAD = 64
Q_LORA = 384
KV_LORA = 256
ROPE_BASE = 10000.0
D_INNER = 2048
SSM_HEADS = 32
SSM_GROUPS = 4
D_STATE = 128
CONV_WIDTH = 5
CHUNK = 128
XBC_DIM = 3072
HP = 128
GW = D_INNER // SSM_GROUPS
HG = SSM_HEADS // SSM_GROUPS
PH = 64
U_Z, U_GA, U_GB, U_XBC, U_SMALL = 0, 2048, 3072, 4096, 7168
S_CQ, S_CKV, S_KPE, S_DT, SMALL_W = 0, 384, 640, 768, 896
U_PAD = U_SMALL + SMALL_W
IN_SPLITS = (Q_LORA, KV_LORA, QK_ROPE, D_INNER, XBC_DIM, SSM_HEADS, SSM_HEADS, D_MODEL, D_MODEL)

ADAM_LR = 0.001
ADAM_B1 = 0.9
ADAM_B2 = 0.999
ADAM_EPS = 1e-08
ADAM_WD = 0.01
ADAM_STEP = 10

NN = (((1,), (0,)), ((), ()))
NT = (((1,), (1,)), ((), ()))
TN = (((0,), (0,)), ((), ()))


def _pick(n, pref):
    best = None
    d = 128
    while d <= min(n, pref):
        if n % d == 0:
            best = d
        d += 128
    return best if best is not None else n


def _silu(x):
    return x * jax.nn.sigmoid(x)


def _dsilu(x):
    s = jax.nn.sigmoid(x)
    return s * (1.0 + x * (1.0 - s))


def _softplus(x):
    return jnp.maximum(x, 0.0) + jnp.log(1.0 + jnp.exp(-jnp.abs(x)))


def _mm(As, Bs, *, name, ta=False, tb=False, out_dtypes=(F32,), epilogue=None, extras=(), extra_offs=None,
        tm=1024, tn=512, tk=2048, separate=False):
    As, Bs, extras = list(As), list(Bs), list(extras)
    a0, b0 = As[0], Bs[0]
    M, K = (a0.shape[1], a0.shape[0]) if ta else a0.shape
    N = b0.shape[0] if tb else b0.shape[1]
    tm, tn, tk = _pick(M, tm), _pick(N, tn), _pick(K, tk)
    nk = K // tk
    n_a, n_b, n_e, n_o = len(As), len(Bs), len(extras), len(out_dtypes)
    n_acc = (n_b if n_a == 1 or separate else 1) if nk > 1 else 0
    if extra_offs is None:
        extra_offs = (0,) * n_e
    dn = (((0,) if ta else (1,), (1,) if tb else (0,)), ((), ()))
    bytes_a = sum(a.size * a.dtype.itemsize for a in As)
    bytes_b = sum(b.size * b.dtype.itemsize for b in Bs)
    n_outer = (N // tn) * bytes_a + bytes_b < (M // tm) * bytes_b + bytes_a

    def products(a_refs, b_refs):
        if n_a == 1:
            a = a_refs[0][...].astype(BF)
            return [lax.dot_general(a, b[...].astype(BF), dn, preferred_element_type=F32) for b in b_refs]
        if separate:
            return [lax.dot_general(a[...].astype(BF), b[...].astype(BF), dn, preferred_element_type=F32)
                    for a, b in zip(a_refs, b_refs)]
        total = None
        for a, b in zip(a_refs, b_refs):
            p = lax.dot_general(a[...].astype(BF), b[...].astype(BF), dn, preferred_element_type=F32)
            total = p if total is None else total + p
        return [total]

    def finish(accs, e_refs, o_refs):
        ex = [e[...] for e in e_refs]
        outs = epilogue(*accs, *ex) if epilogue is not None else tuple(accs)
        for o_ref, val in zip(o_refs, outs):
            o_ref[...] = val.astype(o_ref.dtype)

    def body(*refs):
        a_refs, b_refs = refs[:n_a], refs[n_a:n_a + n_b]
        e_refs = refs[n_a + n_b:n_a + n_b + n_e]
        o_refs = refs[n_a + n_b + n_e:n_a + n_b + n_e + n_o]
        acc_refs = refs[n_a + n_b + n_e + n_o:]
        if nk == 1:
            finish(products(a_refs, b_refs), e_refs, o_refs)
            return
        k = pl.program_id(2)

        @pl.when(k == 0)
        def _():
            for acc in acc_refs:
                acc[...] = jnp.zeros_like(acc)

        for acc, p in zip(acc_refs, products(a_refs, b_refs)):
            acc[...] += p

        @pl.when(k == nk - 1)
        def _():
            finish([acc[...] for acc in acc_refs], e_refs, o_refs)

    def at(f):
        return (lambda j, i, k: f(i, j, k)) if n_outer else f

    a_spec = pl.BlockSpec((tk, tm), at(lambda i, j, k: (k, i))) if ta else pl.BlockSpec((tm, tk), at(lambda i, j, k: (i, k)))
    b_spec = pl.BlockSpec((tn, tk), at(lambda i, j, k: (j, k))) if tb else pl.BlockSpec((tk, tn), at(lambda i, j, k: (k, j)))
    e_specs = [pl.BlockSpec((tm, tn), at(functools.partial(lambda i, j, k, o: (i, j + o), o=off // tn))) for off in extra_offs]
    for off in extra_offs:
        assert off % tn == 0
    outs = pl.pallas_call(
        body, name=name,
        out_shape=tuple(jax.ShapeDtypeStruct((M, N), dt) for dt in out_dtypes),
        grid=(N // tn, M // tm, nk) if n_outer else (M // tm, N // tn, nk),
        in_specs=[a_spec] * n_a + [b_spec] * n_b + e_specs,
        out_specs=tuple(pl.BlockSpec((tm, tn), at(lambda i, j, k: (i, j))) for _ in out_dtypes),
        scratch_shapes=[pltpu.VMEM((tm, tn), F32)] * n_acc,
        compiler_params=pltpu.CompilerParams(dimension_semantics=("parallel", "parallel", "arbitrary")),
    )(*As, *Bs, *extras)
    return outs[0] if n_o == 1 else outs


def _rms_fwd(x, g, *, name, blk_w=None, blk_idx=0, off=0, width=None, out_dtype=BF):
    T = x.shape[0]
    blk_w = x.shape[1] if blk_w is None else blk_w
    width = blk_w if width is None else width
    tt = _pick(T, 512)

    def body(x_ref, g_ref, o_ref):
        xf = x_ref[:, off:off + width]
        r = lax.rsqrt(jnp.mean(xf * xf, axis=-1, keepdims=True) + EPS)
        o_ref[...] = (xf * r * g_ref[...]).astype(o_ref.dtype)

    return pl.pallas_call(
        body, name=name, out_shape=jax.ShapeDtypeStruct((T, width), out_dtype), grid=(T // tt,),
        in_specs=[pl.BlockSpec((tt, blk_w), lambda i: (i, blk_idx)), pl.BlockSpec((1, width), lambda i: (0, 0))],
        out_specs=pl.BlockSpec((tt, width), lambda i: (i, 0)),
    )(x, g)


def _rms_bwd(dy, x, g, *, name, blk_w=None, blk_idx=0, off=0, width=None, add=None, out_dtypes=(F32,)):
    T = x.shape[0]
    blk_w = x.shape[1] if blk_w is None else blk_w
    width = blk_w if width is None else width
    tt = _pick(T, 512)
    has_add = add is not None
    n_dx = len(out_dtypes)

    def body(*refs):
        dy_ref, x_ref, g_ref = refs[:3]
        dx_refs, dg_ref = refs[3 + has_add:3 + has_add + n_dx], refs[-1]
        xf = x_ref[:, off:off + width]
        d = dy_ref[...].astype(F32)
        r = lax.rsqrt(jnp.mean(xf * xf, axis=-1, keepdims=True) + EPS)
        gd = d * g_ref[...]
        dx = r * gd - xf * (r * r * r) * jnp.mean(gd * xf, axis=-1, keepdims=True)
        if has_add:
            dx = dx + refs[3][...]
        for dx_ref in dx_refs:
            dx_ref[...] = dx.astype(dx_ref.dtype)

        @pl.when(pl.program_id(0) == 0)
        def _():
            dg_ref[...] = jnp.zeros_like(dg_ref)

        dg_ref[...] += jnp.broadcast_to(jnp.sum(d * xf * r, axis=0, keepdims=True), dg_ref.shape)

    row = pl.BlockSpec((tt, width), lambda i: (i, 0))
    in_specs = [row, pl.BlockSpec((tt, blk_w), lambda i: (i, blk_idx)), pl.BlockSpec((1, width), lambda i: (0, 0))]
    args = [dy, x, g]
    if has_add:
        in_specs.append(row)
        args.append(add)
    return pl.pallas_call(
        body, name=name,
        out_shape=tuple(jax.ShapeDtypeStruct((T, width), dt) for dt in out_dtypes) + (jax.ShapeDtypeStruct((8, width), F32),),
        grid=(T // tt,), in_specs=in_specs,
        out_specs=(row,) * n_dx + (pl.BlockSpec((8, width), lambda i: (0, 0)),),
        compiler_params=pltpu.CompilerParams(dimension_semantics=("arbitrary",)),
    )(*args)


def _rope_tables(pos_col, freq_lane):
    T = pos_col.shape[0]
    tt = _pick(T, 512)

    def body(p_ref, f_ref, c_ref, s_ref):
        ang = p_ref[...] * f_ref[...]
        lane = lax.broadcasted_iota(jnp.int32, ang.shape, 1)
        c_ref[...] = jnp.where(lane < QK_HEAD, jnp.cos(ang), 0.0)
        sn = jnp.sin(ang)
        s_ref[...] = jnp.where((lane >= QK_NOPE) & (lane < QK_NOPE + 16), -sn,
                               jnp.where((lane >= QK_NOPE + 16) & (lane < QK_HEAD), sn, 0.0))

    return pl.pallas_call(
        body, name="rope_tables", out_shape=(jax.ShapeDtypeStruct((T, HP), F32),) * 2, grid=(T // tt,),
        in_specs=[pl.BlockSpec((tt, 1), lambda i: (i, 0)), pl.BlockSpec((1, HP), lambda i: (0, 0))],
        out_specs=(pl.BlockSpec((tt, HP), lambda i: (i, 0)),) * 2,
    )(pos_col, freq_lane)


def _swap_rope_halves(n):
    lane = lax.broadcasted_iota(jnp.int32, n.shape, 1)
    lo = (lane >= QK_NOPE) & (lane < QK_NOPE + 16)
    hi = (lane >= QK_NOPE + 16) & (lane < QK_HEAD)
    return jnp.where(lo, pltpu.roll(n, HP - 16, 1), jnp.where(hi, pltpu.roll(n, 16, 1), 0.0))


def _qk_prep_fwd(raw, kpe, gain, C, S, *, name, kpe_blk=0, out_scale=1.0):
    T = raw.shape[0]
    tt = _pick(T, 256)
    has_kpe = kpe is not None

    def body(*refs):
        if has_kpe:
            raw_ref, kpe_ref, g_ref, c_ref, s_ref, o_ref = refs
        else:
            raw_ref, g_ref, c_ref, s_ref, o_ref = refs
        for h in range(N_HEADS):
            hs = slice(HP * h, HP * (h + 1))
            xr = raw_ref[:, hs] + kpe_ref[...] if has_kpe else raw_ref[:, hs]
            r = lax.rsqrt(jnp.sum(xr * xr, axis=-1, keepdims=True) * (1.0 / QK_HEAD) + EPS)
            n = xr * r * g_ref[...]
            o_ref[:, hs] = ((n * c_ref[...] + _swap_rope_halves(n) * s_ref[...]) * out_scale).astype(o_ref.dtype)

    heads = pl.BlockSpec((tt, N_HEADS * HP), lambda i: (i, 0))
    shared = pl.BlockSpec((tt, HP), lambda i: (i, 0))
    kpe_spec = pl.BlockSpec((tt, HP), lambda i: (i, kpe_blk))
    in_specs = [heads] + ([kpe_spec] if has_kpe else []) + [pl.BlockSpec((1, HP), lambda i: (0, 0)), shared, shared]
    args = [raw] + ([kpe] if has_kpe else []) + [gain, C, S]
    return pl.pallas_call(
        body, name=name, out_shape=jax.ShapeDtypeStruct(raw.shape, BF), grid=(T // tt,),
        in_specs=in_specs, out_specs=heads,
    )(*args)


def _qk_prep_bwd(dout, raw, kpe, gain, C, S, *, name, kpe_blk=0, in_scale=1.0):
    T = raw.shape[0]
    tt = _pick(T, 256)
    has_kpe = kpe is not None

    def body(*refs):
        if has_kpe:
            d_ref, raw_ref, kpe_ref, g_ref, c_ref, s_ref, dx_ref, dg_ref, dkpe_ref = refs
        else:
            d_ref, raw_ref, g_ref, c_ref, s_ref, dx_ref, dg_ref = refs
        dg = jnp.zeros((1, HP), F32)
        dkpe = jnp.zeros((tt, HP), F32)
        for h in range(N_HEADS):
            hs = slice(HP * h, HP * (h + 1))
            xr = raw_ref[:, hs] + kpe_ref[...] if has_kpe else raw_ref[:, hs]
            d = d_ref[:, hs].astype(F32) * in_scale
            r = lax.rsqrt(jnp.sum(xr * xr, axis=-1, keepdims=True) * (1.0 / QK_HEAD) + EPS)
            dn = d * c_ref[...] + _swap_rope_halves(d * s_ref[...])
            gd = dn * g_ref[...]
            dx = r * gd - xr * (r * r * r) * (jnp.sum(gd * xr, axis=-1, keepdims=True) * (1.0 / QK_HEAD))
            dx_ref[:, hs] = dx.astype(dx_ref.dtype)
            dg = dg + jnp.sum(dn * xr * r, axis=0, keepdims=True)
            dkpe = dkpe + dx

        @pl.when(pl.program_id(0) == 0)
        def _():
            dg_ref[...] = jnp.zeros_like(dg_ref)

        dg_ref[...] += jnp.broadcast_to(dg, dg_ref.shape)
        if has_kpe:
            dkpe_ref[...] = dkpe

    heads = pl.BlockSpec((tt, N_HEADS * HP), lambda i: (i, 0))
    shared = pl.BlockSpec((tt, HP), lambda i: (i, 0))
    kpe_spec = pl.BlockSpec((tt, HP), lambda i: (i, kpe_blk))
    in_specs = [heads, heads] + ([kpe_spec] if has_kpe else []) + [pl.BlockSpec((1, HP), lambda i: (0, 0)), shared, shared]
    args = [dout, raw] + ([kpe] if has_kpe else []) + [gain, C, S]
    out_shape = [jax.ShapeDtypeStruct(raw.shape, BF), jax.ShapeDtypeStruct((8, HP), F32)]
    out_specs = [heads, pl.BlockSpec((8, HP), lambda i: (0, 0))]
    if has_kpe:
        out_shape.append(jax.ShapeDtypeStruct((T, HP), F32))
        out_specs.append(shared)
    return pl.pallas_call(
        body, name=name, out_shape=tuple(out_shape), grid=(T // tt,),
        in_specs=in_specs, out_specs=tuple(out_specs),
        compiler_params=pltpu.CompilerParams(dimension_semantics=("arbitrary",)),
    )(*args)


ATTN_SCALE = 1.0 / math.sqrt(QK_HEAD)
LOG2E = 1.0 / math.log(2.0)
Q_SCALE = ATTN_SCALE * LOG2E


def _attn_fwd(q, k, v):
    T = q.shape[0]
    tq = _pick(T, 256)

    def body(q_ref, k_ref, v_ref, o_ref, lse_ref):
        s = lax.dot_general(q_ref[...], k_ref[...], NT, preferred_element_type=F32)
        m = jnp.max(s, axis=-1, keepdims=True)
        p = jnp.exp2(s - m)
        l = jnp.sum(p, axis=-1, keepdims=True)
        o = jnp.dot(p.astype(BF), v_ref[...], preferred_element_type=F32)
        o_ref[...] = o / l
        lse_ref[...] = jnp.broadcast_to(m + jnp.log2(l), lse_ref.shape)

    qs = pl.BlockSpec((tq, HP), lambda h, i: (i, h))
    kv = pl.BlockSpec((T, HP), lambda h, i: (0, h))
    return pl.pallas_call(
        body, name="attn_fwd", out_shape=(jax.ShapeDtypeStruct(q.shape, F32),) * 2, grid=(N_HEADS, T // tq),
        in_specs=[qs, kv, kv], out_specs=(qs, qs),
        compiler_params=pltpu.CompilerParams(dimension_semantics=("parallel", "parallel")),
    )(q, k, v)


def _attn_bwd(q, k, v, do, o, lse):
    T = q.shape[0]
    tb = _pick(T, 512)
    nb = T // tb

    def body(q_ref, k_ref, v_ref, do_ref, o_ref, lse_ref, dq_ref, dk_ref, dv_ref, delta_scr, dob_scr):
        dq_ref[...] = jnp.zeros_like(dq_ref)

        def per_q_tile(i, carry):
            qs = pl.ds(pl.multiple_of(i * tb, tb), tb)
            doi = do_ref[qs, :]
            delta_scr[qs, :] = jnp.sum(doi * o_ref[qs, :], axis=-1, keepdims=True)
            dob_scr[qs, :] = doi.astype(BF)
            return carry

        lax.fori_loop(0, nb, per_q_tile, 0)

        def k_loop(j, carry):
            ks = pl.ds(pl.multiple_of(j * tb, tb), tb)
            kj, vj = k_ref[ks, :], v_ref[ks, :]

            def q_loop(i, acc):
                dk_acc, dv_acc = acc
                qs = pl.ds(pl.multiple_of(i * tb, tb), tb)
                qi = q_ref[qs, :]
                delta = delta_scr[qs, :]
                dob = dob_scr[qs, :]
                s = lax.dot_general(qi, kj, NT, preferred_element_type=F32)
                p = jnp.exp2(s - lse_ref[qs, 0:1])
                dp = lax.dot_general(dob, vj, NT, preferred_element_type=F32)
                ds = (p * (dp - delta)).astype(BF)
                dv_acc = dv_acc + lax.dot_general(p.astype(BF), dob, TN, preferred_element_type=F32)
                dk_acc = dk_acc + lax.dot_general(ds, qi, TN, preferred_element_type=F32)
                dq_ref[qs, :] += jnp.dot(ds, kj, preferred_element_type=F32)
                return dk_acc, dv_acc

            zero = jnp.zeros((tb, HP), F32)
            dk_acc, dv_acc = lax.fori_loop(0, nb, q_loop, (zero, zero))
            dk_ref[ks, :] = dk_acc
            dv_ref[ks, :] = dv_acc.astype(dv_ref.dtype)
            return carry

        lax.fori_loop(0, nb, k_loop, 0)

    spec = pl.BlockSpec((T, HP), lambda h: (0, h))
    return pl.pallas_call(
        body, name="attn_bwd",
        out_shape=(jax.ShapeDtypeStruct(q.shape, F32), jax.ShapeDtypeStruct(q.shape, F32), jax.ShapeDtypeStruct(q.shape, BF)),
        grid=(N_HEADS,), in_specs=[spec] * 6, out_specs=(spec,) * 3,
        scratch_shapes=[pltpu.VMEM((T, 1), F32), pltpu.VMEM((T, HP), BF)],
        compiler_params=pltpu.CompilerParams(dimension_semantics=("parallel",), vmem_limit_bytes=2 * 15 * T * HP * 2 + (8 << 20)),
    )(q, k, v, do, o, lse)


CONV_TC = 512
CONV_PAD = CONV_WIDTH // 2


def _halo_specs(tr, col_of):
    r8 = tr // 8
    cur = pl.BlockSpec((tr, CONV_TC), lambda j, i: (i, col_of(j)))
    prev = pl.BlockSpec((8, CONV_TC), lambda j, i: (jnp.maximum(i * r8 - 1, 0), col_of(j)))

    def nxt_map(j, i, n8):
        return (jnp.minimum((i + 1) * r8, n8 - 1), col_of(j))

    return cur, prev, nxt_map


def _with_halo(prev_ref, cur_ref, next_ref, i, n_i):
    prev = jnp.where(i == 0, 0.0, prev_ref[...].astype(F32))
    nxt = jnp.where(i == n_i - 1, 0.0, next_ref[...].astype(F32))
    return jnp.concatenate([prev, cur_ref[...].astype(F32), nxt], axis=0)


def _conv_fwd(u, w8, b):
    T = u.shape[0]
    tr = _pick(T, 512)
    n_i = T // tr
    c0 = U_XBC // CONV_TC
    cur, prev, nxt_map = _halo_specs(tr, lambda j: c0 + j)
    nxt = pl.BlockSpec((8, CONV_TC), functools.partial(nxt_map, n8=T // 8))

    def body(p_ref, c_ref, n_ref, w_ref, b_ref, pre_ref, act_ref):
        i = pl.program_id(1)
        full = _with_halo(p_ref, c_ref, n_ref, i, n_i)
        acc = jnp.broadcast_to(b_ref[...], (tr, CONV_TC))
        for kk in range(CONV_WIDTH):
            acc = acc + full[8 - CONV_PAD + kk:8 - CONV_PAD + kk + tr, :] * w_ref[kk:kk + 1, :]
        pre_ref[...] = acc
        act_ref[...] = _silu(acc)

    out = pl.BlockSpec((tr, CONV_TC), lambda j, i: (i, j))
    return pl.pallas_call(
        body, name="conv_fwd", out_shape=(jax.ShapeDtypeStruct((T, XBC_DIM), F32),) * 2,
        grid=(XBC_DIM // CONV_TC, n_i),
        in_specs=[prev, cur, nxt, pl.BlockSpec((8, CONV_TC), lambda j, i: (0, j)), pl.BlockSpec((1, CONV_TC), lambda j, i: (0, j))],
        out_specs=(out, out),
    )(u, u, u, w8, b)


def _conv_dpre(dacts, pre, col0, *, name):
    T, width = dacts[0].shape
    tt = _pick(T, 512)
    n_d = len(dacts)
    c0 = col0 // CONV_TC

    def body(*refs):
        d = refs[0][...]
        for r in refs[1:n_d]:
            d = d + r[...]
        refs[n_d + 1][...] = d * _dsilu(refs[n_d][...])

    blk = pl.BlockSpec((tt, CONV_TC), lambda j, i: (i, j))
    return pl.pallas_call(
        body, name=name, out_shape=jax.ShapeDtypeStruct((T, width), F32), grid=(width // CONV_TC, T // tt),
        in_specs=[blk] * n_d + [pl.BlockSpec((tt, CONV_TC), lambda j, i: (i, c0 + j))], out_specs=blk,
    )(*dacts, pre)


def _conv_bwd(dpre, u, w8, col0, *, name):
    T, width = dpre.shape
    tr = _pick(T, 512)
    n_i = T // tr
    cd = col0 // CONV_TC
    cx = (U_XBC + col0) // CONV_TC
    d_cur, d_prev, d_nxt_map = _halo_specs(tr, lambda j: j)
    x_cur, x_prev, x_nxt_map = _halo_specs(tr, lambda j: cx + j)
    d_nxt = pl.BlockSpec((8, CONV_TC), functools.partial(d_nxt_map, n8=T // 8))
    x_nxt = pl.BlockSpec((8, CONV_TC), functools.partial(x_nxt_map, n8=T // 8))

    def body(dp_ref, dc_ref, dn_ref, xp_ref, xc_ref, xn_ref, w_ref, dx_ref, dw_ref):
        i = pl.program_id(1)
        dfull = _with_halo(dp_ref, dc_ref, dn_ref, i, n_i)
        xfull = _with_halo(xp_ref, xc_ref, xn_ref, i, n_i)
        dcur = dc_ref[...]
        dx = jnp.zeros((tr, CONV_TC), F32)
        rows = []
        for kk in range(CONV_WIDTH):
            dx = dx + dfull[8 + CONV_PAD - kk:8 + CONV_PAD - kk + tr, :] * w_ref[kk:kk + 1, :]
            rows.append(jnp.sum(dcur * xfull[8 - CONV_PAD + kk:8 - CONV_PAD + kk + tr, :], axis=0, keepdims=True))
        rows.append(jnp.sum(dcur, axis=0, keepdims=True))
        rows.append(jnp.zeros((2, CONV_TC), F32))
        dx_ref[...] = dx.astype(dx_ref.dtype)

        @pl.when(i == 0)
        def _():
            dw_ref[...] = jnp.zeros_like(dw_ref)

        dw_ref[...] += jnp.concatenate(rows, axis=0)

    out = pl.BlockSpec((tr, CONV_TC), lambda j, i: (i, j))
    return pl.pallas_call(
        body, name=name, out_shape=(jax.ShapeDtypeStruct((T, width), BF), jax.ShapeDtypeStruct((8, width), F32)),
        grid=(width // CONV_TC, n_i),
        in_specs=[d_prev, d_cur, d_nxt, x_prev, x_cur, x_nxt, pl.BlockSpec((8, CONV_TC), lambda j, i: (0, cd + j))],
        out_specs=(out, pl.BlockSpec((8, CONV_TC), lambda j, i: (0, j))),
        compiler_params=pltpu.CompilerParams(dimension_semantics=("parallel", "arbitrary")),
    )(dpre, dpre, dpre, u, u, u, w8)


N_HB = 2 * SSM_GROUPS
P_DT, P_CS, P_E, P_W = 0, HP, 2 * HP, 3 * HP
DT_BLK = (U_SMALL + S_DT) // HP


def _tri(rev, transpose=False):
    rows = lax.broadcasted_iota(jnp.int32, (CHUNK, CHUNK), 0)
    cols = lax.broadcasted_iota(jnp.int32, (CHUNK, CHUNK), 1)
    if transpose:
        rows, cols = cols, rows
    return (cols >= rows) if rev else (cols <= rows)


def _ssd_prep(u, bias8, alog8):
    T = u.shape[0]
    nc = T // CHUNK

    def body(dt_ref, bias_ref, a_ref, cols_ref, rows_ref):
        lane = lax.broadcasted_iota(jnp.int32, (CHUNK, HP), 1)
        dt = _softplus(dt_ref[...] + bias_ref[0:1, :])
        da = dt * (-jnp.exp(a_ref[0:1, :]))
        cs_f = jnp.dot(jnp.where(_tri(False), 1.0, 0.0).astype(F32), da, precision=HI, preferred_element_type=F32)
        cs_b = jnp.dot(jnp.where(_tri(True), 1.0, 0.0).astype(F32), da, precision=HI, preferred_element_type=F32)
        cs = jnp.where(lane < SSM_HEADS, cs_f, cs_b)
        tot = jnp.where(lane[0:1] < SSM_HEADS, cs_f[CHUNK - 1:CHUNK, :], cs_b[0:1, :])
        e, w = jnp.exp(cs), jnp.exp(tot - cs)
        tot8 = jnp.broadcast_to(tot, (8, HP))
        etot8 = jnp.exp(tot8)
        for b in range(N_HB):
            down = (HP - HG * b) % HP

            def rolled(v):
                return pltpu.roll(v, down, 1) if down else v

            cols_ref[b, :, P_DT:P_DT + HP] = rolled(dt)
            cs_r = rolled(cs)
            cols_ref[b, :, P_CS:P_CS + HP] = cs_r
            cols_ref[b, :, P_E:P_E + HP] = rolled(e)
            cols_ref[b, :, P_W:P_W + HP] = rolled(w)
            rows_ref[b, 0, 0:8, :] = cs_r.T[0:8, :]
            r8 = lax.broadcasted_iota(jnp.int32, (8, HP), 0)
            rows_ref[b, 0, 8:16, :] = jnp.where(r8 == 0, rolled(tot8), jnp.where(r8 == 1, rolled(etot8), 0.0))

    vec = pl.BlockSpec((8, HP), lambda c: (0, 0))
    return pl.pallas_call(
        body, name="ssd_prep",
        out_shape=(jax.ShapeDtypeStruct((N_HB, T, 4 * HP), F32), jax.ShapeDtypeStruct((N_HB, nc, 16, HP), F32)),
        grid=(nc,), in_specs=[pl.BlockSpec((CHUNK, HP), lambda c: (c, DT_BLK)), vec, vec],
        out_specs=(pl.BlockSpec((N_HB, CHUNK, 4 * HP), lambda c: (0, c, 0)), pl.BlockSpec((N_HB, 1, 16, HP), lambda c: (0, c, 0, 0))),
    )(u, bias8, alog8)


def _ssd_specs(T, rev, bwd):
    nc = T // CHUNK
    fwd_order = (lambda c: nc - 1 - c) if rev else (lambda c: c)
    cm = (lambda c: fwd_order(nc - 1 - c)) if bwd else fwd_order
    hb0 = SSM_GROUPS if rev else 0
    xs = pl.BlockSpec((CHUNK, GW), lambda c, g: (cm(c), g))
    bs = pl.BlockSpec((CHUNK, D_STATE), lambda c, g: (cm(c), D_INNER // D_STATE + g))
    cs = pl.BlockSpec((CHUNK, D_STATE), lambda c, g: (cm(c), (D_INNER + SSM_GROUPS * D_STATE) // D_STATE + g))
    cols = pl.BlockSpec((1, CHUNK, 4 * HP), lambda c, g: (hb0 + g, cm(c), 0))
    rows = pl.BlockSpec((1, 1, 16, HP), lambda c, g: (hb0 + g, cm(c), 0, 0))
    return nc, cm, xs, bs, cs, cols, rows


def _head_terms(cols_ref, rows_ref, hh, incl):
    dt = cols_ref[0, :, P_DT + hh:P_DT + hh + 1]
    col = cols_ref[0, :, P_CS + hh:P_CS + hh + 1]
    e = cols_ref[0, :, P_E + hh:P_E + hh + 1]
    w = cols_ref[0, :, P_W + hh:P_W + hh + 1]
    row = rows_ref[0, 0, hh:hh + 1, :]
    etot = rows_ref[0, 0, 9:10, hh:hh + 1]
    lmat = jnp.where(incl, jnp.exp(col - row), 0.0)
    return dt, col, row, e, w, etot, lmat


def _ssd_fwd(act, cols, rows, *, rev, name):
    T = act.shape[0]
    nc, cm, xs_s, b_s, c_s, cols_s, rows_s = _ssd_specs(T, rev, False)

    def body(x_ref, b_ref, c_ref, cols_ref, rows_ref, y_ref, st_ref, state, xdw):
        c, g = pl.program_id(0), pl.program_id(1)

        @pl.when(c == 0)
        def _():
            state[g] = jnp.zeros((D_STATE, GW), F32)

        incl = _tri(rev)
        bm, cmat = b_ref[...].astype(BF), c_ref[...].astype(BF)
        bm_t = b_ref[...].T.astype(BF)
        cb = lax.dot_general(cmat, bm, NT, preferred_element_type=F32)
        prev_all = state[g]
        st_ref[...] = prev_all
        yo_all = jnp.dot(cmat, prev_all.astype(BF), preferred_element_type=F32)
        for hh in range(HG):
            hs = slice(PH * hh, PH * (hh + 1))
            dt, col, row, e, w, etot, lmat = _head_terms(cols_ref, rows_ref, hh, incl)
            xdt = x_ref[:, hs] * dt
            xdw[:, hs] = (xdt * w).astype(BF)
            yd = jnp.dot((cb * lmat).astype(BF), xdt.astype(BF), preferred_element_type=F32)
            y_ref[:, hs] = yd + yo_all[:, hs] * e
            state[g, :, hs] = prev_all[:, hs] * etot
        state[g] += jnp.dot(bm_t, xdw[...], preferred_element_type=F32)

    return pl.pallas_call(
        body, name=name,
        out_shape=(jax.ShapeDtypeStruct((T, D_INNER), F32), jax.ShapeDtypeStruct((nc * D_STATE, D_INNER), F32)),
        grid=(nc, SSM_GROUPS), in_specs=[xs_s, b_s, c_s, cols_s, rows_s], out_specs=(xs_s, xs_s),
        scratch_shapes=[pltpu.VMEM((SSM_GROUPS, D_STATE, GW), F32), pltpu.VMEM((CHUNK, GW), BF)],
        compiler_params=pltpu.CompilerParams(dimension_semantics=("arbitrary", "arbitrary")),
    )(act, act, act, cols, rows)


def _ssd_bwd(act, cols, rows, states, dy, *, rev, name):
    T = act.shape[0]
    nc, cm, xs_s, b_s, c_s, cols_s, rows_s = _ssd_specs(T, rev, True)

    def body(x_ref, b_ref, c_ref, cols_ref, rows_ref, st_ref, dy_ref, dx_ref, db_ref, dc_ref, dsel_ref, dtot_ref,
             dstate, dye, xdw, dcs_rows, dcb):
        c, g = pl.program_id(0), pl.program_id(1)

        @pl.when(c == 0)
        def _():
            dstate[g] = jnp.zeros((D_STATE, GW), F32)

        incl, incl_t = _tri(rev), _tri(rev, transpose=True)
        bm, cmat = b_ref[...].astype(BF), c_ref[...].astype(BF)
        cm_t = c_ref[...].T.astype(BF)
        cb = lax.dot_general(cmat, bm, NT, preferred_element_type=F32)
        cb_t = lax.dot_general(bm, cmat, NT, preferred_element_type=F32)
        prev_all, ds_all = st_ref[...], dstate[g]
        pb_all, dsb_all = prev_all.astype(BF), ds_all.astype(BF)
        cp_all = jnp.dot(cmat, pb_all, preferred_element_type=F32)
        bds_all = jnp.dot(bm, dsb_all, preferred_element_type=F32)
        dsel_ref[...] = jnp.zeros_like(dsel_ref)
        dtot_ref[...] = jnp.zeros_like(dtot_ref)
        dcs_rows[...] = jnp.zeros_like(dcs_rows)
        dcb[...] = jnp.zeros_like(dcb)
        for hh in range(HG):
            hs = slice(PH * hh, PH * (hh + 1))
            dth, col, row, e, w, etot, lmat = _head_terms(cols_ref, rows_ref, hh, incl)
            x = x_ref[:, hs]
            xdt = x * dth
            mmat = cb * lmat
            mmat_t = cb_t * jnp.where(incl_t, jnp.exp(row - col), 0.0)
            prev, ds_ = prev_all[:, hs], ds_all[:, hs]
            dyh = dy_ref[:, hs]
            dyb = dyh.astype(BF)
            dye[:, hs] = (dyh * e).astype(BF)
            xdw[:, hs] = (xdt * w).astype(BF)
            dcs_h = jnp.sum(dyh * cp_all[:, hs], axis=1, keepdims=True) * e
            dm = lax.dot_general(dyb, xdt.astype(BF), NT, preferred_element_type=F32)
            dxdt = jnp.dot(mmat_t.astype(BF), dyb, preferred_element_type=F32)
            qm = dm * mmat
            dcs_h = dcs_h + jnp.sum(qm, axis=1, keepdims=True)
            dcs_rows[hh:hh + 1, :] = jnp.sum(qm, axis=0, keepdims=True)
            dcb[...] += dm * lmat
            bds = bds_all[:, hs] * w
            dxdt = dxdt + bds
            t = jnp.sum(xdt * bds, axis=1, keepdims=True)
            dtot_ref[0, 0, 0:1, hh:hh + 1] = jnp.sum(t, axis=0, keepdims=True) + jnp.sum(ds_ * prev) * etot
            dsel_ref[0, :, HP + hh:HP + hh + 1] = dcs_h - t
            dsel_ref[0, :, hh:hh + 1] = jnp.sum(dxdt * x, axis=1, keepdims=True)
            dx_ref[:, hs] = dxdt * dth
            dstate[g, :, hs] = ds_ * etot
        dye_all, xdw_all, dcb_all = dye[...], xdw[...], dcb[...]
        dstate[g] += jnp.dot(cm_t, dye_all, preferred_element_type=F32)
        dc_ref[...] = (lax.dot_general(dye_all, pb_all, NT, preferred_element_type=F32)
                       + jnp.dot(dcb_all.astype(BF), bm, preferred_element_type=F32))
        db_ref[...] = (lax.dot_general(xdw_all, dsb_all, NT, preferred_element_type=F32)
                       + jnp.dot(dcb_all.T.astype(BF), cmat, preferred_element_type=F32))
        dsel_ref[0, :, HP:2 * HP] -= dcs_rows[...].T

    bc_out = pl.BlockSpec((CHUNK, D_STATE), lambda c, g: (cm(c), g))
    return pl.pallas_call(
        body, name=name,
        out_shape=(jax.ShapeDtypeStruct((T, D_INNER), F32), jax.ShapeDtypeStruct((T, SSM_GROUPS * D_STATE), F32),
                   jax.ShapeDtypeStruct((T, SSM_GROUPS * D_STATE), F32), jax.ShapeDtypeStruct((SSM_GROUPS, T, 2 * HP), F32),
                   jax.ShapeDtypeStruct((SSM_GROUPS, nc, 8, HP), F32)),
        grid=(nc, SSM_GROUPS), in_specs=[xs_s, b_s, c_s, cols_s, rows_s, xs_s, xs_s],
        out_specs=(xs_s, bc_out, bc_out, pl.BlockSpec((1, CHUNK, 2 * HP), lambda c, g: (g, cm(c), 0)),
                   pl.BlockSpec((1, 1, 8, HP), lambda c, g: (g, cm(c), 0, 0))),
        scratch_shapes=[pltpu.VMEM((SSM_GROUPS, D_STATE, GW), F32), pltpu.VMEM((CHUNK, GW), BF), pltpu.VMEM((CHUNK, GW), BF),
                        pltpu.VMEM((CHUNK, CHUNK), F32), pltpu.VMEM((CHUNK, CHUNK), F32)],
        compiler_params=pltpu.CompilerParams(dimension_semantics=("arbitrary", "arbitrary")),
    )(act, act, act, cols, rows, states, dy)


def _ssd_prep_bwd(u, bias8, alog8, dsel_f, dtot_f, dsel_b, dtot_b):
    T = u.shape[0]
    nc = T // CHUNK

    def body(dt_ref, bias_ref, a_ref, sf_ref, tf_ref, sb_ref, tb_ref, ddt_ref, da_ref, dbias_ref):
        @pl.when(pl.program_id(0) == 0)
        def _():
            da_ref[...] = jnp.zeros_like(da_ref)
            dbias_ref[...] = jnp.zeros_like(dbias_ref)

        lane = lax.broadcasted_iota(jnp.int32, (CHUNK, HP), 1)
        pre = dt_ref[...] + bias_ref[0:1, :]
        dt = _softplus(pre)
        a = -jnp.exp(a_ref[0:1, :])
        ddt_x, dcs, dtot = jnp.zeros((CHUNK, HP), F32), jnp.zeros((CHUNK, HP), F32), jnp.zeros((8, HP), F32)
        for b in range(N_HB):
            s_ref, t_ref, g = (sf_ref, tf_ref, b) if b < SSM_GROUPS else (sb_ref, tb_ref, b - SSM_GROUPS)
            mine = (lane >= HG * b) & (lane < HG * (b + 1))

            def up(v):
                return pltpu.roll(v, HG * b, 1) if b else v

            ddt_x = ddt_x + jnp.where(mine, up(s_ref[g, :, 0:HP]), 0.0)
            dcs = dcs + jnp.where(mine, up(s_ref[g, :, HP:2 * HP]), 0.0)
            dtot = dtot + jnp.where(mine[0:8], up(t_ref[g, 0]), 0.0)
        tri_f = jnp.where(_tri(False, transpose=True), 1.0, 0.0).astype(F32)
        tri_b = jnp.where(_tri(True, transpose=True), 1.0, 0.0).astype(F32)
        dda = jnp.where(lane < SSM_HEADS, jnp.dot(tri_f, dcs, precision=HI, preferred_element_type=F32),
                        jnp.dot(tri_b, dcs, precision=HI, preferred_element_type=F32)) + dtot[0:1, :]
        dpre = (ddt_x + dda * a) * jax.nn.sigmoid(pre)
        ddt_ref[...] = jnp.where(lane < 2 * SSM_HEADS, dpre, 0.0)
        dbias_ref[...] += jnp.broadcast_to(jnp.sum(dpre, axis=0, keepdims=True), (8, HP))
        da_ref[...] += jnp.broadcast_to(jnp.sum(dda * dt, axis=0, keepdims=True) * a, (8, HP))

    vec = pl.BlockSpec((8, HP), lambda c: (0, 0))
    sel = pl.BlockSpec((SSM_GROUPS, CHUNK, 2 * HP), lambda c: (0, c, 0))
    tot = pl.BlockSpec((SSM_GROUPS, 1, 8, HP), lambda c: (0, c, 0, 0))
    tile = pl.BlockSpec((CHUNK, HP), lambda c: (c, 0))
    return pl.pallas_call(
        body, name="ssd_prep_bwd",
        out_shape=(jax.ShapeDtypeStruct((T, HP), F32), jax.ShapeDtypeStruct((8, HP), F32), jax.ShapeDtypeStruct((8, HP), F32)),
        grid=(nc,), in_specs=[pl.BlockSpec((CHUNK, HP), lambda c: (c, DT_BLK)), vec, vec, sel, tot, sel, tot],
        out_specs=(tile, vec, vec),
        compiler_params=pltpu.CompilerParams(dimension_semantics=("arbitrary",)),
    )(u, bias8, alog8, dsel_f, dtot_f, dsel_b, dtot_b)


def _ssm_combine_fwd(y_f, y_b, act, u, dskip, gain):
    T = y_f.shape[0]
    tt = _pick(T, 256)

    def body(yf_ref, yb_ref, x_ref, z_ref, ds_ref, g_ref, y_ref, m_ref):
        y = yf_ref[...] + yb_ref[...] + ds_ref[...] * x_ref[...]
        y2 = y * _silu(z_ref[...])
        r = lax.rsqrt(jnp.mean(y2 * y2, axis=-1, keepdims=True) + EPS)
        y_ref[...] = y
        m_ref[...] = (y2 * r * g_ref[...]).astype(m_ref.dtype)

    blk = pl.BlockSpec((tt, GW), lambda i, g: (i, g))
    vec = pl.BlockSpec((1, GW), lambda i, g: (0, g))
    return pl.pallas_call(
        body, name="ssm_combine_fwd",
        out_shape=(jax.ShapeDtypeStruct((T, D_INNER), F32), jax.ShapeDtypeStruct((T, D_INNER), BF)),
        grid=(T // tt, SSM_GROUPS), in_specs=[blk, blk, blk, blk, vec, vec], out_specs=(blk, blk),
    )(y_f, y_b, act, u, dskip, gain)


def _ssm_combine_bwd(dm, y, act, u, dskip, gain):
    T = y.shape[0]
    tt = _pick(T, 256)

    def body(dm_ref, y_ref, x_ref, z_ref, ds_ref, g_ref, dy_ref, dz_ref, dxs_ref, dg_ref, dsk_ref):
        z = z_ref[...]
        y = y_ref[...]
        x = x_ref[...]
        sz = _silu(z)
        y2 = y * sz
        r = lax.rsqrt(jnp.mean(y2 * y2, axis=-1, keepdims=True) + EPS)
        d = dm_ref[...]
        gd = d * g_ref[...]
        dy2 = r * gd - y2 * (r * r * r) * jnp.mean(gd * y2, axis=-1, keepdims=True)
        dy = dy2 * sz
        dy_ref[...] = dy
        dz_ref[...] = (dy2 * y * _dsilu(z)).astype(dz_ref.dtype)
        dxs_ref[...] = dy * ds_ref[...]

        @pl.when(pl.program_id(1) == 0)
        def _():
            dg_ref[...] = jnp.zeros_like(dg_ref)
            dsk_ref[...] = jnp.zeros_like(dsk_ref)

        dg_ref[...] += jnp.broadcast_to(jnp.sum(d * y2 * r, axis=0, keepdims=True), dg_ref.shape)
        lane_sum = jnp.broadcast_to(jnp.sum(dy * x, axis=0, keepdims=True), (8, GW))
        src = lax.broadcasted_iota(jnp.int32, (GW, HP), 0)
        head = lax.broadcasted_iota(jnp.int32, (GW, HP), 1)
        to_head = jnp.where((src >= PH * head) & (src < PH * (head + 1)), 1.0, 0.0).astype(F32)
        dsk_ref[...] += jnp.dot(lane_sum, to_head, precision=HI, preferred_element_type=F32)

    blk = pl.BlockSpec((tt, GW), lambda g, i: (i, g))
    vec = pl.BlockSpec((1, GW), lambda g, i: (0, g))
    acc = pl.BlockSpec((8, GW), lambda g, i: (0, g))
    return pl.pallas_call(
        body, name="ssm_combine_bwd",
        out_shape=(jax.ShapeDtypeStruct((T, D_INNER), F32), jax.ShapeDtypeStruct((T, D_INNER), BF),
                   jax.ShapeDtypeStruct((T, D_INNER), F32), jax.ShapeDtypeStruct((8, D_INNER), F32),
                   jax.ShapeDtypeStruct((8, SSM_GROUPS * HP), F32)),
        grid=(SSM_GROUPS, T // tt), in_specs=[blk, blk, blk, blk, vec, vec],
        out_specs=(blk, blk, blk, acc, pl.BlockSpec((8, HP), lambda g, i: (0, g))),
        compiler_params=pltpu.CompilerParams(dimension_semantics=("parallel", "arbitrary")),
    )(dm, y, act, u, dskip, gain)


def _loss_head(y, target):
    T, D = y.shape
    tt = _pick(T, 512)

    def body(y_ref, t_ref, dy_ref, dyb_ref, l_ref):
        e = y_ref[...] - t_ref[...]
        dy_ref[...] = e * (1.0 / D)
        dyb_ref[...] = (e * (1.0 / D)).astype(dyb_ref.dtype)

        @pl.when(pl.program_id(0) == 0)
        def _():
            l_ref[...] = jnp.zeros_like(l_ref)

        l_ref[...] += jnp.sum(e * e) * (0.5 / D)

    blk = pl.BlockSpec((tt, D), lambda i: (i, 0))
    return pl.pallas_call(
        body, name="loss_head",
        out_shape=(jax.ShapeDtypeStruct((T, D), F32), jax.ShapeDtypeStruct((T, D), BF), jax.ShapeDtypeStruct((8, 128), F32)),
        grid=(T // tt,), in_specs=[blk, blk], out_specs=(blk, blk, pl.BlockSpec((8, 128), lambda i: (0, 0))),
        compiler_params=pltpu.CompilerParams(dimension_semantics=("arbitrary",)),
    )(y, target)


def _adamw(w, g, m, v, *, name):
    R, C = w.shape
    cap = max(8, (1 << 18) // C)
    tr = R
    if R % 8 == 0:
        tr = 8
        for cand in range(8, min(R, cap) + 1, 8):
            if R % cand == 0:
                tr = cand

    def body(w_ref, g_ref, m_ref, v_ref, d_ref, nm_ref, nv_ref):
        gg = g_ref[...]
        nm = ADAM_B1 * m_ref[...] + (1.0 - ADAM_B1) * gg
        nv = ADAM_B2 * v_ref[...] + (1.0 - ADAM_B2) * jnp.square(gg)
        m_hat = nm / (1.0 - ADAM_B1 ** ADAM_STEP)
        v_hat = nv / (1.0 - ADAM_B2 ** ADAM_STEP)
        d_ref[...] = -ADAM_LR * (m_hat / (jnp.sqrt(v_hat) + ADAM_EPS) + ADAM_WD * w_ref[...])
        nm_ref[...] = nm
        nv_ref[...] = nv

    blk = pl.BlockSpec((tr, C), lambda i: (i, 0))
    return pl.pallas_call(
        body, name=name, out_shape=(jax.ShapeDtypeStruct((R, C), F32),) * 3, grid=(R // tr,),
        in_specs=[blk] * 4, out_specs=(blk,) * 3,
    )(w, g, m, v)


ANY = pl.BlockSpec(memory_space=pl.ANY)


def _chip_peers():
    x, y, c = lax.axis_index("x"), lax.axis_index("y"), lax.axis_index("c")
    return x, y, c, [(1 - x, y), (x, 1 - y), (1 - x, 1 - y)]


def _half_rows(c, rh):
    return pl.ds(pl.multiple_of(c * rh, 16), rh)


def _my_chip():
    return 2 * lax.axis_index("x") + lax.axis_index("y")


def _gather_chips(wb, wf):
    rh = wb.shape[0] // 2

    def body(wb_ref, wf_ref, ob_ref, of_ref, send_sems, recv_sems):
        x, y, c, peers = _chip_peers()
        me = 2 * x + y
        half, other = _half_rows(c, rh), _half_rows(1 - c, rh)

        def chip_copy(k, slot):
            px, py = peers[k]
            return pltpu.make_async_remote_copy(
                src_ref=wb_ref.at[half], dst_ref=ob_ref.at[slot, half], send_sem=send_sems.at[k], recv_sem=recv_sems.at[k],
                device_id=(px, py, c), device_id_type=MESH)

        def passed_on(k, slot, rows):
            return pltpu.make_async_remote_copy(
                src_ref=ob_ref.at[slot, rows], dst_ref=ob_ref.at[slot, rows], send_sem=send_sems.at[3 + k],
                recv_sem=recv_sems.at[3 + k], device_id=(x, y, 1 - c), device_id_type=MESH)

        def small_copy(k, slot):
            px, py = peers[k]
            return pltpu.make_async_remote_copy(
                src_ref=wf_ref, dst_ref=of_ref.at[slot], send_sem=send_sems.at[6 + k], recv_sem=recv_sems.at[6 + k],
                device_id=(px, py, c), device_id_type=MESH)

        sends = [chip_copy(k, me) for k in range(3)] + [small_copy(k, me) for k in range(3)]
        for cp in sends:
            cp.start()
        chip_of = [2 * px + py for px, py in peers]
        for k in range(3):
            chip_copy(k, chip_of[k]).wait_recv()
            cp = passed_on(k, chip_of[k], half)
            cp.start()
            sends.append(cp)
        for k in range(3):
            passed_on(k, chip_of[k], other).wait_recv()
            small_copy(k, chip_of[k]).wait_recv()
        for cp in sends:
            cp.wait_send()

    ob, of = pl.pallas_call(
        body, name="gather_weights",
        out_shape=(jax.ShapeDtypeStruct((4,) + wb.shape, wb.dtype), jax.ShapeDtypeStruct((4,) + wf.shape, wf.dtype)),
        in_specs=[ANY, ANY], out_specs=(ANY, ANY),
        scratch_shapes=[pltpu.SemaphoreType.DMA((9,)), pltpu.SemaphoreType.DMA((9,))],
    )(wb, wf)
    me = _my_chip()
    return lax.dynamic_update_slice(ob, wb[None], (me, 0, 0)), lax.dynamic_update_slice(of, wf[None], (me, 0, 0))


def _halves_to_sibling(gp):
    rh = gp.shape[1] // 2

    def body(gp_ref, o_ref, send_sem, recv_sem):
        x, y, c = lax.axis_index("x"), lax.axis_index("y"), lax.axis_index("c")
        cp = pltpu.make_async_remote_copy(src_ref=gp_ref.at[:, _half_rows(1 - c, rh), :], dst_ref=o_ref, send_sem=send_sem,
                                          recv_sem=recv_sem, device_id=(x, y, 1 - c), device_id_type=MESH)
        cp.start()
        cp.wait()

    return pl.pallas_call(
        body, name="halves_to_sibling", out_shape=jax.ShapeDtypeStruct((gp.shape[0], rh, gp.shape[2]), gp.dtype),
        in_specs=[ANY], out_specs=ANY, scratch_shapes=[pltpu.SemaphoreType.DMA, pltpu.SemaphoreType.DMA],
    )(gp)


def _row_tile(rows, cap=1024):
    tr = 16
    for cand in range(16, cap + 1, 16):
        if rows % cand == 0:
            tr = cand
    return tr


def _add_halves(gp, sib, core):
    n, rh, C = sib.shape
    tr = _row_tile(rh)
    nt = rh // tr

    def body(c_ref, g_ref, s_ref, o_ref):
        o_ref[...] = (g_ref[...].astype(F32) + s_ref[...].astype(F32)).astype(o_ref.dtype)

    blk = pl.BlockSpec((1, tr, C), lambda j, i, c: (j, i, 0))
    return pl.pallas_call(
        body, name="add_halves", out_shape=jax.ShapeDtypeStruct(sib.shape, sib.dtype),
        grid_spec=pltpu.PrefetchScalarGridSpec(
            num_scalar_prefetch=1, grid=(n, nt),
            in_specs=[pl.BlockSpec((1, tr, C), lambda j, i, c: (j, c[0] * nt + i, 0)), blk], out_specs=blk),
    )(core, gp, sib)


def _join_halves(mine):
    rh = mine.shape[0]

    def body(m_ref, o_ref, send_sem, recv_sem):
        x, y, c = lax.axis_index("x"), lax.axis_index("y"), lax.axis_index("c")
        half, other = _half_rows(c, rh), _half_rows(1 - c, rh)

        def copy(rows):
            return pltpu.make_async_remote_copy(src_ref=m_ref, dst_ref=o_ref.at[rows], send_sem=send_sem, recv_sem=recv_sem,
                                                device_id=(x, y, 1 - c), device_id_type=MESH)

        send = copy(half)
        send.start()
        copy(other).wait_recv()
        send.wait_send()

    out = pl.pallas_call(
        body, name="join_halves", out_shape=jax.ShapeDtypeStruct((2 * rh, mine.shape[1]), mine.dtype),
        in_specs=[ANY], out_specs=ANY, scratch_shapes=[pltpu.SemaphoreType.DMA, pltpu.SemaphoreType.DMA],
    )(mine)
    return lax.dynamic_update_slice(out, mine, (lax.axis_index("c") * rh, 0))


def _exchange_chips(gp):
    def body(gp_ref, out_ref, send_sems, recv_sems):
        x, y, c, peers = _chip_peers()
        me = 2 * x + y

        def copies(sending):
            out = []
            for k, (px, py) in enumerate(peers):
                p = 2 * px + py
                out.append(pltpu.make_async_remote_copy(
                    src_ref=gp_ref.at[p], dst_ref=out_ref.at[me if sending else p],
                    send_sem=send_sems.at[k], recv_sem=recv_sems.at[k], device_id=(px, py, c), device_id_type=MESH))
            return out

        sends = copies(True)
        for cp in sends:
            cp.start()
        for cp in copies(False):
            cp.wait_recv()
        for cp in sends:
            cp.wait_send()

    out = pl.pallas_call(
        body, name="exchange_grads", out_shape=jax.ShapeDtypeStruct(gp.shape, gp.dtype),
        in_specs=[ANY], out_specs=ANY,
        scratch_shapes=[pltpu.SemaphoreType.DMA((3,)), pltpu.SemaphoreType.DMA((3,))],
    )(gp)
    me = _my_chip()
    return lax.dynamic_update_slice(out, lax.dynamic_slice_in_dim(gp, me, 1, axis=0), (me, 0, 0))


def _sum_slots(r4):
    _, R, C = r4.shape
    tr = _row_tile(R)

    def body(r_ref, o_ref):
        acc = r_ref[0].astype(F32)
        for s in range(1, 4):
            acc = acc + r_ref[s].astype(F32)
        o_ref[...] = acc

    return pl.pallas_call(
        body, name="sum_slots", out_shape=jax.ShapeDtypeStruct((R, C), F32), grid=(R // tr,),
        in_specs=[pl.BlockSpec((4, tr, C), lambda i: (0, i, 0))], out_specs=pl.BlockSpec((tr, C), lambda i: (i, 0)),
    )(r4)


N_DEV = 8


def _allreduce_small(p):
    rs = p.shape[0]

    def body(x_ref, sum_ref, all_ref, send_sems, recv_sems, local_sem):
        x, y, c = lax.axis_index("x"), lax.axis_index("y"), lax.axis_index("c")
        me, sibling = (x, y, c), (x, y, 1 - c)
        chips = [(1 - x, y), (x, 1 - y), (1 - x, 1 - y)]

        def rows(px, py, pc):
            return all_ref.at[pl.ds((4 * px + 2 * py + pc) * rs, rs), :]

        def copy(k, block, to, src=None):
            return pltpu.make_async_remote_copy(
                src_ref=rows(*block) if src is None else src, dst_ref=rows(*block),
                send_sem=send_sems.at[k], recv_sem=recv_sems.at[k], device_id=to, device_id_type=MESH)

        mine = pltpu.make_async_copy(x_ref, rows(*me), local_sem)
        mine.start()
        first = [copy(0, me, sibling, src=x_ref)]
        first += [copy(1 + j, me, (*chip, c), src=x_ref) for j, chip in enumerate(chips)]
        for cp in first:
            cp.start()
        passed = [copy(4 + j, (*chip, c), sibling) for j, chip in enumerate(chips)]
        for j, chip in enumerate(chips):
            copy(1 + j, (*chip, c), me).wait_recv()
            passed[j].start()
        copy(0, sibling, me).wait_recv()
        for j, chip in enumerate(chips):
            copy(4 + j, (*chip, 1 - c), me).wait_recv()
        for cp in first + passed:
            cp.wait_send()
        mine.wait()
        acc = all_ref[0:rs, :]
        for d in range(1, N_DEV):
            acc = acc + all_ref[d * rs:(d + 1) * rs, :]
        sum_ref[...] = acc

    vmem = pl.BlockSpec(memory_space=pltpu.VMEM)
    return pl.pallas_call(
        body, name="allreduce_small", out_shape=jax.ShapeDtypeStruct((rs, 128), F32),
        in_specs=[vmem], out_specs=vmem,
        scratch_shapes=[pltpu.VMEM((N_DEV * rs, 128), F32), pltpu.SemaphoreType.DMA((7,)), pltpu.SemaphoreType.DMA((7,)),
                        pltpu.SemaphoreType.DMA],
    )(p)


WEIGHTS = ('ffn1_norm', 'ffn1_w_gate', 'ffn1_w_up', 'ffn1_w_down', 'mix_norm', 'w_in', 'q_a_norm', 'w_q_b',
           'kv_a_norm', 'w_kv_b', 'q_head_norm', 'k_head_norm', 'conv_w', 'conv_b', 'a_log_fwd', 'a_log_bwd',
           'dt_bias_fwd', 'dt_bias_bwd', 'd_skip', 'ssm_norm', 'w_attn_branch', 'w_ssm_branch', 'w_out',
           'ffn2_norm', 'ffn2_w_gate', 'ffn2_w_up', 'ffn2_w_down')
PACKED = (('ffn1_w_gate', (D_MODEL, D_FF), 1), ('ffn1_w_up', (D_MODEL, D_FF), 1), ('ffn1_w_down', (D_FF, D_MODEL), 0),
          ('w_in', (D_MODEL, sum(IN_SPLITS)), 1), ('w_q_b', (Q_LORA, N_HEADS * QK_HEAD), 1),
          ('w_kv_b', (KV_LORA, N_HEADS * (QK_NOPE + V_HEAD)), 1),
          ('w_attn_branch', (N_HEADS * V_HEAD, D_MODEL), 0), ('w_ssm_branch', (D_INNER, D_MODEL), 0),
          ('w_out', (D_MODEL, D_MODEL), 0),
          ('ffn2_w_gate', (D_MODEL, D_FF), 1), ('ffn2_w_up', (D_MODEL, D_FF), 1), ('ffn2_w_down', (D_FF, D_MODEL), 0))
PACK_W = 1024
N_CHIPS = 4
SMALL = (('ffn1_norm', 1024), ('mix_norm', 1024), ('q_a_norm', 384), ('kv_a_norm', 256), ('q_head_norm', 96),
         ('k_head_norm', 96), ('conv_b', 3072), ('a_log_fwd', 32), ('a_log_bwd', 32), ('dt_bias_fwd', 32),
         ('dt_bias_bwd', 32), ('d_skip', 32), ('ssm_norm', 2048), ('ffn2_norm', 1024),
         ('conv_w', CONV_WIDTH * XBC_DIM), ('loss', 1))


TRANSPOSED = ('ffn1_w_gate', 'ffn1_w_up', 'w_in', 'ffn2_w_gate', 'ffn2_w_up')


def _stored(name, a):
    return a.T if name in TRANSPOSED else a


def _shard_shape(name, shape, axis):
    sh = tuple(s // N_CHIPS if a == axis else s for a, s in enumerate(shape))
    return sh[::-1] if name in TRANSPOSED else sh


def _by_rows(name, axis):
    return name in TRANSPOSED or axis == 0


def _pack_layout():
    out, r = {}, 0
    for name, shape, axis in PACKED:
        n = math.prod(shape) // N_CHIPS // PACK_W
        out[name] = (r, n)
        r += n
    return out, -(-r // 32) * 32


def _pack(shards):
    layout, rows = _pack_layout()
    parts = [shards[name].reshape(-1, PACK_W) for name, _, _ in PACKED]
    parts.append(jnp.zeros((rows - sum(p.shape[0] for p in parts), PACK_W), parts[0].dtype))
    return jnp.concatenate(parts, axis=0)


def _unpack(packed):
    layout, _ = _pack_layout()
    return {name: packed[layout[name][0]:layout[name][0] + layout[name][1]].reshape(_shard_shape(name, shape, axis))
            for name, shape, axis in PACKED}


def _full_from_slots(slots):
    layout, _ = _pack_layout()
    out = {}
    for name, shape, axis in PACKED:
        r, n = layout[name]
        if _by_rows(name, axis):
            out[name] = slots[:, r:r + n].reshape(N_CHIPS * n, PACK_W)
        else:
            sh = _shard_shape(name, shape, axis)
            out[name] = jnp.concatenate([slots[j, r:r + n].reshape(sh) for j in range(N_CHIPS)], axis=axis)
    return out


def _slots_from_full(full):
    layout, rows = _pack_layout()
    parts = []
    for name, shape, axis in PACKED:
        r, n = layout[name]
        if _by_rows(name, axis):
            parts.append(full[name].reshape(N_CHIPS, n, PACK_W))
        else:
            size = shape[axis] // N_CHIPS
            parts.append(jnp.stack([lax.slice_in_dim(full[name], j * size, (j + 1) * size, axis=axis).reshape(n, PACK_W)
                                    for j in range(N_CHIPS)]))
    parts.append(jnp.zeros((N_CHIPS, rows - sum(p.shape[1] for p in parts), PACK_W), parts[0].dtype))
    return jnp.concatenate(parts, axis=1)


def _pack_small(vals):
    parts = []
    for name, n in SMALL:
        pad = -(-n // 128) * 128 - n
        parts.append(jnp.pad(vals[name].reshape(-1).astype(F32), (0, pad)).reshape(-1, 128))
    rows = sum(p.shape[0] for p in parts)
    parts.append(jnp.zeros((-(-rows // 8) * 8 - rows, 128), F32))
    return jnp.concatenate(parts, axis=0)


def _unpack_small(packed):
    out, r = {}, 0
    for name, n in SMALL:
        k = -(-n // 128)
        out[name] = packed[r:r + k].reshape(-1)[:n]
        r += k
    return out


def _pad_heads(w, axis, per_head, lo, hi):
    shape = w.shape
    w = w.reshape(shape[:axis] + (N_HEADS, per_head) + shape[axis + 1:])
    w = lax.slice_in_dim(w, lo, hi, axis=axis + 1)
    pad = [(0, 0)] * w.ndim
    pad[axis + 1] = (0, HP - (hi - lo))
    w = jnp.pad(w, pad)
    return w.reshape(shape[:axis] + (N_HEADS * HP,) + shape[axis + 1:])


def _unpad_heads(w, axis, keep):
    shape = w.shape
    w = w.reshape(shape[:axis] + (N_HEADS, HP) + shape[axis + 1:])
    return lax.slice_in_dim(w, 0, keep, axis=axis + 1)


def _pad_w_in(wt):
    o = [0]
    for s in IN_SPLITS:
        o.append(o[-1] + s)
    cq, ckv, kpe, z, xbc, dtf, dtb, ga, gb = [wt[o[i]:o[i + 1]] for i in range(len(IN_SPLITS))]
    kpe_pad = jnp.pad(kpe, ((QK_NOPE, HP - QK_HEAD), (0, 0)))
    dt_pad = jnp.pad(jnp.concatenate([dtf, dtb], axis=0), ((0, HP - 2 * SSM_HEADS), (0, 0)))
    return jnp.concatenate([z, ga, gb, xbc, cq, ckv, kpe_pad, dt_pad], axis=0)


def _unpad_w_in(gt):
    z, ga, gb, xbc = gt[U_Z:U_GA], gt[U_GA:U_GB], gt[U_GB:U_XBC], gt[U_XBC:U_SMALL]
    s = gt[U_SMALL:]
    cq, ckv = s[S_CQ:S_CKV], s[S_CKV:S_KPE]
    kpe = s[S_KPE + QK_NOPE:S_KPE + QK_HEAD]
    dtf, dtb = s[S_DT:S_DT + SSM_HEADS], s[S_DT + SSM_HEADS:S_DT + 2 * SSM_HEADS]
    return jnp.concatenate([cq, ckv, kpe, z, xbc, dtf, dtb, ga, gb], axis=0)


def _lanes128(parts):
    row = jnp.concatenate([p.reshape(-1) for p in parts])
    return jnp.pad(row, (0, HP - row.shape[0])).reshape(1, HP)


FF_TILE = D_FF // 2
WGRAD = BF


def _ffn_fwd(x, g, wg_t, wu_t, wd, tag):
    h = _rms_fwd(x, g, name=tag + "_norm")
    gate, up, act = _mm([h], [wg_t, wu_t], name=tag + "_up", tb=True, out_dtypes=(F32, F32, BF), tm=512, tn=FF_TILE,
                        epilogue=lambda a, b: (a, b, _silu(a) * b))
    out = _mm([act], [wd], name=tag + "_down", extras=[x], epilogue=lambda acc, r: (r + 0.5 * acc,))
    return out, (h, gate, up, act)


def _ffn_bwd(dout, dout_bf, x, g, wg_t, wu_t, wd, saved, tag):
    h, gate, up, act = saved
    dgate, dup = _mm([dout_bf], [wd], name=tag + "_down_dx", tb=True, extras=[gate, up], out_dtypes=(BF, BF),
                     tm=512, tn=FF_TILE, epilogue=lambda acc, a, b: (0.5 * acc * b * _dsilu(a), 0.5 * acc * _silu(a)))
    dwd = _mm([act], [dout_bf], name=tag + "_down_dw", ta=True, tm=FF_TILE, tk=1024, out_dtypes=(WGRAD,),
              epilogue=lambda acc: (0.5 * acc,))
    dwg_t, dwu_t = _mm([dgate, dup], [h, h], name=tag + "_up_dw", ta=True, separate=True, out_dtypes=(WGRAD, WGRAD),
                       tm=FF_TILE, tk=1024)
    dh = _mm([dgate, dup], [wg_t, wu_t], name=tag + "_up_dx")
    dx, dx_bf, dg = _rms_bwd(dh, x, g, name=tag + "_norm_bwd", add=dout, out_dtypes=(F32, BF))
    return dx, dx_bf, dg, dwg_t, dwu_t, dwd


KPE_BLK = (U_SMALL + S_KPE) // HP
SMALL_BLK = U_SMALL // SMALL_W


def _local_step(x, pos_col, target, W, P):
    T = x.shape[0]
    sig = jax.nn.sigmoid
    x1, ffn1 = _ffn_fwd(x, P["ffn1_norm"], W["wg1"], W["wu1"], W["wd1"], "ffn1")
    h = _rms_fwd(x1, P["mix_norm"], name="mix_norm")
    u = _mm([h], [W["w_in"]], name="in_proj", tb=True, tn=1152)
    cqn = _rms_fwd(u, P["q_a_norm"], name="q_a_norm", blk_w=SMALL_W, blk_idx=SMALL_BLK, off=S_CQ, width=Q_LORA)
    ckvn = _rms_fwd(u, P["kv_a_norm"], name="kv_a_norm", blk_w=SMALL_W, blk_idx=SMALL_BLK, off=S_CKV, width=KV_LORA)
    q_raw = _mm([cqn], [W["wq"]], name="q_proj")
    k_raw, v = _mm([ckvn], [W["wk"], W["wv"]], name="kv_proj", out_dtypes=(F32, BF))
    rc, rs = _rope_tables(pos_col, P["freq"])
    q = _qk_prep_fwd(q_raw, None, P["q_head_norm"], rc, rs, name="q_prep", out_scale=Q_SCALE)
    k = _qk_prep_fwd(k_raw, u, P["k_head_norm"], rc, rs, name="k_prep", kpe_blk=KPE_BLK)
    o, lse = _attn_fwd(q, k, v)
    pre, act = _conv_fwd(u, P["conv_w8"], P["conv_b"])
    scan_cols, scan_rows = _ssd_prep(u, P["dt_bias8"], P["a_log8"])
    y_f, st_f = _ssd_fwd(act, scan_cols, scan_rows, rev=False, name="ssd_fwd_f")
    y_b, st_b = _ssd_fwd(act, scan_cols, scan_rows, rev=True, name="ssd_fwd_b")
    ysum, m = _ssm_combine_fwd(y_f, y_b, act, u, P["d_skip_lanes"], P["ssm_norm"])
    ab = _mm([o], [W["pa"]], name="attn_branch")
    mb, merged = _mm([m], [W["pb"]], name="ssm_branch", extras=[ab, u, u], extra_offs=(0, U_GA, U_GB), out_dtypes=(F32, BF),
                     epilogue=lambda acc, a, ga, gb: (acc, sig(ga) * a + sig(gb) * acc))
    x2 = _mm([merged], [W["wo"]], name="out_proj", extras=[x1], epilogue=lambda acc, r: (r + acc,))
    y, ffn2 = _ffn_fwd(x2, P["ffn2_norm"], W["wg2"], W["wu2"], W["wd2"], "ffn2")
    dy, dy_bf, loss = _loss_head(y, target)
    dx2, dx2_bf, dg_ffn2, dwg2, dwu2, dwd2 = _ffn_bwd(dy, dy_bf, x2, P["ffn2_norm"], W["wg2"], W["wu2"], W["wd2"], ffn2,
                                                      "ffn2")

    def gate_bwd(dmrg, a, b, ga, gb):
        sa, sb = sig(ga), sig(gb)
        return dmrg * sa, dmrg * sb, dmrg * a * sa * (1.0 - sa), dmrg * b * sb * (1.0 - sb)

    dab, dmb, dga, dgb = _mm([dx2_bf], [W["wo"]], name="out_proj_dx", tb=True, extras=[ab, mb, u, u],
                             extra_offs=(0, 0, U_GA, U_GB), out_dtypes=(BF,) * 4, epilogue=gate_bwd)
    dwo = _mm([merged], [dx2_bf], name="out_proj_dw", ta=True, out_dtypes=(WGRAD,))
    dpa = _mm([o], [dab], name="attn_branch_dw", ta=True, out_dtypes=(WGRAD,))
    do = _mm([dab], [W["pa"]], name="attn_branch_dx", tb=True)
    dpb = _mm([m], [dmb], name="ssm_branch_dw", ta=True, out_dtypes=(WGRAD,))
    dm = _mm([dmb], [W["pb"]], name="ssm_branch_dx", tb=True)
    dyssd, dz, dxs_skip, dg_ssm, dskip = _ssm_combine_bwd(dm, ysum, act, u, P["d_skip_lanes"], P["ssm_norm"])
    dxs_f, db_f, dc_f, dsel_f, dtot_f = _ssd_bwd(act, scan_cols, scan_rows, st_f, dyssd, rev=False, name="ssd_bwd_f")
    dxs_b, db_b, dc_b, dsel_b, dtot_b = _ssd_bwd(act, scan_cols, scan_rows, st_b, dyssd, rev=True, name="ssd_bwd_b")
    ddt, dalog, dbias = _ssd_prep_bwd(u, P["dt_bias8"], P["a_log8"], dsel_f, dtot_f, dsel_b, dtot_b)
    dxbc, dconv = [], []
    for tag, col0, parts in (("x", 0, [dxs_f, dxs_b, dxs_skip]), ("b", D_INNER, [db_f, db_b]),
                             ("c", D_INNER + SSM_GROUPS * D_STATE, [dc_f, dc_b])):
        dpre = _conv_dpre(parts, pre, col0, name="conv_dpre_" + tag)
        dxp, dwp = _conv_bwd(dpre, u, P["conv_w8"], col0, name="conv_bwd_" + tag)
        dxbc.append(dxp)
        dconv.append(dwp)
    dconv = jnp.concatenate(dconv, axis=1)
    dq, dk, dv = _attn_bwd(q, k, v, do, o, lse)
    dq_raw, dg_qh = _qk_prep_bwd(dq, q_raw, None, P["q_head_norm"], rc, rs, name="q_prep_bwd", in_scale=ATTN_SCALE)
    dk_raw, dg_kh, dkpe = _qk_prep_bwd(dk, k_raw, u, P["k_head_norm"], rc, rs, name="k_prep_bwd", kpe_blk=KPE_BLK,
                                       in_scale=1.0 / LOG2E)
    dwq = _mm([cqn], [dq_raw], name="q_proj_dw", ta=True, out_dtypes=(WGRAD,))
    dcqn = _mm([dq_raw], [W["wq"]], name="q_proj_dx", tb=True)
    dwk, dwv = _mm([ckvn], [dk_raw, dv], name="kv_proj_dw", ta=True, out_dtypes=(WGRAD, WGRAD))
    dckvn = _mm([dk_raw, dv], [W["wk"], W["wv"]], name="kv_proj_dx", tb=True)
    dcq, dg_qa = _rms_bwd(dcqn, u, P["q_a_norm"], name="q_a_norm_bwd", blk_w=SMALL_W, blk_idx=SMALL_BLK, off=S_CQ,
                          width=Q_LORA, out_dtypes=(BF,))
    dckv, dg_kva = _rms_bwd(dckvn, u, P["kv_a_norm"], name="kv_a_norm_bwd", blk_w=SMALL_W, blk_idx=SMALL_BLK,
                            off=S_CKV, width=KV_LORA, out_dtypes=(BF,))
    du = jnp.concatenate([dz, dga, dgb] + dxbc + [dcq, dckv, dkpe.astype(BF), ddt.astype(BF)], axis=1)
    dw_in = _mm([du], [h], name="in_proj_dw", ta=True, tm=1152, out_dtypes=(WGRAD,))
    dh = _mm([du], [W["w_in"]], name="in_proj_dx")
    dx1, dx1_bf, dg_mix = _rms_bwd(dh, x1, P["mix_norm"], name="mix_norm_bwd", add=dx2, out_dtypes=(F32, BF))
    dx, _, dg_ffn1, dwg1, dwu1, dwd1 = _ffn_bwd(dx1, dx1_bf, x, P["ffn1_norm"], W["wg1"], W["wu1"], W["wd1"], ffn1, "ffn1")
    dW = dict(wg1=dwg1, wu1=dwu1, wd1=dwd1, w_in=dw_in, wq=dwq, wk=dwk, wv=dwv, pa=dpa, pb=dpb, wo=dwo,
              wg2=dwg2, wu2=dwu2, wd2=dwd2)
    dP = dict(ffn1_norm=dg_ffn1[0], mix_norm=dg_mix[0], q_a_norm=dg_qa[0], kv_a_norm=dg_kva[0],
              q_head_norm=dg_qh[0, :QK_HEAD], k_head_norm=dg_kh[0, :QK_HEAD], conv_b=dconv[CONV_WIDTH],
              a_log_fwd=dalog[0, :SSM_HEADS], a_log_bwd=dalog[0, SSM_HEADS:2 * SSM_HEADS],
              dt_bias_fwd=dbias[0, :SSM_HEADS], dt_bias_bwd=dbias[0, SSM_HEADS:2 * SSM_HEADS],
              d_skip=dskip[0].reshape(SSM_GROUPS, HP)[:, :HG], ssm_norm=dg_ssm[0], ffn2_norm=dg_ffn2[0],
              conv_w=dconv[:CONV_WIDTH], loss=loss[0, 0])
    return dx, dW, dP


def _prepare(w, conv_w_full):
    kvb = w["w_kv_b"]
    W = dict(wg1=w["ffn1_w_gate"], wu1=w["ffn1_w_up"], wd1=w["ffn1_w_down"], w_in=_pad_w_in(w["w_in"]),
             wq=_pad_heads(w["w_q_b"], 1, QK_HEAD, 0, QK_HEAD),
             wk=_pad_heads(kvb, 1, QK_NOPE + V_HEAD, 0, QK_NOPE),
             wv=_pad_heads(kvb, 1, QK_NOPE + V_HEAD, QK_NOPE, QK_NOPE + V_HEAD),
             pa=_pad_heads(w["w_attn_branch"], 0, V_HEAD, 0, V_HEAD), pb=w["w_ssm_branch"], wo=w["w_out"],
             wg2=w["ffn2_w_gate"], wu2=w["ffn2_w_up"], wd2=w["ffn2_w_down"])
    inv_freq = [1.0 / (ROPE_BASE ** (j / QK_ROPE)) for j in range(0, QK_ROPE, 2)]
    freq = [0.0] * QK_NOPE + inv_freq + inv_freq + [0.0] * (HP - QK_HEAD)
    P = {n: w[n] for n in ("ffn1_norm", "mix_norm", "q_a_norm", "kv_a_norm", "ssm_norm", "ffn2_norm", "conv_b")}
    P.update(q_head_norm=_lanes128([w["q_head_norm"]]), k_head_norm=_lanes128([w["k_head_norm"]]),
             conv_w8=jnp.pad(conv_w_full, ((0, 8 - CONV_WIDTH), (0, 0))),
             dt_bias8=jnp.broadcast_to(_lanes128([w["dt_bias_fwd"], w["dt_bias_bwd"]]), (8, HP)),
             a_log8=jnp.broadcast_to(_lanes128([w["a_log_fwd"], w["a_log_bwd"]]), (8, HP)),
             d_skip_lanes=jnp.repeat(w["d_skip"].reshape(-1), PH).reshape(1, D_INNER),
             freq=jnp.asarray(freq, F32).reshape(1, HP))
    return W, P


def _unprepare(dW):
    dkvb = jnp.concatenate([_unpad_heads(dW["wk"], 1, QK_NOPE), _unpad_heads(dW["wv"], 1, V_HEAD)], axis=2)
    return dict(ffn1_w_gate=dW["wg1"], ffn1_w_up=dW["wu1"], ffn1_w_down=dW["wd1"], w_in=_unpad_w_in(dW["w_in"]),
                w_q_b=_unpad_heads(dW["wq"], 1, QK_HEAD).reshape(Q_LORA, N_HEADS * QK_HEAD),
                w_kv_b=dkvb.reshape(KV_LORA, N_HEADS * (QK_NOPE + V_HEAD)),
                w_attn_branch=_unpad_heads(dW["pa"], 0, V_HEAD).reshape(N_HEADS * V_HEAD, D_MODEL),
                w_ssm_branch=dW["pb"], w_out=dW["wo"],
                ffn2_w_gate=dW["wg2"], ffn2_w_up=dW["wu2"], ffn2_w_down=dW["wd2"])


def kernel(x, positions, ffn1_norm, ffn1_w_gate, ffn1_w_up, ffn1_w_down, mix_norm, w_in, q_a_norm, w_q_b, kv_a_norm, w_kv_b, q_head_norm, k_head_norm, conv_w, conv_b, a_log_fwd, a_log_bwd, dt_bias_fwd, dt_bias_bwd, d_skip, ssm_norm, w_attn_branch, w_ssm_branch, w_out, ffn2_norm, ffn2_w_gate, ffn2_w_up, ffn2_w_down, loss_target, m_ffn1_norm, m_ffn1_w_gate, m_ffn1_w_up, m_ffn1_w_down, m_mix_norm, m_w_in, m_q_a_norm, m_w_q_b, m_kv_a_norm, m_w_kv_b, m_q_head_norm, m_k_head_norm, m_conv_w, m_conv_b, m_a_log_fwd, m_a_log_bwd, m_dt_bias_fwd, m_dt_bias_bwd, m_d_skip, m_ssm_norm, m_w_attn_branch, m_w_ssm_branch, m_w_out, m_ffn2_norm, m_ffn2_w_gate, m_ffn2_w_up, m_ffn2_w_down, v_ffn1_norm, v_ffn1_w_gate, v_ffn1_w_up, v_ffn1_w_down, v_mix_norm, v_w_in, v_q_a_norm, v_w_q_b, v_kv_a_norm, v_w_kv_b, v_q_head_norm, v_k_head_norm, v_conv_w, v_conv_b, v_a_log_fwd, v_a_log_bwd, v_dt_bias_fwd, v_dt_bias_bwd, v_d_skip, v_ssm_norm, v_w_attn_branch, v_w_ssm_branch, v_w_out, v_ffn2_norm, v_ffn2_w_gate, v_ffn2_w_up, v_ffn2_w_down):
    given = dict(locals())
    T = x.shape[1]
    packed_names = [name for name, _, _ in PACKED]

    def two_d(a):
        return a.reshape(a.shape[1], -1) if a.ndim > 2 else a

    def kept(n, a):
        return _stored(n, two_d(a))

    w_loc = {n: kept(n, given[n]) for n in WEIGHTS}
    wb = _pack({n: w_loc[n].astype(BF) for n in packed_names})
    wf = jnp.pad(w_loc["conv_w"], ((0, 8 - CONV_WIDTH), (0, 0)))
    gb, gf = _gather_chips(wb, wf)
    full = _full_from_slots(gb)
    conv_w_full = jnp.concatenate([gf[j, :CONV_WIDTH] for j in range(N_CHIPS)], axis=1)
    full.update({n: w_loc[n] for n in WEIGHTS if n not in full and n != "conv_w"})
    W, P = _prepare(full, conv_w_full)
    dx, dW, dP = _local_step(x.reshape(T, D_MODEL), positions.reshape(T, 1).astype(F32), loss_target.reshape(T, D_MODEL), W, P)
    gp = _slots_from_full(_unprepare(dW))
    core = lax.axis_index("c").astype(jnp.int32).reshape(1)
    both_cores = _add_halves(gp, _halves_to_sibling(gp), core)
    grads = _unpack(_join_halves(_sum_slots(_exchange_chips(both_cores))))
    small = _unpack_small(_allreduce_small(_pack_small(dP)))
    grads.update({n: small[n].reshape(1, -1) for n, _ in SMALL if n not in ("conv_w", "loss")})
    grads["conv_w"] = lax.dynamic_slice_in_dim(small["conv_w"].reshape(CONV_WIDTH, XBC_DIM), _my_chip() * (XBC_DIM // N_CHIPS),
                                               XBC_DIM // N_CHIPS, axis=1)
    out_g, out_d, out_m, out_v = [], [], [], []
    for n in WEIGHTS:
        shape = given[n].shape
        delta, new_m, new_v = _adamw(w_loc[n], grads[n], kept(n, given["m_" + n]), kept(n, given["v_" + n]), name="adamw_" + n)
        for outs, a in ((out_g, grads[n]), (out_d, delta), (out_m, new_m), (out_v, new_v)):
            outs.append(_stored(n, a).reshape(shape))
    return (small["loss"].reshape(()), dx.reshape(x.shape), *out_g, *out_d, *out_m, *out_v)
```

```python
import functools
import math

import jax
import jax.numpy as jnp
from jax import lax
from jax.experimental import pallas as pl
from jax.experimental.pallas import tpu as pltpu

BF = jnp.bfloat16
F32 = jnp.float32
HI = lax.Precision.HIGHEST
MESH = pl.DeviceIdType.MESH

D_MODEL = 1024
D_FF = 2816
EPS = 1e-6
N_HEADS = 16
QK_NOPE = 64
QK_ROPE = 32
QK_HEAD = 96
V_HEAD = 64
Q_LORA = 384
KV_LORA = 256
ROPE_BASE = 10000.0
D_INNER = 2048
SSM_HEADS = 32
SSM_GROUPS = 4
D_STATE = 128
CONV_WIDTH = 5
CHUNK = 128
XBC_DIM = 3072
HP = 128
GW = D_INNER // SSM_GROUPS
HG = SSM_HEADS // SSM_GROUPS
PH = 64
U_Z, U_GA, U_GB, U_XBC, U_SMALL = 0, 2048, 3072, 4096, 7168
S_CQ, S_CKV, S_KPE, S_DT, SMALL_W = 0, 384, 640, 768, 896
U_PAD = U_SMALL + SMALL_W
IN_SPLITS = (Q_LORA, KV_LORA, QK_ROPE, D_INNER, XBC_DIM, SSM_HEADS, SSM_HEADS, D_MODEL, D_MODEL)

ADAM_LR = 0.001
ADAM_B1 = 0.9
ADAM_B2 = 0.999
ADAM_EPS = 1e-08
ADAM_WD = 0.01
ADAM_STEP = 10

NN = (((1,), (0,)), ((), ()))
NT = (((1,), (1,)), ((), ()))
TN = (((0,), (0,)), ((), ()))


def _pick(n, pref):
    best = None
    d = 128
    while d <= min(n, pref):
        if n % d == 0:
            best = d
        d += 128
    return best if best is not None else n


def _silu(x):
    return x * jax.nn.sigmoid(x)


def _dsilu(x):
    s = jax.nn.sigmoid(x)
    return s * (1.0 + x * (1.0 - s))


def _softplus(x):
    return jnp.maximum(x, 0.0) + jnp.log(1.0 + jnp.exp(-jnp.abs(x)))


def _mm(As, Bs, *, name, ta=False, tb=False, out_dtypes=(F32,), epilogue=None, extras=(), extra_offs=None,
        tm=1024, tn=512, tk=2048, separate=False):
    As, Bs, extras = list(As), list(Bs), list(extras)
    a0, b0 = As[0], Bs[0]
    M, K = (a0.shape[1], a0.shape[0]) if ta else a0.shape
    N = b0.shape[0] if tb else b0.shape[1]
    tm, tn, tk = _pick(M, tm), _pick(N, tn), _pick(K, tk)
    nk = K // tk
    n_a, n_b, n_e, n_o = len(As), len(Bs), len(extras), len(out_dtypes)
    n_acc = (n_b if n_a == 1 or separate else 1) if nk > 1 else 0
    if extra_offs is None:
        extra_offs = (0,) * n_e
    dn = (((0,) if ta else (1,), (1,) if tb else (0,)), ((), ()))
    bytes_a = sum(a.size * a.dtype.itemsize for a in As)
    bytes_b = sum(b.size * b.dtype.itemsize for b in Bs)
    n_outer = (N // tn) * bytes_a + bytes_b < (M // tm) * bytes_b + bytes_a

    def products(a_refs, b_refs):
        if n_a == 1:
            a = a_refs[0][...].astype(BF)
            return [lax.dot_general(a, b[...].astype(BF), dn, preferred_element_type=F32) for b in b_refs]
        if separate:
            return [lax.dot_general(a[...].astype(BF), b[...].astype(BF), dn, preferred_element_type=F32)
                    for a, b in zip(a_refs, b_refs)]
        total = None
        for a, b in zip(a_refs, b_refs):
            p = lax.dot_general(a[...].astype(BF), b[...].astype(BF), dn, preferred_element_type=F32)
            total = p if total is None else total + p
        return [total]

    def finish(accs, e_refs, o_refs):
        ex = [e[...] for e in e_refs]
        outs = epilogue(*accs, *ex) if epilogue is not None else tuple(accs)
        for o_ref, val in zip(o_refs, outs):
            o_ref[...] = val.astype(o_ref.dtype)

    def body(*refs):
        a_refs, b_refs = refs[:n_a], refs[n_a:n_a + n_b]
        e_refs = refs[n_a + n_b:n_a + n_b + n_e]
        o_refs = refs[n_a + n_b + n_e:n_a + n_b + n_e + n_o]
        acc_refs = refs[n_a + n_b + n_e + n_o:]
        if nk == 1:
            finish(products(a_refs, b_refs), e_refs, o_refs)
            return
        k = pl.program_id(2)

        @pl.when(k == 0)
        def _():
            for acc in acc_refs:
                acc[...] = jnp.zeros_like(acc)

        for acc, p in zip(acc_refs, products(a_refs, b_refs)):
            acc[...] += p

        @pl.when(k == nk - 1)
        def _():
            finish([acc[...] for acc in acc_refs], e_refs, o_refs)

    def at(f):
        return (lambda j, i, k: f(i, j, k)) if n_outer else f

    a_spec = pl.BlockSpec((tk, tm), at(lambda i, j, k: (k, i))) if ta else pl.BlockSpec((tm, tk), at(lambda i, j, k: (i, k)))
    b_spec = pl.BlockSpec((tn, tk), at(lambda i, j, k: (j, k))) if tb else pl.BlockSpec((tk, tn), at(lambda i, j, k: (k, j)))
    e_specs = [pl.BlockSpec((tm, tn), at(functools.partial(lambda i, j, k, o: (i, j + o), o=off // tn))) for off in extra_offs]
    for off in extra_offs:
        assert off % tn == 0
    outs = pl.pallas_call(
        body, name=name,
        out_shape=tuple(jax.ShapeDtypeStruct((M, N), dt) for dt in out_dtypes),
        grid=(N // tn, M // tm, nk) if n_outer else (M // tm, N // tn, nk),
        in_specs=[a_spec] * n_a + [b_spec] * n_b + e_specs,
        out_specs=tuple(pl.BlockSpec((tm, tn), at(lambda i, j, k: (i, j))) for _ in out_dtypes),
        scratch_shapes=[pltpu.VMEM((tm, tn), F32)] * n_acc,
        compiler_params=pltpu.CompilerParams(dimension_semantics=("parallel", "parallel", "arbitrary")),
    )(*As, *Bs, *extras)
    return outs[0] if n_o == 1 else outs


def _rms_fwd(x, g, *, name, blk_w=None, blk_idx=0, off=0, width=None, out_dtype=BF):
    T = x.shape[0]
    blk_w = x.shape[1] if blk_w is None else blk_w
    width = blk_w if width is None else width
    tt = _pick(T, 512)

    def body(x_ref, g_ref, o_ref):
        xf = x_ref[:, off:off + width]
        r = lax.rsqrt(jnp.mean(xf * xf, axis=-1, keepdims=True) + EPS)
        o_ref[...] = (xf * r * g_ref[...]).astype(o_ref.dtype)

    return pl.pallas_call(
        body, name=name, out_shape=jax.ShapeDtypeStruct((T, width), out_dtype), grid=(T // tt,),
        in_specs=[pl.BlockSpec((tt, blk_w), lambda i: (i, blk_idx)), pl.BlockSpec((1, width), lambda i: (0, 0))],
        out_specs=pl.BlockSpec((tt, width), lambda i: (i, 0)),
    )(x, g)


def _rms_bwd(dy, x, g, *, name, blk_w=None, blk_idx=0, off=0, width=None, add=None, out_dtypes=(F32,)):
    T = x.shape[0]
    blk_w = x.shape[1] if blk_w is None else blk_w
    width = blk_w if width is None else width
    tt = _pick(T, 512)
    has_add = add is not None
    n_dx = len(out_dtypes)

    def body(*refs):
        dy_ref, x_ref, g_ref = refs[:3]
        dx_refs, dg_ref = refs[3 + has_add:3 + has_add + n_dx], refs[-1]
        xf = x_ref[:, off:off + width]
        d = dy_ref[...].astype(F32)
        r = lax.rsqrt(jnp.mean(xf * xf, axis=-1, keepdims=True) + EPS)
        gd = d * g_ref[...]
        dx = r * gd - xf * (r * r * r) * jnp.mean(gd * xf, axis=-1, keepdims=True)
        if has_add:
            dx = dx + refs[3][...]
        for dx_ref in dx_refs:
            dx_ref[...] = dx.astype(dx_ref.dtype)

        @pl.when(pl.program_id(0) == 0)
        def _():
            dg_ref[...] = jnp.zeros_like(dg_ref)

        dg_ref[...] += jnp.broadcast_to(jnp.sum(d * xf * r, axis=0, keepdims=True), dg_ref.shape)

    row = pl.BlockSpec((tt, width), lambda i: (i, 0))
    in_specs = [row, pl.BlockSpec((tt, blk_w), lambda i: (i, blk_idx)), pl.BlockSpec((1, width), lambda i: (0, 0))]
    args = [dy, x, g]
    if has_add:
        in_specs.append(row)
        args.append(add)
    return pl.pallas_call(
        body, name=name,
        out_shape=tuple(jax.ShapeDtypeStruct((T, width), dt) for dt in out_dtypes) + (jax.ShapeDtypeStruct((8, width), F32),),
        grid=(T // tt,), in_specs=in_specs,
        out_specs=(row,) * n_dx + (pl.BlockSpec((8, width), lambda i: (0, 0)),),
        compiler_params=pltpu.CompilerParams(dimension_semantics=("arbitrary",)),
    )(*args)


def _rope_tables(pos_col, freq_lane):
    T = pos_col.shape[0]
    tt = _pick(T, 512)

    def body(p_ref, f_ref, c_ref, s_ref):
        ang = p_ref[...] * f_ref[...]
        lane = lax.broadcasted_iota(jnp.int32, ang.shape, 1)
        c_ref[...] = jnp.where(lane < QK_HEAD, jnp.cos(ang), 0.0)
        sn = jnp.sin(ang)
        s_ref[...] = jnp.where((lane >= QK_NOPE) & (lane < QK_NOPE + 16), -sn,
                               jnp.where((lane >= QK_NOPE + 16) & (lane < QK_HEAD), sn, 0.0))

    return pl.pallas_call(
        body, name="rope_tables", out_shape=(jax.ShapeDtypeStruct((T, HP), F32),) * 2, grid=(T // tt,),
        in_specs=[pl.BlockSpec((tt, 1), lambda i: (i, 0)), pl.BlockSpec((1, HP), lambda i: (0, 0))],
        out_specs=(pl.BlockSpec((tt, HP), lambda i: (i, 0)),) * 2,
    )(pos_col, freq_lane)


def _swap_rope_halves(n):
    lane = lax.broadcasted_iota(jnp.int32, n.shape, 1)
    lo = (lane >= QK_NOPE) & (lane < QK_NOPE + 16)
    hi = (lane >= QK_NOPE + 16) & (lane < QK_HEAD)
    return jnp.where(lo, pltpu.roll(n, HP - 16, 1), jnp.where(hi, pltpu.roll(n, 16, 1), 0.0))


def _qk_prep_fwd(raw, kpe, gain, C, S, *, name, kpe_blk=0, out_scale=1.0):
    T = raw.shape[0]
    tt = _pick(T, 256)
    has_kpe = kpe is not None

    def body(*refs):
        if has_kpe:
            raw_ref, kpe_ref, g_ref, c_ref, s_ref, o_ref = refs
        else:
            raw_ref, g_ref, c_ref, s_ref, o_ref = refs
        for h in range(N_HEADS):
            hs = slice(HP * h, HP * (h + 1))
            xr = raw_ref[:, hs] + kpe_ref[...] if has_kpe else raw_ref[:, hs]
            r = lax.rsqrt(jnp.sum(xr * xr, axis=-1, keepdims=True) * (1.0 / QK_HEAD) + EPS)
            n = xr * r * g_ref[...]
            o_ref[:, hs] = ((n * c_ref[...] + _swap_rope_halves(n) * s_ref[...]) * out_scale).astype(o_ref.dtype)

    heads = pl.BlockSpec((tt, N_HEADS * HP), lambda i: (i, 0))
    shared = pl.BlockSpec((tt, HP), lambda i: (i, 0))
    kpe_spec = pl.BlockSpec((tt, HP), lambda i: (i, kpe_blk))
    in_specs = [heads] + ([kpe_spec] if has_kpe else []) + [pl.BlockSpec((1, HP), lambda i: (0, 0)), shared, shared]
    args = [raw] + ([kpe] if has_kpe else []) + [gain, C, S]
    return pl.pallas_call(
        body, name=name, out_shape=jax.ShapeDtypeStruct(raw.shape, BF), grid=(T // tt,),
        in_specs=in_specs, out_specs=heads,
    )(*args)


def _qk_prep_bwd(dout, raw, kpe, gain, C, S, *, name, kpe_blk=0, in_scale=1.0):
    T = raw.shape[0]
    tt = _pick(T, 256)
    has_kpe = kpe is not None

    def body(*refs):
        if has_kpe:
            d_ref, raw_ref, kpe_ref, g_ref, c_ref, s_ref, dx_ref, dg_ref, dkpe_ref = refs
        else:
            d_ref, raw_ref, g_ref, c_ref, s_ref, dx_ref, dg_ref = refs
        dg = jnp.zeros((1, HP), F32)
        dkpe = jnp.zeros((tt, HP), F32)
        for h in range(N_HEADS):
            hs = slice(HP * h, HP * (h + 1))
            xr = raw_ref[:, hs] + kpe_ref[...] if has_kpe else raw_ref[:, hs]
            d = d_ref[:, hs].astype(F32) * in_scale
            r = lax.rsqrt(jnp.sum(xr * xr, axis=-1, keepdims=True) * (1.0 / QK_HEAD) + EPS)
            dn = d * c_ref[...] + _swap_rope_halves(d * s_ref[...])
            gd = dn * g_ref[...]
            dx = r * gd - xr * (r * r * r) * (jnp.sum(gd * xr, axis=-1, keepdims=True) * (1.0 / QK_HEAD))
            dx_ref[:, hs] = dx.astype(dx_ref.dtype)
            dg = dg + jnp.sum(dn * xr * r, axis=0, keepdims=True)
            dkpe = dkpe + dx

        @pl.when(pl.program_id(0) == 0)
        def _():
            dg_ref[...] = jnp.zeros_like(dg_ref)

        dg_ref[...] += jnp.broadcast_to(dg, dg_ref.shape)
        if has_kpe:
            dkpe_ref[...] = dkpe

    heads = pl.BlockSpec((tt, N_HEADS * HP), lambda i: (i, 0))
    shared = pl.BlockSpec((tt, HP), lambda i: (i, 0))
    kpe_spec = pl.BlockSpec((tt, HP), lambda i: (i, kpe_blk))
    in_specs = [heads, heads] + ([kpe_spec] if has_kpe else []) + [pl.BlockSpec((1, HP), lambda i: (0, 0)), shared, shared]
    args = [dout, raw] + ([kpe] if has_kpe else []) + [gain, C, S]
    out_shape = [jax.ShapeDtypeStruct(raw.shape, BF), jax.ShapeDtypeStruct((8, HP), F32)]
    out_specs = [heads, pl.BlockSpec((8, HP), lambda i: (0, 0))]
    if has_kpe:
        out_shape.append(jax.ShapeDtypeStruct((T, HP), F32))
        out_specs.append(shared)
    return pl.pallas_call(
        body, name=name, out_shape=tuple(out_shape), grid=(T // tt,),
        in_specs=in_specs, out_specs=tuple(out_specs),
        compiler_params=pltpu.CompilerParams(dimension_semantics=("arbitrary",)),
    )(*args)


ATTN_SCALE = 1.0 / math.sqrt(QK_HEAD)
LOG2E = 1.0 / math.log(2.0)
Q_SCALE = ATTN_SCALE * LOG2E


def _attn_fwd(q, k, v):
    T = q.shape[0]
    tq = _pick(T, 256)

    def body(q_ref, k_ref, v_ref, o_ref, lse_ref):
        s = lax.dot_general(q_ref[...], k_ref[...], NT, preferred_element_type=F32)
        m = jnp.max(s, axis=-1, keepdims=True)
        p = jnp.exp2(s - m)
        l = jnp.sum(p, axis=-1, keepdims=True)
        o = jnp.dot(p.astype(BF), v_ref[...], preferred_element_type=F32)
        o_ref[...] = o / l
        lse_ref[...] = jnp.broadcast_to(m + jnp.log2(l), lse_ref.shape)

    qs = pl.BlockSpec((tq, HP), lambda h, i: (i, h))
    kv = pl.BlockSpec((T, HP), lambda h, i: (0, h))
    return pl.pallas_call(
        body, name="attn_fwd", out_shape=(jax.ShapeDtypeStruct(q.shape, F32),) * 2, grid=(N_HEADS, T // tq),
        in_specs=[qs, kv, kv], out_specs=(qs, qs),
        compiler_params=pltpu.CompilerParams(dimension_semantics=("parallel", "parallel")),
    )(q, k, v)


def _attn_bwd(q, k, v, do, o, lse):
    T = q.shape[0]
    tb = _pick(T, 512)
    nb = T // tb

    def body(q_ref, k_ref, v_ref, do_ref, o_ref, lse_ref, dq_ref, dk_ref, dv_ref, delta_scr, dob_scr):
        dq_ref[...] = jnp.zeros_like(dq_ref)

        def per_q_tile(i, carry):
            qs = pl.ds(pl.multiple_of(i * tb, tb), tb)
            doi = do_ref[qs, :]
            delta_scr[qs, :] = jnp.sum(doi * o_ref[qs, :], axis=-1, keepdims=True)
            dob_scr[qs, :] = doi.astype(BF)
            return carry

        lax.fori_loop(0, nb, per_q_tile, 0)

        def k_loop(j, carry):
            ks = pl.ds(pl.multiple_of(j * tb, tb), tb)
            kj, vj = k_ref[ks, :], v_ref[ks, :]

            def q_loop(i, acc):
                dk_acc, dv_acc = acc
                qs = pl.ds(pl.multiple_of(i * tb, tb), tb)
                qi = q_ref[qs, :]
                delta = delta_scr[qs, :]
                dob = dob_scr[qs, :]
                s = lax.dot_general(qi, kj, NT, preferred_element_type=F32)
                p = jnp.exp2(s - lse_ref[qs, 0:1])
                dp = lax.dot_general(dob, vj, NT, preferred_element_type=F32)
                ds = (p * (dp - delta)).astype(BF)
                dv_acc = dv_acc + lax.dot_general(p.astype(BF), dob, TN, preferred_element_type=F32)
                dk_acc = dk_acc + lax.dot_general(ds, qi, TN, preferred_element_type=F32)
                dq_ref[qs, :] += jnp.dot(ds, kj, preferred_element_type=F32)
                return dk_acc, dv_acc

            zero = jnp.zeros((tb, HP), F32)
            dk_acc, dv_acc = lax.fori_loop(0, nb, q_loop, (zero, zero))
            dk_ref[ks, :] = dk_acc
            dv_ref[ks, :] = dv_acc.astype(dv_ref.dtype)
            return carry

        lax.fori_loop(0, nb, k_loop, 0)

    spec = pl.BlockSpec((T, HP), lambda h: (0, h))
    return pl.pallas_call(
        body, name="attn_bwd",
        out_shape=(jax.ShapeDtypeStruct(q.shape, F32), jax.ShapeDtypeStruct(q.shape, F32), jax.ShapeDtypeStruct(q.shape, BF)),
        grid=(N_HEADS,), in_specs=[spec] * 6, out_specs=(spec,) * 3,
        scratch_shapes=[pltpu.VMEM((T, 1), F32), pltpu.VMEM((T, HP), BF)],
        compiler_params=pltpu.CompilerParams(dimension_semantics=("parallel",), vmem_limit_bytes=2 * 15 * T * HP * 2 + (8 << 20)),
    )(q, k, v, do, o, lse)


CONV_TC = 512
CONV_PAD = CONV_WIDTH // 2


def _halo_specs(tr, col_of):
    r8 = tr // 8
    cur = pl.BlockSpec((tr, CONV_TC), lambda j, i: (i, col_of(j)))
    prev = pl.BlockSpec((8, CONV_TC), lambda j, i: (jnp.maximum(i * r8 - 1, 0), col_of(j)))

    def nxt_map(j, i, n8):
        return (jnp.minimum((i + 1) * r8, n8 - 1), col_of(j))

    return cur, prev, nxt_map


def _with_halo(prev_ref, cur_ref, next_ref, i, n_i):
    prev = jnp.where(i == 0, 0.0, prev_ref[...].astype(F32))
    nxt = jnp.where(i == n_i - 1, 0.0, next_ref[...].astype(F32))
    return jnp.concatenate([prev, cur_ref[...].astype(F32), nxt], axis=0)


def _conv_fwd(u, w8, b):
    T = u.shape[0]
    tr = _pick(T, 512)
    n_i = T // tr
    c0 = U_XBC // CONV_TC
    cur, prev, nxt_map = _halo_specs(tr, lambda j: c0 + j)
    nxt = pl.BlockSpec((8, CONV_TC), functools.partial(nxt_map, n8=T // 8))

    def body(p_ref, c_ref, n_ref, w_ref, b_ref, pre_ref, act_ref):
        i = pl.program_id(1)
        full = _with_halo(p_ref, c_ref, n_ref, i, n_i)
        acc = jnp.broadcast_to(b_ref[...], (tr, CONV_TC))
        for kk in range(CONV_WIDTH):
            acc = acc + full[8 - CONV_PAD + kk:8 - CONV_PAD + kk + tr, :] * w_ref[kk:kk + 1, :]
        pre_ref[...] = acc
        act_ref[...] = _silu(acc)

    out = pl.BlockSpec((tr, CONV_TC), lambda j, i: (i, j))
    return pl.pallas_call(
        body, name="conv_fwd", out_shape=(jax.ShapeDtypeStruct((T, XBC_DIM), F32),) * 2,
        grid=(XBC_DIM // CONV_TC, n_i),
        in_specs=[prev, cur, nxt, pl.BlockSpec((8, CONV_TC), lambda j, i: (0, j)), pl.BlockSpec((1, CONV_TC), lambda j, i: (0, j))],
        out_specs=(out, out),
    )(u, u, u, w8, b)


def _conv_dpre(dacts, pre, col0, *, name):
    T, width = dacts[0].shape
    tt = _pick(T, 512)
    n_d = len(dacts)
    c0 = col0 // CONV_TC

    def body(*refs):
        d = refs[0][...]
        for r in refs[1:n_d]:
            d = d + r[...]
        refs[n_d + 1][...] = d * _dsilu(refs[n_d][...])

    blk = pl.BlockSpec((tt, CONV_TC), lambda j, i: (i, j))
    return pl.pallas_call(
        body, name=name, out_shape=jax.ShapeDtypeStruct((T, width), F32), grid=(width // CONV_TC, T // tt),
        in_specs=[blk] * n_d + [pl.BlockSpec((tt, CONV_TC), lambda j, i: (i, c0 + j))], out_specs=blk,
    )(*dacts, pre)


def _conv_bwd(dpre, u, w8, col0, *, name):
    T, width = dpre.shape
    tr = _pick(T, 512)
    n_i = T // tr
    cd = col0 // CONV_TC
    cx = (U_XBC + col0) // CONV_TC
    d_cur, d_prev, d_nxt_map = _halo_specs(tr, lambda j: j)
    x_cur, x_prev, x_nxt_map = _halo_specs(tr, lambda j: cx + j)
    d_nxt = pl.BlockSpec((8, CONV_TC), functools.partial(d_nxt_map, n8=T // 8))
    x_nxt = pl.BlockSpec((8, CONV_TC), functools.partial(x_nxt_map, n8=T // 8))

    def body(dp_ref, dc_ref, dn_ref, xp_ref, xc_ref, xn_ref, w_ref, dx_ref, dw_ref):
        i = pl.program_id(1)
        dfull = _with_halo(dp_ref, dc_ref, dn_ref, i, n_i)
        xfull = _with_halo(xp_ref, xc_ref, xn_ref, i, n_i)
        dcur = dc_ref[...]
        dx = jnp.zeros((tr, CONV_TC), F32)
        rows = []
        for kk in range(CONV_WIDTH):
            dx = dx + dfull[8 + CONV_PAD - kk:8 + CONV_PAD - kk + tr, :] * w_ref[kk:kk + 1, :]
            rows.append(jnp.sum(dcur * xfull[8 - CONV_PAD + kk:8 - CONV_PAD + kk + tr, :], axis=0, keepdims=True))
        rows.append(jnp.sum(dcur, axis=0, keepdims=True))
        rows.append(jnp.zeros((2, CONV_TC), F32))
        dx_ref[...] = dx.astype(dx_ref.dtype)

        @pl.when(i == 0)
        def _():
            dw_ref[...] = jnp.zeros_like(dw_ref)

        dw_ref[...] += jnp.concatenate(rows, axis=0)

    out = pl.BlockSpec((tr, CONV_TC), lambda j, i: (i, j))
    return pl.pallas_call(
        body, name=name, out_shape=(jax.ShapeDtypeStruct((T, width), BF), jax.ShapeDtypeStruct((8, width), F32)),
        grid=(width // CONV_TC, n_i),
        in_specs=[d_prev, d_cur, d_nxt, x_prev, x_cur, x_nxt, pl.BlockSpec((8, CONV_TC), lambda j, i: (0, cd + j))],
        out_specs=(out, pl.BlockSpec((8, CONV_TC), lambda j, i: (0, j))),
        compiler_params=pltpu.CompilerParams(dimension_semantics=("parallel", "arbitrary")),
    )(dpre, dpre, dpre, u, u, u, w8)


N_HB = 2 * SSM_GROUPS
P_DT, P_CS, P_E, P_W = 0, HP, 2 * HP, 3 * HP
DT_BLK = (U_SMALL + S_DT) // HP


def _tri(rev, transpose=False):
    rows = lax.broadcasted_iota(jnp.int32, (CHUNK, CHUNK), 0)
    cols = lax.broadcasted_iota(jnp.int32, (CHUNK, CHUNK), 1)
    if transpose:
        rows, cols = cols, rows
    return (cols >= rows) if rev else (cols <= rows)


def _ssd_prep(u, bias8, alog8):
    T = u.shape[0]
    nc = T // CHUNK

    def body(dt_ref, bias_ref, a_ref, cols_ref, rows_ref):
        lane = lax.broadcasted_iota(jnp.int32, (CHUNK, HP), 1)
        dt = _softplus(dt_ref[...] + bias_ref[0:1, :])
        da = dt * (-jnp.exp(a_ref[0:1, :]))
        cs_f = jnp.dot(jnp.where(_tri(False), 1.0, 0.0).astype(F32), da, precision=HI, preferred_element_type=F32)
        cs_b = jnp.dot(jnp.where(_tri(True), 1.0, 0.0).astype(F32), da, precision=HI, preferred_element_type=F32)
        cs = jnp.where(lane < SSM_HEADS, cs_f, cs_b)
        tot = jnp.where(lane[0:1] < SSM_HEADS, cs_f[CHUNK - 1:CHUNK, :], cs_b[0:1, :])
        e, w = jnp.exp(cs), jnp.exp(tot - cs)
        tot8 = jnp.broadcast_to(tot, (8, HP))
        etot8 = jnp.exp(tot8)
        for b in range(N_HB):
            down = (HP - HG * b) % HP

            def rolled(v):
                return pltpu.roll(v, down, 1) if down else v

            cols_ref[b, :, P_DT:P_DT + HP] = rolled(dt)
            cs_r = rolled(cs)
            cols_ref[b, :, P_CS:P_CS + HP] = cs_r
            cols_ref[b, :, P_E:P_E + HP] = rolled(e)
            cols_ref[b, :, P_W:P_W + HP] = rolled(w)
            rows_ref[b, 0, 0:8, :] = cs_r.T[0:8, :]
            r8 = lax.broadcasted_iota(jnp.int32, (8, HP), 0)
            rows_ref[b, 0, 8:16, :] = jnp.where(r8 == 0, rolled(tot8), jnp.where(r8 == 1, rolled(etot8), 0.0))

    vec = pl.BlockSpec((8, HP), lambda c: (0, 0))
    return pl.pallas_call(
        body, name="ssd_prep",
        out_shape=(jax.ShapeDtypeStruct((N_HB, T, 4 * HP), F32), jax.ShapeDtypeStruct((N_HB, nc, 16, HP), F32)),
        grid=(nc,), in_specs=[pl.BlockSpec((CHUNK, HP), lambda c: (c, DT_BLK)), vec, vec],
        out_specs=(pl.BlockSpec((N_HB, CHUNK, 4 * HP), lambda c: (0, c, 0)), pl.BlockSpec((N_HB, 1, 16, HP), lambda c: (0, c, 0, 0))),
    )(u, bias8, alog8)


def _ssd_specs(T, rev, bwd):
    nc = T // CHUNK
    fwd_order = (lambda c: nc - 1 - c) if rev else (lambda c: c)
    cm = (lambda c: fwd_order(nc - 1 - c)) if bwd else fwd_order
    hb0 = SSM_GROUPS if rev else 0
    xs = pl.BlockSpec((CHUNK, GW), lambda c, g: (cm(c), g))
    bs = pl.BlockSpec((CHUNK, D_STATE), lambda c, g: (cm(c), D_INNER // D_STATE + g))
    cs = pl.BlockSpec((CHUNK, D_STATE), lambda c, g: (cm(c), (D_INNER + SSM_GROUPS * D_STATE) // D_STATE + g))
    cols = pl.BlockSpec((1, CHUNK, 4 * HP), lambda c, g: (hb0 + g, cm(c), 0))
    rows = pl.BlockSpec((1, 1, 16, HP), lambda c, g: (hb0 + g, cm(c), 0, 0))
    return nc, cm, xs, bs, cs, cols, rows


def _head_lanes(to_heads):
    shape = (GW, HP) if to_heads else (HP, GW)
    wide = lax.broadcasted_iota(jnp.int32, shape, 0 if to_heads else 1)
    head = lax.broadcasted_iota(jnp.int32, shape, 1 if to_heads else 0)
    return jnp.where((wide >= PH * head) & (wide < PH * (head + 1)), 1.0, 0.0).astype(BF)


def _split_dot(v, m, terms):
    total, rest = None, v
    for _ in range(terms):
        piece = rest.astype(BF)
        part = jnp.dot(piece, m, preferred_element_type=F32)
        total = part if total is None else total + part
        rest = rest - piece.astype(F32)
    return total


def _spread_cols(cols_ref, rows_ref):
    spread = _head_lanes(False)
    dt_e = _split_dot(cols_ref[0, :, P_DT:P_DT + HP], spread, 3)
    e_e = _split_dot(cols_ref[0, :, P_E:P_E + HP], spread, 3)
    w_e = _split_dot(cols_ref[0, :, P_W:P_W + HP], spread, 3)
    etot_e = _split_dot(rows_ref[0, 0, 8:16, :], spread, 3)[1:2, :]
    return dt_e, e_e, w_e, etot_e


def _decay(cols_ref, rows_ref, hh, incl, transpose=False):
    col = cols_ref[0, :, P_CS + hh:P_CS + hh + 1]
    row = rows_ref[0, 0, hh:hh + 1, :]
    return jnp.where(incl, jnp.exp(row - col if transpose else col - row), 0.0)


def _ssd_fwd(act, cols, rows, *, rev, name):
    T = act.shape[0]
    nc, cm, xs_s, b_s, c_s, cols_s, rows_s = _ssd_specs(T, rev, False)

    def body(x_ref, b_ref, c_ref, cols_ref, rows_ref, y_ref, st_ref, state):
        c, g = pl.program_id(0), pl.program_id(1)

        @pl.when(c == 0)
        def _():
            state[g] = jnp.zeros((D_STATE, GW), F32)

        incl = _tri(rev)
        bm, cmat = b_ref[...].astype(BF), c_ref[...].astype(BF)
        bm_t = b_ref[...].T.astype(BF)
        cb = lax.dot_general(cmat, bm, NT, preferred_element_type=F32)
        dt_e, e_e, w_e, etot_e = _spread_cols(cols_ref, rows_ref)
        prev_all = state[g]
        st_ref[...] = prev_all
        xdt = x_ref[...] * dt_e
        xdt_b = xdt.astype(BF)
        yo_all = jnp.dot(cmat, prev_all.astype(BF), preferred_element_type=F32) * e_e
        state[g] = prev_all * etot_e + jnp.dot(bm_t, (xdt * w_e).astype(BF), preferred_element_type=F32)
        for hh in range(HG):
            hs = slice(PH * hh, PH * (hh + 1))
            lmat = _decay(cols_ref, rows_ref, hh, incl)
            yd = jnp.dot((cb * lmat).astype(BF), xdt_b[:, hs], preferred_element_type=F32)
            y_ref[:, hs] = yd + yo_all[:, hs]

    return pl.pallas_call(
        body, name=name,
        out_shape=(jax.ShapeDtypeStruct((T, D_INNER), F32), jax.ShapeDtypeStruct((nc * D_STATE, D_INNER), F32)),
        grid=(nc, SSM_GROUPS), in_specs=[xs_s, b_s, c_s, cols_s, rows_s], out_specs=(xs_s, xs_s),
        scratch_shapes=[pltpu.VMEM((SSM_GROUPS, D_STATE, GW), F32)],
        compiler_params=pltpu.CompilerParams(dimension_semantics=("arbitrary", "arbitrary")),
    )(act, act, act, cols, rows)


def _ssd_bwd(act, cols, rows, states, dy, *, rev, name):
    T = act.shape[0]
    nc, cm, xs_s, b_s, c_s, cols_s, rows_s = _ssd_specs(T, rev, True)

    def body(x_ref, b_ref, c_ref, cols_ref, rows_ref, st_ref, dy_ref, dx_ref, db_ref, dc_ref, dsel_ref, dtot_ref,
             dstate, dcs_cols, dcs_rows, dcb, dm_scr, dxdt_scr):
        c, g = pl.program_id(0), pl.program_id(1)

        @pl.when(c == 0)
        def _():
            dstate[g] = jnp.zeros((D_STATE, GW), F32)

        incl, incl_t = _tri(rev), _tri(rev, transpose=True)
        bm, cmat = b_ref[...].astype(BF), c_ref[...].astype(BF)
        cm_t = c_ref[...].T.astype(BF)
        cb = lax.dot_general(cmat, bm, NT, preferred_element_type=F32)
        cb_t = lax.dot_general(bm, cmat, NT, preferred_element_type=F32)
        prev_all, ds_all = st_ref[...], dstate[g]
        pb_all, dsb_all = prev_all.astype(BF), ds_all.astype(BF)
        cp_all = jnp.dot(cmat, pb_all, preferred_element_type=F32)
        bds_all = jnp.dot(bm, dsb_all, preferred_element_type=F32)
        dt_e, e_e, w_e, etot_e = _spread_cols(cols_ref, rows_ref)
        to_heads = _head_lanes(True)
        x, dy = x_ref[...], dy_ref[...]
        xdt = x * dt_e
        xdt_b, dy_b = xdt.astype(BF), dy.astype(BF)
        dye_b, xdw_b = (dy * e_e).astype(BF), (xdt * w_e).astype(BF)
        for hh in range(HG):
            hs = slice(PH * hh, PH * (hh + 1))
            mmat_t = cb_t * _decay(cols_ref, rows_ref, hh, incl_t, transpose=True)
            dm_scr[hh] = lax.dot_general(dy_b[:, hs], xdt_b[:, hs], NT, preferred_element_type=F32)
            dxdt_scr[:, hs] = jnp.dot(mmat_t.astype(BF), dy_b[:, hs], preferred_element_type=F32)
        bdsw = bds_all * w_e
        dxdt = dxdt_scr[...] + bdsw
        dx_ref[...] = dxdt * dt_e
        t = _split_dot(xdt * bdsw, to_heads, 2)
        dcs_state = _split_dot(dy * cp_all, to_heads, 2) * cols_ref[0, :, P_E:P_E + HP] - t
        dsel_ref[0, :, 0:HP] = _split_dot(dxdt * x, to_heads, 2)
        sp = _split_dot(jnp.broadcast_to(jnp.sum(ds_all * prev_all, axis=0, keepdims=True), (8, GW)), to_heads, 2)
        dtot_ref[0, 0] = jnp.sum(t, axis=0, keepdims=True) + sp * rows_ref[0, 0, 9:10, :]
        dstate[g] = ds_all * etot_e + jnp.dot(cm_t, dye_b, preferred_element_type=F32)
        dcs_cols[...] = jnp.zeros_like(dcs_cols)
        dcs_rows[...] = jnp.zeros_like(dcs_rows)
        dcb[...] = jnp.zeros_like(dcb)
        for hh in range(HG):
            lmat = _decay(cols_ref, rows_ref, hh, incl)
            dm = dm_scr[hh]
            qm = dm * (cb * lmat)
            dcs_cols[:, hh:hh + 1] = jnp.sum(qm, axis=1, keepdims=True)
            dcs_rows[hh:hh + 1, :] = jnp.sum(qm, axis=0, keepdims=True)
            dcb[...] += dm * lmat
        dcb_all = dcb[...]
        dsel_ref[0, :, HP:2 * HP] = dcs_state + dcs_cols[...] - dcs_rows[...].T
        dc_ref[...] = (lax.dot_general(dye_b, pb_all, NT, preferred_element_type=F32)
                       + jnp.dot(dcb_all.astype(BF), bm, preferred_element_type=F32))
        db_ref[...] = (lax.dot_general(xdw_b, dsb_all, NT, preferred_element_type=F32)
                       + jnp.dot(dcb_all.T.astype(BF), cmat, preferred_element_type=F32))

    bc_out = pl.BlockSpec((CHUNK, D_STATE), lambda c, g: (cm(c), g))
    return pl.pallas_call(
        body, name=name,
        out_shape=(jax.ShapeDtypeStruct((T, D_INNER), F32), jax.ShapeDtypeStruct((T, SSM_GROUPS * D_STATE), F32),
                   jax.ShapeDtypeStruct((T, SSM_GROUPS * D_STATE), F32), jax.ShapeDtypeStruct((SSM_GROUPS, T, 2 * HP), F32),
                   jax.ShapeDtypeStruct((SSM_GROUPS, nc, 8, HP), F32)),
        grid=(nc, SSM_GROUPS), in_specs=[xs_s, b_s, c_s, cols_s, rows_s, xs_s, xs_s],
        out_specs=(xs_s, bc_out, bc_out, pl.BlockSpec((1, CHUNK, 2 * HP), lambda c, g: (g, cm(c), 0)),
                   pl.BlockSpec((1, 1, 8, HP), lambda c, g: (g, cm(c), 0, 0))),
        scratch_shapes=[pltpu.VMEM((SSM_GROUPS, D_STATE, GW), F32), pltpu.VMEM((CHUNK, CHUNK), F32),
                        pltpu.VMEM((CHUNK, CHUNK), F32), pltpu.VMEM((CHUNK, CHUNK), F32),
                        pltpu.VMEM((HG, CHUNK, CHUNK), F32), pltpu.VMEM((CHUNK, GW), F32)],
        compiler_params=pltpu.CompilerParams(dimension_semantics=("arbitrary", "arbitrary")),
    )(act, act, act, cols, rows, states, dy)


def _ssd_prep_bwd(u, bias8, alog8, dsel_f, dtot_f, dsel_b, dtot_b):
    T = u.shape[0]
    nc = T // CHUNK

    def body(dt_ref, bias_ref, a_ref, sf_ref, tf_ref, sb_ref, tb_ref, ddt_ref, da_ref, dbias_ref):
        @pl.when(pl.program_id(0) == 0)
        def _():
            da_ref[...] = jnp.zeros_like(da_ref)
            dbias_ref[...] = jnp.zeros_like(dbias_ref)

        lane = lax.broadcasted_iota(jnp.int32, (CHUNK, HP), 1)
        pre = dt_ref[...] + bias_ref[0:1, :]
        dt = _softplus(pre)
        a = -jnp.exp(a_ref[0:1, :])
        ddt_x, dcs, dtot = jnp.zeros((CHUNK, HP), F32), jnp.zeros((CHUNK, HP), F32), jnp.zeros((8, HP), F32)
        for b in range(N_HB):
            s_ref, t_ref, g = (sf_ref, tf_ref, b) if b < SSM_GROUPS else (sb_ref, tb_ref, b - SSM_GROUPS)
            mine = (lane >= HG * b) & (lane < HG * (b + 1))

            def up(v):
                return pltpu.roll(v, HG * b, 1) if b else v

            ddt_x = ddt_x + jnp.where(mine, up(s_ref[g, :, 0:HP]), 0.0)
            dcs = dcs + jnp.where(mine, up(s_ref[g, :, HP:2 * HP]), 0.0)
            dtot = dtot + jnp.where(mine[0:8], up(t_ref[g, 0]), 0.0)
        tri_f = jnp.where(_tri(False, transpose=True), 1.0, 0.0).astype(F32)
        tri_b = jnp.where(_tri(True, transpose=True), 1.0, 0.0).astype(F32)
        dda = jnp.where(lane < SSM_HEADS, jnp.dot(tri_f, dcs, precision=HI, preferred_element_type=F32),
                        jnp.dot(tri_b, dcs, precision=HI, preferred_element_type=F32)) + dtot[0:1, :]
        dpre = (ddt_x + dda * a) * jax.nn.sigmoid(pre)
        ddt_ref[...] = jnp.where(lane < 2 * SSM_HEADS, dpre, 0.0)
        dbias_ref[...] += jnp.broadcast_to(jnp.sum(dpre, axis=0, keepdims=True), (8, HP))
        da_ref[...] += jnp.broadcast_to(jnp.sum(dda * dt, axis=0, keepdims=True) * a, (8, HP))

    vec = pl.BlockSpec((8, HP), lambda c: (0, 0))
    sel = pl.BlockSpec((SSM_GROUPS, CHUNK, 2 * HP), lambda c: (0, c, 0))
    tot = pl.BlockSpec((SSM_GROUPS, 1, 8, HP), lambda c: (0, c, 0, 0))
    tile = pl.BlockSpec((CHUNK, HP), lambda c: (c, 0))
    return pl.pallas_call(
        body, name="ssd_prep_bwd",
        out_shape=(jax.ShapeDtypeStruct((T, HP), F32), jax.ShapeDtypeStruct((8, HP), F32), jax.ShapeDtypeStruct((8, HP), F32)),
        grid=(nc,), in_specs=[pl.BlockSpec((CHUNK, HP), lambda c: (c, DT_BLK)), vec, vec, sel, tot, sel, tot],
        out_specs=(tile, vec, vec),
        compiler_params=pltpu.CompilerParams(dimension_semantics=("arbitrary",)),
    )(u, bias8, alog8, dsel_f, dtot_f, dsel_b, dtot_b)


def _ssm_combine_fwd(y_f, y_b, act, u, dskip, gain):
    T = y_f.shape[0]
    tt = _pick(T, 256)

    def body(yf_ref, yb_ref, x_ref, z_ref, ds_ref, g_ref, y_ref, m_ref):
        y = yf_ref[...] + yb_ref[...] + ds_ref[...] * x_ref[...]
        y2 = y * _silu(z_ref[...])
        r = lax.rsqrt(jnp.mean(y2 * y2, axis=-1, keepdims=True) + EPS)
        y_ref[...] = y
        m_ref[...] = (y2 * r * g_ref[...]).astype(m_ref.dtype)

    blk = pl.BlockSpec((tt, GW), lambda i, g: (i, g))
    vec = pl.BlockSpec((1, GW), lambda i, g: (0, g))
    return pl.pallas_call(
        body, name="ssm_combine_fwd",
        out_shape=(jax.ShapeDtypeStruct((T, D_INNER), F32), jax.ShapeDtypeStruct((T, D_INNER), BF)),
        grid=(T // tt, SSM_GROUPS), in_specs=[blk, blk, blk, blk, vec, vec], out_specs=(blk, blk),
    )(y_f, y_b, act, u, dskip, gain)


def _ssm_combine_bwd(dm, y, act, u, dskip, gain):
    T = y.shape[0]
    tt = _pick(T, 256)

    def body(dm_ref, y_ref, x_ref, z_ref, ds_ref, g_ref, dy_ref, dz_ref, dxs_ref, dg_ref, dsk_ref):
        z = z_ref[...]
        y = y_ref[...]
        x = x_ref[...]
        sz = _silu(z)
        y2 = y * sz
        r = lax.rsqrt(jnp.mean(y2 * y2, axis=-1, keepdims=True) + EPS)
        d = dm_ref[...]
        gd = d * g_ref[...]
        dy2 = r * gd - y2 * (r * r * r) * jnp.mean(gd * y2, axis=-1, keepdims=True)
        dy = dy2 * sz
        dy_ref[...] = dy
        dz_ref[...] = (dy2 * y * _dsilu(z)).astype(dz_ref.dtype)
        dxs_ref[...] = dy * ds_ref[...]

        @pl.when(pl.program_id(1) == 0)
        def _():
            dg_ref[...] = jnp.zeros_like(dg_ref)
            dsk_ref[...] = jnp.zeros_like(dsk_ref)

        dg_ref[...] += jnp.broadcast_to(jnp.sum(d * y2 * r, axis=0, keepdims=True), dg_ref.shape)
        lane_sum = jnp.broadcast_to(jnp.sum(dy * x, axis=0, keepdims=True), (8, GW))
        src = lax.broadcasted_iota(jnp.int32, (GW, HP), 0)
        head = lax.broadcasted_iota(jnp.int32, (GW, HP), 1)
        to_head = jnp.where((src >= PH * head) & (src < PH * (head + 1)), 1.0, 0.0).astype(F32)
        dsk_ref[...] += jnp.dot(lane_sum, to_head, precision=HI, preferred_element_type=F32)

    blk = pl.BlockSpec((tt, GW), lambda g, i: (i, g))
    vec = pl.BlockSpec((1, GW), lambda g, i: (0, g))
    acc = pl.BlockSpec((8, GW), lambda g, i: (0, g))
    return pl.pallas_call(
        body, name="ssm_combine_bwd",
        out_shape=(jax.ShapeDtypeStruct((T, D_INNER), F32), jax.ShapeDtypeStruct((T, D_INNER), BF),
                   jax.ShapeDtypeStruct((T, D_INNER), F32), jax.ShapeDtypeStruct((8, D_INNER), F32),
                   jax.ShapeDtypeStruct((8, SSM_GROUPS * HP), F32)),
        grid=(SSM_GROUPS, T // tt), in_specs=[blk, blk, blk, blk, vec, vec],
        out_specs=(blk, blk, blk, acc, pl.BlockSpec((8, HP), lambda g, i: (0, g))),
        compiler_params=pltpu.CompilerParams(dimension_semantics=("parallel", "arbitrary")),
    )(dm, y, act, u, dskip, gain)


def _loss_head(y, target):
    T, D = y.shape
    tt = _pick(T, 512)

    def body(y_ref, t_ref, dy_ref, dyb_ref, l_ref):
        e = y_ref[...] - t_ref[...]
        dy_ref[...] = e * (1.0 / D)
        dyb_ref[...] = (e * (1.0 / D)).astype(dyb_ref.dtype)

        @pl.when(pl.program_id(0) == 0)
        def _():
            l_ref[...] = jnp.zeros_like(l_ref)

        l_ref[...] += jnp.sum(e * e) * (0.5 / D)

    blk = pl.BlockSpec((tt, D), lambda i: (i, 0))
    return pl.pallas_call(
        body, name="loss_head",
        out_shape=(jax.ShapeDtypeStruct((T, D), F32), jax.ShapeDtypeStruct((T, D), BF), jax.ShapeDtypeStruct((8, 128), F32)),
        grid=(T // tt,), in_specs=[blk, blk], out_specs=(blk, blk, pl.BlockSpec((8, 128), lambda i: (0, 0))),
        compiler_params=pltpu.CompilerParams(dimension_semantics=("arbitrary",)),
    )(y, target)


def _adamw(w, g, m, v, *, name):
    R, C = w.shape
    cap = max(8, (1 << 18) // C)
    tr = R
    if R % 8 == 0:
        tr = 8
        for cand in range(8, min(R, cap) + 1, 8):
            if R % cand == 0:
                tr = cand

    def body(w_ref, g_ref, m_ref, v_ref, d_ref, nm_ref, nv_ref):
        gg = g_ref[...]
        nm = ADAM_B1 * m_ref[...] + (1.0 - ADAM_B1) * gg
        nv = ADAM_B2 * v_ref[...] + (1.0 - ADAM_B2) * jnp.square(gg)
        m_hat = nm / (1.0 - ADAM_B1 ** ADAM_STEP)
        v_hat = nv / (1.0 - ADAM_B2 ** ADAM_STEP)
        d_ref[...] = -ADAM_LR * (m_hat / (jnp.sqrt(v_hat) + ADAM_EPS) + ADAM_WD * w_ref[...])
        nm_ref[...] = nm
        nv_ref[...] = nv

    blk = pl.BlockSpec((tr, C), lambda i: (i, 0))
    return pl.pallas_call(
        body, name=name, out_shape=(jax.ShapeDtypeStruct((R, C), F32),) * 3, grid=(R // tr,),
        in_specs=[blk] * 4, out_specs=(blk,) * 3,
    )(w, g, m, v)


ANY = pl.BlockSpec(memory_space=pl.ANY)


def _chip_peers():
    x, y, c = lax.axis_index("x"), lax.axis_index("y"), lax.axis_index("c")
    return x, y, c, [(1 - x, y), (x, 1 - y), (1 - x, 1 - y)]


def _half_rows(c, rh):
    return pl.ds(pl.multiple_of(c * rh, 16), rh)


def _my_chip():
    return 2 * lax.axis_index("x") + lax.axis_index("y")


def _gather_chips(wb, wf):
    rh = wb.shape[0] // 2

    def body(wb_ref, wf_ref, ob_ref, of_ref, send_sems, recv_sems):
        x, y, c, peers = _chip_peers()
        me = 2 * x + y
        half, other = _half_rows(c, rh), _half_rows(1 - c, rh)

        def chip_copy(k, slot):
            px, py = peers[k]
            return pltpu.make_async_remote_copy(
                src_ref=wb_ref.at[half], dst_ref=ob_ref.at[slot, half], send_sem=send_sems.at[k], recv_sem=recv_sems.at[k],
                device_id=(px, py, c), device_id_type=MESH)

        def passed_on(k, slot, rows):
            return pltpu.make_async_remote_copy(
                src_ref=ob_ref.at[slot, rows], dst_ref=ob_ref.at[slot, rows], send_sem=send_sems.at[3 + k],
                recv_sem=recv_sems.at[3 + k], device_id=(x, y, 1 - c), device_id_type=MESH)

        def small_copy(k, slot):
            px, py = peers[k]
            return pltpu.make_async_remote_copy(
                src_ref=wf_ref, dst_ref=of_ref.at[slot], send_sem=send_sems.at[6 + k], recv_sem=recv_sems.at[6 + k],
                device_id=(px, py, c), device_id_type=MESH)

        sends = [chip_copy(k, me) for k in range(3)] + [small_copy(k, me) for k in range(3)]
        for cp in sends:
            cp.start()
        chip_of = [2 * px + py for px, py in peers]
        for k in range(3):
            chip_copy(k, chip_of[k]).wait_recv()
            cp = passed_on(k, chip_of[k], half)
            cp.start()
            sends.append(cp)
        for k in range(3):
            passed_on(k, chip_of[k], other).wait_recv()
            small_copy(k, chip_of[k]).wait_recv()
        for cp in sends:
            cp.wait_send()

    ob, of = pl.pallas_call(
        body, name="gather_weights",
        out_shape=(jax.ShapeDtypeStruct((4,) + wb.shape, wb.dtype), jax.ShapeDtypeStruct((4,) + wf.shape, wf.dtype)),
        in_specs=[ANY, ANY], out_specs=(ANY, ANY),
        scratch_shapes=[pltpu.SemaphoreType.DMA((9,)), pltpu.SemaphoreType.DMA((9,))],
    )(wb, wf)
    me = _my_chip()
    return lax.dynamic_update_slice(ob, wb[None], (me, 0, 0)), lax.dynamic_update_slice(of, wf[None], (me, 0, 0))


def _halves_to_sibling(gp):
    rh = gp.shape[1] // 2

    def body(gp_ref, o_ref, send_sem, recv_sem):
        x, y, c = lax.axis_index("x"), lax.axis_index("y"), lax.axis_index("c")
        cp = pltpu.make_async_remote_copy(src_ref=gp_ref.at[:, _half_rows(1 - c, rh), :], dst_ref=o_ref, send_sem=send_sem,
                                          recv_sem=recv_sem, device_id=(x, y, 1 - c), device_id_type=MESH)
        cp.start()
        cp.wait()

    return pl.pallas_call(
        body, name="halves_to_sibling", out_shape=jax.ShapeDtypeStruct((gp.shape[0], rh, gp.shape[2]), gp.dtype),
        in_specs=[ANY], out_specs=ANY, scratch_shapes=[pltpu.SemaphoreType.DMA, pltpu.SemaphoreType.DMA],
    )(gp)


def _row_tile(rows, cap=1024):
    tr = 16
    for cand in range(16, cap + 1, 16):
        if rows % cand == 0:
            tr = cand
    return tr


def _add_halves(gp, sib, core):
    n, rh, C = sib.shape
    tr = _row_tile(rh)
    nt = rh // tr

    def body(c_ref, g_ref, s_ref, o_ref):
        o_ref[...] = (g_ref[...].astype(F32) + s_ref[...].astype(F32)).astype(o_ref.dtype)

    blk = pl.BlockSpec((1, tr, C), lambda j, i, c: (j, i, 0))
    return pl.pallas_call(
        body, name="add_halves", out_shape=jax.ShapeDtypeStruct(sib.shape, sib.dtype),
        grid_spec=pltpu.PrefetchScalarGridSpec(
            num_scalar_prefetch=1, grid=(n, nt),
            in_specs=[pl.BlockSpec((1, tr, C), lambda j, i, c: (j, c[0] * nt + i, 0)), blk], out_specs=blk),
    )(core, gp, sib)


def _join_halves(mine):
    rh = mine.shape[0]

    def body(m_ref, o_ref, send_sem, recv_sem):
        x, y, c = lax.axis_index("x"), lax.axis_index("y"), lax.axis_index("c")
        half, other = _half_rows(c, rh), _half_rows(1 - c, rh)

        def copy(rows):
            return pltpu.make_async_remote_copy(src_ref=m_ref, dst_ref=o_ref.at[rows], send_sem=send_sem, recv_sem=recv_sem,
                                                device_id=(x, y, 1 - c), device_id_type=MESH)

        send = copy(half)
        send.start()
        copy(other).wait_recv()
        send.wait_send()

    out = pl.pallas_call(
        body, name="join_halves", out_shape=jax.ShapeDtypeStruct((2 * rh, mine.shape[1]), mine.dtype),
        in_specs=[ANY], out_specs=ANY, scratch_shapes=[pltpu.SemaphoreType.DMA, pltpu.SemaphoreType.DMA],
    )(mine)
    return lax.dynamic_update_slice(out, mine, (lax.axis_index("c") * rh, 0))


def _exchange_chips(gp):
    def body(gp_ref, out_ref, send_sems, recv_sems):
        x, y, c, peers = _chip_peers()
        me = 2 * x + y

        def copies(sending):
            out = []
            for k, (px, py) in enumerate(peers):
                p = 2 * px + py
                out.append(pltpu.make_async_remote_copy(
                    src_ref=gp_ref.at[p], dst_ref=out_ref.at[me if sending else p],
                    send_sem=send_sems.at[k], recv_sem=recv_sems.at[k], device_id=(px, py, c), device_id_type=MESH))
            return out

        sends = copies(True)
        for cp in sends:
            cp.start()
        for cp in copies(False):
            cp.wait_recv()
        for cp in sends:
            cp.wait_send()

    out = pl.pallas_call(
        body, name="exchange_grads", out_shape=jax.ShapeDtypeStruct(gp.shape, gp.dtype),
        in_specs=[ANY], out_specs=ANY,
        scratch_shapes=[pltpu.SemaphoreType.DMA((3,)), pltpu.SemaphoreType.DMA((3,))],
    )(gp)
    me = _my_chip()
    return lax.dynamic_update_slice(out, lax.dynamic_slice_in_dim(gp, me, 1, axis=0), (me, 0, 0))


def _sum_slots(r4):
    _, R, C = r4.shape
    tr = _row_tile(R)

    def body(r_ref, o_ref):
        acc = r_ref[0].astype(F32)
        for s in range(1, 4):
            acc = acc + r_ref[s].astype(F32)
        o_ref[...] = acc

    return pl.pallas_call(
        body, name="sum_slots", out_shape=jax.ShapeDtypeStruct((R, C), F32), grid=(R // tr,),
        in_specs=[pl.BlockSpec((4, tr, C), lambda i: (0, i, 0))], out_specs=pl.BlockSpec((tr, C), lambda i: (i, 0)),
    )(r4)


N_DEV = 8


def _allreduce_small(p):
    rs = p.shape[0]

    def body(x_ref, sum_ref, all_ref, send_sems, recv_sems, local_sem):
        x, y, c = lax.axis_index("x"), lax.axis_index("y"), lax.axis_index("c")
        me, sibling = (x, y, c), (x, y, 1 - c)
        chips = [(1 - x, y), (x, 1 - y), (1 - x, 1 - y)]

        def rows(px, py, pc):
            return all_ref.at[pl.ds((4 * px + 2 * py + pc) * rs, rs), :]

        def copy(k, block, to, src=None):
            return pltpu.make_async_remote_copy(
                src_ref=rows(*block) if src is None else src, dst_ref=rows(*block),
                send_sem=send_sems.at[k], recv_sem=recv_sems.at[k], device_id=to, device_id_type=MESH)

        mine = pltpu.make_async_copy(x_ref, rows(*me), local_sem)
        mine.start()
        first = [copy(0, me, sibling, src=x_ref)]
        first += [copy(1 + j, me, (*chip, c), src=x_ref) for j, chip in enumerate(chips)]
        for cp in first:
            cp.start()
        passed = [copy(4 + j, (*chip, c), sibling) for j, chip in enumerate(chips)]
        for j, chip in enumerate(chips):
            copy(1 + j, (*chip, c), me).wait_recv()
            passed[j].start()
        copy(0, sibling, me).wait_recv()
        for j, chip in enumerate(chips):
            copy(4 + j, (*chip, 1 - c), me).wait_recv()
        for cp in first + passed:
            cp.wait_send()
        mine.wait()
        acc = all_ref[0:rs, :]
        for d in range(1, N_DEV):
            acc = acc + all_ref[d * rs:(d + 1) * rs, :]
        sum_ref[...] = acc

    vmem = pl.BlockSpec(memory_space=pltpu.VMEM)
    return pl.pallas_call(
        body, name="allreduce_small", out_shape=jax.ShapeDtypeStruct((rs, 128), F32),
        in_specs=[vmem], out_specs=vmem,
        scratch_shapes=[pltpu.VMEM((N_DEV * rs, 128), F32), pltpu.SemaphoreType.DMA((7,)), pltpu.SemaphoreType.DMA((7,)),
                        pltpu.SemaphoreType.DMA],
    )(p)


WEIGHTS = ('ffn1_norm', 'ffn1_w_gate', 'ffn1_w_up', 'ffn1_w_down', 'mix_norm', 'w_in', 'q_a_norm', 'w_q_b',
           'kv_a_norm', 'w_kv_b', 'q_head_norm', 'k_head_norm', 'conv_w', 'conv_b', 'a_log_fwd', 'a_log_bwd',
           'dt_bias_fwd', 'dt_bias_bwd', 'd_skip', 'ssm_norm', 'w_attn_branch', 'w_ssm_branch', 'w_out',
           'ffn2_norm', 'ffn2_w_gate', 'ffn2_w_up', 'ffn2_w_down')
PACKED = (('ffn1_w_gate', (D_MODEL, D_FF), 1), ('ffn1_w_up', (D_MODEL, D_FF), 1), ('ffn1_w_down', (D_FF, D_MODEL), 0),
          ('w_in', (D_MODEL, sum(IN_SPLITS)), 1), ('w_q_b', (Q_LORA, N_HEADS * QK_HEAD), 1),
          ('w_kv_b', (KV_LORA, N_HEADS * (QK_NOPE + V_HEAD)), 1),
          ('w_attn_branch', (N_HEADS * V_HEAD, D_MODEL), 0), ('w_ssm_branch', (D_INNER, D_MODEL), 0),
          ('w_out', (D_MODEL, D_MODEL), 0),
          ('ffn2_w_gate', (D_MODEL, D_FF), 1), ('ffn2_w_up', (D_MODEL, D_FF), 1), ('ffn2_w_down', (D_FF, D_MODEL), 0))
PACK_W = 1024
N_CHIPS = 4
SMALL = (('ffn1_norm', 1024), ('mix_norm', 1024), ('q_a_norm', 384), ('kv_a_norm', 256), ('q_head_norm', 96),
         ('k_head_norm', 96), ('conv_b', 3072), ('a_log_fwd', 32), ('a_log_bwd', 32), ('dt_bias_fwd', 32),
         ('dt_bias_bwd', 32), ('d_skip', 32), ('ssm_norm', 2048), ('ffn2_norm', 1024),
         ('conv_w', CONV_WIDTH * XBC_DIM), ('loss', 1))


TRANSPOSED = ('ffn1_w_gate', 'ffn1_w_up', 'w_in', 'ffn2_w_gate', 'ffn2_w_up')


def _stored(name, a):
    return a.T if name in TRANSPOSED else a


def _shard_shape(name, shape, axis):
    sh = tuple(s // N_CHIPS if a == axis else s for a, s in enumerate(shape))
    return sh[::-1] if name in TRANSPOSED else sh


def _by_rows(name, axis):
    return name in TRANSPOSED or axis == 0


def _pack_layout():
    out, r = {}, 0
    for name, shape, axis in PACKED:
        n = math.prod(shape) // N_CHIPS // PACK_W
        out[name] = (r, n)
        r += n
    return out, -(-r // 32) * 32


def _pack(shards):
    layout, rows = _pack_layout()
    parts = [shards[name].reshape(-1, PACK_W) for name, _, _ in PACKED]
    parts.append(jnp.zeros((rows - sum(p.shape[0] for p in parts), PACK_W), parts[0].dtype))
    return jnp.concatenate(parts, axis=0)


def _unpack(packed):
    layout, _ = _pack_layout()
    return {name: packed[layout[name][0]:layout[name][0] + layout[name][1]].reshape(_shard_shape(name, shape, axis))
            for name, shape, axis in PACKED}


def _full_from_slots(slots):
    layout, _ = _pack_layout()
    out = {}
    for name, shape, axis in PACKED:
        r, n = layout[name]
        if _by_rows(name, axis):
            out[name] = slots[:, r:r + n].reshape(N_CHIPS * n, PACK_W)
        else:
            sh = _shard_shape(name, shape, axis)
            out[name] = jnp.concatenate([slots[j, r:r + n].reshape(sh) for j in range(N_CHIPS)], axis=axis)
    return out


def _slots_from_full(full):
    layout, rows = _pack_layout()
    parts = []
    for name, shape, axis in PACKED:
        r, n = layout[name]
        if _by_rows(name, axis):
            parts.append(full[name].reshape(N_CHIPS, n, PACK_W))
        else:
            size = shape[axis] // N_CHIPS
            parts.append(jnp.stack([lax.slice_in_dim(full[name], j * size, (j + 1) * size, axis=axis).reshape(n, PACK_W)
                                    for j in range(N_CHIPS)]))
    parts.append(jnp.zeros((N_CHIPS, rows - sum(p.shape[1] for p in parts), PACK_W), parts[0].dtype))
    return jnp.concatenate(parts, axis=1)


def _pack_small(vals):
    parts = []
    for name, n in SMALL:
        pad = -(-n // 128) * 128 - n
        parts.append(jnp.pad(vals[name].reshape(-1).astype(F32), (0, pad)).reshape(-1, 128))
    rows = sum(p.shape[0] for p in parts)
    parts.append(jnp.zeros((-(-rows // 8) * 8 - rows, 128), F32))
    return jnp.concatenate(parts, axis=0)


def _unpack_small(packed):
    out, r = {}, 0
    for name, n in SMALL:
        k = -(-n // 128)
        out[name] = packed[r:r + k].reshape(-1)[:n]
        r += k
    return out


def _pad_heads(w, axis, per_head, lo, hi):
    shape = w.shape
    w = w.reshape(shape[:axis] + (N_HEADS, per_head) + shape[axis + 1:])
    w = lax.slice_in_dim(w, lo, hi, axis=axis + 1)
    pad = [(0, 0)] * w.ndim
    pad[axis + 1] = (0, HP - (hi - lo))
    w = jnp.pad(w, pad)
    return w.reshape(shape[:axis] + (N_HEADS * HP,) + shape[axis + 1:])


def _unpad_heads(w, axis, keep):
    shape = w.shape
    w = w.reshape(shape[:axis] + (N_HEADS, HP) + shape[axis + 1:])
    return lax.slice_in_dim(w, 0, keep, axis=axis + 1)


def _pad_w_in(wt):
    o = [0]
    for s in IN_SPLITS:
        o.append(o[-1] + s)
    cq, ckv, kpe, z, xbc, dtf, dtb, ga, gb = [wt[o[i]:o[i + 1]] for i in range(len(IN_SPLITS))]
    kpe_pad = jnp.pad(kpe, ((QK_NOPE, HP - QK_HEAD), (0, 0)))
    dt_pad = jnp.pad(jnp.concatenate([dtf, dtb], axis=0), ((0, HP - 2 * SSM_HEADS), (0, 0)))
    return jnp.concatenate([z, ga, gb, xbc, cq, ckv, kpe_pad, dt_pad], axis=0)


def _unpad_w_in(gt):
    z, ga, gb, xbc = gt[U_Z:U_GA], gt[U_GA:U_GB], gt[U_GB:U_XBC], gt[U_XBC:U_SMALL]
    s = gt[U_SMALL:]
    cq, ckv = s[S_CQ:S_CKV], s[S_CKV:S_KPE]
    kpe = s[S_KPE + QK_NOPE:S_KPE + QK_HEAD]
    dtf, dtb = s[S_DT:S_DT + SSM_HEADS], s[S_DT + SSM_HEADS:S_DT + 2 * SSM_HEADS]
    return jnp.concatenate([cq, ckv, kpe, z, xbc, dtf, dtb, ga, gb], axis=0)


def _lanes128(parts):
    row = jnp.concatenate([p.reshape(-1) for p in parts])
    return jnp.pad(row, (0, HP - row.shape[0])).reshape(1, HP)


FF_TILE = D_FF // 2
WGRAD = BF


def _ffn_fwd(x, g, wg_t, wu_t, wd, tag):
    h = _rms_fwd(x, g, name=tag + "_norm")
    gate, up, act = _mm([h], [wg_t, wu_t], name=tag + "_up", tb=True, out_dtypes=(F32, F32, BF), tm=512, tn=FF_TILE,
                        epilogue=lambda a, b: (a, b, _silu(a) * b))
    out = _mm([act], [wd], name=tag + "_down", extras=[x], epilogue=lambda acc, r: (r + 0.5 * acc,))
    return out, (h, gate, up, act)


def _ffn_bwd(dout, dout_bf, x, g, wg_t, wu_t, wd, saved, tag):
    h, gate, up, act = saved
    dgate, dup = _mm([dout_bf], [wd], name=tag + "_down_dx", tb=True, extras=[gate, up], out_dtypes=(BF, BF),
                     tm=512, tn=FF_TILE, epilogue=lambda acc, a, b: (0.5 * acc * b * _dsilu(a), 0.5 * acc * _silu(a)))
    dwd = _mm([act], [dout_bf], name=tag + "_down_dw", ta=True, tm=FF_TILE, tk=1024, out_dtypes=(WGRAD,),
              epilogue=lambda acc: (0.5 * acc,))
    dwg_t, dwu_t = _mm([dgate, dup], [h, h], name=tag + "_up_dw", ta=True, separate=True, out_dtypes=(WGRAD, WGRAD),
                       tm=FF_TILE, tk=1024)
    dh = _mm([dgate, dup], [wg_t, wu_t], name=tag + "_up_dx")
    dx, dx_bf, dg = _rms_bwd(dh, x, g, name=tag + "_norm_bwd", add=dout, out_dtypes=(F32, BF))
    return dx, dx_bf, dg, dwg_t, dwu_t, dwd


KPE_BLK = (U_SMALL + S_KPE) // HP
SMALL_BLK = U_SMALL // SMALL_W


def _local_step(x, pos_col, target, W, P):
    T = x.shape[0]
    sig = jax.nn.sigmoid
    x1, ffn1 = _ffn_fwd(x, P["ffn1_norm"], W["wg1"], W["wu1"], W["wd1"], "ffn1")
    h = _rms_fwd(x1, P["mix_norm"], name="mix_norm")
    u = _mm([h], [W["w_in"]], name="in_proj", tb=True, tn=1152)
    cqn = _rms_fwd(u, P["q_a_norm"], name="q_a_norm", blk_w=SMALL_W, blk_idx=SMALL_BLK, off=S_CQ, width=Q_LORA)
    ckvn = _rms_fwd(u, P["kv_a_norm"], name="kv_a_norm", blk_w=SMALL_W, blk_idx=SMALL_BLK, off=S_CKV, width=KV_LORA)
    q_raw = _mm([cqn], [W["wq"]], name="q_proj")
    k_raw, v = _mm([ckvn], [W["wk"], W["wv"]], name="kv_proj", out_dtypes=(F32, BF))
    rc, rs = _rope_tables(pos_col, P["freq"])
    q = _qk_prep_fwd(q_raw, None, P["q_head_norm"], rc, rs, name="q_prep", out_scale=Q_SCALE)
    k = _qk_prep_fwd(k_raw, u, P["k_head_norm"], rc, rs, name="k_prep", kpe_blk=KPE_BLK)
    o, lse = _attn_fwd(q, k, v)
    pre, act = _conv_fwd(u, P["conv_w8"], P["conv_b"])
    scan_cols, scan_rows = _ssd_prep(u, P["dt_bias8"], P["a_log8"])
    y_f, st_f = _ssd_fwd(act, scan_cols, scan_rows, rev=False, name="ssd_fwd_f")
    y_b, st_b = _ssd_fwd(act, scan_cols, scan_rows, rev=True, name="ssd_fwd_b")
    ysum, m = _ssm_combine_fwd(y_f, y_b, act, u, P["d_skip_lanes"], P["ssm_norm"])
    ab = _mm([o], [W["pa"]], name="attn_branch")
    mb, merged = _mm([m], [W["pb"]], name="ssm_branch", extras=[ab, u, u], extra_offs=(0, U_GA, U_GB), out_dtypes=(F32, BF),
                     epilogue=lambda acc, a, ga, gb: (acc, sig(ga) * a + sig(gb) * acc))
    x2 = _mm([merged], [W["wo"]], name="out_proj", extras=[x1], epilogue=lambda acc, r: (r + acc,))
    y, ffn2 = _ffn_fwd(x2, P["ffn2_norm"], W["wg2"], W["wu2"], W["wd2"], "ffn2")
    dy, dy_bf, loss = _loss_head(y, target)
    dx2, dx2_bf, dg_ffn2, dwg2, dwu2, dwd2 = _ffn_bwd(dy, dy_bf, x2, P["ffn2_norm"], W["wg2"], W["wu2"], W["wd2"], ffn2,
                                                      "ffn2")

    def gate_bwd(dmrg, a, b, ga, gb):
        sa, sb = sig(ga), sig(gb)
        return dmrg * sa, dmrg * sb, dmrg * a * sa * (1.0 - sa), dmrg * b * sb * (1.0 - sb)

    dab, dmb, dga, dgb = _mm([dx2_bf], [W["wo"]], name="out_proj_dx", tb=True, extras=[ab, mb, u, u],
                             extra_offs=(0, 0, U_GA, U_GB), out_dtypes=(BF,) * 4, epilogue=gate_bwd)
    dwo = _mm([merged], [dx2_bf], name="out_proj_dw", ta=True, out_dtypes=(WGRAD,))
    dpa = _mm([o], [dab], name="attn_branch_dw", ta=True, out_dtypes=(WGRAD,))
    do = _mm([dab], [W["pa"]], name="attn_branch_dx", tb=True)
    dpb = _mm([m], [dmb], name="ssm_branch_dw", ta=True, out_dtypes=(WGRAD,))
    dm = _mm([dmb], [W["pb"]], name="ssm_branch_dx", tb=True)
    dyssd, dz, dxs_skip, dg_ssm, dskip = _ssm_combine_bwd(dm, ysum, act, u, P["d_skip_lanes"], P["ssm_norm"])
    dxs_f, db_f, dc_f, dsel_f, dtot_f = _ssd_bwd(act, scan_cols, scan_rows, st_f, dyssd, rev=False, name="ssd_bwd_f")
    dxs_b, db_b, dc_b, dsel_b, dtot_b = _ssd_bwd(act, scan_cols, scan_rows, st_b, dyssd, rev=True, name="ssd_bwd_b")
    ddt, dalog, dbias = _ssd_prep_bwd(u, P["dt_bias8"], P["a_log8"], dsel_f, dtot_f, dsel_b, dtot_b)
    dxbc, dconv = [], []
    for tag, col0, parts in (("x", 0, [dxs_f, dxs_b, dxs_skip]), ("b", D_INNER, [db_f, db_b]),
                             ("c", D_INNER + SSM_GROUPS * D_STATE, [dc_f, dc_b])):
        dpre = _conv_dpre(parts, pre, col0, name="conv_dpre_" + tag)
        dxp, dwp = _conv_bwd(dpre, u, P["conv_w8"], col0, name="conv_bwd_" + tag)
        dxbc.append(dxp)
        dconv.append(dwp)
    dconv = jnp.concatenate(dconv, axis=1)
    dq, dk, dv = _attn_bwd(q, k, v, do, o, lse)
    dq_raw, dg_qh = _qk_prep_bwd(dq, q_raw, None, P["q_head_norm"], rc, rs, name="q_prep_bwd", in_scale=ATTN_SCALE)
    dk_raw, dg_kh, dkpe = _qk_prep_bwd(dk, k_raw, u, P["k_head_norm"], rc, rs, name="k_prep_bwd", kpe_blk=KPE_BLK,
                                       in_scale=1.0 / LOG2E)
    dwq = _mm([cqn], [dq_raw], name="q_proj_dw", ta=True, out_dtypes=(WGRAD,))
    dcqn = _mm([dq_raw], [W["wq"]], name="q_proj_dx", tb=True)
    dwk, dwv = _mm([ckvn], [dk_raw, dv], name="kv_proj_dw", ta=True, out_dtypes=(WGRAD, WGRAD))
    dckvn = _mm([dk_raw, dv], [W["wk"], W["wv"]], name="kv_proj_dx", tb=True)
    dcq, dg_qa = _rms_bwd(dcqn, u, P["q_a_norm"], name="q_a_norm_bwd", blk_w=SMALL_W, blk_idx=SMALL_BLK, off=S_CQ,
                          width=Q_LORA, out_dtypes=(BF,))
    dckv, dg_kva = _rms_bwd(dckvn, u, P["kv_a_norm"], name="kv_a_norm_bwd", blk_w=SMALL_W, blk_idx=SMALL_BLK,
                            off=S_CKV, width=KV_LORA, out_dtypes=(BF,))
    du = jnp.concatenate([dz, dga, dgb] + dxbc + [dcq, dckv, dkpe.astype(BF), ddt.astype(BF)], axis=1)
    dw_in = _mm([du], [h], name="in_proj_dw", ta=True, tm=1152, out_dtypes=(WGRAD,))
    dh = _mm([du], [W["w_in"]], name="in_proj_dx")
    dx1, dx1_bf, dg_mix = _rms_bwd(dh, x1, P["mix_norm"], name="mix_norm_bwd", add=dx2, out_dtypes=(F32, BF))
    dx, _, dg_ffn1, dwg1, dwu1, dwd1 = _ffn_bwd(dx1, dx1_bf, x, P["ffn1_norm"], W["wg1"], W["wu1"], W["wd1"], ffn1, "ffn1")
    dW = dict(wg1=dwg1, wu1=dwu1, wd1=dwd1, w_in=dw_in, wq=dwq, wk=dwk, wv=dwv, pa=dpa, pb=dpb, wo=dwo,
              wg2=dwg2, wu2=dwu2, wd2=dwd2)
    dP = dict(ffn1_norm=dg_ffn1[0], mix_norm=dg_mix[0], q_a_norm=dg_qa[0], kv_a_norm=dg_kva[0],
              q_head_norm=dg_qh[0, :QK_HEAD], k_head_norm=dg_kh[0, :QK_HEAD], conv_b=dconv[CONV_WIDTH],
              a_log_fwd=dalog[0, :SSM_HEADS], a_log_bwd=dalog[0, SSM_HEADS:2 * SSM_HEADS],
              dt_bias_fwd=dbias[0, :SSM_HEADS], dt_bias_bwd=dbias[0, SSM_HEADS:2 * SSM_HEADS],
              d_skip=dskip[0].reshape(SSM_GROUPS, HP)[:, :HG], ssm_norm=dg_ssm[0], ffn2_norm=dg_ffn2[0],
              conv_w=dconv[:CONV_WIDTH], loss=loss[0, 0])
    return dx, dW, dP


def _prepare(w, conv_w_full):
    kvb = w["w_kv_b"]
    W = dict(wg1=w["ffn1_w_gate"], wu1=w["ffn1_w_up"], wd1=w["ffn1_w_down"], w_in=_pad_w_in(w["w_in"]),
             wq=_pad_heads(w["w_q_b"], 1, QK_HEAD, 0, QK_HEAD),
             wk=_pad_heads(kvb, 1, QK_NOPE + V_HEAD, 0, QK_NOPE),
             wv=_pad_heads(kvb, 1, QK_NOPE + V_HEAD, QK_NOPE, QK_NOPE + V_HEAD),
             pa=_pad_heads(w["w_attn_branch"], 0, V_HEAD, 0, V_HEAD), pb=w["w_ssm_branch"], wo=w["w_out"],
             wg2=w["ffn2_w_gate"], wu2=w["ffn2_w_up"], wd2=w["ffn2_w_down"])
    inv_freq = [1.0 / (ROPE_BASE ** (j / QK_ROPE)) for j in range(0, QK_ROPE, 2)]
    freq = [0.0] * QK_NOPE + inv_freq + inv_freq + [0.0] * (HP - QK_HEAD)
    P = {n: w[n] for n in ("ffn1_norm", "mix_norm", "q_a_norm", "kv_a_norm", "ssm_norm", "ffn2_norm", "conv_b")}
    P.update(q_head_norm=_lanes128([w["q_head_norm"]]), k_head_norm=_lanes128([w["k_head_norm"]]),
             conv_w8=jnp.pad(conv_w_full, ((0, 8 - CONV_WIDTH), (0, 0))),
             dt_bias8=jnp.broadcast_to(_lanes128([w["dt_bias_fwd"], w["dt_bias_bwd"]]), (8, HP)),
             a_log8=jnp.broadcast_to(_lanes128([w["a_log_fwd"], w["a_log_bwd"]]), (8, HP)),
             d_skip_lanes=jnp.repeat(w["d_skip"].reshape(-1), PH).reshape(1, D_INNER),
             freq=jnp.asarray(freq, F32).reshape(1, HP))
    return W, P


def _unprepare(dW):
    dkvb = jnp.concatenate([_unpad_heads(dW["wk"], 1, QK_NOPE), _unpad_heads(dW["wv"], 1, V_HEAD)], axis=2)
    return dict(ffn1_w_gate=dW["wg1"], ffn1_w_up=dW["wu1"], ffn1_w_down=dW["wd1"], w_in=_unpad_w_in(dW["w_in"]),
                w_q_b=_unpad_heads(dW["wq"], 1, QK_HEAD).reshape(Q_LORA, N_HEADS * QK_HEAD),
                w_kv_b=dkvb.reshape(KV_LORA, N_HEADS * (QK_NOPE + V_HEAD)),
                w_attn_branch=_unpad_heads(dW["pa"], 0, V_HEAD).reshape(N_HEADS * V_HEAD, D_MODEL),
                w_ssm_branch=dW["pb"], w_out=dW["wo"],
                ffn2_w_gate=dW["wg2"], ffn2_w_up=dW["wu2"], ffn2_w_down=dW["wd2"])


def kernel(x, positions, ffn1_norm, ffn1_w_gate, ffn1_w_up, ffn1_w_down, mix_norm, w_in, q_a_norm, w_q_b, kv_a_norm, w_kv_b, q_head_norm, k_head_norm, conv_w, conv_b, a_log_fwd, a_log_bwd, dt_bias_fwd, dt_bias_bwd, d_skip, ssm_norm, w_attn_branch, w_ssm_branch, w_out, ffn2_norm, ffn2_w_gate, ffn2_w_up, ffn2_w_down, loss_target, m_ffn1_norm, m_ffn1_w_gate, m_ffn1_w_up, m_ffn1_w_down, m_mix_norm, m_w_in, m_q_a_norm, m_w_q_b, m_kv_a_norm, m_w_kv_b, m_q_head_norm, m_k_head_norm, m_conv_w, m_conv_b, m_a_log_fwd, m_a_log_bwd, m_dt_bias_fwd, m_dt_bias_bwd, m_d_skip, m_ssm_norm, m_w_attn_branch, m_w_ssm_branch, m_w_out, m_ffn2_norm, m_ffn2_w_gate, m_ffn2_w_up, m_ffn2_w_down, v_ffn1_norm, v_ffn1_w_gate, v_ffn1_w_up, v_ffn1_w_down, v_mix_norm, v_w_in, v_q_a_norm, v_w_q_b, v_kv_a_norm, v_w_kv_b, v_q_head_norm, v_k_head_norm, v_conv_w, v_conv_b, v_a_log_fwd, v_a_log_bwd, v_dt_bias_fwd, v_dt_bias_bwd, v_d_skip, v_ssm_norm, v_w_attn_branch, v_w_ssm_branch, v_w_out, v_ffn2_norm, v_ffn2_w_gate, v_ffn2_w_up, v_ffn2_w_down):
    given = dict(locals())
    T = x.shape[1]
    packed_names = [name for name, _, _ in PACKED]

    def two_d(a):
        return a.reshape(a.shape[1], -1) if a.ndim > 2 else a

    def kept(n, a):
        return _stored(n, two_d(a))

    w_loc = {n: kept(n, given[n]) for n in WEIGHTS}
    wb = _pack({n: w_loc[n].astype(BF) for n in packed_names})
    wf = jnp.pad(w_loc["conv_w"], ((0, 8 - CONV_WIDTH), (0, 0)))
    gb, gf = _gather_chips(wb, wf)
    full = _full_from_slots(gb)
    conv_w_full = jnp.concatenate([gf[j, :CONV_WIDTH] for j in range(N_CHIPS)], axis=1)
    full.update({n: w_loc[n] for n in WEIGHTS if n not in full and n != "conv_w"})
    W, P = _prepare(full, conv_w_full)
    dx, dW, dP = _local_step(x.reshape(T, D_MODEL), positions.reshape(T, 1).astype(F32), loss_target.reshape(T, D_MODEL), W, P)
    gp = _slots_from_full(_unprepare(dW))
    core = lax.axis_index("c").astype(jnp.int32).reshape(1)
    both_cores = _add_halves(gp, _halves_to_sibling(gp), core)
    grads = _unpack(_join_halves(_sum_slots(_exchange_chips(both_cores))))
    small = _unpack_small(_allreduce_small(_pack_small(dP)))
    grads.update({n: small[n].reshape(1, -1) for n, _ in SMALL if n not in ("conv_w", "loss")})
    grads["conv_w"] = lax.dynamic_slice_in_dim(small["conv_w"].reshape(CONV_WIDTH, XBC_DIM), _my_chip() * (XBC_DIM // N_CHIPS),
                                               XBC_DIM // N_CHIPS, axis=1)
    out_g, out_d, out_m, out_v = [], [], [], []
    for n in WEIGHTS:
        shape = given[n].shape
        delta, new_m, new_v = _adamw(w_loc[n], grads[n], kept(n, given["m_" + n]), kept(n, given["v_" + n]), name="adamw_" + n)
        for outs, a in ((out_g, grads[n]), (out_d, delta), (out_m, new_m), (out_v, new_v)):
            outs.append(_stored(n, a).reshape(shape))
    return (small["loss"].reshape(()), dx.reshape(x.shape), *out_g, *out_d, *out_m, *out_v)
```

```python
import functools
import math

import jax
import jax.numpy as jnp
from jax import lax
from jax.experimental import pallas as pl
from jax.experimental.pallas import tpu as pltpu

BF = jnp.bfloat16
F32 = jnp.float32
HI = lax.Precision.HIGHEST
MESH = pl.DeviceIdType.MESH

D_MODEL = 1024
D_FF = 2816
EPS = 1e-6
N_HEADS = 16
QK_NOPE = 64
QK_ROPE = 32
QK_HEAD = 96
V_HEAD = 64
Q_LORA = 384
KV_LORA = 256
ROPE_BASE = 10000.0
D_INNER = 2048
SSM_HEADS = 32
SSM_GROUPS = 4
D_STATE = 128
CONV_WIDTH = 5
CHUNK = 128
XBC_DIM = 3072
HP = 128
GW = D_INNER // SSM_GROUPS
HG = SSM_HEADS // SSM_GROUPS
PH = 64
U_Z, U_GA, U_GB, U_XBC, U_SMALL = 0, 2048, 3072, 4096, 7168
S_CQ, S_CKV, S_KPE, S_DT, SMALL_W = 0, 384, 640, 768, 896
U_PAD = U_SMALL + SMALL_W
IN_SPLITS = (Q_LORA, KV_LORA, QK_ROPE, D_INNER, XBC_DIM, SSM_HEADS, SSM_HEADS, D_MODEL, D_MODEL)

ADAM_LR = 0.001
ADAM_B1 = 0.9
ADAM_B2 = 0.999
ADAM_EPS = 1e-08
ADAM_WD = 0.01
ADAM_STEP = 10

NN = (((1,), (0,)), ((), ()))
NT = (((1,), (1,)), ((), ()))
TN = (((0,), (0,)), ((), ()))


def _pick(n, pref):
    best = None
    d = 128
    while d <= min(n, pref):
        if n % d == 0:
            best = d
        d += 128
    return best if best is not None else n


def _silu(x):
    return x * jax.nn.sigmoid(x)


def _dsilu(x):
    s = jax.nn.sigmoid(x)
    return s * (1.0 + x * (1.0 - s))


def _softplus(x):
    return jnp.maximum(x, 0.0) + jnp.log(1.0 + jnp.exp(-jnp.abs(x)))


def _mm(As, Bs, *, name, ta=False, tb=False, out_dtypes=(F32,), epilogue=None, extras=(), extra_offs=None,
        tm=1024, tn=512, tk=2048, separate=False):
    As, Bs, extras = list(As), list(Bs), list(extras)
    a0, b0 = As[0], Bs[0]
    M, K = (a0.shape[1], a0.shape[0]) if ta else a0.shape
    N = b0.shape[0] if tb else b0.shape[1]
    tm, tn, tk = _pick(M, tm), _pick(N, tn), _pick(K, tk)
    nk = K // tk
    n_a, n_b, n_e, n_o = len(As), len(Bs), len(extras), len(out_dtypes)
    n_acc = (n_b if n_a == 1 or separate else 1) if nk > 1 else 0
    if extra_offs is None:
        extra_offs = (0,) * n_e
    dn = (((0,) if ta else (1,), (1,) if tb else (0,)), ((), ()))
    bytes_a = sum(a.size * a.dtype.itemsize for a in As)
    bytes_b = sum(b.size * b.dtype.itemsize for b in Bs)
    n_outer = (N // tn) * bytes_a + bytes_b < (M // tm) * bytes_b + bytes_a

    def products(a_refs, b_refs):
        if n_a == 1:
            a = a_refs[0][...].astype(BF)
            return [lax.dot_general(a, b[...].astype(BF), dn, preferred_element_type=F32) for b in b_refs]
        if separate:
            return [lax.dot_general(a[...].astype(BF), b[...].astype(BF), dn, preferred_element_type=F32)
                    for a, b in zip(a_refs, b_refs)]
        total = None
        for a, b in zip(a_refs, b_refs):
            p = lax.dot_general(a[...].astype(BF), b[...].astype(BF), dn, preferred_element_type=F32)
            total = p if total is None else total + p
        return [total]

    def finish(accs, e_refs, o_refs):
        ex = [e[...] for e in e_refs]
        outs = epilogue(*accs, *ex) if epilogue is not None else tuple(accs)
        for o_ref, val in zip(o_refs, outs):
            o_ref[...] = val.astype(o_ref.dtype)

    def body(*refs):
        a_refs, b_refs = refs[:n_a], refs[n_a:n_a + n_b]
        e_refs = refs[n_a + n_b:n_a + n_b + n_e]
        o_refs = refs[n_a + n_b + n_e:n_a + n_b + n_e + n_o]
        acc_refs = refs[n_a + n_b + n_e + n_o:]
        if nk == 1:
            finish(products(a_refs, b_refs), e_refs, o_refs)
            return
        k = pl.program_id(2)

        @pl.when(k == 0)
        def _():
            for acc in acc_refs:
                acc[...] = jnp.zeros_like(acc)

        for acc, p in zip(acc_refs, products(a_refs, b_refs)):
            acc[...] += p

        @pl.when(k == nk - 1)
        def _():
            finish([acc[...] for acc in acc_refs], e_refs, o_refs)

    def at(f):
        return (lambda j, i, k: f(i, j, k)) if n_outer else f

    a_spec = pl.BlockSpec((tk, tm), at(lambda i, j, k: (k, i))) if ta else pl.BlockSpec((tm, tk), at(lambda i, j, k: (i, k)))
    b_spec = pl.BlockSpec((tn, tk), at(lambda i, j, k: (j, k))) if tb else pl.BlockSpec((tk, tn), at(lambda i, j, k: (k, j)))
    e_specs = [pl.BlockSpec((tm, tn), at(functools.partial(lambda i, j, k, o: (i, j + o), o=off // tn))) for off in extra_offs]
    for off in extra_offs:
        assert off % tn == 0
    outs = pl.pallas_call(
        body, name=name,
        out_shape=tuple(jax.ShapeDtypeStruct((M, N), dt) for dt in out_dtypes),
        grid=(N // tn, M // tm, nk) if n_outer else (M // tm, N // tn, nk),
        in_specs=[a_spec] * n_a + [b_spec] * n_b + e_specs,
        out_specs=tuple(pl.BlockSpec((tm, tn), at(lambda i, j, k: (i, j))) for _ in out_dtypes),
        scratch_shapes=[pltpu.VMEM((tm, tn), F32)] * n_acc,
        compiler_params=pltpu.CompilerParams(dimension_semantics=("parallel", "parallel", "arbitrary")),
    )(*As, *Bs, *extras)
    return outs[0] if n_o == 1 else outs


def _rms_fwd(x, g, *, name, blk_w=None, blk_idx=0, off=0, width=None, out_dtype=BF):
    T = x.shape[0]
    blk_w = x.shape[1] if blk_w is None else blk_w
    width = blk_w if width is None else width
    tt = _pick(T, 512)

    def body(x_ref, g_ref, o_ref):
        xf = x_ref[:, off:off + width]
        r = lax.rsqrt(jnp.mean(xf * xf, axis=-1, keepdims=True) + EPS)
        o_ref[...] = (xf * r * g_ref[...]).astype(o_ref.dtype)

    return pl.pallas_call(
        body, name=name, out_shape=jax.ShapeDtypeStruct((T, width), out_dtype), grid=(T // tt,),
        in_specs=[pl.BlockSpec((tt, blk_w), lambda i: (i, blk_idx)), pl.BlockSpec((1, width), lambda i: (0, 0))],
        out_specs=pl.BlockSpec((tt, width), lambda i: (i, 0)),
    )(x, g)


def _rms_bwd(dy, x, g, *, name, blk_w=None, blk_idx=0, off=0, width=None, add=None, out_dtypes=(F32,)):
    T = x.shape[0]
    blk_w = x.shape[1] if blk_w is None else blk_w
    width = blk_w if width is None else width
    tt = _pick(T, 512)
    has_add = add is not None
    n_dx = len(out_dtypes)

    def body(*refs):
        dy_ref, x_ref, g_ref = refs[:3]
        dx_refs, dg_ref = refs[3 + has_add:3 + has_add + n_dx], refs[-1]
        xf = x_ref[:, off:off + width]
        d = dy_ref[...].astype(F32)
        r = lax.rsqrt(jnp.mean(xf * xf, axis=-1, keepdims=True) + EPS)
        gd = d * g_ref[...]
        dx = r * gd - xf * (r * r * r) * jnp.mean(gd * xf, axis=-1, keepdims=True)
        if has_add:
            dx = dx + refs[3][...]
        for dx_ref in dx_refs:
            dx_ref[...] = dx.astype(dx_ref.dtype)

        @pl.when(pl.program_id(0) == 0)
        def _():
            dg_ref[...] = jnp.zeros_like(dg_ref)

        dg_ref[...] += jnp.broadcast_to(jnp.sum(d * xf * r, axis=0, keepdims=True), dg_ref.shape)

    row = pl.BlockSpec((tt, width), lambda i: (i, 0))
    in_specs = [row, pl.BlockSpec((tt, blk_w), lambda i: (i, blk_idx)), pl.BlockSpec((1, width), lambda i: (0, 0))]
    args = [dy, x, g]
    if has_add:
        in_specs.append(row)
        args.append(add)
    return pl.pallas_call(
        body, name=name,
        out_shape=tuple(jax.ShapeDtypeStruct((T, width), dt) for dt in out_dtypes) + (jax.ShapeDtypeStruct((8, width), F32),),
        grid=(T // tt,), in_specs=in_specs,
        out_specs=(row,) * n_dx + (pl.BlockSpec((8, width), lambda i: (0, 0)),),
        compiler_params=pltpu.CompilerParams(dimension_semantics=("arbitrary",)),
    )(*args)


def _rope_tables(pos_col, freq_lane):
    T = pos_col.shape[0]
    tt = _pick(T, 512)

    def body(p_ref, f_ref, c_ref, s_ref):
        ang = p_ref[...] * f_ref[...]
        lane = lax.broadcasted_iota(jnp.int32, ang.shape, 1)
        c_ref[...] = jnp.where(lane < QK_HEAD, jnp.cos(ang), 0.0)
        sn = jnp.sin(ang)
        s_ref[...] = jnp.where((lane >= QK_NOPE) & (lane < QK_NOPE + 16), -sn,
                               jnp.where((lane >= QK_NOPE + 16) & (lane < QK_HEAD), sn, 0.0))

    return pl.pallas_call(
        body, name="rope_tables", out_shape=(jax.ShapeDtypeStruct((T, HP), F32),) * 2, grid=(T // tt,),
        in_specs=[pl.BlockSpec((tt, 1), lambda i: (i, 0)), pl.BlockSpec((1, HP), lambda i: (0, 0))],
        out_specs=(pl.BlockSpec((tt, HP), lambda i: (i, 0)),) * 2,
    )(pos_col, freq_lane)


def _swap_rope_halves(n):
    src = lax.broadcasted_iota(jnp.int32, (HP, HP), 0)
    dst = lax.broadcasted_iota(jnp.int32, (HP, HP), 1)
    lo = (dst >= QK_NOPE) & (dst < QK_NOPE + 16) & (src == dst + 16)
    hi = (dst >= QK_NOPE + 16) & (dst < QK_HEAD) & (src == dst - 16)
    return _split_dot(n, jnp.where(lo | hi, 1.0, 0.0).astype(BF), 2)


def _qk_prep_fwd(raw, kpe, gain, C, S, *, name, kpe_blk=0, out_scale=1.0):
    T = raw.shape[0]
    tt = _pick(T, 256)
    has_kpe = kpe is not None

    def body(*refs):
        if has_kpe:
            raw_ref, kpe_ref, g_ref, c_ref, s_ref, o_ref = refs
        else:
            raw_ref, g_ref, c_ref, s_ref, o_ref = refs
        for h in range(N_HEADS):
            hs = slice(HP * h, HP * (h + 1))
            xr = raw_ref[:, hs] + kpe_ref[...] if has_kpe else raw_ref[:, hs]
            r = lax.rsqrt(jnp.sum(xr * xr, axis=-1, keepdims=True) * (1.0 / QK_HEAD) + EPS)
            n = xr * r * g_ref[...]
            o_ref[:, hs] = ((n * c_ref[...] + _swap_rope_halves(n) * s_ref[...]) * out_scale).astype(o_ref.dtype)

    heads = pl.BlockSpec((tt, N_HEADS * HP), lambda i: (i, 0))
    shared = pl.BlockSpec((tt, HP), lambda i: (i, 0))
    kpe_spec = pl.BlockSpec((tt, HP), lambda i: (i, kpe_blk))
    in_specs = [heads] + ([kpe_spec] if has_kpe else []) + [pl.BlockSpec((1, HP), lambda i: (0, 0)), shared, shared]
    args = [raw] + ([kpe] if has_kpe else []) + [gain, C, S]
    return pl.pallas_call(
        body, name=name, out_shape=jax.ShapeDtypeStruct(raw.shape, BF), grid=(T // tt,),
        in_specs=in_specs, out_specs=heads,
    )(*args)


def _qk_prep_bwd(dout, raw, kpe, gain, C, S, *, name, kpe_blk=0, in_scale=1.0):
    T = raw.shape[0]
    tt = _pick(T, 256)
    has_kpe = kpe is not None

    def body(*refs):
        if has_kpe:
            d_ref, raw_ref, kpe_ref, g_ref, c_ref, s_ref, dx_ref, dg_ref, dkpe_ref = refs
        else:
            d_ref, raw_ref, g_ref, c_ref, s_ref, dx_ref, dg_ref = refs
        dg = jnp.zeros((1, HP), F32)
        dkpe = jnp.zeros((tt, HP), F32)
        for h in range(N_HEADS):
            hs = slice(HP * h, HP * (h + 1))
            xr = raw_ref[:, hs] + kpe_ref[...] if has_kpe else raw_ref[:, hs]
            d = d_ref[:, hs].astype(F32) * in_scale
            r = lax.rsqrt(jnp.sum(xr * xr, axis=-1, keepdims=True) * (1.0 / QK_HEAD) + EPS)
            dn = d * c_ref[...] + _swap_rope_halves(d * s_ref[...])
            gd = dn * g_ref[...]
            dx = r * gd - xr * (r * r * r) * (jnp.sum(gd * xr, axis=-1, keepdims=True) * (1.0 / QK_HEAD))
            dx_ref[:, hs] = dx.astype(dx_ref.dtype)
            dg = dg + jnp.sum(dn * xr * r, axis=0, keepdims=True)
            dkpe = dkpe + dx

        @pl.when(pl.program_id(0) == 0)
        def _():
            dg_ref[...] = jnp.zeros_like(dg_ref)

        dg_ref[...] += jnp.broadcast_to(dg, dg_ref.shape)
        if has_kpe:
            dkpe_ref[...] = dkpe

    heads = pl.BlockSpec((tt, N_HEADS * HP), lambda i: (i, 0))
    shared = pl.BlockSpec((tt, HP), lambda i: (i, 0))
    kpe_spec = pl.BlockSpec((tt, HP), lambda i: (i, kpe_blk))
    in_specs = [heads, heads] + ([kpe_spec] if has_kpe else []) + [pl.BlockSpec((1, HP), lambda i: (0, 0)), shared, shared]
    args = [dout, raw] + ([kpe] if has_kpe else []) + [gain, C, S]
    out_shape = [jax.ShapeDtypeStruct(raw.shape, BF), jax.ShapeDtypeStruct((8, HP), F32)]
    out_specs = [heads, pl.BlockSpec((8, HP), lambda i: (0, 0))]
    if has_kpe:
        out_shape.append(jax.ShapeDtypeStruct((T, HP), F32))
        out_specs.append(shared)
    return pl.pallas_call(
        body, name=name, out_shape=tuple(out_shape), grid=(T // tt,),
        in_specs=in_specs, out_specs=tuple(out_specs),
        compiler_params=pltpu.CompilerParams(dimension_semantics=("arbitrary",)),
    )(*args)


ATTN_SCALE = 1.0 / math.sqrt(QK_HEAD)
LOG2E = 1.0 / math.log(2.0)
Q_SCALE = ATTN_SCALE * LOG2E


def _attn_fwd(q, k, v):
    T = q.shape[0]
    tq = _pick(T, 256)

    def body(q_ref, k_ref, v_ref, o_ref, lse_ref):
        s = lax.dot_general(q_ref[...], k_ref[...], NT, preferred_element_type=F32)
        m = jnp.max(s, axis=-1, keepdims=True)
        p = jnp.exp2(s - m)
        l = jnp.sum(p, axis=-1, keepdims=True)
        o = jnp.dot(p.astype(BF), v_ref[...], preferred_element_type=F32)
        o_ref[...] = o / l
        lse_ref[...] = jnp.broadcast_to(m + jnp.log2(l), lse_ref.shape)

    qs = pl.BlockSpec((tq, HP), lambda h, i: (i, h))
    kv = pl.BlockSpec((T, HP), lambda h, i: (0, h))
    return pl.pallas_call(
        body, name="attn_fwd", out_shape=(jax.ShapeDtypeStruct(q.shape, F32),) * 2, grid=(N_HEADS, T // tq),
        in_specs=[qs, kv, kv], out_specs=(qs, qs),
        compiler_params=pltpu.CompilerParams(dimension_semantics=("parallel", "parallel")),
    )(q, k, v)


def _attn_bwd(q, k, v, do, o, lse):
    T = q.shape[0]
    tb = _pick(T, 512)
    nb = T // tb

    def body(q_ref, k_ref, v_ref, do_ref, o_ref, lse_ref, dq_ref, dk_ref, dv_ref, delta_rows, lse_rows, dob_scr):
        dq_ref[...] = jnp.zeros_like(dq_ref)
        lane = lax.broadcasted_iota(jnp.int32, (8, HP), 1)
        ones8 = jnp.ones((8, HP), BF)
        first8 = jnp.where(lane == 0, 1.0, 0.0).astype(BF)

        def as_rows(pick, v):
            total, rest = None, v
            for _ in range(3):
                piece = rest.astype(BF)
                part = lax.dot_general(pick, piece, NT, preferred_element_type=F32)
                total = part if total is None else total + part
                rest = rest - piece.astype(F32)
            return total

        def per_q_tile(i, carry):
            qs = pl.ds(pl.multiple_of(i * tb, tb), tb)
            doi = do_ref[qs, :]
            delta_rows[i] = as_rows(ones8, doi * o_ref[qs, :])
            lse_rows[i] = as_rows(first8, lse_ref[qs, :])
            dob_scr[qs, :] = doi.astype(BF)
            return carry

        lax.fori_loop(0, nb, per_q_tile, 0)

        def k_loop(j, carry):
            ks = pl.ds(pl.multiple_of(j * tb, tb), tb)
            kj, vj = k_ref[ks, :], v_ref[ks, :]

            def q_loop(i, acc):
                dk_acc, dv_acc = acc
                qs = pl.ds(pl.multiple_of(i * tb, tb), tb)
                qi = q_ref[qs, :]
                dob = dob_scr[qs, :]
                s_t = lax.dot_general(kj, qi, NT, preferred_element_type=F32)
                p_t = jnp.exp2(s_t - lse_rows[i, 0:1, :])
                dp_t = lax.dot_general(vj, dob, NT, preferred_element_type=F32)
                ds_t = (p_t * (dp_t - delta_rows[i, 0:1, :])).astype(BF)
                dv_acc = dv_acc + jnp.dot(p_t.astype(BF), dob, preferred_element_type=F32)
                dk_acc = dk_acc + jnp.dot(ds_t, qi, preferred_element_type=F32)
                dq_ref[qs, :] += lax.dot_general(ds_t, kj, TN, preferred_element_type=F32)
                return dk_acc, dv_acc

            zero = jnp.zeros((tb, HP), F32)
            dk_acc, dv_acc = lax.fori_loop(0, nb, q_loop, (zero, zero))
            dk_ref[ks, :] = dk_acc
            dv_ref[ks, :] = dv_acc.astype(dv_ref.dtype)
            return carry

        lax.fori_loop(0, nb, k_loop, 0)

    spec = pl.BlockSpec((T, HP), lambda h: (0, h))
    return pl.pallas_call(
        body, name="attn_bwd",
        out_shape=(jax.ShapeDtypeStruct(q.shape, F32), jax.ShapeDtypeStruct(q.shape, F32), jax.ShapeDtypeStruct(q.shape, BF)),
        grid=(N_HEADS,), in_specs=[spec] * 6, out_specs=(spec,) * 3,
        scratch_shapes=[pltpu.VMEM((nb, 8, tb), F32), pltpu.VMEM((nb, 8, tb), F32), pltpu.VMEM((T, HP), BF)],
        compiler_params=pltpu.CompilerParams(dimension_semantics=("parallel",), vmem_limit_bytes=2 * 15 * T * HP * 2 + (8 << 20)),
    )(q, k, v, do, o, lse)


CONV_TC = 512
CONV_PAD = CONV_WIDTH // 2


def _halo_specs(tr, col_of):
    r8 = tr // 8
    cur = pl.BlockSpec((tr, CONV_TC), lambda j, i: (i, col_of(j)))
    prev = pl.BlockSpec((8, CONV_TC), lambda j, i: (jnp.maximum(i * r8 - 1, 0), col_of(j)))

    def nxt_map(j, i, n8):
        return (jnp.minimum((i + 1) * r8, n8 - 1), col_of(j))

    return cur, prev, nxt_map


def _with_halo(prev_ref, cur_ref, next_ref, i, n_i):
    prev = jnp.where(i == 0, 0.0, prev_ref[...].astype(F32))
    nxt = jnp.where(i == n_i - 1, 0.0, next_ref[...].astype(F32))
    return jnp.concatenate([prev, cur_ref[...].astype(F32), nxt], axis=0)


def _conv_fwd(u, w8, b):
    T = u.shape[0]
    tr = _pick(T, 512)
    n_i = T // tr
    c0 = U_XBC // CONV_TC
    cur, prev, nxt_map = _halo_specs(tr, lambda j: c0 + j)
    nxt = pl.BlockSpec((8, CONV_TC), functools.partial(nxt_map, n8=T // 8))

    def body(p_ref, c_ref, n_ref, w_ref, b_ref, pre_ref, act_ref):
        i = pl.program_id(1)
        full = _with_halo(p_ref, c_ref, n_ref, i, n_i)
        acc = jnp.broadcast_to(b_ref[...], (tr, CONV_TC))
        for kk in range(CONV_WIDTH):
            acc = acc + full[8 - CONV_PAD + kk:8 - CONV_PAD + kk + tr, :] * w_ref[kk:kk + 1, :]
        pre_ref[...] = acc
        act_ref[...] = _silu(acc)

    out = pl.BlockSpec((tr, CONV_TC), lambda j, i: (i, j))
    return pl.pallas_call(
        body, name="conv_fwd", out_shape=(jax.ShapeDtypeStruct((T, XBC_DIM), F32),) * 2,
        grid=(XBC_DIM // CONV_TC, n_i),
        in_specs=[prev, cur, nxt, pl.BlockSpec((8, CONV_TC), lambda j, i: (0, j)), pl.BlockSpec((1, CONV_TC), lambda j, i: (0, j))],
        out_specs=(out, out),
    )(u, u, u, w8, b)


def _conv_dpre(dacts, pre, col0, *, name):
    T, width = dacts[0].shape
    tt = _pick(T, 512)
    n_d = len(dacts)
    c0 = col0 // CONV_TC

    def body(*refs):
        d = refs[0][...]
        for r in refs[1:n_d]:
            d = d + r[...]
        refs[n_d + 1][...] = d * _dsilu(refs[n_d][...])

    blk = pl.BlockSpec((tt, CONV_TC), lambda j, i: (i, j))
    return pl.pallas_call(
        body, name=name, out_shape=jax.ShapeDtypeStruct((T, width), F32), grid=(width // CONV_TC, T // tt),
        in_specs=[blk] * n_d + [pl.BlockSpec((tt, CONV_TC), lambda j, i: (i, c0 + j))], out_specs=blk,
    )(*dacts, pre)


def _conv_bwd(dpre, u, w8, col0, *, name):
    T, width = dpre.shape
    tr = _pick(T, 512)
    n_i = T // tr
    cd = col0 // CONV_TC
    cx = (U_XBC + col0) // CONV_TC
    d_cur, d_prev, d_nxt_map = _halo_specs(tr, lambda j: j)
    x_cur, x_prev, x_nxt_map = _halo_specs(tr, lambda j: cx + j)
    d_nxt = pl.BlockSpec((8, CONV_TC), functools.partial(d_nxt_map, n8=T // 8))
    x_nxt = pl.BlockSpec((8, CONV_TC), functools.partial(x_nxt_map, n8=T // 8))

    def body(dp_ref, dc_ref, dn_ref, xp_ref, xc_ref, xn_ref, w_ref, dx_ref, dw_ref):
        i = pl.program_id(1)
        dfull = _with_halo(dp_ref, dc_ref, dn_ref, i, n_i)
        xfull = _with_halo(xp_ref, xc_ref, xn_ref, i, n_i)
        dcur = dc_ref[...]
        dx = jnp.zeros((tr, CONV_TC), F32)
        rows = []
        for kk in range(CONV_WIDTH):
            dx = dx + dfull[8 + CONV_PAD - kk:8 + CONV_PAD - kk + tr, :] * w_ref[kk:kk + 1, :]
            rows.append(jnp.sum(dcur * xfull[8 - CONV_PAD + kk:8 - CONV_PAD + kk + tr, :], axis=0, keepdims=True))
        rows.append(jnp.sum(dcur, axis=0, keepdims=True))
        rows.append(jnp.zeros((2, CONV_TC), F32))
        dx_ref[...] = dx.astype(dx_ref.dtype)

        @pl.when(i == 0)
        def _():
            dw_ref[...] = jnp.zeros_like(dw_ref)

        dw_ref[...] += jnp.concatenate(rows, axis=0)

    out = pl.BlockSpec((tr, CONV_TC), lambda j, i: (i, j))
    return pl.pallas_call(
        body, name=name, out_shape=(jax.ShapeDtypeStruct((T, width), BF), jax.ShapeDtypeStruct((8, width), F32)),
        grid=(width // CONV_TC, n_i),
        in_specs=[d_prev, d_cur, d_nxt, x_prev, x_cur, x_nxt, pl.BlockSpec((8, CONV_TC), lambda j, i: (0, cd + j))],
        out_specs=(out, pl.BlockSpec((8, CONV_TC), lambda j, i: (0, j))),
        compiler_params=pltpu.CompilerParams(dimension_semantics=("parallel", "arbitrary")),
    )(dpre, dpre, dpre, u, u, u, w8)


N_HB = 2 * SSM_GROUPS
P_DT, P_CS, P_E, P_W = 0, HP, 2 * HP, 3 * HP
DT_BLK = (U_SMALL + S_DT) // HP


def _tri(rev, transpose=False):
    rows = lax.broadcasted_iota(jnp.int32, (CHUNK, CHUNK), 0)
    cols = lax.broadcasted_iota(jnp.int32, (CHUNK, CHUNK), 1)
    if transpose:
        rows, cols = cols, rows
    return (cols >= rows) if rev else (cols <= rows)


def _ssd_prep(u, bias8, alog8):
    T = u.shape[0]
    nc = T // CHUNK

    def body(dt_ref, bias_ref, a_ref, cols_ref, rows_ref):
        lane = lax.broadcasted_iota(jnp.int32, (CHUNK, HP), 1)
        dt = _softplus(dt_ref[...] + bias_ref[0:1, :])
        da = dt * (-jnp.exp(a_ref[0:1, :]))
        cs_f = jnp.dot(jnp.where(_tri(False), 1.0, 0.0).astype(F32), da, precision=HI, preferred_element_type=F32)
        cs_b = jnp.dot(jnp.where(_tri(True), 1.0, 0.0).astype(F32), da, precision=HI, preferred_element_type=F32)
        cs = jnp.where(lane < SSM_HEADS, cs_f, cs_b)
        tot = jnp.where(lane[0:1] < SSM_HEADS, cs_f[CHUNK - 1:CHUNK, :], cs_b[0:1, :])
        e, w = jnp.exp(cs), jnp.exp(tot - cs)
        tot8 = jnp.broadcast_to(tot, (8, HP))
        etot8 = jnp.exp(tot8)
        for b in range(N_HB):
            down = (HP - HG * b) % HP

            def rolled(v):
                return pltpu.roll(v, down, 1) if down else v

            cols_ref[b, :, P_DT:P_DT + HP] = rolled(dt)
            cs_r = rolled(cs)
            cols_ref[b, :, P_CS:P_CS + HP] = cs_r
            cols_ref[b, :, P_E:P_E + HP] = rolled(e)
            cols_ref[b, :, P_W:P_W + HP] = rolled(w)
            rows_ref[b, 0, 0:8, :] = cs_r.T[0:8, :]
            r8 = lax.broadcasted_iota(jnp.int32, (8, HP), 0)
            rows_ref[b, 0, 8:16, :] = jnp.where(r8 == 0, rolled(tot8), jnp.where(r8 == 1, rolled(etot8), 0.0))

    vec = pl.BlockSpec((8, HP), lambda c: (0, 0))
    return pl.pallas_call(
        body, name="ssd_prep",
        out_shape=(jax.ShapeDtypeStruct((N_HB, T, 4 * HP), F32), jax.ShapeDtypeStruct((N_HB, nc, 16, HP), F32)),
        grid=(nc,), in_specs=[pl.BlockSpec((CHUNK, HP), lambda c: (c, DT_BLK)), vec, vec],
        out_specs=(pl.BlockSpec((N_HB, CHUNK, 4 * HP), lambda c: (0, c, 0)), pl.BlockSpec((N_HB, 1, 16, HP), lambda c: (0, c, 0, 0))),
    )(u, bias8, alog8)


def _ssd_specs(T, rev, bwd):
    nc = T // CHUNK
    fwd_order = (lambda c: nc - 1 - c) if rev else (lambda c: c)
    cm = (lambda c: fwd_order(nc - 1 - c)) if bwd else fwd_order
    hb0 = SSM_GROUPS if rev else 0
    xs = pl.BlockSpec((CHUNK, GW), lambda c, g: (cm(c), g))
    bs = pl.BlockSpec((CHUNK, D_STATE), lambda c, g: (cm(c), D_INNER // D_STATE + g))
    cs = pl.BlockSpec((CHUNK, D_STATE), lambda c, g: (cm(c), (D_INNER + SSM_GROUPS * D_STATE) // D_STATE + g))
    cols = pl.BlockSpec((1, CHUNK, 4 * HP), lambda c, g: (hb0 + g, cm(c), 0))
    rows = pl.BlockSpec((1, 1, 16, HP), lambda c, g: (hb0 + g, cm(c), 0, 0))
    return nc, cm, xs, bs, cs, cols, rows


def _head_lanes(to_heads):
    shape = (GW, HP) if to_heads else (HP, GW)
    wide = lax.broadcasted_iota(jnp.int32, shape, 0 if to_heads else 1)
    head = lax.broadcasted_iota(jnp.int32, shape, 1 if to_heads else 0)
    return jnp.where((wide >= PH * head) & (wide < PH * (head + 1)), 1.0, 0.0).astype(BF)


def _split_dot(v, m, terms):
    total, rest = None, v
    for _ in range(terms):
        piece = rest.astype(BF)
        part = jnp.dot(piece, m, preferred_element_type=F32)
        total = part if total is None else total + part
        rest = rest - piece.astype(F32)
    return total


def _spread_cols(cols_ref, rows_ref):
    spread = _head_lanes(False)
    dt_e = _split_dot(cols_ref[0, :, P_DT:P_DT + HP], spread, 3)
    e_e = _split_dot(cols_ref[0, :, P_E:P_E + HP], spread, 3)
    w_e = _split_dot(cols_ref[0, :, P_W:P_W + HP], spread, 3)
    etot_e = _split_dot(rows_ref[0, 0, 8:16, :], spread, 3)[1:2, :]
    return dt_e, e_e, w_e, etot_e


def _decay(cols_ref, rows_ref, hh, incl, transpose=False):
    col = cols_ref[0, :, P_CS + hh:P_CS + hh + 1]
    row = rows_ref[0, 0, hh:hh + 1, :]
    return jnp.where(incl, jnp.exp(row - col if transpose else col - row), 0.0)


def _ssd_fwd(act, cols, rows, *, rev, name):
    T = act.shape[0]
    nc, cm, xs_s, b_s, c_s, cols_s, rows_s = _ssd_specs(T, rev, False)

    def body(x_ref, b_ref, c_ref, cols_ref, rows_ref, y_ref, st_ref, state):
        c, g = pl.program_id(0), pl.program_id(1)

        @pl.when(c == 0)
        def _():
            state[g] = jnp.zeros((D_STATE, GW), F32)

        incl = _tri(rev)
        bm, cmat = b_ref[...].astype(BF), c_ref[...].astype(BF)
        bm_t = b_ref[...].T.astype(BF)
        cb = lax.dot_general(cmat, bm, NT, preferred_element_type=F32)
        dt_e, e_e, w_e, etot_e = _spread_cols(cols_ref, rows_ref)
        prev_all = state[g]
        st_ref[...] = prev_all
        xdt = x_ref[...] * dt_e
        xdt_b = xdt.astype(BF)
        yo_all = jnp.dot(cmat, prev_all.astype(BF), preferred_element_type=F32) * e_e
        state[g] = prev_all * etot_e + jnp.dot(bm_t, (xdt * w_e).astype(BF), preferred_element_type=F32)
        for hh in range(HG):
            hs = slice(PH * hh, PH * (hh + 1))
            lmat = _decay(cols_ref, rows_ref, hh, incl)
            yd = jnp.dot((cb * lmat).astype(BF), xdt_b[:, hs], preferred_element_type=F32)
            y_ref[:, hs] = yd + yo_all[:, hs]

    return pl.pallas_call(
        body, name=name,
        out_shape=(jax.ShapeDtypeStruct((T, D_INNER), F32), jax.ShapeDtypeStruct((nc * D_STATE, D_INNER), F32)),
        grid=(nc, SSM_GROUPS), in_specs=[xs_s, b_s, c_s, cols_s, rows_s], out_specs=(xs_s, xs_s),
        scratch_shapes=[pltpu.VMEM((SSM_GROUPS, D_STATE, GW), F32)],
        compiler_params=pltpu.CompilerParams(dimension_semantics=("arbitrary", "arbitrary")),
    )(act, act, act, cols, rows)


def _ssd_bwd(act, cols, rows, states, dy, *, rev, name):
    T = act.shape[0]
    nc, cm, xs_s, b_s, c_s, cols_s, rows_s = _ssd_specs(T, rev, True)

    def body(x_ref, b_ref, c_ref, cols_ref, rows_ref, st_ref, dy_ref, dx_ref, db_ref, dc_ref, dsel_ref, dtot_ref,
             dstate, dcs_cols, dcs_rows, dcb, dm_scr, dxdt_scr):
        c, g = pl.program_id(0), pl.program_id(1)

        @pl.when(c == 0)
        def _():
            dstate[g] = jnp.zeros((D_STATE, GW), F32)

        incl, incl_t = _tri(rev), _tri(rev, transpose=True)
        bm, cmat = b_ref[...].astype(BF), c_ref[...].astype(BF)
        cm_t = c_ref[...].T.astype(BF)
        cb = lax.dot_general(cmat, bm, NT, preferred_element_type=F32)
        cb_t = lax.dot_general(bm, cmat, NT, preferred_element_type=F32)
        prev_all, ds_all = st_ref[...], dstate[g]
        pb_all, dsb_all = prev_all.astype(BF), ds_all.astype(BF)
        cp_all = jnp.dot(cmat, pb_all, preferred_element_type=F32)
        bds_all = jnp.dot(bm, dsb_all, preferred_element_type=F32)
        dt_e, e_e, w_e, etot_e = _spread_cols(cols_ref, rows_ref)
        to_heads = _head_lanes(True)
        x, dy = x_ref[...], dy_ref[...]
        xdt = x * dt_e
        xdt_b, dy_b = xdt.astype(BF), dy.astype(BF)
        dye_b, xdw_b = (dy * e_e).astype(BF), (xdt * w_e).astype(BF)
        for hh in range(HG):
            hs = slice(PH * hh, PH * (hh + 1))
            mmat_t = cb_t * _decay(cols_ref, rows_ref, hh, incl_t, transpose=True)
            dm_scr[hh] = lax.dot_general(dy_b[:, hs], xdt_b[:, hs], NT, preferred_element_type=F32)
            dxdt_scr[:, hs] = jnp.dot(mmat_t.astype(BF), dy_b[:, hs], preferred_element_type=F32)
        bdsw = bds_all * w_e
        dxdt = dxdt_scr[...] + bdsw
        dx_ref[...] = dxdt * dt_e
        t = _split_dot(xdt * bdsw, to_heads, 2)
        dcs_state = _split_dot(dy * cp_all, to_heads, 2) * cols_ref[0, :, P_E:P_E + HP] - t
        dsel_ref[0, :, 0:HP] = _split_dot(dxdt * x, to_heads, 2)
        sp = _split_dot(jnp.broadcast_to(jnp.sum(ds_all * prev_all, axis=0, keepdims=True), (8, GW)), to_heads, 2)
        dtot_ref[0, 0] = jnp.sum(t, axis=0, keepdims=True) + sp * rows_ref[0, 0, 9:10, :]
        dstate[g] = ds_all * etot_e + jnp.dot(cm_t, dye_b, preferred_element_type=F32)
        dcs_cols[...] = jnp.zeros_like(dcs_cols)
        dcs_rows[...] = jnp.zeros_like(dcs_rows)
        dcb[...] = jnp.zeros_like(dcb)
        for hh in range(HG):
            lmat = _decay(cols_ref, rows_ref, hh, incl)
            dm = dm_scr[hh]
            qm = dm * (cb * lmat)
            dcs_cols[:, hh:hh + 1] = jnp.sum(qm, axis=1, keepdims=True)
            dcs_rows[hh:hh + 1, :] = jnp.sum(qm, axis=0, keepdims=True)
            dcb[...] += dm * lmat
        dcb_all = dcb[...]
        dsel_ref[0, :, HP:2 * HP] = dcs_state + dcs_cols[...] - dcs_rows[...].T
        dc_ref[...] = (lax.dot_general(dye_b, pb_all, NT, preferred_element_type=F32)
                       + jnp.dot(dcb_all.astype(BF), bm, preferred_element_type=F32))
        db_ref[...] = (lax.dot_general(xdw_b, dsb_all, NT, preferred_element_type=F32)
                       + jnp.dot(dcb_all.T.astype(BF), cmat, preferred_element_type=F32))

    bc_out = pl.BlockSpec((CHUNK, D_STATE), lambda c, g: (cm(c), g))
    return pl.pallas_call(
        body, name=name,
        out_shape=(jax.ShapeDtypeStruct((T, D_INNER), F32), jax.ShapeDtypeStruct((T, SSM_GROUPS * D_STATE), F32),
                   jax.ShapeDtypeStruct((T, SSM_GROUPS * D_STATE), F32), jax.ShapeDtypeStruct((SSM_GROUPS, T, 2 * HP), F32),
                   jax.ShapeDtypeStruct((SSM_GROUPS, nc, 8, HP), F32)),
        grid=(nc, SSM_GROUPS), in_specs=[xs_s, b_s, c_s, cols_s, rows_s, xs_s, xs_s],
        out_specs=(xs_s, bc_out, bc_out, pl.BlockSpec((1, CHUNK, 2 * HP), lambda c, g: (g, cm(c), 0)),
                   pl.BlockSpec((1, 1, 8, HP), lambda c, g: (g, cm(c), 0, 0))),
        scratch_shapes=[pltpu.VMEM((SSM_GROUPS, D_STATE, GW), F32), pltpu.VMEM((CHUNK, CHUNK), F32),
                        pltpu.VMEM((CHUNK, CHUNK), F32), pltpu.VMEM((CHUNK, CHUNK), F32),
                        pltpu.VMEM((HG, CHUNK, CHUNK), F32), pltpu.VMEM((CHUNK, GW), F32)],
        compiler_params=pltpu.CompilerParams(dimension_semantics=("arbitrary", "arbitrary")),
    )(act, act, act, cols, rows, states, dy)


def _ssd_prep_bwd(u, bias8, alog8, dsel_f, dtot_f, dsel_b, dtot_b):
    T = u.shape[0]
    nc = T // CHUNK

    def body(dt_ref, bias_ref, a_ref, sf_ref, tf_ref, sb_ref, tb_ref, ddt_ref, da_ref, dbias_ref):
        @pl.when(pl.program_id(0) == 0)
        def _():
            da_ref[...] = jnp.zeros_like(da_ref)
            dbias_ref[...] = jnp.zeros_like(dbias_ref)

        lane = lax.broadcasted_iota(jnp.int32, (CHUNK, HP), 1)
        pre = dt_ref[...] + bias_ref[0:1, :]
        dt = _softplus(pre)
        a = -jnp.exp(a_ref[0:1, :])
        ddt_x, dcs, dtot = jnp.zeros((CHUNK, HP), F32), jnp.zeros((CHUNK, HP), F32), jnp.zeros((8, HP), F32)
        for b in range(N_HB):
            s_ref, t_ref, g = (sf_ref, tf_ref, b) if b < SSM_GROUPS else (sb_ref, tb_ref, b - SSM_GROUPS)
            mine = (lane >= HG * b) & (lane < HG * (b + 1))

            def up(v):
                return pltpu.roll(v, HG * b, 1) if b else v

            ddt_x = ddt_x + jnp.where(mine, up(s_ref[g, :, 0:HP]), 0.0)
            dcs = dcs + jnp.where(mine, up(s_ref[g, :, HP:2 * HP]), 0.0)
            dtot = dtot + jnp.where(mine[0:8], up(t_ref[g, 0]), 0.0)
        tri_f = jnp.where(_tri(False, transpose=True), 1.0, 0.0).astype(F32)
        tri_b = jnp.where(_tri(True, transpose=True), 1.0, 0.0).astype(F32)
        dda = jnp.where(lane < SSM_HEADS, jnp.dot(tri_f, dcs, precision=HI, preferred_element_type=F32),
                        jnp.dot(tri_b, dcs, precision=HI, preferred_element_type=F32)) + dtot[0:1, :]
        dpre = (ddt_x + dda * a) * jax.nn.sigmoid(pre)
        ddt_ref[...] = jnp.where(lane < 2 * SSM_HEADS, dpre, 0.0)
        dbias_ref[...] += jnp.broadcast_to(jnp.sum(dpre, axis=0, keepdims=True), (8, HP))
        da_ref[...] += jnp.broadcast_to(jnp.sum(dda * dt, axis=0, keepdims=True) * a, (8, HP))

    vec = pl.BlockSpec((8, HP), lambda c: (0, 0))
    sel = pl.BlockSpec((SSM_GROUPS, CHUNK, 2 * HP), lambda c: (0, c, 0))
    tot = pl.BlockSpec((SSM_GROUPS, 1, 8, HP), lambda c: (0, c, 0, 0))
    tile = pl.BlockSpec((CHUNK, HP), lambda c: (c, 0))
    return pl.pallas_call(
        body, name="ssd_prep_bwd",
        out_shape=(jax.ShapeDtypeStruct((T, HP), F32), jax.ShapeDtypeStruct((8, HP), F32), jax.ShapeDtypeStruct((8, HP), F32)),
        grid=(nc,), in_specs=[pl.BlockSpec((CHUNK, HP), lambda c: (c, DT_BLK)), vec, vec, sel, tot, sel, tot],
        out_specs=(tile, vec, vec),
        compiler_params=pltpu.CompilerParams(dimension_semantics=("arbitrary",)),
    )(u, bias8, alog8, dsel_f, dtot_f, dsel_b, dtot_b)


def _ssm_combine_fwd(y_f, y_b, act, u, dskip, gain):
    T = y_f.shape[0]
    tt = _pick(T, 256)

    def body(yf_ref, yb_ref, x_ref, z_ref, ds_ref, g_ref, y_ref, m_ref):
        y = yf_ref[...] + yb_ref[...] + ds_ref[...] * x_ref[...]
        y2 = y * _silu(z_ref[...])
        r = lax.rsqrt(jnp.mean(y2 * y2, axis=-1, keepdims=True) + EPS)
        y_ref[...] = y
        m_ref[...] = (y2 * r * g_ref[...]).astype(m_ref.dtype)

    blk = pl.BlockSpec((tt, GW), lambda i, g: (i, g))
    vec = pl.BlockSpec((1, GW), lambda i, g: (0, g))
    return pl.pallas_call(
        body, name="ssm_combine_fwd",
        out_shape=(jax.ShapeDtypeStruct((T, D_INNER), F32), jax.ShapeDtypeStruct((T, D_INNER), BF)),
        grid=(T // tt, SSM_GROUPS), in_specs=[blk, blk, blk, blk, vec, vec], out_specs=(blk, blk),
    )(y_f, y_b, act, u, dskip, gain)


def _ssm_combine_bwd(dm, y, act, u, dskip, gain):
    T = y.shape[0]
    tt = _pick(T, 256)

    def body(dm_ref, y_ref, x_ref, z_ref, ds_ref, g_ref, dy_ref, dz_ref, dxs_ref, dg_ref, dsk_ref):
        z = z_ref[...]
        y = y_ref[...]
        x = x_ref[...]
        sz = _silu(z)
        y2 = y * sz
        r = lax.rsqrt(jnp.mean(y2 * y2, axis=-1, keepdims=True) + EPS)
        d = dm_ref[...]
        gd = d * g_ref[...]
        dy2 = r * gd - y2 * (r * r * r) * jnp.mean(gd * y2, axis=-1, keepdims=True)
        dy = dy2 * sz
        dy_ref[...] = dy
        dz_ref[...] = (dy2 * y * _dsilu(z)).astype(dz_ref.dtype)
        dxs_ref[...] = dy * ds_ref[...]

        @pl.when(pl.program_id(1) == 0)
        def _():
            dg_ref[...] = jnp.zeros_like(dg_ref)
            dsk_ref[...] = jnp.zeros_like(dsk_ref)

        dg_ref[...] += jnp.broadcast_to(jnp.sum(d * y2 * r, axis=0, keepdims=True), dg_ref.shape)
        lane_sum = jnp.broadcast_to(jnp.sum(dy * x, axis=0, keepdims=True), (8, GW))
        src = lax.broadcasted_iota(jnp.int32, (GW, HP), 0)
        head = lax.broadcasted_iota(jnp.int32, (GW, HP), 1)
        to_head = jnp.where((src >= PH * head) & (src < PH * (head + 1)), 1.0, 0.0).astype(F32)
        dsk_ref[...] += jnp.dot(lane_sum, to_head, precision=HI, preferred_element_type=F32)

    blk = pl.BlockSpec((tt, GW), lambda g, i: (i, g))
    vec = pl.BlockSpec((1, GW), lambda g, i: (0, g))
    acc = pl.BlockSpec((8, GW), lambda g, i: (0, g))
    return pl.pallas_call(
        body, name="ssm_combine_bwd",
        out_shape=(jax.ShapeDtypeStruct((T, D_INNER), F32), jax.ShapeDtypeStruct((T, D_INNER), BF),
                   jax.ShapeDtypeStruct((T, D_INNER), F32), jax.ShapeDtypeStruct((8, D_INNER), F32),
                   jax.ShapeDtypeStruct((8, SSM_GROUPS * HP), F32)),
        grid=(SSM_GROUPS, T // tt), in_specs=[blk, blk, blk, blk, vec, vec],
        out_specs=(blk, blk, blk, acc, pl.BlockSpec((8, HP), lambda g, i: (0, g))),
        compiler_params=pltpu.CompilerParams(dimension_semantics=("parallel", "arbitrary")),
    )(dm, y, act, u, dskip, gain)


def _loss_head(y, target):
    T, D = y.shape
    tt = _pick(T, 512)

    def body(y_ref, t_ref, dy_ref, dyb_ref, l_ref):
        e = y_ref[...] - t_ref[...]
        dy_ref[...] = e * (1.0 / D)
        dyb_ref[...] = (e * (1.0 / D)).astype(dyb_ref.dtype)

        @pl.when(pl.program_id(0) == 0)
        def _():
            l_ref[...] = jnp.zeros_like(l_ref)

        l_ref[...] += jnp.sum(e * e) * (0.5 / D)

    blk = pl.BlockSpec((tt, D), lambda i: (i, 0))
    return pl.pallas_call(
        body, name="loss_head",
        out_shape=(jax.ShapeDtypeStruct((T, D), F32), jax.ShapeDtypeStruct((T, D), BF), jax.ShapeDtypeStruct((8, 128), F32)),
        grid=(T // tt,), in_specs=[blk, blk], out_specs=(blk, blk, pl.BlockSpec((8, 128), lambda i: (0, 0))),
        compiler_params=pltpu.CompilerParams(dimension_semantics=("arbitrary",)),
    )(y, target)


def _adamw(w, g, m, v, *, name):
    R, C = w.shape
    cap = max(8, (1 << 18) // C)
    tr = R
    if R % 8 == 0:
        tr = 8
        for cand in range(8, min(R, cap) + 1, 8):
            if R % cand == 0:
                tr = cand

    def body(w_ref, g_ref, m_ref, v_ref, d_ref, nm_ref, nv_ref):
        gg = g_ref[...]
        nm = ADAM_B1 * m_ref[...] + (1.0 - ADAM_B1) * gg
        nv = ADAM_B2 * v_ref[...] + (1.0 - ADAM_B2) * jnp.square(gg)
        m_hat = nm / (1.0 - ADAM_B1 ** ADAM_STEP)
        v_hat = nv / (1.0 - ADAM_B2 ** ADAM_STEP)
        d_ref[...] = -ADAM_LR * (m_hat / (jnp.sqrt(v_hat) + ADAM_EPS) + ADAM_WD * w_ref[...])
        nm_ref[...] = nm
        nv_ref[...] = nv

    blk = pl.BlockSpec((tr, C), lambda i: (i, 0))
    return pl.pallas_call(
        body, name=name, out_shape=(jax.ShapeDtypeStruct((R, C), F32),) * 3, grid=(R // tr,),
        in_specs=[blk] * 4, out_specs=(blk,) * 3,
    )(w, g, m, v)


ANY = pl.BlockSpec(memory_space=pl.ANY)


def _chip_peers():
    x, y, c = lax.axis_index("x"), lax.axis_index("y"), lax.axis_index("c")
    return x, y, c, [(1 - x, y), (x, 1 - y), (1 - x, 1 - y)]


def _half_rows(c, rh):
    return pl.ds(pl.multiple_of(c * rh, 16), rh)


def _my_chip():
    return 2 * lax.axis_index("x") + lax.axis_index("y")


def _gather_chips(wb, wf):
    rh = wb.shape[0] // 2

    def body(wb_ref, wf_ref, ob_ref, of_ref, send_sems, recv_sems):
        x, y, c, peers = _chip_peers()
        me = 2 * x + y
        half, other = _half_rows(c, rh), _half_rows(1 - c, rh)

        def chip_copy(k, slot):
            px, py = peers[k]
            return pltpu.make_async_remote_copy(
                src_ref=wb_ref.at[half], dst_ref=ob_ref.at[slot, half], send_sem=send_sems.at[k], recv_sem=recv_sems.at[k],
                device_id=(px, py, c), device_id_type=MESH)

        def passed_on(k, slot, rows):
            return pltpu.make_async_remote_copy(
                src_ref=ob_ref.at[slot, rows], dst_ref=ob_ref.at[slot, rows], send_sem=send_sems.at[3 + k],
                recv_sem=recv_sems.at[3 + k], device_id=(x, y, 1 - c), device_id_type=MESH)

        def small_copy(k, slot):
            px, py = peers[k]
            return pltpu.make_async_remote_copy(
                src_ref=wf_ref, dst_ref=of_ref.at[slot], send_sem=send_sems.at[6 + k], recv_sem=recv_sems.at[6 + k],
                device_id=(px, py, c), device_id_type=MESH)

        sends = [chip_copy(k, me) for k in range(3)] + [small_copy(k, me) for k in range(3)]
        for cp in sends:
            cp.start()
        chip_of = [2 * px + py for px, py in peers]
        for k in range(3):
            chip_copy(k, chip_of[k]).wait_recv()
            cp = passed_on(k, chip_of[k], half)
            cp.start()
            sends.append(cp)
        for k in range(3):
            passed_on(k, chip_of[k], other).wait_recv()
            small_copy(k, chip_of[k]).wait_recv()
        for cp in sends:
            cp.wait_send()

    ob, of = pl.pallas_call(
        body, name="gather_weights",
        out_shape=(jax.ShapeDtypeStruct((4,) + wb.shape, wb.dtype), jax.ShapeDtypeStruct((4,) + wf.shape, wf.dtype)),
        in_specs=[ANY, ANY], out_specs=(ANY, ANY),
        scratch_shapes=[pltpu.SemaphoreType.DMA((9,)), pltpu.SemaphoreType.DMA((9,))],
    )(wb, wf)
    me = _my_chip()
    return lax.dynamic_update_slice(ob, wb[None], (me, 0, 0)), lax.dynamic_update_slice(of, wf[None], (me, 0, 0))


def _halves_to_sibling(gp):
    rh = gp.shape[1] // 2

    def body(gp_ref, o_ref, send_sem, recv_sem):
        x, y, c = lax.axis_index("x"), lax.axis_index("y"), lax.axis_index("c")
        cp = pltpu.make_async_remote_copy(src_ref=gp_ref.at[:, _half_rows(1 - c, rh), :], dst_ref=o_ref, send_sem=send_sem,
                                          recv_sem=recv_sem, device_id=(x, y, 1 - c), device_id_type=MESH)
        cp.start()
        cp.wait()

    return pl.pallas_call(
        body, name="halves_to_sibling", out_shape=jax.ShapeDtypeStruct((gp.shape[0], rh, gp.shape[2]), gp.dtype),
        in_specs=[ANY], out_specs=ANY, scratch_shapes=[pltpu.SemaphoreType.DMA, pltpu.SemaphoreType.DMA],
    )(gp)


def _row_tile(rows, cap=1024):
    tr = 16
    for cand in range(16, cap + 1, 16):
        if rows % cand == 0:
            tr = cand
    return tr


def _add_halves(gp, sib, core):
    n, rh, C = sib.shape
    tr = _row_tile(rh)
    nt = rh // tr

    def body(c_ref, g_ref, s_ref, o_ref):
        o_ref[...] = (g_ref[...].astype(F32) + s_ref[...].astype(F32)).astype(o_ref.dtype)

    blk = pl.BlockSpec((1, tr, C), lambda j, i, c: (j, i, 0))
    return pl.pallas_call(
        body, name="add_halves", out_shape=jax.ShapeDtypeStruct(sib.shape, sib.dtype),
        grid_spec=pltpu.PrefetchScalarGridSpec(
            num_scalar_prefetch=1, grid=(n, nt),
            in_specs=[pl.BlockSpec((1, tr, C), lambda j, i, c: (j, c[0] * nt + i, 0)), blk], out_specs=blk),
    )(core, gp, sib)


def _join_halves(mine):
    rh = mine.shape[0]

    def body(m_ref, o_ref, send_sem, recv_sem):
        x, y, c = lax.axis_index("x"), lax.axis_index("y"), lax.axis_index("c")
        half, other = _half_rows(c, rh), _half_rows(1 - c, rh)

        def copy(rows):
            return pltpu.make_async_remote_copy(src_ref=m_ref, dst_ref=o_ref.at[rows], send_sem=send_sem, recv_sem=recv_sem,
                                                device_id=(x, y, 1 - c), device_id_type=MESH)

        send = copy(half)
        send.start()
        copy(other).wait_recv()
        send.wait_send()

    out = pl.pallas_call(
        body, name="join_halves", out_shape=jax.ShapeDtypeStruct((2 * rh, mine.shape[1]), mine.dtype),
        in_specs=[ANY], out_specs=ANY, scratch_shapes=[pltpu.SemaphoreType.DMA, pltpu.SemaphoreType.DMA],
    )(mine)
    return lax.dynamic_update_slice(out, mine, (lax.axis_index("c") * rh, 0))


def _exchange_chips(gp):
    def body(gp_ref, out_ref, send_sems, recv_sems):
        x, y, c, peers = _chip_peers()
        me = 2 * x + y

        def copies(sending):
            out = []
            for k, (px, py) in enumerate(peers):
                p = 2 * px + py
                out.append(pltpu.make_async_remote_copy(
                    src_ref=gp_ref.at[p], dst_ref=out_ref.at[me if sending else p],
                    send_sem=send_sems.at[k], recv_sem=recv_sems.at[k], device_id=(px, py, c), device_id_type=MESH))
            return out

        sends = copies(True)
        for cp in sends:
            cp.start()
        for cp in copies(False):
            cp.wait_recv()
        for cp in sends:
            cp.wait_send()

    out = pl.pallas_call(
        body, name="exchange_grads", out_shape=jax.ShapeDtypeStruct(gp.shape, gp.dtype),
        in_specs=[ANY], out_specs=ANY,
        scratch_shapes=[pltpu.SemaphoreType.DMA((3,)), pltpu.SemaphoreType.DMA((3,))],
    )(gp)
    me = _my_chip()
    return lax.dynamic_update_slice(out, lax.dynamic_slice_in_dim(gp, me, 1, axis=0), (me, 0, 0))


def _sum_slots(r4):
    _, R, C = r4.shape
    tr = _row_tile(R)

    def body(r_ref, o_ref):
        acc = r_ref[0].astype(F32)
        for s in range(1, 4):
            acc = acc + r_ref[s].astype(F32)
        o_ref[...] = acc

    return pl.pallas_call(
        body, name="sum_slots", out_shape=jax.ShapeDtypeStruct((R, C), F32), grid=(R // tr,),
        in_specs=[pl.BlockSpec((4, tr, C), lambda i: (0, i, 0))], out_specs=pl.BlockSpec((tr, C), lambda i: (i, 0)),
    )(r4)


N_DEV = 8


def _allreduce_small(p):
    rs = p.shape[0]

    def body(x_ref, sum_ref, all_ref, send_sems, recv_sems, local_sem):
        x, y, c = lax.axis_index("x"), lax.axis_index("y"), lax.axis_index("c")
        me, sibling = (x, y, c), (x, y, 1 - c)
        chips = [(1 - x, y), (x, 1 - y), (1 - x, 1 - y)]

        def rows(px, py, pc):
            return all_ref.at[pl.ds((4 * px + 2 * py + pc) * rs, rs), :]

        def copy(k, block, to, src=None):
            return pltpu.make_async_remote_copy(
                src_ref=rows(*block) if src is None else src, dst_ref=rows(*block),
                send_sem=send_sems.at[k], recv_sem=recv_sems.at[k], device_id=to, device_id_type=MESH)

        mine = pltpu.make_async_copy(x_ref, rows(*me), local_sem)
        mine.start()
        first = [copy(0, me, sibling, src=x_ref)]
        first += [copy(1 + j, me, (*chip, c), src=x_ref) for j, chip in enumerate(chips)]
        for cp in first:
            cp.start()
        passed = [copy(4 + j, (*chip, c), sibling) for j, chip in enumerate(chips)]
        for j, chip in enumerate(chips):
            copy(1 + j, (*chip, c), me).wait_recv()
            passed[j].start()
        copy(0, sibling, me).wait_recv()
        for j, chip in enumerate(chips):
            copy(4 + j, (*chip, 1 - c), me).wait_recv()
        for cp in first + passed:
            cp.wait_send()
        mine.wait()
        acc = all_ref[0:rs, :]
        for d in range(1, N_DEV):
            acc = acc + all_ref[d * rs:(d + 1) * rs, :]
        sum_ref[...] = acc

    vmem = pl.BlockSpec(memory_space=pltpu.VMEM)
    return pl.pallas_call(
        body, name="allreduce_small", out_shape=jax.ShapeDtypeStruct((rs, 128), F32),
        in_specs=[vmem], out_specs=vmem,
        scratch_shapes=[pltpu.VMEM((N_DEV * rs, 128), F32), pltpu.SemaphoreType.DMA((7,)), pltpu.SemaphoreType.DMA((7,)),
                        pltpu.SemaphoreType.DMA],
    )(p)


WEIGHTS = ('ffn1_norm', 'ffn1_w_gate', 'ffn1_w_up', 'ffn1_w_down', 'mix_norm', 'w_in', 'q_a_norm', 'w_q_b',
           'kv_a_norm', 'w_kv_b', 'q_head_norm', 'k_head_norm', 'conv_w', 'conv_b', 'a_log_fwd', 'a_log_bwd',
           'dt_bias_fwd', 'dt_bias_bwd', 'd_skip', 'ssm_norm', 'w_attn_branch', 'w_ssm_branch', 'w_out',
           'ffn2_norm', 'ffn2_w_gate', 'ffn2_w_up', 'ffn2_w_down')
PACKED = (('ffn1_w_gate', (D_MODEL, D_FF), 1), ('ffn1_w_up', (D_MODEL, D_FF), 1), ('ffn1_w_down', (D_FF, D_MODEL), 0),
          ('w_in', (D_MODEL, sum(IN_SPLITS)), 1), ('w_q_b', (Q_LORA, N_HEADS * QK_HEAD), 1),
          ('w_kv_b', (KV_LORA, N_HEADS * (QK_NOPE + V_HEAD)), 1),
          ('w_attn_branch', (N_HEADS * V_HEAD, D_MODEL), 0), ('w_ssm_branch', (D_INNER, D_MODEL), 0),
          ('w_out', (D_MODEL, D_MODEL), 0),
          ('ffn2_w_gate', (D_MODEL, D_FF), 1), ('ffn2_w_up', (D_MODEL, D_FF), 1), ('ffn2_w_down', (D_FF, D_MODEL), 0))
PACK_W = 1024
N_CHIPS = 4
SMALL = (('ffn1_norm', 1024), ('mix_norm', 1024), ('q_a_norm', 384), ('kv_a_norm', 256), ('q_head_norm', 96),
         ('k_head_norm', 96), ('conv_b', 3072), ('a_log_fwd', 32), ('a_log_bwd', 32), ('dt_bias_fwd', 32),
         ('dt_bias_bwd', 32), ('d_skip', 32), ('ssm_norm', 2048), ('ffn2_norm', 1024),
         ('conv_w', CONV_WIDTH * XBC_DIM), ('loss', 1))


TRANSPOSED = ('ffn1_w_gate', 'ffn1_w_up', 'w_in', 'ffn2_w_gate', 'ffn2_w_up')


def _stored(name, a):
    return a.T if name in TRANSPOSED else a


def _shard_shape(name, shape, axis):
    sh = tuple(s // N_CHIPS if a == axis else s for a, s in enumerate(shape))
    return sh[::-1] if name in TRANSPOSED else sh


def _by_rows(name, axis):
    return name in TRANSPOSED or axis == 0


def _pack_layout():
    out, r = {}, 0
    for name, shape, axis in PACKED:
        n = math.prod(shape) // N_CHIPS // PACK_W
        out[name] = (r, n)
        r += n
    return out, -(-r // 32) * 32


def _pack(shards):
    layout, rows = _pack_layout()
    parts = [shards[name].reshape(-1, PACK_W) for name, _, _ in PACKED]
    parts.append(jnp.zeros((rows - sum(p.shape[0] for p in parts), PACK_W), parts[0].dtype))
    return jnp.concatenate(parts, axis=0)


def _unpack(packed):
    layout, _ = _pack_layout()
    return {name: packed[layout[name][0]:layout[name][0] + layout[name][1]].reshape(_shard_shape(name, shape, axis))
            for name, shape, axis in PACKED}


def _full_from_slots(slots):
    layout, _ = _pack_layout()
    out = {}
    for name, shape, axis in PACKED:
        r, n = layout[name]
        if _by_rows(name, axis):
            out[name] = slots[:, r:r + n].reshape(N_CHIPS * n, PACK_W)
        else:
            sh = _shard_shape(name, shape, axis)
            out[name] = jnp.concatenate([slots[j, r:r + n].reshape(sh) for j in range(N_CHIPS)], axis=axis)
    return out


def _slots_from_full(full):
    layout, rows = _pack_layout()
    parts = []
    for name, shape, axis in PACKED:
        r, n = layout[name]
        if _by_rows(name, axis):
            parts.append(full[name].reshape(N_CHIPS, n, PACK_W))
        else:
            size = shape[axis] // N_CHIPS
            parts.append(jnp.stack([lax.slice_in_dim(full[name], j * size, (j + 1) * size, axis=axis).reshape(n, PACK_W)
                                    for j in range(N_CHIPS)]))
    parts.append(jnp.zeros((N_CHIPS, rows - sum(p.shape[1] for p in parts), PACK_W), parts[0].dtype))
    return jnp.concatenate(parts, axis=1)


def _pack_small(vals):
    parts = []
    for name, n in SMALL:
        pad = -(-n // 128) * 128 - n
        parts.append(jnp.pad(vals[name].reshape(-1).astype(F32), (0, pad)).reshape(-1, 128))
    rows = sum(p.shape[0] for p in parts)
    parts.append(jnp.zeros((-(-rows // 8) * 8 - rows, 128), F32))
    return jnp.concatenate(parts, axis=0)


def _unpack_small(packed):
    out, r = {}, 0
    for name, n in SMALL:
        k = -(-n // 128)
        out[name] = packed[r:r + k].reshape(-1)[:n]
        r += k
    return out


def _pad_heads(w, axis, per_head, lo, hi):
    shape = w.shape
    w = w.reshape(shape[:axis] + (N_HEADS, per_head) + shape[axis + 1:])
    w = lax.slice_in_dim(w, lo, hi, axis=axis + 1)
    pad = [(0, 0)] * w.ndim
    pad[axis + 1] = (0, HP - (hi - lo))
    w = jnp.pad(w, pad)
    return w.reshape(shape[:axis] + (N_HEADS * HP,) + shape[axis + 1:])


def _unpad_heads(w, axis, keep):
    shape = w.shape
    w = w.reshape(shape[:axis] + (N_HEADS, HP) + shape[axis + 1:])
    return lax.slice_in_dim(w, 0, keep, axis=axis + 1)


def _pad_w_in(wt):
    o = [0]
    for s in IN_SPLITS:
        o.append(o[-1] + s)
    cq, ckv, kpe, z, xbc, dtf, dtb, ga, gb = [wt[o[i]:o[i + 1]] for i in range(len(IN_SPLITS))]
    kpe_pad = jnp.pad(kpe, ((QK_NOPE, HP - QK_HEAD), (0, 0)))
    dt_pad = jnp.pad(jnp.concatenate([dtf, dtb], axis=0), ((0, HP - 2 * SSM_HEADS), (0, 0)))
    return jnp.concatenate([z, ga, gb, xbc, cq, ckv, kpe_pad, dt_pad], axis=0)


def _unpad_w_in(gt):
    z, ga, gb, xbc = gt[U_Z:U_GA], gt[U_GA:U_GB], gt[U_GB:U_XBC], gt[U_XBC:U_SMALL]
    s = gt[U_SMALL:]
    cq, ckv = s[S_CQ:S_CKV], s[S_CKV:S_KPE]
    kpe = s[S_KPE + QK_NOPE:S_KPE + QK_HEAD]
    dtf, dtb = s[S_DT:S_DT + SSM_HEADS], s[S_DT + SSM_HEADS:S_DT + 2 * SSM_HEADS]
    return jnp.concatenate([cq, ckv, kpe, z, xbc, dtf, dtb, ga, gb], axis=0)


def _lanes128(parts):
    row = jnp.concatenate([p.reshape(-1) for p in parts])
    return jnp.pad(row, (0, HP - row.shape[0])).reshape(1, HP)


FF_TILE = D_FF // 2
WGRAD = BF


def _ffn_fwd(x, g, wg_t, wu_t, wd, tag):
    h = _rms_fwd(x, g, name=tag + "_norm")
    gate, up, act = _mm([h], [wg_t, wu_t], name=tag + "_up", tb=True, out_dtypes=(F32, F32, BF), tm=512, tn=FF_TILE,
                        epilogue=lambda a, b: (a, b, _silu(a) * b))
    out = _mm([act], [wd], name=tag + "_down", extras=[x], epilogue=lambda acc, r: (r + 0.5 * acc,))
    return out, (h, gate, up, act)


def _ffn_bwd(dout, dout_bf, x, g, wg_t, wu_t, wd, saved, tag):
    h, gate, up, act = saved
    dgate, dup = _mm([dout_bf], [wd], name=tag + "_down_dx", tb=True, extras=[gate, up], out_dtypes=(BF, BF),
                     tm=512, tn=FF_TILE, epilogue=lambda acc, a, b: (0.5 * acc * b * _dsilu(a), 0.5 * acc * _silu(a)))
    dwd = _mm([act], [dout_bf], name=tag + "_down_dw", ta=True, tm=FF_TILE, tk=1024, out_dtypes=(WGRAD,),
              epilogue=lambda acc: (0.5 * acc,))
    dwg_t, dwu_t = _mm([dgate, dup], [h, h], name=tag + "_up_dw", ta=True, separate=True, out_dtypes=(WGRAD, WGRAD),
                       tm=FF_TILE, tk=1024)
    dh = _mm([dgate, dup], [wg_t, wu_t], name=tag + "_up_dx")
    dx, dx_bf, dg = _rms_bwd(dh, x, g, name=tag + "_norm_bwd", add=dout, out_dtypes=(F32, BF))
    return dx, dx_bf, dg, dwg_t, dwu_t, dwd


KPE_BLK = (U_SMALL + S_KPE) // HP
SMALL_BLK = U_SMALL // SMALL_W


def _local_step(x, pos_col, target, W, P):
    T = x.shape[0]
    sig = jax.nn.sigmoid
    x1, ffn1 = _ffn_fwd(x, P["ffn1_norm"], W["wg1"], W["wu1"], W["wd1"], "ffn1")
    h = _rms_fwd(x1, P["mix_norm"], name="mix_norm")
    u = _mm([h], [W["w_in"]], name="in_proj", tb=True, tn=1152)
    cqn = _rms_fwd(u, P["q_a_norm"], name="q_a_norm", blk_w=SMALL_W, blk_idx=SMALL_BLK, off=S_CQ, width=Q_LORA)
    ckvn = _rms_fwd(u, P["kv_a_norm"], name="kv_a_norm", blk_w=SMALL_W, blk_idx=SMALL_BLK, off=S_CKV, width=KV_LORA)
    q_raw = _mm([cqn], [W["wq"]], name="q_proj")
    k_raw, v = _mm([ckvn], [W["wk"], W["wv"]], name="kv_proj", out_dtypes=(F32, BF))
    rc, rs = _rope_tables(pos_col, P["freq"])
    q = _qk_prep_fwd(q_raw, None, P["q_head_norm"], rc, rs, name="q_prep", out_scale=Q_SCALE)
    k = _qk_prep_fwd(k_raw, u, P["k_head_norm"], rc, rs, name="k_prep", kpe_blk=KPE_BLK)
    o, lse = _attn_fwd(q, k, v)
    pre, act = _conv_fwd(u, P["conv_w8"], P["conv_b"])
    scan_cols, scan_rows = _ssd_prep(u, P["dt_bias8"], P["a_log8"])
    y_f, st_f = _ssd_fwd(act, scan_cols, scan_rows, rev=False, name="ssd_fwd_f")
    y_b, st_b = _ssd_fwd(act, scan_cols, scan_rows, rev=True, name="ssd_fwd_b")
    ysum, m = _ssm_combine_fwd(y_f, y_b, act, u, P["d_skip_lanes"], P["ssm_norm"])
    ab = _mm([o], [W["pa"]], name="attn_branch")
    mb, merged = _mm([m], [W["pb"]], name="ssm_branch", extras=[ab, u, u], extra_offs=(0, U_GA, U_GB), out_dtypes=(F32, BF),
                     epilogue=lambda acc, a, ga, gb: (acc, sig(ga) * a + sig(gb) * acc))
    x2 = _mm([merged], [W["wo"]], name="out_proj", extras=[x1], epilogue=lambda acc, r: (r + acc,))
    y, ffn2 = _ffn_fwd(x2, P["ffn2_norm"], W["wg2"], W["wu2"], W["wd2"], "ffn2")
    dy, dy_bf, loss = _loss_head(y, target)
    dx2, dx2_bf, dg_ffn2, dwg2, dwu2, dwd2 = _ffn_bwd(dy, dy_bf, x2, P["ffn2_norm"], W["wg2"], W["wu2"], W["wd2"], ffn2,
                                                      "ffn2")

    def gate_bwd(dmrg, a, b, ga, gb):
        sa, sb = sig(ga), sig(gb)
        return dmrg * sa, dmrg * sb, dmrg * a * sa * (1.0 - sa), dmrg * b * sb * (1.0 - sb)

    dab, dmb, dga, dgb = _mm([dx2_bf], [W["wo"]], name="out_proj_dx", tb=True, extras=[ab, mb, u, u],
                             extra_offs=(0, 0, U_GA, U_GB), out_dtypes=(BF,) * 4, epilogue=gate_bwd)
    dwo = _mm([merged], [dx2_bf], name="out_proj_dw", ta=True, out_dtypes=(WGRAD,))
    dpa = _mm([o], [dab], name="attn_branch_dw", ta=True, out_dtypes=(WGRAD,))
    do = _mm([dab], [W["pa"]], name="attn_branch_dx", tb=True)
    dpb = _mm([m], [dmb], name="ssm_branch_dw", ta=True, out_dtypes=(WGRAD,))
    dm = _mm([dmb], [W["pb"]], name="ssm_branch_dx", tb=True)
    dyssd, dz, dxs_skip, dg_ssm, dskip = _ssm_combine_bwd(dm, ysum, act, u, P["d_skip_lanes"], P["ssm_norm"])
    dxs_f, db_f, dc_f, dsel_f, dtot_f = _ssd_bwd(act, scan_cols, scan_rows, st_f, dyssd, rev=False, name="ssd_bwd_f")
    dxs_b, db_b, dc_b, dsel_b, dtot_b = _ssd_bwd(act, scan_cols, scan_rows, st_b, dyssd, rev=True, name="ssd_bwd_b")
    ddt, dalog, dbias = _ssd_prep_bwd(u, P["dt_bias8"], P["a_log8"], dsel_f, dtot_f, dsel_b, dtot_b)
    dxbc, dconv = [], []
    for tag, col0, parts in (("x", 0, [dxs_f, dxs_b, dxs_skip]), ("b", D_INNER, [db_f, db_b]),
                             ("c", D_INNER + SSM_GROUPS * D_STATE, [dc_f, dc_b])):
        dpre = _conv_dpre(parts, pre, col0, name="conv_dpre_" + tag)
        dxp, dwp = _conv_bwd(dpre, u, P["conv_w8"], col0, name="conv_bwd_" + tag)
        dxbc.append(dxp)
        dconv.append(dwp)
    dconv = jnp.concatenate(dconv, axis=1)
    dq, dk, dv = _attn_bwd(q, k, v, do, o, lse)
    dq_raw, dg_qh = _qk_prep_bwd(dq, q_raw, None, P["q_head_norm"], rc, rs, name="q_prep_bwd", in_scale=ATTN_SCALE)
    dk_raw, dg_kh, dkpe = _qk_prep_bwd(dk, k_raw, u, P["k_head_norm"], rc, rs, name="k_prep_bwd", kpe_blk=KPE_BLK,
                                       in_scale=1.0 / LOG2E)
    dwq = _mm([cqn], [dq_raw], name="q_proj_dw", ta=True, out_dtypes=(WGRAD,))
    dcqn = _mm([dq_raw], [W["wq"]], name="q_proj_dx", tb=True)
    dwk, dwv = _mm([ckvn], [dk_raw, dv], name="kv_proj_dw", ta=True, out_dtypes=(WGRAD, WGRAD))
    dckvn = _mm([dk_raw, dv], [W["wk"], W["wv"]], name="kv_proj_dx", tb=True)
    dcq, dg_qa = _rms_bwd(dcqn, u, P["q_a_norm"], name="q_a_norm_bwd", blk_w=SMALL_W, blk_idx=SMALL_BLK, off=S_CQ,
                          width=Q_LORA, out_dtypes=(BF,))
    dckv, dg_kva = _rms_bwd(dckvn, u, P["kv_a_norm"], name="kv_a_norm_bwd", blk_w=SMALL_W, blk_idx=SMALL_BLK,
                            off=S_CKV, width=KV_LORA, out_dtypes=(BF,))
    du = jnp.concatenate([dz, dga, dgb] + dxbc + [dcq, dckv, dkpe.astype(BF), ddt.astype(BF)], axis=1)
    dw_in = _mm([du], [h], name="in_proj_dw", ta=True, tm=1152, out_dtypes=(WGRAD,))
    dh = _mm([du], [W["w_in"]], name="in_proj_dx")
    dx1, dx1_bf, dg_mix = _rms_bwd(dh, x1, P["mix_norm"], name="mix_norm_bwd", add=dx2, out_dtypes=(F32, BF))
    dx, _, dg_ffn1, dwg1, dwu1, dwd1 = _ffn_bwd(dx1, dx1_bf, x, P["ffn1_norm"], W["wg1"], W["wu1"], W["wd1"], ffn1, "ffn1")
    dW = dict(wg1=dwg1, wu1=dwu1, wd1=dwd1, w_in=dw_in, wq=dwq, wk=dwk, wv=dwv, pa=dpa, pb=dpb, wo=dwo,
              wg2=dwg2, wu2=dwu2, wd2=dwd2)
    dP = dict(ffn1_norm=dg_ffn1[0], mix_norm=dg_mix[0], q_a_norm=dg_qa[0], kv_a_norm=dg_kva[0],
              q_head_norm=dg_qh[0, :QK_HEAD], k_head_norm=dg_kh[0, :QK_HEAD], conv_b=dconv[CONV_WIDTH],
              a_log_fwd=dalog[0, :SSM_HEADS], a_log_bwd=dalog[0, SSM_HEADS:2 * SSM_HEADS],
              dt_bias_fwd=dbias[0, :SSM_HEADS], dt_bias_bwd=dbias[0, SSM_HEADS:2 * SSM_HEADS],
              d_skip=dskip[0].reshape(SSM_GROUPS, HP)[:, :HG], ssm_norm=dg_ssm[0], ffn2_norm=dg_ffn2[0],
              conv_w=dconv[:CONV_WIDTH], loss=loss[0, 0])
    return dx, dW, dP


def _prepare(w, conv_w_full):
    kvb = w["w_kv_b"]
    W = dict(wg1=w["ffn1_w_gate"], wu1=w["ffn1_w_up"], wd1=w["ffn1_w_down"], w_in=_pad_w_in(w["w_in"]),
             wq=_pad_heads(w["w_q_b"], 1, QK_HEAD, 0, QK_HEAD),
             wk=_pad_heads(kvb, 1, QK_NOPE + V_HEAD, 0, QK_NOPE),
             wv=_pad_heads(kvb, 1, QK_NOPE + V_HEAD, QK_NOPE, QK_NOPE + V_HEAD),
             pa=_pad_heads(w["w_attn_branch"], 0, V_HEAD, 0, V_HEAD), pb=w["w_ssm_branch"], wo=w["w_out"],
             wg2=w["ffn2_w_gate"], wu2=w["ffn2_w_up"], wd2=w["ffn2_w_down"])
    inv_freq = [1.0 / (ROPE_BASE ** (j / QK_ROPE)) for j in range(0, QK_ROPE, 2)]
    freq = [0.0] * QK_NOPE + inv_freq + inv_freq + [0.0] * (HP - QK_HEAD)
    P = {n: w[n] for n in ("ffn1_norm", "mix_norm", "q_a_norm", "kv_a_norm", "ssm_norm", "ffn2_norm", "conv_b")}
    P.update(q_head_norm=_lanes128([w["q_head_norm"]]), k_head_norm=_lanes128([w["k_head_norm"]]),
             conv_w8=jnp.pad(conv_w_full, ((0, 8 - CONV_WIDTH), (0, 0))),
             dt_bias8=jnp.broadcast_to(_lanes128([w["dt_bias_fwd"], w["dt_bias_bwd"]]), (8, HP)),
             a_log8=jnp.broadcast_to(_lanes128([w["a_log_fwd"], w["a_log_bwd"]]), (8, HP)),
             d_skip_lanes=jnp.repeat(w["d_skip"].reshape(-1), PH).reshape(1, D_INNER),
             freq=jnp.asarray(freq, F32).reshape(1, HP))
    return W, P


def _unprepare(dW):
    dkvb = jnp.concatenate([_unpad_heads(dW["wk"], 1, QK_NOPE), _unpad_heads(dW["wv"], 1, V_HEAD)], axis=2)
    return dict(ffn1_w_gate=dW["wg1"], ffn1_w_up=dW["wu1"], ffn1_w_down=dW["wd1"], w_in=_unpad_w_in(dW["w_in"]),
                w_q_b=_unpad_heads(dW["wq"], 1, QK_HEAD).reshape(Q_LORA, N_HEADS * QK_HEAD),
                w_kv_b=dkvb.reshape(KV_LORA, N_HEADS * (QK_NOPE + V_HEAD)),
                w_attn_branch=_unpad_heads(dW["pa"], 0, V_HEAD).reshape(N_HEADS * V_HEAD, D_MODEL),
                w_ssm_branch=dW["pb"], w_out=dW["wo"],
                ffn2_w_gate=dW["wg2"], ffn2_w_up=dW["wu2"], ffn2_w_down=dW["wd2"])


def kernel(x, positions, ffn1_norm, ffn1_w_gate, ffn1_w_up, ffn1_w_down, mix_norm, w_in, q_a_norm, w_q_b, kv_a_norm, w_kv_b, q_head_norm, k_head_norm, conv_w, conv_b, a_log_fwd, a_log_bwd, dt_bias_fwd, dt_bias_bwd, d_skip, ssm_norm, w_attn_branch, w_ssm_branch, w_out, ffn2_norm, ffn2_w_gate, ffn2_w_up, ffn2_w_down, loss_target, m_ffn1_norm, m_ffn1_w_gate, m_ffn1_w_up, m_ffn1_w_down, m_mix_norm, m_w_in, m_q_a_norm, m_w_q_b, m_kv_a_norm, m_w_kv_b, m_q_head_norm, m_k_head_norm, m_conv_w, m_conv_b, m_a_log_fwd, m_a_log_bwd, m_dt_bias_fwd, m_dt_bias_bwd, m_d_skip, m_ssm_norm, m_w_attn_branch, m_w_ssm_branch, m_w_out, m_ffn2_norm, m_ffn2_w_gate, m_ffn2_w_up, m_ffn2_w_down, v_ffn1_norm, v_ffn1_w_gate, v_ffn1_w_up, v_ffn1_w_down, v_mix_norm, v_w_in, v_q_a_norm, v_w_q_b, v_kv_a_norm, v_w_kv_b, v_q_head_norm, v_k_head_norm, v_conv_w, v_conv_b, v_a_log_fwd, v_a_log_bwd, v_dt_bias_fwd, v_dt_bias_bwd, v_d_skip, v_ssm_norm, v_w_attn_branch, v_w_ssm_branch, v_w_out, v_ffn2_norm, v_ffn2_w_gate, v_ffn2_w_up, v_ffn2_w_down):
    given = dict(locals())
    T = x.shape[1]
    packed_names = [name for name, _, _ in PACKED]

    def two_d(a):
        return a.reshape(a.shape[1], -1) if a.ndim > 2 else a

    def kept(n, a):
        return _stored(n, two_d(a))

    w_loc = {n: kept(n, given[n]) for n in WEIGHTS}
    wb = _pack({n: w_loc[n].astype(BF) for n in packed_names})
    wf = jnp.pad(w_loc["conv_w"], ((0, 8 - CONV_WIDTH), (0, 0)))
    gb, gf = _gather_chips(wb, wf)
    full = _full_from_slots(gb)
    conv_w_full = jnp.concatenate([gf[j, :CONV_WIDTH] for j in range(N_CHIPS)], axis=1)
    full.update({n: w_loc[n] for n in WEIGHTS if n not in full and n != "conv_w"})
    W, P = _prepare(full, conv_w_full)
    dx, dW, dP = _local_step(x.reshape(T, D_MODEL), positions.reshape(T, 1).astype(F32), loss_target.reshape(T, D_MODEL), W, P)
    gp = _slots_from_full(_unprepare(dW))
    core = lax.axis_index("c").astype(jnp.int32).reshape(1)
    both_cores = _add_halves(gp, _halves_to_sibling(gp), core)
    grads = _unpack(_join_halves(_sum_slots(_exchange_chips(both_cores))))
    small = _unpack_small(_allreduce_small(_pack_small(dP)))
    grads.update({n: small[n].reshape(1, -1) for n, _ in SMALL if n not in ("conv_w", "loss")})
    grads["conv_w"] = lax.dynamic_slice_in_dim(small["conv_w"].reshape(CONV_WIDTH, XBC_DIM), _my_chip() * (XBC_DIM // N_CHIPS),
                                               XBC_DIM // N_CHIPS, axis=1)
    out_g, out_d, out_m, out_v = [], [], [], []
    for n in WEIGHTS:
        shape = given[n].shape
        delta, new_m, new_v = _adamw(w_loc[n], grads[n], kept(n, given["m_" + n]), kept(n, given["v_" + n]), name="adamw_" + n)
        for outs, a in ((out_g, grads[n]), (out_d, delta), (out_m, new_m), (out_v, new_v)):
            outs.append(_stored(n, a).reshape(shape))
    return (small["loss"].reshape(()), dx.reshape(x.shape), *out_g, *out_d, *out_m, *out_v)
```

```python
import functools
import math

import jax
import jax.numpy as jnp
from jax import lax
from jax.experimental import pallas as pl
from jax.experimental.pallas import tpu as pltpu

BF = jnp.bfloat16
F32 = jnp.float32
HI = lax.Precision.HIGHEST
MESH = pl.DeviceIdType.MESH

D_MODEL = 1024
D_FF = 2816
EPS = 1e-6
N_HEADS = 16
QK_NOPE = 64
QK_ROPE = 32
QK_HEAD = 96
V_HEAD = 64
Q_LORA = 384
KV_LORA = 256
ROPE_BASE = 10000.0
D_INNER = 2048
SSM_HEADS = 32
SSM_GROUPS = 4
D_STATE = 128
CONV_WIDTH = 5
CHUNK = 128
XBC_DIM = 3072
HP = 128
GW = D_INNER // SSM_GROUPS
HG = SSM_HEADS // SSM_GROUPS
PH = 64
U_Z, U_GA, U_GB, U_XBC, U_SMALL = 0, 2048, 3072, 4096, 7168
S_CQ, S_CKV, S_KPE, S_DT, SMALL_W = 0, 384, 640, 768, 896
U_PAD = U_SMALL + SMALL_W
IN_SPLITS = (Q_LORA, KV_LORA, QK_ROPE, D_INNER, XBC_DIM, SSM_HEADS, SSM_HEADS, D_MODEL, D_MODEL)

ADAM_LR = 0.001
ADAM_B1 = 0.9
ADAM_B2 = 0.999
ADAM_EPS = 1e-08
ADAM_WD = 0.01
ADAM_STEP = 10

NN = (((1,), (0,)), ((), ()))
NT = (((1,), (1,)), ((), ()))
TN = (((0,), (0,)), ((), ()))


def _pick(n, pref):
    best = None
    d = 128
    while d <= min(n, pref):
        if n % d == 0:
            best = d
        d += 128
    return best if best is not None else n


def _silu(x):
    return x * jax.nn.sigmoid(x)


def _dsilu(x):
    s = jax.nn.sigmoid(x)
    return s * (1.0 + x * (1.0 - s))


def _softplus(x):
    return jnp.maximum(x, 0.0) + jnp.log(1.0 + jnp.exp(-jnp.abs(x)))


def _mm(As, Bs, *, name, ta=False, tb=False, out_dtypes=(F32,), epilogue=None, extras=(), extra_offs=None,
        tm=1024, tn=512, tk=2048, separate=False):
    As, Bs, extras = list(As), list(Bs), list(extras)
    a0, b0 = As[0], Bs[0]
    M, K = (a0.shape[1], a0.shape[0]) if ta else a0.shape
    N = b0.shape[0] if tb else b0.shape[1]
    tm, tn, tk = _pick(M, tm), _pick(N, tn), _pick(K, tk)
    nk = K // tk
    n_a, n_b, n_e, n_o = len(As), len(Bs), len(extras), len(out_dtypes)
    n_acc = (n_b if n_a == 1 or separate else 1) if nk > 1 else 0
    if extra_offs is None:
        extra_offs = (0,) * n_e
    dn = (((0,) if ta else (1,), (1,) if tb else (0,)), ((), ()))
    bytes_a = sum(a.size * a.dtype.itemsize for a in As)
    bytes_b = sum(b.size * b.dtype.itemsize for b in Bs)
    n_outer = (N // tn) * bytes_a + bytes_b < (M // tm) * bytes_b + bytes_a

    def products(a_refs, b_refs):
        if n_a == 1:
            a = a_refs[0][...].astype(BF)
            return [lax.dot_general(a, b[...].astype(BF), dn, preferred_element_type=F32) for b in b_refs]
        if separate:
            return [lax.dot_general(a[...].astype(BF), b[...].astype(BF), dn, preferred_element_type=F32)
                    for a, b in zip(a_refs, b_refs)]
        total = None
        for a, b in zip(a_refs, b_refs):
            p = lax.dot_general(a[...].astype(BF), b[...].astype(BF), dn, preferred_element_type=F32)
            total = p if total is None else total + p
        return [total]

    def finish(accs, e_refs, o_refs):
        ex = [e[...] for e in e_refs]
        outs = epilogue(*accs, *ex) if epilogue is not None else tuple(accs)
        for o_ref, val in zip(o_refs, outs):
            o_ref[...] = val.astype(o_ref.dtype)

    def body(*refs):
        a_refs, b_refs = refs[:n_a], refs[n_a:n_a + n_b]
        e_refs = refs[n_a + n_b:n_a + n_b + n_e]
        o_refs = refs[n_a + n_b + n_e:n_a + n_b + n_e + n_o]
        acc_refs = refs[n_a + n_b + n_e + n_o:]
        if nk == 1:
            finish(products(a_refs, b_refs), e_refs, o_refs)
            return
        k = pl.program_id(2)

        @pl.when(k == 0)
        def _():
            for acc in acc_refs:
                acc[...] = jnp.zeros_like(acc)

        for acc, p in zip(acc_refs, products(a_refs, b_refs)):
            acc[...] += p

        @pl.when(k == nk - 1)
        def _():
            finish([acc[...] for acc in acc_refs], e_refs, o_refs)

    def at(f):
        return (lambda j, i, k: f(i, j, k)) if n_outer else f

    a_spec = pl.BlockSpec((tk, tm), at(lambda i, j, k: (k, i))) if ta else pl.BlockSpec((tm, tk), at(lambda i, j, k: (i, k)))
    b_spec = pl.BlockSpec((tn, tk), at(lambda i, j, k: (j, k))) if tb else pl.BlockSpec((tk, tn), at(lambda i, j, k: (k, j)))
    e_specs = [pl.BlockSpec((tm, tn), at(functools.partial(lambda i, j, k, o: (i, j + o), o=off // tn))) for off in extra_offs]
    for off in extra_offs:
        assert off % tn == 0
    outs = pl.pallas_call(
        body, name=name,
        out_shape=tuple(jax.ShapeDtypeStruct((M, N), dt) for dt in out_dtypes),
        grid=(N // tn, M // tm, nk) if n_outer else (M // tm, N // tn, nk),
        in_specs=[a_spec] * n_a + [b_spec] * n_b + e_specs,
        out_specs=tuple(pl.BlockSpec((tm, tn), at(lambda i, j, k: (i, j))) for _ in out_dtypes),
        scratch_shapes=[pltpu.VMEM((tm, tn), F32)] * n_acc,
        compiler_params=pltpu.CompilerParams(dimension_semantics=("parallel", "parallel", "arbitrary")),
    )(*As, *Bs, *extras)
    return outs[0] if n_o == 1 else outs


def _rms_fwd(x, g, *, name, blk_w=None, blk_idx=0, off=0, width=None, out_dtype=BF):
    T = x.shape[0]
    blk_w = x.shape[1] if blk_w is None else blk_w
    width = blk_w if width is None else width
    tt = _pick(T, 512)

    def body(x_ref, g_ref, o_ref):
        xf = x_ref[:, off:off + width]
        r = lax.rsqrt(jnp.mean(xf * xf, axis=-1, keepdims=True) + EPS)
        o_ref[...] = (xf * r * g_ref[...]).astype(o_ref.dtype)

    return pl.pallas_call(
        body, name=name, out_shape=jax.ShapeDtypeStruct((T, width), out_dtype), grid=(T // tt,),
        in_specs=[pl.BlockSpec((tt, blk_w), lambda i: (i, blk_idx)), pl.BlockSpec((1, width), lambda i: (0, 0))],
        out_specs=pl.BlockSpec((tt, width), lambda i: (i, 0)),
    )(x, g)


def _rms_bwd(dy, x, g, *, name, blk_w=None, blk_idx=0, off=0, width=None, add=None, out_dtypes=(F32,)):
    T = x.shape[0]
    blk_w = x.shape[1] if blk_w is None else blk_w
    width = blk_w if width is None else width
    tt = _pick(T, 512)
    has_add = add is not None
    n_dx = len(out_dtypes)

    def body(*refs):
        dy_ref, x_ref, g_ref = refs[:3]
        dx_refs, dg_ref = refs[3 + has_add:3 + has_add + n_dx], refs[-1]
        xf = x_ref[:, off:off + width]
        d = dy_ref[...].astype(F32)
        r = lax.rsqrt(jnp.mean(xf * xf, axis=-1, keepdims=True) + EPS)
        gd = d * g_ref[...]
        dx = r * gd - xf * (r * r * r) * jnp.mean(gd * xf, axis=-1, keepdims=True)
        if has_add:
            dx = dx + refs[3][...]
        for dx_ref in dx_refs:
            dx_ref[...] = dx.astype(dx_ref.dtype)

        @pl.when(pl.program_id(0) == 0)
        def _():
            dg_ref[...] = jnp.zeros_like(dg_ref)

        dg_ref[...] += jnp.broadcast_to(jnp.sum(d * xf * r, axis=0, keepdims=True), dg_ref.shape)

    row = pl.BlockSpec((tt, width), lambda i: (i, 0))
    in_specs = [row, pl.BlockSpec((tt, blk_w), lambda i: (i, blk_idx)), pl.BlockSpec((1, width), lambda i: (0, 0))]
    args = [dy, x, g]
    if has_add:
        in_specs.append(row)
        args.append(add)
    return pl.pallas_call(
        body, name=name,
        out_shape=tuple(jax.ShapeDtypeStruct((T, width), dt) for dt in out_dtypes) + (jax.ShapeDtypeStruct((8, width), F32),),
        grid=(T // tt,), in_specs=in_specs,
        out_specs=(row,) * n_dx + (pl.BlockSpec((8, width), lambda i: (0, 0)),),
        compiler_params=pltpu.CompilerParams(dimension_semantics=("arbitrary",)),
    )(*args)


def _rope_tables(pos_col, freq_lane):
    T = pos_col.shape[0]
    tt = _pick(T, 512)

    def body(p_ref, f_ref, c_ref, s_ref):
        ang = p_ref[...] * f_ref[...]
        lane = lax.broadcasted_iota(jnp.int32, ang.shape, 1)
        c_ref[...] = jnp.where(lane < QK_HEAD, jnp.cos(ang), 0.0)
        sn = jnp.sin(ang)
        s_ref[...] = jnp.where((lane >= QK_NOPE) & (lane < QK_NOPE + 16), -sn,
                               jnp.where((lane >= QK_NOPE + 16) & (lane < QK_HEAD), sn, 0.0))

    return pl.pallas_call(
        body, name="rope_tables", out_shape=(jax.ShapeDtypeStruct((T, HP), F32),) * 2, grid=(T // tt,),
        in_specs=[pl.BlockSpec((tt, 1), lambda i: (i, 0)), pl.BlockSpec((1, HP), lambda i: (0, 0))],
        out_specs=(pl.BlockSpec((tt, HP), lambda i: (i, 0)),) * 2,
    )(pos_col, freq_lane)


def _swap_rope_halves(n):
    src = lax.broadcasted_iota(jnp.int32, (HP, HP), 0)
    dst = lax.broadcasted_iota(jnp.int32, (HP, HP), 1)
    lo = (dst >= QK_NOPE) & (dst < QK_NOPE + 16) & (src == dst + 16)
    hi = (dst >= QK_NOPE + 16) & (dst < QK_HEAD) & (src == dst - 16)
    return _split_dot(n, jnp.where(lo | hi, 1.0, 0.0).astype(BF), 2)


def _qk_prep_fwd(raw, kpe, gain, C, S, *, name, kpe_blk=0, out_scale=1.0):
    T = raw.shape[0]
    tt = _pick(T, 256)
    has_kpe = kpe is not None

    def body(*refs):
        if has_kpe:
            raw_ref, kpe_ref, g_ref, c_ref, s_ref, o_ref = refs
        else:
            raw_ref, g_ref, c_ref, s_ref, o_ref = refs
        for h in range(N_HEADS):
            hs = slice(HP * h, HP * (h + 1))
            xr = raw_ref[:, hs] + kpe_ref[...] if has_kpe else raw_ref[:, hs]
            r = lax.rsqrt(jnp.sum(xr * xr, axis=-1, keepdims=True) * (1.0 / QK_HEAD) + EPS)
            n = xr * r * g_ref[...]
            o_ref[:, hs] = ((n * c_ref[...] + _swap_rope_halves(n) * s_ref[...]) * out_scale).astype(o_ref.dtype)

    heads = pl.BlockSpec((tt, N_HEADS * HP), lambda i: (i, 0))
    shared = pl.BlockSpec((tt, HP), lambda i: (i, 0))
    kpe_spec = pl.BlockSpec((tt, HP), lambda i: (i, kpe_blk))
    in_specs = [heads] + ([kpe_spec] if has_kpe else []) + [pl.BlockSpec((1, HP), lambda i: (0, 0)), shared, shared]
    args = [raw] + ([kpe] if has_kpe else []) + [gain, C, S]
    return pl.pallas_call(
        body, name=name, out_shape=jax.ShapeDtypeStruct(raw.shape, BF), grid=(T // tt,),
        in_specs=in_specs, out_specs=heads,
    )(*args)


def _qk_prep_bwd(dout, raw, kpe, gain, C, S, *, name, kpe_blk=0, in_scale=1.0):
    T = raw.shape[0]
    tt = _pick(T, 256)
    has_kpe = kpe is not None

    def body(*refs):
        if has_kpe:
            d_ref, raw_ref, kpe_ref, g_ref, c_ref, s_ref, dx_ref, dg_ref, dkpe_ref = refs
        else:
            d_ref, raw_ref, g_ref, c_ref, s_ref, dx_ref, dg_ref = refs
        dg = jnp.zeros((1, HP), F32)
        dkpe = jnp.zeros((tt, HP), F32)
        for h in range(N_HEADS):
            hs = slice(HP * h, HP * (h + 1))
            xr = raw_ref[:, hs] + kpe_ref[...] if has_kpe else raw_ref[:, hs]
            d = d_ref[:, hs].astype(F32) * in_scale
            r = lax.rsqrt(jnp.sum(xr * xr, axis=-1, keepdims=True) * (1.0 / QK_HEAD) + EPS)
            dn = d * c_ref[...] + _swap_rope_halves(d * s_ref[...])
            gd = dn * g_ref[...]
            dx = r * gd - xr * (r * r * r) * (jnp.sum(gd * xr, axis=-1, keepdims=True) * (1.0 / QK_HEAD))
            dx_ref[:, hs] = dx.astype(dx_ref.dtype)
            dg = dg + jnp.sum(dn * xr * r, axis=0, keepdims=True)
            dkpe = dkpe + dx

        @pl.when(pl.program_id(0) == 0)
        def _():
            dg_ref[...] = jnp.zeros_like(dg_ref)

        dg_ref[...] += jnp.broadcast_to(dg, dg_ref.shape)
        if has_kpe:
            dkpe_ref[...] = dkpe

    heads = pl.BlockSpec((tt, N_HEADS * HP), lambda i: (i, 0))
    shared = pl.BlockSpec((tt, HP), lambda i: (i, 0))
    kpe_spec = pl.BlockSpec((tt, HP), lambda i: (i, kpe_blk))
    in_specs = [heads, heads] + ([kpe_spec] if has_kpe else []) + [pl.BlockSpec((1, HP), lambda i: (0, 0)), shared, shared]
    args = [dout, raw] + ([kpe] if has_kpe else []) + [gain, C, S]
    out_shape = [jax.ShapeDtypeStruct(raw.shape, BF), jax.ShapeDtypeStruct((8, HP), F32)]
    out_specs = [heads, pl.BlockSpec((8, HP), lambda i: (0, 0))]
    if has_kpe:
        out_shape.append(jax.ShapeDtypeStruct((T, HP), F32))
        out_specs.append(shared)
    return pl.pallas_call(
        body, name=name, out_shape=tuple(out_shape), grid=(T // tt,),
        in_specs=in_specs, out_specs=tuple(out_specs),
        compiler_params=pltpu.CompilerParams(dimension_semantics=("arbitrary",)),
    )(*args)


ATTN_SCALE = 1.0 / math.sqrt(QK_HEAD)
LOG2E = 1.0 / math.log(2.0)
Q_SCALE = ATTN_SCALE * LOG2E


def _attn_fwd(q, k, v):
    T = q.shape[0]
    tq = _pick(T, 256)

    def body(q_ref, k_ref, v_ref, o_ref, lse_ref):
        s = lax.dot_general(q_ref[...], k_ref[...], NT, preferred_element_type=F32)
        m = jnp.max(s, axis=-1, keepdims=True)
        p = jnp.exp2(s - m)
        l = jnp.sum(p, axis=-1, keepdims=True)
        o = jnp.dot(p.astype(BF), v_ref[...], preferred_element_type=F32)
        o_ref[...] = o / l
        lse_ref[...] = jnp.broadcast_to(m + jnp.log2(l), lse_ref.shape)

    qs = pl.BlockSpec((tq, HP), lambda h, i: (i, h))
    kv = pl.BlockSpec((T, HP), lambda h, i: (0, h))
    return pl.pallas_call(
        body, name="attn_fwd", out_shape=(jax.ShapeDtypeStruct(q.shape, F32),) * 2, grid=(N_HEADS, T // tq),
        in_specs=[qs, kv, kv], out_specs=(qs, qs),
        compiler_params=pltpu.CompilerParams(dimension_semantics=("parallel", "parallel")),
    )(q, k, v)


def _attn_bwd(q, k, v, do, o, lse):
    T = q.shape[0]
    tb = _pick(T, 512)
    nb = T // tb

    def body(q_ref, k_ref, v_ref, do_ref, o_ref, lse_ref, dq_ref, dk_ref, dv_ref, delta_rows, lse_rows, dob_scr):
        dq_ref[...] = jnp.zeros_like(dq_ref)
        lane = lax.broadcasted_iota(jnp.int32, (8, HP), 1)
        ones8 = jnp.ones((8, HP), BF)
        first8 = jnp.where(lane == 0, 1.0, 0.0).astype(BF)

        def as_rows(pick, v):
            total, rest = None, v
            for _ in range(3):
                piece = rest.astype(BF)
                part = lax.dot_general(pick, piece, NT, preferred_element_type=F32)
                total = part if total is None else total + part
                rest = rest - piece.astype(F32)
            return total

        def per_q_tile(i, carry):
            qs = pl.ds(pl.multiple_of(i * tb, tb), tb)
            doi = do_ref[qs, :]
            delta_rows[i] = as_rows(ones8, doi * o_ref[qs, :])
            lse_rows[i] = as_rows(first8, lse_ref[qs, :])
            dob_scr[qs, :] = doi.astype(BF)
            return carry

        lax.fori_loop(0, nb, per_q_tile, 0)

        def k_loop(j, carry):
            ks = pl.ds(pl.multiple_of(j * tb, tb), tb)
            kj, vj = k_ref[ks, :], v_ref[ks, :]

            def q_loop(i, acc):
                dk_acc, dv_acc = acc
                qs = pl.ds(pl.multiple_of(i * tb, tb), tb)
                qi = q_ref[qs, :]
                dob = dob_scr[qs, :]
                s_t = lax.dot_general(kj, qi, NT, preferred_element_type=F32)
                p_t = jnp.exp2(s_t - lse_rows[i, 0:1, :])
                dp_t = lax.dot_general(vj, dob, NT, preferred_element_type=F32)
                ds_t = (p_t * (dp_t - delta_rows[i, 0:1, :])).astype(BF)
                dv_acc = dv_acc + jnp.dot(p_t.astype(BF), dob, preferred_element_type=F32)
                dk_acc = dk_acc + jnp.dot(ds_t, qi, preferred_element_type=F32)
                dq_ref[qs, :] += lax.dot_general(ds_t, kj, TN, preferred_element_type=F32)
                return dk_acc, dv_acc

            zero = jnp.zeros((tb, HP), F32)
            dk_acc, dv_acc = lax.fori_loop(0, nb, q_loop, (zero, zero))
            dk_ref[ks, :] = dk_acc
            dv_ref[ks, :] = dv_acc.astype(dv_ref.dtype)
            return carry

        lax.fori_loop(0, nb, k_loop, 0)

    spec = pl.BlockSpec((T, HP), lambda h: (0, h))
    return pl.pallas_call(
        body, name="attn_bwd",
        out_shape=(jax.ShapeDtypeStruct(q.shape, F32), jax.ShapeDtypeStruct(q.shape, F32), jax.ShapeDtypeStruct(q.shape, BF)),
        grid=(N_HEADS,), in_specs=[spec] * 6, out_specs=(spec,) * 3,
        scratch_shapes=[pltpu.VMEM((nb, 8, tb), F32), pltpu.VMEM((nb, 8, tb), F32), pltpu.VMEM((T, HP), BF)],
        compiler_params=pltpu.CompilerParams(dimension_semantics=("parallel",), vmem_limit_bytes=2 * 15 * T * HP * 2 + (8 << 20)),
    )(q, k, v, do, o, lse)


CONV_TC = 512
CONV_PAD = CONV_WIDTH // 2


def _halo_specs(tr, col_of):
    r8 = tr // 8
    cur = pl.BlockSpec((tr, CONV_TC), lambda j, i: (i, col_of(j)))
    prev = pl.BlockSpec((8, CONV_TC), lambda j, i: (jnp.maximum(i * r8 - 1, 0), col_of(j)))

    def nxt_map(j, i, n8):
        return (jnp.minimum((i + 1) * r8, n8 - 1), col_of(j))

    return cur, prev, nxt_map


def _with_halo(prev_ref, cur_ref, next_ref, i, n_i):
    prev = jnp.where(i == 0, 0.0, prev_ref[...].astype(F32))
    nxt = jnp.where(i == n_i - 1, 0.0, next_ref[...].astype(F32))
    return jnp.concatenate([prev, cur_ref[...].astype(F32), nxt], axis=0)


def _conv_fwd(u, w8, b):
    T = u.shape[0]
    tr = _pick(T, 512)
    n_i = T // tr
    c0 = U_XBC // CONV_TC
    cur, prev, nxt_map = _halo_specs(tr, lambda j: c0 + j)
    nxt = pl.BlockSpec((8, CONV_TC), functools.partial(nxt_map, n8=T // 8))

    def body(p_ref, c_ref, n_ref, w_ref, b_ref, pre_ref, act_ref):
        i = pl.program_id(1)
        full = _with_halo(p_ref, c_ref, n_ref, i, n_i)
        acc = jnp.broadcast_to(b_ref[...], (tr, CONV_TC))
        for kk in range(CONV_WIDTH):
            acc = acc + full[8 - CONV_PAD + kk:8 - CONV_PAD + kk + tr, :] * w_ref[kk:kk + 1, :]
        pre_ref[...] = acc
        act_ref[...] = _silu(acc)

    out = pl.BlockSpec((tr, CONV_TC), lambda j, i: (i, j))
    return pl.pallas_call(
        body, name="conv_fwd", out_shape=(jax.ShapeDtypeStruct((T, XBC_DIM), F32),) * 2,
        grid=(XBC_DIM // CONV_TC, n_i),
        in_specs=[prev, cur, nxt, pl.BlockSpec((8, CONV_TC), lambda j, i: (0, j)), pl.BlockSpec((1, CONV_TC), lambda j, i: (0, j))],
        out_specs=(out, out),
    )(u, u, u, w8, b)


def _conv_dpre(dacts, pre, col0, *, name):
    T, width = dacts[0].shape
    tt = _pick(T, 512)
    n_d = len(dacts)
    c0 = col0 // CONV_TC

    def body(*refs):
        d = refs[0][...]
        for r in refs[1:n_d]:
            d = d + r[...]
        refs[n_d + 1][...] = d * _dsilu(refs[n_d][...])

    blk = pl.BlockSpec((tt, CONV_TC), lambda j, i: (i, j))
    return pl.pallas_call(
        body, name=name, out_shape=jax.ShapeDtypeStruct((T, width), F32), grid=(width // CONV_TC, T // tt),
        in_specs=[blk] * n_d + [pl.BlockSpec((tt, CONV_TC), lambda j, i: (i, c0 + j))], out_specs=blk,
    )(*dacts, pre)


def _conv_bwd(dpre, u, w8, col0, *, name):
    T, width = dpre.shape
    tr = _pick(T, 512)
    n_i = T // tr
    cd = col0 // CONV_TC
    cx = (U_XBC + col0) // CONV_TC
    d_cur, d_prev, d_nxt_map = _halo_specs(tr, lambda j: j)
    x_cur, x_prev, x_nxt_map = _halo_specs(tr, lambda j: cx + j)
    d_nxt = pl.BlockSpec((8, CONV_TC), functools.partial(d_nxt_map, n8=T // 8))
    x_nxt = pl.BlockSpec((8, CONV_TC), functools.partial(x_nxt_map, n8=T // 8))

    def body(dp_ref, dc_ref, dn_ref, xp_ref, xc_ref, xn_ref, w_ref, dx_ref, dw_ref):
        i = pl.program_id(1)
        dfull = _with_halo(dp_ref, dc_ref, dn_ref, i, n_i)
        xfull = _with_halo(xp_ref, xc_ref, xn_ref, i, n_i)
        dcur = dc_ref[...]
        dx = jnp.zeros((tr, CONV_TC), F32)
        rows = []
        for kk in range(CONV_WIDTH):
            dx = dx + dfull[8 + CONV_PAD - kk:8 + CONV_PAD - kk + tr, :] * w_ref[kk:kk + 1, :]
            rows.append(jnp.sum(dcur * xfull[8 - CONV_PAD + kk:8 - CONV_PAD + kk + tr, :], axis=0, keepdims=True))
        rows.append(jnp.sum(dcur, axis=0, keepdims=True))
        rows.append(jnp.zeros((2, CONV_TC), F32))
        dx_ref[...] = dx.astype(dx_ref.dtype)

        @pl.when(i == 0)
        def _():
            dw_ref[...] = jnp.zeros_like(dw_ref)

        dw_ref[...] += jnp.concatenate(rows, axis=0)

    out = pl.BlockSpec((tr, CONV_TC), lambda j, i: (i, j))
    return pl.pallas_call(
        body, name=name, out_shape=(jax.ShapeDtypeStruct((T, width), BF), jax.ShapeDtypeStruct((8, width), F32)),
        grid=(width // CONV_TC, n_i),
        in_specs=[d_prev, d_cur, d_nxt, x_prev, x_cur, x_nxt, pl.BlockSpec((8, CONV_TC), lambda j, i: (0, cd + j))],
        out_specs=(out, pl.BlockSpec((8, CONV_TC), lambda j, i: (0, j))),
        compiler_params=pltpu.CompilerParams(dimension_semantics=("parallel", "arbitrary")),
    )(dpre, dpre, dpre, u, u, u, w8)


N_HB = 2 * SSM_GROUPS
P_DT, P_CS, P_E, P_W = 0, HP, 2 * HP, 3 * HP
DT_BLK = (U_SMALL + S_DT) // HP


def _tri(rev, transpose=False):
    rows = lax.broadcasted_iota(jnp.int32, (CHUNK, CHUNK), 0)
    cols = lax.broadcasted_iota(jnp.int32, (CHUNK, CHUNK), 1)
    if transpose:
        rows, cols = cols, rows
    return (cols >= rows) if rev else (cols <= rows)


def _ssd_prep(u, bias8, alog8):
    T = u.shape[0]
    nc = T // CHUNK

    def body(dt_ref, bias_ref, a_ref, cols_ref, rows_ref):
        lane = lax.broadcasted_iota(jnp.int32, (CHUNK, HP), 1)
        dt = _softplus(dt_ref[...] + bias_ref[0:1, :])
        da = dt * (-jnp.exp(a_ref[0:1, :]))
        cs_f = jnp.dot(jnp.where(_tri(False), 1.0, 0.0).astype(F32), da, precision=HI, preferred_element_type=F32)
        cs_b = jnp.dot(jnp.where(_tri(True), 1.0, 0.0).astype(F32), da, precision=HI, preferred_element_type=F32)
        cs = jnp.where(lane < SSM_HEADS, cs_f, cs_b)
        tot = jnp.where(lane[0:1] < SSM_HEADS, cs_f[CHUNK - 1:CHUNK, :], cs_b[0:1, :])
        e, w = jnp.exp(cs), jnp.exp(tot - cs)
        tot8 = jnp.broadcast_to(tot, (8, HP))
        etot8 = jnp.exp(tot8)
        for b in range(N_HB):
            down = (HP - HG * b) % HP

            def rolled(v):
                return pltpu.roll(v, down, 1) if down else v

            cols_ref[b, :, P_DT:P_DT + HP] = rolled(dt)
            cs_r = rolled(cs)
            cols_ref[b, :, P_CS:P_CS + HP] = cs_r
            cols_ref[b, :, P_E:P_E + HP] = rolled(e)
            cols_ref[b, :, P_W:P_W + HP] = rolled(w)
            rows_ref[b, 0, 0:8, :] = cs_r.T[0:8, :]
            r8 = lax.broadcasted_iota(jnp.int32, (8, HP), 0)
            rows_ref[b, 0, 8:16, :] = jnp.where(r8 == 0, rolled(tot8), jnp.where(r8 == 1, rolled(etot8), 0.0))

    vec = pl.BlockSpec((8, HP), lambda c: (0, 0))
    return pl.pallas_call(
        body, name="ssd_prep",
        out_shape=(jax.ShapeDtypeStruct((N_HB, T, 4 * HP), F32), jax.ShapeDtypeStruct((N_HB, nc, 16, HP), F32)),
        grid=(nc,), in_specs=[pl.BlockSpec((CHUNK, HP), lambda c: (c, DT_BLK)), vec, vec],
        out_specs=(pl.BlockSpec((N_HB, CHUNK, 4 * HP), lambda c: (0, c, 0)), pl.BlockSpec((N_HB, 1, 16, HP), lambda c: (0, c, 0, 0))),
    )(u, bias8, alog8)


def _ssd_specs(T, rev, bwd):
    nc = T // CHUNK
    fwd_order = (lambda c: nc - 1 - c) if rev else (lambda c: c)
    cm = (lambda c: fwd_order(nc - 1 - c)) if bwd else fwd_order
    hb0 = SSM_GROUPS if rev else 0
    xs = pl.BlockSpec((CHUNK, GW), lambda c, g: (cm(c), g))
    bs = pl.BlockSpec((CHUNK, D_STATE), lambda c, g: (cm(c), D_INNER // D_STATE + g))
    cs = pl.BlockSpec((CHUNK, D_STATE), lambda c, g: (cm(c), (D_INNER + SSM_GROUPS * D_STATE) // D_STATE + g))
    cols = pl.BlockSpec((1, CHUNK, 4 * HP), lambda c, g: (hb0 + g, cm(c), 0))
    rows = pl.BlockSpec((1, 1, 16, HP), lambda c, g: (hb0 + g, cm(c), 0, 0))
    return nc, cm, xs, bs, cs, cols, rows


def _head_lanes(to_heads):
    shape = (GW, HP) if to_heads else (HP, GW)
    wide = lax.broadcasted_iota(jnp.int32, shape, 0 if to_heads else 1)
    head = lax.broadcasted_iota(jnp.int32, shape, 1 if to_heads else 0)
    return jnp.where((wide >= PH * head) & (wide < PH * (head + 1)), 1.0, 0.0).astype(BF)


def _split_dot(v, m, terms):
    total, rest = None, v
    for _ in range(terms):
        piece = rest.astype(BF)
        part = jnp.dot(piece, m, preferred_element_type=F32)
        total = part if total is None else total + part
        rest = rest - piece.astype(F32)
    return total


def _spread_cols(cols_ref, rows_ref):
    spread = _head_lanes(False)
    dt_e = _split_dot(cols_ref[0, :, P_DT:P_DT + HP], spread, 3)
    e_e = _split_dot(cols_ref[0, :, P_E:P_E + HP], spread, 3)
    w_e = _split_dot(cols_ref[0, :, P_W:P_W + HP], spread, 3)
    etot_e = _split_dot(rows_ref[0, 0, 8:16, :], spread, 3)[1:2, :]
    return dt_e, e_e, w_e, etot_e


def _decay(cols_ref, rows_ref, hh, incl, transpose=False):
    col = cols_ref[0, :, P_CS + hh:P_CS + hh + 1]
    row = rows_ref[0, 0, hh:hh + 1, :]
    return jnp.where(incl, jnp.exp(row - col if transpose else col - row), 0.0)


def _ssd_fwd(act, cols, rows, *, rev, name):
    T = act.shape[0]
    nc, cm, xs_s, b_s, c_s, cols_s, rows_s = _ssd_specs(T, rev, False)

    def body(x_ref, b_ref, c_ref, cols_ref, rows_ref, y_ref, st_ref, state):
        c, g = pl.program_id(0), pl.program_id(1)

        @pl.when(c == 0)
        def _():
            state[g] = jnp.zeros((D_STATE, GW), F32)

        incl = _tri(rev)
        bm, cmat = b_ref[...].astype(BF), c_ref[...].astype(BF)
        bm_t = b_ref[...].T.astype(BF)
        cb = lax.dot_general(cmat, bm, NT, preferred_element_type=F32)
        dt_e, e_e, w_e, etot_e = _spread_cols(cols_ref, rows_ref)
        prev_all = state[g]
        st_ref[...] = prev_all
        xdt = x_ref[...] * dt_e
        xdt_b = xdt.astype(BF)
        yo_all = jnp.dot(cmat, prev_all.astype(BF), preferred_element_type=F32) * e_e
        state[g] = prev_all * etot_e + jnp.dot(bm_t, (xdt * w_e).astype(BF), preferred_element_type=F32)
        for hh in range(HG):
            hs = slice(PH * hh, PH * (hh + 1))
            lmat = _decay(cols_ref, rows_ref, hh, incl)
            yd = jnp.dot((cb * lmat).astype(BF), xdt_b[:, hs], preferred_element_type=F32)
            y_ref[:, hs] = yd + yo_all[:, hs]

    return pl.pallas_call(
        body, name=name,
        out_shape=(jax.ShapeDtypeStruct((T, D_INNER), F32), jax.ShapeDtypeStruct((nc * D_STATE, D_INNER), F32)),
        grid=(nc, SSM_GROUPS), in_specs=[xs_s, b_s, c_s, cols_s, rows_s], out_specs=(xs_s, xs_s),
        scratch_shapes=[pltpu.VMEM((SSM_GROUPS, D_STATE, GW), F32)],
        compiler_params=pltpu.CompilerParams(dimension_semantics=("arbitrary", "arbitrary")),
    )(act, act, act, cols, rows)


def _ssd_bwd(act, cols, rows, states, dy, *, rev, name):
    T = act.shape[0]
    nc, cm, xs_s, b_s, c_s, cols_s, rows_s = _ssd_specs(T, rev, True)

    def body(x_ref, b_ref, c_ref, cols_ref, rows_ref, st_ref, dy_ref, dx_ref, db_ref, dc_ref, dsel_ref, dtot_ref,
             dstate, dcs_cols, dcs_rows, dcb, dm_scr, dxdt_scr):
        c, g = pl.program_id(0), pl.program_id(1)

        @pl.when(c == 0)
        def _():
            dstate[g] = jnp.zeros((D_STATE, GW), F32)

        incl, incl_t = _tri(rev), _tri(rev, transpose=True)
        bm, cmat = b_ref[...].astype(BF), c_ref[...].astype(BF)
        cm_t = c_ref[...].T.astype(BF)
        cb = lax.dot_general(cmat, bm, NT, preferred_element_type=F32)
        cb_t = lax.dot_general(bm, cmat, NT, preferred_element_type=F32)
        prev_all, ds_all = st_ref[...], dstate[g]
        pb_all, dsb_all = prev_all.astype(BF), ds_all.astype(BF)
        cp_all = jnp.dot(cmat, pb_all, preferred_element_type=F32)
        bds_all = jnp.dot(bm, dsb_all, preferred_element_type=F32)
        dt_e, e_e, w_e, etot_e = _spread_cols(cols_ref, rows_ref)
        to_heads = _head_lanes(True)
        x, dy = x_ref[...], dy_ref[...]
        xdt = x * dt_e
        xdt_b, dy_b = xdt.astype(BF), dy.astype(BF)
        dye_b, xdw_b = (dy * e_e).astype(BF), (xdt * w_e).astype(BF)
        for hh in range(HG):
            hs = slice(PH * hh, PH * (hh + 1))
            mmat_t = cb_t * _decay(cols_ref, rows_ref, hh, incl_t, transpose=True)
            dm_scr[hh] = lax.dot_general(dy_b[:, hs], xdt_b[:, hs], NT, preferred_element_type=F32)
            dxdt_scr[:, hs] = jnp.dot(mmat_t.astype(BF), dy_b[:, hs], preferred_element_type=F32)
        bdsw = bds_all * w_e
        dxdt = dxdt_scr[...] + bdsw
        dx_ref[...] = dxdt * dt_e
        t = _split_dot(xdt * bdsw, to_heads, 2)
        dcs_state = _split_dot(dy * cp_all, to_heads, 2) * cols_ref[0, :, P_E:P_E + HP] - t
        dsel_ref[0, :, 0:HP] = _split_dot(dxdt * x, to_heads, 2)
        sp = _split_dot(jnp.broadcast_to(jnp.sum(ds_all * prev_all, axis=0, keepdims=True), (8, GW)), to_heads, 2)
        dtot_ref[0, 0] = jnp.sum(t, axis=0, keepdims=True) + sp * rows_ref[0, 0, 9:10, :]
        dstate[g] = ds_all * etot_e + jnp.dot(cm_t, dye_b, preferred_element_type=F32)
        dcs_cols[...] = jnp.zeros_like(dcs_cols)
        dcs_rows[...] = jnp.zeros_like(dcs_rows)
        dcb[...] = jnp.zeros_like(dcb)
        for hh in range(HG):
            lmat = _decay(cols_ref, rows_ref, hh, incl)
            dm = dm_scr[hh]
            qm = dm * (cb * lmat)
            dcs_cols[:, hh:hh + 1] = jnp.sum(qm, axis=1, keepdims=True)
            dcs_rows[hh:hh + 1, :] = jnp.sum(qm, axis=0, keepdims=True)
            dcb[...] += dm * lmat
        dcb_all = dcb[...]
        dsel_ref[0, :, HP:2 * HP] = dcs_state + dcs_cols[...] - dcs_rows[...].T
        dc_ref[...] = (lax.dot_general(dye_b, pb_all, NT, preferred_element_type=F32)
                       + jnp.dot(dcb_all.astype(BF), bm, preferred_element_type=F32))
        db_ref[...] = (lax.dot_general(xdw_b, dsb_all, NT, preferred_element_type=F32)
                       + jnp.dot(dcb_all.T.astype(BF), cmat, preferred_element_type=F32))

    bc_out = pl.BlockSpec((CHUNK, D_STATE), lambda c, g: (cm(c), g))
    return pl.pallas_call(
        body, name=name,
        out_shape=(jax.ShapeDtypeStruct((T, D_INNER), F32), jax.ShapeDtypeStruct((T, SSM_GROUPS * D_STATE), F32),
                   jax.ShapeDtypeStruct((T, SSM_GROUPS * D_STATE), F32), jax.ShapeDtypeStruct((SSM_GROUPS, T, 2 * HP), F32),
                   jax.ShapeDtypeStruct((SSM_GROUPS, nc, 8, HP), F32)),
        grid=(nc, SSM_GROUPS), in_specs=[xs_s, b_s, c_s, cols_s, rows_s, xs_s, xs_s],
        out_specs=(xs_s, bc_out, bc_out, pl.BlockSpec((1, CHUNK, 2 * HP), lambda c, g: (g, cm(c), 0)),
                   pl.BlockSpec((1, 1, 8, HP), lambda c, g: (g, cm(c), 0, 0))),
        scratch_shapes=[pltpu.VMEM((SSM_GROUPS, D_STATE, GW), F32), pltpu.VMEM((CHUNK, CHUNK), F32),
                        pltpu.VMEM((CHUNK, CHUNK), F32), pltpu.VMEM((CHUNK, CHUNK), F32),
                        pltpu.VMEM((HG, CHUNK, CHUNK), F32), pltpu.VMEM((CHUNK, GW), F32)],
        compiler_params=pltpu.CompilerParams(dimension_semantics=("arbitrary", "arbitrary")),
    )(act, act, act, cols, rows, states, dy)


def _ssd_prep_bwd(u, bias8, alog8, dsel_f, dtot_f, dsel_b, dtot_b):
    T = u.shape[0]
    nc = T // CHUNK

    def body(dt_ref, bias_ref, a_ref, sf_ref, tf_ref, sb_ref, tb_ref, ddt_ref, da_ref, dbias_ref):
        @pl.when(pl.program_id(0) == 0)
        def _():
            da_ref[...] = jnp.zeros_like(da_ref)
            dbias_ref[...] = jnp.zeros_like(dbias_ref)

        lane = lax.broadcasted_iota(jnp.int32, (CHUNK, HP), 1)
        pre = dt_ref[...] + bias_ref[0:1, :]
        dt = _softplus(pre)
        a = -jnp.exp(a_ref[0:1, :])
        ddt_x, dcs, dtot = jnp.zeros((CHUNK, HP), F32), jnp.zeros((CHUNK, HP), F32), jnp.zeros((8, HP), F32)
        for b in range(N_HB):
            s_ref, t_ref, g = (sf_ref, tf_ref, b) if b < SSM_GROUPS else (sb_ref, tb_ref, b - SSM_GROUPS)
            mine = (lane >= HG * b) & (lane < HG * (b + 1))

            def up(v):
                return pltpu.roll(v, HG * b, 1) if b else v

            ddt_x = ddt_x + jnp.where(mine, up(s_ref[g, :, 0:HP]), 0.0)
            dcs = dcs + jnp.where(mine, up(s_ref[g, :, HP:2 * HP]), 0.0)
            dtot = dtot + jnp.where(mine[0:8], up(t_ref[g, 0]), 0.0)
        tri_f = jnp.where(_tri(False, transpose=True), 1.0, 0.0).astype(F32)
        tri_b = jnp.where(_tri(True, transpose=True), 1.0, 0.0).astype(F32)
        dda = jnp.where(lane < SSM_HEADS, jnp.dot(tri_f, dcs, precision=HI, preferred_element_type=F32),
                        jnp.dot(tri_b, dcs, precision=HI, preferred_element_type=F32)) + dtot[0:1, :]
        dpre = (ddt_x + dda * a) * jax.nn.sigmoid(pre)
        ddt_ref[...] = jnp.where(lane < 2 * SSM_HEADS, dpre, 0.0)
        dbias_ref[...] += jnp.broadcast_to(jnp.sum(dpre, axis=0, keepdims=True), (8, HP))
        da_ref[...] += jnp.broadcast_to(jnp.sum(dda * dt, axis=0, keepdims=True) * a, (8, HP))

    vec = pl.BlockSpec((8, HP), lambda c: (0, 0))
    sel = pl.BlockSpec((SSM_GROUPS, CHUNK, 2 * HP), lambda c: (0, c, 0))
    tot = pl.BlockSpec((SSM_GROUPS, 1, 8, HP), lambda c: (0, c, 0, 0))
    tile = pl.BlockSpec((CHUNK, HP), lambda c: (c, 0))
    return pl.pallas_call(
        body, name="ssd_prep_bwd",
        out_shape=(jax.ShapeDtypeStruct((T, HP), F32), jax.ShapeDtypeStruct((8, HP), F32), jax.ShapeDtypeStruct((8, HP), F32)),
        grid=(nc,), in_specs=[pl.BlockSpec((CHUNK, HP), lambda c: (c, DT_BLK)), vec, vec, sel, tot, sel, tot],
        out_specs=(tile, vec, vec),
        compiler_params=pltpu.CompilerParams(dimension_semantics=("arbitrary",)),
    )(u, bias8, alog8, dsel_f, dtot_f, dsel_b, dtot_b)


def _ssm_combine_fwd(y_f, y_b, act, u, dskip, gain):
    T = y_f.shape[0]
    tt = _pick(T, 256)

    def body(yf_ref, yb_ref, x_ref, z_ref, ds_ref, g_ref, y_ref, m_ref):
        y = yf_ref[...] + yb_ref[...] + ds_ref[...] * x_ref[...]
        y2 = y * _silu(z_ref[...])
        r = lax.rsqrt(jnp.mean(y2 * y2, axis=-1, keepdims=True) + EPS)
        y_ref[...] = y
        m_ref[...] = (y2 * r * g_ref[...]).astype(m_ref.dtype)

    blk = pl.BlockSpec((tt, GW), lambda i, g: (i, g))
    vec = pl.BlockSpec((1, GW), lambda i, g: (0, g))
    return pl.pallas_call(
        body, name="ssm_combine_fwd",
        out_shape=(jax.ShapeDtypeStruct((T, D_INNER), F32), jax.ShapeDtypeStruct((T, D_INNER), BF)),
        grid=(T // tt, SSM_GROUPS), in_specs=[blk, blk, blk, blk, vec, vec], out_specs=(blk, blk),
    )(y_f, y_b, act, u, dskip, gain)


def _ssm_combine_bwd(dm, y, act, u, dskip, gain):
    T = y.shape[0]
    tt = _pick(T, 256)

    def body(dm_ref, y_ref, x_ref, z_ref, ds_ref, g_ref, dy_ref, dz_ref, dxs_ref, dg_ref, dsk_ref):
        z = z_ref[...]
        y = y_ref[...]
        x = x_ref[...]
        sz = _silu(z)
        y2 = y * sz
        r = lax.rsqrt(jnp.mean(y2 * y2, axis=-1, keepdims=True) + EPS)
        d = dm_ref[...]
        gd = d * g_ref[...]
        dy2 = r * gd - y2 * (r * r * r) * jnp.mean(gd * y2, axis=-1, keepdims=True)
        dy = dy2 * sz
        dy_ref[...] = dy
        dz_ref[...] = (dy2 * y * _dsilu(z)).astype(dz_ref.dtype)
        dxs_ref[...] = dy * ds_ref[...]

        @pl.when(pl.program_id(1) == 0)
        def _():
            dg_ref[...] = jnp.zeros_like(dg_ref)
            dsk_ref[...] = jnp.zeros_like(dsk_ref)

        dg_ref[...] += jnp.broadcast_to(jnp.sum(d * y2 * r, axis=0, keepdims=True), dg_ref.shape)
        lane_sum = jnp.broadcast_to(jnp.sum(dy * x, axis=0, keepdims=True), (8, GW))
        src = lax.broadcasted_iota(jnp.int32, (GW, HP), 0)
        head = lax.broadcasted_iota(jnp.int32, (GW, HP), 1)
        to_head = jnp.where((src >= PH * head) & (src < PH * (head + 1)), 1.0, 0.0).astype(F32)
        dsk_ref[...] += jnp.dot(lane_sum, to_head, precision=HI, preferred_element_type=F32)

    blk = pl.BlockSpec((tt, GW), lambda g, i: (i, g))
    vec = pl.BlockSpec((1, GW), lambda g, i: (0, g))
    acc = pl.BlockSpec((8, GW), lambda g, i: (0, g))
    return pl.pallas_call(
        body, name="ssm_combine_bwd",
        out_shape=(jax.ShapeDtypeStruct((T, D_INNER), F32), jax.ShapeDtypeStruct((T, D_INNER), BF),
                   jax.ShapeDtypeStruct((T, D_INNER), F32), jax.ShapeDtypeStruct((8, D_INNER), F32),
                   jax.ShapeDtypeStruct((8, SSM_GROUPS * HP), F32)),
        grid=(SSM_GROUPS, T // tt), in_specs=[blk, blk, blk, blk, vec, vec],
        out_specs=(blk, blk, blk, acc, pl.BlockSpec((8, HP), lambda g, i: (0, g))),
        compiler_params=pltpu.CompilerParams(dimension_semantics=("parallel", "arbitrary")),
    )(dm, y, act, u, dskip, gain)


def _loss_head(y, target):
    T, D = y.shape
    tt = _pick(T, 512)

    def body(y_ref, t_ref, dy_ref, dyb_ref, l_ref):
        e = y_ref[...] - t_ref[...]
        dy_ref[...] = e * (1.0 / D)
        dyb_ref[...] = (e * (1.0 / D)).astype(dyb_ref.dtype)

        @pl.when(pl.program_id(0) == 0)
        def _():
            l_ref[...] = jnp.zeros_like(l_ref)

        l_ref[...] += jnp.sum(e * e) * (0.5 / D)

    blk = pl.BlockSpec((tt, D), lambda i: (i, 0))
    return pl.pallas_call(
        body, name="loss_head",
        out_shape=(jax.ShapeDtypeStruct((T, D), F32), jax.ShapeDtypeStruct((T, D), BF), jax.ShapeDtypeStruct((8, 128), F32)),
        grid=(T // tt,), in_specs=[blk, blk], out_specs=(blk, blk, pl.BlockSpec((8, 128), lambda i: (0, 0))),
        compiler_params=pltpu.CompilerParams(dimension_semantics=("arbitrary",)),
    )(y, target)


def _adamw(w, g, m, v, *, name):
    R, C = w.shape
    cap = max(8, (1 << 18) // C)
    tr = R
    if R % 8 == 0:
        tr = 8
        for cand in range(8, min(R, cap) + 1, 8):
            if R % cand == 0:
                tr = cand

    def body(w_ref, g_ref, m_ref, v_ref, d_ref, nm_ref, nv_ref):
        gg = g_ref[...]
        nm = ADAM_B1 * m_ref[...] + (1.0 - ADAM_B1) * gg
        nv = ADAM_B2 * v_ref[...] + (1.0 - ADAM_B2) * jnp.square(gg)
        m_hat = nm / (1.0 - ADAM_B1 ** ADAM_STEP)
        v_hat = nv / (1.0 - ADAM_B2 ** ADAM_STEP)
        d_ref[...] = -ADAM_LR * (m_hat / (jnp.sqrt(v_hat) + ADAM_EPS) + ADAM_WD * w_ref[...])
        nm_ref[...] = nm
        nv_ref[...] = nv

    blk = pl.BlockSpec((tr, C), lambda i: (i, 0))
    return pl.pallas_call(
        body, name=name, out_shape=(jax.ShapeDtypeStruct((R, C), F32),) * 3, grid=(R // tr,),
        in_specs=[blk] * 4, out_specs=(blk,) * 3,
    )(w, g, m, v)


ANY = pl.BlockSpec(memory_space=pl.ANY)


def _chip_peers():
    x, y, c = lax.axis_index("x"), lax.axis_index("y"), lax.axis_index("c")
    return x, y, c, [(1 - x, y), (x, 1 - y), (1 - x, 1 - y)]


def _half_rows(c, rh):
    return pl.ds(pl.multiple_of(c * rh, 16), rh)


def _my_chip():
    return 2 * lax.axis_index("x") + lax.axis_index("y")


def _gather_chips(wb, wf):
    rh = wb.shape[0] // 2
    rq = rh // 2

    def body(wb_ref, wf_ref, ob_ref, of_ref, send_sems, recv_sems):
        x, y, c, peers = _chip_peers()
        nbr_x, nbr_y = peers[0], peers[1]
        me, chip_x, chip_y, chip_d = 2 * x + y, 2 * (1 - x) + y, 2 * x + (1 - y), 2 * (1 - x) + (1 - y)

        def quarter(core, b):
            return pl.ds(pl.multiple_of(core * rh + b * rq, 16), rq)

        ici = [(0, nbr_x, me, 0, chip_x), (1, nbr_y, me, 1, chip_y), (2, nbr_y, me, 0, chip_y), (3, nbr_x, me, 1, chip_x),
               (4, nbr_y, chip_x, 0, chip_d), (5, nbr_x, chip_y, 1, chip_d)]

        def ici_copy(k, to, slot, b, own):
            rows = quarter(c, b)
            return pltpu.make_async_remote_copy(
                src_ref=wb_ref.at[rows] if own else ob_ref.at[slot, rows], dst_ref=ob_ref.at[slot, rows],
                send_sem=send_sems.at[k], recv_sem=recv_sems.at[k], device_id=(to[0], to[1], c), device_id_type=MESH)

        def to_sibling(k, slot, b, core):
            rows = quarter(core, b)
            return pltpu.make_async_remote_copy(
                src_ref=ob_ref.at[slot, rows], dst_ref=ob_ref.at[slot, rows], send_sem=send_sems.at[6 + k],
                recv_sem=recv_sems.at[6 + k], device_id=(x, y, 1 - c), device_id_type=MESH)

        def small_copy(k, slot):
            px, py = peers[k]
            return pltpu.make_async_remote_copy(
                src_ref=wf_ref, dst_ref=of_ref.at[slot], send_sem=send_sems.at[12 + k], recv_sem=recv_sems.at[12 + k],
                device_id=(px, py, c), device_id_type=MESH)

        sends = [ici_copy(k, to, slot, b, True) for k, to, slot, b, _ in ici[:4]] + [small_copy(k, me) for k in range(3)]
        for cp in sends:
            cp.start()
        for k, to, slot, b, arrives in ici:
            ici_copy(k, to, arrives, b, False).wait_recv()
            passed = [to_sibling(k, arrives, b, c)]
            if k < 2:
                passed.append(ici_copy(*ici[4 + k][:4], False))
            for cp in passed:
                cp.start()
            sends += passed
        for k, to, slot, b, arrives in ici:
            to_sibling(k, arrives, b, 1 - c).wait_recv()
        chip_of = [chip_x, chip_y, chip_d]
        for k in range(3):
            small_copy(k, chip_of[k]).wait_recv()
        for cp in sends:
            cp.wait_send()

    ob, of = pl.pallas_call(
        body, name="gather_weights",
        out_shape=(jax.ShapeDtypeStruct((4,) + wb.shape, wb.dtype), jax.ShapeDtypeStruct((4,) + wf.shape, wf.dtype)),
        in_specs=[ANY, ANY], out_specs=(ANY, ANY),
        scratch_shapes=[pltpu.SemaphoreType.DMA((15,)), pltpu.SemaphoreType.DMA((15,))],
    )(wb, wf)
    me = _my_chip()
    return lax.dynamic_update_slice(ob, wb[None], (me, 0, 0)), lax.dynamic_update_slice(of, wf[None], (me, 0, 0))


def _halves_to_sibling(gp):
    rh = gp.shape[1] // 2

    def body(gp_ref, o_ref, send_sem, recv_sem):
        x, y, c = lax.axis_index("x"), lax.axis_index("y"), lax.axis_index("c")
        cp = pltpu.make_async_remote_copy(src_ref=gp_ref.at[:, _half_rows(1 - c, rh), :], dst_ref=o_ref, send_sem=send_sem,
                                          recv_sem=recv_sem, device_id=(x, y, 1 - c), device_id_type=MESH)
        cp.start()
        cp.wait()

    return pl.pallas_call(
        body, name="halves_to_sibling", out_shape=jax.ShapeDtypeStruct((gp.shape[0], rh, gp.shape[2]), gp.dtype),
        in_specs=[ANY], out_specs=ANY, scratch_shapes=[pltpu.SemaphoreType.DMA, pltpu.SemaphoreType.DMA],
    )(gp)


def _row_tile(rows, cap=1024):
    tr = 16
    for cand in range(16, cap + 1, 16):
        if rows % cand == 0:
            tr = cand
    return tr


def _add_halves(gp, sib, core):
    n, rh, C = sib.shape
    tr = _row_tile(rh)
    nt = rh // tr

    def body(c_ref, g_ref, s_ref, o_ref):
        o_ref[...] = (g_ref[...].astype(F32) + s_ref[...].astype(F32)).astype(o_ref.dtype)

    blk = pl.BlockSpec((1, tr, C), lambda j, i, c: (j, i, 0))
    return pl.pallas_call(
        body, name="add_halves", out_shape=jax.ShapeDtypeStruct(sib.shape, sib.dtype),
        grid_spec=pltpu.PrefetchScalarGridSpec(
            num_scalar_prefetch=1, grid=(n, nt),
            in_specs=[pl.BlockSpec((1, tr, C), lambda j, i, c: (j, c[0] * nt + i, 0)), blk], out_specs=blk),
    )(core, gp, sib)


def _join_halves(mine):
    rh = mine.shape[0]

    def body(m_ref, o_ref, send_sem, recv_sem):
        x, y, c = lax.axis_index("x"), lax.axis_index("y"), lax.axis_index("c")
        half, other = _half_rows(c, rh), _half_rows(1 - c, rh)

        def copy(rows):
            return pltpu.make_async_remote_copy(src_ref=m_ref, dst_ref=o_ref.at[rows], send_sem=send_sem, recv_sem=recv_sem,
                                                device_id=(x, y, 1 - c), device_id_type=MESH)

        send = copy(half)
        send.start()
        copy(other).wait_recv()
        send.wait_send()

    out = pl.pallas_call(
        body, name="join_halves", out_shape=jax.ShapeDtypeStruct((2 * rh, mine.shape[1]), mine.dtype),
        in_specs=[ANY], out_specs=ANY, scratch_shapes=[pltpu.SemaphoreType.DMA, pltpu.SemaphoreType.DMA],
    )(mine)
    return lax.dynamic_update_slice(out, mine, (lax.axis_index("c") * rh, 0))


def _exchange_chips(gp):
    def body(gp_ref, out_ref, send_sems, recv_sems):
        x, y, c, peers = _chip_peers()
        me = 2 * x + y

        def copies(sending):
            out = []
            for k, (px, py) in enumerate(peers):
                p = 2 * px + py
                out.append(pltpu.make_async_remote_copy(
                    src_ref=gp_ref.at[p], dst_ref=out_ref.at[me if sending else p],
                    send_sem=send_sems.at[k], recv_sem=recv_sems.at[k], device_id=(px, py, c), device_id_type=MESH))
            return out

        sends = copies(True)
        for cp in sends:
            cp.start()
        for cp in copies(False):
            cp.wait_recv()
        for cp in sends:
            cp.wait_send()

    out = pl.pallas_call(
        body, name="exchange_grads", out_shape=jax.ShapeDtypeStruct(gp.shape, gp.dtype),
        in_specs=[ANY], out_specs=ANY,
        scratch_shapes=[pltpu.SemaphoreType.DMA((3,)), pltpu.SemaphoreType.DMA((3,))],
    )(gp)
    me = _my_chip()
    return lax.dynamic_update_slice(out, lax.dynamic_slice_in_dim(gp, me, 1, axis=0), (me, 0, 0))


def _sum_slots(r4):
    _, R, C = r4.shape
    tr = _row_tile(R)

    def body(r_ref, o_ref):
        acc = r_ref[0].astype(F32)
        for s in range(1, 4):
            acc = acc + r_ref[s].astype(F32)
        o_ref[...] = acc

    return pl.pallas_call(
        body, name="sum_slots", out_shape=jax.ShapeDtypeStruct((R, C), F32), grid=(R // tr,),
        in_specs=[pl.BlockSpec((4, tr, C), lambda i: (0, i, 0))], out_specs=pl.BlockSpec((tr, C), lambda i: (i, 0)),
    )(r4)


N_DEV = 8


def _allreduce_small(p):
    rs = p.shape[0]

    def body(x_ref, sum_ref, all_ref, send_sems, recv_sems, local_sem):
        x, y, c = lax.axis_index("x"), lax.axis_index("y"), lax.axis_index("c")
        me, sibling = (x, y, c), (x, y, 1 - c)
        chips = [(1 - x, y), (x, 1 - y), (1 - x, 1 - y)]

        def rows(px, py, pc):
            return all_ref.at[pl.ds((4 * px + 2 * py + pc) * rs, rs), :]

        def copy(k, block, to, src=None):
            return pltpu.make_async_remote_copy(
                src_ref=rows(*block) if src is None else src, dst_ref=rows(*block),
                send_sem=send_sems.at[k], recv_sem=recv_sems.at[k], device_id=to, device_id_type=MESH)

        mine = pltpu.make_async_copy(x_ref, rows(*me), local_sem)
        mine.start()
        first = [copy(0, me, sibling, src=x_ref)]
        first += [copy(1 + j, me, (*chip, c), src=x_ref) for j, chip in enumerate(chips)]
        for cp in first:
            cp.start()
        passed = [copy(4 + j, (*chip, c), sibling) for j, chip in enumerate(chips)]
        for j, chip in enumerate(chips):
            copy(1 + j, (*chip, c), me).wait_recv()
            passed[j].start()
        copy(0, sibling, me).wait_recv()
        for j, chip in enumerate(chips):
            copy(4 + j, (*chip, 1 - c), me).wait_recv()
        for cp in first + passed:
            cp.wait_send()
        mine.wait()
        acc = all_ref[0:rs, :]
        for d in range(1, N_DEV):
            acc = acc + all_ref[d * rs:(d + 1) * rs, :]
        sum_ref[...] = acc

    vmem = pl.BlockSpec(memory_space=pltpu.VMEM)
    return pl.pallas_call(
        body, name="allreduce_small", out_shape=jax.ShapeDtypeStruct((rs, 128), F32),
        in_specs=[vmem], out_specs=vmem,
        scratch_shapes=[pltpu.VMEM((N_DEV * rs, 128), F32), pltpu.SemaphoreType.DMA((7,)), pltpu.SemaphoreType.DMA((7,)),
                        pltpu.SemaphoreType.DMA],
    )(p)


WEIGHTS = ('ffn1_norm', 'ffn1_w_gate', 'ffn1_w_up', 'ffn1_w_down', 'mix_norm', 'w_in', 'q_a_norm', 'w_q_b',
           'kv_a_norm', 'w_kv_b', 'q_head_norm', 'k_head_norm', 'conv_w', 'conv_b', 'a_log_fwd', 'a_log_bwd',
           'dt_bias_fwd', 'dt_bias_bwd', 'd_skip', 'ssm_norm', 'w_attn_branch', 'w_ssm_branch', 'w_out',
           'ffn2_norm', 'ffn2_w_gate', 'ffn2_w_up', 'ffn2_w_down')
PACKED = (('ffn1_w_gate', (D_MODEL, D_FF), 1), ('ffn1_w_up', (D_MODEL, D_FF), 1), ('ffn1_w_down', (D_FF, D_MODEL), 0),
          ('w_in', (D_MODEL, sum(IN_SPLITS)), 1), ('w_q_b', (Q_LORA, N_HEADS * QK_HEAD), 1),
          ('w_kv_b', (KV_LORA, N_HEADS * (QK_NOPE + V_HEAD)), 1),
          ('w_attn_branch', (N_HEADS * V_HEAD, D_MODEL), 0), ('w_ssm_branch', (D_INNER, D_MODEL), 0),
          ('w_out', (D_MODEL, D_MODEL), 0),
          ('ffn2_w_gate', (D_MODEL, D_FF), 1), ('ffn2_w_up', (D_MODEL, D_FF), 1), ('ffn2_w_down', (D_FF, D_MODEL), 0))
PACK_W = 1024
N_CHIPS = 4
SMALL = (('ffn1_norm', 1024), ('mix_norm', 1024), ('q_a_norm', 384), ('kv_a_norm', 256), ('q_head_norm', 96),
         ('k_head_norm', 96), ('conv_b', 3072), ('a_log_fwd', 32), ('a_log_bwd', 32), ('dt_bias_fwd', 32),
         ('dt_bias_bwd', 32), ('d_skip', 32), ('ssm_norm', 2048), ('ffn2_norm', 1024),
         ('conv_w', CONV_WIDTH * XBC_DIM), ('loss', 1))


TRANSPOSED = ('ffn1_w_gate', 'ffn1_w_up', 'w_in', 'ffn2_w_gate', 'ffn2_w_up')


def _stored(name, a):
    return a.T if name in TRANSPOSED else a


def _shard_shape(name, shape, axis):
    sh = tuple(s // N_CHIPS if a == axis else s for a, s in enumerate(shape))
    return sh[::-1] if name in TRANSPOSED else sh


def _by_rows(name, axis):
    return name in TRANSPOSED or axis == 0


def _pack_layout():
    out, r = {}, 0
    for name, shape, axis in PACKED:
        n = math.prod(shape) // N_CHIPS // PACK_W
        out[name] = (r, n)
        r += n
    return out, -(-r // 64) * 64


def _pack(shards):
    layout, rows = _pack_layout()
    parts = [shards[name].reshape(-1, PACK_W) for name, _, _ in PACKED]
    parts.append(jnp.zeros((rows - sum(p.shape[0] for p in parts), PACK_W), parts[0].dtype))
    return jnp.concatenate(parts, axis=0)


def _unpack(packed):
    layout, _ = _pack_layout()
    return {name: packed[layout[name][0]:layout[name][0] + layout[name][1]].reshape(_shard_shape(name, shape, axis))
            for name, shape, axis in PACKED}


def _full_from_slots(slots):
    layout, _ = _pack_layout()
    out = {}
    for name, shape, axis in PACKED:
        r, n = layout[name]
        if _by_rows(name, axis):
            out[name] = slots[:, r:r + n].reshape(N_CHIPS * n, PACK_W)
        else:
            sh = _shard_shape(name, shape, axis)
            out[name] = jnp.concatenate([slots[j, r:r + n].reshape(sh) for j in range(N_CHIPS)], axis=axis)
    return out


def _slots_from_full(full):
    layout, rows = _pack_layout()
    parts = []
    for name, shape, axis in PACKED:
        r, n = layout[name]
        if _by_rows(name, axis):
            parts.append(full[name].reshape(N_CHIPS, n, PACK_W))
        else:
            size = shape[axis] // N_CHIPS
            parts.append(jnp.stack([lax.slice_in_dim(full[name], j * size, (j + 1) * size, axis=axis).reshape(n, PACK_W)
                                    for j in range(N_CHIPS)]))
    parts.append(jnp.zeros((N_CHIPS, rows - sum(p.shape[1] for p in parts), PACK_W), parts[0].dtype))
    return jnp.concatenate(parts, axis=1)


def _pack_small(vals):
    parts = []
    for name, n in SMALL:
        pad = -(-n // 128) * 128 - n
        parts.append(jnp.pad(vals[name].reshape(-1).astype(F32), (0, pad)).reshape(-1, 128))
    rows = sum(p.shape[0] for p in parts)
    parts.append(jnp.zeros((-(-rows // 8) * 8 - rows, 128), F32))
    return jnp.concatenate(parts, axis=0)


def _unpack_small(packed):
    out, r = {}, 0
    for name, n in SMALL:
        k = -(-n // 128)
        out[name] = packed[r:r + k].reshape(-1)[:n]
        r += k
    return out


def _pad_heads(w, axis, per_head, lo, hi):
    shape = w.shape
    w = w.reshape(shape[:axis] + (N_HEADS, per_head) + shape[axis + 1:])
    w = lax.slice_in_dim(w, lo, hi, axis=axis + 1)
    pad = [(0, 0)] * w.ndim
    pad[axis + 1] = (0, HP - (hi - lo))
    w = jnp.pad(w, pad)
    return w.reshape(shape[:axis] + (N_HEADS * HP,) + shape[axis + 1:])


def _unpad_heads(w, axis, keep):
    shape = w.shape
    w = w.reshape(shape[:axis] + (N_HEADS, HP) + shape[axis + 1:])
    return lax.slice_in_dim(w, 0, keep, axis=axis + 1)


def _pad_w_in(wt):
    o = [0]
    for s in IN_SPLITS:
        o.append(o[-1] + s)
    cq, ckv, kpe, z, xbc, dtf, dtb, ga, gb = [wt[o[i]:o[i + 1]] for i in range(len(IN_SPLITS))]
    kpe_pad = jnp.pad(kpe, ((QK_NOPE, HP - QK_HEAD), (0, 0)))
    dt_pad = jnp.pad(jnp.concatenate([dtf, dtb], axis=0), ((0, HP - 2 * SSM_HEADS), (0, 0)))
    return jnp.concatenate([z, ga, gb, xbc, cq, ckv, kpe_pad, dt_pad], axis=0)


def _unpad_w_in(gt):
    z, ga, gb, xbc = gt[U_Z:U_GA], gt[U_GA:U_GB], gt[U_GB:U_XBC], gt[U_XBC:U_SMALL]
    s = gt[U_SMALL:]
    cq, ckv = s[S_CQ:S_CKV], s[S_CKV:S_KPE]
    kpe = s[S_KPE + QK_NOPE:S_KPE + QK_HEAD]
    dtf, dtb = s[S_DT:S_DT + SSM_HEADS], s[S_DT + SSM_HEADS:S_DT + 2 * SSM_HEADS]
    return jnp.concatenate([cq, ckv, kpe, z, xbc, dtf, dtb, ga, gb], axis=0)


def _lanes128(parts):
    row = jnp.concatenate([p.reshape(-1) for p in parts])
    return jnp.pad(row, (0, HP - row.shape[0])).reshape(1, HP)


FF_TILE = D_FF // 2
WGRAD = BF


def _ffn_fwd(x, g, wg_t, wu_t, wd, tag):
    h = _rms_fwd(x, g, name=tag + "_norm")
    gate, up, act = _mm([h], [wg_t, wu_t], name=tag + "_up", tb=True, out_dtypes=(F32, F32, BF), tm=512, tn=FF_TILE,
                        epilogue=lambda a, b: (a, b, _silu(a) * b))
    out = _mm([act], [wd], name=tag + "_down", extras=[x], epilogue=lambda acc, r: (r + 0.5 * acc,))
    return out, (h, gate, up, act)


def _ffn_bwd(dout, dout_bf, x, g, wg_t, wu_t, wd, saved, tag):
    h, gate, up, act = saved
    dgate, dup = _mm([dout_bf], [wd], name=tag + "_down_dx", tb=True, extras=[gate, up], out_dtypes=(BF, BF),
                     tm=512, tn=FF_TILE, epilogue=lambda acc, a, b: (0.5 * acc * b * _dsilu(a), 0.5 * acc * _silu(a)))
    dwd = _mm([act], [dout_bf], name=tag + "_down_dw", ta=True, tm=FF_TILE, tk=1024, out_dtypes=(WGRAD,),
              epilogue=lambda acc: (0.5 * acc,))
    dwg_t, dwu_t = _mm([dgate, dup], [h, h], name=tag + "_up_dw", ta=True, separate=True, out_dtypes=(WGRAD, WGRAD),
                       tm=FF_TILE, tk=1024)
    dh = _mm([dgate, dup], [wg_t, wu_t], name=tag + "_up_dx")
    dx, dx_bf, dg = _rms_bwd(dh, x, g, name=tag + "_norm_bwd", add=dout, out_dtypes=(F32, BF))
    return dx, dx_bf, dg, dwg_t, dwu_t, dwd


KPE_BLK = (U_SMALL + S_KPE) // HP
SMALL_BLK = U_SMALL // SMALL_W


def _local_step(x, pos_col, target, W, P):
    T = x.shape[0]
    sig = jax.nn.sigmoid
    x1, ffn1 = _ffn_fwd(x, P["ffn1_norm"], W["wg1"], W["wu1"], W["wd1"], "ffn1")
    h = _rms_fwd(x1, P["mix_norm"], name="mix_norm")
    u = _mm([h], [W["w_in"]], name="in_proj", tb=True, tn=1152)
    cqn = _rms_fwd(u, P["q_a_norm"], name="q_a_norm", blk_w=SMALL_W, blk_idx=SMALL_BLK, off=S_CQ, width=Q_LORA)
    ckvn = _rms_fwd(u, P["kv_a_norm"], name="kv_a_norm", blk_w=SMALL_W, blk_idx=SMALL_BLK, off=S_CKV, width=KV_LORA)
    q_raw = _mm([cqn], [W["wq"]], name="q_proj")
    k_raw, v = _mm([ckvn], [W["wk"], W["wv"]], name="kv_proj", out_dtypes=(F32, BF))
    rc, rs = _rope_tables(pos_col, P["freq"])
    q = _qk_prep_fwd(q_raw, None, P["q_head_norm"], rc, rs, name="q_prep", out_scale=Q_SCALE)
    k = _qk_prep_fwd(k_raw, u, P["k_head_norm"], rc, rs, name="k_prep", kpe_blk=KPE_BLK)
    o, lse = _attn_fwd(q, k, v)
    pre, act = _conv_fwd(u, P["conv_w8"], P["conv_b"])
    scan_cols, scan_rows = _ssd_prep(u, P["dt_bias8"], P["a_log8"])
    y_f, st_f = _ssd_fwd(act, scan_cols, scan_rows, rev=False, name="ssd_fwd_f")
    y_b, st_b = _ssd_fwd(act, scan_cols, scan_rows, rev=True, name="ssd_fwd_b")
    ysum, m = _ssm_combine_fwd(y_f, y_b, act, u, P["d_skip_lanes"], P["ssm_norm"])
    ab = _mm([o], [W["pa"]], name="attn_branch")
    mb, merged = _mm([m], [W["pb"]], name="ssm_branch", extras=[ab, u, u], extra_offs=(0, U_GA, U_GB), out_dtypes=(F32, BF),
                     epilogue=lambda acc, a, ga, gb: (acc, sig(ga) * a + sig(gb) * acc))
    x2 = _mm([merged], [W["wo"]], name="out_proj", extras=[x1], epilogue=lambda acc, r: (r + acc,))
    y, ffn2 = _ffn_fwd(x2, P["ffn2_norm"], W["wg2"], W["wu2"], W["wd2"], "ffn2")
    dy, dy_bf, loss = _loss_head(y, target)
    dx2, dx2_bf, dg_ffn2, dwg2, dwu2, dwd2 = _ffn_bwd(dy, dy_bf, x2, P["ffn2_norm"], W["wg2"], W["wu2"], W["wd2"], ffn2,
                                                      "ffn2")

    def gate_bwd(dmrg, a, b, ga, gb):
        sa, sb = sig(ga), sig(gb)
        return dmrg * sa, dmrg * sb, dmrg * a * sa * (1.0 - sa), dmrg * b * sb * (1.0 - sb)

    dab, dmb, dga, dgb = _mm([dx2_bf], [W["wo"]], name="out_proj_dx", tb=True, extras=[ab, mb, u, u],
                             extra_offs=(0, 0, U_GA, U_GB), out_dtypes=(BF,) * 4, epilogue=gate_bwd)
    dwo = _mm([merged], [dx2_bf], name="out_proj_dw", ta=True, out_dtypes=(WGRAD,))
    dpa = _mm([o], [dab], name="attn_branch_dw", ta=True, out_dtypes=(WGRAD,))
    do = _mm([dab], [W["pa"]], name="attn_branch_dx", tb=True)
    dpb = _mm([m], [dmb], name="ssm_branch_dw", ta=True, out_dtypes=(WGRAD,))
    dm = _mm([dmb], [W["pb"]], name="ssm_branch_dx", tb=True)
    dyssd, dz, dxs_skip, dg_ssm, dskip = _ssm_combine_bwd(dm, ysum, act, u, P["d_skip_lanes"], P["ssm_norm"])
    dxs_f, db_f, dc_f, dsel_f, dtot_f = _ssd_bwd(act, scan_cols, scan_rows, st_f, dyssd, rev=False, name="ssd_bwd_f")
    dxs_b, db_b, dc_b, dsel_b, dtot_b = _ssd_bwd(act, scan_cols, scan_rows, st_b, dyssd, rev=True, name="ssd_bwd_b")
    ddt, dalog, dbias = _ssd_prep_bwd(u, P["dt_bias8"], P["a_log8"], dsel_f, dtot_f, dsel_b, dtot_b)
    dxbc, dconv = [], []
    for tag, col0, parts in (("x", 0, [dxs_f, dxs_b, dxs_skip]), ("b", D_INNER, [db_f, db_b]),
                             ("c", D_INNER + SSM_GROUPS * D_STATE, [dc_f, dc_b])):
        dpre = _conv_dpre(parts, pre, col0, name="conv_dpre_" + tag)
        dxp, dwp = _conv_bwd(dpre, u, P["conv_w8"], col0, name="conv_bwd_" + tag)
        dxbc.append(dxp)
        dconv.append(dwp)
    dconv = jnp.concatenate(dconv, axis=1)
    dq, dk, dv = _attn_bwd(q, k, v, do, o, lse)
    dq_raw, dg_qh = _qk_prep_bwd(dq, q_raw, None, P["q_head_norm"], rc, rs, name="q_prep_bwd", in_scale=ATTN_SCALE)
    dk_raw, dg_kh, dkpe = _qk_prep_bwd(dk, k_raw, u, P["k_head_norm"], rc, rs, name="k_prep_bwd", kpe_blk=KPE_BLK,
                                       in_scale=1.0 / LOG2E)
    dwq = _mm([cqn], [dq_raw], name="q_proj_dw", ta=True, out_dtypes=(WGRAD,))
    dcqn = _mm([dq_raw], [W["wq"]], name="q_proj_dx", tb=True)
    dwk, dwv = _mm([ckvn], [dk_raw, dv], name="kv_proj_dw", ta=True, out_dtypes=(WGRAD, WGRAD))
    dckvn = _mm([dk_raw, dv], [W["wk"], W["wv"]], name="kv_proj_dx", tb=True)
    dcq, dg_qa = _rms_bwd(dcqn, u, P["q_a_norm"], name="q_a_norm_bwd", blk_w=SMALL_W, blk_idx=SMALL_BLK, off=S_CQ,
                          width=Q_LORA, out_dtypes=(BF,))
    dckv, dg_kva = _rms_bwd(dckvn, u, P["kv_a_norm"], name="kv_a_norm_bwd", blk_w=SMALL_W, blk_idx=SMALL_BLK,
                            off=S_CKV, width=KV_LORA, out_dtypes=(BF,))
    du = jnp.concatenate([dz, dga, dgb] + dxbc + [dcq, dckv, dkpe.astype(BF), ddt.astype(BF)], axis=1)
    dw_in = _mm([du], [h], name="in_proj_dw", ta=True, tm=1152, out_dtypes=(WGRAD,))
    dh = _mm([du], [W["w_in"]], name="in_proj_dx")
    dx1, dx1_bf, dg_mix = _rms_bwd(dh, x1, P["mix_norm"], name="mix_norm_bwd", add=dx2, out_dtypes=(F32, BF))
    dx, _, dg_ffn1, dwg1, dwu1, dwd1 = _ffn_bwd(dx1, dx1_bf, x, P["ffn1_norm"], W["wg1"], W["wu1"], W["wd1"], ffn1, "ffn1")
    dW = dict(wg1=dwg1, wu1=dwu1, wd1=dwd1, w_in=dw_in, wq=dwq, wk=dwk, wv=dwv, pa=dpa, pb=dpb, wo=dwo,
              wg2=dwg2, wu2=dwu2, wd2=dwd2)
    dP = dict(ffn1_norm=dg_ffn1[0], mix_norm=dg_mix[0], q_a_norm=dg_qa[0], kv_a_norm=dg_kva[0],
              q_head_norm=dg_qh[0, :QK_HEAD], k_head_norm=dg_kh[0, :QK_HEAD], conv_b=dconv[CONV_WIDTH],
              a_log_fwd=dalog[0, :SSM_HEADS], a_log_bwd=dalog[0, SSM_HEADS:2 * SSM_HEADS],
              dt_bias_fwd=dbias[0, :SSM_HEADS], dt_bias_bwd=dbias[0, SSM_HEADS:2 * SSM_HEADS],
              d_skip=dskip[0].reshape(SSM_GROUPS, HP)[:, :HG], ssm_norm=dg_ssm[0], ffn2_norm=dg_ffn2[0],
              conv_w=dconv[:CONV_WIDTH], loss=loss[0, 0])
    return dx, dW, dP


def _prepare(w, conv_w_full):
    kvb = w["w_kv_b"]
    W = dict(wg1=w["ffn1_w_gate"], wu1=w["ffn1_w_up"], wd1=w["ffn1_w_down"], w_in=_pad_w_in(w["w_in"]),
             wq=_pad_heads(w["w_q_b"], 1, QK_HEAD, 0, QK_HEAD),
             wk=_pad_heads(kvb, 1, QK_NOPE + V_HEAD, 0, QK_NOPE),
             wv=_pad_heads(kvb, 1, QK_NOPE + V_HEAD, QK_NOPE, QK_NOPE + V_HEAD),
             pa=_pad_heads(w["w_attn_branch"], 0, V_HEAD, 0, V_HEAD), pb=w["w_ssm_branch"], wo=w["w_out"],
             wg2=w["ffn2_w_gate"], wu2=w["ffn2_w_up"], wd2=w["ffn2_w_down"])
    inv_freq = [1.0 / (ROPE_BASE ** (j / QK_ROPE)) for j in range(0, QK_ROPE, 2)]
    freq = [0.0] * QK_NOPE + inv_freq + inv_freq + [0.0] * (HP - QK_HEAD)
    P = {n: w[n] for n in ("ffn1_norm", "mix_norm", "q_a_norm", "kv_a_norm", "ssm_norm", "ffn2_norm", "conv_b")}
    P.update(q_head_norm=_lanes128([w["q_head_norm"]]), k_head_norm=_lanes128([w["k_head_norm"]]),
             conv_w8=jnp.pad(conv_w_full, ((0, 8 - CONV_WIDTH), (0, 0))),
             dt_bias8=jnp.broadcast_to(_lanes128([w["dt_bias_fwd"], w["dt_bias_bwd"]]), (8, HP)),
             a_log8=jnp.broadcast_to(_lanes128([w["a_log_fwd"], w["a_log_bwd"]]), (8, HP)),
             d_skip_lanes=jnp.repeat(w["d_skip"].reshape(-1), PH).reshape(1, D_INNER),
             freq=jnp.asarray(freq, F32).reshape(1, HP))
    return W, P


def _unprepare(dW):
    dkvb = jnp.concatenate([_unpad_heads(dW["wk"], 1, QK_NOPE), _unpad_heads(dW["wv"], 1, V_HEAD)], axis=2)
    return dict(ffn1_w_gate=dW["wg1"], ffn1_w_up=dW["wu1"], ffn1_w_down=dW["wd1"], w_in=_unpad_w_in(dW["w_in"]),
                w_q_b=_unpad_heads(dW["wq"], 1, QK_HEAD).reshape(Q_LORA, N_HEADS * QK_HEAD),
                w_kv_b=dkvb.reshape(KV_LORA, N_HEADS * (QK_NOPE + V_HEAD)),
                w_attn_branch=_unpad_heads(dW["pa"], 0, V_HEAD).reshape(N_HEADS * V_HEAD, D_MODEL),
                w_ssm_branch=dW["pb"], w_out=dW["wo"],
                ffn2_w_gate=dW["wg2"], ffn2_w_up=dW["wu2"], ffn2_w_down=dW["wd2"])


def kernel(x, positions, ffn1_norm, ffn1_w_gate, ffn1_w_up, ffn1_w_down, mix_norm, w_in, q_a_norm, w_q_b, kv_a_norm, w_kv_b, q_head_norm, k_head_norm, conv_w, conv_b, a_log_fwd, a_log_bwd, dt_bias_fwd, dt_bias_bwd, d_skip, ssm_norm, w_attn_branch, w_ssm_branch, w_out, ffn2_norm, ffn2_w_gate, ffn2_w_up, ffn2_w_down, loss_target, m_ffn1_norm, m_ffn1_w_gate, m_ffn1_w_up, m_ffn1_w_down, m_mix_norm, m_w_in, m_q_a_norm, m_w_q_b, m_kv_a_norm, m_w_kv_b, m_q_head_norm, m_k_head_norm, m_conv_w, m_conv_b, m_a_log_fwd, m_a_log_bwd, m_dt_bias_fwd, m_dt_bias_bwd, m_d_skip, m_ssm_norm, m_w_attn_branch, m_w_ssm_branch, m_w_out, m_ffn2_norm, m_ffn2_w_gate, m_ffn2_w_up, m_ffn2_w_down, v_ffn1_norm, v_ffn1_w_gate, v_ffn1_w_up, v_ffn1_w_down, v_mix_norm, v_w_in, v_q_a_norm, v_w_q_b, v_kv_a_norm, v_w_kv_b, v_q_head_norm, v_k_head_norm, v_conv_w, v_conv_b, v_a_log_fwd, v_a_log_bwd, v_dt_bias_fwd, v_dt_bias_bwd, v_d_skip, v_ssm_norm, v_w_attn_branch, v_w_ssm_branch, v_w_out, v_ffn2_norm, v_ffn2_w_gate, v_ffn2_w_up, v_ffn2_w_down):
    given = dict(locals())
    T = x.shape[1]
    packed_names = [name for name, _, _ in PACKED]

    def two_d(a):
        return a.reshape(a.shape[1], -1) if a.ndim > 2 else a

    def kept(n, a):
        return _stored(n, two_d(a))

    w_loc = {n: kept(n, given[n]) for n in WEIGHTS}
    wb = _pack({n: w_loc[n].astype(BF) for n in packed_names})
    wf = jnp.pad(w_loc["conv_w"], ((0, 8 - CONV_WIDTH), (0, 0)))
    gb, gf = _gather_chips(wb, wf)
    full = _full_from_slots(gb)
    conv_w_full = jnp.concatenate([gf[j, :CONV_WIDTH] for j in range(N_CHIPS)], axis=1)
    full.update({n: w_loc[n] for n in WEIGHTS if n not in full and n != "conv_w"})
    W, P = _prepare(full, conv_w_full)
    dx, dW, dP = _local_step(x.reshape(T, D_MODEL), positions.reshape(T, 1).astype(F32), loss_target.reshape(T, D_MODEL), W, P)
    gp = _slots_from_full(_unprepare(dW))
    core = lax.axis_index("c").astype(jnp.int32).reshape(1)
    both_cores = _add_halves(gp, _halves_to_sibling(gp), core)
    grads = _unpack(_join_halves(_sum_slots(_exchange_chips(both_cores))))
    small = _unpack_small(_allreduce_small(_pack_small(dP)))
    grads.update({n: small[n].reshape(1, -1) for n, _ in SMALL if n not in ("conv_w", "loss")})
    grads["conv_w"] = lax.dynamic_slice_in_dim(small["conv_w"].reshape(CONV_WIDTH, XBC_DIM), _my_chip() * (XBC_DIM // N_CHIPS),
                                               XBC_DIM // N_CHIPS, axis=1)
    out_g, out_d, out_m, out_v = [], [], [], []
    for n in WEIGHTS:
        shape = given[n].shape
        delta, new_m, new_v = _adamw(w_loc[n], grads[n], kept(n, given["m_" + n]), kept(n, given["v_" + n]), name="adamw_" + n)
        for outs, a in ((out_g, grads[n]), (out_d, delta), (out_m, new_m), (out_v, new_v)):
            outs.append(_stored(n, a).reshape(shape))
    return (small["loss"].reshape(()), dx.reshape(x.shape), *out_g, *out_d, *out_m, *out_v)
```

```python
import functools
import math

import jax
import jax.numpy as jnp
from jax import lax
from jax.experimental import pallas as pl
from jax.experimental.pallas import tpu as pltpu

BF = jnp.bfloat16
F32 = jnp.float32
HI = lax.Precision.HIGHEST
MESH = pl.DeviceIdType.MESH

D_MODEL = 1024
D_FF = 2816
EPS = 1e-6
N_HEADS = 16
QK_NOPE = 64
QK_ROPE = 32
QK_HEAD = 96
V_HEAD = 64
Q_LORA = 384
KV_LORA = 256
ROPE_BASE = 10000.0
D_INNER = 2048
SSM_HEADS = 32
SSM_GROUPS = 4
D_STATE = 128
CONV_WIDTH = 5
CHUNK = 128
XBC_DIM = 3072
HP = 128
GW = D_INNER // SSM_GROUPS
HG = SSM_HEADS // SSM_GROUPS
PH = 64
U_Z, U_GA, U_GB, U_XBC, U_SMALL = 0, 2048, 3072, 4096, 7168
S_CQ, S_CKV, S_KPE, S_DT, SMALL_W = 0, 384, 640, 768, 896
U_PAD = U_SMALL + SMALL_W
IN_SPLITS = (Q_LORA, KV_LORA, QK_ROPE, D_INNER, XBC_DIM, SSM_HEADS, SSM_HEADS, D_MODEL, D_MODEL)

ADAM_LR = 0.001
ADAM_B1 = 0.9
ADAM_B2 = 0.999
ADAM_EPS = 1e-08
ADAM_WD = 0.01
ADAM_STEP = 10

NN = (((1,), (0,)), ((), ()))
NT = (((1,), (1,)), ((), ()))
TN = (((0,), (0,)), ((), ()))


def _pick(n, pref):
    best = None
    d = 128
    while d <= min(n, pref):
        if n % d == 0:
            best = d
        d += 128
    return best if best is not None else n


def _silu(x):
    return x * jax.nn.sigmoid(x)


def _dsilu(x):
    s = jax.nn.sigmoid(x)
    return s * (1.0 + x * (1.0 - s))


def _softplus(x):
    return jnp.maximum(x, 0.0) + jnp.log(1.0 + jnp.exp(-jnp.abs(x)))


def _mm(As, Bs, *, name, ta=False, tb=False, out_dtypes=(F32,), epilogue=None, extras=(), extra_offs=None,
        tm=1024, tn=512, tk=2048, separate=False):
    As, Bs, extras = list(As), list(Bs), list(extras)
    a0, b0 = As[0], Bs[0]
    M, K = (a0.shape[1], a0.shape[0]) if ta else a0.shape
    N = b0.shape[0] if tb else b0.shape[1]
    tm, tn, tk = _pick(M, tm), _pick(N, tn), _pick(K, tk)
    nk = K // tk
    n_a, n_b, n_e, n_o = len(As), len(Bs), len(extras), len(out_dtypes)
    n_acc = (n_b if n_a == 1 or separate else 1) if nk > 1 else 0
    if extra_offs is None:
        extra_offs = (0,) * n_e
    dn = (((0,) if ta else (1,), (1,) if tb else (0,)), ((), ()))
    bytes_a = sum(a.size * a.dtype.itemsize for a in As)
    bytes_b = sum(b.size * b.dtype.itemsize for b in Bs)
    n_outer = (N // tn) * bytes_a + bytes_b < (M // tm) * bytes_b + bytes_a

    def products(a_refs, b_refs):
        if n_a == 1:
            a = a_refs[0][...].astype(BF)
            return [lax.dot_general(a, b[...].astype(BF), dn, preferred_element_type=F32) for b in b_refs]
        if separate:
            return [lax.dot_general(a[...].astype(BF), b[...].astype(BF), dn, preferred_element_type=F32)
                    for a, b in zip(a_refs, b_refs)]
        total = None
        for a, b in zip(a_refs, b_refs):
            p = lax.dot_general(a[...].astype(BF), b[...].astype(BF), dn, preferred_element_type=F32)
            total = p if total is None else total + p
        return [total]

    def finish(accs, e_refs, o_refs):
        ex = [e[...] for e in e_refs]
        outs = epilogue(*accs, *ex) if epilogue is not None else tuple(accs)
        for o_ref, val in zip(o_refs, outs):
            o_ref[...] = val.astype(o_ref.dtype)

    def body(*refs):
        a_refs, b_refs = refs[:n_a], refs[n_a:n_a + n_b]
        e_refs = refs[n_a + n_b:n_a + n_b + n_e]
        o_refs = refs[n_a + n_b + n_e:n_a + n_b + n_e + n_o]
        acc_refs = refs[n_a + n_b + n_e + n_o:]
        if nk == 1:
            finish(products(a_refs, b_refs), e_refs, o_refs)
            return
        k = pl.program_id(2)

        @pl.when(k == 0)
        def _():
            for acc in acc_refs:
                acc[...] = jnp.zeros_like(acc)

        for acc, p in zip(acc_refs, products(a_refs, b_refs)):
            acc[...] += p

        @pl.when(k == nk - 1)
        def _():
            finish([acc[...] for acc in acc_refs], e_refs, o_refs)

    def at(f):
        return (lambda j, i, k: f(i, j, k)) if n_outer else f

    a_spec = pl.BlockSpec((tk, tm), at(lambda i, j, k: (k, i))) if ta else pl.BlockSpec((tm, tk), at(lambda i, j, k: (i, k)))
    b_spec = pl.BlockSpec((tn, tk), at(lambda i, j, k: (j, k))) if tb else pl.BlockSpec((tk, tn), at(lambda i, j, k: (k, j)))
    e_specs = [pl.BlockSpec((tm, tn), at(functools.partial(lambda i, j, k, o: (i, j + o), o=off // tn))) for off in extra_offs]
    for off in extra_offs:
        assert off % tn == 0
    outs = pl.pallas_call(
        body, name=name,
        out_shape=tuple(jax.ShapeDtypeStruct((M, N), dt) for dt in out_dtypes),
        grid=(N // tn, M // tm, nk) if n_outer else (M // tm, N // tn, nk),
        in_specs=[a_spec] * n_a + [b_spec] * n_b + e_specs,
        out_specs=tuple(pl.BlockSpec((tm, tn), at(lambda i, j, k: (i, j))) for _ in out_dtypes),
        scratch_shapes=[pltpu.VMEM((tm, tn), F32)] * n_acc,
        compiler_params=pltpu.CompilerParams(dimension_semantics=("parallel", "parallel", "arbitrary")),
    )(*As, *Bs, *extras)
    return outs[0] if n_o == 1 else outs


def _rms_fwd(x, g, *, name, blk_w=None, blk_idx=0, off=0, width=None, out_dtype=BF):
    T = x.shape[0]
    blk_w = x.shape[1] if blk_w is None else blk_w
    width = blk_w if width is None else width
    tt = _pick(T, 512)

    def body(x_ref, g_ref, o_ref):
        xf = x_ref[:, off:off + width]
        r = lax.rsqrt(jnp.mean(xf * xf, axis=-1, keepdims=True) + EPS)
        o_ref[...] = (xf * r * g_ref[...]).astype(o_ref.dtype)

    return pl.pallas_call(
        body, name=name, out_shape=jax.ShapeDtypeStruct((T, width), out_dtype), grid=(T // tt,),
        in_specs=[pl.BlockSpec((tt, blk_w), lambda i: (i, blk_idx)), pl.BlockSpec((1, width), lambda i: (0, 0))],
        out_specs=pl.BlockSpec((tt, width), lambda i: (i, 0)),
    )(x, g)


def _rms_bwd(dy, x, g, *, name, blk_w=None, blk_idx=0, off=0, width=None, add=None, out_dtypes=(F32,)):
    T = x.shape[0]
    blk_w = x.shape[1] if blk_w is None else blk_w
    width = blk_w if width is None else width
    tt = _pick(T, 512)
    has_add = add is not None
    n_dx = len(out_dtypes)

    def body(*refs):
        dy_ref, x_ref, g_ref = refs[:3]
        dx_refs, dg_ref = refs[3 + has_add:3 + has_add + n_dx], refs[-1]
        xf = x_ref[:, off:off + width]
        d = dy_ref[...].astype(F32)
        r = lax.rsqrt(jnp.mean(xf * xf, axis=-1, keepdims=True) + EPS)
        gd = d * g_ref[...]
        dx = r * gd - xf * (r * r * r) * jnp.mean(gd * xf, axis=-1, keepdims=True)
        if has_add:
            dx = dx + refs[3][...]
        for dx_ref in dx_refs:
            dx_ref[...] = dx.astype(dx_ref.dtype)

        @pl.when(pl.program_id(0) == 0)
        def _():
            dg_ref[...] = jnp.zeros_like(dg_ref)

        dg_ref[...] += jnp.broadcast_to(jnp.sum(d * xf * r, axis=0, keepdims=True), dg_ref.shape)

    row = pl.BlockSpec((tt, width), lambda i: (i, 0))
    in_specs = [row, pl.BlockSpec((tt, blk_w), lambda i: (i, blk_idx)), pl.BlockSpec((1, width), lambda i: (0, 0))]
    args = [dy, x, g]
    if has_add:
        in_specs.append(row)
        args.append(add)
    return pl.pallas_call(
        body, name=name,
        out_shape=tuple(jax.ShapeDtypeStruct((T, width), dt) for dt in out_dtypes) + (jax.ShapeDtypeStruct((8, width), F32),),
        grid=(T // tt,), in_specs=in_specs,
        out_specs=(row,) * n_dx + (pl.BlockSpec((8, width), lambda i: (0, 0)),),
        compiler_params=pltpu.CompilerParams(dimension_semantics=("arbitrary",)),
    )(*args)


def _rope_tables(pos_col, freq_lane):
    T = pos_col.shape[0]
    tt = _pick(T, 512)

    def body(p_ref, f_ref, c_ref, s_ref):
        ang = p_ref[...] * f_ref[...]
        lane = lax.broadcasted_iota(jnp.int32, ang.shape, 1)
        c_ref[...] = jnp.where(lane < QK_HEAD, jnp.cos(ang), 0.0)
        sn = jnp.sin(ang)
        s_ref[...] = jnp.where((lane >= QK_NOPE) & (lane < QK_NOPE + 16), -sn,
                               jnp.where((lane >= QK_NOPE + 16) & (lane < QK_HEAD), sn, 0.0))

    return pl.pallas_call(
        body, name="rope_tables", out_shape=(jax.ShapeDtypeStruct((T, HP), F32),) * 2, grid=(T // tt,),
        in_specs=[pl.BlockSpec((tt, 1), lambda i: (i, 0)), pl.BlockSpec((1, HP), lambda i: (0, 0))],
        out_specs=(pl.BlockSpec((tt, HP), lambda i: (i, 0)),) * 2,
    )(pos_col, freq_lane)


def _swap_rope_halves(n):
    src = lax.broadcasted_iota(jnp.int32, (HP, HP), 0)
    dst = lax.broadcasted_iota(jnp.int32, (HP, HP), 1)
    lo = (dst >= QK_NOPE) & (dst < QK_NOPE + 16) & (src == dst + 16)
    hi = (dst >= QK_NOPE + 16) & (dst < QK_HEAD) & (src == dst - 16)
    return _split_dot(n, jnp.where(lo | hi, 1.0, 0.0).astype(BF), 2)


def _qk_prep_fwd(raw, kpe, gain, C, S, *, name, kpe_blk=0, out_scale=1.0):
    T = raw.shape[0]
    tt = _pick(T, 256)
    has_kpe = kpe is not None

    def body(*refs):
        if has_kpe:
            raw_ref, kpe_ref, g_ref, c_ref, s_ref, o_ref = refs
        else:
            raw_ref, g_ref, c_ref, s_ref, o_ref = refs
        for h in range(N_HEADS):
            hs = slice(HP * h, HP * (h + 1))
            xr = raw_ref[:, hs] + kpe_ref[...] if has_kpe else raw_ref[:, hs]
            r = lax.rsqrt(jnp.sum(xr * xr, axis=-1, keepdims=True) * (1.0 / QK_HEAD) + EPS)
            n = xr * r * g_ref[...]
            o_ref[:, hs] = ((n * c_ref[...] + _swap_rope_halves(n) * s_ref[...]) * out_scale).astype(o_ref.dtype)

    heads = pl.BlockSpec((tt, N_HEADS * HP), lambda i: (i, 0))
    shared = pl.BlockSpec((tt, HP), lambda i: (i, 0))
    kpe_spec = pl.BlockSpec((tt, HP), lambda i: (i, kpe_blk))
    in_specs = [heads] + ([kpe_spec] if has_kpe else []) + [pl.BlockSpec((1, HP), lambda i: (0, 0)), shared, shared]
    args = [raw] + ([kpe] if has_kpe else []) + [gain, C, S]
    return pl.pallas_call(
        body, name=name, out_shape=jax.ShapeDtypeStruct(raw.shape, BF), grid=(T // tt,),
        in_specs=in_specs, out_specs=heads,
    )(*args)


def _qk_prep_bwd(dout, raw, kpe, gain, C, S, *, name, kpe_blk=0, in_scale=1.0):
    T = raw.shape[0]
    tt = _pick(T, 256)
    has_kpe = kpe is not None

    def body(*refs):
        if has_kpe:
            d_ref, raw_ref, kpe_ref, g_ref, c_ref, s_ref, dx_ref, dg_ref, dkpe_ref = refs
        else:
            d_ref, raw_ref, g_ref, c_ref, s_ref, dx_ref, dg_ref = refs
        dg = jnp.zeros((1, HP), F32)
        dkpe = jnp.zeros((tt, HP), F32)
        for h in range(N_HEADS):
            hs = slice(HP * h, HP * (h + 1))
            xr = raw_ref[:, hs] + kpe_ref[...] if has_kpe else raw_ref[:, hs]
            d = d_ref[:, hs].astype(F32) * in_scale
            r = lax.rsqrt(jnp.sum(xr * xr, axis=-1, keepdims=True) * (1.0 / QK_HEAD) + EPS)
            dn = d * c_ref[...] + _swap_rope_halves(d * s_ref[...])
            gd = dn * g_ref[...]
            dx = r * gd - xr * (r * r * r) * (jnp.sum(gd * xr, axis=-1, keepdims=True) * (1.0 / QK_HEAD))
            dx_ref[:, hs] = dx.astype(dx_ref.dtype)
            dg = dg + jnp.sum(dn * xr * r, axis=0, keepdims=True)
            dkpe = dkpe + dx

        @pl.when(pl.program_id(0) == 0)
        def _():
            dg_ref[...] = jnp.zeros_like(dg_ref)

        dg_ref[...] += jnp.broadcast_to(dg, dg_ref.shape)
        if has_kpe:
            dkpe_ref[...] = dkpe

    heads = pl.BlockSpec((tt, N_HEADS * HP), lambda i: (i, 0))
    shared = pl.BlockSpec((tt, HP), lambda i: (i, 0))
    kpe_spec = pl.BlockSpec((tt, HP), lambda i: (i, kpe_blk))
    in_specs = [heads, heads] + ([kpe_spec] if has_kpe else []) + [pl.BlockSpec((1, HP), lambda i: (0, 0)), shared, shared]
    args = [dout, raw] + ([kpe] if has_kpe else []) + [gain, C, S]
    out_shape = [jax.ShapeDtypeStruct(raw.shape, BF), jax.ShapeDtypeStruct((8, HP), F32)]
    out_specs = [heads, pl.BlockSpec((8, HP), lambda i: (0, 0))]
    if has_kpe:
        out_shape.append(jax.ShapeDtypeStruct((T, HP), F32))
        out_specs.append(shared)
    return pl.pallas_call(
        body, name=name, out_shape=tuple(out_shape), grid=(T // tt,),
        in_specs=in_specs, out_specs=tuple(out_specs),
        compiler_params=pltpu.CompilerParams(dimension_semantics=("arbitrary",)),
    )(*args)


ATTN_SCALE = 1.0 / math.sqrt(QK_HEAD)
LOG2E = 1.0 / math.log(2.0)
Q_SCALE = ATTN_SCALE * LOG2E


def _attn_fwd(q, k, v):
    T = q.shape[0]
    tq = _pick(T, 256)

    def body(q_ref, k_ref, v_ref, o_ref, lse_ref):
        s = lax.dot_general(q_ref[...], k_ref[...], NT, preferred_element_type=F32)
        m = jnp.max(s, axis=-1, keepdims=True)
        p = jnp.exp2(s - m)
        o = jnp.dot(p.astype(BF), v_ref[...], preferred_element_type=F32)
        l = o[:, V_HEAD:V_HEAD + 1]
        o_ref[...] = o / l
        lse_ref[...] = jnp.broadcast_to(m + jnp.log2(l), lse_ref.shape)

    qs = pl.BlockSpec((tq, HP), lambda h, i: (i, h))
    kv = pl.BlockSpec((T, HP), lambda h, i: (0, h))
    return pl.pallas_call(
        body, name="attn_fwd", out_shape=(jax.ShapeDtypeStruct(q.shape, F32),) * 2, grid=(N_HEADS, T // tq),
        in_specs=[qs, kv, kv], out_specs=(qs, qs),
        compiler_params=pltpu.CompilerParams(dimension_semantics=("parallel", "parallel")),
    )(q, k, v)


def _attn_bwd(q, k, v, do, o, lse):
    T = q.shape[0]
    tb = _pick(T, 512)
    nb = T // tb

    def body(q_ref, k_ref, v_ref, do_ref, o_ref, lse_ref, dq_ref, dk_ref, dv_ref, delta_rows, lse_rows, dob_scr):
        dq_ref[...] = jnp.zeros_like(dq_ref)
        lane = lax.broadcasted_iota(jnp.int32, (8, HP), 1)
        ones8 = jnp.ones((8, HP), BF)
        first8 = jnp.where(lane == 0, 1.0, 0.0).astype(BF)

        def as_rows(pick, v):
            total, rest = None, v
            for _ in range(3):
                piece = rest.astype(BF)
                part = lax.dot_general(pick, piece, NT, preferred_element_type=F32)
                total = part if total is None else total + part
                rest = rest - piece.astype(F32)
            return total

        def per_q_tile(i, carry):
            qs = pl.ds(pl.multiple_of(i * tb, tb), tb)
            doi = do_ref[qs, :]
            delta_rows[i] = as_rows(ones8, doi * o_ref[qs, :])
            lse_rows[i] = as_rows(first8, lse_ref[qs, :])
            dob_scr[qs, :] = doi.astype(BF)
            return carry

        lax.fori_loop(0, nb, per_q_tile, 0)

        def k_loop(j, carry):
            ks = pl.ds(pl.multiple_of(j * tb, tb), tb)
            kj, vj = k_ref[ks, :], v_ref[ks, :]

            def q_loop(i, acc):
                dk_acc, dv_acc = acc
                qs = pl.ds(pl.multiple_of(i * tb, tb), tb)
                qi = q_ref[qs, :]
                dob = dob_scr[qs, :]
                s_t = lax.dot_general(kj, qi, NT, preferred_element_type=F32)
                p_t = jnp.exp2(s_t - lse_rows[i, 0:1, :])
                dp_t = lax.dot_general(vj, dob, NT, preferred_element_type=F32)
                ds_t = (p_t * (dp_t - delta_rows[i, 0:1, :])).astype(BF)
                dv_acc = dv_acc + jnp.dot(p_t.astype(BF), dob, preferred_element_type=F32)
                dk_acc = dk_acc + jnp.dot(ds_t, qi, preferred_element_type=F32)
                dq_ref[qs, :] += lax.dot_general(ds_t, kj, TN, preferred_element_type=F32)
                return dk_acc, dv_acc

            zero = jnp.zeros((tb, HP), F32)
            dk_acc, dv_acc = lax.fori_loop(0, nb, q_loop, (zero, zero))
            dk_ref[ks, :] = dk_acc
            dv_ref[ks, :] = dv_acc.astype(dv_ref.dtype)
            return carry

        lax.fori_loop(0, nb, k_loop, 0)

    spec = pl.BlockSpec((T, HP), lambda h: (0, h))
    return pl.pallas_call(
        body, name="attn_bwd",
        out_shape=(jax.ShapeDtypeStruct(q.shape, F32), jax.ShapeDtypeStruct(q.shape, F32), jax.ShapeDtypeStruct(q.shape, BF)),
        grid=(N_HEADS,), in_specs=[spec] * 6, out_specs=(spec,) * 3,
        scratch_shapes=[pltpu.VMEM((nb, 8, tb), F32), pltpu.VMEM((nb, 8, tb), F32), pltpu.VMEM((T, HP), BF)],
        compiler_params=pltpu.CompilerParams(dimension_semantics=("parallel",), vmem_limit_bytes=2 * 15 * T * HP * 2 + (8 << 20)),
    )(q, k, v, do, o, lse)


CONV_TC = 512
CONV_PAD = CONV_WIDTH // 2


def _halo_specs(tr, col_of):
    r8 = tr // 8
    cur = pl.BlockSpec((tr, CONV_TC), lambda j, i: (i, col_of(j)))
    prev = pl.BlockSpec((8, CONV_TC), lambda j, i: (jnp.maximum(i * r8 - 1, 0), col_of(j)))

    def nxt_map(j, i, n8):
        return (jnp.minimum((i + 1) * r8, n8 - 1), col_of(j))

    return cur, prev, nxt_map


def _with_halo(prev_ref, cur_ref, next_ref, i, n_i):
    prev = jnp.where(i == 0, 0.0, prev_ref[...].astype(F32))
    nxt = jnp.where(i == n_i - 1, 0.0, next_ref[...].astype(F32))
    return jnp.concatenate([prev, cur_ref[...].astype(F32), nxt], axis=0)


def _conv_fwd(u, w8, b):
    T = u.shape[0]
    tr = _pick(T, 512)
    n_i = T // tr
    c0 = U_XBC // CONV_TC
    cur, prev, nxt_map = _halo_specs(tr, lambda j: c0 + j)
    nxt = pl.BlockSpec((8, CONV_TC), functools.partial(nxt_map, n8=T // 8))

    def body(p_ref, c_ref, n_ref, w_ref, b_ref, pre_ref, act_ref):
        i = pl.program_id(1)
        full = _with_halo(p_ref, c_ref, n_ref, i, n_i)
        acc = jnp.broadcast_to(b_ref[...], (tr, CONV_TC))
        for kk in range(CONV_WIDTH):
            acc = acc + full[8 - CONV_PAD + kk:8 - CONV_PAD + kk + tr, :] * w_ref[kk:kk + 1, :]
        pre_ref[...] = acc
        act_ref[...] = _silu(acc)

    out = pl.BlockSpec((tr, CONV_TC), lambda j, i: (i, j))
    return pl.pallas_call(
        body, name="conv_fwd", out_shape=(jax.ShapeDtypeStruct((T, XBC_DIM), F32),) * 2,
        grid=(XBC_DIM // CONV_TC, n_i),
        in_specs=[prev, cur, nxt, pl.BlockSpec((8, CONV_TC), lambda j, i: (0, j)), pl.BlockSpec((1, CONV_TC), lambda j, i: (0, j))],
        out_specs=(out, out),
    )(u, u, u, w8, b)


def _conv_dpre(dacts, pre, col0, *, name):
    T, width = dacts[0].shape
    tt = _pick(T, 512)
    n_d = len(dacts)
    c0 = col0 // CONV_TC

    def body(*refs):
        d = refs[0][...]
        for r in refs[1:n_d]:
            d = d + r[...]
        refs[n_d + 1][...] = d * _dsilu(refs[n_d][...])

    blk = pl.BlockSpec((tt, CONV_TC), lambda j, i: (i, j))
    return pl.pallas_call(
        body, name=name, out_shape=jax.ShapeDtypeStruct((T, width), F32), grid=(width // CONV_TC, T // tt),
        in_specs=[blk] * n_d + [pl.BlockSpec((tt, CONV_TC), lambda j, i: (i, c0 + j))], out_specs=blk,
    )(*dacts, pre)


def _conv_bwd(dpre, u, w8, col0, *, name):
    T, width = dpre.shape
    tr = _pick(T, 512)
    n_i = T // tr
    cd = col0 // CONV_TC
    cx = (U_XBC + col0) // CONV_TC
    d_cur, d_prev, d_nxt_map = _halo_specs(tr, lambda j: j)
    x_cur, x_prev, x_nxt_map = _halo_specs(tr, lambda j: cx + j)
    d_nxt = pl.BlockSpec((8, CONV_TC), functools.partial(d_nxt_map, n8=T // 8))
    x_nxt = pl.BlockSpec((8, CONV_TC), functools.partial(x_nxt_map, n8=T // 8))

    def body(dp_ref, dc_ref, dn_ref, xp_ref, xc_ref, xn_ref, w_ref, dx_ref, dw_ref):
        i = pl.program_id(1)
        dfull = _with_halo(dp_ref, dc_ref, dn_ref, i, n_i)
        xfull = _with_halo(xp_ref, xc_ref, xn_ref, i, n_i)
        dcur = dc_ref[...]
        dx = jnp.zeros((tr, CONV_TC), F32)
        rows = []
        for kk in range(CONV_WIDTH):
            dx = dx + dfull[8 + CONV_PAD - kk:8 + CONV_PAD - kk + tr, :] * w_ref[kk:kk + 1, :]
            rows.append(jnp.sum(dcur * xfull[8 - CONV_PAD + kk:8 - CONV_PAD + kk + tr, :], axis=0, keepdims=True))
        rows.append(jnp.sum(dcur, axis=0, keepdims=True))
        rows.append(jnp.zeros((2, CONV_TC), F32))
        dx_ref[...] = dx.astype(dx_ref.dtype)

        @pl.when(i == 0)
        def _():
            dw_ref[...] = jnp.zeros_like(dw_ref)

        dw_ref[...] += jnp.concatenate(rows, axis=0)

    out = pl.BlockSpec((tr, CONV_TC), lambda j, i: (i, j))
    return pl.pallas_call(
        body, name=name, out_shape=(jax.ShapeDtypeStruct((T, width), BF), jax.ShapeDtypeStruct((8, width), F32)),
        grid=(width // CONV_TC, n_i),
        in_specs=[d_prev, d_cur, d_nxt, x_prev, x_cur, x_nxt, pl.BlockSpec((8, CONV_TC), lambda j, i: (0, cd + j))],
        out_specs=(out, pl.BlockSpec((8, CONV_TC), lambda j, i: (0, j))),
        compiler_params=pltpu.CompilerParams(dimension_semantics=("parallel", "arbitrary")),
    )(dpre, dpre, dpre, u, u, u, w8)


N_HB = 2 * SSM_GROUPS
P_DT, P_CS, P_E, P_W = 0, HP, 2 * HP, 3 * HP
DT_BLK = (U_SMALL + S_DT) // HP


def _tri(rev, transpose=False):
    rows = lax.broadcasted_iota(jnp.int32, (CHUNK, CHUNK), 0)
    cols = lax.broadcasted_iota(jnp.int32, (CHUNK, CHUNK), 1)
    if transpose:
        rows, cols = cols, rows
    return (cols >= rows) if rev else (cols <= rows)


def _ssd_prep(u, bias8, alog8):
    T = u.shape[0]
    nc = T // CHUNK

    def body(dt_ref, bias_ref, a_ref, cols_ref, rows_ref):
        lane = lax.broadcasted_iota(jnp.int32, (CHUNK, HP), 1)
        dt = _softplus(dt_ref[...] + bias_ref[0:1, :])
        da = dt * (-jnp.exp(a_ref[0:1, :]))
        cs_f = jnp.dot(jnp.where(_tri(False), 1.0, 0.0).astype(F32), da, precision=HI, preferred_element_type=F32)
        cs_b = jnp.dot(jnp.where(_tri(True), 1.0, 0.0).astype(F32), da, precision=HI, preferred_element_type=F32)
        cs = jnp.where(lane < SSM_HEADS, cs_f, cs_b)
        tot = jnp.where(lane[0:1] < SSM_HEADS, cs_f[CHUNK - 1:CHUNK, :], cs_b[0:1, :])
        e, w = jnp.exp(cs), jnp.exp(tot - cs)
        tot8 = jnp.broadcast_to(tot, (8, HP))
        etot8 = jnp.exp(tot8)
        for b in range(N_HB):
            down = (HP - HG * b) % HP

            def rolled(v):
                return pltpu.roll(v, down, 1) if down else v

            cols_ref[b, :, P_DT:P_DT + HP] = rolled(dt)
            cs_r = rolled(cs)
            cols_ref[b, :, P_CS:P_CS + HP] = cs_r
            cols_ref[b, :, P_E:P_E + HP] = rolled(e)
            cols_ref[b, :, P_W:P_W + HP] = rolled(w)
            rows_ref[b, 0, 0:8, :] = cs_r.T[0:8, :]
            r8 = lax.broadcasted_iota(jnp.int32, (8, HP), 0)
            rows_ref[b, 0, 8:16, :] = jnp.where(r8 == 0, rolled(tot8), jnp.where(r8 == 1, rolled(etot8), 0.0))

    vec = pl.BlockSpec((8, HP), lambda c: (0, 0))
    return pl.pallas_call(
        body, name="ssd_prep",
        out_shape=(jax.ShapeDtypeStruct((N_HB, T, 4 * HP), F32), jax.ShapeDtypeStruct((N_HB, nc, 16, HP), F32)),
        grid=(nc,), in_specs=[pl.BlockSpec((CHUNK, HP), lambda c: (c, DT_BLK)), vec, vec],
        out_specs=(pl.BlockSpec((N_HB, CHUNK, 4 * HP), lambda c: (0, c, 0)), pl.BlockSpec((N_HB, 1, 16, HP), lambda c: (0, c, 0, 0))),
    )(u, bias8, alog8)


def _ssd_specs(T, rev, bwd):
    nc = T // CHUNK
    fwd_order = (lambda c: nc - 1 - c) if rev else (lambda c: c)
    cm = (lambda c: fwd_order(nc - 1 - c)) if bwd else fwd_order
    hb0 = SSM_GROUPS if rev else 0
    xs = pl.BlockSpec((CHUNK, GW), lambda c, g: (cm(c), g))
    bs = pl.BlockSpec((CHUNK, D_STATE), lambda c, g: (cm(c), D_INNER // D_STATE + g))
    cs = pl.BlockSpec((CHUNK, D_STATE), lambda c, g: (cm(c), (D_INNER + SSM_GROUPS * D_STATE) // D_STATE + g))
    cols = pl.BlockSpec((1, CHUNK, 4 * HP), lambda c, g: (hb0 + g, cm(c), 0))
    rows = pl.BlockSpec((1, 1, 16, HP), lambda c, g: (hb0 + g, cm(c), 0, 0))
    return nc, cm, xs, bs, cs, cols, rows


def _head_lanes(to_heads):
    shape = (GW, HP) if to_heads else (HP, GW)
    wide = lax.broadcasted_iota(jnp.int32, shape, 0 if to_heads else 1)
    head = lax.broadcasted_iota(jnp.int32, shape, 1 if to_heads else 0)
    return jnp.where((wide >= PH * head) & (wide < PH * (head + 1)), 1.0, 0.0).astype(BF)


def _split_dot(v, m, terms):
    total, rest = None, v
    for _ in range(terms):
        piece = rest.astype(BF)
        part = jnp.dot(piece, m, preferred_element_type=F32)
        total = part if total is None else total + part
        rest = rest - piece.astype(F32)
    return total


def _spread_cols(cols_ref, rows_ref):
    spread = _head_lanes(False)
    dt_e = _split_dot(cols_ref[0, :, P_DT:P_DT + HP], spread, 3)
    e_e = _split_dot(cols_ref[0, :, P_E:P_E + HP], spread, 3)
    w_e = _split_dot(cols_ref[0, :, P_W:P_W + HP], spread, 3)
    etot_e = _split_dot(rows_ref[0, 0, 8:16, :], spread, 3)[1:2, :]
    return dt_e, e_e, w_e, etot_e


def _decay(cols_ref, rows_ref, hh, incl, transpose=False):
    col = cols_ref[0, :, P_CS + hh:P_CS + hh + 1]
    row = rows_ref[0, 0, hh:hh + 1, :]
    return jnp.where(incl, jnp.exp(row - col if transpose else col - row), 0.0)


def _ssd_fwd(act, cols, rows, *, rev, name):
    T = act.shape[0]
    nc, cm, xs_s, b_s, c_s, cols_s, rows_s = _ssd_specs(T, rev, False)

    def body(x_ref, b_ref, c_ref, cols_ref, rows_ref, y_ref, st_ref, state):
        c, g = pl.program_id(0), pl.program_id(1)

        @pl.when(c == 0)
        def _():
            state[g] = jnp.zeros((D_STATE, GW), F32)

        incl = _tri(rev)
        bm, cmat = b_ref[...].astype(BF), c_ref[...].astype(BF)
        bm_t = b_ref[...].T.astype(BF)
        cb = lax.dot_general(cmat, bm, NT, preferred_element_type=F32)
        dt_e, e_e, w_e, etot_e = _spread_cols(cols_ref, rows_ref)
        prev_all = state[g]
        st_ref[...] = prev_all
        xdt = x_ref[...] * dt_e
        xdt_b = xdt.astype(BF)
        yo_all = jnp.dot(cmat, prev_all.astype(BF), preferred_element_type=F32) * e_e
        state[g] = prev_all * etot_e + jnp.dot(bm_t, (xdt * w_e).astype(BF), preferred_element_type=F32)
        for hh in range(HG):
            hs = slice(PH * hh, PH * (hh + 1))
            lmat = _decay(cols_ref, rows_ref, hh, incl)
            yd = jnp.dot((cb * lmat).astype(BF), xdt_b[:, hs], preferred_element_type=F32)
            y_ref[:, hs] = yd + yo_all[:, hs]

    return pl.pallas_call(
        body, name=name,
        out_shape=(jax.ShapeDtypeStruct((T, D_INNER), F32), jax.ShapeDtypeStruct((nc * D_STATE, D_INNER), F32)),
        grid=(nc, SSM_GROUPS), in_specs=[xs_s, b_s, c_s, cols_s, rows_s], out_specs=(xs_s, xs_s),
        scratch_shapes=[pltpu.VMEM((SSM_GROUPS, D_STATE, GW), F32)],
        compiler_params=pltpu.CompilerParams(dimension_semantics=("arbitrary", "arbitrary")),
    )(act, act, act, cols, rows)


def _ssd_bwd(act, cols, rows, states, dy, *, rev, name):
    T = act.shape[0]
    nc, cm, xs_s, b_s, c_s, cols_s, rows_s = _ssd_specs(T, rev, True)

    def body(x_ref, b_ref, c_ref, cols_ref, rows_ref, st_ref, dy_ref, dx_ref, db_ref, dc_ref, dsel_ref, dtot_ref,
             dstate, dcs_cols, dcs_rows, dcb, dm_scr, dxdt_scr):
        c, g = pl.program_id(0), pl.program_id(1)

        @pl.when(c == 0)
        def _():
            dstate[g] = jnp.zeros((D_STATE, GW), F32)

        incl, incl_t = _tri(rev), _tri(rev, transpose=True)
        bm, cmat = b_ref[...].astype(BF), c_ref[...].astype(BF)
        cm_t = c_ref[...].T.astype(BF)
        cb = lax.dot_general(cmat, bm, NT, preferred_element_type=F32)
        cb_t = lax.dot_general(bm, cmat, NT, preferred_element_type=F32)
        prev_all, ds_all = st_ref[...], dstate[g]
        pb_all, dsb_all = prev_all.astype(BF), ds_all.astype(BF)
        cp_all = jnp.dot(cmat, pb_all, preferred_element_type=F32)
        bds_all = jnp.dot(bm, dsb_all, preferred_element_type=F32)
        dt_e, e_e, w_e, etot_e = _spread_cols(cols_ref, rows_ref)
        to_heads = _head_lanes(True)
        x, dy = x_ref[...], dy_ref[...]
        xdt = x * dt_e
        xdt_b, dy_b = xdt.astype(BF), dy.astype(BF)
        dye_b, xdw_b = (dy * e_e).astype(BF), (xdt * w_e).astype(BF)
        for hh in range(HG):
            hs = slice(PH * hh, PH * (hh + 1))
            mmat_t = cb_t * _decay(cols_ref, rows_ref, hh, incl_t, transpose=True)
            dm_scr[hh] = lax.dot_general(dy_b[:, hs], xdt_b[:, hs], NT, preferred_element_type=F32)
            dxdt_scr[:, hs] = jnp.dot(mmat_t.astype(BF), dy_b[:, hs], preferred_element_type=F32)
        bdsw = bds_all * w_e
        dxdt = dxdt_scr[...] + bdsw
        dx_ref[...] = dxdt * dt_e
        t = _split_dot(xdt * bdsw, to_heads, 2)
        dcs_state = _split_dot(dy * cp_all, to_heads, 2) * cols_ref[0, :, P_E:P_E + HP] - t
        dsel_ref[0, :, 0:HP] = _split_dot(dxdt * x, to_heads, 2)
        sp = _split_dot(jnp.broadcast_to(jnp.sum(ds_all * prev_all, axis=0, keepdims=True), (8, GW)), to_heads, 2)
        dtot_ref[0, 0] = jnp.sum(t, axis=0, keepdims=True) + sp * rows_ref[0, 0, 9:10, :]
        dstate[g] = ds_all * etot_e + jnp.dot(cm_t, dye_b, preferred_element_type=F32)
        dcs_cols[...] = jnp.zeros_like(dcs_cols)
        dcs_rows[...] = jnp.zeros_like(dcs_rows)
        dcb[...] = jnp.zeros_like(dcb)
        for hh in range(HG):
            lmat = _decay(cols_ref, rows_ref, hh, incl)
            dm = dm_scr[hh]
            qm = dm * (cb * lmat)
            dcs_cols[:, hh:hh + 1] = jnp.sum(qm, axis=1, keepdims=True)
            dcs_rows[hh:hh + 1, :] = jnp.sum(qm, axis=0, keepdims=True)
            dcb[...] += dm * lmat
        dcb_all = dcb[...]
        dsel_ref[0, :, HP:2 * HP] = dcs_state + dcs_cols[...] - dcs_rows[...].T
        dc_ref[...] = (lax.dot_general(dye_b, pb_all, NT, preferred_element_type=F32)
                       + jnp.dot(dcb_all.astype(BF), bm, preferred_element_type=F32))
        db_ref[...] = (lax.dot_general(xdw_b, dsb_all, NT, preferred_element_type=F32)
                       + jnp.dot(dcb_all.T.astype(BF), cmat, preferred_element_type=F32))

    bc_out = pl.BlockSpec((CHUNK, D_STATE), lambda c, g: (cm(c), g))
    return pl.pallas_call(
        body, name=name,
        out_shape=(jax.ShapeDtypeStruct((T, D_INNER), F32), jax.ShapeDtypeStruct((T, SSM_GROUPS * D_STATE), F32),
                   jax.ShapeDtypeStruct((T, SSM_GROUPS * D_STATE), F32), jax.ShapeDtypeStruct((SSM_GROUPS, T, 2 * HP), F32),
                   jax.ShapeDtypeStruct((SSM_GROUPS, nc, 8, HP), F32)),
        grid=(nc, SSM_GROUPS), in_specs=[xs_s, b_s, c_s, cols_s, rows_s, xs_s, xs_s],
        out_specs=(xs_s, bc_out, bc_out, pl.BlockSpec((1, CHUNK, 2 * HP), lambda c, g: (g, cm(c), 0)),
                   pl.BlockSpec((1, 1, 8, HP), lambda c, g: (g, cm(c), 0, 0))),
        scratch_shapes=[pltpu.VMEM((SSM_GROUPS, D_STATE, GW), F32), pltpu.VMEM((CHUNK, CHUNK), F32),
                        pltpu.VMEM((CHUNK, CHUNK), F32), pltpu.VMEM((CHUNK, CHUNK), F32),
                        pltpu.VMEM((HG, CHUNK, CHUNK), F32), pltpu.VMEM((CHUNK, GW), F32)],
        compiler_params=pltpu.CompilerParams(dimension_semantics=("arbitrary", "arbitrary")),
    )(act, act, act, cols, rows, states, dy)


def _ssd_prep_bwd(u, bias8, alog8, dsel_f, dtot_f, dsel_b, dtot_b):
    T = u.shape[0]
    nc = T // CHUNK

    def body(dt_ref, bias_ref, a_ref, sf_ref, tf_ref, sb_ref, tb_ref, ddt_ref, da_ref, dbias_ref):
        @pl.when(pl.program_id(0) == 0)
        def _():
            da_ref[...] = jnp.zeros_like(da_ref)
            dbias_ref[...] = jnp.zeros_like(dbias_ref)

        lane = lax.broadcasted_iota(jnp.int32, (CHUNK, HP), 1)
        pre = dt_ref[...] + bias_ref[0:1, :]
        dt = _softplus(pre)
        a = -jnp.exp(a_ref[0:1, :])
        ddt_x, dcs, dtot = jnp.zeros((CHUNK, HP), F32), jnp.zeros((CHUNK, HP), F32), jnp.zeros((8, HP), F32)
        for b in range(N_HB):
            s_ref, t_ref, g = (sf_ref, tf_ref, b) if b < SSM_GROUPS else (sb_ref, tb_ref, b - SSM_GROUPS)
            mine = (lane >= HG * b) & (lane < HG * (b + 1))

            def up(v):
                return pltpu.roll(v, HG * b, 1) if b else v

            ddt_x = ddt_x + jnp.where(mine, up(s_ref[g, :, 0:HP]), 0.0)
            dcs = dcs + jnp.where(mine, up(s_ref[g, :, HP:2 * HP]), 0.0)
            dtot = dtot + jnp.where(mine[0:8], up(t_ref[g, 0]), 0.0)
        tri_f = jnp.where(_tri(False, transpose=True), 1.0, 0.0).astype(F32)
        tri_b = jnp.where(_tri(True, transpose=True), 1.0, 0.0).astype(F32)
        dda = jnp.where(lane < SSM_HEADS, jnp.dot(tri_f, dcs, precision=HI, preferred_element_type=F32),
                        jnp.dot(tri_b, dcs, precision=HI, preferred_element_type=F32)) + dtot[0:1, :]
        dpre = (ddt_x + dda * a) * jax.nn.sigmoid(pre)
        ddt_ref[...] = jnp.where(lane < 2 * SSM_HEADS, dpre, 0.0)
        dbias_ref[...] += jnp.broadcast_to(jnp.sum(dpre, axis=0, keepdims=True), (8, HP))
        da_ref[...] += jnp.broadcast_to(jnp.sum(dda * dt, axis=0, keepdims=True) * a, (8, HP))

    vec = pl.BlockSpec((8, HP), lambda c: (0, 0))
    sel = pl.BlockSpec((SSM_GROUPS, CHUNK, 2 * HP), lambda c: (0, c, 0))
    tot = pl.BlockSpec((SSM_GROUPS, 1, 8, HP), lambda c: (0, c, 0, 0))
    tile = pl.BlockSpec((CHUNK, HP), lambda c: (c, 0))
    return pl.pallas_call(
        body, name="ssd_prep_bwd",
        out_shape=(jax.ShapeDtypeStruct((T, HP), F32), jax.ShapeDtypeStruct((8, HP), F32), jax.ShapeDtypeStruct((8, HP), F32)),
        grid=(nc,), in_specs=[pl.BlockSpec((CHUNK, HP), lambda c: (c, DT_BLK)), vec, vec, sel, tot, sel, tot],
        out_specs=(tile, vec, vec),
        compiler_params=pltpu.CompilerParams(dimension_semantics=("arbitrary",)),
    )(u, bias8, alog8, dsel_f, dtot_f, dsel_b, dtot_b)


def _ssm_combine_fwd(y_f, y_b, act, u, dskip, gain):
    T = y_f.shape[0]
    tt = _pick(T, 512)

    def body(yf_ref, yb_ref, x_ref, z_ref, ds_ref, g_ref, y_ref, m_ref):
        y = yf_ref[...] + yb_ref[...] + ds_ref[...] * x_ref[...]
        y2 = y * _silu(z_ref[...])
        r = lax.rsqrt(jnp.mean(y2 * y2, axis=-1, keepdims=True) + EPS)
        y_ref[...] = y
        m_ref[...] = (y2 * r * g_ref[...]).astype(m_ref.dtype)

    blk = pl.BlockSpec((tt, GW), lambda i, g: (i, g))
    vec = pl.BlockSpec((1, GW), lambda i, g: (0, g))
    return pl.pallas_call(
        body, name="ssm_combine_fwd",
        out_shape=(jax.ShapeDtypeStruct((T, D_INNER), F32), jax.ShapeDtypeStruct((T, D_INNER), BF)),
        grid=(T // tt, SSM_GROUPS), in_specs=[blk, blk, blk, blk, vec, vec], out_specs=(blk, blk),
    )(y_f, y_b, act, u, dskip, gain)


def _ssm_combine_bwd(dm, y, act, u, dskip, gain):
    T = y.shape[0]
    tt = _pick(T, 512)

    def body(dm_ref, y_ref, x_ref, z_ref, ds_ref, g_ref, dy_ref, dz_ref, dxs_ref, dg_ref, dsk_ref):
        z = z_ref[...]
        y = y_ref[...]
        x = x_ref[...]
        sz = _silu(z)
        y2 = y * sz
        r = lax.rsqrt(jnp.mean(y2 * y2, axis=-1, keepdims=True) + EPS)
        d = dm_ref[...]
        gd = d * g_ref[...]
        dy2 = r * gd - y2 * (r * r * r) * jnp.mean(gd * y2, axis=-1, keepdims=True)
        dy = dy2 * sz
        dy_ref[...] = dy
        dz_ref[...] = (dy2 * y * _dsilu(z)).astype(dz_ref.dtype)
        dxs_ref[...] = dy * ds_ref[...]

        @pl.when(pl.program_id(1) == 0)
        def _():
            dg_ref[...] = jnp.zeros_like(dg_ref)
            dsk_ref[...] = jnp.zeros_like(dsk_ref)

        dg_ref[...] += jnp.broadcast_to(jnp.sum(d * y2 * r, axis=0, keepdims=True), dg_ref.shape)
        lane_sum = jnp.broadcast_to(jnp.sum(dy * x, axis=0, keepdims=True), (8, GW))
        src = lax.broadcasted_iota(jnp.int32, (GW, HP), 0)
        head = lax.broadcasted_iota(jnp.int32, (GW, HP), 1)
        to_head = jnp.where((src >= PH * head) & (src < PH * (head + 1)), 1.0, 0.0).astype(F32)
        dsk_ref[...] += jnp.dot(lane_sum, to_head, precision=HI, preferred_element_type=F32)

    blk = pl.BlockSpec((tt, GW), lambda g, i: (i, g))
    vec = pl.BlockSpec((1, GW), lambda g, i: (0, g))
    acc = pl.BlockSpec((8, GW), lambda g, i: (0, g))
    return pl.pallas_call(
        body, name="ssm_combine_bwd",
        out_shape=(jax.ShapeDtypeStruct((T, D_INNER), F32), jax.ShapeDtypeStruct((T, D_INNER), BF),
                   jax.ShapeDtypeStruct((T, D_INNER), F32), jax.ShapeDtypeStruct((8, D_INNER), F32),
                   jax.ShapeDtypeStruct((8, SSM_GROUPS * HP), F32)),
        grid=(SSM_GROUPS, T // tt), in_specs=[blk, blk, blk, blk, vec, vec],
        out_specs=(blk, blk, blk, acc, pl.BlockSpec((8, HP), lambda g, i: (0, g))),
        compiler_params=pltpu.CompilerParams(dimension_semantics=("parallel", "arbitrary")),
    )(dm, y, act, u, dskip, gain)


def _loss_head(y, target):
    T, D = y.shape
    tt = _pick(T, 512)

    def body(y_ref, t_ref, dy_ref, dyb_ref, l_ref):
        e = y_ref[...] - t_ref[...]
        dy_ref[...] = e * (1.0 / D)
        dyb_ref[...] = (e * (1.0 / D)).astype(dyb_ref.dtype)

        @pl.when(pl.program_id(0) == 0)
        def _():
            l_ref[...] = jnp.zeros_like(l_ref)

        l_ref[...] += jnp.sum(e * e) * (0.5 / D)

    blk = pl.BlockSpec((tt, D), lambda i: (i, 0))
    return pl.pallas_call(
        body, name="loss_head",
        out_shape=(jax.ShapeDtypeStruct((T, D), F32), jax.ShapeDtypeStruct((T, D), BF), jax.ShapeDtypeStruct((8, 128), F32)),
        grid=(T // tt,), in_specs=[blk, blk], out_specs=(blk, blk, pl.BlockSpec((8, 128), lambda i: (0, 0))),
        compiler_params=pltpu.CompilerParams(dimension_semantics=("arbitrary",)),
    )(y, target)


def _adamw(w, g, m, v, *, name):
    R, C = w.shape
    cap = max(8, (1 << 18) // C)
    tr = R
    if R % 8 == 0:
        tr = 8
        for cand in range(8, min(R, cap) + 1, 8):
            if R % cand == 0:
                tr = cand

    def body(w_ref, g_ref, m_ref, v_ref, d_ref, nm_ref, nv_ref):
        gg = g_ref[...]
        nm = ADAM_B1 * m_ref[...] + (1.0 - ADAM_B1) * gg
        nv = ADAM_B2 * v_ref[...] + (1.0 - ADAM_B2) * jnp.square(gg)
        m_hat = nm / (1.0 - ADAM_B1 ** ADAM_STEP)
        v_hat = nv / (1.0 - ADAM_B2 ** ADAM_STEP)
        d_ref[...] = -ADAM_LR * (m_hat / (jnp.sqrt(v_hat) + ADAM_EPS) + ADAM_WD * w_ref[...])
        nm_ref[...] = nm
        nv_ref[...] = nv

    blk = pl.BlockSpec((tr, C), lambda i: (i, 0))
    return pl.pallas_call(
        body, name=name, out_shape=(jax.ShapeDtypeStruct((R, C), F32),) * 3, grid=(R // tr,),
        in_specs=[blk] * 4, out_specs=(blk,) * 3,
    )(w, g, m, v)


ANY = pl.BlockSpec(memory_space=pl.ANY)


def _chip_peers():
    x, y, c = lax.axis_index("x"), lax.axis_index("y"), lax.axis_index("c")
    return x, y, c, [(1 - x, y), (x, 1 - y), (1 - x, 1 - y)]


def _half_rows(c, rh):
    return pl.ds(pl.multiple_of(c * rh, 16), rh)


def _my_chip():
    return 2 * lax.axis_index("x") + lax.axis_index("y")


def _gather_chips(wb, wf):
    rh = wb.shape[0] // 2
    rq = rh // 2

    def body(wb_ref, wf_ref, ob_ref, of_ref, send_sems, recv_sems):
        x, y, c, peers = _chip_peers()
        nbr_x, nbr_y = peers[0], peers[1]
        me, chip_x, chip_y, chip_d = 2 * x + y, 2 * (1 - x) + y, 2 * x + (1 - y), 2 * (1 - x) + (1 - y)

        def quarter(core, b):
            return pl.ds(pl.multiple_of(core * rh + b * rq, 16), rq)

        ici = [(0, nbr_x, me, 0, chip_x), (1, nbr_y, me, 1, chip_y), (2, nbr_y, me, 0, chip_y), (3, nbr_x, me, 1, chip_x),
               (4, nbr_y, chip_x, 0, chip_d), (5, nbr_x, chip_y, 1, chip_d)]

        def ici_copy(k, to, slot, b, own):
            rows = quarter(c, b)
            return pltpu.make_async_remote_copy(
                src_ref=wb_ref.at[rows] if own else ob_ref.at[slot, rows], dst_ref=ob_ref.at[slot, rows],
                send_sem=send_sems.at[k], recv_sem=recv_sems.at[k], device_id=(to[0], to[1], c), device_id_type=MESH)

        def to_sibling(k, slot, b, core):
            rows = quarter(core, b)
            return pltpu.make_async_remote_copy(
                src_ref=ob_ref.at[slot, rows], dst_ref=ob_ref.at[slot, rows], send_sem=send_sems.at[6 + k],
                recv_sem=recv_sems.at[6 + k], device_id=(x, y, 1 - c), device_id_type=MESH)

        def small_copy(k, slot):
            px, py = peers[k]
            return pltpu.make_async_remote_copy(
                src_ref=wf_ref, dst_ref=of_ref.at[slot], send_sem=send_sems.at[12 + k], recv_sem=recv_sems.at[12 + k],
                device_id=(px, py, c), device_id_type=MESH)

        sends = [ici_copy(k, to, slot, b, True) for k, to, slot, b, _ in ici[:4]] + [small_copy(k, me) for k in range(3)]
        for cp in sends:
            cp.start()
        for k, to, slot, b, arrives in ici:
            ici_copy(k, to, arrives, b, False).wait_recv()
            passed = [to_sibling(k, arrives, b, c)]
            if k < 2:
                passed.append(ici_copy(*ici[4 + k][:4], False))
            for cp in passed:
                cp.start()
            sends += passed
        for k, to, slot, b, arrives in ici:
            to_sibling(k, arrives, b, 1 - c).wait_recv()
        chip_of = [chip_x, chip_y, chip_d]
        for k in range(3):
            small_copy(k, chip_of[k]).wait_recv()
        for cp in sends:
            cp.wait_send()

    ob, of = pl.pallas_call(
        body, name="gather_weights",
        out_shape=(jax.ShapeDtypeStruct((4,) + wb.shape, wb.dtype), jax.ShapeDtypeStruct((4,) + wf.shape, wf.dtype)),
        in_specs=[ANY, ANY], out_specs=(ANY, ANY),
        scratch_shapes=[pltpu.SemaphoreType.DMA((15,)), pltpu.SemaphoreType.DMA((15,))],
    )(wb, wf)
    me = _my_chip()
    return lax.dynamic_update_slice(ob, wb[None], (me, 0, 0)), lax.dynamic_update_slice(of, wf[None], (me, 0, 0))


def _halves_to_sibling(gp):
    rh = gp.shape[1] // 2

    def body(gp_ref, o_ref, send_sem, recv_sem):
        x, y, c = lax.axis_index("x"), lax.axis_index("y"), lax.axis_index("c")
        cp = pltpu.make_async_remote_copy(src_ref=gp_ref.at[:, _half_rows(1 - c, rh), :], dst_ref=o_ref, send_sem=send_sem,
                                          recv_sem=recv_sem, device_id=(x, y, 1 - c), device_id_type=MESH)
        cp.start()
        cp.wait()

    return pl.pallas_call(
        body, name="halves_to_sibling", out_shape=jax.ShapeDtypeStruct((gp.shape[0], rh, gp.shape[2]), gp.dtype),
        in_specs=[ANY], out_specs=ANY, scratch_shapes=[pltpu.SemaphoreType.DMA, pltpu.SemaphoreType.DMA],
    )(gp)


def _row_tile(rows, cap=1024):
    tr = 16
    for cand in range(16, cap + 1, 16):
        if rows % cand == 0:
            tr = cand
    return tr


def _add_halves(gp, sib, core):
    n, rh, C = sib.shape
    tr = _row_tile(rh)
    nt = rh // tr

    def body(c_ref, g_ref, s_ref, o_ref):
        o_ref[...] = (g_ref[...].astype(F32) + s_ref[...].astype(F32)).astype(o_ref.dtype)

    blk = pl.BlockSpec((1, tr, C), lambda j, i, c: (j, i, 0))
    return pl.pallas_call(
        body, name="add_halves", out_shape=jax.ShapeDtypeStruct(sib.shape, sib.dtype),
        grid_spec=pltpu.PrefetchScalarGridSpec(
            num_scalar_prefetch=1, grid=(n, nt),
            in_specs=[pl.BlockSpec((1, tr, C), lambda j, i, c: (j, c[0] * nt + i, 0)), blk], out_specs=blk),
    )(core, gp, sib)


def _join_halves(mine):
    rh = mine.shape[0]

    def body(m_ref, o_ref, send_sem, recv_sem):
        x, y, c = lax.axis_index("x"), lax.axis_index("y"), lax.axis_index("c")
        half, other = _half_rows(c, rh), _half_rows(1 - c, rh)

        def copy(rows):
            return pltpu.make_async_remote_copy(src_ref=m_ref, dst_ref=o_ref.at[rows], send_sem=send_sem, recv_sem=recv_sem,
                                                device_id=(x, y, 1 - c), device_id_type=MESH)

        send = copy(half)
        send.start()
        copy(other).wait_recv()
        send.wait_send()

    out = pl.pallas_call(
        body, name="join_halves", out_shape=jax.ShapeDtypeStruct((2 * rh, mine.shape[1]), mine.dtype),
        in_specs=[ANY], out_specs=ANY, scratch_shapes=[pltpu.SemaphoreType.DMA, pltpu.SemaphoreType.DMA],
    )(mine)
    return lax.dynamic_update_slice(out, mine, (lax.axis_index("c") * rh, 0))


def _exchange_chips(gp):
    def body(gp_ref, out_ref, send_sems, recv_sems):
        x, y, c, peers = _chip_peers()
        me = 2 * x + y

        def copies(sending):
            out = []
            for k, (px, py) in enumerate(peers):
                p = 2 * px + py
                out.append(pltpu.make_async_remote_copy(
                    src_ref=gp_ref.at[p], dst_ref=out_ref.at[me if sending else p],
                    send_sem=send_sems.at[k], recv_sem=recv_sems.at[k], device_id=(px, py, c), device_id_type=MESH))
            return out

        sends = copies(True)
        for cp in sends:
            cp.start()
        for cp in copies(False):
            cp.wait_recv()
        for cp in sends:
            cp.wait_send()

    out = pl.pallas_call(
        body, name="exchange_grads", out_shape=jax.ShapeDtypeStruct(gp.shape, gp.dtype),
        in_specs=[ANY], out_specs=ANY,
        scratch_shapes=[pltpu.SemaphoreType.DMA((3,)), pltpu.SemaphoreType.DMA((3,))],
    )(gp)
    me = _my_chip()
    return lax.dynamic_update_slice(out, lax.dynamic_slice_in_dim(gp, me, 1, axis=0), (me, 0, 0))


def _sum_slots(r4):
    _, R, C = r4.shape
    tr = _row_tile(R)

    def body(r_ref, o_ref):
        acc = r_ref[0].astype(F32)
        for s in range(1, 4):
            acc = acc + r_ref[s].astype(F32)
        o_ref[...] = acc

    return pl.pallas_call(
        body, name="sum_slots", out_shape=jax.ShapeDtypeStruct((R, C), F32), grid=(R // tr,),
        in_specs=[pl.BlockSpec((4, tr, C), lambda i: (0, i, 0))], out_specs=pl.BlockSpec((tr, C), lambda i: (i, 0)),
    )(r4)


N_DEV = 8


def _allreduce_small(p):
    rs = p.shape[0]

    def body(x_ref, sum_ref, all_ref, send_sems, recv_sems, local_sem):
        x, y, c = lax.axis_index("x"), lax.axis_index("y"), lax.axis_index("c")
        me, sibling = (x, y, c), (x, y, 1 - c)
        chips = [(1 - x, y), (x, 1 - y), (1 - x, 1 - y)]

        def rows(px, py, pc):
            return all_ref.at[pl.ds((4 * px + 2 * py + pc) * rs, rs), :]

        def copy(k, block, to, src=None):
            return pltpu.make_async_remote_copy(
                src_ref=rows(*block) if src is None else src, dst_ref=rows(*block),
                send_sem=send_sems.at[k], recv_sem=recv_sems.at[k], device_id=to, device_id_type=MESH)

        mine = pltpu.make_async_copy(x_ref, rows(*me), local_sem)
        mine.start()
        first = [copy(0, me, sibling, src=x_ref)]
        first += [copy(1 + j, me, (*chip, c), src=x_ref) for j, chip in enumerate(chips)]
        for cp in first:
            cp.start()
        passed = [copy(4 + j, (*chip, c), sibling) for j, chip in enumerate(chips)]
        for j, chip in enumerate(chips):
            copy(1 + j, (*chip, c), me).wait_recv()
            passed[j].start()
        copy(0, sibling, me).wait_recv()
        for j, chip in enumerate(chips):
            copy(4 + j, (*chip, 1 - c), me).wait_recv()
        for cp in first + passed:
            cp.wait_send()
        mine.wait()
        acc = all_ref[0:rs, :]
        for d in range(1, N_DEV):
            acc = acc + all_ref[d * rs:(d + 1) * rs, :]
        sum_ref[...] = acc

    vmem = pl.BlockSpec(memory_space=pltpu.VMEM)
    return pl.pallas_call(
        body, name="allreduce_small", out_shape=jax.ShapeDtypeStruct((rs, 128), F32),
        in_specs=[vmem], out_specs=vmem,
        scratch_shapes=[pltpu.VMEM((N_DEV * rs, 128), F32), pltpu.SemaphoreType.DMA((7,)), pltpu.SemaphoreType.DMA((7,)),
                        pltpu.SemaphoreType.DMA],
    )(p)


WEIGHTS = ('ffn1_norm', 'ffn1_w_gate', 'ffn1_w_up', 'ffn1_w_down', 'mix_norm', 'w_in', 'q_a_norm', 'w_q_b',
           'kv_a_norm', 'w_kv_b', 'q_head_norm', 'k_head_norm', 'conv_w', 'conv_b', 'a_log_fwd', 'a_log_bwd',
           'dt_bias_fwd', 'dt_bias_bwd', 'd_skip', 'ssm_norm', 'w_attn_branch', 'w_ssm_branch', 'w_out',
           'ffn2_norm', 'ffn2_w_gate', 'ffn2_w_up', 'ffn2_w_down')
PACKED = (('ffn1_w_gate', (D_MODEL, D_FF), 1), ('ffn1_w_up', (D_MODEL, D_FF), 1), ('ffn1_w_down', (D_FF, D_MODEL), 0),
          ('w_in', (D_MODEL, sum(IN_SPLITS)), 1), ('w_q_b', (Q_LORA, N_HEADS * QK_HEAD), 1),
          ('w_kv_b', (KV_LORA, N_HEADS * (QK_NOPE + V_HEAD)), 1),
          ('w_attn_branch', (N_HEADS * V_HEAD, D_MODEL), 0), ('w_ssm_branch', (D_INNER, D_MODEL), 0),
          ('w_out', (D_MODEL, D_MODEL), 0),
          ('ffn2_w_gate', (D_MODEL, D_FF), 1), ('ffn2_w_up', (D_MODEL, D_FF), 1), ('ffn2_w_down', (D_FF, D_MODEL), 0))
PACK_W = 1024
N_CHIPS = 4
SMALL = (('ffn1_norm', 1024), ('mix_norm', 1024), ('q_a_norm', 384), ('kv_a_norm', 256), ('q_head_norm', 96),
         ('k_head_norm', 96), ('conv_b', 3072), ('a_log_fwd', 32), ('a_log_bwd', 32), ('dt_bias_fwd', 32),
         ('dt_bias_bwd', 32), ('d_skip', 32), ('ssm_norm', 2048), ('ffn2_norm', 1024),
         ('conv_w', CONV_WIDTH * XBC_DIM), ('loss', 1))


TRANSPOSED = ('ffn1_w_gate', 'ffn1_w_up', 'w_in', 'ffn2_w_gate', 'ffn2_w_up')


def _stored(name, a):
    return a.T if name in TRANSPOSED else a


def _shard_shape(name, shape, axis):
    sh = tuple(s // N_CHIPS if a == axis else s for a, s in enumerate(shape))
    return sh[::-1] if name in TRANSPOSED else sh


def _by_rows(name, axis):
    return name in TRANSPOSED or axis == 0


def _pack_layout():
    out, r = {}, 0
    for name, shape, axis in PACKED:
        n = math.prod(shape) // N_CHIPS // PACK_W
        out[name] = (r, n)
        r += n
    return out, -(-r // 64) * 64


def _pack(shards):
    layout, rows = _pack_layout()
    parts = [shards[name].reshape(-1, PACK_W) for name, _, _ in PACKED]
    parts.append(jnp.zeros((rows - sum(p.shape[0] for p in parts), PACK_W), parts[0].dtype))
    return jnp.concatenate(parts, axis=0)


def _unpack(packed):
    layout, _ = _pack_layout()
    return {name: packed[layout[name][0]:layout[name][0] + layout[name][1]].reshape(_shard_shape(name, shape, axis))
            for name, shape, axis in PACKED}


def _full_from_slots(slots):
    layout, _ = _pack_layout()
    out = {}
    for name, shape, axis in PACKED:
        r, n = layout[name]
        if _by_rows(name, axis):
            out[name] = slots[:, r:r + n].reshape(N_CHIPS * n, PACK_W)
        else:
            sh = _shard_shape(name, shape, axis)
            out[name] = jnp.concatenate([slots[j, r:r + n].reshape(sh) for j in range(N_CHIPS)], axis=axis)
    return out


def _slots_from_full(full):
    layout, rows = _pack_layout()
    parts = []
    for name, shape, axis in PACKED:
        r, n = layout[name]
        if _by_rows(name, axis):
            parts.append(full[name].reshape(N_CHIPS, n, PACK_W))
        else:
            size = shape[axis] // N_CHIPS
            parts.append(jnp.stack([lax.slice_in_dim(full[name], j * size, (j + 1) * size, axis=axis).reshape(n, PACK_W)
                                    for j in range(N_CHIPS)]))
    parts.append(jnp.zeros((N_CHIPS, rows - sum(p.shape[1] for p in parts), PACK_W), parts[0].dtype))
    return jnp.concatenate(parts, axis=1)


def _pack_small(vals):
    parts = []
    for name, n in SMALL:
        pad = -(-n // 128) * 128 - n
        parts.append(jnp.pad(vals[name].reshape(-1).astype(F32), (0, pad)).reshape(-1, 128))
    rows = sum(p.shape[0] for p in parts)
    parts.append(jnp.zeros((-(-rows // 8) * 8 - rows, 128), F32))
    return jnp.concatenate(parts, axis=0)


def _unpack_small(packed):
    out, r = {}, 0
    for name, n in SMALL:
        k = -(-n // 128)
        out[name] = packed[r:r + k].reshape(-1)[:n]
        r += k
    return out


def _pad_heads(w, axis, per_head, lo, hi):
    shape = w.shape
    w = w.reshape(shape[:axis] + (N_HEADS, per_head) + shape[axis + 1:])
    w = lax.slice_in_dim(w, lo, hi, axis=axis + 1)
    pad = [(0, 0)] * w.ndim
    pad[axis + 1] = (0, HP - (hi - lo))
    w = jnp.pad(w, pad)
    return w.reshape(shape[:axis] + (N_HEADS * HP,) + shape[axis + 1:])


def _unpad_heads(w, axis, keep):
    shape = w.shape
    w = w.reshape(shape[:axis] + (N_HEADS, HP) + shape[axis + 1:])
    return lax.slice_in_dim(w, 0, keep, axis=axis + 1)


def _pad_w_in(wt):
    o = [0]
    for s in IN_SPLITS:
        o.append(o[-1] + s)
    cq, ckv, kpe, z, xbc, dtf, dtb, ga, gb = [wt[o[i]:o[i + 1]] for i in range(len(IN_SPLITS))]
    kpe_pad = jnp.pad(kpe, ((QK_NOPE, HP - QK_HEAD), (0, 0)))
    dt_pad = jnp.pad(jnp.concatenate([dtf, dtb], axis=0), ((0, HP - 2 * SSM_HEADS), (0, 0)))
    return jnp.concatenate([z, ga, gb, xbc, cq, ckv, kpe_pad, dt_pad], axis=0)


def _unpad_w_in(gt):
    z, ga, gb, xbc = gt[U_Z:U_GA], gt[U_GA:U_GB], gt[U_GB:U_XBC], gt[U_XBC:U_SMALL]
    s = gt[U_SMALL:]
    cq, ckv = s[S_CQ:S_CKV], s[S_CKV:S_KPE]
    kpe = s[S_KPE + QK_NOPE:S_KPE + QK_HEAD]
    dtf, dtb = s[S_DT:S_DT + SSM_HEADS], s[S_DT + SSM_HEADS:S_DT + 2 * SSM_HEADS]
    return jnp.concatenate([cq, ckv, kpe, z, xbc, dtf, dtb, ga, gb], axis=0)


def _lanes128(parts):
    row = jnp.concatenate([p.reshape(-1) for p in parts])
    return jnp.pad(row, (0, HP - row.shape[0])).reshape(1, HP)


FF_TILE = D_FF // 2
WGRAD = BF


def _ffn_fwd(x, g, wg_t, wu_t, wd, tag):
    h = _rms_fwd(x, g, name=tag + "_norm")
    gate, up, act = _mm([h], [wg_t, wu_t], name=tag + "_up", tb=True, out_dtypes=(F32, F32, BF), tm=512, tn=FF_TILE,
                        epilogue=lambda a, b: (a, b, _silu(a) * b))
    out = _mm([act], [wd], name=tag + "_down", extras=[x], epilogue=lambda acc, r: (r + 0.5 * acc,))
    return out, (h, gate, up, act)


def _ffn_bwd(dout, dout_bf, x, g, wg_t, wu_t, wd, saved, tag):
    h, gate, up, act = saved
    dgate, dup = _mm([dout_bf], [wd], name=tag + "_down_dx", tb=True, extras=[gate, up], out_dtypes=(BF, BF),
                     tm=512, tn=FF_TILE, epilogue=lambda acc, a, b: (0.5 * acc * b * _dsilu(a), 0.5 * acc * _silu(a)))
    dwd = _mm([act], [dout_bf], name=tag + "_down_dw", ta=True, tm=FF_TILE, tk=1024, out_dtypes=(WGRAD,),
              epilogue=lambda acc: (0.5 * acc,))
    dwg_t, dwu_t = _mm([dgate, dup], [h, h], name=tag + "_up_dw", ta=True, separate=True, out_dtypes=(WGRAD, WGRAD),
                       tm=FF_TILE, tk=1024)
    dh = _mm([dgate, dup], [wg_t, wu_t], name=tag + "_up_dx")
    dx, dx_bf, dg = _rms_bwd(dh, x, g, name=tag + "_norm_bwd", add=dout, out_dtypes=(F32, BF))
    return dx, dx_bf, dg, dwg_t, dwu_t, dwd


KPE_BLK = (U_SMALL + S_KPE) // HP
SMALL_BLK = U_SMALL // SMALL_W


def _local_step(x, pos_col, target, W, P):
    T = x.shape[0]
    sig = jax.nn.sigmoid
    x1, ffn1 = _ffn_fwd(x, P["ffn1_norm"], W["wg1"], W["wu1"], W["wd1"], "ffn1")
    h = _rms_fwd(x1, P["mix_norm"], name="mix_norm")
    u = _mm([h], [W["w_in"]], name="in_proj", tb=True, tn=1152)
    cqn = _rms_fwd(u, P["q_a_norm"], name="q_a_norm", blk_w=SMALL_W, blk_idx=SMALL_BLK, off=S_CQ, width=Q_LORA)
    ckvn = _rms_fwd(u, P["kv_a_norm"], name="kv_a_norm", blk_w=SMALL_W, blk_idx=SMALL_BLK, off=S_CKV, width=KV_LORA)
    q_raw = _mm([cqn], [W["wq"]], name="q_proj")
    def with_ones_lane(acc_k, acc_v):
        lane = lax.broadcasted_iota(jnp.int32, acc_v.shape, 1)
        return acc_k, jnp.where((lane & (HP - 1)) == V_HEAD, 1.0, acc_v)

    k_raw, v = _mm([ckvn], [W["wk"], W["wv"]], name="kv_proj", out_dtypes=(F32, BF), epilogue=with_ones_lane)
    rc, rs = _rope_tables(pos_col, P["freq"])
    q = _qk_prep_fwd(q_raw, None, P["q_head_norm"], rc, rs, name="q_prep", out_scale=Q_SCALE)
    k = _qk_prep_fwd(k_raw, u, P["k_head_norm"], rc, rs, name="k_prep", kpe_blk=KPE_BLK)
    o, lse = _attn_fwd(q, k, v)
    pre, act = _conv_fwd(u, P["conv_w8"], P["conv_b"])
    scan_cols, scan_rows = _ssd_prep(u, P["dt_bias8"], P["a_log8"])
    y_f, st_f = _ssd_fwd(act, scan_cols, scan_rows, rev=False, name="ssd_fwd_f")
    y_b, st_b = _ssd_fwd(act, scan_cols, scan_rows, rev=True, name="ssd_fwd_b")
    ysum, m = _ssm_combine_fwd(y_f, y_b, act, u, P["d_skip_lanes"], P["ssm_norm"])
    ab = _mm([o], [W["pa"]], name="attn_branch")
    mb, merged = _mm([m], [W["pb"]], name="ssm_branch", extras=[ab, u, u], extra_offs=(0, U_GA, U_GB), out_dtypes=(F32, BF),
                     epilogue=lambda acc, a, ga, gb: (acc, sig(ga) * a + sig(gb) * acc))
    x2 = _mm([merged], [W["wo"]], name="out_proj", extras=[x1], epilogue=lambda acc, r: (r + acc,))
    y, ffn2 = _ffn_fwd(x2, P["ffn2_norm"], W["wg2"], W["wu2"], W["wd2"], "ffn2")
    dy, dy_bf, loss = _loss_head(y, target)
    dx2, dx2_bf, dg_ffn2, dwg2, dwu2, dwd2 = _ffn_bwd(dy, dy_bf, x2, P["ffn2_norm"], W["wg2"], W["wu2"], W["wd2"], ffn2,
                                                      "ffn2")

    def gate_bwd(dmrg, a, b, ga, gb):
        sa, sb = sig(ga), sig(gb)
        return dmrg * sa, dmrg * sb, dmrg * a * sa * (1.0 - sa), dmrg * b * sb * (1.0 - sb)

    dab, dmb, dga, dgb = _mm([dx2_bf], [W["wo"]], name="out_proj_dx", tb=True, extras=[ab, mb, u, u],
                             extra_offs=(0, 0, U_GA, U_GB), out_dtypes=(BF,) * 4, epilogue=gate_bwd)
    dwo = _mm([merged], [dx2_bf], name="out_proj_dw", ta=True, out_dtypes=(WGRAD,))
    dpa = _mm([o], [dab], name="attn_branch_dw", ta=True, out_dtypes=(WGRAD,))
    do = _mm([dab], [W["pa"]], name="attn_branch_dx", tb=True)
    dpb = _mm([m], [dmb], name="ssm_branch_dw", ta=True, out_dtypes=(WGRAD,))
    dm = _mm([dmb], [W["pb"]], name="ssm_branch_dx", tb=True)
    dyssd, dz, dxs_skip, dg_ssm, dskip = _ssm_combine_bwd(dm, ysum, act, u, P["d_skip_lanes"], P["ssm_norm"])
    dxs_f, db_f, dc_f, dsel_f, dtot_f = _ssd_bwd(act, scan_cols, scan_rows, st_f, dyssd, rev=False, name="ssd_bwd_f")
    dxs_b, db_b, dc_b, dsel_b, dtot_b = _ssd_bwd(act, scan_cols, scan_rows, st_b, dyssd, rev=True, name="ssd_bwd_b")
    ddt, dalog, dbias = _ssd_prep_bwd(u, P["dt_bias8"], P["a_log8"], dsel_f, dtot_f, dsel_b, dtot_b)
    dxbc, dconv = [], []
    for tag, col0, parts in (("x", 0, [dxs_f, dxs_b, dxs_skip]), ("b", D_INNER, [db_f, db_b]),
                             ("c", D_INNER + SSM_GROUPS * D_STATE, [dc_f, dc_b])):
        dpre = _conv_dpre(parts, pre, col0, name="conv_dpre_" + tag)
        dxp, dwp = _conv_bwd(dpre, u, P["conv_w8"], col0, name="conv_bwd_" + tag)
        dxbc.append(dxp)
        dconv.append(dwp)
    dconv = jnp.concatenate(dconv, axis=1)
    dq, dk, dv = _attn_bwd(q, k, v, do, o, lse)
    dq_raw, dg_qh = _qk_prep_bwd(dq, q_raw, None, P["q_head_norm"], rc, rs, name="q_prep_bwd", in_scale=ATTN_SCALE)
    dk_raw, dg_kh, dkpe = _qk_prep_bwd(dk, k_raw, u, P["k_head_norm"], rc, rs, name="k_prep_bwd", kpe_blk=KPE_BLK,
                                       in_scale=1.0 / LOG2E)
    dwq = _mm([cqn], [dq_raw], name="q_proj_dw", ta=True, out_dtypes=(WGRAD,))
    dcqn = _mm([dq_raw], [W["wq"]], name="q_proj_dx", tb=True)
    dwk, dwv = _mm([ckvn], [dk_raw, dv], name="kv_proj_dw", ta=True, out_dtypes=(WGRAD, WGRAD))
    dckvn = _mm([dk_raw, dv], [W["wk"], W["wv"]], name="kv_proj_dx", tb=True)
    dcq, dg_qa = _rms_bwd(dcqn, u, P["q_a_norm"], name="q_a_norm_bwd", blk_w=SMALL_W, blk_idx=SMALL_BLK, off=S_CQ,
                          width=Q_LORA, out_dtypes=(BF,))
    dckv, dg_kva = _rms_bwd(dckvn, u, P["kv_a_norm"], name="kv_a_norm_bwd", blk_w=SMALL_W, blk_idx=SMALL_BLK,
                            off=S_CKV, width=KV_LORA, out_dtypes=(BF,))
    du = jnp.concatenate([dz, dga, dgb] + dxbc + [dcq, dckv, dkpe.astype(BF), ddt.astype(BF)], axis=1)
    dw_in = _mm([du], [h], name="in_proj_dw", ta=True, tm=1152, out_dtypes=(WGRAD,))
    dh = _mm([du], [W["w_in"]], name="in_proj_dx")
    dx1, dx1_bf, dg_mix = _rms_bwd(dh, x1, P["mix_norm"], name="mix_norm_bwd", add=dx2, out_dtypes=(F32, BF))
    dx, _, dg_ffn1, dwg1, dwu1, dwd1 = _ffn_bwd(dx1, dx1_bf, x, P["ffn1_norm"], W["wg1"], W["wu1"], W["wd1"], ffn1, "ffn1")
    dW = dict(wg1=dwg1, wu1=dwu1, wd1=dwd1, w_in=dw_in, wq=dwq, wk=dwk, wv=dwv, pa=dpa, pb=dpb, wo=dwo,
              wg2=dwg2, wu2=dwu2, wd2=dwd2)
    dP = dict(ffn1_norm=dg_ffn1[0], mix_norm=dg_mix[0], q_a_norm=dg_qa[0], kv_a_norm=dg_kva[0],
              q_head_norm=dg_qh[0, :QK_HEAD], k_head_norm=dg_kh[0, :QK_HEAD], conv_b=dconv[CONV_WIDTH],
              a_log_fwd=dalog[0, :SSM_HEADS], a_log_bwd=dalog[0, SSM_HEADS:2 * SSM_HEADS],
              dt_bias_fwd=dbias[0, :SSM_HEADS], dt_bias_bwd=dbias[0, SSM_HEADS:2 * SSM_HEADS],
              d_skip=dskip[0].reshape(SSM_GROUPS, HP)[:, :HG], ssm_norm=dg_ssm[0], ffn2_norm=dg_ffn2[0],
              conv_w=dconv[:CONV_WIDTH], loss=loss[0, 0])
    return dx, dW, dP


def _prepare(w, conv_w_full):
    kvb = w["w_kv_b"]
    W = dict(wg1=w["ffn1_w_gate"], wu1=w["ffn1_w_up"], wd1=w["ffn1_w_down"], w_in=_pad_w_in(w["w_in"]),
             wq=_pad_heads(w["w_q_b"], 1, QK_HEAD, 0, QK_HEAD),
             wk=_pad_heads(kvb, 1, QK_NOPE + V_HEAD, 0, QK_NOPE),
             wv=_pad_heads(kvb, 1, QK_NOPE + V_HEAD, QK_NOPE, QK_NOPE + V_HEAD),
             pa=_pad_heads(w["w_attn_branch"], 0, V_HEAD, 0, V_HEAD), pb=w["w_ssm_branch"], wo=w["w_out"],
             wg2=w["ffn2_w_gate"], wu2=w["ffn2_w_up"], wd2=w["ffn2_w_down"])
    inv_freq = [1.0 / (ROPE_BASE ** (j / QK_ROPE)) for j in range(0, QK_ROPE, 2)]
    freq = [0.0] * QK_NOPE + inv_freq + inv_freq + [0.0] * (HP - QK_HEAD)
    P = {n: w[n] for n in ("ffn1_norm", "mix_norm", "q_a_norm", "kv_a_norm", "ssm_norm", "ffn2_norm", "conv_b")}
    P.update(q_head_norm=_lanes128([w["q_head_norm"]]), k_head_norm=_lanes128([w["k_head_norm"]]),
             conv_w8=jnp.pad(conv_w_full, ((0, 8 - CONV_WIDTH), (0, 0))),
             dt_bias8=jnp.broadcast_to(_lanes128([w["dt_bias_fwd"], w["dt_bias_bwd"]]), (8, HP)),
             a_log8=jnp.broadcast_to(_lanes128([w["a_log_fwd"], w["a_log_bwd"]]), (8, HP)),
             d_skip_lanes=jnp.repeat(w["d_skip"].reshape(-1), PH).reshape(1, D_INNER),
             freq=jnp.asarray(freq, F32).reshape(1, HP))
    return W, P


def _unprepare(dW):
    dkvb = jnp.concatenate([_unpad_heads(dW["wk"], 1, QK_NOPE), _unpad_heads(dW["wv"], 1, V_HEAD)], axis=2)
    return dict(ffn1_w_gate=dW["wg1"], ffn1_w_up=dW["wu1"], ffn1_w_down=dW["wd1"], w_in=_unpad_w_in(dW["w_in"]),
                w_q_b=_unpad_heads(dW["wq"], 1, QK_HEAD).reshape(Q_LORA, N_HEADS * QK_HEAD),
                w_kv_b=dkvb.reshape(KV_LORA, N_HEADS * (QK_NOPE + V_HEAD)),
                w_attn_branch=_unpad_heads(dW["pa"], 0, V_HEAD).reshape(N_HEADS * V_HEAD, D_MODEL),
                w_ssm_branch=dW["pb"], w_out=dW["wo"],
                ffn2_w_gate=dW["wg2"], ffn2_w_up=dW["wu2"], ffn2_w_down=dW["wd2"])


def kernel(x, positions, ffn1_norm, ffn1_w_gate, ffn1_w_up, ffn1_w_down, mix_norm, w_in, q_a_norm, w_q_b, kv_a_norm, w_kv_b, q_head_norm, k_head_norm, conv_w, conv_b, a_log_fwd, a_log_bwd, dt_bias_fwd, dt_bias_bwd, d_skip, ssm_norm, w_attn_branch, w_ssm_branch, w_out, ffn2_norm, ffn2_w_gate, ffn2_w_up, ffn2_w_down, loss_target, m_ffn1_norm, m_ffn1_w_gate, m_ffn1_w_up, m_ffn1_w_down, m_mix_norm, m_w_in, m_q_a_norm, m_w_q_b, m_kv_a_norm, m_w_kv_b, m_q_head_norm, m_k_head_norm, m_conv_w, m_conv_b, m_a_log_fwd, m_a_log_bwd, m_dt_bias_fwd, m_dt_bias_bwd, m_d_skip, m_ssm_norm, m_w_attn_branch, m_w_ssm_branch, m_w_out, m_ffn2_norm, m_ffn2_w_gate, m_ffn2_w_up, m_ffn2_w_down, v_ffn1_norm, v_ffn1_w_gate, v_ffn1_w_up, v_ffn1_w_down, v_mix_norm, v_w_in, v_q_a_norm, v_w_q_b, v_kv_a_norm, v_w_kv_b, v_q_head_norm, v_k_head_norm, v_conv_w, v_conv_b, v_a_log_fwd, v_a_log_bwd, v_dt_bias_fwd, v_dt_bias_bwd, v_d_skip, v_ssm_norm, v_w_attn_branch, v_w_ssm_branch, v_w_out, v_ffn2_norm, v_ffn2_w_gate, v_ffn2_w_up, v_ffn2_w_down):
    given = dict(locals())
    T = x.shape[1]
    packed_names = [name for name, _, _ in PACKED]

    def two_d(a):
        return a.reshape(a.shape[1], -1) if a.ndim > 2 else a

    def kept(n, a):
        return _stored(n, two_d(a))

    w_loc = {n: kept(n, given[n]) for n in WEIGHTS}
    wb = _pack({n: w_loc[n].astype(BF) for n in packed_names})
    wf = jnp.pad(w_loc["conv_w"], ((0, 8 - CONV_WIDTH), (0, 0)))
    gb, gf = _gather_chips(wb, wf)
    full = _full_from_slots(gb)
    conv_w_full = jnp.concatenate([gf[j, :CONV_WIDTH] for j in range(N_CHIPS)], axis=1)
    full.update({n: w_loc[n] for n in WEIGHTS if n not in full and n != "conv_w"})
    W, P = _prepare(full, conv_w_full)
    dx, dW, dP = _local_step(x.reshape(T, D_MODEL), positions.reshape(T, 1).astype(F32), loss_target.reshape(T, D_MODEL), W, P)
    gp = _slots_from_full(_unprepare(dW))
    core = lax.axis_index("c").astype(jnp.int32).reshape(1)
    both_cores = _add_halves(gp, _halves_to_sibling(gp), core)
    grads = _unpack(_join_halves(_sum_slots(_exchange_chips(both_cores))))
    small = _unpack_small(_allreduce_small(_pack_small(dP)))
    grads.update({n: small[n].reshape(1, -1) for n, _ in SMALL if n not in ("conv_w", "loss")})
    grads["conv_w"] = lax.dynamic_slice_in_dim(small["conv_w"].reshape(CONV_WIDTH, XBC_DIM), _my_chip() * (XBC_DIM // N_CHIPS),
                                               XBC_DIM // N_CHIPS, axis=1)
    out_g, out_d, out_m, out_v = [], [], [], []
    for n in WEIGHTS:
        shape = given[n].shape
        delta, new_m, new_v = _adamw(w_loc[n], grads[n], kept(n, given["m_" + n]), kept(n, given["v_" + n]), name="adamw_" + n)
        for outs, a in ((out_g, grads[n]), (out_d, delta), (out_m, new_m), (out_v, new_v)):
            outs.append(_stored(n, a).reshape(shape))
    return (small["loss"].reshape(()), dx.reshape(x.shape), *out_g, *out_d, *out_m, *out_v)
```

```python
import functools
import math

import jax
import jax.numpy as jnp
from jax import lax
from jax.experimental import pallas as pl
from jax.experimental.pallas import tpu as pltpu

BF = jnp.bfloat16
F32 = jnp.float32
HI = lax.Precision.HIGHEST
MESH = pl.DeviceIdType.MESH

D_MODEL = 1024
D_FF = 2816
EPS = 1e-6
N_HEADS = 16
QK_NOPE = 64
QK_ROPE = 32
QK_HEAD = 96
V_HEAD = 64
Q_LORA = 384
KV_LORA = 256
ROPE_BASE = 10000.0
D_INNER = 2048
SSM_HEADS = 32
SSM_GROUPS = 4
D_STATE = 128
CONV_WIDTH = 5
CHUNK = 128
XBC_DIM = 3072
HP = 128
GW = D_INNER // SSM_GROUPS
HG = SSM_HEADS // SSM_GROUPS
PH = 64
U_Z, U_GA, U_GB, U_XBC, U_SMALL = 0, 2048, 3072, 4096, 7168
S_CQ, S_CKV, S_KPE, S_DT, SMALL_W = 0, 384, 640, 768, 896
U_PAD = U_SMALL + SMALL_W
IN_SPLITS = (Q_LORA, KV_LORA, QK_ROPE, D_INNER, XBC_DIM, SSM_HEADS, SSM_HEADS, D_MODEL, D_MODEL)

ADAM_LR = 0.001
ADAM_B1 = 0.9
ADAM_B2 = 0.999
ADAM_EPS = 1e-08
ADAM_WD = 0.01
ADAM_STEP = 10

NN = (((1,), (0,)), ((), ()))
NT = (((1,), (1,)), ((), ()))
TN = (((0,), (0,)), ((), ()))


def _pick(n, pref):
    best = None
    d = 128
    while d <= min(n, pref):
        if n % d == 0:
            best = d
        d += 128
    return best if best is not None else n


def _silu(x):
    return x * jax.nn.sigmoid(x)


def _dsilu(x):
    s = jax.nn.sigmoid(x)
    return s * (1.0 + x * (1.0 - s))


def _softplus(x):
    return jnp.maximum(x, 0.0) + jnp.log(1.0 + jnp.exp(-jnp.abs(x)))


def _mm(As, Bs, *, name, ta=False, tb=False, out_dtypes=(F32,), epilogue=None, extras=(), extra_offs=None,
        tm=1024, tn=512, tk=2048, separate=False):
    As, Bs, extras = list(As), list(Bs), list(extras)
    a0, b0 = As[0], Bs[0]
    M, K = (a0.shape[1], a0.shape[0]) if ta else a0.shape
    N = b0.shape[0] if tb else b0.shape[1]
    tm, tn, tk = _pick(M, tm), _pick(N, tn), _pick(K, tk)
    nk = K // tk
    n_a, n_b, n_e, n_o = len(As), len(Bs), len(extras), len(out_dtypes)
    n_acc = (n_b if n_a == 1 or separate else 1) if nk > 1 else 0
    if extra_offs is None:
        extra_offs = (0,) * n_e
    dn = (((0,) if ta else (1,), (1,) if tb else (0,)), ((), ()))
    bytes_a = sum(a.size * a.dtype.itemsize for a in As)
    bytes_b = sum(b.size * b.dtype.itemsize for b in Bs)
    n_outer = (N // tn) * bytes_a + bytes_b < (M // tm) * bytes_b + bytes_a

    def products(a_refs, b_refs):
        if n_a == 1:
            a = a_refs[0][...].astype(BF)
            return [lax.dot_general(a, b[...].astype(BF), dn, preferred_element_type=F32) for b in b_refs]
        if separate:
            return [lax.dot_general(a[...].astype(BF), b[...].astype(BF), dn, preferred_element_type=F32)
                    for a, b in zip(a_refs, b_refs)]
        total = None
        for a, b in zip(a_refs, b_refs):
            p = lax.dot_general(a[...].astype(BF), b[...].astype(BF), dn, preferred_element_type=F32)
            total = p if total is None else total + p
        return [total]

    def finish(accs, e_refs, o_refs):
        ex = [e[...] for e in e_refs]
        outs = epilogue(*accs, *ex) if epilogue is not None else tuple(accs)
        for o_ref, val in zip(o_refs, outs):
            o_ref[...] = val.astype(o_ref.dtype)

    def body(*refs):
        a_refs, b_refs = refs[:n_a], refs[n_a:n_a + n_b]
        e_refs = refs[n_a + n_b:n_a + n_b + n_e]
        o_refs = refs[n_a + n_b + n_e:n_a + n_b + n_e + n_o]
        acc_refs = refs[n_a + n_b + n_e + n_o:]
        if nk == 1:
            finish(products(a_refs, b_refs), e_refs, o_refs)
            return
        k = pl.program_id(2)

        @pl.when(k == 0)
        def _():
            for acc in acc_refs:
                acc[...] = jnp.zeros_like(acc)

        for acc, p in zip(acc_refs, products(a_refs, b_refs)):
            acc[...] += p

        @pl.when(k == nk - 1)
        def _():
            finish([acc[...] for acc in acc_refs], e_refs, o_refs)

    def at(f):
        return (lambda j, i, k: f(i, j, k)) if n_outer else f

    a_spec = pl.BlockSpec((tk, tm), at(lambda i, j, k: (k, i))) if ta else pl.BlockSpec((tm, tk), at(lambda i, j, k: (i, k)))
    b_spec = pl.BlockSpec((tn, tk), at(lambda i, j, k: (j, k))) if tb else pl.BlockSpec((tk, tn), at(lambda i, j, k: (k, j)))
    e_specs = [pl.BlockSpec((tm, tn), at(functools.partial(lambda i, j, k, o: (i, j + o), o=off // tn))) for off in extra_offs]
    for off in extra_offs:
        assert off % tn == 0
    outs = pl.pallas_call(
        body, name=name,
        out_shape=tuple(jax.ShapeDtypeStruct((M, N), dt) for dt in out_dtypes),
        grid=(N // tn, M // tm, nk) if n_outer else (M // tm, N // tn, nk),
        in_specs=[a_spec] * n_a + [b_spec] * n_b + e_specs,
        out_specs=tuple(pl.BlockSpec((tm, tn), at(lambda i, j, k: (i, j))) for _ in out_dtypes),
        scratch_shapes=[pltpu.VMEM((tm, tn), F32)] * n_acc,
        compiler_params=pltpu.CompilerParams(dimension_semantics=("parallel", "parallel", "arbitrary")),
    )(*As, *Bs, *extras)
    return outs[0] if n_o == 1 else outs


def _rms_fwd(x, g, *, name, blk_w=None, blk_idx=0, off=0, width=None, out_dtype=BF):
    T = x.shape[0]
    blk_w = x.shape[1] if blk_w is None else blk_w
    width = blk_w if width is None else width
    tt = _pick(T, 512)

    def body(x_ref, g_ref, o_ref):
        xf = x_ref[:, off:off + width]
        r = lax.rsqrt(jnp.mean(xf * xf, axis=-1, keepdims=True) + EPS)
        o_ref[...] = (xf * r * g_ref[...]).astype(o_ref.dtype)

    return pl.pallas_call(
        body, name=name, out_shape=jax.ShapeDtypeStruct((T, width), out_dtype), grid=(T // tt,),
        in_specs=[pl.BlockSpec((tt, blk_w), lambda i: (i, blk_idx)), pl.BlockSpec((1, width), lambda i: (0, 0))],
        out_specs=pl.BlockSpec((tt, width), lambda i: (i, 0)),
    )(x, g)


def _rms_bwd(dy, x, g, *, name, blk_w=None, blk_idx=0, off=0, width=None, add=None, out_dtypes=(F32,)):
    T = x.shape[0]
    blk_w = x.shape[1] if blk_w is None else blk_w
    width = blk_w if width is None else width
    tt = _pick(T, 512)
    has_add = add is not None
    n_dx = len(out_dtypes)

    def body(*refs):
        dy_ref, x_ref, g_ref = refs[:3]
        dx_refs, dg_ref = refs[3 + has_add:3 + has_add + n_dx], refs[-1]
        xf = x_ref[:, off:off + width]
        d = dy_ref[...].astype(F32)
        r = lax.rsqrt(jnp.mean(xf * xf, axis=-1, keepdims=True) + EPS)
        gd = d * g_ref[...]
        dx = r * gd - xf * (r * r * r) * jnp.mean(gd * xf, axis=-1, keepdims=True)
        if has_add:
            dx = dx + refs[3][...]
        for dx_ref in dx_refs:
            dx_ref[...] = dx.astype(dx_ref.dtype)

        @pl.when(pl.program_id(0) == 0)
        def _():
            dg_ref[...] = jnp.zeros_like(dg_ref)

        dg_ref[...] += jnp.broadcast_to(jnp.sum(d * xf * r, axis=0, keepdims=True), dg_ref.shape)

    row = pl.BlockSpec((tt, width), lambda i: (i, 0))
    in_specs = [row, pl.BlockSpec((tt, blk_w), lambda i: (i, blk_idx)), pl.BlockSpec((1, width), lambda i: (0, 0))]
    args = [dy, x, g]
    if has_add:
        in_specs.append(row)
        args.append(add)
    return pl.pallas_call(
        body, name=name,
        out_shape=tuple(jax.ShapeDtypeStruct((T, width), dt) for dt in out_dtypes) + (jax.ShapeDtypeStruct((8, width), F32),),
        grid=(T // tt,), in_specs=in_specs,
        out_specs=(row,) * n_dx + (pl.BlockSpec((8, width), lambda i: (0, 0)),),
        compiler_params=pltpu.CompilerParams(dimension_semantics=("arbitrary",)),
    )(*args)


def _rope_tables(pos_col, freq_lane):
    T = pos_col.shape[0]
    tt = _pick(T, 512)

    def body(p_ref, f_ref, c_ref, s_ref):
        ang = p_ref[...] * f_ref[...]
        lane = lax.broadcasted_iota(jnp.int32, ang.shape, 1)
        c_ref[...] = jnp.where(lane < QK_HEAD, jnp.cos(ang), 0.0)
        sn = jnp.sin(ang)
        s_ref[...] = jnp.where((lane >= QK_NOPE) & (lane < QK_NOPE + 16), -sn,
                               jnp.where((lane >= QK_NOPE + 16) & (lane < QK_HEAD), sn, 0.0))

    return pl.pallas_call(
        body, name="rope_tables", out_shape=(jax.ShapeDtypeStruct((T, HP), F32),) * 2, grid=(T // tt,),
        in_specs=[pl.BlockSpec((tt, 1), lambda i: (i, 0)), pl.BlockSpec((1, HP), lambda i: (0, 0))],
        out_specs=(pl.BlockSpec((tt, HP), lambda i: (i, 0)),) * 2,
    )(pos_col, freq_lane)


def _swap_rope_halves(n):
    src = lax.broadcasted_iota(jnp.int32, (HP, HP), 0)
    dst = lax.broadcasted_iota(jnp.int32, (HP, HP), 1)
    lo = (dst >= QK_NOPE) & (dst < QK_NOPE + 16) & (src == dst + 16)
    hi = (dst >= QK_NOPE + 16) & (dst < QK_HEAD) & (src == dst - 16)
    return _split_dot(n, jnp.where(lo | hi, 1.0, 0.0).astype(BF), 2)


def _qk_prep_fwd(raw, kpe, gain, C, S, *, name, kpe_blk=0, out_scale=1.0):
    T = raw.shape[0]
    tt = _pick(T, 256)
    has_kpe = kpe is not None

    def body(*refs):
        if has_kpe:
            raw_ref, kpe_ref, g_ref, c_ref, s_ref, o_ref = refs
        else:
            raw_ref, g_ref, c_ref, s_ref, o_ref = refs
        for h in range(N_HEADS):
            hs = slice(HP * h, HP * (h + 1))
            xr = raw_ref[:, hs] + kpe_ref[...] if has_kpe else raw_ref[:, hs]
            r = lax.rsqrt(jnp.sum(xr * xr, axis=-1, keepdims=True) * (1.0 / QK_HEAD) + EPS)
            n = xr * r * g_ref[...]
            o_ref[:, hs] = ((n * c_ref[...] + _swap_rope_halves(n) * s_ref[...]) * out_scale).astype(o_ref.dtype)

    heads = pl.BlockSpec((tt, N_HEADS * HP), lambda i: (i, 0))
    shared = pl.BlockSpec((tt, HP), lambda i: (i, 0))
    kpe_spec = pl.BlockSpec((tt, HP), lambda i: (i, kpe_blk))
    in_specs = [heads] + ([kpe_spec] if has_kpe else []) + [pl.BlockSpec((1, HP), lambda i: (0, 0)), shared, shared]
    args = [raw] + ([kpe] if has_kpe else []) + [gain, C, S]
    return pl.pallas_call(
        body, name=name, out_shape=jax.ShapeDtypeStruct(raw.shape, BF), grid=(T // tt,),
        in_specs=in_specs, out_specs=heads,
    )(*args)


def _qk_prep_bwd(dout, raw, kpe, gain, C, S, *, name, kpe_blk=0, in_scale=1.0):
    T = raw.shape[0]
    tt = _pick(T, 256)
    has_kpe = kpe is not None

    def body(*refs):
        if has_kpe:
            d_ref, raw_ref, kpe_ref, g_ref, c_ref, s_ref, dx_ref, dg_ref, dkpe_ref = refs
        else:
            d_ref, raw_ref, g_ref, c_ref, s_ref, dx_ref, dg_ref = refs
        dg = jnp.zeros((1, HP), F32)
        dkpe = jnp.zeros((tt, HP), F32)
        for h in range(N_HEADS):
            hs = slice(HP * h, HP * (h + 1))
            xr = raw_ref[:, hs] + kpe_ref[...] if has_kpe else raw_ref[:, hs]
            d = d_ref[:, hs].astype(F32) * in_scale
            r = lax.rsqrt(jnp.sum(xr * xr, axis=-1, keepdims=True) * (1.0 / QK_HEAD) + EPS)
            dn = d * c_ref[...] + _swap_rope_halves(d * s_ref[...])
            gd = dn * g_ref[...]
            dx = r * gd - xr * (r * r * r) * (jnp.sum(gd * xr, axis=-1, keepdims=True) * (1.0 / QK_HEAD))
            dx_ref[:, hs] = dx.astype(dx_ref.dtype)
            dg = dg + jnp.sum(dn * xr * r, axis=0, keepdims=True)
            dkpe = dkpe + dx

        @pl.when(pl.program_id(0) == 0)
        def _():
            dg_ref[...] = jnp.zeros_like(dg_ref)

        dg_ref[...] += jnp.broadcast_to(dg, dg_ref.shape)
        if has_kpe:
            dkpe_ref[...] = dkpe

    heads = pl.BlockSpec((tt, N_HEADS * HP), lambda i: (i, 0))
    shared = pl.BlockSpec((tt, HP), lambda i: (i, 0))
    kpe_spec = pl.BlockSpec((tt, HP), lambda i: (i, kpe_blk))
    in_specs = [heads, heads] + ([kpe_spec] if has_kpe else []) + [pl.BlockSpec((1, HP), lambda i: (0, 0)), shared, shared]
    args = [dout, raw] + ([kpe] if has_kpe else []) + [gain, C, S]
    out_shape = [jax.ShapeDtypeStruct(raw.shape, BF), jax.ShapeDtypeStruct((8, HP), F32)]
    out_specs = [heads, pl.BlockSpec((8, HP), lambda i: (0, 0))]
    if has_kpe:
        out_shape.append(jax.ShapeDtypeStruct((T, HP), F32))
        out_specs.append(shared)
    return pl.pallas_call(
        body, name=name, out_shape=tuple(out_shape), grid=(T // tt,),
        in_specs=in_specs, out_specs=tuple(out_specs),
        compiler_params=pltpu.CompilerParams(dimension_semantics=("arbitrary",)),
    )(*args)


ATTN_SCALE = 1.0 / math.sqrt(QK_HEAD)
LOG2E = 1.0 / math.log(2.0)
Q_SCALE = ATTN_SCALE * LOG2E


def _attn_fwd(q, k, v):
    T = q.shape[0]
    tq = _pick(T, 256)

    def body(q_ref, k_ref, v_ref, o_ref, lse_ref):
        s = lax.dot_general(q_ref[...], k_ref[...], NT, preferred_element_type=F32)
        m = jnp.max(s, axis=-1, keepdims=True)
        p = jnp.exp2(s - m)
        o = jnp.dot(p.astype(BF), v_ref[...], preferred_element_type=F32)
        l = o[:, V_HEAD:V_HEAD + 1]
        o_ref[...] = o / l
        lse_ref[...] = jnp.broadcast_to(m + jnp.log2(l), lse_ref.shape)

    qs = pl.BlockSpec((tq, HP), lambda h, i: (i, h))
    kv = pl.BlockSpec((T, HP), lambda h, i: (0, h))
    return pl.pallas_call(
        body, name="attn_fwd", out_shape=(jax.ShapeDtypeStruct(q.shape, F32),) * 2, grid=(N_HEADS, T // tq),
        in_specs=[qs, kv, kv], out_specs=(qs, qs),
        compiler_params=pltpu.CompilerParams(dimension_semantics=("parallel", "parallel")),
    )(q, k, v)


def _attn_bwd(q, k, v, do, o, lse):
    T = q.shape[0]
    tb = _pick(T, 512)
    nb = T // tb

    def body(q_ref, k_ref, v_ref, do_ref, o_ref, lse_ref, dq_ref, dk_ref, dv_ref, delta_rows, lse_rows, dob_scr):
        dq_ref[...] = jnp.zeros_like(dq_ref)
        lane = lax.broadcasted_iota(jnp.int32, (8, HP), 1)
        ones8 = jnp.ones((8, HP), BF)
        first8 = jnp.where(lane == 0, 1.0, 0.0).astype(BF)

        def as_rows(pick, v):
            total, rest = None, v
            for _ in range(3):
                piece = rest.astype(BF)
                part = lax.dot_general(pick, piece, NT, preferred_element_type=F32)
                total = part if total is None else total + part
                rest = rest - piece.astype(F32)
            return total

        def per_q_tile(i, carry):
            qs = pl.ds(pl.multiple_of(i * tb, tb), tb)
            doi = do_ref[qs, :]
            delta_rows[i] = as_rows(ones8, doi * o_ref[qs, :])
            lse_rows[i] = as_rows(first8, lse_ref[qs, :])
            dob_scr[qs, :] = doi.astype(BF)
            return carry

        lax.fori_loop(0, nb, per_q_tile, 0)

        def k_loop(j, carry):
            ks = pl.ds(pl.multiple_of(j * tb, tb), tb)
            kj, vj = k_ref[ks, :], v_ref[ks, :]

            def q_loop(i, acc):
                dk_acc, dv_acc = acc
                qs = pl.ds(pl.multiple_of(i * tb, tb), tb)
                qi = q_ref[qs, :]
                dob = dob_scr[qs, :]
                s_t = lax.dot_general(kj, qi, NT, preferred_element_type=F32)
                p_t = jnp.exp2(s_t - lse_rows[i, 0:1, :])
                dp_t = lax.dot_general(vj, dob, NT, preferred_element_type=F32)
                ds_t = (p_t * (dp_t - delta_rows[i, 0:1, :])).astype(BF)
                dv_acc = dv_acc + jnp.dot(p_t.astype(BF), dob, preferred_element_type=F32)
                dk_acc = dk_acc + jnp.dot(ds_t, qi, preferred_element_type=F32)
                dq_ref[qs, :] += lax.dot_general(ds_t, kj, TN, preferred_element_type=F32)
                return dk_acc, dv_acc

            zero = jnp.zeros((tb, HP), F32)
            dk_acc, dv_acc = lax.fori_loop(0, nb, q_loop, (zero, zero))
            dk_ref[ks, :] = dk_acc
            dv_ref[ks, :] = dv_acc.astype(dv_ref.dtype)
            return carry

        lax.fori_loop(0, nb, k_loop, 0)

    spec = pl.BlockSpec((T, HP), lambda h: (0, h))
    return pl.pallas_call(
        body, name="attn_bwd",
        out_shape=(jax.ShapeDtypeStruct(q.shape, F32), jax.ShapeDtypeStruct(q.shape, F32), jax.ShapeDtypeStruct(q.shape, BF)),
        grid=(N_HEADS,), in_specs=[spec] * 6, out_specs=(spec,) * 3,
        scratch_shapes=[pltpu.VMEM((nb, 8, tb), F32), pltpu.VMEM((nb, 8, tb), F32), pltpu.VMEM((T, HP), BF)],
        compiler_params=pltpu.CompilerParams(dimension_semantics=("parallel",), vmem_limit_bytes=2 * 15 * T * HP * 2 + (8 << 20)),
    )(q, k, v, do, o, lse)


CONV_TC = 512
CONV_PAD = CONV_WIDTH // 2


def _halo_specs(tr, col_of):
    r8 = tr // 8
    cur = pl.BlockSpec((tr, CONV_TC), lambda j, i: (i, col_of(j)))
    prev = pl.BlockSpec((8, CONV_TC), lambda j, i: (jnp.maximum(i * r8 - 1, 0), col_of(j)))

    def nxt_map(j, i, n8):
        return (jnp.minimum((i + 1) * r8, n8 - 1), col_of(j))

    return cur, prev, nxt_map


def _with_halo(prev_ref, cur_ref, next_ref, i, n_i):
    prev = jnp.where(i == 0, 0.0, prev_ref[...].astype(F32))
    nxt = jnp.where(i == n_i - 1, 0.0, next_ref[...].astype(F32))
    return jnp.concatenate([prev, cur_ref[...].astype(F32), nxt], axis=0)


def _conv_fwd(u, w8, b):
    T = u.shape[0]
    tr = _pick(T, 512)
    n_i = T // tr
    c0 = U_XBC // CONV_TC
    cur, prev, nxt_map = _halo_specs(tr, lambda j: c0 + j)
    nxt = pl.BlockSpec((8, CONV_TC), functools.partial(nxt_map, n8=T // 8))

    def body(p_ref, c_ref, n_ref, w_ref, b_ref, pre_ref, act_ref):
        i = pl.program_id(1)
        full = _with_halo(p_ref, c_ref, n_ref, i, n_i)
        acc = jnp.broadcast_to(b_ref[...], (tr, CONV_TC))
        for kk in range(CONV_WIDTH):
            acc = acc + full[8 - CONV_PAD + kk:8 - CONV_PAD + kk + tr, :] * w_ref[kk:kk + 1, :]
        pre_ref[...] = acc
        act_ref[...] = _silu(acc)

    out = pl.BlockSpec((tr, CONV_TC), lambda j, i: (i, j))
    return pl.pallas_call(
        body, name="conv_fwd", out_shape=(jax.ShapeDtypeStruct((T, XBC_DIM), F32),) * 2,
        grid=(XBC_DIM // CONV_TC, n_i),
        in_specs=[prev, cur, nxt, pl.BlockSpec((8, CONV_TC), lambda j, i: (0, j)), pl.BlockSpec((1, CONV_TC), lambda j, i: (0, j))],
        out_specs=(out, out),
    )(u, u, u, w8, b)


def _conv_dpre(dacts, pre, col0, *, name):
    T, width = dacts[0].shape
    tt = _pick(T, 512)
    n_d = len(dacts)
    c0 = col0 // CONV_TC

    def body(*refs):
        d = refs[0][...]
        for r in refs[1:n_d]:
            d = d + r[...]
        refs[n_d + 1][...] = d * _dsilu(refs[n_d][...])

    blk = pl.BlockSpec((tt, CONV_TC), lambda j, i: (i, j))
    return pl.pallas_call(
        body, name=name, out_shape=jax.ShapeDtypeStruct((T, width), F32), grid=(width // CONV_TC, T // tt),
        in_specs=[blk] * n_d + [pl.BlockSpec((tt, CONV_TC), lambda j, i: (i, c0 + j))], out_specs=blk,
    )(*dacts, pre)


def _conv_bwd(dpre, u, w8, col0, *, name):
    T, width = dpre.shape
    tr = _pick(T, 512)
    n_i = T // tr
    cd = col0 // CONV_TC
    cx = (U_XBC + col0) // CONV_TC
    d_cur, d_prev, d_nxt_map = _halo_specs(tr, lambda j: j)
    x_cur, x_prev, x_nxt_map = _halo_specs(tr, lambda j: cx + j)
    d_nxt = pl.BlockSpec((8, CONV_TC), functools.partial(d_nxt_map, n8=T // 8))
    x_nxt = pl.BlockSpec((8, CONV_TC), functools.partial(x_nxt_map, n8=T // 8))

    def body(dp_ref, dc_ref, dn_ref, xp_ref, xc_ref, xn_ref, w_ref, dx_ref, dw_ref):
        i = pl.program_id(1)
        dfull = _with_halo(dp_ref, dc_ref, dn_ref, i, n_i)
        xfull = _with_halo(xp_ref, xc_ref, xn_ref, i, n_i)
        dcur = dc_ref[...]
        dx = jnp.zeros((tr, CONV_TC), F32)
        rows = []
        for kk in range(CONV_WIDTH):
            dx = dx + dfull[8 + CONV_PAD - kk:8 + CONV_PAD - kk + tr, :] * w_ref[kk:kk + 1, :]
            rows.append(jnp.sum(dcur * xfull[8 - CONV_PAD + kk:8 - CONV_PAD + kk + tr, :], axis=0, keepdims=True))
        rows.append(jnp.sum(dcur, axis=0, keepdims=True))
        rows.append(jnp.zeros((2, CONV_TC), F32))
        dx_ref[...] = dx.astype(dx_ref.dtype)

        @pl.when(i == 0)
        def _():
            dw_ref[...] = jnp.zeros_like(dw_ref)

        dw_ref[...] += jnp.concatenate(rows, axis=0)

    out = pl.BlockSpec((tr, CONV_TC), lambda j, i: (i, j))
    return pl.pallas_call(
        body, name=name, out_shape=(jax.ShapeDtypeStruct((T, width), BF), jax.ShapeDtypeStruct((8, width), F32)),
        grid=(width // CONV_TC, n_i),
        in_specs=[d_prev, d_cur, d_nxt, x_prev, x_cur, x_nxt, pl.BlockSpec((8, CONV_TC), lambda j, i: (0, cd + j))],
        out_specs=(out, pl.BlockSpec((8, CONV_TC), lambda j, i: (0, j))),
        compiler_params=pltpu.CompilerParams(dimension_semantics=("parallel", "arbitrary")),
    )(dpre, dpre, dpre, u, u, u, w8)


N_HB = 2 * SSM_GROUPS
P_DT, P_CS, P_E, P_W = 0, HP, 2 * HP, 3 * HP
DT_BLK = (U_SMALL + S_DT) // HP


def _tri(rev, transpose=False):
    rows = lax.broadcasted_iota(jnp.int32, (CHUNK, CHUNK), 0)
    cols = lax.broadcasted_iota(jnp.int32, (CHUNK, CHUNK), 1)
    if transpose:
        rows, cols = cols, rows
    return (cols >= rows) if rev else (cols <= rows)


def _ssd_prep(u, bias8, alog8):
    T = u.shape[0]
    nc = T // CHUNK

    def body(dt_ref, bias_ref, a_ref, cols_ref, rows_ref):
        lane = lax.broadcasted_iota(jnp.int32, (CHUNK, HP), 1)
        dt = _softplus(dt_ref[...] + bias_ref[0:1, :])
        da = dt * (-jnp.exp(a_ref[0:1, :]))
        cs_f = jnp.dot(jnp.where(_tri(False), 1.0, 0.0).astype(F32), da, precision=HI, preferred_element_type=F32)
        cs_b = jnp.dot(jnp.where(_tri(True), 1.0, 0.0).astype(F32), da, precision=HI, preferred_element_type=F32)
        cs = jnp.where(lane < SSM_HEADS, cs_f, cs_b)
        tot = jnp.where(lane[0:1] < SSM_HEADS, cs_f[CHUNK - 1:CHUNK, :], cs_b[0:1, :])
        e, w = jnp.exp(cs), jnp.exp(tot - cs)
        tot8 = jnp.broadcast_to(tot, (8, HP))
        etot8 = jnp.exp(tot8)
        for b in range(N_HB):
            down = (HP - HG * b) % HP

            def rolled(v):
                return pltpu.roll(v, down, 1) if down else v

            cols_ref[b, :, P_DT:P_DT + HP] = rolled(dt)
            cs_r = rolled(cs)
            cols_ref[b, :, P_CS:P_CS + HP] = cs_r
            cols_ref[b, :, P_E:P_E + HP] = rolled(e)
            cols_ref[b, :, P_W:P_W + HP] = rolled(w)
            rows_ref[b, 0, 0:8, :] = cs_r.T[0:8, :]
            r8 = lax.broadcasted_iota(jnp.int32, (8, HP), 0)
            rows_ref[b, 0, 8:16, :] = jnp.where(r8 == 0, rolled(tot8), jnp.where(r8 == 1, rolled(etot8), 0.0))

    vec = pl.BlockSpec((8, HP), lambda c: (0, 0))
    return pl.pallas_call(
        body, name="ssd_prep",
        out_shape=(jax.ShapeDtypeStruct((N_HB, T, 4 * HP), F32), jax.ShapeDtypeStruct((N_HB, nc, 16, HP), F32)),
        grid=(nc,), in_specs=[pl.BlockSpec((CHUNK, HP), lambda c: (c, DT_BLK)), vec, vec],
        out_specs=(pl.BlockSpec((N_HB, CHUNK, 4 * HP), lambda c: (0, c, 0)), pl.BlockSpec((N_HB, 1, 16, HP), lambda c: (0, c, 0, 0))),
    )(u, bias8, alog8)


def _ssd_specs(T, rev, bwd):
    nc = T // CHUNK
    fwd_order = (lambda c: nc - 1 - c) if rev else (lambda c: c)
    cm = (lambda c: fwd_order(nc - 1 - c)) if bwd else fwd_order
    hb0 = SSM_GROUPS if rev else 0
    xs = pl.BlockSpec((CHUNK, GW), lambda c, g: (cm(c), g))
    bs = pl.BlockSpec((CHUNK, D_STATE), lambda c, g: (cm(c), D_INNER // D_STATE + g))
    cs = pl.BlockSpec((CHUNK, D_STATE), lambda c, g: (cm(c), (D_INNER + SSM_GROUPS * D_STATE) // D_STATE + g))
    cols = pl.BlockSpec((1, CHUNK, 4 * HP), lambda c, g: (hb0 + g, cm(c), 0))
    rows = pl.BlockSpec((1, 1, 16, HP), lambda c, g: (hb0 + g, cm(c), 0, 0))
    return nc, cm, xs, bs, cs, cols, rows


def _head_lanes(to_heads):
    shape = (GW, HP) if to_heads else (HP, GW)
    wide = lax.broadcasted_iota(jnp.int32, shape, 0 if to_heads else 1)
    head = lax.broadcasted_iota(jnp.int32, shape, 1 if to_heads else 0)
    return jnp.where((wide >= PH * head) & (wide < PH * (head + 1)), 1.0, 0.0).astype(BF)


def _split_dot(v, m, terms):
    total, rest = None, v
    for _ in range(terms):
        piece = rest.astype(BF)
        part = jnp.dot(piece, m, preferred_element_type=F32)
        total = part if total is None else total + part
        rest = rest - piece.astype(F32)
    return total


def _spread_cols(cols_ref, rows_ref):
    spread = _head_lanes(False)
    dt_e = _split_dot(cols_ref[0, :, P_DT:P_DT + HP], spread, 3)
    e_e = _split_dot(cols_ref[0, :, P_E:P_E + HP], spread, 2)
    w_e = _split_dot(cols_ref[0, :, P_W:P_W + HP], spread, 2)
    etot_e = _split_dot(rows_ref[0, 0, 8:16, :], spread, 3)[1:2, :]
    return dt_e, e_e, w_e, etot_e


def _decay(cols_ref, rows_ref, hh, incl, transpose=False):
    col = cols_ref[0, :, P_CS + hh:P_CS + hh + 1]
    row = rows_ref[0, 0, hh:hh + 1, :]
    return jnp.where(incl, jnp.exp(row - col if transpose else col - row), 0.0)


def _ssd_fwd(act, cols, rows, *, rev, name):
    T = act.shape[0]
    nc, cm, xs_s, b_s, c_s, cols_s, rows_s = _ssd_specs(T, rev, False)

    def body(x_ref, b_ref, c_ref, cols_ref, rows_ref, y_ref, st_ref, state):
        c, g = pl.program_id(0), pl.program_id(1)

        @pl.when(c == 0)
        def _():
            state[g] = jnp.zeros((D_STATE, GW), F32)

        incl = _tri(rev)
        bm, cmat = b_ref[...].astype(BF), c_ref[...].astype(BF)
        bm_t = b_ref[...].T.astype(BF)
        cb = lax.dot_general(cmat, bm, NT, preferred_element_type=F32)
        dt_e, e_e, w_e, etot_e = _spread_cols(cols_ref, rows_ref)
        prev_all = state[g]
        st_ref[...] = prev_all
        xdt = x_ref[...] * dt_e
        xdt_b = xdt.astype(BF)
        yo_all = jnp.dot(cmat, prev_all.astype(BF), preferred_element_type=F32) * e_e
        state[g] = prev_all * etot_e + jnp.dot(bm_t, (xdt * w_e).astype(BF), preferred_element_type=F32)
        for hh in range(HG):
            hs = slice(PH * hh, PH * (hh + 1))
            lmat = _decay(cols_ref, rows_ref, hh, incl)
            yd = jnp.dot((cb * lmat).astype(BF), xdt_b[:, hs], preferred_element_type=F32)
            y_ref[:, hs] = yd + yo_all[:, hs]

    return pl.pallas_call(
        body, name=name,
        out_shape=(jax.ShapeDtypeStruct((T, D_INNER), F32), jax.ShapeDtypeStruct((nc * D_STATE, D_INNER), F32)),
        grid=(nc, SSM_GROUPS), in_specs=[xs_s, b_s, c_s, cols_s, rows_s], out_specs=(xs_s, xs_s),
        scratch_shapes=[pltpu.VMEM((SSM_GROUPS, D_STATE, GW), F32)],
        compiler_params=pltpu.CompilerParams(dimension_semantics=("arbitrary", "arbitrary")),
    )(act, act, act, cols, rows)


def _ssd_bwd(act, cols, rows, states, dy, *, rev, name):
    T = act.shape[0]
    nc, cm, xs_s, b_s, c_s, cols_s, rows_s = _ssd_specs(T, rev, True)

    def body(x_ref, b_ref, c_ref, cols_ref, rows_ref, st_ref, dy_ref, dx_ref, db_ref, dc_ref, dsel_ref, dtot_ref,
             dstate, dcs_cols, dcs_rows, dcb, dm_scr, dxdt_scr):
        c, g = pl.program_id(0), pl.program_id(1)

        @pl.when(c == 0)
        def _():
            dstate[g] = jnp.zeros((D_STATE, GW), F32)

        incl, incl_t = _tri(rev), _tri(rev, transpose=True)
        bm, cmat = b_ref[...].astype(BF), c_ref[...].astype(BF)
        cm_t = c_ref[...].T.astype(BF)
        cb = lax.dot_general(cmat, bm, NT, preferred_element_type=F32)
        cb_t = lax.dot_general(bm, cmat, NT, preferred_element_type=F32)
        prev_all, ds_all = st_ref[...], dstate[g]
        pb_all, dsb_all = prev_all.astype(BF), ds_all.astype(BF)
        cp_all = jnp.dot(cmat, pb_all, preferred_element_type=F32)
        bds_all = jnp.dot(bm, dsb_all, preferred_element_type=F32)
        dt_e, e_e, w_e, etot_e = _spread_cols(cols_ref, rows_ref)
        to_heads = _head_lanes(True)
        x, dy = x_ref[...], dy_ref[...]
        xdt = x * dt_e
        xdt_b, dy_b = xdt.astype(BF), dy.astype(BF)
        dye_b, xdw_b = (dy * e_e).astype(BF), (xdt * w_e).astype(BF)
        for hh in range(HG):
            hs = slice(PH * hh, PH * (hh + 1))
            mmat_t = cb_t * _decay(cols_ref, rows_ref, hh, incl_t, transpose=True)
            dm_scr[hh] = lax.dot_general(dy_b[:, hs], xdt_b[:, hs], NT, preferred_element_type=F32)
            dxdt_scr[:, hs] = jnp.dot(mmat_t.astype(BF), dy_b[:, hs], preferred_element_type=F32)
        bdsw = bds_all * w_e
        dxdt = dxdt_scr[...] + bdsw
        dx_ref[...] = dxdt * dt_e
        t = _split_dot(xdt * bdsw, to_heads, 2)
        dcs_state = _split_dot(dy * cp_all, to_heads, 2) * cols_ref[0, :, P_E:P_E + HP] - t
        dsel_ref[0, :, 0:HP] = _split_dot(dxdt * x, to_heads, 2)
        sp = _split_dot(jnp.broadcast_to(jnp.sum(ds_all * prev_all, axis=0, keepdims=True), (8, GW)), to_heads, 2)
        dtot_ref[0, 0] = jnp.sum(t, axis=0, keepdims=True) + sp * rows_ref[0, 0, 9:10, :]
        dstate[g] = ds_all * etot_e + jnp.dot(cm_t, dye_b, preferred_element_type=F32)
        dcs_cols[...] = jnp.zeros_like(dcs_cols)
        dcs_rows[...] = jnp.zeros_like(dcs_rows)
        dcb[...] = jnp.zeros_like(dcb)
        for hh in range(HG):
            lmat = _decay(cols_ref, rows_ref, hh, incl)
            dm = dm_scr[hh]
            qm = dm * (cb * lmat)
            dcs_cols[:, hh:hh + 1] = jnp.sum(qm, axis=1, keepdims=True)
            dcs_rows[hh:hh + 1, :] = jnp.sum(qm, axis=0, keepdims=True)
            dcb[...] += dm * lmat
        dcb_all = dcb[...]
        dsel_ref[0, :, HP:2 * HP] = dcs_state + dcs_cols[...] - dcs_rows[...].T
        dc_ref[...] = (lax.dot_general(dye_b, pb_all, NT, preferred_element_type=F32)
                       + jnp.dot(dcb_all.astype(BF), bm, preferred_element_type=F32))
        db_ref[...] = (lax.dot_general(xdw_b, dsb_all, NT, preferred_element_type=F32)
                       + jnp.dot(dcb_all.T.astype(BF), cmat, preferred_element_type=F32))

    bc_out = pl.BlockSpec((CHUNK, D_STATE), lambda c, g: (cm(c), g))
    return pl.pallas_call(
        body, name=name,
        out_shape=(jax.ShapeDtypeStruct((T, D_INNER), F32), jax.ShapeDtypeStruct((T, SSM_GROUPS * D_STATE), F32),
                   jax.ShapeDtypeStruct((T, SSM_GROUPS * D_STATE), F32), jax.ShapeDtypeStruct((SSM_GROUPS, T, 2 * HP), F32),
                   jax.ShapeDtypeStruct((SSM_GROUPS, nc, 8, HP), F32)),
        grid=(nc, SSM_GROUPS), in_specs=[xs_s, b_s, c_s, cols_s, rows_s, xs_s, xs_s],
        out_specs=(xs_s, bc_out, bc_out, pl.BlockSpec((1, CHUNK, 2 * HP), lambda c, g: (g, cm(c), 0)),
                   pl.BlockSpec((1, 1, 8, HP), lambda c, g: (g, cm(c), 0, 0))),
        scratch_shapes=[pltpu.VMEM((SSM_GROUPS, D_STATE, GW), F32), pltpu.VMEM((CHUNK, CHUNK), F32),
                        pltpu.VMEM((CHUNK, CHUNK), F32), pltpu.VMEM((CHUNK, CHUNK), F32),
                        pltpu.VMEM((HG, CHUNK, CHUNK), F32), pltpu.VMEM((CHUNK, GW), F32)],
        compiler_params=pltpu.CompilerParams(dimension_semantics=("arbitrary", "arbitrary")),
    )(act, act, act, cols, rows, states, dy)


def _ssd_prep_bwd(u, bias8, alog8, dsel_f, dtot_f, dsel_b, dtot_b):
    T = u.shape[0]
    nc = T // CHUNK

    def body(dt_ref, bias_ref, a_ref, sf_ref, tf_ref, sb_ref, tb_ref, ddt_ref, da_ref, dbias_ref):
        @pl.when(pl.program_id(0) == 0)
        def _():
            da_ref[...] = jnp.zeros_like(da_ref)
            dbias_ref[...] = jnp.zeros_like(dbias_ref)

        lane = lax.broadcasted_iota(jnp.int32, (CHUNK, HP), 1)
        pre = dt_ref[...] + bias_ref[0:1, :]
        dt = _softplus(pre)
        a = -jnp.exp(a_ref[0:1, :])
        ddt_x, dcs, dtot = jnp.zeros((CHUNK, HP), F32), jnp.zeros((CHUNK, HP), F32), jnp.zeros((8, HP), F32)
        for b in range(N_HB):
            s_ref, t_ref, g = (sf_ref, tf_ref, b) if b < SSM_GROUPS else (sb_ref, tb_ref, b - SSM_GROUPS)
            mine = (lane >= HG * b) & (lane < HG * (b + 1))

            def up(v):
                return pltpu.roll(v, HG * b, 1) if b else v

            ddt_x = ddt_x + jnp.where(mine, up(s_ref[g, :, 0:HP]), 0.0)
            dcs = dcs + jnp.where(mine, up(s_ref[g, :, HP:2 * HP]), 0.0)
            dtot = dtot + jnp.where(mine[0:8], up(t_ref[g, 0]), 0.0)
        tri_f = jnp.where(_tri(False, transpose=True), 1.0, 0.0).astype(F32)
        tri_b = jnp.where(_tri(True, transpose=True), 1.0, 0.0).astype(F32)
        dda = jnp.where(lane < SSM_HEADS, jnp.dot(tri_f, dcs, precision=HI, preferred_element_type=F32),
                        jnp.dot(tri_b, dcs, precision=HI, preferred_element_type=F32)) + dtot[0:1, :]
        dpre = (ddt_x + dda * a) * jax.nn.sigmoid(pre)
        ddt_ref[...] = jnp.where(lane < 2 * SSM_HEADS, dpre, 0.0)
        dbias_ref[...] += jnp.broadcast_to(jnp.sum(dpre, axis=0, keepdims=True), (8, HP))
        da_ref[...] += jnp.broadcast_to(jnp.sum(dda * dt, axis=0, keepdims=True) * a, (8, HP))

    vec = pl.BlockSpec((8, HP), lambda c: (0, 0))
    sel = pl.BlockSpec((SSM_GROUPS, CHUNK, 2 * HP), lambda c: (0, c, 0))
    tot = pl.BlockSpec((SSM_GROUPS, 1, 8, HP), lambda c: (0, c, 0, 0))
    tile = pl.BlockSpec((CHUNK, HP), lambda c: (c, 0))
    return pl.pallas_call(
        body, name="ssd_prep_bwd",
        out_shape=(jax.ShapeDtypeStruct((T, HP), F32), jax.ShapeDtypeStruct((8, HP), F32), jax.ShapeDtypeStruct((8, HP), F32)),
        grid=(nc,), in_specs=[pl.BlockSpec((CHUNK, HP), lambda c: (c, DT_BLK)), vec, vec, sel, tot, sel, tot],
        out_specs=(tile, vec, vec),
        compiler_params=pltpu.CompilerParams(dimension_semantics=("arbitrary",)),
    )(u, bias8, alog8, dsel_f, dtot_f, dsel_b, dtot_b)


def _ssm_combine_fwd(y_f, y_b, act, u, dskip, gain):
    T = y_f.shape[0]
    tt = _pick(T, 512)

    def body(yf_ref, yb_ref, x_ref, z_ref, ds_ref, g_ref, y_ref, m_ref):
        y = yf_ref[...] + yb_ref[...] + ds_ref[...] * x_ref[...]
        y2 = y * _silu(z_ref[...])
        r = lax.rsqrt(jnp.mean(y2 * y2, axis=-1, keepdims=True) + EPS)
        y_ref[...] = y
        m_ref[...] = (y2 * r * g_ref[...]).astype(m_ref.dtype)

    blk = pl.BlockSpec((tt, GW), lambda i, g: (i, g))
    vec = pl.BlockSpec((1, GW), lambda i, g: (0, g))
    return pl.pallas_call(
        body, name="ssm_combine_fwd",
        out_shape=(jax.ShapeDtypeStruct((T, D_INNER), F32), jax.ShapeDtypeStruct((T, D_INNER), BF)),
        grid=(T // tt, SSM_GROUPS), in_specs=[blk, blk, blk, blk, vec, vec], out_specs=(blk, blk),
    )(y_f, y_b, act, u, dskip, gain)


def _ssm_combine_bwd(dm, y, act, u, dskip, gain):
    T = y.shape[0]
    tt = _pick(T, 512)

    def body(dm_ref, y_ref, x_ref, z_ref, ds_ref, g_ref, dy_ref, dz_ref, dxs_ref, dg_ref, dsk_ref):
        z = z_ref[...]
        y = y_ref[...]
        x = x_ref[...]
        sz = _silu(z)
        y2 = y * sz
        r = lax.rsqrt(jnp.mean(y2 * y2, axis=-1, keepdims=True) + EPS)
        d = dm_ref[...]
        gd = d * g_ref[...]
        dy2 = r * gd - y2 * (r * r * r) * jnp.mean(gd * y2, axis=-1, keepdims=True)
        dy = dy2 * sz
        dy_ref[...] = dy
        dz_ref[...] = (dy2 * y * _dsilu(z)).astype(dz_ref.dtype)
        dxs_ref[...] = dy * ds_ref[...]

        @pl.when(pl.program_id(1) == 0)
        def _():
            dg_ref[...] = jnp.zeros_like(dg_ref)
            dsk_ref[...] = jnp.zeros_like(dsk_ref)

        dg_ref[...] += jnp.broadcast_to(jnp.sum(d * y2 * r, axis=0, keepdims=True), dg_ref.shape)
        lane_sum = jnp.broadcast_to(jnp.sum(dy * x, axis=0, keepdims=True), (8, GW))
        src = lax.broadcasted_iota(jnp.int32, (GW, HP), 0)
        head = lax.broadcasted_iota(jnp.int32, (GW, HP), 1)
        to_head = jnp.where((src >= PH * head) & (src < PH * (head + 1)), 1.0, 0.0).astype(F32)
        dsk_ref[...] += jnp.dot(lane_sum, to_head, precision=HI, preferred_element_type=F32)

    blk = pl.BlockSpec((tt, GW), lambda g, i: (i, g))
    vec = pl.BlockSpec((1, GW), lambda g, i: (0, g))
    acc = pl.BlockSpec((8, GW), lambda g, i: (0, g))
    return pl.pallas_call(
        body, name="ssm_combine_bwd",
        out_shape=(jax.ShapeDtypeStruct((T, D_INNER), F32), jax.ShapeDtypeStruct((T, D_INNER), BF),
                   jax.ShapeDtypeStruct((T, D_INNER), F32), jax.ShapeDtypeStruct((8, D_INNER), F32),
                   jax.ShapeDtypeStruct((8, SSM_GROUPS * HP), F32)),
        grid=(SSM_GROUPS, T // tt), in_specs=[blk, blk, blk, blk, vec, vec],
        out_specs=(blk, blk, blk, acc, pl.BlockSpec((8, HP), lambda g, i: (0, g))),
        compiler_params=pltpu.CompilerParams(dimension_semantics=("parallel", "arbitrary")),
    )(dm, y, act, u, dskip, gain)


def _loss_head(y, target):
    T, D = y.shape
    tt = _pick(T, 512)

    def body(y_ref, t_ref, dy_ref, dyb_ref, l_ref):
        e = y_ref[...] - t_ref[...]
        dy_ref[...] = e * (1.0 / D)
        dyb_ref[...] = (e * (1.0 / D)).astype(dyb_ref.dtype)

        @pl.when(pl.program_id(0) == 0)
        def _():
            l_ref[...] = jnp.zeros_like(l_ref)

        l_ref[...] += jnp.sum(e * e) * (0.5 / D)

    blk = pl.BlockSpec((tt, D), lambda i: (i, 0))
    return pl.pallas_call(
        body, name="loss_head",
        out_shape=(jax.ShapeDtypeStruct((T, D), F32), jax.ShapeDtypeStruct((T, D), BF), jax.ShapeDtypeStruct((8, 128), F32)),
        grid=(T // tt,), in_specs=[blk, blk], out_specs=(blk, blk, pl.BlockSpec((8, 128), lambda i: (0, 0))),
        compiler_params=pltpu.CompilerParams(dimension_semantics=("arbitrary",)),
    )(y, target)


def _adamw(w, g, m, v, *, name):
    R, C = w.shape
    cap = max(8, (1 << 18) // C)
    tr = R
    if R % 8 == 0:
        tr = 8
        for cand in range(8, min(R, cap) + 1, 8):
            if R % cand == 0:
                tr = cand

    def body(w_ref, g_ref, m_ref, v_ref, d_ref, nm_ref, nv_ref):
        gg = g_ref[...]
        nm = ADAM_B1 * m_ref[...] + (1.0 - ADAM_B1) * gg
        nv = ADAM_B2 * v_ref[...] + (1.0 - ADAM_B2) * jnp.square(gg)
        m_hat = nm / (1.0 - ADAM_B1 ** ADAM_STEP)
        v_hat = nv / (1.0 - ADAM_B2 ** ADAM_STEP)
        d_ref[...] = -ADAM_LR * (m_hat / (jnp.sqrt(v_hat) + ADAM_EPS) + ADAM_WD * w_ref[...])
        nm_ref[...] = nm
        nv_ref[...] = nv

    blk = pl.BlockSpec((tr, C), lambda i: (i, 0))
    return pl.pallas_call(
        body, name=name, out_shape=(jax.ShapeDtypeStruct((R, C), F32),) * 3, grid=(R // tr,),
        in_specs=[blk] * 4, out_specs=(blk,) * 3,
    )(w, g, m, v)


ANY = pl.BlockSpec(memory_space=pl.ANY)


def _chip_peers():
    x, y, c = lax.axis_index("x"), lax.axis_index("y"), lax.axis_index("c")
    return x, y, c, [(1 - x, y), (x, 1 - y), (1 - x, 1 - y)]


def _half_rows(c, rh):
    return pl.ds(pl.multiple_of(c * rh, 16), rh)


def _my_chip():
    return 2 * lax.axis_index("x") + lax.axis_index("y")


def _gather_chips(wb, wf):
    rh = wb.shape[0] // 2
    rq = rh // 2

    def body(wb_ref, wf_ref, ob_ref, of_ref, send_sems, recv_sems):
        x, y, c, peers = _chip_peers()
        nbr_x, nbr_y = peers[0], peers[1]
        me, chip_x, chip_y, chip_d = 2 * x + y, 2 * (1 - x) + y, 2 * x + (1 - y), 2 * (1 - x) + (1 - y)

        def quarter(core, b):
            return pl.ds(pl.multiple_of(core * rh + b * rq, 16), rq)

        ici = [(0, nbr_x, me, 0, chip_x), (1, nbr_y, me, 1, chip_y), (2, nbr_y, me, 0, chip_y), (3, nbr_x, me, 1, chip_x),
               (4, nbr_y, chip_x, 0, chip_d), (5, nbr_x, chip_y, 1, chip_d)]

        def ici_copy(k, to, slot, b, own):
            rows = quarter(c, b)
            return pltpu.make_async_remote_copy(
                src_ref=wb_ref.at[rows] if own else ob_ref.at[slot, rows], dst_ref=ob_ref.at[slot, rows],
                send_sem=send_sems.at[k], recv_sem=recv_sems.at[k], device_id=(to[0], to[1], c), device_id_type=MESH)

        def to_sibling(k, slot, b, core):
            rows = quarter(core, b)
            return pltpu.make_async_remote_copy(
                src_ref=ob_ref.at[slot, rows], dst_ref=ob_ref.at[slot, rows], send_sem=send_sems.at[6 + k],
                recv_sem=recv_sems.at[6 + k], device_id=(x, y, 1 - c), device_id_type=MESH)

        def small_copy(k, slot):
            px, py = peers[k]
            return pltpu.make_async_remote_copy(
                src_ref=wf_ref, dst_ref=of_ref.at[slot], send_sem=send_sems.at[12 + k], recv_sem=recv_sems.at[12 + k],
                device_id=(px, py, c), device_id_type=MESH)

        sends = [ici_copy(k, to, slot, b, True) for k, to, slot, b, _ in ici[:4]] + [small_copy(k, me) for k in range(3)]
        for cp in sends:
            cp.start()
        for k, to, slot, b, arrives in ici:
            ici_copy(k, to, arrives, b, False).wait_recv()
            passed = [to_sibling(k, arrives, b, c)]
            if k < 2:
                passed.append(ici_copy(*ici[4 + k][:4], False))
            for cp in passed:
                cp.start()
            sends += passed
        for k, to, slot, b, arrives in ici:
            to_sibling(k, arrives, b, 1 - c).wait_recv()
        chip_of = [chip_x, chip_y, chip_d]
        for k in range(3):
            small_copy(k, chip_of[k]).wait_recv()
        for cp in sends:
            cp.wait_send()

    ob, of = pl.pallas_call(
        body, name="gather_weights",
        out_shape=(jax.ShapeDtypeStruct((4,) + wb.shape, wb.dtype), jax.ShapeDtypeStruct((4,) + wf.shape, wf.dtype)),
        in_specs=[ANY, ANY], out_specs=(ANY, ANY),
        scratch_shapes=[pltpu.SemaphoreType.DMA((15,)), pltpu.SemaphoreType.DMA((15,))],
    )(wb, wf)
    me = _my_chip()
    return lax.dynamic_update_slice(ob, wb[None], (me, 0, 0)), lax.dynamic_update_slice(of, wf[None], (me, 0, 0))


def _halves_to_sibling(gp):
    rh = gp.shape[1] // 2

    def body(gp_ref, o_ref, send_sem, recv_sem):
        x, y, c = lax.axis_index("x"), lax.axis_index("y"), lax.axis_index("c")
        cp = pltpu.make_async_remote_copy(src_ref=gp_ref.at[:, _half_rows(1 - c, rh), :], dst_ref=o_ref, send_sem=send_sem,
                                          recv_sem=recv_sem, device_id=(x, y, 1 - c), device_id_type=MESH)
        cp.start()
        cp.wait()

    return pl.pallas_call(
        body, name="halves_to_sibling", out_shape=jax.ShapeDtypeStruct((gp.shape[0], rh, gp.shape[2]), gp.dtype),
        in_specs=[ANY], out_specs=ANY, scratch_shapes=[pltpu.SemaphoreType.DMA, pltpu.SemaphoreType.DMA],
    )(gp)


def _row_tile(rows, cap=1024):
    tr = 16
    for cand in range(16, cap + 1, 16):
        if rows % cand == 0:
            tr = cand
    return tr


def _add_halves(gp, sib, core):
    n, rh, C = sib.shape
    tr = _row_tile(rh)
    nt = rh // tr

    def body(c_ref, g_ref, s_ref, o_ref):
        o_ref[...] = (g_ref[...].astype(F32) + s_ref[...].astype(F32)).astype(o_ref.dtype)

    blk = pl.BlockSpec((1, tr, C), lambda j, i, c: (j, i, 0))
    return pl.pallas_call(
        body, name="add_halves", out_shape=jax.ShapeDtypeStruct(sib.shape, sib.dtype),
        grid_spec=pltpu.PrefetchScalarGridSpec(
            num_scalar_prefetch=1, grid=(n, nt),
            in_specs=[pl.BlockSpec((1, tr, C), lambda j, i, c: (j, c[0] * nt + i, 0)), blk], out_specs=blk),
    )(core, gp, sib)


def _join_halves(buf):
    rh = buf.shape[0] // 2

    def body(in_ref, o_ref, send_sem, recv_sem):
        x, y, c = lax.axis_index("x"), lax.axis_index("y"), lax.axis_index("c")

        def copy(rows):
            return pltpu.make_async_remote_copy(src_ref=o_ref.at[rows], dst_ref=o_ref.at[rows], send_sem=send_sem,
                                                recv_sem=recv_sem, device_id=(x, y, 1 - c), device_id_type=MESH)

        send = copy(_half_rows(c, rh))
        send.start()
        copy(_half_rows(1 - c, rh)).wait_recv()
        send.wait_send()

    return pl.pallas_call(
        body, name="join_halves", out_shape=jax.ShapeDtypeStruct(buf.shape, buf.dtype),
        in_specs=[ANY], out_specs=ANY, input_output_aliases={0: 0},
        scratch_shapes=[pltpu.SemaphoreType.DMA, pltpu.SemaphoreType.DMA],
    )(buf)


def _exchange_near(gp):
    rq = gp.shape[1] // 2

    def body(gp_ref, out_ref, send_sems, recv_sems):
        x, y, c, peers = _chip_peers()
        chip_x, chip_y, chip_d = 2 * (1 - x) + y, 2 * x + (1 - y), 2 * (1 - x) + (1 - y)
        plan = [(peers[0], chip_x, 0), (peers[0], chip_d, 0), (peers[1], chip_y, 1), (peers[1], chip_d, 1)]
        copies = [pltpu.make_async_remote_copy(
            src_ref=gp_ref.at[slot, pl.ds(b * rq, rq)], dst_ref=out_ref.at[k], send_sem=send_sems.at[k],
            recv_sem=recv_sems.at[k], device_id=(to[0], to[1], c), device_id_type=MESH) for k, (to, slot, b) in enumerate(plan)]
        for cp in copies:
            cp.start()
        for cp in copies:
            cp.wait_recv()
        for cp in copies:
            cp.wait_send()

    return pl.pallas_call(
        body, name="exchange_grads_near", out_shape=jax.ShapeDtypeStruct((4, rq, gp.shape[2]), gp.dtype),
        in_specs=[ANY], out_specs=ANY, scratch_shapes=[pltpu.SemaphoreType.DMA((4,)), pltpu.SemaphoreType.DMA((4,))],
    )(gp)


def _add_near(gp, near, chips):
    _, rq, C = near.shape
    tr = _row_tile(rq)
    nt = rq // tr

    def body(ch_ref, mine_a, mine_b, on_a, on_b, near_ref, part_ref, on_ref):
        part_ref[0] = mine_a[0].astype(F32) + near_ref[0].astype(F32)
        part_ref[1] = mine_b[0].astype(F32) + near_ref[2].astype(F32)
        on_ref[0] = (on_a[0].astype(F32) + near_ref[1].astype(F32)).astype(on_ref.dtype)
        on_ref[1] = (on_b[0].astype(F32) + near_ref[3].astype(F32)).astype(on_ref.dtype)

    def slot(which, b):
        return pl.BlockSpec((1, tr, C), lambda i, ch: (ch[which], b * nt + i, 0))

    return pl.pallas_call(
        body, name="add_near",
        out_shape=(jax.ShapeDtypeStruct((2, rq, C), F32), jax.ShapeDtypeStruct((2, rq, C), near.dtype)),
        grid_spec=pltpu.PrefetchScalarGridSpec(
            num_scalar_prefetch=1, grid=(nt,),
            in_specs=[slot(0, 0), slot(0, 1), slot(2, 0), slot(1, 1), pl.BlockSpec((4, tr, C), lambda i, ch: (0, i, 0))],
            out_specs=(pl.BlockSpec((2, tr, C), lambda i, ch: (0, i, 0)),) * 2),
    )(chips, gp, gp, gp, gp, near)


def _exchange_far(on):
    def body(on_ref, out_ref, send_sems, recv_sems):
        x, y, c, peers = _chip_peers()
        copies = [pltpu.make_async_remote_copy(
            src_ref=on_ref.at[k], dst_ref=out_ref.at[k], send_sem=send_sems.at[k], recv_sem=recv_sems.at[k],
            device_id=(to[0], to[1], c), device_id_type=MESH) for k, to in enumerate((peers[1], peers[0]))]
        for cp in copies:
            cp.start()
        for cp in copies:
            cp.wait_recv()
        for cp in copies:
            cp.wait_send()

    return pl.pallas_call(
        body, name="exchange_grads_far", out_shape=jax.ShapeDtypeStruct(on.shape, on.dtype),
        in_specs=[ANY], out_specs=ANY, scratch_shapes=[pltpu.SemaphoreType.DMA((2,)), pltpu.SemaphoreType.DMA((2,))],
    )(on)


def _add_far(part, far, core):
    _, rq, C = part.shape
    tr = _row_tile(rq)
    nt = rq // tr

    def body(c_ref, p_ref, f_ref, o_ref):
        o_ref[...] = p_ref[0] + f_ref[0].astype(F32)

    blk = pl.BlockSpec((1, tr, C), lambda b, i, c: (b, i, 0))
    return pl.pallas_call(
        body, name="add_far", out_shape=jax.ShapeDtypeStruct((4 * rq, C), F32),
        grid_spec=pltpu.PrefetchScalarGridSpec(
            num_scalar_prefetch=1, grid=(2, nt), in_specs=[blk, blk],
            out_specs=pl.BlockSpec((tr, C), lambda b, i, c: ((2 * c[0] + b) * nt + i, 0))),
    )(core, part, far)


N_DEV = 8


def _allreduce_small(p):
    rs = p.shape[0]

    def body(x_ref, sum_ref, all_ref, send_sems, recv_sems, local_sem):
        x, y, c = lax.axis_index("x"), lax.axis_index("y"), lax.axis_index("c")
        me, sibling = (x, y, c), (x, y, 1 - c)
        chips = [(1 - x, y), (x, 1 - y), (1 - x, 1 - y)]

        def rows(px, py, pc):
            return all_ref.at[pl.ds((4 * px + 2 * py + pc) * rs, rs), :]

        def copy(k, block, to, src=None):
            return pltpu.make_async_remote_copy(
                src_ref=rows(*block) if src is None else src, dst_ref=rows(*block),
                send_sem=send_sems.at[k], recv_sem=recv_sems.at[k], device_id=to, device_id_type=MESH)

        mine = pltpu.make_async_copy(x_ref, rows(*me), local_sem)
        mine.start()
        first = [copy(0, me, sibling, src=x_ref)]
        first += [copy(1 + j, me, (*chip, c), src=x_ref) for j, chip in enumerate(chips)]
        for cp in first:
            cp.start()
        passed = [copy(4 + j, (*chip, c), sibling) for j, chip in enumerate(chips)]
        for j, chip in enumerate(chips):
            copy(1 + j, (*chip, c), me).wait_recv()
            passed[j].start()
        copy(0, sibling, me).wait_recv()
        for j, chip in enumerate(chips):
            copy(4 + j, (*chip, 1 - c), me).wait_recv()
        for cp in first + passed:
            cp.wait_send()
        mine.wait()
        acc = all_ref[0:rs, :]
        for d in range(1, N_DEV):
            acc = acc + all_ref[d * rs:(d + 1) * rs, :]
        sum_ref[...] = acc

    vmem = pl.BlockSpec(memory_space=pltpu.VMEM)
    return pl.pallas_call(
        body, name="allreduce_small", out_shape=jax.ShapeDtypeStruct((rs, 128), F32),
        in_specs=[vmem], out_specs=vmem,
        scratch_shapes=[pltpu.VMEM((N_DEV * rs, 128), F32), pltpu.SemaphoreType.DMA((7,)), pltpu.SemaphoreType.DMA((7,)),
                        pltpu.SemaphoreType.DMA],
    )(p)


WEIGHTS = ('ffn1_norm', 'ffn1_w_gate', 'ffn1_w_up', 'ffn1_w_down', 'mix_norm', 'w_in', 'q_a_norm', 'w_q_b',
           'kv_a_norm', 'w_kv_b', 'q_head_norm', 'k_head_norm', 'conv_w', 'conv_b', 'a_log_fwd', 'a_log_bwd',
           'dt_bias_fwd', 'dt_bias_bwd', 'd_skip', 'ssm_norm', 'w_attn_branch', 'w_ssm_branch', 'w_out',
           'ffn2_norm', 'ffn2_w_gate', 'ffn2_w_up', 'ffn2_w_down')
PACKED = (('ffn1_w_gate', (D_MODEL, D_FF), 1), ('ffn1_w_up', (D_MODEL, D_FF), 1), ('ffn1_w_down', (D_FF, D_MODEL), 0),
          ('w_in', (D_MODEL, sum(IN_SPLITS)), 1), ('w_q_b', (Q_LORA, N_HEADS * QK_HEAD), 1),
          ('w_kv_b', (KV_LORA, N_HEADS * (QK_NOPE + V_HEAD)), 1),
          ('w_attn_branch', (N_HEADS * V_HEAD, D_MODEL), 0), ('w_ssm_branch', (D_INNER, D_MODEL), 0),
          ('w_out', (D_MODEL, D_MODEL), 0),
          ('ffn2_w_gate', (D_MODEL, D_FF), 1), ('ffn2_w_up', (D_MODEL, D_FF), 1), ('ffn2_w_down', (D_FF, D_MODEL), 0))
PACK_W = 1024
N_CHIPS = 4
SMALL = (('ffn1_norm', 1024), ('mix_norm', 1024), ('q_a_norm', 384), ('kv_a_norm', 256), ('q_head_norm', 96),
         ('k_head_norm', 96), ('conv_b', 3072), ('a_log_fwd', 32), ('a_log_bwd', 32), ('dt_bias_fwd', 32),
         ('dt_bias_bwd', 32), ('d_skip', 32), ('ssm_norm', 2048), ('ffn2_norm', 1024),
         ('conv_w', CONV_WIDTH * XBC_DIM), ('loss', 1))


TRANSPOSED = ('ffn1_w_gate', 'ffn1_w_up', 'w_in', 'ffn2_w_gate', 'ffn2_w_up')


def _stored(name, a):
    return a.T if name in TRANSPOSED else a


def _shard_shape(name, shape, axis):
    sh = tuple(s // N_CHIPS if a == axis else s for a, s in enumerate(shape))
    return sh[::-1] if name in TRANSPOSED else sh


def _by_rows(name, axis):
    return name in TRANSPOSED or axis == 0


def _pack_layout():
    out, r = {}, 0
    for name, shape, axis in PACKED:
        n = math.prod(shape) // N_CHIPS // PACK_W
        out[name] = (r, n)
        r += n
    return out, -(-r // 64) * 64


def _pack(shards):
    layout, rows = _pack_layout()
    parts = [shards[name].reshape(-1, PACK_W) for name, _, _ in PACKED]
    parts.append(jnp.zeros((rows - sum(p.shape[0] for p in parts), PACK_W), parts[0].dtype))
    return jnp.concatenate(parts, axis=0)


def _unpack(packed):
    layout, _ = _pack_layout()
    return {name: packed[layout[name][0]:layout[name][0] + layout[name][1]].reshape(_shard_shape(name, shape, axis))
            for name, shape, axis in PACKED}


def _full_from_slots(slots):
    layout, _ = _pack_layout()
    out = {}
    for name, shape, axis in PACKED:
        r, n = layout[name]
        if _by_rows(name, axis):
            out[name] = slots[:, r:r + n].reshape(N_CHIPS * n, PACK_W)
        else:
            sh = _shard_shape(name, shape, axis)
            out[name] = jnp.concatenate([slots[j, r:r + n].reshape(sh) for j in range(N_CHIPS)], axis=axis)
    return out


def _slots_from_full(full):
    layout, rows = _pack_layout()
    parts = []
    for name, shape, axis in PACKED:
        r, n = layout[name]
        if _by_rows(name, axis):
            parts.append(full[name].reshape(N_CHIPS, n, PACK_W))
        else:
            size = shape[axis] // N_CHIPS
            parts.append(jnp.stack([lax.slice_in_dim(full[name], j * size, (j + 1) * size, axis=axis).reshape(n, PACK_W)
                                    for j in range(N_CHIPS)]))
    parts.append(jnp.zeros((N_CHIPS, rows - sum(p.shape[1] for p in parts), PACK_W), parts[0].dtype))
    return jnp.concatenate(parts, axis=1)


def _pack_small(vals):
    parts = []
    for name, n in SMALL:
        pad = -(-n // 128) * 128 - n
        parts.append(jnp.pad(vals[name].reshape(-1).astype(F32), (0, pad)).reshape(-1, 128))
    rows = sum(p.shape[0] for p in parts)
    parts.append(jnp.zeros((-(-rows // 8) * 8 - rows, 128), F32))
    return jnp.concatenate(parts, axis=0)


def _unpack_small(packed):
    out, r = {}, 0
    for name, n in SMALL:
        k = -(-n // 128)
        out[name] = packed[r:r + k].reshape(-1)[:n]
        r += k
    return out


def _pad_heads(w, axis, per_head, lo, hi):
    shape = w.shape
    w = w.reshape(shape[:axis] + (N_HEADS, per_head) + shape[axis + 1:])
    w = lax.slice_in_dim(w, lo, hi, axis=axis + 1)
    pad = [(0, 0)] * w.ndim
    pad[axis + 1] = (0, HP - (hi - lo))
    w = jnp.pad(w, pad)
    return w.reshape(shape[:axis] + (N_HEADS * HP,) + shape[axis + 1:])


def _unpad_heads(w, axis, keep):
    shape = w.shape
    w = w.reshape(shape[:axis] + (N_HEADS, HP) + shape[axis + 1:])
    return lax.slice_in_dim(w, 0, keep, axis=axis + 1)


def _pad_w_in(wt):
    o = [0]
    for s in IN_SPLITS:
        o.append(o[-1] + s)
    cq, ckv, kpe, z, xbc, dtf, dtb, ga, gb = [wt[o[i]:o[i + 1]] for i in range(len(IN_SPLITS))]
    kpe_pad = jnp.pad(kpe, ((QK_NOPE, HP - QK_HEAD), (0, 0)))
    dt_pad = jnp.pad(jnp.concatenate([dtf, dtb], axis=0), ((0, HP - 2 * SSM_HEADS), (0, 0)))
    return jnp.concatenate([z, ga, gb, xbc, cq, ckv, kpe_pad, dt_pad], axis=0)


def _unpad_w_in(gt):
    z, ga, gb, xbc = gt[U_Z:U_GA], gt[U_GA:U_GB], gt[U_GB:U_XBC], gt[U_XBC:U_SMALL]
    s = gt[U_SMALL:]
    cq, ckv = s[S_CQ:S_CKV], s[S_CKV:S_KPE]
    kpe = s[S_KPE + QK_NOPE:S_KPE + QK_HEAD]
    dtf, dtb = s[S_DT:S_DT + SSM_HEADS], s[S_DT + SSM_HEADS:S_DT + 2 * SSM_HEADS]
    return jnp.concatenate([cq, ckv, kpe, z, xbc, dtf, dtb, ga, gb], axis=0)


def _lanes128(parts):
    row = jnp.concatenate([p.reshape(-1) for p in parts])
    return jnp.pad(row, (0, HP - row.shape[0])).reshape(1, HP)


FF_TILE = D_FF // 2
WGRAD = BF


def _ffn_fwd(x, g, wg_t, wu_t, wd, tag):
    h = _rms_fwd(x, g, name=tag + "_norm")
    gate, up, act = _mm([h], [wg_t, wu_t], name=tag + "_up", tb=True, out_dtypes=(F32, F32, BF), tm=512, tn=FF_TILE,
                        epilogue=lambda a, b: (a, b, _silu(a) * b))
    out = _mm([act], [wd], name=tag + "_down", extras=[x], epilogue=lambda acc, r: (r + 0.5 * acc,))
    return out, (h, gate, up, act)


def _ffn_bwd(dout, dout_bf, x, g, wg_t, wu_t, wd, saved, tag):
    h, gate, up, act = saved
    dgate, dup = _mm([dout_bf], [wd], name=tag + "_down_dx", tb=True, extras=[gate, up], out_dtypes=(BF, BF),
                     tm=512, tn=FF_TILE, epilogue=lambda acc, a, b: (0.5 * acc * b * _dsilu(a), 0.5 * acc * _silu(a)))
    dwd = _mm([act], [dout_bf], name=tag + "_down_dw", ta=True, tm=FF_TILE, tk=1024, out_dtypes=(WGRAD,),
              epilogue=lambda acc: (0.5 * acc,))
    dwg_t, dwu_t = _mm([dgate, dup], [h, h], name=tag + "_up_dw", ta=True, separate=True, out_dtypes=(WGRAD, WGRAD),
                       tm=FF_TILE, tk=1024)
    dh = _mm([dgate, dup], [wg_t, wu_t], name=tag + "_up_dx")
    dx, dx_bf, dg = _rms_bwd(dh, x, g, name=tag + "_norm_bwd", add=dout, out_dtypes=(F32, BF))
    return dx, dx_bf, dg, dwg_t, dwu_t, dwd


KPE_BLK = (U_SMALL + S_KPE) // HP
SMALL_BLK = U_SMALL // SMALL_W


def _local_step(x, pos_col, target, W, P):
    T = x.shape[0]
    sig = jax.nn.sigmoid
    x1, ffn1 = _ffn_fwd(x, P["ffn1_norm"], W["wg1"], W["wu1"], W["wd1"], "ffn1")
    h = _rms_fwd(x1, P["mix_norm"], name="mix_norm")
    u = _mm([h], [W["w_in"]], name="in_proj", tb=True, tn=1152)
    cqn = _rms_fwd(u, P["q_a_norm"], name="q_a_norm", blk_w=SMALL_W, blk_idx=SMALL_BLK, off=S_CQ, width=Q_LORA)
    ckvn = _rms_fwd(u, P["kv_a_norm"], name="kv_a_norm", blk_w=SMALL_W, blk_idx=SMALL_BLK, off=S_CKV, width=KV_LORA)
    q_raw = _mm([cqn], [W["wq"]], name="q_proj")
    def with_ones_lane(acc_k, acc_v):
        lane = lax.broadcasted_iota(jnp.int32, acc_v.shape, 1)
        return acc_k, jnp.where((lane & (HP - 1)) == V_HEAD, 1.0, acc_v)

    k_raw, v = _mm([ckvn], [W["wk"], W["wv"]], name="kv_proj", out_dtypes=(F32, BF), epilogue=with_ones_lane)
    rc, rs = _rope_tables(pos_col, P["freq"])
    q = _qk_prep_fwd(q_raw, None, P["q_head_norm"], rc, rs, name="q_prep", out_scale=Q_SCALE)
    k = _qk_prep_fwd(k_raw, u, P["k_head_norm"], rc, rs, name="k_prep", kpe_blk=KPE_BLK)
    o, lse = _attn_fwd(q, k, v)
    pre, act = _conv_fwd(u, P["conv_w8"], P["conv_b"])
    scan_cols, scan_rows = _ssd_prep(u, P["dt_bias8"], P["a_log8"])
    y_f, st_f = _ssd_fwd(act, scan_cols, scan_rows, rev=False, name="ssd_fwd_f")
    y_b, st_b = _ssd_fwd(act, scan_cols, scan_rows, rev=True, name="ssd_fwd_b")
    ysum, m = _ssm_combine_fwd(y_f, y_b, act, u, P["d_skip_lanes"], P["ssm_norm"])
    ab = _mm([o], [W["pa"]], name="attn_branch")
    mb, merged = _mm([m], [W["pb"]], name="ssm_branch", extras=[ab, u, u], extra_offs=(0, U_GA, U_GB), out_dtypes=(F32, BF),
                     epilogue=lambda acc, a, ga, gb: (acc, sig(ga) * a + sig(gb) * acc))
    x2 = _mm([merged], [W["wo"]], name="out_proj", extras=[x1], epilogue=lambda acc, r: (r + acc,))
    y, ffn2 = _ffn_fwd(x2, P["ffn2_norm"], W["wg2"], W["wu2"], W["wd2"], "ffn2")
    dy, dy_bf, loss = _loss_head(y, target)
    dx2, dx2_bf, dg_ffn2, dwg2, dwu2, dwd2 = _ffn_bwd(dy, dy_bf, x2, P["ffn2_norm"], W["wg2"], W["wu2"], W["wd2"], ffn2,
                                                      "ffn2")

    def gate_bwd(dmrg, a, b, ga, gb):
        sa, sb = sig(ga), sig(gb)
        return dmrg * sa, dmrg * sb, dmrg * a * sa * (1.0 - sa), dmrg * b * sb * (1.0 - sb)

    dab, dmb, dga, dgb = _mm([dx2_bf], [W["wo"]], name="out_proj_dx", tb=True, extras=[ab, mb, u, u],
                             extra_offs=(0, 0, U_GA, U_GB), out_dtypes=(BF,) * 4, epilogue=gate_bwd)
    dwo = _mm([merged], [dx2_bf], name="out_proj_dw", ta=True, out_dtypes=(WGRAD,))
    dpa = _mm([o], [dab], name="attn_branch_dw", ta=True, out_dtypes=(WGRAD,))
    do = _mm([dab], [W["pa"]], name="attn_branch_dx", tb=True)
    dpb = _mm([m], [dmb], name="ssm_branch_dw", ta=True, out_dtypes=(WGRAD,))
    dm = _mm([dmb], [W["pb"]], name="ssm_branch_dx", tb=True)
    dyssd, dz, dxs_skip, dg_ssm, dskip = _ssm_combine_bwd(dm, ysum, act, u, P["d_skip_lanes"], P["ssm_norm"])
    dxs_f, db_f, dc_f, dsel_f, dtot_f = _ssd_bwd(act, scan_cols, scan_rows, st_f, dyssd, rev=False, name="ssd_bwd_f")
    dxs_b, db_b, dc_b, dsel_b, dtot_b = _ssd_bwd(act, scan_cols, scan_rows, st_b, dyssd, rev=True, name="ssd_bwd_b")
    ddt, dalog, dbias = _ssd_prep_bwd(u, P["dt_bias8"], P["a_log8"], dsel_f, dtot_f, dsel_b, dtot_b)
    dxbc, dconv = [], []
    for tag, col0, parts in (("x", 0, [dxs_f, dxs_b, dxs_skip]), ("b", D_INNER, [db_f, db_b]),
                             ("c", D_INNER + SSM_GROUPS * D_STATE, [dc_f, dc_b])):
        dpre = _conv_dpre(parts, pre, col0, name="conv_dpre_" + tag)
        dxp, dwp = _conv_bwd(dpre, u, P["conv_w8"], col0, name="conv_bwd_" + tag)
        dxbc.append(dxp)
        dconv.append(dwp)
    dconv = jnp.concatenate(dconv, axis=1)
    dq, dk, dv = _attn_bwd(q, k, v, do, o, lse)
    dq_raw, dg_qh = _qk_prep_bwd(dq, q_raw, None, P["q_head_norm"], rc, rs, name="q_prep_bwd", in_scale=ATTN_SCALE)
    dk_raw, dg_kh, dkpe = _qk_prep_bwd(dk, k_raw, u, P["k_head_norm"], rc, rs, name="k_prep_bwd", kpe_blk=KPE_BLK,
                                       in_scale=1.0 / LOG2E)
    dwq = _mm([cqn], [dq_raw], name="q_proj_dw", ta=True, out_dtypes=(WGRAD,))
    dcqn = _mm([dq_raw], [W["wq"]], name="q_proj_dx", tb=True)
    dwk, dwv = _mm([ckvn], [dk_raw, dv], name="kv_proj_dw", ta=True, out_dtypes=(WGRAD, WGRAD))
    dckvn = _mm([dk_raw, dv], [W["wk"], W["wv"]], name="kv_proj_dx", tb=True)
    dcq, dg_qa = _rms_bwd(dcqn, u, P["q_a_norm"], name="q_a_norm_bwd", blk_w=SMALL_W, blk_idx=SMALL_BLK, off=S_CQ,
                          width=Q_LORA, out_dtypes=(BF,))
    dckv, dg_kva = _rms_bwd(dckvn, u, P["kv_a_norm"], name="kv_a_norm_bwd", blk_w=SMALL_W, blk_idx=SMALL_BLK,
                            off=S_CKV, width=KV_LORA, out_dtypes=(BF,))
    du = jnp.concatenate([dz, dga, dgb] + dxbc + [dcq, dckv, dkpe.astype(BF), ddt.astype(BF)], axis=1)
    dw_in = _mm([du], [h], name="in_proj_dw", ta=True, tm=1152, out_dtypes=(WGRAD,))
    dh = _mm([du], [W["w_in"]], name="in_proj_dx")
    dx1, dx1_bf, dg_mix = _rms_bwd(dh, x1, P["mix_norm"], name="mix_norm_bwd", add=dx2, out_dtypes=(F32, BF))
    dx, _, dg_ffn1, dwg1, dwu1, dwd1 = _ffn_bwd(dx1, dx1_bf, x, P["ffn1_norm"], W["wg1"], W["wu1"], W["wd1"], ffn1, "ffn1")
    dW = dict(wg1=dwg1, wu1=dwu1, wd1=dwd1, w_in=dw_in, wq=dwq, wk=dwk, wv=dwv, pa=dpa, pb=dpb, wo=dwo,
              wg2=dwg2, wu2=dwu2, wd2=dwd2)
    dP = dict(ffn1_norm=dg_ffn1[0], mix_norm=dg_mix[0], q_a_norm=dg_qa[0], kv_a_norm=dg_kva[0],
              q_head_norm=dg_qh[0, :QK_HEAD], k_head_norm=dg_kh[0, :QK_HEAD], conv_b=dconv[CONV_WIDTH],
              a_log_fwd=dalog[0, :SSM_HEADS], a_log_bwd=dalog[0, SSM_HEADS:2 * SSM_HEADS],
              dt_bias_fwd=dbias[0, :SSM_HEADS], dt_bias_bwd=dbias[0, SSM_HEADS:2 * SSM_HEADS],
              d_skip=dskip[0].reshape(SSM_GROUPS, HP)[:, :HG], ssm_norm=dg_ssm[0], ffn2_norm=dg_ffn2[0],
              conv_w=dconv[:CONV_WIDTH], loss=loss[0, 0])
    return dx, dW, dP


def _prepare(w, conv_w_full):
    kvb = w["w_kv_b"]
    W = dict(wg1=w["ffn1_w_gate"], wu1=w["ffn1_w_up"], wd1=w["ffn1_w_down"], w_in=_pad_w_in(w["w_in"]),
             wq=_pad_heads(w["w_q_b"], 1, QK_HEAD, 0, QK_HEAD),
             wk=_pad_heads(kvb, 1, QK_NOPE + V_HEAD, 0, QK_NOPE),
             wv=_pad_heads(kvb, 1, QK_NOPE + V_HEAD, QK_NOPE, QK_NOPE + V_HEAD),
             pa=_pad_heads(w["w_attn_branch"], 0, V_HEAD, 0, V_HEAD), pb=w["w_ssm_branch"], wo=w["w_out"],
             wg2=w["ffn2_w_gate"], wu2=w["ffn2_w_up"], wd2=w["ffn2_w_down"])
    inv_freq = [1.0 / (ROPE_BASE ** (j / QK_ROPE)) for j in range(0, QK_ROPE, 2)]
    freq = [0.0] * QK_NOPE + inv_freq + inv_freq + [0.0] * (HP - QK_HEAD)
    P = {n: w[n] for n in ("ffn1_norm", "mix_norm", "q_a_norm", "kv_a_norm", "ssm_norm", "ffn2_norm", "conv_b")}
    P.update(q_head_norm=_lanes128([w["q_head_norm"]]), k_head_norm=_lanes128([w["k_head_norm"]]),
             conv_w8=jnp.pad(conv_w_full, ((0, 8 - CONV_WIDTH), (0, 0))),
             dt_bias8=jnp.broadcast_to(_lanes128([w["dt_bias_fwd"], w["dt_bias_bwd"]]), (8, HP)),
             a_log8=jnp.broadcast_to(_lanes128([w["a_log_fwd"], w["a_log_bwd"]]), (8, HP)),
             d_skip_lanes=jnp.repeat(w["d_skip"].reshape(-1), PH).reshape(1, D_INNER),
             freq=jnp.asarray(freq, F32).reshape(1, HP))
    return W, P


def _unprepare(dW):
    dkvb = jnp.concatenate([_unpad_heads(dW["wk"], 1, QK_NOPE), _unpad_heads(dW["wv"], 1, V_HEAD)], axis=2)
    return dict(ffn1_w_gate=dW["wg1"], ffn1_w_up=dW["wu1"], ffn1_w_down=dW["wd1"], w_in=_unpad_w_in(dW["w_in"]),
                w_q_b=_unpad_heads(dW["wq"], 1, QK_HEAD).reshape(Q_LORA, N_HEADS * QK_HEAD),
                w_kv_b=dkvb.reshape(KV_LORA, N_HEADS * (QK_NOPE + V_HEAD)),
                w_attn_branch=_unpad_heads(dW["pa"], 0, V_HEAD).reshape(N_HEADS * V_HEAD, D_MODEL),
                w_ssm_branch=dW["pb"], w_out=dW["wo"],
                ffn2_w_gate=dW["wg2"], ffn2_w_up=dW["wu2"], ffn2_w_down=dW["wd2"])


def kernel(x, positions, ffn1_norm, ffn1_w_gate, ffn1_w_up, ffn1_w_down, mix_norm, w_in, q_a_norm, w_q_b, kv_a_norm, w_kv_b, q_head_norm, k_head_norm, conv_w, conv_b, a_log_fwd, a_log_bwd, dt_bias_fwd, dt_bias_bwd, d_skip, ssm_norm, w_attn_branch, w_ssm_branch, w_out, ffn2_norm, ffn2_w_gate, ffn2_w_up, ffn2_w_down, loss_target, m_ffn1_norm, m_ffn1_w_gate, m_ffn1_w_up, m_ffn1_w_down, m_mix_norm, m_w_in, m_q_a_norm, m_w_q_b, m_kv_a_norm, m_w_kv_b, m_q_head_norm, m_k_head_norm, m_conv_w, m_conv_b, m_a_log_fwd, m_a_log_bwd, m_dt_bias_fwd, m_dt_bias_bwd, m_d_skip, m_ssm_norm, m_w_attn_branch, m_w_ssm_branch, m_w_out, m_ffn2_norm, m_ffn2_w_gate, m_ffn2_w_up, m_ffn2_w_down, v_ffn1_norm, v_ffn1_w_gate, v_ffn1_w_up, v_ffn1_w_down, v_mix_norm, v_w_in, v_q_a_norm, v_w_q_b, v_kv_a_norm, v_w_kv_b, v_q_head_norm, v_k_head_norm, v_conv_w, v_conv_b, v_a_log_fwd, v_a_log_bwd, v_dt_bias_fwd, v_dt_bias_bwd, v_d_skip, v_ssm_norm, v_w_attn_branch, v_w_ssm_branch, v_w_out, v_ffn2_norm, v_ffn2_w_gate, v_ffn2_w_up, v_ffn2_w_down):
    given = dict(locals())
    T = x.shape[1]
    packed_names = [name for name, _, _ in PACKED]

    def two_d(a):
        return a.reshape(a.shape[1], -1) if a.ndim > 2 else a

    def kept(n, a):
        return _stored(n, two_d(a))

    w_loc = {n: kept(n, given[n]) for n in WEIGHTS}
    wb = _pack({n: w_loc[n].astype(BF) for n in packed_names})
    wf = jnp.pad(w_loc["conv_w"], ((0, 8 - CONV_WIDTH), (0, 0)))
    gb, gf = _gather_chips(wb, wf)
    full = _full_from_slots(gb)
    conv_w_full = jnp.concatenate([gf[j, :CONV_WIDTH] for j in range(N_CHIPS)], axis=1)
    full.update({n: w_loc[n] for n in WEIGHTS if n not in full and n != "conv_w"})
    W, P = _prepare(full, conv_w_full)
    dx, dW, dP = _local_step(x.reshape(T, D_MODEL), positions.reshape(T, 1).astype(F32), loss_target.reshape(T, D_MODEL), W, P)
    gp = _slots_from_full(_unprepare(dW))
    core = lax.axis_index("c").astype(jnp.int32).reshape(1)
    both_cores = _add_halves(gp, _halves_to_sibling(gp), core)
    cx, cy = lax.axis_index("x"), lax.axis_index("y")
    chips = jnp.stack([2 * cx + cy, 2 * (1 - cx) + cy, 2 * cx + (1 - cy)]).astype(jnp.int32)
    part, on = _add_near(both_cores, _exchange_near(both_cores), chips)
    grads = _unpack(_join_halves(_add_far(part, _exchange_far(on), core)))
    small = _unpack_small(_allreduce_small(_pack_small(dP)))
    grads.update({n: small[n].reshape(1, -1) for n, _ in SMALL if n not in ("conv_w", "loss")})
    grads["conv_w"] = lax.dynamic_slice_in_dim(small["conv_w"].reshape(CONV_WIDTH, XBC_DIM), _my_chip() * (XBC_DIM // N_CHIPS),
                                               XBC_DIM // N_CHIPS, axis=1)
    out_g, out_d, out_m, out_v = [], [], [], []
    for n in WEIGHTS:
        shape = given[n].shape
        delta, new_m, new_v = _adamw(w_loc[n], grads[n], kept(n, given["m_" + n]), kept(n, given["v_" + n]), name="adamw_" + n)
        for outs, a in ((out_g, grads[n]), (out_d, delta), (out_m, new_m), (out_v, new_v)):
            outs.append(_stored(n, a).reshape(shape))
    return (small["loss"].reshape(()), dx.reshape(x.shape), *out_g, *out_d, *out_m, *out_v)
```

```python
import functools
import math

import jax
import jax.numpy as jnp
from jax import lax
from jax.experimental import pallas as pl
from jax.experimental.pallas import tpu as pltpu

BF = jnp.bfloat16
F32 = jnp.float32
HI = lax.Precision.HIGHEST
MESH = pl.DeviceIdType.MESH

D_MODEL = 1024
D_FF = 2816
EPS = 1e-6
N_HEADS = 16
QK_NOPE = 64
QK_ROPE = 32
QK_HEAD = 96
V_HEAD = 64
Q_LORA = 384
KV_LORA = 256
ROPE_BASE = 10000.0
D_INNER = 2048
SSM_HEADS = 32
SSM_GROUPS = 4
D_STATE = 128
CONV_WIDTH = 5
CHUNK = 128
XBC_DIM = 3072
HP = 128
GW = D_INNER // SSM_GROUPS
HG = SSM_HEADS // SSM_GROUPS
PH = 64
U_Z, U_GA, U_GB, U_XBC, U_SMALL = 0, 2048, 3072, 4096, 7168
S_CQ, S_CKV, S_KPE, S_DT, SMALL_W = 0, 384, 640, 768, 896
U_PAD = U_SMALL + SMALL_W
IN_SPLITS = (Q_LORA, KV_LORA, QK_ROPE, D_INNER, XBC_DIM, SSM_HEADS, SSM_HEADS, D_MODEL, D_MODEL)

ADAM_LR = 0.001
ADAM_B1 = 0.9
ADAM_B2 = 0.999
ADAM_EPS = 1e-08
ADAM_WD = 0.01
ADAM_STEP = 10

NN = (((1,), (0,)), ((), ()))
NT = (((1,), (1,)), ((), ()))
TN = (((0,), (0,)), ((), ()))


def _pick(n, pref):
    best = None
    d = 128
    while d <= min(n, pref):
        if n % d == 0:
            best = d
        d += 128
    return best if best is not None else n


def _silu(x):
    return x * jax.nn.sigmoid(x)


def _dsilu(x):
    s = jax.nn.sigmoid(x)
    return s * (1.0 + x * (1.0 - s))


def _softplus(x):
    return jnp.maximum(x, 0.0) + jnp.log(1.0 + jnp.exp(-jnp.abs(x)))


def _mm(As, Bs, *, name, ta=False, tb=False, out_dtypes=(F32,), epilogue=None, extras=(), extra_offs=None,
        tm=1024, tn=512, tk=2048, separate=False):
    As, Bs, extras = list(As), list(Bs), list(extras)
    a0, b0 = As[0], Bs[0]
    M, K = (a0.shape[1], a0.shape[0]) if ta else a0.shape
    N = b0.shape[0] if tb else b0.shape[1]
    tm, tn, tk = _pick(M, tm), _pick(N, tn), _pick(K, tk)
    nk = K // tk
    n_a, n_b, n_e, n_o = len(As), len(Bs), len(extras), len(out_dtypes)
    n_acc = (n_b if n_a == 1 or separate else 1) if nk > 1 else 0
    if extra_offs is None:
        extra_offs = (0,) * n_e
    dn = (((0,) if ta else (1,), (1,) if tb else (0,)), ((), ()))
    bytes_a = sum(a.size * a.dtype.itemsize for a in As)
    bytes_b = sum(b.size * b.dtype.itemsize for b in Bs)
    n_outer = (N // tn) * bytes_a + bytes_b < (M // tm) * bytes_b + bytes_a

    def products(a_refs, b_refs):
        if n_a == 1:
            a = a_refs[0][...].astype(BF)
            return [lax.dot_general(a, b[...].astype(BF), dn, preferred_element_type=F32) for b in b_refs]
        if separate:
            return [lax.dot_general(a[...].astype(BF), b[...].astype(BF), dn, preferred_element_type=F32)
                    for a, b in zip(a_refs, b_refs)]
        total = None
        for a, b in zip(a_refs, b_refs):
            p = lax.dot_general(a[...].astype(BF), b[...].astype(BF), dn, preferred_element_type=F32)
            total = p if total is None else total + p
        return [total]

    def finish(accs, e_refs, o_refs):
        ex = [e[...] for e in e_refs]
        outs = epilogue(*accs, *ex) if epilogue is not None else tuple(accs)
        for o_ref, val in zip(o_refs, outs):
            o_ref[...] = val.astype(o_ref.dtype)

    def body(*refs):
        a_refs, b_refs = refs[:n_a], refs[n_a:n_a + n_b]
        e_refs = refs[n_a + n_b:n_a + n_b + n_e]
        o_refs = refs[n_a + n_b + n_e:n_a + n_b + n_e + n_o]
        acc_refs = refs[n_a + n_b + n_e + n_o:]
        if nk == 1:
            finish(products(a_refs, b_refs), e_refs, o_refs)
            return
        k = pl.program_id(2)

        @pl.when(k == 0)
        def _():
            for acc in acc_refs:
                acc[...] = jnp.zeros_like(acc)

        for acc, p in zip(acc_refs, products(a_refs, b_refs)):
            acc[...] += p

        @pl.when(k == nk - 1)
        def _():
            finish([acc[...] for acc in acc_refs], e_refs, o_refs)

    def at(f):
        return (lambda j, i, k: f(i, j, k)) if n_outer else f

    a_spec = pl.BlockSpec((tk, tm), at(lambda i, j, k: (k, i))) if ta else pl.BlockSpec((tm, tk), at(lambda i, j, k: (i, k)))
    b_spec = pl.BlockSpec((tn, tk), at(lambda i, j, k: (j, k))) if tb else pl.BlockSpec((tk, tn), at(lambda i, j, k: (k, j)))
    e_specs = [pl.BlockSpec((tm, tn), at(functools.partial(lambda i, j, k, o: (i, j + o), o=off // tn))) for off in extra_offs]
    for off in extra_offs:
        assert off % tn == 0
    outs = pl.pallas_call(
        body, name=name,
        out_shape=tuple(jax.ShapeDtypeStruct((M, N), dt) for dt in out_dtypes),
        grid=(N // tn, M // tm, nk) if n_outer else (M // tm, N // tn, nk),
        in_specs=[a_spec] * n_a + [b_spec] * n_b + e_specs,
        out_specs=tuple(pl.BlockSpec((tm, tn), at(lambda i, j, k: (i, j))) for _ in out_dtypes),
        scratch_shapes=[pltpu.VMEM((tm, tn), F32)] * n_acc,
        compiler_params=pltpu.CompilerParams(dimension_semantics=("parallel", "parallel", "arbitrary")),
    )(*As, *Bs, *extras)
    return outs[0] if n_o == 1 else outs


def _rms_fwd(x, g, *, name, blk_w=None, blk_idx=0, off=0, width=None, out_dtype=BF):
    T = x.shape[0]
    blk_w = x.shape[1] if blk_w is None else blk_w
    width = blk_w if width is None else width
    tt = _pick(T, 512)

    def body(x_ref, g_ref, o_ref):
        xf = x_ref[:, off:off + width]
        r = lax.rsqrt(jnp.mean(xf * xf, axis=-1, keepdims=True) + EPS)
        o_ref[...] = (xf * r * g_ref[...]).astype(o_ref.dtype)

    return pl.pallas_call(
        body, name=name, out_shape=jax.ShapeDtypeStruct((T, width), out_dtype), grid=(T // tt,),
        in_specs=[pl.BlockSpec((tt, blk_w), lambda i: (i, blk_idx)), pl.BlockSpec((1, width), lambda i: (0, 0))],
        out_specs=pl.BlockSpec((tt, width), lambda i: (i, 0)),
    )(x, g)


def _rms_bwd(dy, x, g, *, name, blk_w=None, blk_idx=0, off=0, width=None, add=None, out_dtypes=(F32,)):
    T = x.shape[0]
    blk_w = x.shape[1] if blk_w is None else blk_w
    width = blk_w if width is None else width
    tt = _pick(T, 512)
    has_add = add is not None
    n_dx = len(out_dtypes)

    def body(*refs):
        dy_ref, x_ref, g_ref = refs[:3]
        dx_refs, dg_ref = refs[3 + has_add:3 + has_add + n_dx], refs[-1]
        xf = x_ref[:, off:off + width]
        d = dy_ref[...].astype(F32)
        r = lax.rsqrt(jnp.mean(xf * xf, axis=-1, keepdims=True) + EPS)
        gd = d * g_ref[...]
        dx = r * gd - xf * (r * r * r) * jnp.mean(gd * xf, axis=-1, keepdims=True)
        if has_add:
            dx = dx + refs[3][...]
        for dx_ref in dx_refs:
            dx_ref[...] = dx.astype(dx_ref.dtype)

        @pl.when(pl.program_id(0) == 0)
        def _():
            dg_ref[...] = jnp.zeros_like(dg_ref)

        dg_ref[...] += jnp.broadcast_to(jnp.sum(d * xf * r, axis=0, keepdims=True), dg_ref.shape)

    row = pl.BlockSpec((tt, width), lambda i: (i, 0))
    in_specs = [row, pl.BlockSpec((tt, blk_w), lambda i: (i, blk_idx)), pl.BlockSpec((1, width), lambda i: (0, 0))]
    args = [dy, x, g]
    if has_add:
        in_specs.append(row)
        args.append(add)
    return pl.pallas_call(
        body, name=name,
        out_shape=tuple(jax.ShapeDtypeStruct((T, width), dt) for dt in out_dtypes) + (jax.ShapeDtypeStruct((8, width), F32),),
        grid=(T // tt,), in_specs=in_specs,
        out_specs=(row,) * n_dx + (pl.BlockSpec((8, width), lambda i: (0, 0)),),
        compiler_params=pltpu.CompilerParams(dimension_semantics=("arbitrary",)),
    )(*args)


def _rope_tables(pos_col, freq_lane):
    T = pos_col.shape[0]
    tt = _pick(T, 512)

    def body(p_ref, f_ref, c_ref, s_ref):
        ang = p_ref[...] * f_ref[...]
        lane = lax.broadcasted_iota(jnp.int32, ang.shape, 1)
        c_ref[...] = jnp.where(lane < QK_HEAD, jnp.cos(ang), 0.0)
        sn = jnp.sin(ang)
        s_ref[...] = jnp.where((lane >= QK_NOPE) & (lane < QK_NOPE + 16), -sn,
                               jnp.where((lane >= QK_NOPE + 16) & (lane < QK_HEAD), sn, 0.0))

    return pl.pallas_call(
        body, name="rope_tables", out_shape=(jax.ShapeDtypeStruct((T, HP), F32),) * 2, grid=(T // tt,),
        in_specs=[pl.BlockSpec((tt, 1), lambda i: (i, 0)), pl.BlockSpec((1, HP), lambda i: (0, 0))],
        out_specs=(pl.BlockSpec((tt, HP), lambda i: (i, 0)),) * 2,
    )(pos_col, freq_lane)


def _swap_rope_halves(n):
    src = lax.broadcasted_iota(jnp.int32, (HP, HP), 0)
    dst = lax.broadcasted_iota(jnp.int32, (HP, HP), 1)
    lo = (dst >= QK_NOPE) & (dst < QK_NOPE + 16) & (src == dst + 16)
    hi = (dst >= QK_NOPE + 16) & (dst < QK_HEAD) & (src == dst - 16)
    return _split_dot(n, jnp.where(lo | hi, 1.0, 0.0).astype(BF), 2)


def _qk_prep_fwd(raw, kpe, gain, C, S, *, name, kpe_blk=0, out_scale=1.0):
    T = raw.shape[0]
    tt = _pick(T, 256)
    has_kpe = kpe is not None

    def body(*refs):
        if has_kpe:
            raw_ref, kpe_ref, g_ref, c_ref, s_ref, o_ref = refs
        else:
            raw_ref, g_ref, c_ref, s_ref, o_ref = refs
        for h in range(N_HEADS):
            hs = slice(HP * h, HP * (h + 1))
            xr = raw_ref[:, hs] + kpe_ref[...] if has_kpe else raw_ref[:, hs]
            r = lax.rsqrt(jnp.sum(xr * xr, axis=-1, keepdims=True) * (1.0 / QK_HEAD) + EPS)
            n = xr * r * g_ref[...]
            o_ref[:, hs] = ((n * c_ref[...] + _swap_rope_halves(n) * s_ref[...]) * out_scale).astype(o_ref.dtype)

    heads = pl.BlockSpec((tt, N_HEADS * HP), lambda i: (i, 0))
    shared = pl.BlockSpec((tt, HP), lambda i: (i, 0))
    kpe_spec = pl.BlockSpec((tt, HP), lambda i: (i, kpe_blk))
    in_specs = [heads] + ([kpe_spec] if has_kpe else []) + [pl.BlockSpec((1, HP), lambda i: (0, 0)), shared, shared]
    args = [raw] + ([kpe] if has_kpe else []) + [gain, C, S]
    return pl.pallas_call(
        body, name=name, out_shape=jax.ShapeDtypeStruct(raw.shape, BF), grid=(T // tt,),
        in_specs=in_specs, out_specs=heads,
    )(*args)


def _qk_prep_bwd(dout, raw, kpe, gain, C, S, *, name, kpe_blk=0, in_scale=1.0):
    T = raw.shape[0]
    tt = _pick(T, 256)
    has_kpe = kpe is not None

    def body(*refs):
        if has_kpe:
            d_ref, raw_ref, kpe_ref, g_ref, c_ref, s_ref, dx_ref, dg_ref, dkpe_ref = refs
        else:
            d_ref, raw_ref, g_ref, c_ref, s_ref, dx_ref, dg_ref = refs
        dg = jnp.zeros((1, HP), F32)
        dkpe = jnp.zeros((tt, HP), F32)
        for h in range(N_HEADS):
            hs = slice(HP * h, HP * (h + 1))
            xr = raw_ref[:, hs] + kpe_ref[...] if has_kpe else raw_ref[:, hs]
            d = d_ref[:, hs].astype(F32) * in_scale
            r = lax.rsqrt(jnp.sum(xr * xr, axis=-1, keepdims=True) * (1.0 / QK_HEAD) + EPS)
            dn = d * c_ref[...] + _swap_rope_halves(d * s_ref[...])
            gd = dn * g_ref[...]
            dx = r * gd - xr * (r * r * r) * (jnp.sum(gd * xr, axis=-1, keepdims=True) * (1.0 / QK_HEAD))
            dx_ref[:, hs] = dx.astype(dx_ref.dtype)
            dg = dg + jnp.sum(dn * xr * r, axis=0, keepdims=True)
            dkpe = dkpe + dx

        @pl.when(pl.program_id(0) == 0)
        def _():
            dg_ref[...] = jnp.zeros_like(dg_ref)

        dg_ref[...] += jnp.broadcast_to(dg, dg_ref.shape)
        if has_kpe:
            dkpe_ref[...] = dkpe

    heads = pl.BlockSpec((tt, N_HEADS * HP), lambda i: (i, 0))
    shared = pl.BlockSpec((tt, HP), lambda i: (i, 0))
    kpe_spec = pl.BlockSpec((tt, HP), lambda i: (i, kpe_blk))
    in_specs = [heads, heads] + ([kpe_spec] if has_kpe else []) + [pl.BlockSpec((1, HP), lambda i: (0, 0)), shared, shared]
    args = [dout, raw] + ([kpe] if has_kpe else []) + [gain, C, S]
    out_shape = [jax.ShapeDtypeStruct(raw.shape, BF), jax.ShapeDtypeStruct((8, HP), F32)]
    out_specs = [heads, pl.BlockSpec((8, HP), lambda i: (0, 0))]
    if has_kpe:
        out_shape.append(jax.ShapeDtypeStruct((T, HP), F32))
        out_specs.append(shared)
    return pl.pallas_call(
        body, name=name, out_shape=tuple(out_shape), grid=(T // tt,),
        in_specs=in_specs, out_specs=tuple(out_specs),
        compiler_params=pltpu.CompilerParams(dimension_semantics=("arbitrary",)),
    )(*args)


ATTN_SCALE = 1.0 / math.sqrt(QK_HEAD)
LOG2E = 1.0 / math.log(2.0)
Q_SCALE = ATTN_SCALE * LOG2E


def _attn_fwd(q, k, v):
    T = q.shape[0]
    tq = _pick(T, 256)

    def body(q_ref, k_ref, v_ref, o_ref, lse_ref):
        s = lax.dot_general(q_ref[...], k_ref[...], NT, preferred_element_type=F32)
        m = jnp.max(s, axis=-1, keepdims=True)
        p = jnp.exp2(s - m)
        o = jnp.dot(p.astype(BF), v_ref[...], preferred_element_type=F32)
        l = o[:, V_HEAD:V_HEAD + 1]
        o_ref[...] = o / l
        lse_ref[...] = jnp.broadcast_to(m + jnp.log2(l), lse_ref.shape)

    qs = pl.BlockSpec((tq, HP), lambda h, i: (i, h))
    kv = pl.BlockSpec((T, HP), lambda h, i: (0, h))
    return pl.pallas_call(
        body, name="attn_fwd", out_shape=(jax.ShapeDtypeStruct(q.shape, F32),) * 2, grid=(N_HEADS, T // tq),
        in_specs=[qs, kv, kv], out_specs=(qs, qs),
        compiler_params=pltpu.CompilerParams(dimension_semantics=("parallel", "parallel")),
    )(q, k, v)


def _attn_bwd(q, k, v, do, o, lse):
    T = q.shape[0]
    tb = _pick(T, 512)
    nb = T // tb

    def body(q_ref, k_ref, v_ref, do_ref, o_ref, lse_ref, dq_ref, dk_ref, dv_ref, delta_rows, lse_rows, dob_scr):
        dq_ref[...] = jnp.zeros_like(dq_ref)
        lane = lax.broadcasted_iota(jnp.int32, (8, HP), 1)
        ones8 = jnp.ones((8, HP), BF)
        first8 = jnp.where(lane == 0, 1.0, 0.0).astype(BF)

        def as_rows(pick, v):
            total, rest = None, v
            for _ in range(3):
                piece = rest.astype(BF)
                part = lax.dot_general(pick, piece, NT, preferred_element_type=F32)
                total = part if total is None else total + part
                rest = rest - piece.astype(F32)
            return total

        def per_q_tile(i, carry):
            qs = pl.ds(pl.multiple_of(i * tb, tb), tb)
            doi = do_ref[qs, :]
            delta_rows[i] = as_rows(ones8, doi * o_ref[qs, :])
            lse_rows[i] = as_rows(first8, lse_ref[qs, :])
            dob_scr[qs, :] = doi.astype(BF)
            return carry

        lax.fori_loop(0, nb, per_q_tile, 0)

        def k_loop(j, carry):
            ks = pl.ds(pl.multiple_of(j * tb, tb), tb)
            kj, vj = k_ref[ks, :], v_ref[ks, :]

            def q_loop(i, acc):
                dk_acc, dv_acc = acc
                qs = pl.ds(pl.multiple_of(i * tb, tb), tb)
                qi = q_ref[qs, :]
                dob = dob_scr[qs, :]
                s_t = lax.dot_general(kj, qi, NT, preferred_element_type=F32)
                p_t = jnp.exp2(s_t - lse_rows[i, 0:1, :])
                dp_t = lax.dot_general(vj, dob, NT, preferred_element_type=F32)
                ds_t = (p_t * (dp_t - delta_rows[i, 0:1, :])).astype(BF)
                dv_acc = dv_acc + jnp.dot(p_t.astype(BF), dob, preferred_element_type=F32)
                dk_acc = dk_acc + jnp.dot(ds_t, qi, preferred_element_type=F32)
                dq_ref[qs, :] += lax.dot_general(ds_t, kj, TN, preferred_element_type=F32)
                return dk_acc, dv_acc

            zero = jnp.zeros((tb, HP), F32)
            dk_acc, dv_acc = lax.fori_loop(0, nb, q_loop, (zero, zero))
            dk_ref[ks, :] = dk_acc
            dv_ref[ks, :] = dv_acc.astype(dv_ref.dtype)
            return carry

        lax.fori_loop(0, nb, k_loop, 0)

    spec = pl.BlockSpec((T, HP), lambda h: (0, h))
    return pl.pallas_call(
        body, name="attn_bwd",
        out_shape=(jax.ShapeDtypeStruct(q.shape, F32), jax.ShapeDtypeStruct(q.shape, F32), jax.ShapeDtypeStruct(q.shape, BF)),
        grid=(N_HEADS,), in_specs=[spec] * 6, out_specs=(spec,) * 3,
        scratch_shapes=[pltpu.VMEM((nb, 8, tb), F32), pltpu.VMEM((nb, 8, tb), F32), pltpu.VMEM((T, HP), BF)],
        compiler_params=pltpu.CompilerParams(dimension_semantics=("parallel",), vmem_limit_bytes=2 * 15 * T * HP * 2 + (8 << 20)),
    )(q, k, v, do, o, lse)


CONV_TC = 512
CONV_PAD = CONV_WIDTH // 2


def _halo_specs(tr, col_of):
    r8 = tr // 8
    cur = pl.BlockSpec((tr, CONV_TC), lambda j, i: (i, col_of(j)))
    prev = pl.BlockSpec((8, CONV_TC), lambda j, i: (jnp.maximum(i * r8 - 1, 0), col_of(j)))

    def nxt_map(j, i, n8):
        return (jnp.minimum((i + 1) * r8, n8 - 1), col_of(j))

    return cur, prev, nxt_map


def _with_halo(prev_ref, cur_ref, next_ref, i, n_i):
    prev = jnp.where(i == 0, 0.0, prev_ref[...].astype(F32))
    nxt = jnp.where(i == n_i - 1, 0.0, next_ref[...].astype(F32))
    return jnp.concatenate([prev, cur_ref[...].astype(F32), nxt], axis=0)


def _conv_fwd(u, w8, b):
    T = u.shape[0]
    tr = _pick(T, 512)
    n_i = T // tr
    c0 = U_XBC // CONV_TC
    cur, prev, nxt_map = _halo_specs(tr, lambda j: c0 + j)
    nxt = pl.BlockSpec((8, CONV_TC), functools.partial(nxt_map, n8=T // 8))

    def body(p_ref, c_ref, n_ref, w_ref, b_ref, pre_ref, act_ref):
        i = pl.program_id(1)
        full = _with_halo(p_ref, c_ref, n_ref, i, n_i)
        acc = jnp.broadcast_to(b_ref[...], (tr, CONV_TC))
        for kk in range(CONV_WIDTH):
            acc = acc + full[8 - CONV_PAD + kk:8 - CONV_PAD + kk + tr, :] * w_ref[kk:kk + 1, :]
        pre_ref[...] = acc
        act_ref[...] = _silu(acc)

    out = pl.BlockSpec((tr, CONV_TC), lambda j, i: (i, j))
    return pl.pallas_call(
        body, name="conv_fwd", out_shape=(jax.ShapeDtypeStruct((T, XBC_DIM), F32),) * 2,
        grid=(XBC_DIM // CONV_TC, n_i),
        in_specs=[prev, cur, nxt, pl.BlockSpec((8, CONV_TC), lambda j, i: (0, j)), pl.BlockSpec((1, CONV_TC), lambda j, i: (0, j))],
        out_specs=(out, out),
    )(u, u, u, w8, b)


def _conv_dpre(dacts, pre, col0, *, name):
    T, width = dacts[0].shape
    tt = _pick(T, 512)
    n_d = len(dacts)
    c0 = col0 // CONV_TC

    def body(*refs):
        d = refs[0][...]
        for r in refs[1:n_d]:
            d = d + r[...]
        refs[n_d + 1][...] = d * _dsilu(refs[n_d][...])

    blk = pl.BlockSpec((tt, CONV_TC), lambda j, i: (i, j))
    return pl.pallas_call(
        body, name=name, out_shape=jax.ShapeDtypeStruct((T, width), F32), grid=(width // CONV_TC, T // tt),
        in_specs=[blk] * n_d + [pl.BlockSpec((tt, CONV_TC), lambda j, i: (i, c0 + j))], out_specs=blk,
    )(*dacts, pre)


def _conv_bwd(dpre, u, w8, col0, *, name):
    T, width = dpre.shape
    tr = _pick(T, 512)
    n_i = T // tr
    cd = col0 // CONV_TC
    cx = (U_XBC + col0) // CONV_TC
    d_cur, d_prev, d_nxt_map = _halo_specs(tr, lambda j: j)
    x_cur, x_prev, x_nxt_map = _halo_specs(tr, lambda j: cx + j)
    d_nxt = pl.BlockSpec((8, CONV_TC), functools.partial(d_nxt_map, n8=T // 8))
    x_nxt = pl.BlockSpec((8, CONV_TC), functools.partial(x_nxt_map, n8=T // 8))

    def body(dp_ref, dc_ref, dn_ref, xp_ref, xc_ref, xn_ref, w_ref, dx_ref, dw_ref):
        i = pl.program_id(1)
        dfull = _with_halo(dp_ref, dc_ref, dn_ref, i, n_i)
        xfull = _with_halo(xp_ref, xc_ref, xn_ref, i, n_i)
        dcur = dc_ref[...]
        dx = jnp.zeros((tr, CONV_TC), F32)
        rows = []
        for kk in range(CONV_WIDTH):
            dx = dx + dfull[8 + CONV_PAD - kk:8 + CONV_PAD - kk + tr, :] * w_ref[kk:kk + 1, :]
            rows.append(jnp.sum(dcur * xfull[8 - CONV_PAD + kk:8 - CONV_PAD + kk + tr, :], axis=0, keepdims=True))
        rows.append(jnp.sum(dcur, axis=0, keepdims=True))
        rows.append(jnp.zeros((2, CONV_TC), F32))
        dx_ref[...] = dx.astype(dx_ref.dtype)

        @pl.when(i == 0)
        def _():
            dw_ref[...] = jnp.zeros_like(dw_ref)

        dw_ref[...] += jnp.concatenate(rows, axis=0)

    out = pl.BlockSpec((tr, CONV_TC), lambda j, i: (i, j))
    return pl.pallas_call(
        body, name=name, out_shape=(jax.ShapeDtypeStruct((T, width), BF), jax.ShapeDtypeStruct((8, width), F32)),
        grid=(width // CONV_TC, n_i),
        in_specs=[d_prev, d_cur, d_nxt, x_prev, x_cur, x_nxt, pl.BlockSpec((8, CONV_TC), lambda j, i: (0, cd + j))],
        out_specs=(out, pl.BlockSpec((8, CONV_TC), lambda j, i: (0, j))),
        compiler_params=pltpu.CompilerParams(dimension_semantics=("parallel", "arbitrary")),
    )(dpre, dpre, dpre, u, u, u, w8)


N_HB = 2 * SSM_GROUPS
P_DT, P_CS, P_E, P_W = 0, HP, 2 * HP, 3 * HP
DT_BLK = (U_SMALL + S_DT) // HP


def _tri(rev, transpose=False):
    rows = lax.broadcasted_iota(jnp.int32, (CHUNK, CHUNK), 0)
    cols = lax.broadcasted_iota(jnp.int32, (CHUNK, CHUNK), 1)
    if transpose:
        rows, cols = cols, rows
    return (cols >= rows) if rev else (cols <= rows)


def _ssd_prep(u, bias8, alog8):
    T = u.shape[0]
    nc = T // CHUNK

    def body(dt_ref, bias_ref, a_ref, cols_ref, rows_ref):
        lane = lax.broadcasted_iota(jnp.int32, (CHUNK, HP), 1)
        dt = _softplus(dt_ref[...] + bias_ref[0:1, :])
        da = dt * (-jnp.exp(a_ref[0:1, :]))
        cs_f = jnp.dot(jnp.where(_tri(False), 1.0, 0.0).astype(F32), da, precision=HI, preferred_element_type=F32)
        cs_b = jnp.dot(jnp.where(_tri(True), 1.0, 0.0).astype(F32), da, precision=HI, preferred_element_type=F32)
        cs = jnp.where(lane < SSM_HEADS, cs_f, cs_b)
        tot = jnp.where(lane[0:1] < SSM_HEADS, cs_f[CHUNK - 1:CHUNK, :], cs_b[0:1, :])
        e, w = jnp.exp(cs), jnp.exp(tot - cs)
        tot8 = jnp.broadcast_to(tot, (8, HP))
        etot8 = jnp.exp(tot8)
        for b in range(N_HB):
            down = (HP - HG * b) % HP

            def rolled(v):
                return pltpu.roll(v, down, 1) if down else v

            cols_ref[b, :, P_DT:P_DT + HP] = rolled(dt)
            cs_r = rolled(cs)
            cols_ref[b, :, P_CS:P_CS + HP] = cs_r
            cols_ref[b, :, P_E:P_E + HP] = rolled(e)
            cols_ref[b, :, P_W:P_W + HP] = rolled(w)
            rows_ref[b, 0, 0:8, :] = cs_r.T[0:8, :]
            r8 = lax.broadcasted_iota(jnp.int32, (8, HP), 0)
            rows_ref[b, 0, 8:16, :] = jnp.where(r8 == 0, rolled(tot8), jnp.where(r8 == 1, rolled(etot8), 0.0))

    vec = pl.BlockSpec((8, HP), lambda c: (0, 0))
    return pl.pallas_call(
        body, name="ssd_prep",
        out_shape=(jax.ShapeDtypeStruct((N_HB, T, 4 * HP), F32), jax.ShapeDtypeStruct((N_HB, nc, 16, HP), F32)),
        grid=(nc,), in_specs=[pl.BlockSpec((CHUNK, HP), lambda c: (c, DT_BLK)), vec, vec],
        out_specs=(pl.BlockSpec((N_HB, CHUNK, 4 * HP), lambda c: (0, c, 0)), pl.BlockSpec((N_HB, 1, 16, HP), lambda c: (0, c, 0, 0))),
    )(u, bias8, alog8)


def _ssd_specs(T, rev, bwd):
    nc = T // CHUNK
    fwd_order = (lambda c: nc - 1 - c) if rev else (lambda c: c)
    cm = (lambda c: fwd_order(nc - 1 - c)) if bwd else fwd_order
    hb0 = SSM_GROUPS if rev else 0
    xs = pl.BlockSpec((CHUNK, GW), lambda c, g: (cm(c), g))
    bs = pl.BlockSpec((CHUNK, D_STATE), lambda c, g: (cm(c), D_INNER // D_STATE + g))
    cs = pl.BlockSpec((CHUNK, D_STATE), lambda c, g: (cm(c), (D_INNER + SSM_GROUPS * D_STATE) // D_STATE + g))
    cols = pl.BlockSpec((1, CHUNK, 4 * HP), lambda c, g: (hb0 + g, cm(c), 0))
    rows = pl.BlockSpec((1, 1, 16, HP), lambda c, g: (hb0 + g, cm(c), 0, 0))
    return nc, cm, xs, bs, cs, cols, rows


def _head_lanes(to_heads):
    shape = (GW, HP) if to_heads else (HP, GW)
    wide = lax.broadcasted_iota(jnp.int32, shape, 0 if to_heads else 1)
    head = lax.broadcasted_iota(jnp.int32, shape, 1 if to_heads else 0)
    return jnp.where((wide >= PH * head) & (wide < PH * (head + 1)), 1.0, 0.0).astype(BF)


def _split_dot(v, m, terms):
    total, rest = None, v
    for _ in range(terms):
        piece = rest.astype(BF)
        part = jnp.dot(piece, m, preferred_element_type=F32)
        total = part if total is None else total + part
        rest = rest - piece.astype(F32)
    return total


def _spread_cols(cols_ref, rows_ref):
    spread = _head_lanes(False)
    dt_e = _split_dot(cols_ref[0, :, P_DT:P_DT + HP], spread, 3)
    e_e = _split_dot(cols_ref[0, :, P_E:P_E + HP], spread, 2)
    w_e = _split_dot(cols_ref[0, :, P_W:P_W + HP], spread, 2)
    etot_e = _split_dot(rows_ref[0, 0, 8:16, :], spread, 3)[1:2, :]
    return dt_e, e_e, w_e, etot_e


def _decay(cols_ref, rows_ref, hh, incl, transpose=False):
    col = cols_ref[0, :, P_CS + hh:P_CS + hh + 1]
    row = rows_ref[0, 0, hh:hh + 1, :]
    return jnp.where(incl, jnp.exp(row - col if transpose else col - row), 0.0)


def _ssd_fwd(act, cols, rows, *, rev, name):
    T = act.shape[0]
    nc, cm, xs_s, b_s, c_s, cols_s, rows_s = _ssd_specs(T, rev, False)

    def body(x_ref, b_ref, c_ref, cols_ref, rows_ref, y_ref, st_ref, state):
        c, g = pl.program_id(0), pl.program_id(1)

        @pl.when(c == 0)
        def _():
            state[g] = jnp.zeros((D_STATE, GW), F32)

        incl = _tri(rev)
        bm, cmat = b_ref[...].astype(BF), c_ref[...].astype(BF)
        bm_t = b_ref[...].T.astype(BF)
        cb = lax.dot_general(cmat, bm, NT, preferred_element_type=F32)
        dt_e, e_e, w_e, etot_e = _spread_cols(cols_ref, rows_ref)
        prev_all = state[g]
        st_ref[...] = prev_all
        xdt = x_ref[...] * dt_e
        xdt_b = xdt.astype(BF)
        yo_all = jnp.dot(cmat, prev_all.astype(BF), preferred_element_type=F32) * e_e
        state[g] = prev_all * etot_e + jnp.dot(bm_t, (xdt * w_e).astype(BF), preferred_element_type=F32)
        for hh in range(HG):
            hs = slice(PH * hh, PH * (hh + 1))
            lmat = _decay(cols_ref, rows_ref, hh, incl)
            yd = jnp.dot((cb * lmat).astype(BF), xdt_b[:, hs], preferred_element_type=F32)
            y_ref[:, hs] = yd + yo_all[:, hs]

    return pl.pallas_call(
        body, name=name,
        out_shape=(jax.ShapeDtypeStruct((T, D_INNER), F32), jax.ShapeDtypeStruct((nc * D_STATE, D_INNER), F32)),
        grid=(nc, SSM_GROUPS), in_specs=[xs_s, b_s, c_s, cols_s, rows_s], out_specs=(xs_s, xs_s),
        scratch_shapes=[pltpu.VMEM((SSM_GROUPS, D_STATE, GW), F32)],
        compiler_params=pltpu.CompilerParams(dimension_semantics=("arbitrary", "arbitrary")),
    )(act, act, act, cols, rows)


def _ssd_bwd(act, cols, rows, states, dy, *, rev, name):
    T = act.shape[0]
    nc, cm, xs_s, b_s, c_s, cols_s, rows_s = _ssd_specs(T, rev, True)

    def body(x_ref, b_ref, c_ref, cols_ref, rows_ref, st_ref, dy_ref, dx_ref, db_ref, dc_ref, dsel_ref, dtot_ref,
             dstate, dcs_cols, dcs_rows, dcb, dm_scr, dxdt_scr):
        c, g = pl.program_id(0), pl.program_id(1)

        @pl.when(c == 0)
        def _():
            dstate[g] = jnp.zeros((D_STATE, GW), F32)

        incl, incl_t = _tri(rev), _tri(rev, transpose=True)
        bm, cmat = b_ref[...].astype(BF), c_ref[...].astype(BF)
        cm_t = c_ref[...].T.astype(BF)
        cb = lax.dot_general(cmat, bm, NT, preferred_element_type=F32)
        cb_t = lax.dot_general(bm, cmat, NT, preferred_element_type=F32)
        prev_all, ds_all = st_ref[...], dstate[g]
        pb_all, dsb_all = prev_all.astype(BF), ds_all.astype(BF)
        cp_all = jnp.dot(cmat, pb_all, preferred_element_type=F32)
        bds_all = jnp.dot(bm, dsb_all, preferred_element_type=F32)
        dt_e, e_e, w_e, etot_e = _spread_cols(cols_ref, rows_ref)
        to_heads = _head_lanes(True)
        x, dy = x_ref[...], dy_ref[...]
        xdt = x * dt_e
        xdt_b, dy_b = xdt.astype(BF), dy.astype(BF)
        dye_b, xdw_b = (dy * e_e).astype(BF), (xdt * w_e).astype(BF)
        for hh in range(HG):
            hs = slice(PH * hh, PH * (hh + 1))
            mmat_t = cb_t * _decay(cols_ref, rows_ref, hh, incl_t, transpose=True)
            dm_scr[hh] = lax.dot_general(dy_b[:, hs], xdt_b[:, hs], NT, preferred_element_type=F32)
            dxdt_scr[:, hs] = jnp.dot(mmat_t.astype(BF), dy_b[:, hs], preferred_element_type=F32)
        bdsw = bds_all * w_e
        dxdt = dxdt_scr[...] + bdsw
        dx_ref[...] = dxdt * dt_e
        t = _split_dot(xdt * bdsw, to_heads, 2)
        dcs_state = _split_dot(dy * cp_all, to_heads, 2) * cols_ref[0, :, P_E:P_E + HP] - t
        dsel_ref[0, :, 0:HP] = _split_dot(dxdt * x, to_heads, 2)
        sp = _split_dot(jnp.broadcast_to(jnp.sum(ds_all * prev_all, axis=0, keepdims=True), (8, GW)), to_heads, 2)
        dtot_ref[0, 0] = jnp.sum(t, axis=0, keepdims=True) + sp * rows_ref[0, 0, 9:10, :]
        dstate[g] = ds_all * etot_e + jnp.dot(cm_t, dye_b, preferred_element_type=F32)
        dcs_cols[...] = jnp.zeros_like(dcs_cols)
        dcs_rows[...] = jnp.zeros_like(dcs_rows)
        dcb[...] = jnp.zeros_like(dcb)
        for hh in range(HG):
            lmat = _decay(cols_ref, rows_ref, hh, incl)
            dm = dm_scr[hh]
            qm = dm * (cb * lmat)
            dcs_cols[:, hh:hh + 1] = jnp.sum(qm, axis=1, keepdims=True)
            dcs_rows[hh:hh + 1, :] = jnp.sum(qm, axis=0, keepdims=True)
            dcb[...] += dm * lmat
        dcb_all = dcb[...]
        dsel_ref[0, :, HP:2 * HP] = dcs_state + dcs_cols[...] - dcs_rows[...].T
        dc_ref[...] = (lax.dot_general(dye_b, pb_all, NT, preferred_element_type=F32)
                       + jnp.dot(dcb_all.astype(BF), bm, preferred_element_type=F32))
        db_ref[...] = (lax.dot_general(xdw_b, dsb_all, NT, preferred_element_type=F32)
                       + jnp.dot(dcb_all.T.astype(BF), cmat, preferred_element_type=F32))

    bc_out = pl.BlockSpec((CHUNK, D_STATE), lambda c, g: (cm(c), g))
    return pl.pallas_call(
        body, name=name,
        out_shape=(jax.ShapeDtypeStruct((T, D_INNER), F32), jax.ShapeDtypeStruct((T, SSM_GROUPS * D_STATE), F32),
                   jax.ShapeDtypeStruct((T, SSM_GROUPS * D_STATE), F32), jax.ShapeDtypeStruct((SSM_GROUPS, T, 2 * HP), F32),
                   jax.ShapeDtypeStruct((SSM_GROUPS, nc, 8, HP), F32)),
        grid=(nc, SSM_GROUPS), in_specs=[xs_s, b_s, c_s, cols_s, rows_s, xs_s, xs_s],
        out_specs=(xs_s, bc_out, bc_out, pl.BlockSpec((1, CHUNK, 2 * HP), lambda c, g: (g, cm(c), 0)),
                   pl.BlockSpec((1, 1, 8, HP), lambda c, g: (g, cm(c), 0, 0))),
        scratch_shapes=[pltpu.VMEM((SSM_GROUPS, D_STATE, GW), F32), pltpu.VMEM((CHUNK, CHUNK), F32),
                        pltpu.VMEM((CHUNK, CHUNK), F32), pltpu.VMEM((CHUNK, CHUNK), F32),
                        pltpu.VMEM((HG, CHUNK, CHUNK), F32), pltpu.VMEM((CHUNK, GW), F32)],
        compiler_params=pltpu.CompilerParams(dimension_semantics=("arbitrary", "arbitrary")),
    )(act, act, act, cols, rows, states, dy)


def _ssd_prep_bwd(u, bias8, alog8, dsel_f, dtot_f, dsel_b, dtot_b):
    T = u.shape[0]
    nc = T // CHUNK

    def body(dt_ref, bias_ref, a_ref, sf_ref, tf_ref, sb_ref, tb_ref, ddt_ref, da_ref, dbias_ref):
        @pl.when(pl.program_id(0) == 0)
        def _():
            da_ref[...] = jnp.zeros_like(da_ref)
            dbias_ref[...] = jnp.zeros_like(dbias_ref)

        lane = lax.broadcasted_iota(jnp.int32, (CHUNK, HP), 1)
        pre = dt_ref[...] + bias_ref[0:1, :]
        dt = _softplus(pre)
        a = -jnp.exp(a_ref[0:1, :])
        ddt_x, dcs, dtot = jnp.zeros((CHUNK, HP), F32), jnp.zeros((CHUNK, HP), F32), jnp.zeros((8, HP), F32)
        for b in range(N_HB):
            s_ref, t_ref, g = (sf_ref, tf_ref, b) if b < SSM_GROUPS else (sb_ref, tb_ref, b - SSM_GROUPS)
            mine = (lane >= HG * b) & (lane < HG * (b + 1))

            def up(v):
                return pltpu.roll(v, HG * b, 1) if b else v

            ddt_x = ddt_x + jnp.where(mine, up(s_ref[g, :, 0:HP]), 0.0)
            dcs = dcs + jnp.where(mine, up(s_ref[g, :, HP:2 * HP]), 0.0)
            dtot = dtot + jnp.where(mine[0:8], up(t_ref[g, 0]), 0.0)
        tri_f = jnp.where(_tri(False, transpose=True), 1.0, 0.0).astype(F32)
        tri_b = jnp.where(_tri(True, transpose=True), 1.0, 0.0).astype(F32)
        dda = jnp.where(lane < SSM_HEADS, jnp.dot(tri_f, dcs, precision=HI, preferred_element_type=F32),
                        jnp.dot(tri_b, dcs, precision=HI, preferred_element_type=F32)) + dtot[0:1, :]
        dpre = (ddt_x + dda * a) * jax.nn.sigmoid(pre)
        ddt_ref[...] = jnp.where(lane < 2 * SSM_HEADS, dpre, 0.0)
        dbias_ref[...] += jnp.broadcast_to(jnp.sum(dpre, axis=0, keepdims=True), (8, HP))
        da_ref[...] += jnp.broadcast_to(jnp.sum(dda * dt, axis=0, keepdims=True) * a, (8, HP))

    vec = pl.BlockSpec((8, HP), lambda c: (0, 0))
    sel = pl.BlockSpec((SSM_GROUPS, CHUNK, 2 * HP), lambda c: (0, c, 0))
    tot = pl.BlockSpec((SSM_GROUPS, 1, 8, HP), lambda c: (0, c, 0, 0))
    tile = pl.BlockSpec((CHUNK, HP), lambda c: (c, 0))
    return pl.pallas_call(
        body, name="ssd_prep_bwd",
        out_shape=(jax.ShapeDtypeStruct((T, HP), F32), jax.ShapeDtypeStruct((8, HP), F32), jax.ShapeDtypeStruct((8, HP), F32)),
        grid=(nc,), in_specs=[pl.BlockSpec((CHUNK, HP), lambda c: (c, DT_BLK)), vec, vec, sel, tot, sel, tot],
        out_specs=(tile, vec, vec),
        compiler_params=pltpu.CompilerParams(dimension_semantics=("arbitrary",)),
    )(u, bias8, alog8, dsel_f, dtot_f, dsel_b, dtot_b)


def _ssm_combine_fwd(y_f, y_b, act, u, dskip, gain):
    T = y_f.shape[0]
    tt = _pick(T, 512)

    def body(yf_ref, yb_ref, x_ref, z_ref, ds_ref, g_ref, y_ref, m_ref):
        y = yf_ref[...] + yb_ref[...] + ds_ref[...] * x_ref[...]
        y2 = y * _silu(z_ref[...])
        r = lax.rsqrt(jnp.mean(y2 * y2, axis=-1, keepdims=True) + EPS)
        y_ref[...] = y
        m_ref[...] = (y2 * r * g_ref[...]).astype(m_ref.dtype)

    blk = pl.BlockSpec((tt, GW), lambda i, g: (i, g))
    vec = pl.BlockSpec((1, GW), lambda i, g: (0, g))
    return pl.pallas_call(
        body, name="ssm_combine_fwd",
        out_shape=(jax.ShapeDtypeStruct((T, D_INNER), F32), jax.ShapeDtypeStruct((T, D_INNER), BF)),
        grid=(T // tt, SSM_GROUPS), in_specs=[blk, blk, blk, blk, vec, vec], out_specs=(blk, blk),
    )(y_f, y_b, act, u, dskip, gain)


def _ssm_combine_bwd(dm, y, act, u, dskip, gain):
    T = y.shape[0]
    tt = _pick(T, 512)

    def body(dm_ref, y_ref, x_ref, z_ref, ds_ref, g_ref, dy_ref, dz_ref, dxs_ref, dg_ref, dsk_ref):
        z = z_ref[...]
        y = y_ref[...]
        x = x_ref[...]
        sz = _silu(z)
        y2 = y * sz
        r = lax.rsqrt(jnp.mean(y2 * y2, axis=-1, keepdims=True) + EPS)
        d = dm_ref[...]
        gd = d * g_ref[...]
        dy2 = r * gd - y2 * (r * r * r) * jnp.mean(gd * y2, axis=-1, keepdims=True)
        dy = dy2 * sz
        dy_ref[...] = dy
        dz_ref[...] = (dy2 * y * _dsilu(z)).astype(dz_ref.dtype)
        dxs_ref[...] = dy * ds_ref[...]

        @pl.when(pl.program_id(1) == 0)
        def _():
            dg_ref[...] = jnp.zeros_like(dg_ref)
            dsk_ref[...] = jnp.zeros_like(dsk_ref)

        dg_ref[...] += jnp.broadcast_to(jnp.sum(d * y2 * r, axis=0, keepdims=True), dg_ref.shape)
        lane_sum = jnp.broadcast_to(jnp.sum(dy * x, axis=0, keepdims=True), (8, GW))
        src = lax.broadcasted_iota(jnp.int32, (GW, HP), 0)
        head = lax.broadcasted_iota(jnp.int32, (GW, HP), 1)
        to_head = jnp.where((src >= PH * head) & (src < PH * (head + 1)), 1.0, 0.0).astype(F32)
        dsk_ref[...] += jnp.dot(lane_sum, to_head, precision=HI, preferred_element_type=F32)

    blk = pl.BlockSpec((tt, GW), lambda g, i: (i, g))
    vec = pl.BlockSpec((1, GW), lambda g, i: (0, g))
    acc = pl.BlockSpec((8, GW), lambda g, i: (0, g))
    return pl.pallas_call(
        body, name="ssm_combine_bwd",
        out_shape=(jax.ShapeDtypeStruct((T, D_INNER), F32), jax.ShapeDtypeStruct((T, D_INNER), BF),
                   jax.ShapeDtypeStruct((T, D_INNER), F32), jax.ShapeDtypeStruct((8, D_INNER), F32),
                   jax.ShapeDtypeStruct((8, SSM_GROUPS * HP), F32)),
        grid=(SSM_GROUPS, T // tt), in_specs=[blk, blk, blk, blk, vec, vec],
        out_specs=(blk, blk, blk, acc, pl.BlockSpec((8, HP), lambda g, i: (0, g))),
        compiler_params=pltpu.CompilerParams(dimension_semantics=("parallel", "arbitrary")),
    )(dm, y, act, u, dskip, gain)


def _loss_head(y, target):
    T, D = y.shape
    tt = _pick(T, 512)

    def body(y_ref, t_ref, dy_ref, dyb_ref, l_ref):
        e = y_ref[...] - t_ref[...]
        dy_ref[...] = e * (1.0 / D)
        dyb_ref[...] = (e * (1.0 / D)).astype(dyb_ref.dtype)

        @pl.when(pl.program_id(0) == 0)
        def _():
            l_ref[...] = jnp.zeros_like(l_ref)

        l_ref[...] += jnp.sum(e * e) * (0.5 / D)

    blk = pl.BlockSpec((tt, D), lambda i: (i, 0))
    return pl.pallas_call(
        body, name="loss_head",
        out_shape=(jax.ShapeDtypeStruct((T, D), F32), jax.ShapeDtypeStruct((T, D), BF), jax.ShapeDtypeStruct((8, 128), F32)),
        grid=(T // tt,), in_specs=[blk, blk], out_specs=(blk, blk, pl.BlockSpec((8, 128), lambda i: (0, 0))),
        compiler_params=pltpu.CompilerParams(dimension_semantics=("arbitrary",)),
    )(y, target)


def _adamw(w, g, m, v, *, name):
    R, C = w.shape
    cap = max(8, (1 << 18) // C)
    tr = R
    if R % 8 == 0:
        tr = 8
        for cand in range(8, min(R, cap) + 1, 8):
            if R % cand == 0:
                tr = cand

    def body(w_ref, g_ref, m_ref, v_ref, d_ref, nm_ref, nv_ref):
        gg = g_ref[...]
        nm = ADAM_B1 * m_ref[...] + (1.0 - ADAM_B1) * gg
        nv = ADAM_B2 * v_ref[...] + (1.0 - ADAM_B2) * jnp.square(gg)
        m_hat = nm / (1.0 - ADAM_B1 ** ADAM_STEP)
        v_hat = nv / (1.0 - ADAM_B2 ** ADAM_STEP)
        d_ref[...] = -ADAM_LR * (m_hat / (jnp.sqrt(v_hat) + ADAM_EPS) + ADAM_WD * w_ref[...])
        nm_ref[...] = nm
        nv_ref[...] = nv

    blk = pl.BlockSpec((tr, C), lambda i: (i, 0))
    return pl.pallas_call(
        body, name=name, out_shape=(jax.ShapeDtypeStruct((R, C), F32),) * 3, grid=(R // tr,),
        in_specs=[blk] * 4, out_specs=(blk,) * 3,
    )(w, g, m, v)


ANY = pl.BlockSpec(memory_space=pl.ANY)


def _chip_peers():
    x, y, c = lax.axis_index("x"), lax.axis_index("y"), lax.axis_index("c")
    return x, y, c, [(1 - x, y), (x, 1 - y), (1 - x, 1 - y)]


def _half_rows(c, rh):
    return pl.ds(pl.multiple_of(c * rh, 16), rh)


def _my_chip():
    return 2 * lax.axis_index("x") + lax.axis_index("y")


def _gather_chips(wb, wf):
    rh = wb.shape[0] // 2
    rq = rh // 2

    def body(wb_ref, wf_ref, ob_ref, of_ref, send_sems, recv_sems):
        x, y, c, peers = _chip_peers()
        nbr_x, nbr_y = peers[0], peers[1]
        me, chip_x, chip_y, chip_d = 2 * x + y, 2 * (1 - x) + y, 2 * x + (1 - y), 2 * (1 - x) + (1 - y)

        def quarter(core, b):
            return pl.ds(pl.multiple_of(core * rh + b * rq, 16), rq)

        ici = [(0, nbr_x, me, 0, chip_x), (1, nbr_y, me, 1, chip_y), (2, nbr_y, me, 0, chip_y), (3, nbr_x, me, 1, chip_x),
               (4, nbr_y, chip_x, 0, chip_d), (5, nbr_x, chip_y, 1, chip_d)]

        def ici_copy(k, to, slot, b, own):
            rows = quarter(c, b)
            return pltpu.make_async_remote_copy(
                src_ref=wb_ref.at[rows] if own else ob_ref.at[slot, rows], dst_ref=ob_ref.at[slot, rows],
                send_sem=send_sems.at[k], recv_sem=recv_sems.at[k], device_id=(to[0], to[1], c), device_id_type=MESH)

        def to_sibling(k, slot, b, core):
            rows = quarter(core, b)
            return pltpu.make_async_remote_copy(
                src_ref=ob_ref.at[slot, rows], dst_ref=ob_ref.at[slot, rows], send_sem=send_sems.at[6 + k],
                recv_sem=recv_sems.at[6 + k], device_id=(x, y, 1 - c), device_id_type=MESH)

        def small_copy(k, slot):
            px, py = peers[k]
            return pltpu.make_async_remote_copy(
                src_ref=wf_ref, dst_ref=of_ref.at[slot], send_sem=send_sems.at[12 + k], recv_sem=recv_sems.at[12 + k],
                device_id=(px, py, c), device_id_type=MESH)

        sends = [ici_copy(k, to, slot, b, True) for k, to, slot, b, _ in ici[:4]] + [small_copy(k, me) for k in range(3)]
        for cp in sends:
            cp.start()
        for k, to, slot, b, arrives in ici:
            ici_copy(k, to, arrives, b, False).wait_recv()
            passed = [to_sibling(k, arrives, b, c)]
            if k < 2:
                passed.append(ici_copy(*ici[4 + k][:4], False))
            for cp in passed:
                cp.start()
            sends += passed
        for k, to, slot, b, arrives in ici:
            to_sibling(k, arrives, b, 1 - c).wait_recv()
        chip_of = [chip_x, chip_y, chip_d]
        for k in range(3):
            small_copy(k, chip_of[k]).wait_recv()
        for cp in sends:
            cp.wait_send()

    ob, of = pl.pallas_call(
        body, name="gather_weights",
        out_shape=(jax.ShapeDtypeStruct((4,) + wb.shape, wb.dtype), jax.ShapeDtypeStruct((4,) + wf.shape, wf.dtype)),
        in_specs=[ANY, ANY], out_specs=(ANY, ANY),
        scratch_shapes=[pltpu.SemaphoreType.DMA((15,)), pltpu.SemaphoreType.DMA((15,))],
    )(wb, wf)
    me = _my_chip()
    return lax.dynamic_update_slice(ob, wb[None], (me, 0, 0)), lax.dynamic_update_slice(of, wf[None], (me, 0, 0))


def _halves_to_sibling(gp):
    rh = gp.shape[1] // 2

    def body(gp_ref, o_ref, send_sem, recv_sem):
        x, y, c = lax.axis_index("x"), lax.axis_index("y"), lax.axis_index("c")
        cp = pltpu.make_async_remote_copy(src_ref=gp_ref.at[:, _half_rows(1 - c, rh), :], dst_ref=o_ref, send_sem=send_sem,
                                          recv_sem=recv_sem, device_id=(x, y, 1 - c), device_id_type=MESH)
        cp.start()
        cp.wait()

    return pl.pallas_call(
        body, name="halves_to_sibling", out_shape=jax.ShapeDtypeStruct((gp.shape[0], rh, gp.shape[2]), gp.dtype),
        in_specs=[ANY], out_specs=ANY, scratch_shapes=[pltpu.SemaphoreType.DMA, pltpu.SemaphoreType.DMA],
    )(gp)


def _row_tile(rows, cap=1024):
    tr = 16
    for cand in range(16, cap + 1, 16):
        if rows % cand == 0:
            tr = cand
    return tr


def _add_halves(gp, sib, core):
    n, rh, C = sib.shape
    tr = _row_tile(rh)
    nt = rh // tr

    def body(c_ref, g_ref, s_ref, o_ref):
        o_ref[...] = (g_ref[...].astype(F32) + s_ref[...].astype(F32)).astype(o_ref.dtype)

    blk = pl.BlockSpec((1, tr, C), lambda j, i, c: (j, i, 0))
    return pl.pallas_call(
        body, name="add_halves", out_shape=jax.ShapeDtypeStruct(sib.shape, sib.dtype),
        grid_spec=pltpu.PrefetchScalarGridSpec(
            num_scalar_prefetch=1, grid=(n, nt),
            in_specs=[pl.BlockSpec((1, tr, C), lambda j, i, c: (j, c[0] * nt + i, 0)), blk], out_specs=blk),
    )(core, gp, sib)


def _join_halves(buf):
    rh = buf.shape[0] // 2

    def body(in_ref, o_ref, send_sem, recv_sem):
        x, y, c = lax.axis_index("x"), lax.axis_index("y"), lax.axis_index("c")

        def copy(rows):
            return pltpu.make_async_remote_copy(src_ref=o_ref.at[rows], dst_ref=o_ref.at[rows], send_sem=send_sem,
                                                recv_sem=recv_sem, device_id=(x, y, 1 - c), device_id_type=MESH)

        send = copy(_half_rows(c, rh))
        send.start()
        copy(_half_rows(1 - c, rh)).wait_recv()
        send.wait_send()

    return pl.pallas_call(
        body, name="join_halves", out_shape=jax.ShapeDtypeStruct(buf.shape, buf.dtype),
        in_specs=[ANY], out_specs=ANY, input_output_aliases={0: 0},
        scratch_shapes=[pltpu.SemaphoreType.DMA, pltpu.SemaphoreType.DMA],
    )(buf)


def _exchange_near(gp):
    rq = gp.shape[1] // 2

    def body(gp_ref, out_ref, send_sems, recv_sems):
        x, y, c, peers = _chip_peers()
        chip_x, chip_y, chip_d = 2 * (1 - x) + y, 2 * x + (1 - y), 2 * (1 - x) + (1 - y)
        plan = [(peers[0], chip_x, 0), (peers[0], chip_d, 0), (peers[1], chip_y, 1), (peers[1], chip_d, 1)]
        copies = [pltpu.make_async_remote_copy(
            src_ref=gp_ref.at[slot, pl.ds(b * rq, rq)], dst_ref=out_ref.at[k], send_sem=send_sems.at[k],
            recv_sem=recv_sems.at[k], device_id=(to[0], to[1], c), device_id_type=MESH) for k, (to, slot, b) in enumerate(plan)]
        for cp in copies:
            cp.start()
        for cp in copies:
            cp.wait_recv()
        for cp in copies:
            cp.wait_send()

    return pl.pallas_call(
        body, name="exchange_grads_near", out_shape=jax.ShapeDtypeStruct((4, rq, gp.shape[2]), gp.dtype),
        in_specs=[ANY], out_specs=ANY, scratch_shapes=[pltpu.SemaphoreType.DMA((4,)), pltpu.SemaphoreType.DMA((4,))],
    )(gp)


def _add_near(gp, near, chips):
    _, rq, C = near.shape
    tr = _row_tile(rq)
    nt = rq // tr

    def body(ch_ref, mine_a, mine_b, on_a, on_b, near_ref, part_ref, on_ref):
        part_ref[0] = mine_a[0].astype(F32) + near_ref[0].astype(F32)
        part_ref[1] = mine_b[0].astype(F32) + near_ref[2].astype(F32)
        on_ref[0] = (on_a[0].astype(F32) + near_ref[1].astype(F32)).astype(on_ref.dtype)
        on_ref[1] = (on_b[0].astype(F32) + near_ref[3].astype(F32)).astype(on_ref.dtype)

    def slot(which, b):
        return pl.BlockSpec((1, tr, C), lambda i, ch: (ch[which], b * nt + i, 0))

    return pl.pallas_call(
        body, name="add_near",
        out_shape=(jax.ShapeDtypeStruct((2, rq, C), F32), jax.ShapeDtypeStruct((2, rq, C), near.dtype)),
        grid_spec=pltpu.PrefetchScalarGridSpec(
            num_scalar_prefetch=1, grid=(nt,),
            in_specs=[slot(0, 0), slot(0, 1), slot(2, 0), slot(1, 1), pl.BlockSpec((4, tr, C), lambda i, ch: (0, i, 0))],
            out_specs=(pl.BlockSpec((2, tr, C), lambda i, ch: (0, i, 0)),) * 2),
    )(chips, gp, gp, gp, gp, near)


def _exchange_far(on):
    def body(on_ref, out_ref, send_sems, recv_sems):
        x, y, c, peers = _chip_peers()
        copies = [pltpu.make_async_remote_copy(
            src_ref=on_ref.at[k], dst_ref=out_ref.at[k], send_sem=send_sems.at[k], recv_sem=recv_sems.at[k],
            device_id=(to[0], to[1], c), device_id_type=MESH) for k, to in enumerate((peers[1], peers[0]))]
        for cp in copies:
            cp.start()
        for cp in copies:
            cp.wait_recv()
        for cp in copies:
            cp.wait_send()

    return pl.pallas_call(
        body, name="exchange_grads_far", out_shape=jax.ShapeDtypeStruct(on.shape, on.dtype),
        in_specs=[ANY], out_specs=ANY, scratch_shapes=[pltpu.SemaphoreType.DMA((2,)), pltpu.SemaphoreType.DMA((2,))],
    )(on)


def _add_far(part, far, core):
    _, rq, C = part.shape
    tr = _row_tile(rq)
    nt = rq // tr

    def body(c_ref, p_ref, f_ref, o_ref):
        o_ref[...] = p_ref[0] + f_ref[0].astype(F32)

    blk = pl.BlockSpec((1, tr, C), lambda b, i, c: (b, i, 0))
    return pl.pallas_call(
        body, name="add_far", out_shape=jax.ShapeDtypeStruct((4 * rq, C), F32),
        grid_spec=pltpu.PrefetchScalarGridSpec(
            num_scalar_prefetch=1, grid=(2, nt), in_specs=[blk, blk],
            out_specs=pl.BlockSpec((tr, C), lambda b, i, c: ((2 * c[0] + b) * nt + i, 0))),
    )(core, part, far)


N_DEV = 8


def _allreduce_small(p):
    rs = p.shape[0]

    def body(x_ref, sum_ref, all_ref, send_sems, recv_sems, local_sem):
        x, y, c = lax.axis_index("x"), lax.axis_index("y"), lax.axis_index("c")
        me, sibling = (x, y, c), (x, y, 1 - c)
        chips = [(1 - x, y), (x, 1 - y), (1 - x, 1 - y)]

        def rows(px, py, pc):
            return all_ref.at[pl.ds((4 * px + 2 * py + pc) * rs, rs), :]

        def copy(k, block, to, src=None):
            return pltpu.make_async_remote_copy(
                src_ref=rows(*block) if src is None else src, dst_ref=rows(*block),
                send_sem=send_sems.at[k], recv_sem=recv_sems.at[k], device_id=to, device_id_type=MESH)

        mine = pltpu.make_async_copy(x_ref, rows(*me), local_sem)
        mine.start()
        first = [copy(0, me, sibling, src=x_ref)]
        first += [copy(1 + j, me, (*chip, c), src=x_ref) for j, chip in enumerate(chips)]
        for cp in first:
            cp.start()
        passed = [copy(4 + j, (*chip, c), sibling) for j, chip in enumerate(chips)]
        for j, chip in enumerate(chips):
            copy(1 + j, (*chip, c), me).wait_recv()
            passed[j].start()
        copy(0, sibling, me).wait_recv()
        for j, chip in enumerate(chips):
            copy(4 + j, (*chip, 1 - c), me).wait_recv()
        for cp in first + passed:
            cp.wait_send()
        mine.wait()
        acc = all_ref[0:rs, :]
        for d in range(1, N_DEV):
            acc = acc + all_ref[d * rs:(d + 1) * rs, :]
        sum_ref[...] = acc

    vmem = pl.BlockSpec(memory_space=pltpu.VMEM)
    return pl.pallas_call(
        body, name="allreduce_small", out_shape=jax.ShapeDtypeStruct((rs, 128), F32),
        in_specs=[vmem], out_specs=vmem,
        scratch_shapes=[pltpu.VMEM((N_DEV * rs, 128), F32), pltpu.SemaphoreType.DMA((7,)), pltpu.SemaphoreType.DMA((7,)),
                        pltpu.SemaphoreType.DMA],
    )(p)


WEIGHTS = ('ffn1_norm', 'ffn1_w_gate', 'ffn1_w_up', 'ffn1_w_down', 'mix_norm', 'w_in', 'q_a_norm', 'w_q_b',
           'kv_a_norm', 'w_kv_b', 'q_head_norm', 'k_head_norm', 'conv_w', 'conv_b', 'a_log_fwd', 'a_log_bwd',
           'dt_bias_fwd', 'dt_bias_bwd', 'd_skip', 'ssm_norm', 'w_attn_branch', 'w_ssm_branch', 'w_out',
           'ffn2_norm', 'ffn2_w_gate', 'ffn2_w_up', 'ffn2_w_down')
PACKED = (('ffn1_w_gate', (D_MODEL, D_FF), 1), ('ffn1_w_up', (D_MODEL, D_FF), 1), ('ffn1_w_down', (D_FF, D_MODEL), 0),
          ('w_in', (D_MODEL, sum(IN_SPLITS)), 1), ('w_q_b', (Q_LORA, N_HEADS * QK_HEAD), 1),
          ('w_kv_b', (KV_LORA, N_HEADS * (QK_NOPE + V_HEAD)), 1),
          ('w_attn_branch', (N_HEADS * V_HEAD, D_MODEL), 0), ('w_ssm_branch', (D_INNER, D_MODEL), 0),
          ('w_out', (D_MODEL, D_MODEL), 0),
          ('ffn2_w_gate', (D_MODEL, D_FF), 1), ('ffn2_w_up', (D_MODEL, D_FF), 1), ('ffn2_w_down', (D_FF, D_MODEL), 0))
PACK_W = 1024
N_CHIPS = 4
SMALL = (('ffn1_norm', 1024), ('mix_norm', 1024), ('q_a_norm', 384), ('kv_a_norm', 256), ('q_head_norm', 96),
         ('k_head_norm', 96), ('conv_b', 3072), ('a_log_fwd', 32), ('a_log_bwd', 32), ('dt_bias_fwd', 32),
         ('dt_bias_bwd', 32), ('d_skip', 32), ('ssm_norm', 2048), ('ffn2_norm', 1024),
         ('conv_w', CONV_WIDTH * XBC_DIM), ('loss', 1))


TRANSPOSED = ('ffn1_w_gate', 'ffn1_w_up', 'w_in', 'ffn2_w_gate', 'ffn2_w_up')


def _stored(name, a):
    return a.T if name in TRANSPOSED else a


def _shard_shape(name, shape, axis):
    sh = tuple(s // N_CHIPS if a == axis else s for a, s in enumerate(shape))
    return sh[::-1] if name in TRANSPOSED else sh


def _by_rows(name, axis):
    return name in TRANSPOSED or axis == 0


def _pack_layout():
    out, r = {}, 0
    for name, shape, axis in PACKED:
        n = math.prod(shape) // N_CHIPS // PACK_W
        out[name] = (r, n)
        r += n
    return out, -(-r // 64) * 64


def _pack(shards):
    layout, rows = _pack_layout()
    parts = [shards[name].reshape(-1, PACK_W) for name, _, _ in PACKED]
    parts.append(jnp.zeros((rows - sum(p.shape[0] for p in parts), PACK_W), parts[0].dtype))
    return jnp.concatenate(parts, axis=0)


def _unpack(packed):
    layout, _ = _pack_layout()
    return {name: packed[layout[name][0]:layout[name][0] + layout[name][1]].reshape(_shard_shape(name, shape, axis))
            for name, shape, axis in PACKED}


def _full_from_slots(slots):
    layout, _ = _pack_layout()
    out = {}
    for name, shape, axis in PACKED:
        r, n = layout[name]
        if _by_rows(name, axis):
            out[name] = slots[:, r:r + n].reshape(N_CHIPS * n, PACK_W)
        else:
            sh = _shard_shape(name, shape, axis)
            out[name] = jnp.concatenate([slots[j, r:r + n].reshape(sh) for j in range(N_CHIPS)], axis=axis)
    return out


def _slots_from_full(full):
    layout, rows = _pack_layout()
    parts = []
    for name, shape, axis in PACKED:
        r, n = layout[name]
        if _by_rows(name, axis):
            parts.append(full[name].reshape(N_CHIPS, n, PACK_W))
        else:
            size = shape[axis] // N_CHIPS
            parts.append(jnp.stack([lax.slice_in_dim(full[name], j * size, (j + 1) * size, axis=axis).reshape(n, PACK_W)
                                    for j in range(N_CHIPS)]))
    parts.append(jnp.zeros((N_CHIPS, rows - sum(p.shape[1] for p in parts), PACK_W), parts[0].dtype))
    return jnp.concatenate(parts, axis=1)


def _pack_small(vals):
    parts = []
    for name, n in SMALL:
        pad = -(-n // 128) * 128 - n
        parts.append(jnp.pad(vals[name].reshape(-1).astype(F32), (0, pad)).reshape(-1, 128))
    rows = sum(p.shape[0] for p in parts)
    parts.append(jnp.zeros((-(-rows // 8) * 8 - rows, 128), F32))
    return jnp.concatenate(parts, axis=0)


def _unpack_small(packed):
    out, r = {}, 0
    for name, n in SMALL:
        k = -(-n // 128)
        out[name] = packed[r:r + k].reshape(-1)[:n]
        r += k
    return out


def _pad_heads(w, axis, per_head, lo, hi):
    shape = w.shape
    w = w.reshape(shape[:axis] + (N_HEADS, per_head) + shape[axis + 1:])
    w = lax.slice_in_dim(w, lo, hi, axis=axis + 1)
    pad = [(0, 0)] * w.ndim
    pad[axis + 1] = (0, HP - (hi - lo))
    w = jnp.pad(w, pad)
    return w.reshape(shape[:axis] + (N_HEADS * HP,) + shape[axis + 1:])


def _unpad_heads(w, axis, keep):
    shape = w.shape
    w = w.reshape(shape[:axis] + (N_HEADS, HP) + shape[axis + 1:])
    return lax.slice_in_dim(w, 0, keep, axis=axis + 1)


def _pad_w_in(wt):
    o = [0]
    for s in IN_SPLITS:
        o.append(o[-1] + s)
    cq, ckv, kpe, z, xbc, dtf, dtb, ga, gb = [wt[o[i]:o[i + 1]] for i in range(len(IN_SPLITS))]
    kpe_pad = jnp.pad(kpe, ((QK_NOPE, HP - QK_HEAD), (0, 0)))
    dt_pad = jnp.pad(jnp.concatenate([dtf, dtb], axis=0), ((0, HP - 2 * SSM_HEADS), (0, 0)))
    return jnp.concatenate([z, ga, gb, xbc, cq, ckv, kpe_pad, dt_pad], axis=0)


def _unpad_w_in(gt):
    z, ga, gb, xbc = gt[U_Z:U_GA], gt[U_GA:U_GB], gt[U_GB:U_XBC], gt[U_XBC:U_SMALL]
    s = gt[U_SMALL:]
    cq, ckv = s[S_CQ:S_CKV], s[S_CKV:S_KPE]
    kpe = s[S_KPE + QK_NOPE:S_KPE + QK_HEAD]
    dtf, dtb = s[S_DT:S_DT + SSM_HEADS], s[S_DT + SSM_HEADS:S_DT + 2 * SSM_HEADS]
    return jnp.concatenate([cq, ckv, kpe, z, xbc, dtf, dtb, ga, gb], axis=0)


def _lanes128(parts):
    row = jnp.concatenate([p.reshape(-1) for p in parts])
    return jnp.pad(row, (0, HP - row.shape[0])).reshape(1, HP)


FF_TILE = D_FF // 2
WGRAD = BF


def _ffn_fwd(x, g, wg_t, wu_t, wd, tag):
    h = _rms_fwd(x, g, name=tag + "_norm")
    gate, up, act = _mm([h], [wg_t, wu_t], name=tag + "_up", tb=True, out_dtypes=(BF, BF, BF), tm=512, tn=FF_TILE,
                        epilogue=lambda a, b: (a, b, _silu(a) * b))
    out = _mm([act], [wd], name=tag + "_down", extras=[x], epilogue=lambda acc, r: (r + 0.5 * acc,))
    return out, (h, gate, up, act)


def _ffn_bwd(dout, dout_bf, x, g, wg_t, wu_t, wd, saved, tag):
    h, gate, up, act = saved

    def swiglu_bwd(acc, a, b):
        a, b, half = a.astype(F32), b.astype(F32), 0.5 * acc
        s = jax.nn.sigmoid(a)
        return half * b * (s * (1.0 + a * (1.0 - s))), half * (a * s)

    dgate, dup = _mm([dout_bf], [wd], name=tag + "_down_dx", tb=True, extras=[gate, up], out_dtypes=(BF, BF),
                     tm=512, tn=FF_TILE, epilogue=swiglu_bwd)
    dwd = _mm([act], [dout_bf], name=tag + "_down_dw", ta=True, tm=FF_TILE, tk=1024, out_dtypes=(WGRAD,),
              epilogue=lambda acc: (0.5 * acc,))
    dwg_t, dwu_t = _mm([dgate, dup], [h, h], name=tag + "_up_dw", ta=True, separate=True, out_dtypes=(WGRAD, WGRAD),
                       tm=FF_TILE, tk=1024)
    dh = _mm([dgate, dup], [wg_t, wu_t], name=tag + "_up_dx")
    dx, dx_bf, dg = _rms_bwd(dh, x, g, name=tag + "_norm_bwd", add=dout, out_dtypes=(F32, BF))
    return dx, dx_bf, dg, dwg_t, dwu_t, dwd


KPE_BLK = (U_SMALL + S_KPE) // HP
SMALL_BLK = U_SMALL // SMALL_W


def _local_step(x, pos_col, target, W, P):
    T = x.shape[0]
    sig = jax.nn.sigmoid
    x1, ffn1 = _ffn_fwd(x, P["ffn1_norm"], W["wg1"], W["wu1"], W["wd1"], "ffn1")
    h = _rms_fwd(x1, P["mix_norm"], name="mix_norm")
    u = _mm([h], [W["w_in"]], name="in_proj", tb=True, tn=1152)
    cqn = _rms_fwd(u, P["q_a_norm"], name="q_a_norm", blk_w=SMALL_W, blk_idx=SMALL_BLK, off=S_CQ, width=Q_LORA)
    ckvn = _rms_fwd(u, P["kv_a_norm"], name="kv_a_norm", blk_w=SMALL_W, blk_idx=SMALL_BLK, off=S_CKV, width=KV_LORA)
    q_raw = _mm([cqn], [W["wq"]], name="q_proj")
    def with_ones_lane(acc_k, acc_v):
        lane = lax.broadcasted_iota(jnp.int32, acc_v.shape, 1)
        return acc_k, jnp.where((lane & (HP - 1)) == V_HEAD, 1.0, acc_v)

    k_raw, v = _mm([ckvn], [W["wk"], W["wv"]], name="kv_proj", out_dtypes=(F32, BF), epilogue=with_ones_lane)
    rc, rs = _rope_tables(pos_col, P["freq"])
    q = _qk_prep_fwd(q_raw, None, P["q_head_norm"], rc, rs, name="q_prep", out_scale=Q_SCALE)
    k = _qk_prep_fwd(k_raw, u, P["k_head_norm"], rc, rs, name="k_prep", kpe_blk=KPE_BLK)
    o, lse = _attn_fwd(q, k, v)
    pre, act = _conv_fwd(u, P["conv_w8"], P["conv_b"])
    scan_cols, scan_rows = _ssd_prep(u, P["dt_bias8"], P["a_log8"])
    y_f, st_f = _ssd_fwd(act, scan_cols, scan_rows, rev=False, name="ssd_fwd_f")
    y_b, st_b = _ssd_fwd(act, scan_cols, scan_rows, rev=True, name="ssd_fwd_b")
    ysum, m = _ssm_combine_fwd(y_f, y_b, act, u, P["d_skip_lanes"], P["ssm_norm"])
    ab = _mm([o], [W["pa"]], name="attn_branch")
    mb, merged = _mm([m], [W["pb"]], name="ssm_branch", extras=[ab, u, u], extra_offs=(0, U_GA, U_GB), out_dtypes=(F32, BF),
                     epilogue=lambda acc, a, ga, gb: (acc, sig(ga) * a + sig(gb) * acc))
    x2 = _mm([merged], [W["wo"]], name="out_proj", extras=[x1], epilogue=lambda acc, r: (r + acc,))
    y, ffn2 = _ffn_fwd(x2, P["ffn2_norm"], W["wg2"], W["wu2"], W["wd2"], "ffn2")
    dy, dy_bf, loss = _loss_head(y, target)
    dx2, dx2_bf, dg_ffn2, dwg2, dwu2, dwd2 = _ffn_bwd(dy, dy_bf, x2, P["ffn2_norm"], W["wg2"], W["wu2"], W["wd2"], ffn2,
                                                      "ffn2")

    def gate_bwd(dmrg, a, b, ga, gb):
        sa, sb = sig(ga), sig(gb)
        return dmrg * sa, dmrg * sb, dmrg * a * sa * (1.0 - sa), dmrg * b * sb * (1.0 - sb)

    dab, dmb, dga, dgb = _mm([dx2_bf], [W["wo"]], name="out_proj_dx", tb=True, extras=[ab, mb, u, u],
                             extra_offs=(0, 0, U_GA, U_GB), out_dtypes=(BF,) * 4, epilogue=gate_bwd)
    dwo = _mm([merged], [dx2_bf], name="out_proj_dw", ta=True, out_dtypes=(WGRAD,))
    dpa = _mm([o], [dab], name="attn_branch_dw", ta=True, out_dtypes=(WGRAD,))
    do = _mm([dab], [W["pa"]], name="attn_branch_dx", tb=True)
    dpb = _mm([m], [dmb], name="ssm_branch_dw", ta=True, out_dtypes=(WGRAD,))
    dm = _mm([dmb], [W["pb"]], name="ssm_branch_dx", tb=True)
    dyssd, dz, dxs_skip, dg_ssm, dskip = _ssm_combine_bwd(dm, ysum, act, u, P["d_skip_lanes"], P["ssm_norm"])
    dxs_f, db_f, dc_f, dsel_f, dtot_f = _ssd_bwd(act, scan_cols, scan_rows, st_f, dyssd, rev=False, name="ssd_bwd_f")
    dxs_b, db_b, dc_b, dsel_b, dtot_b = _ssd_bwd(act, scan_cols, scan_rows, st_b, dyssd, rev=True, name="ssd_bwd_b")
    ddt, dalog, dbias = _ssd_prep_bwd(u, P["dt_bias8"], P["a_log8"], dsel_f, dtot_f, dsel_b, dtot_b)
    dxbc, dconv = [], []
    for tag, col0, parts in (("x", 0, [dxs_f, dxs_b, dxs_skip]), ("b", D_INNER, [db_f, db_b]),
                             ("c", D_INNER + SSM_GROUPS * D_STATE, [dc_f, dc_b])):
        dpre = _conv_dpre(parts, pre, col0, name="conv_dpre_" + tag)
        dxp, dwp = _conv_bwd(dpre, u, P["conv_w8"], col0, name="conv_bwd_" + tag)
        dxbc.append(dxp)
        dconv.append(dwp)
    dconv = jnp.concatenate(dconv, axis=1)
    dq, dk, dv = _attn_bwd(q, k, v, do, o, lse)
    dq_raw, dg_qh = _qk_prep_bwd(dq, q_raw, None, P["q_head_norm"], rc, rs, name="q_prep_bwd", in_scale=ATTN_SCALE)
    dk_raw, dg_kh, dkpe = _qk_prep_bwd(dk, k_raw, u, P["k_head_norm"], rc, rs, name="k_prep_bwd", kpe_blk=KPE_BLK,
                                       in_scale=1.0 / LOG2E)
    dwq = _mm([cqn], [dq_raw], name="q_proj_dw", ta=True, out_dtypes=(WGRAD,))
    dcqn = _mm([dq_raw], [W["wq"]], name="q_proj_dx", tb=True)
    dwk, dwv = _mm([ckvn], [dk_raw, dv], name="kv_proj_dw", ta=True, out_dtypes=(WGRAD, WGRAD))
    dckvn = _mm([dk_raw, dv], [W["wk"], W["wv"]], name="kv_proj_dx", tb=True)
    dcq, dg_qa = _rms_bwd(dcqn, u, P["q_a_norm"], name="q_a_norm_bwd", blk_w=SMALL_W, blk_idx=SMALL_BLK, off=S_CQ,
                          width=Q_LORA, out_dtypes=(BF,))
    dckv, dg_kva = _rms_bwd(dckvn, u, P["kv_a_norm"], name="kv_a_norm_bwd", blk_w=SMALL_W, blk_idx=SMALL_BLK,
                            off=S_CKV, width=KV_LORA, out_dtypes=(BF,))
    du = jnp.concatenate([dz, dga, dgb] + dxbc + [dcq, dckv, dkpe.astype(BF), ddt.astype(BF)], axis=1)
    dw_in = _mm([du], [h], name="in_proj_dw", ta=True, tm=1152, out_dtypes=(WGRAD,))
    dh = _mm([du], [W["w_in"]], name="in_proj_dx", tk=U_PAD // 3)
    dx1, dx1_bf, dg_mix = _rms_bwd(dh, x1, P["mix_norm"], name="mix_norm_bwd", add=dx2, out_dtypes=(F32, BF))
    dx, _, dg_ffn1, dwg1, dwu1, dwd1 = _ffn_bwd(dx1, dx1_bf, x, P["ffn1_norm"], W["wg1"], W["wu1"], W["wd1"], ffn1, "ffn1")
    dW = dict(wg1=dwg1, wu1=dwu1, wd1=dwd1, w_in=dw_in, wq=dwq, wk=dwk, wv=dwv, pa=dpa, pb=dpb, wo=dwo,
              wg2=dwg2, wu2=dwu2, wd2=dwd2)
    dP = dict(ffn1_norm=dg_ffn1[0], mix_norm=dg_mix[0], q_a_norm=dg_qa[0], kv_a_norm=dg_kva[0],
              q_head_norm=dg_qh[0, :QK_HEAD], k_head_norm=dg_kh[0, :QK_HEAD], conv_b=dconv[CONV_WIDTH],
              a_log_fwd=dalog[0, :SSM_HEADS], a_log_bwd=dalog[0, SSM_HEADS:2 * SSM_HEADS],
              dt_bias_fwd=dbias[0, :SSM_HEADS], dt_bias_bwd=dbias[0, SSM_HEADS:2 * SSM_HEADS],
              d_skip=dskip[0].reshape(SSM_GROUPS, HP)[:, :HG], ssm_norm=dg_ssm[0], ffn2_norm=dg_ffn2[0],
              conv_w=dconv[:CONV_WIDTH], loss=loss[0, 0])
    return dx, dW, dP


def _prepare(w, conv_w_full):
    kvb = w["w_kv_b"]
    W = dict(wg1=w["ffn1_w_gate"], wu1=w["ffn1_w_up"], wd1=w["ffn1_w_down"], w_in=_pad_w_in(w["w_in"]),
             wq=_pad_heads(w["w_q_b"], 1, QK_HEAD, 0, QK_HEAD),
             wk=_pad_heads(kvb, 1, QK_NOPE + V_HEAD, 0, QK_NOPE),
             wv=_pad_heads(kvb, 1, QK_NOPE + V_HEAD, QK_NOPE, QK_NOPE + V_HEAD),
             pa=_pad_heads(w["w_attn_branch"], 0, V_HEAD, 0, V_HEAD), pb=w["w_ssm_branch"], wo=w["w_out"],
             wg2=w["ffn2_w_gate"], wu2=w["ffn2_w_up"], wd2=w["ffn2_w_down"])
    inv_freq = [1.0 / (ROPE_BASE ** (j / QK_ROPE)) for j in range(0, QK_ROPE, 2)]
    freq = [0.0] * QK_NOPE + inv_freq + inv_freq + [0.0] * (HP - QK_HEAD)
    P = {n: w[n] for n in ("ffn1_norm", "mix_norm", "q_a_norm", "kv_a_norm", "ssm_norm", "ffn2_norm", "conv_b")}
    P.update(q_head_norm=_lanes128([w["q_head_norm"]]), k_head_norm=_lanes128([w["k_head_norm"]]),
             conv_w8=jnp.pad(conv_w_full, ((0, 8 - CONV_WIDTH), (0, 0))),
             dt_bias8=jnp.broadcast_to(_lanes128([w["dt_bias_fwd"], w["dt_bias_bwd"]]), (8, HP)),
             a_log8=jnp.broadcast_to(_lanes128([w["a_log_fwd"], w["a_log_bwd"]]), (8, HP)),
             d_skip_lanes=jnp.repeat(w["d_skip"].reshape(-1), PH).reshape(1, D_INNER),
             freq=jnp.asarray(freq, F32).reshape(1, HP))
    return W, P


def _unprepare(dW):
    dkvb = jnp.concatenate([_unpad_heads(dW["wk"], 1, QK_NOPE), _unpad_heads(dW["wv"], 1, V_HEAD)], axis=2)
    return dict(ffn1_w_gate=dW["wg1"], ffn1_w_up=dW["wu1"], ffn1_w_down=dW["wd1"], w_in=_unpad_w_in(dW["w_in"]),
                w_q_b=_unpad_heads(dW["wq"], 1, QK_HEAD).reshape(Q_LORA, N_HEADS * QK_HEAD),
                w_kv_b=dkvb.reshape(KV_LORA, N_HEADS * (QK_NOPE + V_HEAD)),
                w_attn_branch=_unpad_heads(dW["pa"], 0, V_HEAD).reshape(N_HEADS * V_HEAD, D_MODEL),
                w_ssm_branch=dW["pb"], w_out=dW["wo"],
                ffn2_w_gate=dW["wg2"], ffn2_w_up=dW["wu2"], ffn2_w_down=dW["wd2"])


def kernel(x, positions, ffn1_norm, ffn1_w_gate, ffn1_w_up, ffn1_w_down, mix_norm, w_in, q_a_norm, w_q_b, kv_a_norm, w_kv_b, q_head_norm, k_head_norm, conv_w, conv_b, a_log_fwd, a_log_bwd, dt_bias_fwd, dt_bias_bwd, d_skip, ssm_norm, w_attn_branch, w_ssm_branch, w_out, ffn2_norm, ffn2_w_gate, ffn2_w_up, ffn2_w_down, loss_target, m_ffn1_norm, m_ffn1_w_gate, m_ffn1_w_up, m_ffn1_w_down, m_mix_norm, m_w_in, m_q_a_norm, m_w_q_b, m_kv_a_norm, m_w_kv_b, m_q_head_norm, m_k_head_norm, m_conv_w, m_conv_b, m_a_log_fwd, m_a_log_bwd, m_dt_bias_fwd, m_dt_bias_bwd, m_d_skip, m_ssm_norm, m_w_attn_branch, m_w_ssm_branch, m_w_out, m_ffn2_norm, m_ffn2_w_gate, m_ffn2_w_up, m_ffn2_w_down, v_ffn1_norm, v_ffn1_w_gate, v_ffn1_w_up, v_ffn1_w_down, v_mix_norm, v_w_in, v_q_a_norm, v_w_q_b, v_kv_a_norm, v_w_kv_b, v_q_head_norm, v_k_head_norm, v_conv_w, v_conv_b, v_a_log_fwd, v_a_log_bwd, v_dt_bias_fwd, v_dt_bias_bwd, v_d_skip, v_ssm_norm, v_w_attn_branch, v_w_ssm_branch, v_w_out, v_ffn2_norm, v_ffn2_w_gate, v_ffn2_w_up, v_ffn2_w_down):
    given = dict(locals())
    T = x.shape[1]
    packed_names = [name for name, _, _ in PACKED]

    def two_d(a):
        return a.reshape(a.shape[1], -1) if a.ndim > 2 else a

    def kept(n, a):
        return _stored(n, two_d(a))

    w_loc = {n: kept(n, given[n]) for n in WEIGHTS}
    wb = _pack({n: w_loc[n].astype(BF) for n in packed_names})
    wf = jnp.pad(w_loc["conv_w"], ((0, 8 - CONV_WIDTH), (0, 0)))
    gb, gf = _gather_chips(wb, wf)
    full = _full_from_slots(gb)
    conv_w_full = jnp.concatenate([gf[j, :CONV_WIDTH] for j in range(N_CHIPS)], axis=1)
    full.update({n: w_loc[n] for n in WEIGHTS if n not in full and n != "conv_w"})
    W, P = _prepare(full, conv_w_full)
    dx, dW, dP = _local_step(x.reshape(T, D_MODEL), positions.reshape(T, 1).astype(F32), loss_target.reshape(T, D_MODEL), W, P)
    gp = _slots_from_full(_unprepare(dW))
    core = lax.axis_index("c").astype(jnp.int32).reshape(1)
    both_cores = _add_halves(gp, _halves_to_sibling(gp), core)
    cx, cy = lax.axis_index("x"), lax.axis_index("y")
    chips = jnp.stack([2 * cx + cy, 2 * (1 - cx) + cy, 2 * cx + (1 - cy)]).astype(jnp.int32)
    part, on = _add_near(both_cores, _exchange_near(both_cores), chips)
    grads = _unpack(_join_halves(_add_far(part, _exchange_far(on), core)))
    small = _unpack_small(_allreduce_small(_pack_small(dP)))
    grads.update({n: small[n].reshape(1, -1) for n, _ in SMALL if n not in ("conv_w", "loss")})
    grads["conv_w"] = lax.dynamic_slice_in_dim(small["conv_w"].reshape(CONV_WIDTH, XBC_DIM), _my_chip() * (XBC_DIM // N_CHIPS),
                                               XBC_DIM // N_CHIPS, axis=1)
    out_g, out_d, out_m, out_v = [], [], [], []
    for n in WEIGHTS:
        shape = given[n].shape
        delta, new_m, new_v = _adamw(w_loc[n], grads[n], kept(n, given["m_" + n]), kept(n, given["v_" + n]), name="adamw_" + n)
        for outs, a in ((out_g, grads[n]), (out_d, delta), (out_m, new_m), (out_v, new_v)):
            outs.append(_stored(n, a).reshape(shape))
    return (small["loss"].reshape(()), dx.reshape(x.shape), *out_g, *out_d, *out_m, *out_v)
```

```python
import functools
import math

import jax
import jax.numpy as jnp
from jax import lax
from jax.experimental import pallas as pl
from jax.experimental.pallas import tpu as pltpu

BF = jnp.bfloat16
F32 = jnp.float32
HI = lax.Precision.HIGHEST
MESH = pl.DeviceIdType.MESH

D_MODEL = 1024
D_FF = 2816
EPS = 1e-6
N_HEADS = 16
QK_NOPE = 64
QK_ROPE = 32
QK_HEAD = 96
V_HEAD = 64
Q_LORA = 384
KV_LORA = 256
ROPE_BASE = 10000.0
D_INNER = 2048
SSM_HEADS = 32
SSM_GROUPS = 4
D_STATE = 128
CONV_WIDTH = 5
CHUNK = 128
XBC_DIM = 3072
HP = 128
GW = D_INNER // SSM_GROUPS
HG = SSM_HEADS // SSM_GROUPS
PH = 64
U_Z, U_GA, U_GB, U_XBC, U_SMALL = 0, 2048, 3072, 4096, 7168
S_CQ, S_CKV, S_KPE, S_DT, SMALL_W = 0, 384, 640, 768, 896
U_PAD = U_SMALL + SMALL_W
IN_SPLITS = (Q_LORA, KV_LORA, QK_ROPE, D_INNER, XBC_DIM, SSM_HEADS, SSM_HEADS, D_MODEL, D_MODEL)

ADAM_LR = 0.001
ADAM_B1 = 0.9
ADAM_B2 = 0.999
ADAM_EPS = 1e-08
ADAM_WD = 0.01
ADAM_STEP = 10

NN = (((1,), (0,)), ((), ()))
NT = (((1,), (1,)), ((), ()))
TN = (((0,), (0,)), ((), ()))


def _pick(n, pref):
    best = None
    d = 128
    while d <= min(n, pref):
        if n % d == 0:
            best = d
        d += 128
    return best if best is not None else n


def _silu(x):
    return x * jax.nn.sigmoid(x)


def _dsilu(x):
    s = jax.nn.sigmoid(x)
    return s * (1.0 + x * (1.0 - s))


def _softplus(x):
    return jnp.maximum(x, 0.0) + jnp.log(1.0 + jnp.exp(-jnp.abs(x)))


def _mm(As, Bs, *, name, ta=False, tb=False, out_dtypes=(F32,), epilogue=None, extras=(), extra_offs=None,
        tm=1024, tn=512, tk=2048, separate=False):
    As, Bs, extras = list(As), list(Bs), list(extras)
    a0, b0 = As[0], Bs[0]
    M, K = (a0.shape[1], a0.shape[0]) if ta else a0.shape
    N = b0.shape[0] if tb else b0.shape[1]
    tm, tn, tk = _pick(M, tm), _pick(N, tn), _pick(K, tk)
    nk = K // tk
    n_a, n_b, n_e, n_o = len(As), len(Bs), len(extras), len(out_dtypes)
    n_acc = (n_b if n_a == 1 or separate else 1) if nk > 1 else 0
    if extra_offs is None:
        extra_offs = (0,) * n_e
    dn = (((0,) if ta else (1,), (1,) if tb else (0,)), ((), ()))
    bytes_a = sum(a.size * a.dtype.itemsize for a in As)
    bytes_b = sum(b.size * b.dtype.itemsize for b in Bs)
    n_outer = (N // tn) * bytes_a + bytes_b < (M // tm) * bytes_b + bytes_a

    def products(a_refs, b_refs):
        if n_a == 1:
            a = a_refs[0][...].astype(BF)
            return [lax.dot_general(a, b[...].astype(BF), dn, preferred_element_type=F32) for b in b_refs]
        if separate:
            return [lax.dot_general(a[...].astype(BF), b[...].astype(BF), dn, preferred_element_type=F32)
                    for a, b in zip(a_refs, b_refs)]
        total = None
        for a, b in zip(a_refs, b_refs):
            p = lax.dot_general(a[...].astype(BF), b[...].astype(BF), dn, preferred_element_type=F32)
            total = p if total is None else total + p
        return [total]

    def finish(accs, e_refs, o_refs):
        ex = [e[...] for e in e_refs]
        outs = epilogue(*accs, *ex) if epilogue is not None else tuple(accs)
        for o_ref, val in zip(o_refs, outs):
            o_ref[...] = val.astype(o_ref.dtype)

    def body(*refs):
        a_refs, b_refs = refs[:n_a], refs[n_a:n_a + n_b]
        e_refs = refs[n_a + n_b:n_a + n_b + n_e]
        o_refs = refs[n_a + n_b + n_e:n_a + n_b + n_e + n_o]
        acc_refs = refs[n_a + n_b + n_e + n_o:]
        if nk == 1:
            finish(products(a_refs, b_refs), e_refs, o_refs)
            return
        k = pl.program_id(2)

        @pl.when(k == 0)
        def _():
            for acc in acc_refs:
                acc[...] = jnp.zeros_like(acc)

        for acc, p in zip(acc_refs, products(a_refs, b_refs)):
            acc[...] += p

        @pl.when(k == nk - 1)
        def _():
            finish([acc[...] for acc in acc_refs], e_refs, o_refs)

    def at(f):
        return (lambda j, i, k: f(i, j, k)) if n_outer else f

    a_spec = pl.BlockSpec((tk, tm), at(lambda i, j, k: (k, i))) if ta else pl.BlockSpec((tm, tk), at(lambda i, j, k: (i, k)))
    b_spec = pl.BlockSpec((tn, tk), at(lambda i, j, k: (j, k))) if tb else pl.BlockSpec((tk, tn), at(lambda i, j, k: (k, j)))
    e_specs = [pl.BlockSpec((tm, tn), at(functools.partial(lambda i, j, k, o: (i, j + o), o=off // tn))) for off in extra_offs]
    for off in extra_offs:
        assert off % tn == 0
    outs = pl.pallas_call(
        body, name=name,
        out_shape=tuple(jax.ShapeDtypeStruct((M, N), dt) for dt in out_dtypes),
        grid=(N // tn, M // tm, nk) if n_outer else (M // tm, N // tn, nk),
        in_specs=[a_spec] * n_a + [b_spec] * n_b + e_specs,
        out_specs=tuple(pl.BlockSpec((tm, tn), at(lambda i, j, k: (i, j))) for _ in out_dtypes),
        scratch_shapes=[pltpu.VMEM((tm, tn), F32)] * n_acc,
        compiler_params=pltpu.CompilerParams(dimension_semantics=("parallel", "parallel", "arbitrary")),
    )(*As, *Bs, *extras)
    return outs[0] if n_o == 1 else outs


def _rms_fwd(x, g, *, name, blk_w=None, blk_idx=0, off=0, width=None, out_dtype=BF):
    T = x.shape[0]
    blk_w = x.shape[1] if blk_w is None else blk_w
    width = blk_w if width is None else width
    tt = _pick(T, 512)

    def body(x_ref, g_ref, o_ref):
        xf = x_ref[:, off:off + width]
        r = lax.rsqrt(jnp.mean(xf * xf, axis=-1, keepdims=True) + EPS)
        o_ref[...] = (xf * r * g_ref[...]).astype(o_ref.dtype)

    return pl.pallas_call(
        body, name=name, out_shape=jax.ShapeDtypeStruct((T, width), out_dtype), grid=(T // tt,),
        in_specs=[pl.BlockSpec((tt, blk_w), lambda i: (i, blk_idx)), pl.BlockSpec((1, width), lambda i: (0, 0))],
        out_specs=pl.BlockSpec((tt, width), lambda i: (i, 0)),
    )(x, g)


def _rms_bwd(dy, x, g, *, name, blk_w=None, blk_idx=0, off=0, width=None, add=None, out_dtypes=(F32,)):
    T = x.shape[0]
    blk_w = x.shape[1] if blk_w is None else blk_w
    width = blk_w if width is None else width
    tt = _pick(T, 512)
    has_add = add is not None
    n_dx = len(out_dtypes)

    def body(*refs):
        dy_ref, x_ref, g_ref = refs[:3]
        dx_refs, dg_ref = refs[3 + has_add:3 + has_add + n_dx], refs[-1]
        xf = x_ref[:, off:off + width]
        d = dy_ref[...].astype(F32)
        r = lax.rsqrt(jnp.mean(xf * xf, axis=-1, keepdims=True) + EPS)
        gd = d * g_ref[...]
        dx = r * gd - xf * (r * r * r) * jnp.mean(gd * xf, axis=-1, keepdims=True)
        if has_add:
            dx = dx + refs[3][...]
        for dx_ref in dx_refs:
            dx_ref[...] = dx.astype(dx_ref.dtype)

        @pl.when(pl.program_id(0) == 0)
        def _():
            dg_ref[...] = jnp.zeros_like(dg_ref)

        dg_ref[...] += jnp.broadcast_to(jnp.sum(d * xf * r, axis=0, keepdims=True), dg_ref.shape)

    row = pl.BlockSpec((tt, width), lambda i: (i, 0))
    in_specs = [row, pl.BlockSpec((tt, blk_w), lambda i: (i, blk_idx)), pl.BlockSpec((1, width), lambda i: (0, 0))]
    args = [dy, x, g]
    if has_add:
        in_specs.append(row)
        args.append(add)
    return pl.pallas_call(
        body, name=name,
        out_shape=tuple(jax.ShapeDtypeStruct((T, width), dt) for dt in out_dtypes) + (jax.ShapeDtypeStruct((8, width), F32),),
        grid=(T // tt,), in_specs=in_specs,
        out_specs=(row,) * n_dx + (pl.BlockSpec((8, width), lambda i: (0, 0)),),
        compiler_params=pltpu.CompilerParams(dimension_semantics=("arbitrary",)),
    )(*args)


def _rope_tables(pos_col, freq_lane):
    T = pos_col.shape[0]
    tt = _pick(T, 512)

    def body(p_ref, f_ref, c_ref, s_ref):
        ang = p_ref[...] * f_ref[...]
        lane = lax.broadcasted_iota(jnp.int32, ang.shape, 1)
        c_ref[...] = jnp.where(lane < QK_HEAD, jnp.cos(ang), 0.0)
        sn = jnp.sin(ang)
        s_ref[...] = jnp.where((lane >= QK_NOPE) & (lane < QK_NOPE + 16), -sn,
                               jnp.where((lane >= QK_NOPE + 16) & (lane < QK_HEAD), sn, 0.0))

    return pl.pallas_call(
        body, name="rope_tables", out_shape=(jax.ShapeDtypeStruct((T, HP), F32),) * 2, grid=(T // tt,),
        in_specs=[pl.BlockSpec((tt, 1), lambda i: (i, 0)), pl.BlockSpec((1, HP), lambda i: (0, 0))],
        out_specs=(pl.BlockSpec((tt, HP), lambda i: (i, 0)),) * 2,
    )(pos_col, freq_lane)


def _swap_rope_halves(n):
    src = lax.broadcasted_iota(jnp.int32, (HP, HP), 0)
    dst = lax.broadcasted_iota(jnp.int32, (HP, HP), 1)
    lo = (dst >= QK_NOPE) & (dst < QK_NOPE + 16) & (src == dst + 16)
    hi = (dst >= QK_NOPE + 16) & (dst < QK_HEAD) & (src == dst - 16)
    return _split_dot(n, jnp.where(lo | hi, 1.0, 0.0).astype(BF), 2)


def _qk_prep_fwd(raw, kpe, gain, C, S, *, name, kpe_blk=0, out_scale=1.0):
    T = raw.shape[0]
    tt = _pick(T, 256)
    has_kpe = kpe is not None

    def body(*refs):
        if has_kpe:
            raw_ref, kpe_ref, g_ref, c_ref, s_ref, o_ref = refs
        else:
            raw_ref, g_ref, c_ref, s_ref, o_ref = refs
        for h in range(N_HEADS):
            hs = slice(HP * h, HP * (h + 1))
            xr = raw_ref[:, hs] + kpe_ref[...] if has_kpe else raw_ref[:, hs]
            r = lax.rsqrt(jnp.sum(xr * xr, axis=-1, keepdims=True) * (1.0 / QK_HEAD) + EPS)
            n = xr * r * g_ref[...]
            o_ref[:, hs] = ((n * c_ref[...] + _swap_rope_halves(n) * s_ref[...]) * out_scale).astype(o_ref.dtype)

    heads = pl.BlockSpec((tt, N_HEADS * HP), lambda i: (i, 0))
    shared = pl.BlockSpec((tt, HP), lambda i: (i, 0))
    kpe_spec = pl.BlockSpec((tt, HP), lambda i: (i, kpe_blk))
    in_specs = [heads] + ([kpe_spec] if has_kpe else []) + [pl.BlockSpec((1, HP), lambda i: (0, 0)), shared, shared]
    args = [raw] + ([kpe] if has_kpe else []) + [gain, C, S]
    return pl.pallas_call(
        body, name=name, out_shape=jax.ShapeDtypeStruct(raw.shape, BF), grid=(T // tt,),
        in_specs=in_specs, out_specs=heads,
    )(*args)


def _qk_prep_bwd(dout, raw, kpe, gain, C, S, *, name, kpe_blk=0, in_scale=1.0):
    T = raw.shape[0]
    tt = _pick(T, 256)
    has_kpe = kpe is not None

    def body(*refs):
        if has_kpe:
            d_ref, raw_ref, kpe_ref, g_ref, c_ref, s_ref, dx_ref, dg_ref, dkpe_ref = refs
        else:
            d_ref, raw_ref, g_ref, c_ref, s_ref, dx_ref, dg_ref = refs
        dg = jnp.zeros((1, HP), F32)
        dkpe = jnp.zeros((tt, HP), F32)
        for h in range(N_HEADS):
            hs = slice(HP * h, HP * (h + 1))
            xr = raw_ref[:, hs] + kpe_ref[...] if has_kpe else raw_ref[:, hs]
            d = d_ref[:, hs].astype(F32) * in_scale
            r = lax.rsqrt(jnp.sum(xr * xr, axis=-1, keepdims=True) * (1.0 / QK_HEAD) + EPS)
            dn = d * c_ref[...] + _swap_rope_halves(d * s_ref[...])
            gd = dn * g_ref[...]
            dx = r * gd - xr * (r * r * r) * (jnp.sum(gd * xr, axis=-1, keepdims=True) * (1.0 / QK_HEAD))
            dx_ref[:, hs] = dx.astype(dx_ref.dtype)
            dg = dg + jnp.sum(dn * xr * r, axis=0, keepdims=True)
            dkpe = dkpe + dx

        @pl.when(pl.program_id(0) == 0)
        def _():
            dg_ref[...] = jnp.zeros_like(dg_ref)

        dg_ref[...] += jnp.broadcast_to(dg, dg_ref.shape)
        if has_kpe:
            dkpe_ref[...] = dkpe

    heads = pl.BlockSpec((tt, N_HEADS * HP), lambda i: (i, 0))
    shared = pl.BlockSpec((tt, HP), lambda i: (i, 0))
    kpe_spec = pl.BlockSpec((tt, HP), lambda i: (i, kpe_blk))
    in_specs = [heads, heads] + ([kpe_spec] if has_kpe else []) + [pl.BlockSpec((1, HP), lambda i: (0, 0)), shared, shared]
    args = [dout, raw] + ([kpe] if has_kpe else []) + [gain, C, S]
    out_shape = [jax.ShapeDtypeStruct(raw.shape, BF), jax.ShapeDtypeStruct((8, HP), F32)]
    out_specs = [heads, pl.BlockSpec((8, HP), lambda i: (0, 0))]
    if has_kpe:
        out_shape.append(jax.ShapeDtypeStruct((T, HP), F32))
        out_specs.append(shared)
    return pl.pallas_call(
        body, name=name, out_shape=tuple(out_shape), grid=(T // tt,),
        in_specs=in_specs, out_specs=tuple(out_specs),
        compiler_params=pltpu.CompilerParams(dimension_semantics=("arbitrary",)),
    )(*args)


ATTN_SCALE = 1.0 / math.sqrt(QK_HEAD)
LOG2E = 1.0 / math.log(2.0)
Q_SCALE = ATTN_SCALE * LOG2E


def _attn_fwd(q, k, v):
    T = q.shape[0]
    tq = _pick(T, 256)

    def body(q_ref, k_ref, v_ref, o_ref, lse_ref):
        s = lax.dot_general(q_ref[...], k_ref[...], NT, preferred_element_type=F32)
        m = jnp.max(s, axis=-1, keepdims=True)
        p = jnp.exp2(s - m)
        o = jnp.dot(p.astype(BF), v_ref[...], preferred_element_type=F32)
        l = o[:, V_HEAD:V_HEAD + 1]
        o_ref[...] = o / l
        lse_ref[...] = jnp.broadcast_to(m + jnp.log2(l), lse_ref.shape)

    qs = pl.BlockSpec((tq, HP), lambda h, i: (i, h))
    kv = pl.BlockSpec((T, HP), lambda h, i: (0, h))
    return pl.pallas_call(
        body, name="attn_fwd", out_shape=(jax.ShapeDtypeStruct(q.shape, F32),) * 2, grid=(N_HEADS, T // tq),
        in_specs=[qs, kv, kv], out_specs=(qs, qs),
        compiler_params=pltpu.CompilerParams(dimension_semantics=("parallel", "parallel")),
    )(q, k, v)


def _attn_bwd(q, k, v, do, o, lse):
    T = q.shape[0]
    tb = _pick(T, 512)
    nb = T // tb
    tkey = _pick(T, 1024)

    def body(q_ref, k_ref, v_ref, do_ref, o_ref, lse_ref, dq_ref, dk_ref, dv_ref, delta_rows, lse_rows, dob_scr, dv_acc):
        dq_ref[...] = jnp.zeros_like(dq_ref)
        dk_ref[...] = jnp.zeros_like(dk_ref)
        lane = lax.broadcasted_iota(jnp.int32, (8, HP), 1)
        ones8 = jnp.ones((8, HP), BF)
        first8 = jnp.where(lane == 0, 1.0, 0.0).astype(BF)

        def as_rows(pick, v):
            total, rest = None, v
            for _ in range(3):
                piece = rest.astype(BF)
                part = lax.dot_general(pick, piece, NT, preferred_element_type=F32)
                total = part if total is None else total + part
                rest = rest - piece.astype(F32)
            return total

        def per_q_tile(i, carry):
            qs = pl.ds(pl.multiple_of(i * tb, tb), tb)
            doi = do_ref[qs, :]
            delta_rows[i] = as_rows(ones8, doi * o_ref[qs, :])
            lse_rows[i] = as_rows(first8, lse_ref[qs, :])
            dob_scr[qs, :] = doi.astype(BF)
            return carry

        lax.fori_loop(0, nb, per_q_tile, 0)

        def k_loop(j, carry):
            ks = pl.ds(pl.multiple_of(j * tkey, tkey), tkey)
            kj, vj = k_ref[ks, :], v_ref[ks, :]

            dv_acc[...] = jnp.zeros_like(dv_acc)

            def q_loop(i, carry_q):
                qs = pl.ds(pl.multiple_of(i * tb, tb), tb)
                qi = q_ref[qs, :]
                dob = dob_scr[qs, :]
                s_t = lax.dot_general(kj, qi, NT, preferred_element_type=F32)
                p_t = jnp.exp2(s_t - lse_rows[i, 0:1, :])
                dp_t = lax.dot_general(vj, dob, NT, preferred_element_type=F32)
                ds_t = (p_t * (dp_t - delta_rows[i, 0:1, :])).astype(BF)
                dv_acc[...] += jnp.dot(p_t.astype(BF), dob, preferred_element_type=F32)
                dk_ref[ks, :] += jnp.dot(ds_t, qi, preferred_element_type=F32)
                dq_ref[qs, :] += lax.dot_general(ds_t, kj, TN, preferred_element_type=F32)
                return carry_q

            lax.fori_loop(0, nb, q_loop, 0)
            dv_ref[ks, :] = dv_acc[...].astype(dv_ref.dtype)
            return carry

        lax.fori_loop(0, T // tkey, k_loop, 0)

    spec = pl.BlockSpec((T, HP), lambda h: (0, h))
    return pl.pallas_call(
        body, name="attn_bwd",
        out_shape=(jax.ShapeDtypeStruct(q.shape, F32), jax.ShapeDtypeStruct(q.shape, F32), jax.ShapeDtypeStruct(q.shape, BF)),
        grid=(N_HEADS,), in_specs=[spec] * 6, out_specs=(spec,) * 3,
        scratch_shapes=[pltpu.VMEM((nb, 8, tb), F32), pltpu.VMEM((nb, 8, tb), F32), pltpu.VMEM((T, HP), BF),
                        pltpu.VMEM((tkey, HP), F32)],
        compiler_params=pltpu.CompilerParams(dimension_semantics=("parallel",), vmem_limit_bytes=2 * 15 * T * HP * 2 + (8 << 20)),
    )(q, k, v, do, o, lse)


CONV_TC = 512
CONV_PAD = CONV_WIDTH // 2


def _halo_specs(tr, col_of):
    r8 = tr // 8
    cur = pl.BlockSpec((tr, CONV_TC), lambda j, i: (i, col_of(j)))
    prev = pl.BlockSpec((8, CONV_TC), lambda j, i: (jnp.maximum(i * r8 - 1, 0), col_of(j)))

    def nxt_map(j, i, n8):
        return (jnp.minimum((i + 1) * r8, n8 - 1), col_of(j))

    return cur, prev, nxt_map


def _with_halo(prev_ref, cur_ref, next_ref, i, n_i):
    prev = jnp.where(i == 0, 0.0, prev_ref[...].astype(F32))
    nxt = jnp.where(i == n_i - 1, 0.0, next_ref[...].astype(F32))
    return jnp.concatenate([prev, cur_ref[...].astype(F32), nxt], axis=0)


def _conv_fwd(u, w8, b):
    T = u.shape[0]
    tr = _pick(T, 512)
    n_i = T // tr
    c0 = U_XBC // CONV_TC
    cur, prev, nxt_map = _halo_specs(tr, lambda j: c0 + j)
    nxt = pl.BlockSpec((8, CONV_TC), functools.partial(nxt_map, n8=T // 8))

    def body(p_ref, c_ref, n_ref, w_ref, b_ref, pre_ref, act_ref):
        i = pl.program_id(1)
        full = _with_halo(p_ref, c_ref, n_ref, i, n_i)
        acc = jnp.broadcast_to(b_ref[...], (tr, CONV_TC))
        for kk in range(CONV_WIDTH):
            acc = acc + full[8 - CONV_PAD + kk:8 - CONV_PAD + kk + tr, :] * w_ref[kk:kk + 1, :]
        pre_ref[...] = acc
        act_ref[...] = _silu(acc)

    out = pl.BlockSpec((tr, CONV_TC), lambda j, i: (i, j))
    return pl.pallas_call(
        body, name="conv_fwd", out_shape=(jax.ShapeDtypeStruct((T, XBC_DIM), F32),) * 2,
        grid=(XBC_DIM // CONV_TC, n_i),
        in_specs=[prev, cur, nxt, pl.BlockSpec((8, CONV_TC), lambda j, i: (0, j)), pl.BlockSpec((1, CONV_TC), lambda j, i: (0, j))],
        out_specs=(out, out),
    )(u, u, u, w8, b)


def _conv_dpre(dacts, pre, col0, *, name):
    T, width = dacts[0].shape
    tt = _pick(T, 512)
    n_d = len(dacts)
    c0 = col0 // CONV_TC

    def body(*refs):
        d = refs[0][...]
        for r in refs[1:n_d]:
            d = d + r[...]
        refs[n_d + 1][...] = d * _dsilu(refs[n_d][...])

    blk = pl.BlockSpec((tt, CONV_TC), lambda j, i: (i, j))
    return pl.pallas_call(
        body, name=name, out_shape=jax.ShapeDtypeStruct((T, width), F32), grid=(width // CONV_TC, T // tt),
        in_specs=[blk] * n_d + [pl.BlockSpec((tt, CONV_TC), lambda j, i: (i, c0 + j))], out_specs=blk,
    )(*dacts, pre)


def _conv_bwd(dpre, u, w8, col0, *, name):
    T, width = dpre.shape
    tr = _pick(T, 512)
    n_i = T // tr
    cd = col0 // CONV_TC
    cx = (U_XBC + col0) // CONV_TC
    d_cur, d_prev, d_nxt_map = _halo_specs(tr, lambda j: j)
    x_cur, x_prev, x_nxt_map = _halo_specs(tr, lambda j: cx + j)
    d_nxt = pl.BlockSpec((8, CONV_TC), functools.partial(d_nxt_map, n8=T // 8))
    x_nxt = pl.BlockSpec((8, CONV_TC), functools.partial(x_nxt_map, n8=T // 8))

    def body(dp_ref, dc_ref, dn_ref, xp_ref, xc_ref, xn_ref, w_ref, dx_ref, dw_ref):
        i = pl.program_id(1)
        dfull = _with_halo(dp_ref, dc_ref, dn_ref, i, n_i)
        xfull = _with_halo(xp_ref, xc_ref, xn_ref, i, n_i)
        dcur = dc_ref[...]
        dx = jnp.zeros((tr, CONV_TC), F32)
        rows = []
        for kk in range(CONV_WIDTH):
            dx = dx + dfull[8 + CONV_PAD - kk:8 + CONV_PAD - kk + tr, :] * w_ref[kk:kk + 1, :]
            rows.append(jnp.sum(dcur * xfull[8 - CONV_PAD + kk:8 - CONV_PAD + kk + tr, :], axis=0, keepdims=True))
        rows.append(jnp.sum(dcur, axis=0, keepdims=True))
        rows.append(jnp.zeros((2, CONV_TC), F32))
        dx_ref[...] = dx.astype(dx_ref.dtype)

        @pl.when(i == 0)
        def _():
            dw_ref[...] = jnp.zeros_like(dw_ref)

        dw_ref[...] += jnp.concatenate(rows, axis=0)

    out = pl.BlockSpec((tr, CONV_TC), lambda j, i: (i, j))
    return pl.pallas_call(
        body, name=name, out_shape=(jax.ShapeDtypeStruct((T, width), BF), jax.ShapeDtypeStruct((8, width), F32)),
        grid=(width // CONV_TC, n_i),
        in_specs=[d_prev, d_cur, d_nxt, x_prev, x_cur, x_nxt, pl.BlockSpec((8, CONV_TC), lambda j, i: (0, cd + j))],
        out_specs=(out, pl.BlockSpec((8, CONV_TC), lambda j, i: (0, j))),
        compiler_params=pltpu.CompilerParams(dimension_semantics=("parallel", "arbitrary")),
    )(dpre, dpre, dpre, u, u, u, w8)


N_HB = 2 * SSM_GROUPS
P_DT, P_CS, P_E, P_W = 0, HP, 2 * HP, 3 * HP
DT_BLK = (U_SMALL + S_DT) // HP


def _tri(rev, transpose=False):
    rows = lax.broadcasted_iota(jnp.int32, (CHUNK, CHUNK), 0)
    cols = lax.broadcasted_iota(jnp.int32, (CHUNK, CHUNK), 1)
    if transpose:
        rows, cols = cols, rows
    return (cols >= rows) if rev else (cols <= rows)


def _ssd_prep(u, bias8, alog8):
    T = u.shape[0]
    nc = T // CHUNK

    def body(dt_ref, bias_ref, a_ref, cols_ref, rows_ref):
        lane = lax.broadcasted_iota(jnp.int32, (CHUNK, HP), 1)
        dt = _softplus(dt_ref[...] + bias_ref[0:1, :])
        da = dt * (-jnp.exp(a_ref[0:1, :]))
        cs_f = jnp.dot(jnp.where(_tri(False), 1.0, 0.0).astype(F32), da, precision=HI, preferred_element_type=F32)
        cs_b = jnp.dot(jnp.where(_tri(True), 1.0, 0.0).astype(F32), da, precision=HI, preferred_element_type=F32)
        cs = jnp.where(lane < SSM_HEADS, cs_f, cs_b)
        tot = jnp.where(lane[0:1] < SSM_HEADS, cs_f[CHUNK - 1:CHUNK, :], cs_b[0:1, :])
        e, w = jnp.exp(cs), jnp.exp(tot - cs)
        tot8 = jnp.broadcast_to(tot, (8, HP))
        etot8 = jnp.exp(tot8)
        for b in range(N_HB):
            down = (HP - HG * b) % HP

            def rolled(v):
                return pltpu.roll(v, down, 1) if down else v

            cols_ref[b, :, P_DT:P_DT + HP] = rolled(dt)
            cs_r = rolled(cs)
            cols_ref[b, :, P_CS:P_CS + HP] = cs_r
            cols_ref[b, :, P_E:P_E + HP] = rolled(e)
            cols_ref[b, :, P_W:P_W + HP] = rolled(w)
            rows_ref[b, 0, 0:8, :] = cs_r.T[0:8, :]
            r8 = lax.broadcasted_iota(jnp.int32, (8, HP), 0)
            rows_ref[b, 0, 8:16, :] = jnp.where(r8 == 0, rolled(tot8), jnp.where(r8 == 1, rolled(etot8), 0.0))

    vec = pl.BlockSpec((8, HP), lambda c: (0, 0))
    return pl.pallas_call(
        body, name="ssd_prep",
        out_shape=(jax.ShapeDtypeStruct((N_HB, T, 4 * HP), F32), jax.ShapeDtypeStruct((N_HB, nc, 16, HP), F32)),
        grid=(nc,), in_specs=[pl.BlockSpec((CHUNK, HP), lambda c: (c, DT_BLK)), vec, vec],
        out_specs=(pl.BlockSpec((N_HB, CHUNK, 4 * HP), lambda c: (0, c, 0)), pl.BlockSpec((N_HB, 1, 16, HP), lambda c: (0, c, 0, 0))),
    )(u, bias8, alog8)


def _ssd_specs(T, rev, bwd):
    nc = T // CHUNK
    fwd_order = (lambda c: nc - 1 - c) if rev else (lambda c: c)
    cm = (lambda c: fwd_order(nc - 1 - c)) if bwd else fwd_order
    hb0 = SSM_GROUPS if rev else 0
    xs = pl.BlockSpec((CHUNK, GW), lambda c, g: (cm(c), g))
    bs = pl.BlockSpec((CHUNK, D_STATE), lambda c, g: (cm(c), D_INNER // D_STATE + g))
    cs = pl.BlockSpec((CHUNK, D_STATE), lambda c, g: (cm(c), (D_INNER + SSM_GROUPS * D_STATE) // D_STATE + g))
    cols = pl.BlockSpec((1, CHUNK, 4 * HP), lambda c, g: (hb0 + g, cm(c), 0))
    rows = pl.BlockSpec((1, 1, 16, HP), lambda c, g: (hb0 + g, cm(c), 0, 0))
    return nc, cm, xs, bs, cs, cols, rows


def _head_lanes(to_heads):
    shape = (GW, HP) if to_heads else (HP, GW)
    wide = lax.broadcasted_iota(jnp.int32, shape, 0 if to_heads else 1)
    head = lax.broadcasted_iota(jnp.int32, shape, 1 if to_heads else 0)
    return jnp.where((wide >= PH * head) & (wide < PH * (head + 1)), 1.0, 0.0).astype(BF)


def _split_dot(v, m, terms):
    total, rest = None, v
    for _ in range(terms):
        piece = rest.astype(BF)
        part = jnp.dot(piece, m, preferred_element_type=F32)
        total = part if total is None else total + part
        rest = rest - piece.astype(F32)
    return total


def _spread_cols(cols_ref, rows_ref):
    spread = _head_lanes(False)
    dt_e = _split_dot(cols_ref[0, :, P_DT:P_DT + HP], spread, 3)
    e_e = _split_dot(cols_ref[0, :, P_E:P_E + HP], spread, 2)
    w_e = _split_dot(cols_ref[0, :, P_W:P_W + HP], spread, 2)
    etot_e = _split_dot(rows_ref[0, 0, 8:16, :], spread, 3)[1:2, :]
    return dt_e, e_e, w_e, etot_e


def _decay(cols_ref, rows_ref, hh, incl, transpose=False):
    col = cols_ref[0, :, P_CS + hh:P_CS + hh + 1]
    row = rows_ref[0, 0, hh:hh + 1, :]
    return jnp.where(incl, jnp.exp(row - col if transpose else col - row), 0.0)


def _ssd_fwd(act, cols, rows, *, rev, name):
    T = act.shape[0]
    nc, cm, xs_s, b_s, c_s, cols_s, rows_s = _ssd_specs(T, rev, False)

    def body(x_ref, b_ref, c_ref, cols_ref, rows_ref, y_ref, st_ref, state):
        c, g = pl.program_id(0), pl.program_id(1)

        @pl.when(c == 0)
        def _():
            state[g] = jnp.zeros((D_STATE, GW), F32)

        incl = _tri(rev)
        bm, cmat = b_ref[...].astype(BF), c_ref[...].astype(BF)
        bm_t = b_ref[...].T.astype(BF)
        cb = lax.dot_general(cmat, bm, NT, preferred_element_type=F32)
        dt_e, e_e, w_e, etot_e = _spread_cols(cols_ref, rows_ref)
        prev_all = state[g]
        st_ref[...] = prev_all
        xdt = x_ref[...] * dt_e
        xdt_b = xdt.astype(BF)
        yo_all = jnp.dot(cmat, prev_all.astype(BF), preferred_element_type=F32) * e_e
        state[g] = prev_all * etot_e + jnp.dot(bm_t, (xdt * w_e).astype(BF), preferred_element_type=F32)
        for hh in range(HG):
            hs = slice(PH * hh, PH * (hh + 1))
            lmat = _decay(cols_ref, rows_ref, hh, incl)
            yd = jnp.dot((cb * lmat).astype(BF), xdt_b[:, hs], preferred_element_type=F32)
            y_ref[:, hs] = yd + yo_all[:, hs]

    return pl.pallas_call(
        body, name=name,
        out_shape=(jax.ShapeDtypeStruct((T, D_INNER), F32), jax.ShapeDtypeStruct((nc * D_STATE, D_INNER), F32)),
        grid=(nc, SSM_GROUPS), in_specs=[xs_s, b_s, c_s, cols_s, rows_s], out_specs=(xs_s, xs_s),
        scratch_shapes=[pltpu.VMEM((SSM_GROUPS, D_STATE, GW), F32)],
        compiler_params=pltpu.CompilerParams(dimension_semantics=("arbitrary", "arbitrary")),
    )(act, act, act, cols, rows)


def _ssd_bwd(act, cols, rows, states, dy, *, rev, name):
    T = act.shape[0]
    nc, cm, xs_s, b_s, c_s, cols_s, rows_s = _ssd_specs(T, rev, True)

    def body(x_ref, b_ref, c_ref, cols_ref, rows_ref, st_ref, dy_ref, dx_ref, db_ref, dc_ref, dsel_ref, dtot_ref,
             dstate, dcs_cols, dcs_rows, dcb, dm_scr, dxdt_scr):
        c, g = pl.program_id(0), pl.program_id(1)

        @pl.when(c == 0)
        def _():
            dstate[g] = jnp.zeros((D_STATE, GW), F32)

        incl, incl_t = _tri(rev), _tri(rev, transpose=True)
        bm, cmat = b_ref[...].astype(BF), c_ref[...].astype(BF)
        cm_t = c_ref[...].T.astype(BF)
        cb = lax.dot_general(cmat, bm, NT, preferred_element_type=F32)
        cb_t = lax.dot_general(bm, cmat, NT, preferred_element_type=F32)
        prev_all, ds_all = st_ref[...], dstate[g]
        pb_all, dsb_all = prev_all.astype(BF), ds_all.astype(BF)
        cp_all = jnp.dot(cmat, pb_all, preferred_element_type=F32)
        bds_all = jnp.dot(bm, dsb_all, preferred_element_type=F32)
        dt_e, e_e, w_e, etot_e = _spread_cols(cols_ref, rows_ref)
        to_heads = _head_lanes(True)
        x, dy = x_ref[...], dy_ref[...]
        xdt = x * dt_e
        xdt_b, dy_b = xdt.astype(BF), dy.astype(BF)
        dye_b, xdw_b = (dy * e_e).astype(BF), (xdt * w_e).astype(BF)
        for hh in range(HG):
            hs = slice(PH * hh, PH * (hh + 1))
            mmat_t = cb_t * _decay(cols_ref, rows_ref, hh, incl_t, transpose=True)
            dm_scr[hh] = lax.dot_general(dy_b[:, hs], xdt_b[:, hs], NT, preferred_element_type=F32)
            dxdt_scr[:, hs] = jnp.dot(mmat_t.astype(BF), dy_b[:, hs], preferred_element_type=F32)
        bdsw = bds_all * w_e
        dxdt = dxdt_scr[...] + bdsw
        dx_ref[...] = dxdt * dt_e
        t = _split_dot(xdt * bdsw, to_heads, 2)
        dcs_state = _split_dot(dy * cp_all, to_heads, 2) * cols_ref[0, :, P_E:P_E + HP] - t
        dsel_ref[0, :, 0:HP] = _split_dot(dxdt * x, to_heads, 2)
        sp = _split_dot(jnp.broadcast_to(jnp.sum(ds_all * prev_all, axis=0, keepdims=True), (8, GW)), to_heads, 2)
        dtot_ref[0, 0] = jnp.sum(t, axis=0, keepdims=True) + sp * rows_ref[0, 0, 9:10, :]
        dstate[g] = ds_all * etot_e + jnp.dot(cm_t, dye_b, preferred_element_type=F32)
        dcs_cols[...] = jnp.zeros_like(dcs_cols)
        dcs_rows[...] = jnp.zeros_like(dcs_rows)
        dcb[...] = jnp.zeros_like(dcb)
        for hh in range(HG):
            lmat = _decay(cols_ref, rows_ref, hh, incl)
            dm = dm_scr[hh]
            qm = dm * (cb * lmat)
            dcs_cols[:, hh:hh + 1] = jnp.sum(qm, axis=1, keepdims=True)
            dcs_rows[hh:hh + 1, :] = jnp.sum(qm, axis=0, keepdims=True)
            dcb[...] += dm * lmat
        dcb_all = dcb[...]
        dsel_ref[0, :, HP:2 * HP] = dcs_state + dcs_cols[...] - dcs_rows[...].T
        dc_ref[...] = (lax.dot_general(dye_b, pb_all, NT, preferred_element_type=F32)
                       + jnp.dot(dcb_all.astype(BF), bm, preferred_element_type=F32))
        db_ref[...] = (lax.dot_general(xdw_b, dsb_all, NT, preferred_element_type=F32)
                       + jnp.dot(dcb_all.T.astype(BF), cmat, preferred_element_type=F32))

    bc_out = pl.BlockSpec((CHUNK, D_STATE), lambda c, g: (cm(c), g))
    return pl.pallas_call(
        body, name=name,
        out_shape=(jax.ShapeDtypeStruct((T, D_INNER), F32), jax.ShapeDtypeStruct((T, SSM_GROUPS * D_STATE), F32),
                   jax.ShapeDtypeStruct((T, SSM_GROUPS * D_STATE), F32), jax.ShapeDtypeStruct((SSM_GROUPS, T, 2 * HP), F32),
                   jax.ShapeDtypeStruct((SSM_GROUPS, nc, 8, HP), F32)),
        grid=(nc, SSM_GROUPS), in_specs=[xs_s, b_s, c_s, cols_s, rows_s, xs_s, xs_s],
        out_specs=(xs_s, bc_out, bc_out, pl.BlockSpec((1, CHUNK, 2 * HP), lambda c, g: (g, cm(c), 0)),
                   pl.BlockSpec((1, 1, 8, HP), lambda c, g: (g, cm(c), 0, 0))),
        scratch_shapes=[pltpu.VMEM((SSM_GROUPS, D_STATE, GW), F32), pltpu.VMEM((CHUNK, CHUNK), F32),
                        pltpu.VMEM((CHUNK, CHUNK), F32), pltpu.VMEM((CHUNK, CHUNK), F32),
                        pltpu.VMEM((HG, CHUNK, CHUNK), F32), pltpu.VMEM((CHUNK, GW), F32)],
        compiler_params=pltpu.CompilerParams(dimension_semantics=("arbitrary", "arbitrary")),
    )(act, act, act, cols, rows, states, dy)


def _ssd_prep_bwd(u, bias8, alog8, dsel_f, dtot_f, dsel_b, dtot_b):
    T = u.shape[0]
    nc = T // CHUNK

    def body(dt_ref, bias_ref, a_ref, sf_ref, tf_ref, sb_ref, tb_ref, ddt_ref, da_ref, dbias_ref):
        @pl.when(pl.program_id(0) == 0)
        def _():
            da_ref[...] = jnp.zeros_like(da_ref)
            dbias_ref[...] = jnp.zeros_like(dbias_ref)

        lane = lax.broadcasted_iota(jnp.int32, (CHUNK, HP), 1)
        pre = dt_ref[...] + bias_ref[0:1, :]
        dt = _softplus(pre)
        a = -jnp.exp(a_ref[0:1, :])
        ddt_x, dcs, dtot = jnp.zeros((CHUNK, HP), F32), jnp.zeros((CHUNK, HP), F32), jnp.zeros((8, HP), F32)
        for b in range(N_HB):
            s_ref, t_ref, g = (sf_ref, tf_ref, b) if b < SSM_GROUPS else (sb_ref, tb_ref, b - SSM_GROUPS)
            mine = (lane >= HG * b) & (lane < HG * (b + 1))

            def up(v):
                return pltpu.roll(v, HG * b, 1) if b else v

            ddt_x = ddt_x + jnp.where(mine, up(s_ref[g, :, 0:HP]), 0.0)
            dcs = dcs + jnp.where(mine, up(s_ref[g, :, HP:2 * HP]), 0.0)
            dtot = dtot + jnp.where(mine[0:8], up(t_ref[g, 0]), 0.0)
        tri_f = jnp.where(_tri(False, transpose=True), 1.0, 0.0).astype(F32)
        tri_b = jnp.where(_tri(True, transpose=True), 1.0, 0.0).astype(F32)
        dda = jnp.where(lane < SSM_HEADS, jnp.dot(tri_f, dcs, precision=HI, preferred_element_type=F32),
                        jnp.dot(tri_b, dcs, precision=HI, preferred_element_type=F32)) + dtot[0:1, :]
        dpre = (ddt_x + dda * a) * jax.nn.sigmoid(pre)
        ddt_ref[...] = jnp.where(lane < 2 * SSM_HEADS, dpre, 0.0)
        dbias_ref[...] += jnp.broadcast_to(jnp.sum(dpre, axis=0, keepdims=True), (8, HP))
        da_ref[...] += jnp.broadcast_to(jnp.sum(dda * dt, axis=0, keepdims=True) * a, (8, HP))

    vec = pl.BlockSpec((8, HP), lambda c: (0, 0))
    sel = pl.BlockSpec((SSM_GROUPS, CHUNK, 2 * HP), lambda c: (0, c, 0))
    tot = pl.BlockSpec((SSM_GROUPS, 1, 8, HP), lambda c: (0, c, 0, 0))
    tile = pl.BlockSpec((CHUNK, HP), lambda c: (c, 0))
    return pl.pallas_call(
        body, name="ssd_prep_bwd",
        out_shape=(jax.ShapeDtypeStruct((T, HP), F32), jax.ShapeDtypeStruct((8, HP), F32), jax.ShapeDtypeStruct((8, HP), F32)),
        grid=(nc,), in_specs=[pl.BlockSpec((CHUNK, HP), lambda c: (c, DT_BLK)), vec, vec, sel, tot, sel, tot],
        out_specs=(tile, vec, vec),
        compiler_params=pltpu.CompilerParams(dimension_semantics=("arbitrary",)),
    )(u, bias8, alog8, dsel_f, dtot_f, dsel_b, dtot_b)


def _ssm_combine_fwd(y_f, y_b, act, u, dskip, gain):
    T = y_f.shape[0]
    tt = _pick(T, 512)

    def body(yf_ref, yb_ref, x_ref, z_ref, ds_ref, g_ref, y_ref, m_ref):
        y = yf_ref[...] + yb_ref[...] + ds_ref[...] * x_ref[...]
        y2 = y * _silu(z_ref[...])
        r = lax.rsqrt(jnp.mean(y2 * y2, axis=-1, keepdims=True) + EPS)
        y_ref[...] = y
        m_ref[...] = (y2 * r * g_ref[...]).astype(m_ref.dtype)

    blk = pl.BlockSpec((tt, GW), lambda i, g: (i, g))
    vec = pl.BlockSpec((1, GW), lambda i, g: (0, g))
    return pl.pallas_call(
        body, name="ssm_combine_fwd",
        out_shape=(jax.ShapeDtypeStruct((T, D_INNER), F32), jax.ShapeDtypeStruct((T, D_INNER), BF)),
        grid=(T // tt, SSM_GROUPS), in_specs=[blk, blk, blk, blk, vec, vec], out_specs=(blk, blk),
    )(y_f, y_b, act, u, dskip, gain)


def _ssm_combine_bwd(dm, y, act, u, dskip, gain):
    T = y.shape[0]
    tt = _pick(T, 512)

    def body(dm_ref, y_ref, x_ref, z_ref, ds_ref, g_ref, dy_ref, dz_ref, dxs_ref, dg_ref, dsk_ref):
        z = z_ref[...]
        y = y_ref[...]
        x = x_ref[...]
        sz = _silu(z)
        y2 = y * sz
        r = lax.rsqrt(jnp.mean(y2 * y2, axis=-1, keepdims=True) + EPS)
        d = dm_ref[...]
        gd = d * g_ref[...]
        dy2 = r * gd - y2 * (r * r * r) * jnp.mean(gd * y2, axis=-1, keepdims=True)
        dy = dy2 * sz
        dy_ref[...] = dy
        dz_ref[...] = (dy2 * y * _dsilu(z)).astype(dz_ref.dtype)
        dxs_ref[...] = dy * ds_ref[...]

        @pl.when(pl.program_id(1) == 0)
        def _():
            dg_ref[...] = jnp.zeros_like(dg_ref)
            dsk_ref[...] = jnp.zeros_like(dsk_ref)

        dg_ref[...] += jnp.broadcast_to(jnp.sum(d * y2 * r, axis=0, keepdims=True), dg_ref.shape)
        lane_sum = jnp.broadcast_to(jnp.sum(dy * x, axis=0, keepdims=True), (8, GW))
        src = lax.broadcasted_iota(jnp.int32, (GW, HP), 0)
        head = lax.broadcasted_iota(jnp.int32, (GW, HP), 1)
        to_head = jnp.where((src >= PH * head) & (src < PH * (head + 1)), 1.0, 0.0).astype(F32)
        dsk_ref[...] += jnp.dot(lane_sum, to_head, precision=HI, preferred_element_type=F32)

    blk = pl.BlockSpec((tt, GW), lambda g, i: (i, g))
    vec = pl.BlockSpec((1, GW), lambda g, i: (0, g))
    acc = pl.BlockSpec((8, GW), lambda g, i: (0, g))
    return pl.pallas_call(
        body, name="ssm_combine_bwd",
        out_shape=(jax.ShapeDtypeStruct((T, D_INNER), F32), jax.ShapeDtypeStruct((T, D_INNER), BF),
                   jax.ShapeDtypeStruct((T, D_INNER), F32), jax.ShapeDtypeStruct((8, D_INNER), F32),
                   jax.ShapeDtypeStruct((8, SSM_GROUPS * HP), F32)),
        grid=(SSM_GROUPS, T // tt), in_specs=[blk, blk, blk, blk, vec, vec],
        out_specs=(blk, blk, blk, acc, pl.BlockSpec((8, HP), lambda g, i: (0, g))),
        compiler_params=pltpu.CompilerParams(dimension_semantics=("parallel", "arbitrary")),
    )(dm, y, act, u, dskip, gain)


def _loss_head(y, target):
    T, D = y.shape
    tt = _pick(T, 512)

    def body(y_ref, t_ref, dy_ref, dyb_ref, l_ref):
        e = y_ref[...] - t_ref[...]
        dy_ref[...] = e * (1.0 / D)
        dyb_ref[...] = (e * (1.0 / D)).astype(dyb_ref.dtype)

        @pl.when(pl.program_id(0) == 0)
        def _():
            l_ref[...] = jnp.zeros_like(l_ref)

        l_ref[...] += jnp.sum(e * e) * (0.5 / D)

    blk = pl.BlockSpec((tt, D), lambda i: (i, 0))
    return pl.pallas_call(
        body, name="loss_head",
        out_shape=(jax.ShapeDtypeStruct((T, D), F32), jax.ShapeDtypeStruct((T, D), BF), jax.ShapeDtypeStruct((8, 128), F32)),
        grid=(T // tt,), in_specs=[blk, blk], out_specs=(blk, blk, pl.BlockSpec((8, 128), lambda i: (0, 0))),
        compiler_params=pltpu.CompilerParams(dimension_semantics=("arbitrary",)),
    )(y, target)


def _adamw(w, g, m, v, *, name):
    R, C = w.shape
    cap = max(8, (1 << 18) // C)
    tr = R
    if R % 8 == 0:
        tr = 8
        for cand in range(8, min(R, cap) + 1, 8):
            if R % cand == 0:
                tr = cand

    def body(w_ref, g_ref, m_ref, v_ref, d_ref, nm_ref, nv_ref):
        gg = g_ref[...]
        nm = ADAM_B1 * m_ref[...] + (1.0 - ADAM_B1) * gg
        nv = ADAM_B2 * v_ref[...] + (1.0 - ADAM_B2) * jnp.square(gg)
        m_hat = nm / (1.0 - ADAM_B1 ** ADAM_STEP)
        v_hat = nv / (1.0 - ADAM_B2 ** ADAM_STEP)
        d_ref[...] = -ADAM_LR * (m_hat / (jnp.sqrt(v_hat) + ADAM_EPS) + ADAM_WD * w_ref[...])
        nm_ref[...] = nm
        nv_ref[...] = nv

    blk = pl.BlockSpec((tr, C), lambda i: (i, 0))
    return pl.pallas_call(
        body, name=name, out_shape=(jax.ShapeDtypeStruct((R, C), F32),) * 3, grid=(R // tr,),
        in_specs=[blk] * 4, out_specs=(blk,) * 3,
    )(w, g, m, v)


ANY = pl.BlockSpec(memory_space=pl.ANY)


def _chip_peers():
    x, y, c = lax.axis_index("x"), lax.axis_index("y"), lax.axis_index("c")
    return x, y, c, [(1 - x, y), (x, 1 - y), (1 - x, 1 - y)]


def _half_rows(c, rh):
    return pl.ds(pl.multiple_of(c * rh, 16), rh)


def _my_chip():
    return 2 * lax.axis_index("x") + lax.axis_index("y")


def _gather_chips(wb, wf):
    rh = wb.shape[0] // 2
    rq = rh // 2

    def body(wb_ref, wf_ref, ob_ref, of_ref, send_sems, recv_sems):
        x, y, c, peers = _chip_peers()
        nbr_x, nbr_y = peers[0], peers[1]
        me, chip_x, chip_y, chip_d = 2 * x + y, 2 * (1 - x) + y, 2 * x + (1 - y), 2 * (1 - x) + (1 - y)

        def quarter(core, b):
            return pl.ds(pl.multiple_of(core * rh + b * rq, 16), rq)

        ici = [(0, nbr_x, me, 0, chip_x), (1, nbr_y, me, 1, chip_y), (2, nbr_y, me, 0, chip_y), (3, nbr_x, me, 1, chip_x),
               (4, nbr_y, chip_x, 0, chip_d), (5, nbr_x, chip_y, 1, chip_d)]

        def ici_copy(k, to, slot, b, own):
            rows = quarter(c, b)
            return pltpu.make_async_remote_copy(
                src_ref=wb_ref.at[rows] if own else ob_ref.at[slot, rows], dst_ref=ob_ref.at[slot, rows],
                send_sem=send_sems.at[k], recv_sem=recv_sems.at[k], device_id=(to[0], to[1], c), device_id_type=MESH)

        def to_sibling(k, slot, b, core):
            rows = quarter(core, b)
            return pltpu.make_async_remote_copy(
                src_ref=ob_ref.at[slot, rows], dst_ref=ob_ref.at[slot, rows], send_sem=send_sems.at[6 + k],
                recv_sem=recv_sems.at[6 + k], device_id=(x, y, 1 - c), device_id_type=MESH)

        def small_copy(k, slot):
            px, py = peers[k]
            return pltpu.make_async_remote_copy(
                src_ref=wf_ref, dst_ref=of_ref.at[slot], send_sem=send_sems.at[12 + k], recv_sem=recv_sems.at[12 + k],
                device_id=(px, py, c), device_id_type=MESH)

        sends = [ici_copy(k, to, slot, b, True) for k, to, slot, b, _ in ici[:4]] + [small_copy(k, me) for k in range(3)]
        for cp in sends:
            cp.start()
        for k, to, slot, b, arrives in ici:
            ici_copy(k, to, arrives, b, False).wait_recv()
            passed = [to_sibling(k, arrives, b, c)]
            if k < 2:
                passed.append(ici_copy(*ici[4 + k][:4], False))
            for cp in passed:
                cp.start()
            sends += passed
        for k, to, slot, b, arrives in ici:
            to_sibling(k, arrives, b, 1 - c).wait_recv()
        chip_of = [chip_x, chip_y, chip_d]
        for k in range(3):
            small_copy(k, chip_of[k]).wait_recv()
        for cp in sends:
            cp.wait_send()

    ob, of = pl.pallas_call(
        body, name="gather_weights",
        out_shape=(jax.ShapeDtypeStruct((4,) + wb.shape, wb.dtype), jax.ShapeDtypeStruct((4,) + wf.shape, wf.dtype)),
        in_specs=[ANY, ANY], out_specs=(ANY, ANY),
        scratch_shapes=[pltpu.SemaphoreType.DMA((15,)), pltpu.SemaphoreType.DMA((15,))],
    )(wb, wf)
    me = _my_chip()
    return lax.dynamic_update_slice(ob, wb[None], (me, 0, 0)), lax.dynamic_update_slice(of, wf[None], (me, 0, 0))


def _halves_to_sibling(gp):
    rh = gp.shape[1] // 2

    def body(gp_ref, o_ref, send_sem, recv_sem):
        x, y, c = lax.axis_index("x"), lax.axis_index("y"), lax.axis_index("c")
        cp = pltpu.make_async_remote_copy(src_ref=gp_ref.at[:, _half_rows(1 - c, rh), :], dst_ref=o_ref, send_sem=send_sem,
                                          recv_sem=recv_sem, device_id=(x, y, 1 - c), device_id_type=MESH)
        cp.start()
        cp.wait()

    return pl.pallas_call(
        body, name="halves_to_sibling", out_shape=jax.ShapeDtypeStruct((gp.shape[0], rh, gp.shape[2]), gp.dtype),
        in_specs=[ANY], out_specs=ANY, scratch_shapes=[pltpu.SemaphoreType.DMA, pltpu.SemaphoreType.DMA],
    )(gp)


def _row_tile(rows, cap=1024):
    tr = 16
    for cand in range(16, cap + 1, 16):
        if rows % cand == 0:
            tr = cand
    return tr


def _add_halves(gp, sib, core):
    n, rh, C = sib.shape
    tr = _row_tile(rh)
    nt = rh // tr

    def body(c_ref, g_ref, s_ref, o_ref):
        o_ref[...] = (g_ref[...].astype(F32) + s_ref[...].astype(F32)).astype(o_ref.dtype)

    blk = pl.BlockSpec((1, tr, C), lambda j, i, c: (j, i, 0))
    return pl.pallas_call(
        body, name="add_halves", out_shape=jax.ShapeDtypeStruct(sib.shape, sib.dtype),
        grid_spec=pltpu.PrefetchScalarGridSpec(
            num_scalar_prefetch=1, grid=(n, nt),
            in_specs=[pl.BlockSpec((1, tr, C), lambda j, i, c: (j, c[0] * nt + i, 0)), blk], out_specs=blk),
    )(core, gp, sib)


def _join_halves(buf):
    rh = buf.shape[0] // 2

    def body(in_ref, o_ref, send_sem, recv_sem):
        x, y, c = lax.axis_index("x"), lax.axis_index("y"), lax.axis_index("c")

        def copy(rows):
            return pltpu.make_async_remote_copy(src_ref=o_ref.at[rows], dst_ref=o_ref.at[rows], send_sem=send_sem,
                                                recv_sem=recv_sem, device_id=(x, y, 1 - c), device_id_type=MESH)

        send = copy(_half_rows(c, rh))
        send.start()
        copy(_half_rows(1 - c, rh)).wait_recv()
        send.wait_send()

    return pl.pallas_call(
        body, name="join_halves", out_shape=jax.ShapeDtypeStruct(buf.shape, buf.dtype),
        in_specs=[ANY], out_specs=ANY, input_output_aliases={0: 0},
        scratch_shapes=[pltpu.SemaphoreType.DMA, pltpu.SemaphoreType.DMA],
    )(buf)


def _exchange_near(gp):
    rq = gp.shape[1] // 2

    def body(gp_ref, out_ref, send_sems, recv_sems):
        x, y, c, peers = _chip_peers()
        chip_x, chip_y, chip_d = 2 * (1 - x) + y, 2 * x + (1 - y), 2 * (1 - x) + (1 - y)
        plan = [(peers[0], chip_x, 0), (peers[0], chip_d, 0), (peers[1], chip_y, 1), (peers[1], chip_d, 1)]
        copies = [pltpu.make_async_remote_copy(
            src_ref=gp_ref.at[slot, pl.ds(b * rq, rq)], dst_ref=out_ref.at[k], send_sem=send_sems.at[k],
            recv_sem=recv_sems.at[k], device_id=(to[0], to[1], c), device_id_type=MESH) for k, (to, slot, b) in enumerate(plan)]
        for cp in copies:
            cp.start()
        for cp in copies:
            cp.wait_recv()
        for cp in copies:
            cp.wait_send()

    return pl.pallas_call(
        body, name="exchange_grads_near", out_shape=jax.ShapeDtypeStruct((4, rq, gp.shape[2]), gp.dtype),
        in_specs=[ANY], out_specs=ANY, scratch_shapes=[pltpu.SemaphoreType.DMA((4,)), pltpu.SemaphoreType.DMA((4,))],
    )(gp)


def _add_near(gp, near, chips):
    _, rq, C = near.shape
    tr = _row_tile(rq)
    nt = rq // tr

    def body(ch_ref, mine_a, mine_b, on_a, on_b, near_ref, part_ref, on_ref):
        part_ref[0] = mine_a[0].astype(F32) + near_ref[0].astype(F32)
        part_ref[1] = mine_b[0].astype(F32) + near_ref[2].astype(F32)
        on_ref[0] = (on_a[0].astype(F32) + near_ref[1].astype(F32)).astype(on_ref.dtype)
        on_ref[1] = (on_b[0].astype(F32) + near_ref[3].astype(F32)).astype(on_ref.dtype)

    def slot(which, b):
        return pl.BlockSpec((1, tr, C), lambda i, ch: (ch[which], b * nt + i, 0))

    return pl.pallas_call(
        body, name="add_near",
        out_shape=(jax.ShapeDtypeStruct((2, rq, C), F32), jax.ShapeDtypeStruct((2, rq, C), near.dtype)),
        grid_spec=pltpu.PrefetchScalarGridSpec(
            num_scalar_prefetch=1, grid=(nt,),
            in_specs=[slot(0, 0), slot(0, 1), slot(2, 0), slot(1, 1), pl.BlockSpec((4, tr, C), lambda i, ch: (0, i, 0))],
            out_specs=(pl.BlockSpec((2, tr, C), lambda i, ch: (0, i, 0)),) * 2),
    )(chips, gp, gp, gp, gp, near)


def _exchange_far(on):
    def body(on_ref, out_ref, send_sems, recv_sems):
        x, y, c, peers = _chip_peers()
        copies = [pltpu.make_async_remote_copy(
            src_ref=on_ref.at[k], dst_ref=out_ref.at[k], send_sem=send_sems.at[k], recv_sem=recv_sems.at[k],
            device_id=(to[0], to[1], c), device_id_type=MESH) for k, to in enumerate((peers[1], peers[0]))]
        for cp in copies:
            cp.start()
        for cp in copies:
            cp.wait_recv()
        for cp in copies:
            cp.wait_send()

    return pl.pallas_call(
        body, name="exchange_grads_far", out_shape=jax.ShapeDtypeStruct(on.shape, on.dtype),
        in_specs=[ANY], out_specs=ANY, scratch_shapes=[pltpu.SemaphoreType.DMA((2,)), pltpu.SemaphoreType.DMA((2,))],
    )(on)


def _add_far(part, far, core):
    _, rq, C = part.shape
    tr = _row_tile(rq)
    nt = rq // tr

    def body(c_ref, p_ref, f_ref, o_ref):
        o_ref[...] = p_ref[0] + f_ref[0].astype(F32)

    blk = pl.BlockSpec((1, tr, C), lambda b, i, c: (b, i, 0))
    return pl.pallas_call(
        body, name="add_far", out_shape=jax.ShapeDtypeStruct((4 * rq, C), F32),
        grid_spec=pltpu.PrefetchScalarGridSpec(
            num_scalar_prefetch=1, grid=(2, nt), in_specs=[blk, blk],
            out_specs=pl.BlockSpec((tr, C), lambda b, i, c: ((2 * c[0] + b) * nt + i, 0))),
    )(core, part, far)


N_DEV = 8


def _allreduce_small(p):
    rs = p.shape[0]

    def body(x_ref, sum_ref, all_ref, send_sems, recv_sems, local_sem):
        x, y, c = lax.axis_index("x"), lax.axis_index("y"), lax.axis_index("c")
        me, sibling = (x, y, c), (x, y, 1 - c)
        chips = [(1 - x, y), (x, 1 - y), (1 - x, 1 - y)]

        def rows(px, py, pc):
            return all_ref.at[pl.ds((4 * px + 2 * py + pc) * rs, rs), :]

        def copy(k, block, to, src=None):
            return pltpu.make_async_remote_copy(
                src_ref=rows(*block) if src is None else src, dst_ref=rows(*block),
                send_sem=send_sems.at[k], recv_sem=recv_sems.at[k], device_id=to, device_id_type=MESH)

        mine = pltpu.make_async_copy(x_ref, rows(*me), local_sem)
        mine.start()
        first = [copy(0, me, sibling, src=x_ref)]
        first += [copy(1 + j, me, (*chip, c), src=x_ref) for j, chip in enumerate(chips)]
        for cp in first:
            cp.start()
        passed = [copy(4 + j, (*chip, c), sibling) for j, chip in enumerate(chips)]
        for j, chip in enumerate(chips):
            copy(1 + j, (*chip, c), me).wait_recv()
            passed[j].start()
        copy(0, sibling, me).wait_recv()
        for j, chip in enumerate(chips):
            copy(4 + j, (*chip, 1 - c), me).wait_recv()
        for cp in first + passed:
            cp.wait_send()
        mine.wait()
        acc = all_ref[0:rs, :]
        for d in range(1, N_DEV):
            acc = acc + all_ref[d * rs:(d + 1) * rs, :]
        sum_ref[...] = acc

    vmem = pl.BlockSpec(memory_space=pltpu.VMEM)
    return pl.pallas_call(
        body, name="allreduce_small", out_shape=jax.ShapeDtypeStruct((rs, 128), F32),
        in_specs=[vmem], out_specs=vmem,
        scratch_shapes=[pltpu.VMEM((N_DEV * rs, 128), F32), pltpu.SemaphoreType.DMA((7,)), pltpu.SemaphoreType.DMA((7,)),
                        pltpu.SemaphoreType.DMA],
    )(p)


WEIGHTS = ('ffn1_norm', 'ffn1_w_gate', 'ffn1_w_up', 'ffn1_w_down', 'mix_norm', 'w_in', 'q_a_norm', 'w_q_b',
           'kv_a_norm', 'w_kv_b', 'q_head_norm', 'k_head_norm', 'conv_w', 'conv_b', 'a_log_fwd', 'a_log_bwd',
           'dt_bias_fwd', 'dt_bias_bwd', 'd_skip', 'ssm_norm', 'w_attn_branch', 'w_ssm_branch', 'w_out',
           'ffn2_norm', 'ffn2_w_gate', 'ffn2_w_up', 'ffn2_w_down')
PACKED = (('ffn1_w_gate', (D_MODEL, D_FF), 1), ('ffn1_w_up', (D_MODEL, D_FF), 1), ('ffn1_w_down', (D_FF, D_MODEL), 0),
          ('w_in', (D_MODEL, sum(IN_SPLITS)), 1), ('w_q_b', (Q_LORA, N_HEADS * QK_HEAD), 1),
          ('w_kv_b', (KV_LORA, N_HEADS * (QK_NOPE + V_HEAD)), 1),
          ('w_attn_branch', (N_HEADS * V_HEAD, D_MODEL), 0), ('w_ssm_branch', (D_INNER, D_MODEL), 0),
          ('w_out', (D_MODEL, D_MODEL), 0),
          ('ffn2_w_gate', (D_MODEL, D_FF), 1), ('ffn2_w_up', (D_MODEL, D_FF), 1), ('ffn2_w_down', (D_FF, D_MODEL), 0))
PACK_W = 1024
N_CHIPS = 4
SMALL = (('ffn1_norm', 1024), ('mix_norm', 1024), ('q_a_norm', 384), ('kv_a_norm', 256), ('q_head_norm', 96),
         ('k_head_norm', 96), ('conv_b', 3072), ('a_log_fwd', 32), ('a_log_bwd', 32), ('dt_bias_fwd', 32),
         ('dt_bias_bwd', 32), ('d_skip', 32), ('ssm_norm', 2048), ('ffn2_norm', 1024),
         ('conv_w', CONV_WIDTH * XBC_DIM), ('loss', 1))


TRANSPOSED = ('ffn1_w_gate', 'ffn1_w_up', 'w_in', 'ffn2_w_gate', 'ffn2_w_up')


def _stored(name, a):
    return a.T if name in TRANSPOSED else a


def _shard_shape(name, shape, axis):
    sh = tuple(s // N_CHIPS if a == axis else s for a, s in enumerate(shape))
    return sh[::-1] if name in TRANSPOSED else sh


def _by_rows(name, axis):
    return name in TRANSPOSED or axis == 0


def _pack_layout():
    out, r = {}, 0
    for name, shape, axis in PACKED:
        n = math.prod(shape) // N_CHIPS // PACK_W
        out[name] = (r, n)
        r += n
    return out, -(-r // 64) * 64


def _pack(shards):
    layout, rows = _pack_layout()
    parts = [shards[name].reshape(-1, PACK_W) for name, _, _ in PACKED]
    parts.append(jnp.zeros((rows - sum(p.shape[0] for p in parts), PACK_W), parts[0].dtype))
    return jnp.concatenate(parts, axis=0)


def _unpack(packed):
    layout, _ = _pack_layout()
    return {name: packed[layout[name][0]:layout[name][0] + layout[name][1]].reshape(_shard_shape(name, shape, axis))
            for name, shape, axis in PACKED}


def _full_from_slots(slots):
    layout, _ = _pack_layout()
    out = {}
    for name, shape, axis in PACKED:
        r, n = layout[name]
        if _by_rows(name, axis):
            out[name] = slots[:, r:r + n].reshape(N_CHIPS * n, PACK_W)
        else:
            sh = _shard_shape(name, shape, axis)
            out[name] = jnp.concatenate([slots[j, r:r + n].reshape(sh) for j in range(N_CHIPS)], axis=axis)
    return out


def _slots_from_full(full):
    layout, rows = _pack_layout()
    parts = []
    for name, shape, axis in PACKED:
        r, n = layout[name]
        if _by_rows(name, axis):
            parts.append(full[name].reshape(N_CHIPS, n, PACK_W))
        else:
            size = shape[axis] // N_CHIPS
            parts.append(jnp.stack([lax.slice_in_dim(full[name], j * size, (j + 1) * size, axis=axis).reshape(n, PACK_W)
                                    for j in range(N_CHIPS)]))
    parts.append(jnp.zeros((N_CHIPS, rows - sum(p.shape[1] for p in parts), PACK_W), parts[0].dtype))
    return jnp.concatenate(parts, axis=1)


def _pack_small(vals):
    parts = []
    for name, n in SMALL:
        pad = -(-n // 128) * 128 - n
        parts.append(jnp.pad(vals[name].reshape(-1).astype(F32), (0, pad)).reshape(-1, 128))
    rows = sum(p.shape[0] for p in parts)
    parts.append(jnp.zeros((-(-rows // 8) * 8 - rows, 128), F32))
    return jnp.concatenate(parts, axis=0)


def _unpack_small(packed):
    out, r = {}, 0
    for name, n in SMALL:
        k = -(-n // 128)
        out[name] = packed[r:r + k].reshape(-1)[:n]
        r += k
    return out


def _pad_heads(w, axis, per_head, lo, hi):
    shape = w.shape
    w = w.reshape(shape[:axis] + (N_HEADS, per_head) + shape[axis + 1:])
    w = lax.slice_in_dim(w, lo, hi, axis=axis + 1)
    pad = [(0, 0)] * w.ndim
    pad[axis + 1] = (0, HP - (hi - lo))
    w = jnp.pad(w, pad)
    return w.reshape(shape[:axis] + (N_HEADS * HP,) + shape[axis + 1:])


def _unpad_heads(w, axis, keep):
    shape = w.shape
    w = w.reshape(shape[:axis] + (N_HEADS, HP) + shape[axis + 1:])
    return lax.slice_in_dim(w, 0, keep, axis=axis + 1)


def _pad_w_in(wt):
    o = [0]
    for s in IN_SPLITS:
        o.append(o[-1] + s)
    cq, ckv, kpe, z, xbc, dtf, dtb, ga, gb = [wt[o[i]:o[i + 1]] for i in range(len(IN_SPLITS))]
    kpe_pad = jnp.pad(kpe, ((QK_NOPE, HP - QK_HEAD), (0, 0)))
    dt_pad = jnp.pad(jnp.concatenate([dtf, dtb], axis=0), ((0, HP - 2 * SSM_HEADS), (0, 0)))
    return jnp.concatenate([z, ga, gb, xbc, cq, ckv, kpe_pad, dt_pad], axis=0)


def _unpad_w_in(gt):
    z, ga, gb, xbc = gt[U_Z:U_GA], gt[U_GA:U_GB], gt[U_GB:U_XBC], gt[U_XBC:U_SMALL]
    s = gt[U_SMALL:]
    cq, ckv = s[S_CQ:S_CKV], s[S_CKV:S_KPE]
    kpe = s[S_KPE + QK_NOPE:S_KPE + QK_HEAD]
    dtf, dtb = s[S_DT:S_DT + SSM_HEADS], s[S_DT + SSM_HEADS:S_DT + 2 * SSM_HEADS]
    return jnp.concatenate([cq, ckv, kpe, z, xbc, dtf, dtb, ga, gb], axis=0)


def _lanes128(parts):
    row = jnp.concatenate([p.reshape(-1) for p in parts])
    return jnp.pad(row, (0, HP - row.shape[0])).reshape(1, HP)


FF_TILE = D_FF // 2
WGRAD = BF


def _ffn_fwd(x, g, wg_t, wu_t, wd, tag):
    h = _rms_fwd(x, g, name=tag + "_norm")
    gate, up, act = _mm([h], [wg_t, wu_t], name=tag + "_up", tb=True, out_dtypes=(BF, BF, BF), tm=512, tn=FF_TILE,
                        epilogue=lambda a, b: (a, b, _silu(a) * b))
    out = _mm([act], [wd], name=tag + "_down", extras=[x], epilogue=lambda acc, r: (r + 0.5 * acc,))
    return out, (h, gate, up, act)


def _ffn_bwd(dout, dout_bf, x, g, wg_t, wu_t, wd, saved, tag):
    h, gate, up, act = saved

    def swiglu_bwd(acc, a, b):
        a, b, half = a.astype(F32), b.astype(F32), 0.5 * acc
        s = jax.nn.sigmoid(a)
        return half * b * (s * (1.0 + a * (1.0 - s))), half * (a * s)

    dgate, dup = _mm([dout_bf], [wd], name=tag + "_down_dx", tb=True, extras=[gate, up], out_dtypes=(BF, BF),
                     tm=512, tn=FF_TILE, epilogue=swiglu_bwd)
    dwd = _mm([act], [dout_bf], name=tag + "_down_dw", ta=True, tm=FF_TILE, tk=1024, out_dtypes=(WGRAD,),
              epilogue=lambda acc: (0.5 * acc,))
    dwg_t, dwu_t = _mm([dgate, dup], [h, h], name=tag + "_up_dw", ta=True, separate=True, out_dtypes=(WGRAD, WGRAD),
                       tm=FF_TILE, tk=1024)
    dh = _mm([dgate, dup], [wg_t, wu_t], name=tag + "_up_dx")
    dx, dx_bf, dg = _rms_bwd(dh, x, g, name=tag + "_norm_bwd", add=dout, out_dtypes=(F32, BF))
    return dx, dx_bf, dg, dwg_t, dwu_t, dwd


KPE_BLK = (U_SMALL + S_KPE) // HP
SMALL_BLK = U_SMALL // SMALL_W


def _local_step(x, pos_col, target, W, P):
    T = x.shape[0]
    sig = jax.nn.sigmoid
    x1, ffn1 = _ffn_fwd(x, P["ffn1_norm"], W["wg1"], W["wu1"], W["wd1"], "ffn1")
    h = _rms_fwd(x1, P["mix_norm"], name="mix_norm")
    u = _mm([h], [W["w_in"]], name="in_proj", tb=True, tn=1152)
    cqn = _rms_fwd(u, P["q_a_norm"], name="q_a_norm", blk_w=SMALL_W, blk_idx=SMALL_BLK, off=S_CQ, width=Q_LORA)
    ckvn = _rms_fwd(u, P["kv_a_norm"], name="kv_a_norm", blk_w=SMALL_W, blk_idx=SMALL_BLK, off=S_CKV, width=KV_LORA)
    q_raw = _mm([cqn], [W["wq"]], name="q_proj")
    def with_ones_lane(acc_k, acc_v):
        lane = lax.broadcasted_iota(jnp.int32, acc_v.shape, 1)
        return acc_k, jnp.where((lane & (HP - 1)) == V_HEAD, 1.0, acc_v)

    k_raw, v = _mm([ckvn], [W["wk"], W["wv"]], name="kv_proj", out_dtypes=(F32, BF), epilogue=with_ones_lane)
    rc, rs = _rope_tables(pos_col, P["freq"])
    q = _qk_prep_fwd(q_raw, None, P["q_head_norm"], rc, rs, name="q_prep", out_scale=Q_SCALE)
    k = _qk_prep_fwd(k_raw, u, P["k_head_norm"], rc, rs, name="k_prep", kpe_blk=KPE_BLK)
    o, lse = _attn_fwd(q, k, v)
    pre, act = _conv_fwd(u, P["conv_w8"], P["conv_b"])
    scan_cols, scan_rows = _ssd_prep(u, P["dt_bias8"], P["a_log8"])
    y_f, st_f = _ssd_fwd(act, scan_cols, scan_rows, rev=False, name="ssd_fwd_f")
    y_b, st_b = _ssd_fwd(act, scan_cols, scan_rows, rev=True, name="ssd_fwd_b")
    ysum, m = _ssm_combine_fwd(y_f, y_b, act, u, P["d_skip_lanes"], P["ssm_norm"])
    ab = _mm([o], [W["pa"]], name="attn_branch")
    mb, merged = _mm([m], [W["pb"]], name="ssm_branch", extras=[ab, u, u], extra_offs=(0, U_GA, U_GB), out_dtypes=(F32, BF),
                     epilogue=lambda acc, a, ga, gb: (acc, sig(ga) * a + sig(gb) * acc))
    x2 = _mm([merged], [W["wo"]], name="out_proj", extras=[x1], epilogue=lambda acc, r: (r + acc,))
    y, ffn2 = _ffn_fwd(x2, P["ffn2_norm"], W["wg2"], W["wu2"], W["wd2"], "ffn2")
    dy, dy_bf, loss = _loss_head(y, target)
    dx2, dx2_bf, dg_ffn2, dwg2, dwu2, dwd2 = _ffn_bwd(dy, dy_bf, x2, P["ffn2_norm"], W["wg2"], W["wu2"], W["wd2"], ffn2,
                                                      "ffn2")

    def gate_bwd(dmrg, a, b, ga, gb):
        sa, sb = sig(ga), sig(gb)
        return dmrg * sa, dmrg * sb, dmrg * a * sa * (1.0 - sa), dmrg * b * sb * (1.0 - sb)

    dab, dmb, dga, dgb = _mm([dx2_bf], [W["wo"]], name="out_proj_dx", tb=True, extras=[ab, mb, u, u],
                             extra_offs=(0, 0, U_GA, U_GB), out_dtypes=(BF,) * 4, epilogue=gate_bwd)
    dwo = _mm([merged], [dx2_bf], name="out_proj_dw", ta=True, out_dtypes=(WGRAD,))
    dpa = _mm([o], [dab], name="attn_branch_dw", ta=True, out_dtypes=(WGRAD,))
    do = _mm([dab], [W["pa"]], name="attn_branch_dx", tb=True)
    dpb = _mm([m], [dmb], name="ssm_branch_dw", ta=True, out_dtypes=(WGRAD,))
    dm = _mm([dmb], [W["pb"]], name="ssm_branch_dx", tb=True)
    dyssd, dz, dxs_skip, dg_ssm, dskip = _ssm_combine_bwd(dm, ysum, act, u, P["d_skip_lanes"], P["ssm_norm"])
    dxs_f, db_f, dc_f, dsel_f, dtot_f = _ssd_bwd(act, scan_cols, scan_rows, st_f, dyssd, rev=False, name="ssd_bwd_f")
    dxs_b, db_b, dc_b, dsel_b, dtot_b = _ssd_bwd(act, scan_cols, scan_rows, st_b, dyssd, rev=True, name="ssd_bwd_b")
    ddt, dalog, dbias = _ssd_prep_bwd(u, P["dt_bias8"], P["a_log8"], dsel_f, dtot_f, dsel_b, dtot_b)
    dxbc, dconv = [], []
    for tag, col0, parts in (("x", 0, [dxs_f, dxs_b, dxs_skip]), ("b", D_INNER, [db_f, db_b]),
                             ("c", D_INNER + SSM_GROUPS * D_STATE, [dc_f, dc_b])):
        dpre = _conv_dpre(parts, pre, col0, name="conv_dpre_" + tag)
        dxp, dwp = _conv_bwd(dpre, u, P["conv_w8"], col0, name="conv_bwd_" + tag)
        dxbc.append(dxp)
        dconv.append(dwp)
    dconv = jnp.concatenate(dconv, axis=1)
    dq, dk, dv = _attn_bwd(q, k, v, do, o, lse)
    dq_raw, dg_qh = _qk_prep_bwd(dq, q_raw, None, P["q_head_norm"], rc, rs, name="q_prep_bwd", in_scale=ATTN_SCALE)
    dk_raw, dg_kh, dkpe = _qk_prep_bwd(dk, k_raw, u, P["k_head_norm"], rc, rs, name="k_prep_bwd", kpe_blk=KPE_BLK,
                                       in_scale=1.0 / LOG2E)
    dwq = _mm([cqn], [dq_raw], name="q_proj_dw", ta=True, out_dtypes=(WGRAD,))
    dcqn = _mm([dq_raw], [W["wq"]], name="q_proj_dx", tb=True)
    dwk, dwv = _mm([ckvn], [dk_raw, dv], name="kv_proj_dw", ta=True, out_dtypes=(WGRAD, WGRAD))
    dckvn = _mm([dk_raw, dv], [W["wk"], W["wv"]], name="kv_proj_dx", tb=True)
    dcq, dg_qa = _rms_bwd(dcqn, u, P["q_a_norm"], name="q_a_norm_bwd", blk_w=SMALL_W, blk_idx=SMALL_BLK, off=S_CQ,
                          width=Q_LORA, out_dtypes=(BF,))
    dckv, dg_kva = _rms_bwd(dckvn, u, P["kv_a_norm"], name="kv_a_norm_bwd", blk_w=SMALL_W, blk_idx=SMALL_BLK,
                            off=S_CKV, width=KV_LORA, out_dtypes=(BF,))
    du = jnp.concatenate([dz, dga, dgb] + dxbc + [dcq, dckv, dkpe.astype(BF), ddt.astype(BF)], axis=1)
    dw_in = _mm([du], [h], name="in_proj_dw", ta=True, tm=1152, out_dtypes=(WGRAD,))
    dh = _mm([du], [W["w_in"]], name="in_proj_dx", tk=U_PAD // 3)
    dx1, dx1_bf, dg_mix = _rms_bwd(dh, x1, P["mix_norm"], name="mix_norm_bwd", add=dx2, out_dtypes=(F32, BF))
    dx, _, dg_ffn1, dwg1, dwu1, dwd1 = _ffn_bwd(dx1, dx1_bf, x, P["ffn1_norm"], W["wg1"], W["wu1"], W["wd1"], ffn1, "ffn1")
    dW = dict(wg1=dwg1, wu1=dwu1, wd1=dwd1, w_in=dw_in, wq=dwq, wk=dwk, wv=dwv, pa=dpa, pb=dpb, wo=dwo,
              wg2=dwg2, wu2=dwu2, wd2=dwd2)
    dP = dict(ffn1_norm=dg_ffn1[0], mix_norm=dg_mix[0], q_a_norm=dg_qa[0], kv_a_norm=dg_kva[0],
              q_head_norm=dg_qh[0, :QK_HEAD], k_head_norm=dg_kh[0, :QK_HEAD], conv_b=dconv[CONV_WIDTH],
              a_log_fwd=dalog[0, :SSM_HEADS], a_log_bwd=dalog[0, SSM_HEADS:2 * SSM_HEADS],
              dt_bias_fwd=dbias[0, :SSM_HEADS], dt_bias_bwd=dbias[0, SSM_HEADS:2 * SSM_HEADS],
              d_skip=dskip[0].reshape(SSM_GROUPS, HP)[:, :HG], ssm_norm=dg_ssm[0], ffn2_norm=dg_ffn2[0],
              conv_w=dconv[:CONV_WIDTH], loss=loss[0, 0])
    return dx, dW, dP


def _prepare(w, conv_w_full):
    kvb = w["w_kv_b"]
    W = dict(wg1=w["ffn1_w_gate"], wu1=w["ffn1_w_up"], wd1=w["ffn1_w_down"], w_in=_pad_w_in(w["w_in"]),
             wq=_pad_heads(w["w_q_b"], 1, QK_HEAD, 0, QK_HEAD),
             wk=_pad_heads(kvb, 1, QK_NOPE + V_HEAD, 0, QK_NOPE),
             wv=_pad_heads(kvb, 1, QK_NOPE + V_HEAD, QK_NOPE, QK_NOPE + V_HEAD),
             pa=_pad_heads(w["w_attn_branch"], 0, V_HEAD, 0, V_HEAD), pb=w["w_ssm_branch"], wo=w["w_out"],
             wg2=w["ffn2_w_gate"], wu2=w["ffn2_w_up"], wd2=w["ffn2_w_down"])
    inv_freq = [1.0 / (ROPE_BASE ** (j / QK_ROPE)) for j in range(0, QK_ROPE, 2)]
    freq = [0.0] * QK_NOPE + inv_freq + inv_freq + [0.0] * (HP - QK_HEAD)
    P = {n: w[n] for n in ("ffn1_norm", "mix_norm", "q_a_norm", "kv_a_norm", "ssm_norm", "ffn2_norm", "conv_b")}
    P.update(q_head_norm=_lanes128([w["q_head_norm"]]), k_head_norm=_lanes128([w["k_head_norm"]]),
             conv_w8=jnp.pad(conv_w_full, ((0, 8 - CONV_WIDTH), (0, 0))),
             dt_bias8=jnp.broadcast_to(_lanes128([w["dt_bias_fwd"], w["dt_bias_bwd"]]), (8, HP)),
             a_log8=jnp.broadcast_to(_lanes128([w["a_log_fwd"], w["a_log_bwd"]]), (8, HP)),
             d_skip_lanes=jnp.repeat(w["d_skip"].reshape(-1), PH).reshape(1, D_INNER),
             freq=jnp.asarray(freq, F32).reshape(1, HP))
    return W, P


def _unprepare(dW):
    dkvb = jnp.concatenate([_unpad_heads(dW["wk"], 1, QK_NOPE), _unpad_heads(dW["wv"], 1, V_HEAD)], axis=2)
    return dict(ffn1_w_gate=dW["wg1"], ffn1_w_up=dW["wu1"], ffn1_w_down=dW["wd1"], w_in=_unpad_w_in(dW["w_in"]),
                w_q_b=_unpad_heads(dW["wq"], 1, QK_HEAD).reshape(Q_LORA, N_HEADS * QK_HEAD),
                w_kv_b=dkvb.reshape(KV_LORA, N_HEADS * (QK_NOPE + V_HEAD)),
                w_attn_branch=_unpad_heads(dW["pa"], 0, V_HEAD).reshape(N_HEADS * V_HEAD, D_MODEL),
                w_ssm_branch=dW["pb"], w_out=dW["wo"],
                ffn2_w_gate=dW["wg2"], ffn2_w_up=dW["wu2"], ffn2_w_down=dW["wd2"])


def kernel(x, positions, ffn1_norm, ffn1_w_gate, ffn1_w_up, ffn1_w_down, mix_norm, w_in, q_a_norm, w_q_b, kv_a_norm, w_kv_b, q_head_norm, k_head_norm, conv_w, conv_b, a_log_fwd, a_log_bwd, dt_bias_fwd, dt_bias_bwd, d_skip, ssm_norm, w_attn_branch, w_ssm_branch, w_out, ffn2_norm, ffn2_w_gate, ffn2_w_up, ffn2_w_down, loss_target, m_ffn1_norm, m_ffn1_w_gate, m_ffn1_w_up, m_ffn1_w_down, m_mix_norm, m_w_in, m_q_a_norm, m_w_q_b, m_kv_a_norm, m_w_kv_b, m_q_head_norm, m_k_head_norm, m_conv_w, m_conv_b, m_a_log_fwd, m_a_log_bwd, m_dt_bias_fwd, m_dt_bias_bwd, m_d_skip, m_ssm_norm, m_w_attn_branch, m_w_ssm_branch, m_w_out, m_ffn2_norm, m_ffn2_w_gate, m_ffn2_w_up, m_ffn2_w_down, v_ffn1_norm, v_ffn1_w_gate, v_ffn1_w_up, v_ffn1_w_down, v_mix_norm, v_w_in, v_q_a_norm, v_w_q_b, v_kv_a_norm, v_w_kv_b, v_q_head_norm, v_k_head_norm, v_conv_w, v_conv_b, v_a_log_fwd, v_a_log_bwd, v_dt_bias_fwd, v_dt_bias_bwd, v_d_skip, v_ssm_norm, v_w_attn_branch, v_w_ssm_branch, v_w_out, v_ffn2_norm, v_ffn2_w_gate, v_ffn2_w_up, v_ffn2_w_down):
    given = dict(locals())
    T = x.shape[1]
    packed_names = [name for name, _, _ in PACKED]

    def two_d(a):
        return a.reshape(a.shape[1], -1) if a.ndim > 2 else a

    def kept(n, a):
        return _stored(n, two_d(a))

    w_loc = {n: kept(n, given[n]) for n in WEIGHTS}
    wb = _pack({n: w_loc[n].astype(BF) for n in packed_names})
    wf = jnp.pad(w_loc["conv_w"], ((0, 8 - CONV_WIDTH), (0, 0)))
    gb, gf = _gather_chips(wb, wf)
    full = _full_from_slots(gb)
    conv_w_full = jnp.concatenate([gf[j, :CONV_WIDTH] for j in range(N_CHIPS)], axis=1)
    full.update({n: w_loc[n] for n in WEIGHTS if n not in full and n != "conv_w"})
    W, P = _prepare(full, conv_w_full)
    dx, dW, dP = _local_step(x.reshape(T, D_MODEL), positions.reshape(T, 1).astype(F32), loss_target.reshape(T, D_MODEL), W, P)
    gp = _slots_from_full(_unprepare(dW))
    core = lax.axis_index("c").astype(jnp.int32).reshape(1)
    both_cores = _add_halves(gp, _halves_to_sibling(gp), core)
    cx, cy = lax.axis_index("x"), lax.axis_index("y")
    chips = jnp.stack([2 * cx + cy, 2 * (1 - cx) + cy, 2 * cx + (1 - cy)]).astype(jnp.int32)
    part, on = _add_near(both_cores, _exchange_near(both_cores), chips)
    grads = _unpack(_join_halves(_add_far(part, _exchange_far(on), core)))
    small = _unpack_small(_allreduce_small(_pack_small(dP)))
    grads.update({n: small[n].reshape(1, -1) for n, _ in SMALL if n not in ("conv_w", "loss")})
    grads["conv_w"] = lax.dynamic_slice_in_dim(small["conv_w"].reshape(CONV_WIDTH, XBC_DIM), _my_chip() * (XBC_DIM // N_CHIPS),
                                               XBC_DIM // N_CHIPS, axis=1)
    out_g, out_d, out_m, out_v = [], [], [], []
    for n in WEIGHTS:
        shape = given[n].shape
        delta, new_m, new_v = _adamw(w_loc[n], grads[n], kept(n, given["m_" + n]), kept(n, given["v_" + n]), name="adamw_" + n)
        for outs, a in ((out_g, grads[n]), (out_d, delta), (out_m, new_m), (out_v, new_v)):
            outs.append(_stored(n, a).reshape(shape))
    return (small["loss"].reshape(()), dx.reshape(x.shape), *out_g, *out_d, *out_m, *out_v)
```

```python
import functools
import math

import jax
import jax.numpy as jnp
from jax import lax
from jax.experimental import pallas as pl
from jax.experimental.pallas import tpu as pltpu

BF = jnp.bfloat16
F32 = jnp.float32
HI = lax.Precision.HIGHEST
MESH = pl.DeviceIdType.MESH

D_MODEL = 1024
D_FF = 2816
EPS = 1e-6
N_HEADS = 16
QK_NOPE = 64
QK_ROPE = 32
QK_HEAD = 96
V_HEAD = 64
Q_LORA = 384
KV_LORA = 256
ROPE_BASE = 10000.0
D_INNER = 2048
SSM_HEADS = 32
SSM_GROUPS = 4
D_STATE = 128
CONV_WIDTH = 5
CHUNK = 128
XBC_DIM = 3072
HP = 128
GW = D_INNER // SSM_GROUPS
HG = SSM_HEADS // SSM_GROUPS
PH = 64
U_Z, U_GA, U_GB, U_XBC, U_SMALL = 0, 2048, 3072, 4096, 7168
S_CQ, S_CKV, S_KPE, S_DT, SMALL_W = 0, 384, 640, 768, 896
U_PAD = U_SMALL + SMALL_W
IN_SPLITS = (Q_LORA, KV_LORA, QK_ROPE, D_INNER, XBC_DIM, SSM_HEADS, SSM_HEADS, D_MODEL, D_MODEL)

ADAM_LR = 0.001
ADAM_B1 = 0.9
ADAM_B2 = 0.999
ADAM_EPS = 1e-08
ADAM_WD = 0.01
ADAM_STEP = 10

NN = (((1,), (0,)), ((), ()))
NT = (((1,), (1,)), ((), ()))
TN = (((0,), (0,)), ((), ()))


def _pick(n, pref):
    best = None
    d = 128
    while d <= min(n, pref):
        if n % d == 0:
            best = d
        d += 128
    return best if best is not None else n


def _silu(x):
    return x * jax.nn.sigmoid(x)


def _dsilu(x):
    s = jax.nn.sigmoid(x)
    return s * (1.0 + x * (1.0 - s))


def _softplus(x):
    return jnp.maximum(x, 0.0) + jnp.log(1.0 + jnp.exp(-jnp.abs(x)))


def _mm(As, Bs, *, name, ta=False, tb=False, out_dtypes=(F32,), epilogue=None, extras=(), extra_offs=None,
        tm=1024, tn=512, tk=2048, separate=False):
    As, Bs, extras = list(As), list(Bs), list(extras)
    a0, b0 = As[0], Bs[0]
    M, K = (a0.shape[1], a0.shape[0]) if ta else a0.shape
    N = b0.shape[0] if tb else b0.shape[1]
    tm, tn, tk = _pick(M, tm), _pick(N, tn), _pick(K, tk)
    nk = K // tk
    n_a, n_b, n_e, n_o = len(As), len(Bs), len(extras), len(out_dtypes)
    n_acc = (n_b if n_a == 1 or separate else 1) if nk > 1 else 0
    if extra_offs is None:
        extra_offs = (0,) * n_e
    dn = (((0,) if ta else (1,), (1,) if tb else (0,)), ((), ()))
    bytes_a = sum(a.size * a.dtype.itemsize for a in As)
    bytes_b = sum(b.size * b.dtype.itemsize for b in Bs)
    n_outer = (N // tn) * bytes_a + bytes_b < (M // tm) * bytes_b + bytes_a

    def products(a_refs, b_refs):
        if n_a == 1:
            a = a_refs[0][...].astype(BF)
            return [lax.dot_general(a, b[...].astype(BF), dn, preferred_element_type=F32) for b in b_refs]
        if separate:
            return [lax.dot_general(a[...].astype(BF), b[...].astype(BF), dn, preferred_element_type=F32)
                    for a, b in zip(a_refs, b_refs)]
        total = None
        for a, b in zip(a_refs, b_refs):
            p = lax.dot_general(a[...].astype(BF), b[...].astype(BF), dn, preferred_element_type=F32)
            total = p if total is None else total + p
        return [total]

    def finish(accs, e_refs, o_refs):
        ex = [e[...] for e in e_refs]
        outs = epilogue(*accs, *ex) if epilogue is not None else tuple(accs)
        for o_ref, val in zip(o_refs, outs):
            o_ref[...] = val.astype(o_ref.dtype)

    def body(*refs):
        a_refs, b_refs = refs[:n_a], refs[n_a:n_a + n_b]
        e_refs = refs[n_a + n_b:n_a + n_b + n_e]
        o_refs = refs[n_a + n_b + n_e:n_a + n_b + n_e + n_o]
        acc_refs = refs[n_a + n_b + n_e + n_o:]
        if nk == 1:
            finish(products(a_refs, b_refs), e_refs, o_refs)
            return
        k = pl.program_id(2)

        @pl.when(k == 0)
        def _():
            for acc in acc_refs:
                acc[...] = jnp.zeros_like(acc)

        for acc, p in zip(acc_refs, products(a_refs, b_refs)):
            acc[...] += p

        @pl.when(k == nk - 1)
        def _():
            finish([acc[...] for acc in acc_refs], e_refs, o_refs)

    def at(f):
        return (lambda j, i, k: f(i, j, k)) if n_outer else f

    a_spec = pl.BlockSpec((tk, tm), at(lambda i, j, k: (k, i))) if ta else pl.BlockSpec((tm, tk), at(lambda i, j, k: (i, k)))
    b_spec = pl.BlockSpec((tn, tk), at(lambda i, j, k: (j, k))) if tb else pl.BlockSpec((tk, tn), at(lambda i, j, k: (k, j)))
    e_specs = [pl.BlockSpec((tm, tn), at(functools.partial(lambda i, j, k, o: (i, j + o), o=off // tn))) for off in extra_offs]
    for off in extra_offs:
        assert off % tn == 0
    outs = pl.pallas_call(
        body, name=name,
        out_shape=tuple(jax.ShapeDtypeStruct((M, N), dt) for dt in out_dtypes),
        grid=(N // tn, M // tm, nk) if n_outer else (M // tm, N // tn, nk),
        in_specs=[a_spec] * n_a + [b_spec] * n_b + e_specs,
        out_specs=tuple(pl.BlockSpec((tm, tn), at(lambda i, j, k: (i, j))) for _ in out_dtypes),
        scratch_shapes=[pltpu.VMEM((tm, tn), F32)] * n_acc,
        compiler_params=pltpu.CompilerParams(dimension_semantics=("parallel", "parallel", "arbitrary")),
    )(*As, *Bs, *extras)
    return outs[0] if n_o == 1 else outs


def _rms_fwd(x, g, *, name, blk_w=None, blk_idx=0, off=0, width=None, out_dtype=BF):
    T = x.shape[0]
    blk_w = x.shape[1] if blk_w is None else blk_w
    width = blk_w if width is None else width
    tt = _pick(T, 512)

    def body(x_ref, g_ref, o_ref):
        xf = x_ref[:, off:off + width]
        r = lax.rsqrt(jnp.mean(xf * xf, axis=-1, keepdims=True) + EPS)
        o_ref[...] = (xf * r * g_ref[...]).astype(o_ref.dtype)

    return pl.pallas_call(
        body, name=name, out_shape=jax.ShapeDtypeStruct((T, width), out_dtype), grid=(T // tt,),
        in_specs=[pl.BlockSpec((tt, blk_w), lambda i: (i, blk_idx)), pl.BlockSpec((1, width), lambda i: (0, 0))],
        out_specs=pl.BlockSpec((tt, width), lambda i: (i, 0)),
    )(x, g)


def _rms_bwd(dy, x, g, *, name, blk_w=None, blk_idx=0, off=0, width=None, add=None, out_dtypes=(F32,)):
    T = x.shape[0]
    blk_w = x.shape[1] if blk_w is None else blk_w
    width = blk_w if width is None else width
    tt = _pick(T, 512)
    has_add = add is not None
    n_dx = len(out_dtypes)

    def body(*refs):
        dy_ref, x_ref, g_ref = refs[:3]
        dx_refs, dg_ref = refs[3 + has_add:3 + has_add + n_dx], refs[-1]
        xf = x_ref[:, off:off + width]
        d = dy_ref[...].astype(F32)
        r = lax.rsqrt(jnp.mean(xf * xf, axis=-1, keepdims=True) + EPS)
        gd = d * g_ref[...]
        dx = r * gd - xf * (r * r * r) * jnp.mean(gd * xf, axis=-1, keepdims=True)
        if has_add:
            dx = dx + refs[3][...]
        for dx_ref in dx_refs:
            dx_ref[...] = dx.astype(dx_ref.dtype)

        @pl.when(pl.program_id(0) == 0)
        def _():
            dg_ref[...] = jnp.zeros_like(dg_ref)

        dg_ref[...] += jnp.broadcast_to(jnp.sum(d * xf * r, axis=0, keepdims=True), dg_ref.shape)

    row = pl.BlockSpec((tt, width), lambda i: (i, 0))
    in_specs = [row, pl.BlockSpec((tt, blk_w), lambda i: (i, blk_idx)), pl.BlockSpec((1, width), lambda i: (0, 0))]
    args = [dy, x, g]
    if has_add:
        in_specs.append(row)
        args.append(add)
    return pl.pallas_call(
        body, name=name,
        out_shape=tuple(jax.ShapeDtypeStruct((T, width), dt) for dt in out_dtypes) + (jax.ShapeDtypeStruct((8, width), F32),),
        grid=(T // tt,), in_specs=in_specs,
        out_specs=(row,) * n_dx + (pl.BlockSpec((8, width), lambda i: (0, 0)),),
        compiler_params=pltpu.CompilerParams(dimension_semantics=("arbitrary",)),
    )(*args)


def _rope_tables(pos_col, freq_lane):
    T = pos_col.shape[0]
    tt = _pick(T, 512)

    def body(p_ref, f_ref, c_ref, s_ref):
        ang = p_ref[...] * f_ref[...]
        lane = lax.broadcasted_iota(jnp.int32, ang.shape, 1)
        c_ref[...] = jnp.where(lane < QK_HEAD, jnp.cos(ang), 0.0)
        sn = jnp.sin(ang)
        s_ref[...] = jnp.where((lane >= QK_NOPE) & (lane < QK_NOPE + 16), -sn,
                               jnp.where((lane >= QK_NOPE + 16) & (lane < QK_HEAD), sn, 0.0))

    return pl.pallas_call(
        body, name="rope_tables", out_shape=(jax.ShapeDtypeStruct((T, HP), F32),) * 2, grid=(T // tt,),
        in_specs=[pl.BlockSpec((tt, 1), lambda i: (i, 0)), pl.BlockSpec((1, HP), lambda i: (0, 0))],
        out_specs=(pl.BlockSpec((tt, HP), lambda i: (i, 0)),) * 2,
    )(pos_col, freq_lane)


def _swap_rope_halves(n):
    src = lax.broadcasted_iota(jnp.int32, (HP, HP), 0)
    dst = lax.broadcasted_iota(jnp.int32, (HP, HP), 1)
    lo = (dst >= QK_NOPE) & (dst < QK_NOPE + 16) & (src == dst + 16)
    hi = (dst >= QK_NOPE + 16) & (dst < QK_HEAD) & (src == dst - 16)
    return _split_dot(n, jnp.where(lo | hi, 1.0, 0.0).astype(BF), 2)


def _qk_prep_fwd(raw, kpe, gain, C, S, *, name, kpe_blk=0, out_scale=1.0):
    T = raw.shape[0]
    tt = _pick(T, 256)
    has_kpe = kpe is not None

    def body(*refs):
        if has_kpe:
            raw_ref, kpe_ref, g_ref, c_ref, s_ref, o_ref = refs
        else:
            raw_ref, g_ref, c_ref, s_ref, o_ref = refs
        for h in range(N_HEADS):
            hs = slice(HP * h, HP * (h + 1))
            xr = raw_ref[:, hs] + kpe_ref[...] if has_kpe else raw_ref[:, hs]
            r = lax.rsqrt(jnp.sum(xr * xr, axis=-1, keepdims=True) * (1.0 / QK_HEAD) + EPS)
            n = xr * r * g_ref[...]
            o_ref[:, hs] = ((n * c_ref[...] + _swap_rope_halves(n) * s_ref[...]) * out_scale).astype(o_ref.dtype)

    heads = pl.BlockSpec((tt, N_HEADS * HP), lambda i: (i, 0))
    shared = pl.BlockSpec((tt, HP), lambda i: (i, 0))
    kpe_spec = pl.BlockSpec((tt, HP), lambda i: (i, kpe_blk))
    in_specs = [heads] + ([kpe_spec] if has_kpe else []) + [pl.BlockSpec((1, HP), lambda i: (0, 0)), shared, shared]
    args = [raw] + ([kpe] if has_kpe else []) + [gain, C, S]
    return pl.pallas_call(
        body, name=name, out_shape=jax.ShapeDtypeStruct(raw.shape, BF), grid=(T // tt,),
        in_specs=in_specs, out_specs=heads,
    )(*args)


def _qk_prep_bwd(dout, raw, kpe, gain, C, S, *, name, kpe_blk=0, in_scale=1.0):
    T = raw.shape[0]
    tt = _pick(T, 256)
    has_kpe = kpe is not None

    def body(*refs):
        if has_kpe:
            d_ref, raw_ref, kpe_ref, g_ref, c_ref, s_ref, dx_ref, dg_ref, dkpe_ref = refs
        else:
            d_ref, raw_ref, g_ref, c_ref, s_ref, dx_ref, dg_ref = refs
        dg = jnp.zeros((1, HP), F32)
        dkpe = jnp.zeros((tt, HP), F32)
        for h in range(N_HEADS):
            hs = slice(HP * h, HP * (h + 1))
            xr = raw_ref[:, hs] + kpe_ref[...] if has_kpe else raw_ref[:, hs]
            d = d_ref[:, hs].astype(F32) * in_scale
            r = lax.rsqrt(jnp.sum(xr * xr, axis=-1, keepdims=True) * (1.0 / QK_HEAD) + EPS)
            dn = d * c_ref[...] + _swap_rope_halves(d * s_ref[...])
            gd = dn * g_ref[...]
            dx = r * gd - xr * (r * r * r) * (jnp.sum(gd * xr, axis=-1, keepdims=True) * (1.0 / QK_HEAD))
            dx_ref[:, hs] = dx.astype(dx_ref.dtype)
            dg = dg + jnp.sum(dn * xr * r, axis=0, keepdims=True)
            dkpe = dkpe + dx

        @pl.when(pl.program_id(0) == 0)
        def _():
            dg_ref[...] = jnp.zeros_like(dg_ref)

        dg_ref[...] += jnp.broadcast_to(dg, dg_ref.shape)
        if has_kpe:
            dkpe_ref[...] = dkpe

    heads = pl.BlockSpec((tt, N_HEADS * HP), lambda i: (i, 0))
    shared = pl.BlockSpec((tt, HP), lambda i: (i, 0))
    kpe_spec = pl.BlockSpec((tt, HP), lambda i: (i, kpe_blk))
    in_specs = [heads, heads] + ([kpe_spec] if has_kpe else []) + [pl.BlockSpec((1, HP), lambda i: (0, 0)), shared, shared]
    args = [dout, raw] + ([kpe] if has_kpe else []) + [gain, C, S]
    out_shape = [jax.ShapeDtypeStruct(raw.shape, BF), jax.ShapeDtypeStruct((8, HP), F32)]
    out_specs = [heads, pl.BlockSpec((8, HP), lambda i: (0, 0))]
    if has_kpe:
        out_shape.append(jax.ShapeDtypeStruct((T, HP), F32))
        out_specs.append(shared)
    return pl.pallas_call(
        body, name=name, out_shape=tuple(out_shape), grid=(T // tt,),
        in_specs=in_specs, out_specs=tuple(out_specs),
        compiler_params=pltpu.CompilerParams(dimension_semantics=("arbitrary",)),
    )(*args)


ATTN_SCALE = 1.0 / math.sqrt(QK_HEAD)
LOG2E = 1.0 / math.log(2.0)
Q_SCALE = ATTN_SCALE * LOG2E


def _attn_fwd(q, k, v):
    T = q.shape[0]
    tq = _pick(T, 1024)
    tk = _pick(T, 1024)

    def body(q_ref, k_ref, v_ref, o_ref, lse_ref):
        qt = q_ref[...]
        m = o = None
        for j in range(T // tk):
            ks = slice(j * tk, (j + 1) * tk)
            s = lax.dot_general(qt, k_ref[ks, :], NT, preferred_element_type=F32)
            m_j = jnp.max(s, axis=-1, keepdims=True)
            m_new = m_j if m is None else jnp.maximum(m, m_j)
            o_j = jnp.dot(jnp.exp2(s - m_new).astype(BF), v_ref[ks, :], preferred_element_type=F32)
            o = o_j if o is None else o * jnp.exp2(m - m_new) + o_j
            m = m_new
        l = o[:, V_HEAD:V_HEAD + 1]
        o_ref[...] = o / l
        lse_ref[...] = jnp.broadcast_to(m + jnp.log2(l), lse_ref.shape)

    qs = pl.BlockSpec((tq, HP), lambda h, i: (i, h))
    kv = pl.BlockSpec((T, HP), lambda h, i: (0, h))
    return pl.pallas_call(
        body, name="attn_fwd", out_shape=(jax.ShapeDtypeStruct(q.shape, F32),) * 2, grid=(N_HEADS, T // tq),
        in_specs=[qs, kv, kv], out_specs=(qs, qs),
        compiler_params=pltpu.CompilerParams(dimension_semantics=("parallel", "parallel")),
    )(q, k, v)


def _attn_bwd(q, k, v, do, o, lse):
    T = q.shape[0]
    tb = _pick(T, 512)
    nb = T // tb
    tkey = _pick(T, 1024)

    def body(q_ref, k_ref, v_ref, do_ref, o_ref, lse_ref, dq_ref, dk_ref, dv_ref, delta_rows, lse_rows, dob_scr, dv_acc):
        dq_ref[...] = jnp.zeros_like(dq_ref)
        dk_ref[...] = jnp.zeros_like(dk_ref)
        lane = lax.broadcasted_iota(jnp.int32, (8, HP), 1)
        ones8 = jnp.ones((8, HP), BF)
        first8 = jnp.where(lane == 0, 1.0, 0.0).astype(BF)

        def as_rows(pick, v):
            total, rest = None, v
            for _ in range(3):
                piece = rest.astype(BF)
                part = lax.dot_general(pick, piece, NT, preferred_element_type=F32)
                total = part if total is None else total + part
                rest = rest - piece.astype(F32)
            return total

        def per_q_tile(i, carry):
            qs = pl.ds(pl.multiple_of(i * tb, tb), tb)
            doi = do_ref[qs, :]
            delta_rows[i] = as_rows(ones8, doi * o_ref[qs, :])
            lse_rows[i] = as_rows(first8, lse_ref[qs, :])
            dob_scr[qs, :] = doi.astype(BF)
            return carry

        lax.fori_loop(0, nb, per_q_tile, 0)

        def k_loop(j, carry):
            ks = pl.ds(pl.multiple_of(j * tkey, tkey), tkey)
            kj, vj = k_ref[ks, :], v_ref[ks, :]

            dv_acc[...] = jnp.zeros_like(dv_acc)

            def q_loop(i, carry_q):
                qs = pl.ds(pl.multiple_of(i * tb, tb), tb)
                qi = q_ref[qs, :]
                dob = dob_scr[qs, :]
                s_t = lax.dot_general(kj, qi, NT, preferred_element_type=F32)
                p_t = jnp.exp2(s_t - lse_rows[i, 0:1, :])
                dp_t = lax.dot_general(vj, dob, NT, preferred_element_type=F32)
                ds_t = (p_t * (dp_t - delta_rows[i, 0:1, :])).astype(BF)
                dv_acc[...] += jnp.dot(p_t.astype(BF), dob, preferred_element_type=F32)
                dk_ref[ks, :] += jnp.dot(ds_t, qi, preferred_element_type=F32)
                dq_ref[qs, :] += lax.dot_general(ds_t, kj, TN, preferred_element_type=F32)
                return carry_q

            lax.fori_loop(0, nb, q_loop, 0)
            dv_ref[ks, :] = dv_acc[...].astype(dv_ref.dtype)
            return carry

        lax.fori_loop(0, T // tkey, k_loop, 0)

    spec = pl.BlockSpec((T, HP), lambda h: (0, h))
    return pl.pallas_call(
        body, name="attn_bwd",
        out_shape=(jax.ShapeDtypeStruct(q.shape, F32), jax.ShapeDtypeStruct(q.shape, F32), jax.ShapeDtypeStruct(q.shape, BF)),
        grid=(N_HEADS,), in_specs=[spec] * 6, out_specs=(spec,) * 3,
        scratch_shapes=[pltpu.VMEM((nb, 8, tb), F32), pltpu.VMEM((nb, 8, tb), F32), pltpu.VMEM((T, HP), BF),
                        pltpu.VMEM((tkey, HP), F32)],
        compiler_params=pltpu.CompilerParams(dimension_semantics=("parallel",), vmem_limit_bytes=2 * 15 * T * HP * 2 + (8 << 20)),
    )(q, k, v, do, o, lse)


CONV_TC = 512
CONV_PAD = CONV_WIDTH // 2


def _halo_specs(tr, col_of):
    r8 = tr // 8
    cur = pl.BlockSpec((tr, CONV_TC), lambda j, i: (i, col_of(j)))
    prev = pl.BlockSpec((8, CONV_TC), lambda j, i: (jnp.maximum(i * r8 - 1, 0), col_of(j)))

    def nxt_map(j, i, n8):
        return (jnp.minimum((i + 1) * r8, n8 - 1), col_of(j))

    return cur, prev, nxt_map


def _with_halo(prev_ref, cur_ref, next_ref, i, n_i):
    prev = jnp.where(i == 0, 0.0, prev_ref[...].astype(F32))
    nxt = jnp.where(i == n_i - 1, 0.0, next_ref[...].astype(F32))
    return jnp.concatenate([prev, cur_ref[...].astype(F32), nxt], axis=0)


def _conv_fwd(u, w8, b):
    T = u.shape[0]
    tr = _pick(T, 512)
    n_i = T // tr
    c0 = U_XBC // CONV_TC
    cur, prev, nxt_map = _halo_specs(tr, lambda j: c0 + j)
    nxt = pl.BlockSpec((8, CONV_TC), functools.partial(nxt_map, n8=T // 8))

    def body(p_ref, c_ref, n_ref, w_ref, b_ref, pre_ref, act_ref):
        i = pl.program_id(1)
        full = _with_halo(p_ref, c_ref, n_ref, i, n_i)
        acc = jnp.broadcast_to(b_ref[...], (tr, CONV_TC))
        for kk in range(CONV_WIDTH):
            acc = acc + full[8 - CONV_PAD + kk:8 - CONV_PAD + kk + tr, :] * w_ref[kk:kk + 1, :]
        pre_ref[...] = acc
        act_ref[...] = _silu(acc)

    out = pl.BlockSpec((tr, CONV_TC), lambda j, i: (i, j))
    return pl.pallas_call(
        body, name="conv_fwd", out_shape=(jax.ShapeDtypeStruct((T, XBC_DIM), F32),) * 2,
        grid=(XBC_DIM // CONV_TC, n_i),
        in_specs=[prev, cur, nxt, pl.BlockSpec((8, CONV_TC), lambda j, i: (0, j)), pl.BlockSpec((1, CONV_TC), lambda j, i: (0, j))],
        out_specs=(out, out),
    )(u, u, u, w8, b)


def _conv_dpre(dacts, pre, col0, *, name):
    T, width = dacts[0].shape
    tt = _pick(T, 512)
    n_d = len(dacts)
    c0 = col0 // CONV_TC

    def body(*refs):
        d = refs[0][...]
        for r in refs[1:n_d]:
            d = d + r[...]
        refs[n_d + 1][...] = d * _dsilu(refs[n_d][...])

    blk = pl.BlockSpec((tt, CONV_TC), lambda j, i: (i, j))
    return pl.pallas_call(
        body, name=name, out_shape=jax.ShapeDtypeStruct((T, width), F32), grid=(width // CONV_TC, T // tt),
        in_specs=[blk] * n_d + [pl.BlockSpec((tt, CONV_TC), lambda j, i: (i, c0 + j))], out_specs=blk,
    )(*dacts, pre)


def _conv_bwd(dpre, u, w8, col0, *, name):
    T, width = dpre.shape
    tr = _pick(T, 512)
    n_i = T // tr
    cd = col0 // CONV_TC
    cx = (U_XBC + col0) // CONV_TC
    d_cur, d_prev, d_nxt_map = _halo_specs(tr, lambda j: j)
    x_cur, x_prev, x_nxt_map = _halo_specs(tr, lambda j: cx + j)
    d_nxt = pl.BlockSpec((8, CONV_TC), functools.partial(d_nxt_map, n8=T // 8))
    x_nxt = pl.BlockSpec((8, CONV_TC), functools.partial(x_nxt_map, n8=T // 8))

    def body(dp_ref, dc_ref, dn_ref, xp_ref, xc_ref, xn_ref, w_ref, dx_ref, dw_ref):
        i = pl.program_id(1)
        dfull = _with_halo(dp_ref, dc_ref, dn_ref, i, n_i)
        xfull = _with_halo(xp_ref, xc_ref, xn_ref, i, n_i)
        dcur = dc_ref[...]
        dx = jnp.zeros((tr, CONV_TC), F32)
        rows = []
        for kk in range(CONV_WIDTH):
            dx = dx + dfull[8 + CONV_PAD - kk:8 + CONV_PAD - kk + tr, :] * w_ref[kk:kk + 1, :]
            rows.append(jnp.sum(dcur * xfull[8 - CONV_PAD + kk:8 - CONV_PAD + kk + tr, :], axis=0, keepdims=True))
        rows.append(jnp.sum(dcur, axis=0, keepdims=True))
        rows.append(jnp.zeros((2, CONV_TC), F32))
        dx_ref[...] = dx.astype(dx_ref.dtype)

        @pl.when(i == 0)
        def _():
            dw_ref[...] = jnp.zeros_like(dw_ref)

        dw_ref[...] += jnp.concatenate(rows, axis=0)

    out = pl.BlockSpec((tr, CONV_TC), lambda j, i: (i, j))
    return pl.pallas_call(
        body, name=name, out_shape=(jax.ShapeDtypeStruct((T, width), BF), jax.ShapeDtypeStruct((8, width), F32)),
        grid=(width // CONV_TC, n_i),
        in_specs=[d_prev, d_cur, d_nxt, x_prev, x_cur, x_nxt, pl.BlockSpec((8, CONV_TC), lambda j, i: (0, cd + j))],
        out_specs=(out, pl.BlockSpec((8, CONV_TC), lambda j, i: (0, j))),
        compiler_params=pltpu.CompilerParams(dimension_semantics=("parallel", "arbitrary")),
    )(dpre, dpre, dpre, u, u, u, w8)


N_HB = 2 * SSM_GROUPS
P_DT, P_CS, P_E, P_W = 0, HP, 2 * HP, 3 * HP
DT_BLK = (U_SMALL + S_DT) // HP


def _tri(rev, transpose=False):
    rows = lax.broadcasted_iota(jnp.int32, (CHUNK, CHUNK), 0)
    cols = lax.broadcasted_iota(jnp.int32, (CHUNK, CHUNK), 1)
    if transpose:
        rows, cols = cols, rows
    return (cols >= rows) if rev else (cols <= rows)


def _ssd_prep(u, bias8, alog8):
    T = u.shape[0]
    nc = T // CHUNK

    def body(dt_ref, bias_ref, a_ref, cols_ref, rows_ref):
        lane = lax.broadcasted_iota(jnp.int32, (CHUNK, HP), 1)
        dt = _softplus(dt_ref[...] + bias_ref[0:1, :])
        da = dt * (-jnp.exp(a_ref[0:1, :]))
        cs_f = jnp.dot(jnp.where(_tri(False), 1.0, 0.0).astype(F32), da, precision=HI, preferred_element_type=F32)
        cs_b = jnp.dot(jnp.where(_tri(True), 1.0, 0.0).astype(F32), da, precision=HI, preferred_element_type=F32)
        cs = jnp.where(lane < SSM_HEADS, cs_f, cs_b)
        tot = jnp.where(lane[0:1] < SSM_HEADS, cs_f[CHUNK - 1:CHUNK, :], cs_b[0:1, :])
        e, w = jnp.exp(cs), jnp.exp(tot - cs)
        tot8 = jnp.broadcast_to(tot, (8, HP))
        etot8 = jnp.exp(tot8)
        for b in range(N_HB):
            down = (HP - HG * b) % HP

            def rolled(v):
                return pltpu.roll(v, down, 1) if down else v

            cols_ref[b, :, P_DT:P_DT + HP] = rolled(dt)
            cs_r = rolled(cs)
            cols_ref[b, :, P_CS:P_CS + HP] = cs_r
            cols_ref[b, :, P_E:P_E + HP] = rolled(e)
            cols_ref[b, :, P_W:P_W + HP] = rolled(w)
            rows_ref[b, 0, 0:8, :] = cs_r.T[0:8, :]
            r8 = lax.broadcasted_iota(jnp.int32, (8, HP), 0)
            rows_ref[b, 0, 8:16, :] = jnp.where(r8 == 0, rolled(tot8), jnp.where(r8 == 1, rolled(etot8), 0.0))

    vec = pl.BlockSpec((8, HP), lambda c: (0, 0))
    return pl.pallas_call(
        body, name="ssd_prep",
        out_shape=(jax.ShapeDtypeStruct((N_HB, T, 4 * HP), F32), jax.ShapeDtypeStruct((N_HB, nc, 16, HP), F32)),
        grid=(nc,), in_specs=[pl.BlockSpec((CHUNK, HP), lambda c: (c, DT_BLK)), vec, vec],
        out_specs=(pl.BlockSpec((N_HB, CHUNK, 4 * HP), lambda c: (0, c, 0)), pl.BlockSpec((N_HB, 1, 16, HP), lambda c: (0, c, 0, 0))),
    )(u, bias8, alog8)


def _ssd_specs(T, rev, bwd):
    nc = T // CHUNK
    fwd_order = (lambda c: nc - 1 - c) if rev else (lambda c: c)
    cm = (lambda c: fwd_order(nc - 1 - c)) if bwd else fwd_order
    hb0 = SSM_GROUPS if rev else 0
    xs = pl.BlockSpec((CHUNK, GW), lambda c, g: (cm(c), g))
    bs = pl.BlockSpec((CHUNK, D_STATE), lambda c, g: (cm(c), D_INNER // D_STATE + g))
    cs = pl.BlockSpec((CHUNK, D_STATE), lambda c, g: (cm(c), (D_INNER + SSM_GROUPS * D_STATE) // D_STATE + g))
    cols = pl.BlockSpec((1, CHUNK, 4 * HP), lambda c, g: (hb0 + g, cm(c), 0))
    rows = pl.BlockSpec((1, 1, 16, HP), lambda c, g: (hb0 + g, cm(c), 0, 0))
    return nc, cm, xs, bs, cs, cols, rows


def _head_lanes(to_heads):
    shape = (GW, HP) if to_heads else (HP, GW)
    wide = lax.broadcasted_iota(jnp.int32, shape, 0 if to_heads else 1)
    head = lax.broadcasted_iota(jnp.int32, shape, 1 if to_heads else 0)
    return jnp.where((wide >= PH * head) & (wide < PH * (head + 1)), 1.0, 0.0).astype(BF)


def _split_dot(v, m, terms):
    total, rest = None, v
    for _ in range(terms):
        piece = rest.astype(BF)
        part = jnp.dot(piece, m, preferred_element_type=F32)
        total = part if total is None else total + part
        rest = rest - piece.astype(F32)
    return total


def _spread_cols(cols_ref, rows_ref):
    spread = _head_lanes(False)
    dt_e = _split_dot(cols_ref[0, :, P_DT:P_DT + HP], spread, 3)
    e_e = _split_dot(cols_ref[0, :, P_E:P_E + HP], spread, 2)
    w_e = _split_dot(cols_ref[0, :, P_W:P_W + HP], spread, 2)
    etot_e = _split_dot(rows_ref[0, 0, 8:16, :], spread, 3)[1:2, :]
    return dt_e, e_e, w_e, etot_e


def _decay(cols_ref, rows_ref, hh, incl, transpose=False):
    col = cols_ref[0, :, P_CS + hh:P_CS + hh + 1]
    row = rows_ref[0, 0, hh:hh + 1, :]
    return jnp.where(incl, jnp.exp(row - col if transpose else col - row), 0.0)


def _ssd_fwd(act, cols, rows, *, rev, name):
    T = act.shape[0]
    nc, cm, xs_s, b_s, c_s, cols_s, rows_s = _ssd_specs(T, rev, False)

    def body(x_ref, b_ref, c_ref, cols_ref, rows_ref, y_ref, st_ref, state):
        c, g = pl.program_id(0), pl.program_id(1)

        @pl.when(c == 0)
        def _():
            state[g] = jnp.zeros((D_STATE, GW), F32)

        incl = _tri(rev)
        bm, cmat = b_ref[...].astype(BF), c_ref[...].astype(BF)
        bm_t = b_ref[...].T.astype(BF)
        cb = lax.dot_general(cmat, bm, NT, preferred_element_type=F32)
        dt_e, e_e, w_e, etot_e = _spread_cols(cols_ref, rows_ref)
        prev_all = state[g]
        st_ref[...] = prev_all
        xdt = x_ref[...] * dt_e
        xdt_b = xdt.astype(BF)
        yo_all = jnp.dot(cmat, prev_all.astype(BF), preferred_element_type=F32) * e_e
        state[g] = prev_all * etot_e + jnp.dot(bm_t, (xdt * w_e).astype(BF), preferred_element_type=F32)
        for hh in range(HG):
            hs = slice(PH * hh, PH * (hh + 1))
            lmat = _decay(cols_ref, rows_ref, hh, incl)
            yd = jnp.dot((cb * lmat).astype(BF), xdt_b[:, hs], preferred_element_type=F32)
            y_ref[:, hs] = yd + yo_all[:, hs]

    return pl.pallas_call(
        body, name=name,
        out_shape=(jax.ShapeDtypeStruct((T, D_INNER), F32), jax.ShapeDtypeStruct((nc * D_STATE, D_INNER), F32)),
        grid=(nc, SSM_GROUPS), in_specs=[xs_s, b_s, c_s, cols_s, rows_s], out_specs=(xs_s, xs_s),
        scratch_shapes=[pltpu.VMEM((SSM_GROUPS, D_STATE, GW), F32)],
        compiler_params=pltpu.CompilerParams(dimension_semantics=("arbitrary", "arbitrary")),
    )(act, act, act, cols, rows)


def _ssd_bwd(act, cols, rows, states, dy, *, rev, name):
    T = act.shape[0]
    nc, cm, xs_s, b_s, c_s, cols_s, rows_s = _ssd_specs(T, rev, True)

    def body(x_ref, b_ref, c_ref, cols_ref, rows_ref, st_ref, dy_ref, dx_ref, db_ref, dc_ref, dsel_ref, dtot_ref,
             dstate, dcs_cols, dcs_rows, dcb, dm_scr, dxdt_scr):
        c, g = pl.program_id(0), pl.program_id(1)

        @pl.when(c == 0)
        def _():
            dstate[g] = jnp.zeros((D_STATE, GW), F32)

        incl, incl_t = _tri(rev), _tri(rev, transpose=True)
        bm, cmat = b_ref[...].astype(BF), c_ref[...].astype(BF)
        cm_t = c_ref[...].T.astype(BF)
        cb = lax.dot_general(cmat, bm, NT, preferred_element_type=F32)
        cb_t = lax.dot_general(bm, cmat, NT, preferred_element_type=F32)
        prev_all, ds_all = st_ref[...], dstate[g]
        pb_all, dsb_all = prev_all.astype(BF), ds_all.astype(BF)
        cp_all = jnp.dot(cmat, pb_all, preferred_element_type=F32)
        bds_all = jnp.dot(bm, dsb_all, preferred_element_type=F32)
        dt_e, e_e, w_e, etot_e = _spread_cols(cols_ref, rows_ref)
        to_heads = _head_lanes(True)
        x, dy = x_ref[...], dy_ref[...]
        xdt = x * dt_e
        xdt_b, dy_b = xdt.astype(BF), dy.astype(BF)
        dye_b, xdw_b = (dy * e_e).astype(BF), (xdt * w_e).astype(BF)
        for hh in range(HG):
            hs = slice(PH * hh, PH * (hh + 1))
            mmat_t = cb_t * _decay(cols_ref, rows_ref, hh, incl_t, transpose=True)
            dm_scr[hh] = lax.dot_general(dy_b[:, hs], xdt_b[:, hs], NT, preferred_element_type=F32)
            dxdt_scr[:, hs] = jnp.dot(mmat_t.astype(BF), dy_b[:, hs], preferred_element_type=F32)
        bdsw = bds_all * w_e
        dxdt = dxdt_scr[...] + bdsw
        dx_ref[...] = dxdt * dt_e
        t = _split_dot(xdt * bdsw, to_heads, 2)
        dcs_state = _split_dot(dy * cp_all, to_heads, 2) * cols_ref[0, :, P_E:P_E + HP] - t
        dsel_ref[0, :, 0:HP] = _split_dot(dxdt * x, to_heads, 2)
        sp = _split_dot(jnp.broadcast_to(jnp.sum(ds_all * prev_all, axis=0, keepdims=True), (8, GW)), to_heads, 2)
        dtot_ref[0, 0] = jnp.sum(t, axis=0, keepdims=True) + sp * rows_ref[0, 0, 9:10, :]
        dstate[g] = ds_all * etot_e + jnp.dot(cm_t, dye_b, preferred_element_type=F32)
        dcs_cols[...] = jnp.zeros_like(dcs_cols)
        dcs_rows[...] = jnp.zeros_like(dcs_rows)
        dcb[...] = jnp.zeros_like(dcb)
        for hh in range(HG):
            lmat = _decay(cols_ref, rows_ref, hh, incl)
            dm = dm_scr[hh]
            qm = dm * (cb * lmat)
            dcs_cols[:, hh:hh + 1] = jnp.sum(qm, axis=1, keepdims=True)
            dcs_rows[hh:hh + 1, :] = jnp.sum(qm, axis=0, keepdims=True)
            dcb[...] += dm * lmat
        dcb_all = dcb[...]
        dsel_ref[0, :, HP:2 * HP] = dcs_state + dcs_cols[...] - dcs_rows[...].T
        dc_ref[...] = (lax.dot_general(dye_b, pb_all, NT, preferred_element_type=F32)
                       + jnp.dot(dcb_all.astype(BF), bm, preferred_element_type=F32))
        db_ref[...] = (lax.dot_general(xdw_b, dsb_all, NT, preferred_element_type=F32)
                       + jnp.dot(dcb_all.T.astype(BF), cmat, preferred_element_type=F32))

    bc_out = pl.BlockSpec((CHUNK, D_STATE), lambda c, g: (cm(c), g))
    return pl.pallas_call(
        body, name=name,
        out_shape=(jax.ShapeDtypeStruct((T, D_INNER), F32), jax.ShapeDtypeStruct((T, SSM_GROUPS * D_STATE), F32),
                   jax.ShapeDtypeStruct((T, SSM_GROUPS * D_STATE), F32), jax.ShapeDtypeStruct((SSM_GROUPS, T, 2 * HP), F32),
                   jax.ShapeDtypeStruct((SSM_GROUPS, nc, 8, HP), F32)),
        grid=(nc, SSM_GROUPS), in_specs=[xs_s, b_s, c_s, cols_s, rows_s, xs_s, xs_s],
        out_specs=(xs_s, bc_out, bc_out, pl.BlockSpec((1, CHUNK, 2 * HP), lambda c, g: (g, cm(c), 0)),
                   pl.BlockSpec((1, 1, 8, HP), lambda c, g: (g, cm(c), 0, 0))),
        scratch_shapes=[pltpu.VMEM((SSM_GROUPS, D_STATE, GW), F32), pltpu.VMEM((CHUNK, CHUNK), F32),
                        pltpu.VMEM((CHUNK, CHUNK), F32), pltpu.VMEM((CHUNK, CHUNK), F32),
                        pltpu.VMEM((HG, CHUNK, CHUNK), F32), pltpu.VMEM((CHUNK, GW), F32)],
        compiler_params=pltpu.CompilerParams(dimension_semantics=("arbitrary", "arbitrary")),
    )(act, act, act, cols, rows, states, dy)


def _ssd_prep_bwd(u, bias8, alog8, dsel_f, dtot_f, dsel_b, dtot_b):
    T = u.shape[0]
    nc = T // CHUNK

    def body(dt_ref, bias_ref, a_ref, sf_ref, tf_ref, sb_ref, tb_ref, ddt_ref, da_ref, dbias_ref):
        @pl.when(pl.program_id(0) == 0)
        def _():
            da_ref[...] = jnp.zeros_like(da_ref)
            dbias_ref[...] = jnp.zeros_like(dbias_ref)

        lane = lax.broadcasted_iota(jnp.int32, (CHUNK, HP), 1)
        pre = dt_ref[...] + bias_ref[0:1, :]
        dt = _softplus(pre)
        a = -jnp.exp(a_ref[0:1, :])
        ddt_x, dcs, dtot = jnp.zeros((CHUNK, HP), F32), jnp.zeros((CHUNK, HP), F32), jnp.zeros((8, HP), F32)
        for b in range(N_HB):
            s_ref, t_ref, g = (sf_ref, tf_ref, b) if b < SSM_GROUPS else (sb_ref, tb_ref, b - SSM_GROUPS)
            mine = (lane >= HG * b) & (lane < HG * (b + 1))

            def up(v):
                return pltpu.roll(v, HG * b, 1) if b else v

            ddt_x = ddt_x + jnp.where(mine, up(s_ref[g, :, 0:HP]), 0.0)
            dcs = dcs + jnp.where(mine, up(s_ref[g, :, HP:2 * HP]), 0.0)
            dtot = dtot + jnp.where(mine[0:8], up(t_ref[g, 0]), 0.0)
        tri_f = jnp.where(_tri(False, transpose=True), 1.0, 0.0).astype(F32)
        tri_b = jnp.where(_tri(True, transpose=True), 1.0, 0.0).astype(F32)
        dda = jnp.where(lane < SSM_HEADS, jnp.dot(tri_f, dcs, precision=HI, preferred_element_type=F32),
                        jnp.dot(tri_b, dcs, precision=HI, preferred_element_type=F32)) + dtot[0:1, :]
        dpre = (ddt_x + dda * a) * jax.nn.sigmoid(pre)
        ddt_ref[...] = jnp.where(lane < 2 * SSM_HEADS, dpre, 0.0)
        dbias_ref[...] += jnp.broadcast_to(jnp.sum(dpre, axis=0, keepdims=True), (8, HP))
        da_ref[...] += jnp.broadcast_to(jnp.sum(dda * dt, axis=0, keepdims=True) * a, (8, HP))

    vec = pl.BlockSpec((8, HP), lambda c: (0, 0))
    sel = pl.BlockSpec((SSM_GROUPS, CHUNK, 2 * HP), lambda c: (0, c, 0))
    tot = pl.BlockSpec((SSM_GROUPS, 1, 8, HP), lambda c: (0, c, 0, 0))
    tile = pl.BlockSpec((CHUNK, HP), lambda c: (c, 0))
    return pl.pallas_call(
        body, name="ssd_prep_bwd",
        out_shape=(jax.ShapeDtypeStruct((T, HP), F32), jax.ShapeDtypeStruct((8, HP), F32), jax.ShapeDtypeStruct((8, HP), F32)),
        grid=(nc,), in_specs=[pl.BlockSpec((CHUNK, HP), lambda c: (c, DT_BLK)), vec, vec, sel, tot, sel, tot],
        out_specs=(tile, vec, vec),
        compiler_params=pltpu.CompilerParams(dimension_semantics=("arbitrary",)),
    )(u, bias8, alog8, dsel_f, dtot_f, dsel_b, dtot_b)


def _ssm_combine_fwd(y_f, y_b, act, u, dskip, gain):
    T = y_f.shape[0]
    tt = _pick(T, 512)

    def body(yf_ref, yb_ref, x_ref, z_ref, ds_ref, g_ref, y_ref, m_ref):
        y = yf_ref[...] + yb_ref[...] + ds_ref[...] * x_ref[...]
        y2 = y * _silu(z_ref[...])
        r = lax.rsqrt(jnp.mean(y2 * y2, axis=-1, keepdims=True) + EPS)
        y_ref[...] = y
        m_ref[...] = (y2 * r * g_ref[...]).astype(m_ref.dtype)

    blk = pl.BlockSpec((tt, GW), lambda i, g: (i, g))
    vec = pl.BlockSpec((1, GW), lambda i, g: (0, g))
    return pl.pallas_call(
        body, name="ssm_combine_fwd",
        out_shape=(jax.ShapeDtypeStruct((T, D_INNER), F32), jax.ShapeDtypeStruct((T, D_INNER), BF)),
        grid=(T // tt, SSM_GROUPS), in_specs=[blk, blk, blk, blk, vec, vec], out_specs=(blk, blk),
    )(y_f, y_b, act, u, dskip, gain)


def _ssm_combine_bwd(dm, y, act, u, dskip, gain):
    T = y.shape[0]
    tt = _pick(T, 512)

    def body(dm_ref, y_ref, x_ref, z_ref, ds_ref, g_ref, dy_ref, dz_ref, dxs_ref, dg_ref, dsk_ref):
        z = z_ref[...]
        y = y_ref[...]
        x = x_ref[...]
        sz = _silu(z)
        y2 = y * sz
        r = lax.rsqrt(jnp.mean(y2 * y2, axis=-1, keepdims=True) + EPS)
        d = dm_ref[...]
        gd = d * g_ref[...]
        dy2 = r * gd - y2 * (r * r * r) * jnp.mean(gd * y2, axis=-1, keepdims=True)
        dy = dy2 * sz
        dy_ref[...] = dy
        dz_ref[...] = (dy2 * y * _dsilu(z)).astype(dz_ref.dtype)
        dxs_ref[...] = dy * ds_ref[...]

        @pl.when(pl.program_id(1) == 0)
        def _():
            dg_ref[...] = jnp.zeros_like(dg_ref)
            dsk_ref[...] = jnp.zeros_like(dsk_ref)

        dg_ref[...] += jnp.broadcast_to(jnp.sum(d * y2 * r, axis=0, keepdims=True), dg_ref.shape)
        lane_sum = jnp.broadcast_to(jnp.sum(dy * x, axis=0, keepdims=True), (8, GW))
        src = lax.broadcasted_iota(jnp.int32, (GW, HP), 0)
        head = lax.broadcasted_iota(jnp.int32, (GW, HP), 1)
        to_head = jnp.where((src >= PH * head) & (src < PH * (head + 1)), 1.0, 0.0).astype(F32)
        dsk_ref[...] += jnp.dot(lane_sum, to_head, precision=HI, preferred_element_type=F32)

    blk = pl.BlockSpec((tt, GW), lambda g, i: (i, g))
    vec = pl.BlockSpec((1, GW), lambda g, i: (0, g))
    acc = pl.BlockSpec((8, GW), lambda g, i: (0, g))
    return pl.pallas_call(
        body, name="ssm_combine_bwd",
        out_shape=(jax.ShapeDtypeStruct((T, D_INNER), F32), jax.ShapeDtypeStruct((T, D_INNER), BF),
                   jax.ShapeDtypeStruct((T, D_INNER), F32), jax.ShapeDtypeStruct((8, D_INNER), F32),
                   jax.ShapeDtypeStruct((8, SSM_GROUPS * HP), F32)),
        grid=(SSM_GROUPS, T // tt), in_specs=[blk, blk, blk, blk, vec, vec],
        out_specs=(blk, blk, blk, acc, pl.BlockSpec((8, HP), lambda g, i: (0, g))),
        compiler_params=pltpu.CompilerParams(dimension_semantics=("parallel", "arbitrary")),
    )(dm, y, act, u, dskip, gain)


def _loss_head(y, target):
    T, D = y.shape
    tt = _pick(T, 512)

    def body(y_ref, t_ref, dy_ref, dyb_ref, l_ref):
        e = y_ref[...] - t_ref[...]
        dy_ref[...] = e * (1.0 / D)
        dyb_ref[...] = (e * (1.0 / D)).astype(dyb_ref.dtype)

        @pl.when(pl.program_id(0) == 0)
        def _():
            l_ref[...] = jnp.zeros_like(l_ref)

        l_ref[...] += jnp.sum(e * e) * (0.5 / D)

    blk = pl.BlockSpec((tt, D), lambda i: (i, 0))
    return pl.pallas_call(
        body, name="loss_head",
        out_shape=(jax.ShapeDtypeStruct((T, D), F32), jax.ShapeDtypeStruct((T, D), BF), jax.ShapeDtypeStruct((8, 128), F32)),
        grid=(T // tt,), in_specs=[blk, blk], out_specs=(blk, blk, pl.BlockSpec((8, 128), lambda i: (0, 0))),
        compiler_params=pltpu.CompilerParams(dimension_semantics=("arbitrary",)),
    )(y, target)


def _adamw(w, g, m, v, *, name):
    R, C = w.shape
    cap = max(8, (1 << 18) // C)
    tr = R
    if R % 8 == 0:
        tr = 8
        for cand in range(8, min(R, cap) + 1, 8):
            if R % cand == 0:
                tr = cand

    def body(w_ref, g_ref, m_ref, v_ref, d_ref, nm_ref, nv_ref):
        gg = g_ref[...]
        nm = ADAM_B1 * m_ref[...] + (1.0 - ADAM_B1) * gg
        nv = ADAM_B2 * v_ref[...] + (1.0 - ADAM_B2) * jnp.square(gg)
        m_hat = nm / (1.0 - ADAM_B1 ** ADAM_STEP)
        v_hat = nv / (1.0 - ADAM_B2 ** ADAM_STEP)
        d_ref[...] = -ADAM_LR * (m_hat / (jnp.sqrt(v_hat) + ADAM_EPS) + ADAM_WD * w_ref[...])
        nm_ref[...] = nm
        nv_ref[...] = nv

    blk = pl.BlockSpec((tr, C), lambda i: (i, 0))
    return pl.pallas_call(
        body, name=name, out_shape=(jax.ShapeDtypeStruct((R, C), F32),) * 3, grid=(R // tr,),
        in_specs=[blk] * 4, out_specs=(blk,) * 3,
    )(w, g, m, v)


ANY = pl.BlockSpec(memory_space=pl.ANY)


def _chip_peers():
    x, y, c = lax.axis_index("x"), lax.axis_index("y"), lax.axis_index("c")
    return x, y, c, [(1 - x, y), (x, 1 - y), (1 - x, 1 - y)]


def _half_rows(c, rh):
    return pl.ds(pl.multiple_of(c * rh, 16), rh)


def _my_chip():
    return 2 * lax.axis_index("x") + lax.axis_index("y")


def _gather_chips(wb, wf):
    rh = wb.shape[0] // 2
    rq = rh // 2

    def body(wb_ref, wf_ref, ob_ref, of_ref, send_sems, recv_sems):
        x, y, c, peers = _chip_peers()
        nbr_x, nbr_y = peers[0], peers[1]
        me, chip_x, chip_y, chip_d = 2 * x + y, 2 * (1 - x) + y, 2 * x + (1 - y), 2 * (1 - x) + (1 - y)

        def quarter(core, b):
            return pl.ds(pl.multiple_of(core * rh + b * rq, 16), rq)

        ici = [(0, nbr_x, me, 0, chip_x), (1, nbr_y, me, 1, chip_y), (2, nbr_y, me, 0, chip_y), (3, nbr_x, me, 1, chip_x),
               (4, nbr_y, chip_x, 0, chip_d), (5, nbr_x, chip_y, 1, chip_d)]

        def ici_copy(k, to, slot, b, own):
            rows = quarter(c, b)
            return pltpu.make_async_remote_copy(
                src_ref=wb_ref.at[rows] if own else ob_ref.at[slot, rows], dst_ref=ob_ref.at[slot, rows],
                send_sem=send_sems.at[k], recv_sem=recv_sems.at[k], device_id=(to[0], to[1], c), device_id_type=MESH)

        def to_sibling(k, slot, b, core):
            rows = quarter(core, b)
            return pltpu.make_async_remote_copy(
                src_ref=ob_ref.at[slot, rows], dst_ref=ob_ref.at[slot, rows], send_sem=send_sems.at[6 + k],
                recv_sem=recv_sems.at[6 + k], device_id=(x, y, 1 - c), device_id_type=MESH)

        def small_copy(k, slot):
            px, py = peers[k]
            return pltpu.make_async_remote_copy(
                src_ref=wf_ref, dst_ref=of_ref.at[slot], send_sem=send_sems.at[12 + k], recv_sem=recv_sems.at[12 + k],
                device_id=(px, py, c), device_id_type=MESH)

        sends = [ici_copy(k, to, slot, b, True) for k, to, slot, b, _ in ici[:4]] + [small_copy(k, me) for k in range(3)]
        for cp in sends:
            cp.start()
        for k, to, slot, b, arrives in ici:
            ici_copy(k, to, arrives, b, False).wait_recv()
            passed = [to_sibling(k, arrives, b, c)]
            if k < 2:
                passed.append(ici_copy(*ici[4 + k][:4], False))
            for cp in passed:
                cp.start()
            sends += passed
        for k, to, slot, b, arrives in ici:
            to_sibling(k, arrives, b, 1 - c).wait_recv()
        chip_of = [chip_x, chip_y, chip_d]
        for k in range(3):
            small_copy(k, chip_of[k]).wait_recv()
        for cp in sends:
            cp.wait_send()

    ob, of = pl.pallas_call(
        body, name="gather_weights",
        out_shape=(jax.ShapeDtypeStruct((4,) + wb.shape, wb.dtype), jax.ShapeDtypeStruct((4,) + wf.shape, wf.dtype)),
        in_specs=[ANY, ANY], out_specs=(ANY, ANY),
        scratch_shapes=[pltpu.SemaphoreType.DMA((15,)), pltpu.SemaphoreType.DMA((15,))],
    )(wb, wf)
    me = _my_chip()
    return lax.dynamic_update_slice(ob, wb[None], (me, 0, 0)), lax.dynamic_update_slice(of, wf[None], (me, 0, 0))


def _halves_to_sibling(gp):
    rh = gp.shape[1] // 2

    def body(gp_ref, o_ref, send_sem, recv_sem):
        x, y, c = lax.axis_index("x"), lax.axis_index("y"), lax.axis_index("c")
        cp = pltpu.make_async_remote_copy(src_ref=gp_ref.at[:, _half_rows(1 - c, rh), :], dst_ref=o_ref, send_sem=send_sem,
                                          recv_sem=recv_sem, device_id=(x, y, 1 - c), device_id_type=MESH)
        cp.start()
        cp.wait()

    return pl.pallas_call(
        body, name="halves_to_sibling", out_shape=jax.ShapeDtypeStruct((gp.shape[0], rh, gp.shape[2]), gp.dtype),
        in_specs=[ANY], out_specs=ANY, scratch_shapes=[pltpu.SemaphoreType.DMA, pltpu.SemaphoreType.DMA],
    )(gp)


def _row_tile(rows, cap=1024):
    tr = 16
    for cand in range(16, cap + 1, 16):
        if rows % cand == 0:
            tr = cand
    return tr


def _add_halves(gp, sib, core):
    n, rh, C = sib.shape
    tr = _row_tile(rh)
    nt = rh // tr

    def body(c_ref, g_ref, s_ref, o_ref):
        o_ref[...] = (g_ref[...].astype(F32) + s_ref[...].astype(F32)).astype(o_ref.dtype)

    blk = pl.BlockSpec((1, tr, C), lambda j, i, c: (j, i, 0))
    return pl.pallas_call(
        body, name="add_halves", out_shape=jax.ShapeDtypeStruct(sib.shape, sib.dtype),
        grid_spec=pltpu.PrefetchScalarGridSpec(
            num_scalar_prefetch=1, grid=(n, nt),
            in_specs=[pl.BlockSpec((1, tr, C), lambda j, i, c: (j, c[0] * nt + i, 0)), blk], out_specs=blk),
    )(core, gp, sib)


def _join_halves(buf):
    rh = buf.shape[0] // 2

    def body(in_ref, o_ref, send_sem, recv_sem):
        x, y, c = lax.axis_index("x"), lax.axis_index("y"), lax.axis_index("c")

        def copy(rows):
            return pltpu.make_async_remote_copy(src_ref=o_ref.at[rows], dst_ref=o_ref.at[rows], send_sem=send_sem,
                                                recv_sem=recv_sem, device_id=(x, y, 1 - c), device_id_type=MESH)

        send = copy(_half_rows(c, rh))
        send.start()
        copy(_half_rows(1 - c, rh)).wait_recv()
        send.wait_send()

    return pl.pallas_call(
        body, name="join_halves", out_shape=jax.ShapeDtypeStruct(buf.shape, buf.dtype),
        in_specs=[ANY], out_specs=ANY, input_output_aliases={0: 0},
        scratch_shapes=[pltpu.SemaphoreType.DMA, pltpu.SemaphoreType.DMA],
    )(buf)


def _exchange_near(gp):
    rq = gp.shape[1] // 2

    def body(gp_ref, out_ref, send_sems, recv_sems):
        x, y, c, peers = _chip_peers()
        chip_x, chip_y, chip_d = 2 * (1 - x) + y, 2 * x + (1 - y), 2 * (1 - x) + (1 - y)
        plan = [(peers[0], chip_x, 0), (peers[0], chip_d, 0), (peers[1], chip_y, 1), (peers[1], chip_d, 1)]
        copies = [pltpu.make_async_remote_copy(
            src_ref=gp_ref.at[slot, pl.ds(b * rq, rq)], dst_ref=out_ref.at[k], send_sem=send_sems.at[k],
            recv_sem=recv_sems.at[k], device_id=(to[0], to[1], c), device_id_type=MESH) for k, (to, slot, b) in enumerate(plan)]
        for cp in copies:
            cp.start()
        for cp in copies:
            cp.wait_recv()
        for cp in copies:
            cp.wait_send()

    return pl.pallas_call(
        body, name="exchange_grads_near", out_shape=jax.ShapeDtypeStruct((4, rq, gp.shape[2]), gp.dtype),
        in_specs=[ANY], out_specs=ANY, scratch_shapes=[pltpu.SemaphoreType.DMA((4,)), pltpu.SemaphoreType.DMA((4,))],
    )(gp)


def _add_near(gp, near, chips):
    _, rq, C = near.shape
    tr = _row_tile(rq)
    nt = rq // tr

    def body(ch_ref, mine_a, mine_b, on_a, on_b, near_ref, part_ref, on_ref):
        part_ref[0] = mine_a[0].astype(F32) + near_ref[0].astype(F32)
        part_ref[1] = mine_b[0].astype(F32) + near_ref[2].astype(F32)
        on_ref[0] = (on_a[0].astype(F32) + near_ref[1].astype(F32)).astype(on_ref.dtype)
        on_ref[1] = (on_b[0].astype(F32) + near_ref[3].astype(F32)).astype(on_ref.dtype)

    def slot(which, b):
        return pl.BlockSpec((1, tr, C), lambda i, ch: (ch[which], b * nt + i, 0))

    return pl.pallas_call(
        body, name="add_near",
        out_shape=(jax.ShapeDtypeStruct((2, rq, C), F32), jax.ShapeDtypeStruct((2, rq, C), near.dtype)),
        grid_spec=pltpu.PrefetchScalarGridSpec(
            num_scalar_prefetch=1, grid=(nt,),
            in_specs=[slot(0, 0), slot(0, 1), slot(2, 0), slot(1, 1), pl.BlockSpec((4, tr, C), lambda i, ch: (0, i, 0))],
            out_specs=(pl.BlockSpec((2, tr, C), lambda i, ch: (0, i, 0)),) * 2),
    )(chips, gp, gp, gp, gp, near)


def _exchange_far(on):
    def body(on_ref, out_ref, send_sems, recv_sems):
        x, y, c, peers = _chip_peers()
        copies = [pltpu.make_async_remote_copy(
            src_ref=on_ref.at[k], dst_ref=out_ref.at[k], send_sem=send_sems.at[k], recv_sem=recv_sems.at[k],
            device_id=(to[0], to[1], c), device_id_type=MESH) for k, to in enumerate((peers[1], peers[0]))]
        for cp in copies:
            cp.start()
        for cp in copies:
            cp.wait_recv()
        for cp in copies:
            cp.wait_send()

    return pl.pallas_call(
        body, name="exchange_grads_far", out_shape=jax.ShapeDtypeStruct(on.shape, on.dtype),
        in_specs=[ANY], out_specs=ANY, scratch_shapes=[pltpu.SemaphoreType.DMA((2,)), pltpu.SemaphoreType.DMA((2,))],
    )(on)


def _add_far(part, far, core):
    _, rq, C = part.shape
    tr = _row_tile(rq)
    nt = rq // tr

    def body(c_ref, p_ref, f_ref, o_ref):
        o_ref[...] = p_ref[0] + f_ref[0].astype(F32)

    blk = pl.BlockSpec((1, tr, C), lambda b, i, c: (b, i, 0))
    return pl.pallas_call(
        body, name="add_far", out_shape=jax.ShapeDtypeStruct((4 * rq, C), F32),
        grid_spec=pltpu.PrefetchScalarGridSpec(
            num_scalar_prefetch=1, grid=(2, nt), in_specs=[blk, blk],
            out_specs=pl.BlockSpec((tr, C), lambda b, i, c: ((2 * c[0] + b) * nt + i, 0))),
    )(core, part, far)


N_DEV = 8


def _allreduce_small(p):
    rs = p.shape[0]

    def body(x_ref, sum_ref, all_ref, send_sems, recv_sems, local_sem):
        x, y, c = lax.axis_index("x"), lax.axis_index("y"), lax.axis_index("c")
        me, sibling = (x, y, c), (x, y, 1 - c)
        chips = [(1 - x, y), (x, 1 - y), (1 - x, 1 - y)]

        def rows(px, py, pc):
            return all_ref.at[pl.ds((4 * px + 2 * py + pc) * rs, rs), :]

        def copy(k, block, to, src=None):
            return pltpu.make_async_remote_copy(
                src_ref=rows(*block) if src is None else src, dst_ref=rows(*block),
                send_sem=send_sems.at[k], recv_sem=recv_sems.at[k], device_id=to, device_id_type=MESH)

        mine = pltpu.make_async_copy(x_ref, rows(*me), local_sem)
        mine.start()
        first = [copy(0, me, sibling, src=x_ref)]
        first += [copy(1 + j, me, (*chip, c), src=x_ref) for j, chip in enumerate(chips)]
        for cp in first:
            cp.start()
        passed = [copy(4 + j, (*chip, c), sibling) for j, chip in enumerate(chips)]
        for j, chip in enumerate(chips):
            copy(1 + j, (*chip, c), me).wait_recv()
            passed[j].start()
        copy(0, sibling, me).wait_recv()
        for j, chip in enumerate(chips):
            copy(4 + j, (*chip, 1 - c), me).wait_recv()
        for cp in first + passed:
            cp.wait_send()
        mine.wait()
        acc = all_ref[0:rs, :]
        for d in range(1, N_DEV):
            acc = acc + all_ref[d * rs:(d + 1) * rs, :]
        sum_ref[...] = acc

    vmem = pl.BlockSpec(memory_space=pltpu.VMEM)
    return pl.pallas_call(
        body, name="allreduce_small", out_shape=jax.ShapeDtypeStruct((rs, 128), F32),
        in_specs=[vmem], out_specs=vmem,
        scratch_shapes=[pltpu.VMEM((N_DEV * rs, 128), F32), pltpu.SemaphoreType.DMA((7,)), pltpu.SemaphoreType.DMA((7,)),
                        pltpu.SemaphoreType.DMA],
    )(p)


WEIGHTS = ('ffn1_norm', 'ffn1_w_gate', 'ffn1_w_up', 'ffn1_w_down', 'mix_norm', 'w_in', 'q_a_norm', 'w_q_b',
           'kv_a_norm', 'w_kv_b', 'q_head_norm', 'k_head_norm', 'conv_w', 'conv_b', 'a_log_fwd', 'a_log_bwd',
           'dt_bias_fwd', 'dt_bias_bwd', 'd_skip', 'ssm_norm', 'w_attn_branch', 'w_ssm_branch', 'w_out',
           'ffn2_norm', 'ffn2_w_gate', 'ffn2_w_up', 'ffn2_w_down')
PACKED = (('ffn1_w_gate', (D_MODEL, D_FF), 1), ('ffn1_w_up', (D_MODEL, D_FF), 1), ('ffn1_w_down', (D_FF, D_MODEL), 0),
          ('w_in', (D_MODEL, sum(IN_SPLITS)), 1), ('w_q_b', (Q_LORA, N_HEADS * QK_HEAD), 1),
          ('w_kv_b', (KV_LORA, N_HEADS * (QK_NOPE + V_HEAD)), 1),
          ('w_attn_branch', (N_HEADS * V_HEAD, D_MODEL), 0), ('w_ssm_branch', (D_INNER, D_MODEL), 0),
          ('w_out', (D_MODEL, D_MODEL), 0),
          ('ffn2_w_gate', (D_MODEL, D_FF), 1), ('ffn2_w_up', (D_MODEL, D_FF), 1), ('ffn2_w_down', (D_FF, D_MODEL), 0))
PACK_W = 1024
N_CHIPS = 4
SMALL = (('ffn1_norm', 1024), ('mix_norm', 1024), ('q_a_norm', 384), ('kv_a_norm', 256), ('q_head_norm', 96),
         ('k_head_norm', 96), ('conv_b', 3072), ('a_log_fwd', 32), ('a_log_bwd', 32), ('dt_bias_fwd', 32),
         ('dt_bias_bwd', 32), ('d_skip', 32), ('ssm_norm', 2048), ('ffn2_norm', 1024),
         ('conv_w', CONV_WIDTH * XBC_DIM), ('loss', 1))


TRANSPOSED = ('ffn1_w_gate', 'ffn1_w_up', 'w_in', 'ffn2_w_gate', 'ffn2_w_up')


def _stored(name, a):
    return a.T if name in TRANSPOSED else a


def _shard_shape(name, shape, axis):
    sh = tuple(s // N_CHIPS if a == axis else s for a, s in enumerate(shape))
    return sh[::-1] if name in TRANSPOSED else sh


def _by_rows(name, axis):
    return name in TRANSPOSED or axis == 0


def _pack_layout():
    out, r = {}, 0
    for name, shape, axis in PACKED:
        n = math.prod(shape) // N_CHIPS // PACK_W
        out[name] = (r, n)
        r += n
    return out, -(-r // 64) * 64


def _pack(shards):
    layout, rows = _pack_layout()
    parts = [shards[name].reshape(-1, PACK_W) for name, _, _ in PACKED]
    parts.append(jnp.zeros((rows - sum(p.shape[0] for p in parts), PACK_W), parts[0].dtype))
    return jnp.concatenate(parts, axis=0)


def _unpack(packed):
    layout, _ = _pack_layout()
    return {name: packed[layout[name][0]:layout[name][0] + layout[name][1]].reshape(_shard_shape(name, shape, axis))
            for name, shape, axis in PACKED}


def _full_from_slots(slots):
    layout, _ = _pack_layout()
    out = {}
    for name, shape, axis in PACKED:
        r, n = layout[name]
        if _by_rows(name, axis):
            out[name] = slots[:, r:r + n].reshape(N_CHIPS * n, PACK_W)
        else:
            sh = _shard_shape(name, shape, axis)
            out[name] = jnp.concatenate([slots[j, r:r + n].reshape(sh) for j in range(N_CHIPS)], axis=axis)
    return out


def _slots_from_full(full):
    layout, rows = _pack_layout()
    parts = []
    for name, shape, axis in PACKED:
        r, n = layout[name]
        if _by_rows(name, axis):
            parts.append(full[name].reshape(N_CHIPS, n, PACK_W))
        else:
            size = shape[axis] // N_CHIPS
            parts.append(jnp.stack([lax.slice_in_dim(full[name], j * size, (j + 1) * size, axis=axis).reshape(n, PACK_W)
                                    for j in range(N_CHIPS)]))
    parts.append(jnp.zeros((N_CHIPS, rows - sum(p.shape[1] for p in parts), PACK_W), parts[0].dtype))
    return jnp.concatenate(parts, axis=1)


def _pack_small(vals):
    parts = []
    for name, n in SMALL:
        pad = -(-n // 128) * 128 - n
        parts.append(jnp.pad(vals[name].reshape(-1).astype(F32), (0, pad)).reshape(-1, 128))
    rows = sum(p.shape[0] for p in parts)
    parts.append(jnp.zeros((-(-rows // 8) * 8 - rows, 128), F32))
    return jnp.concatenate(parts, axis=0)


def _unpack_small(packed):
    out, r = {}, 0
    for name, n in SMALL:
        k = -(-n // 128)
        out[name] = packed[r:r + k].reshape(-1)[:n]
        r += k
    return out


def _pad_heads(w, axis, per_head, lo, hi):
    shape = w.shape
    w = w.reshape(shape[:axis] + (N_HEADS, per_head) + shape[axis + 1:])
    w = lax.slice_in_dim(w, lo, hi, axis=axis + 1)
    pad = [(0, 0)] * w.ndim
    pad[axis + 1] = (0, HP - (hi - lo))
    w = jnp.pad(w, pad)
    return w.reshape(shape[:axis] + (N_HEADS * HP,) + shape[axis + 1:])


def _unpad_heads(w, axis, keep):
    shape = w.shape
    w = w.reshape(shape[:axis] + (N_HEADS, HP) + shape[axis + 1:])
    return lax.slice_in_dim(w, 0, keep, axis=axis + 1)


def _pad_w_in(wt):
    o = [0]
    for s in IN_SPLITS:
        o.append(o[-1] + s)
    cq, ckv, kpe, z, xbc, dtf, dtb, ga, gb = [wt[o[i]:o[i + 1]] for i in range(len(IN_SPLITS))]
    kpe_pad = jnp.pad(kpe, ((QK_NOPE, HP - QK_HEAD), (0, 0)))
    dt_pad = jnp.pad(jnp.concatenate([dtf, dtb], axis=0), ((0, HP - 2 * SSM_HEADS), (0, 0)))
    return jnp.concatenate([z, ga, gb, xbc, cq, ckv, kpe_pad, dt_pad], axis=0)


def _unpad_w_in(gt):
    z, ga, gb, xbc = gt[U_Z:U_GA], gt[U_GA:U_GB], gt[U_GB:U_XBC], gt[U_XBC:U_SMALL]
    s = gt[U_SMALL:]
    cq, ckv = s[S_CQ:S_CKV], s[S_CKV:S_KPE]
    kpe = s[S_KPE + QK_NOPE:S_KPE + QK_HEAD]
    dtf, dtb = s[S_DT:S_DT + SSM_HEADS], s[S_DT + SSM_HEADS:S_DT + 2 * SSM_HEADS]
    return jnp.concatenate([cq, ckv, kpe, z, xbc, dtf, dtb, ga, gb], axis=0)


def _lanes128(parts):
    row = jnp.concatenate([p.reshape(-1) for p in parts])
    return jnp.pad(row, (0, HP - row.shape[0])).reshape(1, HP)


FF_TILE = D_FF // 2
WGRAD = BF


def _ffn_fwd(x, g, wg_t, wu_t, wd, tag):
    h = _rms_fwd(x, g, name=tag + "_norm")
    gate, up, act = _mm([h], [wg_t, wu_t], name=tag + "_up", tb=True, out_dtypes=(BF, BF, BF), tm=512, tn=FF_TILE,
                        epilogue=lambda a, b: (a, b, _silu(a) * b))
    out = _mm([act], [wd], name=tag + "_down", extras=[x], epilogue=lambda acc, r: (r + 0.5 * acc,))
    return out, (h, gate, up, act)


def _ffn_bwd(dout, dout_bf, x, g, wg_t, wu_t, wd, saved, tag):
    h, gate, up, act = saved

    def swiglu_bwd(acc, a, b):
        a, b, half = a.astype(F32), b.astype(F32), 0.5 * acc
        s = jax.nn.sigmoid(a)
        return half * b * (s * (1.0 + a * (1.0 - s))), half * (a * s)

    dgate, dup = _mm([dout_bf], [wd], name=tag + "_down_dx", tb=True, extras=[gate, up], out_dtypes=(BF, BF),
                     tm=512, tn=FF_TILE, epilogue=swiglu_bwd)
    dwd = _mm([act], [dout_bf], name=tag + "_down_dw", ta=True, tm=FF_TILE, tk=1024, out_dtypes=(WGRAD,),
              epilogue=lambda acc: (0.5 * acc,))
    dwg_t, dwu_t = _mm([dgate, dup], [h, h], name=tag + "_up_dw", ta=True, separate=True, out_dtypes=(WGRAD, WGRAD),
                       tm=FF_TILE, tk=1024)
    dh = _mm([dgate, dup], [wg_t, wu_t], name=tag + "_up_dx")
    dx, dx_bf, dg = _rms_bwd(dh, x, g, name=tag + "_norm_bwd", add=dout, out_dtypes=(F32, BF))
    return dx, dx_bf, dg, dwg_t, dwu_t, dwd


KPE_BLK = (U_SMALL + S_KPE) // HP
SMALL_BLK = U_SMALL // SMALL_W


def _local_step(x, pos_col, target, W, P):
    T = x.shape[0]
    sig = jax.nn.sigmoid
    x1, ffn1 = _ffn_fwd(x, P["ffn1_norm"], W["wg1"], W["wu1"], W["wd1"], "ffn1")
    h = _rms_fwd(x1, P["mix_norm"], name="mix_norm")
    u = _mm([h], [W["w_in"]], name="in_proj", tb=True, tn=1152)
    cqn = _rms_fwd(u, P["q_a_norm"], name="q_a_norm", blk_w=SMALL_W, blk_idx=SMALL_BLK, off=S_CQ, width=Q_LORA)
    ckvn = _rms_fwd(u, P["kv_a_norm"], name="kv_a_norm", blk_w=SMALL_W, blk_idx=SMALL_BLK, off=S_CKV, width=KV_LORA)
    q_raw = _mm([cqn], [W["wq"]], name="q_proj")
    def with_ones_lane(acc_k, acc_v):
        lane = lax.broadcasted_iota(jnp.int32, acc_v.shape, 1)
        return acc_k, jnp.where((lane & (HP - 1)) == V_HEAD, 1.0, acc_v)

    k_raw, v = _mm([ckvn], [W["wk"], W["wv"]], name="kv_proj", out_dtypes=(F32, BF), epilogue=with_ones_lane)
    rc, rs = _rope_tables(pos_col, P["freq"])
    q = _qk_prep_fwd(q_raw, None, P["q_head_norm"], rc, rs, name="q_prep", out_scale=Q_SCALE)
    k = _qk_prep_fwd(k_raw, u, P["k_head_norm"], rc, rs, name="k_prep", kpe_blk=KPE_BLK)
    o, lse = _attn_fwd(q, k, v)
    pre, act = _conv_fwd(u, P["conv_w8"], P["conv_b"])
    scan_cols, scan_rows = _ssd_prep(u, P["dt_bias8"], P["a_log8"])
    y_f, st_f = _ssd_fwd(act, scan_cols, scan_rows, rev=False, name="ssd_fwd_f")
    y_b, st_b = _ssd_fwd(act, scan_cols, scan_rows, rev=True, name="ssd_fwd_b")
    ysum, m = _ssm_combine_fwd(y_f, y_b, act, u, P["d_skip_lanes"], P["ssm_norm"])
    ab = _mm([o], [W["pa"]], name="attn_branch")
    mb, merged = _mm([m], [W["pb"]], name="ssm_branch", extras=[ab, u, u], extra_offs=(0, U_GA, U_GB), out_dtypes=(F32, BF),
                     epilogue=lambda acc, a, ga, gb: (acc, sig(ga) * a + sig(gb) * acc))
    x2 = _mm([merged], [W["wo"]], name="out_proj", extras=[x1], epilogue=lambda acc, r: (r + acc,))
    y, ffn2 = _ffn_fwd(x2, P["ffn2_norm"], W["wg2"], W["wu2"], W["wd2"], "ffn2")
    dy, dy_bf, loss = _loss_head(y, target)
    dx2, dx2_bf, dg_ffn2, dwg2, dwu2, dwd2 = _ffn_bwd(dy, dy_bf, x2, P["ffn2_norm"], W["wg2"], W["wu2"], W["wd2"], ffn2,
                                                      "ffn2")

    def gate_bwd(dmrg, a, b, ga, gb):
        sa, sb = sig(ga), sig(gb)
        return dmrg * sa, dmrg * sb, dmrg * a * sa * (1.0 - sa), dmrg * b * sb * (1.0 - sb)

    dab, dmb, dga, dgb = _mm([dx2_bf], [W["wo"]], name="out_proj_dx", tb=True, extras=[ab, mb, u, u],
                             extra_offs=(0, 0, U_GA, U_GB), out_dtypes=(BF,) * 4, epilogue=gate_bwd)
    dwo = _mm([merged], [dx2_bf], name="out_proj_dw", ta=True, out_dtypes=(WGRAD,))
    dpa = _mm([o], [dab], name="attn_branch_dw", ta=True, out_dtypes=(WGRAD,))
    do = _mm([dab], [W["pa"]], name="attn_branch_dx", tb=True)
    dpb = _mm([m], [dmb], name="ssm_branch_dw", ta=True, out_dtypes=(WGRAD,))
    dm = _mm([dmb], [W["pb"]], name="ssm_branch_dx", tb=True)
    dyssd, dz, dxs_skip, dg_ssm, dskip = _ssm_combine_bwd(dm, ysum, act, u, P["d_skip_lanes"], P["ssm_norm"])
    dxs_f, db_f, dc_f, dsel_f, dtot_f = _ssd_bwd(act, scan_cols, scan_rows, st_f, dyssd, rev=False, name="ssd_bwd_f")
    dxs_b, db_b, dc_b, dsel_b, dtot_b = _ssd_bwd(act, scan_cols, scan_rows, st_b, dyssd, rev=True, name="ssd_bwd_b")
    ddt, dalog, dbias = _ssd_prep_bwd(u, P["dt_bias8"], P["a_log8"], dsel_f, dtot_f, dsel_b, dtot_b)
    dxbc, dconv = [], []
    for tag, col0, parts in (("x", 0, [dxs_f, dxs_b, dxs_skip]), ("b", D_INNER, [db_f, db_b]),
                             ("c", D_INNER + SSM_GROUPS * D_STATE, [dc_f, dc_b])):
        dpre = _conv_dpre(parts, pre, col0, name="conv_dpre_" + tag)
        dxp, dwp = _conv_bwd(dpre, u, P["conv_w8"], col0, name="conv_bwd_" + tag)
        dxbc.append(dxp)
        dconv.append(dwp)
    dconv = jnp.concatenate(dconv, axis=1)
    dq, dk, dv = _attn_bwd(q, k, v, do, o, lse)
    dq_raw, dg_qh = _qk_prep_bwd(dq, q_raw, None, P["q_head_norm"], rc, rs, name="q_prep_bwd", in_scale=ATTN_SCALE)
    dk_raw, dg_kh, dkpe = _qk_prep_bwd(dk, k_raw, u, P["k_head_norm"], rc, rs, name="k_prep_bwd", kpe_blk=KPE_BLK,
                                       in_scale=1.0 / LOG2E)
    dwq = _mm([cqn], [dq_raw], name="q_proj_dw", ta=True, out_dtypes=(WGRAD,))
    dcqn = _mm([dq_raw], [W["wq"]], name="q_proj_dx", tb=True)
    dwk, dwv = _mm([ckvn], [dk_raw, dv], name="kv_proj_dw", ta=True, out_dtypes=(WGRAD, WGRAD))
    dckvn = _mm([dk_raw, dv], [W["wk"], W["wv"]], name="kv_proj_dx", tb=True)
    dcq, dg_qa = _rms_bwd(dcqn, u, P["q_a_norm"], name="q_a_norm_bwd", blk_w=SMALL_W, blk_idx=SMALL_BLK, off=S_CQ,
                          width=Q_LORA, out_dtypes=(BF,))
    dckv, dg_kva = _rms_bwd(dckvn, u, P["kv_a_norm"], name="kv_a_norm_bwd", blk_w=SMALL_W, blk_idx=SMALL_BLK,
                            off=S_CKV, width=KV_LORA, out_dtypes=(BF,))
    du = jnp.concatenate([dz, dga, dgb] + dxbc + [dcq, dckv, dkpe.astype(BF), ddt.astype(BF)], axis=1)
    dw_in = _mm([du], [h], name="in_proj_dw", ta=True, tm=1152, out_dtypes=(WGRAD,))
    dh = _mm([du], [W["w_in"]], name="in_proj_dx", tk=U_PAD // 3)
    dx1, dx1_bf, dg_mix = _rms_bwd(dh, x1, P["mix_norm"], name="mix_norm_bwd", add=dx2, out_dtypes=(F32, BF))
    dx, _, dg_ffn1, dwg1, dwu1, dwd1 = _ffn_bwd(dx1, dx1_bf, x, P["ffn1_norm"], W["wg1"], W["wu1"], W["wd1"], ffn1, "ffn1")
    dW = dict(wg1=dwg1, wu1=dwu1, wd1=dwd1, w_in=dw_in, wq=dwq, wk=dwk, wv=dwv, pa=dpa, pb=dpb, wo=dwo,
              wg2=dwg2, wu2=dwu2, wd2=dwd2)
    dP = dict(ffn1_norm=dg_ffn1[0], mix_norm=dg_mix[0], q_a_norm=dg_qa[0], kv_a_norm=dg_kva[0],
              q_head_norm=dg_qh[0, :QK_HEAD], k_head_norm=dg_kh[0, :QK_HEAD], conv_b=dconv[CONV_WIDTH],
              a_log_fwd=dalog[0, :SSM_HEADS], a_log_bwd=dalog[0, SSM_HEADS:2 * SSM_HEADS],
              dt_bias_fwd=dbias[0, :SSM_HEADS], dt_bias_bwd=dbias[0, SSM_HEADS:2 * SSM_HEADS],
              d_skip=dskip[0].reshape(SSM_GROUPS, HP)[:, :HG], ssm_norm=dg_ssm[0], ffn2_norm=dg_ffn2[0],
              conv_w=dconv[:CONV_WIDTH], loss=loss[0, 0])
    return dx, dW, dP


def _prepare(w, conv_w_full):
    kvb = w["w_kv_b"]
    W = dict(wg1=w["ffn1_w_gate"], wu1=w["ffn1_w_up"], wd1=w["ffn1_w_down"], w_in=_pad_w_in(w["w_in"]),
             wq=_pad_heads(w["w_q_b"], 1, QK_HEAD, 0, QK_HEAD),
             wk=_pad_heads(kvb, 1, QK_NOPE + V_HEAD, 0, QK_NOPE),
             wv=_pad_heads(kvb, 1, QK_NOPE + V_HEAD, QK_NOPE, QK_NOPE + V_HEAD),
             pa=_pad_heads(w["w_attn_branch"], 0, V_HEAD, 0, V_HEAD), pb=w["w_ssm_branch"], wo=w["w_out"],
             wg2=w["ffn2_w_gate"], wu2=w["ffn2_w_up"], wd2=w["ffn2_w_down"])
    inv_freq = [1.0 / (ROPE_BASE ** (j / QK_ROPE)) for j in range(0, QK_ROPE, 2)]
    freq = [0.0] * QK_NOPE + inv_freq + inv_freq + [0.0] * (HP - QK_HEAD)
    P = {n: w[n] for n in ("ffn1_norm", "mix_norm", "q_a_norm", "kv_a_norm", "ssm_norm", "ffn2_norm", "conv_b")}
    P.update(q_head_norm=_lanes128([w["q_head_norm"]]), k_head_norm=_lanes128([w["k_head_norm"]]),
             conv_w8=jnp.pad(conv_w_full, ((0, 8 - CONV_WIDTH), (0, 0))),
             dt_bias8=jnp.broadcast_to(_lanes128([w["dt_bias_fwd"], w["dt_bias_bwd"]]), (8, HP)),
             a_log8=jnp.broadcast_to(_lanes128([w["a_log_fwd"], w["a_log_bwd"]]), (8, HP)),
             d_skip_lanes=jnp.repeat(w["d_skip"].reshape(-1), PH).reshape(1, D_INNER),
             freq=jnp.asarray(freq, F32).reshape(1, HP))
    return W, P


def _unprepare(dW):
    dkvb = jnp.concatenate([_unpad_heads(dW["wk"], 1, QK_NOPE), _unpad_heads(dW["wv"], 1, V_HEAD)], axis=2)
    return dict(ffn1_w_gate=dW["wg1"], ffn1_w_up=dW["wu1"], ffn1_w_down=dW["wd1"], w_in=_unpad_w_in(dW["w_in"]),
                w_q_b=_unpad_heads(dW["wq"], 1, QK_HEAD).reshape(Q_LORA, N_HEADS * QK_HEAD),
                w_kv_b=dkvb.reshape(KV_LORA, N_HEADS * (QK_NOPE + V_HEAD)),
                w_attn_branch=_unpad_heads(dW["pa"], 0, V_HEAD).reshape(N_HEADS * V_HEAD, D_MODEL),
                w_ssm_branch=dW["pb"], w_out=dW["wo"],
                ffn2_w_gate=dW["wg2"], ffn2_w_up=dW["wu2"], ffn2_w_down=dW["wd2"])


def kernel(x, positions, ffn1_norm, ffn1_w_gate, ffn1_w_up, ffn1_w_down, mix_norm, w_in, q_a_norm, w_q_b, kv_a_norm, w_kv_b, q_head_norm, k_head_norm, conv_w, conv_b, a_log_fwd, a_log_bwd, dt_bias_fwd, dt_bias_bwd, d_skip, ssm_norm, w_attn_branch, w_ssm_branch, w_out, ffn2_norm, ffn2_w_gate, ffn2_w_up, ffn2_w_down, loss_target, m_ffn1_norm, m_ffn1_w_gate, m_ffn1_w_up, m_ffn1_w_down, m_mix_norm, m_w_in, m_q_a_norm, m_w_q_b, m_kv_a_norm, m_w_kv_b, m_q_head_norm, m_k_head_norm, m_conv_w, m_conv_b, m_a_log_fwd, m_a_log_bwd, m_dt_bias_fwd, m_dt_bias_bwd, m_d_skip, m_ssm_norm, m_w_attn_branch, m_w_ssm_branch, m_w_out, m_ffn2_norm, m_ffn2_w_gate, m_ffn2_w_up, m_ffn2_w_down, v_ffn1_norm, v_ffn1_w_gate, v_ffn1_w_up, v_ffn1_w_down, v_mix_norm, v_w_in, v_q_a_norm, v_w_q_b, v_kv_a_norm, v_w_kv_b, v_q_head_norm, v_k_head_norm, v_conv_w, v_conv_b, v_a_log_fwd, v_a_log_bwd, v_dt_bias_fwd, v_dt_bias_bwd, v_d_skip, v_ssm_norm, v_w_attn_branch, v_w_ssm_branch, v_w_out, v_ffn2_norm, v_ffn2_w_gate, v_ffn2_w_up, v_ffn2_w_down):
    given = dict(locals())
    T = x.shape[1]
    packed_names = [name for name, _, _ in PACKED]

    def two_d(a):
        return a.reshape(a.shape[1], -1) if a.ndim > 2 else a

    def kept(n, a):
        return _stored(n, two_d(a))

    w_loc = {n: kept(n, given[n]) for n in WEIGHTS}
    wb = _pack({n: w_loc[n].astype(BF) for n in packed_names})
    wf = jnp.pad(w_loc["conv_w"], ((0, 8 - CONV_WIDTH), (0, 0)))
    gb, gf = _gather_chips(wb, wf)
    full = _full_from_slots(gb)
    conv_w_full = jnp.concatenate([gf[j, :CONV_WIDTH] for j in range(N_CHIPS)], axis=1)
    full.update({n: w_loc[n] for n in WEIGHTS if n not in full and n != "conv_w"})
    W, P = _prepare(full, conv_w_full)
    dx, dW, dP = _local_step(x.reshape(T, D_MODEL), positions.reshape(T, 1).astype(F32), loss_target.reshape(T, D_MODEL), W, P)
    gp = _slots_from_full(_unprepare(dW))
    core = lax.axis_index("c").astype(jnp.int32).reshape(1)
    both_cores = _add_halves(gp, _halves_to_sibling(gp), core)
    cx, cy = lax.axis_index("x"), lax.axis_index("y")
    chips = jnp.stack([2 * cx + cy, 2 * (1 - cx) + cy, 2 * cx + (1 - cy)]).astype(jnp.int32)
    part, on = _add_near(both_cores, _exchange_near(both_cores), chips)
    grads = _unpack(_join_halves(_add_far(part, _exchange_far(on), core)))
    small = _unpack_small(_allreduce_small(_pack_small(dP)))
    grads.update({n: small[n].reshape(1, -1) for n, _ in SMALL if n not in ("conv_w", "loss")})
    grads["conv_w"] = lax.dynamic_slice_in_dim(small["conv_w"].reshape(CONV_WIDTH, XBC_DIM), _my_chip() * (XBC_DIM // N_CHIPS),
                                               XBC_DIM // N_CHIPS, axis=1)
    out_g, out_d, out_m, out_v = [], [], [], []
    for n in WEIGHTS:
        shape = given[n].shape
        delta, new_m, new_v = _adamw(w_loc[n], grads[n], kept(n, given["m_" + n]), kept(n, given["v_" + n]), name="adamw_" + n)
        for outs, a in ((out_g, grads[n]), (out_d, delta), (out_m, new_m), (out_v, new_v)):
            outs.append(_stored(n, a).reshape(shape))
    return (small["loss"].reshape(()), dx.reshape(x.shape), *out_g, *out_d, *out_m, *out_v)
```

```python
import functools
import math

import jax
import jax.numpy as jnp
from jax import lax
from jax.experimental import pallas as pl
from jax.experimental.pallas import tpu as pltpu

BF = jnp.bfloat16
F32 = jnp.float32
HI = lax.Precision.HIGHEST
MESH = pl.DeviceIdType.MESH

D_MODEL = 1024
D_FF = 2816
EPS = 1e-6
N_HEADS = 16
QK_NOPE = 64
QK_ROPE = 32
QK_HEAD = 96
V_HEAD = 64
Q_LORA = 384
KV_LORA = 256
ROPE_BASE = 10000.0
D_INNER = 2048
SSM_HEADS = 32
SSM_GROUPS = 4
D_STATE = 128
CONV_WIDTH = 5
CHUNK = 128
XBC_DIM = 3072
HP = 128
GW = D_INNER // SSM_GROUPS
HG = SSM_HEADS // SSM_GROUPS
PH = 64
U_Z, U_GA, U_GB, U_XBC, U_SMALL = 0, 2048, 3072, 4096, 7168
S_CQ, S_CKV, S_KPE, S_DT, SMALL_W = 0, 384, 640, 768, 896
U_PAD = U_SMALL + SMALL_W
IN_SPLITS = (Q_LORA, KV_LORA, QK_ROPE, D_INNER, XBC_DIM, SSM_HEADS, SSM_HEADS, D_MODEL, D_MODEL)

ADAM_LR = 0.001
ADAM_B1 = 0.9
ADAM_B2 = 0.999
ADAM_EPS = 1e-08
ADAM_WD = 0.01
ADAM_STEP = 10

NN = (((1,), (0,)), ((), ()))
NT = (((1,), (1,)), ((), ()))
TN = (((0,), (0,)), ((), ()))


def _pick(n, pref):
    best = None
    d = 128
    while d <= min(n, pref):
        if n % d == 0:
            best = d
        d += 128
    return best if best is not None else n


def _silu(x):
    return x * jax.nn.sigmoid(x)


def _dsilu(x):
    s = jax.nn.sigmoid(x)
    return s * (1.0 + x * (1.0 - s))


def _softplus(x):
    return jnp.maximum(x, 0.0) + jnp.log(1.0 + jnp.exp(-jnp.abs(x)))


def _mm(As, Bs, *, name, ta=False, tb=False, out_dtypes=(F32,), epilogue=None, extras=(), extra_offs=None,
        tm=1024, tn=512, tk=2048, separate=False):
    As, Bs, extras = list(As), list(Bs), list(extras)
    a0, b0 = As[0], Bs[0]
    M, K = (a0.shape[1], a0.shape[0]) if ta else a0.shape
    N = b0.shape[0] if tb else b0.shape[1]
    tm, tn, tk = _pick(M, tm), _pick(N, tn), _pick(K, tk)
    nk = K // tk
    n_a, n_b, n_e, n_o = len(As), len(Bs), len(extras), len(out_dtypes)
    n_acc = (n_b if n_a == 1 or separate else 1) if nk > 1 else 0
    if extra_offs is None:
        extra_offs = (0,) * n_e
    dn = (((0,) if ta else (1,), (1,) if tb else (0,)), ((), ()))
    bytes_a = sum(a.size * a.dtype.itemsize for a in As)
    bytes_b = sum(b.size * b.dtype.itemsize for b in Bs)
    n_outer = (N // tn) * bytes_a + bytes_b < (M // tm) * bytes_b + bytes_a

    def products(a_refs, b_refs):
        if n_a == 1:
            a = a_refs[0][...].astype(BF)
            return [lax.dot_general(a, b[...].astype(BF), dn, preferred_element_type=F32) for b in b_refs]
        if separate:
            return [lax.dot_general(a[...].astype(BF), b[...].astype(BF), dn, preferred_element_type=F32)
                    for a, b in zip(a_refs, b_refs)]
        total = None
        for a, b in zip(a_refs, b_refs):
            p = lax.dot_general(a[...].astype(BF), b[...].astype(BF), dn, preferred_element_type=F32)
            total = p if total is None else total + p
        return [total]

    def finish(accs, e_refs, o_refs):
        ex = [e[...] for e in e_refs]
        outs = epilogue(*accs, *ex) if epilogue is not None else tuple(accs)
        for o_ref, val in zip(o_refs, outs):
            o_ref[...] = val.astype(o_ref.dtype)

    def body(*refs):
        a_refs, b_refs = refs[:n_a], refs[n_a:n_a + n_b]
        e_refs = refs[n_a + n_b:n_a + n_b + n_e]
        o_refs = refs[n_a + n_b + n_e:n_a + n_b + n_e + n_o]
        acc_refs = refs[n_a + n_b + n_e + n_o:]
        if nk == 1:
            finish(products(a_refs, b_refs), e_refs, o_refs)
            return
        k = pl.program_id(2)

        @pl.when(k == 0)
        def _():
            for acc in acc_refs:
                acc[...] = jnp.zeros_like(acc)

        for acc, p in zip(acc_refs, products(a_refs, b_refs)):
            acc[...] += p

        @pl.when(k == nk - 1)
        def _():
            finish([acc[...] for acc in acc_refs], e_refs, o_refs)

    def at(f):
        return (lambda j, i, k: f(i, j, k)) if n_outer else f

    a_spec = pl.BlockSpec((tk, tm), at(lambda i, j, k: (k, i))) if ta else pl.BlockSpec((tm, tk), at(lambda i, j, k: (i, k)))
    b_spec = pl.BlockSpec((tn, tk), at(lambda i, j, k: (j, k))) if tb else pl.BlockSpec((tk, tn), at(lambda i, j, k: (k, j)))
    e_specs = [pl.BlockSpec((tm, tn), at(functools.partial(lambda i, j, k, o: (i, j + o), o=off // tn))) for off in extra_offs]
    for off in extra_offs:
        assert off % tn == 0
    outs = pl.pallas_call(
        body, name=name,
        out_shape=tuple(jax.ShapeDtypeStruct((M, N), dt) for dt in out_dtypes),
        grid=(N // tn, M // tm, nk) if n_outer else (M // tm, N // tn, nk),
        in_specs=[a_spec] * n_a + [b_spec] * n_b + e_specs,
        out_specs=tuple(pl.BlockSpec((tm, tn), at(lambda i, j, k: (i, j))) for _ in out_dtypes),
        scratch_shapes=[pltpu.VMEM((tm, tn), F32)] * n_acc,
        compiler_params=pltpu.CompilerParams(dimension_semantics=("parallel", "parallel", "arbitrary")),
    )(*As, *Bs, *extras)
    return outs[0] if n_o == 1 else outs


def _rms_fwd(x, g, *, name, blk_w=None, blk_idx=0, off=0, width=None, out_dtype=BF):
    T = x.shape[0]
    blk_w = x.shape[1] if blk_w is None else blk_w
    width = blk_w if width is None else width
    tt = _pick(T, 512)

    def body(x_ref, g_ref, o_ref):
        xf = x_ref[:, off:off + width]
        r = lax.rsqrt(jnp.mean(xf * xf, axis=-1, keepdims=True) + EPS)
        o_ref[...] = (xf * r * g_ref[...]).astype(o_ref.dtype)

    return pl.pallas_call(
        body, name=name, out_shape=jax.ShapeDtypeStruct((T, width), out_dtype), grid=(T // tt,),
        in_specs=[pl.BlockSpec((tt, blk_w), lambda i: (i, blk_idx)), pl.BlockSpec((1, width), lambda i: (0, 0))],
        out_specs=pl.BlockSpec((tt, width), lambda i: (i, 0)),
    )(x, g)


def _rms_bwd(dy, x, g, *, name, blk_w=None, blk_idx=0, off=0, width=None, add=None, out_dtypes=(F32,)):
    T = x.shape[0]
    blk_w = x.shape[1] if blk_w is None else blk_w
    width = blk_w if width is None else width
    tt = _pick(T, 512)
    has_add = add is not None
    n_dx = len(out_dtypes)

    def body(*refs):
        dy_ref, x_ref, g_ref = refs[:3]
        dx_refs, dg_ref = refs[3 + has_add:3 + has_add + n_dx], refs[-1]
        xf = x_ref[:, off:off + width]
        d = dy_ref[...].astype(F32)
        r = lax.rsqrt(jnp.mean(xf * xf, axis=-1, keepdims=True) + EPS)
        gd = d * g_ref[...]
        dx = r * gd - xf * (r * r * r) * jnp.mean(gd * xf, axis=-1, keepdims=True)
        if has_add:
            dx = dx + refs[3][...]
        for dx_ref in dx_refs:
            dx_ref[...] = dx.astype(dx_ref.dtype)

        @pl.when(pl.program_id(0) == 0)
        def _():
            dg_ref[...] = jnp.zeros_like(dg_ref)

        dg_ref[...] += jnp.broadcast_to(jnp.sum(d * xf * r, axis=0, keepdims=True), dg_ref.shape)

    row = pl.BlockSpec((tt, width), lambda i: (i, 0))
    in_specs = [row, pl.BlockSpec((tt, blk_w), lambda i: (i, blk_idx)), pl.BlockSpec((1, width), lambda i: (0, 0))]
    args = [dy, x, g]
    if has_add:
        in_specs.append(row)
        args.append(add)
    return pl.pallas_call(
        body, name=name,
        out_shape=tuple(jax.ShapeDtypeStruct((T, width), dt) for dt in out_dtypes) + (jax.ShapeDtypeStruct((8, width), F32),),
        grid=(T // tt,), in_specs=in_specs,
        out_specs=(row,) * n_dx + (pl.BlockSpec((8, width), lambda i: (0, 0)),),
        compiler_params=pltpu.CompilerParams(dimension_semantics=("arbitrary",)),
    )(*args)


def _rope_tables(pos_col, freq_lane):
    T = pos_col.shape[0]
    tt = _pick(T, 512)

    def body(p_ref, f_ref, c_ref, s_ref):
        ang = p_ref[...] * f_ref[...]
        lane = lax.broadcasted_iota(jnp.int32, ang.shape, 1)
        c_ref[...] = jnp.where(lane < QK_HEAD, jnp.cos(ang), 0.0)
        sn = jnp.sin(ang)
        s_ref[...] = jnp.where((lane >= QK_NOPE) & (lane < QK_NOPE + 16), -sn,
                               jnp.where((lane >= QK_NOPE + 16) & (lane < QK_HEAD), sn, 0.0))

    return pl.pallas_call(
        body, name="rope_tables", out_shape=(jax.ShapeDtypeStruct((T, HP), F32),) * 2, grid=(T // tt,),
        in_specs=[pl.BlockSpec((tt, 1), lambda i: (i, 0)), pl.BlockSpec((1, HP), lambda i: (0, 0))],
        out_specs=(pl.BlockSpec((tt, HP), lambda i: (i, 0)),) * 2,
    )(pos_col, freq_lane)


def _swap_rope_halves(n):
    src = lax.broadcasted_iota(jnp.int32, (HP, HP), 0)
    dst = lax.broadcasted_iota(jnp.int32, (HP, HP), 1)
    lo = (dst >= QK_NOPE) & (dst < QK_NOPE + 16) & (src == dst + 16)
    hi = (dst >= QK_NOPE + 16) & (dst < QK_HEAD) & (src == dst - 16)
    return _split_dot(n, jnp.where(lo | hi, 1.0, 0.0).astype(BF), 2)


def _qk_prep_fwd(raw, kpe, gain, C, S, *, name, kpe_blk=0, out_scale=1.0):
    T = raw.shape[0]
    tt = _pick(T, 256)
    has_kpe = kpe is not None

    def body(*refs):
        if has_kpe:
            raw_ref, kpe_ref, g_ref, c_ref, s_ref, o_ref = refs
        else:
            raw_ref, g_ref, c_ref, s_ref, o_ref = refs
        for h in range(N_HEADS):
            hs = slice(HP * h, HP * (h + 1))
            xr = raw_ref[:, hs] + kpe_ref[...] if has_kpe else raw_ref[:, hs]
            r = lax.rsqrt(jnp.sum(xr * xr, axis=-1, keepdims=True) * (1.0 / QK_HEAD) + EPS)
            n = xr * r * g_ref[...]
            o_ref[:, hs] = ((n * c_ref[...] + _swap_rope_halves(n) * s_ref[...]) * out_scale).astype(o_ref.dtype)

    heads = pl.BlockSpec((tt, N_HEADS * HP), lambda i: (i, 0))
    shared = pl.BlockSpec((tt, HP), lambda i: (i, 0))
    kpe_spec = pl.BlockSpec((tt, HP), lambda i: (i, kpe_blk))
    in_specs = [heads] + ([kpe_spec] if has_kpe else []) + [pl.BlockSpec((1, HP), lambda i: (0, 0)), shared, shared]
    args = [raw] + ([kpe] if has_kpe else []) + [gain, C, S]
    return pl.pallas_call(
        body, name=name, out_shape=jax.ShapeDtypeStruct(raw.shape, BF), grid=(T // tt,),
        in_specs=in_specs, out_specs=heads,
    )(*args)


def _qk_prep_bwd(dout, raw, kpe, gain, C, S, *, name, kpe_blk=0, in_scale=1.0):
    T = raw.shape[0]
    tt = _pick(T, 256)
    has_kpe = kpe is not None

    def body(*refs):
        if has_kpe:
            d_ref, raw_ref, kpe_ref, g_ref, c_ref, s_ref, dx_ref, dg_ref, dkpe_ref = refs
        else:
            d_ref, raw_ref, g_ref, c_ref, s_ref, dx_ref, dg_ref = refs
        dg = jnp.zeros((1, HP), F32)
        dkpe = jnp.zeros((tt, HP), F32)
        for h in range(N_HEADS):
            hs = slice(HP * h, HP * (h + 1))
            xr = raw_ref[:, hs] + kpe_ref[...] if has_kpe else raw_ref[:, hs]
            d = d_ref[:, hs].astype(F32) * in_scale
            r = lax.rsqrt(jnp.sum(xr * xr, axis=-1, keepdims=True) * (1.0 / QK_HEAD) + EPS)
            dn = d * c_ref[...] + _swap_rope_halves(d * s_ref[...])
            gd = dn * g_ref[...]
            dx = r * gd - xr * (r * r * r) * (jnp.sum(gd * xr, axis=-1, keepdims=True) * (1.0 / QK_HEAD))
            dx_ref[:, hs] = dx.astype(dx_ref.dtype)
            dg = dg + jnp.sum(dn * xr * r, axis=0, keepdims=True)
            dkpe = dkpe + dx

        @pl.when(pl.program_id(0) == 0)
        def _():
            dg_ref[...] = jnp.zeros_like(dg_ref)

        dg_ref[...] += jnp.broadcast_to(dg, dg_ref.shape)
        if has_kpe:
            dkpe_ref[...] = dkpe

    heads = pl.BlockSpec((tt, N_HEADS * HP), lambda i: (i, 0))
    shared = pl.BlockSpec((tt, HP), lambda i: (i, 0))
    kpe_spec = pl.BlockSpec((tt, HP), lambda i: (i, kpe_blk))
    in_specs = [heads, heads] + ([kpe_spec] if has_kpe else []) + [pl.BlockSpec((1, HP), lambda i: (0, 0)), shared, shared]
    args = [dout, raw] + ([kpe] if has_kpe else []) + [gain, C, S]
    out_shape = [jax.ShapeDtypeStruct(raw.shape, BF), jax.ShapeDtypeStruct((8, HP), F32)]
    out_specs = [heads, pl.BlockSpec((8, HP), lambda i: (0, 0))]
    if has_kpe:
        out_shape.append(jax.ShapeDtypeStruct((T, HP), F32))
        out_specs.append(shared)
    return pl.pallas_call(
        body, name=name, out_shape=tuple(out_shape), grid=(T // tt,),
        in_specs=in_specs, out_specs=tuple(out_specs),
        compiler_params=pltpu.CompilerParams(dimension_semantics=("arbitrary",)),
    )(*args)


ATTN_SCALE = 1.0 / math.sqrt(QK_HEAD)
LOG2E = 1.0 / math.log(2.0)
Q_SCALE = ATTN_SCALE * LOG2E


def _attn_fwd(q, k, v):
    T = q.shape[0]
    tq = _pick(T, 1024)
    tk = _pick(T, 1024)

    def body(q_ref, k_ref, v_ref, o_ref, ob_ref, lse_ref):
        qt = q_ref[...]
        m = o = None
        for j in range(T // tk):
            ks = slice(j * tk, (j + 1) * tk)
            s = lax.dot_general(qt, k_ref[ks, :], NT, preferred_element_type=F32)
            m_j = jnp.max(s, axis=-1, keepdims=True)
            m_new = m_j if m is None else jnp.maximum(m, m_j)
            o_j = jnp.dot(jnp.exp2(s - m_new).astype(BF), v_ref[ks, :], preferred_element_type=F32)
            o = o_j if o is None else o * jnp.exp2(m - m_new) + o_j
            m = m_new
        l = o[:, V_HEAD:V_HEAD + 1]
        o = o / l
        o_ref[...] = o
        ob_ref[...] = o.astype(ob_ref.dtype)
        lse_ref[...] = jnp.broadcast_to(m + jnp.log2(l), lse_ref.shape)

    qs = pl.BlockSpec((tq, HP), lambda h, i: (i, h))
    kv = pl.BlockSpec((T, HP), lambda h, i: (0, h))
    return pl.pallas_call(
        body, name="attn_fwd",
        out_shape=(jax.ShapeDtypeStruct(q.shape, F32), jax.ShapeDtypeStruct(q.shape, BF), jax.ShapeDtypeStruct(q.shape, F32)),
        grid=(N_HEADS, T // tq), in_specs=[qs, kv, kv], out_specs=(qs, qs, qs),
        compiler_params=pltpu.CompilerParams(dimension_semantics=("parallel", "parallel")),
    )(q, k, v)


def _attn_bwd(q, k, v, do, o, lse):
    T = q.shape[0]
    tb = _pick(T, 512)
    nb = T // tb
    tkey = _pick(T, 1024)

    def body(q_ref, k_ref, v_ref, do_ref, o_ref, lse_ref, dq_ref, dk_ref, dv_ref, delta_rows, lse_rows, dob_scr, dv_acc):
        dq_ref[...] = jnp.zeros_like(dq_ref)
        dk_ref[...] = jnp.zeros_like(dk_ref)
        lane = lax.broadcasted_iota(jnp.int32, (8, HP), 1)
        ones8 = jnp.ones((8, HP), BF)
        first8 = jnp.where(lane == 0, 1.0, 0.0).astype(BF)

        def as_rows(pick, v):
            total, rest = None, v
            for _ in range(3):
                piece = rest.astype(BF)
                part = lax.dot_general(pick, piece, NT, preferred_element_type=F32)
                total = part if total is None else total + part
                rest = rest - piece.astype(F32)
            return total

        def per_q_tile(i, carry):
            qs = pl.ds(pl.multiple_of(i * tb, tb), tb)
            doi = do_ref[qs, :]
            delta_rows[i] = as_rows(ones8, doi * o_ref[qs, :])
            lse_rows[i] = as_rows(first8, lse_ref[qs, :])
            dob_scr[qs, :] = doi.astype(BF)
            return carry

        lax.fori_loop(0, nb, per_q_tile, 0)

        def k_loop(j, carry):
            ks = pl.ds(pl.multiple_of(j * tkey, tkey), tkey)
            kj, vj = k_ref[ks, :], v_ref[ks, :]

            dv_acc[...] = jnp.zeros_like(dv_acc)

            def q_loop(i, carry_q):
                qs = pl.ds(pl.multiple_of(i * tb, tb), tb)
                qi = q_ref[qs, :]
                dob = dob_scr[qs, :]
                s_t = lax.dot_general(kj, qi, NT, preferred_element_type=F32)
                p_t = jnp.exp2(s_t - lse_rows[i, 0:1, :])
                dp_t = lax.dot_general(vj, dob, NT, preferred_element_type=F32)
                ds_t = (p_t * (dp_t - delta_rows[i, 0:1, :])).astype(BF)
                dv_acc[...] += jnp.dot(p_t.astype(BF), dob, preferred_element_type=F32)
                dk_ref[ks, :] += jnp.dot(ds_t, qi, preferred_element_type=F32)
                dq_ref[qs, :] += lax.dot_general(ds_t, kj, TN, preferred_element_type=F32)
                return carry_q

            lax.fori_loop(0, nb, q_loop, 0)
            dv_ref[ks, :] = dv_acc[...].astype(dv_ref.dtype)
            return carry

        lax.fori_loop(0, T // tkey, k_loop, 0)

    spec = pl.BlockSpec((T, HP), lambda h: (0, h))
    return pl.pallas_call(
        body, name="attn_bwd",
        out_shape=(jax.ShapeDtypeStruct(q.shape, F32), jax.ShapeDtypeStruct(q.shape, F32), jax.ShapeDtypeStruct(q.shape, BF)),
        grid=(N_HEADS,), in_specs=[spec] * 6, out_specs=(spec,) * 3,
        scratch_shapes=[pltpu.VMEM((nb, 8, tb), F32), pltpu.VMEM((nb, 8, tb), F32), pltpu.VMEM((T, HP), BF),
                        pltpu.VMEM((tkey, HP), F32)],
        compiler_params=pltpu.CompilerParams(dimension_semantics=("parallel",), vmem_limit_bytes=2 * 15 * T * HP * 2 + (8 << 20)),
    )(q, k, v, do, o, lse)


CONV_TC = 512
CONV_PAD = CONV_WIDTH // 2


def _halo_specs(tr, col_of):
    r8 = tr // 8
    cur = pl.BlockSpec((tr, CONV_TC), lambda j, i: (i, col_of(j)))
    prev = pl.BlockSpec((8, CONV_TC), lambda j, i: (jnp.maximum(i * r8 - 1, 0), col_of(j)))

    def nxt_map(j, i, n8):
        return (jnp.minimum((i + 1) * r8, n8 - 1), col_of(j))

    return cur, prev, nxt_map


def _with_halo(prev_ref, cur_ref, next_ref, i, n_i):
    prev = jnp.where(i == 0, 0.0, prev_ref[...].astype(F32))
    nxt = jnp.where(i == n_i - 1, 0.0, next_ref[...].astype(F32))
    return jnp.concatenate([prev, cur_ref[...].astype(F32), nxt], axis=0)


def _conv_fwd(u, w8, b):
    T = u.shape[0]
    tr = _pick(T, 512)
    n_i = T // tr
    c0 = U_XBC // CONV_TC
    cur, prev, nxt_map = _halo_specs(tr, lambda j: c0 + j)
    nxt = pl.BlockSpec((8, CONV_TC), functools.partial(nxt_map, n8=T // 8))

    def body(p_ref, c_ref, n_ref, w_ref, b_ref, pre_ref, act_ref):
        i = pl.program_id(1)
        full = _with_halo(p_ref, c_ref, n_ref, i, n_i)
        acc = jnp.broadcast_to(b_ref[...], (tr, CONV_TC))
        for kk in range(CONV_WIDTH):
            acc = acc + full[8 - CONV_PAD + kk:8 - CONV_PAD + kk + tr, :] * w_ref[kk:kk + 1, :]
        pre_ref[...] = acc
        act_ref[...] = _silu(acc)

    out = pl.BlockSpec((tr, CONV_TC), lambda j, i: (i, j))
    return pl.pallas_call(
        body, name="conv_fwd", out_shape=(jax.ShapeDtypeStruct((T, XBC_DIM), F32),) * 2,
        grid=(XBC_DIM // CONV_TC, n_i),
        in_specs=[prev, cur, nxt, pl.BlockSpec((8, CONV_TC), lambda j, i: (0, j)), pl.BlockSpec((1, CONV_TC), lambda j, i: (0, j))],
        out_specs=(out, out),
    )(u, u, u, w8, b)


def _conv_dpre(dacts, pre, col0, *, name):
    T, width = dacts[0].shape
    tt = _pick(T, 512)
    n_d = len(dacts)
    c0 = col0 // CONV_TC

    def body(*refs):
        d = refs[0][...]
        for r in refs[1:n_d]:
            d = d + r[...]
        refs[n_d + 1][...] = d * _dsilu(refs[n_d][...])

    blk = pl.BlockSpec((tt, CONV_TC), lambda j, i: (i, j))
    return pl.pallas_call(
        body, name=name, out_shape=jax.ShapeDtypeStruct((T, width), F32), grid=(width // CONV_TC, T // tt),
        in_specs=[blk] * n_d + [pl.BlockSpec((tt, CONV_TC), lambda j, i: (i, c0 + j))], out_specs=blk,
    )(*dacts, pre)


def _conv_bwd(dpre, u, w8, col0, *, name):
    T, width = dpre.shape
    tr = _pick(T, 512)
    n_i = T // tr
    cd = col0 // CONV_TC
    cx = (U_XBC + col0) // CONV_TC
    d_cur, d_prev, d_nxt_map = _halo_specs(tr, lambda j: j)
    x_cur, x_prev, x_nxt_map = _halo_specs(tr, lambda j: cx + j)
    d_nxt = pl.BlockSpec((8, CONV_TC), functools.partial(d_nxt_map, n8=T // 8))
    x_nxt = pl.BlockSpec((8, CONV_TC), functools.partial(x_nxt_map, n8=T // 8))

    def body(dp_ref, dc_ref, dn_ref, xp_ref, xc_ref, xn_ref, w_ref, dx_ref, dw_ref):
        i = pl.program_id(1)
        dfull = _with_halo(dp_ref, dc_ref, dn_ref, i, n_i)
        xfull = _with_halo(xp_ref, xc_ref, xn_ref, i, n_i)
        dcur = dc_ref[...]
        dx = jnp.zeros((tr, CONV_TC), F32)
        rows = []
        for kk in range(CONV_WIDTH):
            dx = dx + dfull[8 + CONV_PAD - kk:8 + CONV_PAD - kk + tr, :] * w_ref[kk:kk + 1, :]
            rows.append(jnp.sum(dcur * xfull[8 - CONV_PAD + kk:8 - CONV_PAD + kk + tr, :], axis=0, keepdims=True))
        rows.append(jnp.sum(dcur, axis=0, keepdims=True))
        rows.append(jnp.zeros((2, CONV_TC), F32))
        dx_ref[...] = dx.astype(dx_ref.dtype)

        @pl.when(i == 0)
        def _():
            dw_ref[...] = jnp.zeros_like(dw_ref)

        dw_ref[...] += jnp.concatenate(rows, axis=0)

    out = pl.BlockSpec((tr, CONV_TC), lambda j, i: (i, j))
    return pl.pallas_call(
        body, name=name, out_shape=(jax.ShapeDtypeStruct((T, width), BF), jax.ShapeDtypeStruct((8, width), F32)),
        grid=(width // CONV_TC, n_i),
        in_specs=[d_prev, d_cur, d_nxt, x_prev, x_cur, x_nxt, pl.BlockSpec((8, CONV_TC), lambda j, i: (0, cd + j))],
        out_specs=(out, pl.BlockSpec((8, CONV_TC), lambda j, i: (0, j))),
        compiler_params=pltpu.CompilerParams(dimension_semantics=("parallel", "arbitrary")),
    )(dpre, dpre, dpre, u, u, u, w8)


N_HB = 2 * SSM_GROUPS
P_DT, P_CS, P_E, P_W = 0, HP, 2 * HP, 3 * HP
DT_BLK = (U_SMALL + S_DT) // HP


def _tri(rev, transpose=False):
    rows = lax.broadcasted_iota(jnp.int32, (CHUNK, CHUNK), 0)
    cols = lax.broadcasted_iota(jnp.int32, (CHUNK, CHUNK), 1)
    if transpose:
        rows, cols = cols, rows
    return (cols >= rows) if rev else (cols <= rows)


def _ssd_prep(u, bias8, alog8):
    T = u.shape[0]
    nc = T // CHUNK

    def body(dt_ref, bias_ref, a_ref, cols_ref, rows_ref):
        lane = lax.broadcasted_iota(jnp.int32, (CHUNK, HP), 1)
        dt = _softplus(dt_ref[...] + bias_ref[0:1, :])
        da = dt * (-jnp.exp(a_ref[0:1, :]))
        cs_f = jnp.dot(jnp.where(_tri(False), 1.0, 0.0).astype(F32), da, precision=HI, preferred_element_type=F32)
        cs_b = jnp.dot(jnp.where(_tri(True), 1.0, 0.0).astype(F32), da, precision=HI, preferred_element_type=F32)
        cs = jnp.where(lane < SSM_HEADS, cs_f, cs_b)
        tot = jnp.where(lane[0:1] < SSM_HEADS, cs_f[CHUNK - 1:CHUNK, :], cs_b[0:1, :])
        e, w = jnp.exp(cs), jnp.exp(tot - cs)
        tot8 = jnp.broadcast_to(tot, (8, HP))
        etot8 = jnp.exp(tot8)
        for b in range(N_HB):
            down = (HP - HG * b) % HP

            def rolled(v):
                return pltpu.roll(v, down, 1) if down else v

            cols_ref[b, :, P_DT:P_DT + HP] = rolled(dt)
            cs_r = rolled(cs)
            cols_ref[b, :, P_CS:P_CS + HP] = cs_r
            cols_ref[b, :, P_E:P_E + HP] = rolled(e)
            cols_ref[b, :, P_W:P_W + HP] = rolled(w)
            rows_ref[b, 0, 0:8, :] = cs_r.T[0:8, :]
            r8 = lax.broadcasted_iota(jnp.int32, (8, HP), 0)
            rows_ref[b, 0, 8:16, :] = jnp.where(r8 == 0, rolled(tot8), jnp.where(r8 == 1, rolled(etot8), 0.0))

    vec = pl.BlockSpec((8, HP), lambda c: (0, 0))
    return pl.pallas_call(
        body, name="ssd_prep",
        out_shape=(jax.ShapeDtypeStruct((N_HB, T, 4 * HP), F32), jax.ShapeDtypeStruct((N_HB, nc, 16, HP), F32)),
        grid=(nc,), in_specs=[pl.BlockSpec((CHUNK, HP), lambda c: (c, DT_BLK)), vec, vec],
        out_specs=(pl.BlockSpec((N_HB, CHUNK, 4 * HP), lambda c: (0, c, 0)), pl.BlockSpec((N_HB, 1, 16, HP), lambda c: (0, c, 0, 0))),
    )(u, bias8, alog8)


def _ssd_specs(T, rev, bwd):
    nc = T // CHUNK
    fwd_order = (lambda c: nc - 1 - c) if rev else (lambda c: c)
    cm = (lambda c: fwd_order(nc - 1 - c)) if bwd else fwd_order
    hb0 = SSM_GROUPS if rev else 0
    xs = pl.BlockSpec((CHUNK, GW), lambda c, g: (cm(c), g))
    bs = pl.BlockSpec((CHUNK, D_STATE), lambda c, g: (cm(c), D_INNER // D_STATE + g))
    cs = pl.BlockSpec((CHUNK, D_STATE), lambda c, g: (cm(c), (D_INNER + SSM_GROUPS * D_STATE) // D_STATE + g))
    cols = pl.BlockSpec((1, CHUNK, 4 * HP), lambda c, g: (hb0 + g, cm(c), 0))
    rows = pl.BlockSpec((1, 1, 16, HP), lambda c, g: (hb0 + g, cm(c), 0, 0))
    return nc, cm, xs, bs, cs, cols, rows


def _head_lanes(to_heads):
    shape = (GW, HP) if to_heads else (HP, GW)
    wide = lax.broadcasted_iota(jnp.int32, shape, 0 if to_heads else 1)
    head = lax.broadcasted_iota(jnp.int32, shape, 1 if to_heads else 0)
    return jnp.where((wide >= PH * head) & (wide < PH * (head + 1)), 1.0, 0.0).astype(BF)


def _split_dot(v, m, terms):
    total, rest = None, v
    for _ in range(terms):
        piece = rest.astype(BF)
        part = jnp.dot(piece, m, preferred_element_type=F32)
        total = part if total is None else total + part
        rest = rest - piece.astype(F32)
    return total


def _spread_cols(cols_ref, rows_ref):
    spread = _head_lanes(False)
    dt_e = _split_dot(cols_ref[0, :, P_DT:P_DT + HP], spread, 3)
    e_e = _split_dot(cols_ref[0, :, P_E:P_E + HP], spread, 2)
    w_e = _split_dot(cols_ref[0, :, P_W:P_W + HP], spread, 2)
    etot_e = _split_dot(rows_ref[0, 0, 8:16, :], spread, 3)[1:2, :]
    return dt_e, e_e, w_e, etot_e


def _decay(cols_ref, rows_ref, hh, incl, transpose=False):
    col = cols_ref[0, :, P_CS + hh:P_CS + hh + 1]
    row = rows_ref[0, 0, hh:hh + 1, :]
    return jnp.where(incl, jnp.exp(row - col if transpose else col - row), 0.0)


def _ssd_fwd(act, cols, rows, *, rev, name):
    T = act.shape[0]
    nc, cm, xs_s, b_s, c_s, cols_s, rows_s = _ssd_specs(T, rev, False)

    def body(x_ref, b_ref, c_ref, cols_ref, rows_ref, y_ref, st_ref, state):
        c, g = pl.program_id(0), pl.program_id(1)

        @pl.when(c == 0)
        def _():
            state[g] = jnp.zeros((D_STATE, GW), F32)

        incl = _tri(rev)
        bm, cmat = b_ref[...].astype(BF), c_ref[...].astype(BF)
        bm_t = b_ref[...].T.astype(BF)
        cb = lax.dot_general(cmat, bm, NT, preferred_element_type=F32)
        dt_e, e_e, w_e, etot_e = _spread_cols(cols_ref, rows_ref)
        prev_all = state[g]
        st_ref[...] = prev_all
        xdt = x_ref[...] * dt_e
        xdt_b = xdt.astype(BF)
        yo_all = jnp.dot(cmat, prev_all.astype(BF), preferred_element_type=F32) * e_e
        state[g] = prev_all * etot_e + jnp.dot(bm_t, (xdt * w_e).astype(BF), preferred_element_type=F32)
        for hh in range(HG):
            hs = slice(PH * hh, PH * (hh + 1))
            lmat = _decay(cols_ref, rows_ref, hh, incl)
            yd = jnp.dot((cb * lmat).astype(BF), xdt_b[:, hs], preferred_element_type=F32)
            y_ref[:, hs] = yd + yo_all[:, hs]

    return pl.pallas_call(
        body, name=name,
        out_shape=(jax.ShapeDtypeStruct((T, D_INNER), F32), jax.ShapeDtypeStruct((nc * D_STATE, D_INNER), F32)),
        grid=(nc, SSM_GROUPS), in_specs=[xs_s, b_s, c_s, cols_s, rows_s], out_specs=(xs_s, xs_s),
        scratch_shapes=[pltpu.VMEM((SSM_GROUPS, D_STATE, GW), F32)],
        compiler_params=pltpu.CompilerParams(dimension_semantics=("arbitrary", "arbitrary")),
    )(act, act, act, cols, rows)


def _ssd_bwd(act, cols, rows, states, dy, *, rev, name):
    T = act.shape[0]
    nc, cm, xs_s, b_s, c_s, cols_s, rows_s = _ssd_specs(T, rev, True)

    def body(x_ref, b_ref, c_ref, cols_ref, rows_ref, st_ref, dy_ref, dx_ref, db_ref, dc_ref, dsel_ref, dtot_ref,
             dstate, dcs_cols, dcs_rows, dcb, dm_scr, dxdt_scr):
        c, g = pl.program_id(0), pl.program_id(1)

        @pl.when(c == 0)
        def _():
            dstate[g] = jnp.zeros((D_STATE, GW), F32)

        incl, incl_t = _tri(rev), _tri(rev, transpose=True)
        bm, cmat = b_ref[...].astype(BF), c_ref[...].astype(BF)
        cm_t = c_ref[...].T.astype(BF)
        cb = lax.dot_general(cmat, bm, NT, preferred_element_type=F32)
        cb_t = lax.dot_general(bm, cmat, NT, preferred_element_type=F32)
        prev_all, ds_all = st_ref[...], dstate[g]
        pb_all, dsb_all = prev_all.astype(BF), ds_all.astype(BF)
        cp_all = jnp.dot(cmat, pb_all, preferred_element_type=F32)
        bds_all = jnp.dot(bm, dsb_all, preferred_element_type=F32)
        dt_e, e_e, w_e, etot_e = _spread_cols(cols_ref, rows_ref)
        to_heads = _head_lanes(True)
        x, dy = x_ref[...], dy_ref[...]
        xdt = x * dt_e
        xdt_b, dy_b = xdt.astype(BF), dy.astype(BF)
        dye_b, xdw_b = (dy * e_e).astype(BF), (xdt * w_e).astype(BF)
        for hh in range(HG):
            hs = slice(PH * hh, PH * (hh + 1))
            mmat_t = cb_t * _decay(cols_ref, rows_ref, hh, incl_t, transpose=True)
            dm_scr[hh] = lax.dot_general(dy_b[:, hs], xdt_b[:, hs], NT, preferred_element_type=F32)
            dxdt_scr[:, hs] = jnp.dot(mmat_t.astype(BF), dy_b[:, hs], preferred_element_type=F32)
        bdsw = bds_all * w_e
        dxdt = dxdt_scr[...] + bdsw
        dx_ref[...] = dxdt * dt_e
        t = _split_dot(xdt * bdsw, to_heads, 2)
        dcs_state = _split_dot(dy * cp_all, to_heads, 2) * cols_ref[0, :, P_E:P_E + HP] - t
        dsel_ref[0, :, 0:HP] = _split_dot(dxdt * x, to_heads, 2)
        sp = _split_dot(jnp.broadcast_to(jnp.sum(ds_all * prev_all, axis=0, keepdims=True), (8, GW)), to_heads, 2)
        dtot_ref[0, 0] = jnp.sum(t, axis=0, keepdims=True) + sp * rows_ref[0, 0, 9:10, :]
        dstate[g] = ds_all * etot_e + jnp.dot(cm_t, dye_b, preferred_element_type=F32)
        dcs_cols[...] = jnp.zeros_like(dcs_cols)
        dcs_rows[...] = jnp.zeros_like(dcs_rows)
        dcb[...] = jnp.zeros_like(dcb)
        for hh in range(HG):
            lmat = _decay(cols_ref, rows_ref, hh, incl)
            dm = dm_scr[hh]
            qm = dm * (cb * lmat)
            dcs_cols[:, hh:hh + 1] = jnp.sum(qm, axis=1, keepdims=True)
            dcs_rows[hh:hh + 1, :] = jnp.sum(qm, axis=0, keepdims=True)
            dcb[...] += dm * lmat
        dcb_all = dcb[...]
        dsel_ref[0, :, HP:2 * HP] = dcs_state + dcs_cols[...] - dcs_rows[...].T
        dc_ref[...] = (lax.dot_general(dye_b, pb_all, NT, preferred_element_type=F32)
                       + jnp.dot(dcb_all.astype(BF), bm, preferred_element_type=F32))
        db_ref[...] = (lax.dot_general(xdw_b, dsb_all, NT, preferred_element_type=F32)
                       + jnp.dot(dcb_all.T.astype(BF), cmat, preferred_element_type=F32))

    bc_out = pl.BlockSpec((CHUNK, D_STATE), lambda c, g: (cm(c), g))
    return pl.pallas_call(
        body, name=name,
        out_shape=(jax.ShapeDtypeStruct((T, D_INNER), F32), jax.ShapeDtypeStruct((T, SSM_GROUPS * D_STATE), F32),
                   jax.ShapeDtypeStruct((T, SSM_GROUPS * D_STATE), F32), jax.ShapeDtypeStruct((SSM_GROUPS, T, 2 * HP), F32),
                   jax.ShapeDtypeStruct((SSM_GROUPS, nc, 8, HP), F32)),
        grid=(nc, SSM_GROUPS), in_specs=[xs_s, b_s, c_s, cols_s, rows_s, xs_s, xs_s],
        out_specs=(xs_s, bc_out, bc_out, pl.BlockSpec((1, CHUNK, 2 * HP), lambda c, g: (g, cm(c), 0)),
                   pl.BlockSpec((1, 1, 8, HP), lambda c, g: (g, cm(c), 0, 0))),
        scratch_shapes=[pltpu.VMEM((SSM_GROUPS, D_STATE, GW), F32), pltpu.VMEM((CHUNK, CHUNK), F32),
                        pltpu.VMEM((CHUNK, CHUNK), F32), pltpu.VMEM((CHUNK, CHUNK), F32),
                        pltpu.VMEM((HG, CHUNK, CHUNK), F32), pltpu.VMEM((CHUNK, GW), F32)],
        compiler_params=pltpu.CompilerParams(dimension_semantics=("arbitrary", "arbitrary")),
    )(act, act, act, cols, rows, states, dy)


def _ssd_prep_bwd(u, bias8, alog8, dsel_f, dtot_f, dsel_b, dtot_b):
    T = u.shape[0]
    nc = T // CHUNK

    def body(dt_ref, bias_ref, a_ref, sf_ref, tf_ref, sb_ref, tb_ref, ddt_ref, da_ref, dbias_ref):
        @pl.when(pl.program_id(0) == 0)
        def _():
            da_ref[...] = jnp.zeros_like(da_ref)
            dbias_ref[...] = jnp.zeros_like(dbias_ref)

        lane = lax.broadcasted_iota(jnp.int32, (CHUNK, HP), 1)
        pre = dt_ref[...] + bias_ref[0:1, :]
        dt = _softplus(pre)
        a = -jnp.exp(a_ref[0:1, :])
        ddt_x, dcs, dtot = jnp.zeros((CHUNK, HP), F32), jnp.zeros((CHUNK, HP), F32), jnp.zeros((8, HP), F32)
        for b in range(N_HB):
            s_ref, t_ref, g = (sf_ref, tf_ref, b) if b < SSM_GROUPS else (sb_ref, tb_ref, b - SSM_GROUPS)
            mine = (lane >= HG * b) & (lane < HG * (b + 1))

            def up(v):
                return pltpu.roll(v, HG * b, 1) if b else v

            ddt_x = ddt_x + jnp.where(mine, up(s_ref[g, :, 0:HP]), 0.0)
            dcs = dcs + jnp.where(mine, up(s_ref[g, :, HP:2 * HP]), 0.0)
            dtot = dtot + jnp.where(mine[0:8], up(t_ref[g, 0]), 0.0)
        tri_f = jnp.where(_tri(False, transpose=True), 1.0, 0.0).astype(F32)
        tri_b = jnp.where(_tri(True, transpose=True), 1.0, 0.0).astype(F32)
        dda = jnp.where(lane < SSM_HEADS, jnp.dot(tri_f, dcs, precision=HI, preferred_element_type=F32),
                        jnp.dot(tri_b, dcs, precision=HI, preferred_element_type=F32)) + dtot[0:1, :]
        dpre = (ddt_x + dda * a) * jax.nn.sigmoid(pre)
        ddt_ref[...] = jnp.where(lane < 2 * SSM_HEADS, dpre, 0.0)
        dbias_ref[...] += jnp.broadcast_to(jnp.sum(dpre, axis=0, keepdims=True), (8, HP))
        da_ref[...] += jnp.broadcast_to(jnp.sum(dda * dt, axis=0, keepdims=True) * a, (8, HP))

    vec = pl.BlockSpec((8, HP), lambda c: (0, 0))
    sel = pl.BlockSpec((SSM_GROUPS, CHUNK, 2 * HP), lambda c: (0, c, 0))
    tot = pl.BlockSpec((SSM_GROUPS, 1, 8, HP), lambda c: (0, c, 0, 0))
    tile = pl.BlockSpec((CHUNK, HP), lambda c: (c, 0))
    return pl.pallas_call(
        body, name="ssd_prep_bwd",
        out_shape=(jax.ShapeDtypeStruct((T, HP), F32), jax.ShapeDtypeStruct((8, HP), F32), jax.ShapeDtypeStruct((8, HP), F32)),
        grid=(nc,), in_specs=[pl.BlockSpec((CHUNK, HP), lambda c: (c, DT_BLK)), vec, vec, sel, tot, sel, tot],
        out_specs=(tile, vec, vec),
        compiler_params=pltpu.CompilerParams(dimension_semantics=("arbitrary",)),
    )(u, bias8, alog8, dsel_f, dtot_f, dsel_b, dtot_b)


def _ssm_combine_fwd(y_f, y_b, act, u, dskip, gain):
    T = y_f.shape[0]
    tt = _pick(T, 512)

    def body(yf_ref, yb_ref, x_ref, z_ref, ds_ref, g_ref, y_ref, m_ref):
        y = yf_ref[...] + yb_ref[...] + ds_ref[...] * x_ref[...]
        y2 = y * _silu(z_ref[...])
        r = lax.rsqrt(jnp.mean(y2 * y2, axis=-1, keepdims=True) + EPS)
        y_ref[...] = y
        m_ref[...] = (y2 * r * g_ref[...]).astype(m_ref.dtype)

    blk = pl.BlockSpec((tt, GW), lambda i, g: (i, g))
    vec = pl.BlockSpec((1, GW), lambda i, g: (0, g))
    return pl.pallas_call(
        body, name="ssm_combine_fwd",
        out_shape=(jax.ShapeDtypeStruct((T, D_INNER), F32), jax.ShapeDtypeStruct((T, D_INNER), BF)),
        grid=(T // tt, SSM_GROUPS), in_specs=[blk, blk, blk, blk, vec, vec], out_specs=(blk, blk),
    )(y_f, y_b, act, u, dskip, gain)


def _ssm_combine_bwd(dm, y, act, u, dskip, gain):
    T = y.shape[0]
    tt = _pick(T, 512)

    def body(dm_ref, y_ref, x_ref, z_ref, ds_ref, g_ref, dy_ref, dz_ref, dxs_ref, dg_ref, dsk_ref):
        z = z_ref[...]
        y = y_ref[...]
        x = x_ref[...]
        sz = _silu(z)
        y2 = y * sz
        r = lax.rsqrt(jnp.mean(y2 * y2, axis=-1, keepdims=True) + EPS)
        d = dm_ref[...]
        gd = d * g_ref[...]
        dy2 = r * gd - y2 * (r * r * r) * jnp.mean(gd * y2, axis=-1, keepdims=True)
        dy = dy2 * sz
        dy_ref[...] = dy
        dz_ref[...] = (dy2 * y * _dsilu(z)).astype(dz_ref.dtype)
        dxs_ref[...] = dy * ds_ref[...]

        @pl.when(pl.program_id(1) == 0)
        def _():
            dg_ref[...] = jnp.zeros_like(dg_ref)
            dsk_ref[...] = jnp.zeros_like(dsk_ref)

        dg_ref[...] += jnp.broadcast_to(jnp.sum(d * y2 * r, axis=0, keepdims=True), dg_ref.shape)
        lane_sum = jnp.broadcast_to(jnp.sum(dy * x, axis=0, keepdims=True), (8, GW))
        src = lax.broadcasted_iota(jnp.int32, (GW, HP), 0)
        head = lax.broadcasted_iota(jnp.int32, (GW, HP), 1)
        to_head = jnp.where((src >= PH * head) & (src < PH * (head + 1)), 1.0, 0.0).astype(F32)
        dsk_ref[...] += jnp.dot(lane_sum, to_head, precision=HI, preferred_element_type=F32)

    blk = pl.BlockSpec((tt, GW), lambda g, i: (i, g))
    vec = pl.BlockSpec((1, GW), lambda g, i: (0, g))
    acc = pl.BlockSpec((8, GW), lambda g, i: (0, g))
    return pl.pallas_call(
        body, name="ssm_combine_bwd",
        out_shape=(jax.ShapeDtypeStruct((T, D_INNER), F32), jax.ShapeDtypeStruct((T, D_INNER), BF),
                   jax.ShapeDtypeStruct((T, D_INNER), F32), jax.ShapeDtypeStruct((8, D_INNER), F32),
                   jax.ShapeDtypeStruct((8, SSM_GROUPS * HP), F32)),
        grid=(SSM_GROUPS, T // tt), in_specs=[blk, blk, blk, blk, vec, vec],
        out_specs=(blk, blk, blk, acc, pl.BlockSpec((8, HP), lambda g, i: (0, g))),
        compiler_params=pltpu.CompilerParams(dimension_semantics=("parallel", "arbitrary")),
    )(dm, y, act, u, dskip, gain)


def _loss_head(y, target):
    T, D = y.shape
    tt = _pick(T, 512)

    def body(y_ref, t_ref, dy_ref, dyb_ref, l_ref):
        e = y_ref[...] - t_ref[...]
        dy_ref[...] = e * (1.0 / D)
        dyb_ref[...] = (e * (1.0 / D)).astype(dyb_ref.dtype)

        @pl.when(pl.program_id(0) == 0)
        def _():
            l_ref[...] = jnp.zeros_like(l_ref)

        l_ref[...] += jnp.sum(e * e) * (0.5 / D)

    blk = pl.BlockSpec((tt, D), lambda i: (i, 0))
    return pl.pallas_call(
        body, name="loss_head",
        out_shape=(jax.ShapeDtypeStruct((T, D), F32), jax.ShapeDtypeStruct((T, D), BF), jax.ShapeDtypeStruct((8, 128), F32)),
        grid=(T // tt,), in_specs=[blk, blk], out_specs=(blk, blk, pl.BlockSpec((8, 128), lambda i: (0, 0))),
        compiler_params=pltpu.CompilerParams(dimension_semantics=("arbitrary",)),
    )(y, target)


def _adamw(w, g, m, v, *, name):
    R, C = w.shape
    cap = max(8, (1 << 18) // C)
    tr = R
    if R % 8 == 0:
        tr = 8
        for cand in range(8, min(R, cap) + 1, 8):
            if R % cand == 0:
                tr = cand

    def body(w_ref, g_ref, m_ref, v_ref, d_ref, nm_ref, nv_ref):
        gg = g_ref[...]
        nm = ADAM_B1 * m_ref[...] + (1.0 - ADAM_B1) * gg
        nv = ADAM_B2 * v_ref[...] + (1.0 - ADAM_B2) * jnp.square(gg)
        m_hat = nm / (1.0 - ADAM_B1 ** ADAM_STEP)
        v_hat = nv / (1.0 - ADAM_B2 ** ADAM_STEP)
        d_ref[...] = -ADAM_LR * (m_hat / (jnp.sqrt(v_hat) + ADAM_EPS) + ADAM_WD * w_ref[...])
        nm_ref[...] = nm
        nv_ref[...] = nv

    blk = pl.BlockSpec((tr, C), lambda i: (i, 0))
    return pl.pallas_call(
        body, name=name, out_shape=(jax.ShapeDtypeStruct((R, C), F32),) * 3, grid=(R // tr,),
        in_specs=[blk] * 4, out_specs=(blk,) * 3,
    )(w, g, m, v)


ANY = pl.BlockSpec(memory_space=pl.ANY)


def _chip_peers():
    x, y, c = lax.axis_index("x"), lax.axis_index("y"), lax.axis_index("c")
    return x, y, c, [(1 - x, y), (x, 1 - y), (1 - x, 1 - y)]


def _half_rows(c, rh):
    return pl.ds(pl.multiple_of(c * rh, 16), rh)


def _my_chip():
    return 2 * lax.axis_index("x") + lax.axis_index("y")


def _gather_chips(wb, wf):
    rh = wb.shape[0] // 2
    rq = rh // 2

    def body(wb_ref, wf_ref, ob_ref, of_ref, send_sems, recv_sems):
        x, y, c, peers = _chip_peers()
        nbr_x, nbr_y = peers[0], peers[1]
        me, chip_x, chip_y, chip_d = 2 * x + y, 2 * (1 - x) + y, 2 * x + (1 - y), 2 * (1 - x) + (1 - y)

        def quarter(core, b):
            return pl.ds(pl.multiple_of(core * rh + b * rq, 16), rq)

        ici = [(0, nbr_x, me, 0, chip_x), (1, nbr_y, me, 1, chip_y), (2, nbr_y, me, 0, chip_y), (3, nbr_x, me, 1, chip_x),
               (4, nbr_y, chip_x, 0, chip_d), (5, nbr_x, chip_y, 1, chip_d)]

        def ici_copy(k, to, slot, b, own):
            rows = quarter(c, b)
            return pltpu.make_async_remote_copy(
                src_ref=wb_ref.at[rows] if own else ob_ref.at[slot, rows], dst_ref=ob_ref.at[slot, rows],
                send_sem=send_sems.at[k], recv_sem=recv_sems.at[k], device_id=(to[0], to[1], c), device_id_type=MESH)

        def to_sibling(k, slot, b, core):
            rows = quarter(core, b)
            return pltpu.make_async_remote_copy(
                src_ref=ob_ref.at[slot, rows], dst_ref=ob_ref.at[slot, rows], send_sem=send_sems.at[6 + k],
                recv_sem=recv_sems.at[6 + k], device_id=(x, y, 1 - c), device_id_type=MESH)

        def small_copy(k, slot):
            px, py = peers[k]
            return pltpu.make_async_remote_copy(
                src_ref=wf_ref, dst_ref=of_ref.at[slot], send_sem=send_sems.at[12 + k], recv_sem=recv_sems.at[12 + k],
                device_id=(px, py, c), device_id_type=MESH)

        sends = [ici_copy(k, to, slot, b, True) for k, to, slot, b, _ in ici[:4]] + [small_copy(k, me) for k in range(3)]
        for cp in sends:
            cp.start()
        for k, to, slot, b, arrives in ici:
            ici_copy(k, to, arrives, b, False).wait_recv()
            passed = [to_sibling(k, arrives, b, c)]
            if k < 2:
                passed.append(ici_copy(*ici[4 + k][:4], False))
            for cp in passed:
                cp.start()
            sends += passed
        for k, to, slot, b, arrives in ici:
            to_sibling(k, arrives, b, 1 - c).wait_recv()
        chip_of = [chip_x, chip_y, chip_d]
        for k in range(3):
            small_copy(k, chip_of[k]).wait_recv()
        for cp in sends:
            cp.wait_send()

    ob, of = pl.pallas_call(
        body, name="gather_weights",
        out_shape=(jax.ShapeDtypeStruct((4,) + wb.shape, wb.dtype), jax.ShapeDtypeStruct((4,) + wf.shape, wf.dtype)),
        in_specs=[ANY, ANY], out_specs=(ANY, ANY),
        scratch_shapes=[pltpu.SemaphoreType.DMA((15,)), pltpu.SemaphoreType.DMA((15,))],
    )(wb, wf)
    me = _my_chip()
    return lax.dynamic_update_slice(ob, wb[None], (me, 0, 0)), lax.dynamic_update_slice(of, wf[None], (me, 0, 0))


def _halves_to_sibling(gp):
    rh = gp.shape[1] // 2

    def body(gp_ref, o_ref, send_sem, recv_sem):
        x, y, c = lax.axis_index("x"), lax.axis_index("y"), lax.axis_index("c")
        cp = pltpu.make_async_remote_copy(src_ref=gp_ref.at[:, _half_rows(1 - c, rh), :], dst_ref=o_ref, send_sem=send_sem,
                                          recv_sem=recv_sem, device_id=(x, y, 1 - c), device_id_type=MESH)
        cp.start()
        cp.wait()

    return pl.pallas_call(
        body, name="halves_to_sibling", out_shape=jax.ShapeDtypeStruct((gp.shape[0], rh, gp.shape[2]), gp.dtype),
        in_specs=[ANY], out_specs=ANY, scratch_shapes=[pltpu.SemaphoreType.DMA, pltpu.SemaphoreType.DMA],
    )(gp)


def _row_tile(rows, cap=1024):
    tr = 16
    for cand in range(16, cap + 1, 16):
        if rows % cand == 0:
            tr = cand
    return tr


def _add_halves(gp, sib, core):
    n, rh, C = sib.shape
    tr = _row_tile(rh)
    nt = rh // tr

    def body(c_ref, g_ref, s_ref, o_ref):
        o_ref[...] = (g_ref[...].astype(F32) + s_ref[...].astype(F32)).astype(o_ref.dtype)

    blk = pl.BlockSpec((1, tr, C), lambda j, i, c: (j, i, 0))
    return pl.pallas_call(
        body, name="add_halves", out_shape=jax.ShapeDtypeStruct(sib.shape, sib.dtype),
        grid_spec=pltpu.PrefetchScalarGridSpec(
            num_scalar_prefetch=1, grid=(n, nt),
            in_specs=[pl.BlockSpec((1, tr, C), lambda j, i, c: (j, c[0] * nt + i, 0)), blk], out_specs=blk),
    )(core, gp, sib)


def _join_halves(buf):
    rh = buf.shape[0] // 2

    def body(in_ref, o_ref, send_sem, recv_sem):
        x, y, c = lax.axis_index("x"), lax.axis_index("y"), lax.axis_index("c")

        def copy(rows):
            return pltpu.make_async_remote_copy(src_ref=o_ref.at[rows], dst_ref=o_ref.at[rows], send_sem=send_sem,
                                                recv_sem=recv_sem, device_id=(x, y, 1 - c), device_id_type=MESH)

        send = copy(_half_rows(c, rh))
        send.start()
        copy(_half_rows(1 - c, rh)).wait_recv()
        send.wait_send()

    return pl.pallas_call(
        body, name="join_halves", out_shape=jax.ShapeDtypeStruct(buf.shape, buf.dtype),
        in_specs=[ANY], out_specs=ANY, input_output_aliases={0: 0},
        scratch_shapes=[pltpu.SemaphoreType.DMA, pltpu.SemaphoreType.DMA],
    )(buf)


def _exchange_near(gp):
    rq = gp.shape[1] // 2

    def body(gp_ref, out_ref, send_sems, recv_sems):
        x, y, c, peers = _chip_peers()
        chip_x, chip_y, chip_d = 2 * (1 - x) + y, 2 * x + (1 - y), 2 * (1 - x) + (1 - y)
        plan = [(peers[0], chip_x, 0), (peers[0], chip_d, 0), (peers[1], chip_y, 1), (peers[1], chip_d, 1)]
        copies = [pltpu.make_async_remote_copy(
            src_ref=gp_ref.at[slot, pl.ds(b * rq, rq)], dst_ref=out_ref.at[k], send_sem=send_sems.at[k],
            recv_sem=recv_sems.at[k], device_id=(to[0], to[1], c), device_id_type=MESH) for k, (to, slot, b) in enumerate(plan)]
        for cp in copies:
            cp.start()
        for cp in copies:
            cp.wait_recv()
        for cp in copies:
            cp.wait_send()

    return pl.pallas_call(
        body, name="exchange_grads_near", out_shape=jax.ShapeDtypeStruct((4, rq, gp.shape[2]), gp.dtype),
        in_specs=[ANY], out_specs=ANY, scratch_shapes=[pltpu.SemaphoreType.DMA((4,)), pltpu.SemaphoreType.DMA((4,))],
    )(gp)


def _add_near(gp, near, chips):
    _, rq, C = near.shape
    tr = _row_tile(rq)
    nt = rq // tr

    def body(ch_ref, mine_a, mine_b, on_a, on_b, near_ref, part_ref, on_ref):
        part_ref[0] = mine_a[0].astype(F32) + near_ref[0].astype(F32)
        part_ref[1] = mine_b[0].astype(F32) + near_ref[2].astype(F32)
        on_ref[0] = (on_a[0].astype(F32) + near_ref[1].astype(F32)).astype(on_ref.dtype)
        on_ref[1] = (on_b[0].astype(F32) + near_ref[3].astype(F32)).astype(on_ref.dtype)

    def slot(which, b):
        return pl.BlockSpec((1, tr, C), lambda i, ch: (ch[which], b * nt + i, 0))

    return pl.pallas_call(
        body, name="add_near",
        out_shape=(jax.ShapeDtypeStruct((2, rq, C), F32), jax.ShapeDtypeStruct((2, rq, C), near.dtype)),
        grid_spec=pltpu.PrefetchScalarGridSpec(
            num_scalar_prefetch=1, grid=(nt,),
            in_specs=[slot(0, 0), slot(0, 1), slot(2, 0), slot(1, 1), pl.BlockSpec((4, tr, C), lambda i, ch: (0, i, 0))],
            out_specs=(pl.BlockSpec((2, tr, C), lambda i, ch: (0, i, 0)),) * 2),
    )(chips, gp, gp, gp, gp, near)


def _exchange_far(on):
    def body(on_ref, out_ref, send_sems, recv_sems):
        x, y, c, peers = _chip_peers()
        copies = [pltpu.make_async_remote_copy(
            src_ref=on_ref.at[k], dst_ref=out_ref.at[k], send_sem=send_sems.at[k], recv_sem=recv_sems.at[k],
            device_id=(to[0], to[1], c), device_id_type=MESH) for k, to in enumerate((peers[1], peers[0]))]
        for cp in copies:
            cp.start()
        for cp in copies:
            cp.wait_recv()
        for cp in copies:
            cp.wait_send()

    return pl.pallas_call(
        body, name="exchange_grads_far", out_shape=jax.ShapeDtypeStruct(on.shape, on.dtype),
        in_specs=[ANY], out_specs=ANY, scratch_shapes=[pltpu.SemaphoreType.DMA((2,)), pltpu.SemaphoreType.DMA((2,))],
    )(on)


def _add_far(part, far, core):
    _, rq, C = part.shape
    tr = _row_tile(rq)
    nt = rq // tr

    def body(c_ref, p_ref, f_ref, o_ref):
        o_ref[...] = p_ref[0] + f_ref[0].astype(F32)

    blk = pl.BlockSpec((1, tr, C), lambda b, i, c: (b, i, 0))
    return pl.pallas_call(
        body, name="add_far", out_shape=jax.ShapeDtypeStruct((4 * rq, C), F32),
        grid_spec=pltpu.PrefetchScalarGridSpec(
            num_scalar_prefetch=1, grid=(2, nt), in_specs=[blk, blk],
            out_specs=pl.BlockSpec((tr, C), lambda b, i, c: ((2 * c[0] + b) * nt + i, 0))),
    )(core, part, far)


N_DEV = 8


def _allreduce_small(p):
    rs = p.shape[0]

    def body(x_ref, sum_ref, all_ref, send_sems, recv_sems, local_sem):
        x, y, c = lax.axis_index("x"), lax.axis_index("y"), lax.axis_index("c")
        me, sibling = (x, y, c), (x, y, 1 - c)
        chips = [(1 - x, y), (x, 1 - y), (1 - x, 1 - y)]

        def rows(px, py, pc):
            return all_ref.at[pl.ds((4 * px + 2 * py + pc) * rs, rs), :]

        def copy(k, block, to, src=None):
            return pltpu.make_async_remote_copy(
                src_ref=rows(*block) if src is None else src, dst_ref=rows(*block),
                send_sem=send_sems.at[k], recv_sem=recv_sems.at[k], device_id=to, device_id_type=MESH)

        mine = pltpu.make_async_copy(x_ref, rows(*me), local_sem)
        mine.start()
        first = [copy(0, me, sibling, src=x_ref)]
        first += [copy(1 + j, me, (*chip, c), src=x_ref) for j, chip in enumerate(chips)]
        for cp in first:
            cp.start()
        passed = [copy(4 + j, (*chip, c), sibling) for j, chip in enumerate(chips)]
        for j, chip in enumerate(chips):
            copy(1 + j, (*chip, c), me).wait_recv()
            passed[j].start()
        copy(0, sibling, me).wait_recv()
        for j, chip in enumerate(chips):
            copy(4 + j, (*chip, 1 - c), me).wait_recv()
        for cp in first + passed:
            cp.wait_send()
        mine.wait()
        acc = all_ref[0:rs, :]
        for d in range(1, N_DEV):
            acc = acc + all_ref[d * rs:(d + 1) * rs, :]
        sum_ref[...] = acc

    vmem = pl.BlockSpec(memory_space=pltpu.VMEM)
    return pl.pallas_call(
        body, name="allreduce_small", out_shape=jax.ShapeDtypeStruct((rs, 128), F32),
        in_specs=[vmem], out_specs=vmem,
        scratch_shapes=[pltpu.VMEM((N_DEV * rs, 128), F32), pltpu.SemaphoreType.DMA((7,)), pltpu.SemaphoreType.DMA((7,)),
                        pltpu.SemaphoreType.DMA],
    )(p)


WEIGHTS = ('ffn1_norm', 'ffn1_w_gate', 'ffn1_w_up', 'ffn1_w_down', 'mix_norm', 'w_in', 'q_a_norm', 'w_q_b',
           'kv_a_norm', 'w_kv_b', 'q_head_norm', 'k_head_norm', 'conv_w', 'conv_b', 'a_log_fwd', 'a_log_bwd',
           'dt_bias_fwd', 'dt_bias_bwd', 'd_skip', 'ssm_norm', 'w_attn_branch', 'w_ssm_branch', 'w_out',
           'ffn2_norm', 'ffn2_w_gate', 'ffn2_w_up', 'ffn2_w_down')
PACKED = (('ffn1_w_gate', (D_MODEL, D_FF), 1), ('ffn1_w_up', (D_MODEL, D_FF), 1), ('ffn1_w_down', (D_FF, D_MODEL), 0),
          ('w_in', (D_MODEL, sum(IN_SPLITS)), 1), ('w_q_b', (Q_LORA, N_HEADS * QK_HEAD), 1),
          ('w_kv_b', (KV_LORA, N_HEADS * (QK_NOPE + V_HEAD)), 1),
          ('w_attn_branch', (N_HEADS * V_HEAD, D_MODEL), 0), ('w_ssm_branch', (D_INNER, D_MODEL), 0),
          ('w_out', (D_MODEL, D_MODEL), 0),
          ('ffn2_w_gate', (D_MODEL, D_FF), 1), ('ffn2_w_up', (D_MODEL, D_FF), 1), ('ffn2_w_down', (D_FF, D_MODEL), 0))
PACK_W = 1024
N_CHIPS = 4
SMALL = (('ffn1_norm', 1024), ('mix_norm', 1024), ('q_a_norm', 384), ('kv_a_norm', 256), ('q_head_norm', 96),
         ('k_head_norm', 96), ('conv_b', 3072), ('a_log_fwd', 32), ('a_log_bwd', 32), ('dt_bias_fwd', 32),
         ('dt_bias_bwd', 32), ('d_skip', 32), ('ssm_norm', 2048), ('ffn2_norm', 1024),
         ('conv_w', CONV_WIDTH * XBC_DIM), ('loss', 1))


TRANSPOSED = ('ffn1_w_gate', 'ffn1_w_up', 'w_in', 'ffn2_w_gate', 'ffn2_w_up')


def _stored(name, a):
    return a.T if name in TRANSPOSED else a


def _shard_shape(name, shape, axis):
    sh = tuple(s // N_CHIPS if a == axis else s for a, s in enumerate(shape))
    return sh[::-1] if name in TRANSPOSED else sh


def _by_rows(name, axis):
    return name in TRANSPOSED or axis == 0


def _pack_layout():
    out, r = {}, 0
    for name, shape, axis in PACKED:
        n = math.prod(shape) // N_CHIPS // PACK_W
        out[name] = (r, n)
        r += n
    return out, -(-r // 64) * 64


def _pack(shards):
    layout, rows = _pack_layout()
    parts = [shards[name].reshape(-1, PACK_W) for name, _, _ in PACKED]
    parts.append(jnp.zeros((rows - sum(p.shape[0] for p in parts), PACK_W), parts[0].dtype))
    return jnp.concatenate(parts, axis=0)


def _unpack(packed):
    layout, _ = _pack_layout()
    return {name: packed[layout[name][0]:layout[name][0] + layout[name][1]].reshape(_shard_shape(name, shape, axis))
            for name, shape, axis in PACKED}


def _full_from_slots(slots):
    layout, _ = _pack_layout()
    out = {}
    for name, shape, axis in PACKED:
        r, n = layout[name]
        if _by_rows(name, axis):
            out[name] = slots[:, r:r + n].reshape(N_CHIPS * n, PACK_W)
        else:
            sh = _shard_shape(name, shape, axis)
            out[name] = jnp.concatenate([slots[j, r:r + n].reshape(sh) for j in range(N_CHIPS)], axis=axis)
    return out


def _slots_from_full(full):
    layout, rows = _pack_layout()
    parts = []
    for name, shape, axis in PACKED:
        r, n = layout[name]
        if _by_rows(name, axis):
            parts.append(full[name].reshape(N_CHIPS, n, PACK_W))
        else:
            size = shape[axis] // N_CHIPS
            parts.append(jnp.stack([lax.slice_in_dim(full[name], j * size, (j + 1) * size, axis=axis).reshape(n, PACK_W)
                                    for j in range(N_CHIPS)]))
    parts.append(jnp.zeros((N_CHIPS, rows - sum(p.shape[1] for p in parts), PACK_W), parts[0].dtype))
    return jnp.concatenate(parts, axis=1)


def _pack_small(vals):
    parts = []
    for name, n in SMALL:
        pad = -(-n // 128) * 128 - n
        parts.append(jnp.pad(vals[name].reshape(-1).astype(F32), (0, pad)).reshape(-1, 128))
    rows = sum(p.shape[0] for p in parts)
    parts.append(jnp.zeros((-(-rows // 8) * 8 - rows, 128), F32))
    return jnp.concatenate(parts, axis=0)


def _unpack_small(packed):
    out, r = {}, 0
    for name, n in SMALL:
        k = -(-n // 128)
        out[name] = packed[r:r + k].reshape(-1)[:n]
        r += k
    return out


def _pad_heads(w, axis, per_head, lo, hi):
    shape = w.shape
    w = w.reshape(shape[:axis] + (N_HEADS, per_head) + shape[axis + 1:])
    w = lax.slice_in_dim(w, lo, hi, axis=axis + 1)
    pad = [(0, 0)] * w.ndim
    pad[axis + 1] = (0, HP - (hi - lo))
    w = jnp.pad(w, pad)
    return w.reshape(shape[:axis] + (N_HEADS * HP,) + shape[axis + 1:])


def _unpad_heads(w, axis, keep):
    shape = w.shape
    w = w.reshape(shape[:axis] + (N_HEADS, HP) + shape[axis + 1:])
    return lax.slice_in_dim(w, 0, keep, axis=axis + 1)


def _pad_w_in(wt):
    o = [0]
    for s in IN_SPLITS:
        o.append(o[-1] + s)
    cq, ckv, kpe, z, xbc, dtf, dtb, ga, gb = [wt[o[i]:o[i + 1]] for i in range(len(IN_SPLITS))]
    kpe_pad = jnp.pad(kpe, ((QK_NOPE, HP - QK_HEAD), (0, 0)))
    dt_pad = jnp.pad(jnp.concatenate([dtf, dtb], axis=0), ((0, HP - 2 * SSM_HEADS), (0, 0)))
    return jnp.concatenate([z, ga, gb, xbc, cq, ckv, kpe_pad, dt_pad], axis=0)


def _unpad_w_in(gt):
    z, ga, gb, xbc = gt[U_Z:U_GA], gt[U_GA:U_GB], gt[U_GB:U_XBC], gt[U_XBC:U_SMALL]
    s = gt[U_SMALL:]
    cq, ckv = s[S_CQ:S_CKV], s[S_CKV:S_KPE]
    kpe = s[S_KPE + QK_NOPE:S_KPE + QK_HEAD]
    dtf, dtb = s[S_DT:S_DT + SSM_HEADS], s[S_DT + SSM_HEADS:S_DT + 2 * SSM_HEADS]
    return jnp.concatenate([cq, ckv, kpe, z, xbc, dtf, dtb, ga, gb], axis=0)


def _lanes128(parts):
    row = jnp.concatenate([p.reshape(-1) for p in parts])
    return jnp.pad(row, (0, HP - row.shape[0])).reshape(1, HP)


FF_TILE = D_FF // 2
WGRAD = BF


def _ffn_fwd(x, g, wg_t, wu_t, wd, tag):
    h = _rms_fwd(x, g, name=tag + "_norm")
    gate, up, act = _mm([h], [wg_t, wu_t], name=tag + "_up", tb=True, out_dtypes=(BF, BF, BF), tm=512, tn=FF_TILE,
                        epilogue=lambda a, b: (a, b, _silu(a) * b))
    out = _mm([act], [wd], name=tag + "_down", extras=[x], tk=D_FF, epilogue=lambda acc, r: (r + 0.5 * acc,))
    return out, (h, gate, up, act)


def _ffn_bwd(dout, dout_bf, x, g, wg_t, wu_t, wd, saved, tag):
    h, gate, up, act = saved

    def swiglu_bwd(acc, a, b):
        a, b, half = a.astype(F32), b.astype(F32), 0.5 * acc
        s = jax.nn.sigmoid(a)
        return half * b * (s * (1.0 + a * (1.0 - s))), half * (a * s)

    dgate, dup = _mm([dout_bf], [wd], name=tag + "_down_dx", tb=True, extras=[gate, up], out_dtypes=(BF, BF),
                     tm=512, tn=FF_TILE, epilogue=swiglu_bwd)
    dwd = _mm([act], [dout_bf], name=tag + "_down_dw", ta=True, tm=FF_TILE, tk=2048, out_dtypes=(WGRAD,),
              epilogue=lambda acc: (0.5 * acc,))
    dwg_t, dwu_t = _mm([dgate, dup], [h, h], name=tag + "_up_dw", ta=True, separate=True, out_dtypes=(WGRAD, WGRAD),
                       tm=FF_TILE, tk=1024)
    dh = _mm([dgate, dup], [wg_t, wu_t], name=tag + "_up_dx")
    dx, dx_bf, dg = _rms_bwd(dh, x, g, name=tag + "_norm_bwd", add=dout, out_dtypes=(F32, BF))
    return dx, dx_bf, dg, dwg_t, dwu_t, dwd


KPE_BLK = (U_SMALL + S_KPE) // HP
SMALL_BLK = U_SMALL // SMALL_W


def _local_step(x, pos_col, target, W, P):
    T = x.shape[0]
    sig = jax.nn.sigmoid
    x1, ffn1 = _ffn_fwd(x, P["ffn1_norm"], W["wg1"], W["wu1"], W["wd1"], "ffn1")
    h = _rms_fwd(x1, P["mix_norm"], name="mix_norm")
    u = _mm([h], [W["w_in"]], name="in_proj", tb=True, tn=1152)
    cqn = _rms_fwd(u, P["q_a_norm"], name="q_a_norm", blk_w=SMALL_W, blk_idx=SMALL_BLK, off=S_CQ, width=Q_LORA)
    ckvn = _rms_fwd(u, P["kv_a_norm"], name="kv_a_norm", blk_w=SMALL_W, blk_idx=SMALL_BLK, off=S_CKV, width=KV_LORA)
    q_raw = _mm([cqn], [W["wq"]], name="q_proj")
    def with_ones_lane(acc_k, acc_v):
        lane = lax.broadcasted_iota(jnp.int32, acc_v.shape, 1)
        return acc_k, jnp.where((lane & (HP - 1)) == V_HEAD, 1.0, acc_v)

    k_raw, v = _mm([ckvn], [W["wk"], W["wv"]], name="kv_proj", out_dtypes=(F32, BF), epilogue=with_ones_lane)
    rc, rs = _rope_tables(pos_col, P["freq"])
    q = _qk_prep_fwd(q_raw, None, P["q_head_norm"], rc, rs, name="q_prep", out_scale=Q_SCALE)
    k = _qk_prep_fwd(k_raw, u, P["k_head_norm"], rc, rs, name="k_prep", kpe_blk=KPE_BLK)
    o, o_bf, lse = _attn_fwd(q, k, v)
    pre, act = _conv_fwd(u, P["conv_w8"], P["conv_b"])
    scan_cols, scan_rows = _ssd_prep(u, P["dt_bias8"], P["a_log8"])
    y_f, st_f = _ssd_fwd(act, scan_cols, scan_rows, rev=False, name="ssd_fwd_f")
    y_b, st_b = _ssd_fwd(act, scan_cols, scan_rows, rev=True, name="ssd_fwd_b")
    ysum, m = _ssm_combine_fwd(y_f, y_b, act, u, P["d_skip_lanes"], P["ssm_norm"])
    ab = _mm([o_bf], [W["pa"]], name="attn_branch")
    mb, merged = _mm([m], [W["pb"]], name="ssm_branch", extras=[ab, u, u], extra_offs=(0, U_GA, U_GB), out_dtypes=(F32, BF),
                     epilogue=lambda acc, a, ga, gb: (acc, sig(ga) * a + sig(gb) * acc))
    x2 = _mm([merged], [W["wo"]], name="out_proj", extras=[x1], epilogue=lambda acc, r: (r + acc,))
    y, ffn2 = _ffn_fwd(x2, P["ffn2_norm"], W["wg2"], W["wu2"], W["wd2"], "ffn2")
    dy, dy_bf, loss = _loss_head(y, target)
    dx2, dx2_bf, dg_ffn2, dwg2, dwu2, dwd2 = _ffn_bwd(dy, dy_bf, x2, P["ffn2_norm"], W["wg2"], W["wu2"], W["wd2"], ffn2,
                                                      "ffn2")

    def gate_bwd(dmrg, a, b, ga, gb):
        sa, sb = sig(ga), sig(gb)
        return dmrg * sa, dmrg * sb, dmrg * a * sa * (1.0 - sa), dmrg * b * sb * (1.0 - sb)

    dab, dmb, dga, dgb = _mm([dx2_bf], [W["wo"]], name="out_proj_dx", tb=True, extras=[ab, mb, u, u],
                             extra_offs=(0, 0, U_GA, U_GB), out_dtypes=(BF,) * 4, epilogue=gate_bwd)
    dwo = _mm([merged], [dx2_bf], name="out_proj_dw", ta=True, out_dtypes=(WGRAD,))
    dpa = _mm([o_bf], [dab], name="attn_branch_dw", ta=True, out_dtypes=(WGRAD,))
    do = _mm([dab], [W["pa"]], name="attn_branch_dx", tb=True)
    dpb = _mm([m], [dmb], name="ssm_branch_dw", ta=True, out_dtypes=(WGRAD,))
    dm = _mm([dmb], [W["pb"]], name="ssm_branch_dx", tb=True)
    dyssd, dz, dxs_skip, dg_ssm, dskip = _ssm_combine_bwd(dm, ysum, act, u, P["d_skip_lanes"], P["ssm_norm"])
    dxs_f, db_f, dc_f, dsel_f, dtot_f = _ssd_bwd(act, scan_cols, scan_rows, st_f, dyssd, rev=False, name="ssd_bwd_f")
    dxs_b, db_b, dc_b, dsel_b, dtot_b = _ssd_bwd(act, scan_cols, scan_rows, st_b, dyssd, rev=True, name="ssd_bwd_b")
    ddt, dalog, dbias = _ssd_prep_bwd(u, P["dt_bias8"], P["a_log8"], dsel_f, dtot_f, dsel_b, dtot_b)
    dxbc, dconv = [], []
    for tag, col0, parts in (("x", 0, [dxs_f, dxs_b, dxs_skip]), ("b", D_INNER, [db_f, db_b]),
                             ("c", D_INNER + SSM_GROUPS * D_STATE, [dc_f, dc_b])):
        dpre = _conv_dpre(parts, pre, col0, name="conv_dpre_" + tag)
        dxp, dwp = _conv_bwd(dpre, u, P["conv_w8"], col0, name="conv_bwd_" + tag)
        dxbc.append(dxp)
        dconv.append(dwp)
    dconv = jnp.concatenate(dconv, axis=1)
    dq, dk, dv = _attn_bwd(q, k, v, do, o, lse)
    dq_raw, dg_qh = _qk_prep_bwd(dq, q_raw, None, P["q_head_norm"], rc, rs, name="q_prep_bwd", in_scale=ATTN_SCALE)
    dk_raw, dg_kh, dkpe = _qk_prep_bwd(dk, k_raw, u, P["k_head_norm"], rc, rs, name="k_prep_bwd", kpe_blk=KPE_BLK,
                                       in_scale=1.0 / LOG2E)
    dwq = _mm([cqn], [dq_raw], name="q_proj_dw", ta=True, out_dtypes=(WGRAD,))
    dcqn = _mm([dq_raw], [W["wq"]], name="q_proj_dx", tb=True)
    dwk, dwv = _mm([ckvn], [dk_raw, dv], name="kv_proj_dw", ta=True, out_dtypes=(WGRAD, WGRAD))
    dckvn = _mm([dk_raw, dv], [W["wk"], W["wv"]], name="kv_proj_dx", tb=True)
    dcq, dg_qa = _rms_bwd(dcqn, u, P["q_a_norm"], name="q_a_norm_bwd", blk_w=SMALL_W, blk_idx=SMALL_BLK, off=S_CQ,
                          width=Q_LORA, out_dtypes=(BF,))
    dckv, dg_kva = _rms_bwd(dckvn, u, P["kv_a_norm"], name="kv_a_norm_bwd", blk_w=SMALL_W, blk_idx=SMALL_BLK,
                            off=S_CKV, width=KV_LORA, out_dtypes=(BF,))
    du = jnp.concatenate([dz, dga, dgb] + dxbc + [dcq, dckv, dkpe.astype(BF), ddt.astype(BF)], axis=1)
    dw_in = _mm([du], [h], name="in_proj_dw", ta=True, tm=1152, out_dtypes=(WGRAD,))
    dh = _mm([du], [W["w_in"]], name="in_proj_dx", tk=U_PAD // 3)
    dx1, dx1_bf, dg_mix = _rms_bwd(dh, x1, P["mix_norm"], name="mix_norm_bwd", add=dx2, out_dtypes=(F32, BF))
    dx, _, dg_ffn1, dwg1, dwu1, dwd1 = _ffn_bwd(dx1, dx1_bf, x, P["ffn1_norm"], W["wg1"], W["wu1"], W["wd1"], ffn1, "ffn1")
    dW = dict(wg1=dwg1, wu1=dwu1, wd1=dwd1, w_in=dw_in, wq=dwq, wk=dwk, wv=dwv, pa=dpa, pb=dpb, wo=dwo,
              wg2=dwg2, wu2=dwu2, wd2=dwd2)
    dP = dict(ffn1_norm=dg_ffn1[0], mix_norm=dg_mix[0], q_a_norm=dg_qa[0], kv_a_norm=dg_kva[0],
              q_head_norm=dg_qh[0, :QK_HEAD], k_head_norm=dg_kh[0, :QK_HEAD], conv_b=dconv[CONV_WIDTH],
              a_log_fwd=dalog[0, :SSM_HEADS], a_log_bwd=dalog[0, SSM_HEADS:2 * SSM_HEADS],
              dt_bias_fwd=dbias[0, :SSM_HEADS], dt_bias_bwd=dbias[0, SSM_HEADS:2 * SSM_HEADS],
              d_skip=dskip[0].reshape(SSM_GROUPS, HP)[:, :HG], ssm_norm=dg_ssm[0], ffn2_norm=dg_ffn2[0],
              conv_w=dconv[:CONV_WIDTH], loss=loss[0, 0])
    return dx, dW, dP


def _prepare(w, conv_w_full):
    kvb = w["w_kv_b"]
    W = dict(wg1=w["ffn1_w_gate"], wu1=w["ffn1_w_up"], wd1=w["ffn1_w_down"], w_in=_pad_w_in(w["w_in"]),
             wq=_pad_heads(w["w_q_b"], 1, QK_HEAD, 0, QK_HEAD),
             wk=_pad_heads(kvb, 1, QK_NOPE + V_HEAD, 0, QK_NOPE),
             wv=_pad_heads(kvb, 1, QK_NOPE + V_HEAD, QK_NOPE, QK_NOPE + V_HEAD),
             pa=_pad_heads(w["w_attn_branch"], 0, V_HEAD, 0, V_HEAD), pb=w["w_ssm_branch"], wo=w["w_out"],
             wg2=w["ffn2_w_gate"], wu2=w["ffn2_w_up"], wd2=w["ffn2_w_down"])
    inv_freq = [1.0 / (ROPE_BASE ** (j / QK_ROPE)) for j in range(0, QK_ROPE, 2)]
    freq = [0.0] * QK_NOPE + inv_freq + inv_freq + [0.0] * (HP - QK_HEAD)
    P = {n: w[n] for n in ("ffn1_norm", "mix_norm", "q_a_norm", "kv_a_norm", "ssm_norm", "ffn2_norm", "conv_b")}
    P.update(q_head_norm=_lanes128([w["q_head_norm"]]), k_head_norm=_lanes128([w["k_head_norm"]]),
             conv_w8=jnp.pad(conv_w_full, ((0, 8 - CONV_WIDTH), (0, 0))),
             dt_bias8=jnp.broadcast_to(_lanes128([w["dt_bias_fwd"], w["dt_bias_bwd"]]), (8, HP)),
             a_log8=jnp.broadcast_to(_lanes128([w["a_log_fwd"], w["a_log_bwd"]]), (8, HP)),
             d_skip_lanes=jnp.repeat(w["d_skip"].reshape(-1), PH).reshape(1, D_INNER),
             freq=jnp.asarray(freq, F32).reshape(1, HP))
    return W, P


def _unprepare(dW):
    dkvb = jnp.concatenate([_unpad_heads(dW["wk"], 1, QK_NOPE), _unpad_heads(dW["wv"], 1, V_HEAD)], axis=2)
    return dict(ffn1_w_gate=dW["wg1"], ffn1_w_up=dW["wu1"], ffn1_w_down=dW["wd1"], w_in=_unpad_w_in(dW["w_in"]),
                w_q_b=_unpad_heads(dW["wq"], 1, QK_HEAD).reshape(Q_LORA, N_HEADS * QK_HEAD),
                w_kv_b=dkvb.reshape(KV_LORA, N_HEADS * (QK_NOPE + V_HEAD)),
                w_attn_branch=_unpad_heads(dW["pa"], 0, V_HEAD).reshape(N_HEADS * V_HEAD, D_MODEL),
                w_ssm_branch=dW["pb"], w_out=dW["wo"],
                ffn2_w_gate=dW["wg2"], ffn2_w_up=dW["wu2"], ffn2_w_down=dW["wd2"])


def kernel(x, positions, ffn1_norm, ffn1_w_gate, ffn1_w_up, ffn1_w_down, mix_norm, w_in, q_a_norm, w_q_b, kv_a_norm, w_kv_b, q_head_norm, k_head_norm, conv_w, conv_b, a_log_fwd, a_log_bwd, dt_bias_fwd, dt_bias_bwd, d_skip, ssm_norm, w_attn_branch, w_ssm_branch, w_out, ffn2_norm, ffn2_w_gate, ffn2_w_up, ffn2_w_down, loss_target, m_ffn1_norm, m_ffn1_w_gate, m_ffn1_w_up, m_ffn1_w_down, m_mix_norm, m_w_in, m_q_a_norm, m_w_q_b, m_kv_a_norm, m_w_kv_b, m_q_head_norm, m_k_head_norm, m_conv_w, m_conv_b, m_a_log_fwd, m_a_log_bwd, m_dt_bias_fwd, m_dt_bias_bwd, m_d_skip, m_ssm_norm, m_w_attn_branch, m_w_ssm_branch, m_w_out, m_ffn2_norm, m_ffn2_w_gate, m_ffn2_w_up, m_ffn2_w_down, v_ffn1_norm, v_ffn1_w_gate, v_ffn1_w_up, v_ffn1_w_down, v_mix_norm, v_w_in, v_q_a_norm, v_w_q_b, v_kv_a_norm, v_w_kv_b, v_q_head_norm, v_k_head_norm, v_conv_w, v_conv_b, v_a_log_fwd, v_a_log_bwd, v_dt_bias_fwd, v_dt_bias_bwd, v_d_skip, v_ssm_norm, v_w_attn_branch, v_w_ssm_branch, v_w_out, v_ffn2_norm, v_ffn2_w_gate, v_ffn2_w_up, v_ffn2_w_down):
    given = dict(locals())
    T = x.shape[1]
    packed_names = [name for name, _, _ in PACKED]

    def two_d(a):
        return a.reshape(a.shape[1], -1) if a.ndim > 2 else a

    def kept(n, a):
        return _stored(n, two_d(a))

    w_loc = {n: kept(n, given[n]) for n in WEIGHTS}
    wb = _pack({n: w_loc[n].astype(BF) for n in packed_names})
    wf = jnp.pad(w_loc["conv_w"], ((0, 8 - CONV_WIDTH), (0, 0)))
    gb, gf = _gather_chips(wb, wf)
    full = _full_from_slots(gb)
    conv_w_full = jnp.concatenate([gf[j, :CONV_WIDTH] for j in range(N_CHIPS)], axis=1)
    full.update({n: w_loc[n] for n in WEIGHTS if n not in full and n != "conv_w"})
    W, P = _prepare(full, conv_w_full)
    dx, dW, dP = _local_step(x.reshape(T, D_MODEL), positions.reshape(T, 1).astype(F32), loss_target.reshape(T, D_MODEL), W, P)
    gp = _slots_from_full(_unprepare(dW))
    core = lax.axis_index("c").astype(jnp.int32).reshape(1)
    both_cores = _add_halves(gp, _halves_to_sibling(gp), core)
    cx, cy = lax.axis_index("x"), lax.axis_index("y")
    chips = jnp.stack([2 * cx + cy, 2 * (1 - cx) + cy, 2 * cx + (1 - cy)]).astype(jnp.int32)
    part, on = _add_near(both_cores, _exchange_near(both_cores), chips)
    grads = _unpack(_join_halves(_add_far(part, _exchange_far(on), core)))
    small = _unpack_small(_allreduce_small(_pack_small(dP)))
    grads.update({n: small[n].reshape(1, -1) for n, _ in SMALL if n not in ("conv_w", "loss")})
    grads["conv_w"] = lax.dynamic_slice_in_dim(small["conv_w"].reshape(CONV_WIDTH, XBC_DIM), _my_chip() * (XBC_DIM // N_CHIPS),
                                               XBC_DIM // N_CHIPS, axis=1)
    out_g, out_d, out_m, out_v = [], [], [], []
    for n in WEIGHTS:
        shape = given[n].shape
        delta, new_m, new_v = _adamw(w_loc[n], grads[n], kept(n, given["m_" + n]), kept(n, given["v_" + n]), name="adamw_" + n)
        for outs, a in ((out_g, grads[n]), (out_d, delta), (out_m, new_m), (out_v, new_v)):
            outs.append(_stored(n, a).reshape(shape))
    return (small["loss"].reshape(()), dx.reshape(x.shape), *out_g, *out_d, *out_m, *out_v)
```

```python
import functools
import math

import jax
import jax.numpy as jnp
from jax import lax
from jax.experimental import pallas as pl
from jax.experimental.pallas import tpu as pltpu

BF = jnp.bfloat16
F32 = jnp.float32
HI = lax.Precision.HIGHEST
MESH = pl.DeviceIdType.MESH

D_MODEL = 1024
D_FF = 2816
EPS = 1e-6
N_HEADS = 16
QK_NOPE = 64
QK_ROPE = 32
QK_HEAD = 96
V_HEAD = 64
Q_LORA = 384
KV_LORA = 256
ROPE_BASE = 10000.0
D_INNER = 2048
SSM_HEADS = 32
SSM_GROUPS = 4
D_STATE = 128
CONV_WIDTH = 5
CHUNK = 128
XBC_DIM = 3072
HP = 128
GW = D_INNER // SSM_GROUPS
HG = SSM_HEADS // SSM_GROUPS
PH = 64
U_Z, U_GA, U_GB, U_XBC, U_SMALL = 0, 2048, 3072, 4096, 7168
S_CQ, S_CKV, S_KPE, S_DT, SMALL_W = 0, 384, 640, 768, 896
U_PAD = U_SMALL + SMALL_W
IN_SPLITS = (Q_LORA, KV_LORA, QK_ROPE, D_INNER, XBC_DIM, SSM_HEADS, SSM_HEADS, D_MODEL, D_MODEL)

ADAM_LR = 0.001
ADAM_B1 = 0.9
ADAM_B2 = 0.999
ADAM_EPS = 1e-08
ADAM_WD = 0.01
ADAM_STEP = 10

V7X_VMEM_BYTES = 64 << 20
MM_VMEM_BUDGET = V7X_VMEM_BYTES * 5 // 8

NT = (((1,), (1,)), ((), ()))
TN = (((0,), (0,)), ((), ()))


def _pick(n, pref):
    best = None
    d = 128
    while d <= min(n, pref):
        if n % d == 0:
            best = d
        d += 128
    return best if best is not None else n


def _silu(x):
    return x * jax.nn.sigmoid(x)


def _dsilu(x):
    s = jax.nn.sigmoid(x)
    return s * (1.0 + x * (1.0 - s))


def _softplus(x):
    return jnp.maximum(x, 0.0) + jnp.log(1.0 + jnp.exp(-jnp.abs(x)))


def _mm(As, Bs, *, name, ta=False, tb=False, out_dtypes=(F32,), epilogue=None, extras=(), extra_offs=None,
        tm=1024, tn=512, tk=None, separate=False):
    As, Bs, extras = list(As), list(Bs), list(extras)
    a0, b0 = As[0], Bs[0]
    M, K = (a0.shape[1], a0.shape[0]) if ta else a0.shape
    N = b0.shape[0] if tb else b0.shape[1]
    tm, tn = _pick(M, tm), _pick(N, tn)
    n_a, n_b, n_e, n_o = len(As), len(Bs), len(extras), len(out_dtypes)
    n_res = n_b if n_a == 1 or separate else 1

    def vmem_bytes(k_tile):
        blocks = sum(tm * k_tile * a.dtype.itemsize for a in As) + sum(k_tile * tn * b.dtype.itemsize for b in Bs)
        tiles = tm * tn * (sum(jnp.dtype(dt).itemsize for dt in out_dtypes) + sum(e.dtype.itemsize for e in extras))
        return 2 * (blocks + tiles) + 2 * n_res * tm * tn * 4

    if tk is None:
        tk = K
        while vmem_bytes(tk) > MM_VMEM_BUDGET and tk > 128:
            tk = _pick(K, tk - 128)
    else:
        tk = _pick(K, tk)
    nk = K // tk
    n_acc = n_res if nk > 1 else 0
    if extra_offs is None:
        extra_offs = (0,) * n_e
    dn = (((0,) if ta else (1,), (1,) if tb else (0,)), ((), ()))
    bytes_a = sum(a.size * a.dtype.itemsize for a in As)
    bytes_b = sum(b.size * b.dtype.itemsize for b in Bs)
    n_outer = (N // tn) * bytes_a + bytes_b < (M // tm) * bytes_b + bytes_a

    def products(a_refs, b_refs):
        if n_a == 1:
            a = a_refs[0][...].astype(BF)
            return [lax.dot_general(a, b[...].astype(BF), dn, preferred_element_type=F32) for b in b_refs]
        if separate:
            return [lax.dot_general(a[...].astype(BF), b[...].astype(BF), dn, preferred_element_type=F32)
                    for a, b in zip(a_refs, b_refs)]
        total = None
        for a, b in zip(a_refs, b_refs):
            p = lax.dot_general(a[...].astype(BF), b[...].astype(BF), dn, preferred_element_type=F32)
            total = p if total is None else total + p
        return [total]

    def finish(accs, e_refs, o_refs):
        ex = [e[...] for e in e_refs]
        outs = epilogue(*accs, *ex) if epilogue is not None else tuple(accs)
        for o_ref, val in zip(o_refs, outs):
            o_ref[...] = val.astype(o_ref.dtype)

    def body(*refs):
        a_refs, b_refs = refs[:n_a], refs[n_a:n_a + n_b]
        e_refs = refs[n_a + n_b:n_a + n_b + n_e]
        o_refs = refs[n_a + n_b + n_e:n_a + n_b + n_e + n_o]
        acc_refs = refs[n_a + n_b + n_e + n_o:]
        if nk == 1:
            finish(products(a_refs, b_refs), e_refs, o_refs)
            return
        k = pl.program_id(2)

        @pl.when(k == 0)
        def _():
            for acc in acc_refs:
                acc[...] = jnp.zeros_like(acc)

        for acc, p in zip(acc_refs, products(a_refs, b_refs)):
            acc[...] += p

        @pl.when(k == nk - 1)
        def _():
            finish([acc[...] for acc in acc_refs], e_refs, o_refs)

    def at(f):
        return (lambda j, i, k: f(i, j, k)) if n_outer else f

    a_spec = pl.BlockSpec((tk, tm), at(lambda i, j, k: (k, i))) if ta else pl.BlockSpec((tm, tk), at(lambda i, j, k: (i, k)))
    b_spec = pl.BlockSpec((tn, tk), at(lambda i, j, k: (j, k))) if tb else pl.BlockSpec((tk, tn), at(lambda i, j, k: (k, j)))
    e_specs = [pl.BlockSpec((tm, tn), at(functools.partial(lambda i, j, k, o: (i, j + o), o=off // tn))) for off in extra_offs]
    for off in extra_offs:
        assert off % tn == 0
    outs = pl.pallas_call(
        body, name=name,
        out_shape=tuple(jax.ShapeDtypeStruct((M, N), dt) for dt in out_dtypes),
        grid=(N // tn, M // tm, nk) if n_outer else (M // tm, N // tn, nk),
        in_specs=[a_spec] * n_a + [b_spec] * n_b + e_specs,
        out_specs=tuple(pl.BlockSpec((tm, tn), at(lambda i, j, k: (i, j))) for _ in out_dtypes),
        scratch_shapes=[pltpu.VMEM((tm, tn), F32)] * n_acc,
        compiler_params=pltpu.CompilerParams(dimension_semantics=("parallel", "parallel", "arbitrary")),
    )(*As, *Bs, *extras)
    return outs[0] if n_o == 1 else outs


def _rms_fwd(x, g, *, name, blk_w=None, blk_idx=0, off=0, width=None, out_dtype=BF):
    T = x.shape[0]
    blk_w = x.shape[1] if blk_w is None else blk_w
    width = blk_w if width is None else width
    tt = _pick(T, 512)

    def body(x_ref, g_ref, o_ref):
        xf = x_ref[:, off:off + width]
        r = lax.rsqrt(jnp.mean(xf * xf, axis=-1, keepdims=True) + EPS)
        o_ref[...] = (xf * r * g_ref[...]).astype(o_ref.dtype)

    return pl.pallas_call(
        body, name=name, out_shape=jax.ShapeDtypeStruct((T, width), out_dtype), grid=(T // tt,),
        in_specs=[pl.BlockSpec((tt, blk_w), lambda i: (i, blk_idx)), pl.BlockSpec((1, width), lambda i: (0, 0))],
        out_specs=pl.BlockSpec((tt, width), lambda i: (i, 0)),
    )(x, g)


def _rms_bwd(dy, x, g, *, name, blk_w=None, blk_idx=0, off=0, width=None, add=None, out_dtypes=(F32,)):
    T = x.shape[0]
    blk_w = x.shape[1] if blk_w is None else blk_w
    width = blk_w if width is None else width
    tt = _pick(T, 512)
    has_add = add is not None
    n_dx = len(out_dtypes)

    def body(*refs):
        dy_ref, x_ref, g_ref = refs[:3]
        dx_refs, dg_ref = refs[3 + has_add:3 + has_add + n_dx], refs[-1]
        xf = x_ref[:, off:off + width]
        d = dy_ref[...].astype(F32)
        r = lax.rsqrt(jnp.mean(xf * xf, axis=-1, keepdims=True) + EPS)
        gd = d * g_ref[...]
        dx = r * gd - xf * (r * r * r) * jnp.mean(gd * xf, axis=-1, keepdims=True)
        if has_add:
            dx = dx + refs[3][...]
        for dx_ref in dx_refs:
            dx_ref[...] = dx.astype(dx_ref.dtype)

        @pl.when(pl.program_id(0) == 0)
        def _():
            dg_ref[...] = jnp.zeros_like(dg_ref)

        dg_ref[...] += jnp.broadcast_to(jnp.sum(d * xf * r, axis=0, keepdims=True), dg_ref.shape)

    row = pl.BlockSpec((tt, width), lambda i: (i, 0))
    in_specs = [row, pl.BlockSpec((tt, blk_w), lambda i: (i, blk_idx)), pl.BlockSpec((1, width), lambda i: (0, 0))]
    args = [dy, x, g]
    if has_add:
        in_specs.append(row)
        args.append(add)
    return pl.pallas_call(
        body, name=name,
        out_shape=tuple(jax.ShapeDtypeStruct((T, width), dt) for dt in out_dtypes) + (jax.ShapeDtypeStruct((8, width), F32),),
        grid=(T // tt,), in_specs=in_specs,
        out_specs=(row,) * n_dx + (pl.BlockSpec((8, width), lambda i: (0, 0)),),
        compiler_params=pltpu.CompilerParams(dimension_semantics=("arbitrary",)),
    )(*args)


def _rope_tables(pos_col, freq_lane):
    T = pos_col.shape[0]
    tt = _pick(T, 512)

    def body(p_ref, f_ref, c_ref, s_ref):
        ang = p_ref[...] * f_ref[...]
        lane = lax.broadcasted_iota(jnp.int32, ang.shape, 1)
        c_ref[...] = jnp.where(lane < QK_HEAD, jnp.cos(ang), 0.0)
        sn = jnp.sin(ang)
        s_ref[...] = jnp.where((lane >= QK_NOPE) & (lane < QK_NOPE + 16), -sn,
                               jnp.where((lane >= QK_NOPE + 16) & (lane < QK_HEAD), sn, 0.0))

    return pl.pallas_call(
        body, name="rope_tables", out_shape=(jax.ShapeDtypeStruct((T, HP), F32),) * 2, grid=(T // tt,),
        in_specs=[pl.BlockSpec((tt, 1), lambda i: (i, 0)), pl.BlockSpec((1, HP), lambda i: (0, 0))],
        out_specs=(pl.BlockSpec((tt, HP), lambda i: (i, 0)),) * 2,
    )(pos_col, freq_lane)


def _swap_rope_halves(n):
    src = lax.broadcasted_iota(jnp.int32, (HP, HP), 0)
    dst = lax.broadcasted_iota(jnp.int32, (HP, HP), 1)
    lo = (dst >= QK_NOPE) & (dst < QK_NOPE + 16) & (src == dst + 16)
    hi = (dst >= QK_NOPE + 16) & (dst < QK_HEAD) & (src == dst - 16)
    return _split_dot(n, jnp.where(lo | hi, 1.0, 0.0).astype(BF), 2)


def _qk_prep_fwd(raw, kpe, gain, C, S, *, name, kpe_blk=0, out_scale=1.0):
    T = raw.shape[0]
    tt = _pick(T, 256)
    has_kpe = kpe is not None

    def body(*refs):
        if has_kpe:
            raw_ref, kpe_ref, g_ref, c_ref, s_ref, o_ref = refs
        else:
            raw_ref, g_ref, c_ref, s_ref, o_ref = refs
        for h in range(N_HEADS):
            hs = slice(HP * h, HP * (h + 1))
            xr = raw_ref[:, hs] + kpe_ref[...] if has_kpe else raw_ref[:, hs]
            r = lax.rsqrt(jnp.sum(xr * xr, axis=-1, keepdims=True) * (1.0 / QK_HEAD) + EPS)
            n = xr * r * g_ref[...]
            o_ref[:, hs] = ((n * c_ref[...] + _swap_rope_halves(n) * s_ref[...]) * out_scale).astype(o_ref.dtype)

    heads = pl.BlockSpec((tt, N_HEADS * HP), lambda i: (i, 0))
    shared = pl.BlockSpec((tt, HP), lambda i: (i, 0))
    kpe_spec = pl.BlockSpec((tt, HP), lambda i: (i, kpe_blk))
    in_specs = [heads] + ([kpe_spec] if has_kpe else []) + [pl.BlockSpec((1, HP), lambda i: (0, 0)), shared, shared]
    args = [raw] + ([kpe] if has_kpe else []) + [gain, C, S]
    return pl.pallas_call(
        body, name=name, out_shape=jax.ShapeDtypeStruct(raw.shape, BF), grid=(T // tt,),
        in_specs=in_specs, out_specs=heads,
    )(*args)


def _qk_prep_bwd(dout, raw, kpe, gain, C, S, *, name, kpe_blk=0, in_scale=1.0):
    T = raw.shape[0]
    tt = _pick(T, 256)
    has_kpe = kpe is not None

    def body(*refs):
        if has_kpe:
            d_ref, raw_ref, kpe_ref, g_ref, c_ref, s_ref, dx_ref, dg_ref, dkpe_ref = refs
        else:
            d_ref, raw_ref, g_ref, c_ref, s_ref, dx_ref, dg_ref = refs
        dg = jnp.zeros((1, HP), F32)
        dkpe = jnp.zeros((tt, HP), F32)
        for h in range(N_HEADS):
            hs = slice(HP * h, HP * (h + 1))
            xr = raw_ref[:, hs] + kpe_ref[...] if has_kpe else raw_ref[:, hs]
            d = d_ref[:, hs].astype(F32) * in_scale
            r = lax.rsqrt(jnp.sum(xr * xr, axis=-1, keepdims=True) * (1.0 / QK_HEAD) + EPS)
            dn = d * c_ref[...] + _swap_rope_halves(d * s_ref[...])
            gd = dn * g_ref[...]
            dx = r * gd - xr * (r * r * r) * (jnp.sum(gd * xr, axis=-1, keepdims=True) * (1.0 / QK_HEAD))
            dx_ref[:, hs] = dx.astype(dx_ref.dtype)
            dg = dg + jnp.sum(dn * xr * r, axis=0, keepdims=True)
            dkpe = dkpe + dx

        @pl.when(pl.program_id(0) == 0)
        def _():
            dg_ref[...] = jnp.zeros_like(dg_ref)

        dg_ref[...] += jnp.broadcast_to(dg, dg_ref.shape)
        if has_kpe:
            dkpe_ref[...] = dkpe

    heads = pl.BlockSpec((tt, N_HEADS * HP), lambda i: (i, 0))
    shared = pl.BlockSpec((tt, HP), lambda i: (i, 0))
    kpe_spec = pl.BlockSpec((tt, HP), lambda i: (i, kpe_blk))
    in_specs = [heads, heads] + ([kpe_spec] if has_kpe else []) + [pl.BlockSpec((1, HP), lambda i: (0, 0)), shared, shared]
    args = [dout, raw] + ([kpe] if has_kpe else []) + [gain, C, S]
    out_shape = [jax.ShapeDtypeStruct(raw.shape, BF), jax.ShapeDtypeStruct((8, HP), F32)]
    out_specs = [heads, pl.BlockSpec((8, HP), lambda i: (0, 0))]
    if has_kpe:
        out_shape.append(jax.ShapeDtypeStruct((T, HP), F32))
        out_specs.append(shared)
    return pl.pallas_call(
        body, name=name, out_shape=tuple(out_shape), grid=(T // tt,),
        in_specs=in_specs, out_specs=tuple(out_specs),
        compiler_params=pltpu.CompilerParams(dimension_semantics=("arbitrary",)),
    )(*args)


ATTN_SCALE = 1.0 / math.sqrt(QK_HEAD)
LOG2E = 1.0 / math.log(2.0)
Q_SCALE = ATTN_SCALE * LOG2E


def _attn_fwd(q, k, v):
    T = q.shape[0]
    tq = _pick(T, 1024)
    tk = _pick(T, 1024)

    def body(q_ref, k_ref, v_ref, o_ref, ob_ref, lse_ref):
        qt = q_ref[...]
        m = o = None
        for j in range(T // tk):
            ks = slice(j * tk, (j + 1) * tk)
            s = lax.dot_general(qt, k_ref[ks, :], NT, preferred_element_type=F32)
            m_j = jnp.max(s, axis=-1, keepdims=True)
            m_new = m_j if m is None else jnp.maximum(m, m_j)
            o_j = jnp.dot(jnp.exp2(s - m_new).astype(BF), v_ref[ks, :], preferred_element_type=F32)
            o = o_j if o is None else o * jnp.exp2(m - m_new) + o_j
            m = m_new
        l = o[:, V_HEAD:V_HEAD + 1]
        o = o / l
        o_ref[...] = o
        ob_ref[...] = o.astype(ob_ref.dtype)
        lse_ref[...] = jnp.broadcast_to(m + jnp.log2(l), lse_ref.shape)

    qs = pl.BlockSpec((tq, HP), lambda h, i: (i, h))
    kv = pl.BlockSpec((T, HP), lambda h, i: (0, h))
    return pl.pallas_call(
        body, name="attn_fwd",
        out_shape=(jax.ShapeDtypeStruct(q.shape, F32), jax.ShapeDtypeStruct(q.shape, BF), jax.ShapeDtypeStruct(q.shape, F32)),
        grid=(N_HEADS, T // tq), in_specs=[qs, kv, kv], out_specs=(qs, qs, qs),
        compiler_params=pltpu.CompilerParams(dimension_semantics=("parallel", "parallel")),
    )(q, k, v)


def _attn_bwd(q, k, v, do, o, lse):
    T = q.shape[0]
    tb = _pick(T, 512)
    nb = T // tb
    tkey = _pick(T, 1024)

    def body(q_ref, k_ref, v_ref, do_ref, o_ref, lse_ref, dq_ref, dk_ref, dv_ref, delta_rows, lse_rows, dob_scr, dv_acc):
        dq_ref[...] = jnp.zeros_like(dq_ref)
        dk_ref[...] = jnp.zeros_like(dk_ref)
        lane = lax.broadcasted_iota(jnp.int32, (8, HP), 1)
        ones8 = jnp.ones((8, HP), BF)
        first8 = jnp.where(lane == 0, 1.0, 0.0).astype(BF)

        def as_rows(pick, v):
            total, rest = None, v
            for _ in range(3):
                piece = rest.astype(BF)
                part = lax.dot_general(pick, piece, NT, preferred_element_type=F32)
                total = part if total is None else total + part
                rest = rest - piece.astype(F32)
            return total

        def per_q_tile(i, carry):
            qs = pl.ds(pl.multiple_of(i * tb, tb), tb)
            doi = do_ref[qs, :]
            delta_rows[i] = as_rows(ones8, doi * o_ref[qs, :])
            lse_rows[i] = as_rows(first8, lse_ref[qs, :])
            dob_scr[qs, :] = doi.astype(BF)
            return carry

        lax.fori_loop(0, nb, per_q_tile, 0)

        def k_loop(j, carry):
            ks = pl.ds(pl.multiple_of(j * tkey, tkey), tkey)
            kj, vj = k_ref[ks, :], v_ref[ks, :]

            dv_acc[...] = jnp.zeros_like(dv_acc)

            def q_loop(i, carry_q):
                qs = pl.ds(pl.multiple_of(i * tb, tb), tb)
                qi = q_ref[qs, :]
                dob = dob_scr[qs, :]
                s_t = lax.dot_general(kj, qi, NT, preferred_element_type=F32)
                p_t = jnp.exp2(s_t - lse_rows[i, 0:1, :])
                dp_t = lax.dot_general(vj, dob, NT, preferred_element_type=F32)
                ds_t = (p_t * (dp_t - delta_rows[i, 0:1, :])).astype(BF)
                dv_acc[...] += jnp.dot(p_t.astype(BF), dob, preferred_element_type=F32)
                dk_ref[ks, :] += jnp.dot(ds_t, qi, preferred_element_type=F32)
                dq_ref[qs, :] += lax.dot_general(ds_t, kj, TN, preferred_element_type=F32)
                return carry_q

            lax.fori_loop(0, nb, q_loop, 0)
            dv_ref[ks, :] = dv_acc[...].astype(dv_ref.dtype)
            return carry

        lax.fori_loop(0, T // tkey, k_loop, 0)

    spec = pl.BlockSpec((T, HP), lambda h: (0, h))
    return pl.pallas_call(
        body, name="attn_bwd",
        out_shape=(jax.ShapeDtypeStruct(q.shape, F32), jax.ShapeDtypeStruct(q.shape, F32), jax.ShapeDtypeStruct(q.shape, BF)),
        grid=(N_HEADS,), in_specs=[spec] * 6, out_specs=(spec,) * 3,
        scratch_shapes=[pltpu.VMEM((nb, 8, tb), F32), pltpu.VMEM((nb, 8, tb), F32), pltpu.VMEM((T, HP), BF),
                        pltpu.VMEM((tkey, HP), F32)],
        compiler_params=pltpu.CompilerParams(dimension_semantics=("parallel",), vmem_limit_bytes=2 * 15 * T * HP * 2 + (8 << 20)),
    )(q, k, v, do, o, lse)


CONV_TC = 512
CONV_PAD = CONV_WIDTH // 2


def _halo_specs(tr, col_of):
    r8 = tr // 8
    cur = pl.BlockSpec((tr, CONV_TC), lambda j, i: (i, col_of(j)))
    prev = pl.BlockSpec((8, CONV_TC), lambda j, i: (jnp.maximum(i * r8 - 1, 0), col_of(j)))

    def nxt_map(j, i, n8):
        return (jnp.minimum((i + 1) * r8, n8 - 1), col_of(j))

    return cur, prev, nxt_map


def _with_halo(prev_ref, cur_ref, next_ref, i, n_i):
    prev = jnp.where(i == 0, 0.0, prev_ref[...].astype(F32))
    nxt = jnp.where(i == n_i - 1, 0.0, next_ref[...].astype(F32))
    return jnp.concatenate([prev, cur_ref[...].astype(F32), nxt], axis=0)


def _conv_fwd(u, w8, b):
    T = u.shape[0]
    tr = _pick(T, 512)
    n_i = T // tr
    c0 = U_XBC // CONV_TC
    cur, prev, nxt_map = _halo_specs(tr, lambda j: c0 + j)
    nxt = pl.BlockSpec((8, CONV_TC), functools.partial(nxt_map, n8=T // 8))

    def body(p_ref, c_ref, n_ref, w_ref, b_ref, pre_ref, act_ref):
        i = pl.program_id(1)
        full = _with_halo(p_ref, c_ref, n_ref, i, n_i)
        acc = jnp.broadcast_to(b_ref[...], (tr, CONV_TC))
        for kk in range(CONV_WIDTH):
            acc = acc + full[8 - CONV_PAD + kk:8 - CONV_PAD + kk + tr, :] * w_ref[kk:kk + 1, :]
        pre_ref[...] = acc
        act_ref[...] = _silu(acc)

    out = pl.BlockSpec((tr, CONV_TC), lambda j, i: (i, j))
    return pl.pallas_call(
        body, name="conv_fwd", out_shape=(jax.ShapeDtypeStruct((T, XBC_DIM), F32),) * 2,
        grid=(XBC_DIM // CONV_TC, n_i),
        in_specs=[prev, cur, nxt, pl.BlockSpec((8, CONV_TC), lambda j, i: (0, j)), pl.BlockSpec((1, CONV_TC), lambda j, i: (0, j))],
        out_specs=(out, out),
    )(u, u, u, w8, b)


def _conv_dpre(dacts, pre, col0, *, name):
    T, width = dacts[0].shape
    tt = _pick(T, 512)
    n_d = len(dacts)
    c0 = col0 // CONV_TC

    def body(*refs):
        d = refs[0][...]
        for r in refs[1:n_d]:
            d = d + r[...]
        refs[n_d + 1][...] = d * _dsilu(refs[n_d][...])

    blk = pl.BlockSpec((tt, CONV_TC), lambda j, i: (i, j))
    return pl.pallas_call(
        body, name=name, out_shape=jax.ShapeDtypeStruct((T, width), F32), grid=(width // CONV_TC, T // tt),
        in_specs=[blk] * n_d + [pl.BlockSpec((tt, CONV_TC), lambda j, i: (i, c0 + j))], out_specs=blk,
    )(*dacts, pre)


def _conv_bwd(dpre, u, w8, col0, *, name):
    T, width = dpre.shape
    tr = _pick(T, 512)
    n_i = T // tr
    cd = col0 // CONV_TC
    cx = (U_XBC + col0) // CONV_TC
    d_cur, d_prev, d_nxt_map = _halo_specs(tr, lambda j: j)
    x_cur, x_prev, x_nxt_map = _halo_specs(tr, lambda j: cx + j)
    d_nxt = pl.BlockSpec((8, CONV_TC), functools.partial(d_nxt_map, n8=T // 8))
    x_nxt = pl.BlockSpec((8, CONV_TC), functools.partial(x_nxt_map, n8=T // 8))

    def body(dp_ref, dc_ref, dn_ref, xp_ref, xc_ref, xn_ref, w_ref, dx_ref, dw_ref):
        i = pl.program_id(1)
        dfull = _with_halo(dp_ref, dc_ref, dn_ref, i, n_i)
        xfull = _with_halo(xp_ref, xc_ref, xn_ref, i, n_i)
        dcur = dc_ref[...]
        dx = jnp.zeros((tr, CONV_TC), F32)
        rows = []
        for kk in range(CONV_WIDTH):
            dx = dx + dfull[8 + CONV_PAD - kk:8 + CONV_PAD - kk + tr, :] * w_ref[kk:kk + 1, :]
            rows.append(jnp.sum(dcur * xfull[8 - CONV_PAD + kk:8 - CONV_PAD + kk + tr, :], axis=0, keepdims=True))
        rows.append(jnp.sum(dcur, axis=0, keepdims=True))
        rows.append(jnp.zeros((2, CONV_TC), F32))
        dx_ref[...] = dx.astype(dx_ref.dtype)

        @pl.when(i == 0)
        def _():
            dw_ref[...] = jnp.zeros_like(dw_ref)

        dw_ref[...] += jnp.concatenate(rows, axis=0)

    out = pl.BlockSpec((tr, CONV_TC), lambda j, i: (i, j))
    return pl.pallas_call(
        body, name=name, out_shape=(jax.ShapeDtypeStruct((T, width), BF), jax.ShapeDtypeStruct((8, width), F32)),
        grid=(width // CONV_TC, n_i),
        in_specs=[d_prev, d_cur, d_nxt, x_prev, x_cur, x_nxt, pl.BlockSpec((8, CONV_TC), lambda j, i: (0, cd + j))],
        out_specs=(out, pl.BlockSpec((8, CONV_TC), lambda j, i: (0, j))),
        compiler_params=pltpu.CompilerParams(dimension_semantics=("parallel", "arbitrary")),
    )(dpre, dpre, dpre, u, u, u, w8)


N_HB = 2 * SSM_GROUPS
P_DT, P_CS, P_E, P_W = 0, HP, 2 * HP, 3 * HP
DT_BLK = (U_SMALL + S_DT) // HP


def _tri(rev, transpose=False):
    rows = lax.broadcasted_iota(jnp.int32, (CHUNK, CHUNK), 0)
    cols = lax.broadcasted_iota(jnp.int32, (CHUNK, CHUNK), 1)
    if transpose:
        rows, cols = cols, rows
    return (cols >= rows) if rev else (cols <= rows)


def _ssd_prep(u, bias8, alog8):
    T = u.shape[0]
    nc = T // CHUNK

    def body(dt_ref, bias_ref, a_ref, cols_ref, rows_ref):
        lane = lax.broadcasted_iota(jnp.int32, (CHUNK, HP), 1)
        dt = _softplus(dt_ref[...] + bias_ref[0:1, :])
        da = dt * (-jnp.exp(a_ref[0:1, :]))
        cs_f = jnp.dot(jnp.where(_tri(False), 1.0, 0.0).astype(F32), da, precision=HI, preferred_element_type=F32)
        cs_b = jnp.dot(jnp.where(_tri(True), 1.0, 0.0).astype(F32), da, precision=HI, preferred_element_type=F32)
        cs = jnp.where(lane < SSM_HEADS, cs_f, cs_b)
        tot = jnp.where(lane[0:1] < SSM_HEADS, cs_f[CHUNK - 1:CHUNK, :], cs_b[0:1, :])
        e, w = jnp.exp(cs), jnp.exp(tot - cs)
        tot8 = jnp.broadcast_to(tot, (8, HP))
        etot8 = jnp.exp(tot8)
        for b in range(N_HB):
            down = (HP - HG * b) % HP

            def rolled(v):
                return pltpu.roll(v, down, 1) if down else v

            cols_ref[b, :, P_DT:P_DT + HP] = rolled(dt)
            cs_r = rolled(cs)
            cols_ref[b, :, P_CS:P_CS + HP] = cs_r
            cols_ref[b, :, P_E:P_E + HP] = rolled(e)
            cols_ref[b, :, P_W:P_W + HP] = rolled(w)
            rows_ref[b, 0, 0:8, :] = cs_r.T[0:8, :]
            r8 = lax.broadcasted_iota(jnp.int32, (8, HP), 0)
            rows_ref[b, 0, 8:16, :] = jnp.where(r8 == 0, rolled(tot8), jnp.where(r8 == 1, rolled(etot8), 0.0))

    vec = pl.BlockSpec((8, HP), lambda c: (0, 0))
    return pl.pallas_call(
        body, name="ssd_prep",
        out_shape=(jax.ShapeDtypeStruct((N_HB, T, 4 * HP), F32), jax.ShapeDtypeStruct((N_HB, nc, 16, HP), F32)),
        grid=(nc,), in_specs=[pl.BlockSpec((CHUNK, HP), lambda c: (c, DT_BLK)), vec, vec],
        out_specs=(pl.BlockSpec((N_HB, CHUNK, 4 * HP), lambda c: (0, c, 0)), pl.BlockSpec((N_HB, 1, 16, HP), lambda c: (0, c, 0, 0))),
    )(u, bias8, alog8)


def _ssd_specs(T, rev, bwd):
    nc = T // CHUNK
    fwd_order = (lambda c: nc - 1 - c) if rev else (lambda c: c)
    cm = (lambda c: fwd_order(nc - 1 - c)) if bwd else fwd_order
    hb0 = SSM_GROUPS if rev else 0
    xs = pl.BlockSpec((CHUNK, GW), lambda c, g: (cm(c), g))
    bs = pl.BlockSpec((CHUNK, D_STATE), lambda c, g: (cm(c), D_INNER // D_STATE + g))
    cs = pl.BlockSpec((CHUNK, D_STATE), lambda c, g: (cm(c), (D_INNER + SSM_GROUPS * D_STATE) // D_STATE + g))
    cols = pl.BlockSpec((1, CHUNK, 4 * HP), lambda c, g: (hb0 + g, cm(c), 0))
    rows = pl.BlockSpec((1, 1, 16, HP), lambda c, g: (hb0 + g, cm(c), 0, 0))
    return nc, cm, xs, bs, cs, cols, rows


def _head_lanes(to_heads):
    shape = (GW, HP) if to_heads else (HP, GW)
    wide = lax.broadcasted_iota(jnp.int32, shape, 0 if to_heads else 1)
    head = lax.broadcasted_iota(jnp.int32, shape, 1 if to_heads else 0)
    return jnp.where((wide >= PH * head) & (wide < PH * (head + 1)), 1.0, 0.0).astype(BF)


def _split_dot(v, m, terms):
    total, rest = None, v
    for _ in range(terms):
        piece = rest.astype(BF)
        part = jnp.dot(piece, m, preferred_element_type=F32)
        total = part if total is None else total + part
        rest = rest - piece.astype(F32)
    return total


def _spread_cols(cols_ref, rows_ref):
    spread = _head_lanes(False)
    dt_e = _split_dot(cols_ref[0, :, P_DT:P_DT + HP], spread, 3)
    e_e = _split_dot(cols_ref[0, :, P_E:P_E + HP], spread, 2)
    w_e = _split_dot(cols_ref[0, :, P_W:P_W + HP], spread, 2)
    etot_e = _split_dot(rows_ref[0, 0, 8:16, :], spread, 3)[1:2, :]
    return dt_e, e_e, w_e, etot_e


def _decay(cols_ref, rows_ref, hh, incl, transpose=False):
    col = cols_ref[0, :, P_CS + hh:P_CS + hh + 1]
    row = rows_ref[0, 0, hh:hh + 1, :]
    return jnp.where(incl, jnp.exp(row - col if transpose else col - row), 0.0)


def _ssd_fwd(act, cols, rows, *, rev, name):
    T = act.shape[0]
    nc, cm, xs_s, b_s, c_s, cols_s, rows_s = _ssd_specs(T, rev, False)

    def body(x_ref, b_ref, c_ref, cols_ref, rows_ref, y_ref, st_ref, state):
        c, g = pl.program_id(0), pl.program_id(1)

        @pl.when(c == 0)
        def _():
            state[g] = jnp.zeros((D_STATE, GW), F32)

        incl = _tri(rev)
        bm, cmat = b_ref[...].astype(BF), c_ref[...].astype(BF)
        bm_t = b_ref[...].T.astype(BF)
        cb = lax.dot_general(cmat, bm, NT, preferred_element_type=F32)
        dt_e, e_e, w_e, etot_e = _spread_cols(cols_ref, rows_ref)
        prev_all = state[g]
        st_ref[...] = prev_all
        xdt = x_ref[...] * dt_e
        xdt_b = xdt.astype(BF)
        yo_all = jnp.dot(cmat, prev_all.astype(BF), preferred_element_type=F32) * e_e
        state[g] = prev_all * etot_e + jnp.dot(bm_t, (xdt * w_e).astype(BF), preferred_element_type=F32)
        for hh in range(HG):
            hs = slice(PH * hh, PH * (hh + 1))
            lmat = _decay(cols_ref, rows_ref, hh, incl)
            yd = jnp.dot((cb * lmat).astype(BF), xdt_b[:, hs], preferred_element_type=F32)
            y_ref[:, hs] = yd + yo_all[:, hs]

    return pl.pallas_call(
        body, name=name,
        out_shape=(jax.ShapeDtypeStruct((T, D_INNER), F32), jax.ShapeDtypeStruct((nc * D_STATE, D_INNER), F32)),
        grid=(nc, SSM_GROUPS), in_specs=[xs_s, b_s, c_s, cols_s, rows_s], out_specs=(xs_s, xs_s),
        scratch_shapes=[pltpu.VMEM((SSM_GROUPS, D_STATE, GW), F32)],
        compiler_params=pltpu.CompilerParams(dimension_semantics=("arbitrary", "arbitrary")),
    )(act, act, act, cols, rows)


def _ssd_bwd(act, cols, rows, states, dy, *, rev, name):
    T = act.shape[0]
    nc, cm, xs_s, b_s, c_s, cols_s, rows_s = _ssd_specs(T, rev, True)

    def body(x_ref, b_ref, c_ref, cols_ref, rows_ref, st_ref, dy_ref, dx_ref, db_ref, dc_ref, dsel_ref, dtot_ref,
             dstate, dcs_cols, dcs_rows, dcb, dm_scr, dxdt_scr):
        c, g = pl.program_id(0), pl.program_id(1)

        @pl.when(c == 0)
        def _():
            dstate[g] = jnp.zeros((D_STATE, GW), F32)

        incl, incl_t = _tri(rev), _tri(rev, transpose=True)
        bm, cmat = b_ref[...].astype(BF), c_ref[...].astype(BF)
        cm_t = c_ref[...].T.astype(BF)
        cb = lax.dot_general(cmat, bm, NT, preferred_element_type=F32)
        cb_t = lax.dot_general(bm, cmat, NT, preferred_element_type=F32)
        prev_all, ds_all = st_ref[...], dstate[g]
        pb_all, dsb_all = prev_all.astype(BF), ds_all.astype(BF)
        cp_all = jnp.dot(cmat, pb_all, preferred_element_type=F32)
        bds_all = jnp.dot(bm, dsb_all, preferred_element_type=F32)
        dt_e, e_e, w_e, etot_e = _spread_cols(cols_ref, rows_ref)
        to_heads = _head_lanes(True)
        x, dy = x_ref[...], dy_ref[...]
        xdt = x * dt_e
        xdt_b, dy_b = xdt.astype(BF), dy.astype(BF)
        dye_b, xdw_b = (dy * e_e).astype(BF), (xdt * w_e).astype(BF)
        for hh in range(HG):
            hs = slice(PH * hh, PH * (hh + 1))
            mmat_t = cb_t * _decay(cols_ref, rows_ref, hh, incl_t, transpose=True)
            dm_scr[hh] = lax.dot_general(dy_b[:, hs], xdt_b[:, hs], NT, preferred_element_type=F32)
            dxdt_scr[:, hs] = jnp.dot(mmat_t.astype(BF), dy_b[:, hs], preferred_element_type=F32)
        bdsw = bds_all * w_e
        dxdt = dxdt_scr[...] + bdsw
        dx_ref[...] = dxdt * dt_e
        t = _split_dot(xdt * bdsw, to_heads, 2)
        dcs_state = _split_dot(dy * cp_all, to_heads, 2) * cols_ref[0, :, P_E:P_E + HP] - t
        dsel_ref[0, :, 0:HP] = _split_dot(dxdt * x, to_heads, 2)
        sp = _split_dot(jnp.broadcast_to(jnp.sum(ds_all * prev_all, axis=0, keepdims=True), (8, GW)), to_heads, 2)
        dtot_ref[0, 0] = jnp.sum(t, axis=0, keepdims=True) + sp * rows_ref[0, 0, 9:10, :]
        dstate[g] = ds_all * etot_e + jnp.dot(cm_t, dye_b, preferred_element_type=F32)
        dcs_cols[...] = jnp.zeros_like(dcs_cols)
        dcs_rows[...] = jnp.zeros_like(dcs_rows)
        dcb[...] = jnp.zeros_like(dcb)
        for hh in range(HG):
            lmat = _decay(cols_ref, rows_ref, hh, incl)
            dm = dm_scr[hh]
            qm = dm * (cb * lmat)
            dcs_cols[:, hh:hh + 1] = jnp.sum(qm, axis=1, keepdims=True)
            dcs_rows[hh:hh + 1, :] = jnp.sum(qm, axis=0, keepdims=True)
            dcb[...] += dm * lmat
        dcb_all = dcb[...]
        dsel_ref[0, :, HP:2 * HP] = dcs_state + dcs_cols[...] - dcs_rows[...].T
        dc_ref[...] = (lax.dot_general(dye_b, pb_all, NT, preferred_element_type=F32)
                       + jnp.dot(dcb_all.astype(BF), bm, preferred_element_type=F32))
        db_ref[...] = (lax.dot_general(xdw_b, dsb_all, NT, preferred_element_type=F32)
                       + jnp.dot(dcb_all.T.astype(BF), cmat, preferred_element_type=F32))

    bc_out = pl.BlockSpec((CHUNK, D_STATE), lambda c, g: (cm(c), g))
    return pl.pallas_call(
        body, name=name,
        out_shape=(jax.ShapeDtypeStruct((T, D_INNER), F32), jax.ShapeDtypeStruct((T, SSM_GROUPS * D_STATE), F32),
                   jax.ShapeDtypeStruct((T, SSM_GROUPS * D_STATE), F32), jax.ShapeDtypeStruct((SSM_GROUPS, T, 2 * HP), F32),
                   jax.ShapeDtypeStruct((SSM_GROUPS, nc, 8, HP), F32)),
        grid=(nc, SSM_GROUPS), in_specs=[xs_s, b_s, c_s, cols_s, rows_s, xs_s, xs_s],
        out_specs=(xs_s, bc_out, bc_out, pl.BlockSpec((1, CHUNK, 2 * HP), lambda c, g: (g, cm(c), 0)),
                   pl.BlockSpec((1, 1, 8, HP), lambda c, g: (g, cm(c), 0, 0))),
        scratch_shapes=[pltpu.VMEM((SSM_GROUPS, D_STATE, GW), F32), pltpu.VMEM((CHUNK, CHUNK), F32),
                        pltpu.VMEM((CHUNK, CHUNK), F32), pltpu.VMEM((CHUNK, CHUNK), F32),
                        pltpu.VMEM((HG, CHUNK, CHUNK), F32), pltpu.VMEM((CHUNK, GW), F32)],
        compiler_params=pltpu.CompilerParams(dimension_semantics=("arbitrary", "arbitrary")),
    )(act, act, act, cols, rows, states, dy)


def _ssd_prep_bwd(u, bias8, alog8, dsel_f, dtot_f, dsel_b, dtot_b):
    T = u.shape[0]
    nc = T // CHUNK

    def body(dt_ref, bias_ref, a_ref, sf_ref, tf_ref, sb_ref, tb_ref, ddt_ref, da_ref, dbias_ref):
        @pl.when(pl.program_id(0) == 0)
        def _():
            da_ref[...] = jnp.zeros_like(da_ref)
            dbias_ref[...] = jnp.zeros_like(dbias_ref)

        lane = lax.broadcasted_iota(jnp.int32, (CHUNK, HP), 1)
        pre = dt_ref[...] + bias_ref[0:1, :]
        dt = _softplus(pre)
        a = -jnp.exp(a_ref[0:1, :])
        ddt_x, dcs, dtot = jnp.zeros((CHUNK, HP), F32), jnp.zeros((CHUNK, HP), F32), jnp.zeros((8, HP), F32)
        for b in range(N_HB):
            s_ref, t_ref, g = (sf_ref, tf_ref, b) if b < SSM_GROUPS else (sb_ref, tb_ref, b - SSM_GROUPS)
            mine = (lane >= HG * b) & (lane < HG * (b + 1))

            def up(v):
                return pltpu.roll(v, HG * b, 1) if b else v

            ddt_x = ddt_x + jnp.where(mine, up(s_ref[g, :, 0:HP]), 0.0)
            dcs = dcs + jnp.where(mine, up(s_ref[g, :, HP:2 * HP]), 0.0)
            dtot = dtot + jnp.where(mine[0:8], up(t_ref[g, 0]), 0.0)
        tri_f = jnp.where(_tri(False, transpose=True), 1.0, 0.0).astype(F32)
        tri_b = jnp.where(_tri(True, transpose=True), 1.0, 0.0).astype(F32)
        dda = jnp.where(lane < SSM_HEADS, jnp.dot(tri_f, dcs, precision=HI, preferred_element_type=F32),
                        jnp.dot(tri_b, dcs, precision=HI, preferred_element_type=F32)) + dtot[0:1, :]
        dpre = (ddt_x + dda * a) * jax.nn.sigmoid(pre)
        ddt_ref[...] = jnp.where(lane < 2 * SSM_HEADS, dpre, 0.0)
        dbias_ref[...] += jnp.broadcast_to(jnp.sum(dpre, axis=0, keepdims=True), (8, HP))
        da_ref[...] += jnp.broadcast_to(jnp.sum(dda * dt, axis=0, keepdims=True) * a, (8, HP))

    vec = pl.BlockSpec((8, HP), lambda c: (0, 0))
    sel = pl.BlockSpec((SSM_GROUPS, CHUNK, 2 * HP), lambda c: (0, c, 0))
    tot = pl.BlockSpec((SSM_GROUPS, 1, 8, HP), lambda c: (0, c, 0, 0))
    tile = pl.BlockSpec((CHUNK, HP), lambda c: (c, 0))
    return pl.pallas_call(
        body, name="ssd_prep_bwd",
        out_shape=(jax.ShapeDtypeStruct((T, HP), F32), jax.ShapeDtypeStruct((8, HP), F32), jax.ShapeDtypeStruct((8, HP), F32)),
        grid=(nc,), in_specs=[pl.BlockSpec((CHUNK, HP), lambda c: (c, DT_BLK)), vec, vec, sel, tot, sel, tot],
        out_specs=(tile, vec, vec),
        compiler_params=pltpu.CompilerParams(dimension_semantics=("arbitrary",)),
    )(u, bias8, alog8, dsel_f, dtot_f, dsel_b, dtot_b)


def _ssm_combine_fwd(y_f, y_b, act, u, dskip, gain):
    T = y_f.shape[0]
    tt = _pick(T, 512)

    def body(yf_ref, yb_ref, x_ref, z_ref, ds_ref, g_ref, y_ref, m_ref):
        y = yf_ref[...] + yb_ref[...] + ds_ref[...] * x_ref[...]
        y2 = y * _silu(z_ref[...])
        r = lax.rsqrt(jnp.mean(y2 * y2, axis=-1, keepdims=True) + EPS)
        y_ref[...] = y
        m_ref[...] = (y2 * r * g_ref[...]).astype(m_ref.dtype)

    blk = pl.BlockSpec((tt, GW), lambda i, g: (i, g))
    vec = pl.BlockSpec((1, GW), lambda i, g: (0, g))
    return pl.pallas_call(
        body, name="ssm_combine_fwd",
        out_shape=(jax.ShapeDtypeStruct((T, D_INNER), F32), jax.ShapeDtypeStruct((T, D_INNER), BF)),
        grid=(T // tt, SSM_GROUPS), in_specs=[blk, blk, blk, blk, vec, vec], out_specs=(blk, blk),
    )(y_f, y_b, act, u, dskip, gain)


def _ssm_combine_bwd(dm, y, act, u, dskip, gain):
    T = y.shape[0]
    tt = _pick(T, 512)

    def body(dm_ref, y_ref, x_ref, z_ref, ds_ref, g_ref, dy_ref, dz_ref, dxs_ref, dg_ref, dsk_ref):
        z = z_ref[...]
        y = y_ref[...]
        x = x_ref[...]
        sz = _silu(z)
        y2 = y * sz
        r = lax.rsqrt(jnp.mean(y2 * y2, axis=-1, keepdims=True) + EPS)
        d = dm_ref[...]
        gd = d * g_ref[...]
        dy2 = r * gd - y2 * (r * r * r) * jnp.mean(gd * y2, axis=-1, keepdims=True)
        dy = dy2 * sz
        dy_ref[...] = dy
        dz_ref[...] = (dy2 * y * _dsilu(z)).astype(dz_ref.dtype)
        dxs_ref[...] = dy * ds_ref[...]

        @pl.when(pl.program_id(1) == 0)
        def _():
            dg_ref[...] = jnp.zeros_like(dg_ref)
            dsk_ref[...] = jnp.zeros_like(dsk_ref)

        dg_ref[...] += jnp.broadcast_to(jnp.sum(d * y2 * r, axis=0, keepdims=True), dg_ref.shape)
        lane_sum = jnp.broadcast_to(jnp.sum(dy * x, axis=0, keepdims=True), (8, GW))
        src = lax.broadcasted_iota(jnp.int32, (GW, HP), 0)
        head = lax.broadcasted_iota(jnp.int32, (GW, HP), 1)
        to_head = jnp.where((src >= PH * head) & (src < PH * (head + 1)), 1.0, 0.0).astype(F32)
        dsk_ref[...] += jnp.dot(lane_sum, to_head, precision=HI, preferred_element_type=F32)

    blk = pl.BlockSpec((tt, GW), lambda g, i: (i, g))
    vec = pl.BlockSpec((1, GW), lambda g, i: (0, g))
    acc = pl.BlockSpec((8, GW), lambda g, i: (0, g))
    return pl.pallas_call(
        body, name="ssm_combine_bwd",
        out_shape=(jax.ShapeDtypeStruct((T, D_INNER), F32), jax.ShapeDtypeStruct((T, D_INNER), BF),
                   jax.ShapeDtypeStruct((T, D_INNER), F32), jax.ShapeDtypeStruct((8, D_INNER), F32),
                   jax.ShapeDtypeStruct((8, SSM_GROUPS * HP), F32)),
        grid=(SSM_GROUPS, T // tt), in_specs=[blk, blk, blk, blk, vec, vec],
        out_specs=(blk, blk, blk, acc, pl.BlockSpec((8, HP), lambda g, i: (0, g))),
        compiler_params=pltpu.CompilerParams(dimension_semantics=("parallel", "arbitrary")),
    )(dm, y, act, u, dskip, gain)


def _loss_head(y, target):
    T, D = y.shape
    tt = _pick(T, 512)

    def body(y_ref, t_ref, dy_ref, dyb_ref, l_ref):
        e = y_ref[...] - t_ref[...]
        dy_ref[...] = e * (1.0 / D)
        dyb_ref[...] = (e * (1.0 / D)).astype(dyb_ref.dtype)

        @pl.when(pl.program_id(0) == 0)
        def _():
            l_ref[...] = jnp.zeros_like(l_ref)

        l_ref[...] += jnp.sum(e * e) * (0.5 / D)

    blk = pl.BlockSpec((tt, D), lambda i: (i, 0))
    return pl.pallas_call(
        body, name="loss_head",
        out_shape=(jax.ShapeDtypeStruct((T, D), F32), jax.ShapeDtypeStruct((T, D), BF), jax.ShapeDtypeStruct((8, 128), F32)),
        grid=(T // tt,), in_specs=[blk, blk], out_specs=(blk, blk, pl.BlockSpec((8, 128), lambda i: (0, 0))),
        compiler_params=pltpu.CompilerParams(dimension_semantics=("arbitrary",)),
    )(y, target)


def _adamw(w, g, m, v, *, name):
    R, C = w.shape
    cap = max(8, (1 << 18) // C)
    tr = R
    if R % 8 == 0:
        tr = 8
        for cand in range(8, min(R, cap) + 1, 8):
            if R % cand == 0:
                tr = cand

    def body(w_ref, g_ref, m_ref, v_ref, d_ref, nm_ref, nv_ref):
        gg = g_ref[...]
        nm = ADAM_B1 * m_ref[...] + (1.0 - ADAM_B1) * gg
        nv = ADAM_B2 * v_ref[...] + (1.0 - ADAM_B2) * jnp.square(gg)
        m_hat = nm / (1.0 - ADAM_B1 ** ADAM_STEP)
        v_hat = nv / (1.0 - ADAM_B2 ** ADAM_STEP)
        d_ref[...] = -ADAM_LR * (m_hat / (jnp.sqrt(v_hat) + ADAM_EPS) + ADAM_WD * w_ref[...])
        nm_ref[...] = nm
        nv_ref[...] = nv

    blk = pl.BlockSpec((tr, C), lambda i: (i, 0))
    return pl.pallas_call(
        body, name=name, out_shape=(jax.ShapeDtypeStruct((R, C), F32),) * 3, grid=(R // tr,),
        in_specs=[blk] * 4, out_specs=(blk,) * 3,
    )(w, g, m, v)


ANY = pl.BlockSpec(memory_space=pl.ANY)


def _chip_peers():
    x, y, c = lax.axis_index("x"), lax.axis_index("y"), lax.axis_index("c")
    return x, y, c, [(1 - x, y), (x, 1 - y), (1 - x, 1 - y)]


def _half_rows(c, rh):
    return pl.ds(pl.multiple_of(c * rh, 16), rh)


def _my_chip():
    return 2 * lax.axis_index("x") + lax.axis_index("y")


def _gather_chips(wb, wf):
    rh = wb.shape[0] // 2
    rq = rh // 2

    def body(wb_ref, wf_ref, ob_ref, of_ref, send_sems, recv_sems):
        x, y, c, peers = _chip_peers()
        nbr_x, nbr_y = peers[0], peers[1]
        me, chip_x, chip_y, chip_d = 2 * x + y, 2 * (1 - x) + y, 2 * x + (1 - y), 2 * (1 - x) + (1 - y)

        def quarter(core, b):
            return pl.ds(pl.multiple_of(core * rh + b * rq, 16), rq)

        ici = [(0, nbr_x, me, 0, chip_x), (1, nbr_y, me, 1, chip_y), (2, nbr_y, me, 0, chip_y), (3, nbr_x, me, 1, chip_x),
               (4, nbr_y, chip_x, 0, chip_d), (5, nbr_x, chip_y, 1, chip_d)]

        def ici_copy(k, to, slot, b, own):
            rows = quarter(c, b)
            return pltpu.make_async_remote_copy(
                src_ref=wb_ref.at[rows] if own else ob_ref.at[slot, rows], dst_ref=ob_ref.at[slot, rows],
                send_sem=send_sems.at[k], recv_sem=recv_sems.at[k], device_id=(to[0], to[1], c), device_id_type=MESH)

        def to_sibling(k, slot, b, core):
            rows = quarter(core, b)
            return pltpu.make_async_remote_copy(
                src_ref=ob_ref.at[slot, rows], dst_ref=ob_ref.at[slot, rows], send_sem=send_sems.at[6 + k],
                recv_sem=recv_sems.at[6 + k], device_id=(x, y, 1 - c), device_id_type=MESH)

        def small_copy(k, slot):
            px, py = peers[k]
            return pltpu.make_async_remote_copy(
                src_ref=wf_ref, dst_ref=of_ref.at[slot], send_sem=send_sems.at[12 + k], recv_sem=recv_sems.at[12 + k],
                device_id=(px, py, c), device_id_type=MESH)

        sends = [ici_copy(k, to, slot, b, True) for k, to, slot, b, _ in ici[:4]] + [small_copy(k, me) for k in range(3)]
        for cp in sends:
            cp.start()
        for k, to, slot, b, arrives in ici:
            ici_copy(k, to, arrives, b, False).wait_recv()
            passed = [to_sibling(k, arrives, b, c)]
            if k < 2:
                passed.append(ici_copy(*ici[4 + k][:4], False))
            for cp in passed:
                cp.start()
            sends += passed
        for k, to, slot, b, arrives in ici:
            to_sibling(k, arrives, b, 1 - c).wait_recv()
        chip_of = [chip_x, chip_y, chip_d]
        for k in range(3):
            small_copy(k, chip_of[k]).wait_recv()
        for cp in sends:
            cp.wait_send()

    ob, of = pl.pallas_call(
        body, name="gather_weights",
        out_shape=(jax.ShapeDtypeStruct((4,) + wb.shape, wb.dtype), jax.ShapeDtypeStruct((4,) + wf.shape, wf.dtype)),
        in_specs=[ANY, ANY], out_specs=(ANY, ANY),
        scratch_shapes=[pltpu.SemaphoreType.DMA((15,)), pltpu.SemaphoreType.DMA((15,))],
    )(wb, wf)
    me = _my_chip()
    return lax.dynamic_update_slice(ob, wb[None], (me, 0, 0)), lax.dynamic_update_slice(of, wf[None], (me, 0, 0))


def _halves_to_sibling(gp):
    rh = gp.shape[1] // 2

    def body(gp_ref, o_ref, send_sem, recv_sem):
        x, y, c = lax.axis_index("x"), lax.axis_index("y"), lax.axis_index("c")
        cp = pltpu.make_async_remote_copy(src_ref=gp_ref.at[:, _half_rows(1 - c, rh), :], dst_ref=o_ref, send_sem=send_sem,
                                          recv_sem=recv_sem, device_id=(x, y, 1 - c), device_id_type=MESH)
        cp.start()
        cp.wait()

    return pl.pallas_call(
        body, name="halves_to_sibling", out_shape=jax.ShapeDtypeStruct((gp.shape[0], rh, gp.shape[2]), gp.dtype),
        in_specs=[ANY], out_specs=ANY, scratch_shapes=[pltpu.SemaphoreType.DMA, pltpu.SemaphoreType.DMA],
    )(gp)


def _row_tile(rows, cap=1024):
    tr = 16
    for cand in range(16, cap + 1, 16):
        if rows % cand == 0:
            tr = cand
    return tr


def _add_halves(gp, sib, core):
    n, rh, C = sib.shape
    tr = _row_tile(rh)
    nt = rh // tr

    def body(c_ref, g_ref, s_ref, o_ref):
        o_ref[...] = (g_ref[...].astype(F32) + s_ref[...].astype(F32)).astype(o_ref.dtype)

    blk = pl.BlockSpec((1, tr, C), lambda j, i, c: (j, i, 0))
    return pl.pallas_call(
        body, name="add_halves", out_shape=jax.ShapeDtypeStruct(sib.shape, sib.dtype),
        grid_spec=pltpu.PrefetchScalarGridSpec(
            num_scalar_prefetch=1, grid=(n, nt),
            in_specs=[pl.BlockSpec((1, tr, C), lambda j, i, c: (j, c[0] * nt + i, 0)), blk], out_specs=blk),
    )(core, gp, sib)


def _join_halves(buf):
    rh = buf.shape[0] // 2

    def body(in_ref, o_ref, send_sem, recv_sem):
        x, y, c = lax.axis_index("x"), lax.axis_index("y"), lax.axis_index("c")

        def copy(rows):
            return pltpu.make_async_remote_copy(src_ref=o_ref.at[rows], dst_ref=o_ref.at[rows], send_sem=send_sem,
                                                recv_sem=recv_sem, device_id=(x, y, 1 - c), device_id_type=MESH)

        send = copy(_half_rows(c, rh))
        send.start()
        copy(_half_rows(1 - c, rh)).wait_recv()
        send.wait_send()

    return pl.pallas_call(
        body, name="join_halves", out_shape=jax.ShapeDtypeStruct(buf.shape, buf.dtype),
        in_specs=[ANY], out_specs=ANY, input_output_aliases={0: 0},
        scratch_shapes=[pltpu.SemaphoreType.DMA, pltpu.SemaphoreType.DMA],
    )(buf)


def _exchange_near(gp):
    rq = gp.shape[1] // 2

    def body(gp_ref, out_ref, send_sems, recv_sems):
        x, y, c, peers = _chip_peers()
        chip_x, chip_y, chip_d = 2 * (1 - x) + y, 2 * x + (1 - y), 2 * (1 - x) + (1 - y)
        plan = [(peers[0], chip_x, 0), (peers[0], chip_d, 0), (peers[1], chip_y, 1), (peers[1], chip_d, 1)]
        copies = [pltpu.make_async_remote_copy(
            src_ref=gp_ref.at[slot, pl.ds(b * rq, rq)], dst_ref=out_ref.at[k], send_sem=send_sems.at[k],
            recv_sem=recv_sems.at[k], device_id=(to[0], to[1], c), device_id_type=MESH) for k, (to, slot, b) in enumerate(plan)]
        for cp in copies:
            cp.start()
        for cp in copies:
            cp.wait_recv()
        for cp in copies:
            cp.wait_send()

    return pl.pallas_call(
        body, name="exchange_grads_near", out_shape=jax.ShapeDtypeStruct((4, rq, gp.shape[2]), gp.dtype),
        in_specs=[ANY], out_specs=ANY, scratch_shapes=[pltpu.SemaphoreType.DMA((4,)), pltpu.SemaphoreType.DMA((4,))],
    )(gp)


def _add_near(gp, near, chips):
    _, rq, C = near.shape
    tr = _row_tile(rq)
    nt = rq // tr

    def body(ch_ref, mine_a, mine_b, on_a, on_b, near_ref, part_ref, on_ref):
        part_ref[0] = mine_a[0].astype(F32) + near_ref[0].astype(F32)
        part_ref[1] = mine_b[0].astype(F32) + near_ref[2].astype(F32)
        on_ref[0] = (on_a[0].astype(F32) + near_ref[1].astype(F32)).astype(on_ref.dtype)
        on_ref[1] = (on_b[0].astype(F32) + near_ref[3].astype(F32)).astype(on_ref.dtype)

    def slot(which, b):
        return pl.BlockSpec((1, tr, C), lambda i, ch: (ch[which], b * nt + i, 0))

    return pl.pallas_call(
        body, name="add_near",
        out_shape=(jax.ShapeDtypeStruct((2, rq, C), F32), jax.ShapeDtypeStruct((2, rq, C), near.dtype)),
        grid_spec=pltpu.PrefetchScalarGridSpec(
            num_scalar_prefetch=1, grid=(nt,),
            in_specs=[slot(0, 0), slot(0, 1), slot(2, 0), slot(1, 1), pl.BlockSpec((4, tr, C), lambda i, ch: (0, i, 0))],
            out_specs=(pl.BlockSpec((2, tr, C), lambda i, ch: (0, i, 0)),) * 2),
    )(chips, gp, gp, gp, gp, near)


def _exchange_far(on):
    def body(on_ref, out_ref, send_sems, recv_sems):
        x, y, c, peers = _chip_peers()
        copies = [pltpu.make_async_remote_copy(
            src_ref=on_ref.at[k], dst_ref=out_ref.at[k], send_sem=send_sems.at[k], recv_sem=recv_sems.at[k],
            device_id=(to[0], to[1], c), device_id_type=MESH) for k, to in enumerate((peers[1], peers[0]))]
        for cp in copies:
            cp.start()
        for cp in copies:
            cp.wait_recv()
        for cp in copies:
            cp.wait_send()

    return pl.pallas_call(
        body, name="exchange_grads_far", out_shape=jax.ShapeDtypeStruct(on.shape, on.dtype),
        in_specs=[ANY], out_specs=ANY, scratch_shapes=[pltpu.SemaphoreType.DMA((2,)), pltpu.SemaphoreType.DMA((2,))],
    )(on)


def _add_far(part, far, core):
    _, rq, C = part.shape
    tr = _row_tile(rq)
    nt = rq // tr

    def body(c_ref, p_ref, f_ref, o_ref):
        o_ref[...] = p_ref[0] + f_ref[0].astype(F32)

    blk = pl.BlockSpec((1, tr, C), lambda b, i, c: (b, i, 0))
    return pl.pallas_call(
        body, name="add_far", out_shape=jax.ShapeDtypeStruct((4 * rq, C), F32),
        grid_spec=pltpu.PrefetchScalarGridSpec(
            num_scalar_prefetch=1, grid=(2, nt), in_specs=[blk, blk],
            out_specs=pl.BlockSpec((tr, C), lambda b, i, c: ((2 * c[0] + b) * nt + i, 0))),
    )(core, part, far)


N_DEV = 8


def _allreduce_small(p):
    rs = p.shape[0]

    def body(x_ref, sum_ref, all_ref, send_sems, recv_sems, local_sem):
        x, y, c = lax.axis_index("x"), lax.axis_index("y"), lax.axis_index("c")
        me, sibling = (x, y, c), (x, y, 1 - c)
        chips = [(1 - x, y), (x, 1 - y), (1 - x, 1 - y)]

        def rows(px, py, pc):
            return all_ref.at[pl.ds((4 * px + 2 * py + pc) * rs, rs), :]

        def copy(k, block, to, src=None):
            return pltpu.make_async_remote_copy(
                src_ref=rows(*block) if src is None else src, dst_ref=rows(*block),
                send_sem=send_sems.at[k], recv_sem=recv_sems.at[k], device_id=to, device_id_type=MESH)

        mine = pltpu.make_async_copy(x_ref, rows(*me), local_sem)
        mine.start()
        first = [copy(0, me, sibling, src=x_ref)]
        first += [copy(1 + j, me, (*chip, c), src=x_ref) for j, chip in enumerate(chips)]
        for cp in first:
            cp.start()
        passed = [copy(4 + j, (*chip, c), sibling) for j, chip in enumerate(chips)]
        for j, chip in enumerate(chips):
            copy(1 + j, (*chip, c), me).wait_recv()
            passed[j].start()
        copy(0, sibling, me).wait_recv()
        for j, chip in enumerate(chips):
            copy(4 + j, (*chip, 1 - c), me).wait_recv()
        for cp in first + passed:
            cp.wait_send()
        mine.wait()
        acc = all_ref[0:rs, :]
        for d in range(1, N_DEV):
            acc = acc + all_ref[d * rs:(d + 1) * rs, :]
        sum_ref[...] = acc

    vmem = pl.BlockSpec(memory_space=pltpu.VMEM)
    return pl.pallas_call(
        body, name="allreduce_small", out_shape=jax.ShapeDtypeStruct((rs, 128), F32),
        in_specs=[vmem], out_specs=vmem,
        scratch_shapes=[pltpu.VMEM((N_DEV * rs, 128), F32), pltpu.SemaphoreType.DMA((7,)), pltpu.SemaphoreType.DMA((7,)),
                        pltpu.SemaphoreType.DMA],
    )(p)


WEIGHTS = ('ffn1_norm', 'ffn1_w_gate', 'ffn1_w_up', 'ffn1_w_down', 'mix_norm', 'w_in', 'q_a_norm', 'w_q_b',
           'kv_a_norm', 'w_kv_b', 'q_head_norm', 'k_head_norm', 'conv_w', 'conv_b', 'a_log_fwd', 'a_log_bwd',
           'dt_bias_fwd', 'dt_bias_bwd', 'd_skip', 'ssm_norm', 'w_attn_branch', 'w_ssm_branch', 'w_out',
           'ffn2_norm', 'ffn2_w_gate', 'ffn2_w_up', 'ffn2_w_down')
PACKED = (('ffn1_w_gate', (D_MODEL, D_FF), 1), ('ffn1_w_up', (D_MODEL, D_FF), 1), ('ffn1_w_down', (D_FF, D_MODEL), 0),
          ('w_in', (D_MODEL, sum(IN_SPLITS)), 1), ('w_q_b', (Q_LORA, N_HEADS * QK_HEAD), 1),
          ('w_kv_b', (KV_LORA, N_HEADS * (QK_NOPE + V_HEAD)), 1),
          ('w_attn_branch', (N_HEADS * V_HEAD, D_MODEL), 0), ('w_ssm_branch', (D_INNER, D_MODEL), 0),
          ('w_out', (D_MODEL, D_MODEL), 0),
          ('ffn2_w_gate', (D_MODEL, D_FF), 1), ('ffn2_w_up', (D_MODEL, D_FF), 1), ('ffn2_w_down', (D_FF, D_MODEL), 0))
PACK_W = 1024
N_CHIPS = 4
SMALL = (('ffn1_norm', 1024), ('mix_norm', 1024), ('q_a_norm', 384), ('kv_a_norm', 256), ('q_head_norm', 96),
         ('k_head_norm', 96), ('conv_b', 3072), ('a_log_fwd', 32), ('a_log_bwd', 32), ('dt_bias_fwd', 32),
         ('dt_bias_bwd', 32), ('d_skip', 32), ('ssm_norm', 2048), ('ffn2_norm', 1024),
         ('conv_w', CONV_WIDTH * XBC_DIM), ('loss', 1))


TRANSPOSED = ('ffn1_w_gate', 'ffn1_w_up', 'w_in', 'ffn2_w_gate', 'ffn2_w_up')


def _stored(name, a):
    return a.T if name in TRANSPOSED else a


def _shard_shape(name, shape, axis):
    sh = tuple(s // N_CHIPS if a == axis else s for a, s in enumerate(shape))
    return sh[::-1] if name in TRANSPOSED else sh


def _by_rows(name, axis):
    return name in TRANSPOSED or axis == 0


def _pack_layout():
    out, r = {}, 0
    for name, shape, axis in PACKED:
        n = math.prod(shape) // N_CHIPS // PACK_W
        out[name] = (r, n)
        r += n
    return out, -(-r // 64) * 64


def _pack(shards):
    layout, rows = _pack_layout()
    parts = [shards[name].reshape(-1, PACK_W) for name, _, _ in PACKED]
    parts.append(jnp.zeros((rows - sum(p.shape[0] for p in parts), PACK_W), parts[0].dtype))
    return jnp.concatenate(parts, axis=0)


def _unpack(packed):
    layout, _ = _pack_layout()
    return {name: packed[layout[name][0]:layout[name][0] + layout[name][1]].reshape(_shard_shape(name, shape, axis))
            for name, shape, axis in PACKED}


def _full_from_slots(slots):
    layout, _ = _pack_layout()
    out = {}
    for name, shape, axis in PACKED:
        r, n = layout[name]
        if _by_rows(name, axis):
            out[name] = slots[:, r:r + n].reshape(N_CHIPS * n, PACK_W)
        else:
            sh = _shard_shape(name, shape, axis)
            out[name] = jnp.concatenate([slots[j, r:r + n].reshape(sh) for j in range(N_CHIPS)], axis=axis)
    return out


def _slots_from_full(full):
    layout, rows = _pack_layout()
    parts = []
    for name, shape, axis in PACKED:
        r, n = layout[name]
        if _by_rows(name, axis):
            parts.append(full[name].reshape(N_CHIPS, n, PACK_W))
        else:
            size = shape[axis] // N_CHIPS
            parts.append(jnp.stack([lax.slice_in_dim(full[name], j * size, (j + 1) * size, axis=axis).reshape(n, PACK_W)
                                    for j in range(N_CHIPS)]))
    parts.append(jnp.zeros((N_CHIPS, rows - sum(p.shape[1] for p in parts), PACK_W), parts[0].dtype))
    return jnp.concatenate(parts, axis=1)


def _pack_small(vals):
    parts = []
    for name, n in SMALL:
        pad = -(-n // 128) * 128 - n
        parts.append(jnp.pad(vals[name].reshape(-1).astype(F32), (0, pad)).reshape(-1, 128))
    rows = sum(p.shape[0] for p in parts)
    parts.append(jnp.zeros((-(-rows // 8) * 8 - rows, 128), F32))
    return jnp.concatenate(parts, axis=0)


def _unpack_small(packed):
    out, r = {}, 0
    for name, n in SMALL:
        k = -(-n // 128)
        out[name] = packed[r:r + k].reshape(-1)[:n]
        r += k
    return out


def _pad_heads(w, axis, per_head, lo, hi):
    shape = w.shape
    w = w.reshape(shape[:axis] + (N_HEADS, per_head) + shape[axis + 1:])
    w = lax.slice_in_dim(w, lo, hi, axis=axis + 1)
    pad = [(0, 0)] * w.ndim
    pad[axis + 1] = (0, HP - (hi - lo))
    w = jnp.pad(w, pad)
    return w.reshape(shape[:axis] + (N_HEADS * HP,) + shape[axis + 1:])


def _unpad_heads(w, axis, keep):
    shape = w.shape
    w = w.reshape(shape[:axis] + (N_HEADS, HP) + shape[axis + 1:])
    return lax.slice_in_dim(w, 0, keep, axis=axis + 1)


def _pad_w_in(wt):
    o = [0]
    for s in IN_SPLITS:
        o.append(o[-1] + s)
    cq, ckv, kpe, z, xbc, dtf, dtb, ga, gb = [wt[o[i]:o[i + 1]] for i in range(len(IN_SPLITS))]
    kpe_pad = jnp.pad(kpe, ((QK_NOPE, HP - QK_HEAD), (0, 0)))
    dt_pad = jnp.pad(jnp.concatenate([dtf, dtb], axis=0), ((0, HP - 2 * SSM_HEADS), (0, 0)))
    return jnp.concatenate([z, ga, gb, xbc, cq, ckv, kpe_pad, dt_pad], axis=0)


def _unpad_w_in(gt):
    z, ga, gb, xbc = gt[U_Z:U_GA], gt[U_GA:U_GB], gt[U_GB:U_XBC], gt[U_XBC:U_SMALL]
    s = gt[U_SMALL:]
    cq, ckv = s[S_CQ:S_CKV], s[S_CKV:S_KPE]
    kpe = s[S_KPE + QK_NOPE:S_KPE + QK_HEAD]
    dtf, dtb = s[S_DT:S_DT + SSM_HEADS], s[S_DT + SSM_HEADS:S_DT + 2 * SSM_HEADS]
    return jnp.concatenate([cq, ckv, kpe, z, xbc, dtf, dtb, ga, gb], axis=0)


def _lanes128(parts):
    row = jnp.concatenate([p.reshape(-1) for p in parts])
    return jnp.pad(row, (0, HP - row.shape[0])).reshape(1, HP)


FF_TILE = D_FF // 2
WGRAD = BF


def _ffn_fwd(x, g, wg_t, wu_t, wd, tag):
    h = _rms_fwd(x, g, name=tag + "_norm")
    gate, up, act = _mm([h], [wg_t, wu_t], name=tag + "_up", tb=True, out_dtypes=(BF, BF, BF), tm=512, tn=FF_TILE,
                        epilogue=lambda a, b: (a, b, _silu(a) * b))
    out = _mm([act], [wd], name=tag + "_down", extras=[x], epilogue=lambda acc, r: (r + 0.5 * acc,))
    return out, (h, gate, up, act)


def _ffn_bwd(dout, dout_bf, x, g, wg_t, wu_t, wd, saved, tag):
    h, gate, up, act = saved

    def swiglu_bwd(acc, a, b):
        a, b, half = a.astype(F32), b.astype(F32), 0.5 * acc
        s = jax.nn.sigmoid(a)
        return half * b * (s * (1.0 + a * (1.0 - s))), half * (a * s)

    dgate, dup = _mm([dout_bf], [wd], name=tag + "_down_dx", tb=True, extras=[gate, up], out_dtypes=(BF, BF),
                     tm=512, tn=FF_TILE, epilogue=swiglu_bwd)
    dwd = _mm([act], [dout_bf], name=tag + "_down_dw", ta=True, tm=FF_TILE, out_dtypes=(WGRAD,),
              epilogue=lambda acc: (0.5 * acc,))
    dwg_t, dwu_t = _mm([dgate, dup], [h, h], name=tag + "_up_dw", ta=True, separate=True, out_dtypes=(WGRAD, WGRAD),
                       tm=FF_TILE)
    dh = _mm([dgate, dup], [wg_t, wu_t], name=tag + "_up_dx")
    dx, dx_bf, dg = _rms_bwd(dh, x, g, name=tag + "_norm_bwd", add=dout, out_dtypes=(F32, BF))
    return dx, dx_bf, dg, dwg_t, dwu_t, dwd


KPE_BLK = (U_SMALL + S_KPE) // HP
SMALL_BLK = U_SMALL // SMALL_W


def _local_step(x, pos_col, target, W, P):
    T = x.shape[0]
    sig = jax.nn.sigmoid
    x1, ffn1 = _ffn_fwd(x, P["ffn1_norm"], W["wg1"], W["wu1"], W["wd1"], "ffn1")
    h = _rms_fwd(x1, P["mix_norm"], name="mix_norm")
    u = _mm([h], [W["w_in"]], name="in_proj", tb=True, tn=1152)
    cqn = _rms_fwd(u, P["q_a_norm"], name="q_a_norm", blk_w=SMALL_W, blk_idx=SMALL_BLK, off=S_CQ, width=Q_LORA)
    ckvn = _rms_fwd(u, P["kv_a_norm"], name="kv_a_norm", blk_w=SMALL_W, blk_idx=SMALL_BLK, off=S_CKV, width=KV_LORA)
    q_raw = _mm([cqn], [W["wq"]], name="q_proj")
    def with_ones_lane(acc_k, acc_v):
        lane = lax.broadcasted_iota(jnp.int32, acc_v.shape, 1)
        return acc_k, jnp.where((lane & (HP - 1)) == V_HEAD, 1.0, acc_v)

    k_raw, v = _mm([ckvn], [W["wk"], W["wv"]], name="kv_proj", out_dtypes=(F32, BF), epilogue=with_ones_lane)
    rc, rs = _rope_tables(pos_col, P["freq"])
    q = _qk_prep_fwd(q_raw, None, P["q_head_norm"], rc, rs, name="q_prep", out_scale=Q_SCALE)
    k = _qk_prep_fwd(k_raw, u, P["k_head_norm"], rc, rs, name="k_prep", kpe_blk=KPE_BLK)
    o, o_bf, lse = _attn_fwd(q, k, v)
    pre, act = _conv_fwd(u, P["conv_w8"], P["conv_b"])
    scan_cols, scan_rows = _ssd_prep(u, P["dt_bias8"], P["a_log8"])
    y_f, st_f = _ssd_fwd(act, scan_cols, scan_rows, rev=False, name="ssd_fwd_f")
    y_b, st_b = _ssd_fwd(act, scan_cols, scan_rows, rev=True, name="ssd_fwd_b")
    ysum, m = _ssm_combine_fwd(y_f, y_b, act, u, P["d_skip_lanes"], P["ssm_norm"])
    ab = _mm([o_bf], [W["pa"]], name="attn_branch")
    mb, merged = _mm([m], [W["pb"]], name="ssm_branch", extras=[ab, u, u], extra_offs=(0, U_GA, U_GB), out_dtypes=(F32, BF),
                     epilogue=lambda acc, a, ga, gb: (acc, sig(ga) * a + sig(gb) * acc))
    x2 = _mm([merged], [W["wo"]], name="out_proj", extras=[x1], epilogue=lambda acc, r: (r + acc,))
    y, ffn2 = _ffn_fwd(x2, P["ffn2_norm"], W["wg2"], W["wu2"], W["wd2"], "ffn2")
    dy, dy_bf, loss = _loss_head(y, target)
    dx2, dx2_bf, dg_ffn2, dwg2, dwu2, dwd2 = _ffn_bwd(dy, dy_bf, x2, P["ffn2_norm"], W["wg2"], W["wu2"], W["wd2"], ffn2,
                                                      "ffn2")

    def gate_bwd(dmrg, a, b, ga, gb):
        sa, sb = sig(ga), sig(gb)
        return dmrg * sa, dmrg * sb, dmrg * a * sa * (1.0 - sa), dmrg * b * sb * (1.0 - sb)

    dab, dmb, dga, dgb = _mm([dx2_bf], [W["wo"]], name="out_proj_dx", tb=True, extras=[ab, mb, u, u],
                             extra_offs=(0, 0, U_GA, U_GB), out_dtypes=(BF,) * 4, epilogue=gate_bwd)
    dwo = _mm([merged], [dx2_bf], name="out_proj_dw", ta=True, out_dtypes=(WGRAD,))
    dpa = _mm([o_bf], [dab], name="attn_branch_dw", ta=True, out_dtypes=(WGRAD,))
    do = _mm([dab], [W["pa"]], name="attn_branch_dx", tb=True)
    dpb = _mm([m], [dmb], name="ssm_branch_dw", ta=True, out_dtypes=(WGRAD,))
    dm = _mm([dmb], [W["pb"]], name="ssm_branch_dx", tb=True)
    dyssd, dz, dxs_skip, dg_ssm, dskip = _ssm_combine_bwd(dm, ysum, act, u, P["d_skip_lanes"], P["ssm_norm"])
    dxs_f, db_f, dc_f, dsel_f, dtot_f = _ssd_bwd(act, scan_cols, scan_rows, st_f, dyssd, rev=False, name="ssd_bwd_f")
    dxs_b, db_b, dc_b, dsel_b, dtot_b = _ssd_bwd(act, scan_cols, scan_rows, st_b, dyssd, rev=True, name="ssd_bwd_b")
    ddt, dalog, dbias = _ssd_prep_bwd(u, P["dt_bias8"], P["a_log8"], dsel_f, dtot_f, dsel_b, dtot_b)
    dxbc, dconv = [], []
    for tag, col0, parts in (("x", 0, [dxs_f, dxs_b, dxs_skip]), ("b", D_INNER, [db_f, db_b]),
                             ("c", D_INNER + SSM_GROUPS * D_STATE, [dc_f, dc_b])):
        dpre = _conv_dpre(parts, pre, col0, name="conv_dpre_" + tag)
        dxp, dwp = _conv_bwd(dpre, u, P["conv_w8"], col0, name="conv_bwd_" + tag)
        dxbc.append(dxp)
        dconv.append(dwp)
    dconv = jnp.concatenate(dconv, axis=1)
    dq, dk, dv = _attn_bwd(q, k, v, do, o, lse)
    dq_raw, dg_qh = _qk_prep_bwd(dq, q_raw, None, P["q_head_norm"], rc, rs, name="q_prep_bwd", in_scale=ATTN_SCALE)
    dk_raw, dg_kh, dkpe = _qk_prep_bwd(dk, k_raw, u, P["k_head_norm"], rc, rs, name="k_prep_bwd", kpe_blk=KPE_BLK,
                                       in_scale=1.0 / LOG2E)
    dwq = _mm([cqn], [dq_raw], name="q_proj_dw", ta=True, out_dtypes=(WGRAD,))
    dcqn = _mm([dq_raw], [W["wq"]], name="q_proj_dx", tb=True)
    dwk, dwv = _mm([ckvn], [dk_raw, dv], name="kv_proj_dw", ta=True, out_dtypes=(WGRAD, WGRAD))
    dckvn = _mm([dk_raw, dv], [W["wk"], W["wv"]], name="kv_proj_dx", tb=True)
    dcq, dg_qa = _rms_bwd(dcqn, u, P["q_a_norm"], name="q_a_norm_bwd", blk_w=SMALL_W, blk_idx=SMALL_BLK, off=S_CQ,
                          width=Q_LORA, out_dtypes=(BF,))
    dckv, dg_kva = _rms_bwd(dckvn, u, P["kv_a_norm"], name="kv_a_norm_bwd", blk_w=SMALL_W, blk_idx=SMALL_BLK,
                            off=S_CKV, width=KV_LORA, out_dtypes=(BF,))
    du = jnp.concatenate([dz, dga, dgb] + dxbc + [dcq, dckv, dkpe.astype(BF), ddt.astype(BF)], axis=1)
    dw_in = _mm([du], [h], name="in_proj_dw", ta=True, tm=1152, out_dtypes=(WGRAD,))
    dh = _mm([du], [W["w_in"]], name="in_proj_dx")
    dx1, dx1_bf, dg_mix = _rms_bwd(dh, x1, P["mix_norm"], name="mix_norm_bwd", add=dx2, out_dtypes=(F32, BF))
    dx, _, dg_ffn1, dwg1, dwu1, dwd1 = _ffn_bwd(dx1, dx1_bf, x, P["ffn1_norm"], W["wg1"], W["wu1"], W["wd1"], ffn1, "ffn1")
    dW = dict(wg1=dwg1, wu1=dwu1, wd1=dwd1, w_in=dw_in, wq=dwq, wk=dwk, wv=dwv, pa=dpa, pb=dpb, wo=dwo,
              wg2=dwg2, wu2=dwu2, wd2=dwd2)
    dP = dict(ffn1_norm=dg_ffn1[0], mix_norm=dg_mix[0], q_a_norm=dg_qa[0], kv_a_norm=dg_kva[0],
              q_head_norm=dg_qh[0, :QK_HEAD], k_head_norm=dg_kh[0, :QK_HEAD], conv_b=dconv[CONV_WIDTH],
              a_log_fwd=dalog[0, :SSM_HEADS], a_log_bwd=dalog[0, SSM_HEADS:2 * SSM_HEADS],
              dt_bias_fwd=dbias[0, :SSM_HEADS], dt_bias_bwd=dbias[0, SSM_HEADS:2 * SSM_HEADS],
              d_skip=dskip[0].reshape(SSM_GROUPS, HP)[:, :HG], ssm_norm=dg_ssm[0], ffn2_norm=dg_ffn2[0],
              conv_w=dconv[:CONV_WIDTH], loss=loss[0, 0])
    return dx, dW, dP


def _prepare(w, conv_w_full):
    kvb = w["w_kv_b"]
    W = dict(wg1=w["ffn1_w_gate"], wu1=w["ffn1_w_up"], wd1=w["ffn1_w_down"], w_in=_pad_w_in(w["w_in"]),
             wq=_pad_heads(w["w_q_b"], 1, QK_HEAD, 0, QK_HEAD),
             wk=_pad_heads(kvb, 1, QK_NOPE + V_HEAD, 0, QK_NOPE),
             wv=_pad_heads(kvb, 1, QK_NOPE + V_HEAD, QK_NOPE, QK_NOPE + V_HEAD),
             pa=_pad_heads(w["w_attn_branch"], 0, V_HEAD, 0, V_HEAD), pb=w["w_ssm_branch"], wo=w["w_out"],
             wg2=w["ffn2_w_gate"], wu2=w["ffn2_w_up"], wd2=w["ffn2_w_down"])
    inv_freq = [1.0 / (ROPE_BASE ** (j / QK_ROPE)) for j in range(0, QK_ROPE, 2)]
    freq = [0.0] * QK_NOPE + inv_freq + inv_freq + [0.0] * (HP - QK_HEAD)
    P = {n: w[n] for n in ("ffn1_norm", "mix_norm", "q_a_norm", "kv_a_norm", "ssm_norm", "ffn2_norm", "conv_b")}
    P.update(q_head_norm=_lanes128([w["q_head_norm"]]), k_head_norm=_lanes128([w["k_head_norm"]]),
             conv_w8=jnp.pad(conv_w_full, ((0, 8 - CONV_WIDTH), (0, 0))),
             dt_bias8=jnp.broadcast_to(_lanes128([w["dt_bias_fwd"], w["dt_bias_bwd"]]), (8, HP)),
             a_log8=jnp.broadcast_to(_lanes128([w["a_log_fwd"], w["a_log_bwd"]]), (8, HP)),
             d_skip_lanes=jnp.repeat(w["d_skip"].reshape(-1), PH).reshape(1, D_INNER),
             freq=jnp.asarray(freq, F32).reshape(1, HP))
    return W, P


def _unprepare(dW):
    dkvb = jnp.concatenate([_unpad_heads(dW["wk"], 1, QK_NOPE), _unpad_heads(dW["wv"], 1, V_HEAD)], axis=2)
    return dict(ffn1_w_gate=dW["wg1"], ffn1_w_up=dW["wu1"], ffn1_w_down=dW["wd1"], w_in=_unpad_w_in(dW["w_in"]),
                w_q_b=_unpad_heads(dW["wq"], 1, QK_HEAD).reshape(Q_LORA, N_HEADS * QK_HEAD),
                w_kv_b=dkvb.reshape(KV_LORA, N_HEADS * (QK_NOPE + V_HEAD)),
                w_attn_branch=_unpad_heads(dW["pa"], 0, V_HEAD).reshape(N_HEADS * V_HEAD, D_MODEL),
                w_ssm_branch=dW["pb"], w_out=dW["wo"],
                ffn2_w_gate=dW["wg2"], ffn2_w_up=dW["wu2"], ffn2_w_down=dW["wd2"])


def kernel(x, positions, ffn1_norm, ffn1_w_gate, ffn1_w_up, ffn1_w_down, mix_norm, w_in, q_a_norm, w_q_b, kv_a_norm, w_kv_b, q_head_norm, k_head_norm, conv_w, conv_b, a_log_fwd, a_log_bwd, dt_bias_fwd, dt_bias_bwd, d_skip, ssm_norm, w_attn_branch, w_ssm_branch, w_out, ffn2_norm, ffn2_w_gate, ffn2_w_up, ffn2_w_down, loss_target, m_ffn1_norm, m_ffn1_w_gate, m_ffn1_w_up, m_ffn1_w_down, m_mix_norm, m_w_in, m_q_a_norm, m_w_q_b, m_kv_a_norm, m_w_kv_b, m_q_head_norm, m_k_head_norm, m_conv_w, m_conv_b, m_a_log_fwd, m_a_log_bwd, m_dt_bias_fwd, m_dt_bias_bwd, m_d_skip, m_ssm_norm, m_w_attn_branch, m_w_ssm_branch, m_w_out, m_ffn2_norm, m_ffn2_w_gate, m_ffn2_w_up, m_ffn2_w_down, v_ffn1_norm, v_ffn1_w_gate, v_ffn1_w_up, v_ffn1_w_down, v_mix_norm, v_w_in, v_q_a_norm, v_w_q_b, v_kv_a_norm, v_w_kv_b, v_q_head_norm, v_k_head_norm, v_conv_w, v_conv_b, v_a_log_fwd, v_a_log_bwd, v_dt_bias_fwd, v_dt_bias_bwd, v_d_skip, v_ssm_norm, v_w_attn_branch, v_w_ssm_branch, v_w_out, v_ffn2_norm, v_ffn2_w_gate, v_ffn2_w_up, v_ffn2_w_down):
    given = dict(locals())
    T = x.shape[1]
    packed_names = [name for name, _, _ in PACKED]

    def two_d(a):
        return a.reshape(a.shape[1], -1) if a.ndim > 2 else a

    def kept(n, a):
        return _stored(n, two_d(a))

    w_loc = {n: kept(n, given[n]) for n in WEIGHTS}
    wb = _pack({n: w_loc[n].astype(BF) for n in packed_names})
    wf = jnp.pad(w_loc["conv_w"], ((0, 8 - CONV_WIDTH), (0, 0)))
    gb, gf = _gather_chips(wb, wf)
    full = _full_from_slots(gb)
    conv_w_full = jnp.concatenate([gf[j, :CONV_WIDTH] for j in range(N_CHIPS)], axis=1)
    full.update({n: w_loc[n] for n in WEIGHTS if n not in full and n != "conv_w"})
    W, P = _prepare(full, conv_w_full)
    dx, dW, dP = _local_step(x.reshape(T, D_MODEL), positions.reshape(T, 1).astype(F32), loss_target.reshape(T, D_MODEL), W, P)
    gp = _slots_from_full(_unprepare(dW))
    core = lax.axis_index("c").astype(jnp.int32).reshape(1)
    both_cores = _add_halves(gp, _halves_to_sibling(gp), core)
    cx, cy = lax.axis_index("x"), lax.axis_index("y")
    chips = jnp.stack([2 * cx + cy, 2 * (1 - cx) + cy, 2 * cx + (1 - cy)]).astype(jnp.int32)
    part, on = _add_near(both_cores, _exchange_near(both_cores), chips)
    grads = _unpack(_join_halves(_add_far(part, _exchange_far(on), core)))
    small = _unpack_small(_allreduce_small(_pack_small(dP)))
    grads.update({n: small[n].reshape(1, -1) for n, _ in SMALL if n not in ("conv_w", "loss")})
    grads["conv_w"] = lax.dynamic_slice_in_dim(small["conv_w"].reshape(CONV_WIDTH, XBC_DIM), _my_chip() * (XBC_DIM // N_CHIPS),
                                               XBC_DIM // N_CHIPS, axis=1)
    out_g, out_d, out_m, out_v = [], [], [], []
    for n in WEIGHTS:
        shape = given[n].shape
        delta, new_m, new_v = _adamw(w_loc[n], grads[n], kept(n, given["m_" + n]), kept(n, given["v_" + n]), name="adamw_" + n)
        for outs, a in ((out_g, grads[n]), (out_d, delta), (out_m, new_m), (out_v, new_v)):
            outs.append(_stored(n, a).reshape(shape))
    return (small["loss"].reshape(()), dx.reshape(x.shape), *out_g, *out_d, *out_m, *out_v)
```

```python
import functools
import math

import jax
import jax.numpy as jnp
from jax import lax
from jax.experimental import pallas as pl
from jax.experimental.pallas import tpu as pltpu

BF = jnp.bfloat16
F32 = jnp.float32
HI = lax.Precision.HIGHEST
MESH = pl.DeviceIdType.MESH

D_MODEL = 1024
D_FF = 2816
EPS = 1e-6
N_HEADS = 16
QK_NOPE = 64
QK_ROPE = 32
QK_HEAD = 96
V_HEAD = 64
Q_LORA = 384
KV_LORA = 256
ROPE_BASE = 10000.0
D_INNER = 2048
SSM_HEADS = 32
SSM_GROUPS = 4
D_STATE = 128
CONV_WIDTH = 5
CHUNK = 128
XBC_DIM = 3072
HP = 128
GW = D_INNER // SSM_GROUPS
HG = SSM_HEADS // SSM_GROUPS
PH = 64
U_Z, U_GA, U_GB, U_XBC, U_SMALL = 0, 2048, 3072, 4096, 7168
S_CQ, S_CKV, S_KPE, S_DT, SMALL_W = 0, 384, 640, 768, 896
U_PAD = U_SMALL + SMALL_W
IN_SPLITS = (Q_LORA, KV_LORA, QK_ROPE, D_INNER, XBC_DIM, SSM_HEADS, SSM_HEADS, D_MODEL, D_MODEL)

ADAM_LR = 0.001
ADAM_B1 = 0.9
ADAM_B2 = 0.999
ADAM_EPS = 1e-08
ADAM_WD = 0.01
ADAM_STEP = 10

V7X_VMEM_BYTES = 64 << 20
MM_VMEM_BUDGET = V7X_VMEM_BYTES * 5 // 8

NT = (((1,), (1,)), ((), ()))
TN = (((0,), (0,)), ((), ()))


def _pick(n, pref):
    best = None
    d = 128
    while d <= min(n, pref):
        if n % d == 0:
            best = d
        d += 128
    return best if best is not None else n


def _silu(x):
    return x * jax.nn.sigmoid(x)


def _dsilu(x):
    s = jax.nn.sigmoid(x)
    return s * (1.0 + x * (1.0 - s))


def _softplus(x):
    return jnp.maximum(x, 0.0) + jnp.log(1.0 + jnp.exp(-jnp.abs(x)))


def _mm(As, Bs, *, name, ta=False, tb=False, out_dtypes=(F32,), epilogue=None, extras=(), extra_offs=None,
        tm=1024, tn=512, tk=None, separate=False):
    As, Bs, extras = list(As), list(Bs), list(extras)
    a0, b0 = As[0], Bs[0]
    M, K = (a0.shape[1], a0.shape[0]) if ta else a0.shape
    N = b0.shape[0] if tb else b0.shape[1]
    tm, tn = _pick(M, tm), _pick(N, tn)
    n_a, n_b, n_e, n_o = len(As), len(Bs), len(extras), len(out_dtypes)
    n_res = n_b if n_a == 1 or separate else 1

    def vmem_bytes(k_tile):
        blocks = sum(tm * k_tile * a.dtype.itemsize for a in As) + sum(k_tile * tn * b.dtype.itemsize for b in Bs)
        tiles = tm * tn * (sum(jnp.dtype(dt).itemsize for dt in out_dtypes) + sum(e.dtype.itemsize for e in extras))
        return 2 * (blocks + tiles) + 2 * n_res * tm * tn * 4

    if tk is None:
        tk = K
        while vmem_bytes(tk) > MM_VMEM_BUDGET and tk > 128:
            tk = _pick(K, tk - 128)
    else:
        tk = _pick(K, tk)
    nk = K // tk
    n_acc = n_res if nk > 1 else 0
    if extra_offs is None:
        extra_offs = (0,) * n_e
    dn = (((0,) if ta else (1,), (1,) if tb else (0,)), ((), ()))
    bytes_a = sum(a.size * a.dtype.itemsize for a in As)
    bytes_b = sum(b.size * b.dtype.itemsize for b in Bs)
    n_outer = (N // tn) * bytes_a + bytes_b < (M // tm) * bytes_b + bytes_a

    def products(a_refs, b_refs):
        if n_a == 1:
            a = a_refs[0][...].astype(BF)
            return [lax.dot_general(a, b[...].astype(BF), dn, preferred_element_type=F32) for b in b_refs]
        if separate:
            return [lax.dot_general(a[...].astype(BF), b[...].astype(BF), dn, preferred_element_type=F32)
                    for a, b in zip(a_refs, b_refs)]
        total = None
        for a, b in zip(a_refs, b_refs):
            p = lax.dot_general(a[...].astype(BF), b[...].astype(BF), dn, preferred_element_type=F32)
            total = p if total is None else total + p
        return [total]

    def finish(accs, e_refs, o_refs):
        ex = [e[...] for e in e_refs]
        outs = epilogue(*accs, *ex) if epilogue is not None else tuple(accs)
        for o_ref, val in zip(o_refs, outs):
            o_ref[...] = val.astype(o_ref.dtype)

    def body(*refs):
        a_refs, b_refs = refs[:n_a], refs[n_a:n_a + n_b]
        e_refs = refs[n_a + n_b:n_a + n_b + n_e]
        o_refs = refs[n_a + n_b + n_e:n_a + n_b + n_e + n_o]
        acc_refs = refs[n_a + n_b + n_e + n_o:]
        if nk == 1:
            finish(products(a_refs, b_refs), e_refs, o_refs)
            return
        k = pl.program_id(2)

        @pl.when(k == 0)
        def _():
            for acc in acc_refs:
                acc[...] = jnp.zeros_like(acc)

        for acc, p in zip(acc_refs, products(a_refs, b_refs)):
            acc[...] += p

        @pl.when(k == nk - 1)
        def _():
            finish([acc[...] for acc in acc_refs], e_refs, o_refs)

    def at(f):
        return (lambda j, i, k: f(i, j, k)) if n_outer else f

    a_spec = pl.BlockSpec((tk, tm), at(lambda i, j, k: (k, i))) if ta else pl.BlockSpec((tm, tk), at(lambda i, j, k: (i, k)))
    b_spec = pl.BlockSpec((tn, tk), at(lambda i, j, k: (j, k))) if tb else pl.BlockSpec((tk, tn), at(lambda i, j, k: (k, j)))
    e_specs = [pl.BlockSpec((tm, tn), at(functools.partial(lambda i, j, k, o: (i, j + o), o=off // tn))) for off in extra_offs]
    for off in extra_offs:
        assert off % tn == 0
    outs = pl.pallas_call(
        body, name=name,
        out_shape=tuple(jax.ShapeDtypeStruct((M, N), dt) for dt in out_dtypes),
        grid=(N // tn, M // tm, nk) if n_outer else (M // tm, N // tn, nk),
        in_specs=[a_spec] * n_a + [b_spec] * n_b + e_specs,
        out_specs=tuple(pl.BlockSpec((tm, tn), at(lambda i, j, k: (i, j))) for _ in out_dtypes),
        scratch_shapes=[pltpu.VMEM((tm, tn), F32)] * n_acc,
        compiler_params=pltpu.CompilerParams(dimension_semantics=("parallel", "parallel", "arbitrary")),
    )(*As, *Bs, *extras)
    return outs[0] if n_o == 1 else outs


def _rms_fwd(x, g, *, name, blk_w=None, blk_idx=0, off=0, width=None, out_dtype=BF):
    T = x.shape[0]
    blk_w = x.shape[1] if blk_w is None else blk_w
    width = blk_w if width is None else width
    tt = _pick(T, 512)

    def body(x_ref, g_ref, o_ref):
        xf = x_ref[:, off:off + width]
        r = lax.rsqrt(jnp.mean(xf * xf, axis=-1, keepdims=True) + EPS)
        o_ref[...] = (xf * r * g_ref[...]).astype(o_ref.dtype)

    return pl.pallas_call(
        body, name=name, out_shape=jax.ShapeDtypeStruct((T, width), out_dtype), grid=(T // tt,),
        in_specs=[pl.BlockSpec((tt, blk_w), lambda i: (i, blk_idx)), pl.BlockSpec((1, width), lambda i: (0, 0))],
        out_specs=pl.BlockSpec((tt, width), lambda i: (i, 0)),
    )(x, g)


def _rms_bwd(dy, x, g, *, name, blk_w=None, blk_idx=0, off=0, width=None, add=None, out_dtypes=(F32,)):
    T = x.shape[0]
    blk_w = x.shape[1] if blk_w is None else blk_w
    width = blk_w if width is None else width
    tt = _pick(T, 512)
    has_add = add is not None
    n_dx = len(out_dtypes)

    def body(*refs):
        dy_ref, x_ref, g_ref = refs[:3]
        dx_refs, dg_ref = refs[3 + has_add:3 + has_add + n_dx], refs[-1]
        xf = x_ref[:, off:off + width]
        d = dy_ref[...].astype(F32)
        r = lax.rsqrt(jnp.mean(xf * xf, axis=-1, keepdims=True) + EPS)
        gd = d * g_ref[...]
        dx = r * gd - xf * (r * r * r) * jnp.mean(gd * xf, axis=-1, keepdims=True)
        if has_add:
            dx = dx + refs[3][...]
        for dx_ref in dx_refs:
            dx_ref[...] = dx.astype(dx_ref.dtype)

        @pl.when(pl.program_id(0) == 0)
        def _():
            dg_ref[...] = jnp.zeros_like(dg_ref)

        dg_ref[...] += jnp.broadcast_to(jnp.sum(d * xf * r, axis=0, keepdims=True), dg_ref.shape)

    row = pl.BlockSpec((tt, width), lambda i: (i, 0))
    in_specs = [row, pl.BlockSpec((tt, blk_w), lambda i: (i, blk_idx)), pl.BlockSpec((1, width), lambda i: (0, 0))]
    args = [dy, x, g]
    if has_add:
        in_specs.append(row)
        args.append(add)
    return pl.pallas_call(
        body, name=name,
        out_shape=tuple(jax.ShapeDtypeStruct((T, width), dt) for dt in out_dtypes) + (jax.ShapeDtypeStruct((8, width), F32),),
        grid=(T // tt,), in_specs=in_specs,
        out_specs=(row,) * n_dx + (pl.BlockSpec((8, width), lambda i: (0, 0)),),
        compiler_params=pltpu.CompilerParams(dimension_semantics=("arbitrary",)),
    )(*args)


def _rope_tables(pos_col, freq_lane):
    T = pos_col.shape[0]
    tt = _pick(T, 512)

    def body(p_ref, f_ref, c_ref, s_ref):
        ang = p_ref[...] * f_ref[...]
        lane = lax.broadcasted_iota(jnp.int32, ang.shape, 1)
        c_ref[...] = jnp.where(lane < QK_HEAD, jnp.cos(ang), 0.0)
        sn = jnp.sin(ang)
        s_ref[...] = jnp.where((lane >= QK_NOPE) & (lane < QK_NOPE + 16), -sn,
                               jnp.where((lane >= QK_NOPE + 16) & (lane < QK_HEAD), sn, 0.0))

    return pl.pallas_call(
        body, name="rope_tables", out_shape=(jax.ShapeDtypeStruct((T, HP), F32),) * 2, grid=(T // tt,),
        in_specs=[pl.BlockSpec((tt, 1), lambda i: (i, 0)), pl.BlockSpec((1, HP), lambda i: (0, 0))],
        out_specs=(pl.BlockSpec((tt, HP), lambda i: (i, 0)),) * 2,
    )(pos_col, freq_lane)


def _swap_rope_halves(n):
    src = lax.broadcasted_iota(jnp.int32, (HP, HP), 0)
    dst = lax.broadcasted_iota(jnp.int32, (HP, HP), 1)
    lo = (dst >= QK_NOPE) & (dst < QK_NOPE + 16) & (src == dst + 16)
    hi = (dst >= QK_NOPE + 16) & (dst < QK_HEAD) & (src == dst - 16)
    return _split_dot(n, jnp.where(lo | hi, 1.0, 0.0).astype(BF), 2)


def _qk_prep_fwd(raw, kpe, gain, C, S, *, name, kpe_blk=0, out_scale=1.0):
    T = raw.shape[0]
    tt = _pick(T, 256)
    has_kpe = kpe is not None

    def body(*refs):
        if has_kpe:
            raw_ref, kpe_ref, g_ref, c_ref, s_ref, o_ref = refs
        else:
            raw_ref, g_ref, c_ref, s_ref, o_ref = refs
        for h in range(N_HEADS):
            hs = slice(HP * h, HP * (h + 1))
            xr = raw_ref[:, hs] + kpe_ref[...] if has_kpe else raw_ref[:, hs]
            r = lax.rsqrt(jnp.sum(xr * xr, axis=-1, keepdims=True) * (1.0 / QK_HEAD) + EPS)
            n = xr * r * g_ref[...]
            o_ref[:, hs] = ((n * c_ref[...] + _swap_rope_halves(n) * s_ref[...]) * out_scale).astype(o_ref.dtype)

    heads = pl.BlockSpec((tt, N_HEADS * HP), lambda i: (i, 0))
    shared = pl.BlockSpec((tt, HP), lambda i: (i, 0))
    kpe_spec = pl.BlockSpec((tt, HP), lambda i: (i, kpe_blk))
    in_specs = [heads] + ([kpe_spec] if has_kpe else []) + [pl.BlockSpec((1, HP), lambda i: (0, 0)), shared, shared]
    args = [raw] + ([kpe] if has_kpe else []) + [gain, C, S]
    return pl.pallas_call(
        body, name=name, out_shape=jax.ShapeDtypeStruct(raw.shape, BF), grid=(T // tt,),
        in_specs=in_specs, out_specs=heads,
    )(*args)


def _qk_prep_bwd(dout, raw, kpe, gain, C, S, *, name, kpe_blk=0, in_scale=1.0):
    T = raw.shape[0]
    tt = _pick(T, 256)
    has_kpe = kpe is not None

    def body(*refs):
        if has_kpe:
            d_ref, raw_ref, kpe_ref, g_ref, c_ref, s_ref, dx_ref, dg_ref, dkpe_ref = refs
        else:
            d_ref, raw_ref, g_ref, c_ref, s_ref, dx_ref, dg_ref = refs
        dg = jnp.zeros((1, HP), F32)
        dkpe = jnp.zeros((tt, HP), F32)
        for h in range(N_HEADS):
            hs = slice(HP * h, HP * (h + 1))
            xr = raw_ref[:, hs] + kpe_ref[...] if has_kpe else raw_ref[:, hs]
            d = d_ref[:, hs].astype(F32) * in_scale
            r = lax.rsqrt(jnp.sum(xr * xr, axis=-1, keepdims=True) * (1.0 / QK_HEAD) + EPS)
            dn = d * c_ref[...] + _swap_rope_halves(d * s_ref[...])
            gd = dn * g_ref[...]
            dx = r * gd - xr * (r * r * r) * (jnp.sum(gd * xr, axis=-1, keepdims=True) * (1.0 / QK_HEAD))
            dx_ref[:, hs] = dx.astype(dx_ref.dtype)
            dg = dg + jnp.sum(dn * xr * r, axis=0, keepdims=True)
            dkpe = dkpe + dx

        @pl.when(pl.program_id(0) == 0)
        def _():
            dg_ref[...] = jnp.zeros_like(dg_ref)

        dg_ref[...] += jnp.broadcast_to(dg, dg_ref.shape)
        if has_kpe:
            dkpe_ref[...] = dkpe

    heads = pl.BlockSpec((tt, N_HEADS * HP), lambda i: (i, 0))
    shared = pl.BlockSpec((tt, HP), lambda i: (i, 0))
    kpe_spec = pl.BlockSpec((tt, HP), lambda i: (i, kpe_blk))
    in_specs = [heads, heads] + ([kpe_spec] if has_kpe else []) + [pl.BlockSpec((1, HP), lambda i: (0, 0)), shared, shared]
    args = [dout, raw] + ([kpe] if has_kpe else []) + [gain, C, S]
    out_shape = [jax.ShapeDtypeStruct(raw.shape, BF), jax.ShapeDtypeStruct((8, HP), F32)]
    out_specs = [heads, pl.BlockSpec((8, HP), lambda i: (0, 0))]
    if has_kpe:
        out_shape.append(jax.ShapeDtypeStruct((T, HP), F32))
        out_specs.append(shared)
    return pl.pallas_call(
        body, name=name, out_shape=tuple(out_shape), grid=(T // tt,),
        in_specs=in_specs, out_specs=tuple(out_specs),
        compiler_params=pltpu.CompilerParams(dimension_semantics=("arbitrary",)),
    )(*args)


ATTN_SCALE = 1.0 / math.sqrt(QK_HEAD)
LOG2E = 1.0 / math.log(2.0)
Q_SCALE = ATTN_SCALE * LOG2E


def _attn_fwd(q, k, v):
    T = q.shape[0]
    tq = _pick(T, 1024)
    tk = _pick(T, 1024)

    def body(q_ref, k_ref, v_ref, o_ref, ob_ref, lse_ref):
        qt = q_ref[...]
        m = o = None
        for j in range(T // tk):
            ks = slice(j * tk, (j + 1) * tk)
            s = lax.dot_general(qt, k_ref[ks, :], NT, preferred_element_type=F32)
            m_j = jnp.max(s, axis=-1, keepdims=True)
            m_new = m_j if m is None else jnp.maximum(m, m_j)
            o_j = jnp.dot(jnp.exp2(s - m_new).astype(BF), v_ref[ks, :], preferred_element_type=F32)
            o = o_j if o is None else o * jnp.exp2(m - m_new) + o_j
            m = m_new
        l = o[:, V_HEAD:V_HEAD + 1]
        o = o / l
        o_ref[...] = o
        ob_ref[...] = o.astype(ob_ref.dtype)
        lse_ref[...] = jnp.broadcast_to(m + jnp.log2(l), lse_ref.shape)

    qs = pl.BlockSpec((tq, HP), lambda h, i: (i, h))
    kv = pl.BlockSpec((T, HP), lambda h, i: (0, h))
    return pl.pallas_call(
        body, name="attn_fwd",
        out_shape=(jax.ShapeDtypeStruct(q.shape, F32), jax.ShapeDtypeStruct(q.shape, BF), jax.ShapeDtypeStruct(q.shape, F32)),
        grid=(N_HEADS, T // tq), in_specs=[qs, kv, kv], out_specs=(qs, qs, qs),
        compiler_params=pltpu.CompilerParams(dimension_semantics=("parallel", "parallel")),
    )(q, k, v)


def _attn_bwd(q, k, v, do, o, lse):
    T = q.shape[0]
    tb = _pick(T, 512)
    nb = T // tb
    tkey = _pick(T, 1024)

    def body(q_ref, k_ref, v_ref, do_ref, o_ref, lse_ref, dq_ref, dk_ref, dv_ref, delta_rows, lse_rows, dob_scr, dv_acc):
        dq_ref[...] = jnp.zeros_like(dq_ref)
        dk_ref[...] = jnp.zeros_like(dk_ref)
        lane = lax.broadcasted_iota(jnp.int32, (8, HP), 1)
        ones8 = jnp.ones((8, HP), BF)
        first8 = jnp.where(lane == 0, 1.0, 0.0).astype(BF)

        def as_rows(pick, v):
            total, rest = None, v
            for _ in range(3):
                piece = rest.astype(BF)
                part = lax.dot_general(pick, piece, NT, preferred_element_type=F32)
                total = part if total is None else total + part
                rest = rest - piece.astype(F32)
            return total

        def per_q_tile(i, carry):
            qs = pl.ds(pl.multiple_of(i * tb, tb), tb)
            doi = do_ref[qs, :]
            delta_rows[i] = as_rows(ones8, doi * o_ref[qs, :])
            lse_rows[i] = as_rows(first8, lse_ref[qs, :])
            dob_scr[qs, :] = doi.astype(BF)
            return carry

        lax.fori_loop(0, nb, per_q_tile, 0)

        def k_loop(j, carry):
            ks = pl.ds(pl.multiple_of(j * tkey, tkey), tkey)
            kj, vj = k_ref[ks, :], v_ref[ks, :]

            dv_acc[...] = jnp.zeros_like(dv_acc)

            def q_loop(i, carry_q):
                qs = pl.ds(pl.multiple_of(i * tb, tb), tb)
                qi = q_ref[qs, :]
                dob = dob_scr[qs, :]
                s_t = lax.dot_general(kj, qi, NT, preferred_element_type=F32)
                p_t = jnp.exp2(s_t - lse_rows[i, 0:1, :])
                dp_t = lax.dot_general(vj, dob, NT, preferred_element_type=F32)
                ds_t = (p_t * (dp_t - delta_rows[i, 0:1, :])).astype(BF)
                dv_acc[...] += jnp.dot(p_t.astype(BF), dob, preferred_element_type=F32)
                dk_ref[ks, :] += jnp.dot(ds_t, qi, preferred_element_type=F32)
                dq_ref[qs, :] += lax.dot_general(ds_t, kj, TN, preferred_element_type=F32)
                return carry_q

            lax.fori_loop(0, nb, q_loop, 0)
            dv_ref[ks, :] = dv_acc[...].astype(dv_ref.dtype)
            return carry

        lax.fori_loop(0, T // tkey, k_loop, 0)

    spec = pl.BlockSpec((T, HP), lambda h: (0, h))
    return pl.pallas_call(
        body, name="attn_bwd",
        out_shape=(jax.ShapeDtypeStruct(q.shape, F32), jax.ShapeDtypeStruct(q.shape, F32), jax.ShapeDtypeStruct(q.shape, BF)),
        grid=(N_HEADS,), in_specs=[spec] * 6, out_specs=(spec,) * 3,
        scratch_shapes=[pltpu.VMEM((nb, 8, tb), F32), pltpu.VMEM((nb, 8, tb), F32), pltpu.VMEM((T, HP), BF),
                        pltpu.VMEM((tkey, HP), F32)],
        compiler_params=pltpu.CompilerParams(dimension_semantics=("parallel",), vmem_limit_bytes=2 * 15 * T * HP * 2 + (8 << 20)),
    )(q, k, v, do, o, lse)


CONV_TC = 512
CONV_PAD = CONV_WIDTH // 2


def _halo_specs(tr, col_of):
    r8 = tr // 8
    cur = pl.BlockSpec((tr, CONV_TC), lambda j, i: (i, col_of(j)))
    prev = pl.BlockSpec((8, CONV_TC), lambda j, i: (jnp.maximum(i * r8 - 1, 0), col_of(j)))

    def nxt_map(j, i, n8):
        return (jnp.minimum((i + 1) * r8, n8 - 1), col_of(j))

    return cur, prev, nxt_map


def _with_halo(prev_ref, cur_ref, next_ref, i, n_i):
    prev = jnp.where(i == 0, 0.0, prev_ref[...].astype(F32))
    nxt = jnp.where(i == n_i - 1, 0.0, next_ref[...].astype(F32))
    return jnp.concatenate([prev, cur_ref[...].astype(F32), nxt], axis=0)


def _conv_fwd(u, w8, b):
    T = u.shape[0]
    tr = _pick(T, 512)
    n_i = T // tr
    c0 = U_XBC // CONV_TC
    cur, prev, nxt_map = _halo_specs(tr, lambda j: c0 + j)
    nxt = pl.BlockSpec((8, CONV_TC), functools.partial(nxt_map, n8=T // 8))

    def body(p_ref, c_ref, n_ref, w_ref, b_ref, pre_ref, act_ref):
        i = pl.program_id(1)
        full = _with_halo(p_ref, c_ref, n_ref, i, n_i)
        acc = jnp.broadcast_to(b_ref[...], (tr, CONV_TC))
        for kk in range(CONV_WIDTH):
            acc = acc + full[8 - CONV_PAD + kk:8 - CONV_PAD + kk + tr, :] * w_ref[kk:kk + 1, :]
        pre_ref[...] = acc
        act_ref[...] = _silu(acc)

    out = pl.BlockSpec((tr, CONV_TC), lambda j, i: (i, j))
    return pl.pallas_call(
        body, name="conv_fwd", out_shape=(jax.ShapeDtypeStruct((T, XBC_DIM), F32),) * 2,
        grid=(XBC_DIM // CONV_TC, n_i),
        in_specs=[prev, cur, nxt, pl.BlockSpec((8, CONV_TC), lambda j, i: (0, j)), pl.BlockSpec((1, CONV_TC), lambda j, i: (0, j))],
        out_specs=(out, out),
    )(u, u, u, w8, b)


def _conv_bwd(dacts, pre, u, w8, col0, *, name):
    T, width = dacts[0].shape
    tr = _pick(T, 512)
    n_i = T // tr
    n_d = len(dacts)
    cd = col0 // CONV_TC
    cx = (U_XBC + col0) // CONV_TC

    def halo(col_of):
        cur, prev, nxt_map = _halo_specs(tr, col_of)
        return [prev, cur, pl.BlockSpec((8, CONV_TC), functools.partial(nxt_map, n8=T // 8))]

    def body(*refs):
        d_refs, pre_refs, x_refs = refs[:3 * n_d], refs[3 * n_d:3 * n_d + 3], refs[3 * n_d + 3:3 * n_d + 6]
        w_ref, dx_ref, dw_ref = refs[3 * n_d + 6:]
        i = pl.program_id(1)
        dfull = _with_halo(*d_refs[0:3], i, n_i)
        for p in range(1, n_d):
            dfull = dfull + _with_halo(*d_refs[3 * p:3 * p + 3], i, n_i)
        dfull = dfull * _dsilu(_with_halo(*pre_refs, i, n_i))
        xfull = _with_halo(*x_refs, i, n_i)
        dcur = dfull[8:8 + tr, :]
        dx = jnp.zeros((tr, CONV_TC), F32)
        rows = []
        for kk in range(CONV_WIDTH):
            dx = dx + dfull[8 + CONV_PAD - kk:8 + CONV_PAD - kk + tr, :] * w_ref[kk:kk + 1, :]
            rows.append(jnp.sum(dcur * xfull[8 - CONV_PAD + kk:8 - CONV_PAD + kk + tr, :], axis=0, keepdims=True))
        rows.append(jnp.sum(dcur, axis=0, keepdims=True))
        rows.append(jnp.zeros((2, CONV_TC), F32))
        dx_ref[...] = dx.astype(dx_ref.dtype)

        @pl.when(i == 0)
        def _():
            dw_ref[...] = jnp.zeros_like(dw_ref)

        dw_ref[...] += jnp.concatenate(rows, axis=0)

    out = pl.BlockSpec((tr, CONV_TC), lambda j, i: (i, j))
    return pl.pallas_call(
        body, name=name, out_shape=(jax.ShapeDtypeStruct((T, width), BF), jax.ShapeDtypeStruct((8, width), F32)),
        grid=(width // CONV_TC, n_i),
        in_specs=halo(lambda j: j) * n_d + halo(lambda j: cd + j) + halo(lambda j: cx + j)
        + [pl.BlockSpec((8, CONV_TC), lambda j, i: (0, cd + j))],
        out_specs=(out, pl.BlockSpec((8, CONV_TC), lambda j, i: (0, j))),
        compiler_params=pltpu.CompilerParams(dimension_semantics=("parallel", "arbitrary")),
    )(*[d for d in dacts for _ in range(3)], pre, pre, pre, u, u, u, w8)


N_HB = 2 * SSM_GROUPS
P_DT, P_CS, P_E, P_W = 0, HP, 2 * HP, 3 * HP
DT_BLK = (U_SMALL + S_DT) // HP


def _tri(rev, transpose=False):
    rows = lax.broadcasted_iota(jnp.int32, (CHUNK, CHUNK), 0)
    cols = lax.broadcasted_iota(jnp.int32, (CHUNK, CHUNK), 1)
    if transpose:
        rows, cols = cols, rows
    return (cols >= rows) if rev else (cols <= rows)


def _ssd_prep(u, bias8, alog8):
    T = u.shape[0]
    nc = T // CHUNK

    def body(dt_ref, bias_ref, a_ref, cols_ref, rows_ref):
        lane = lax.broadcasted_iota(jnp.int32, (CHUNK, HP), 1)
        dt = _softplus(dt_ref[...] + bias_ref[0:1, :])
        da = dt * (-jnp.exp(a_ref[0:1, :]))
        cs_f = jnp.dot(jnp.where(_tri(False), 1.0, 0.0).astype(F32), da, precision=HI, preferred_element_type=F32)
        cs_b = jnp.dot(jnp.where(_tri(True), 1.0, 0.0).astype(F32), da, precision=HI, preferred_element_type=F32)
        cs = jnp.where(lane < SSM_HEADS, cs_f, cs_b)
        tot = jnp.where(lane[0:1] < SSM_HEADS, cs_f[CHUNK - 1:CHUNK, :], cs_b[0:1, :])
        e, w = jnp.exp(cs), jnp.exp(tot - cs)
        tot8 = jnp.broadcast_to(tot, (8, HP))
        etot8 = jnp.exp(tot8)
        for b in range(N_HB):
            down = (HP - HG * b) % HP

            def rolled(v):
                return pltpu.roll(v, down, 1) if down else v

            cols_ref[b, :, P_DT:P_DT + HP] = rolled(dt)
            cs_r = rolled(cs)
            cols_ref[b, :, P_CS:P_CS + HP] = cs_r
            cols_ref[b, :, P_E:P_E + HP] = rolled(e)
            cols_ref[b, :, P_W:P_W + HP] = rolled(w)
            rows_ref[b, 0, 0:8, :] = cs_r.T[0:8, :]
            r8 = lax.broadcasted_iota(jnp.int32, (8, HP), 0)
            rows_ref[b, 0, 8:16, :] = jnp.where(r8 == 0, rolled(tot8), jnp.where(r8 == 1, rolled(etot8), 0.0))

    vec = pl.BlockSpec((8, HP), lambda c: (0, 0))
    return pl.pallas_call(
        body, name="ssd_prep",
        out_shape=(jax.ShapeDtypeStruct((N_HB, T, 4 * HP), F32), jax.ShapeDtypeStruct((N_HB, nc, 16, HP), F32)),
        grid=(nc,), in_specs=[pl.BlockSpec((CHUNK, HP), lambda c: (c, DT_BLK)), vec, vec],
        out_specs=(pl.BlockSpec((N_HB, CHUNK, 4 * HP), lambda c: (0, c, 0)), pl.BlockSpec((N_HB, 1, 16, HP), lambda c: (0, c, 0, 0))),
    )(u, bias8, alog8)


def _ssd_specs(T, rev, bwd):
    nc = T // CHUNK
    fwd_order = (lambda c: nc - 1 - c) if rev else (lambda c: c)
    cm = (lambda c: fwd_order(nc - 1 - c)) if bwd else fwd_order
    hb0 = SSM_GROUPS if rev else 0
    xs = pl.BlockSpec((CHUNK, GW), lambda c, g: (cm(c), g))
    bs = pl.BlockSpec((CHUNK, D_STATE), lambda c, g: (cm(c), D_INNER // D_STATE + g))
    cs = pl.BlockSpec((CHUNK, D_STATE), lambda c, g: (cm(c), (D_INNER + SSM_GROUPS * D_STATE) // D_STATE + g))
    cols = pl.BlockSpec((1, CHUNK, 4 * HP), lambda c, g: (hb0 + g, cm(c), 0))
    rows = pl.BlockSpec((1, 1, 16, HP), lambda c, g: (hb0 + g, cm(c), 0, 0))
    return nc, cm, xs, bs, cs, cols, rows


def _head_lanes(to_heads):
    shape = (GW, HP) if to_heads else (HP, GW)
    wide = lax.broadcasted_iota(jnp.int32, shape, 0 if to_heads else 1)
    head = lax.broadcasted_iota(jnp.int32, shape, 1 if to_heads else 0)
    return jnp.where((wide >= PH * head) & (wide < PH * (head + 1)), 1.0, 0.0).astype(BF)


def _split_dot(v, m, terms):
    total, rest = None, v
    for _ in range(terms):
        piece = rest.astype(BF)
        part = jnp.dot(piece, m, preferred_element_type=F32)
        total = part if total is None else total + part
        rest = rest - piece.astype(F32)
    return total


def _spread_cols(cols_ref, rows_ref):
    spread = _head_lanes(False)
    dt_e = _split_dot(cols_ref[0, :, P_DT:P_DT + HP], spread, 3)
    e_e = _split_dot(cols_ref[0, :, P_E:P_E + HP], spread, 2)
    w_e = _split_dot(cols_ref[0, :, P_W:P_W + HP], spread, 2)
    etot_e = _split_dot(rows_ref[0, 0, 8:16, :], spread, 3)[1:2, :]
    return dt_e, e_e, w_e, etot_e


def _decay(cols_ref, rows_ref, hh, incl, transpose=False):
    col = cols_ref[0, :, P_CS + hh:P_CS + hh + 1]
    row = rows_ref[0, 0, hh:hh + 1, :]
    return jnp.where(incl, jnp.exp(row - col if transpose else col - row), 0.0)


def _ssd_fwd(act, cols, rows, *, rev, name):
    T = act.shape[0]
    nc, cm, xs_s, b_s, c_s, cols_s, rows_s = _ssd_specs(T, rev, False)

    def body(x_ref, b_ref, c_ref, cols_ref, rows_ref, y_ref, st_ref, state):
        c, g = pl.program_id(0), pl.program_id(1)

        @pl.when(c == 0)
        def _():
            state[g] = jnp.zeros((D_STATE, GW), F32)

        incl = _tri(rev)
        bm, cmat = b_ref[...].astype(BF), c_ref[...].astype(BF)
        bm_t = b_ref[...].T.astype(BF)
        cb = lax.dot_general(cmat, bm, NT, preferred_element_type=F32)
        dt_e, e_e, w_e, etot_e = _spread_cols(cols_ref, rows_ref)
        prev_all = state[g]
        st_ref[...] = prev_all
        xdt = x_ref[...] * dt_e
        xdt_b = xdt.astype(BF)
        yo_all = jnp.dot(cmat, prev_all.astype(BF), preferred_element_type=F32) * e_e
        state[g] = prev_all * etot_e + jnp.dot(bm_t, (xdt * w_e).astype(BF), preferred_element_type=F32)
        for hh in range(HG):
            hs = slice(PH * hh, PH * (hh + 1))
            lmat = _decay(cols_ref, rows_ref, hh, incl)
            yd = jnp.dot((cb * lmat).astype(BF), xdt_b[:, hs], preferred_element_type=F32)
            y_ref[:, hs] = yd + yo_all[:, hs]

    return pl.pallas_call(
        body, name=name,
        out_shape=(jax.ShapeDtypeStruct((T, D_INNER), F32), jax.ShapeDtypeStruct((nc * D_STATE, D_INNER), F32)),
        grid=(nc, SSM_GROUPS), in_specs=[xs_s, b_s, c_s, cols_s, rows_s], out_specs=(xs_s, xs_s),
        scratch_shapes=[pltpu.VMEM((SSM_GROUPS, D_STATE, GW), F32)],
        compiler_params=pltpu.CompilerParams(dimension_semantics=("arbitrary", "arbitrary")),
    )(act, act, act, cols, rows)


def _ssd_bwd(act, cols, rows, states, dy, *, rev, name):
    T = act.shape[0]
    nc, cm, xs_s, b_s, c_s, cols_s, rows_s = _ssd_specs(T, rev, True)

    def body(x_ref, b_ref, c_ref, cols_ref, rows_ref, st_ref, dy_ref, dx_ref, db_ref, dc_ref, dsel_ref, dtot_ref,
             dstate, dcs_cols, dcs_rows, dcb, dm_scr, dxdt_scr):
        c, g = pl.program_id(0), pl.program_id(1)

        @pl.when(c == 0)
        def _():
            dstate[g] = jnp.zeros((D_STATE, GW), F32)

        incl, incl_t = _tri(rev), _tri(rev, transpose=True)
        bm, cmat = b_ref[...].astype(BF), c_ref[...].astype(BF)
        cm_t = c_ref[...].T.astype(BF)
        cb = lax.dot_general(cmat, bm, NT, preferred_element_type=F32)
        cb_t = lax.dot_general(bm, cmat, NT, preferred_element_type=F32)
        prev_all, ds_all = st_ref[...], dstate[g]
        pb_all, dsb_all = prev_all.astype(BF), ds_all.astype(BF)
        cp_all = jnp.dot(cmat, pb_all, preferred_element_type=F32)
        bds_all = jnp.dot(bm, dsb_all, preferred_element_type=F32)
        dt_e, e_e, w_e, etot_e = _spread_cols(cols_ref, rows_ref)
        to_heads = _head_lanes(True)
        x, dy = x_ref[...], dy_ref[...]
        xdt = x * dt_e
        xdt_b, dy_b = xdt.astype(BF), dy.astype(BF)
        dye_b, xdw_b = (dy * e_e).astype(BF), (xdt * w_e).astype(BF)
        for hh in range(HG):
            hs = slice(PH * hh, PH * (hh + 1))
            mmat_t = cb_t * _decay(cols_ref, rows_ref, hh, incl_t, transpose=True)
            dm_scr[hh] = lax.dot_general(dy_b[:, hs], xdt_b[:, hs], NT, preferred_element_type=F32)
            dxdt_scr[:, hs] = jnp.dot(mmat_t.astype(BF), dy_b[:, hs], preferred_element_type=F32)
        bdsw = bds_all * w_e
        dxdt = dxdt_scr[...] + bdsw
        dx_ref[...] = dxdt * dt_e
        t = _split_dot(xdt * bdsw, to_heads, 2)
        dcs_state = _split_dot(dy * cp_all, to_heads, 2) * cols_ref[0, :, P_E:P_E + HP] - t
        dsel_ref[0, :, 0:HP] = _split_dot(dxdt * x, to_heads, 2)
        sp = _split_dot(jnp.broadcast_to(jnp.sum(ds_all * prev_all, axis=0, keepdims=True), (8, GW)), to_heads, 2)
        dtot_ref[0, 0] = jnp.sum(t, axis=0, keepdims=True) + sp * rows_ref[0, 0, 9:10, :]
        dstate[g] = ds_all * etot_e + jnp.dot(cm_t, dye_b, preferred_element_type=F32)
        dcs_cols[...] = jnp.zeros_like(dcs_cols)
        dcs_rows[...] = jnp.zeros_like(dcs_rows)
        dcb[...] = jnp.zeros_like(dcb)
        for hh in range(HG):
            lmat = _decay(cols_ref, rows_ref, hh, incl)
            dm = dm_scr[hh]
            qm = dm * (cb * lmat)
            dcs_cols[:, hh:hh + 1] = jnp.sum(qm, axis=1, keepdims=True)
            dcs_rows[hh:hh + 1, :] = jnp.sum(qm, axis=0, keepdims=True)
            dcb[...] += dm * lmat
        dcb_all = dcb[...]
        dsel_ref[0, :, HP:2 * HP] = dcs_state + dcs_cols[...] - dcs_rows[...].T
        dc_ref[...] = (lax.dot_general(dye_b, pb_all, NT, preferred_element_type=F32)
                       + jnp.dot(dcb_all.astype(BF), bm, preferred_element_type=F32))
        db_ref[...] = (lax.dot_general(xdw_b, dsb_all, NT, preferred_element_type=F32)
                       + jnp.dot(dcb_all.T.astype(BF), cmat, preferred_element_type=F32))

    bc_out = pl.BlockSpec((CHUNK, D_STATE), lambda c, g: (cm(c), g))
    return pl.pallas_call(
        body, name=name,
        out_shape=(jax.ShapeDtypeStruct((T, D_INNER), F32), jax.ShapeDtypeStruct((T, SSM_GROUPS * D_STATE), F32),
                   jax.ShapeDtypeStruct((T, SSM_GROUPS * D_STATE), F32), jax.ShapeDtypeStruct((SSM_GROUPS, T, 2 * HP), F32),
                   jax.ShapeDtypeStruct((SSM_GROUPS, nc, 8, HP), F32)),
        grid=(nc, SSM_GROUPS), in_specs=[xs_s, b_s, c_s, cols_s, rows_s, xs_s, xs_s],
        out_specs=(xs_s, bc_out, bc_out, pl.BlockSpec((1, CHUNK, 2 * HP), lambda c, g: (g, cm(c), 0)),
                   pl.BlockSpec((1, 1, 8, HP), lambda c, g: (g, cm(c), 0, 0))),
        scratch_shapes=[pltpu.VMEM((SSM_GROUPS, D_STATE, GW), F32), pltpu.VMEM((CHUNK, CHUNK), F32),
                        pltpu.VMEM((CHUNK, CHUNK), F32), pltpu.VMEM((CHUNK, CHUNK), F32),
                        pltpu.VMEM((HG, CHUNK, CHUNK), F32), pltpu.VMEM((CHUNK, GW), F32)],
        compiler_params=pltpu.CompilerParams(dimension_semantics=("arbitrary", "arbitrary")),
    )(act, act, act, cols, rows, states, dy)


def _ssd_prep_bwd(u, bias8, alog8, dsel_f, dtot_f, dsel_b, dtot_b):
    T = u.shape[0]
    nc = T // CHUNK

    def body(dt_ref, bias_ref, a_ref, sf_ref, tf_ref, sb_ref, tb_ref, ddt_ref, da_ref, dbias_ref):
        @pl.when(pl.program_id(0) == 0)
        def _():
            da_ref[...] = jnp.zeros_like(da_ref)
            dbias_ref[...] = jnp.zeros_like(dbias_ref)

        lane = lax.broadcasted_iota(jnp.int32, (CHUNK, HP), 1)
        pre = dt_ref[...] + bias_ref[0:1, :]
        dt = _softplus(pre)
        a = -jnp.exp(a_ref[0:1, :])
        ddt_x, dcs, dtot = jnp.zeros((CHUNK, HP), F32), jnp.zeros((CHUNK, HP), F32), jnp.zeros((8, HP), F32)
        for b in range(N_HB):
            s_ref, t_ref, g = (sf_ref, tf_ref, b) if b < SSM_GROUPS else (sb_ref, tb_ref, b - SSM_GROUPS)
            mine = (lane >= HG * b) & (lane < HG * (b + 1))

            def up(v):
                return pltpu.roll(v, HG * b, 1) if b else v

            ddt_x = ddt_x + jnp.where(mine, up(s_ref[g, :, 0:HP]), 0.0)
            dcs = dcs + jnp.where(mine, up(s_ref[g, :, HP:2 * HP]), 0.0)
            dtot = dtot + jnp.where(mine[0:8], up(t_ref[g, 0]), 0.0)
        tri_f = jnp.where(_tri(False, transpose=True), 1.0, 0.0).astype(F32)
        tri_b = jnp.where(_tri(True, transpose=True), 1.0, 0.0).astype(F32)
        dda = jnp.where(lane < SSM_HEADS, jnp.dot(tri_f, dcs, precision=HI, preferred_element_type=F32),
                        jnp.dot(tri_b, dcs, precision=HI, preferred_element_type=F32)) + dtot[0:1, :]
        dpre = (ddt_x + dda * a) * jax.nn.sigmoid(pre)
        ddt_ref[...] = jnp.where(lane < 2 * SSM_HEADS, dpre, 0.0)
        dbias_ref[...] += jnp.broadcast_to(jnp.sum(dpre, axis=0, keepdims=True), (8, HP))
        da_ref[...] += jnp.broadcast_to(jnp.sum(dda * dt, axis=0, keepdims=True) * a, (8, HP))

    vec = pl.BlockSpec((8, HP), lambda c: (0, 0))
    sel = pl.BlockSpec((SSM_GROUPS, CHUNK, 2 * HP), lambda c: (0, c, 0))
    tot = pl.BlockSpec((SSM_GROUPS, 1, 8, HP), lambda c: (0, c, 0, 0))
    tile = pl.BlockSpec((CHUNK, HP), lambda c: (c, 0))
    return pl.pallas_call(
        body, name="ssd_prep_bwd",
        out_shape=(jax.ShapeDtypeStruct((T, HP), F32), jax.ShapeDtypeStruct((8, HP), F32), jax.ShapeDtypeStruct((8, HP), F32)),
        grid=(nc,), in_specs=[pl.BlockSpec((CHUNK, HP), lambda c: (c, DT_BLK)), vec, vec, sel, tot, sel, tot],
        out_specs=(tile, vec, vec),
        compiler_params=pltpu.CompilerParams(dimension_semantics=("arbitrary",)),
    )(u, bias8, alog8, dsel_f, dtot_f, dsel_b, dtot_b)


def _ssm_combine_fwd(y_f, y_b, act, u, dskip, gain):
    T = y_f.shape[0]
    tt = _pick(T, 512)

    def body(yf_ref, yb_ref, x_ref, z_ref, ds_ref, g_ref, y_ref, m_ref):
        y = yf_ref[...] + yb_ref[...] + ds_ref[...] * x_ref[...]
        y2 = y * _silu(z_ref[...])
        r = lax.rsqrt(jnp.mean(y2 * y2, axis=-1, keepdims=True) + EPS)
        y_ref[...] = y
        m_ref[...] = (y2 * r * g_ref[...]).astype(m_ref.dtype)

    blk = pl.BlockSpec((tt, GW), lambda i, g: (i, g))
    vec = pl.BlockSpec((1, GW), lambda i, g: (0, g))
    return pl.pallas_call(
        body, name="ssm_combine_fwd",
        out_shape=(jax.ShapeDtypeStruct((T, D_INNER), F32), jax.ShapeDtypeStruct((T, D_INNER), BF)),
        grid=(T // tt, SSM_GROUPS), in_specs=[blk, blk, blk, blk, vec, vec], out_specs=(blk, blk),
    )(y_f, y_b, act, u, dskip, gain)


def _ssm_combine_bwd(dm, y, act, u, dskip, gain):
    T = y.shape[0]
    tt = _pick(T, 512)

    def body(dm_ref, y_ref, x_ref, z_ref, ds_ref, g_ref, dy_ref, dz_ref, dxs_ref, dg_ref, dsk_ref):
        z = z_ref[...]
        y = y_ref[...]
        x = x_ref[...]
        sz = _silu(z)
        y2 = y * sz
        r = lax.rsqrt(jnp.mean(y2 * y2, axis=-1, keepdims=True) + EPS)
        d = dm_ref[...]
        gd = d * g_ref[...]
        dy2 = r * gd - y2 * (r * r * r) * jnp.mean(gd * y2, axis=-1, keepdims=True)
        dy = dy2 * sz
        dy_ref[...] = dy
        dz_ref[...] = (dy2 * y * _dsilu(z)).astype(dz_ref.dtype)
        dxs_ref[...] = dy * ds_ref[...]

        @pl.when(pl.program_id(1) == 0)
        def _():
            dg_ref[...] = jnp.zeros_like(dg_ref)
            dsk_ref[...] = jnp.zeros_like(dsk_ref)

        dg_ref[...] += jnp.broadcast_to(jnp.sum(d * y2 * r, axis=0, keepdims=True), dg_ref.shape)
        lane_sum = jnp.broadcast_to(jnp.sum(dy * x, axis=0, keepdims=True), (8, GW))
        src = lax.broadcasted_iota(jnp.int32, (GW, HP), 0)
        head = lax.broadcasted_iota(jnp.int32, (GW, HP), 1)
        to_head = jnp.where((src >= PH * head) & (src < PH * (head + 1)), 1.0, 0.0).astype(F32)
        dsk_ref[...] += jnp.dot(lane_sum, to_head, precision=HI, preferred_element_type=F32)

    blk = pl.BlockSpec((tt, GW), lambda g, i: (i, g))
    vec = pl.BlockSpec((1, GW), lambda g, i: (0, g))
    acc = pl.BlockSpec((8, GW), lambda g, i: (0, g))
    return pl.pallas_call(
        body, name="ssm_combine_bwd",
        out_shape=(jax.ShapeDtypeStruct((T, D_INNER), F32), jax.ShapeDtypeStruct((T, D_INNER), BF),
                   jax.ShapeDtypeStruct((T, D_INNER), F32), jax.ShapeDtypeStruct((8, D_INNER), F32),
                   jax.ShapeDtypeStruct((8, SSM_GROUPS * HP), F32)),
        grid=(SSM_GROUPS, T // tt), in_specs=[blk, blk, blk, blk, vec, vec],
        out_specs=(blk, blk, blk, acc, pl.BlockSpec((8, HP), lambda g, i: (0, g))),
        compiler_params=pltpu.CompilerParams(dimension_semantics=("parallel", "arbitrary")),
    )(dm, y, act, u, dskip, gain)


def _loss_head(y, target):
    T, D = y.shape
    tt = _pick(T, 512)

    def body(y_ref, t_ref, dy_ref, dyb_ref, l_ref):
        e = y_ref[...] - t_ref[...]
        dy_ref[...] = e * (1.0 / D)
        dyb_ref[...] = (e * (1.0 / D)).astype(dyb_ref.dtype)

        @pl.when(pl.program_id(0) == 0)
        def _():
            l_ref[...] = jnp.zeros_like(l_ref)

        l_ref[...] += jnp.sum(e * e) * (0.5 / D)

    blk = pl.BlockSpec((tt, D), lambda i: (i, 0))
    return pl.pallas_call(
        body, name="loss_head",
        out_shape=(jax.ShapeDtypeStruct((T, D), F32), jax.ShapeDtypeStruct((T, D), BF), jax.ShapeDtypeStruct((8, 128), F32)),
        grid=(T // tt,), in_specs=[blk, blk], out_specs=(blk, blk, pl.BlockSpec((8, 128), lambda i: (0, 0))),
        compiler_params=pltpu.CompilerParams(dimension_semantics=("arbitrary",)),
    )(y, target)


def _adamw(w, g, m, v, *, name):
    R, C = w.shape
    cap = max(8, (1 << 18) // C)
    tr = R
    if R % 8 == 0:
        tr = 8
        for cand in range(8, min(R, cap) + 1, 8):
            if R % cand == 0:
                tr = cand

    def body(w_ref, g_ref, m_ref, v_ref, d_ref, nm_ref, nv_ref):
        gg = g_ref[...]
        nm = ADAM_B1 * m_ref[...] + (1.0 - ADAM_B1) * gg
        nv = ADAM_B2 * v_ref[...] + (1.0 - ADAM_B2) * jnp.square(gg)
        m_hat = nm / (1.0 - ADAM_B1 ** ADAM_STEP)
        v_hat = nv / (1.0 - ADAM_B2 ** ADAM_STEP)
        d_ref[...] = -ADAM_LR * (m_hat / (jnp.sqrt(v_hat) + ADAM_EPS) + ADAM_WD * w_ref[...])
        nm_ref[...] = nm
        nv_ref[...] = nv

    blk = pl.BlockSpec((tr, C), lambda i: (i, 0))
    return pl.pallas_call(
        body, name=name, out_shape=(jax.ShapeDtypeStruct((R, C), F32),) * 3, grid=(R // tr,),
        in_specs=[blk] * 4, out_specs=(blk,) * 3,
    )(w, g, m, v)


ANY = pl.BlockSpec(memory_space=pl.ANY)


def _chip_peers():
    x, y, c = lax.axis_index("x"), lax.axis_index("y"), lax.axis_index("c")
    return x, y, c, [(1 - x, y), (x, 1 - y), (1 - x, 1 - y)]


def _half_rows(c, rh):
    return pl.ds(pl.multiple_of(c * rh, 16), rh)


def _my_chip():
    return 2 * lax.axis_index("x") + lax.axis_index("y")


def _gather_chips(wb, wf):
    rh = wb.shape[0] // 2
    rq = rh // 2

    def body(wb_ref, wf_ref, ob_ref, of_ref, send_sems, recv_sems):
        x, y, c, peers = _chip_peers()
        nbr_x, nbr_y = peers[0], peers[1]
        me, chip_x, chip_y, chip_d = 2 * x + y, 2 * (1 - x) + y, 2 * x + (1 - y), 2 * (1 - x) + (1 - y)

        def quarter(core, b):
            return pl.ds(pl.multiple_of(core * rh + b * rq, 16), rq)

        ici = [(0, nbr_x, me, 0, chip_x), (1, nbr_y, me, 1, chip_y), (2, nbr_y, me, 0, chip_y), (3, nbr_x, me, 1, chip_x),
               (4, nbr_y, chip_x, 0, chip_d), (5, nbr_x, chip_y, 1, chip_d)]

        def ici_copy(k, to, slot, b, own):
            rows = quarter(c, b)
            return pltpu.make_async_remote_copy(
                src_ref=wb_ref.at[rows] if own else ob_ref.at[slot, rows], dst_ref=ob_ref.at[slot, rows],
                send_sem=send_sems.at[k], recv_sem=recv_sems.at[k], device_id=(to[0], to[1], c), device_id_type=MESH)

        def to_sibling(k, slot, b, core):
            rows = quarter(core, b)
            return pltpu.make_async_remote_copy(
                src_ref=ob_ref.at[slot, rows], dst_ref=ob_ref.at[slot, rows], send_sem=send_sems.at[6 + k],
                recv_sem=recv_sems.at[6 + k], device_id=(x, y, 1 - c), device_id_type=MESH)

        def small_copy(k, slot):
            px, py = peers[k]
            return pltpu.make_async_remote_copy(
                src_ref=wf_ref, dst_ref=of_ref.at[slot], send_sem=send_sems.at[12 + k], recv_sem=recv_sems.at[12 + k],
                device_id=(px, py, c), device_id_type=MESH)

        sends = [ici_copy(k, to, slot, b, True) for k, to, slot, b, _ in ici[:4]] + [small_copy(k, me) for k in range(3)]
        for cp in sends:
            cp.start()
        for k, to, slot, b, arrives in ici:
            ici_copy(k, to, arrives, b, False).wait_recv()
            passed = [to_sibling(k, arrives, b, c)]
            if k < 2:
                passed.append(ici_copy(*ici[4 + k][:4], False))
            for cp in passed:
                cp.start()
            sends += passed
        for k, to, slot, b, arrives in ici:
            to_sibling(k, arrives, b, 1 - c).wait_recv()
        chip_of = [chip_x, chip_y, chip_d]
        for k in range(3):
            small_copy(k, chip_of[k]).wait_recv()
        for cp in sends:
            cp.wait_send()

    ob, of = pl.pallas_call(
        body, name="gather_weights",
        out_shape=(jax.ShapeDtypeStruct((4,) + wb.shape, wb.dtype), jax.ShapeDtypeStruct((4,) + wf.shape, wf.dtype)),
        in_specs=[ANY, ANY], out_specs=(ANY, ANY),
        scratch_shapes=[pltpu.SemaphoreType.DMA((15,)), pltpu.SemaphoreType.DMA((15,))],
    )(wb, wf)
    me = _my_chip()
    return lax.dynamic_update_slice(ob, wb[None], (me, 0, 0)), lax.dynamic_update_slice(of, wf[None], (me, 0, 0))


def _halves_to_sibling(gp):
    rh = gp.shape[1] // 2

    def body(gp_ref, o_ref, send_sem, recv_sem):
        x, y, c = lax.axis_index("x"), lax.axis_index("y"), lax.axis_index("c")
        cp = pltpu.make_async_remote_copy(src_ref=gp_ref.at[:, _half_rows(1 - c, rh), :], dst_ref=o_ref, send_sem=send_sem,
                                          recv_sem=recv_sem, device_id=(x, y, 1 - c), device_id_type=MESH)
        cp.start()
        cp.wait()

    return pl.pallas_call(
        body, name="halves_to_sibling", out_shape=jax.ShapeDtypeStruct((gp.shape[0], rh, gp.shape[2]), gp.dtype),
        in_specs=[ANY], out_specs=ANY, scratch_shapes=[pltpu.SemaphoreType.DMA, pltpu.SemaphoreType.DMA],
    )(gp)


def _row_tile(rows, cap=1024):
    tr = 16
    for cand in range(16, cap + 1, 16):
        if rows % cand == 0:
            tr = cand
    return tr


def _add_halves(gp, sib, core):
    n, rh, C = sib.shape
    tr = _row_tile(rh)
    nt = rh // tr

    def body(c_ref, g_ref, s_ref, o_ref):
        o_ref[...] = (g_ref[...].astype(F32) + s_ref[...].astype(F32)).astype(o_ref.dtype)

    blk = pl.BlockSpec((1, tr, C), lambda j, i, c: (j, i, 0))
    return pl.pallas_call(
        body, name="add_halves", out_shape=jax.ShapeDtypeStruct(sib.shape, sib.dtype),
        grid_spec=pltpu.PrefetchScalarGridSpec(
            num_scalar_prefetch=1, grid=(n, nt),
            in_specs=[pl.BlockSpec((1, tr, C), lambda j, i, c: (j, c[0] * nt + i, 0)), blk], out_specs=blk),
    )(core, gp, sib)


def _join_halves(buf):
    rh = buf.shape[0] // 2

    def body(in_ref, o_ref, send_sem, recv_sem):
        x, y, c = lax.axis_index("x"), lax.axis_index("y"), lax.axis_index("c")

        def copy(rows):
            return pltpu.make_async_remote_copy(src_ref=o_ref.at[rows], dst_ref=o_ref.at[rows], send_sem=send_sem,
                                                recv_sem=recv_sem, device_id=(x, y, 1 - c), device_id_type=MESH)

        send = copy(_half_rows(c, rh))
        send.start()
        copy(_half_rows(1 - c, rh)).wait_recv()
        send.wait_send()

    return pl.pallas_call(
        body, name="join_halves", out_shape=jax.ShapeDtypeStruct(buf.shape, buf.dtype),
        in_specs=[ANY], out_specs=ANY, input_output_aliases={0: 0},
        scratch_shapes=[pltpu.SemaphoreType.DMA, pltpu.SemaphoreType.DMA],
    )(buf)


def _exchange_near(gp):
    rq = gp.shape[1] // 2

    def body(gp_ref, out_ref, send_sems, recv_sems):
        x, y, c, peers = _chip_peers()
        chip_x, chip_y, chip_d = 2 * (1 - x) + y, 2 * x + (1 - y), 2 * (1 - x) + (1 - y)
        plan = [(peers[0], chip_x, 0), (peers[0], chip_d, 0), (peers[1], chip_y, 1), (peers[1], chip_d, 1)]
        copies = [pltpu.make_async_remote_copy(
            src_ref=gp_ref.at[slot, pl.ds(b * rq, rq)], dst_ref=out_ref.at[k], send_sem=send_sems.at[k],
            recv_sem=recv_sems.at[k], device_id=(to[0], to[1], c), device_id_type=MESH) for k, (to, slot, b) in enumerate(plan)]
        for cp in copies:
            cp.start()
        for cp in copies:
            cp.wait_recv()
        for cp in copies:
            cp.wait_send()

    return pl.pallas_call(
        body, name="exchange_grads_near", out_shape=jax.ShapeDtypeStruct((4, rq, gp.shape[2]), gp.dtype),
        in_specs=[ANY], out_specs=ANY, scratch_shapes=[pltpu.SemaphoreType.DMA((4,)), pltpu.SemaphoreType.DMA((4,))],
    )(gp)


def _add_near(gp, near, chips):
    _, rq, C = near.shape
    tr = _row_tile(rq)
    nt = rq // tr

    def body(ch_ref, mine_a, mine_b, on_a, on_b, near_ref, part_ref, on_ref):
        part_ref[0] = mine_a[0].astype(F32) + near_ref[0].astype(F32)
        part_ref[1] = mine_b[0].astype(F32) + near_ref[2].astype(F32)
        on_ref[0] = (on_a[0].astype(F32) + near_ref[1].astype(F32)).astype(on_ref.dtype)
        on_ref[1] = (on_b[0].astype(F32) + near_ref[3].astype(F32)).astype(on_ref.dtype)

    def slot(which, b):
        return pl.BlockSpec((1, tr, C), lambda i, ch: (ch[which], b * nt + i, 0))

    return pl.pallas_call(
        body, name="add_near",
        out_shape=(jax.ShapeDtypeStruct((2, rq, C), F32), jax.ShapeDtypeStruct((2, rq, C), near.dtype)),
        grid_spec=pltpu.PrefetchScalarGridSpec(
            num_scalar_prefetch=1, grid=(nt,),
            in_specs=[slot(0, 0), slot(0, 1), slot(2, 0), slot(1, 1), pl.BlockSpec((4, tr, C), lambda i, ch: (0, i, 0))],
            out_specs=(pl.BlockSpec((2, tr, C), lambda i, ch: (0, i, 0)),) * 2),
    )(chips, gp, gp, gp, gp, near)


def _exchange_far(on):
    def body(on_ref, out_ref, send_sems, recv_sems):
        x, y, c, peers = _chip_peers()
        copies = [pltpu.make_async_remote_copy(
            src_ref=on_ref.at[k], dst_ref=out_ref.at[k], send_sem=send_sems.at[k], recv_sem=recv_sems.at[k],
            device_id=(to[0], to[1], c), device_id_type=MESH) for k, to in enumerate((peers[1], peers[0]))]
        for cp in copies:
            cp.start()
        for cp in copies:
            cp.wait_recv()
        for cp in copies:
            cp.wait_send()

    return pl.pallas_call(
        body, name="exchange_grads_far", out_shape=jax.ShapeDtypeStruct(on.shape, on.dtype),
        in_specs=[ANY], out_specs=ANY, scratch_shapes=[pltpu.SemaphoreType.DMA((2,)), pltpu.SemaphoreType.DMA((2,))],
    )(on)


def _add_far(part, far, core):
    _, rq, C = part.shape
    tr = _row_tile(rq)
    nt = rq // tr

    def body(c_ref, p_ref, f_ref, o_ref):
        o_ref[...] = p_ref[0] + f_ref[0].astype(F32)

    blk = pl.BlockSpec((1, tr, C), lambda b, i, c: (b, i, 0))
    return pl.pallas_call(
        body, name="add_far", out_shape=jax.ShapeDtypeStruct((4 * rq, C), F32),
        grid_spec=pltpu.PrefetchScalarGridSpec(
            num_scalar_prefetch=1, grid=(2, nt), in_specs=[blk, blk],
            out_specs=pl.BlockSpec((tr, C), lambda b, i, c: ((2 * c[0] + b) * nt + i, 0))),
    )(core, part, far)


N_DEV = 8


def _allreduce_small(p):
    rs = p.shape[0]

    def body(x_ref, sum_ref, all_ref, send_sems, recv_sems, local_sem):
        x, y, c = lax.axis_index("x"), lax.axis_index("y"), lax.axis_index("c")
        me, sibling = (x, y, c), (x, y, 1 - c)
        chips = [(1 - x, y), (x, 1 - y), (1 - x, 1 - y)]

        def rows(px, py, pc):
            return all_ref.at[pl.ds((4 * px + 2 * py + pc) * rs, rs), :]

        def copy(k, block, to, src=None):
            return pltpu.make_async_remote_copy(
                src_ref=rows(*block) if src is None else src, dst_ref=rows(*block),
                send_sem=send_sems.at[k], recv_sem=recv_sems.at[k], device_id=to, device_id_type=MESH)

        mine = pltpu.make_async_copy(x_ref, rows(*me), local_sem)
        mine.start()
        first = [copy(0, me, sibling, src=x_ref)]
        first += [copy(1 + j, me, (*chip, c), src=x_ref) for j, chip in enumerate(chips)]
        for cp in first:
            cp.start()
        passed = [copy(4 + j, (*chip, c), sibling) for j, chip in enumerate(chips)]
        for j, chip in enumerate(chips):
            copy(1 + j, (*chip, c), me).wait_recv()
            passed[j].start()
        copy(0, sibling, me).wait_recv()
        for j, chip in enumerate(chips):
            copy(4 + j, (*chip, 1 - c), me).wait_recv()
        for cp in first + passed:
            cp.wait_send()
        mine.wait()
        acc = all_ref[0:rs, :]
        for d in range(1, N_DEV):
            acc = acc + all_ref[d * rs:(d + 1) * rs, :]
        sum_ref[...] = acc

    vmem = pl.BlockSpec(memory_space=pltpu.VMEM)
    return pl.pallas_call(
        body, name="allreduce_small", out_shape=jax.ShapeDtypeStruct((rs, 128), F32),
        in_specs=[vmem], out_specs=vmem,
        scratch_shapes=[pltpu.VMEM((N_DEV * rs, 128), F32), pltpu.SemaphoreType.DMA((7,)), pltpu.SemaphoreType.DMA((7,)),
                        pltpu.SemaphoreType.DMA],
    )(p)


WEIGHTS = ('ffn1_norm', 'ffn1_w_gate', 'ffn1_w_up', 'ffn1_w_down', 'mix_norm', 'w_in', 'q_a_norm', 'w_q_b',
           'kv_a_norm', 'w_kv_b', 'q_head_norm', 'k_head_norm', 'conv_w', 'conv_b', 'a_log_fwd', 'a_log_bwd',
           'dt_bias_fwd', 'dt_bias_bwd', 'd_skip', 'ssm_norm', 'w_attn_branch', 'w_ssm_branch', 'w_out',
           'ffn2_norm', 'ffn2_w_gate', 'ffn2_w_up', 'ffn2_w_down')
PACKED = (('ffn1_w_gate', (D_MODEL, D_FF), 1), ('ffn1_w_up', (D_MODEL, D_FF), 1), ('ffn1_w_down', (D_FF, D_MODEL), 0),
          ('w_in', (D_MODEL, sum(IN_SPLITS)), 1), ('w_q_b', (Q_LORA, N_HEADS * QK_HEAD), 1),
          ('w_kv_b', (KV_LORA, N_HEADS * (QK_NOPE + V_HEAD)), 1),
          ('w_attn_branch', (N_HEADS * V_HEAD, D_MODEL), 0), ('w_ssm_branch', (D_INNER, D_MODEL), 0),
          ('w_out', (D_MODEL, D_MODEL), 0),
          ('ffn2_w_gate', (D_MODEL, D_FF), 1), ('ffn2_w_up', (D_MODEL, D_FF), 1), ('ffn2_w_down', (D_FF, D_MODEL), 0))
PACK_W = 1024
N_CHIPS = 4
SMALL = (('ffn1_norm', 1024), ('mix_norm', 1024), ('q_a_norm', 384), ('kv_a_norm', 256), ('q_head_norm', 96),
         ('k_head_norm', 96), ('conv_b', 3072), ('a_log_fwd', 32), ('a_log_bwd', 32), ('dt_bias_fwd', 32),
         ('dt_bias_bwd', 32), ('d_skip', 32), ('ssm_norm', 2048), ('ffn2_norm', 1024),
         ('conv_w', CONV_WIDTH * XBC_DIM), ('loss', 1))


TRANSPOSED = ('ffn1_w_gate', 'ffn1_w_up', 'w_in', 'ffn2_w_gate', 'ffn2_w_up')


def _stored(name, a):
    return a.T if name in TRANSPOSED else a


def _shard_shape(name, shape, axis):
    sh = tuple(s // N_CHIPS if a == axis else s for a, s in enumerate(shape))
    return sh[::-1] if name in TRANSPOSED else sh


def _by_rows(name, axis):
    return name in TRANSPOSED or axis == 0


def _pack_layout():
    out, r = {}, 0
    for name, shape, axis in PACKED:
        n = math.prod(shape) // N_CHIPS // PACK_W
        out[name] = (r, n)
        r += n
    return out, -(-r // 64) * 64


def _pack(shards):
    layout, rows = _pack_layout()
    parts = [shards[name].reshape(-1, PACK_W) for name, _, _ in PACKED]
    parts.append(jnp.zeros((rows - sum(p.shape[0] for p in parts), PACK_W), parts[0].dtype))
    return jnp.concatenate(parts, axis=0)


def _unpack(packed):
    layout, _ = _pack_layout()
    return {name: packed[layout[name][0]:layout[name][0] + layout[name][1]].reshape(_shard_shape(name, shape, axis))
            for name, shape, axis in PACKED}


def _full_from_slots(slots):
    layout, _ = _pack_layout()
    out = {}
    for name, shape, axis in PACKED:
        r, n = layout[name]
        if _by_rows(name, axis):
            out[name] = slots[:, r:r + n].reshape(N_CHIPS * n, PACK_W)
        else:
            sh = _shard_shape(name, shape, axis)
            out[name] = jnp.concatenate([slots[j, r:r + n].reshape(sh) for j in range(N_CHIPS)], axis=axis)
    return out


def _slots_from_full(full):
    layout, rows = _pack_layout()
    parts = []
    for name, shape, axis in PACKED:
        r, n = layout[name]
        if _by_rows(name, axis):
            parts.append(full[name].reshape(N_CHIPS, n, PACK_W))
        else:
            size = shape[axis] // N_CHIPS
            parts.append(jnp.stack([lax.slice_in_dim(full[name], j * size, (j + 1) * size, axis=axis).reshape(n, PACK_W)
                                    for j in range(N_CHIPS)]))
    parts.append(jnp.zeros((N_CHIPS, rows - sum(p.shape[1] for p in parts), PACK_W), parts[0].dtype))
    return jnp.concatenate(parts, axis=1)


def _pack_small(vals):
    parts = []
    for name, n in SMALL:
        pad = -(-n // 128) * 128 - n
        parts.append(jnp.pad(vals[name].reshape(-1).astype(F32), (0, pad)).reshape(-1, 128))
    rows = sum(p.shape[0] for p in parts)
    parts.append(jnp.zeros((-(-rows // 8) * 8 - rows, 128), F32))
    return jnp.concatenate(parts, axis=0)


def _unpack_small(packed):
    out, r = {}, 0
    for name, n in SMALL:
        k = -(-n // 128)
        out[name] = packed[r:r + k].reshape(-1)[:n]
        r += k
    return out


def _pad_heads(w, axis, per_head, lo, hi):
    shape = w.shape
    w = w.reshape(shape[:axis] + (N_HEADS, per_head) + shape[axis + 1:])
    w = lax.slice_in_dim(w, lo, hi, axis=axis + 1)
    pad = [(0, 0)] * w.ndim
    pad[axis + 1] = (0, HP - (hi - lo))
    w = jnp.pad(w, pad)
    return w.reshape(shape[:axis] + (N_HEADS * HP,) + shape[axis + 1:])


def _unpad_heads(w, axis, keep):
    shape = w.shape
    w = w.reshape(shape[:axis] + (N_HEADS, HP) + shape[axis + 1:])
    return lax.slice_in_dim(w, 0, keep, axis=axis + 1)


def _pad_w_in(wt):
    o = [0]
    for s in IN_SPLITS:
        o.append(o[-1] + s)
    cq, ckv, kpe, z, xbc, dtf, dtb, ga, gb = [wt[o[i]:o[i + 1]] for i in range(len(IN_SPLITS))]
    kpe_pad = jnp.pad(kpe, ((QK_NOPE, HP - QK_HEAD), (0, 0)))
    dt_pad = jnp.pad(jnp.concatenate([dtf, dtb], axis=0), ((0, HP - 2 * SSM_HEADS), (0, 0)))
    return jnp.concatenate([z, ga, gb, xbc, cq, ckv, kpe_pad, dt_pad], axis=0)


def _unpad_w_in(gt):
    z, ga, gb, xbc = gt[U_Z:U_GA], gt[U_GA:U_GB], gt[U_GB:U_XBC], gt[U_XBC:U_SMALL]
    s = gt[U_SMALL:]
    cq, ckv = s[S_CQ:S_CKV], s[S_CKV:S_KPE]
    kpe = s[S_KPE + QK_NOPE:S_KPE + QK_HEAD]
    dtf, dtb = s[S_DT:S_DT + SSM_HEADS], s[S_DT + SSM_HEADS:S_DT + 2 * SSM_HEADS]
    return jnp.concatenate([cq, ckv, kpe, z, xbc, dtf, dtb, ga, gb], axis=0)


def _lanes128(parts):
    row = jnp.concatenate([p.reshape(-1) for p in parts])
    return jnp.pad(row, (0, HP - row.shape[0])).reshape(1, HP)


FF_TILE = D_FF // 2
WGRAD = BF


def _ffn_fwd(x, g, wg_t, wu_t, wd, tag):
    h = _rms_fwd(x, g, name=tag + "_norm")
    gate, up, act = _mm([h], [wg_t, wu_t], name=tag + "_up", tb=True, out_dtypes=(BF, BF, BF), tm=512, tn=FF_TILE,
                        epilogue=lambda a, b: (a, b, _silu(a) * b))
    out = _mm([act], [wd], name=tag + "_down", extras=[x], epilogue=lambda acc, r: (r + 0.5 * acc,))
    return out, (h, gate, up, act)


def _ffn_bwd(dout, dout_bf, x, g, wg_t, wu_t, wd, saved, tag):
    h, gate, up, act = saved

    def swiglu_bwd(acc, a, b):
        a, b, half = a.astype(F32), b.astype(F32), 0.5 * acc
        s = jax.nn.sigmoid(a)
        return half * b * (s * (1.0 + a * (1.0 - s))), half * (a * s)

    dgate, dup = _mm([dout_bf], [wd], name=tag + "_down_dx", tb=True, extras=[gate, up], out_dtypes=(BF, BF),
                     tm=512, tn=FF_TILE, epilogue=swiglu_bwd)
    dwd = _mm([act], [dout_bf], name=tag + "_down_dw", ta=True, tm=FF_TILE, out_dtypes=(WGRAD,),
              epilogue=lambda acc: (0.5 * acc,))
    dwg_t, dwu_t = _mm([dgate, dup], [h, h], name=tag + "_up_dw", ta=True, separate=True, out_dtypes=(WGRAD, WGRAD),
                       tm=FF_TILE)
    dh = _mm([dgate, dup], [wg_t, wu_t], name=tag + "_up_dx")
    dx, dx_bf, dg = _rms_bwd(dh, x, g, name=tag + "_norm_bwd", add=dout, out_dtypes=(F32, BF))
    return dx, dx_bf, dg, dwg_t, dwu_t, dwd


KPE_BLK = (U_SMALL + S_KPE) // HP
SMALL_BLK = U_SMALL // SMALL_W


def _local_step(x, pos_col, target, W, P):
    T = x.shape[0]
    sig = jax.nn.sigmoid
    x1, ffn1 = _ffn_fwd(x, P["ffn1_norm"], W["wg1"], W["wu1"], W["wd1"], "ffn1")
    h = _rms_fwd(x1, P["mix_norm"], name="mix_norm")
    u = _mm([h], [W["w_in"]], name="in_proj", tb=True, tn=1152)
    cqn = _rms_fwd(u, P["q_a_norm"], name="q_a_norm", blk_w=SMALL_W, blk_idx=SMALL_BLK, off=S_CQ, width=Q_LORA)
    ckvn = _rms_fwd(u, P["kv_a_norm"], name="kv_a_norm", blk_w=SMALL_W, blk_idx=SMALL_BLK, off=S_CKV, width=KV_LORA)
    q_raw = _mm([cqn], [W["wq"]], name="q_proj")
    def with_ones_lane(acc_k, acc_v):
        lane = lax.broadcasted_iota(jnp.int32, acc_v.shape, 1)
        return acc_k, jnp.where((lane & (HP - 1)) == V_HEAD, 1.0, acc_v)

    k_raw, v = _mm([ckvn], [W["wk"], W["wv"]], name="kv_proj", out_dtypes=(F32, BF), epilogue=with_ones_lane)
    rc, rs = _rope_tables(pos_col, P["freq"])
    q = _qk_prep_fwd(q_raw, None, P["q_head_norm"], rc, rs, name="q_prep", out_scale=Q_SCALE)
    k = _qk_prep_fwd(k_raw, u, P["k_head_norm"], rc, rs, name="k_prep", kpe_blk=KPE_BLK)
    o, o_bf, lse = _attn_fwd(q, k, v)
    pre, act = _conv_fwd(u, P["conv_w8"], P["conv_b"])
    scan_cols, scan_rows = _ssd_prep(u, P["dt_bias8"], P["a_log8"])
    y_f, st_f = _ssd_fwd(act, scan_cols, scan_rows, rev=False, name="ssd_fwd_f")
    y_b, st_b = _ssd_fwd(act, scan_cols, scan_rows, rev=True, name="ssd_fwd_b")
    ysum, m = _ssm_combine_fwd(y_f, y_b, act, u, P["d_skip_lanes"], P["ssm_norm"])
    ab = _mm([o_bf], [W["pa"]], name="attn_branch")
    mb, merged = _mm([m], [W["pb"]], name="ssm_branch", extras=[ab, u, u], extra_offs=(0, U_GA, U_GB), out_dtypes=(F32, BF),
                     epilogue=lambda acc, a, ga, gb: (acc, sig(ga) * a + sig(gb) * acc))
    x2 = _mm([merged], [W["wo"]], name="out_proj", extras=[x1], epilogue=lambda acc, r: (r + acc,))
    y, ffn2 = _ffn_fwd(x2, P["ffn2_norm"], W["wg2"], W["wu2"], W["wd2"], "ffn2")
    dy, dy_bf, loss = _loss_head(y, target)
    dx2, dx2_bf, dg_ffn2, dwg2, dwu2, dwd2 = _ffn_bwd(dy, dy_bf, x2, P["ffn2_norm"], W["wg2"], W["wu2"], W["wd2"], ffn2,
                                                      "ffn2")

    def gate_bwd(dmrg, a, b, ga, gb):
        sa, sb = sig(ga), sig(gb)
        return dmrg * sa, dmrg * sb, dmrg * a * sa * (1.0 - sa), dmrg * b * sb * (1.0 - sb)

    dab, dmb, dga, dgb = _mm([dx2_bf], [W["wo"]], name="out_proj_dx", tb=True, extras=[ab, mb, u, u],
                             extra_offs=(0, 0, U_GA, U_GB), out_dtypes=(BF,) * 4, epilogue=gate_bwd)
    dwo = _mm([merged], [dx2_bf], name="out_proj_dw", ta=True, out_dtypes=(WGRAD,))
    dpa = _mm([o_bf], [dab], name="attn_branch_dw", ta=True, out_dtypes=(WGRAD,))
    do = _mm([dab], [W["pa"]], name="attn_branch_dx", tb=True)
    dpb = _mm([m], [dmb], name="ssm_branch_dw", ta=True, out_dtypes=(WGRAD,))
    dm = _mm([dmb], [W["pb"]], name="ssm_branch_dx", tb=True)
    dyssd, dz, dxs_skip, dg_ssm, dskip = _ssm_combine_bwd(dm, ysum, act, u, P["d_skip_lanes"], P["ssm_norm"])
    dxs_f, db_f, dc_f, dsel_f, dtot_f = _ssd_bwd(act, scan_cols, scan_rows, st_f, dyssd, rev=False, name="ssd_bwd_f")
    dxs_b, db_b, dc_b, dsel_b, dtot_b = _ssd_bwd(act, scan_cols, scan_rows, st_b, dyssd, rev=True, name="ssd_bwd_b")
    ddt, dalog, dbias = _ssd_prep_bwd(u, P["dt_bias8"], P["a_log8"], dsel_f, dtot_f, dsel_b, dtot_b)
    dxbc, dconv = [], []
    for tag, col0, parts in (("x", 0, [dxs_f, dxs_b, dxs_skip]), ("b", D_INNER, [db_f, db_b]),
                             ("c", D_INNER + SSM_GROUPS * D_STATE, [dc_f, dc_b])):
        dxp, dwp = _conv_bwd(parts, pre, u, P["conv_w8"], col0, name="conv_bwd_" + tag)
        dxbc.append(dxp)
        dconv.append(dwp)
    dconv = jnp.concatenate(dconv, axis=1)
    dq, dk, dv = _attn_bwd(q, k, v, do, o, lse)
    dq_raw, dg_qh = _qk_prep_bwd(dq, q_raw, None, P["q_head_norm"], rc, rs, name="q_prep_bwd", in_scale=ATTN_SCALE)
    dk_raw, dg_kh, dkpe = _qk_prep_bwd(dk, k_raw, u, P["k_head_norm"], rc, rs, name="k_prep_bwd", kpe_blk=KPE_BLK,
                                       in_scale=1.0 / LOG2E)
    dwq = _mm([cqn], [dq_raw], name="q_proj_dw", ta=True, out_dtypes=(WGRAD,))
    dcqn = _mm([dq_raw], [W["wq"]], name="q_proj_dx", tb=True)
    dwk, dwv = _mm([ckvn], [dk_raw, dv], name="kv_proj_dw", ta=True, out_dtypes=(WGRAD, WGRAD))
    dckvn = _mm([dk_raw, dv], [W["wk"], W["wv"]], name="kv_proj_dx", tb=True)
    dcq, dg_qa = _rms_bwd(dcqn, u, P["q_a_norm"], name="q_a_norm_bwd", blk_w=SMALL_W, blk_idx=SMALL_BLK, off=S_CQ,
                          width=Q_LORA, out_dtypes=(BF,))
    dckv, dg_kva = _rms_bwd(dckvn, u, P["kv_a_norm"], name="kv_a_norm_bwd", blk_w=SMALL_W, blk_idx=SMALL_BLK,
                            off=S_CKV, width=KV_LORA, out_dtypes=(BF,))
    du = jnp.concatenate([dz, dga, dgb] + dxbc + [dcq, dckv, dkpe.astype(BF), ddt.astype(BF)], axis=1)
    dw_in = _mm([du], [h], name="in_proj_dw", ta=True, tm=1152, out_dtypes=(WGRAD,))
    dh = _mm([du], [W["w_in"]], name="in_proj_dx")
    dx1, dx1_bf, dg_mix = _rms_bwd(dh, x1, P["mix_norm"], name="mix_norm_bwd", add=dx2, out_dtypes=(F32, BF))
    dx, _, dg_ffn1, dwg1, dwu1, dwd1 = _ffn_bwd(dx1, dx1_bf, x, P["ffn1_norm"], W["wg1"], W["wu1"], W["wd1"], ffn1, "ffn1")
    dW = dict(wg1=dwg1, wu1=dwu1, wd1=dwd1, w_in=dw_in, wq=dwq, wk=dwk, wv=dwv, pa=dpa, pb=dpb, wo=dwo,
              wg2=dwg2, wu2=dwu2, wd2=dwd2)
    dP = dict(ffn1_norm=dg_ffn1[0], mix_norm=dg_mix[0], q_a_norm=dg_qa[0], kv_a_norm=dg_kva[0],
              q_head_norm=dg_qh[0, :QK_HEAD], k_head_norm=dg_kh[0, :QK_HEAD], conv_b=dconv[CONV_WIDTH],
              a_log_fwd=dalog[0, :SSM_HEADS], a_log_bwd=dalog[0, SSM_HEADS:2 * SSM_HEADS],
              dt_bias_fwd=dbias[0, :SSM_HEADS], dt_bias_bwd=dbias[0, SSM_HEADS:2 * SSM_HEADS],
              d_skip=dskip[0].reshape(SSM_GROUPS, HP)[:, :HG], ssm_norm=dg_ssm[0], ffn2_norm=dg_ffn2[0],
              conv_w=dconv[:CONV_WIDTH], loss=loss[0, 0])
    return dx, dW, dP


def _prepare(w, conv_w_full):
    kvb = w["w_kv_b"]
    W = dict(wg1=w["ffn1_w_gate"], wu1=w["ffn1_w_up"], wd1=w["ffn1_w_down"], w_in=_pad_w_in(w["w_in"]),
             wq=_pad_heads(w["w_q_b"], 1, QK_HEAD, 0, QK_HEAD),
             wk=_pad_heads(kvb, 1, QK_NOPE + V_HEAD, 0, QK_NOPE),
             wv=_pad_heads(kvb, 1, QK_NOPE + V_HEAD, QK_NOPE, QK_NOPE + V_HEAD),
             pa=_pad_heads(w["w_attn_branch"], 0, V_HEAD, 0, V_HEAD), pb=w["w_ssm_branch"], wo=w["w_out"],
             wg2=w["ffn2_w_gate"], wu2=w["ffn2_w_up"], wd2=w["ffn2_w_down"])
    inv_freq = [1.0 / (ROPE_BASE ** (j / QK_ROPE)) for j in range(0, QK_ROPE, 2)]
    freq = [0.0] * QK_NOPE + inv_freq + inv_freq + [0.0] * (HP - QK_HEAD)
    P = {n: w[n] for n in ("ffn1_norm", "mix_norm", "q_a_norm", "kv_a_norm", "ssm_norm", "ffn2_norm", "conv_b")}
    P.update(q_head_norm=_lanes128([w["q_head_norm"]]), k_head_norm=_lanes128([w["k_head_norm"]]),
             conv_w8=jnp.pad(conv_w_full, ((0, 8 - CONV_WIDTH), (0, 0))),
             dt_bias8=jnp.broadcast_to(_lanes128([w["dt_bias_fwd"], w["dt_bias_bwd"]]), (8, HP)),
             a_log8=jnp.broadcast_to(_lanes128([w["a_log_fwd"], w["a_log_bwd"]]), (8, HP)),
             d_skip_lanes=jnp.repeat(w["d_skip"].reshape(-1), PH).reshape(1, D_INNER),
             freq=jnp.asarray(freq, F32).reshape(1, HP))
    return W, P


def _unprepare(dW):
    dkvb = jnp.concatenate([_unpad_heads(dW["wk"], 1, QK_NOPE), _unpad_heads(dW["wv"], 1, V_HEAD)], axis=2)
    return dict(ffn1_w_gate=dW["wg1"], ffn1_w_up=dW["wu1"], ffn1_w_down=dW["wd1"], w_in=_unpad_w_in(dW["w_in"]),
                w_q_b=_unpad_heads(dW["wq"], 1, QK_HEAD).reshape(Q_LORA, N_HEADS * QK_HEAD),
                w_kv_b=dkvb.reshape(KV_LORA, N_HEADS * (QK_NOPE + V_HEAD)),
                w_attn_branch=_unpad_heads(dW["pa"], 0, V_HEAD).reshape(N_HEADS * V_HEAD, D_MODEL),
                w_ssm_branch=dW["pb"], w_out=dW["wo"],
                ffn2_w_gate=dW["wg2"], ffn2_w_up=dW["wu2"], ffn2_w_down=dW["wd2"])


def kernel(x, positions, ffn1_norm, ffn1_w_gate, ffn1_w_up, ffn1_w_down, mix_norm, w_in, q_a_norm, w_q_b, kv_a_norm, w_kv_b, q_head_norm, k_head_norm, conv_w, conv_b, a_log_fwd, a_log_bwd, dt_bias_fwd, dt_bias_bwd, d_skip, ssm_norm, w_attn_branch, w_ssm_branch, w_out, ffn2_norm, ffn2_w_gate, ffn2_w_up, ffn2_w_down, loss_target, m_ffn1_norm, m_ffn1_w_gate, m_ffn1_w_up, m_ffn1_w_down, m_mix_norm, m_w_in, m_q_a_norm, m_w_q_b, m_kv_a_norm, m_w_kv_b, m_q_head_norm, m_k_head_norm, m_conv_w, m_conv_b, m_a_log_fwd, m_a_log_bwd, m_dt_bias_fwd, m_dt_bias_bwd, m_d_skip, m_ssm_norm, m_w_attn_branch, m_w_ssm_branch, m_w_out, m_ffn2_norm, m_ffn2_w_gate, m_ffn2_w_up, m_ffn2_w_down, v_ffn1_norm, v_ffn1_w_gate, v_ffn1_w_up, v_ffn1_w_down, v_mix_norm, v_w_in, v_q_a_norm, v_w_q_b, v_kv_a_norm, v_w_kv_b, v_q_head_norm, v_k_head_norm, v_conv_w, v_conv_b, v_a_log_fwd, v_a_log_bwd, v_dt_bias_fwd, v_dt_bias_bwd, v_d_skip, v_ssm_norm, v_w_attn_branch, v_w_ssm_branch, v_w_out, v_ffn2_norm, v_ffn2_w_gate, v_ffn2_w_up, v_ffn2_w_down):
    given = dict(locals())
    T = x.shape[1]
    packed_names = [name for name, _, _ in PACKED]

    def two_d(a):
        return a.reshape(a.shape[1], -1) if a.ndim > 2 else a

    def kept(n, a):
        return _stored(n, two_d(a))

    w_loc = {n: kept(n, given[n]) for n in WEIGHTS}
    wb = _pack({n: w_loc[n].astype(BF) for n in packed_names})
    wf = jnp.pad(w_loc["conv_w"], ((0, 8 - CONV_WIDTH), (0, 0)))
    gb, gf = _gather_chips(wb, wf)
    full = _full_from_slots(gb)
    conv_w_full = jnp.concatenate([gf[j, :CONV_WIDTH] for j in range(N_CHIPS)], axis=1)
    full.update({n: w_loc[n] for n in WEIGHTS if n not in full and n != "conv_w"})
    W, P = _prepare(full, conv_w_full)
    dx, dW, dP = _local_step(x.reshape(T, D_MODEL), positions.reshape(T, 1).astype(F32), loss_target.reshape(T, D_MODEL), W, P)
    gp = _slots_from_full(_unprepare(dW))
    core = lax.axis_index("c").astype(jnp.int32).reshape(1)
    both_cores = _add_halves(gp, _halves_to_sibling(gp), core)
    cx, cy = lax.axis_index("x"), lax.axis_index("y")
    chips = jnp.stack([2 * cx + cy, 2 * (1 - cx) + cy, 2 * cx + (1 - cy)]).astype(jnp.int32)
    part, on = _add_near(both_cores, _exchange_near(both_cores), chips)
    grads = _unpack(_join_halves(_add_far(part, _exchange_far(on), core)))
    small = _unpack_small(_allreduce_small(_pack_small(dP)))
    grads.update({n: small[n].reshape(1, -1) for n, _ in SMALL if n not in ("conv_w", "loss")})
    grads["conv_w"] = lax.dynamic_slice_in_dim(small["conv_w"].reshape(CONV_WIDTH, XBC_DIM), _my_chip() * (XBC_DIM // N_CHIPS),
                                               XBC_DIM // N_CHIPS, axis=1)
    out_g, out_d, out_m, out_v = [], [], [], []
    for n in WEIGHTS:
        shape = given[n].shape
        delta, new_m, new_v = _adamw(w_loc[n], grads[n], kept(n, given["m_" + n]), kept(n, given["v_" + n]), name="adamw_" + n)
        for outs, a in ((out_g, grads[n]), (out_d, delta), (out_m, new_m), (out_v, new_v)):
            outs.append(_stored(n, a).reshape(shape))
    return (small["loss"].reshape(()), dx.reshape(x.shape), *out_g, *out_d, *out_m, *out_v)
```

```python
import functools
import math

import jax
import jax.numpy as jnp
from jax import lax
from jax.experimental import pallas as pl
from jax.experimental.pallas import tpu as pltpu

BF = jnp.bfloat16
F32 = jnp.float32
HI = lax.Precision.HIGHEST
MESH = pl.DeviceIdType.MESH

D_MODEL = 1024
D_FF = 2816
EPS = 1e-6
N_HEADS = 16
QK_NOPE = 64
QK_ROPE = 32
QK_HEAD = 96
V_HEAD = 64
Q_LORA = 384
KV_LORA = 256
ROPE_BASE = 10000.0
D_INNER = 2048
SSM_HEADS = 32
SSM_GROUPS = 4
D_STATE = 128
CONV_WIDTH = 5
CHUNK = 128
XBC_DIM = 3072
HP = 128
GW = D_INNER // SSM_GROUPS
HG = SSM_HEADS // SSM_GROUPS
PH = 64
U_Z, U_GA, U_GB, U_XBC, U_SMALL = 0, 2048, 3072, 4096, 7168
S_CQ, S_CKV, S_KPE, S_DT, SMALL_W = 0, 384, 640, 768, 896
U_PAD = U_SMALL + SMALL_W
IN_SPLITS = (Q_LORA, KV_LORA, QK_ROPE, D_INNER, XBC_DIM, SSM_HEADS, SSM_HEADS, D_MODEL, D_MODEL)

ADAM_LR = 0.001
ADAM_B1 = 0.9
ADAM_B2 = 0.999
ADAM_EPS = 1e-08
ADAM_WD = 0.01
ADAM_STEP = 10

V7X_VMEM_BYTES = 64 << 20
MM_VMEM_BUDGET = V7X_VMEM_BYTES * 5 // 8

NT = (((1,), (1,)), ((), ()))
TN = (((0,), (0,)), ((), ()))


def _pick(n, pref):
    best = None
    d = 128
    while d <= min(n, pref):
        if n % d == 0:
            best = d
        d += 128
    return best if best is not None else n


def _silu(x):
    return x * jax.nn.sigmoid(x)


def _dsilu(x):
    s = jax.nn.sigmoid(x)
    return s * (1.0 + x * (1.0 - s))


def _softplus(x):
    return jnp.maximum(x, 0.0) + jnp.log(1.0 + jnp.exp(-jnp.abs(x)))


def _mm(As, Bs, *, name, ta=False, tb=False, out_dtypes=(F32,), epilogue=None, extras=(), extra_offs=None,
        tm=1024, tn=512, tk=None, separate=False):
    As, Bs, extras = list(As), list(Bs), list(extras)
    a0, b0 = As[0], Bs[0]
    M, K = (a0.shape[1], a0.shape[0]) if ta else a0.shape
    N = b0.shape[0] if tb else b0.shape[1]
    tm, tn = _pick(M, tm), _pick(N, tn)
    n_a, n_b, n_e, n_o = len(As), len(Bs), len(extras), len(out_dtypes)
    n_res = n_b if n_a == 1 or separate else 1

    def vmem_bytes(k_tile):
        blocks = sum(tm * k_tile * a.dtype.itemsize for a in As) + sum(k_tile * tn * b.dtype.itemsize for b in Bs)
        tiles = tm * tn * (sum(jnp.dtype(dt).itemsize for dt in out_dtypes) + sum(e.dtype.itemsize for e in extras))
        return 2 * (blocks + tiles) + 2 * n_res * tm * tn * 4

    if tk is None:
        tk = K
        while vmem_bytes(tk) > MM_VMEM_BUDGET and tk > 128:
            tk = _pick(K, tk - 128)
    else:
        tk = _pick(K, tk)
    nk = K // tk
    n_acc = n_res if nk > 1 else 0
    if extra_offs is None:
        extra_offs = (0,) * n_e
    dn = (((0,) if ta else (1,), (1,) if tb else (0,)), ((), ()))
    bytes_a = sum(a.size * a.dtype.itemsize for a in As)
    bytes_b = sum(b.size * b.dtype.itemsize for b in Bs)
    n_outer = (N // tn) * bytes_a + bytes_b < (M // tm) * bytes_b + bytes_a

    def products(a_refs, b_refs):
        if n_a == 1:
            a = a_refs[0][...].astype(BF)
            return [lax.dot_general(a, b[...].astype(BF), dn, preferred_element_type=F32) for b in b_refs]
        if separate:
            return [lax.dot_general(a[...].astype(BF), b[...].astype(BF), dn, preferred_element_type=F32)
                    for a, b in zip(a_refs, b_refs)]
        total = None
        for a, b in zip(a_refs, b_refs):
            p = lax.dot_general(a[...].astype(BF), b[...].astype(BF), dn, preferred_element_type=F32)
            total = p if total is None else total + p
        return [total]

    def finish(accs, e_refs, o_refs):
        ex = [e[...] for e in e_refs]
        outs = epilogue(*accs, *ex) if epilogue is not None else tuple(accs)
        for o_ref, val in zip(o_refs, outs):
            o_ref[...] = val.astype(o_ref.dtype)

    def body(*refs):
        a_refs, b_refs = refs[:n_a], refs[n_a:n_a + n_b]
        e_refs = refs[n_a + n_b:n_a + n_b + n_e]
        o_refs = refs[n_a + n_b + n_e:n_a + n_b + n_e + n_o]
        acc_refs = refs[n_a + n_b + n_e + n_o:]
        if nk == 1:
            finish(products(a_refs, b_refs), e_refs, o_refs)
            return
        k = pl.program_id(2)

        @pl.when(k == 0)
        def _():
            for acc in acc_refs:
                acc[...] = jnp.zeros_like(acc)

        for acc, p in zip(acc_refs, products(a_refs, b_refs)):
            acc[...] += p

        @pl.when(k == nk - 1)
        def _():
            finish([acc[...] for acc in acc_refs], e_refs, o_refs)

    def at(f):
        return (lambda j, i, k: f(i, j, k)) if n_outer else f

    a_spec = pl.BlockSpec((tk, tm), at(lambda i, j, k: (k, i))) if ta else pl.BlockSpec((tm, tk), at(lambda i, j, k: (i, k)))
    b_spec = pl.BlockSpec((tn, tk), at(lambda i, j, k: (j, k))) if tb else pl.BlockSpec((tk, tn), at(lambda i, j, k: (k, j)))
    e_specs = [pl.BlockSpec((tm, tn), at(functools.partial(lambda i, j, k, o: (i, j + o), o=off // tn))) for off in extra_offs]
    for off in extra_offs:
        assert off % tn == 0
    outs = pl.pallas_call(
        body, name=name,
        out_shape=tuple(jax.ShapeDtypeStruct((M, N), dt) for dt in out_dtypes),
        grid=(N // tn, M // tm, nk) if n_outer else (M // tm, N // tn, nk),
        in_specs=[a_spec] * n_a + [b_spec] * n_b + e_specs,
        out_specs=tuple(pl.BlockSpec((tm, tn), at(lambda i, j, k: (i, j))) for _ in out_dtypes),
        scratch_shapes=[pltpu.VMEM((tm, tn), F32)] * n_acc,
        compiler_params=pltpu.CompilerParams(dimension_semantics=("parallel", "parallel", "arbitrary")),
    )(*As, *Bs, *extras)
    return outs[0] if n_o == 1 else outs


def _rms_fwd(x, g, *, name, blk_w=None, blk_idx=0, off=0, width=None, out_dtype=BF):
    T = x.shape[0]
    blk_w = x.shape[1] if blk_w is None else blk_w
    width = blk_w if width is None else width
    tt = _pick(T, 512)

    def body(x_ref, g_ref, o_ref):
        xf = x_ref[:, off:off + width]
        r = lax.rsqrt(jnp.mean(xf * xf, axis=-1, keepdims=True) + EPS)
        o_ref[...] = (xf * r * g_ref[...]).astype(o_ref.dtype)

    return pl.pallas_call(
        body, name=name, out_shape=jax.ShapeDtypeStruct((T, width), out_dtype), grid=(T // tt,),
        in_specs=[pl.BlockSpec((tt, blk_w), lambda i: (i, blk_idx)), pl.BlockSpec((1, width), lambda i: (0, 0))],
        out_specs=pl.BlockSpec((tt, width), lambda i: (i, 0)),
    )(x, g)


def _rms_bwd(dy, x, g, *, name, blk_w=None, blk_idx=0, off=0, width=None, add=None, out_dtypes=(F32,)):
    T = x.shape[0]
    blk_w = x.shape[1] if blk_w is None else blk_w
    width = blk_w if width is None else width
    tt = _pick(T, 512)
    has_add = add is not None
    n_dx = len(out_dtypes)

    def body(*refs):
        dy_ref, x_ref, g_ref = refs[:3]
        dx_refs, dg_ref = refs[3 + has_add:3 + has_add + n_dx], refs[-1]
        xf = x_ref[:, off:off + width]
        d = dy_ref[...].astype(F32)
        r = lax.rsqrt(jnp.mean(xf * xf, axis=-1, keepdims=True) + EPS)
        gd = d * g_ref[...]
        dx = r * gd - xf * (r * r * r) * jnp.mean(gd * xf, axis=-1, keepdims=True)
        if has_add:
            dx = dx + refs[3][...]
        for dx_ref in dx_refs:
            dx_ref[...] = dx.astype(dx_ref.dtype)

        @pl.when(pl.program_id(0) == 0)
        def _():
            dg_ref[...] = jnp.zeros_like(dg_ref)

        dg_ref[...] += jnp.broadcast_to(jnp.sum(d * xf * r, axis=0, keepdims=True), dg_ref.shape)

    row = pl.BlockSpec((tt, width), lambda i: (i, 0))
    in_specs = [row, pl.BlockSpec((tt, blk_w), lambda i: (i, blk_idx)), pl.BlockSpec((1, width), lambda i: (0, 0))]
    args = [dy, x, g]
    if has_add:
        in_specs.append(row)
        args.append(add)
    return pl.pallas_call(
        body, name=name,
        out_shape=tuple(jax.ShapeDtypeStruct((T, width), dt) for dt in out_dtypes) + (jax.ShapeDtypeStruct((8, width), F32),),
        grid=(T // tt,), in_specs=in_specs,
        out_specs=(row,) * n_dx + (pl.BlockSpec((8, width), lambda i: (0, 0)),),
        compiler_params=pltpu.CompilerParams(dimension_semantics=("arbitrary",)),
    )(*args)


def _rope_tables(pos_col, freq_lane):
    T = pos_col.shape[0]
    tt = _pick(T, 512)

    def body(p_ref, f_ref, c_ref, s_ref):
        ang = p_ref[...] * f_ref[...]
        lane = lax.broadcasted_iota(jnp.int32, ang.shape, 1)
        c_ref[...] = jnp.where(lane < QK_HEAD, jnp.cos(ang), 0.0)
        sn = jnp.sin(ang)
        s_ref[...] = jnp.where((lane >= QK_NOPE) & (lane < QK_NOPE + 16), -sn,
                               jnp.where((lane >= QK_NOPE + 16) & (lane < QK_HEAD), sn, 0.0))

    return pl.pallas_call(
        body, name="rope_tables", out_shape=(jax.ShapeDtypeStruct((T, HP), F32),) * 2, grid=(T // tt,),
        in_specs=[pl.BlockSpec((tt, 1), lambda i: (i, 0)), pl.BlockSpec((1, HP), lambda i: (0, 0))],
        out_specs=(pl.BlockSpec((tt, HP), lambda i: (i, 0)),) * 2,
    )(pos_col, freq_lane)


def _swap_rope_halves(n):
    src = lax.broadcasted_iota(jnp.int32, (HP, HP), 0)
    dst = lax.broadcasted_iota(jnp.int32, (HP, HP), 1)
    lo = (dst >= QK_NOPE) & (dst < QK_NOPE + 16) & (src == dst + 16)
    hi = (dst >= QK_NOPE + 16) & (dst < QK_HEAD) & (src == dst - 16)
    return _split_dot(n, jnp.where(lo | hi, 1.0, 0.0).astype(BF), 2)


def _qk_prep_fwd(raw, kpe, gain, C, S, *, name, kpe_blk=0, out_scale=1.0):
    T = raw.shape[0]
    tt = _pick(T, 256)
    has_kpe = kpe is not None

    def body(*refs):
        if has_kpe:
            raw_ref, kpe_ref, g_ref, c_ref, s_ref, o_ref = refs
        else:
            raw_ref, g_ref, c_ref, s_ref, o_ref = refs
        for h in range(N_HEADS):
            hs = slice(HP * h, HP * (h + 1))
            xr = raw_ref[:, hs] + kpe_ref[...] if has_kpe else raw_ref[:, hs]
            r = lax.rsqrt(jnp.sum(xr * xr, axis=-1, keepdims=True) * (1.0 / QK_HEAD) + EPS)
            n = xr * r * g_ref[...]
            o_ref[:, hs] = ((n * c_ref[...] + _swap_rope_halves(n) * s_ref[...]) * out_scale).astype(o_ref.dtype)

    heads = pl.BlockSpec((tt, N_HEADS * HP), lambda i: (i, 0))
    shared = pl.BlockSpec((tt, HP), lambda i: (i, 0))
    kpe_spec = pl.BlockSpec((tt, HP), lambda i: (i, kpe_blk))
    in_specs = [heads] + ([kpe_spec] if has_kpe else []) + [pl.BlockSpec((1, HP), lambda i: (0, 0)), shared, shared]
    args = [raw] + ([kpe] if has_kpe else []) + [gain, C, S]
    return pl.pallas_call(
        body, name=name, out_shape=jax.ShapeDtypeStruct(raw.shape, BF), grid=(T // tt,),
        in_specs=in_specs, out_specs=heads,
    )(*args)


def _qk_prep_bwd(dout, raw, kpe, gain, C, S, *, name, kpe_blk=0, in_scale=1.0):
    T = raw.shape[0]
    tt = _pick(T, 256)
    has_kpe = kpe is not None

    def body(*refs):
        if has_kpe:
            d_ref, raw_ref, kpe_ref, g_ref, c_ref, s_ref, dx_ref, dg_ref, dkpe_ref = refs
        else:
            d_ref, raw_ref, g_ref, c_ref, s_ref, dx_ref, dg_ref = refs
        dg = jnp.zeros((1, HP), F32)
        dkpe = jnp.zeros((tt, HP), F32)
        for h in range(N_HEADS):
            hs = slice(HP * h, HP * (h + 1))
            xr = raw_ref[:, hs] + kpe_ref[...] if has_kpe else raw_ref[:, hs]
            d = d_ref[:, hs].astype(F32) * in_scale
            r = lax.rsqrt(jnp.sum(xr * xr, axis=-1, keepdims=True) * (1.0 / QK_HEAD) + EPS)
            dn = d * c_ref[...] + _swap_rope_halves(d * s_ref[...])
            gd = dn * g_ref[...]
            dx = r * gd - xr * (r * r * r) * (jnp.sum(gd * xr, axis=-1, keepdims=True) * (1.0 / QK_HEAD))
            dx_ref[:, hs] = dx.astype(dx_ref.dtype)
            dg = dg + jnp.sum(dn * xr * r, axis=0, keepdims=True)
            dkpe = dkpe + dx

        @pl.when(pl.program_id(0) == 0)
        def _():
            dg_ref[...] = jnp.zeros_like(dg_ref)

        dg_ref[...] += jnp.broadcast_to(dg, dg_ref.shape)
        if has_kpe:
            dkpe_ref[...] = dkpe

    heads = pl.BlockSpec((tt, N_HEADS * HP), lambda i: (i, 0))
    shared = pl.BlockSpec((tt, HP), lambda i: (i, 0))
    kpe_spec = pl.BlockSpec((tt, HP), lambda i: (i, kpe_blk))
    in_specs = [heads, heads] + ([kpe_spec] if has_kpe else []) + [pl.BlockSpec((1, HP), lambda i: (0, 0)), shared, shared]
    args = [dout, raw] + ([kpe] if has_kpe else []) + [gain, C, S]
    out_shape = [jax.ShapeDtypeStruct(raw.shape, BF), jax.ShapeDtypeStruct((8, HP), F32)]
    out_specs = [heads, pl.BlockSpec((8, HP), lambda i: (0, 0))]
    if has_kpe:
        out_shape.append(jax.ShapeDtypeStruct((T, HP), F32))
        out_specs.append(shared)
    return pl.pallas_call(
        body, name=name, out_shape=tuple(out_shape), grid=(T // tt,),
        in_specs=in_specs, out_specs=tuple(out_specs),
        compiler_params=pltpu.CompilerParams(dimension_semantics=("arbitrary",)),
    )(*args)


ATTN_SCALE = 1.0 / math.sqrt(QK_HEAD)
LOG2E = 1.0 / math.log(2.0)
Q_SCALE = ATTN_SCALE * LOG2E


def _attn_fwd(q, k, v):
    T = q.shape[0]
    tq = _pick(T, 1024)
    tk = _pick(T, 1024)

    def body(q_ref, k_ref, v_ref, o_ref, ob_ref, lse_ref):
        qt = q_ref[...]
        m = o = None
        for j in range(T // tk):
            ks = slice(j * tk, (j + 1) * tk)
            s = lax.dot_general(qt, k_ref[ks, :], NT, preferred_element_type=F32)
            m_j = jnp.max(s, axis=-1, keepdims=True)
            m_new = m_j if m is None else jnp.maximum(m, m_j)
            o_j = jnp.dot(jnp.exp2(s - m_new).astype(BF), v_ref[ks, :], preferred_element_type=F32)
            o = o_j if o is None else o * jnp.exp2(m - m_new) + o_j
            m = m_new
        l = o[:, V_HEAD:V_HEAD + 1]
        o = o / l
        o_ref[...] = o
        ob_ref[...] = o.astype(ob_ref.dtype)
        lse_ref[...] = jnp.broadcast_to(m + jnp.log2(l), lse_ref.shape)

    qs = pl.BlockSpec((tq, HP), lambda h, i: (i, h))
    kv = pl.BlockSpec((T, HP), lambda h, i: (0, h))
    return pl.pallas_call(
        body, name="attn_fwd",
        out_shape=(jax.ShapeDtypeStruct(q.shape, F32), jax.ShapeDtypeStruct(q.shape, BF), jax.ShapeDtypeStruct(q.shape, F32)),
        grid=(N_HEADS, T // tq), in_specs=[qs, kv, kv], out_specs=(qs, qs, qs),
        compiler_params=pltpu.CompilerParams(dimension_semantics=("parallel", "parallel")),
    )(q, k, v)


def _attn_bwd(q, k, v, do, o, lse):
    T = q.shape[0]
    tb = _pick(T, 512)
    nb = T // tb
    tkey = _pick(T, 1024)

    def body(q_ref, k_ref, v_ref, do_ref, o_ref, lse_ref, dq_ref, dk_ref, dv_ref, delta_rows, lse_rows, dob_scr, dv_acc):
        dq_ref[...] = jnp.zeros_like(dq_ref)
        dk_ref[...] = jnp.zeros_like(dk_ref)
        lane = lax.broadcasted_iota(jnp.int32, (8, HP), 1)
        ones8 = jnp.ones((8, HP), BF)
        first8 = jnp.where(lane == 0, 1.0, 0.0).astype(BF)

        def as_rows(pick, v):
            total, rest = None, v
            for _ in range(3):
                piece = rest.astype(BF)
                part = lax.dot_general(pick, piece, NT, preferred_element_type=F32)
                total = part if total is None else total + part
                rest = rest - piece.astype(F32)
            return total

        def per_q_tile(i, carry):
            qs = pl.ds(pl.multiple_of(i * tb, tb), tb)
            doi = do_ref[qs, :]
            delta_rows[i] = as_rows(ones8, doi * o_ref[qs, :])
            lse_rows[i] = as_rows(first8, lse_ref[qs, :])
            dob_scr[qs, :] = doi.astype(BF)
            return carry

        lax.fori_loop(0, nb, per_q_tile, 0)

        def k_loop(j, carry):
            ks = pl.ds(pl.multiple_of(j * tkey, tkey), tkey)
            kj, vj = k_ref[ks, :], v_ref[ks, :]

            dv_acc[...] = jnp.zeros_like(dv_acc)

            def q_loop(i, carry_q):
                qs = pl.ds(pl.multiple_of(i * tb, tb), tb)
                qi = q_ref[qs, :]
                dob = dob_scr[qs, :]
                s_t = lax.dot_general(kj, qi, NT, preferred_element_type=F32)
                p_t = jnp.exp2(s_t - lse_rows[i, 0:1, :])
                dp_t = lax.dot_general(vj, dob, NT, preferred_element_type=F32)
                ds_t = (p_t * (dp_t - delta_rows[i, 0:1, :])).astype(BF)
                dv_acc[...] += jnp.dot(p_t.astype(BF), dob, preferred_element_type=F32)
                dk_ref[ks, :] += jnp.dot(ds_t, qi, preferred_element_type=F32)
                dq_ref[qs, :] += lax.dot_general(ds_t, kj, TN, preferred_element_type=F32)
                return carry_q

            lax.fori_loop(0, nb, q_loop, 0)
            dv_ref[ks, :] = dv_acc[...].astype(dv_ref.dtype)
            return carry

        lax.fori_loop(0, T // tkey, k_loop, 0)

    spec = pl.BlockSpec((T, HP), lambda h: (0, h))
    return pl.pallas_call(
        body, name="attn_bwd",
        out_shape=(jax.ShapeDtypeStruct(q.shape, F32), jax.ShapeDtypeStruct(q.shape, F32), jax.ShapeDtypeStruct(q.shape, BF)),
        grid=(N_HEADS,), in_specs=[spec] * 6, out_specs=(spec,) * 3,
        scratch_shapes=[pltpu.VMEM((nb, 8, tb), F32), pltpu.VMEM((nb, 8, tb), F32), pltpu.VMEM((T, HP), BF),
                        pltpu.VMEM((tkey, HP), F32)],
        compiler_params=pltpu.CompilerParams(dimension_semantics=("parallel",), vmem_limit_bytes=2 * 15 * T * HP * 2 + (8 << 20)),
    )(q, k, v, do, o, lse)


CONV_TC = 512
CONV_PAD = CONV_WIDTH // 2


def _halo_specs(tr, col_of):
    r8 = tr // 8
    cur = pl.BlockSpec((tr, CONV_TC), lambda j, i: (i, col_of(j)))
    prev = pl.BlockSpec((8, CONV_TC), lambda j, i: (jnp.maximum(i * r8 - 1, 0), col_of(j)))

    def nxt_map(j, i, n8):
        return (jnp.minimum((i + 1) * r8, n8 - 1), col_of(j))

    return cur, prev, nxt_map


def _with_halo(prev_ref, cur_ref, next_ref, i, n_i):
    prev = jnp.where(i == 0, 0.0, prev_ref[...].astype(F32))
    nxt = jnp.where(i == n_i - 1, 0.0, next_ref[...].astype(F32))
    return jnp.concatenate([prev, cur_ref[...].astype(F32), nxt], axis=0)


def _conv_fwd(u, w8, b):
    T = u.shape[0]
    tr = _pick(T, 512)
    n_i = T // tr
    c0 = U_XBC // CONV_TC
    cur, prev, nxt_map = _halo_specs(tr, lambda j: c0 + j)
    nxt = pl.BlockSpec((8, CONV_TC), functools.partial(nxt_map, n8=T // 8))

    def body(p_ref, c_ref, n_ref, w_ref, b_ref, pre_ref, act_ref):
        i = pl.program_id(1)
        full = _with_halo(p_ref, c_ref, n_ref, i, n_i)
        acc = jnp.broadcast_to(b_ref[...], (tr, CONV_TC))
        for kk in range(CONV_WIDTH):
            acc = acc + full[8 - CONV_PAD + kk:8 - CONV_PAD + kk + tr, :] * w_ref[kk:kk + 1, :]
        pre_ref[...] = acc
        act_ref[...] = _silu(acc)

    out = pl.BlockSpec((tr, CONV_TC), lambda j, i: (i, j))
    return pl.pallas_call(
        body, name="conv_fwd", out_shape=(jax.ShapeDtypeStruct((T, XBC_DIM), F32),) * 2,
        grid=(XBC_DIM // CONV_TC, n_i),
        in_specs=[prev, cur, nxt, pl.BlockSpec((8, CONV_TC), lambda j, i: (0, j)), pl.BlockSpec((1, CONV_TC), lambda j, i: (0, j))],
        out_specs=(out, out),
    )(u, u, u, w8, b)


def _conv_bwd(dacts, pre, u, w8, col0, *, name):
    T, width = dacts[0].shape
    tr = _pick(T, 512)
    n_i = T // tr
    n_d = len(dacts)
    cd = col0 // CONV_TC
    cx = (U_XBC + col0) // CONV_TC

    def halo(col_of):
        cur, prev, nxt_map = _halo_specs(tr, col_of)
        return [prev, cur, pl.BlockSpec((8, CONV_TC), functools.partial(nxt_map, n8=T // 8))]

    def body(*refs):
        d_refs, pre_refs, x_refs = refs[:3 * n_d], refs[3 * n_d:3 * n_d + 3], refs[3 * n_d + 3:3 * n_d + 6]
        w_ref, dx_ref, dw_ref = refs[3 * n_d + 6:]
        i = pl.program_id(1)
        dfull = _with_halo(*d_refs[0:3], i, n_i)
        for p in range(1, n_d):
            dfull = dfull + _with_halo(*d_refs[3 * p:3 * p + 3], i, n_i)
        dfull = dfull * _dsilu(_with_halo(*pre_refs, i, n_i))
        xfull = _with_halo(*x_refs, i, n_i)
        dcur = dfull[8:8 + tr, :]
        dx = jnp.zeros((tr, CONV_TC), F32)
        rows = []
        for kk in range(CONV_WIDTH):
            dx = dx + dfull[8 + CONV_PAD - kk:8 + CONV_PAD - kk + tr, :] * w_ref[kk:kk + 1, :]
            rows.append(jnp.sum(dcur * xfull[8 - CONV_PAD + kk:8 - CONV_PAD + kk + tr, :], axis=0, keepdims=True))
        rows.append(jnp.sum(dcur, axis=0, keepdims=True))
        rows.append(jnp.zeros((2, CONV_TC), F32))
        dx_ref[...] = dx.astype(dx_ref.dtype)

        @pl.when(i == 0)
        def _():
            dw_ref[...] = jnp.zeros_like(dw_ref)

        dw_ref[...] += jnp.concatenate(rows, axis=0)

    out = pl.BlockSpec((tr, CONV_TC), lambda j, i: (i, j))
    return pl.pallas_call(
        body, name=name, out_shape=(jax.ShapeDtypeStruct((T, width), BF), jax.ShapeDtypeStruct((8, width), F32)),
        grid=(width // CONV_TC, n_i),
        in_specs=halo(lambda j: j) * n_d + halo(lambda j: cd + j) + halo(lambda j: cx + j)
        + [pl.BlockSpec((8, CONV_TC), lambda j, i: (0, cd + j))],
        out_specs=(out, pl.BlockSpec((8, CONV_TC), lambda j, i: (0, j))),
        compiler_params=pltpu.CompilerParams(dimension_semantics=("parallel", "arbitrary")),
    )(*[d for d in dacts for _ in range(3)], pre, pre, pre, u, u, u, w8)


N_HB = 2 * SSM_GROUPS
P_DT, P_CS, P_E, P_W = 0, HP, 2 * HP, 3 * HP
DT_BLK = (U_SMALL + S_DT) // HP


def _tri(rev, transpose=False):
    rows = lax.broadcasted_iota(jnp.int32, (CHUNK, CHUNK), 0)
    cols = lax.broadcasted_iota(jnp.int32, (CHUNK, CHUNK), 1)
    if transpose:
        rows, cols = cols, rows
    return (cols >= rows) if rev else (cols <= rows)


def _ssd_prep(u, bias8, alog8):
    T = u.shape[0]
    nc = T // CHUNK

    def body(dt_ref, bias_ref, a_ref, cols_ref, rows_ref):
        lane = lax.broadcasted_iota(jnp.int32, (CHUNK, HP), 1)
        dt = _softplus(dt_ref[...] + bias_ref[0:1, :])
        da = dt * (-jnp.exp(a_ref[0:1, :]))
        cs_f = jnp.dot(jnp.where(_tri(False), 1.0, 0.0).astype(F32), da, precision=HI, preferred_element_type=F32)
        cs_b = jnp.dot(jnp.where(_tri(True), 1.0, 0.0).astype(F32), da, precision=HI, preferred_element_type=F32)
        cs = jnp.where(lane < SSM_HEADS, cs_f, cs_b)
        tot = jnp.where(lane[0:1] < SSM_HEADS, cs_f[CHUNK - 1:CHUNK, :], cs_b[0:1, :])
        e, w = jnp.exp(cs), jnp.exp(tot - cs)
        tot8 = jnp.broadcast_to(tot, (8, HP))
        etot8 = jnp.exp(tot8)
        for b in range(N_HB):
            down = (HP - HG * b) % HP

            def rolled(v):
                return pltpu.roll(v, down, 1) if down else v

            cols_ref[b, :, P_DT:P_DT + HP] = rolled(dt)
            cs_r = rolled(cs)
            cols_ref[b, :, P_CS:P_CS + HP] = cs_r
            cols_ref[b, :, P_E:P_E + HP] = rolled(e)
            cols_ref[b, :, P_W:P_W + HP] = rolled(w)
            rows_ref[b, 0, 0:8, :] = cs_r.T[0:8, :]
            r8 = lax.broadcasted_iota(jnp.int32, (8, HP), 0)
            rows_ref[b, 0, 8:16, :] = jnp.where(r8 == 0, rolled(tot8), jnp.where(r8 == 1, rolled(etot8), 0.0))

    vec = pl.BlockSpec((8, HP), lambda c: (0, 0))
    return pl.pallas_call(
        body, name="ssd_prep",
        out_shape=(jax.ShapeDtypeStruct((N_HB, T, 4 * HP), F32), jax.ShapeDtypeStruct((N_HB, nc, 16, HP), F32)),
        grid=(nc,), in_specs=[pl.BlockSpec((CHUNK, HP), lambda c: (c, DT_BLK)), vec, vec],
        out_specs=(pl.BlockSpec((N_HB, CHUNK, 4 * HP), lambda c: (0, c, 0)), pl.BlockSpec((N_HB, 1, 16, HP), lambda c: (0, c, 0, 0))),
    )(u, bias8, alog8)


def _ssd_specs(T, rev, bwd):
    nc = T // CHUNK
    fwd_order = (lambda c: nc - 1 - c) if rev else (lambda c: c)
    cm = (lambda c: fwd_order(nc - 1 - c)) if bwd else fwd_order
    hb0 = SSM_GROUPS if rev else 0
    xs = pl.BlockSpec((CHUNK, GW), lambda c, g: (cm(c), g))
    bs = pl.BlockSpec((CHUNK, D_STATE), lambda c, g: (cm(c), D_INNER // D_STATE + g))
    cs = pl.BlockSpec((CHUNK, D_STATE), lambda c, g: (cm(c), (D_INNER + SSM_GROUPS * D_STATE) // D_STATE + g))
    cols = pl.BlockSpec((1, CHUNK, 4 * HP), lambda c, g: (hb0 + g, cm(c), 0))
    rows = pl.BlockSpec((1, 1, 16, HP), lambda c, g: (hb0 + g, cm(c), 0, 0))
    return nc, cm, xs, bs, cs, cols, rows


def _head_lanes(to_heads):
    shape = (GW, HP) if to_heads else (HP, GW)
    wide = lax.broadcasted_iota(jnp.int32, shape, 0 if to_heads else 1)
    head = lax.broadcasted_iota(jnp.int32, shape, 1 if to_heads else 0)
    return jnp.where((wide >= PH * head) & (wide < PH * (head + 1)), 1.0, 0.0).astype(BF)


def _split_dot(v, m, terms):
    total, rest = None, v
    for _ in range(terms):
        piece = rest.astype(BF)
        part = jnp.dot(piece, m, preferred_element_type=F32)
        total = part if total is None else total + part
        rest = rest - piece.astype(F32)
    return total


def _spread_cols(cols_ref, rows_ref):
    spread = _head_lanes(False)
    dt_e = _split_dot(cols_ref[0, :, P_DT:P_DT + HP], spread, 3)
    e_e = _split_dot(cols_ref[0, :, P_E:P_E + HP], spread, 2)
    w_e = _split_dot(cols_ref[0, :, P_W:P_W + HP], spread, 2)
    etot_e = _split_dot(rows_ref[0, 0, 8:16, :], spread, 3)[1:2, :]
    return dt_e, e_e, w_e, etot_e


def _decay(cols_ref, rows_ref, hh, incl, transpose=False):
    col = cols_ref[0, :, P_CS + hh:P_CS + hh + 1]
    row = rows_ref[0, 0, hh:hh + 1, :]
    return jnp.where(incl, jnp.exp(row - col if transpose else col - row), 0.0)


def _ssd_fwd(act, cols, rows, *, rev, name, add=None):
    T = act.shape[0]
    nc, cm, xs_s, b_s, c_s, cols_s, rows_s = _ssd_specs(T, rev, False)
    has_add = add is not None

    def body(*refs):
        x_ref, b_ref, c_ref, cols_ref, rows_ref = refs[:5]
        y_ref, st_ref, state = refs[5 + has_add:]
        c, g = pl.program_id(0), pl.program_id(1)

        @pl.when(c == 0)
        def _():
            state[g] = jnp.zeros((D_STATE, GW), F32)

        incl = _tri(rev)
        bm, cmat = b_ref[...].astype(BF), c_ref[...].astype(BF)
        bm_t = b_ref[...].T.astype(BF)
        cb = lax.dot_general(cmat, bm, NT, preferred_element_type=F32)
        dt_e, e_e, w_e, etot_e = _spread_cols(cols_ref, rows_ref)
        prev_all = state[g]
        st_ref[...] = prev_all
        xdt = x_ref[...] * dt_e
        xdt_b = xdt.astype(BF)
        yo_all = jnp.dot(cmat, prev_all.astype(BF), preferred_element_type=F32) * e_e
        state[g] = prev_all * etot_e + jnp.dot(bm_t, (xdt * w_e).astype(BF), preferred_element_type=F32)
        for hh in range(HG):
            hs = slice(PH * hh, PH * (hh + 1))
            lmat = _decay(cols_ref, rows_ref, hh, incl)
            yd = jnp.dot((cb * lmat).astype(BF), xdt_b[:, hs], preferred_element_type=F32)
            y_ref[:, hs] = yd + yo_all[:, hs] + refs[5][:, hs] if has_add else yd + yo_all[:, hs]

    return pl.pallas_call(
        body, name=name,
        out_shape=(jax.ShapeDtypeStruct((T, D_INNER), F32), jax.ShapeDtypeStruct((nc * D_STATE, D_INNER), F32)),
        grid=(nc, SSM_GROUPS), in_specs=[xs_s, b_s, c_s, cols_s, rows_s] + [xs_s] * has_add, out_specs=(xs_s, xs_s),
        scratch_shapes=[pltpu.VMEM((SSM_GROUPS, D_STATE, GW), F32)],
        compiler_params=pltpu.CompilerParams(dimension_semantics=("arbitrary", "arbitrary")),
    )(act, act, act, cols, rows, *([add] if has_add else []))


def _ssd_bwd(act, cols, rows, states, dy, *, rev, name, skip=None, add=None):
    T = act.shape[0]
    nc, cm, xs_s, b_s, c_s, cols_s, rows_s = _ssd_specs(T, rev, True)
    has_skip, has_add = skip is not None, add is not None
    n_in = 7 + has_skip + 3 * has_add

    def body(*refs):
        x_ref, b_ref, c_ref, cols_ref, rows_ref, st_ref, dy_ref = refs[:7]
        extra = list(refs[7:n_in])
        dx_ref, db_ref, dc_ref, dsel_ref, dtot_ref, dstate, dcs_cols, dcs_rows, dcb, dm_scr, dxdt_scr = refs[n_in:]
        c, g = pl.program_id(0), pl.program_id(1)

        @pl.when(c == 0)
        def _():
            dstate[g] = jnp.zeros((D_STATE, GW), F32)

        incl, incl_t = _tri(rev), _tri(rev, transpose=True)
        bm, cmat = b_ref[...].astype(BF), c_ref[...].astype(BF)
        cm_t = c_ref[...].T.astype(BF)
        cb = lax.dot_general(cmat, bm, NT, preferred_element_type=F32)
        cb_t = lax.dot_general(bm, cmat, NT, preferred_element_type=F32)
        prev_all, ds_all = st_ref[...], dstate[g]
        pb_all, dsb_all = prev_all.astype(BF), ds_all.astype(BF)
        cp_all = jnp.dot(cmat, pb_all, preferred_element_type=F32)
        bds_all = jnp.dot(bm, dsb_all, preferred_element_type=F32)
        dt_e, e_e, w_e, etot_e = _spread_cols(cols_ref, rows_ref)
        to_heads = _head_lanes(True)
        x, dy = x_ref[...], dy_ref[...]
        xdt = x * dt_e
        xdt_b, dy_b = xdt.astype(BF), dy.astype(BF)
        dye_b, xdw_b = (dy * e_e).astype(BF), (xdt * w_e).astype(BF)
        for hh in range(HG):
            hs = slice(PH * hh, PH * (hh + 1))
            mmat_t = cb_t * _decay(cols_ref, rows_ref, hh, incl_t, transpose=True)
            dm_scr[hh] = lax.dot_general(dy_b[:, hs], xdt_b[:, hs], NT, preferred_element_type=F32)
            dxdt_scr[:, hs] = jnp.dot(mmat_t.astype(BF), dy_b[:, hs], preferred_element_type=F32)
        bdsw = bds_all * w_e
        dxdt = dxdt_scr[...] + bdsw
        dx = dxdt * dt_e
        if has_skip:
            dx = dx + dy * extra.pop(0)[...]
        if has_add:
            dx = dx + extra[0][...]
        dx_ref[...] = dx
        t = _split_dot(xdt * bdsw, to_heads, 2)
        dcs_state = _split_dot(dy * cp_all, to_heads, 2) * cols_ref[0, :, P_E:P_E + HP] - t
        dsel_ref[0, :, 0:HP] = _split_dot(dxdt * x, to_heads, 2)
        sp = _split_dot(jnp.broadcast_to(jnp.sum(ds_all * prev_all, axis=0, keepdims=True), (8, GW)), to_heads, 2)
        dtot_ref[0, 0] = jnp.sum(t, axis=0, keepdims=True) + sp * rows_ref[0, 0, 9:10, :]
        dstate[g] = ds_all * etot_e + jnp.dot(cm_t, dye_b, preferred_element_type=F32)
        dcs_cols[...] = jnp.zeros_like(dcs_cols)
        dcs_rows[...] = jnp.zeros_like(dcs_rows)
        dcb[...] = jnp.zeros_like(dcb)
        for hh in range(HG):
            lmat = _decay(cols_ref, rows_ref, hh, incl)
            dm = dm_scr[hh]
            qm = dm * (cb * lmat)
            dcs_cols[:, hh:hh + 1] = jnp.sum(qm, axis=1, keepdims=True)
            dcs_rows[hh:hh + 1, :] = jnp.sum(qm, axis=0, keepdims=True)
            dcb[...] += dm * lmat
        dcb_all = dcb[...]
        dsel_ref[0, :, HP:2 * HP] = dcs_state + dcs_cols[...] - dcs_rows[...].T
        dc = (lax.dot_general(dye_b, pb_all, NT, preferred_element_type=F32)
              + jnp.dot(dcb_all.astype(BF), bm, preferred_element_type=F32))
        db = (lax.dot_general(xdw_b, dsb_all, NT, preferred_element_type=F32)
              + jnp.dot(dcb_all.T.astype(BF), cmat, preferred_element_type=F32))
        db_ref[...] = db + extra[1][...] if has_add else db
        dc_ref[...] = dc + extra[2][...] if has_add else dc

    bc_out = pl.BlockSpec((CHUNK, D_STATE), lambda c, g: (cm(c), g))
    more_specs = [pl.BlockSpec((1, GW), lambda c, g: (0, g))] * has_skip + [xs_s, bc_out, bc_out] * has_add
    more_args = ([skip] if has_skip else []) + (list(add) if has_add else [])
    return pl.pallas_call(
        body, name=name,
        out_shape=(jax.ShapeDtypeStruct((T, D_INNER), F32), jax.ShapeDtypeStruct((T, SSM_GROUPS * D_STATE), F32),
                   jax.ShapeDtypeStruct((T, SSM_GROUPS * D_STATE), F32), jax.ShapeDtypeStruct((SSM_GROUPS, T, 2 * HP), F32),
                   jax.ShapeDtypeStruct((SSM_GROUPS, nc, 8, HP), F32)),
        grid=(nc, SSM_GROUPS), in_specs=[xs_s, b_s, c_s, cols_s, rows_s, xs_s, xs_s] + more_specs,
        out_specs=(xs_s, bc_out, bc_out, pl.BlockSpec((1, CHUNK, 2 * HP), lambda c, g: (g, cm(c), 0)),
                   pl.BlockSpec((1, 1, 8, HP), lambda c, g: (g, cm(c), 0, 0))),
        scratch_shapes=[pltpu.VMEM((SSM_GROUPS, D_STATE, GW), F32), pltpu.VMEM((CHUNK, CHUNK), F32),
                        pltpu.VMEM((CHUNK, CHUNK), F32), pltpu.VMEM((CHUNK, CHUNK), F32),
                        pltpu.VMEM((HG, CHUNK, CHUNK), F32), pltpu.VMEM((CHUNK, GW), F32)],
        compiler_params=pltpu.CompilerParams(dimension_semantics=("arbitrary", "arbitrary")),
    )(act, act, act, cols, rows, states, dy, *more_args)


def _ssd_prep_bwd(u, bias8, alog8, dsel_f, dtot_f, dsel_b, dtot_b):
    T = u.shape[0]
    nc = T // CHUNK

    def body(dt_ref, bias_ref, a_ref, sf_ref, tf_ref, sb_ref, tb_ref, ddt_ref, da_ref, dbias_ref):
        @pl.when(pl.program_id(0) == 0)
        def _():
            da_ref[...] = jnp.zeros_like(da_ref)
            dbias_ref[...] = jnp.zeros_like(dbias_ref)

        lane = lax.broadcasted_iota(jnp.int32, (CHUNK, HP), 1)
        pre = dt_ref[...] + bias_ref[0:1, :]
        dt = _softplus(pre)
        a = -jnp.exp(a_ref[0:1, :])
        ddt_x, dcs, dtot = jnp.zeros((CHUNK, HP), F32), jnp.zeros((CHUNK, HP), F32), jnp.zeros((8, HP), F32)
        for b in range(N_HB):
            s_ref, t_ref, g = (sf_ref, tf_ref, b) if b < SSM_GROUPS else (sb_ref, tb_ref, b - SSM_GROUPS)
            mine = (lane >= HG * b) & (lane < HG * (b + 1))

            def up(v):
                return pltpu.roll(v, HG * b, 1) if b else v

            ddt_x = ddt_x + jnp.where(mine, up(s_ref[g, :, 0:HP]), 0.0)
            dcs = dcs + jnp.where(mine, up(s_ref[g, :, HP:2 * HP]), 0.0)
            dtot = dtot + jnp.where(mine[0:8], up(t_ref[g, 0]), 0.0)
        tri_f = jnp.where(_tri(False, transpose=True), 1.0, 0.0).astype(F32)
        tri_b = jnp.where(_tri(True, transpose=True), 1.0, 0.0).astype(F32)
        dda = jnp.where(lane < SSM_HEADS, jnp.dot(tri_f, dcs, precision=HI, preferred_element_type=F32),
                        jnp.dot(tri_b, dcs, precision=HI, preferred_element_type=F32)) + dtot[0:1, :]
        dpre = (ddt_x + dda * a) * jax.nn.sigmoid(pre)
        ddt_ref[...] = jnp.where(lane < 2 * SSM_HEADS, dpre, 0.0)
        dbias_ref[...] += jnp.broadcast_to(jnp.sum(dpre, axis=0, keepdims=True), (8, HP))
        da_ref[...] += jnp.broadcast_to(jnp.sum(dda * dt, axis=0, keepdims=True) * a, (8, HP))

    vec = pl.BlockSpec((8, HP), lambda c: (0, 0))
    sel = pl.BlockSpec((SSM_GROUPS, CHUNK, 2 * HP), lambda c: (0, c, 0))
    tot = pl.BlockSpec((SSM_GROUPS, 1, 8, HP), lambda c: (0, c, 0, 0))
    tile = pl.BlockSpec((CHUNK, HP), lambda c: (c, 0))
    return pl.pallas_call(
        body, name="ssd_prep_bwd",
        out_shape=(jax.ShapeDtypeStruct((T, HP), F32), jax.ShapeDtypeStruct((8, HP), F32), jax.ShapeDtypeStruct((8, HP), F32)),
        grid=(nc,), in_specs=[pl.BlockSpec((CHUNK, HP), lambda c: (c, DT_BLK)), vec, vec, sel, tot, sel, tot],
        out_specs=(tile, vec, vec),
        compiler_params=pltpu.CompilerParams(dimension_semantics=("arbitrary",)),
    )(u, bias8, alog8, dsel_f, dtot_f, dsel_b, dtot_b)


def _ssm_combine_fwd(y_scans, act, u, dskip, gain):
    T = y_scans.shape[0]
    tt = _pick(T, 512)

    def body(ys_ref, x_ref, z_ref, ds_ref, g_ref, y_ref, m_ref):
        y = ys_ref[...] + ds_ref[...] * x_ref[...]
        y2 = y * _silu(z_ref[...])
        r = lax.rsqrt(jnp.mean(y2 * y2, axis=-1, keepdims=True) + EPS)
        y_ref[...] = y
        m_ref[...] = (y2 * r * g_ref[...]).astype(m_ref.dtype)

    blk = pl.BlockSpec((tt, GW), lambda i, g: (i, g))
    vec = pl.BlockSpec((1, GW), lambda i, g: (0, g))
    return pl.pallas_call(
        body, name="ssm_combine_fwd",
        out_shape=(jax.ShapeDtypeStruct((T, D_INNER), F32), jax.ShapeDtypeStruct((T, D_INNER), BF)),
        grid=(T // tt, SSM_GROUPS), in_specs=[blk, blk, blk, vec, vec], out_specs=(blk, blk),
    )(y_scans, act, u, dskip, gain)


def _ssm_combine_bwd(dm, y, act, u, gain):
    T = y.shape[0]
    tt = _pick(T, 512)

    def body(dm_ref, y_ref, x_ref, z_ref, g_ref, dy_ref, dz_ref, dg_ref, dsk_ref):
        z = z_ref[...]
        y = y_ref[...]
        x = x_ref[...]
        sz = _silu(z)
        y2 = y * sz
        r = lax.rsqrt(jnp.mean(y2 * y2, axis=-1, keepdims=True) + EPS)
        d = dm_ref[...]
        gd = d * g_ref[...]
        dy2 = r * gd - y2 * (r * r * r) * jnp.mean(gd * y2, axis=-1, keepdims=True)
        dy = dy2 * sz
        dy_ref[...] = dy
        dz_ref[...] = (dy2 * y * _dsilu(z)).astype(dz_ref.dtype)

        @pl.when(pl.program_id(1) == 0)
        def _():
            dg_ref[...] = jnp.zeros_like(dg_ref)
            dsk_ref[...] = jnp.zeros_like(dsk_ref)

        dg_ref[...] += jnp.broadcast_to(jnp.sum(d * y2 * r, axis=0, keepdims=True), dg_ref.shape)
        lane_sum = jnp.broadcast_to(jnp.sum(dy * x, axis=0, keepdims=True), (8, GW))
        src = lax.broadcasted_iota(jnp.int32, (GW, HP), 0)
        head = lax.broadcasted_iota(jnp.int32, (GW, HP), 1)
        to_head = jnp.where((src >= PH * head) & (src < PH * (head + 1)), 1.0, 0.0).astype(F32)
        dsk_ref[...] += jnp.dot(lane_sum, to_head, precision=HI, preferred_element_type=F32)

    blk = pl.BlockSpec((tt, GW), lambda g, i: (i, g))
    vec = pl.BlockSpec((1, GW), lambda g, i: (0, g))
    acc = pl.BlockSpec((8, GW), lambda g, i: (0, g))
    return pl.pallas_call(
        body, name="ssm_combine_bwd",
        out_shape=(jax.ShapeDtypeStruct((T, D_INNER), F32), jax.ShapeDtypeStruct((T, D_INNER), BF),
                   jax.ShapeDtypeStruct((8, D_INNER), F32), jax.ShapeDtypeStruct((8, SSM_GROUPS * HP), F32)),
        grid=(SSM_GROUPS, T // tt), in_specs=[blk, blk, blk, blk, vec],
        out_specs=(blk, blk, acc, pl.BlockSpec((8, HP), lambda g, i: (0, g))),
        compiler_params=pltpu.CompilerParams(dimension_semantics=("parallel", "arbitrary")),
    )(dm, y, act, u, gain)


def _loss_head(y, target):
    T, D = y.shape
    tt = _pick(T, 512)

    def body(y_ref, t_ref, dy_ref, dyb_ref, l_ref):
        e = y_ref[...] - t_ref[...]
        dy_ref[...] = e * (1.0 / D)
        dyb_ref[...] = (e * (1.0 / D)).astype(dyb_ref.dtype)

        @pl.when(pl.program_id(0) == 0)
        def _():
            l_ref[...] = jnp.zeros_like(l_ref)

        l_ref[...] += jnp.sum(e * e) * (0.5 / D)

    blk = pl.BlockSpec((tt, D), lambda i: (i, 0))
    return pl.pallas_call(
        body, name="loss_head",
        out_shape=(jax.ShapeDtypeStruct((T, D), F32), jax.ShapeDtypeStruct((T, D), BF), jax.ShapeDtypeStruct((8, 128), F32)),
        grid=(T // tt,), in_specs=[blk, blk], out_specs=(blk, blk, pl.BlockSpec((8, 128), lambda i: (0, 0))),
        compiler_params=pltpu.CompilerParams(dimension_semantics=("arbitrary",)),
    )(y, target)


def _adamw(w, g, m, v, *, name):
    R, C = w.shape
    cap = max(8, (1 << 18) // C)
    tr = R
    if R % 8 == 0:
        tr = 8
        for cand in range(8, min(R, cap) + 1, 8):
            if R % cand == 0:
                tr = cand

    def body(w_ref, g_ref, m_ref, v_ref, d_ref, nm_ref, nv_ref):
        gg = g_ref[...]
        nm = ADAM_B1 * m_ref[...] + (1.0 - ADAM_B1) * gg
        nv = ADAM_B2 * v_ref[...] + (1.0 - ADAM_B2) * jnp.square(gg)
        m_hat = nm / (1.0 - ADAM_B1 ** ADAM_STEP)
        v_hat = nv / (1.0 - ADAM_B2 ** ADAM_STEP)
        d_ref[...] = -ADAM_LR * (m_hat / (jnp.sqrt(v_hat) + ADAM_EPS) + ADAM_WD * w_ref[...])
        nm_ref[...] = nm
        nv_ref[...] = nv

    blk = pl.BlockSpec((tr, C), lambda i: (i, 0))
    return pl.pallas_call(
        body, name=name, out_shape=(jax.ShapeDtypeStruct((R, C), F32),) * 3, grid=(R // tr,),
        in_specs=[blk] * 4, out_specs=(blk,) * 3,
    )(w, g, m, v)


ANY = pl.BlockSpec(memory_space=pl.ANY)


def _chip_peers():
    x, y, c = lax.axis_index("x"), lax.axis_index("y"), lax.axis_index("c")
    return x, y, c, [(1 - x, y), (x, 1 - y), (1 - x, 1 - y)]


def _half_rows(c, rh):
    return pl.ds(pl.multiple_of(c * rh, 16), rh)


def _my_chip():
    return 2 * lax.axis_index("x") + lax.axis_index("y")


def _gather_chips(wb, wf):
    rh = wb.shape[0] // 2
    rq = rh // 2

    def body(wb_ref, wf_ref, ob_ref, of_ref, send_sems, recv_sems):
        x, y, c, peers = _chip_peers()
        nbr_x, nbr_y = peers[0], peers[1]
        me, chip_x, chip_y, chip_d = 2 * x + y, 2 * (1 - x) + y, 2 * x + (1 - y), 2 * (1 - x) + (1 - y)

        def quarter(core, b):
            return pl.ds(pl.multiple_of(core * rh + b * rq, 16), rq)

        ici = [(0, nbr_x, me, 0, chip_x), (1, nbr_y, me, 1, chip_y), (2, nbr_y, me, 0, chip_y), (3, nbr_x, me, 1, chip_x),
               (4, nbr_y, chip_x, 0, chip_d), (5, nbr_x, chip_y, 1, chip_d)]

        def ici_copy(k, to, slot, b, own):
            rows = quarter(c, b)
            return pltpu.make_async_remote_copy(
                src_ref=wb_ref.at[rows] if own else ob_ref.at[slot, rows], dst_ref=ob_ref.at[slot, rows],
                send_sem=send_sems.at[k], recv_sem=recv_sems.at[k], device_id=(to[0], to[1], c), device_id_type=MESH)

        def to_sibling(k, slot, b, core):
            rows = quarter(core, b)
            return pltpu.make_async_remote_copy(
                src_ref=ob_ref.at[slot, rows], dst_ref=ob_ref.at[slot, rows], send_sem=send_sems.at[6 + k],
                recv_sem=recv_sems.at[6 + k], device_id=(x, y, 1 - c), device_id_type=MESH)

        def small_copy(k, slot):
            px, py = peers[k]
            return pltpu.make_async_remote_copy(
                src_ref=wf_ref, dst_ref=of_ref.at[slot], send_sem=send_sems.at[12 + k], recv_sem=recv_sems.at[12 + k],
                device_id=(px, py, c), device_id_type=MESH)

        sends = [ici_copy(k, to, slot, b, True) for k, to, slot, b, _ in ici[:4]] + [small_copy(k, me) for k in range(3)]
        for cp in sends:
            cp.start()
        for k, to, slot, b, arrives in ici:
            ici_copy(k, to, arrives, b, False).wait_recv()
            passed = [to_sibling(k, arrives, b, c)]
            if k < 2:
                passed.append(ici_copy(*ici[4 + k][:4], False))
            for cp in passed:
                cp.start()
            sends += passed
        for k, to, slot, b, arrives in ici:
            to_sibling(k, arrives, b, 1 - c).wait_recv()
        chip_of = [chip_x, chip_y, chip_d]
        for k in range(3):
            small_copy(k, chip_of[k]).wait_recv()
        for cp in sends:
            cp.wait_send()

    ob, of = pl.pallas_call(
        body, name="gather_weights",
        out_shape=(jax.ShapeDtypeStruct((4,) + wb.shape, wb.dtype), jax.ShapeDtypeStruct((4,) + wf.shape, wf.dtype)),
        in_specs=[ANY, ANY], out_specs=(ANY, ANY),
        scratch_shapes=[pltpu.SemaphoreType.DMA((15,)), pltpu.SemaphoreType.DMA((15,))],
    )(wb, wf)
    me = _my_chip()
    return lax.dynamic_update_slice(ob, wb[None], (me, 0, 0)), lax.dynamic_update_slice(of, wf[None], (me, 0, 0))


def _halves_to_sibling(gp):
    rh = gp.shape[1] // 2

    def body(gp_ref, o_ref, send_sem, recv_sem):
        x, y, c = lax.axis_index("x"), lax.axis_index("y"), lax.axis_index("c")
        cp = pltpu.make_async_remote_copy(src_ref=gp_ref.at[:, _half_rows(1 - c, rh), :], dst_ref=o_ref, send_sem=send_sem,
                                          recv_sem=recv_sem, device_id=(x, y, 1 - c), device_id_type=MESH)
        cp.start()
        cp.wait()

    return pl.pallas_call(
        body, name="halves_to_sibling", out_shape=jax.ShapeDtypeStruct((gp.shape[0], rh, gp.shape[2]), gp.dtype),
        in_specs=[ANY], out_specs=ANY, scratch_shapes=[pltpu.SemaphoreType.DMA, pltpu.SemaphoreType.DMA],
    )(gp)


def _row_tile(rows, cap=1024):
    tr = 16
    for cand in range(16, cap + 1, 16):
        if rows % cand == 0:
            tr = cand
    return tr


def _add_halves(gp, sib, core):
    n, rh, C = sib.shape
    tr = _row_tile(rh)
    nt = rh // tr

    def body(c_ref, g_ref, s_ref, o_ref):
        o_ref[...] = (g_ref[...].astype(F32) + s_ref[...].astype(F32)).astype(o_ref.dtype)

    blk = pl.BlockSpec((1, tr, C), lambda j, i, c: (j, i, 0))
    return pl.pallas_call(
        body, name="add_halves", out_shape=jax.ShapeDtypeStruct(sib.shape, sib.dtype),
        grid_spec=pltpu.PrefetchScalarGridSpec(
            num_scalar_prefetch=1, grid=(n, nt),
            in_specs=[pl.BlockSpec((1, tr, C), lambda j, i, c: (j, c[0] * nt + i, 0)), blk], out_specs=blk),
    )(core, gp, sib)


def _join_halves(buf):
    rh = buf.shape[0] // 2

    def body(in_ref, o_ref, send_sem, recv_sem):
        x, y, c = lax.axis_index("x"), lax.axis_index("y"), lax.axis_index("c")

        def copy(rows):
            return pltpu.make_async_remote_copy(src_ref=o_ref.at[rows], dst_ref=o_ref.at[rows], send_sem=send_sem,
                                                recv_sem=recv_sem, device_id=(x, y, 1 - c), device_id_type=MESH)

        send = copy(_half_rows(c, rh))
        send.start()
        copy(_half_rows(1 - c, rh)).wait_recv()
        send.wait_send()

    return pl.pallas_call(
        body, name="join_halves", out_shape=jax.ShapeDtypeStruct(buf.shape, buf.dtype),
        in_specs=[ANY], out_specs=ANY, input_output_aliases={0: 0},
        scratch_shapes=[pltpu.SemaphoreType.DMA, pltpu.SemaphoreType.DMA],
    )(buf)


def _exchange_near(gp):
    rq = gp.shape[1] // 2

    def body(gp_ref, out_ref, send_sems, recv_sems):
        x, y, c, peers = _chip_peers()
        chip_x, chip_y, chip_d = 2 * (1 - x) + y, 2 * x + (1 - y), 2 * (1 - x) + (1 - y)
        plan = [(peers[0], chip_x, 0), (peers[0], chip_d, 0), (peers[1], chip_y, 1), (peers[1], chip_d, 1)]
        copies = [pltpu.make_async_remote_copy(
            src_ref=gp_ref.at[slot, pl.ds(b * rq, rq)], dst_ref=out_ref.at[k], send_sem=send_sems.at[k],
            recv_sem=recv_sems.at[k], device_id=(to[0], to[1], c), device_id_type=MESH) for k, (to, slot, b) in enumerate(plan)]
        for cp in copies:
            cp.start()
        for cp in copies:
            cp.wait_recv()
        for cp in copies:
            cp.wait_send()

    return pl.pallas_call(
        body, name="exchange_grads_near", out_shape=jax.ShapeDtypeStruct((4, rq, gp.shape[2]), gp.dtype),
        in_specs=[ANY], out_specs=ANY, scratch_shapes=[pltpu.SemaphoreType.DMA((4,)), pltpu.SemaphoreType.DMA((4,))],
    )(gp)


def _add_near(gp, near, chips):
    _, rq, C = near.shape
    tr = _row_tile(rq)
    nt = rq // tr

    def body(ch_ref, mine_a, mine_b, on_a, on_b, near_ref, part_ref, on_ref):
        part_ref[0] = mine_a[0].astype(F32) + near_ref[0].astype(F32)
        part_ref[1] = mine_b[0].astype(F32) + near_ref[2].astype(F32)
        on_ref[0] = (on_a[0].astype(F32) + near_ref[1].astype(F32)).astype(on_ref.dtype)
        on_ref[1] = (on_b[0].astype(F32) + near_ref[3].astype(F32)).astype(on_ref.dtype)

    def slot(which, b):
        return pl.BlockSpec((1, tr, C), lambda i, ch: (ch[which], b * nt + i, 0))

    return pl.pallas_call(
        body, name="add_near",
        out_shape=(jax.ShapeDtypeStruct((2, rq, C), F32), jax.ShapeDtypeStruct((2, rq, C), near.dtype)),
        grid_spec=pltpu.PrefetchScalarGridSpec(
            num_scalar_prefetch=1, grid=(nt,),
            in_specs=[slot(0, 0), slot(0, 1), slot(2, 0), slot(1, 1), pl.BlockSpec((4, tr, C), lambda i, ch: (0, i, 0))],
            out_specs=(pl.BlockSpec((2, tr, C), lambda i, ch: (0, i, 0)),) * 2),
    )(chips, gp, gp, gp, gp, near)


def _exchange_far(on):
    def body(on_ref, out_ref, send_sems, recv_sems):
        x, y, c, peers = _chip_peers()
        copies = [pltpu.make_async_remote_copy(
            src_ref=on_ref.at[k], dst_ref=out_ref.at[k], send_sem=send_sems.at[k], recv_sem=recv_sems.at[k],
            device_id=(to[0], to[1], c), device_id_type=MESH) for k, to in enumerate((peers[1], peers[0]))]
        for cp in copies:
            cp.start()
        for cp in copies:
            cp.wait_recv()
        for cp in copies:
            cp.wait_send()

    return pl.pallas_call(
        body, name="exchange_grads_far", out_shape=jax.ShapeDtypeStruct(on.shape, on.dtype),
        in_specs=[ANY], out_specs=ANY, scratch_shapes=[pltpu.SemaphoreType.DMA((2,)), pltpu.SemaphoreType.DMA((2,))],
    )(on)


def _add_far(part, far, core):
    _, rq, C = part.shape
    tr = _row_tile(rq)
    nt = rq // tr

    def body(c_ref, p_ref, f_ref, o_ref):
        o_ref[...] = p_ref[0] + f_ref[0].astype(F32)

    blk = pl.BlockSpec((1, tr, C), lambda b, i, c: (b, i, 0))
    return pl.pallas_call(
        body, name="add_far", out_shape=jax.ShapeDtypeStruct((4 * rq, C), F32),
        grid_spec=pltpu.PrefetchScalarGridSpec(
            num_scalar_prefetch=1, grid=(2, nt), in_specs=[blk, blk],
            out_specs=pl.BlockSpec((tr, C), lambda b, i, c: ((2 * c[0] + b) * nt + i, 0))),
    )(core, part, far)


N_DEV = 8


def _allreduce_small(p):
    rs = p.shape[0]

    def body(x_ref, sum_ref, all_ref, send_sems, recv_sems, local_sem):
        x, y, c = lax.axis_index("x"), lax.axis_index("y"), lax.axis_index("c")
        me, sibling = (x, y, c), (x, y, 1 - c)
        chips = [(1 - x, y), (x, 1 - y), (1 - x, 1 - y)]

        def rows(px, py, pc):
            return all_ref.at[pl.ds((4 * px + 2 * py + pc) * rs, rs), :]

        def copy(k, block, to, src=None):
            return pltpu.make_async_remote_copy(
                src_ref=rows(*block) if src is None else src, dst_ref=rows(*block),
                send_sem=send_sems.at[k], recv_sem=recv_sems.at[k], device_id=to, device_id_type=MESH)

        mine = pltpu.make_async_copy(x_ref, rows(*me), local_sem)
        mine.start()
        first = [copy(0, me, sibling, src=x_ref)]
        first += [copy(1 + j, me, (*chip, c), src=x_ref) for j, chip in enumerate(chips)]
        for cp in first:
            cp.start()
        passed = [copy(4 + j, (*chip, c), sibling) for j, chip in enumerate(chips)]
        for j, chip in enumerate(chips):
            copy(1 + j, (*chip, c), me).wait_recv()
            passed[j].start()
        copy(0, sibling, me).wait_recv()
        for j, chip in enumerate(chips):
            copy(4 + j, (*chip, 1 - c), me).wait_recv()
        for cp in first + passed:
            cp.wait_send()
        mine.wait()
        acc = all_ref[0:rs, :]
        for d in range(1, N_DEV):
            acc = acc + all_ref[d * rs:(d + 1) * rs, :]
        sum_ref[...] = acc

    vmem = pl.BlockSpec(memory_space=pltpu.VMEM)
    return pl.pallas_call(
        body, name="allreduce_small", out_shape=jax.ShapeDtypeStruct((rs, 128), F32),
        in_specs=[vmem], out_specs=vmem,
        scratch_shapes=[pltpu.VMEM((N_DEV * rs, 128), F32), pltpu.SemaphoreType.DMA((7,)), pltpu.SemaphoreType.DMA((7,)),
                        pltpu.SemaphoreType.DMA],
    )(p)


WEIGHTS = ('ffn1_norm', 'ffn1_w_gate', 'ffn1_w_up', 'ffn1_w_down', 'mix_norm', 'w_in', 'q_a_norm', 'w_q_b',
           'kv_a_norm', 'w_kv_b', 'q_head_norm', 'k_head_norm', 'conv_w', 'conv_b', 'a_log_fwd', 'a_log_bwd',
           'dt_bias_fwd', 'dt_bias_bwd', 'd_skip', 'ssm_norm', 'w_attn_branch', 'w_ssm_branch', 'w_out',
           'ffn2_norm', 'ffn2_w_gate', 'ffn2_w_up', 'ffn2_w_down')
PACKED = (('ffn1_w_gate', (D_MODEL, D_FF), 1), ('ffn1_w_up', (D_MODEL, D_FF), 1), ('ffn1_w_down', (D_FF, D_MODEL), 0),
          ('w_in', (D_MODEL, sum(IN_SPLITS)), 1), ('w_q_b', (Q_LORA, N_HEADS * QK_HEAD), 1),
          ('w_kv_b', (KV_LORA, N_HEADS * (QK_NOPE + V_HEAD)), 1),
          ('w_attn_branch', (N_HEADS * V_HEAD, D_MODEL), 0), ('w_ssm_branch', (D_INNER, D_MODEL), 0),
          ('w_out', (D_MODEL, D_MODEL), 0),
          ('ffn2_w_gate', (D_MODEL, D_FF), 1), ('ffn2_w_up', (D_MODEL, D_FF), 1), ('ffn2_w_down', (D_FF, D_MODEL), 0))
PACK_W = 1024
N_CHIPS = 4
SMALL = (('ffn1_norm', 1024), ('mix_norm', 1024), ('q_a_norm', 384), ('kv_a_norm', 256), ('q_head_norm', 96),
         ('k_head_norm', 96), ('conv_b', 3072), ('a_log_fwd', 32), ('a_log_bwd', 32), ('dt_bias_fwd', 32),
         ('dt_bias_bwd', 32), ('d_skip', 32), ('ssm_norm', 2048), ('ffn2_norm', 1024),
         ('conv_w', CONV_WIDTH * XBC_DIM), ('loss', 1))


TRANSPOSED = ('ffn1_w_gate', 'ffn1_w_up', 'w_in', 'ffn2_w_gate', 'ffn2_w_up')


def _stored(name, a):
    return a.T if name in TRANSPOSED else a


def _shard_shape(name, shape, axis):
    sh = tuple(s // N_CHIPS if a == axis else s for a, s in enumerate(shape))
    return sh[::-1] if name in TRANSPOSED else sh


def _by_rows(name, axis):
    return name in TRANSPOSED or axis == 0


def _pack_layout():
    out, r = {}, 0
    for name, shape, axis in PACKED:
        n = math.prod(shape) // N_CHIPS // PACK_W
        out[name] = (r, n)
        r += n
    return out, -(-r // 64) * 64


def _pack(shards):
    layout, rows = _pack_layout()
    parts = [shards[name].reshape(-1, PACK_W) for name, _, _ in PACKED]
    parts.append(jnp.zeros((rows - sum(p.shape[0] for p in parts), PACK_W), parts[0].dtype))
    return jnp.concatenate(parts, axis=0)


def _unpack(packed):
    layout, _ = _pack_layout()
    return {name: packed[layout[name][0]:layout[name][0] + layout[name][1]].reshape(_shard_shape(name, shape, axis))
            for name, shape, axis in PACKED}


def _full_from_slots(slots):
    layout, _ = _pack_layout()
    out = {}
    for name, shape, axis in PACKED:
        r, n = layout[name]
        if _by_rows(name, axis):
            out[name] = slots[:, r:r + n].reshape(N_CHIPS * n, PACK_W)
        else:
            sh = _shard_shape(name, shape, axis)
            out[name] = jnp.concatenate([slots[j, r:r + n].reshape(sh) for j in range(N_CHIPS)], axis=axis)
    return out


def _slots_from_full(full):
    layout, rows = _pack_layout()
    parts = []
    for name, shape, axis in PACKED:
        r, n = layout[name]
        if _by_rows(name, axis):
            parts.append(full[name].reshape(N_CHIPS, n, PACK_W))
        else:
            size = shape[axis] // N_CHIPS
            parts.append(jnp.stack([lax.slice_in_dim(full[name], j * size, (j + 1) * size, axis=axis).reshape(n, PACK_W)
                                    for j in range(N_CHIPS)]))
    parts.append(jnp.zeros((N_CHIPS, rows - sum(p.shape[1] for p in parts), PACK_W), parts[0].dtype))
    return jnp.concatenate(parts, axis=1)


def _pack_small(vals):
    parts = []
    for name, n in SMALL:
        pad = -(-n // 128) * 128 - n
        parts.append(jnp.pad(vals[name].reshape(-1).astype(F32), (0, pad)).reshape(-1, 128))
    rows = sum(p.shape[0] for p in parts)
    parts.append(jnp.zeros((-(-rows // 8) * 8 - rows, 128), F32))
    return jnp.concatenate(parts, axis=0)


def _unpack_small(packed):
    out, r = {}, 0
    for name, n in SMALL:
        k = -(-n // 128)
        out[name] = packed[r:r + k].reshape(-1)[:n]
        r += k
    return out


def _pad_heads(w, axis, per_head, lo, hi):
    shape = w.shape
    w = w.reshape(shape[:axis] + (N_HEADS, per_head) + shape[axis + 1:])
    w = lax.slice_in_dim(w, lo, hi, axis=axis + 1)
    pad = [(0, 0)] * w.ndim
    pad[axis + 1] = (0, HP - (hi - lo))
    w = jnp.pad(w, pad)
    return w.reshape(shape[:axis] + (N_HEADS * HP,) + shape[axis + 1:])


def _unpad_heads(w, axis, keep):
    shape = w.shape
    w = w.reshape(shape[:axis] + (N_HEADS, HP) + shape[axis + 1:])
    return lax.slice_in_dim(w, 0, keep, axis=axis + 1)


def _pad_w_in(wt):
    o = [0]
    for s in IN_SPLITS:
        o.append(o[-1] + s)
    cq, ckv, kpe, z, xbc, dtf, dtb, ga, gb = [wt[o[i]:o[i + 1]] for i in range(len(IN_SPLITS))]
    kpe_pad = jnp.pad(kpe, ((QK_NOPE, HP - QK_HEAD), (0, 0)))
    dt_pad = jnp.pad(jnp.concatenate([dtf, dtb], axis=0), ((0, HP - 2 * SSM_HEADS), (0, 0)))
    return jnp.concatenate([z, ga, gb, xbc, cq, ckv, kpe_pad, dt_pad], axis=0)


def _unpad_w_in(gt):
    z, ga, gb, xbc = gt[U_Z:U_GA], gt[U_GA:U_GB], gt[U_GB:U_XBC], gt[U_XBC:U_SMALL]
    s = gt[U_SMALL:]
    cq, ckv = s[S_CQ:S_CKV], s[S_CKV:S_KPE]
    kpe = s[S_KPE + QK_NOPE:S_KPE + QK_HEAD]
    dtf, dtb = s[S_DT:S_DT + SSM_HEADS], s[S_DT + SSM_HEADS:S_DT + 2 * SSM_HEADS]
    return jnp.concatenate([cq, ckv, kpe, z, xbc, dtf, dtb, ga, gb], axis=0)


def _lanes128(parts):
    row = jnp.concatenate([p.reshape(-1) for p in parts])
    return jnp.pad(row, (0, HP - row.shape[0])).reshape(1, HP)


FF_TILE = D_FF // 2
WGRAD = BF


def _ffn_fwd(x, g, wg_t, wu_t, wd, tag):
    h = _rms_fwd(x, g, name=tag + "_norm")
    gate, up, act = _mm([h], [wg_t, wu_t], name=tag + "_up", tb=True, out_dtypes=(BF, BF, BF), tm=512, tn=FF_TILE,
                        epilogue=lambda a, b: (a, b, _silu(a) * b))
    out = _mm([act], [wd], name=tag + "_down", extras=[x], epilogue=lambda acc, r: (r + 0.5 * acc,))
    return out, (h, gate, up, act)


def _ffn_bwd(dout, dout_bf, x, g, wg_t, wu_t, wd, saved, tag):
    h, gate, up, act = saved

    def swiglu_bwd(acc, a, b):
        a, b, half = a.astype(F32), b.astype(F32), 0.5 * acc
        s = jax.nn.sigmoid(a)
        return half * b * (s * (1.0 + a * (1.0 - s))), half * (a * s)

    dgate, dup = _mm([dout_bf], [wd], name=tag + "_down_dx", tb=True, extras=[gate, up], out_dtypes=(BF, BF),
                     tm=512, tn=FF_TILE, epilogue=swiglu_bwd)
    dwd = _mm([act], [dout_bf], name=tag + "_down_dw", ta=True, tm=FF_TILE, out_dtypes=(WGRAD,),
              epilogue=lambda acc: (0.5 * acc,))
    dwg_t, dwu_t = _mm([dgate, dup], [h, h], name=tag + "_up_dw", ta=True, separate=True, out_dtypes=(WGRAD, WGRAD),
                       tm=FF_TILE)
    dh = _mm([dgate, dup], [wg_t, wu_t], name=tag + "_up_dx")
    dx, dx_bf, dg = _rms_bwd(dh, x, g, name=tag + "_norm_bwd", add=dout, out_dtypes=(F32, BF))
    return dx, dx_bf, dg, dwg_t, dwu_t, dwd


KPE_BLK = (U_SMALL + S_KPE) // HP
SMALL_BLK = U_SMALL // SMALL_W


def _local_step(x, pos_col, target, W, P):
    T = x.shape[0]
    sig = jax.nn.sigmoid
    x1, ffn1 = _ffn_fwd(x, P["ffn1_norm"], W["wg1"], W["wu1"], W["wd1"], "ffn1")
    h = _rms_fwd(x1, P["mix_norm"], name="mix_norm")
    u = _mm([h], [W["w_in"]], name="in_proj", tb=True, tn=1152)
    cqn = _rms_fwd(u, P["q_a_norm"], name="q_a_norm", blk_w=SMALL_W, blk_idx=SMALL_BLK, off=S_CQ, width=Q_LORA)
    ckvn = _rms_fwd(u, P["kv_a_norm"], name="kv_a_norm", blk_w=SMALL_W, blk_idx=SMALL_BLK, off=S_CKV, width=KV_LORA)
    q_raw = _mm([cqn], [W["wq"]], name="q_proj")
    def with_ones_lane(acc_k, acc_v):
        lane = lax.broadcasted_iota(jnp.int32, acc_v.shape, 1)
        return acc_k, jnp.where((lane & (HP - 1)) == V_HEAD, 1.0, acc_v)

    k_raw, v = _mm([ckvn], [W["wk"], W["wv"]], name="kv_proj", out_dtypes=(F32, BF), epilogue=with_ones_lane)
    rc, rs = _rope_tables(pos_col, P["freq"])
    q = _qk_prep_fwd(q_raw, None, P["q_head_norm"], rc, rs, name="q_prep", out_scale=Q_SCALE)
    k = _qk_prep_fwd(k_raw, u, P["k_head_norm"], rc, rs, name="k_prep", kpe_blk=KPE_BLK)
    o, o_bf, lse = _attn_fwd(q, k, v)
    pre, act = _conv_fwd(u, P["conv_w8"], P["conv_b"])
    scan_cols, scan_rows = _ssd_prep(u, P["dt_bias8"], P["a_log8"])
    y_f, st_f = _ssd_fwd(act, scan_cols, scan_rows, rev=False, name="ssd_fwd_f")
    y_fb, st_b = _ssd_fwd(act, scan_cols, scan_rows, rev=True, name="ssd_fwd_b", add=y_f)
    ysum, m = _ssm_combine_fwd(y_fb, act, u, P["d_skip_lanes"], P["ssm_norm"])
    ab = _mm([o_bf], [W["pa"]], name="attn_branch")
    mb, merged = _mm([m], [W["pb"]], name="ssm_branch", extras=[ab, u, u], extra_offs=(0, U_GA, U_GB), out_dtypes=(F32, BF),
                     epilogue=lambda acc, a, ga, gb: (acc, sig(ga) * a + sig(gb) * acc))
    x2 = _mm([merged], [W["wo"]], name="out_proj", extras=[x1], epilogue=lambda acc, r: (r + acc,))
    y, ffn2 = _ffn_fwd(x2, P["ffn2_norm"], W["wg2"], W["wu2"], W["wd2"], "ffn2")
    dy, dy_bf, loss = _loss_head(y, target)
    dx2, dx2_bf, dg_ffn2, dwg2, dwu2, dwd2 = _ffn_bwd(dy, dy_bf, x2, P["ffn2_norm"], W["wg2"], W["wu2"], W["wd2"], ffn2,
                                                      "ffn2")

    def gate_bwd(dmrg, a, b, ga, gb):
        sa, sb = sig(ga), sig(gb)
        return dmrg * sa, dmrg * sb, dmrg * a * sa * (1.0 - sa), dmrg * b * sb * (1.0 - sb)

    dab, dmb, dga, dgb = _mm([dx2_bf], [W["wo"]], name="out_proj_dx", tb=True, extras=[ab, mb, u, u],
                             extra_offs=(0, 0, U_GA, U_GB), out_dtypes=(BF,) * 4, epilogue=gate_bwd)
    dwo = _mm([merged], [dx2_bf], name="out_proj_dw", ta=True, out_dtypes=(WGRAD,))
    dpa = _mm([o_bf], [dab], name="attn_branch_dw", ta=True, out_dtypes=(WGRAD,))
    do = _mm([dab], [W["pa"]], name="attn_branch_dx", tb=True)
    dpb = _mm([m], [dmb], name="ssm_branch_dw", ta=True, out_dtypes=(WGRAD,))
    dm = _mm([dmb], [W["pb"]], name="ssm_branch_dx", tb=True)
    dyssd, dz, dg_ssm, dskip = _ssm_combine_bwd(dm, ysum, act, u, P["ssm_norm"])
    dxs_f, db_f, dc_f, dsel_f, dtot_f = _ssd_bwd(act, scan_cols, scan_rows, st_f, dyssd, rev=False, name="ssd_bwd_f",
                                                 skip=P["d_skip_lanes"])
    dxs, db, dc, dsel_b, dtot_b = _ssd_bwd(act, scan_cols, scan_rows, st_b, dyssd, rev=True, name="ssd_bwd_b",
                                           add=(dxs_f, db_f, dc_f))
    ddt, dalog, dbias = _ssd_prep_bwd(u, P["dt_bias8"], P["a_log8"], dsel_f, dtot_f, dsel_b, dtot_b)
    dxbc, dconv = [], []
    for tag, col0, part in (("x", 0, dxs), ("b", D_INNER, db), ("c", D_INNER + SSM_GROUPS * D_STATE, dc)):
        dxp, dwp = _conv_bwd([part], pre, u, P["conv_w8"], col0, name="conv_bwd_" + tag)
        dxbc.append(dxp)
        dconv.append(dwp)
    dconv = jnp.concatenate(dconv, axis=1)
    dq, dk, dv = _attn_bwd(q, k, v, do, o, lse)
    dq_raw, dg_qh = _qk_prep_bwd(dq, q_raw, None, P["q_head_norm"], rc, rs, name="q_prep_bwd", in_scale=ATTN_SCALE)
    dk_raw, dg_kh, dkpe = _qk_prep_bwd(dk, k_raw, u, P["k_head_norm"], rc, rs, name="k_prep_bwd", kpe_blk=KPE_BLK,
                                       in_scale=1.0 / LOG2E)
    dwq = _mm([cqn], [dq_raw], name="q_proj_dw", ta=True, out_dtypes=(WGRAD,))
    dcqn = _mm([dq_raw], [W["wq"]], name="q_proj_dx", tb=True)
    dwk, dwv = _mm([ckvn], [dk_raw, dv], name="kv_proj_dw", ta=True, out_dtypes=(WGRAD, WGRAD))
    dckvn = _mm([dk_raw, dv], [W["wk"], W["wv"]], name="kv_proj_dx", tb=True)
    dcq, dg_qa = _rms_bwd(dcqn, u, P["q_a_norm"], name="q_a_norm_bwd", blk_w=SMALL_W, blk_idx=SMALL_BLK, off=S_CQ,
                          width=Q_LORA, out_dtypes=(BF,))
    dckv, dg_kva = _rms_bwd(dckvn, u, P["kv_a_norm"], name="kv_a_norm_bwd", blk_w=SMALL_W, blk_idx=SMALL_BLK,
                            off=S_CKV, width=KV_LORA, out_dtypes=(BF,))
    du = jnp.concatenate([dz, dga, dgb] + dxbc + [dcq, dckv, dkpe.astype(BF), ddt.astype(BF)], axis=1)
    dw_in = _mm([du], [h], name="in_proj_dw", ta=True, tm=1152, out_dtypes=(WGRAD,))
    dh = _mm([du], [W["w_in"]], name="in_proj_dx")
    dx1, dx1_bf, dg_mix = _rms_bwd(dh, x1, P["mix_norm"], name="mix_norm_bwd", add=dx2, out_dtypes=(F32, BF))
    dx, _, dg_ffn1, dwg1, dwu1, dwd1 = _ffn_bwd(dx1, dx1_bf, x, P["ffn1_norm"], W["wg1"], W["wu1"], W["wd1"], ffn1, "ffn1")
    dW = dict(wg1=dwg1, wu1=dwu1, wd1=dwd1, w_in=dw_in, wq=dwq, wk=dwk, wv=dwv, pa=dpa, pb=dpb, wo=dwo,
              wg2=dwg2, wu2=dwu2, wd2=dwd2)
    dP = dict(ffn1_norm=dg_ffn1[0], mix_norm=dg_mix[0], q_a_norm=dg_qa[0], kv_a_norm=dg_kva[0],
              q_head_norm=dg_qh[0, :QK_HEAD], k_head_norm=dg_kh[0, :QK_HEAD], conv_b=dconv[CONV_WIDTH],
              a_log_fwd=dalog[0, :SSM_HEADS], a_log_bwd=dalog[0, SSM_HEADS:2 * SSM_HEADS],
              dt_bias_fwd=dbias[0, :SSM_HEADS], dt_bias_bwd=dbias[0, SSM_HEADS:2 * SSM_HEADS],
              d_skip=dskip[0].reshape(SSM_GROUPS, HP)[:, :HG], ssm_norm=dg_ssm[0], ffn2_norm=dg_ffn2[0],
              conv_w=dconv[:CONV_WIDTH], loss=loss[0, 0])
    return dx, dW, dP


def _prepare(w, conv_w_full):
    kvb = w["w_kv_b"]
    W = dict(wg1=w["ffn1_w_gate"], wu1=w["ffn1_w_up"], wd1=w["ffn1_w_down"], w_in=_pad_w_in(w["w_in"]),
             wq=_pad_heads(w["w_q_b"], 1, QK_HEAD, 0, QK_HEAD),
             wk=_pad_heads(kvb, 1, QK_NOPE + V_HEAD, 0, QK_NOPE),
             wv=_pad_heads(kvb, 1, QK_NOPE + V_HEAD, QK_NOPE, QK_NOPE + V_HEAD),
             pa=_pad_heads(w["w_attn_branch"], 0, V_HEAD, 0, V_HEAD), pb=w["w_ssm_branch"], wo=w["w_out"],
             wg2=w["ffn2_w_gate"], wu2=w["ffn2_w_up"], wd2=w["ffn2_w_down"])
    inv_freq = [1.0 / (ROPE_BASE ** (j / QK_ROPE)) for j in range(0, QK_ROPE, 2)]
    freq = [0.0] * QK_NOPE + inv_freq + inv_freq + [0.0] * (HP - QK_HEAD)
    P = {n: w[n] for n in ("ffn1_norm", "mix_norm", "q_a_norm", "kv_a_norm", "ssm_norm", "ffn2_norm", "conv_b")}
    P.update(q_head_norm=_lanes128([w["q_head_norm"]]), k_head_norm=_lanes128([w["k_head_norm"]]),
             conv_w8=jnp.pad(conv_w_full, ((0, 8 - CONV_WIDTH), (0, 0))),
             dt_bias8=jnp.broadcast_to(_lanes128([w["dt_bias_fwd"], w["dt_bias_bwd"]]), (8, HP)),
             a_log8=jnp.broadcast_to(_lanes128([w["a_log_fwd"], w["a_log_bwd"]]), (8, HP)),
             d_skip_lanes=jnp.repeat(w["d_skip"].reshape(-1), PH).reshape(1, D_INNER),
             freq=jnp.asarray(freq, F32).reshape(1, HP))
    return W, P


def _unprepare(dW):
    dkvb = jnp.concatenate([_unpad_heads(dW["wk"], 1, QK_NOPE), _unpad_heads(dW["wv"], 1, V_HEAD)], axis=2)
    return dict(ffn1_w_gate=dW["wg1"], ffn1_w_up=dW["wu1"], ffn1_w_down=dW["wd1"], w_in=_unpad_w_in(dW["w_in"]),
                w_q_b=_unpad_heads(dW["wq"], 1, QK_HEAD).reshape(Q_LORA, N_HEADS * QK_HEAD),
                w_kv_b=dkvb.reshape(KV_LORA, N_HEADS * (QK_NOPE + V_HEAD)),
                w_attn_branch=_unpad_heads(dW["pa"], 0, V_HEAD).reshape(N_HEADS * V_HEAD, D_MODEL),
                w_ssm_branch=dW["pb"], w_out=dW["wo"],
                ffn2_w_gate=dW["wg2"], ffn2_w_up=dW["wu2"], ffn2_w_down=dW["wd2"])


def kernel(x, positions, ffn1_norm, ffn1_w_gate, ffn1_w_up, ffn1_w_down, mix_norm, w_in, q_a_norm, w_q_b, kv_a_norm, w_kv_b, q_head_norm, k_head_norm, conv_w, conv_b, a_log_fwd, a_log_bwd, dt_bias_fwd, dt_bias_bwd, d_skip, ssm_norm, w_attn_branch, w_ssm_branch, w_out, ffn2_norm, ffn2_w_gate, ffn2_w_up, ffn2_w_down, loss_target, m_ffn1_norm, m_ffn1_w_gate, m_ffn1_w_up, m_ffn1_w_down, m_mix_norm, m_w_in, m_q_a_norm, m_w_q_b, m_kv_a_norm, m_w_kv_b, m_q_head_norm, m_k_head_norm, m_conv_w, m_conv_b, m_a_log_fwd, m_a_log_bwd, m_dt_bias_fwd, m_dt_bias_bwd, m_d_skip, m_ssm_norm, m_w_attn_branch, m_w_ssm_branch, m_w_out, m_ffn2_norm, m_ffn2_w_gate, m_ffn2_w_up, m_ffn2_w_down, v_ffn1_norm, v_ffn1_w_gate, v_ffn1_w_up, v_ffn1_w_down, v_mix_norm, v_w_in, v_q_a_norm, v_w_q_b, v_kv_a_norm, v_w_kv_b, v_q_head_norm, v_k_head_norm, v_conv_w, v_conv_b, v_a_log_fwd, v_a_log_bwd, v_dt_bias_fwd, v_dt_bias_bwd, v_d_skip, v_ssm_norm, v_w_attn_branch, v_w_ssm_branch, v_w_out, v_ffn2_norm, v_ffn2_w_gate, v_ffn2_w_up, v_ffn2_w_down):
    given = dict(locals())
    T = x.shape[1]
    packed_names = [name for name, _, _ in PACKED]

    def two_d(a):
        return a.reshape(a.shape[1], -1) if a.ndim > 2 else a

    def kept(n, a):
        return _stored(n, two_d(a))

    w_loc = {n: kept(n, given[n]) for n in WEIGHTS}
    wb = _pack({n: w_loc[n].astype(BF) for n in packed_names})
    wf = jnp.pad(w_loc["conv_w"], ((0, 8 - CONV_WIDTH), (0, 0)))
    gb, gf = _gather_chips(wb, wf)
    full = _full_from_slots(gb)
    conv_w_full = jnp.concatenate([gf[j, :CONV_WIDTH] for j in range(N_CHIPS)], axis=1)
    full.update({n: w_loc[n] for n in WEIGHTS if n not in full and n != "conv_w"})
    W, P = _prepare(full, conv_w_full)
    dx, dW, dP = _local_step(x.reshape(T, D_MODEL), positions.reshape(T, 1).astype(F32), loss_target.reshape(T, D_MODEL), W, P)
    gp = _slots_from_full(_unprepare(dW))
    core = lax.axis_index("c").astype(jnp.int32).reshape(1)
    both_cores = _add_halves(gp, _halves_to_sibling(gp), core)
    cx, cy = lax.axis_index("x"), lax.axis_index("y")
    chips = jnp.stack([2 * cx + cy, 2 * (1 - cx) + cy, 2 * cx + (1 - cy)]).astype(jnp.int32)
    part, on = _add_near(both_cores, _exchange_near(both_cores), chips)
    grads = _unpack(_join_halves(_add_far(part, _exchange_far(on), core)))
    small = _unpack_small(_allreduce_small(_pack_small(dP)))
    grads.update({n: small[n].reshape(1, -1) for n, _ in SMALL if n not in ("conv_w", "loss")})
    grads["conv_w"] = lax.dynamic_slice_in_dim(small["conv_w"].reshape(CONV_WIDTH, XBC_DIM), _my_chip() * (XBC_DIM // N_CHIPS),
                                               XBC_DIM // N_CHIPS, axis=1)
    out_g, out_d, out_m, out_v = [], [], [], []
    for n in WEIGHTS:
        shape = given[n].shape
        delta, new_m, new_v = _adamw(w_loc[n], grads[n], kept(n, given["m_" + n]), kept(n, given["v_" + n]), name="adamw_" + n)
        for outs, a in ((out_g, grads[n]), (out_d, delta), (out_m, new_m), (out_v, new_v)):
            outs.append(_stored(n, a).reshape(shape))
    return (small["loss"].reshape(()), dx.reshape(x.shape), *out_g, *out_d, *out_m, *out_v)
```

```python
import functools
import math

import jax
import jax.numpy as jnp
from jax import lax
from jax.experimental import pallas as pl
from jax.experimental.pallas import tpu as pltpu

BF = jnp.bfloat16
F32 = jnp.float32
HI = lax.Precision.HIGHEST
MESH = pl.DeviceIdType.MESH

D_MODEL = 1024
D_FF = 2816
EPS = 1e-6
N_HEADS = 16
QK_NOPE = 64
QK_ROPE = 32
QK_HEAD = 96
V_HEAD = 64
Q_LORA = 384
KV_LORA = 256
ROPE_BASE = 10000.0
D_INNER = 2048
SSM_HEADS = 32
SSM_GROUPS = 4
D_STATE = 128
CONV_WIDTH = 5
CHUNK = 128
XBC_DIM = 3072
HP = 128
GW = D_INNER // SSM_GROUPS
HG = SSM_HEADS // SSM_GROUPS
PH = 64
U_Z, U_GA, U_GB, U_XBC, U_SMALL = 0, 2048, 3072, 4096, 7168
S_CQ, S_CKV, S_KPE, S_DT, SMALL_W = 0, 384, 640, 768, 896
U_PAD = U_SMALL + SMALL_W
IN_SPLITS = (Q_LORA, KV_LORA, QK_ROPE, D_INNER, XBC_DIM, SSM_HEADS, SSM_HEADS, D_MODEL, D_MODEL)

ADAM_LR = 0.001
ADAM_B1 = 0.9
ADAM_B2 = 0.999
ADAM_EPS = 1e-08
ADAM_WD = 0.01
ADAM_STEP = 10

V7X_VMEM_BYTES = 64 << 20
MM_VMEM_BUDGET = V7X_VMEM_BYTES * 5 // 8

NT = (((1,), (1,)), ((), ()))
TN = (((0,), (0,)), ((), ()))


def _pick(n, pref):
    best = None
    d = 128
    while d <= min(n, pref):
        if n % d == 0:
            best = d
        d += 128
    return best if best is not None else n


def _silu(x):
    return x * jax.nn.sigmoid(x)


def _dsilu(x):
    s = jax.nn.sigmoid(x)
    return s * (1.0 + x * (1.0 - s))


def _softplus(x):
    return jnp.maximum(x, 0.0) + jnp.log(1.0 + jnp.exp(-jnp.abs(x)))


def _mm(As, Bs, *, name, ta=False, tb=False, out_dtypes=(F32,), epilogue=None, extras=(), extra_offs=None,
        tm=1024, tn=512, tk=None, separate=False):
    As, Bs, extras = list(As), list(Bs), list(extras)
    a0, b0 = As[0], Bs[0]
    M, K = (a0.shape[1], a0.shape[0]) if ta else a0.shape
    N = b0.shape[0] if tb else b0.shape[1]
    tm, tn = _pick(M, tm), _pick(N, tn)
    n_a, n_b, n_e, n_o = len(As), len(Bs), len(extras), len(out_dtypes)
    n_res = n_b if n_a == 1 or separate else 1

    def vmem_bytes(k_tile):
        blocks = sum(tm * k_tile * a.dtype.itemsize for a in As) + sum(k_tile * tn * b.dtype.itemsize for b in Bs)
        tiles = tm * tn * (sum(jnp.dtype(dt).itemsize for dt in out_dtypes) + sum(e.dtype.itemsize for e in extras))
        return 2 * (blocks + tiles) + 2 * n_res * tm * tn * 4

    if tk is None:
        tk = K
        while vmem_bytes(tk) > MM_VMEM_BUDGET and tk > 128:
            tk = _pick(K, tk - 128)
    else:
        tk = _pick(K, tk)
    nk = K // tk
    n_acc = n_res if nk > 1 else 0
    if extra_offs is None:
        extra_offs = (0,) * n_e
    dn = (((0,) if ta else (1,), (1,) if tb else (0,)), ((), ()))
    bytes_a = sum(a.size * a.dtype.itemsize for a in As)
    bytes_b = sum(b.size * b.dtype.itemsize for b in Bs)
    n_outer = (N // tn) * bytes_a + bytes_b < (M // tm) * bytes_b + bytes_a

    def products(a_refs, b_refs):
        if n_a == 1:
            a = a_refs[0][...].astype(BF)
            return [lax.dot_general(a, b[...].astype(BF), dn, preferred_element_type=F32) for b in b_refs]
        if separate:
            return [lax.dot_general(a[...].astype(BF), b[...].astype(BF), dn, preferred_element_type=F32)
                    for a, b in zip(a_refs, b_refs)]
        total = None
        for a, b in zip(a_refs, b_refs):
            p = lax.dot_general(a[...].astype(BF), b[...].astype(BF), dn, preferred_element_type=F32)
            total = p if total is None else total + p
        return [total]

    def finish(accs, e_refs, o_refs):
        ex = [e[...] for e in e_refs]
        outs = epilogue(*accs, *ex) if epilogue is not None else tuple(accs)
        for o_ref, val in zip(o_refs, outs):
            o_ref[...] = val.astype(o_ref.dtype)

    def body(*refs):
        a_refs, b_refs = refs[:n_a], refs[n_a:n_a + n_b]
        e_refs = refs[n_a + n_b:n_a + n_b + n_e]
        o_refs = refs[n_a + n_b + n_e:n_a + n_b + n_e + n_o]
        acc_refs = refs[n_a + n_b + n_e + n_o:]
        if nk == 1:
            finish(products(a_refs, b_refs), e_refs, o_refs)
            return
        k = pl.program_id(2)

        @pl.when(k == 0)
        def _():
            for acc in acc_refs:
                acc[...] = jnp.zeros_like(acc)

        for acc, p in zip(acc_refs, products(a_refs, b_refs)):
            acc[...] += p

        @pl.when(k == nk - 1)
        def _():
            finish([acc[...] for acc in acc_refs], e_refs, o_refs)

    def at(f):
        return (lambda j, i, k: f(i, j, k)) if n_outer else f

    a_spec = pl.BlockSpec((tk, tm), at(lambda i, j, k: (k, i))) if ta else pl.BlockSpec((tm, tk), at(lambda i, j, k: (i, k)))
    b_spec = pl.BlockSpec((tn, tk), at(lambda i, j, k: (j, k))) if tb else pl.BlockSpec((tk, tn), at(lambda i, j, k: (k, j)))
    e_specs = [pl.BlockSpec((tm, tn), at(functools.partial(lambda i, j, k, o: (i, j + o), o=off // tn))) for off in extra_offs]
    for off in extra_offs:
        assert off % tn == 0
    outs = pl.pallas_call(
        body, name=name,
        out_shape=tuple(jax.ShapeDtypeStruct((M, N), dt) for dt in out_dtypes),
        grid=(N // tn, M // tm, nk) if n_outer else (M // tm, N // tn, nk),
        in_specs=[a_spec] * n_a + [b_spec] * n_b + e_specs,
        out_specs=tuple(pl.BlockSpec((tm, tn), at(lambda i, j, k: (i, j))) for _ in out_dtypes),
        scratch_shapes=[pltpu.VMEM((tm, tn), F32)] * n_acc,
        compiler_params=pltpu.CompilerParams(dimension_semantics=("parallel", "parallel", "arbitrary")),
    )(*As, *Bs, *extras)
    return outs[0] if n_o == 1 else outs


def _rms_fwd(x, g, *, name, blk_w=None, blk_idx=0, off=0, width=None, out_dtype=BF):
    T = x.shape[0]
    blk_w = x.shape[1] if blk_w is None else blk_w
    width = blk_w if width is None else width
    tt = _pick(T, 512)

    def body(x_ref, g_ref, o_ref):
        xf = x_ref[:, off:off + width]
        r = lax.rsqrt(jnp.mean(xf * xf, axis=-1, keepdims=True) + EPS)
        o_ref[...] = (xf * r * g_ref[...]).astype(o_ref.dtype)

    return pl.pallas_call(
        body, name=name, out_shape=jax.ShapeDtypeStruct((T, width), out_dtype), grid=(T // tt,),
        in_specs=[pl.BlockSpec((tt, blk_w), lambda i: (i, blk_idx)), pl.BlockSpec((1, width), lambda i: (0, 0))],
        out_specs=pl.BlockSpec((tt, width), lambda i: (i, 0)),
    )(x, g)


def _rms_bwd(dy, x, g, *, name, blk_w=None, blk_idx=0, off=0, width=None, add=None, out_dtypes=(F32,)):
    T = x.shape[0]
    blk_w = x.shape[1] if blk_w is None else blk_w
    width = blk_w if width is None else width
    tt = _pick(T, 512)
    has_add = add is not None
    n_dx = len(out_dtypes)

    def body(*refs):
        dy_ref, x_ref, g_ref = refs[:3]
        dx_refs, dg_ref = refs[3 + has_add:3 + has_add + n_dx], refs[-1]
        xf = x_ref[:, off:off + width]
        d = dy_ref[...].astype(F32)
        r = lax.rsqrt(jnp.mean(xf * xf, axis=-1, keepdims=True) + EPS)
        gd = d * g_ref[...]
        dx = r * gd - xf * (r * r * r) * jnp.mean(gd * xf, axis=-1, keepdims=True)
        if has_add:
            dx = dx + refs[3][...]
        for dx_ref in dx_refs:
            dx_ref[...] = dx.astype(dx_ref.dtype)

        @pl.when(pl.program_id(0) == 0)
        def _():
            dg_ref[...] = jnp.zeros_like(dg_ref)

        dg_ref[...] += jnp.broadcast_to(jnp.sum(d * xf * r, axis=0, keepdims=True), dg_ref.shape)

    row = pl.BlockSpec((tt, width), lambda i: (i, 0))
    in_specs = [row, pl.BlockSpec((tt, blk_w), lambda i: (i, blk_idx)), pl.BlockSpec((1, width), lambda i: (0, 0))]
    args = [dy, x, g]
    if has_add:
        in_specs.append(row)
        args.append(add)
    return pl.pallas_call(
        body, name=name,
        out_shape=tuple(jax.ShapeDtypeStruct((T, width), dt) for dt in out_dtypes) + (jax.ShapeDtypeStruct((8, width), F32),),
        grid=(T // tt,), in_specs=in_specs,
        out_specs=(row,) * n_dx + (pl.BlockSpec((8, width), lambda i: (0, 0)),),
        compiler_params=pltpu.CompilerParams(dimension_semantics=("arbitrary",)),
    )(*args)


def _rope_tables(pos_col, freq_lane):
    T = pos_col.shape[0]
    tt = _pick(T, 512)

    def body(p_ref, f_ref, c_ref, s_ref):
        ang = p_ref[...] * f_ref[...]
        lane = lax.broadcasted_iota(jnp.int32, ang.shape, 1)
        c_ref[...] = jnp.where(lane < QK_HEAD, jnp.cos(ang), 0.0)
        sn = jnp.sin(ang)
        s_ref[...] = jnp.where((lane >= QK_NOPE) & (lane < QK_NOPE + 16), -sn,
                               jnp.where((lane >= QK_NOPE + 16) & (lane < QK_HEAD), sn, 0.0))

    return pl.pallas_call(
        body, name="rope_tables", out_shape=(jax.ShapeDtypeStruct((T, HP), F32),) * 2, grid=(T // tt,),
        in_specs=[pl.BlockSpec((tt, 1), lambda i: (i, 0)), pl.BlockSpec((1, HP), lambda i: (0, 0))],
        out_specs=(pl.BlockSpec((tt, HP), lambda i: (i, 0)),) * 2,
    )(pos_col, freq_lane)


def _swap_rope_halves(n):
    src = lax.broadcasted_iota(jnp.int32, (HP, HP), 0)
    dst = lax.broadcasted_iota(jnp.int32, (HP, HP), 1)
    lo = (dst >= QK_NOPE) & (dst < QK_NOPE + 16) & (src == dst + 16)
    hi = (dst >= QK_NOPE + 16) & (dst < QK_HEAD) & (src == dst - 16)
    return _split_dot(n, jnp.where(lo | hi, 1.0, 0.0).astype(BF), 2)


def _qk_prep_fwd(raw, kpe, gain, C, S, *, name, kpe_blk=0, out_scale=1.0):
    T = raw.shape[0]
    tt = _pick(T, 256)
    has_kpe = kpe is not None

    def body(*refs):
        if has_kpe:
            raw_ref, kpe_ref, g_ref, c_ref, s_ref, o_ref = refs
        else:
            raw_ref, g_ref, c_ref, s_ref, o_ref = refs
        for h in range(N_HEADS):
            hs = slice(HP * h, HP * (h + 1))
            xr = raw_ref[:, hs] + kpe_ref[...] if has_kpe else raw_ref[:, hs]
            r = lax.rsqrt(jnp.sum(xr * xr, axis=-1, keepdims=True) * (1.0 / QK_HEAD) + EPS)
            n = xr * r * g_ref[...]
            o_ref[:, hs] = ((n * c_ref[...] + _swap_rope_halves(n) * s_ref[...]) * out_scale).astype(o_ref.dtype)

    heads = pl.BlockSpec((tt, N_HEADS * HP), lambda i: (i, 0))
    shared = pl.BlockSpec((tt, HP), lambda i: (i, 0))
    kpe_spec = pl.BlockSpec((tt, HP), lambda i: (i, kpe_blk))
    in_specs = [heads] + ([kpe_spec] if has_kpe else []) + [pl.BlockSpec((1, HP), lambda i: (0, 0)), shared, shared]
    args = [raw] + ([kpe] if has_kpe else []) + [gain, C, S]
    return pl.pallas_call(
        body, name=name, out_shape=jax.ShapeDtypeStruct(raw.shape, BF), grid=(T // tt,),
        in_specs=in_specs, out_specs=heads,
    )(*args)


def _qk_prep_bwd(dout, raw, kpe, gain, C, S, *, name, kpe_blk=0, in_scale=1.0):
    T = raw.shape[0]
    tt = _pick(T, 256)
    has_kpe = kpe is not None

    def body(*refs):
        if has_kpe:
            d_ref, raw_ref, kpe_ref, g_ref, c_ref, s_ref, dx_ref, dg_ref, dkpe_ref = refs
        else:
            d_ref, raw_ref, g_ref, c_ref, s_ref, dx_ref, dg_ref = refs
        dg = jnp.zeros((1, HP), F32)
        dkpe = jnp.zeros((tt, HP), F32)
        for h in range(N_HEADS):
            hs = slice(HP * h, HP * (h + 1))
            xr = raw_ref[:, hs] + kpe_ref[...] if has_kpe else raw_ref[:, hs]
            d = d_ref[:, hs].astype(F32) * in_scale
            r = lax.rsqrt(jnp.sum(xr * xr, axis=-1, keepdims=True) * (1.0 / QK_HEAD) + EPS)
            dn = d * c_ref[...] + _swap_rope_halves(d * s_ref[...])
            gd = dn * g_ref[...]
            dx = r * gd - xr * (r * r * r) * (jnp.sum(gd * xr, axis=-1, keepdims=True) * (1.0 / QK_HEAD))
            dx_ref[:, hs] = dx.astype(dx_ref.dtype)
            dg = dg + jnp.sum(dn * xr * r, axis=0, keepdims=True)
            dkpe = dkpe + dx

        @pl.when(pl.program_id(0) == 0)
        def _():
            dg_ref[...] = jnp.zeros_like(dg_ref)

        dg_ref[...] += jnp.broadcast_to(dg, dg_ref.shape)
        if has_kpe:
            dkpe_ref[...] = dkpe

    heads = pl.BlockSpec((tt, N_HEADS * HP), lambda i: (i, 0))
    shared = pl.BlockSpec((tt, HP), lambda i: (i, 0))
    kpe_spec = pl.BlockSpec((tt, HP), lambda i: (i, kpe_blk))
    in_specs = [heads, heads] + ([kpe_spec] if has_kpe else []) + [pl.BlockSpec((1, HP), lambda i: (0, 0)), shared, shared]
    args = [dout, raw] + ([kpe] if has_kpe else []) + [gain, C, S]
    out_shape = [jax.ShapeDtypeStruct(raw.shape, BF), jax.ShapeDtypeStruct((8, HP), F32)]
    out_specs = [heads, pl.BlockSpec((8, HP), lambda i: (0, 0))]
    if has_kpe:
        out_shape.append(jax.ShapeDtypeStruct((T, HP), F32))
        out_specs.append(shared)
    return pl.pallas_call(
        body, name=name, out_shape=tuple(out_shape), grid=(T // tt,),
        in_specs=in_specs, out_specs=tuple(out_specs),
        compiler_params=pltpu.CompilerParams(dimension_semantics=("arbitrary",)),
    )(*args)


ATTN_SCALE = 1.0 / math.sqrt(QK_HEAD)
LOG2E = 1.0 / math.log(2.0)
Q_SCALE = ATTN_SCALE * LOG2E


def _attn_fwd(q, k, v):
    T = q.shape[0]
    tq = _pick(T, 1024)
    tk = _pick(T, 1024)

    def body(q_ref, k_ref, v_ref, o_ref, ob_ref, lse_ref):
        qt = q_ref[...]
        m = o = None
        for j in range(T // tk):
            ks = slice(j * tk, (j + 1) * tk)
            s = lax.dot_general(qt, k_ref[ks, :], NT, preferred_element_type=F32)
            m_j = jnp.max(s, axis=-1, keepdims=True)
            m_new = m_j if m is None else jnp.maximum(m, m_j)
            o_j = jnp.dot(jnp.exp2(s - m_new).astype(BF), v_ref[ks, :], preferred_element_type=F32)
            o = o_j if o is None else o * jnp.exp2(m - m_new) + o_j
            m = m_new
        l = o[:, V_HEAD:V_HEAD + 1]
        o = o / l
        o_ref[...] = o
        ob_ref[...] = o.astype(ob_ref.dtype)
        lse_ref[...] = jnp.broadcast_to(m + jnp.log2(l), lse_ref.shape)

    qs = pl.BlockSpec((tq, HP), lambda h, i: (i, h))
    kv = pl.BlockSpec((T, HP), lambda h, i: (0, h))
    return pl.pallas_call(
        body, name="attn_fwd",
        out_shape=(jax.ShapeDtypeStruct(q.shape, F32), jax.ShapeDtypeStruct(q.shape, BF), jax.ShapeDtypeStruct(q.shape, F32)),
        grid=(N_HEADS, T // tq), in_specs=[qs, kv, kv], out_specs=(qs, qs, qs),
        compiler_params=pltpu.CompilerParams(dimension_semantics=("parallel", "parallel")),
    )(q, k, v)


def _attn_bwd(q, k, v, do, o, lse):
    T = q.shape[0]
    tb = _pick(T, 1024)
    nb = T // tb
    tkey = _pick(T, 1024)

    def body(q_ref, k_ref, v_ref, do_ref, o_ref, lse_ref, dq_ref, dk_ref, dv_ref, delta_rows, lse_rows, dob_scr, dv_acc):
        dq_ref[...] = jnp.zeros_like(dq_ref)
        dk_ref[...] = jnp.zeros_like(dk_ref)
        lane = lax.broadcasted_iota(jnp.int32, (8, HP), 1)
        ones8 = jnp.ones((8, HP), BF)
        first8 = jnp.where(lane == 0, 1.0, 0.0).astype(BF)

        def as_rows(pick, v):
            total, rest = None, v
            for _ in range(3):
                piece = rest.astype(BF)
                part = lax.dot_general(pick, piece, NT, preferred_element_type=F32)
                total = part if total is None else total + part
                rest = rest - piece.astype(F32)
            return total

        def per_q_tile(i, carry):
            qs = pl.ds(pl.multiple_of(i * tb, tb), tb)
            doi = do_ref[qs, :]
            delta_rows[i] = as_rows(ones8, doi * o_ref[qs, :])
            lse_rows[i] = as_rows(first8, lse_ref[qs, :])
            dob_scr[qs, :] = doi.astype(BF)
            return carry

        lax.fori_loop(0, nb, per_q_tile, 0)

        def k_loop(j, carry):
            ks = pl.ds(pl.multiple_of(j * tkey, tkey), tkey)
            kj, vj = k_ref[ks, :], v_ref[ks, :]

            dv_acc[...] = jnp.zeros_like(dv_acc)

            def q_loop(i, carry_q):
                qs = pl.ds(pl.multiple_of(i * tb, tb), tb)
                qi = q_ref[qs, :]
                dob = dob_scr[qs, :]
                s_t = lax.dot_general(kj, qi, NT, preferred_element_type=F32)
                p_t = jnp.exp2(s_t - lse_rows[i, 0:1, :])
                dp_t = lax.dot_general(vj, dob, NT, preferred_element_type=F32)
                ds_t = (p_t * (dp_t - delta_rows[i, 0:1, :])).astype(BF)
                dv_acc[...] += jnp.dot(p_t.astype(BF), dob, preferred_element_type=F32)
                dk_ref[ks, :] += jnp.dot(ds_t, qi, preferred_element_type=F32)
                dq_ref[qs, :] += lax.dot_general(ds_t, kj, TN, preferred_element_type=F32)
                return carry_q

            lax.fori_loop(0, nb, q_loop, 0)
            dv_ref[ks, :] = dv_acc[...].astype(dv_ref.dtype)
            return carry

        lax.fori_loop(0, T // tkey, k_loop, 0)

    spec = pl.BlockSpec((T, HP), lambda h: (0, h))
    return pl.pallas_call(
        body, name="attn_bwd",
        out_shape=(jax.ShapeDtypeStruct(q.shape, F32), jax.ShapeDtypeStruct(q.shape, F32), jax.ShapeDtypeStruct(q.shape, BF)),
        grid=(N_HEADS,), in_specs=[spec] * 6, out_specs=(spec,) * 3,
        scratch_shapes=[pltpu.VMEM((nb, 8, tb), F32), pltpu.VMEM((nb, 8, tb), F32), pltpu.VMEM((T, HP), BF),
                        pltpu.VMEM((tkey, HP), F32)],
        compiler_params=pltpu.CompilerParams(dimension_semantics=("parallel",), vmem_limit_bytes=2 * 15 * T * HP * 2 + (8 << 20)),
    )(q, k, v, do, o, lse)


CONV_TC = 512
CONV_PAD = CONV_WIDTH // 2


def _halo_specs(tr, col_of):
    r8 = tr // 8
    cur = pl.BlockSpec((tr, CONV_TC), lambda j, i: (i, col_of(j)))
    prev = pl.BlockSpec((8, CONV_TC), lambda j, i: (jnp.maximum(i * r8 - 1, 0), col_of(j)))

    def nxt_map(j, i, n8):
        return (jnp.minimum((i + 1) * r8, n8 - 1), col_of(j))

    return cur, prev, nxt_map


def _with_halo(prev_ref, cur_ref, next_ref, i, n_i):
    prev = jnp.where(i == 0, 0.0, prev_ref[...].astype(F32))
    nxt = jnp.where(i == n_i - 1, 0.0, next_ref[...].astype(F32))
    return jnp.concatenate([prev, cur_ref[...].astype(F32), nxt], axis=0)


def _conv_fwd(u, w8, b):
    T = u.shape[0]
    tr = _pick(T, 512)
    n_i = T // tr
    c0 = U_XBC // CONV_TC
    cur, prev, nxt_map = _halo_specs(tr, lambda j: c0 + j)
    nxt = pl.BlockSpec((8, CONV_TC), functools.partial(nxt_map, n8=T // 8))

    def body(p_ref, c_ref, n_ref, w_ref, b_ref, pre_ref, act_ref):
        i = pl.program_id(1)
        full = _with_halo(p_ref, c_ref, n_ref, i, n_i)
        acc = jnp.broadcast_to(b_ref[...], (tr, CONV_TC))
        for kk in range(CONV_WIDTH):
            acc = acc + full[8 - CONV_PAD + kk:8 - CONV_PAD + kk + tr, :] * w_ref[kk:kk + 1, :]
        pre_ref[...] = acc
        act_ref[...] = _silu(acc)

    out = pl.BlockSpec((tr, CONV_TC), lambda j, i: (i, j))
    return pl.pallas_call(
        body, name="conv_fwd", out_shape=(jax.ShapeDtypeStruct((T, XBC_DIM), F32),) * 2,
        grid=(XBC_DIM // CONV_TC, n_i),
        in_specs=[prev, cur, nxt, pl.BlockSpec((8, CONV_TC), lambda j, i: (0, j)), pl.BlockSpec((1, CONV_TC), lambda j, i: (0, j))],
        out_specs=(out, out),
    )(u, u, u, w8, b)


def _conv_bwd(dacts, pre, u, w8, col0, *, name):
    T, width = dacts[0].shape
    tr = _pick(T, 512)
    n_i = T // tr
    n_d = len(dacts)
    cd = col0 // CONV_TC
    cx = (U_XBC + col0) // CONV_TC

    def halo(col_of):
        cur, prev, nxt_map = _halo_specs(tr, col_of)
        return [prev, cur, pl.BlockSpec((8, CONV_TC), functools.partial(nxt_map, n8=T // 8))]

    def body(*refs):
        d_refs, pre_refs, x_refs = refs[:3 * n_d], refs[3 * n_d:3 * n_d + 3], refs[3 * n_d + 3:3 * n_d + 6]
        w_ref, dx_ref, dw_ref = refs[3 * n_d + 6:]
        i = pl.program_id(1)
        dfull = _with_halo(*d_refs[0:3], i, n_i)
        for p in range(1, n_d):
            dfull = dfull + _with_halo(*d_refs[3 * p:3 * p + 3], i, n_i)
        dfull = dfull * _dsilu(_with_halo(*pre_refs, i, n_i))
        xfull = _with_halo(*x_refs, i, n_i)
        dcur = dfull[8:8 + tr, :]
        dx = jnp.zeros((tr, CONV_TC), F32)
        rows = []
        for kk in range(CONV_WIDTH):
            dx = dx + dfull[8 + CONV_PAD - kk:8 + CONV_PAD - kk + tr, :] * w_ref[kk:kk + 1, :]
            rows.append(jnp.sum(dcur * xfull[8 - CONV_PAD + kk:8 - CONV_PAD + kk + tr, :], axis=0, keepdims=True))
        rows.append(jnp.sum(dcur, axis=0, keepdims=True))
        rows.append(jnp.zeros((2, CONV_TC), F32))
        dx_ref[...] = dx.astype(dx_ref.dtype)

        @pl.when(i == 0)
        def _():
            dw_ref[...] = jnp.zeros_like(dw_ref)

        dw_ref[...] += jnp.concatenate(rows, axis=0)

    out = pl.BlockSpec((tr, CONV_TC), lambda j, i: (i, j))
    return pl.pallas_call(
        body, name=name, out_shape=(jax.ShapeDtypeStruct((T, width), BF), jax.ShapeDtypeStruct((8, width), F32)),
        grid=(width // CONV_TC, n_i),
        in_specs=halo(lambda j: j) * n_d + halo(lambda j: cd + j) + halo(lambda j: cx + j)
        + [pl.BlockSpec((8, CONV_TC), lambda j, i: (0, cd + j))],
        out_specs=(out, pl.BlockSpec((8, CONV_TC), lambda j, i: (0, j))),
        compiler_params=pltpu.CompilerParams(dimension_semantics=("parallel", "arbitrary")),
    )(*[d for d in dacts for _ in range(3)], pre, pre, pre, u, u, u, w8)


N_HB = 2 * SSM_GROUPS
P_DT, P_CS, P_E, P_W = 0, HP, 2 * HP, 3 * HP
DT_BLK = (U_SMALL + S_DT) // HP


def _tri(rev, transpose=False):
    rows = lax.broadcasted_iota(jnp.int32, (CHUNK, CHUNK), 0)
    cols = lax.broadcasted_iota(jnp.int32, (CHUNK, CHUNK), 1)
    if transpose:
        rows, cols = cols, rows
    return (cols >= rows) if rev else (cols <= rows)


def _ssd_prep(u, bias8, alog8):
    T = u.shape[0]
    nc = T // CHUNK

    def body(dt_ref, bias_ref, a_ref, cols_ref, rows_ref):
        lane = lax.broadcasted_iota(jnp.int32, (CHUNK, HP), 1)
        dt = _softplus(dt_ref[...] + bias_ref[0:1, :])
        da = dt * (-jnp.exp(a_ref[0:1, :]))
        cs_f = jnp.dot(jnp.where(_tri(False), 1.0, 0.0).astype(F32), da, precision=HI, preferred_element_type=F32)
        cs_b = jnp.dot(jnp.where(_tri(True), 1.0, 0.0).astype(F32), da, precision=HI, preferred_element_type=F32)
        cs = jnp.where(lane < SSM_HEADS, cs_f, cs_b)
        tot = jnp.where(lane[0:1] < SSM_HEADS, cs_f[CHUNK - 1:CHUNK, :], cs_b[0:1, :])
        e, w = jnp.exp(cs), jnp.exp(tot - cs)
        tot8 = jnp.broadcast_to(tot, (8, HP))
        etot8 = jnp.exp(tot8)
        for b in range(N_HB):
            down = (HP - HG * b) % HP

            def rolled(v):
                return pltpu.roll(v, down, 1) if down else v

            cols_ref[b, :, P_DT:P_DT + HP] = rolled(dt)
            cs_r = rolled(cs)
            cols_ref[b, :, P_CS:P_CS + HP] = cs_r
            cols_ref[b, :, P_E:P_E + HP] = rolled(e)
            cols_ref[b, :, P_W:P_W + HP] = rolled(w)
            rows_ref[b, 0, 0:8, :] = cs_r.T[0:8, :]
            r8 = lax.broadcasted_iota(jnp.int32, (8, HP), 0)
            rows_ref[b, 0, 8:16, :] = jnp.where(r8 == 0, rolled(tot8), jnp.where(r8 == 1, rolled(etot8), 0.0))

    vec = pl.BlockSpec((8, HP), lambda c: (0, 0))
    return pl.pallas_call(
        body, name="ssd_prep",
        out_shape=(jax.ShapeDtypeStruct((N_HB, T, 4 * HP), F32), jax.ShapeDtypeStruct((N_HB, nc, 16, HP), F32)),
        grid=(nc,), in_specs=[pl.BlockSpec((CHUNK, HP), lambda c: (c, DT_BLK)), vec, vec],
        out_specs=(pl.BlockSpec((N_HB, CHUNK, 4 * HP), lambda c: (0, c, 0)), pl.BlockSpec((N_HB, 1, 16, HP), lambda c: (0, c, 0, 0))),
    )(u, bias8, alog8)


def _ssd_specs(T, rev, bwd):
    nc = T // CHUNK
    fwd_order = (lambda c: nc - 1 - c) if rev else (lambda c: c)
    cm = (lambda c: fwd_order(nc - 1 - c)) if bwd else fwd_order
    hb0 = SSM_GROUPS if rev else 0
    xs = pl.BlockSpec((CHUNK, GW), lambda c, g: (cm(c), g))
    bs = pl.BlockSpec((CHUNK, D_STATE), lambda c, g: (cm(c), D_INNER // D_STATE + g))
    cs = pl.BlockSpec((CHUNK, D_STATE), lambda c, g: (cm(c), (D_INNER + SSM_GROUPS * D_STATE) // D_STATE + g))
    cols = pl.BlockSpec((1, CHUNK, 4 * HP), lambda c, g: (hb0 + g, cm(c), 0))
    rows = pl.BlockSpec((1, 1, 16, HP), lambda c, g: (hb0 + g, cm(c), 0, 0))
    return nc, cm, xs, bs, cs, cols, rows


def _head_lanes(to_heads):
    shape = (GW, HP) if to_heads else (HP, GW)
    wide = lax.broadcasted_iota(jnp.int32, shape, 0 if to_heads else 1)
    head = lax.broadcasted_iota(jnp.int32, shape, 1 if to_heads else 0)
    return jnp.where((wide >= PH * head) & (wide < PH * (head + 1)), 1.0, 0.0).astype(BF)


def _split_dot(v, m, terms):
    total, rest = None, v
    for _ in range(terms):
        piece = rest.astype(BF)
        part = jnp.dot(piece, m, preferred_element_type=F32)
        total = part if total is None else total + part
        rest = rest - piece.astype(F32)
    return total


def _spread_cols(cols_ref, rows_ref):
    spread = _head_lanes(False)
    dt_e = _split_dot(cols_ref[0, :, P_DT:P_DT + HP], spread, 3)
    e_e = _split_dot(cols_ref[0, :, P_E:P_E + HP], spread, 2)
    w_e = _split_dot(cols_ref[0, :, P_W:P_W + HP], spread, 2)
    etot_e = _split_dot(rows_ref[0, 0, 8:16, :], spread, 3)[1:2, :]
    return dt_e, e_e, w_e, etot_e


def _decay(cols_ref, rows_ref, hh, incl, transpose=False):
    col = cols_ref[0, :, P_CS + hh:P_CS + hh + 1]
    row = rows_ref[0, 0, hh:hh + 1, :]
    return jnp.where(incl, jnp.exp(row - col if transpose else col - row), 0.0)


def _ssd_fwd(act, cols, rows, *, rev, name, add=None):
    T = act.shape[0]
    nc, cm, xs_s, b_s, c_s, cols_s, rows_s = _ssd_specs(T, rev, False)
    has_add = add is not None

    def body(*refs):
        x_ref, b_ref, c_ref, cols_ref, rows_ref = refs[:5]
        y_ref, st_ref, state = refs[5 + has_add:]
        c, g = pl.program_id(0), pl.program_id(1)

        @pl.when(c == 0)
        def _():
            state[g] = jnp.zeros((D_STATE, GW), F32)

        incl = _tri(rev)
        bm, cmat = b_ref[...].astype(BF), c_ref[...].astype(BF)
        bm_t = b_ref[...].T.astype(BF)
        cb = lax.dot_general(cmat, bm, NT, preferred_element_type=F32)
        dt_e, e_e, w_e, etot_e = _spread_cols(cols_ref, rows_ref)
        prev_all = state[g]
        st_ref[...] = prev_all
        xdt = x_ref[...] * dt_e
        xdt_b = xdt.astype(BF)
        yo_all = jnp.dot(cmat, prev_all.astype(BF), preferred_element_type=F32) * e_e
        state[g] = prev_all * etot_e + jnp.dot(bm_t, (xdt * w_e).astype(BF), preferred_element_type=F32)
        for hh in range(HG):
            hs = slice(PH * hh, PH * (hh + 1))
            lmat = _decay(cols_ref, rows_ref, hh, incl)
            yd = jnp.dot((cb * lmat).astype(BF), xdt_b[:, hs], preferred_element_type=F32)
            y_ref[:, hs] = yd + yo_all[:, hs] + refs[5][:, hs] if has_add else yd + yo_all[:, hs]

    return pl.pallas_call(
        body, name=name,
        out_shape=(jax.ShapeDtypeStruct((T, D_INNER), F32), jax.ShapeDtypeStruct((nc * D_STATE, D_INNER), F32)),
        grid=(nc, SSM_GROUPS), in_specs=[xs_s, b_s, c_s, cols_s, rows_s] + [xs_s] * has_add, out_specs=(xs_s, xs_s),
        scratch_shapes=[pltpu.VMEM((SSM_GROUPS, D_STATE, GW), F32)],
        compiler_params=pltpu.CompilerParams(dimension_semantics=("arbitrary", "arbitrary")),
    )(act, act, act, cols, rows, *([add] if has_add else []))


def _ssd_bwd(act, cols, rows, states, dy, *, rev, name, skip=None, add=None):
    T = act.shape[0]
    nc, cm, xs_s, b_s, c_s, cols_s, rows_s = _ssd_specs(T, rev, True)
    has_skip, has_add = skip is not None, add is not None
    n_in = 7 + has_skip + 3 * has_add

    def body(*refs):
        x_ref, b_ref, c_ref, cols_ref, rows_ref, st_ref, dy_ref = refs[:7]
        extra = list(refs[7:n_in])
        dx_ref, db_ref, dc_ref, dsel_ref, dtot_ref, dstate, dcs_cols, dcs_rows, dcb, dm_scr, dxdt_scr = refs[n_in:]
        c, g = pl.program_id(0), pl.program_id(1)

        @pl.when(c == 0)
        def _():
            dstate[g] = jnp.zeros((D_STATE, GW), F32)

        incl, incl_t = _tri(rev), _tri(rev, transpose=True)
        bm, cmat = b_ref[...].astype(BF), c_ref[...].astype(BF)
        cm_t = c_ref[...].T.astype(BF)
        cb = lax.dot_general(cmat, bm, NT, preferred_element_type=F32)
        cb_t = lax.dot_general(bm, cmat, NT, preferred_element_type=F32)
        prev_all, ds_all = st_ref[...], dstate[g]
        pb_all, dsb_all = prev_all.astype(BF), ds_all.astype(BF)
        cp_all = jnp.dot(cmat, pb_all, preferred_element_type=F32)
        bds_all = jnp.dot(bm, dsb_all, preferred_element_type=F32)
        dt_e, e_e, w_e, etot_e = _spread_cols(cols_ref, rows_ref)
        to_heads = _head_lanes(True)
        x, dy = x_ref[...], dy_ref[...]
        xdt = x * dt_e
        xdt_b, dy_b = xdt.astype(BF), dy.astype(BF)
        dye_b, xdw_b = (dy * e_e).astype(BF), (xdt * w_e).astype(BF)
        for hh in range(HG):
            hs = slice(PH * hh, PH * (hh + 1))
            mmat_t = cb_t * _decay(cols_ref, rows_ref, hh, incl_t, transpose=True)
            dm_scr[hh] = lax.dot_general(dy_b[:, hs], xdt_b[:, hs], NT, preferred_element_type=F32)
            dxdt_scr[:, hs] = jnp.dot(mmat_t.astype(BF), dy_b[:, hs], preferred_element_type=F32)
        bdsw = bds_all * w_e
        dxdt = dxdt_scr[...] + bdsw
        dx = dxdt * dt_e
        if has_skip:
            dx = dx + dy * extra.pop(0)[...]
        if has_add:
            dx = dx + extra[0][...]
        dx_ref[...] = dx
        t = _split_dot(xdt * bdsw, to_heads, 2)
        dcs_state = _split_dot(dy * cp_all, to_heads, 2) * cols_ref[0, :, P_E:P_E + HP] - t
        dsel_ref[0, :, 0:HP] = _split_dot(dxdt * x, to_heads, 2)
        sp = _split_dot(jnp.broadcast_to(jnp.sum(ds_all * prev_all, axis=0, keepdims=True), (8, GW)), to_heads, 2)
        dtot_ref[0, 0] = jnp.sum(t, axis=0, keepdims=True) + sp * rows_ref[0, 0, 9:10, :]
        dstate[g] = ds_all * etot_e + jnp.dot(cm_t, dye_b, preferred_element_type=F32)
        dcs_cols[...] = jnp.zeros_like(dcs_cols)
        dcs_rows[...] = jnp.zeros_like(dcs_rows)
        dcb[...] = jnp.zeros_like(dcb)
        for hh in range(HG):
            lmat = _decay(cols_ref, rows_ref, hh, incl)
            dm = dm_scr[hh]
            qm = dm * (cb * lmat)
            dcs_cols[:, hh:hh + 1] = jnp.sum(qm, axis=1, keepdims=True)
            dcs_rows[hh:hh + 1, :] = jnp.sum(qm, axis=0, keepdims=True)
            dcb[...] += dm * lmat
        dcb_all = dcb[...]
        dsel_ref[0, :, HP:2 * HP] = dcs_state + dcs_cols[...] - dcs_rows[...].T
        dc = (lax.dot_general(dye_b, pb_all, NT, preferred_element_type=F32)
              + jnp.dot(dcb_all.astype(BF), bm, preferred_element_type=F32))
        db = (lax.dot_general(xdw_b, dsb_all, NT, preferred_element_type=F32)
              + jnp.dot(dcb_all.T.astype(BF), cmat, preferred_element_type=F32))
        db_ref[...] = db + extra[1][...] if has_add else db
        dc_ref[...] = dc + extra[2][...] if has_add else dc

    bc_out = pl.BlockSpec((CHUNK, D_STATE), lambda c, g: (cm(c), g))
    more_specs = [pl.BlockSpec((1, GW), lambda c, g: (0, g))] * has_skip + [xs_s, bc_out, bc_out] * has_add
    more_args = ([skip] if has_skip else []) + (list(add) if has_add else [])
    return pl.pallas_call(
        body, name=name,
        out_shape=(jax.ShapeDtypeStruct((T, D_INNER), F32), jax.ShapeDtypeStruct((T, SSM_GROUPS * D_STATE), F32),
                   jax.ShapeDtypeStruct((T, SSM_GROUPS * D_STATE), F32), jax.ShapeDtypeStruct((SSM_GROUPS, T, 2 * HP), F32),
                   jax.ShapeDtypeStruct((SSM_GROUPS, nc, 8, HP), F32)),
        grid=(nc, SSM_GROUPS), in_specs=[xs_s, b_s, c_s, cols_s, rows_s, xs_s, xs_s] + more_specs,
        out_specs=(xs_s, bc_out, bc_out, pl.BlockSpec((1, CHUNK, 2 * HP), lambda c, g: (g, cm(c), 0)),
                   pl.BlockSpec((1, 1, 8, HP), lambda c, g: (g, cm(c), 0, 0))),
        scratch_shapes=[pltpu.VMEM((SSM_GROUPS, D_STATE, GW), F32), pltpu.VMEM((CHUNK, CHUNK), F32),
                        pltpu.VMEM((CHUNK, CHUNK), F32), pltpu.VMEM((CHUNK, CHUNK), F32),
                        pltpu.VMEM((HG, CHUNK, CHUNK), F32), pltpu.VMEM((CHUNK, GW), F32)],
        compiler_params=pltpu.CompilerParams(dimension_semantics=("arbitrary", "arbitrary")),
    )(act, act, act, cols, rows, states, dy, *more_args)


def _ssd_prep_bwd(u, bias8, alog8, dsel_f, dtot_f, dsel_b, dtot_b):
    T = u.shape[0]
    nc = T // CHUNK

    def body(dt_ref, bias_ref, a_ref, sf_ref, tf_ref, sb_ref, tb_ref, ddt_ref, da_ref, dbias_ref):
        @pl.when(pl.program_id(0) == 0)
        def _():
            da_ref[...] = jnp.zeros_like(da_ref)
            dbias_ref[...] = jnp.zeros_like(dbias_ref)

        lane = lax.broadcasted_iota(jnp.int32, (CHUNK, HP), 1)
        pre = dt_ref[...] + bias_ref[0:1, :]
        dt = _softplus(pre)
        a = -jnp.exp(a_ref[0:1, :])
        ddt_x, dcs, dtot = jnp.zeros((CHUNK, HP), F32), jnp.zeros((CHUNK, HP), F32), jnp.zeros((8, HP), F32)
        for b in range(N_HB):
            s_ref, t_ref, g = (sf_ref, tf_ref, b) if b < SSM_GROUPS else (sb_ref, tb_ref, b - SSM_GROUPS)
            mine = (lane >= HG * b) & (lane < HG * (b + 1))

            def up(v):
                return pltpu.roll(v, HG * b, 1) if b else v

            ddt_x = ddt_x + jnp.where(mine, up(s_ref[g, :, 0:HP]), 0.0)
            dcs = dcs + jnp.where(mine, up(s_ref[g, :, HP:2 * HP]), 0.0)
            dtot = dtot + jnp.where(mine[0:8], up(t_ref[g, 0]), 0.0)
        tri_f = jnp.where(_tri(False, transpose=True), 1.0, 0.0).astype(F32)
        tri_b = jnp.where(_tri(True, transpose=True), 1.0, 0.0).astype(F32)
        dda = jnp.where(lane < SSM_HEADS, jnp.dot(tri_f, dcs, precision=HI, preferred_element_type=F32),
                        jnp.dot(tri_b, dcs, precision=HI, preferred_element_type=F32)) + dtot[0:1, :]
        dpre = (ddt_x + dda * a) * jax.nn.sigmoid(pre)
        ddt_ref[...] = jnp.where(lane < 2 * SSM_HEADS, dpre, 0.0)
        dbias_ref[...] += jnp.broadcast_to(jnp.sum(dpre, axis=0, keepdims=True), (8, HP))
        da_ref[...] += jnp.broadcast_to(jnp.sum(dda * dt, axis=0, keepdims=True) * a, (8, HP))

    vec = pl.BlockSpec((8, HP), lambda c: (0, 0))
    sel = pl.BlockSpec((SSM_GROUPS, CHUNK, 2 * HP), lambda c: (0, c, 0))
    tot = pl.BlockSpec((SSM_GROUPS, 1, 8, HP), lambda c: (0, c, 0, 0))
    tile = pl.BlockSpec((CHUNK, HP), lambda c: (c, 0))
    return pl.pallas_call(
        body, name="ssd_prep_bwd",
        out_shape=(jax.ShapeDtypeStruct((T, HP), F32), jax.ShapeDtypeStruct((8, HP), F32), jax.ShapeDtypeStruct((8, HP), F32)),
        grid=(nc,), in_specs=[pl.BlockSpec((CHUNK, HP), lambda c: (c, DT_BLK)), vec, vec, sel, tot, sel, tot],
        out_specs=(tile, vec, vec),
        compiler_params=pltpu.CompilerParams(dimension_semantics=("arbitrary",)),
    )(u, bias8, alog8, dsel_f, dtot_f, dsel_b, dtot_b)


def _ssm_combine_fwd(y_scans, act, u, dskip, gain):
    T = y_scans.shape[0]
    tt = _pick(T, 512)

    def body(ys_ref, x_ref, z_ref, ds_ref, g_ref, y_ref, m_ref):
        y = ys_ref[...] + ds_ref[...] * x_ref[...]
        y2 = y * _silu(z_ref[...])
        r = lax.rsqrt(jnp.mean(y2 * y2, axis=-1, keepdims=True) + EPS)
        y_ref[...] = y
        m_ref[...] = (y2 * r * g_ref[...]).astype(m_ref.dtype)

    blk = pl.BlockSpec((tt, GW), lambda i, g: (i, g))
    vec = pl.BlockSpec((1, GW), lambda i, g: (0, g))
    return pl.pallas_call(
        body, name="ssm_combine_fwd",
        out_shape=(jax.ShapeDtypeStruct((T, D_INNER), F32), jax.ShapeDtypeStruct((T, D_INNER), BF)),
        grid=(T // tt, SSM_GROUPS), in_specs=[blk, blk, blk, vec, vec], out_specs=(blk, blk),
    )(y_scans, act, u, dskip, gain)


def _ssm_combine_bwd(dm, y, act, u, gain):
    T = y.shape[0]
    tt = _pick(T, 512)

    def body(dm_ref, y_ref, x_ref, z_ref, g_ref, dy_ref, dz_ref, dg_ref, dsk_ref):
        z = z_ref[...]
        y = y_ref[...]
        x = x_ref[...]
        sz = _silu(z)
        y2 = y * sz
        r = lax.rsqrt(jnp.mean(y2 * y2, axis=-1, keepdims=True) + EPS)
        d = dm_ref[...]
        gd = d * g_ref[...]
        dy2 = r * gd - y2 * (r * r * r) * jnp.mean(gd * y2, axis=-1, keepdims=True)
        dy = dy2 * sz
        dy_ref[...] = dy
        dz_ref[...] = (dy2 * y * _dsilu(z)).astype(dz_ref.dtype)

        @pl.when(pl.program_id(1) == 0)
        def _():
            dg_ref[...] = jnp.zeros_like(dg_ref)
            dsk_ref[...] = jnp.zeros_like(dsk_ref)

        dg_ref[...] += jnp.broadcast_to(jnp.sum(d * y2 * r, axis=0, keepdims=True), dg_ref.shape)
        lane_sum = jnp.broadcast_to(jnp.sum(dy * x, axis=0, keepdims=True), (8, GW))
        src = lax.broadcasted_iota(jnp.int32, (GW, HP), 0)
        head = lax.broadcasted_iota(jnp.int32, (GW, HP), 1)
        to_head = jnp.where((src >= PH * head) & (src < PH * (head + 1)), 1.0, 0.0).astype(F32)
        dsk_ref[...] += jnp.dot(lane_sum, to_head, precision=HI, preferred_element_type=F32)

    blk = pl.BlockSpec((tt, GW), lambda g, i: (i, g))
    vec = pl.BlockSpec((1, GW), lambda g, i: (0, g))
    acc = pl.BlockSpec((8, GW), lambda g, i: (0, g))
    return pl.pallas_call(
        body, name="ssm_combine_bwd",
        out_shape=(jax.ShapeDtypeStruct((T, D_INNER), F32), jax.ShapeDtypeStruct((T, D_INNER), BF),
                   jax.ShapeDtypeStruct((8, D_INNER), F32), jax.ShapeDtypeStruct((8, SSM_GROUPS * HP), F32)),
        grid=(SSM_GROUPS, T // tt), in_specs=[blk, blk, blk, blk, vec],
        out_specs=(blk, blk, acc, pl.BlockSpec((8, HP), lambda g, i: (0, g))),
        compiler_params=pltpu.CompilerParams(dimension_semantics=("parallel", "arbitrary")),
    )(dm, y, act, u, gain)


def _loss_head(y, target):
    T, D = y.shape
    tt = _pick(T, 512)

    def body(y_ref, t_ref, dy_ref, dyb_ref, l_ref):
        e = y_ref[...] - t_ref[...]
        dy_ref[...] = e * (1.0 / D)
        dyb_ref[...] = (e * (1.0 / D)).astype(dyb_ref.dtype)

        @pl.when(pl.program_id(0) == 0)
        def _():
            l_ref[...] = jnp.zeros_like(l_ref)

        l_ref[...] += jnp.sum(e * e) * (0.5 / D)

    blk = pl.BlockSpec((tt, D), lambda i: (i, 0))
    return pl.pallas_call(
        body, name="loss_head",
        out_shape=(jax.ShapeDtypeStruct((T, D), F32), jax.ShapeDtypeStruct((T, D), BF), jax.ShapeDtypeStruct((8, 128), F32)),
        grid=(T // tt,), in_specs=[blk, blk], out_specs=(blk, blk, pl.BlockSpec((8, 128), lambda i: (0, 0))),
        compiler_params=pltpu.CompilerParams(dimension_semantics=("arbitrary",)),
    )(y, target)


def _adamw(w, g, m, v, *, name):
    R, C = w.shape
    cap = max(8, (1 << 18) // C)
    tr = R
    if R % 8 == 0:
        tr = 8
        for cand in range(8, min(R, cap) + 1, 8):
            if R % cand == 0:
                tr = cand

    def body(w_ref, g_ref, m_ref, v_ref, d_ref, nm_ref, nv_ref):
        gg = g_ref[...]
        nm = ADAM_B1 * m_ref[...] + (1.0 - ADAM_B1) * gg
        nv = ADAM_B2 * v_ref[...] + (1.0 - ADAM_B2) * jnp.square(gg)
        m_hat = nm / (1.0 - ADAM_B1 ** ADAM_STEP)
        v_hat = nv / (1.0 - ADAM_B2 ** ADAM_STEP)
        d_ref[...] = -ADAM_LR * (m_hat / (jnp.sqrt(v_hat) + ADAM_EPS) + ADAM_WD * w_ref[...])
        nm_ref[...] = nm
        nv_ref[...] = nv

    blk = pl.BlockSpec((tr, C), lambda i: (i, 0))
    return pl.pallas_call(
        body, name=name, out_shape=(jax.ShapeDtypeStruct((R, C), F32),) * 3, grid=(R // tr,),
        in_specs=[blk] * 4, out_specs=(blk,) * 3,
    )(w, g, m, v)


ANY = pl.BlockSpec(memory_space=pl.ANY)


def _chip_peers():
    x, y, c = lax.axis_index("x"), lax.axis_index("y"), lax.axis_index("c")
    return x, y, c, [(1 - x, y), (x, 1 - y), (1 - x, 1 - y)]


def _half_rows(c, rh):
    return pl.ds(pl.multiple_of(c * rh, 16), rh)


def _my_chip():
    return 2 * lax.axis_index("x") + lax.axis_index("y")


def _gather_chips(wb, wf):
    rh = wb.shape[0] // 2
    rq = rh // 2

    def body(wb_ref, wf_ref, ob_ref, of_ref, send_sems, recv_sems):
        x, y, c, peers = _chip_peers()
        nbr_x, nbr_y = peers[0], peers[1]
        me, chip_x, chip_y, chip_d = 2 * x + y, 2 * (1 - x) + y, 2 * x + (1 - y), 2 * (1 - x) + (1 - y)

        def quarter(core, b):
            return pl.ds(pl.multiple_of(core * rh + b * rq, 16), rq)

        ici = [(0, nbr_x, me, 0, chip_x), (1, nbr_y, me, 1, chip_y), (2, nbr_y, me, 0, chip_y), (3, nbr_x, me, 1, chip_x),
               (4, nbr_y, chip_x, 0, chip_d), (5, nbr_x, chip_y, 1, chip_d)]

        def ici_copy(k, to, slot, b, own):
            rows = quarter(c, b)
            return pltpu.make_async_remote_copy(
                src_ref=wb_ref.at[rows] if own else ob_ref.at[slot, rows], dst_ref=ob_ref.at[slot, rows],
                send_sem=send_sems.at[k], recv_sem=recv_sems.at[k], device_id=(to[0], to[1], c), device_id_type=MESH)

        def to_sibling(k, slot, b, core):
            rows = quarter(core, b)
            return pltpu.make_async_remote_copy(
                src_ref=ob_ref.at[slot, rows], dst_ref=ob_ref.at[slot, rows], send_sem=send_sems.at[6 + k],
                recv_sem=recv_sems.at[6 + k], device_id=(x, y, 1 - c), device_id_type=MESH)

        def small_copy(k, slot):
            px, py = peers[k]
            return pltpu.make_async_remote_copy(
                src_ref=wf_ref, dst_ref=of_ref.at[slot], send_sem=send_sems.at[12 + k], recv_sem=recv_sems.at[12 + k],
                device_id=(px, py, c), device_id_type=MESH)

        sends = [ici_copy(k, to, slot, b, True) for k, to, slot, b, _ in ici[:4]] + [small_copy(k, me) for k in range(3)]
        for cp in sends:
            cp.start()
        for k, to, slot, b, arrives in ici:
            ici_copy(k, to, arrives, b, False).wait_recv()
            passed = [to_sibling(k, arrives, b, c)]
            if k < 2:
                passed.append(ici_copy(*ici[4 + k][:4], False))
            for cp in passed:
                cp.start()
            sends += passed
        for k, to, slot, b, arrives in ici:
            to_sibling(k, arrives, b, 1 - c).wait_recv()
        chip_of = [chip_x, chip_y, chip_d]
        for k in range(3):
            small_copy(k, chip_of[k]).wait_recv()
        for cp in sends:
            cp.wait_send()

    ob, of = pl.pallas_call(
        body, name="gather_weights",
        out_shape=(jax.ShapeDtypeStruct((4,) + wb.shape, wb.dtype), jax.ShapeDtypeStruct((4,) + wf.shape, wf.dtype)),
        in_specs=[ANY, ANY], out_specs=(ANY, ANY),
        scratch_shapes=[pltpu.SemaphoreType.DMA((15,)), pltpu.SemaphoreType.DMA((15,))],
    )(wb, wf)
    me = _my_chip()
    return lax.dynamic_update_slice(ob, wb[None], (me, 0, 0)), lax.dynamic_update_slice(of, wf[None], (me, 0, 0))


def _halves_to_sibling(gp):
    rh = gp.shape[1] // 2

    def body(gp_ref, o_ref, send_sem, recv_sem):
        x, y, c = lax.axis_index("x"), lax.axis_index("y"), lax.axis_index("c")
        cp = pltpu.make_async_remote_copy(src_ref=gp_ref.at[:, _half_rows(1 - c, rh), :], dst_ref=o_ref, send_sem=send_sem,
                                          recv_sem=recv_sem, device_id=(x, y, 1 - c), device_id_type=MESH)
        cp.start()
        cp.wait()

    return pl.pallas_call(
        body, name="halves_to_sibling", out_shape=jax.ShapeDtypeStruct((gp.shape[0], rh, gp.shape[2]), gp.dtype),
        in_specs=[ANY], out_specs=ANY, scratch_shapes=[pltpu.SemaphoreType.DMA, pltpu.SemaphoreType.DMA],
    )(gp)


def _row_tile(rows, cap=1024):
    tr = 16
    for cand in range(16, cap + 1, 16):
        if rows % cand == 0:
            tr = cand
    return tr


def _add_halves(gp, sib, core):
    n, rh, C = sib.shape
    tr = _row_tile(rh)
    nt = rh // tr

    def body(c_ref, g_ref, s_ref, o_ref):
        o_ref[...] = (g_ref[...].astype(F32) + s_ref[...].astype(F32)).astype(o_ref.dtype)

    blk = pl.BlockSpec((1, tr, C), lambda j, i, c: (j, i, 0))
    return pl.pallas_call(
        body, name="add_halves", out_shape=jax.ShapeDtypeStruct(sib.shape, sib.dtype),
        grid_spec=pltpu.PrefetchScalarGridSpec(
            num_scalar_prefetch=1, grid=(n, nt),
            in_specs=[pl.BlockSpec((1, tr, C), lambda j, i, c: (j, c[0] * nt + i, 0)), blk], out_specs=blk),
    )(core, gp, sib)


def _join_halves(buf):
    rh = buf.shape[0] // 2

    def body(in_ref, o_ref, send_sem, recv_sem):
        x, y, c = lax.axis_index("x"), lax.axis_index("y"), lax.axis_index("c")

        def copy(rows):
            return pltpu.make_async_remote_copy(src_ref=o_ref.at[rows], dst_ref=o_ref.at[rows], send_sem=send_sem,
                                                recv_sem=recv_sem, device_id=(x, y, 1 - c), device_id_type=MESH)

        send = copy(_half_rows(c, rh))
        send.start()
        copy(_half_rows(1 - c, rh)).wait_recv()
        send.wait_send()

    return pl.pallas_call(
        body, name="join_halves", out_shape=jax.ShapeDtypeStruct(buf.shape, buf.dtype),
        in_specs=[ANY], out_specs=ANY, input_output_aliases={0: 0},
        scratch_shapes=[pltpu.SemaphoreType.DMA, pltpu.SemaphoreType.DMA],
    )(buf)


def _exchange_near(gp):
    rq = gp.shape[1] // 2

    def body(gp_ref, out_ref, send_sems, recv_sems):
        x, y, c, peers = _chip_peers()
        chip_x, chip_y, chip_d = 2 * (1 - x) + y, 2 * x + (1 - y), 2 * (1 - x) + (1 - y)
        plan = [(peers[0], chip_x, 0), (peers[0], chip_d, 0), (peers[1], chip_y, 1), (peers[1], chip_d, 1)]
        copies = [pltpu.make_async_remote_copy(
            src_ref=gp_ref.at[slot, pl.ds(b * rq, rq)], dst_ref=out_ref.at[k], send_sem=send_sems.at[k],
            recv_sem=recv_sems.at[k], device_id=(to[0], to[1], c), device_id_type=MESH) for k, (to, slot, b) in enumerate(plan)]
        for cp in copies:
            cp.start()
        for cp in copies:
            cp.wait_recv()
        for cp in copies:
            cp.wait_send()

    return pl.pallas_call(
        body, name="exchange_grads_near", out_shape=jax.ShapeDtypeStruct((4, rq, gp.shape[2]), gp.dtype),
        in_specs=[ANY], out_specs=ANY, scratch_shapes=[pltpu.SemaphoreType.DMA((4,)), pltpu.SemaphoreType.DMA((4,))],
    )(gp)


def _add_near(gp, near, chips):
    _, rq, C = near.shape
    tr = _row_tile(rq)
    nt = rq // tr

    def body(ch_ref, mine_a, mine_b, on_a, on_b, near_ref, part_ref, on_ref):
        part_ref[0] = mine_a[0].astype(F32) + near_ref[0].astype(F32)
        part_ref[1] = mine_b[0].astype(F32) + near_ref[2].astype(F32)
        on_ref[0] = (on_a[0].astype(F32) + near_ref[1].astype(F32)).astype(on_ref.dtype)
        on_ref[1] = (on_b[0].astype(F32) + near_ref[3].astype(F32)).astype(on_ref.dtype)

    def slot(which, b):
        return pl.BlockSpec((1, tr, C), lambda i, ch: (ch[which], b * nt + i, 0))

    return pl.pallas_call(
        body, name="add_near",
        out_shape=(jax.ShapeDtypeStruct((2, rq, C), F32), jax.ShapeDtypeStruct((2, rq, C), near.dtype)),
        grid_spec=pltpu.PrefetchScalarGridSpec(
            num_scalar_prefetch=1, grid=(nt,),
            in_specs=[slot(0, 0), slot(0, 1), slot(2, 0), slot(1, 1), pl.BlockSpec((4, tr, C), lambda i, ch: (0, i, 0))],
            out_specs=(pl.BlockSpec((2, tr, C), lambda i, ch: (0, i, 0)),) * 2),
    )(chips, gp, gp, gp, gp, near)


def _exchange_far(on):
    def body(on_ref, out_ref, send_sems, recv_sems):
        x, y, c, peers = _chip_peers()
        copies = [pltpu.make_async_remote_copy(
            src_ref=on_ref.at[k], dst_ref=out_ref.at[k], send_sem=send_sems.at[k], recv_sem=recv_sems.at[k],
            device_id=(to[0], to[1], c), device_id_type=MESH) for k, to in enumerate((peers[1], peers[0]))]
        for cp in copies:
            cp.start()
        for cp in copies:
            cp.wait_recv()
        for cp in copies:
            cp.wait_send()

    return pl.pallas_call(
        body, name="exchange_grads_far", out_shape=jax.ShapeDtypeStruct(on.shape, on.dtype),
        in_specs=[ANY], out_specs=ANY, scratch_shapes=[pltpu.SemaphoreType.DMA((2,)), pltpu.SemaphoreType.DMA((2,))],
    )(on)


def _add_far(part, far, core):
    _, rq, C = part.shape
    tr = _row_tile(rq)
    nt = rq // tr

    def body(c_ref, p_ref, f_ref, o_ref):
        o_ref[...] = p_ref[0] + f_ref[0].astype(F32)

    blk = pl.BlockSpec((1, tr, C), lambda b, i, c: (b, i, 0))
    return pl.pallas_call(
        body, name="add_far", out_shape=jax.ShapeDtypeStruct((4 * rq, C), F32),
        grid_spec=pltpu.PrefetchScalarGridSpec(
            num_scalar_prefetch=1, grid=(2, nt), in_specs=[blk, blk],
            out_specs=pl.BlockSpec((tr, C), lambda b, i, c: ((2 * c[0] + b) * nt + i, 0))),
    )(core, part, far)


N_DEV = 8


def _allreduce_small(p):
    rs = p.shape[0]

    def body(x_ref, sum_ref, all_ref, send_sems, recv_sems, local_sem):
        x, y, c = lax.axis_index("x"), lax.axis_index("y"), lax.axis_index("c")
        me, sibling = (x, y, c), (x, y, 1 - c)
        chips = [(1 - x, y), (x, 1 - y), (1 - x, 1 - y)]

        def rows(px, py, pc):
            return all_ref.at[pl.ds((4 * px + 2 * py + pc) * rs, rs), :]

        def copy(k, block, to, src=None):
            return pltpu.make_async_remote_copy(
                src_ref=rows(*block) if src is None else src, dst_ref=rows(*block),
                send_sem=send_sems.at[k], recv_sem=recv_sems.at[k], device_id=to, device_id_type=MESH)

        mine = pltpu.make_async_copy(x_ref, rows(*me), local_sem)
        mine.start()
        first = [copy(0, me, sibling, src=x_ref)]
        first += [copy(1 + j, me, (*chip, c), src=x_ref) for j, chip in enumerate(chips)]
        for cp in first:
            cp.start()
        passed = [copy(4 + j, (*chip, c), sibling) for j, chip in enumerate(chips)]
        for j, chip in enumerate(chips):
            copy(1 + j, (*chip, c), me).wait_recv()
            passed[j].start()
        copy(0, sibling, me).wait_recv()
        for j, chip in enumerate(chips):
            copy(4 + j, (*chip, 1 - c), me).wait_recv()
        for cp in first + passed:
            cp.wait_send()
        mine.wait()
        acc = all_ref[0:rs, :]
        for d in range(1, N_DEV):
            acc = acc + all_ref[d * rs:(d + 1) * rs, :]
        sum_ref[...] = acc

    vmem = pl.BlockSpec(memory_space=pltpu.VMEM)
    return pl.pallas_call(
        body, name="allreduce_small", out_shape=jax.ShapeDtypeStruct((rs, 128), F32),
        in_specs=[vmem], out_specs=vmem,
        scratch_shapes=[pltpu.VMEM((N_DEV * rs, 128), F32), pltpu.SemaphoreType.DMA((7,)), pltpu.SemaphoreType.DMA((7,)),
                        pltpu.SemaphoreType.DMA],
    )(p)


WEIGHTS = ('ffn1_norm', 'ffn1_w_gate', 'ffn1_w_up', 'ffn1_w_down', 'mix_norm', 'w_in', 'q_a_norm', 'w_q_b',
           'kv_a_norm', 'w_kv_b', 'q_head_norm', 'k_head_norm', 'conv_w', 'conv_b', 'a_log_fwd', 'a_log_bwd',
           'dt_bias_fwd', 'dt_bias_bwd', 'd_skip', 'ssm_norm', 'w_attn_branch', 'w_ssm_branch', 'w_out',
           'ffn2_norm', 'ffn2_w_gate', 'ffn2_w_up', 'ffn2_w_down')
PACKED = (('ffn1_w_gate', (D_MODEL, D_FF), 1), ('ffn1_w_up', (D_MODEL, D_FF), 1), ('ffn1_w_down', (D_FF, D_MODEL), 0),
          ('w_in', (D_MODEL, sum(IN_SPLITS)), 1), ('w_q_b', (Q_LORA, N_HEADS * QK_HEAD), 1),
          ('w_kv_b', (KV_LORA, N_HEADS * (QK_NOPE + V_HEAD)), 1),
          ('w_attn_branch', (N_HEADS * V_HEAD, D_MODEL), 0), ('w_ssm_branch', (D_INNER, D_MODEL), 0),
          ('w_out', (D_MODEL, D_MODEL), 0),
          ('ffn2_w_gate', (D_MODEL, D_FF), 1), ('ffn2_w_up', (D_MODEL, D_FF), 1), ('ffn2_w_down', (D_FF, D_MODEL), 0))
PACK_W = 1024
N_CHIPS = 4
SMALL = (('ffn1_norm', 1024), ('mix_norm', 1024), ('q_a_norm', 384), ('kv_a_norm', 256), ('q_head_norm', 96),
         ('k_head_norm', 96), ('conv_b', 3072), ('a_log_fwd', 32), ('a_log_bwd', 32), ('dt_bias_fwd', 32),
         ('dt_bias_bwd', 32), ('d_skip', 32), ('ssm_norm', 2048), ('ffn2_norm', 1024),
         ('conv_w', CONV_WIDTH * XBC_DIM), ('loss', 1))


TRANSPOSED = ('ffn1_w_gate', 'ffn1_w_up', 'w_in', 'ffn2_w_gate', 'ffn2_w_up')


def _stored(name, a):
    return a.T if name in TRANSPOSED else a


def _shard_shape(name, shape, axis):
    sh = tuple(s // N_CHIPS if a == axis else s for a, s in enumerate(shape))
    return sh[::-1] if name in TRANSPOSED else sh


def _by_rows(name, axis):
    return name in TRANSPOSED or axis == 0


def _pack_layout():
    out, r = {}, 0
    for name, shape, axis in PACKED:
        n = math.prod(shape) // N_CHIPS // PACK_W
        out[name] = (r, n)
        r += n
    return out, -(-r // 64) * 64


def _pack(shards):
    layout, rows = _pack_layout()
    parts = [shards[name].reshape(-1, PACK_W) for name, _, _ in PACKED]
    parts.append(jnp.zeros((rows - sum(p.shape[0] for p in parts), PACK_W), parts[0].dtype))
    return jnp.concatenate(parts, axis=0)


def _unpack(packed):
    layout, _ = _pack_layout()
    return {name: packed[layout[name][0]:layout[name][0] + layout[name][1]].reshape(_shard_shape(name, shape, axis))
            for name, shape, axis in PACKED}


def _full_from_slots(slots):
    layout, _ = _pack_layout()
    out = {}
    for name, shape, axis in PACKED:
        r, n = layout[name]
        if _by_rows(name, axis):
            out[name] = slots[:, r:r + n].reshape(N_CHIPS * n, PACK_W)
        else:
            sh = _shard_shape(name, shape, axis)
            out[name] = jnp.concatenate([slots[j, r:r + n].reshape(sh) for j in range(N_CHIPS)], axis=axis)
    return out


def _slots_from_full(full):
    layout, rows = _pack_layout()
    parts = []
    for name, shape, axis in PACKED:
        r, n = layout[name]
        if _by_rows(name, axis):
            parts.append(full[name].reshape(N_CHIPS, n, PACK_W))
        else:
            size = shape[axis] // N_CHIPS
            parts.append(jnp.stack([lax.slice_in_dim(full[name], j * size, (j + 1) * size, axis=axis).reshape(n, PACK_W)
                                    for j in range(N_CHIPS)]))
    parts.append(jnp.zeros((N_CHIPS, rows - sum(p.shape[1] for p in parts), PACK_W), parts[0].dtype))
    return jnp.concatenate(parts, axis=1)


def _pack_small(vals):
    parts = []
    for name, n in SMALL:
        pad = -(-n // 128) * 128 - n
        parts.append(jnp.pad(vals[name].reshape(-1).astype(F32), (0, pad)).reshape(-1, 128))
    rows = sum(p.shape[0] for p in parts)
    parts.append(jnp.zeros((-(-rows // 8) * 8 - rows, 128), F32))
    return jnp.concatenate(parts, axis=0)


def _unpack_small(packed):
    out, r = {}, 0
    for name, n in SMALL:
        k = -(-n // 128)
        out[name] = packed[r:r + k].reshape(-1)[:n]
        r += k
    return out


def _pad_heads(w, axis, per_head, lo, hi):
    shape = w.shape
    w = w.reshape(shape[:axis] + (N_HEADS, per_head) + shape[axis + 1:])
    w = lax.slice_in_dim(w, lo, hi, axis=axis + 1)
    pad = [(0, 0)] * w.ndim
    pad[axis + 1] = (0, HP - (hi - lo))
    w = jnp.pad(w, pad)
    return w.reshape(shape[:axis] + (N_HEADS * HP,) + shape[axis + 1:])


def _unpad_heads(w, axis, keep):
    shape = w.shape
    w = w.reshape(shape[:axis] + (N_HEADS, HP) + shape[axis + 1:])
    return lax.slice_in_dim(w, 0, keep, axis=axis + 1)


def _pad_w_in(wt):
    o = [0]
    for s in IN_SPLITS:
        o.append(o[-1] + s)
    cq, ckv, kpe, z, xbc, dtf, dtb, ga, gb = [wt[o[i]:o[i + 1]] for i in range(len(IN_SPLITS))]
    kpe_pad = jnp.pad(kpe, ((QK_NOPE, HP - QK_HEAD), (0, 0)))
    dt_pad = jnp.pad(jnp.concatenate([dtf, dtb], axis=0), ((0, HP - 2 * SSM_HEADS), (0, 0)))
    return jnp.concatenate([z, ga, gb, xbc, cq, ckv, kpe_pad, dt_pad], axis=0)


def _unpad_w_in(gt):
    z, ga, gb, xbc = gt[U_Z:U_GA], gt[U_GA:U_GB], gt[U_GB:U_XBC], gt[U_XBC:U_SMALL]
    s = gt[U_SMALL:]
    cq, ckv = s[S_CQ:S_CKV], s[S_CKV:S_KPE]
    kpe = s[S_KPE + QK_NOPE:S_KPE + QK_HEAD]
    dtf, dtb = s[S_DT:S_DT + SSM_HEADS], s[S_DT + SSM_HEADS:S_DT + 2 * SSM_HEADS]
    return jnp.concatenate([cq, ckv, kpe, z, xbc, dtf, dtb, ga, gb], axis=0)


def _lanes128(parts):
    row = jnp.concatenate([p.reshape(-1) for p in parts])
    return jnp.pad(row, (0, HP - row.shape[0])).reshape(1, HP)


FF_TILE = D_FF // 2
WGRAD = BF


def _ffn_fwd(x, g, wg_t, wu_t, wd, tag):
    h = _rms_fwd(x, g, name=tag + "_norm")
    gate, up, act = _mm([h], [wg_t, wu_t], name=tag + "_up", tb=True, out_dtypes=(BF, BF, BF), tm=512, tn=FF_TILE,
                        epilogue=lambda a, b: (a, b, _silu(a) * b))
    out = _mm([act], [wd], name=tag + "_down", extras=[x], epilogue=lambda acc, r: (r + 0.5 * acc,))
    return out, (h, gate, up, act)


def _ffn_bwd(dout, dout_bf, x, g, wg_t, wu_t, wd, saved, tag):
    h, gate, up, act = saved

    def swiglu_bwd(acc, a, b):
        a, b, half = a.astype(F32), b.astype(F32), 0.5 * acc
        s = jax.nn.sigmoid(a)
        return half * b * (s * (1.0 + a * (1.0 - s))), half * (a * s)

    dgate, dup = _mm([dout_bf], [wd], name=tag + "_down_dx", tb=True, extras=[gate, up], out_dtypes=(BF, BF),
                     tm=512, tn=FF_TILE, epilogue=swiglu_bwd)
    dwd = _mm([act], [dout_bf], name=tag + "_down_dw", ta=True, tm=FF_TILE, out_dtypes=(WGRAD,),
              epilogue=lambda acc: (0.5 * acc,))
    dwg_t, dwu_t = _mm([dgate, dup], [h, h], name=tag + "_up_dw", ta=True, separate=True, out_dtypes=(WGRAD, WGRAD),
                       tm=FF_TILE)
    dh = _mm([dgate, dup], [wg_t, wu_t], name=tag + "_up_dx")
    dx, dx_bf, dg = _rms_bwd(dh, x, g, name=tag + "_norm_bwd", add=dout, out_dtypes=(F32, BF))
    return dx, dx_bf, dg, dwg_t, dwu_t, dwd


KPE_BLK = (U_SMALL + S_KPE) // HP
SMALL_BLK = U_SMALL // SMALL_W


def _local_step(x, pos_col, target, W, P):
    T = x.shape[0]
    sig = jax.nn.sigmoid
    x1, ffn1 = _ffn_fwd(x, P["ffn1_norm"], W["wg1"], W["wu1"], W["wd1"], "ffn1")
    h = _rms_fwd(x1, P["mix_norm"], name="mix_norm")
    u = _mm([h], [W["w_in"]], name="in_proj", tb=True, tn=1152)
    cqn = _rms_fwd(u, P["q_a_norm"], name="q_a_norm", blk_w=SMALL_W, blk_idx=SMALL_BLK, off=S_CQ, width=Q_LORA)
    ckvn = _rms_fwd(u, P["kv_a_norm"], name="kv_a_norm", blk_w=SMALL_W, blk_idx=SMALL_BLK, off=S_CKV, width=KV_LORA)
    q_raw = _mm([cqn], [W["wq"]], name="q_proj")
    def with_ones_lane(acc_k, acc_v):
        lane = lax.broadcasted_iota(jnp.int32, acc_v.shape, 1)
        return acc_k, jnp.where((lane & (HP - 1)) == V_HEAD, 1.0, acc_v)

    k_raw, v = _mm([ckvn], [W["wk"], W["wv"]], name="kv_proj", out_dtypes=(F32, BF), epilogue=with_ones_lane)
    rc, rs = _rope_tables(pos_col, P["freq"])
    q = _qk_prep_fwd(q_raw, None, P["q_head_norm"], rc, rs, name="q_prep", out_scale=Q_SCALE)
    k = _qk_prep_fwd(k_raw, u, P["k_head_norm"], rc, rs, name="k_prep", kpe_blk=KPE_BLK)
    o, o_bf, lse = _attn_fwd(q, k, v)
    pre, act = _conv_fwd(u, P["conv_w8"], P["conv_b"])
    scan_cols, scan_rows = _ssd_prep(u, P["dt_bias8"], P["a_log8"])
    y_f, st_f = _ssd_fwd(act, scan_cols, scan_rows, rev=False, name="ssd_fwd_f")
    y_fb, st_b = _ssd_fwd(act, scan_cols, scan_rows, rev=True, name="ssd_fwd_b", add=y_f)
    ysum, m = _ssm_combine_fwd(y_fb, act, u, P["d_skip_lanes"], P["ssm_norm"])
    ab = _mm([o_bf], [W["pa"]], name="attn_branch")
    mb, merged = _mm([m], [W["pb"]], name="ssm_branch", extras=[ab, u, u], extra_offs=(0, U_GA, U_GB), out_dtypes=(F32, BF),
                     epilogue=lambda acc, a, ga, gb: (acc, sig(ga) * a + sig(gb) * acc))
    x2 = _mm([merged], [W["wo"]], name="out_proj", extras=[x1], epilogue=lambda acc, r: (r + acc,))
    y, ffn2 = _ffn_fwd(x2, P["ffn2_norm"], W["wg2"], W["wu2"], W["wd2"], "ffn2")
    dy, dy_bf, loss = _loss_head(y, target)
    dx2, dx2_bf, dg_ffn2, dwg2, dwu2, dwd2 = _ffn_bwd(dy, dy_bf, x2, P["ffn2_norm"], W["wg2"], W["wu2"], W["wd2"], ffn2,
                                                      "ffn2")

    def gate_bwd(dmrg, a, b, ga, gb):
        sa, sb = sig(ga), sig(gb)
        return dmrg * sa, dmrg * sb, dmrg * a * sa * (1.0 - sa), dmrg * b * sb * (1.0 - sb)

    dab, dmb, dga, dgb = _mm([dx2_bf], [W["wo"]], name="out_proj_dx", tb=True, extras=[ab, mb, u, u],
                             extra_offs=(0, 0, U_GA, U_GB), out_dtypes=(BF,) * 4, epilogue=gate_bwd)
    dwo = _mm([merged], [dx2_bf], name="out_proj_dw", ta=True, out_dtypes=(WGRAD,))
    dpa = _mm([o_bf], [dab], name="attn_branch_dw", ta=True, out_dtypes=(WGRAD,))
    do = _mm([dab], [W["pa"]], name="attn_branch_dx", tb=True)
    dpb = _mm([m], [dmb], name="ssm_branch_dw", ta=True, out_dtypes=(WGRAD,))
    dm = _mm([dmb], [W["pb"]], name="ssm_branch_dx", tb=True)
    dyssd, dz, dg_ssm, dskip = _ssm_combine_bwd(dm, ysum, act, u, P["ssm_norm"])
    dxs_f, db_f, dc_f, dsel_f, dtot_f = _ssd_bwd(act, scan_cols, scan_rows, st_f, dyssd, rev=False, name="ssd_bwd_f",
                                                 skip=P["d_skip_lanes"])
    dxs, db, dc, dsel_b, dtot_b = _ssd_bwd(act, scan_cols, scan_rows, st_b, dyssd, rev=True, name="ssd_bwd_b",
                                           add=(dxs_f, db_f, dc_f))
    ddt, dalog, dbias = _ssd_prep_bwd(u, P["dt_bias8"], P["a_log8"], dsel_f, dtot_f, dsel_b, dtot_b)
    dxbc, dconv = [], []
    for tag, col0, part in (("x", 0, dxs), ("b", D_INNER, db), ("c", D_INNER + SSM_GROUPS * D_STATE, dc)):
        dxp, dwp = _conv_bwd([part], pre, u, P["conv_w8"], col0, name="conv_bwd_" + tag)
        dxbc.append(dxp)
        dconv.append(dwp)
    dconv = jnp.concatenate(dconv, axis=1)
    dq, dk, dv = _attn_bwd(q, k, v, do, o, lse)
    dq_raw, dg_qh = _qk_prep_bwd(dq, q_raw, None, P["q_head_norm"], rc, rs, name="q_prep_bwd", in_scale=ATTN_SCALE)
    dk_raw, dg_kh, dkpe = _qk_prep_bwd(dk, k_raw, u, P["k_head_norm"], rc, rs, name="k_prep_bwd", kpe_blk=KPE_BLK,
                                       in_scale=1.0 / LOG2E)
    dwq = _mm([cqn], [dq_raw], name="q_proj_dw", ta=True, out_dtypes=(WGRAD,))
    dcqn = _mm([dq_raw], [W["wq"]], name="q_proj_dx", tb=True)
    dwk, dwv = _mm([ckvn], [dk_raw, dv], name="kv_proj_dw", ta=True, out_dtypes=(WGRAD, WGRAD))
    dckvn = _mm([dk_raw, dv], [W["wk"], W["wv"]], name="kv_proj_dx", tb=True)
    dcq, dg_qa = _rms_bwd(dcqn, u, P["q_a_norm"], name="q_a_norm_bwd", blk_w=SMALL_W, blk_idx=SMALL_BLK, off=S_CQ,
                          width=Q_LORA, out_dtypes=(BF,))
    dckv, dg_kva = _rms_bwd(dckvn, u, P["kv_a_norm"], name="kv_a_norm_bwd", blk_w=SMALL_W, blk_idx=SMALL_BLK,
                            off=S_CKV, width=KV_LORA, out_dtypes=(BF,))
    du = jnp.concatenate([dz, dga, dgb] + dxbc + [dcq, dckv, dkpe.astype(BF), ddt.astype(BF)], axis=1)
    dw_in = _mm([du], [h], name="in_proj_dw", ta=True, tm=1152, out_dtypes=(WGRAD,))
    dh = _mm([du], [W["w_in"]], name="in_proj_dx")
    dx1, dx1_bf, dg_mix = _rms_bwd(dh, x1, P["mix_norm"], name="mix_norm_bwd", add=dx2, out_dtypes=(F32, BF))
    dx, _, dg_ffn1, dwg1, dwu1, dwd1 = _ffn_bwd(dx1, dx1_bf, x, P["ffn1_norm"], W["wg1"], W["wu1"], W["wd1"], ffn1, "ffn1")
    dW = dict(wg1=dwg1, wu1=dwu1, wd1=dwd1, w_in=dw_in, wq=dwq, wk=dwk, wv=dwv, pa=dpa, pb=dpb, wo=dwo,
              wg2=dwg2, wu2=dwu2, wd2=dwd2)
    dP = dict(ffn1_norm=dg_ffn1[0], mix_norm=dg_mix[0], q_a_norm=dg_qa[0], kv_a_norm=dg_kva[0],
              q_head_norm=dg_qh[0, :QK_HEAD], k_head_norm=dg_kh[0, :QK_HEAD], conv_b=dconv[CONV_WIDTH],
              a_log_fwd=dalog[0, :SSM_HEADS], a_log_bwd=dalog[0, SSM_HEADS:2 * SSM_HEADS],
              dt_bias_fwd=dbias[0, :SSM_HEADS], dt_bias_bwd=dbias[0, SSM_HEADS:2 * SSM_HEADS],
              d_skip=dskip[0].reshape(SSM_GROUPS, HP)[:, :HG], ssm_norm=dg_ssm[0], ffn2_norm=dg_ffn2[0],
              conv_w=dconv[:CONV_WIDTH], loss=loss[0, 0])
    return dx, dW, dP


def _prepare(w, conv_w_full):
    kvb = w["w_kv_b"]
    W = dict(wg1=w["ffn1_w_gate"], wu1=w["ffn1_w_up"], wd1=w["ffn1_w_down"], w_in=_pad_w_in(w["w_in"]),
             wq=_pad_heads(w["w_q_b"], 1, QK_HEAD, 0, QK_HEAD),
             wk=_pad_heads(kvb, 1, QK_NOPE + V_HEAD, 0, QK_NOPE),
             wv=_pad_heads(kvb, 1, QK_NOPE + V_HEAD, QK_NOPE, QK_NOPE + V_HEAD),
             pa=_pad_heads(w["w_attn_branch"], 0, V_HEAD, 0, V_HEAD), pb=w["w_ssm_branch"], wo=w["w_out"],
             wg2=w["ffn2_w_gate"], wu2=w["ffn2_w_up"], wd2=w["ffn2_w_down"])
    inv_freq = [1.0 / (ROPE_BASE ** (j / QK_ROPE)) for j in range(0, QK_ROPE, 2)]
    freq = [0.0] * QK_NOPE + inv_freq + inv_freq + [0.0] * (HP - QK_HEAD)
    P = {n: w[n] for n in ("ffn1_norm", "mix_norm", "q_a_norm", "kv_a_norm", "ssm_norm", "ffn2_norm", "conv_b")}
    P.update(q_head_norm=_lanes128([w["q_head_norm"]]), k_head_norm=_lanes128([w["k_head_norm"]]),
             conv_w8=jnp.pad(conv_w_full, ((0, 8 - CONV_WIDTH), (0, 0))),
             dt_bias8=jnp.broadcast_to(_lanes128([w["dt_bias_fwd"], w["dt_bias_bwd"]]), (8, HP)),
             a_log8=jnp.broadcast_to(_lanes128([w["a_log_fwd"], w["a_log_bwd"]]), (8, HP)),
             d_skip_lanes=jnp.repeat(w["d_skip"].reshape(-1), PH).reshape(1, D_INNER),
             freq=jnp.asarray(freq, F32).reshape(1, HP))
    return W, P


def _unprepare(dW):
    dkvb = jnp.concatenate([_unpad_heads(dW["wk"], 1, QK_NOPE), _unpad_heads(dW["wv"], 1, V_HEAD)], axis=2)
    return dict(ffn1_w_gate=dW["wg1"], ffn1_w_up=dW["wu1"], ffn1_w_down=dW["wd1"], w_in=_unpad_w_in(dW["w_in"]),
                w_q_b=_unpad_heads(dW["wq"], 1, QK_HEAD).reshape(Q_LORA, N_HEADS * QK_HEAD),
                w_kv_b=dkvb.reshape(KV_LORA, N_HEADS * (QK_NOPE + V_HEAD)),
                w_attn_branch=_unpad_heads(dW["pa"], 0, V_HEAD).reshape(N_HEADS * V_HEAD, D_MODEL),
                w_ssm_branch=dW["pb"], w_out=dW["wo"],
                ffn2_w_gate=dW["wg2"], ffn2_w_up=dW["wu2"], ffn2_w_down=dW["wd2"])


def kernel(x, positions, ffn1_norm, ffn1_w_gate, ffn1_w_up, ffn1_w_down, mix_norm, w_in, q_a_norm, w_q_b, kv_a_norm, w_kv_b, q_head_norm, k_head_norm, conv_w, conv_b, a_log_fwd, a_log_bwd, dt_bias_fwd, dt_bias_bwd, d_skip, ssm_norm, w_attn_branch, w_ssm_branch, w_out, ffn2_norm, ffn2_w_gate, ffn2_w_up, ffn2_w_down, loss_target, m_ffn1_norm, m_ffn1_w_gate, m_ffn1_w_up, m_ffn1_w_down, m_mix_norm, m_w_in, m_q_a_norm, m_w_q_b, m_kv_a_norm, m_w_kv_b, m_q_head_norm, m_k_head_norm, m_conv_w, m_conv_b, m_a_log_fwd, m_a_log_bwd, m_dt_bias_fwd, m_dt_bias_bwd, m_d_skip, m_ssm_norm, m_w_attn_branch, m_w_ssm_branch, m_w_out, m_ffn2_norm, m_ffn2_w_gate, m_ffn2_w_up, m_ffn2_w_down, v_ffn1_norm, v_ffn1_w_gate, v_ffn1_w_up, v_ffn1_w_down, v_mix_norm, v_w_in, v_q_a_norm, v_w_q_b, v_kv_a_norm, v_w_kv_b, v_q_head_norm, v_k_head_norm, v_conv_w, v_conv_b, v_a_log_fwd, v_a_log_bwd, v_dt_bias_fwd, v_dt_bias_bwd, v_d_skip, v_ssm_norm, v_w_attn_branch, v_w_ssm_branch, v_w_out, v_ffn2_norm, v_ffn2_w_gate, v_ffn2_w_up, v_ffn2_w_down):
    given = dict(locals())
    T = x.shape[1]
    packed_names = [name for name, _, _ in PACKED]

    def two_d(a):
        return a.reshape(a.shape[1], -1) if a.ndim > 2 else a

    def kept(n, a):
        return _stored(n, two_d(a))

    w_loc = {n: kept(n, given[n]) for n in WEIGHTS}
    wb = _pack({n: w_loc[n].astype(BF) for n in packed_names})
    wf = jnp.pad(w_loc["conv_w"], ((0, 8 - CONV_WIDTH), (0, 0)))
    gb, gf = _gather_chips(wb, wf)
    full = _full_from_slots(gb)
    conv_w_full = jnp.concatenate([gf[j, :CONV_WIDTH] for j in range(N_CHIPS)], axis=1)
    full.update({n: w_loc[n] for n in WEIGHTS if n not in full and n != "conv_w"})
    W, P = _prepare(full, conv_w_full)
    dx, dW, dP = _local_step(x.reshape(T, D_MODEL), positions.reshape(T, 1).astype(F32), loss_target.reshape(T, D_MODEL), W, P)
    gp = _slots_from_full(_unprepare(dW))
    core = lax.axis_index("c").astype(jnp.int32).reshape(1)
    both_cores = _add_halves(gp, _halves_to_sibling(gp), core)
    cx, cy = lax.axis_index("x"), lax.axis_index("y")
    chips = jnp.stack([2 * cx + cy, 2 * (1 - cx) + cy, 2 * cx + (1 - cy)]).astype(jnp.int32)
    part, on = _add_near(both_cores, _exchange_near(both_cores), chips)
    grads = _unpack(_join_halves(_add_far(part, _exchange_far(on), core)))
    small = _unpack_small(_allreduce_small(_pack_small(dP)))
    grads.update({n: small[n].reshape(1, -1) for n, _ in SMALL if n not in ("conv_w", "loss")})
    grads["conv_w"] = lax.dynamic_slice_in_dim(small["conv_w"].reshape(CONV_WIDTH, XBC_DIM), _my_chip() * (XBC_DIM // N_CHIPS),
                                               XBC_DIM // N_CHIPS, axis=1)
    out_g, out_d, out_m, out_v = [], [], [], []
    for n in WEIGHTS:
        shape = given[n].shape
        delta, new_m, new_v = _adamw(w_loc[n], grads[n], kept(n, given["m_" + n]), kept(n, given["v_" + n]), name="adamw_" + n)
        for outs, a in ((out_g, grads[n]), (out_d, delta), (out_m, new_m), (out_v, new_v)):
            outs.append(_stored(n, a).reshape(shape))
    return (small["loss"].reshape(()), dx.reshape(x.shape), *out_g, *out_d, *out_m, *out_v)
```

```python
import functools
import math

import jax
import jax.numpy as jnp
from jax import lax
from jax.experimental import pallas as pl
from jax.experimental.pallas import tpu as pltpu

BF = jnp.bfloat16
F32 = jnp.float32
HI = lax.Precision.HIGHEST
MESH = pl.DeviceIdType.MESH

D_MODEL = 1024
D_FF = 2816
EPS = 1e-6
N_HEADS = 16
QK_NOPE = 64
QK_ROPE = 32
QK_HEAD = 96
V_HEAD = 64
Q_LORA = 384
KV_LORA = 256
ROPE_BASE = 10000.0
D_INNER = 2048
SSM_HEADS = 32
SSM_GROUPS = 4
D_STATE = 128
CONV_WIDTH = 5
CHUNK = 128
XBC_DIM = 3072
HP = 128
GW = D_INNER // SSM_GROUPS
HG = SSM_HEADS // SSM_GROUPS
PH = 64
U_Z, U_GA, U_GB, U_XBC, U_SMALL = 0, 2048, 3072, 4096, 7168
S_CQ, S_CKV, S_KPE, S_DT, SMALL_W = 0, 384, 640, 768, 896
U_PAD = U_SMALL + SMALL_W
IN_SPLITS = (Q_LORA, KV_LORA, QK_ROPE, D_INNER, XBC_DIM, SSM_HEADS, SSM_HEADS, D_MODEL, D_MODEL)

ADAM_LR = 0.001
ADAM_B1 = 0.9
ADAM_B2 = 0.999
ADAM_EPS = 1e-08
ADAM_WD = 0.01
ADAM_STEP = 10

V7X_VMEM_BYTES = 64 << 20
MM_VMEM_BUDGET = V7X_VMEM_BYTES * 13 // 16

NT = (((1,), (1,)), ((), ()))
TN = (((0,), (0,)), ((), ()))


def _pick(n, pref):
    best = None
    d = 128
    while d <= min(n, pref):
        if n % d == 0:
            best = d
        d += 128
    return best if best is not None else n


def _silu(x):
    return x * jax.nn.sigmoid(x)


def _dsilu(x):
    s = jax.nn.sigmoid(x)
    return s * (1.0 + x * (1.0 - s))


def _softplus(x):
    return jnp.maximum(x, 0.0) + jnp.log(1.0 + jnp.exp(-jnp.abs(x)))


def _mm(As, Bs, *, name, ta=False, tb=False, out_dtypes=(F32,), epilogue=None, extras=(), extra_offs=None,
        tm=1024, tn=512, tk=None, separate=False):
    As, Bs, extras = list(As), list(Bs), list(extras)
    a0, b0 = As[0], Bs[0]
    M, K = (a0.shape[1], a0.shape[0]) if ta else a0.shape
    N = b0.shape[0] if tb else b0.shape[1]
    tm, tn = _pick(M, tm), _pick(N, tn)
    n_a, n_b, n_e, n_o = len(As), len(Bs), len(extras), len(out_dtypes)
    n_res = n_b if n_a == 1 or separate else 1

    def vmem_bytes(k_tile):
        blocks = sum(tm * k_tile * a.dtype.itemsize for a in As) + sum(k_tile * tn * b.dtype.itemsize for b in Bs)
        tiles = tm * tn * (sum(jnp.dtype(dt).itemsize for dt in out_dtypes) + sum(e.dtype.itemsize for e in extras))
        return 2 * (blocks + tiles) + 2 * n_res * tm * tn * 4

    if tk is None:
        tk = K
        while vmem_bytes(tk) > MM_VMEM_BUDGET and tk > 128:
            tk = _pick(K, tk - 128)
    else:
        tk = _pick(K, tk)
    nk = K // tk
    n_acc = n_res if nk > 1 else 0
    if extra_offs is None:
        extra_offs = (0,) * n_e
    dn = (((0,) if ta else (1,), (1,) if tb else (0,)), ((), ()))
    bytes_a = sum(a.size * a.dtype.itemsize for a in As)
    bytes_b = sum(b.size * b.dtype.itemsize for b in Bs)
    n_outer = (N // tn) * bytes_a + bytes_b < (M // tm) * bytes_b + bytes_a

    def products(a_refs, b_refs):
        if n_a == 1:
            a = a_refs[0][...].astype(BF)
            return [lax.dot_general(a, b[...].astype(BF), dn, preferred_element_type=F32) for b in b_refs]
        if separate:
            return [lax.dot_general(a[...].astype(BF), b[...].astype(BF), dn, preferred_element_type=F32)
                    for a, b in zip(a_refs, b_refs)]
        total = None
        for a, b in zip(a_refs, b_refs):
            p = lax.dot_general(a[...].astype(BF), b[...].astype(BF), dn, preferred_element_type=F32)
            total = p if total is None else total + p
        return [total]

    def finish(accs, e_refs, o_refs):
        ex = [e[...] for e in e_refs]
        outs = epilogue(*accs, *ex) if epilogue is not None else tuple(accs)
        for o_ref, val in zip(o_refs, outs):
            o_ref[...] = val.astype(o_ref.dtype)

    def body(*refs):
        a_refs, b_refs = refs[:n_a], refs[n_a:n_a + n_b]
        e_refs = refs[n_a + n_b:n_a + n_b + n_e]
        o_refs = refs[n_a + n_b + n_e:n_a + n_b + n_e + n_o]
        acc_refs = refs[n_a + n_b + n_e + n_o:]
        if nk == 1:
            finish(products(a_refs, b_refs), e_refs, o_refs)
            return
        k = pl.program_id(2)

        @pl.when(k == 0)
        def _():
            for acc in acc_refs:
                acc[...] = jnp.zeros_like(acc)

        for acc, p in zip(acc_refs, products(a_refs, b_refs)):
            acc[...] += p

        @pl.when(k == nk - 1)
        def _():
            finish([acc[...] for acc in acc_refs], e_refs, o_refs)

    def at(f):
        return (lambda j, i, k: f(i, j, k)) if n_outer else f

    a_spec = pl.BlockSpec((tk, tm), at(lambda i, j, k: (k, i))) if ta else pl.BlockSpec((tm, tk), at(lambda i, j, k: (i, k)))
    b_spec = pl.BlockSpec((tn, tk), at(lambda i, j, k: (j, k))) if tb else pl.BlockSpec((tk, tn), at(lambda i, j, k: (k, j)))
    e_specs = [pl.BlockSpec((tm, tn), at(functools.partial(lambda i, j, k, o: (i, j + o), o=off // tn))) for off in extra_offs]
    for off in extra_offs:
        assert off % tn == 0
    outs = pl.pallas_call(
        body, name=name,
        out_shape=tuple(jax.ShapeDtypeStruct((M, N), dt) for dt in out_dtypes),
        grid=(N // tn, M // tm, nk) if n_outer else (M // tm, N // tn, nk),
        in_specs=[a_spec] * n_a + [b_spec] * n_b + e_specs,
        out_specs=tuple(pl.BlockSpec((tm, tn), at(lambda i, j, k: (i, j))) for _ in out_dtypes),
        scratch_shapes=[pltpu.VMEM((tm, tn), F32)] * n_acc,
        compiler_params=pltpu.CompilerParams(dimension_semantics=("parallel", "parallel", "arbitrary")),
    )(*As, *Bs, *extras)
    return outs[0] if n_o == 1 else outs


def _rms_fwd(x, g, *, name, blk_w=None, blk_idx=0, off=0, width=None, out_dtype=BF):
    T = x.shape[0]
    blk_w = x.shape[1] if blk_w is None else blk_w
    width = blk_w if width is None else width
    tt = _pick(T, 512)

    def body(x_ref, g_ref, o_ref):
        xf = x_ref[:, off:off + width]
        r = lax.rsqrt(jnp.mean(xf * xf, axis=-1, keepdims=True) + EPS)
        o_ref[...] = (xf * r * g_ref[...]).astype(o_ref.dtype)

    return pl.pallas_call(
        body, name=name, out_shape=jax.ShapeDtypeStruct((T, width), out_dtype), grid=(T // tt,),
        in_specs=[pl.BlockSpec((tt, blk_w), lambda i: (i, blk_idx)), pl.BlockSpec((1, width), lambda i: (0, 0))],
        out_specs=pl.BlockSpec((tt, width), lambda i: (i, 0)),
    )(x, g)


def _rms_bwd(dy, x, g, *, name, blk_w=None, blk_idx=0, off=0, width=None, add=None, out_dtypes=(F32,)):
    T = x.shape[0]
    blk_w = x.shape[1] if blk_w is None else blk_w
    width = blk_w if width is None else width
    tt = _pick(T, 512)
    has_add = add is not None
    n_dx = len(out_dtypes)

    def body(*refs):
        dy_ref, x_ref, g_ref = refs[:3]
        dx_refs, dg_ref = refs[3 + has_add:3 + has_add + n_dx], refs[-1]
        xf = x_ref[:, off:off + width]
        d = dy_ref[...].astype(F32)
        r = lax.rsqrt(jnp.mean(xf * xf, axis=-1, keepdims=True) + EPS)
        gd = d * g_ref[...]
        dx = r * gd - xf * (r * r * r) * jnp.mean(gd * xf, axis=-1, keepdims=True)
        if has_add:
            dx = dx + refs[3][...]
        for dx_ref in dx_refs:
            dx_ref[...] = dx.astype(dx_ref.dtype)

        @pl.when(pl.program_id(0) == 0)
        def _():
            dg_ref[...] = jnp.zeros_like(dg_ref)

        dg_ref[...] += jnp.broadcast_to(jnp.sum(d * xf * r, axis=0, keepdims=True), dg_ref.shape)

    row = pl.BlockSpec((tt, width), lambda i: (i, 0))
    in_specs = [row, pl.BlockSpec((tt, blk_w), lambda i: (i, blk_idx)), pl.BlockSpec((1, width), lambda i: (0, 0))]
    args = [dy, x, g]
    if has_add:
        in_specs.append(row)
        args.append(add)
    return pl.pallas_call(
        body, name=name,
        out_shape=tuple(jax.ShapeDtypeStruct((T, width), dt) for dt in out_dtypes) + (jax.ShapeDtypeStruct((8, width), F32),),
        grid=(T // tt,), in_specs=in_specs,
        out_specs=(row,) * n_dx + (pl.BlockSpec((8, width), lambda i: (0, 0)),),
        compiler_params=pltpu.CompilerParams(dimension_semantics=("arbitrary",)),
    )(*args)


def _rope_tables(pos_col, freq_lane):
    T = pos_col.shape[0]
    tt = _pick(T, 512)

    def body(p_ref, f_ref, c_ref, s_ref):
        ang = p_ref[...] * f_ref[...]
        lane = lax.broadcasted_iota(jnp.int32, ang.shape, 1)
        c_ref[...] = jnp.where(lane < QK_HEAD, jnp.cos(ang), 0.0)
        sn = jnp.sin(ang)
        s_ref[...] = jnp.where((lane >= QK_NOPE) & (lane < QK_NOPE + 16), -sn,
                               jnp.where((lane >= QK_NOPE + 16) & (lane < QK_HEAD), sn, 0.0))

    return pl.pallas_call(
        body, name="rope_tables", out_shape=(jax.ShapeDtypeStruct((T, HP), F32),) * 2, grid=(T // tt,),
        in_specs=[pl.BlockSpec((tt, 1), lambda i: (i, 0)), pl.BlockSpec((1, HP), lambda i: (0, 0))],
        out_specs=(pl.BlockSpec((tt, HP), lambda i: (i, 0)),) * 2,
    )(pos_col, freq_lane)


def _swap_rope_halves(n):
    src = lax.broadcasted_iota(jnp.int32, (HP, HP), 0)
    dst = lax.broadcasted_iota(jnp.int32, (HP, HP), 1)
    lo = (dst >= QK_NOPE) & (dst < QK_NOPE + 16) & (src == dst + 16)
    hi = (dst >= QK_NOPE + 16) & (dst < QK_HEAD) & (src == dst - 16)
    return _split_dot(n, jnp.where(lo | hi, 1.0, 0.0).astype(BF), 2)


def _qk_prep_fwd(raw, kpe, gain, C, S, *, name, kpe_blk=0, out_scale=1.0):
    T = raw.shape[0]
    tt = _pick(T, 256)
    has_kpe = kpe is not None

    def body(*refs):
        if has_kpe:
            raw_ref, kpe_ref, g_ref, c_ref, s_ref, o_ref = refs
        else:
            raw_ref, g_ref, c_ref, s_ref, o_ref = refs
        for h in range(N_HEADS):
            hs = slice(HP * h, HP * (h + 1))
            xr = raw_ref[:, hs] + kpe_ref[...] if has_kpe else raw_ref[:, hs]
            r = lax.rsqrt(jnp.sum(xr * xr, axis=-1, keepdims=True) * (1.0 / QK_HEAD) + EPS)
            n = xr * r * g_ref[...]
            o_ref[:, hs] = ((n * c_ref[...] + _swap_rope_halves(n) * s_ref[...]) * out_scale).astype(o_ref.dtype)

    heads = pl.BlockSpec((tt, N_HEADS * HP), lambda i: (i, 0))
    shared = pl.BlockSpec((tt, HP), lambda i: (i, 0))
    kpe_spec = pl.BlockSpec((tt, HP), lambda i: (i, kpe_blk))
    in_specs = [heads] + ([kpe_spec] if has_kpe else []) + [pl.BlockSpec((1, HP), lambda i: (0, 0)), shared, shared]
    args = [raw] + ([kpe] if has_kpe else []) + [gain, C, S]
    return pl.pallas_call(
        body, name=name, out_shape=jax.ShapeDtypeStruct(raw.shape, BF), grid=(T // tt,),
        in_specs=in_specs, out_specs=heads,
    )(*args)


def _qk_prep_bwd(dout, raw, kpe, gain, C, S, *, name, kpe_blk=0, in_scale=1.0):
    T = raw.shape[0]
    tt = _pick(T, 256)
    has_kpe = kpe is not None

    def body(*refs):
        if has_kpe:
            d_ref, raw_ref, kpe_ref, g_ref, c_ref, s_ref, dx_ref, dg_ref, dkpe_ref = refs
        else:
            d_ref, raw_ref, g_ref, c_ref, s_ref, dx_ref, dg_ref = refs
        dg = jnp.zeros((1, HP), F32)
        dkpe = jnp.zeros((tt, HP), F32)
        for h in range(N_HEADS):
            hs = slice(HP * h, HP * (h + 1))
            xr = raw_ref[:, hs] + kpe_ref[...] if has_kpe else raw_ref[:, hs]
            d = d_ref[:, hs].astype(F32) * in_scale
            r = lax.rsqrt(jnp.sum(xr * xr, axis=-1, keepdims=True) * (1.0 / QK_HEAD) + EPS)
            dn = d * c_ref[...] + _swap_rope_halves(d * s_ref[...])
            gd = dn * g_ref[...]
            dx = r * gd - xr * (r * r * r) * (jnp.sum(gd * xr, axis=-1, keepdims=True) * (1.0 / QK_HEAD))
            dx_ref[:, hs] = dx.astype(dx_ref.dtype)
            dg = dg + jnp.sum(dn * xr * r, axis=0, keepdims=True)
            dkpe = dkpe + dx

        @pl.when(pl.program_id(0) == 0)
        def _():
            dg_ref[...] = jnp.zeros_like(dg_ref)

        dg_ref[...] += jnp.broadcast_to(dg, dg_ref.shape)
        if has_kpe:
            dkpe_ref[...] = dkpe

    heads = pl.BlockSpec((tt, N_HEADS * HP), lambda i: (i, 0))
    shared = pl.BlockSpec((tt, HP), lambda i: (i, 0))
    kpe_spec = pl.BlockSpec((tt, HP), lambda i: (i, kpe_blk))
    in_specs = [heads, heads] + ([kpe_spec] if has_kpe else []) + [pl.BlockSpec((1, HP), lambda i: (0, 0)), shared, shared]
    args = [dout, raw] + ([kpe] if has_kpe else []) + [gain, C, S]
    out_shape = [jax.ShapeDtypeStruct(raw.shape, BF), jax.ShapeDtypeStruct((8, HP), F32)]
    out_specs = [heads, pl.BlockSpec((8, HP), lambda i: (0, 0))]
    if has_kpe:
        out_shape.append(jax.ShapeDtypeStruct((T, HP), F32))
        out_specs.append(shared)
    return pl.pallas_call(
        body, name=name, out_shape=tuple(out_shape), grid=(T // tt,),
        in_specs=in_specs, out_specs=tuple(out_specs),
        compiler_params=pltpu.CompilerParams(dimension_semantics=("arbitrary",)),
    )(*args)


ATTN_SCALE = 1.0 / math.sqrt(QK_HEAD)
LOG2E = 1.0 / math.log(2.0)
Q_SCALE = ATTN_SCALE * LOG2E


def _attn_fwd(q, k, v):
    T = q.shape[0]
    tq = _pick(T, 1024)
    tk = _pick(T, 1024)

    def body(q_ref, k_ref, v_ref, o_ref, ob_ref, lse_ref):
        qt = q_ref[...]
        m = o = None
        for j in range(T // tk):
            ks = slice(j * tk, (j + 1) * tk)
            s = lax.dot_general(qt, k_ref[ks, :], NT, preferred_element_type=F32)
            m_j = jnp.max(s, axis=-1, keepdims=True)
            m_new = m_j if m is None else jnp.maximum(m, m_j)
            o_j = jnp.dot(jnp.exp2(s - m_new).astype(BF), v_ref[ks, :], preferred_element_type=F32)
            o = o_j if o is None else o * jnp.exp2(m - m_new) + o_j
            m = m_new
        l = o[:, V_HEAD:V_HEAD + 1]
        o = o / l
        o_ref[...] = o
        ob_ref[...] = o.astype(ob_ref.dtype)
        lse_ref[...] = jnp.broadcast_to(m + jnp.log2(l), lse_ref.shape)

    qs = pl.BlockSpec((tq, HP), lambda h, i: (i, h))
    kv = pl.BlockSpec((T, HP), lambda h, i: (0, h))
    return pl.pallas_call(
        body, name="attn_fwd",
        out_shape=(jax.ShapeDtypeStruct(q.shape, F32), jax.ShapeDtypeStruct(q.shape, BF), jax.ShapeDtypeStruct(q.shape, F32)),
        grid=(N_HEADS, T // tq), in_specs=[qs, kv, kv], out_specs=(qs, qs, qs),
        compiler_params=pltpu.CompilerParams(dimension_semantics=("parallel", "parallel")),
    )(q, k, v)


def _attn_bwd(q, k, v, do, o, lse):
    T = q.shape[0]
    tb = _pick(T, 1024)
    nb = T // tb
    tkey = _pick(T, 1024)

    def body(q_ref, k_ref, v_ref, do_ref, o_ref, lse_ref, dq_ref, dk_ref, dv_ref, delta_rows, lse_rows, dob_scr, dv_acc):
        dq_ref[...] = jnp.zeros_like(dq_ref)
        dk_ref[...] = jnp.zeros_like(dk_ref)
        lane = lax.broadcasted_iota(jnp.int32, (8, HP), 1)
        ones8 = jnp.ones((8, HP), BF)
        first8 = jnp.where(lane == 0, 1.0, 0.0).astype(BF)

        def as_rows(pick, v):
            total, rest = None, v
            for _ in range(3):
                piece = rest.astype(BF)
                part = lax.dot_general(pick, piece, NT, preferred_element_type=F32)
                total = part if total is None else total + part
                rest = rest - piece.astype(F32)
            return total

        def per_q_tile(i, carry):
            qs = pl.ds(pl.multiple_of(i * tb, tb), tb)
            doi = do_ref[qs, :]
            delta_rows[i] = as_rows(ones8, doi * o_ref[qs, :])
            lse_rows[i] = as_rows(first8, lse_ref[qs, :])
            dob_scr[qs, :] = doi.astype(BF)
            return carry

        lax.fori_loop(0, nb, per_q_tile, 0)

        def k_loop(j, carry):
            ks = pl.ds(pl.multiple_of(j * tkey, tkey), tkey)
            kj, vj = k_ref[ks, :], v_ref[ks, :]

            dv_acc[...] = jnp.zeros_like(dv_acc)

            def q_loop(i, carry_q):
                qs = pl.ds(pl.multiple_of(i * tb, tb), tb)
                qi = q_ref[qs, :]
                dob = dob_scr[qs, :]
                s_t = lax.dot_general(kj, qi, NT, preferred_element_type=F32)
                p_t = jnp.exp2(s_t - lse_rows[i, 0:1, :])
                dp_t = lax.dot_general(vj, dob, NT, preferred_element_type=F32)
                ds_t = (p_t * (dp_t - delta_rows[i, 0:1, :])).astype(BF)
                dv_acc[...] += jnp.dot(p_t.astype(BF), dob, preferred_element_type=F32)
                dk_ref[ks, :] += jnp.dot(ds_t, qi, preferred_element_type=F32)
                dq_ref[qs, :] += lax.dot_general(ds_t, kj, TN, preferred_element_type=F32)
                return carry_q

            lax.fori_loop(0, nb, q_loop, 0)
            dv_ref[ks, :] = dv_acc[...].astype(dv_ref.dtype)
            return carry

        lax.fori_loop(0, T // tkey, k_loop, 0)

    spec = pl.BlockSpec((T, HP), lambda h: (0, h))
    return pl.pallas_call(
        body, name="attn_bwd",
        out_shape=(jax.ShapeDtypeStruct(q.shape, F32), jax.ShapeDtypeStruct(q.shape, F32), jax.ShapeDtypeStruct(q.shape, BF)),
        grid=(N_HEADS,), in_specs=[spec] * 6, out_specs=(spec,) * 3,
        scratch_shapes=[pltpu.VMEM((nb, 8, tb), F32), pltpu.VMEM((nb, 8, tb), F32), pltpu.VMEM((T, HP), BF),
                        pltpu.VMEM((tkey, HP), F32)],
        compiler_params=pltpu.CompilerParams(dimension_semantics=("parallel",), vmem_limit_bytes=2 * 15 * T * HP * 2 + (8 << 20)),
    )(q, k, v, do, o, lse)


CONV_TC = 512
CONV_PAD = CONV_WIDTH // 2


def _halo_specs(tr, col_of):
    r8 = tr // 8
    cur = pl.BlockSpec((tr, CONV_TC), lambda j, i: (i, col_of(j)))
    prev = pl.BlockSpec((8, CONV_TC), lambda j, i: (jnp.maximum(i * r8 - 1, 0), col_of(j)))

    def nxt_map(j, i, n8):
        return (jnp.minimum((i + 1) * r8, n8 - 1), col_of(j))

    return cur, prev, nxt_map


def _with_halo(prev_ref, cur_ref, next_ref, i, n_i):
    prev = jnp.where(i == 0, 0.0, prev_ref[...].astype(F32))
    nxt = jnp.where(i == n_i - 1, 0.0, next_ref[...].astype(F32))
    return jnp.concatenate([prev, cur_ref[...].astype(F32), nxt], axis=0)


def _conv_fwd(u, w8, b):
    T = u.shape[0]
    tr = _pick(T, 512)
    n_i = T // tr
    c0 = U_XBC // CONV_TC
    cur, prev, nxt_map = _halo_specs(tr, lambda j: c0 + j)
    nxt = pl.BlockSpec((8, CONV_TC), functools.partial(nxt_map, n8=T // 8))

    def body(p_ref, c_ref, n_ref, w_ref, b_ref, pre_ref, act_ref):
        i = pl.program_id(1)
        full = _with_halo(p_ref, c_ref, n_ref, i, n_i)
        acc = jnp.broadcast_to(b_ref[...], (tr, CONV_TC))
        for kk in range(CONV_WIDTH):
            acc = acc + full[8 - CONV_PAD + kk:8 - CONV_PAD + kk + tr, :] * w_ref[kk:kk + 1, :]
        pre_ref[...] = acc
        act_ref[...] = _silu(acc)

    out = pl.BlockSpec((tr, CONV_TC), lambda j, i: (i, j))
    return pl.pallas_call(
        body, name="conv_fwd", out_shape=(jax.ShapeDtypeStruct((T, XBC_DIM), F32),) * 2,
        grid=(XBC_DIM // CONV_TC, n_i),
        in_specs=[prev, cur, nxt, pl.BlockSpec((8, CONV_TC), lambda j, i: (0, j)), pl.BlockSpec((1, CONV_TC), lambda j, i: (0, j))],
        out_specs=(out, out),
    )(u, u, u, w8, b)


def _conv_bwd(dacts, pre, u, w8, col0, *, name):
    T, width = dacts[0].shape
    tr = _pick(T, 512)
    n_i = T // tr
    n_d = len(dacts)
    cd = col0 // CONV_TC
    cx = (U_XBC + col0) // CONV_TC

    def halo(col_of):
        cur, prev, nxt_map = _halo_specs(tr, col_of)
        return [prev, cur, pl.BlockSpec((8, CONV_TC), functools.partial(nxt_map, n8=T // 8))]

    def body(*refs):
        d_refs, pre_refs, x_refs = refs[:3 * n_d], refs[3 * n_d:3 * n_d + 3], refs[3 * n_d + 3:3 * n_d + 6]
        w_ref, dx_ref, dw_ref = refs[3 * n_d + 6:]
        i = pl.program_id(1)
        dfull = _with_halo(*d_refs[0:3], i, n_i)
        for p in range(1, n_d):
            dfull = dfull + _with_halo(*d_refs[3 * p:3 * p + 3], i, n_i)
        dfull = dfull * _dsilu(_with_halo(*pre_refs, i, n_i))
        xfull = _with_halo(*x_refs, i, n_i)
        dcur = dfull[8:8 + tr, :]
        dx = jnp.zeros((tr, CONV_TC), F32)
        rows = []
        for kk in range(CONV_WIDTH):
            dx = dx + dfull[8 + CONV_PAD - kk:8 + CONV_PAD - kk + tr, :] * w_ref[kk:kk + 1, :]
            rows.append(jnp.sum(dcur * xfull[8 - CONV_PAD + kk:8 - CONV_PAD + kk + tr, :], axis=0, keepdims=True))
        rows.append(jnp.sum(dcur, axis=0, keepdims=True))
        rows.append(jnp.zeros((2, CONV_TC), F32))
        dx_ref[...] = dx.astype(dx_ref.dtype)

        @pl.when(i == 0)
        def _():
            dw_ref[...] = jnp.zeros_like(dw_ref)

        dw_ref[...] += jnp.concatenate(rows, axis=0)

    out = pl.BlockSpec((tr, CONV_TC), lambda j, i: (i, j))
    return pl.pallas_call(
        body, name=name, out_shape=(jax.ShapeDtypeStruct((T, width), BF), jax.ShapeDtypeStruct((8, width), F32)),
        grid=(width // CONV_TC, n_i),
        in_specs=halo(lambda j: j) * n_d + halo(lambda j: cd + j) + halo(lambda j: cx + j)
        + [pl.BlockSpec((8, CONV_TC), lambda j, i: (0, cd + j))],
        out_specs=(out, pl.BlockSpec((8, CONV_TC), lambda j, i: (0, j))),
        compiler_params=pltpu.CompilerParams(dimension_semantics=("parallel", "arbitrary")),
    )(*[d for d in dacts for _ in range(3)], pre, pre, pre, u, u, u, w8)


N_HB = 2 * SSM_GROUPS
P_DT, P_CS, P_E, P_W = 0, HP, 2 * HP, 3 * HP
DT_BLK = (U_SMALL + S_DT) // HP


def _tri(rev, transpose=False):
    rows = lax.broadcasted_iota(jnp.int32, (CHUNK, CHUNK), 0)
    cols = lax.broadcasted_iota(jnp.int32, (CHUNK, CHUNK), 1)
    if transpose:
        rows, cols = cols, rows
    return (cols >= rows) if rev else (cols <= rows)


def _ssd_prep(u, bias8, alog8):
    T = u.shape[0]
    nc = T // CHUNK

    def body(dt_ref, bias_ref, a_ref, cols_ref, rows_ref):
        lane = lax.broadcasted_iota(jnp.int32, (CHUNK, HP), 1)
        dt = _softplus(dt_ref[...] + bias_ref[0:1, :])
        da = dt * (-jnp.exp(a_ref[0:1, :]))
        cs_f = jnp.dot(jnp.where(_tri(False), 1.0, 0.0).astype(F32), da, precision=HI, preferred_element_type=F32)
        cs_b = jnp.dot(jnp.where(_tri(True), 1.0, 0.0).astype(F32), da, precision=HI, preferred_element_type=F32)
        cs = jnp.where(lane < SSM_HEADS, cs_f, cs_b)
        tot = jnp.where(lane[0:1] < SSM_HEADS, cs_f[CHUNK - 1:CHUNK, :], cs_b[0:1, :])
        e, w = jnp.exp(cs), jnp.exp(tot - cs)
        tot8 = jnp.broadcast_to(tot, (8, HP))
        etot8 = jnp.exp(tot8)
        for b in range(N_HB):
            down = (HP - HG * b) % HP

            def rolled(v):
                return pltpu.roll(v, down, 1) if down else v

            cols_ref[b, :, P_DT:P_DT + HP] = rolled(dt)
            cs_r = rolled(cs)
            cols_ref[b, :, P_CS:P_CS + HP] = cs_r
            cols_ref[b, :, P_E:P_E + HP] = rolled(e)
            cols_ref[b, :, P_W:P_W + HP] = rolled(w)
            rows_ref[b, 0, 0:8, :] = cs_r.T[0:8, :]
            r8 = lax.broadcasted_iota(jnp.int32, (8, HP), 0)
            rows_ref[b, 0, 8:16, :] = jnp.where(r8 == 0, rolled(tot8), jnp.where(r8 == 1, rolled(etot8), 0.0))

    vec = pl.BlockSpec((8, HP), lambda c: (0, 0))
    return pl.pallas_call(
        body, name="ssd_prep",
        out_shape=(jax.ShapeDtypeStruct((N_HB, T, 4 * HP), F32), jax.ShapeDtypeStruct((N_HB, nc, 16, HP), F32)),
        grid=(nc,), in_specs=[pl.BlockSpec((CHUNK, HP), lambda c: (c, DT_BLK)), vec, vec],
        out_specs=(pl.BlockSpec((N_HB, CHUNK, 4 * HP), lambda c: (0, c, 0)), pl.BlockSpec((N_HB, 1, 16, HP), lambda c: (0, c, 0, 0))),
    )(u, bias8, alog8)


def _ssd_specs(T, rev, bwd):
    nc = T // CHUNK
    fwd_order = (lambda c: nc - 1 - c) if rev else (lambda c: c)
    cm = (lambda c: fwd_order(nc - 1 - c)) if bwd else fwd_order
    hb0 = SSM_GROUPS if rev else 0
    xs = pl.BlockSpec((CHUNK, GW), lambda c, g: (cm(c), g))
    bs = pl.BlockSpec((CHUNK, D_STATE), lambda c, g: (cm(c), D_INNER // D_STATE + g))
    cs = pl.BlockSpec((CHUNK, D_STATE), lambda c, g: (cm(c), (D_INNER + SSM_GROUPS * D_STATE) // D_STATE + g))
    cols = pl.BlockSpec((1, CHUNK, 4 * HP), lambda c, g: (hb0 + g, cm(c), 0))
    rows = pl.BlockSpec((1, 1, 16, HP), lambda c, g: (hb0 + g, cm(c), 0, 0))
    return nc, cm, xs, bs, cs, cols, rows


def _head_lanes(to_heads):
    shape = (GW, HP) if to_heads else (HP, GW)
    wide = lax.broadcasted_iota(jnp.int32, shape, 0 if to_heads else 1)
    head = lax.broadcasted_iota(jnp.int32, shape, 1 if to_heads else 0)
    return jnp.where((wide >= PH * head) & (wide < PH * (head + 1)), 1.0, 0.0).astype(BF)


def _split_dot(v, m, terms):
    total, rest = None, v
    for _ in range(terms):
        piece = rest.astype(BF)
        part = jnp.dot(piece, m, preferred_element_type=F32)
        total = part if total is None else total + part
        rest = rest - piece.astype(F32)
    return total


def _spread_cols(cols_ref, rows_ref):
    spread = _head_lanes(False)
    dt_e = _split_dot(cols_ref[0, :, P_DT:P_DT + HP], spread, 3)
    e_e = _split_dot(cols_ref[0, :, P_E:P_E + HP], spread, 2)
    w_e = _split_dot(cols_ref[0, :, P_W:P_W + HP], spread, 2)
    etot_e = _split_dot(rows_ref[0, 0, 8:16, :], spread, 3)[1:2, :]
    return dt_e, e_e, w_e, etot_e


def _decay(cols_ref, rows_ref, hh, incl, transpose=False):
    col = cols_ref[0, :, P_CS + hh:P_CS + hh + 1]
    row = rows_ref[0, 0, hh:hh + 1, :]
    return jnp.where(incl, jnp.exp(row - col if transpose else col - row), 0.0)


def _ssd_fwd(act, cols, rows, *, rev, name, add=None):
    T = act.shape[0]
    nc, cm, xs_s, b_s, c_s, cols_s, rows_s = _ssd_specs(T, rev, False)
    has_add = add is not None

    def body(*refs):
        x_ref, b_ref, c_ref, cols_ref, rows_ref = refs[:5]
        y_ref, st_ref, state = refs[5 + has_add:]
        c, g = pl.program_id(0), pl.program_id(1)

        @pl.when(c == 0)
        def _():
            state[g] = jnp.zeros((D_STATE, GW), F32)

        incl = _tri(rev)
        bm, cmat = b_ref[...].astype(BF), c_ref[...].astype(BF)
        bm_t = b_ref[...].T.astype(BF)
        cb = lax.dot_general(cmat, bm, NT, preferred_element_type=F32)
        dt_e, e_e, w_e, etot_e = _spread_cols(cols_ref, rows_ref)
        prev_all = state[g]
        st_ref[...] = prev_all
        xdt = x_ref[...] * dt_e
        xdt_b = xdt.astype(BF)
        yo_all = jnp.dot(cmat, prev_all.astype(BF), preferred_element_type=F32) * e_e
        state[g] = prev_all * etot_e + jnp.dot(bm_t, (xdt * w_e).astype(BF), preferred_element_type=F32)
        for hh in range(HG):
            hs = slice(PH * hh, PH * (hh + 1))
            lmat = _decay(cols_ref, rows_ref, hh, incl)
            yd = jnp.dot((cb * lmat).astype(BF), xdt_b[:, hs], preferred_element_type=F32)
            y_ref[:, hs] = yd + yo_all[:, hs] + refs[5][:, hs] if has_add else yd + yo_all[:, hs]

    return pl.pallas_call(
        body, name=name,
        out_shape=(jax.ShapeDtypeStruct((T, D_INNER), F32), jax.ShapeDtypeStruct((nc * D_STATE, D_INNER), F32)),
        grid=(nc, SSM_GROUPS), in_specs=[xs_s, b_s, c_s, cols_s, rows_s] + [xs_s] * has_add, out_specs=(xs_s, xs_s),
        scratch_shapes=[pltpu.VMEM((SSM_GROUPS, D_STATE, GW), F32)],
        compiler_params=pltpu.CompilerParams(dimension_semantics=("arbitrary", "arbitrary")),
    )(act, act, act, cols, rows, *([add] if has_add else []))


def _ssd_bwd(act, cols, rows, states, dy, *, rev, name, skip=None, add=None):
    T = act.shape[0]
    nc, cm, xs_s, b_s, c_s, cols_s, rows_s = _ssd_specs(T, rev, True)
    has_skip, has_add = skip is not None, add is not None
    n_in = 7 + has_skip + 3 * has_add

    def body(*refs):
        x_ref, b_ref, c_ref, cols_ref, rows_ref, st_ref, dy_ref = refs[:7]
        extra = list(refs[7:n_in])
        dx_ref, db_ref, dc_ref, dsel_ref, dtot_ref, dstate, dcs_cols, dcs_rows, dcb, dm_scr, dxdt_scr = refs[n_in:]
        c, g = pl.program_id(0), pl.program_id(1)

        @pl.when(c == 0)
        def _():
            dstate[g] = jnp.zeros((D_STATE, GW), F32)

        incl, incl_t = _tri(rev), _tri(rev, transpose=True)
        bm, cmat = b_ref[...].astype(BF), c_ref[...].astype(BF)
        cm_t = c_ref[...].T.astype(BF)
        cb = lax.dot_general(cmat, bm, NT, preferred_element_type=F32)
        cb_t = lax.dot_general(bm, cmat, NT, preferred_element_type=F32)
        prev_all, ds_all = st_ref[...], dstate[g]
        pb_all, dsb_all = prev_all.astype(BF), ds_all.astype(BF)
        cp_all = jnp.dot(cmat, pb_all, preferred_element_type=F32)
        bds_all = jnp.dot(bm, dsb_all, preferred_element_type=F32)
        dt_e, e_e, w_e, etot_e = _spread_cols(cols_ref, rows_ref)
        to_heads = _head_lanes(True)
        x, dy = x_ref[...], dy_ref[...]
        xdt = x * dt_e
        xdt_b, dy_b = xdt.astype(BF), dy.astype(BF)
        dye_b, xdw_b = (dy * e_e).astype(BF), (xdt * w_e).astype(BF)
        for hh in range(HG):
            hs = slice(PH * hh, PH * (hh + 1))
            mmat_t = cb_t * _decay(cols_ref, rows_ref, hh, incl_t, transpose=True)
            dm_scr[hh] = lax.dot_general(dy_b[:, hs], xdt_b[:, hs], NT, preferred_element_type=F32)
            dxdt_scr[:, hs] = jnp.dot(mmat_t.astype(BF), dy_b[:, hs], preferred_element_type=F32)
        bdsw = bds_all * w_e
        dxdt = dxdt_scr[...] + bdsw
        dx = dxdt * dt_e
        if has_skip:
            dx = dx + dy * extra.pop(0)[...]
        if has_add:
            dx = dx + extra[0][...]
        dx_ref[...] = dx
        t = _split_dot(xdt * bdsw, to_heads, 2)
        dcs_state = _split_dot(dy * cp_all, to_heads, 2) * cols_ref[0, :, P_E:P_E + HP] - t
        dsel_ref[0, :, 0:HP] = _split_dot(dxdt * x, to_heads, 2)
        sp = _split_dot(jnp.broadcast_to(jnp.sum(ds_all * prev_all, axis=0, keepdims=True), (8, GW)), to_heads, 2)
        dtot_ref[0, 0] = jnp.sum(t, axis=0, keepdims=True) + sp * rows_ref[0, 0, 9:10, :]
        dstate[g] = ds_all * etot_e + jnp.dot(cm_t, dye_b, preferred_element_type=F32)
        dcs_cols[...] = jnp.zeros_like(dcs_cols)
        dcs_rows[...] = jnp.zeros_like(dcs_rows)
        dcb[...] = jnp.zeros_like(dcb)
        for hh in range(HG):
            lmat = _decay(cols_ref, rows_ref, hh, incl)
            dm = dm_scr[hh]
            qm = dm * (cb * lmat)
            dcs_cols[:, hh:hh + 1] = jnp.sum(qm, axis=1, keepdims=True)
            dcs_rows[hh:hh + 1, :] = jnp.sum(qm, axis=0, keepdims=True)
            dcb[...] += dm * lmat
        dcb_all = dcb[...]
        dsel_ref[0, :, HP:2 * HP] = dcs_state + dcs_cols[...] - dcs_rows[...].T
        dc = (lax.dot_general(dye_b, pb_all, NT, preferred_element_type=F32)
              + jnp.dot(dcb_all.astype(BF), bm, preferred_element_type=F32))
        db = (lax.dot_general(xdw_b, dsb_all, NT, preferred_element_type=F32)
              + jnp.dot(dcb_all.T.astype(BF), cmat, preferred_element_type=F32))
        db_ref[...] = db + extra[1][...] if has_add else db
        dc_ref[...] = dc + extra[2][...] if has_add else dc

    bc_out = pl.BlockSpec((CHUNK, D_STATE), lambda c, g: (cm(c), g))
    more_specs = [pl.BlockSpec((1, GW), lambda c, g: (0, g))] * has_skip + [xs_s, bc_out, bc_out] * has_add
    more_args = ([skip] if has_skip else []) + (list(add) if has_add else [])
    return pl.pallas_call(
        body, name=name,
        out_shape=(jax.ShapeDtypeStruct((T, D_INNER), F32), jax.ShapeDtypeStruct((T, SSM_GROUPS * D_STATE), F32),
                   jax.ShapeDtypeStruct((T, SSM_GROUPS * D_STATE), F32), jax.ShapeDtypeStruct((SSM_GROUPS, T, 2 * HP), F32),
                   jax.ShapeDtypeStruct((SSM_GROUPS, nc, 8, HP), F32)),
        grid=(nc, SSM_GROUPS), in_specs=[xs_s, b_s, c_s, cols_s, rows_s, xs_s, xs_s] + more_specs,
        out_specs=(xs_s, bc_out, bc_out, pl.BlockSpec((1, CHUNK, 2 * HP), lambda c, g: (g, cm(c), 0)),
                   pl.BlockSpec((1, 1, 8, HP), lambda c, g: (g, cm(c), 0, 0))),
        scratch_shapes=[pltpu.VMEM((SSM_GROUPS, D_STATE, GW), F32), pltpu.VMEM((CHUNK, CHUNK), F32),
                        pltpu.VMEM((CHUNK, CHUNK), F32), pltpu.VMEM((CHUNK, CHUNK), F32),
                        pltpu.VMEM((HG, CHUNK, CHUNK), F32), pltpu.VMEM((CHUNK, GW), F32)],
        compiler_params=pltpu.CompilerParams(dimension_semantics=("arbitrary", "arbitrary")),
    )(act, act, act, cols, rows, states, dy, *more_args)


def _ssd_prep_bwd(u, bias8, alog8, dsel_f, dtot_f, dsel_b, dtot_b):
    T = u.shape[0]
    nc = T // CHUNK

    def body(dt_ref, bias_ref, a_ref, sf_ref, tf_ref, sb_ref, tb_ref, ddt_ref, da_ref, dbias_ref):
        @pl.when(pl.program_id(0) == 0)
        def _():
            da_ref[...] = jnp.zeros_like(da_ref)
            dbias_ref[...] = jnp.zeros_like(dbias_ref)

        lane = lax.broadcasted_iota(jnp.int32, (CHUNK, HP), 1)
        pre = dt_ref[...] + bias_ref[0:1, :]
        dt = _softplus(pre)
        a = -jnp.exp(a_ref[0:1, :])
        ddt_x, dcs, dtot = jnp.zeros((CHUNK, HP), F32), jnp.zeros((CHUNK, HP), F32), jnp.zeros((8, HP), F32)
        for b in range(N_HB):
            s_ref, t_ref, g = (sf_ref, tf_ref, b) if b < SSM_GROUPS else (sb_ref, tb_ref, b - SSM_GROUPS)
            mine = (lane >= HG * b) & (lane < HG * (b + 1))

            def up(v):
                return pltpu.roll(v, HG * b, 1) if b else v

            ddt_x = ddt_x + jnp.where(mine, up(s_ref[g, :, 0:HP]), 0.0)
            dcs = dcs + jnp.where(mine, up(s_ref[g, :, HP:2 * HP]), 0.0)
            dtot = dtot + jnp.where(mine[0:8], up(t_ref[g, 0]), 0.0)
        tri_f = jnp.where(_tri(False, transpose=True), 1.0, 0.0).astype(F32)
        tri_b = jnp.where(_tri(True, transpose=True), 1.0, 0.0).astype(F32)
        dda = jnp.where(lane < SSM_HEADS, jnp.dot(tri_f, dcs, precision=HI, preferred_element_type=F32),
                        jnp.dot(tri_b, dcs, precision=HI, preferred_element_type=F32)) + dtot[0:1, :]
        dpre = (ddt_x + dda * a) * jax.nn.sigmoid(pre)
        ddt_ref[...] = jnp.where(lane < 2 * SSM_HEADS, dpre, 0.0)
        dbias_ref[...] += jnp.broadcast_to(jnp.sum(dpre, axis=0, keepdims=True), (8, HP))
        da_ref[...] += jnp.broadcast_to(jnp.sum(dda * dt, axis=0, keepdims=True) * a, (8, HP))

    vec = pl.BlockSpec((8, HP), lambda c: (0, 0))
    sel = pl.BlockSpec((SSM_GROUPS, CHUNK, 2 * HP), lambda c: (0, c, 0))
    tot = pl.BlockSpec((SSM_GROUPS, 1, 8, HP), lambda c: (0, c, 0, 0))
    tile = pl.BlockSpec((CHUNK, HP), lambda c: (c, 0))
    return pl.pallas_call(
        body, name="ssd_prep_bwd",
        out_shape=(jax.ShapeDtypeStruct((T, HP), F32), jax.ShapeDtypeStruct((8, HP), F32), jax.ShapeDtypeStruct((8, HP), F32)),
        grid=(nc,), in_specs=[pl.BlockSpec((CHUNK, HP), lambda c: (c, DT_BLK)), vec, vec, sel, tot, sel, tot],
        out_specs=(tile, vec, vec),
        compiler_params=pltpu.CompilerParams(dimension_semantics=("arbitrary",)),
    )(u, bias8, alog8, dsel_f, dtot_f, dsel_b, dtot_b)


def _ssm_combine_fwd(y_scans, act, u, dskip, gain):
    T = y_scans.shape[0]
    tt = _pick(T, 512)

    def body(ys_ref, x_ref, z_ref, ds_ref, g_ref, y_ref, m_ref):
        y = ys_ref[...] + ds_ref[...] * x_ref[...]
        y2 = y * _silu(z_ref[...])
        r = lax.rsqrt(jnp.mean(y2 * y2, axis=-1, keepdims=True) + EPS)
        y_ref[...] = y
        m_ref[...] = (y2 * r * g_ref[...]).astype(m_ref.dtype)

    blk = pl.BlockSpec((tt, GW), lambda i, g: (i, g))
    vec = pl.BlockSpec((1, GW), lambda i, g: (0, g))
    return pl.pallas_call(
        body, name="ssm_combine_fwd",
        out_shape=(jax.ShapeDtypeStruct((T, D_INNER), F32), jax.ShapeDtypeStruct((T, D_INNER), BF)),
        grid=(T // tt, SSM_GROUPS), in_specs=[blk, blk, blk, vec, vec], out_specs=(blk, blk),
    )(y_scans, act, u, dskip, gain)


def _ssm_combine_bwd(dm, y, act, u, gain):
    T = y.shape[0]
    tt = _pick(T, 512)

    def body(dm_ref, y_ref, x_ref, z_ref, g_ref, dy_ref, dz_ref, dg_ref, dsk_ref):
        z = z_ref[...]
        y = y_ref[...]
        x = x_ref[...]
        sz = _silu(z)
        y2 = y * sz
        r = lax.rsqrt(jnp.mean(y2 * y2, axis=-1, keepdims=True) + EPS)
        d = dm_ref[...]
        gd = d * g_ref[...]
        dy2 = r * gd - y2 * (r * r * r) * jnp.mean(gd * y2, axis=-1, keepdims=True)
        dy = dy2 * sz
        dy_ref[...] = dy
        dz_ref[...] = (dy2 * y * _dsilu(z)).astype(dz_ref.dtype)

        @pl.when(pl.program_id(1) == 0)
        def _():
            dg_ref[...] = jnp.zeros_like(dg_ref)
            dsk_ref[...] = jnp.zeros_like(dsk_ref)

        dg_ref[...] += jnp.broadcast_to(jnp.sum(d * y2 * r, axis=0, keepdims=True), dg_ref.shape)
        lane_sum = jnp.broadcast_to(jnp.sum(dy * x, axis=0, keepdims=True), (8, GW))
        src = lax.broadcasted_iota(jnp.int32, (GW, HP), 0)
        head = lax.broadcasted_iota(jnp.int32, (GW, HP), 1)
        to_head = jnp.where((src >= PH * head) & (src < PH * (head + 1)), 1.0, 0.0).astype(F32)
        dsk_ref[...] += jnp.dot(lane_sum, to_head, precision=HI, preferred_element_type=F32)

    blk = pl.BlockSpec((tt, GW), lambda g, i: (i, g))
    vec = pl.BlockSpec((1, GW), lambda g, i: (0, g))
    acc = pl.BlockSpec((8, GW), lambda g, i: (0, g))
    return pl.pallas_call(
        body, name="ssm_combine_bwd",
        out_shape=(jax.ShapeDtypeStruct((T, D_INNER), F32), jax.ShapeDtypeStruct((T, D_INNER), BF),
                   jax.ShapeDtypeStruct((8, D_INNER), F32), jax.ShapeDtypeStruct((8, SSM_GROUPS * HP), F32)),
        grid=(SSM_GROUPS, T // tt), in_specs=[blk, blk, blk, blk, vec],
        out_specs=(blk, blk, acc, pl.BlockSpec((8, HP), lambda g, i: (0, g))),
        compiler_params=pltpu.CompilerParams(dimension_semantics=("parallel", "arbitrary")),
    )(dm, y, act, u, gain)


def _loss_head(y, target):
    T, D = y.shape
    tt = _pick(T, 512)

    def body(y_ref, t_ref, dy_ref, dyb_ref, l_ref):
        e = y_ref[...] - t_ref[...]
        dy_ref[...] = e * (1.0 / D)
        dyb_ref[...] = (e * (1.0 / D)).astype(dyb_ref.dtype)

        @pl.when(pl.program_id(0) == 0)
        def _():
            l_ref[...] = jnp.zeros_like(l_ref)

        l_ref[...] += jnp.sum(e * e) * (0.5 / D)

    blk = pl.BlockSpec((tt, D), lambda i: (i, 0))
    return pl.pallas_call(
        body, name="loss_head",
        out_shape=(jax.ShapeDtypeStruct((T, D), F32), jax.ShapeDtypeStruct((T, D), BF), jax.ShapeDtypeStruct((8, 128), F32)),
        grid=(T // tt,), in_specs=[blk, blk], out_specs=(blk, blk, pl.BlockSpec((8, 128), lambda i: (0, 0))),
        compiler_params=pltpu.CompilerParams(dimension_semantics=("arbitrary",)),
    )(y, target)


def _adamw(w, g, m, v, *, name):
    R, C = w.shape
    cap = max(8, (1 << 18) // C)
    tr = R
    if R % 8 == 0:
        tr = 8
        for cand in range(8, min(R, cap) + 1, 8):
            if R % cand == 0:
                tr = cand

    def body(w_ref, g_ref, m_ref, v_ref, d_ref, nm_ref, nv_ref):
        gg = g_ref[...]
        nm = ADAM_B1 * m_ref[...] + (1.0 - ADAM_B1) * gg
        nv = ADAM_B2 * v_ref[...] + (1.0 - ADAM_B2) * jnp.square(gg)
        m_hat = nm / (1.0 - ADAM_B1 ** ADAM_STEP)
        v_hat = nv / (1.0 - ADAM_B2 ** ADAM_STEP)
        d_ref[...] = -ADAM_LR * (m_hat / (jnp.sqrt(v_hat) + ADAM_EPS) + ADAM_WD * w_ref[...])
        nm_ref[...] = nm
        nv_ref[...] = nv

    blk = pl.BlockSpec((tr, C), lambda i: (i, 0))
    return pl.pallas_call(
        body, name=name, out_shape=(jax.ShapeDtypeStruct((R, C), F32),) * 3, grid=(R // tr,),
        in_specs=[blk] * 4, out_specs=(blk,) * 3,
    )(w, g, m, v)


ANY = pl.BlockSpec(memory_space=pl.ANY)


def _chip_peers():
    x, y, c = lax.axis_index("x"), lax.axis_index("y"), lax.axis_index("c")
    return x, y, c, [(1 - x, y), (x, 1 - y), (1 - x, 1 - y)]


def _half_rows(c, rh):
    return pl.ds(pl.multiple_of(c * rh, 16), rh)


def _my_chip():
    return 2 * lax.axis_index("x") + lax.axis_index("y")


def _gather_chips(wb, wf):
    rh = wb.shape[0] // 2
    rq = rh // 2

    def body(wb_ref, wf_ref, ob_ref, of_ref, send_sems, recv_sems):
        x, y, c, peers = _chip_peers()
        nbr_x, nbr_y = peers[0], peers[1]
        me, chip_x, chip_y, chip_d = 2 * x + y, 2 * (1 - x) + y, 2 * x + (1 - y), 2 * (1 - x) + (1 - y)

        def quarter(core, b):
            return pl.ds(pl.multiple_of(core * rh + b * rq, 16), rq)

        ici = [(0, nbr_x, me, 0, chip_x), (1, nbr_y, me, 1, chip_y), (2, nbr_y, me, 0, chip_y), (3, nbr_x, me, 1, chip_x),
               (4, nbr_y, chip_x, 0, chip_d), (5, nbr_x, chip_y, 1, chip_d)]

        def ici_copy(k, to, slot, b, own):
            rows = quarter(c, b)
            return pltpu.make_async_remote_copy(
                src_ref=wb_ref.at[rows] if own else ob_ref.at[slot, rows], dst_ref=ob_ref.at[slot, rows],
                send_sem=send_sems.at[k], recv_sem=recv_sems.at[k], device_id=(to[0], to[1], c), device_id_type=MESH)

        def to_sibling(k, slot, b, core):
            rows = quarter(core, b)
            return pltpu.make_async_remote_copy(
                src_ref=ob_ref.at[slot, rows], dst_ref=ob_ref.at[slot, rows], send_sem=send_sems.at[6 + k],
                recv_sem=recv_sems.at[6 + k], device_id=(x, y, 1 - c), device_id_type=MESH)

        def small_copy(k, slot):
            px, py = peers[k]
            return pltpu.make_async_remote_copy(
                src_ref=wf_ref, dst_ref=of_ref.at[slot], send_sem=send_sems.at[12 + k], recv_sem=recv_sems.at[12 + k],
                device_id=(px, py, c), device_id_type=MESH)

        sends = [ici_copy(k, to, slot, b, True) for k, to, slot, b, _ in ici[:4]] + [small_copy(k, me) for k in range(3)]
        for cp in sends:
            cp.start()
        for k, to, slot, b, arrives in ici:
            ici_copy(k, to, arrives, b, False).wait_recv()
            passed = [to_sibling(k, arrives, b, c)]
            if k < 2:
                passed.append(ici_copy(*ici[4 + k][:4], False))
            for cp in passed:
                cp.start()
            sends += passed
        for k, to, slot, b, arrives in ici:
            to_sibling(k, arrives, b, 1 - c).wait_recv()
        chip_of = [chip_x, chip_y, chip_d]
        for k in range(3):
            small_copy(k, chip_of[k]).wait_recv()
        for cp in sends:
            cp.wait_send()

    ob, of = pl.pallas_call(
        body, name="gather_weights",
        out_shape=(jax.ShapeDtypeStruct((4,) + wb.shape, wb.dtype), jax.ShapeDtypeStruct((4,) + wf.shape, wf.dtype)),
        in_specs=[ANY, ANY], out_specs=(ANY, ANY),
        scratch_shapes=[pltpu.SemaphoreType.DMA((15,)), pltpu.SemaphoreType.DMA((15,))],
    )(wb, wf)
    me = _my_chip()
    return lax.dynamic_update_slice(ob, wb[None], (me, 0, 0)), lax.dynamic_update_slice(of, wf[None], (me, 0, 0))


def _halves_to_sibling(gp):
    rh = gp.shape[1] // 2

    def body(gp_ref, o_ref, send_sem, recv_sem):
        x, y, c = lax.axis_index("x"), lax.axis_index("y"), lax.axis_index("c")
        cp = pltpu.make_async_remote_copy(src_ref=gp_ref.at[:, _half_rows(1 - c, rh), :], dst_ref=o_ref, send_sem=send_sem,
                                          recv_sem=recv_sem, device_id=(x, y, 1 - c), device_id_type=MESH)
        cp.start()
        cp.wait()

    return pl.pallas_call(
        body, name="halves_to_sibling", out_shape=jax.ShapeDtypeStruct((gp.shape[0], rh, gp.shape[2]), gp.dtype),
        in_specs=[ANY], out_specs=ANY, scratch_shapes=[pltpu.SemaphoreType.DMA, pltpu.SemaphoreType.DMA],
    )(gp)


def _row_tile(rows, cap=1024):
    tr = 16
    for cand in range(16, cap + 1, 16):
        if rows % cand == 0:
            tr = cand
    return tr


def _add_halves(gp, sib, core):
    n, rh, C = sib.shape
    tr = _row_tile(rh)
    nt = rh // tr

    def body(c_ref, g_ref, s_ref, o_ref):
        o_ref[...] = (g_ref[...].astype(F32) + s_ref[...].astype(F32)).astype(o_ref.dtype)

    blk = pl.BlockSpec((1, tr, C), lambda j, i, c: (j, i, 0))
    return pl.pallas_call(
        body, name="add_halves", out_shape=jax.ShapeDtypeStruct(sib.shape, sib.dtype),
        grid_spec=pltpu.PrefetchScalarGridSpec(
            num_scalar_prefetch=1, grid=(n, nt),
            in_specs=[pl.BlockSpec((1, tr, C), lambda j, i, c: (j, c[0] * nt + i, 0)), blk], out_specs=blk),
    )(core, gp, sib)


def _join_halves(buf):
    rh = buf.shape[0] // 2

    def body(in_ref, o_ref, send_sem, recv_sem):
        x, y, c = lax.axis_index("x"), lax.axis_index("y"), lax.axis_index("c")

        def copy(rows):
            return pltpu.make_async_remote_copy(src_ref=o_ref.at[rows], dst_ref=o_ref.at[rows], send_sem=send_sem,
                                                recv_sem=recv_sem, device_id=(x, y, 1 - c), device_id_type=MESH)

        send = copy(_half_rows(c, rh))
        send.start()
        copy(_half_rows(1 - c, rh)).wait_recv()
        send.wait_send()

    return pl.pallas_call(
        body, name="join_halves", out_shape=jax.ShapeDtypeStruct(buf.shape, buf.dtype),
        in_specs=[ANY], out_specs=ANY, input_output_aliases={0: 0},
        scratch_shapes=[pltpu.SemaphoreType.DMA, pltpu.SemaphoreType.DMA],
    )(buf)


def _exchange_near(gp):
    rq = gp.shape[1] // 2

    def body(gp_ref, out_ref, send_sems, recv_sems):
        x, y, c, peers = _chip_peers()
        chip_x, chip_y, chip_d = 2 * (1 - x) + y, 2 * x + (1 - y), 2 * (1 - x) + (1 - y)
        plan = [(peers[0], chip_x, 0), (peers[0], chip_d, 0), (peers[1], chip_y, 1), (peers[1], chip_d, 1)]
        copies = [pltpu.make_async_remote_copy(
            src_ref=gp_ref.at[slot, pl.ds(b * rq, rq)], dst_ref=out_ref.at[k], send_sem=send_sems.at[k],
            recv_sem=recv_sems.at[k], device_id=(to[0], to[1], c), device_id_type=MESH) for k, (to, slot, b) in enumerate(plan)]
        for cp in copies:
            cp.start()
        for cp in copies:
            cp.wait_recv()
        for cp in copies:
            cp.wait_send()

    return pl.pallas_call(
        body, name="exchange_grads_near", out_shape=jax.ShapeDtypeStruct((4, rq, gp.shape[2]), gp.dtype),
        in_specs=[ANY], out_specs=ANY, scratch_shapes=[pltpu.SemaphoreType.DMA((4,)), pltpu.SemaphoreType.DMA((4,))],
    )(gp)


def _add_near(gp, near, chips):
    _, rq, C = near.shape
    tr = _row_tile(rq)
    nt = rq // tr

    def body(ch_ref, mine_a, mine_b, on_a, on_b, near_ref, part_ref, on_ref):
        part_ref[0] = mine_a[0].astype(F32) + near_ref[0].astype(F32)
        part_ref[1] = mine_b[0].astype(F32) + near_ref[2].astype(F32)
        on_ref[0] = (on_a[0].astype(F32) + near_ref[1].astype(F32)).astype(on_ref.dtype)
        on_ref[1] = (on_b[0].astype(F32) + near_ref[3].astype(F32)).astype(on_ref.dtype)

    def slot(which, b):
        return pl.BlockSpec((1, tr, C), lambda i, ch: (ch[which], b * nt + i, 0))

    return pl.pallas_call(
        body, name="add_near",
        out_shape=(jax.ShapeDtypeStruct((2, rq, C), F32), jax.ShapeDtypeStruct((2, rq, C), near.dtype)),
        grid_spec=pltpu.PrefetchScalarGridSpec(
            num_scalar_prefetch=1, grid=(nt,),
            in_specs=[slot(0, 0), slot(0, 1), slot(2, 0), slot(1, 1), pl.BlockSpec((4, tr, C), lambda i, ch: (0, i, 0))],
            out_specs=(pl.BlockSpec((2, tr, C), lambda i, ch: (0, i, 0)),) * 2),
    )(chips, gp, gp, gp, gp, near)


def _exchange_far(on):
    def body(on_ref, out_ref, send_sems, recv_sems):
        x, y, c, peers = _chip_peers()
        copies = [pltpu.make_async_remote_copy(
            src_ref=on_ref.at[k], dst_ref=out_ref.at[k], send_sem=send_sems.at[k], recv_sem=recv_sems.at[k],
            device_id=(to[0], to[1], c), device_id_type=MESH) for k, to in enumerate((peers[1], peers[0]))]
        for cp in copies:
            cp.start()
        for cp in copies:
            cp.wait_recv()
        for cp in copies:
            cp.wait_send()

    return pl.pallas_call(
        body, name="exchange_grads_far", out_shape=jax.ShapeDtypeStruct(on.shape, on.dtype),
        in_specs=[ANY], out_specs=ANY, scratch_shapes=[pltpu.SemaphoreType.DMA((2,)), pltpu.SemaphoreType.DMA((2,))],
    )(on)


def _add_far(part, far, core):
    _, rq, C = part.shape
    tr = _row_tile(rq)
    nt = rq // tr

    def body(c_ref, p_ref, f_ref, o_ref):
        o_ref[...] = p_ref[0] + f_ref[0].astype(F32)

    blk = pl.BlockSpec((1, tr, C), lambda b, i, c: (b, i, 0))
    return pl.pallas_call(
        body, name="add_far", out_shape=jax.ShapeDtypeStruct((4 * rq, C), F32),
        grid_spec=pltpu.PrefetchScalarGridSpec(
            num_scalar_prefetch=1, grid=(2, nt), in_specs=[blk, blk],
            out_specs=pl.BlockSpec((tr, C), lambda b, i, c: ((2 * c[0] + b) * nt + i, 0))),
    )(core, part, far)


N_DEV = 8


def _allreduce_small(p):
    rs = p.shape[0]

    def body(x_ref, sum_ref, all_ref, send_sems, recv_sems, local_sem):
        x, y, c = lax.axis_index("x"), lax.axis_index("y"), lax.axis_index("c")
        me, sibling = (x, y, c), (x, y, 1 - c)
        chips = [(1 - x, y), (x, 1 - y), (1 - x, 1 - y)]

        def rows(px, py, pc):
            return all_ref.at[pl.ds((4 * px + 2 * py + pc) * rs, rs), :]

        def copy(k, block, to, src=None):
            return pltpu.make_async_remote_copy(
                src_ref=rows(*block) if src is None else src, dst_ref=rows(*block),
                send_sem=send_sems.at[k], recv_sem=recv_sems.at[k], device_id=to, device_id_type=MESH)

        mine = pltpu.make_async_copy(x_ref, rows(*me), local_sem)
        mine.start()
        first = [copy(0, me, sibling, src=x_ref)]
        first += [copy(1 + j, me, (*chip, c), src=x_ref) for j, chip in enumerate(chips)]
        for cp in first:
            cp.start()
        passed = [copy(4 + j, (*chip, c), sibling) for j, chip in enumerate(chips)]
        for j, chip in enumerate(chips):
            copy(1 + j, (*chip, c), me).wait_recv()
            passed[j].start()
        copy(0, sibling, me).wait_recv()
        for j, chip in enumerate(chips):
            copy(4 + j, (*chip, 1 - c), me).wait_recv()
        for cp in first + passed:
            cp.wait_send()
        mine.wait()
        acc = all_ref[0:rs, :]
        for d in range(1, N_DEV):
            acc = acc + all_ref[d * rs:(d + 1) * rs, :]
        sum_ref[...] = acc

    vmem = pl.BlockSpec(memory_space=pltpu.VMEM)
    return pl.pallas_call(
        body, name="allreduce_small", out_shape=jax.ShapeDtypeStruct((rs, 128), F32),
        in_specs=[vmem], out_specs=vmem,
        scratch_shapes=[pltpu.VMEM((N_DEV * rs, 128), F32), pltpu.SemaphoreType.DMA((7,)), pltpu.SemaphoreType.DMA((7,)),
                        pltpu.SemaphoreType.DMA],
    )(p)


WEIGHTS = ('ffn1_norm', 'ffn1_w_gate', 'ffn1_w_up', 'ffn1_w_down', 'mix_norm', 'w_in', 'q_a_norm', 'w_q_b',
           'kv_a_norm', 'w_kv_b', 'q_head_norm', 'k_head_norm', 'conv_w', 'conv_b', 'a_log_fwd', 'a_log_bwd',
           'dt_bias_fwd', 'dt_bias_bwd', 'd_skip', 'ssm_norm', 'w_attn_branch', 'w_ssm_branch', 'w_out',
           'ffn2_norm', 'ffn2_w_gate', 'ffn2_w_up', 'ffn2_w_down')
PACKED = (('ffn1_w_gate', (D_MODEL, D_FF), 1), ('ffn1_w_up', (D_MODEL, D_FF), 1), ('ffn1_w_down', (D_FF, D_MODEL), 0),
          ('w_in', (D_MODEL, sum(IN_SPLITS)), 1), ('w_q_b', (Q_LORA, N_HEADS * QK_HEAD), 1),
          ('w_kv_b', (KV_LORA, N_HEADS * (QK_NOPE + V_HEAD)), 1),
          ('w_attn_branch', (N_HEADS * V_HEAD, D_MODEL), 0), ('w_ssm_branch', (D_INNER, D_MODEL), 0),
          ('w_out', (D_MODEL, D_MODEL), 0),
          ('ffn2_w_gate', (D_MODEL, D_FF), 1), ('ffn2_w_up', (D_MODEL, D_FF), 1), ('ffn2_w_down', (D_FF, D_MODEL), 0))
PACK_W = 1024
N_CHIPS = 4
SMALL = (('ffn1_norm', 1024), ('mix_norm', 1024), ('q_a_norm', 384), ('kv_a_norm', 256), ('q_head_norm', 96),
         ('k_head_norm', 96), ('conv_b', 3072), ('a_log_fwd', 32), ('a_log_bwd', 32), ('dt_bias_fwd', 32),
         ('dt_bias_bwd', 32), ('d_skip', 32), ('ssm_norm', 2048), ('ffn2_norm', 1024),
         ('conv_w', CONV_WIDTH * XBC_DIM), ('loss', 1))


TRANSPOSED = ('ffn1_w_gate', 'ffn1_w_up', 'w_in', 'ffn2_w_gate', 'ffn2_w_up')


def _stored(name, a):
    return a.T if name in TRANSPOSED else a


def _shard_shape(name, shape, axis):
    sh = tuple(s // N_CHIPS if a == axis else s for a, s in enumerate(shape))
    return sh[::-1] if name in TRANSPOSED else sh


def _by_rows(name, axis):
    return name in TRANSPOSED or axis == 0


def _pack_layout():
    out, r = {}, 0
    for name, shape, axis in PACKED:
        n = math.prod(shape) // N_CHIPS // PACK_W
        out[name] = (r, n)
        r += n
    return out, -(-r // 64) * 64


def _pack(shards):
    layout, rows = _pack_layout()
    parts = [shards[name].reshape(-1, PACK_W) for name, _, _ in PACKED]
    parts.append(jnp.zeros((rows - sum(p.shape[0] for p in parts), PACK_W), parts[0].dtype))
    return jnp.concatenate(parts, axis=0)


def _unpack(packed):
    layout, _ = _pack_layout()
    return {name: packed[layout[name][0]:layout[name][0] + layout[name][1]].reshape(_shard_shape(name, shape, axis))
            for name, shape, axis in PACKED}


def _full_from_slots(slots):
    layout, _ = _pack_layout()
    out = {}
    for name, shape, axis in PACKED:
        r, n = layout[name]
        if _by_rows(name, axis):
            out[name] = slots[:, r:r + n].reshape(N_CHIPS * n, PACK_W)
        else:
            sh = _shard_shape(name, shape, axis)
            out[name] = jnp.concatenate([slots[j, r:r + n].reshape(sh) for j in range(N_CHIPS)], axis=axis)
    return out


def _slots_from_full(full):
    layout, rows = _pack_layout()
    parts = []
    for name, shape, axis in PACKED:
        r, n = layout[name]
        if _by_rows(name, axis):
            parts.append(full[name].reshape(N_CHIPS, n, PACK_W))
        else:
            size = shape[axis] // N_CHIPS
            parts.append(jnp.stack([lax.slice_in_dim(full[name], j * size, (j + 1) * size, axis=axis).reshape(n, PACK_W)
                                    for j in range(N_CHIPS)]))
    parts.append(jnp.zeros((N_CHIPS, rows - sum(p.shape[1] for p in parts), PACK_W), parts[0].dtype))
    return jnp.concatenate(parts, axis=1)


def _pack_small(vals):
    parts = []
    for name, n in SMALL:
        pad = -(-n // 128) * 128 - n
        parts.append(jnp.pad(vals[name].reshape(-1).astype(F32), (0, pad)).reshape(-1, 128))
    rows = sum(p.shape[0] for p in parts)
    parts.append(jnp.zeros((-(-rows // 8) * 8 - rows, 128), F32))
    return jnp.concatenate(parts, axis=0)


def _unpack_small(packed):
    out, r = {}, 0
    for name, n in SMALL:
        k = -(-n // 128)
        out[name] = packed[r:r + k].reshape(-1)[:n]
        r += k
    return out


def _pad_heads(w, axis, per_head, lo, hi):
    shape = w.shape
    w = w.reshape(shape[:axis] + (N_HEADS, per_head) + shape[axis + 1:])
    w = lax.slice_in_dim(w, lo, hi, axis=axis + 1)
    pad = [(0, 0)] * w.ndim
    pad[axis + 1] = (0, HP - (hi - lo))
    w = jnp.pad(w, pad)
    return w.reshape(shape[:axis] + (N_HEADS * HP,) + shape[axis + 1:])


def _unpad_heads(w, axis, keep):
    shape = w.shape
    w = w.reshape(shape[:axis] + (N_HEADS, HP) + shape[axis + 1:])
    return lax.slice_in_dim(w, 0, keep, axis=axis + 1)


def _pad_w_in(wt):
    o = [0]
    for s in IN_SPLITS:
        o.append(o[-1] + s)
    cq, ckv, kpe, z, xbc, dtf, dtb, ga, gb = [wt[o[i]:o[i + 1]] for i in range(len(IN_SPLITS))]
    kpe_pad = jnp.pad(kpe, ((QK_NOPE, HP - QK_HEAD), (0, 0)))
    dt_pad = jnp.pad(jnp.concatenate([dtf, dtb], axis=0), ((0, HP - 2 * SSM_HEADS), (0, 0)))
    return jnp.concatenate([z, ga, gb, xbc, cq, ckv, kpe_pad, dt_pad], axis=0)


def _unpad_w_in(gt):
    z, ga, gb, xbc = gt[U_Z:U_GA], gt[U_GA:U_GB], gt[U_GB:U_XBC], gt[U_XBC:U_SMALL]
    s = gt[U_SMALL:]
    cq, ckv = s[S_CQ:S_CKV], s[S_CKV:S_KPE]
    kpe = s[S_KPE + QK_NOPE:S_KPE + QK_HEAD]
    dtf, dtb = s[S_DT:S_DT + SSM_HEADS], s[S_DT + SSM_HEADS:S_DT + 2 * SSM_HEADS]
    return jnp.concatenate([cq, ckv, kpe, z, xbc, dtf, dtb, ga, gb], axis=0)


def _lanes128(parts):
    row = jnp.concatenate([p.reshape(-1) for p in parts])
    return jnp.pad(row, (0, HP - row.shape[0])).reshape(1, HP)


FF_TILE = D_FF // 2
WGRAD = BF


def _ffn_fwd(x, g, wg_t, wu_t, wd, tag):
    h = _rms_fwd(x, g, name=tag + "_norm")
    gate, up, act = _mm([h], [wg_t, wu_t], name=tag + "_up", tb=True, out_dtypes=(BF, BF, BF), tm=512, tn=FF_TILE,
                        epilogue=lambda a, b: (a, b, _silu(a) * b))
    out = _mm([act], [wd], name=tag + "_down", extras=[x], epilogue=lambda acc, r: (r + 0.5 * acc,))
    return out, (h, gate, up, act)


def _ffn_bwd(dout, dout_bf, x, g, wg_t, wu_t, wd, saved, tag):
    h, gate, up, act = saved

    def swiglu_bwd(acc, a, b):
        a, b, half = a.astype(F32), b.astype(F32), 0.5 * acc
        s = jax.nn.sigmoid(a)
        return half * b * (s * (1.0 + a * (1.0 - s))), half * (a * s)

    dgate, dup = _mm([dout_bf], [wd], name=tag + "_down_dx", tb=True, extras=[gate, up], out_dtypes=(BF, BF),
                     tm=512, tn=FF_TILE, epilogue=swiglu_bwd)
    dwd = _mm([act], [dout_bf], name=tag + "_down_dw", ta=True, tm=FF_TILE, out_dtypes=(WGRAD,),
              epilogue=lambda acc: (0.5 * acc,))
    dwg_t, dwu_t = _mm([dgate, dup], [h, h], name=tag + "_up_dw", ta=True, separate=True, out_dtypes=(WGRAD, WGRAD),
                       tm=FF_TILE)
    dh = _mm([dgate, dup], [wg_t, wu_t], name=tag + "_up_dx")
    dx, dx_bf, dg = _rms_bwd(dh, x, g, name=tag + "_norm_bwd", add=dout, out_dtypes=(F32, BF))
    return dx, dx_bf, dg, dwg_t, dwu_t, dwd


KPE_BLK = (U_SMALL + S_KPE) // HP
SMALL_BLK = U_SMALL // SMALL_W


def _local_step(x, pos_col, target, W, P):
    T = x.shape[0]
    sig = jax.nn.sigmoid
    x1, ffn1 = _ffn_fwd(x, P["ffn1_norm"], W["wg1"], W["wu1"], W["wd1"], "ffn1")
    h = _rms_fwd(x1, P["mix_norm"], name="mix_norm")
    u = _mm([h], [W["w_in"]], name="in_proj", tb=True, tn=1152)
    cqn = _rms_fwd(u, P["q_a_norm"], name="q_a_norm", blk_w=SMALL_W, blk_idx=SMALL_BLK, off=S_CQ, width=Q_LORA)
    ckvn = _rms_fwd(u, P["kv_a_norm"], name="kv_a_norm", blk_w=SMALL_W, blk_idx=SMALL_BLK, off=S_CKV, width=KV_LORA)
    q_raw = _mm([cqn], [W["wq"]], name="q_proj")
    def with_ones_lane(acc_k, acc_v):
        lane = lax.broadcasted_iota(jnp.int32, acc_v.shape, 1)
        return acc_k, jnp.where((lane & (HP - 1)) == V_HEAD, 1.0, acc_v)

    k_raw, v = _mm([ckvn], [W["wk"], W["wv"]], name="kv_proj", out_dtypes=(F32, BF), epilogue=with_ones_lane)
    rc, rs = _rope_tables(pos_col, P["freq"])
    q = _qk_prep_fwd(q_raw, None, P["q_head_norm"], rc, rs, name="q_prep", out_scale=Q_SCALE)
    k = _qk_prep_fwd(k_raw, u, P["k_head_norm"], rc, rs, name="k_prep", kpe_blk=KPE_BLK)
    o, o_bf, lse = _attn_fwd(q, k, v)
    pre, act = _conv_fwd(u, P["conv_w8"], P["conv_b"])
    scan_cols, scan_rows = _ssd_prep(u, P["dt_bias8"], P["a_log8"])
    y_f, st_f = _ssd_fwd(act, scan_cols, scan_rows, rev=False, name="ssd_fwd_f")
    y_fb, st_b = _ssd_fwd(act, scan_cols, scan_rows, rev=True, name="ssd_fwd_b", add=y_f)
    ysum, m = _ssm_combine_fwd(y_fb, act, u, P["d_skip_lanes"], P["ssm_norm"])
    ab = _mm([o_bf], [W["pa"]], name="attn_branch")
    mb, merged = _mm([m], [W["pb"]], name="ssm_branch", extras=[ab, u, u], extra_offs=(0, U_GA, U_GB), out_dtypes=(F32, BF),
                     epilogue=lambda acc, a, ga, gb: (acc, sig(ga) * a + sig(gb) * acc))
    x2 = _mm([merged], [W["wo"]], name="out_proj", extras=[x1], epilogue=lambda acc, r: (r + acc,))
    y, ffn2 = _ffn_fwd(x2, P["ffn2_norm"], W["wg2"], W["wu2"], W["wd2"], "ffn2")
    dy, dy_bf, loss = _loss_head(y, target)
    dx2, dx2_bf, dg_ffn2, dwg2, dwu2, dwd2 = _ffn_bwd(dy, dy_bf, x2, P["ffn2_norm"], W["wg2"], W["wu2"], W["wd2"], ffn2,
                                                      "ffn2")

    def gate_bwd(dmrg, a, b, ga, gb):
        sa, sb = sig(ga), sig(gb)
        return dmrg * sa, dmrg * sb, dmrg * a * sa * (1.0 - sa), dmrg * b * sb * (1.0 - sb)

    dab, dmb, dga, dgb = _mm([dx2_bf], [W["wo"]], name="out_proj_dx", tb=True, extras=[ab, mb, u, u],
                             extra_offs=(0, 0, U_GA, U_GB), out_dtypes=(BF,) * 4, epilogue=gate_bwd)
    dwo = _mm([merged], [dx2_bf], name="out_proj_dw", ta=True, out_dtypes=(WGRAD,))
    dpa = _mm([o_bf], [dab], name="attn_branch_dw", ta=True, out_dtypes=(WGRAD,))
    do = _mm([dab], [W["pa"]], name="attn_branch_dx", tb=True)
    dpb = _mm([m], [dmb], name="ssm_branch_dw", ta=True, out_dtypes=(WGRAD,))
    dm = _mm([dmb], [W["pb"]], name="ssm_branch_dx", tb=True)
    dyssd, dz, dg_ssm, dskip = _ssm_combine_bwd(dm, ysum, act, u, P["ssm_norm"])
    dxs_f, db_f, dc_f, dsel_f, dtot_f = _ssd_bwd(act, scan_cols, scan_rows, st_f, dyssd, rev=False, name="ssd_bwd_f",
                                                 skip=P["d_skip_lanes"])
    dxs, db, dc, dsel_b, dtot_b = _ssd_bwd(act, scan_cols, scan_rows, st_b, dyssd, rev=True, name="ssd_bwd_b",
                                           add=(dxs_f, db_f, dc_f))
    ddt, dalog, dbias = _ssd_prep_bwd(u, P["dt_bias8"], P["a_log8"], dsel_f, dtot_f, dsel_b, dtot_b)
    dxbc, dconv = [], []
    for tag, col0, part in (("x", 0, dxs), ("b", D_INNER, db), ("c", D_INNER + SSM_GROUPS * D_STATE, dc)):
        dxp, dwp = _conv_bwd([part], pre, u, P["conv_w8"], col0, name="conv_bwd_" + tag)
        dxbc.append(dxp)
        dconv.append(dwp)
    dconv = jnp.concatenate(dconv, axis=1)
    dq, dk, dv = _attn_bwd(q, k, v, do, o, lse)
    dq_raw, dg_qh = _qk_prep_bwd(dq, q_raw, None, P["q_head_norm"], rc, rs, name="q_prep_bwd", in_scale=ATTN_SCALE)
    dk_raw, dg_kh, dkpe = _qk_prep_bwd(dk, k_raw, u, P["k_head_norm"], rc, rs, name="k_prep_bwd", kpe_blk=KPE_BLK,
                                       in_scale=1.0 / LOG2E)
    dwq = _mm([cqn], [dq_raw], name="q_proj_dw", ta=True, out_dtypes=(WGRAD,))
    dcqn = _mm([dq_raw], [W["wq"]], name="q_proj_dx", tb=True)
    dwk, dwv = _mm([ckvn], [dk_raw, dv], name="kv_proj_dw", ta=True, out_dtypes=(WGRAD, WGRAD))
    dckvn = _mm([dk_raw, dv], [W["wk"], W["wv"]], name="kv_proj_dx", tb=True)
    dcq, dg_qa = _rms_bwd(dcqn, u, P["q_a_norm"], name="q_a_norm_bwd", blk_w=SMALL_W, blk_idx=SMALL_BLK, off=S_CQ,
                          width=Q_LORA, out_dtypes=(BF,))
    dckv, dg_kva = _rms_bwd(dckvn, u, P["kv_a_norm"], name="kv_a_norm_bwd", blk_w=SMALL_W, blk_idx=SMALL_BLK,
                            off=S_CKV, width=KV_LORA, out_dtypes=(BF,))
    du = jnp.concatenate([dz, dga, dgb] + dxbc + [dcq, dckv, dkpe.astype(BF), ddt.astype(BF)], axis=1)
    dw_in = _mm([du], [h], name="in_proj_dw", ta=True, tm=1152, out_dtypes=(WGRAD,))
    dh = _mm([du], [W["w_in"]], name="in_proj_dx")
    dx1, dx1_bf, dg_mix = _rms_bwd(dh, x1, P["mix_norm"], name="mix_norm_bwd", add=dx2, out_dtypes=(F32, BF))
    dx, _, dg_ffn1, dwg1, dwu1, dwd1 = _ffn_bwd(dx1, dx1_bf, x, P["ffn1_norm"], W["wg1"], W["wu1"], W["wd1"], ffn1, "ffn1")
    dW = dict(wg1=dwg1, wu1=dwu1, wd1=dwd1, w_in=dw_in, wq=dwq, wk=dwk, wv=dwv, pa=dpa, pb=dpb, wo=dwo,
              wg2=dwg2, wu2=dwu2, wd2=dwd2)
    dP = dict(ffn1_norm=dg_ffn1[0], mix_norm=dg_mix[0], q_a_norm=dg_qa[0], kv_a_norm=dg_kva[0],
              q_head_norm=dg_qh[0, :QK_HEAD], k_head_norm=dg_kh[0, :QK_HEAD], conv_b=dconv[CONV_WIDTH],
              a_log_fwd=dalog[0, :SSM_HEADS], a_log_bwd=dalog[0, SSM_HEADS:2 * SSM_HEADS],
              dt_bias_fwd=dbias[0, :SSM_HEADS], dt_bias_bwd=dbias[0, SSM_HEADS:2 * SSM_HEADS],
              d_skip=dskip[0].reshape(SSM_GROUPS, HP)[:, :HG], ssm_norm=dg_ssm[0], ffn2_norm=dg_ffn2[0],
              conv_w=dconv[:CONV_WIDTH], loss=loss[0, 0])
    return dx, dW, dP


def _prepare(w, conv_w_full):
    kvb = w["w_kv_b"]
    W = dict(wg1=w["ffn1_w_gate"], wu1=w["ffn1_w_up"], wd1=w["ffn1_w_down"], w_in=_pad_w_in(w["w_in"]),
             wq=_pad_heads(w["w_q_b"], 1, QK_HEAD, 0, QK_HEAD),
             wk=_pad_heads(kvb, 1, QK_NOPE + V_HEAD, 0, QK_NOPE),
             wv=_pad_heads(kvb, 1, QK_NOPE + V_HEAD, QK_NOPE, QK_NOPE + V_HEAD),
             pa=_pad_heads(w["w_attn_branch"], 0, V_HEAD, 0, V_HEAD), pb=w["w_ssm_branch"], wo=w["w_out"],
             wg2=w["ffn2_w_gate"], wu2=w["ffn2_w_up"], wd2=w["ffn2_w_down"])
    inv_freq = [1.0 / (ROPE_BASE ** (j / QK_ROPE)) for j in range(0, QK_ROPE, 2)]
    freq = [0.0] * QK_NOPE + inv_freq + inv_freq + [0.0] * (HP - QK_HEAD)
    P = {n: w[n] for n in ("ffn1_norm", "mix_norm", "q_a_norm", "kv_a_norm", "ssm_norm", "ffn2_norm", "conv_b")}
    P.update(q_head_norm=_lanes128([w["q_head_norm"]]), k_head_norm=_lanes128([w["k_head_norm"]]),
             conv_w8=jnp.pad(conv_w_full, ((0, 8 - CONV_WIDTH), (0, 0))),
             dt_bias8=jnp.broadcast_to(_lanes128([w["dt_bias_fwd"], w["dt_bias_bwd"]]), (8, HP)),
             a_log8=jnp.broadcast_to(_lanes128([w["a_log_fwd"], w["a_log_bwd"]]), (8, HP)),
             d_skip_lanes=jnp.repeat(w["d_skip"].reshape(-1), PH).reshape(1, D_INNER),
             freq=jnp.asarray(freq, F32).reshape(1, HP))
    return W, P


def _unprepare(dW):
    dkvb = jnp.concatenate([_unpad_heads(dW["wk"], 1, QK_NOPE), _unpad_heads(dW["wv"], 1, V_HEAD)], axis=2)
    return dict(ffn1_w_gate=dW["wg1"], ffn1_w_up=dW["wu1"], ffn1_w_down=dW["wd1"], w_in=_unpad_w_in(dW["w_in"]),
                w_q_b=_unpad_heads(dW["wq"], 1, QK_HEAD).reshape(Q_LORA, N_HEADS * QK_HEAD),
                w_kv_b=dkvb.reshape(KV_LORA, N_HEADS * (QK_NOPE + V_HEAD)),
                w_attn_branch=_unpad_heads(dW["pa"], 0, V_HEAD).reshape(N_HEADS * V_HEAD, D_MODEL),
                w_ssm_branch=dW["pb"], w_out=dW["wo"],
                ffn2_w_gate=dW["wg2"], ffn2_w_up=dW["wu2"], ffn2_w_down=dW["wd2"])


def kernel(x, positions, ffn1_norm, ffn1_w_gate, ffn1_w_up, ffn1_w_down, mix_norm, w_in, q_a_norm, w_q_b, kv_a_norm, w_kv_b, q_head_norm, k_head_norm, conv_w, conv_b, a_log_fwd, a_log_bwd, dt_bias_fwd, dt_bias_bwd, d_skip, ssm_norm, w_attn_branch, w_ssm_branch, w_out, ffn2_norm, ffn2_w_gate, ffn2_w_up, ffn2_w_down, loss_target, m_ffn1_norm, m_ffn1_w_gate, m_ffn1_w_up, m_ffn1_w_down, m_mix_norm, m_w_in, m_q_a_norm, m_w_q_b, m_kv_a_norm, m_w_kv_b, m_q_head_norm, m_k_head_norm, m_conv_w, m_conv_b, m_a_log_fwd, m_a_log_bwd, m_dt_bias_fwd, m_dt_bias_bwd, m_d_skip, m_ssm_norm, m_w_attn_branch, m_w_ssm_branch, m_w_out, m_ffn2_norm, m_ffn2_w_gate, m_ffn2_w_up, m_ffn2_w_down, v_ffn1_norm, v_ffn1_w_gate, v_ffn1_w_up, v_ffn1_w_down, v_mix_norm, v_w_in, v_q_a_norm, v_w_q_b, v_kv_a_norm, v_w_kv_b, v_q_head_norm, v_k_head_norm, v_conv_w, v_conv_b, v_a_log_fwd, v_a_log_bwd, v_dt_bias_fwd, v_dt_bias_bwd, v_d_skip, v_ssm_norm, v_w_attn_branch, v_w_ssm_branch, v_w_out, v_ffn2_norm, v_ffn2_w_gate, v_ffn2_w_up, v_ffn2_w_down):
    given = dict(locals())
    T = x.shape[1]
    packed_names = [name for name, _, _ in PACKED]

    def two_d(a):
        return a.reshape(a.shape[1], -1) if a.ndim > 2 else a

    def kept(n, a):
        return _stored(n, two_d(a))

    w_loc = {n: kept(n, given[n]) for n in WEIGHTS}
    wb = _pack({n: w_loc[n].astype(BF) for n in packed_names})
    wf = jnp.pad(w_loc["conv_w"], ((0, 8 - CONV_WIDTH), (0, 0)))
    gb, gf = _gather_chips(wb, wf)
    full = _full_from_slots(gb)
    conv_w_full = jnp.concatenate([gf[j, :CONV_WIDTH] for j in range(N_CHIPS)], axis=1)
    full.update({n: w_loc[n] for n in WEIGHTS if n not in full and n != "conv_w"})
    W, P = _prepare(full, conv_w_full)
    dx, dW, dP = _local_step(x.reshape(T, D_MODEL), positions.reshape(T, 1).astype(F32), loss_target.reshape(T, D_MODEL), W, P)
    gp = _slots_from_full(_unprepare(dW))
    core = lax.axis_index("c").astype(jnp.int32).reshape(1)
    both_cores = _add_halves(gp, _halves_to_sibling(gp), core)
    cx, cy = lax.axis_index("x"), lax.axis_index("y")
    chips = jnp.stack([2 * cx + cy, 2 * (1 - cx) + cy, 2 * cx + (1 - cy)]).astype(jnp.int32)
    part, on = _add_near(both_cores, _exchange_near(both_cores), chips)
    grads = _unpack(_join_halves(_add_far(part, _exchange_far(on), core)))
    small = _unpack_small(_allreduce_small(_pack_small(dP)))
    grads.update({n: small[n].reshape(1, -1) for n, _ in SMALL if n not in ("conv_w", "loss")})
    grads["conv_w"] = lax.dynamic_slice_in_dim(small["conv_w"].reshape(CONV_WIDTH, XBC_DIM), _my_chip() * (XBC_DIM // N_CHIPS),
                                               XBC_DIM // N_CHIPS, axis=1)
    out_g, out_d, out_m, out_v = [], [], [], []
    for n in WEIGHTS:
        shape = given[n].shape
        delta, new_m, new_v = _adamw(w_loc[n], grads[n], kept(n, given["m_" + n]), kept(n, given["v_" + n]), name="adamw_" + n)
        for outs, a in ((out_g, grads[n]), (out_d, delta), (out_m, new_m), (out_v, new_v)):
            outs.append(_stored(n, a).reshape(shape))
    return (small["loss"].reshape(()), dx.reshape(x.shape), *out_g, *out_d, *out_m, *out_v)
```

```python
import functools
import math

import jax
import jax.numpy as jnp
from jax import lax
from jax.experimental import pallas as pl
from jax.experimental.pallas import tpu as pltpu

BF = jnp.bfloat16
F32 = jnp.float32
HI = lax.Precision.HIGHEST
MESH = pl.DeviceIdType.MESH

D_MODEL = 1024
D_FF = 2816
EPS = 1e-6
N_HEADS = 16
QK_NOPE = 64
QK_ROPE = 32
QK_HEAD = 96
V_HEAD = 64
Q_LORA = 384
KV_LORA = 256
ROPE_BASE = 10000.0
D_INNER = 2048
SSM_HEADS = 32
SSM_GROUPS = 4
D_STATE = 128
CONV_WIDTH = 5
CHUNK = 128
XBC_DIM = 3072
HP = 128
GW = D_INNER // SSM_GROUPS
HG = SSM_HEADS // SSM_GROUPS
PH = 64
U_Z, U_GA, U_GB, U_XBC, U_SMALL = 0, 2048, 3072, 4096, 7168
S_CQ, S_CKV, S_KPE, S_DT, SMALL_W = 0, 384, 640, 768, 896
U_PAD = U_SMALL + SMALL_W
IN_SPLITS = (Q_LORA, KV_LORA, QK_ROPE, D_INNER, XBC_DIM, SSM_HEADS, SSM_HEADS, D_MODEL, D_MODEL)

ADAM_LR = 0.001
ADAM_B1 = 0.9
ADAM_B2 = 0.999
ADAM_EPS = 1e-08
ADAM_WD = 0.01
ADAM_STEP = 10

V7X_VMEM_BYTES = 64 << 20
MM_VMEM_BUDGET = V7X_VMEM_BYTES * 13 // 16

NT = (((1,), (1,)), ((), ()))
TN = (((0,), (0,)), ((), ()))


def _pick(n, pref):
    best = None
    d = 128
    while d <= min(n, pref):
        if n % d == 0:
            best = d
        d += 128
    return best if best is not None else n


def _silu(x):
    return x * jax.nn.sigmoid(x)


def _dsilu(x):
    s = jax.nn.sigmoid(x)
    return s * (1.0 + x * (1.0 - s))


def _softplus(x):
    return jnp.maximum(x, 0.0) + jnp.log(1.0 + jnp.exp(-jnp.abs(x)))


def _mm(As, Bs, *, name, ta=False, tb=False, out_dtypes=(F32,), epilogue=None, extras=(), extra_offs=None,
        tm=1024, tn=512, tk=None, separate=False):
    As, Bs, extras = list(As), list(Bs), list(extras)
    a0, b0 = As[0], Bs[0]
    M, K = (a0.shape[1], a0.shape[0]) if ta else a0.shape
    N = b0.shape[0] if tb else b0.shape[1]
    tm, tn = _pick(M, tm), _pick(N, tn)
    n_a, n_b, n_e, n_o = len(As), len(Bs), len(extras), len(out_dtypes)
    n_res = n_b if n_a == 1 or separate else 1

    def vmem_bytes(k_tile):
        blocks = sum(tm * k_tile * a.dtype.itemsize for a in As) + sum(k_tile * tn * b.dtype.itemsize for b in Bs)
        tiles = tm * tn * (sum(jnp.dtype(dt).itemsize for dt in out_dtypes) + sum(e.dtype.itemsize for e in extras))
        return 2 * (blocks + tiles) + 2 * n_res * tm * tn * 4

    if tk is None:
        tk = K
        while vmem_bytes(tk) > MM_VMEM_BUDGET and tk > 128:
            tk = _pick(K, tk - 128)
    else:
        tk = _pick(K, tk)
    nk = K // tk
    n_acc = n_res if nk > 1 else 0
    if extra_offs is None:
        extra_offs = (0,) * n_e
    dn = (((0,) if ta else (1,), (1,) if tb else (0,)), ((), ()))
    bytes_a = sum(a.size * a.dtype.itemsize for a in As)
    bytes_b = sum(b.size * b.dtype.itemsize for b in Bs)
    n_outer = (N // tn) * bytes_a + bytes_b < (M // tm) * bytes_b + bytes_a

    def products(a_refs, b_refs):
        if n_a == 1:
            a = a_refs[0][...].astype(BF)
            return [lax.dot_general(a, b[...].astype(BF), dn, preferred_element_type=F32) for b in b_refs]
        if separate:
            return [lax.dot_general(a[...].astype(BF), b[...].astype(BF), dn, preferred_element_type=F32)
                    for a, b in zip(a_refs, b_refs)]
        total = None
        for a, b in zip(a_refs, b_refs):
            p = lax.dot_general(a[...].astype(BF), b[...].astype(BF), dn, preferred_element_type=F32)
            total = p if total is None else total + p
        return [total]

    def finish(accs, e_refs, o_refs):
        ex = [e[...] for e in e_refs]
        outs = epilogue(*accs, *ex) if epilogue is not None else tuple(accs)
        for o_ref, val in zip(o_refs, outs):
            o_ref[...] = val.astype(o_ref.dtype)

    def body(*refs):
        a_refs, b_refs = refs[:n_a], refs[n_a:n_a + n_b]
        e_refs = refs[n_a + n_b:n_a + n_b + n_e]
        o_refs = refs[n_a + n_b + n_e:n_a + n_b + n_e + n_o]
        acc_refs = refs[n_a + n_b + n_e + n_o:]
        if nk == 1:
            finish(products(a_refs, b_refs), e_refs, o_refs)
            return
        k = pl.program_id(2)

        @pl.when(k == 0)
        def _():
            for acc in acc_refs:
                acc[...] = jnp.zeros_like(acc)

        for acc, p in zip(acc_refs, products(a_refs, b_refs)):
            acc[...] += p

        @pl.when(k == nk - 1)
        def _():
            finish([acc[...] for acc in acc_refs], e_refs, o_refs)

    def at(f):
        return (lambda j, i, k: f(i, j, k)) if n_outer else f

    a_spec = pl.BlockSpec((tk, tm), at(lambda i, j, k: (k, i))) if ta else pl.BlockSpec((tm, tk), at(lambda i, j, k: (i, k)))
    b_spec = pl.BlockSpec((tn, tk), at(lambda i, j, k: (j, k))) if tb else pl.BlockSpec((tk, tn), at(lambda i, j, k: (k, j)))
    e_specs = [pl.BlockSpec((tm, tn), at(functools.partial(lambda i, j, k, o: (i, j + o), o=off // tn))) for off in extra_offs]
    for off in extra_offs:
        assert off % tn == 0
    outs = pl.pallas_call(
        body, name=name,
        out_shape=tuple(jax.ShapeDtypeStruct((M, N), dt) for dt in out_dtypes),
        grid=(N // tn, M // tm, nk) if n_outer else (M // tm, N // tn, nk),
        in_specs=[a_spec] * n_a + [b_spec] * n_b + e_specs,
        out_specs=tuple(pl.BlockSpec((tm, tn), at(lambda i, j, k: (i, j))) for _ in out_dtypes),
        scratch_shapes=[pltpu.VMEM((tm, tn), F32)] * n_acc,
        compiler_params=pltpu.CompilerParams(dimension_semantics=("parallel", "parallel", "arbitrary")),
    )(*As, *Bs, *extras)
    return outs[0] if n_o == 1 else outs


def _rms_fwd(x, g, *, name, blk_w=None, blk_idx=0, off=0, width=None, out_dtype=BF):
    T = x.shape[0]
    blk_w = x.shape[1] if blk_w is None else blk_w
    width = blk_w if width is None else width
    tt = _pick(T, 512)

    def body(x_ref, g_ref, o_ref):
        xf = x_ref[:, off:off + width]
        r = lax.rsqrt(jnp.mean(xf * xf, axis=-1, keepdims=True) + EPS)
        o_ref[...] = (xf * r * g_ref[...]).astype(o_ref.dtype)

    return pl.pallas_call(
        body, name=name, out_shape=jax.ShapeDtypeStruct((T, width), out_dtype), grid=(T // tt,),
        in_specs=[pl.BlockSpec((tt, blk_w), lambda i: (i, blk_idx)), pl.BlockSpec((1, width), lambda i: (0, 0))],
        out_specs=pl.BlockSpec((tt, width), lambda i: (i, 0)),
    )(x, g)


def _rms_bwd(dy, x, g, *, name, blk_w=None, blk_idx=0, off=0, width=None, add=None, out_dtypes=(F32,)):
    T = x.shape[0]
    blk_w = x.shape[1] if blk_w is None else blk_w
    width = blk_w if width is None else width
    tt = _pick(T, 512)
    has_add = add is not None
    n_dx = len(out_dtypes)

    def body(*refs):
        dy_ref, x_ref, g_ref = refs[:3]
        dx_refs, dg_ref = refs[3 + has_add:3 + has_add + n_dx], refs[-1]
        xf = x_ref[:, off:off + width]
        d = dy_ref[...].astype(F32)
        r = lax.rsqrt(jnp.mean(xf * xf, axis=-1, keepdims=True) + EPS)
        gd = d * g_ref[...]
        dx = r * gd - xf * (r * r * r) * jnp.mean(gd * xf, axis=-1, keepdims=True)
        if has_add:
            dx = dx + refs[3][...]
        for dx_ref in dx_refs:
            dx_ref[...] = dx.astype(dx_ref.dtype)

        @pl.when(pl.program_id(0) == 0)
        def _():
            dg_ref[...] = jnp.zeros_like(dg_ref)

        dg_ref[...] += jnp.broadcast_to(jnp.sum(d * xf * r, axis=0, keepdims=True), dg_ref.shape)

    row = pl.BlockSpec((tt, width), lambda i: (i, 0))
    in_specs = [row, pl.BlockSpec((tt, blk_w), lambda i: (i, blk_idx)), pl.BlockSpec((1, width), lambda i: (0, 0))]
    args = [dy, x, g]
    if has_add:
        in_specs.append(row)
        args.append(add)
    return pl.pallas_call(
        body, name=name,
        out_shape=tuple(jax.ShapeDtypeStruct((T, width), dt) for dt in out_dtypes) + (jax.ShapeDtypeStruct((8, width), F32),),
        grid=(T // tt,), in_specs=in_specs,
        out_specs=(row,) * n_dx + (pl.BlockSpec((8, width), lambda i: (0, 0)),),
        compiler_params=pltpu.CompilerParams(dimension_semantics=("arbitrary",)),
    )(*args)


def _rope_tables(pos_col, freq_lane):
    T = pos_col.shape[0]
    tt = _pick(T, 512)

    def body(p_ref, f_ref, c_ref, s_ref):
        ang = p_ref[...] * f_ref[...]
        lane = lax.broadcasted_iota(jnp.int32, ang.shape, 1)
        c_ref[...] = jnp.where(lane < QK_HEAD, jnp.cos(ang), 0.0)
        sn = jnp.sin(ang)
        s_ref[...] = jnp.where((lane >= QK_NOPE) & (lane < QK_NOPE + 16), -sn,
                               jnp.where((lane >= QK_NOPE + 16) & (lane < QK_HEAD), sn, 0.0))

    return pl.pallas_call(
        body, name="rope_tables", out_shape=(jax.ShapeDtypeStruct((T, HP), F32),) * 2, grid=(T // tt,),
        in_specs=[pl.BlockSpec((tt, 1), lambda i: (i, 0)), pl.BlockSpec((1, HP), lambda i: (0, 0))],
        out_specs=(pl.BlockSpec((tt, HP), lambda i: (i, 0)),) * 2,
    )(pos_col, freq_lane)


def _swap_rope_halves(n):
    src = lax.broadcasted_iota(jnp.int32, (HP, HP), 0)
    dst = lax.broadcasted_iota(jnp.int32, (HP, HP), 1)
    lo = (dst >= QK_NOPE) & (dst < QK_NOPE + 16) & (src == dst + 16)
    hi = (dst >= QK_NOPE + 16) & (dst < QK_HEAD) & (src == dst - 16)
    return _split_dot(n, jnp.where(lo | hi, 1.0, 0.0).astype(BF), 2)


def _qk_prep_fwd(raw, kpe, gain, C, S, *, name, kpe_blk=0, out_scale=1.0):
    T = raw.shape[0]
    tt = _pick(T, 256)
    has_kpe = kpe is not None

    def body(*refs):
        if has_kpe:
            raw_ref, kpe_ref, g_ref, c_ref, s_ref, o_ref = refs
        else:
            raw_ref, g_ref, c_ref, s_ref, o_ref = refs
        for h in range(N_HEADS):
            hs = slice(HP * h, HP * (h + 1))
            xr = raw_ref[:, hs] + kpe_ref[...] if has_kpe else raw_ref[:, hs]
            r = lax.rsqrt(jnp.sum(xr * xr, axis=-1, keepdims=True) * (1.0 / QK_HEAD) + EPS)
            n = xr * r * g_ref[...]
            o_ref[:, hs] = ((n * c_ref[...] + _swap_rope_halves(n) * s_ref[...]) * out_scale).astype(o_ref.dtype)

    heads = pl.BlockSpec((tt, N_HEADS * HP), lambda i: (i, 0))
    shared = pl.BlockSpec((tt, HP), lambda i: (i, 0))
    kpe_spec = pl.BlockSpec((tt, HP), lambda i: (i, kpe_blk))
    in_specs = [heads] + ([kpe_spec] if has_kpe else []) + [pl.BlockSpec((1, HP), lambda i: (0, 0)), shared, shared]
    args = [raw] + ([kpe] if has_kpe else []) + [gain, C, S]
    return pl.pallas_call(
        body, name=name, out_shape=jax.ShapeDtypeStruct(raw.shape, BF), grid=(T // tt,),
        in_specs=in_specs, out_specs=heads,
    )(*args)


def _qk_prep_bwd(dout, raw, kpe, gain, C, S, *, name, kpe_blk=0, in_scale=1.0):
    T = raw.shape[0]
    tt = _pick(T, 256)
    has_kpe = kpe is not None

    def body(*refs):
        if has_kpe:
            d_ref, raw_ref, kpe_ref, g_ref, c_ref, s_ref, dx_ref, dg_ref, dkpe_ref = refs
        else:
            d_ref, raw_ref, g_ref, c_ref, s_ref, dx_ref, dg_ref = refs
        dg = jnp.zeros((1, HP), F32)
        dkpe = jnp.zeros((tt, HP), F32)
        for h in range(N_HEADS):
            hs = slice(HP * h, HP * (h + 1))
            xr = raw_ref[:, hs] + kpe_ref[...] if has_kpe else raw_ref[:, hs]
            d = d_ref[:, hs].astype(F32) * in_scale
            r = lax.rsqrt(jnp.sum(xr * xr, axis=-1, keepdims=True) * (1.0 / QK_HEAD) + EPS)
            dn = d * c_ref[...] + _swap_rope_halves(d * s_ref[...])
            gd = dn * g_ref[...]
            dx = r * gd - xr * (r * r * r) * (jnp.sum(gd * xr, axis=-1, keepdims=True) * (1.0 / QK_HEAD))
            dx_ref[:, hs] = dx.astype(dx_ref.dtype)
            dg = dg + jnp.sum(dn * xr * r, axis=0, keepdims=True)
            dkpe = dkpe + dx

        @pl.when(pl.program_id(0) == 0)
        def _():
            dg_ref[...] = jnp.zeros_like(dg_ref)

        dg_ref[...] += jnp.broadcast_to(dg, dg_ref.shape)
        if has_kpe:
            dkpe_ref[...] = dkpe

    heads = pl.BlockSpec((tt, N_HEADS * HP), lambda i: (i, 0))
    shared = pl.BlockSpec((tt, HP), lambda i: (i, 0))
    kpe_spec = pl.BlockSpec((tt, HP), lambda i: (i, kpe_blk))
    in_specs = [heads, heads] + ([kpe_spec] if has_kpe else []) + [pl.BlockSpec((1, HP), lambda i: (0, 0)), shared, shared]
    args = [dout, raw] + ([kpe] if has_kpe else []) + [gain, C, S]
    out_shape = [jax.ShapeDtypeStruct(raw.shape, BF), jax.ShapeDtypeStruct((8, HP), F32)]
    out_specs = [heads, pl.BlockSpec((8, HP), lambda i: (0, 0))]
    if has_kpe:
        out_shape.append(jax.ShapeDtypeStruct((T, HP), F32))
        out_specs.append(shared)
    return pl.pallas_call(
        body, name=name, out_shape=tuple(out_shape), grid=(T // tt,),
        in_specs=in_specs, out_specs=tuple(out_specs),
        compiler_params=pltpu.CompilerParams(dimension_semantics=("arbitrary",)),
    )(*args)


ATTN_SCALE = 1.0 / math.sqrt(QK_HEAD)
LOG2E = 1.0 / math.log(2.0)
Q_SCALE = ATTN_SCALE * LOG2E


def _attn_fwd(q, k, v):
    T = q.shape[0]
    tq = _pick(T, 1024)
    tk = _pick(T, 1024)

    def body(q_ref, k_ref, v_ref, o_ref, ob_ref, lse_ref):
        qt = q_ref[...]
        m = o = None
        for j in range(T // tk):
            ks = slice(j * tk, (j + 1) * tk)
            s = lax.dot_general(qt, k_ref[ks, :], NT, preferred_element_type=F32)
            m_j = jnp.max(s, axis=-1, keepdims=True)
            m_new = m_j if m is None else jnp.maximum(m, m_j)
            o_j = jnp.dot(jnp.exp2(s - m_new).astype(BF), v_ref[ks, :], preferred_element_type=F32)
            o = o_j if o is None else o * jnp.exp2(m - m_new) + o_j
            m = m_new
        l = o[:, V_HEAD:V_HEAD + 1]
        o = o / l
        o_ref[...] = o
        ob_ref[...] = o.astype(ob_ref.dtype)
        lse_ref[...] = jnp.broadcast_to(m + jnp.log2(l), lse_ref.shape)

    qs = pl.BlockSpec((tq, HP), lambda h, i: (i, h))
    kv = pl.BlockSpec((T, HP), lambda h, i: (0, h))
    return pl.pallas_call(
        body, name="attn_fwd",
        out_shape=(jax.ShapeDtypeStruct(q.shape, F32), jax.ShapeDtypeStruct(q.shape, BF), jax.ShapeDtypeStruct(q.shape, F32)),
        grid=(N_HEADS, T // tq), in_specs=[qs, kv, kv], out_specs=(qs, qs, qs),
        compiler_params=pltpu.CompilerParams(dimension_semantics=("parallel", "parallel")),
    )(q, k, v)


def _attn_bwd(q, k, v, do, o, lse):
    T = q.shape[0]
    tb = _pick(T, 1024)
    nb = T // tb
    tkey = _pick(T, 1024)

    def body(q_ref, k_ref, v_ref, do_ref, o_ref, lse_ref, dq_ref, dk_ref, dv_ref, delta_rows, lse_rows, dob_scr, dv_acc):
        dq_ref[...] = jnp.zeros_like(dq_ref)
        dk_ref[...] = jnp.zeros_like(dk_ref)
        lane = lax.broadcasted_iota(jnp.int32, (8, HP), 1)
        ones8 = jnp.ones((8, HP), BF)
        first8 = jnp.where(lane == 0, 1.0, 0.0).astype(BF)

        def as_rows(pick, v):
            total, rest = None, v
            for _ in range(3):
                piece = rest.astype(BF)
                part = lax.dot_general(pick, piece, NT, preferred_element_type=F32)
                total = part if total is None else total + part
                rest = rest - piece.astype(F32)
            return total

        def per_q_tile(i, carry):
            qs = pl.ds(pl.multiple_of(i * tb, tb), tb)
            doi = do_ref[qs, :]
            delta_rows[i] = as_rows(ones8, doi * o_ref[qs, :])
            lse_rows[i] = as_rows(first8, lse_ref[qs, :])
            dob_scr[qs, :] = doi.astype(BF)
            return carry

        lax.fori_loop(0, nb, per_q_tile, 0)

        def k_loop(j, carry):
            ks = pl.ds(pl.multiple_of(j * tkey, tkey), tkey)
            kj, vj = k_ref[ks, :], v_ref[ks, :]

            dv_acc[...] = jnp.zeros_like(dv_acc)

            def q_loop(i, carry_q):
                qs = pl.ds(pl.multiple_of(i * tb, tb), tb)
                qi = q_ref[qs, :]
                dob = dob_scr[qs, :]
                s_t = lax.dot_general(kj, qi, NT, preferred_element_type=F32)
                p_t = jnp.exp2(s_t - lse_rows[i, 0:1, :])
                dp_t = lax.dot_general(vj, dob, NT, preferred_element_type=F32)
                ds_t = (p_t * (dp_t - delta_rows[i, 0:1, :])).astype(BF)
                dv_acc[...] += jnp.dot(p_t.astype(BF), dob, preferred_element_type=F32)
                dk_ref[ks, :] += jnp.dot(ds_t, qi, preferred_element_type=F32)
                dq_ref[qs, :] += lax.dot_general(ds_t, kj, TN, preferred_element_type=F32)
                return carry_q

            lax.fori_loop(0, nb, q_loop, 0)
            dv_ref[ks, :] = dv_acc[...].astype(dv_ref.dtype)
            return carry

        lax.fori_loop(0, T // tkey, k_loop, 0)

    spec = pl.BlockSpec((T, HP), lambda h: (0, h))
    return pl.pallas_call(
        body, name="attn_bwd",
        out_shape=(jax.ShapeDtypeStruct(q.shape, F32), jax.ShapeDtypeStruct(q.shape, F32), jax.ShapeDtypeStruct(q.shape, BF)),
        grid=(N_HEADS,), in_specs=[spec] * 6, out_specs=(spec,) * 3,
        scratch_shapes=[pltpu.VMEM((nb, 8, tb), F32), pltpu.VMEM((nb, 8, tb), F32), pltpu.VMEM((T, HP), BF),
                        pltpu.VMEM((tkey, HP), F32)],
        compiler_params=pltpu.CompilerParams(dimension_semantics=("parallel",), vmem_limit_bytes=2 * 15 * T * HP * 2 + (8 << 20)),
    )(q, k, v, do, o, lse)


CONV_TC = 512
CONV_PAD = CONV_WIDTH // 2


def _halo_specs(tr, col_of):
    r8 = tr // 8
    cur = pl.BlockSpec((tr, CONV_TC), lambda j, i: (i, col_of(j)))
    prev = pl.BlockSpec((8, CONV_TC), lambda j, i: (jnp.maximum(i * r8 - 1, 0), col_of(j)))

    def nxt_map(j, i, n8):
        return (jnp.minimum((i + 1) * r8, n8 - 1), col_of(j))

    return cur, prev, nxt_map


def _with_halo(prev_ref, cur_ref, next_ref, i, n_i):
    prev = jnp.where(i == 0, 0.0, prev_ref[...].astype(F32))
    nxt = jnp.where(i == n_i - 1, 0.0, next_ref[...].astype(F32))
    return jnp.concatenate([prev, cur_ref[...].astype(F32), nxt], axis=0)


def _conv_fwd(u, w8, b):
    T = u.shape[0]
    tr = _pick(T, 512)
    n_i = T // tr
    c0 = U_XBC // CONV_TC
    cur, prev, nxt_map = _halo_specs(tr, lambda j: c0 + j)
    nxt = pl.BlockSpec((8, CONV_TC), functools.partial(nxt_map, n8=T // 8))

    def body(p_ref, c_ref, n_ref, w_ref, b_ref, pre_ref, act_ref):
        i = pl.program_id(1)
        full = _with_halo(p_ref, c_ref, n_ref, i, n_i)
        acc = jnp.broadcast_to(b_ref[...], (tr, CONV_TC))
        for kk in range(CONV_WIDTH):
            acc = acc + full[8 - CONV_PAD + kk:8 - CONV_PAD + kk + tr, :] * w_ref[kk:kk + 1, :]
        pre_ref[...] = acc
        act_ref[...] = _silu(acc)

    out = pl.BlockSpec((tr, CONV_TC), lambda j, i: (i, j))
    return pl.pallas_call(
        body, name="conv_fwd", out_shape=(jax.ShapeDtypeStruct((T, XBC_DIM), F32),) * 2,
        grid=(XBC_DIM // CONV_TC, n_i),
        in_specs=[prev, cur, nxt, pl.BlockSpec((8, CONV_TC), lambda j, i: (0, j)), pl.BlockSpec((1, CONV_TC), lambda j, i: (0, j))],
        out_specs=(out, out),
    )(u, u, u, w8, b)


def _conv_bwd(dacts, pre, u, w8, col0, *, name):
    T, width = dacts[0].shape
    tr = _pick(T, 512)
    n_i = T // tr
    n_d = len(dacts)
    cd = col0 // CONV_TC
    cx = (U_XBC + col0) // CONV_TC

    def halo(col_of):
        cur, prev, nxt_map = _halo_specs(tr, col_of)
        return [prev, cur, pl.BlockSpec((8, CONV_TC), functools.partial(nxt_map, n8=T // 8))]

    def body(*refs):
        d_refs, pre_refs, x_refs = refs[:3 * n_d], refs[3 * n_d:3 * n_d + 3], refs[3 * n_d + 3:3 * n_d + 6]
        w_ref, dx_ref, dw_ref = refs[3 * n_d + 6:]
        i = pl.program_id(1)
        dfull = _with_halo(*d_refs[0:3], i, n_i)
        for p in range(1, n_d):
            dfull = dfull + _with_halo(*d_refs[3 * p:3 * p + 3], i, n_i)
        dfull = dfull * _dsilu(_with_halo(*pre_refs, i, n_i))
        xfull = _with_halo(*x_refs, i, n_i)
        dcur = dfull[8:8 + tr, :]
        dx = jnp.zeros((tr, CONV_TC), F32)
        rows = []
        for kk in range(CONV_WIDTH):
            dx = dx + dfull[8 + CONV_PAD - kk:8 + CONV_PAD - kk + tr, :] * w_ref[kk:kk + 1, :]
            rows.append(jnp.sum(dcur * xfull[8 - CONV_PAD + kk:8 - CONV_PAD + kk + tr, :], axis=0, keepdims=True))
        rows.append(jnp.sum(dcur, axis=0, keepdims=True))
        rows.append(jnp.zeros((2, CONV_TC), F32))
        dx_ref[...] = dx.astype(dx_ref.dtype)

        @pl.when(i == 0)
        def _():
            dw_ref[...] = jnp.zeros_like(dw_ref)

        dw_ref[...] += jnp.concatenate(rows, axis=0)

    out = pl.BlockSpec((tr, CONV_TC), lambda j, i: (i, j))
    return pl.pallas_call(
        body, name=name, out_shape=(jax.ShapeDtypeStruct((T, width), BF), jax.ShapeDtypeStruct((8, width), F32)),
        grid=(width // CONV_TC, n_i),
        in_specs=halo(lambda j: j) * n_d + halo(lambda j: cd + j) + halo(lambda j: cx + j)
        + [pl.BlockSpec((8, CONV_TC), lambda j, i: (0, cd + j))],
        out_specs=(out, pl.BlockSpec((8, CONV_TC), lambda j, i: (0, j))),
        compiler_params=pltpu.CompilerParams(dimension_semantics=("parallel", "arbitrary")),
    )(*[d for d in dacts for _ in range(3)], pre, pre, pre, u, u, u, w8)


N_HB = 2 * SSM_GROUPS
P_DT, P_CS, P_E, P_W = 0, HP, 2 * HP, 3 * HP
DT_BLK = (U_SMALL + S_DT) // HP


def _tri(rev, transpose=False):
    rows = lax.broadcasted_iota(jnp.int32, (CHUNK, CHUNK), 0)
    cols = lax.broadcasted_iota(jnp.int32, (CHUNK, CHUNK), 1)
    if transpose:
        rows, cols = cols, rows
    return (cols >= rows) if rev else (cols <= rows)


def _ssd_prep(u, bias8, alog8):
    T = u.shape[0]
    nc = T // CHUNK

    def body(dt_ref, bias_ref, a_ref, cols_ref, rows_ref):
        lane = lax.broadcasted_iota(jnp.int32, (CHUNK, HP), 1)
        dt = _softplus(dt_ref[...] + bias_ref[0:1, :])
        da = dt * (-jnp.exp(a_ref[0:1, :]))
        cs_f = jnp.dot(jnp.where(_tri(False), 1.0, 0.0).astype(F32), da, precision=HI, preferred_element_type=F32)
        cs_b = jnp.dot(jnp.where(_tri(True), 1.0, 0.0).astype(F32), da, precision=HI, preferred_element_type=F32)
        cs = jnp.where(lane < SSM_HEADS, cs_f, cs_b)
        tot = jnp.where(lane[0:1] < SSM_HEADS, cs_f[CHUNK - 1:CHUNK, :], cs_b[0:1, :])
        e, w = jnp.exp(cs), jnp.exp(tot - cs)
        tot8 = jnp.broadcast_to(tot, (8, HP))
        etot8 = jnp.exp(tot8)
        for b in range(N_HB):
            down = (HP - HG * b) % HP

            def rolled(v):
                return pltpu.roll(v, down, 1) if down else v

            cols_ref[b, :, P_DT:P_DT + HP] = rolled(dt)
            cs_r = rolled(cs)
            cols_ref[b, :, P_CS:P_CS + HP] = cs_r
            cols_ref[b, :, P_E:P_E + HP] = rolled(e)
            cols_ref[b, :, P_W:P_W + HP] = rolled(w)
            rows_ref[b, 0, 0:8, :] = cs_r.T[0:8, :]
            r8 = lax.broadcasted_iota(jnp.int32, (8, HP), 0)
            rows_ref[b, 0, 8:16, :] = jnp.where(r8 == 0, rolled(tot8), jnp.where(r8 == 1, rolled(etot8), 0.0))

    vec = pl.BlockSpec((8, HP), lambda c: (0, 0))
    return pl.pallas_call(
        body, name="ssd_prep",
        out_shape=(jax.ShapeDtypeStruct((N_HB, T, 4 * HP), F32), jax.ShapeDtypeStruct((N_HB, nc, 16, HP), F32)),
        grid=(nc,), in_specs=[pl.BlockSpec((CHUNK, HP), lambda c: (c, DT_BLK)), vec, vec],
        out_specs=(pl.BlockSpec((N_HB, CHUNK, 4 * HP), lambda c: (0, c, 0)), pl.BlockSpec((N_HB, 1, 16, HP), lambda c: (0, c, 0, 0))),
    )(u, bias8, alog8)


def _ssd_specs(T, rev, bwd):
    nc = T // CHUNK
    fwd_order = (lambda c: nc - 1 - c) if rev else (lambda c: c)
    cm = (lambda c: fwd_order(nc - 1 - c)) if bwd else fwd_order
    hb0 = SSM_GROUPS if rev else 0
    xs = pl.BlockSpec((CHUNK, GW), lambda c, g: (cm(c), g))
    bs = pl.BlockSpec((CHUNK, D_STATE), lambda c, g: (cm(c), D_INNER // D_STATE + g))
    cs = pl.BlockSpec((CHUNK, D_STATE), lambda c, g: (cm(c), (D_INNER + SSM_GROUPS * D_STATE) // D_STATE + g))
    cols = pl.BlockSpec((1, CHUNK, 4 * HP), lambda c, g: (hb0 + g, cm(c), 0))
    rows = pl.BlockSpec((1, 1, 16, HP), lambda c, g: (hb0 + g, cm(c), 0, 0))
    return nc, cm, xs, bs, cs, cols, rows


def _head_lanes(to_heads):
    shape = (GW, HP) if to_heads else (HP, GW)
    wide = lax.broadcasted_iota(jnp.int32, shape, 0 if to_heads else 1)
    head = lax.broadcasted_iota(jnp.int32, shape, 1 if to_heads else 0)
    return jnp.where((wide >= PH * head) & (wide < PH * (head + 1)), 1.0, 0.0).astype(BF)


def _split_dot(v, m, terms):
    total, rest = None, v
    for _ in range(terms):
        piece = rest.astype(BF)
        part = jnp.dot(piece, m, preferred_element_type=F32)
        total = part if total is None else total + part
        rest = rest - piece.astype(F32)
    return total


def _spread_cols(cols_ref, rows_ref):
    spread = _head_lanes(False)
    dt_e = _split_dot(cols_ref[0, :, P_DT:P_DT + HP], spread, 3)
    e_e = _split_dot(cols_ref[0, :, P_E:P_E + HP], spread, 2)
    w_e = _split_dot(cols_ref[0, :, P_W:P_W + HP], spread, 2)
    etot_e = _split_dot(rows_ref[0, 0, 8:16, :], spread, 3)[1:2, :]
    return dt_e, e_e, w_e, etot_e


def _decay(cols_ref, rows_ref, hh, incl, transpose=False):
    col = cols_ref[0, :, P_CS + hh:P_CS + hh + 1]
    row = rows_ref[0, 0, hh:hh + 1, :]
    return jnp.where(incl, jnp.exp(row - col if transpose else col - row), 0.0)


def _ssd_fwd(act, cols, rows, *, rev, name, add=None):
    T = act.shape[0]
    nc, cm, xs_s, b_s, c_s, cols_s, rows_s = _ssd_specs(T, rev, False)
    has_add = add is not None

    def body(*refs):
        x_ref, b_ref, c_ref, cols_ref, rows_ref = refs[:5]
        y_ref, st_ref, state = refs[5 + has_add:]
        c, g = pl.program_id(0), pl.program_id(1)

        @pl.when(c == 0)
        def _():
            state[g] = jnp.zeros((D_STATE, GW), F32)

        incl = _tri(rev)
        bm, cmat = b_ref[...].astype(BF), c_ref[...].astype(BF)
        bm_t = b_ref[...].T.astype(BF)
        cb = lax.dot_general(cmat, bm, NT, preferred_element_type=F32)
        dt_e, e_e, w_e, etot_e = _spread_cols(cols_ref, rows_ref)
        prev_all = state[g]
        st_ref[...] = prev_all
        xdt = x_ref[...] * dt_e
        xdt_b = xdt.astype(BF)
        yo_all = jnp.dot(cmat, prev_all.astype(BF), preferred_element_type=F32) * e_e
        state[g] = prev_all * etot_e + jnp.dot(bm_t, (xdt * w_e).astype(BF), preferred_element_type=F32)
        for hh in range(HG):
            hs = slice(PH * hh, PH * (hh + 1))
            lmat = _decay(cols_ref, rows_ref, hh, incl)
            yd = jnp.dot((cb * lmat).astype(BF), xdt_b[:, hs], preferred_element_type=F32)
            y_ref[:, hs] = yd + yo_all[:, hs] + refs[5][:, hs] if has_add else yd + yo_all[:, hs]

    return pl.pallas_call(
        body, name=name,
        out_shape=(jax.ShapeDtypeStruct((T, D_INNER), F32), jax.ShapeDtypeStruct((nc * D_STATE, D_INNER), F32)),
        grid=(nc, SSM_GROUPS), in_specs=[xs_s, b_s, c_s, cols_s, rows_s] + [xs_s] * has_add, out_specs=(xs_s, xs_s),
        scratch_shapes=[pltpu.VMEM((SSM_GROUPS, D_STATE, GW), F32)],
        compiler_params=pltpu.CompilerParams(dimension_semantics=("arbitrary", "arbitrary")),
    )(act, act, act, cols, rows, *([add] if has_add else []))


def _ssd_bwd(act, cols, rows, states, dy, *, rev, name, skip=None, add=None):
    T = act.shape[0]
    nc, cm, xs_s, b_s, c_s, cols_s, rows_s = _ssd_specs(T, rev, True)
    has_skip, has_add = skip is not None, add is not None
    n_in = 7 + has_skip + 3 * has_add

    def body(*refs):
        x_ref, b_ref, c_ref, cols_ref, rows_ref, st_ref, dy_ref = refs[:7]
        extra = list(refs[7:n_in])
        dx_ref, db_ref, dc_ref, dsel_ref, dtot_ref, dstate, dcs_cols, dcs_rows, dcb, dm_scr, dxdt_scr = refs[n_in:]
        c, g = pl.program_id(0), pl.program_id(1)

        @pl.when(c == 0)
        def _():
            dstate[g] = jnp.zeros((D_STATE, GW), F32)

        incl, incl_t = _tri(rev), _tri(rev, transpose=True)
        bm, cmat = b_ref[...].astype(BF), c_ref[...].astype(BF)
        cm_t = c_ref[...].T.astype(BF)
        cb = lax.dot_general(cmat, bm, NT, preferred_element_type=F32)
        cb_t = lax.dot_general(bm, cmat, NT, preferred_element_type=F32)
        prev_all, ds_all = st_ref[...], dstate[g]
        pb_all, dsb_all = prev_all.astype(BF), ds_all.astype(BF)
        cp_all = jnp.dot(cmat, pb_all, preferred_element_type=F32)
        bds_all = jnp.dot(bm, dsb_all, preferred_element_type=F32)
        dt_e, e_e, w_e, etot_e = _spread_cols(cols_ref, rows_ref)
        to_heads = _head_lanes(True)
        x, dy = x_ref[...], dy_ref[...]
        xdt = x * dt_e
        xdt_b, dy_b = xdt.astype(BF), dy.astype(BF)
        dye_b, xdw_b = (dy * e_e).astype(BF), (xdt * w_e).astype(BF)
        for hh in range(HG):
            hs = slice(PH * hh, PH * (hh + 1))
            mmat_t = cb_t * _decay(cols_ref, rows_ref, hh, incl_t, transpose=True)
            dm_scr[hh] = lax.dot_general(dy_b[:, hs], xdt_b[:, hs], NT, preferred_element_type=F32)
            dxdt_scr[:, hs] = jnp.dot(mmat_t.astype(BF), dy_b[:, hs], preferred_element_type=F32)
        bdsw = bds_all * w_e
        dxdt = dxdt_scr[...] + bdsw
        dx = dxdt * dt_e
        if has_skip:
            dx = dx + dy * extra.pop(0)[...]
        if has_add:
            dx = dx + extra[0][...]
        dx_ref[...] = dx
        t = _split_dot(xdt * bdsw, to_heads, 2)
        dcs_state = _split_dot(dy * cp_all, to_heads, 2) * cols_ref[0, :, P_E:P_E + HP] - t
        dsel_ref[0, :, 0:HP] = _split_dot(dxdt * x, to_heads, 2)
        sp = _split_dot(jnp.broadcast_to(jnp.sum(ds_all * prev_all, axis=0, keepdims=True), (8, GW)), to_heads, 2)
        dtot_ref[0, 0] = jnp.sum(t, axis=0, keepdims=True) + sp * rows_ref[0, 0, 9:10, :]
        dstate[g] = ds_all * etot_e + jnp.dot(cm_t, dye_b, preferred_element_type=F32)
        dcs_cols[...] = jnp.zeros_like(dcs_cols)
        dcs_rows[...] = jnp.zeros_like(dcs_rows)
        dcb[...] = jnp.zeros_like(dcb)
        for hh in range(HG):
            lmat = _decay(cols_ref, rows_ref, hh, incl)
            dm = dm_scr[hh]
            qm = dm * (cb * lmat)
            dcs_cols[:, hh:hh + 1] = jnp.sum(qm, axis=1, keepdims=True)
            dcs_rows[hh:hh + 1, :] = jnp.sum(qm, axis=0, keepdims=True)
            dcb[...] += dm * lmat
        dcb_all = dcb[...]
        dsel_ref[0, :, HP:2 * HP] = dcs_state + dcs_cols[...] - dcs_rows[...].T
        dc = (lax.dot_general(dye_b, pb_all, NT, preferred_element_type=F32)
              + jnp.dot(dcb_all.astype(BF), bm, preferred_element_type=F32))
        db = (lax.dot_general(xdw_b, dsb_all, NT, preferred_element_type=F32)
              + jnp.dot(dcb_all.T.astype(BF), cmat, preferred_element_type=F32))
        db_ref[...] = db + extra[1][...] if has_add else db
        dc_ref[...] = dc + extra[2][...] if has_add else dc

    bc_out = pl.BlockSpec((CHUNK, D_STATE), lambda c, g: (cm(c), g))
    more_specs = [pl.BlockSpec((1, GW), lambda c, g: (0, g))] * has_skip + [xs_s, bc_out, bc_out] * has_add
    more_args = ([skip] if has_skip else []) + (list(add) if has_add else [])
    return pl.pallas_call(
        body, name=name,
        out_shape=(jax.ShapeDtypeStruct((T, D_INNER), F32), jax.ShapeDtypeStruct((T, SSM_GROUPS * D_STATE), F32),
                   jax.ShapeDtypeStruct((T, SSM_GROUPS * D_STATE), F32), jax.ShapeDtypeStruct((SSM_GROUPS, T, 2 * HP), F32),
                   jax.ShapeDtypeStruct((SSM_GROUPS, nc, 8, HP), F32)),
        grid=(nc, SSM_GROUPS), in_specs=[xs_s, b_s, c_s, cols_s, rows_s, xs_s, xs_s] + more_specs,
        out_specs=(xs_s, bc_out, bc_out, pl.BlockSpec((1, CHUNK, 2 * HP), lambda c, g: (g, cm(c), 0)),
                   pl.BlockSpec((1, 1, 8, HP), lambda c, g: (g, cm(c), 0, 0))),
        scratch_shapes=[pltpu.VMEM((SSM_GROUPS, D_STATE, GW), F32), pltpu.VMEM((CHUNK, CHUNK), F32),
                        pltpu.VMEM((CHUNK, CHUNK), F32), pltpu.VMEM((CHUNK, CHUNK), F32),
                        pltpu.VMEM((HG, CHUNK, CHUNK), F32), pltpu.VMEM((CHUNK, GW), F32)],
        compiler_params=pltpu.CompilerParams(dimension_semantics=("arbitrary", "arbitrary")),
    )(act, act, act, cols, rows, states, dy, *more_args)


def _ssd_prep_bwd(u, bias8, alog8, dsel_f, dtot_f, dsel_b, dtot_b):
    T = u.shape[0]
    nc = T // CHUNK

    def body(dt_ref, bias_ref, a_ref, sf_ref, tf_ref, sb_ref, tb_ref, ddt_ref, da_ref, dbias_ref):
        @pl.when(pl.program_id(0) == 0)
        def _():
            da_ref[...] = jnp.zeros_like(da_ref)
            dbias_ref[...] = jnp.zeros_like(dbias_ref)

        lane = lax.broadcasted_iota(jnp.int32, (CHUNK, HP), 1)
        pre = dt_ref[...] + bias_ref[0:1, :]
        dt = _softplus(pre)
        a = -jnp.exp(a_ref[0:1, :])
        ddt_x, dcs, dtot = jnp.zeros((CHUNK, HP), F32), jnp.zeros((CHUNK, HP), F32), jnp.zeros((8, HP), F32)
        for b in range(N_HB):
            s_ref, t_ref, g = (sf_ref, tf_ref, b) if b < SSM_GROUPS else (sb_ref, tb_ref, b - SSM_GROUPS)
            mine = (lane >= HG * b) & (lane < HG * (b + 1))

            def up(v):
                return pltpu.roll(v, HG * b, 1) if b else v

            ddt_x = ddt_x + jnp.where(mine, up(s_ref[g, :, 0:HP]), 0.0)
            dcs = dcs + jnp.where(mine, up(s_ref[g, :, HP:2 * HP]), 0.0)
            dtot = dtot + jnp.where(mine[0:8], up(t_ref[g, 0]), 0.0)
        tri_f = jnp.where(_tri(False, transpose=True), 1.0, 0.0).astype(F32)
        tri_b = jnp.where(_tri(True, transpose=True), 1.0, 0.0).astype(F32)
        dda = jnp.where(lane < SSM_HEADS, jnp.dot(tri_f, dcs, precision=HI, preferred_element_type=F32),
                        jnp.dot(tri_b, dcs, precision=HI, preferred_element_type=F32)) + dtot[0:1, :]
        dpre = (ddt_x + dda * a) * jax.nn.sigmoid(pre)
        ddt_ref[...] = jnp.where(lane < 2 * SSM_HEADS, dpre, 0.0)
        dbias_ref[...] += jnp.broadcast_to(jnp.sum(dpre, axis=0, keepdims=True), (8, HP))
        da_ref[...] += jnp.broadcast_to(jnp.sum(dda * dt, axis=0, keepdims=True) * a, (8, HP))

    vec = pl.BlockSpec((8, HP), lambda c: (0, 0))
    sel = pl.BlockSpec((SSM_GROUPS, CHUNK, 2 * HP), lambda c: (0, c, 0))
    tot = pl.BlockSpec((SSM_GROUPS, 1, 8, HP), lambda c: (0, c, 0, 0))
    tile = pl.BlockSpec((CHUNK, HP), lambda c: (c, 0))
    return pl.pallas_call(
        body, name="ssd_prep_bwd",
        out_shape=(jax.ShapeDtypeStruct((T, HP), F32), jax.ShapeDtypeStruct((8, HP), F32), jax.ShapeDtypeStruct((8, HP), F32)),
        grid=(nc,), in_specs=[pl.BlockSpec((CHUNK, HP), lambda c: (c, DT_BLK)), vec, vec, sel, tot, sel, tot],
        out_specs=(tile, vec, vec),
        compiler_params=pltpu.CompilerParams(dimension_semantics=("arbitrary",)),
    )(u, bias8, alog8, dsel_f, dtot_f, dsel_b, dtot_b)


def _ssm_combine_fwd(y_scans, act, u, dskip, gain):
    T = y_scans.shape[0]
    tt = _pick(T, 512)

    def body(ys_ref, x_ref, z_ref, ds_ref, g_ref, y_ref, m_ref):
        y = ys_ref[...] + ds_ref[...] * x_ref[...]
        y2 = y * _silu(z_ref[...])
        r = lax.rsqrt(jnp.mean(y2 * y2, axis=-1, keepdims=True) + EPS)
        y_ref[...] = y
        m_ref[...] = (y2 * r * g_ref[...]).astype(m_ref.dtype)

    blk = pl.BlockSpec((tt, GW), lambda i, g: (i, g))
    vec = pl.BlockSpec((1, GW), lambda i, g: (0, g))
    return pl.pallas_call(
        body, name="ssm_combine_fwd",
        out_shape=(jax.ShapeDtypeStruct((T, D_INNER), F32), jax.ShapeDtypeStruct((T, D_INNER), BF)),
        grid=(T // tt, SSM_GROUPS), in_specs=[blk, blk, blk, vec, vec], out_specs=(blk, blk),
    )(y_scans, act, u, dskip, gain)


def _ssm_combine_bwd(dm, y, act, u, gain):
    T = y.shape[0]
    tt = _pick(T, 512)

    def body(dm_ref, y_ref, x_ref, z_ref, g_ref, dy_ref, dz_ref, dg_ref, dsk_ref):
        z = z_ref[...]
        y = y_ref[...]
        x = x_ref[...]
        sz = _silu(z)
        y2 = y * sz
        r = lax.rsqrt(jnp.mean(y2 * y2, axis=-1, keepdims=True) + EPS)
        d = dm_ref[...]
        gd = d * g_ref[...]
        dy2 = r * gd - y2 * (r * r * r) * jnp.mean(gd * y2, axis=-1, keepdims=True)
        dy = dy2 * sz
        dy_ref[...] = dy
        dz_ref[...] = (dy2 * y * _dsilu(z)).astype(dz_ref.dtype)

        @pl.when(pl.program_id(1) == 0)
        def _():
            dg_ref[...] = jnp.zeros_like(dg_ref)
            dsk_ref[...] = jnp.zeros_like(dsk_ref)

        dg_ref[...] += jnp.broadcast_to(jnp.sum(d * y2 * r, axis=0, keepdims=True), dg_ref.shape)
        lane_sum = jnp.broadcast_to(jnp.sum(dy * x, axis=0, keepdims=True), (8, GW))
        src = lax.broadcasted_iota(jnp.int32, (GW, HP), 0)
        head = lax.broadcasted_iota(jnp.int32, (GW, HP), 1)
        to_head = jnp.where((src >= PH * head) & (src < PH * (head + 1)), 1.0, 0.0).astype(F32)
        dsk_ref[...] += jnp.dot(lane_sum, to_head, precision=HI, preferred_element_type=F32)

    blk = pl.BlockSpec((tt, GW), lambda g, i: (i, g))
    vec = pl.BlockSpec((1, GW), lambda g, i: (0, g))
    acc = pl.BlockSpec((8, GW), lambda g, i: (0, g))
    return pl.pallas_call(
        body, name="ssm_combine_bwd",
        out_shape=(jax.ShapeDtypeStruct((T, D_INNER), F32), jax.ShapeDtypeStruct((T, D_INNER), BF),
                   jax.ShapeDtypeStruct((8, D_INNER), F32), jax.ShapeDtypeStruct((8, SSM_GROUPS * HP), F32)),
        grid=(SSM_GROUPS, T // tt), in_specs=[blk, blk, blk, blk, vec],
        out_specs=(blk, blk, acc, pl.BlockSpec((8, HP), lambda g, i: (0, g))),
        compiler_params=pltpu.CompilerParams(dimension_semantics=("parallel", "arbitrary")),
    )(dm, y, act, u, gain)


def _loss_head(y, target):
    T, D = y.shape
    tt = _pick(T, 512)

    def body(y_ref, t_ref, dy_ref, dyb_ref, l_ref):
        e = y_ref[...] - t_ref[...]
        dy_ref[...] = e * (1.0 / D)
        dyb_ref[...] = (e * (1.0 / D)).astype(dyb_ref.dtype)

        @pl.when(pl.program_id(0) == 0)
        def _():
            l_ref[...] = jnp.zeros_like(l_ref)

        l_ref[...] += jnp.sum(e * e) * (0.5 / D)

    blk = pl.BlockSpec((tt, D), lambda i: (i, 0))
    return pl.pallas_call(
        body, name="loss_head",
        out_shape=(jax.ShapeDtypeStruct((T, D), F32), jax.ShapeDtypeStruct((T, D), BF), jax.ShapeDtypeStruct((8, 128), F32)),
        grid=(T // tt,), in_specs=[blk, blk], out_specs=(blk, blk, pl.BlockSpec((8, 128), lambda i: (0, 0))),
        compiler_params=pltpu.CompilerParams(dimension_semantics=("arbitrary",)),
    )(y, target)


def _adamw(w, g, m, v, *, name):
    R, C = w.shape
    cap = max(8, (1 << 18) // C)
    tr = R
    if R % 8 == 0:
        tr = 8
        for cand in range(8, min(R, cap) + 1, 8):
            if R % cand == 0:
                tr = cand

    def body(w_ref, g_ref, m_ref, v_ref, d_ref, nm_ref, nv_ref):
        gg = g_ref[...]
        nm = ADAM_B1 * m_ref[...] + (1.0 - ADAM_B1) * gg
        nv = ADAM_B2 * v_ref[...] + (1.0 - ADAM_B2) * jnp.square(gg)
        m_hat = nm / (1.0 - ADAM_B1 ** ADAM_STEP)
        v_hat = nv / (1.0 - ADAM_B2 ** ADAM_STEP)
        d_ref[...] = -ADAM_LR * (m_hat / (jnp.sqrt(v_hat) + ADAM_EPS) + ADAM_WD * w_ref[...])
        nm_ref[...] = nm
        nv_ref[...] = nv

    blk = pl.BlockSpec((tr, C), lambda i: (i, 0))
    return pl.pallas_call(
        body, name=name, out_shape=(jax.ShapeDtypeStruct((R, C), F32),) * 3, grid=(R // tr,),
        in_specs=[blk] * 4, out_specs=(blk,) * 3,
    )(w, g, m, v)


ANY = pl.BlockSpec(memory_space=pl.ANY)


def _chip_peers():
    x, y, c = lax.axis_index("x"), lax.axis_index("y"), lax.axis_index("c")
    return x, y, c, [(1 - x, y), (x, 1 - y), (1 - x, 1 - y)]


def _half_rows(c, rh):
    return pl.ds(pl.multiple_of(c * rh, 16), rh)


def _my_chip():
    return 2 * lax.axis_index("x") + lax.axis_index("y")


def _gather_chips(wb, wf):
    rh = wb.shape[0] // 2
    rq = rh // 2

    def body(wb_ref, wf_ref, ob_ref, of_ref, send_sems, recv_sems):
        x, y, c, peers = _chip_peers()
        nbr_x, nbr_y = peers[0], peers[1]
        me, chip_x, chip_y, chip_d = 2 * x + y, 2 * (1 - x) + y, 2 * x + (1 - y), 2 * (1 - x) + (1 - y)

        def quarter(core, b):
            return pl.ds(pl.multiple_of(core * rh + b * rq, 16), rq)

        ici = [(0, nbr_x, me, 0, chip_x), (1, nbr_y, me, 1, chip_y), (2, nbr_y, me, 0, chip_y), (3, nbr_x, me, 1, chip_x),
               (4, nbr_y, chip_x, 0, chip_d), (5, nbr_x, chip_y, 1, chip_d)]

        def ici_copy(k, to, slot, b, own):
            rows = quarter(c, b)
            return pltpu.make_async_remote_copy(
                src_ref=wb_ref.at[rows] if own else ob_ref.at[slot, rows], dst_ref=ob_ref.at[slot, rows],
                send_sem=send_sems.at[k], recv_sem=recv_sems.at[k], device_id=(to[0], to[1], c), device_id_type=MESH)

        def to_sibling(k, slot, b, core):
            rows = quarter(core, b)
            return pltpu.make_async_remote_copy(
                src_ref=ob_ref.at[slot, rows], dst_ref=ob_ref.at[slot, rows], send_sem=send_sems.at[6 + k],
                recv_sem=recv_sems.at[6 + k], device_id=(x, y, 1 - c), device_id_type=MESH)

        def small_copy(k, slot):
            px, py = peers[k]
            return pltpu.make_async_remote_copy(
                src_ref=wf_ref, dst_ref=of_ref.at[slot], send_sem=send_sems.at[12 + k], recv_sem=recv_sems.at[12 + k],
                device_id=(px, py, c), device_id_type=MESH)

        sends = [ici_copy(k, to, slot, b, True) for k, to, slot, b, _ in ici[:4]] + [small_copy(k, me) for k in range(3)]
        for cp in sends:
            cp.start()
        for k, to, slot, b, arrives in ici:
            ici_copy(k, to, arrives, b, False).wait_recv()
            passed = [to_sibling(k, arrives, b, c)]
            if k < 2:
                passed.append(ici_copy(*ici[4 + k][:4], False))
            for cp in passed:
                cp.start()
            sends += passed
        for k, to, slot, b, arrives in ici:
            to_sibling(k, arrives, b, 1 - c).wait_recv()
        chip_of = [chip_x, chip_y, chip_d]
        for k in range(3):
            small_copy(k, chip_of[k]).wait_recv()
        for cp in sends:
            cp.wait_send()

    ob, of = pl.pallas_call(
        body, name="gather_weights",
        out_shape=(jax.ShapeDtypeStruct((4,) + wb.shape, wb.dtype), jax.ShapeDtypeStruct((4,) + wf.shape, wf.dtype)),
        in_specs=[ANY, ANY], out_specs=(ANY, ANY),
        scratch_shapes=[pltpu.SemaphoreType.DMA((15,)), pltpu.SemaphoreType.DMA((15,))],
    )(wb, wf)
    me = _my_chip()
    return lax.dynamic_update_slice(ob, wb[None], (me, 0, 0)), lax.dynamic_update_slice(of, wf[None], (me, 0, 0))


def _halves_to_sibling(gp):
    rh = gp.shape[1] // 2

    def body(gp_ref, o_ref, send_sem, recv_sem):
        x, y, c = lax.axis_index("x"), lax.axis_index("y"), lax.axis_index("c")
        cp = pltpu.make_async_remote_copy(src_ref=gp_ref.at[:, _half_rows(1 - c, rh), :], dst_ref=o_ref, send_sem=send_sem,
                                          recv_sem=recv_sem, device_id=(x, y, 1 - c), device_id_type=MESH)
        cp.start()
        cp.wait()

    return pl.pallas_call(
        body, name="halves_to_sibling", out_shape=jax.ShapeDtypeStruct((gp.shape[0], rh, gp.shape[2]), gp.dtype),
        in_specs=[ANY], out_specs=ANY, scratch_shapes=[pltpu.SemaphoreType.DMA, pltpu.SemaphoreType.DMA],
    )(gp)


def _row_tile(rows, cap=1024):
    tr = 16
    for cand in range(16, cap + 1, 16):
        if rows % cand == 0:
            tr = cand
    return tr


def _add_halves(gp, sib, core):
    n, rh, C = sib.shape
    tr = _row_tile(rh)
    nt = rh // tr

    def body(c_ref, g_ref, s_ref, o_ref):
        o_ref[...] = (g_ref[...].astype(F32) + s_ref[...].astype(F32)).astype(o_ref.dtype)

    blk = pl.BlockSpec((1, tr, C), lambda j, i, c: (j, i, 0))
    return pl.pallas_call(
        body, name="add_halves", out_shape=jax.ShapeDtypeStruct(sib.shape, sib.dtype),
        grid_spec=pltpu.PrefetchScalarGridSpec(
            num_scalar_prefetch=1, grid=(n, nt),
            in_specs=[pl.BlockSpec((1, tr, C), lambda j, i, c: (j, c[0] * nt + i, 0)), blk], out_specs=blk),
    )(core, gp, sib)


def _join_halves(buf):
    rh = buf.shape[0] // 2

    def body(in_ref, o_ref, send_sem, recv_sem):
        x, y, c = lax.axis_index("x"), lax.axis_index("y"), lax.axis_index("c")

        def copy(rows):
            return pltpu.make_async_remote_copy(src_ref=o_ref.at[rows], dst_ref=o_ref.at[rows], send_sem=send_sem,
                                                recv_sem=recv_sem, device_id=(x, y, 1 - c), device_id_type=MESH)

        send = copy(_half_rows(c, rh))
        send.start()
        copy(_half_rows(1 - c, rh)).wait_recv()
        send.wait_send()

    return pl.pallas_call(
        body, name="join_halves", out_shape=jax.ShapeDtypeStruct(buf.shape, buf.dtype),
        in_specs=[ANY], out_specs=ANY, input_output_aliases={0: 0},
        scratch_shapes=[pltpu.SemaphoreType.DMA, pltpu.SemaphoreType.DMA],
    )(buf)


def _exchange_near(gp):
    rq = gp.shape[1] // 2

    def body(gp_ref, out_ref, send_sems, recv_sems):
        x, y, c, peers = _chip_peers()
        chip_x, chip_y, chip_d = 2 * (1 - x) + y, 2 * x + (1 - y), 2 * (1 - x) + (1 - y)
        plan = [(peers[0], chip_x, 0), (peers[0], chip_d, 0), (peers[1], chip_y, 1), (peers[1], chip_d, 1)]
        copies = [pltpu.make_async_remote_copy(
            src_ref=gp_ref.at[slot, pl.ds(b * rq, rq)], dst_ref=out_ref.at[k], send_sem=send_sems.at[k],
            recv_sem=recv_sems.at[k], device_id=(to[0], to[1], c), device_id_type=MESH) for k, (to, slot, b) in enumerate(plan)]
        for cp in copies:
            cp.start()
        for cp in copies:
            cp.wait_recv()
        for cp in copies:
            cp.wait_send()

    return pl.pallas_call(
        body, name="exchange_grads_near", out_shape=jax.ShapeDtypeStruct((4, rq, gp.shape[2]), gp.dtype),
        in_specs=[ANY], out_specs=ANY, scratch_shapes=[pltpu.SemaphoreType.DMA((4,)), pltpu.SemaphoreType.DMA((4,))],
    )(gp)


def _add_near(gp, near, chips):
    _, rq, C = near.shape
    tr = _row_tile(rq)
    nt = rq // tr

    def body(ch_ref, mine_a, mine_b, on_a, on_b, near_ref, part_ref, on_ref):
        part_ref[0] = mine_a[0].astype(F32) + near_ref[0].astype(F32)
        part_ref[1] = mine_b[0].astype(F32) + near_ref[2].astype(F32)
        on_ref[0] = (on_a[0].astype(F32) + near_ref[1].astype(F32)).astype(on_ref.dtype)
        on_ref[1] = (on_b[0].astype(F32) + near_ref[3].astype(F32)).astype(on_ref.dtype)

    def slot(which, b):
        return pl.BlockSpec((1, tr, C), lambda i, ch: (ch[which], b * nt + i, 0))

    return pl.pallas_call(
        body, name="add_near",
        out_shape=(jax.ShapeDtypeStruct((2, rq, C), F32), jax.ShapeDtypeStruct((2, rq, C), near.dtype)),
        grid_spec=pltpu.PrefetchScalarGridSpec(
            num_scalar_prefetch=1, grid=(nt,),
            in_specs=[slot(0, 0), slot(0, 1), slot(2, 0), slot(1, 1), pl.BlockSpec((4, tr, C), lambda i, ch: (0, i, 0))],
            out_specs=(pl.BlockSpec((2, tr, C), lambda i, ch: (0, i, 0)),) * 2),
    )(chips, gp, gp, gp, gp, near)


def _exchange_far(on):
    def body(on_ref, out_ref, send_sems, recv_sems):
        x, y, c, peers = _chip_peers()
        copies = [pltpu.make_async_remote_copy(
            src_ref=on_ref.at[k], dst_ref=out_ref.at[k], send_sem=send_sems.at[k], recv_sem=recv_sems.at[k],
            device_id=(to[0], to[1], c), device_id_type=MESH) for k, to in enumerate((peers[1], peers[0]))]
        for cp in copies:
            cp.start()
        for cp in copies:
            cp.wait_recv()
        for cp in copies:
            cp.wait_send()

    return pl.pallas_call(
        body, name="exchange_grads_far", out_shape=jax.ShapeDtypeStruct(on.shape, on.dtype),
        in_specs=[ANY], out_specs=ANY, scratch_shapes=[pltpu.SemaphoreType.DMA((2,)), pltpu.SemaphoreType.DMA((2,))],
    )(on)


def _add_far(part, far, core):
    _, rq, C = part.shape
    tr = _row_tile(rq)
    nt = rq // tr

    def body(c_ref, p_ref, f_ref, o_ref):
        o_ref[...] = p_ref[0] + f_ref[0].astype(F32)

    blk = pl.BlockSpec((1, tr, C), lambda b, i, c: (b, i, 0))
    return pl.pallas_call(
        body, name="add_far", out_shape=jax.ShapeDtypeStruct((4 * rq, C), F32),
        grid_spec=pltpu.PrefetchScalarGridSpec(
            num_scalar_prefetch=1, grid=(2, nt), in_specs=[blk, blk],
            out_specs=pl.BlockSpec((tr, C), lambda b, i, c: ((2 * c[0] + b) * nt + i, 0))),
    )(core, part, far)


N_DEV = 8


def _allreduce_small(p):
    rs = p.shape[0]

    def body(x_ref, sum_ref, all_ref, send_sems, recv_sems, local_sem):
        x, y, c = lax.axis_index("x"), lax.axis_index("y"), lax.axis_index("c")
        me, sibling = (x, y, c), (x, y, 1 - c)
        chips = [(1 - x, y), (x, 1 - y), (1 - x, 1 - y)]

        def rows(px, py, pc):
            return all_ref.at[pl.ds((4 * px + 2 * py + pc) * rs, rs), :]

        def copy(k, block, to, src=None):
            return pltpu.make_async_remote_copy(
                src_ref=rows(*block) if src is None else src, dst_ref=rows(*block),
                send_sem=send_sems.at[k], recv_sem=recv_sems.at[k], device_id=to, device_id_type=MESH)

        mine = pltpu.make_async_copy(x_ref, rows(*me), local_sem)
        mine.start()
        first = [copy(0, me, sibling, src=x_ref)]
        first += [copy(1 + j, me, (*chip, c), src=x_ref) for j, chip in enumerate(chips)]
        for cp in first:
            cp.start()
        passed = [copy(4 + j, (*chip, c), sibling) for j, chip in enumerate(chips)]
        for j, chip in enumerate(chips):
            copy(1 + j, (*chip, c), me).wait_recv()
            passed[j].start()
        copy(0, sibling, me).wait_recv()
        for j, chip in enumerate(chips):
            copy(4 + j, (*chip, 1 - c), me).wait_recv()
        for cp in first + passed:
            cp.wait_send()
        mine.wait()
        acc = all_ref[0:rs, :]
        for d in range(1, N_DEV):
            acc = acc + all_ref[d * rs:(d + 1) * rs, :]
        sum_ref[...] = acc

    vmem = pl.BlockSpec(memory_space=pltpu.VMEM)
    return pl.pallas_call(
        body, name="allreduce_small", out_shape=jax.ShapeDtypeStruct((rs, 128), F32),
        in_specs=[vmem], out_specs=vmem,
        scratch_shapes=[pltpu.VMEM((N_DEV * rs, 128), F32), pltpu.SemaphoreType.DMA((7,)), pltpu.SemaphoreType.DMA((7,)),
                        pltpu.SemaphoreType.DMA],
    )(p)


WEIGHTS = ('ffn1_norm', 'ffn1_w_gate', 'ffn1_w_up', 'ffn1_w_down', 'mix_norm', 'w_in', 'q_a_norm', 'w_q_b',
           'kv_a_norm', 'w_kv_b', 'q_head_norm', 'k_head_norm', 'conv_w', 'conv_b', 'a_log_fwd', 'a_log_bwd',
           'dt_bias_fwd', 'dt_bias_bwd', 'd_skip', 'ssm_norm', 'w_attn_branch', 'w_ssm_branch', 'w_out',
           'ffn2_norm', 'ffn2_w_gate', 'ffn2_w_up', 'ffn2_w_down')
PACKED = (('ffn1_w_gate', (D_MODEL, D_FF), 1), ('ffn1_w_up', (D_MODEL, D_FF), 1), ('ffn1_w_down', (D_FF, D_MODEL), 0),
          ('w_in', (D_MODEL, sum(IN_SPLITS)), 1), ('w_q_b', (Q_LORA, N_HEADS * QK_HEAD), 1),
          ('w_kv_b', (KV_LORA, N_HEADS * (QK_NOPE + V_HEAD)), 1),
          ('w_attn_branch', (N_HEADS * V_HEAD, D_MODEL), 0), ('w_ssm_branch', (D_INNER, D_MODEL), 0),
          ('w_out', (D_MODEL, D_MODEL), 0),
          ('ffn2_w_gate', (D_MODEL, D_FF), 1), ('ffn2_w_up', (D_MODEL, D_FF), 1), ('ffn2_w_down', (D_FF, D_MODEL), 0))
PACK_W = 1024
N_CHIPS = 4
SMALL = (('ffn1_norm', 1024), ('mix_norm', 1024), ('q_a_norm', 384), ('kv_a_norm', 256), ('q_head_norm', 96),
         ('k_head_norm', 96), ('conv_b', 3072), ('a_log_fwd', 32), ('a_log_bwd', 32), ('dt_bias_fwd', 32),
         ('dt_bias_bwd', 32), ('d_skip', 32), ('ssm_norm', 2048), ('ffn2_norm', 1024),
         ('conv_w', CONV_WIDTH * XBC_DIM), ('loss', 1))


TRANSPOSED = ('ffn1_w_gate', 'ffn1_w_up', 'w_in', 'ffn2_w_gate', 'ffn2_w_up')


def _stored(name, a):
    return a.T if name in TRANSPOSED else a


def _shard_shape(name, shape, axis):
    sh = tuple(s // N_CHIPS if a == axis else s for a, s in enumerate(shape))
    return sh[::-1] if name in TRANSPOSED else sh


def _by_rows(name, axis):
    return name in TRANSPOSED or axis == 0


def _pack_layout():
    out, r = {}, 0
    for name, shape, axis in PACKED:
        n = math.prod(shape) // N_CHIPS // PACK_W
        out[name] = (r, n)
        r += n
    return out, -(-r // 64) * 64


def _pack(shards):
    layout, rows = _pack_layout()
    parts = [shards[name].reshape(-1, PACK_W) for name, _, _ in PACKED]
    parts.append(jnp.zeros((rows - sum(p.shape[0] for p in parts), PACK_W), parts[0].dtype))
    return jnp.concatenate(parts, axis=0)


def _unpack(packed):
    layout, _ = _pack_layout()
    return {name: packed[layout[name][0]:layout[name][0] + layout[name][1]].reshape(_shard_shape(name, shape, axis))
            for name, shape, axis in PACKED}


def _full_from_slots(slots):
    layout, _ = _pack_layout()
    out = {}
    for name, shape, axis in PACKED:
        r, n = layout[name]
        if _by_rows(name, axis):
            out[name] = slots[:, r:r + n].reshape(N_CHIPS * n, PACK_W)
        else:
            sh = _shard_shape(name, shape, axis)
            out[name] = jnp.concatenate([slots[j, r:r + n].reshape(sh) for j in range(N_CHIPS)], axis=axis)
    return out


def _slots_from_full(full):
    layout, rows = _pack_layout()
    parts = []
    for name, shape, axis in PACKED:
        r, n = layout[name]
        if _by_rows(name, axis):
            parts.append(full[name].reshape(N_CHIPS, n, PACK_W))
        else:
            size = shape[axis] // N_CHIPS
            parts.append(jnp.stack([lax.slice_in_dim(full[name], j * size, (j + 1) * size, axis=axis).reshape(n, PACK_W)
                                    for j in range(N_CHIPS)]))
    parts.append(jnp.zeros((N_CHIPS, rows - sum(p.shape[1] for p in parts), PACK_W), parts[0].dtype))
    return jnp.concatenate(parts, axis=1)


def _pack_small(vals):
    parts = []
    for name, n in SMALL:
        pad = -(-n // 128) * 128 - n
        parts.append(jnp.pad(vals[name].reshape(-1).astype(F32), (0, pad)).reshape(-1, 128))
    rows = sum(p.shape[0] for p in parts)
    parts.append(jnp.zeros((-(-rows // 8) * 8 - rows, 128), F32))
    return jnp.concatenate(parts, axis=0)


def _unpack_small(packed):
    out, r = {}, 0
    for name, n in SMALL:
        k = -(-n // 128)
        out[name] = packed[r:r + k].reshape(-1)[:n]
        r += k
    return out


def _pad_heads(w, axis, per_head, lo, hi):
    shape = w.shape
    w = w.reshape(shape[:axis] + (N_HEADS, per_head) + shape[axis + 1:])
    w = lax.slice_in_dim(w, lo, hi, axis=axis + 1)
    pad = [(0, 0)] * w.ndim
    pad[axis + 1] = (0, HP - (hi - lo))
    w = jnp.pad(w, pad)
    return w.reshape(shape[:axis] + (N_HEADS * HP,) + shape[axis + 1:])


def _unpad_heads(w, axis, keep):
    shape = w.shape
    w = w.reshape(shape[:axis] + (N_HEADS, HP) + shape[axis + 1:])
    return lax.slice_in_dim(w, 0, keep, axis=axis + 1)


def _pad_w_in(wt):
    o = [0]
    for s in IN_SPLITS:
        o.append(o[-1] + s)
    cq, ckv, kpe, z, xbc, dtf, dtb, ga, gb = [wt[o[i]:o[i + 1]] for i in range(len(IN_SPLITS))]
    kpe_pad = jnp.pad(kpe, ((QK_NOPE, HP - QK_HEAD), (0, 0)))
    dt_pad = jnp.pad(jnp.concatenate([dtf, dtb], axis=0), ((0, HP - 2 * SSM_HEADS), (0, 0)))
    return jnp.concatenate([z, ga, gb, xbc, cq, ckv, kpe_pad, dt_pad], axis=0)


def _unpad_w_in(gt):
    z, ga, gb, xbc = gt[U_Z:U_GA], gt[U_GA:U_GB], gt[U_GB:U_XBC], gt[U_XBC:U_SMALL]
    s = gt[U_SMALL:]
    cq, ckv = s[S_CQ:S_CKV], s[S_CKV:S_KPE]
    kpe = s[S_KPE + QK_NOPE:S_KPE + QK_HEAD]
    dtf, dtb = s[S_DT:S_DT + SSM_HEADS], s[S_DT + SSM_HEADS:S_DT + 2 * SSM_HEADS]
    return jnp.concatenate([cq, ckv, kpe, z, xbc, dtf, dtb, ga, gb], axis=0)


def _lanes128(parts):
    row = jnp.concatenate([p.reshape(-1) for p in parts])
    return jnp.pad(row, (0, HP - row.shape[0])).reshape(1, HP)


FF_TILE = D_FF // 2
WGRAD = BF


def _ffn_fwd(x, g, wg_t, wu_t, wd, tag):
    h = _rms_fwd(x, g, name=tag + "_norm")
    gate, up, act = _mm([h], [wg_t, wu_t], name=tag + "_up", tb=True, out_dtypes=(BF, BF, BF), tm=512, tn=FF_TILE,
                        epilogue=lambda a, b: (a, b, _silu(a) * b))
    out = _mm([act], [wd], name=tag + "_down", extras=[x], epilogue=lambda acc, r: (r + 0.5 * acc,))
    return out, (h, gate, up, act)


def _ffn_bwd(dout, dout_bf, x, g, wg_t, wu_t, wd, saved, tag):
    h, gate, up, act = saved

    def swiglu_bwd(acc, a, b):
        a, b, half = a.astype(F32), b.astype(F32), 0.5 * acc
        s = jax.nn.sigmoid(a)
        return half * b * (s * (1.0 + a * (1.0 - s))), half * (a * s)

    dgate, dup = _mm([dout_bf], [wd], name=tag + "_down_dx", tb=True, extras=[gate, up], out_dtypes=(BF, BF),
                     tm=512, tn=FF_TILE, epilogue=swiglu_bwd)
    dwd = _mm([act], [dout_bf], name=tag + "_down_dw", ta=True, tm=FF_TILE, out_dtypes=(WGRAD,),
              epilogue=lambda acc: (0.5 * acc,))
    dwg_t, dwu_t = _mm([dgate, dup], [h, h], name=tag + "_up_dw", ta=True, separate=True, out_dtypes=(WGRAD, WGRAD),
                       tm=FF_TILE)
    dh = _mm([dgate, dup], [wg_t, wu_t], name=tag + "_up_dx")
    dx, dx_bf, dg = _rms_bwd(dh, x, g, name=tag + "_norm_bwd", add=dout, out_dtypes=(F32, BF))
    return dx, dx_bf, dg, dwg_t, dwu_t, dwd


KPE_BLK = (U_SMALL + S_KPE) // HP
SMALL_BLK = U_SMALL // SMALL_W


def _local_step(x, pos_col, target, W, P):
    T = x.shape[0]
    sig = jax.nn.sigmoid
    x1, ffn1 = _ffn_fwd(x, P["ffn1_norm"], W["wg1"], W["wu1"], W["wd1"], "ffn1")
    h = _rms_fwd(x1, P["mix_norm"], name="mix_norm")
    u = _mm([h], [W["w_in"]], name="in_proj", tb=True, tn=1152)
    cqn = _rms_fwd(u, P["q_a_norm"], name="q_a_norm", blk_w=SMALL_W, blk_idx=SMALL_BLK, off=S_CQ, width=Q_LORA)
    ckvn = _rms_fwd(u, P["kv_a_norm"], name="kv_a_norm", blk_w=SMALL_W, blk_idx=SMALL_BLK, off=S_CKV, width=KV_LORA)
    q_raw = _mm([cqn], [W["wq"]], name="q_proj")
    def with_ones_lane(acc_k, acc_v):
        lane = lax.broadcasted_iota(jnp.int32, acc_v.shape, 1)
        return acc_k, jnp.where((lane & (HP - 1)) == V_HEAD, 1.0, acc_v)

    k_raw, v = _mm([ckvn], [W["wk"], W["wv"]], name="kv_proj", out_dtypes=(F32, BF), epilogue=with_ones_lane)
    rc, rs = _rope_tables(pos_col, P["freq"])
    q = _qk_prep_fwd(q_raw, None, P["q_head_norm"], rc, rs, name="q_prep", out_scale=Q_SCALE)
    k = _qk_prep_fwd(k_raw, u, P["k_head_norm"], rc, rs, name="k_prep", kpe_blk=KPE_BLK)
    o, o_bf, lse = _attn_fwd(q, k, v)
    pre, act = _conv_fwd(u, P["conv_w8"], P["conv_b"])
    scan_cols, scan_rows = _ssd_prep(u, P["dt_bias8"], P["a_log8"])
    y_f, st_f = _ssd_fwd(act, scan_cols, scan_rows, rev=False, name="ssd_fwd_f")
    y_fb, st_b = _ssd_fwd(act, scan_cols, scan_rows, rev=True, name="ssd_fwd_b", add=y_f)
    ysum, m = _ssm_combine_fwd(y_fb, act, u, P["d_skip_lanes"], P["ssm_norm"])
    ab, mb, merged = _mm([o_bf, m], [W["pa"], W["pb"]], name="branches", separate=True, extras=[u, u],
                         extra_offs=(U_GA, U_GB), out_dtypes=(F32, F32, BF),
                         epilogue=lambda a, b, ga, gb: (a, b, sig(ga) * a + sig(gb) * b))
    x2 = _mm([merged], [W["wo"]], name="out_proj", extras=[x1], epilogue=lambda acc, r: (r + acc,))
    y, ffn2 = _ffn_fwd(x2, P["ffn2_norm"], W["wg2"], W["wu2"], W["wd2"], "ffn2")
    dy, dy_bf, loss = _loss_head(y, target)
    dx2, dx2_bf, dg_ffn2, dwg2, dwu2, dwd2 = _ffn_bwd(dy, dy_bf, x2, P["ffn2_norm"], W["wg2"], W["wu2"], W["wd2"], ffn2,
                                                      "ffn2")

    def gate_bwd(dmrg, a, b, ga, gb):
        sa, sb = sig(ga), sig(gb)
        return dmrg * sa, dmrg * sb, dmrg * a * sa * (1.0 - sa), dmrg * b * sb * (1.0 - sb)

    dab, dmb, dga, dgb = _mm([dx2_bf], [W["wo"]], name="out_proj_dx", tb=True, extras=[ab, mb, u, u],
                             extra_offs=(0, 0, U_GA, U_GB), out_dtypes=(BF,) * 4, epilogue=gate_bwd)
    dwo = _mm([merged], [dx2_bf], name="out_proj_dw", ta=True, out_dtypes=(WGRAD,))
    dpa, dpb = _mm([o_bf, m], [dab, dmb], name="branches_dw", ta=True, separate=True, out_dtypes=(WGRAD, WGRAD))
    do, dm = _mm([dab, dmb], [W["pa"], W["pb"]], name="branches_dx", tb=True, separate=True, out_dtypes=(F32, F32))
    dyssd, dz, dg_ssm, dskip = _ssm_combine_bwd(dm, ysum, act, u, P["ssm_norm"])
    dxs_f, db_f, dc_f, dsel_f, dtot_f = _ssd_bwd(act, scan_cols, scan_rows, st_f, dyssd, rev=False, name="ssd_bwd_f",
                                                 skip=P["d_skip_lanes"])
    dxs, db, dc, dsel_b, dtot_b = _ssd_bwd(act, scan_cols, scan_rows, st_b, dyssd, rev=True, name="ssd_bwd_b",
                                           add=(dxs_f, db_f, dc_f))
    ddt, dalog, dbias = _ssd_prep_bwd(u, P["dt_bias8"], P["a_log8"], dsel_f, dtot_f, dsel_b, dtot_b)
    dxbc, dconv = [], []
    for tag, col0, part in (("x", 0, dxs), ("b", D_INNER, db), ("c", D_INNER + SSM_GROUPS * D_STATE, dc)):
        dxp, dwp = _conv_bwd([part], pre, u, P["conv_w8"], col0, name="conv_bwd_" + tag)
        dxbc.append(dxp)
        dconv.append(dwp)
    dconv = jnp.concatenate(dconv, axis=1)
    dq, dk, dv = _attn_bwd(q, k, v, do, o, lse)
    dq_raw, dg_qh = _qk_prep_bwd(dq, q_raw, None, P["q_head_norm"], rc, rs, name="q_prep_bwd", in_scale=ATTN_SCALE)
    dk_raw, dg_kh, dkpe = _qk_prep_bwd(dk, k_raw, u, P["k_head_norm"], rc, rs, name="k_prep_bwd", kpe_blk=KPE_BLK,
                                       in_scale=1.0 / LOG2E)
    dwq = _mm([cqn], [dq_raw], name="q_proj_dw", ta=True, out_dtypes=(WGRAD,))
    dcqn = _mm([dq_raw], [W["wq"]], name="q_proj_dx", tb=True)
    dwk, dwv = _mm([ckvn], [dk_raw, dv], name="kv_proj_dw", ta=True, out_dtypes=(WGRAD, WGRAD))
    dckvn = _mm([dk_raw, dv], [W["wk"], W["wv"]], name="kv_proj_dx", tb=True)
    dcq, dg_qa = _rms_bwd(dcqn, u, P["q_a_norm"], name="q_a_norm_bwd", blk_w=SMALL_W, blk_idx=SMALL_BLK, off=S_CQ,
                          width=Q_LORA, out_dtypes=(BF,))
    dckv, dg_kva = _rms_bwd(dckvn, u, P["kv_a_norm"], name="kv_a_norm_bwd", blk_w=SMALL_W, blk_idx=SMALL_BLK,
                            off=S_CKV, width=KV_LORA, out_dtypes=(BF,))
    du = jnp.concatenate([dz, dga, dgb] + dxbc + [dcq, dckv, dkpe.astype(BF), ddt.astype(BF)], axis=1)
    dw_in = _mm([du], [h], name="in_proj_dw", ta=True, tm=1152, out_dtypes=(WGRAD,))
    dh = _mm([du], [W["w_in"]], name="in_proj_dx")
    dx1, dx1_bf, dg_mix = _rms_bwd(dh, x1, P["mix_norm"], name="mix_norm_bwd", add=dx2, out_dtypes=(F32, BF))
    dx, _, dg_ffn1, dwg1, dwu1, dwd1 = _ffn_bwd(dx1, dx1_bf, x, P["ffn1_norm"], W["wg1"], W["wu1"], W["wd1"], ffn1, "ffn1")
    dW = dict(wg1=dwg1, wu1=dwu1, wd1=dwd1, w_in=dw_in, wq=dwq, wk=dwk, wv=dwv, pa=dpa, pb=dpb, wo=dwo,
              wg2=dwg2, wu2=dwu2, wd2=dwd2)
    dP = dict(ffn1_norm=dg_ffn1[0], mix_norm=dg_mix[0], q_a_norm=dg_qa[0], kv_a_norm=dg_kva[0],
              q_head_norm=dg_qh[0, :QK_HEAD], k_head_norm=dg_kh[0, :QK_HEAD], conv_b=dconv[CONV_WIDTH],
              a_log_fwd=dalog[0, :SSM_HEADS], a_log_bwd=dalog[0, SSM_HEADS:2 * SSM_HEADS],
              dt_bias_fwd=dbias[0, :SSM_HEADS], dt_bias_bwd=dbias[0, SSM_HEADS:2 * SSM_HEADS],
              d_skip=dskip[0].reshape(SSM_GROUPS, HP)[:, :HG], ssm_norm=dg_ssm[0], ffn2_norm=dg_ffn2[0],
              conv_w=dconv[:CONV_WIDTH], loss=loss[0, 0])
    return dx, dW, dP


def _prepare(w, conv_w_full):
    kvb = w["w_kv_b"]
    W = dict(wg1=w["ffn1_w_gate"], wu1=w["ffn1_w_up"], wd1=w["ffn1_w_down"], w_in=_pad_w_in(w["w_in"]),
             wq=_pad_heads(w["w_q_b"], 1, QK_HEAD, 0, QK_HEAD),
             wk=_pad_heads(kvb, 1, QK_NOPE + V_HEAD, 0, QK_NOPE),
             wv=_pad_heads(kvb, 1, QK_NOPE + V_HEAD, QK_NOPE, QK_NOPE + V_HEAD),
             pa=_pad_heads(w["w_attn_branch"], 0, V_HEAD, 0, V_HEAD), pb=w["w_ssm_branch"], wo=w["w_out"],
             wg2=w["ffn2_w_gate"], wu2=w["ffn2_w_up"], wd2=w["ffn2_w_down"])
    inv_freq = [1.0 / (ROPE_BASE ** (j / QK_ROPE)) for j in range(0, QK_ROPE, 2)]
    freq = [0.0] * QK_NOPE + inv_freq + inv_freq + [0.0] * (HP - QK_HEAD)
    P = {n: w[n] for n in ("ffn1_norm", "mix_norm", "q_a_norm", "kv_a_norm", "ssm_norm", "ffn2_norm", "conv_b")}
    P.update(q_head_norm=_lanes128([w["q_head_norm"]]), k_head_norm=_lanes128([w["k_head_norm"]]),
             conv_w8=jnp.pad(conv_w_full, ((0, 8 - CONV_WIDTH), (0, 0))),
             dt_bias8=jnp.broadcast_to(_lanes128([w["dt_bias_fwd"], w["dt_bias_bwd"]]), (8, HP)),
             a_log8=jnp.broadcast_to(_lanes128([w["a_log_fwd"], w["a_log_bwd"]]), (8, HP)),
             d_skip_lanes=jnp.repeat(w["d_skip"].reshape(-1), PH).reshape(1, D_INNER),
             freq=jnp.asarray(freq, F32).reshape(1, HP))
    return W, P


def _unprepare(dW):
    dkvb = jnp.concatenate([_unpad_heads(dW["wk"], 1, QK_NOPE), _unpad_heads(dW["wv"], 1, V_HEAD)], axis=2)
    return dict(ffn1_w_gate=dW["wg1"], ffn1_w_up=dW["wu1"], ffn1_w_down=dW["wd1"], w_in=_unpad_w_in(dW["w_in"]),
                w_q_b=_unpad_heads(dW["wq"], 1, QK_HEAD).reshape(Q_LORA, N_HEADS * QK_HEAD),
                w_kv_b=dkvb.reshape(KV_LORA, N_HEADS * (QK_NOPE + V_HEAD)),
                w_attn_branch=_unpad_heads(dW["pa"], 0, V_HEAD).reshape(N_HEADS * V_HEAD, D_MODEL),
                w_ssm_branch=dW["pb"], w_out=dW["wo"],
                ffn2_w_gate=dW["wg2"], ffn2_w_up=dW["wu2"], ffn2_w_down=dW["wd2"])


def kernel(x, positions, ffn1_norm, ffn1_w_gate, ffn1_w_up, ffn1_w_down, mix_norm, w_in, q_a_norm, w_q_b, kv_a_norm, w_kv_b, q_head_norm, k_head_norm, conv_w, conv_b, a_log_fwd, a_log_bwd, dt_bias_fwd, dt_bias_bwd, d_skip, ssm_norm, w_attn_branch, w_ssm_branch, w_out, ffn2_norm, ffn2_w_gate, ffn2_w_up, ffn2_w_down, loss_target, m_ffn1_norm, m_ffn1_w_gate, m_ffn1_w_up, m_ffn1_w_down, m_mix_norm, m_w_in, m_q_a_norm, m_w_q_b, m_kv_a_norm, m_w_kv_b, m_q_head_norm, m_k_head_norm, m_conv_w, m_conv_b, m_a_log_fwd, m_a_log_bwd, m_dt_bias_fwd, m_dt_bias_bwd, m_d_skip, m_ssm_norm, m_w_attn_branch, m_w_ssm_branch, m_w_out, m_ffn2_norm, m_ffn2_w_gate, m_ffn2_w_up, m_ffn2_w_down, v_ffn1_norm, v_ffn1_w_gate, v_ffn1_w_up, v_ffn1_w_down, v_mix_norm, v_w_in, v_q_a_norm, v_w_q_b, v_kv_a_norm, v_w_kv_b, v_q_head_norm, v_k_head_norm, v_conv_w, v_conv_b, v_a_log_fwd, v_a_log_bwd, v_dt_bias_fwd, v_dt_bias_bwd, v_d_skip, v_ssm_norm, v_w_attn_branch, v_w_ssm_branch, v_w_out, v_ffn2_norm, v_ffn2_w_gate, v_ffn2_w_up, v_ffn2_w_down):
    given = dict(locals())
    T = x.shape[1]
    packed_names = [name for name, _, _ in PACKED]

    def two_d(a):
        return a.reshape(a.shape[1], -1) if a.ndim > 2 else a

    def kept(n, a):
        return _stored(n, two_d(a))

    w_loc = {n: kept(n, given[n]) for n in WEIGHTS}
    wb = _pack({n: w_loc[n].astype(BF) for n in packed_names})
    wf = jnp.pad(w_loc["conv_w"], ((0, 8 - CONV_WIDTH), (0, 0)))
    gb, gf = _gather_chips(wb, wf)
    full = _full_from_slots(gb)
    conv_w_full = jnp.concatenate([gf[j, :CONV_WIDTH] for j in range(N_CHIPS)], axis=1)
    full.update({n: w_loc[n] for n in WEIGHTS if n not in full and n != "conv_w"})
    W, P = _prepare(full, conv_w_full)
    dx, dW, dP = _local_step(x.reshape(T, D_MODEL), positions.reshape(T, 1).astype(F32), loss_target.reshape(T, D_MODEL), W, P)
    gp = _slots_from_full(_unprepare(dW))
    core = lax.axis_index("c").astype(jnp.int32).reshape(1)
    both_cores = _add_halves(gp, _halves_to_sibling(gp), core)
    cx, cy = lax.axis_index("x"), lax.axis_index("y")
    chips = jnp.stack([2 * cx + cy, 2 * (1 - cx) + cy, 2 * cx + (1 - cy)]).astype(jnp.int32)
    part, on = _add_near(both_cores, _exchange_near(both_cores), chips)
    grads = _unpack(_join_halves(_add_far(part, _exchange_far(on), core)))
    small = _unpack_small(_allreduce_small(_pack_small(dP)))
    grads.update({n: small[n].reshape(1, -1) for n, _ in SMALL if n not in ("conv_w", "loss")})
    grads["conv_w"] = lax.dynamic_slice_in_dim(small["conv_w"].reshape(CONV_WIDTH, XBC_DIM), _my_chip() * (XBC_DIM // N_CHIPS),
                                               XBC_DIM // N_CHIPS, axis=1)
    out_g, out_d, out_m, out_v = [], [], [], []
    for n in WEIGHTS:
        shape = given[n].shape
        delta, new_m, new_v = _adamw(w_loc[n], grads[n], kept(n, given["m_" + n]), kept(n, given["v_" + n]), name="adamw_" + n)
        for outs, a in ((out_g, grads[n]), (out_d, delta), (out_m, new_m), (out_v, new_v)):
            outs.append(_stored(n, a).reshape(shape))
    return (small["loss"].reshape(()), dx.reshape(x.shape), *out_g, *out_d, *out_m, *out_v)
```

```python
import functools
import math

import jax
import jax.numpy as jnp
from jax import lax
from jax.experimental import pallas as pl
from jax.experimental.pallas import tpu as pltpu

BF = jnp.bfloat16
F32 = jnp.float32
HI = lax.Precision.HIGHEST
MESH = pl.DeviceIdType.MESH

D_MODEL = 1024
D_FF = 2816
EPS = 1e-6
N_HEADS = 16
QK_NOPE = 64
QK_ROPE = 32
QK_HEAD = 96
V_HEAD = 64
Q_LORA = 384
KV_LORA = 256
ROPE_BASE = 10000.0
D_INNER = 2048
SSM_HEADS = 32
SSM_GROUPS = 4
D_STATE = 128
CONV_WIDTH = 5
CHUNK = 128
XBC_DIM = 3072
HP = 128
GW = D_INNER // SSM_GROUPS
HG = SSM_HEADS // SSM_GROUPS
PH = 64
U_Z, U_GA, U_GB, U_XBC, U_SMALL = 0, 2048, 3072, 4096, 7168
S_CQ, S_CKV, S_KPE, S_DT, SMALL_W = 0, 384, 640, 768, 896
U_PAD = U_SMALL + SMALL_W
IN_SPLITS = (Q_LORA, KV_LORA, QK_ROPE, D_INNER, XBC_DIM, SSM_HEADS, SSM_HEADS, D_MODEL, D_MODEL)

ADAM_LR = 0.001
ADAM_B1 = 0.9
ADAM_B2 = 0.999
ADAM_EPS = 1e-08
ADAM_WD = 0.01
ADAM_STEP = 10

V7X_VMEM_BYTES = 64 << 20
MM_VMEM_BUDGET = V7X_VMEM_BYTES * 13 // 16

NT = (((1,), (1,)), ((), ()))
TN = (((0,), (0,)), ((), ()))


def _pick(n, pref):
    best = None
    d = 128
    while d <= min(n, pref):
        if n % d == 0:
            best = d
        d += 128
    return best if best is not None else n


def _silu(x):
    return x * jax.nn.sigmoid(x)


def _dsilu(x):
    s = jax.nn.sigmoid(x)
    return s * (1.0 + x * (1.0 - s))


def _softplus(x):
    return jnp.maximum(x, 0.0) + jnp.log(1.0 + jnp.exp(-jnp.abs(x)))


def _mm(As, Bs, *, name, ta=False, tb=False, out_dtypes=(F32,), epilogue=None, extras=(), extra_offs=None,
        tm=1024, tn=512, tk=None, separate=False):
    As, Bs, extras = list(As), list(Bs), list(extras)
    a0, b0 = As[0], Bs[0]
    M, K = (a0.shape[1], a0.shape[0]) if ta else a0.shape
    N = b0.shape[0] if tb else b0.shape[1]
    tm, tn = _pick(M, tm), _pick(N, tn)
    n_a, n_b, n_e, n_o = len(As), len(Bs), len(extras), len(out_dtypes)
    n_res = n_b if n_a == 1 or separate else 1

    def vmem_bytes(k_tile):
        blocks = sum(tm * k_tile * a.dtype.itemsize for a in As) + sum(k_tile * tn * b.dtype.itemsize for b in Bs)
        tiles = tm * tn * (sum(jnp.dtype(dt).itemsize for dt in out_dtypes) + sum(e.dtype.itemsize for e in extras))
        return 2 * (blocks + tiles) + 2 * n_res * tm * tn * 4

    if tk is None:
        tk = K
        while vmem_bytes(tk) > MM_VMEM_BUDGET and tk > 128:
            tk = _pick(K, tk - 128)
    else:
        tk = _pick(K, tk)
    nk = K // tk
    n_acc = n_res if nk > 1 else 0
    if extra_offs is None:
        extra_offs = (0,) * n_e
    dn = (((0,) if ta else (1,), (1,) if tb else (0,)), ((), ()))
    bytes_a = sum(a.size * a.dtype.itemsize for a in As)
    bytes_b = sum(b.size * b.dtype.itemsize for b in Bs)
    n_outer = (N // tn) * bytes_a + bytes_b < (M // tm) * bytes_b + bytes_a

    def products(a_refs, b_refs):
        if n_a == 1:
            a = a_refs[0][...].astype(BF)
            return [lax.dot_general(a, b[...].astype(BF), dn, preferred_element_type=F32) for b in b_refs]
        if separate:
            return [lax.dot_general(a[...].astype(BF), b[...].astype(BF), dn, preferred_element_type=F32)
                    for a, b in zip(a_refs, b_refs)]
        total = None
        for a, b in zip(a_refs, b_refs):
            p = lax.dot_general(a[...].astype(BF), b[...].astype(BF), dn, preferred_element_type=F32)
            total = p if total is None else total + p
        return [total]

    def finish(accs, e_refs, o_refs):
        ex = [e[...] for e in e_refs]
        outs = epilogue(*accs, *ex) if epilogue is not None else tuple(accs)
        for o_ref, val in zip(o_refs, outs):
            o_ref[...] = val.astype(o_ref.dtype)

    def body(*refs):
        a_refs, b_refs = refs[:n_a], refs[n_a:n_a + n_b]
        e_refs = refs[n_a + n_b:n_a + n_b + n_e]
        o_refs = refs[n_a + n_b + n_e:n_a + n_b + n_e + n_o]
        acc_refs = refs[n_a + n_b + n_e + n_o:]
        if nk == 1:
            finish(products(a_refs, b_refs), e_refs, o_refs)
            return
        k = pl.program_id(2)

        @pl.when(k == 0)
        def _():
            for acc in acc_refs:
                acc[...] = jnp.zeros_like(acc)

        for acc, p in zip(acc_refs, products(a_refs, b_refs)):
            acc[...] += p

        @pl.when(k == nk - 1)
        def _():
            finish([acc[...] for acc in acc_refs], e_refs, o_refs)

    def at(f):
        return (lambda j, i, k: f(i, j, k)) if n_outer else f

    a_spec = pl.BlockSpec((tk, tm), at(lambda i, j, k: (k, i))) if ta else pl.BlockSpec((tm, tk), at(lambda i, j, k: (i, k)))
    b_spec = pl.BlockSpec((tn, tk), at(lambda i, j, k: (j, k))) if tb else pl.BlockSpec((tk, tn), at(lambda i, j, k: (k, j)))
    e_specs = [pl.BlockSpec((tm, tn), at(functools.partial(lambda i, j, k, o: (i, j + o), o=off // tn))) for off in extra_offs]
    for off in extra_offs:
        assert off % tn == 0
    outs = pl.pallas_call(
        body, name=name,
        out_shape=tuple(jax.ShapeDtypeStruct((M, N), dt) for dt in out_dtypes),
        grid=(N // tn, M // tm, nk) if n_outer else (M // tm, N // tn, nk),
        in_specs=[a_spec] * n_a + [b_spec] * n_b + e_specs,
        out_specs=tuple(pl.BlockSpec((tm, tn), at(lambda i, j, k: (i, j))) for _ in out_dtypes),
        scratch_shapes=[pltpu.VMEM((tm, tn), F32)] * n_acc,
        compiler_params=pltpu.CompilerParams(dimension_semantics=("parallel", "parallel", "arbitrary")),
    )(*As, *Bs, *extras)
    return outs[0] if n_o == 1 else outs


def _rms_fwd(x, g, *, name, blk_w=None, blk_idx=0, off=0, width=None, out_dtype=BF):
    T = x.shape[0]
    blk_w = x.shape[1] if blk_w is None else blk_w
    width = blk_w if width is None else width
    tt = _pick(T, 512)

    def body(x_ref, g_ref, o_ref):
        xf = x_ref[:, off:off + width]
        r = lax.rsqrt(jnp.mean(xf * xf, axis=-1, keepdims=True) + EPS)
        o_ref[...] = (xf * r * g_ref[...]).astype(o_ref.dtype)

    return pl.pallas_call(
        body, name=name, out_shape=jax.ShapeDtypeStruct((T, width), out_dtype), grid=(T // tt,),
        in_specs=[pl.BlockSpec((tt, blk_w), lambda i: (i, blk_idx)), pl.BlockSpec((1, width), lambda i: (0, 0))],
        out_specs=pl.BlockSpec((tt, width), lambda i: (i, 0)),
    )(x, g)


def _rms_fwd_slices(x, slices, *, name, blk_w, blk_idx):
    T = x.shape[0]
    tt = _pick(T, 512)
    n = len(slices)

    def body(x_ref, *refs):
        for (_, off, width), g_ref, o_ref in zip(slices, refs[:n], refs[n:]):
            xf = x_ref[:, off:off + width]
            r = lax.rsqrt(jnp.mean(xf * xf, axis=-1, keepdims=True) + EPS)
            o_ref[...] = (xf * r * g_ref[...]).astype(o_ref.dtype)

    return pl.pallas_call(
        body, name=name, out_shape=tuple(jax.ShapeDtypeStruct((T, w), BF) for _, _, w in slices), grid=(T // tt,),
        in_specs=[pl.BlockSpec((tt, blk_w), lambda i: (i, blk_idx))]
        + [pl.BlockSpec((1, w), lambda i: (0, 0)) for _, _, w in slices],
        out_specs=tuple(pl.BlockSpec((tt, w), lambda i: (i, 0)) for _, _, w in slices),
    )(x, *[g for g, _, _ in slices])


def _rms_bwd(dy, x, g, *, name, blk_w=None, blk_idx=0, off=0, width=None, add=None, out_dtypes=(F32,)):
    T = x.shape[0]
    blk_w = x.shape[1] if blk_w is None else blk_w
    width = blk_w if width is None else width
    tt = _pick(T, 512)
    has_add = add is not None
    n_dx = len(out_dtypes)

    def body(*refs):
        dy_ref, x_ref, g_ref = refs[:3]
        dx_refs, dg_ref = refs[3 + has_add:3 + has_add + n_dx], refs[-1]
        xf = x_ref[:, off:off + width]
        d = dy_ref[...].astype(F32)
        r = lax.rsqrt(jnp.mean(xf * xf, axis=-1, keepdims=True) + EPS)
        gd = d * g_ref[...]
        dx = r * gd - xf * (r * r * r) * jnp.mean(gd * xf, axis=-1, keepdims=True)
        if has_add:
            dx = dx + refs[3][...]
        for dx_ref in dx_refs:
            dx_ref[...] = dx.astype(dx_ref.dtype)

        @pl.when(pl.program_id(0) == 0)
        def _():
            dg_ref[...] = jnp.zeros_like(dg_ref)

        dg_ref[...] += jnp.broadcast_to(jnp.sum(d * xf * r, axis=0, keepdims=True), dg_ref.shape)

    row = pl.BlockSpec((tt, width), lambda i: (i, 0))
    in_specs = [row, pl.BlockSpec((tt, blk_w), lambda i: (i, blk_idx)), pl.BlockSpec((1, width), lambda i: (0, 0))]
    args = [dy, x, g]
    if has_add:
        in_specs.append(row)
        args.append(add)
    return pl.pallas_call(
        body, name=name,
        out_shape=tuple(jax.ShapeDtypeStruct((T, width), dt) for dt in out_dtypes) + (jax.ShapeDtypeStruct((8, width), F32),),
        grid=(T // tt,), in_specs=in_specs,
        out_specs=(row,) * n_dx + (pl.BlockSpec((8, width), lambda i: (0, 0)),),
        compiler_params=pltpu.CompilerParams(dimension_semantics=("arbitrary",)),
    )(*args)


def _rope_tables(pos_col, freq_lane):
    T = pos_col.shape[0]
    tt = _pick(T, 512)

    def body(p_ref, f_ref, c_ref, s_ref):
        ang = p_ref[...] * f_ref[...]
        lane = lax.broadcasted_iota(jnp.int32, ang.shape, 1)
        c_ref[...] = jnp.where(lane < QK_HEAD, jnp.cos(ang), 0.0)
        sn = jnp.sin(ang)
        s_ref[...] = jnp.where((lane >= QK_NOPE) & (lane < QK_NOPE + 16), -sn,
                               jnp.where((lane >= QK_NOPE + 16) & (lane < QK_HEAD), sn, 0.0))

    return pl.pallas_call(
        body, name="rope_tables", out_shape=(jax.ShapeDtypeStruct((T, HP), F32),) * 2, grid=(T // tt,),
        in_specs=[pl.BlockSpec((tt, 1), lambda i: (i, 0)), pl.BlockSpec((1, HP), lambda i: (0, 0))],
        out_specs=(pl.BlockSpec((tt, HP), lambda i: (i, 0)),) * 2,
    )(pos_col, freq_lane)


def _swap_rope_halves(n):
    src = lax.broadcasted_iota(jnp.int32, (HP, HP), 0)
    dst = lax.broadcasted_iota(jnp.int32, (HP, HP), 1)
    lo = (dst >= QK_NOPE) & (dst < QK_NOPE + 16) & (src == dst + 16)
    hi = (dst >= QK_NOPE + 16) & (dst < QK_HEAD) & (src == dst - 16)
    return _split_dot(n, jnp.where(lo | hi, 1.0, 0.0).astype(BF), 2)


def _qk_prep_fwd(raw, kpe, gain, C, S, *, name, kpe_blk=0, out_scale=1.0):
    T = raw.shape[0]
    tt = _pick(T, 256)
    has_kpe = kpe is not None

    def body(*refs):
        if has_kpe:
            raw_ref, kpe_ref, g_ref, c_ref, s_ref, o_ref = refs
        else:
            raw_ref, g_ref, c_ref, s_ref, o_ref = refs
        for h in range(N_HEADS):
            hs = slice(HP * h, HP * (h + 1))
            xr = raw_ref[:, hs] + kpe_ref[...] if has_kpe else raw_ref[:, hs]
            r = lax.rsqrt(jnp.sum(xr * xr, axis=-1, keepdims=True) * (1.0 / QK_HEAD) + EPS)
            n = xr * r * g_ref[...]
            o_ref[:, hs] = ((n * c_ref[...] + _swap_rope_halves(n) * s_ref[...]) * out_scale).astype(o_ref.dtype)

    heads = pl.BlockSpec((tt, N_HEADS * HP), lambda i: (i, 0))
    shared = pl.BlockSpec((tt, HP), lambda i: (i, 0))
    kpe_spec = pl.BlockSpec((tt, HP), lambda i: (i, kpe_blk))
    in_specs = [heads] + ([kpe_spec] if has_kpe else []) + [pl.BlockSpec((1, HP), lambda i: (0, 0)), shared, shared]
    args = [raw] + ([kpe] if has_kpe else []) + [gain, C, S]
    return pl.pallas_call(
        body, name=name, out_shape=jax.ShapeDtypeStruct(raw.shape, BF), grid=(T // tt,),
        in_specs=in_specs, out_specs=heads,
    )(*args)


def _qk_prep_bwd(dout, raw, kpe, gain, C, S, *, name, kpe_blk=0, in_scale=1.0):
    T = raw.shape[0]
    tt = _pick(T, 256)
    has_kpe = kpe is not None

    def body(*refs):
        if has_kpe:
            d_ref, raw_ref, kpe_ref, g_ref, c_ref, s_ref, dx_ref, dg_ref, dkpe_ref = refs
        else:
            d_ref, raw_ref, g_ref, c_ref, s_ref, dx_ref, dg_ref = refs
        dg = jnp.zeros((1, HP), F32)
        dkpe = jnp.zeros((tt, HP), F32)
        for h in range(N_HEADS):
            hs = slice(HP * h, HP * (h + 1))
            xr = raw_ref[:, hs] + kpe_ref[...] if has_kpe else raw_ref[:, hs]
            d = d_ref[:, hs].astype(F32) * in_scale
            r = lax.rsqrt(jnp.sum(xr * xr, axis=-1, keepdims=True) * (1.0 / QK_HEAD) + EPS)
            dn = d * c_ref[...] + _swap_rope_halves(d * s_ref[...])
            gd = dn * g_ref[...]
            dx = r * gd - xr * (r * r * r) * (jnp.sum(gd * xr, axis=-1, keepdims=True) * (1.0 / QK_HEAD))
            dx_ref[:, hs] = dx.astype(dx_ref.dtype)
            dg = dg + jnp.sum(dn * xr * r, axis=0, keepdims=True)
            dkpe = dkpe + dx

        @pl.when(pl.program_id(0) == 0)
        def _():
            dg_ref[...] = jnp.zeros_like(dg_ref)

        dg_ref[...] += jnp.broadcast_to(dg, dg_ref.shape)
        if has_kpe:
            dkpe_ref[...] = dkpe

    heads = pl.BlockSpec((tt, N_HEADS * HP), lambda i: (i, 0))
    shared = pl.BlockSpec((tt, HP), lambda i: (i, 0))
    kpe_spec = pl.BlockSpec((tt, HP), lambda i: (i, kpe_blk))
    in_specs = [heads, heads] + ([kpe_spec] if has_kpe else []) + [pl.BlockSpec((1, HP), lambda i: (0, 0)), shared, shared]
    args = [dout, raw] + ([kpe] if has_kpe else []) + [gain, C, S]
    out_shape = [jax.ShapeDtypeStruct(raw.shape, BF), jax.ShapeDtypeStruct((8, HP), F32)]
    out_specs = [heads, pl.BlockSpec((8, HP), lambda i: (0, 0))]
    if has_kpe:
        out_shape.append(jax.ShapeDtypeStruct((T, HP), F32))
        out_specs.append(shared)
    return pl.pallas_call(
        body, name=name, out_shape=tuple(out_shape), grid=(T // tt,),
        in_specs=in_specs, out_specs=tuple(out_specs),
        compiler_params=pltpu.CompilerParams(dimension_semantics=("arbitrary",)),
    )(*args)


ATTN_SCALE = 1.0 / math.sqrt(QK_HEAD)
LOG2E = 1.0 / math.log(2.0)
Q_SCALE = ATTN_SCALE * LOG2E


def _attn_fwd(q, k, v):
    T = q.shape[0]
    tq = _pick(T, 1024)
    tk = _pick(T, 1024)

    def body(q_ref, k_ref, v_ref, o_ref, ob_ref, lse_ref):
        qt = q_ref[...]
        m = o = None
        for j in range(T // tk):
            ks = slice(j * tk, (j + 1) * tk)
            s = lax.dot_general(qt, k_ref[ks, :], NT, preferred_element_type=F32)
            m_j = jnp.max(s, axis=-1, keepdims=True)
            m_new = m_j if m is None else jnp.maximum(m, m_j)
            o_j = jnp.dot(jnp.exp2(s - m_new).astype(BF), v_ref[ks, :], preferred_element_type=F32)
            o = o_j if o is None else o * jnp.exp2(m - m_new) + o_j
            m = m_new
        l = o[:, V_HEAD:V_HEAD + 1]
        o = o / l
        o_ref[...] = o
        ob_ref[...] = o.astype(ob_ref.dtype)
        lse_ref[...] = jnp.broadcast_to(m + jnp.log2(l), lse_ref.shape)

    qs = pl.BlockSpec((tq, HP), lambda h, i: (i, h))
    kv = pl.BlockSpec((T, HP), lambda h, i: (0, h))
    return pl.pallas_call(
        body, name="attn_fwd",
        out_shape=(jax.ShapeDtypeStruct(q.shape, F32), jax.ShapeDtypeStruct(q.shape, BF), jax.ShapeDtypeStruct(q.shape, F32)),
        grid=(N_HEADS, T // tq), in_specs=[qs, kv, kv], out_specs=(qs, qs, qs),
        compiler_params=pltpu.CompilerParams(dimension_semantics=("parallel", "parallel")),
    )(q, k, v)


def _attn_bwd(q, k, v, do, o, lse):
    T = q.shape[0]
    tb = _pick(T, 1024)
    nb = T // tb
    tkey = _pick(T, 1024)

    def body(q_ref, k_ref, v_ref, do_ref, o_ref, lse_ref, dq_ref, dk_ref, dv_ref, delta_rows, lse_rows, dob_scr, dv_acc):
        dq_ref[...] = jnp.zeros_like(dq_ref)
        dk_ref[...] = jnp.zeros_like(dk_ref)
        lane = lax.broadcasted_iota(jnp.int32, (8, HP), 1)
        ones8 = jnp.ones((8, HP), BF)
        first8 = jnp.where(lane == 0, 1.0, 0.0).astype(BF)

        def as_rows(pick, v):
            total, rest = None, v
            for _ in range(3):
                piece = rest.astype(BF)
                part = lax.dot_general(pick, piece, NT, preferred_element_type=F32)
                total = part if total is None else total + part
                rest = rest - piece.astype(F32)
            return total

        def per_q_tile(i, carry):
            qs = pl.ds(pl.multiple_of(i * tb, tb), tb)
            doi = do_ref[qs, :]
            delta_rows[i] = as_rows(ones8, doi * o_ref[qs, :])
            lse_rows[i] = as_rows(first8, lse_ref[qs, :])
            dob_scr[qs, :] = doi.astype(BF)
            return carry

        lax.fori_loop(0, nb, per_q_tile, 0)

        def k_loop(j, carry):
            ks = pl.ds(pl.multiple_of(j * tkey, tkey), tkey)
            kj, vj = k_ref[ks, :], v_ref[ks, :]

            dv_acc[...] = jnp.zeros_like(dv_acc)

            def q_loop(i, carry_q):
                qs = pl.ds(pl.multiple_of(i * tb, tb), tb)
                qi = q_ref[qs, :]
                dob = dob_scr[qs, :]
                s_t = lax.dot_general(kj, qi, NT, preferred_element_type=F32)
                p_t = jnp.exp2(s_t - lse_rows[i, 0:1, :])
                dp_t = lax.dot_general(vj, dob, NT, preferred_element_type=F32)
                ds_t = (p_t * (dp_t - delta_rows[i, 0:1, :])).astype(BF)
                dv_acc[...] += jnp.dot(p_t.astype(BF), dob, preferred_element_type=F32)
                dk_ref[ks, :] += jnp.dot(ds_t, qi, preferred_element_type=F32)
                dq_ref[qs, :] += lax.dot_general(ds_t, kj, TN, preferred_element_type=F32)
                return carry_q

            lax.fori_loop(0, nb, q_loop, 0)
            dv_ref[ks, :] = dv_acc[...].astype(dv_ref.dtype)
            return carry

        lax.fori_loop(0, T // tkey, k_loop, 0)

    spec = pl.BlockSpec((T, HP), lambda h: (0, h))
    return pl.pallas_call(
        body, name="attn_bwd",
        out_shape=(jax.ShapeDtypeStruct(q.shape, F32), jax.ShapeDtypeStruct(q.shape, F32), jax.ShapeDtypeStruct(q.shape, BF)),
        grid=(N_HEADS,), in_specs=[spec] * 6, out_specs=(spec,) * 3,
        scratch_shapes=[pltpu.VMEM((nb, 8, tb), F32), pltpu.VMEM((nb, 8, tb), F32), pltpu.VMEM((T, HP), BF),
                        pltpu.VMEM((tkey, HP), F32)],
        compiler_params=pltpu.CompilerParams(dimension_semantics=("parallel",), vmem_limit_bytes=2 * 15 * T * HP * 2 + (8 << 20)),
    )(q, k, v, do, o, lse)


CONV_TC = 512
CONV_PAD = CONV_WIDTH // 2


def _halo_specs(tr, col_of):
    r8 = tr // 8
    cur = pl.BlockSpec((tr, CONV_TC), lambda j, i: (i, col_of(j)))
    prev = pl.BlockSpec((8, CONV_TC), lambda j, i: (jnp.maximum(i * r8 - 1, 0), col_of(j)))

    def nxt_map(j, i, n8):
        return (jnp.minimum((i + 1) * r8, n8 - 1), col_of(j))

    return cur, prev, nxt_map


def _with_halo(prev_ref, cur_ref, next_ref, i, n_i):
    prev = jnp.where(i == 0, 0.0, prev_ref[...].astype(F32))
    nxt = jnp.where(i == n_i - 1, 0.0, next_ref[...].astype(F32))
    return jnp.concatenate([prev, cur_ref[...].astype(F32), nxt], axis=0)


def _conv_fwd(u, w8, b):
    T = u.shape[0]
    tr = _pick(T, 512)
    n_i = T // tr
    c0 = U_XBC // CONV_TC
    cur, prev, nxt_map = _halo_specs(tr, lambda j: c0 + j)
    nxt = pl.BlockSpec((8, CONV_TC), functools.partial(nxt_map, n8=T // 8))

    def body(p_ref, c_ref, n_ref, w_ref, b_ref, pre_ref, act_ref):
        i = pl.program_id(1)
        full = _with_halo(p_ref, c_ref, n_ref, i, n_i)
        acc = jnp.broadcast_to(b_ref[...], (tr, CONV_TC))
        for kk in range(CONV_WIDTH):
            acc = acc + full[8 - CONV_PAD + kk:8 - CONV_PAD + kk + tr, :] * w_ref[kk:kk + 1, :]
        pre_ref[...] = acc
        act_ref[...] = _silu(acc)

    out = pl.BlockSpec((tr, CONV_TC), lambda j, i: (i, j))
    return pl.pallas_call(
        body, name="conv_fwd", out_shape=(jax.ShapeDtypeStruct((T, XBC_DIM), F32),) * 2,
        grid=(XBC_DIM // CONV_TC, n_i),
        in_specs=[prev, cur, nxt, pl.BlockSpec((8, CONV_TC), lambda j, i: (0, j)), pl.BlockSpec((1, CONV_TC), lambda j, i: (0, j))],
        out_specs=(out, out),
    )(u, u, u, w8, b)


def _conv_bwd(dacts, pre, u, w8, col0, *, name):
    T, width = dacts[0].shape
    tr = _pick(T, 512)
    n_i = T // tr
    n_d = len(dacts)
    cd = col0 // CONV_TC
    cx = (U_XBC + col0) // CONV_TC

    def halo(col_of):
        cur, prev, nxt_map = _halo_specs(tr, col_of)
        return [prev, cur, pl.BlockSpec((8, CONV_TC), functools.partial(nxt_map, n8=T // 8))]

    def body(*refs):
        d_refs, pre_refs, x_refs = refs[:3 * n_d], refs[3 * n_d:3 * n_d + 3], refs[3 * n_d + 3:3 * n_d + 6]
        w_ref, dx_ref, dw_ref = refs[3 * n_d + 6:]
        i = pl.program_id(1)
        dfull = _with_halo(*d_refs[0:3], i, n_i)
        for p in range(1, n_d):
            dfull = dfull + _with_halo(*d_refs[3 * p:3 * p + 3], i, n_i)
        dfull = dfull * _dsilu(_with_halo(*pre_refs, i, n_i))
        xfull = _with_halo(*x_refs, i, n_i)
        dcur = dfull[8:8 + tr, :]
        dx = jnp.zeros((tr, CONV_TC), F32)
        rows = []
        for kk in range(CONV_WIDTH):
            dx = dx + dfull[8 + CONV_PAD - kk:8 + CONV_PAD - kk + tr, :] * w_ref[kk:kk + 1, :]
            rows.append(jnp.sum(dcur * xfull[8 - CONV_PAD + kk:8 - CONV_PAD + kk + tr, :], axis=0, keepdims=True))
        rows.append(jnp.sum(dcur, axis=0, keepdims=True))
        rows.append(jnp.zeros((2, CONV_TC), F32))
        dx_ref[...] = dx.astype(dx_ref.dtype)

        @pl.when(i == 0)
        def _():
            dw_ref[...] = jnp.zeros_like(dw_ref)

        dw_ref[...] += jnp.concatenate(rows, axis=0)

    out = pl.BlockSpec((tr, CONV_TC), lambda j, i: (i, j))
    return pl.pallas_call(
        body, name=name, out_shape=(jax.ShapeDtypeStruct((T, width), BF), jax.ShapeDtypeStruct((8, width), F32)),
        grid=(width // CONV_TC, n_i),
        in_specs=halo(lambda j: j) * n_d + halo(lambda j: cd + j) + halo(lambda j: cx + j)
        + [pl.BlockSpec((8, CONV_TC), lambda j, i: (0, cd + j))],
        out_specs=(out, pl.BlockSpec((8, CONV_TC), lambda j, i: (0, j))),
        compiler_params=pltpu.CompilerParams(dimension_semantics=("parallel", "arbitrary")),
    )(*[d for d in dacts for _ in range(3)], pre, pre, pre, u, u, u, w8)


N_HB = 2 * SSM_GROUPS
P_DT, P_CS, P_E, P_W = 0, HP, 2 * HP, 3 * HP
DT_BLK = (U_SMALL + S_DT) // HP


def _tri(rev, transpose=False):
    rows = lax.broadcasted_iota(jnp.int32, (CHUNK, CHUNK), 0)
    cols = lax.broadcasted_iota(jnp.int32, (CHUNK, CHUNK), 1)
    if transpose:
        rows, cols = cols, rows
    return (cols >= rows) if rev else (cols <= rows)


def _ssd_prep(u, bias8, alog8):
    T = u.shape[0]
    nc = T // CHUNK

    def body(dt_ref, bias_ref, a_ref, cols_ref, rows_ref):
        lane = lax.broadcasted_iota(jnp.int32, (CHUNK, HP), 1)
        dt = _softplus(dt_ref[...] + bias_ref[0:1, :])
        da = dt * (-jnp.exp(a_ref[0:1, :]))
        cs_f = jnp.dot(jnp.where(_tri(False), 1.0, 0.0).astype(F32), da, precision=HI, preferred_element_type=F32)
        cs_b = jnp.dot(jnp.where(_tri(True), 1.0, 0.0).astype(F32), da, precision=HI, preferred_element_type=F32)
        cs = jnp.where(lane < SSM_HEADS, cs_f, cs_b)
        tot = jnp.where(lane[0:1] < SSM_HEADS, cs_f[CHUNK - 1:CHUNK, :], cs_b[0:1, :])
        e, w = jnp.exp(cs), jnp.exp(tot - cs)
        tot8 = jnp.broadcast_to(tot, (8, HP))
        etot8 = jnp.exp(tot8)
        for b in range(N_HB):
            down = (HP - HG * b) % HP

            def rolled(v):
                return pltpu.roll(v, down, 1) if down else v

            cols_ref[b, :, P_DT:P_DT + HP] = rolled(dt)
            cs_r = rolled(cs)
            cols_ref[b, :, P_CS:P_CS + HP] = cs_r
            cols_ref[b, :, P_E:P_E + HP] = rolled(e)
            cols_ref[b, :, P_W:P_W + HP] = rolled(w)
            rows_ref[b, 0, 0:8, :] = cs_r.T[0:8, :]
            r8 = lax.broadcasted_iota(jnp.int32, (8, HP), 0)
            rows_ref[b, 0, 8:16, :] = jnp.where(r8 == 0, rolled(tot8), jnp.where(r8 == 1, rolled(etot8), 0.0))

    vec = pl.BlockSpec((8, HP), lambda c: (0, 0))
    return pl.pallas_call(
        body, name="ssd_prep",
        out_shape=(jax.ShapeDtypeStruct((N_HB, T, 4 * HP), F32), jax.ShapeDtypeStruct((N_HB, nc, 16, HP), F32)),
        grid=(nc,), in_specs=[pl.BlockSpec((CHUNK, HP), lambda c: (c, DT_BLK)), vec, vec],
        out_specs=(pl.BlockSpec((N_HB, CHUNK, 4 * HP), lambda c: (0, c, 0)), pl.BlockSpec((N_HB, 1, 16, HP), lambda c: (0, c, 0, 0))),
    )(u, bias8, alog8)


def _ssd_specs(T, rev, bwd):
    nc = T // CHUNK
    fwd_order = (lambda c: nc - 1 - c) if rev else (lambda c: c)
    cm = (lambda c: fwd_order(nc - 1 - c)) if bwd else fwd_order
    hb0 = SSM_GROUPS if rev else 0
    xs = pl.BlockSpec((CHUNK, GW), lambda c, g: (cm(c), g))
    bs = pl.BlockSpec((CHUNK, D_STATE), lambda c, g: (cm(c), D_INNER // D_STATE + g))
    cs = pl.BlockSpec((CHUNK, D_STATE), lambda c, g: (cm(c), (D_INNER + SSM_GROUPS * D_STATE) // D_STATE + g))
    cols = pl.BlockSpec((1, CHUNK, 4 * HP), lambda c, g: (hb0 + g, cm(c), 0))
    rows = pl.BlockSpec((1, 1, 16, HP), lambda c, g: (hb0 + g, cm(c), 0, 0))
    return nc, cm, xs, bs, cs, cols, rows


def _head_lanes(to_heads):
    shape = (GW, HP) if to_heads else (HP, GW)
    wide = lax.broadcasted_iota(jnp.int32, shape, 0 if to_heads else 1)
    head = lax.broadcasted_iota(jnp.int32, shape, 1 if to_heads else 0)
    return jnp.where((wide >= PH * head) & (wide < PH * (head + 1)), 1.0, 0.0).astype(BF)


def _split_dot(v, m, terms):
    total, rest = None, v
    for _ in range(terms):
        piece = rest.astype(BF)
        part = jnp.dot(piece, m, preferred_element_type=F32)
        total = part if total is None else total + part
        rest = rest - piece.astype(F32)
    return total


def _spread_cols(cols_ref, rows_ref):
    spread = _head_lanes(False)
    dt_e = _split_dot(cols_ref[0, :, P_DT:P_DT + HP], spread, 3)
    e_e = _split_dot(cols_ref[0, :, P_E:P_E + HP], spread, 2)
    w_e = _split_dot(cols_ref[0, :, P_W:P_W + HP], spread, 2)
    etot_e = _split_dot(rows_ref[0, 0, 8:16, :], spread, 3)[1:2, :]
    return dt_e, e_e, w_e, etot_e


def _decay(cols_ref, rows_ref, hh, incl, transpose=False):
    col = cols_ref[0, :, P_CS + hh:P_CS + hh + 1]
    row = rows_ref[0, 0, hh:hh + 1, :]
    return jnp.where(incl, jnp.exp(row - col if transpose else col - row), 0.0)


def _ssd_fwd(act, cols, rows, *, rev, name, add=None):
    T = act.shape[0]
    nc, cm, xs_s, b_s, c_s, cols_s, rows_s = _ssd_specs(T, rev, False)
    has_add = add is not None

    def body(*refs):
        x_ref, b_ref, c_ref, cols_ref, rows_ref = refs[:5]
        y_ref, st_ref, state = refs[5 + has_add:]
        c, g = pl.program_id(0), pl.program_id(1)

        @pl.when(c == 0)
        def _():
            state[g] = jnp.zeros((D_STATE, GW), F32)

        incl = _tri(rev)
        bm, cmat = b_ref[...].astype(BF), c_ref[...].astype(BF)
        bm_t = b_ref[...].T.astype(BF)
        cb = lax.dot_general(cmat, bm, NT, preferred_element_type=F32)
        dt_e, e_e, w_e, etot_e = _spread_cols(cols_ref, rows_ref)
        prev_all = state[g]
        st_ref[...] = prev_all
        xdt = x_ref[...] * dt_e
        xdt_b = xdt.astype(BF)
        yo_all = jnp.dot(cmat, prev_all.astype(BF), preferred_element_type=F32) * e_e
        state[g] = prev_all * etot_e + jnp.dot(bm_t, (xdt * w_e).astype(BF), preferred_element_type=F32)
        for hh in range(HG):
            hs = slice(PH * hh, PH * (hh + 1))
            lmat = _decay(cols_ref, rows_ref, hh, incl)
            yd = jnp.dot((cb * lmat).astype(BF), xdt_b[:, hs], preferred_element_type=F32)
            y_ref[:, hs] = yd + yo_all[:, hs] + refs[5][:, hs] if has_add else yd + yo_all[:, hs]

    return pl.pallas_call(
        body, name=name,
        out_shape=(jax.ShapeDtypeStruct((T, D_INNER), F32), jax.ShapeDtypeStruct((nc * D_STATE, D_INNER), F32)),
        grid=(nc, SSM_GROUPS), in_specs=[xs_s, b_s, c_s, cols_s, rows_s] + [xs_s] * has_add, out_specs=(xs_s, xs_s),
        scratch_shapes=[pltpu.VMEM((SSM_GROUPS, D_STATE, GW), F32)],
        compiler_params=pltpu.CompilerParams(dimension_semantics=("arbitrary", "arbitrary")),
    )(act, act, act, cols, rows, *([add] if has_add else []))


def _ssd_bwd(act, cols, rows, states, dy, *, rev, name, skip=None, add=None):
    T = act.shape[0]
    nc, cm, xs_s, b_s, c_s, cols_s, rows_s = _ssd_specs(T, rev, True)
    has_skip, has_add = skip is not None, add is not None
    n_in = 7 + has_skip + 3 * has_add

    def body(*refs):
        x_ref, b_ref, c_ref, cols_ref, rows_ref, st_ref, dy_ref = refs[:7]
        extra = list(refs[7:n_in])
        dx_ref, db_ref, dc_ref, dsel_ref, dtot_ref, dstate, dcs_cols, dcs_rows, dcb, dm_scr, dxdt_scr = refs[n_in:]
        c, g = pl.program_id(0), pl.program_id(1)

        @pl.when(c == 0)
        def _():
            dstate[g] = jnp.zeros((D_STATE, GW), F32)

        incl, incl_t = _tri(rev), _tri(rev, transpose=True)
        bm, cmat = b_ref[...].astype(BF), c_ref[...].astype(BF)
        cm_t = c_ref[...].T.astype(BF)
        cb = lax.dot_general(cmat, bm, NT, preferred_element_type=F32)
        cb_t = lax.dot_general(bm, cmat, NT, preferred_element_type=F32)
        prev_all, ds_all = st_ref[...], dstate[g]
        pb_all, dsb_all = prev_all.astype(BF), ds_all.astype(BF)
        cp_all = jnp.dot(cmat, pb_all, preferred_element_type=F32)
        bds_all = jnp.dot(bm, dsb_all, preferred_element_type=F32)
        dt_e, e_e, w_e, etot_e = _spread_cols(cols_ref, rows_ref)
        to_heads = _head_lanes(True)
        x, dy = x_ref[...], dy_ref[...]
        xdt = x * dt_e
        xdt_b, dy_b = xdt.astype(BF), dy.astype(BF)
        dye_b, xdw_b = (dy * e_e).astype(BF), (xdt * w_e).astype(BF)
        for hh in range(HG):
            hs = slice(PH * hh, PH * (hh + 1))
            mmat_t = cb_t * _decay(cols_ref, rows_ref, hh, incl_t, transpose=True)
            dm_scr[hh] = lax.dot_general(dy_b[:, hs], xdt_b[:, hs], NT, preferred_element_type=F32)
            dxdt_scr[:, hs] = jnp.dot(mmat_t.astype(BF), dy_b[:, hs], preferred_element_type=F32)
        bdsw = bds_all * w_e
        dxdt = dxdt_scr[...] + bdsw
        dx = dxdt * dt_e
        if has_skip:
            dx = dx + dy * extra.pop(0)[...]
        if has_add:
            dx = dx + extra[0][...]
        dx_ref[...] = dx
        t = _split_dot(xdt * bdsw, to_heads, 2)
        dcs_state = _split_dot(dy * cp_all, to_heads, 2) * cols_ref[0, :, P_E:P_E + HP] - t
        dsel_ref[0, :, 0:HP] = _split_dot(dxdt * x, to_heads, 2)
        sp = _split_dot(jnp.broadcast_to(jnp.sum(ds_all * prev_all, axis=0, keepdims=True), (8, GW)), to_heads, 2)
        dtot_ref[0, 0] = jnp.sum(t, axis=0, keepdims=True) + sp * rows_ref[0, 0, 9:10, :]
        dstate[g] = ds_all * etot_e + jnp.dot(cm_t, dye_b, preferred_element_type=F32)
        dcs_cols[...] = jnp.zeros_like(dcs_cols)
        dcs_rows[...] = jnp.zeros_like(dcs_rows)
        dcb[...] = jnp.zeros_like(dcb)
        for hh in range(HG):
            lmat = _decay(cols_ref, rows_ref, hh, incl)
            dm = dm_scr[hh]
            qm = dm * (cb * lmat)
            dcs_cols[:, hh:hh + 1] = jnp.sum(qm, axis=1, keepdims=True)
            dcs_rows[hh:hh + 1, :] = jnp.sum(qm, axis=0, keepdims=True)
            dcb[...] += dm * lmat
        dcb_all = dcb[...]
        dsel_ref[0, :, HP:2 * HP] = dcs_state + dcs_cols[...] - dcs_rows[...].T
        dc = (lax.dot_general(dye_b, pb_all, NT, preferred_element_type=F32)
              + jnp.dot(dcb_all.astype(BF), bm, preferred_element_type=F32))
        db = (lax.dot_general(xdw_b, dsb_all, NT, preferred_element_type=F32)
              + jnp.dot(dcb_all.T.astype(BF), cmat, preferred_element_type=F32))
        db_ref[...] = db + extra[1][...] if has_add else db
        dc_ref[...] = dc + extra[2][...] if has_add else dc

    bc_out = pl.BlockSpec((CHUNK, D_STATE), lambda c, g: (cm(c), g))
    more_specs = [pl.BlockSpec((1, GW), lambda c, g: (0, g))] * has_skip + [xs_s, bc_out, bc_out] * has_add
    more_args = ([skip] if has_skip else []) + (list(add) if has_add else [])
    return pl.pallas_call(
        body, name=name,
        out_shape=(jax.ShapeDtypeStruct((T, D_INNER), F32), jax.ShapeDtypeStruct((T, SSM_GROUPS * D_STATE), F32),
                   jax.ShapeDtypeStruct((T, SSM_GROUPS * D_STATE), F32), jax.ShapeDtypeStruct((SSM_GROUPS, T, 2 * HP), F32),
                   jax.ShapeDtypeStruct((SSM_GROUPS, nc, 8, HP), F32)),
        grid=(nc, SSM_GROUPS), in_specs=[xs_s, b_s, c_s, cols_s, rows_s, xs_s, xs_s] + more_specs,
        out_specs=(xs_s, bc_out, bc_out, pl.BlockSpec((1, CHUNK, 2 * HP), lambda c, g: (g, cm(c), 0)),
                   pl.BlockSpec((1, 1, 8, HP), lambda c, g: (g, cm(c), 0, 0))),
        scratch_shapes=[pltpu.VMEM((SSM_GROUPS, D_STATE, GW), F32), pltpu.VMEM((CHUNK, CHUNK), F32),
                        pltpu.VMEM((CHUNK, CHUNK), F32), pltpu.VMEM((CHUNK, CHUNK), F32),
                        pltpu.VMEM((HG, CHUNK, CHUNK), F32), pltpu.VMEM((CHUNK, GW), F32)],
        compiler_params=pltpu.CompilerParams(dimension_semantics=("arbitrary", "arbitrary")),
    )(act, act, act, cols, rows, states, dy, *more_args)


def _ssd_prep_bwd(u, bias8, alog8, dsel_f, dtot_f, dsel_b, dtot_b):
    T = u.shape[0]
    nc = T // CHUNK

    def body(dt_ref, bias_ref, a_ref, sf_ref, tf_ref, sb_ref, tb_ref, ddt_ref, da_ref, dbias_ref):
        @pl.when(pl.program_id(0) == 0)
        def _():
            da_ref[...] = jnp.zeros_like(da_ref)
            dbias_ref[...] = jnp.zeros_like(dbias_ref)

        lane = lax.broadcasted_iota(jnp.int32, (CHUNK, HP), 1)
        pre = dt_ref[...] + bias_ref[0:1, :]
        dt = _softplus(pre)
        a = -jnp.exp(a_ref[0:1, :])
        ddt_x, dcs, dtot = jnp.zeros((CHUNK, HP), F32), jnp.zeros((CHUNK, HP), F32), jnp.zeros((8, HP), F32)
        for b in range(N_HB):
            s_ref, t_ref, g = (sf_ref, tf_ref, b) if b < SSM_GROUPS else (sb_ref, tb_ref, b - SSM_GROUPS)
            mine = (lane >= HG * b) & (lane < HG * (b + 1))

            def up(v):
                return pltpu.roll(v, HG * b, 1) if b else v

            ddt_x = ddt_x + jnp.where(mine, up(s_ref[g, :, 0:HP]), 0.0)
            dcs = dcs + jnp.where(mine, up(s_ref[g, :, HP:2 * HP]), 0.0)
            dtot = dtot + jnp.where(mine[0:8], up(t_ref[g, 0]), 0.0)
        tri_f = jnp.where(_tri(False, transpose=True), 1.0, 0.0).astype(F32)
        tri_b = jnp.where(_tri(True, transpose=True), 1.0, 0.0).astype(F32)
        dda = jnp.where(lane < SSM_HEADS, jnp.dot(tri_f, dcs, precision=HI, preferred_element_type=F32),
                        jnp.dot(tri_b, dcs, precision=HI, preferred_element_type=F32)) + dtot[0:1, :]
        dpre = (ddt_x + dda * a) * jax.nn.sigmoid(pre)
        ddt_ref[...] = jnp.where(lane < 2 * SSM_HEADS, dpre, 0.0)
        dbias_ref[...] += jnp.broadcast_to(jnp.sum(dpre, axis=0, keepdims=True), (8, HP))
        da_ref[...] += jnp.broadcast_to(jnp.sum(dda * dt, axis=0, keepdims=True) * a, (8, HP))

    vec = pl.BlockSpec((8, HP), lambda c: (0, 0))
    sel = pl.BlockSpec((SSM_GROUPS, CHUNK, 2 * HP), lambda c: (0, c, 0))
    tot = pl.BlockSpec((SSM_GROUPS, 1, 8, HP), lambda c: (0, c, 0, 0))
    tile = pl.BlockSpec((CHUNK, HP), lambda c: (c, 0))
    return pl.pallas_call(
        body, name="ssd_prep_bwd",
        out_shape=(jax.ShapeDtypeStruct((T, HP), F32), jax.ShapeDtypeStruct((8, HP), F32), jax.ShapeDtypeStruct((8, HP), F32)),
        grid=(nc,), in_specs=[pl.BlockSpec((CHUNK, HP), lambda c: (c, DT_BLK)), vec, vec, sel, tot, sel, tot],
        out_specs=(tile, vec, vec),
        compiler_params=pltpu.CompilerParams(dimension_semantics=("arbitrary",)),
    )(u, bias8, alog8, dsel_f, dtot_f, dsel_b, dtot_b)


def _ssm_combine_fwd(y_scans, act, u, dskip, gain):
    T = y_scans.shape[0]
    tt = _pick(T, 512)

    def body(ys_ref, x_ref, z_ref, ds_ref, g_ref, y_ref, m_ref):
        y = ys_ref[...] + ds_ref[...] * x_ref[...]
        y2 = y * _silu(z_ref[...])
        r = lax.rsqrt(jnp.mean(y2 * y2, axis=-1, keepdims=True) + EPS)
        y_ref[...] = y
        m_ref[...] = (y2 * r * g_ref[...]).astype(m_ref.dtype)

    blk = pl.BlockSpec((tt, GW), lambda i, g: (i, g))
    vec = pl.BlockSpec((1, GW), lambda i, g: (0, g))
    return pl.pallas_call(
        body, name="ssm_combine_fwd",
        out_shape=(jax.ShapeDtypeStruct((T, D_INNER), F32), jax.ShapeDtypeStruct((T, D_INNER), BF)),
        grid=(T // tt, SSM_GROUPS), in_specs=[blk, blk, blk, vec, vec], out_specs=(blk, blk),
    )(y_scans, act, u, dskip, gain)


def _ssm_combine_bwd(dm, y, act, u, gain):
    T = y.shape[0]
    tt = _pick(T, 512)

    def body(dm_ref, y_ref, x_ref, z_ref, g_ref, dy_ref, dz_ref, dg_ref, dsk_ref):
        z = z_ref[...]
        y = y_ref[...]
        x = x_ref[...]
        sz = _silu(z)
        y2 = y * sz
        r = lax.rsqrt(jnp.mean(y2 * y2, axis=-1, keepdims=True) + EPS)
        d = dm_ref[...]
        gd = d * g_ref[...]
        dy2 = r * gd - y2 * (r * r * r) * jnp.mean(gd * y2, axis=-1, keepdims=True)
        dy = dy2 * sz
        dy_ref[...] = dy
        dz_ref[...] = (dy2 * y * _dsilu(z)).astype(dz_ref.dtype)

        @pl.when(pl.program_id(1) == 0)
        def _():
            dg_ref[...] = jnp.zeros_like(dg_ref)
            dsk_ref[...] = jnp.zeros_like(dsk_ref)

        dg_ref[...] += jnp.broadcast_to(jnp.sum(d * y2 * r, axis=0, keepdims=True), dg_ref.shape)
        lane_sum = jnp.broadcast_to(jnp.sum(dy * x, axis=0, keepdims=True), (8, GW))
        src = lax.broadcasted_iota(jnp.int32, (GW, HP), 0)
        head = lax.broadcasted_iota(jnp.int32, (GW, HP), 1)
        to_head = jnp.where((src >= PH * head) & (src < PH * (head + 1)), 1.0, 0.0).astype(F32)
        dsk_ref[...] += jnp.dot(lane_sum, to_head, precision=HI, preferred_element_type=F32)

    blk = pl.BlockSpec((tt, GW), lambda g, i: (i, g))
    vec = pl.BlockSpec((1, GW), lambda g, i: (0, g))
    acc = pl.BlockSpec((8, GW), lambda g, i: (0, g))
    return pl.pallas_call(
        body, name="ssm_combine_bwd",
        out_shape=(jax.ShapeDtypeStruct((T, D_INNER), F32), jax.ShapeDtypeStruct((T, D_INNER), BF),
                   jax.ShapeDtypeStruct((8, D_INNER), F32), jax.ShapeDtypeStruct((8, SSM_GROUPS * HP), F32)),
        grid=(SSM_GROUPS, T // tt), in_specs=[blk, blk, blk, blk, vec],
        out_specs=(blk, blk, acc, pl.BlockSpec((8, HP), lambda g, i: (0, g))),
        compiler_params=pltpu.CompilerParams(dimension_semantics=("parallel", "arbitrary")),
    )(dm, y, act, u, gain)


def _loss_head(y, target):
    T, D = y.shape
    tt = _pick(T, 512)

    def body(y_ref, t_ref, dy_ref, dyb_ref, l_ref):
        e = y_ref[...] - t_ref[...]
        dy_ref[...] = e * (1.0 / D)
        dyb_ref[...] = (e * (1.0 / D)).astype(dyb_ref.dtype)

        @pl.when(pl.program_id(0) == 0)
        def _():
            l_ref[...] = jnp.zeros_like(l_ref)

        l_ref[...] += jnp.sum(e * e) * (0.5 / D)

    blk = pl.BlockSpec((tt, D), lambda i: (i, 0))
    return pl.pallas_call(
        body, name="loss_head",
        out_shape=(jax.ShapeDtypeStruct((T, D), F32), jax.ShapeDtypeStruct((T, D), BF), jax.ShapeDtypeStruct((8, 128), F32)),
        grid=(T // tt,), in_specs=[blk, blk], out_specs=(blk, blk, pl.BlockSpec((8, 128), lambda i: (0, 0))),
        compiler_params=pltpu.CompilerParams(dimension_semantics=("arbitrary",)),
    )(y, target)


def _adamw(w, g, m, v, *, name):
    R, C = w.shape
    cap = max(8, (1 << 18) // C)
    tr = R
    if R % 8 == 0:
        tr = 8
        for cand in range(8, min(R, cap) + 1, 8):
            if R % cand == 0:
                tr = cand

    def body(w_ref, g_ref, m_ref, v_ref, d_ref, nm_ref, nv_ref):
        gg = g_ref[...]
        nm = ADAM_B1 * m_ref[...] + (1.0 - ADAM_B1) * gg
        nv = ADAM_B2 * v_ref[...] + (1.0 - ADAM_B2) * jnp.square(gg)
        m_hat = nm / (1.0 - ADAM_B1 ** ADAM_STEP)
        v_hat = nv / (1.0 - ADAM_B2 ** ADAM_STEP)
        d_ref[...] = -ADAM_LR * (m_hat / (jnp.sqrt(v_hat) + ADAM_EPS) + ADAM_WD * w_ref[...])
        nm_ref[...] = nm
        nv_ref[...] = nv

    blk = pl.BlockSpec((tr, C), lambda i: (i, 0))
    return pl.pallas_call(
        body, name=name, out_shape=(jax.ShapeDtypeStruct((R, C), F32),) * 3, grid=(R // tr,),
        in_specs=[blk] * 4, out_specs=(blk,) * 3,
    )(w, g, m, v)


ANY = pl.BlockSpec(memory_space=pl.ANY)


def _chip_peers():
    x, y, c = lax.axis_index("x"), lax.axis_index("y"), lax.axis_index("c")
    return x, y, c, [(1 - x, y), (x, 1 - y), (1 - x, 1 - y)]


def _half_rows(c, rh):
    return pl.ds(pl.multiple_of(c * rh, 16), rh)


def _my_chip():
    return 2 * lax.axis_index("x") + lax.axis_index("y")


def _gather_chips(wb, wf):
    rh = wb.shape[0] // 2
    rq = rh // 2

    def body(wb_ref, wf_ref, ob_ref, of_ref, send_sems, recv_sems):
        x, y, c, peers = _chip_peers()
        nbr_x, nbr_y = peers[0], peers[1]
        me, chip_x, chip_y, chip_d = 2 * x + y, 2 * (1 - x) + y, 2 * x + (1 - y), 2 * (1 - x) + (1 - y)

        def quarter(core, b):
            return pl.ds(pl.multiple_of(core * rh + b * rq, 16), rq)

        ici = [(0, nbr_x, me, 0, chip_x), (1, nbr_y, me, 1, chip_y), (2, nbr_y, me, 0, chip_y), (3, nbr_x, me, 1, chip_x),
               (4, nbr_y, chip_x, 0, chip_d), (5, nbr_x, chip_y, 1, chip_d)]

        def ici_copy(k, to, slot, b, own):
            rows = quarter(c, b)
            return pltpu.make_async_remote_copy(
                src_ref=wb_ref.at[rows] if own else ob_ref.at[slot, rows], dst_ref=ob_ref.at[slot, rows],
                send_sem=send_sems.at[k], recv_sem=recv_sems.at[k], device_id=(to[0], to[1], c), device_id_type=MESH)

        def to_sibling(k, slot, b, core):
            rows = quarter(core, b)
            return pltpu.make_async_remote_copy(
                src_ref=ob_ref.at[slot, rows], dst_ref=ob_ref.at[slot, rows], send_sem=send_sems.at[6 + k],
                recv_sem=recv_sems.at[6 + k], device_id=(x, y, 1 - c), device_id_type=MESH)

        def small_copy(k, slot):
            px, py = peers[k]
            return pltpu.make_async_remote_copy(
                src_ref=wf_ref, dst_ref=of_ref.at[slot], send_sem=send_sems.at[12 + k], recv_sem=recv_sems.at[12 + k],
                device_id=(px, py, c), device_id_type=MESH)

        sends = [ici_copy(k, to, slot, b, True) for k, to, slot, b, _ in ici[:4]] + [small_copy(k, me) for k in range(3)]
        for cp in sends:
            cp.start()
        for k, to, slot, b, arrives in ici:
            ici_copy(k, to, arrives, b, False).wait_recv()
            passed = [to_sibling(k, arrives, b, c)]
            if k < 2:
                passed.append(ici_copy(*ici[4 + k][:4], False))
            for cp in passed:
                cp.start()
            sends += passed
        for k, to, slot, b, arrives in ici:
            to_sibling(k, arrives, b, 1 - c).wait_recv()
        chip_of = [chip_x, chip_y, chip_d]
        for k in range(3):
            small_copy(k, chip_of[k]).wait_recv()
        for cp in sends:
            cp.wait_send()

    ob, of = pl.pallas_call(
        body, name="gather_weights",
        out_shape=(jax.ShapeDtypeStruct((4,) + wb.shape, wb.dtype), jax.ShapeDtypeStruct((4,) + wf.shape, wf.dtype)),
        in_specs=[ANY, ANY], out_specs=(ANY, ANY),
        scratch_shapes=[pltpu.SemaphoreType.DMA((15,)), pltpu.SemaphoreType.DMA((15,))],
    )(wb, wf)
    me = _my_chip()
    return lax.dynamic_update_slice(ob, wb[None], (me, 0, 0)), lax.dynamic_update_slice(of, wf[None], (me, 0, 0))


def _halves_to_sibling(gp):
    rh = gp.shape[1] // 2

    def body(gp_ref, o_ref, send_sem, recv_sem):
        x, y, c = lax.axis_index("x"), lax.axis_index("y"), lax.axis_index("c")
        cp = pltpu.make_async_remote_copy(src_ref=gp_ref.at[:, _half_rows(1 - c, rh), :], dst_ref=o_ref, send_sem=send_sem,
                                          recv_sem=recv_sem, device_id=(x, y, 1 - c), device_id_type=MESH)
        cp.start()
        cp.wait()

    return pl.pallas_call(
        body, name="halves_to_sibling", out_shape=jax.ShapeDtypeStruct((gp.shape[0], rh, gp.shape[2]), gp.dtype),
        in_specs=[ANY], out_specs=ANY, scratch_shapes=[pltpu.SemaphoreType.DMA, pltpu.SemaphoreType.DMA],
    )(gp)


def _row_tile(rows, cap=1024):
    tr = 16
    for cand in range(16, cap + 1, 16):
        if rows % cand == 0:
            tr = cand
    return tr


def _add_halves(gp, sib, core):
    n, rh, C = sib.shape
    tr = _row_tile(rh)
    nt = rh // tr

    def body(c_ref, g_ref, s_ref, o_ref):
        o_ref[...] = (g_ref[...].astype(F32) + s_ref[...].astype(F32)).astype(o_ref.dtype)

    blk = pl.BlockSpec((1, tr, C), lambda j, i, c: (j, i, 0))
    return pl.pallas_call(
        body, name="add_halves", out_shape=jax.ShapeDtypeStruct(sib.shape, sib.dtype),
        grid_spec=pltpu.PrefetchScalarGridSpec(
            num_scalar_prefetch=1, grid=(n, nt),
            in_specs=[pl.BlockSpec((1, tr, C), lambda j, i, c: (j, c[0] * nt + i, 0)), blk], out_specs=blk),
    )(core, gp, sib)


def _join_halves(buf):
    rh = buf.shape[0] // 2

    def body(in_ref, o_ref, send_sem, recv_sem):
        x, y, c = lax.axis_index("x"), lax.axis_index("y"), lax.axis_index("c")

        def copy(rows):
            return pltpu.make_async_remote_copy(src_ref=o_ref.at[rows], dst_ref=o_ref.at[rows], send_sem=send_sem,
                                                recv_sem=recv_sem, device_id=(x, y, 1 - c), device_id_type=MESH)

        send = copy(_half_rows(c, rh))
        send.start()
        copy(_half_rows(1 - c, rh)).wait_recv()
        send.wait_send()

    return pl.pallas_call(
        body, name="join_halves", out_shape=jax.ShapeDtypeStruct(buf.shape, buf.dtype),
        in_specs=[ANY], out_specs=ANY, input_output_aliases={0: 0},
        scratch_shapes=[pltpu.SemaphoreType.DMA, pltpu.SemaphoreType.DMA],
    )(buf)


def _exchange_near(gp):
    rq = gp.shape[1] // 2

    def body(gp_ref, out_ref, send_sems, recv_sems):
        x, y, c, peers = _chip_peers()
        chip_x, chip_y, chip_d = 2 * (1 - x) + y, 2 * x + (1 - y), 2 * (1 - x) + (1 - y)
        plan = [(peers[0], chip_x, 0), (peers[0], chip_d, 0), (peers[1], chip_y, 1), (peers[1], chip_d, 1)]
        copies = [pltpu.make_async_remote_copy(
            src_ref=gp_ref.at[slot, pl.ds(b * rq, rq)], dst_ref=out_ref.at[k], send_sem=send_sems.at[k],
            recv_sem=recv_sems.at[k], device_id=(to[0], to[1], c), device_id_type=MESH) for k, (to, slot, b) in enumerate(plan)]
        for cp in copies:
            cp.start()
        for cp in copies:
            cp.wait_recv()
        for cp in copies:
            cp.wait_send()

    return pl.pallas_call(
        body, name="exchange_grads_near", out_shape=jax.ShapeDtypeStruct((4, rq, gp.shape[2]), gp.dtype),
        in_specs=[ANY], out_specs=ANY, scratch_shapes=[pltpu.SemaphoreType.DMA((4,)), pltpu.SemaphoreType.DMA((4,))],
    )(gp)


def _add_near(gp, near, chips):
    _, rq, C = near.shape
    tr = _row_tile(rq)
    nt = rq // tr

    def body(ch_ref, mine_a, mine_b, on_a, on_b, near_ref, part_ref, on_ref):
        part_ref[0] = mine_a[0].astype(F32) + near_ref[0].astype(F32)
        part_ref[1] = mine_b[0].astype(F32) + near_ref[2].astype(F32)
        on_ref[0] = (on_a[0].astype(F32) + near_ref[1].astype(F32)).astype(on_ref.dtype)
        on_ref[1] = (on_b[0].astype(F32) + near_ref[3].astype(F32)).astype(on_ref.dtype)

    def slot(which, b):
        return pl.BlockSpec((1, tr, C), lambda i, ch: (ch[which], b * nt + i, 0))

    return pl.pallas_call(
        body, name="add_near",
        out_shape=(jax.ShapeDtypeStruct((2, rq, C), F32), jax.ShapeDtypeStruct((2, rq, C), near.dtype)),
        grid_spec=pltpu.PrefetchScalarGridSpec(
            num_scalar_prefetch=1, grid=(nt,),
            in_specs=[slot(0, 0), slot(0, 1), slot(2, 0), slot(1, 1), pl.BlockSpec((4, tr, C), lambda i, ch: (0, i, 0))],
            out_specs=(pl.BlockSpec((2, tr, C), lambda i, ch: (0, i, 0)),) * 2),
    )(chips, gp, gp, gp, gp, near)


def _exchange_far(on):
    def body(on_ref, out_ref, send_sems, recv_sems):
        x, y, c, peers = _chip_peers()
        copies = [pltpu.make_async_remote_copy(
            src_ref=on_ref.at[k], dst_ref=out_ref.at[k], send_sem=send_sems.at[k], recv_sem=recv_sems.at[k],
            device_id=(to[0], to[1], c), device_id_type=MESH) for k, to in enumerate((peers[1], peers[0]))]
        for cp in copies:
            cp.start()
        for cp in copies:
            cp.wait_recv()
        for cp in copies:
            cp.wait_send()

    return pl.pallas_call(
        body, name="exchange_grads_far", out_shape=jax.ShapeDtypeStruct(on.shape, on.dtype),
        in_specs=[ANY], out_specs=ANY, scratch_shapes=[pltpu.SemaphoreType.DMA((2,)), pltpu.SemaphoreType.DMA((2,))],
    )(on)


def _add_far(part, far, core):
    _, rq, C = part.shape
    tr = _row_tile(rq)
    nt = rq // tr

    def body(c_ref, p_ref, f_ref, o_ref):
        o_ref[...] = p_ref[0] + f_ref[0].astype(F32)

    blk = pl.BlockSpec((1, tr, C), lambda b, i, c: (b, i, 0))
    return pl.pallas_call(
        body, name="add_far", out_shape=jax.ShapeDtypeStruct((4 * rq, C), F32),
        grid_spec=pltpu.PrefetchScalarGridSpec(
            num_scalar_prefetch=1, grid=(2, nt), in_specs=[blk, blk],
            out_specs=pl.BlockSpec((tr, C), lambda b, i, c: ((2 * c[0] + b) * nt + i, 0))),
    )(core, part, far)


N_DEV = 8


def _allreduce_small(p):
    rs = p.shape[0]

    def body(x_ref, sum_ref, all_ref, send_sems, recv_sems, local_sem):
        x, y, c = lax.axis_index("x"), lax.axis_index("y"), lax.axis_index("c")
        me, sibling = (x, y, c), (x, y, 1 - c)
        chips = [(1 - x, y), (x, 1 - y), (1 - x, 1 - y)]

        def rows(px, py, pc):
            return all_ref.at[pl.ds((4 * px + 2 * py + pc) * rs, rs), :]

        def copy(k, block, to, src=None):
            return pltpu.make_async_remote_copy(
                src_ref=rows(*block) if src is None else src, dst_ref=rows(*block),
                send_sem=send_sems.at[k], recv_sem=recv_sems.at[k], device_id=to, device_id_type=MESH)

        mine = pltpu.make_async_copy(x_ref, rows(*me), local_sem)
        mine.start()
        first = [copy(0, me, sibling, src=x_ref)]
        first += [copy(1 + j, me, (*chip, c), src=x_ref) for j, chip in enumerate(chips)]
        for cp in first:
            cp.start()
        passed = [copy(4 + j, (*chip, c), sibling) for j, chip in enumerate(chips)]
        for j, chip in enumerate(chips):
            copy(1 + j, (*chip, c), me).wait_recv()
            passed[j].start()
        copy(0, sibling, me).wait_recv()
        for j, chip in enumerate(chips):
            copy(4 + j, (*chip, 1 - c), me).wait_recv()
        for cp in first + passed:
            cp.wait_send()
        mine.wait()
        acc = all_ref[0:rs, :]
        for d in range(1, N_DEV):
            acc = acc + all_ref[d * rs:(d + 1) * rs, :]
        sum_ref[...] = acc

    vmem = pl.BlockSpec(memory_space=pltpu.VMEM)
    return pl.pallas_call(
        body, name="allreduce_small", out_shape=jax.ShapeDtypeStruct((rs, 128), F32),
        in_specs=[vmem], out_specs=vmem,
        scratch_shapes=[pltpu.VMEM((N_DEV * rs, 128), F32), pltpu.SemaphoreType.DMA((7,)), pltpu.SemaphoreType.DMA((7,)),
                        pltpu.SemaphoreType.DMA],
    )(p)


WEIGHTS = ('ffn1_norm', 'ffn1_w_gate', 'ffn1_w_up', 'ffn1_w_down', 'mix_norm', 'w_in', 'q_a_norm', 'w_q_b',
           'kv_a_norm', 'w_kv_b', 'q_head_norm', 'k_head_norm', 'conv_w', 'conv_b', 'a_log_fwd', 'a_log_bwd',
           'dt_bias_fwd', 'dt_bias_bwd', 'd_skip', 'ssm_norm', 'w_attn_branch', 'w_ssm_branch', 'w_out',
           'ffn2_norm', 'ffn2_w_gate', 'ffn2_w_up', 'ffn2_w_down')
PACKED = (('ffn1_w_gate', (D_MODEL, D_FF), 1), ('ffn1_w_up', (D_MODEL, D_FF), 1), ('ffn1_w_down', (D_FF, D_MODEL), 0),
          ('w_in', (D_MODEL, sum(IN_SPLITS)), 1), ('w_q_b', (Q_LORA, N_HEADS * QK_HEAD), 1),
          ('w_kv_b', (KV_LORA, N_HEADS * (QK_NOPE + V_HEAD)), 1),
          ('w_attn_branch', (N_HEADS * V_HEAD, D_MODEL), 0), ('w_ssm_branch', (D_INNER, D_MODEL), 0),
          ('w_out', (D_MODEL, D_MODEL), 0),
          ('ffn2_w_gate', (D_MODEL, D_FF), 1), ('ffn2_w_up', (D_MODEL, D_FF), 1), ('ffn2_w_down', (D_FF, D_MODEL), 0))
PACK_W = 1024
N_CHIPS = 4
SMALL = (('ffn1_norm', 1024), ('mix_norm', 1024), ('q_a_norm', 384), ('kv_a_norm', 256), ('q_head_norm', 96),
         ('k_head_norm', 96), ('conv_b', 3072), ('a_log_fwd', 32), ('a_log_bwd', 32), ('dt_bias_fwd', 32),
         ('dt_bias_bwd', 32), ('d_skip', 32), ('ssm_norm', 2048), ('ffn2_norm', 1024),
         ('conv_w', CONV_WIDTH * XBC_DIM), ('loss', 1))


TRANSPOSED = ('ffn1_w_gate', 'ffn1_w_up', 'w_in', 'ffn2_w_gate', 'ffn2_w_up')


def _stored(name, a):
    return a.T if name in TRANSPOSED else a


def _shard_shape(name, shape, axis):
    sh = tuple(s // N_CHIPS if a == axis else s for a, s in enumerate(shape))
    return sh[::-1] if name in TRANSPOSED else sh


def _by_rows(name, axis):
    return name in TRANSPOSED or axis == 0


def _pack_layout():
    out, r = {}, 0
    for name, shape, axis in PACKED:
        n = math.prod(shape) // N_CHIPS // PACK_W
        out[name] = (r, n)
        r += n
    return out, -(-r // 64) * 64


def _pack(shards):
    layout, rows = _pack_layout()
    parts = [shards[name].reshape(-1, PACK_W) for name, _, _ in PACKED]
    parts.append(jnp.zeros((rows - sum(p.shape[0] for p in parts), PACK_W), parts[0].dtype))
    return jnp.concatenate(parts, axis=0)


def _unpack(packed):
    layout, _ = _pack_layout()
    return {name: packed[layout[name][0]:layout[name][0] + layout[name][1]].reshape(_shard_shape(name, shape, axis))
            for name, shape, axis in PACKED}


def _full_from_slots(slots):
    layout, _ = _pack_layout()
    out = {}
    for name, shape, axis in PACKED:
        r, n = layout[name]
        if _by_rows(name, axis):
            out[name] = slots[:, r:r + n].reshape(N_CHIPS * n, PACK_W)
        else:
            sh = _shard_shape(name, shape, axis)
            out[name] = jnp.concatenate([slots[j, r:r + n].reshape(sh) for j in range(N_CHIPS)], axis=axis)
    return out


def _slots_from_full(full):
    layout, rows = _pack_layout()
    parts = []
    for name, shape, axis in PACKED:
        r, n = layout[name]
        if _by_rows(name, axis):
            parts.append(full[name].reshape(N_CHIPS, n, PACK_W))
        else:
            size = shape[axis] // N_CHIPS
            parts.append(jnp.stack([lax.slice_in_dim(full[name], j * size, (j + 1) * size, axis=axis).reshape(n, PACK_W)
                                    for j in range(N_CHIPS)]))
    parts.append(jnp.zeros((N_CHIPS, rows - sum(p.shape[1] for p in parts), PACK_W), parts[0].dtype))
    return jnp.concatenate(parts, axis=1)


def _pack_small(vals):
    parts = []
    for name, n in SMALL:
        pad = -(-n // 128) * 128 - n
        parts.append(jnp.pad(vals[name].reshape(-1).astype(F32), (0, pad)).reshape(-1, 128))
    rows = sum(p.shape[0] for p in parts)
    parts.append(jnp.zeros((-(-rows // 8) * 8 - rows, 128), F32))
    return jnp.concatenate(parts, axis=0)


def _unpack_small(packed):
    out, r = {}, 0
    for name, n in SMALL:
        k = -(-n // 128)
        out[name] = packed[r:r + k].reshape(-1)[:n]
        r += k
    return out


def _pad_heads(w, axis, per_head, lo, hi):
    shape = w.shape
    w = w.reshape(shape[:axis] + (N_HEADS, per_head) + shape[axis + 1:])
    w = lax.slice_in_dim(w, lo, hi, axis=axis + 1)
    pad = [(0, 0)] * w.ndim
    pad[axis + 1] = (0, HP - (hi - lo))
    w = jnp.pad(w, pad)
    return w.reshape(shape[:axis] + (N_HEADS * HP,) + shape[axis + 1:])


def _unpad_heads(w, axis, keep):
    shape = w.shape
    w = w.reshape(shape[:axis] + (N_HEADS, HP) + shape[axis + 1:])
    return lax.slice_in_dim(w, 0, keep, axis=axis + 1)


def _pad_w_in(wt):
    o = [0]
    for s in IN_SPLITS:
        o.append(o[-1] + s)
    cq, ckv, kpe, z, xbc, dtf, dtb, ga, gb = [wt[o[i]:o[i + 1]] for i in range(len(IN_SPLITS))]
    kpe_pad = jnp.pad(kpe, ((QK_NOPE, HP - QK_HEAD), (0, 0)))
    dt_pad = jnp.pad(jnp.concatenate([dtf, dtb], axis=0), ((0, HP - 2 * SSM_HEADS), (0, 0)))
    return jnp.concatenate([z, ga, gb, xbc, cq, ckv, kpe_pad, dt_pad], axis=0)


def _unpad_w_in(gt):
    z, ga, gb, xbc = gt[U_Z:U_GA], gt[U_GA:U_GB], gt[U_GB:U_XBC], gt[U_XBC:U_SMALL]
    s = gt[U_SMALL:]
    cq, ckv = s[S_CQ:S_CKV], s[S_CKV:S_KPE]
    kpe = s[S_KPE + QK_NOPE:S_KPE + QK_HEAD]
    dtf, dtb = s[S_DT:S_DT + SSM_HEADS], s[S_DT + SSM_HEADS:S_DT + 2 * SSM_HEADS]
    return jnp.concatenate([cq, ckv, kpe, z, xbc, dtf, dtb, ga, gb], axis=0)


def _lanes128(parts):
    row = jnp.concatenate([p.reshape(-1) for p in parts])
    return jnp.pad(row, (0, HP - row.shape[0])).reshape(1, HP)


FF_TILE = D_FF // 2
WGRAD = BF


def _ffn_fwd(x, g, wg_t, wu_t, wd, tag):
    h = _rms_fwd(x, g, name=tag + "_norm")
    gate, up, act = _mm([h], [wg_t, wu_t], name=tag + "_up", tb=True, out_dtypes=(BF, BF, BF), tm=512, tn=FF_TILE,
                        epilogue=lambda a, b: (a, b, _silu(a) * b))
    out = _mm([act], [wd], name=tag + "_down", extras=[x], epilogue=lambda acc, r: (r + 0.5 * acc,))
    return out, (h, gate, up, act)


def _ffn_bwd(dout, dout_bf, x, g, wg_t, wu_t, wd, saved, tag):
    h, gate, up, act = saved

    def swiglu_bwd(acc, a, b):
        a, b, half = a.astype(F32), b.astype(F32), 0.5 * acc
        s = jax.nn.sigmoid(a)
        return half * b * (s * (1.0 + a * (1.0 - s))), half * (a * s)

    dgate, dup = _mm([dout_bf], [wd], name=tag + "_down_dx", tb=True, extras=[gate, up], out_dtypes=(BF, BF),
                     tm=512, tn=FF_TILE, epilogue=swiglu_bwd)
    dwd = _mm([act], [dout_bf], name=tag + "_down_dw", ta=True, tm=FF_TILE, out_dtypes=(WGRAD,),
              epilogue=lambda acc: (0.5 * acc,))
    dwg_t, dwu_t = _mm([dgate, dup], [h, h], name=tag + "_up_dw", ta=True, separate=True, out_dtypes=(WGRAD, WGRAD),
                       tm=FF_TILE)
    dh = _mm([dgate, dup], [wg_t, wu_t], name=tag + "_up_dx")
    dx, dx_bf, dg = _rms_bwd(dh, x, g, name=tag + "_norm_bwd", add=dout, out_dtypes=(F32, BF))
    return dx, dx_bf, dg, dwg_t, dwu_t, dwd


KPE_BLK = (U_SMALL + S_KPE) // HP
SMALL_BLK = U_SMALL // SMALL_W


def _local_step(x, pos_col, target, W, P):
    T = x.shape[0]
    sig = jax.nn.sigmoid
    x1, ffn1 = _ffn_fwd(x, P["ffn1_norm"], W["wg1"], W["wu1"], W["wd1"], "ffn1")
    h = _rms_fwd(x1, P["mix_norm"], name="mix_norm")
    u = _mm([h], [W["w_in"]], name="in_proj", tb=True, tn=1152)
    cqn, ckvn = _rms_fwd_slices(u, [(P["q_a_norm"], S_CQ, Q_LORA), (P["kv_a_norm"], S_CKV, KV_LORA)], name="latent_norms",
                                blk_w=SMALL_W, blk_idx=SMALL_BLK)
    q_raw = _mm([cqn], [W["wq"]], name="q_proj")
    def with_ones_lane(acc_k, acc_v):
        lane = lax.broadcasted_iota(jnp.int32, acc_v.shape, 1)
        return acc_k, jnp.where((lane & (HP - 1)) == V_HEAD, 1.0, acc_v)

    k_raw, v = _mm([ckvn], [W["wk"], W["wv"]], name="kv_proj", out_dtypes=(F32, BF), epilogue=with_ones_lane)
    rc, rs = _rope_tables(pos_col, P["freq"])
    q = _qk_prep_fwd(q_raw, None, P["q_head_norm"], rc, rs, name="q_prep", out_scale=Q_SCALE)
    k = _qk_prep_fwd(k_raw, u, P["k_head_norm"], rc, rs, name="k_prep", kpe_blk=KPE_BLK)
    o, o_bf, lse = _attn_fwd(q, k, v)
    pre, act = _conv_fwd(u, P["conv_w8"], P["conv_b"])
    scan_cols, scan_rows = _ssd_prep(u, P["dt_bias8"], P["a_log8"])
    y_f, st_f = _ssd_fwd(act, scan_cols, scan_rows, rev=False, name="ssd_fwd_f")
    y_fb, st_b = _ssd_fwd(act, scan_cols, scan_rows, rev=True, name="ssd_fwd_b", add=y_f)
    ysum, m = _ssm_combine_fwd(y_fb, act, u, P["d_skip_lanes"], P["ssm_norm"])
    ab, mb, merged = _mm([o_bf, m], [W["pa"], W["pb"]], name="branches", separate=True, extras=[u, u],
                         extra_offs=(U_GA, U_GB), out_dtypes=(F32, F32, BF),
                         epilogue=lambda a, b, ga, gb: (a, b, sig(ga) * a + sig(gb) * b))
    x2 = _mm([merged], [W["wo"]], name="out_proj", extras=[x1], epilogue=lambda acc, r: (r + acc,))
    y, ffn2 = _ffn_fwd(x2, P["ffn2_norm"], W["wg2"], W["wu2"], W["wd2"], "ffn2")
    dy, dy_bf, loss = _loss_head(y, target)
    dx2, dx2_bf, dg_ffn2, dwg2, dwu2, dwd2 = _ffn_bwd(dy, dy_bf, x2, P["ffn2_norm"], W["wg2"], W["wu2"], W["wd2"], ffn2,
                                                      "ffn2")

    def gate_bwd(dmrg, a, b, ga, gb):
        sa, sb = sig(ga), sig(gb)
        return dmrg * sa, dmrg * sb, dmrg * a * sa * (1.0 - sa), dmrg * b * sb * (1.0 - sb)

    dab, dmb, dga, dgb = _mm([dx2_bf], [W["wo"]], name="out_proj_dx", tb=True, extras=[ab, mb, u, u],
                             extra_offs=(0, 0, U_GA, U_GB), out_dtypes=(BF,) * 4, epilogue=gate_bwd)
    dwo = _mm([merged], [dx2_bf], name="out_proj_dw", ta=True, out_dtypes=(WGRAD,))
    dpa, dpb = _mm([o_bf, m], [dab, dmb], name="branches_dw", ta=True, separate=True, out_dtypes=(WGRAD, WGRAD))
    do, dm = _mm([dab, dmb], [W["pa"], W["pb"]], name="branches_dx", tb=True, separate=True, out_dtypes=(F32, F32))
    dyssd, dz, dg_ssm, dskip = _ssm_combine_bwd(dm, ysum, act, u, P["ssm_norm"])
    dxs_f, db_f, dc_f, dsel_f, dtot_f = _ssd_bwd(act, scan_cols, scan_rows, st_f, dyssd, rev=False, name="ssd_bwd_f",
                                                 skip=P["d_skip_lanes"])
    dxs, db, dc, dsel_b, dtot_b = _ssd_bwd(act, scan_cols, scan_rows, st_b, dyssd, rev=True, name="ssd_bwd_b",
                                           add=(dxs_f, db_f, dc_f))
    ddt, dalog, dbias = _ssd_prep_bwd(u, P["dt_bias8"], P["a_log8"], dsel_f, dtot_f, dsel_b, dtot_b)
    dxbc, dconv = [], []
    for tag, col0, part in (("x", 0, dxs), ("b", D_INNER, db), ("c", D_INNER + SSM_GROUPS * D_STATE, dc)):
        dxp, dwp = _conv_bwd([part], pre, u, P["conv_w8"], col0, name="conv_bwd_" + tag)
        dxbc.append(dxp)
        dconv.append(dwp)
    dconv = jnp.concatenate(dconv, axis=1)
    dq, dk, dv = _attn_bwd(q, k, v, do, o, lse)
    dq_raw, dg_qh = _qk_prep_bwd(dq, q_raw, None, P["q_head_norm"], rc, rs, name="q_prep_bwd", in_scale=ATTN_SCALE)
    dk_raw, dg_kh, dkpe = _qk_prep_bwd(dk, k_raw, u, P["k_head_norm"], rc, rs, name="k_prep_bwd", kpe_blk=KPE_BLK,
                                       in_scale=1.0 / LOG2E)
    dwq = _mm([cqn], [dq_raw], name="q_proj_dw", ta=True, out_dtypes=(WGRAD,))
    dcqn = _mm([dq_raw], [W["wq"]], name="q_proj_dx", tb=True)
    dwk, dwv = _mm([ckvn], [dk_raw, dv], name="kv_proj_dw", ta=True, out_dtypes=(WGRAD, WGRAD))
    dckvn = _mm([dk_raw, dv], [W["wk"], W["wv"]], name="kv_proj_dx", tb=True)
    dcq, dg_qa = _rms_bwd(dcqn, u, P["q_a_norm"], name="q_a_norm_bwd", blk_w=SMALL_W, blk_idx=SMALL_BLK, off=S_CQ,
                          width=Q_LORA, out_dtypes=(BF,))
    dckv, dg_kva = _rms_bwd(dckvn, u, P["kv_a_norm"], name="kv_a_norm_bwd", blk_w=SMALL_W, blk_idx=SMALL_BLK,
                            off=S_CKV, width=KV_LORA, out_dtypes=(BF,))
    du = jnp.concatenate([dz, dga, dgb] + dxbc + [dcq, dckv, dkpe.astype(BF), ddt.astype(BF)], axis=1)
    dw_in = _mm([du], [h], name="in_proj_dw", ta=True, tm=1152, out_dtypes=(WGRAD,))
    dh = _mm([du], [W["w_in"]], name="in_proj_dx")
    dx1, dx1_bf, dg_mix = _rms_bwd(dh, x1, P["mix_norm"], name="mix_norm_bwd", add=dx2, out_dtypes=(F32, BF))
    dx, _, dg_ffn1, dwg1, dwu1, dwd1 = _ffn_bwd(dx1, dx1_bf, x, P["ffn1_norm"], W["wg1"], W["wu1"], W["wd1"], ffn1, "ffn1")
    dW = dict(wg1=dwg1, wu1=dwu1, wd1=dwd1, w_in=dw_in, wq=dwq, wk=dwk, wv=dwv, pa=dpa, pb=dpb, wo=dwo,
              wg2=dwg2, wu2=dwu2, wd2=dwd2)
    dP = dict(ffn1_norm=dg_ffn1[0], mix_norm=dg_mix[0], q_a_norm=dg_qa[0], kv_a_norm=dg_kva[0],
              q_head_norm=dg_qh[0, :QK_HEAD], k_head_norm=dg_kh[0, :QK_HEAD], conv_b=dconv[CONV_WIDTH],
              a_log_fwd=dalog[0, :SSM_HEADS], a_log_bwd=dalog[0, SSM_HEADS:2 * SSM_HEADS],
              dt_bias_fwd=dbias[0, :SSM_HEADS], dt_bias_bwd=dbias[0, SSM_HEADS:2 * SSM_HEADS],
              d_skip=dskip[0].reshape(SSM_GROUPS, HP)[:, :HG], ssm_norm=dg_ssm[0], ffn2_norm=dg_ffn2[0],
              conv_w=dconv[:CONV_WIDTH], loss=loss[0, 0])
    return dx, dW, dP


def _prepare(w, conv_w_full):
    kvb = w["w_kv_b"]
    W = dict(wg1=w["ffn1_w_gate"], wu1=w["ffn1_w_up"], wd1=w["ffn1_w_down"], w_in=_pad_w_in(w["w_in"]),
             wq=_pad_heads(w["w_q_b"], 1, QK_HEAD, 0, QK_HEAD),
             wk=_pad_heads(kvb, 1, QK_NOPE + V_HEAD, 0, QK_NOPE),
             wv=_pad_heads(kvb, 1, QK_NOPE + V_HEAD, QK_NOPE, QK_NOPE + V_HEAD),
             pa=_pad_heads(w["w_attn_branch"], 0, V_HEAD, 0, V_HEAD), pb=w["w_ssm_branch"], wo=w["w_out"],
             wg2=w["ffn2_w_gate"], wu2=w["ffn2_w_up"], wd2=w["ffn2_w_down"])
    inv_freq = [1.0 / (ROPE_BASE ** (j / QK_ROPE)) for j in range(0, QK_ROPE, 2)]
    freq = [0.0] * QK_NOPE + inv_freq + inv_freq + [0.0] * (HP - QK_HEAD)
    P = {n: w[n] for n in ("ffn1_norm", "mix_norm", "q_a_norm", "kv_a_norm", "ssm_norm", "ffn2_norm", "conv_b")}
    P.update(q_head_norm=_lanes128([w["q_head_norm"]]), k_head_norm=_lanes128([w["k_head_norm"]]),
             conv_w8=jnp.pad(conv_w_full, ((0, 8 - CONV_WIDTH), (0, 0))),
             dt_bias8=jnp.broadcast_to(_lanes128([w["dt_bias_fwd"], w["dt_bias_bwd"]]), (8, HP)),
             a_log8=jnp.broadcast_to(_lanes128([w["a_log_fwd"], w["a_log_bwd"]]), (8, HP)),
             d_skip_lanes=jnp.repeat(w["d_skip"].reshape(-1), PH).reshape(1, D_INNER),
             freq=jnp.asarray(freq, F32).reshape(1, HP))
    return W, P


def _unprepare(dW):
    dkvb = jnp.concatenate([_unpad_heads(dW["wk"], 1, QK_NOPE), _unpad_heads(dW["wv"], 1, V_HEAD)], axis=2)
    return dict(ffn1_w_gate=dW["wg1"], ffn1_w_up=dW["wu1"], ffn1_w_down=dW["wd1"], w_in=_unpad_w_in(dW["w_in"]),
                w_q_b=_unpad_heads(dW["wq"], 1, QK_HEAD).reshape(Q_LORA, N_HEADS * QK_HEAD),
                w_kv_b=dkvb.reshape(KV_LORA, N_HEADS * (QK_NOPE + V_HEAD)),
                w_attn_branch=_unpad_heads(dW["pa"], 0, V_HEAD).reshape(N_HEADS * V_HEAD, D_MODEL),
                w_ssm_branch=dW["pb"], w_out=dW["wo"],
                ffn2_w_gate=dW["wg2"], ffn2_w_up=dW["wu2"], ffn2_w_down=dW["wd2"])


def kernel(x, positions, ffn1_norm, ffn1_w_gate, ffn1_w_up, ffn1_w_down, mix_norm, w_in, q_a_norm, w_q_b, kv_a_norm, w_kv_b, q_head_norm, k_head_norm, conv_w, conv_b, a_log_fwd, a_log_bwd, dt_bias_fwd, dt_bias_bwd, d_skip, ssm_norm, w_attn_branch, w_ssm_branch, w_out, ffn2_norm, ffn2_w_gate, ffn2_w_up, ffn2_w_down, loss_target, m_ffn1_norm, m_ffn1_w_gate, m_ffn1_w_up, m_ffn1_w_down, m_mix_norm, m_w_in, m_q_a_norm, m_w_q_b, m_kv_a_norm, m_w_kv_b, m_q_head_norm, m_k_head_norm, m_conv_w, m_conv_b, m_a_log_fwd, m_a_log_bwd, m_dt_bias_fwd, m_dt_bias_bwd, m_d_skip, m_ssm_norm, m_w_attn_branch, m_w_ssm_branch, m_w_out, m_ffn2_norm, m_ffn2_w_gate, m_ffn2_w_up, m_ffn2_w_down, v_ffn1_norm, v_ffn1_w_gate, v_ffn1_w_up, v_ffn1_w_down, v_mix_norm, v_w_in, v_q_a_norm, v_w_q_b, v_kv_a_norm, v_w_kv_b, v_q_head_norm, v_k_head_norm, v_conv_w, v_conv_b, v_a_log_fwd, v_a_log_bwd, v_dt_bias_fwd, v_dt_bias_bwd, v_d_skip, v_ssm_norm, v_w_attn_branch, v_w_ssm_branch, v_w_out, v_ffn2_norm, v_ffn2_w_gate, v_ffn2_w_up, v_ffn2_w_down):
    given = dict(locals())
    T = x.shape[1]
    packed_names = [name for name, _, _ in PACKED]

    def two_d(a):
        return a.reshape(a.shape[1], -1) if a.ndim > 2 else a

    def kept(n, a):
        return _stored(n, two_d(a))

    w_loc = {n: kept(n, given[n]) for n in WEIGHTS}
    wb = _pack({n: w_loc[n].astype(BF) for n in packed_names})
    wf = jnp.pad(w_loc["conv_w"], ((0, 8 - CONV_WIDTH), (0, 0)))
    gb, gf = _gather_chips(wb, wf)
    full = _full_from_slots(gb)
    conv_w_full = jnp.concatenate([gf[j, :CONV_WIDTH] for j in range(N_CHIPS)], axis=1)
    full.update({n: w_loc[n] for n in WEIGHTS if n not in full and n != "conv_w"})
    W, P = _prepare(full, conv_w_full)
    dx, dW, dP = _local_step(x.reshape(T, D_MODEL), positions.reshape(T, 1).astype(F32), loss_target.reshape(T, D_MODEL), W, P)
    gp = _slots_from_full(_unprepare(dW))
    core = lax.axis_index("c").astype(jnp.int32).reshape(1)
    both_cores = _add_halves(gp, _halves_to_sibling(gp), core)
    cx, cy = lax.axis_index("x"), lax.axis_index("y")
    chips = jnp.stack([2 * cx + cy, 2 * (1 - cx) + cy, 2 * cx + (1 - cy)]).astype(jnp.int32)
    part, on = _add_near(both_cores, _exchange_near(both_cores), chips)
    grads = _unpack(_join_halves(_add_far(part, _exchange_far(on), core)))
    small = _unpack_small(_allreduce_small(_pack_small(dP)))
    grads.update({n: small[n].reshape(1, -1) for n, _ in SMALL if n not in ("conv_w", "loss")})
    grads["conv_w"] = lax.dynamic_slice_in_dim(small["conv_w"].reshape(CONV_WIDTH, XBC_DIM), _my_chip() * (XBC_DIM // N_CHIPS),
                                               XBC_DIM // N_CHIPS, axis=1)
    out_g, out_d, out_m, out_v = [], [], [], []
    for n in WEIGHTS:
        shape = given[n].shape
        delta, new_m, new_v = _adamw(w_loc[n], grads[n], kept(n, given["m_" + n]), kept(n, given["v_" + n]), name="adamw_" + n)
        for outs, a in ((out_g, grads[n]), (out_d, delta), (out_m, new_m), (out_v, new_v)):
            outs.append(_stored(n, a).reshape(shape))
    return (small["loss"].reshape(()), dx.reshape(x.shape), *out_g, *out_d, *out_m, *out_v)
```
